```python
import math
import jax, jax.numpy as jnp
from jax import lax
import numpy as np

D_MODEL = 1024
BATCH = 16
SEQ = 2048
DEPTH = 1

D_MIX = D_MODEL
GM_WIDTH = D_MIX // 2
GM_HEAD_DIM = 64
GM_HEADS = GM_WIDTH // GM_HEAD_DIM
GM_CHUNK = 128
SSM_WIDTH = D_MIX - GM_WIDTH
SSM_HEAD_DIM = 64
SSM_HEADS = SSM_WIDTH // SSM_HEAD_DIM
SSM_GROUPS = 2
SSM_STATE = 128
SSM_CONV = 4
SSM_CHUNK = 128
SSM_CONV_CH = SSM_WIDTH + 2 * SSM_GROUPS * SSM_STATE
D_FF = 4 * D_MODEL
EPS = 1e-6

IN_COLS = 2 * GM_WIDTH + SSM_WIDTH + SSM_CONV_CH + SSM_HEADS
SPLITS = (GM_WIDTH, 2 * GM_WIDTH, 2 * GM_WIDTH + SSM_WIDTH,
          2 * GM_WIDTH + SSM_WIDTH + SSM_CONV_CH)

kernel_name = "hybrid_gmlp_ssd_sandwich_block"


def rms_norm(x, w):
    xf = x.astype(jnp.float32)
    y = xf * lax.rsqrt(jnp.mean(xf * xf, axis=-1, keepdims=True) + EPS)
    return (y * w.astype(jnp.float32)).astype(x.dtype)


def layer_norm(x, w, b):
    xf = x.astype(jnp.float32)
    mu = jnp.mean(xf, axis=-1, keepdims=True)
    var = jnp.mean(jnp.square(xf - mu), axis=-1, keepdims=True)
    y = (xf - mu) * lax.rsqrt(var + EPS)
    return (y * w.astype(jnp.float32) + b.astype(jnp.float32)).astype(x.dtype)


def gmlp_mixer(u, v, ln_w, ln_b, w_s, b_s):
    bsz, L, _ = u.shape
    nc = L // GM_CHUNK
    u = jax.nn.gelu(u)
    v = jax.nn.gelu(v).reshape(bsz, L, GM_HEADS, GM_HEAD_DIM)
    v = layer_norm(v, ln_w, ln_b).reshape(bsz, nc, GM_CHUNK, GM_HEADS, GM_HEAD_DIM)
    causal = jnp.tril(jnp.ones((GM_CHUNK, GM_CHUNK), dtype=bool))
    w = jnp.where(causal[None], w_s, jnp.zeros((), w_s.dtype))
    mixed = jnp.einsum("hts,bcshp->bcthp", w, v) + b_s.T[None, None, :, :, None]
    return u * mixed.reshape(bsz, L, GM_WIDTH)


def causal_depthwise_conv(x, w, b):
    ch = x.shape[-1]
    y = lax.conv_general_dilated(
        x, w[:, None, :].astype(x.dtype), window_strides=(1,),
        padding=((SSM_CONV - 1, 0),), dimension_numbers=("NWC", "WIO", "NWC"),
        feature_group_count=ch)
    return y + b


def ssd_chunked(x, dt, a, bmat, cmat, d_skip):
    bsz, L = x.shape[0], x.shape[1]
    nc = L // SSM_CHUNK
    R = SSM_HEADS // SSM_GROUPS
    Q = SSM_CHUNK
    x = x.astype(jnp.float32).reshape(bsz, nc, Q, SSM_GROUPS, R, SSM_HEAD_DIM)
    dt = dt.astype(jnp.float32).reshape(bsz, nc, Q, SSM_GROUPS, R)
    bmat = bmat.astype(jnp.float32).reshape(bsz, nc, Q, SSM_GROUPS, SSM_STATE)
    cmat = cmat.astype(jnp.float32).reshape(bsz, nc, Q, SSM_GROUPS, SSM_STATE)
    a = a.astype(jnp.float32).reshape(SSM_GROUPS, R)
    d_skip = d_skip.astype(jnp.float32).reshape(SSM_GROUPS, R)

    a_cs = jnp.cumsum(dt * a, axis=2)
    x_dt = x * dt[..., None]

    causal = jnp.tril(jnp.ones((Q, Q), dtype=bool))[:, :, None, None]
    seg = a_cs[:, :, :, None] - a_cs[:, :, None, :]
    decay = jnp.exp(jnp.where(causal, seg, -jnp.inf))
    cb = jnp.einsum("bclgn,bcsgn->bclsg", cmat, bmat)
    y_diag = jnp.einsum("bclsg,bclsgr,bcsgrp->bclgrp", cb, decay, x_dt)

    decay_to_end = jnp.exp(a_cs[:, :, -1:] - a_cs)
    states = jnp.einsum("bcsgn,bcsgr,bcsgrp->bcgrpn", bmat, decay_to_end, x_dt)
    chunk_decay = jnp.exp(a_cs[:, :, -1])

    def step(h, inp):
        s, dcy = inp
        return h * dcy[..., None, None] + s, h
    h0 = jnp.zeros((bsz, SSM_GROUPS, R, SSM_HEAD_DIM, SSM_STATE), jnp.float32)
    _, prev = lax.scan(step, h0, (jnp.moveaxis(states, 1, 0), jnp.moveaxis(chunk_decay, 1, 0)))
    prev = jnp.moveaxis(prev, 0, 1)

    y_off = jnp.einsum("bclgn,bcgrpn,bclgr->bclgrp", cmat, prev, jnp.exp(a_cs))
    y = y_diag + y_off + d_skip[:, :, None] * x
    return y.reshape(bsz, L, SSM_HEADS * SSM_HEAD_DIM)


def mamba2_mixer(z, xbc, dt_raw, conv_w, conv_b, dt_bias, a_log, d_skip, norm_w):
    bsz, L, _ = z.shape
    xbc = jax.nn.silu(causal_depthwise_conv(xbc, conv_w, conv_b))
    xs = xbc[..., :SSM_WIDTH].reshape(bsz, L, SSM_HEADS, SSM_HEAD_DIM)
    bmat = xbc[..., SSM_WIDTH:SSM_WIDTH + SSM_GROUPS * SSM_STATE].reshape(bsz, L, SSM_GROUPS, SSM_STATE)
    cmat = xbc[..., SSM_WIDTH + SSM_GROUPS * SSM_STATE:].reshape(bsz, L, SSM_GROUPS, SSM_STATE)
    dt = jax.nn.softplus(dt_raw.astype(jnp.float32) + dt_bias.astype(jnp.float32))
    a = -jnp.exp(a_log.astype(jnp.float32))
    y = ssd_chunked(xs, dt, a, bmat, cmat, d_skip)
    y = y * jax.nn.silu(z.astype(jnp.float32))
    y = y.reshape(bsz, L, SSM_GROUPS, SSM_WIDTH // SSM_GROUPS)
    y = y * lax.rsqrt(jnp.mean(y * y, axis=-1, keepdims=True) + EPS)
    y = y.reshape(bsz, L, SSM_WIDTH) * norm_w.astype(jnp.float32)
    return y.astype(z.dtype)


def _fwd_setup_inputs(seed: int = 0) -> dict:
    key = jax.random.key(seed)
    ks = jax.random.split(key, 24)
    f32 = jnp.float32

    def gain(k, shape):
        return 1.0 + 0.02 * jax.random.normal(k, shape, f32)

    x = jax.random.normal(ks[0], (BATCH, SEQ, D_MODEL), f32)
    norm_mix_pre = gain(ks[1], (DEPTH, D_MODEL))
    w_in = jax.random.normal(ks[2], (DEPTH, D_MODEL, IN_COLS), f32) * D_MODEL ** -0.5
    gm_ln_w = gain(ks[3], (DEPTH, GM_HEADS, GM_HEAD_DIM))
    gm_ln_b = 0.02 * jax.random.normal(ks[4], (DEPTH, GM_HEADS, GM_HEAD_DIM), f32)
    gm_w_s = jax.random.normal(ks[5], (DEPTH, GM_HEADS, GM_CHUNK, GM_CHUNK), f32) * GM_CHUNK ** -0.5
    gm_b_s = gain(ks[6], (DEPTH, GM_HEADS, GM_CHUNK))
    conv_w = jax.random.normal(ks[7], (DEPTH, SSM_CONV, SSM_CONV_CH), f32) * SSM_CONV ** -0.5
    conv_b = 0.02 * jax.random.normal(ks[8], (DEPTH, SSM_CONV_CH), f32)
    dt_min, dt_max = 1e-3, 1e-1
    u = jax.random.uniform(ks[9], (DEPTH, SSM_HEADS), f32)
    dt0 = jnp.maximum(jnp.exp(u * (math.log(dt_max) - math.log(dt_min)) + math.log(dt_min)), 1e-4)
    dt_bias = dt0 + jnp.log(-jnp.expm1(-dt0))
    a_log = jnp.log(jax.random.uniform(ks[10], (DEPTH, SSM_HEADS), f32, 1.0, 16.0))
    d_skip = gain(ks[11], (DEPTH, SSM_HEADS))
    ssm_norm_w = gain(ks[12], (DEPTH, SSM_WIDTH))
    w_out = jax.random.normal(ks[13], (DEPTH, D_MIX, D_MODEL), f32) * D_MIX ** -0.5
    norm_mix_post = gain(ks[14], (DEPTH, D_MODEL))
    norm_ffn_pre = gain(ks[15], (DEPTH, D_MODEL))
    w_up = jax.random.normal(ks[16], (DEPTH, D_MODEL, D_FF), f32) * D_MODEL ** -0.5
    w_down = jax.random.normal(ks[17], (DEPTH, D_FF, D_MODEL), f32) * D_FF ** -0.5
    norm_ffn_post = gain(ks[18], (DEPTH, D_MODEL))
    return {"x": x, "norm_mix_pre": norm_mix_pre, "w_in": w_in, "gm_ln_w": gm_ln_w,
            "gm_ln_b": gm_ln_b, "gm_w_s": gm_w_s, "gm_b_s": gm_b_s, "conv_w": conv_w,
            "conv_b": conv_b, "dt_bias": dt_bias, "a_log": a_log, "d_skip": d_skip,
            "ssm_norm_w": ssm_norm_w, "w_out": w_out, "norm_mix_post": norm_mix_post,
            "norm_ffn_pre": norm_ffn_pre, "w_up": w_up, "w_down": w_down,
            "norm_ffn_post": norm_ffn_post}


def _fwd_reference(x, norm_mix_pre, w_in, gm_ln_w, gm_ln_b, gm_w_s, gm_b_s, conv_w, conv_b,
              dt_bias, a_log, d_skip, ssm_norm_w, w_out, norm_mix_post, norm_ffn_pre,
              w_up, w_down, norm_ffn_post):
    for i in range(DEPTH):
        h = rms_norm(x, norm_mix_pre[i])
        proj = jnp.einsum("bld,dk->blk", h, w_in[i])
        u_a, v_a, z_b, xbc_b, dt_b = jnp.split(proj, SPLITS, axis=-1)
        y_a = gmlp_mixer(u_a, v_a, gm_ln_w[i], gm_ln_b[i], gm_w_s[i], gm_b_s[i])
        y_b = mamba2_mixer(z_b, xbc_b, dt_b, conv_w[i], conv_b[i], dt_bias[i], a_log[i],
                           d_skip[i], ssm_norm_w[i])
        mix = jnp.concatenate([y_a, y_b], axis=-1)
        x = x + rms_norm(jnp.einsum("blk,kd->bld", mix, w_out[i]), norm_mix_post[i])
        h = rms_norm(x, norm_ffn_pre[i])
        f = jnp.square(jax.nn.relu(jnp.einsum("bld,df->blf", h, w_up[i])))
        x = x + rms_norm(jnp.einsum("blf,fd->bld", f, w_down[i]), norm_ffn_post[i])
    return x


import jax as _jax
import jax.numpy as _jnp

TWIN_FORMAT = 'train_step'
FWD_PARAMS = ['x', 'norm_mix_pre', 'w_in', 'gm_ln_w', 'gm_ln_b', 'gm_w_s', 'gm_b_s', 'conv_w', 'conv_b', 'dt_bias', 'a_log', 'd_skip', 'ssm_norm_w', 'w_out', 'norm_mix_post', 'norm_ffn_pre', 'w_up', 'w_down', 'norm_ffn_post']
TWIN_WEIGHTS = ['norm_mix_pre', 'w_in', 'gm_ln_w', 'gm_ln_b', 'gm_w_s', 'gm_b_s', 'conv_w', 'conv_b', 'dt_bias', 'a_log', 'd_skip', 'ssm_norm_w', 'w_out', 'norm_mix_post', 'norm_ffn_pre', 'w_up', 'w_down', 'norm_ffn_post']
TWIN_DIFF_INPUT = 'x'
TWIN_INPUTS = ['x', 'norm_mix_pre', 'w_in', 'gm_ln_w', 'gm_ln_b', 'gm_w_s', 'gm_b_s', 'conv_w', 'conv_b', 'dt_bias', 'a_log', 'd_skip', 'ssm_norm_w', 'w_out', 'norm_mix_post', 'norm_ffn_pre', 'w_up', 'w_down', 'norm_ffn_post', 'loss_target', 'm_norm_mix_pre', 'm_w_in', 'm_gm_ln_w', 'm_gm_ln_b', 'm_gm_w_s', 'm_gm_b_s', 'm_conv_w', 'm_conv_b', 'm_dt_bias', 'm_a_log', 'm_d_skip', 'm_ssm_norm_w', 'm_w_out', 'm_norm_mix_post', 'm_norm_ffn_pre', 'm_w_up', 'm_w_down', 'm_norm_ffn_post', 'v_norm_mix_pre', 'v_w_in', 'v_gm_ln_w', 'v_gm_ln_b', 'v_gm_w_s', 'v_gm_b_s', 'v_conv_w', 'v_conv_b', 'v_dt_bias', 'v_a_log', 'v_d_skip', 'v_ssm_norm_w', 'v_w_out', 'v_norm_mix_post', 'v_norm_ffn_pre', 'v_w_up', 'v_w_down', 'v_norm_ffn_post']
TWIN_OUTPUTS = ['loss', 'grad_x', 'grad_norm_mix_pre', 'grad_w_in', 'grad_gm_ln_w', 'grad_gm_ln_b', 'grad_gm_w_s', 'grad_gm_b_s', 'grad_conv_w', 'grad_conv_b', 'grad_dt_bias', 'grad_a_log', 'grad_d_skip', 'grad_ssm_norm_w', 'grad_w_out', 'grad_norm_mix_post', 'grad_norm_ffn_pre', 'grad_w_up', 'grad_w_down', 'grad_norm_ffn_post', 'delta_norm_mix_pre', 'delta_w_in', 'delta_gm_ln_w', 'delta_gm_ln_b', 'delta_gm_w_s', 'delta_gm_b_s', 'delta_conv_w', 'delta_conv_b', 'delta_dt_bias', 'delta_a_log', 'delta_d_skip', 'delta_ssm_norm_w', 'delta_w_out', 'delta_norm_mix_post', 'delta_norm_ffn_pre', 'delta_w_up', 'delta_w_down', 'delta_norm_ffn_post', 'new_m_norm_mix_pre', 'new_m_w_in', 'new_m_gm_ln_w', 'new_m_gm_ln_b', 'new_m_gm_w_s', 'new_m_gm_b_s', 'new_m_conv_w', 'new_m_conv_b', 'new_m_dt_bias', 'new_m_a_log', 'new_m_d_skip', 'new_m_ssm_norm_w', 'new_m_w_out', 'new_m_norm_mix_post', 'new_m_norm_ffn_pre', 'new_m_w_up', 'new_m_w_down', 'new_m_norm_ffn_post', 'new_v_norm_mix_pre', 'new_v_w_in', 'new_v_gm_ln_w', 'new_v_gm_ln_b', 'new_v_gm_w_s', 'new_v_gm_b_s', 'new_v_conv_w', 'new_v_conv_b', 'new_v_dt_bias', 'new_v_a_log', 'new_v_d_skip', 'new_v_ssm_norm_w', 'new_v_w_out', 'new_v_norm_mix_post', 'new_v_norm_ffn_pre', 'new_v_w_up', 'new_v_w_down', 'new_v_norm_ffn_post']
TWIN_LEAF_KINDS = {'loss': 'loss', 'grad_x': 'grad_x', 'grad_norm_mix_pre': 'grad_w', 'grad_w_in': 'grad_w', 'grad_gm_ln_w': 'grad_w', 'grad_gm_ln_b': 'grad_w', 'grad_gm_w_s': 'grad_w', 'grad_gm_b_s': 'grad_w', 'grad_conv_w': 'grad_w', 'grad_conv_b': 'grad_w', 'grad_dt_bias': 'grad_w', 'grad_a_log': 'grad_w', 'grad_d_skip': 'grad_w', 'grad_ssm_norm_w': 'grad_w', 'grad_w_out': 'grad_w', 'grad_norm_mix_post': 'grad_w', 'grad_norm_ffn_pre': 'grad_w', 'grad_w_up': 'grad_w', 'grad_w_down': 'grad_w', 'grad_norm_ffn_post': 'grad_w', 'delta_norm_mix_pre': 'delta_w', 'delta_w_in': 'delta_w', 'delta_gm_ln_w': 'delta_w', 'delta_gm_ln_b': 'delta_w', 'delta_gm_w_s': 'delta_w', 'delta_gm_b_s': 'delta_w', 'delta_conv_w': 'delta_w', 'delta_conv_b': 'delta_w', 'delta_dt_bias': 'delta_w', 'delta_a_log': 'delta_w', 'delta_d_skip': 'delta_w', 'delta_ssm_norm_w': 'delta_w', 'delta_w_out': 'delta_w', 'delta_norm_mix_post': 'delta_w', 'delta_norm_ffn_pre': 'delta_w', 'delta_w_up': 'delta_w', 'delta_w_down': 'delta_w', 'delta_norm_ffn_post': 'delta_w', 'new_m_norm_mix_pre': 'new_m', 'new_m_w_in': 'new_m', 'new_m_gm_ln_w': 'new_m', 'new_m_gm_ln_b': 'new_m', 'new_m_gm_w_s': 'new_m', 'new_m_gm_b_s': 'new_m', 'new_m_conv_w': 'new_m', 'new_m_conv_b': 'new_m', 'new_m_dt_bias': 'new_m', 'new_m_a_log': 'new_m', 'new_m_d_skip': 'new_m', 'new_m_ssm_norm_w': 'new_m', 'new_m_w_out': 'new_m', 'new_m_norm_mix_post': 'new_m', 'new_m_norm_ffn_pre': 'new_m', 'new_m_w_up': 'new_m', 'new_m_w_down': 'new_m', 'new_m_norm_ffn_post': 'new_m', 'new_v_norm_mix_pre': 'new_v', 'new_v_w_in': 'new_v', 'new_v_gm_ln_w': 'new_v', 'new_v_gm_ln_b': 'new_v', 'new_v_gm_w_s': 'new_v', 'new_v_gm_b_s': 'new_v', 'new_v_conv_w': 'new_v', 'new_v_conv_b': 'new_v', 'new_v_dt_bias': 'new_v', 'new_v_a_log': 'new_v', 'new_v_d_skip': 'new_v', 'new_v_ssm_norm_w': 'new_v', 'new_v_w_out': 'new_v', 'new_v_norm_mix_post': 'new_v', 'new_v_norm_ffn_pre': 'new_v', 'new_v_w_up': 'new_v', 'new_v_w_down': 'new_v', 'new_v_norm_ffn_post': 'new_v'}


def _forward(args):
    return _fwd_reference(*[args[k] for k in FWD_PARAMS])


def _output_shape():
    out = _jax.eval_shape(lambda: _forward(_fwd_setup_inputs(0)))
    return out.shape, out.dtype

N_MICROBATCH = 1
ADAM_LR = 0.001
ADAM_B1 = 0.9
ADAM_B2 = 0.999
ADAM_EPS = 1e-08
ADAM_WD = 0.01
ADAM_STEP = 10
PER_EXAMPLE_BATCH_AXIS = {'x': 0, 'loss_target': 0}
SHARED_INPUTS = []
_WEIGHT_DTYPES = {'norm_mix_pre': _jnp.float32, 'w_in': _jnp.float32, 'gm_ln_w': _jnp.float32, 'gm_ln_b': _jnp.float32, 'gm_w_s': _jnp.float32, 'gm_b_s': _jnp.float32, 'conv_w': _jnp.float32, 'conv_b': _jnp.float32, 'dt_bias': _jnp.float32, 'a_log': _jnp.float32, 'd_skip': _jnp.float32, 'ssm_norm_w': _jnp.float32, 'w_out': _jnp.float32, 'norm_mix_post': _jnp.float32, 'norm_ffn_pre': _jnp.float32, 'w_up': _jnp.float32, 'w_down': _jnp.float32, 'norm_ffn_post': _jnp.float32}
MOMENT_SCALE = {'norm_mix_pre': 7.008192e-01, 'w_in': 4.317972e-01, 'gm_ln_w': 2.208209e-01, 'gm_ln_b': 2.350261e-01, 'gm_w_s': 1.401357e-01, 'gm_b_s': 2.329425e-01, 'conv_w': 1.081518e+00, 'conv_b': 4.176985e+00, 'dt_bias': 7.653137e-01, 'a_log': 2.015162e+00, 'd_skip': 9.417943e+00, 'ssm_norm_w': 2.717395e+00, 'w_out': 3.480255e+00, 'norm_mix_post': 3.262616e+01, 'norm_ffn_pre': 1.251248e+00, 'w_up': 6.155681e-01, 'w_down': 3.582377e+00, 'norm_ffn_post': 3.350701e+01}


def _to_microbatches(a, axis):
    t = _jnp.moveaxis(a, axis, 0)
    t = t.reshape((N_MICROBATCH, t.shape[0] // N_MICROBATCH) + t.shape[1:])
    return _jnp.moveaxis(t, 1, axis + 1)


def setup_inputs(seed: int = 0) -> dict:
    inp = _fwd_setup_inputs(seed)
    key = _jax.random.fold_in(_jax.random.key(seed), 7919)
    shape, _ = _output_shape()
    out = dict(inp)
    out["loss_target"] = _jax.random.normal(_jax.random.fold_in(key, 0), shape, _jnp.float32)
    for i, name in enumerate(TWIN_WEIGHTS):
        w = inp[name].astype(_jnp.float32)
        if MOMENT_SCALE is None:
            s = _jnp.sqrt(_jnp.mean(_jnp.square(w)) + 1e-30)
        else:
            s = MOMENT_SCALE[name]
        km, kv = _jax.random.split(_jax.random.fold_in(key, i + 1))
        out[name] = w
        out["m_" + name] = s * _jax.random.normal(km, w.shape, _jnp.float32)
        out["v_" + name] = (s * s) * _jax.random.uniform(kv, w.shape, _jnp.float32, 0.5, 1.5)
    if N_MICROBATCH > 1:
        for name, axis in PER_EXAMPLE_BATCH_AXIS.items():
            out[name] = _to_microbatches(out[name], axis)
    return {'x': out['x'], 'norm_mix_pre': out['norm_mix_pre'], 'w_in': out['w_in'], 'gm_ln_w': out['gm_ln_w'], 'gm_ln_b': out['gm_ln_b'], 'gm_w_s': out['gm_w_s'], 'gm_b_s': out['gm_b_s'], 'conv_w': out['conv_w'], 'conv_b': out['conv_b'], 'dt_bias': out['dt_bias'], 'a_log': out['a_log'], 'd_skip': out['d_skip'], 'ssm_norm_w': out['ssm_norm_w'], 'w_out': out['w_out'], 'norm_mix_post': out['norm_mix_post'], 'norm_ffn_pre': out['norm_ffn_pre'], 'w_up': out['w_up'], 'w_down': out['w_down'], 'norm_ffn_post': out['norm_ffn_post'], 'loss_target': out['loss_target'], 'm_norm_mix_pre': out['m_norm_mix_pre'], 'm_w_in': out['m_w_in'], 'm_gm_ln_w': out['m_gm_ln_w'], 'm_gm_ln_b': out['m_gm_ln_b'], 'm_gm_w_s': out['m_gm_w_s'], 'm_gm_b_s': out['m_gm_b_s'], 'm_conv_w': out['m_conv_w'], 'm_conv_b': out['m_conv_b'], 'm_dt_bias': out['m_dt_bias'], 'm_a_log': out['m_a_log'], 'm_d_skip': out['m_d_skip'], 'm_ssm_norm_w': out['m_ssm_norm_w'], 'm_w_out': out['m_w_out'], 'm_norm_mix_post': out['m_norm_mix_post'], 'm_norm_ffn_pre': out['m_norm_ffn_pre'], 'm_w_up': out['m_w_up'], 'm_w_down': out['m_w_down'], 'm_norm_ffn_post': out['m_norm_ffn_post'], 'v_norm_mix_pre': out['v_norm_mix_pre'], 'v_w_in': out['v_w_in'], 'v_gm_ln_w': out['v_gm_ln_w'], 'v_gm_ln_b': out['v_gm_ln_b'], 'v_gm_w_s': out['v_gm_w_s'], 'v_gm_b_s': out['v_gm_b_s'], 'v_conv_w': out['v_conv_w'], 'v_conv_b': out['v_conv_b'], 'v_dt_bias': out['v_dt_bias'], 'v_a_log': out['v_a_log'], 'v_d_skip': out['v_d_skip'], 'v_ssm_norm_w': out['v_ssm_norm_w'], 'v_w_out': out['v_w_out'], 'v_norm_mix_post': out['v_norm_mix_post'], 'v_norm_ffn_pre': out['v_norm_ffn_pre'], 'v_w_up': out['v_w_up'], 'v_w_down': out['v_w_down'], 'v_norm_ffn_post': out['v_norm_ffn_post']}


def _loss(weights, diff, rest, loss_target):
    with _jax.named_scope("forward"):
        args = {**rest, TWIN_DIFF_INPUT: diff, **{k: w.astype(_WEIGHT_DTYPES[k]) for k, w in weights.items()}}
        y = _forward(args)
    with _jax.named_scope("loss_head"):
        err = _jnp.square(y.astype(_jnp.float32) - loss_target)
        return 0.5 * _jnp.sum(_jnp.mean(err, axis=-1)) if err.ndim else 0.5 * err


def _adamw(w, g, m, v):
    m = ADAM_B1 * m + (1.0 - ADAM_B1) * g
    v = ADAM_B2 * v + (1.0 - ADAM_B2) * _jnp.square(g)
    m_hat = m / (1.0 - ADAM_B1 ** ADAM_STEP)
    v_hat = v / (1.0 - ADAM_B2 ** ADAM_STEP)
    delta = -ADAM_LR * (m_hat / (_jnp.sqrt(v_hat) + ADAM_EPS) + ADAM_WD * w)
    return delta, m, v


def reference(x, norm_mix_pre, w_in, gm_ln_w, gm_ln_b, gm_w_s, gm_b_s, conv_w, conv_b, dt_bias, a_log, d_skip, ssm_norm_w, w_out, norm_mix_post, norm_ffn_pre, w_up, w_down, norm_ffn_post, loss_target, m_norm_mix_pre, m_w_in, m_gm_ln_w, m_gm_ln_b, m_gm_w_s, m_gm_b_s, m_conv_w, m_conv_b, m_dt_bias, m_a_log, m_d_skip, m_ssm_norm_w, m_w_out, m_norm_mix_post, m_norm_ffn_pre, m_w_up, m_w_down, m_norm_ffn_post, v_norm_mix_pre, v_w_in, v_gm_ln_w, v_gm_ln_b, v_gm_w_s, v_gm_b_s, v_conv_w, v_conv_b, v_dt_bias, v_a_log, v_d_skip, v_ssm_norm_w, v_w_out, v_norm_mix_post, v_norm_ffn_pre, v_w_up, v_w_down, v_norm_ffn_post):
    given = dict(x=x, norm_mix_pre=norm_mix_pre, w_in=w_in, gm_ln_w=gm_ln_w, gm_ln_b=gm_ln_b, gm_w_s=gm_w_s, gm_b_s=gm_b_s, conv_w=conv_w, conv_b=conv_b, dt_bias=dt_bias, a_log=a_log, d_skip=d_skip, ssm_norm_w=ssm_norm_w, w_out=w_out, norm_mix_post=norm_mix_post, norm_ffn_pre=norm_ffn_pre, w_up=w_up, w_down=w_down, norm_ffn_post=norm_ffn_post, loss_target=loss_target, m_norm_mix_pre=m_norm_mix_pre, m_w_in=m_w_in, m_gm_ln_w=m_gm_ln_w, m_gm_ln_b=m_gm_ln_b, m_gm_w_s=m_gm_w_s, m_gm_b_s=m_gm_b_s, m_conv_w=m_conv_w, m_conv_b=m_conv_b, m_dt_bias=m_dt_bias, m_a_log=m_a_log, m_d_skip=m_d_skip, m_ssm_norm_w=m_ssm_norm_w, m_w_out=m_w_out, m_norm_mix_post=m_norm_mix_post, m_norm_ffn_pre=m_norm_ffn_pre, m_w_up=m_w_up, m_w_down=m_w_down, m_norm_ffn_post=m_norm_ffn_post, v_norm_mix_pre=v_norm_mix_pre, v_w_in=v_w_in, v_gm_ln_w=v_gm_ln_w, v_gm_ln_b=v_gm_ln_b, v_gm_w_s=v_gm_w_s, v_gm_b_s=v_gm_b_s, v_conv_w=v_conv_w, v_conv_b=v_conv_b, v_dt_bias=v_dt_bias, v_a_log=v_a_log, v_d_skip=v_d_skip, v_ssm_norm_w=v_ssm_norm_w, v_w_out=v_w_out, v_norm_mix_post=v_norm_mix_post, v_norm_ffn_pre=v_norm_ffn_pre, v_w_up=v_w_up, v_w_down=v_w_down, v_norm_ffn_post=v_norm_ffn_post)
    weights = {n: given[n] for n in TWIN_WEIGHTS}
    shared = {n: given[n] for n in SHARED_INPUTS}
    per_example = {n: given[n] for n in ['x']}
    grad_fn = _jax.value_and_grad(_loss, argnums=(0, 1))

    def one_microbatch(ex, loss_target):
        ex = dict(ex)
        diff = ex.pop(TWIN_DIFF_INPUT)
        return grad_fn(weights, diff, {**shared, **ex}, loss_target)

    if N_MICROBATCH == 1:
        loss, (grad_w, grad_x) = one_microbatch(per_example, given["loss_target"])
    else:
        def body(carry, xs):
            loss_sum, grad_sum = carry
            l_k, (gw_k, gx_k) = one_microbatch(xs[0], xs[1])
            with _jax.named_scope("update"):
                return (loss_sum + l_k, _jax.tree.map(_jnp.add, grad_sum, gw_k)), gx_k

        init = (_jnp.zeros((), _jnp.float32), _jax.tree.map(_jnp.zeros_like, weights))
        (loss, grad_w), grad_x = _jax.lax.scan(body, init, (per_example, given["loss_target"]))
    with _jax.named_scope("update"):
        delta_w, new_m, new_v = {}, {}, {}
        for n in TWIN_WEIGHTS:
            delta_w[n], new_m[n], new_v[n] = _adamw(weights[n], grad_w[n], given["m_" + n], given["v_" + n])
    return (loss, grad_x, *[grad_w[n] for n in TWIN_WEIGHTS], *[delta_w[n] for n in TWIN_WEIGHTS],
            *[new_m[n] for n in TWIN_WEIGHTS], *[new_v[n] for n in TWIN_WEIGHTS])
```

```python
import functools

import jax
import jax.numpy as jnp
from jax import lax
from jax.experimental import pallas as pl
from jax.experimental.pallas import tpu as pltpu

F32 = jnp.float32
BF16 = jnp.bfloat16
HI = lax.Precision.HIGHEST
MESH = pl.DeviceIdType.MESH

EPS = 1e-6
D_MODEL = 1024
GM_WIDTH = 512
SSM_WIDTH = 512
N_HEADS = 8
HEAD_DIM = 64
CHUNK = 128
SSM_GROUPS = 2
GROUP_W = SSM_WIDTH // SSM_GROUPS
SSM_STATE = 128
CONV_K = 4
CONV_CH = 1024
D_FF = 4096
IN_COLS = 2568
DT_PAD = 128
N_CHIPS = 4
N_DEV = 8

ADAM_LR = 0.001
ADAM_B1 = 0.9
ADAM_B2 = 0.999
ADAM_EPS = 1e-08
ADAM_WD = 0.01
ADAM_STEP = 10

VMEM_LIMIT_BYTES = 56 * 1024 * 1024
FF_TILE = 512


def _cparams(n_axes):
    return pltpu.CompilerParams(dimension_semantics=("arbitrary",) * n_axes, vmem_limit_bytes=VMEM_LIMIT_BYTES)


def _dot(a, b):
    return jnp.dot(a.astype(BF16), b.astype(BF16), preferred_element_type=F32)


def _dot_nt(a, b):
    return lax.dot_general(a.astype(BF16), b.astype(BF16), (((1,), (1,)), ((), ())), preferred_element_type=F32)


def _dot_tn(a, b):
    return lax.dot_general(a.astype(BF16), b.astype(BF16), (((0,), (0,)), ((), ())), preferred_element_type=F32)


def _dot_hi(a, b):
    return jnp.dot(a, b, precision=HI, preferred_element_type=F32)


def _dot_nt_hi(a, b):
    return lax.dot_general(a, b, (((1,), (1,)), ((), ())), precision=HI, preferred_element_type=F32)


def _dot_tn_hi(a, b):
    return lax.dot_general(a, b, (((0,), (0,)), ((), ())), precision=HI, preferred_element_type=F32)


def _sigmoid(x):
    return 1.0 / (1.0 + jnp.exp(-x))


_GELU_C = 0.7978845608028654
_GELU_A = 0.044715


def _gelu(x):
    t = jnp.tanh(_GELU_C * (x + _GELU_A * (x * x * x)))
    return 0.5 * x * (1.0 + t), t


def _gelu_grad(x, t):
    return 0.5 * (1.0 + t) + 0.5 * x * (1.0 - t * t) * (_GELU_C * (1.0 + 3.0 * _GELU_A * x * x))


def _rms_fwd(x, w):
    r = lax.rsqrt(jnp.mean(x * x, axis=-1, keepdims=True) + EPS)
    return x * r * w, r


def _rms_bwd(x, r, w, dy):
    g = dy * w
    dx = r * g - x * (r * r * r) * jnp.mean(g * x, axis=-1, keepdims=True)
    dw = jnp.sum(dy * x * r, axis=0, keepdims=True)
    return dx, dw


def _rows_call(name, body, tm, row_ins, const_ins, row_outs, acc_outs=()):
    n_rows = row_ins[0].shape[0]
    assert n_rows % tm == 0
    n_in = len(row_ins) + len(const_ins)
    n_ro = len(row_outs)

    def kern(*refs):
        accs = refs[n_in + n_ro:]

        @pl.when(pl.program_id(0) == 0)
        def _():
            for a in accs:
                a[...] = jnp.zeros_like(a)

        body(*refs)

    def whole(shape):
        nd = len(shape)
        return pl.BlockSpec(tuple(shape), lambda i: (0,) * nd)

    in_specs = [pl.BlockSpec((tm, a.shape[1]), lambda i: (i, 0)) for a in row_ins]
    in_specs += [whole(a.shape) for a in const_ins]
    out_specs = [pl.BlockSpec((tm, s.shape[1]), lambda i: (i, 0)) for s in row_outs]
    out_specs += [whole(s.shape) for s in acc_outs]
    return pl.pallas_call(
        kern, name=name, grid=(n_rows // tm,), in_specs=in_specs, out_specs=out_specs,
        out_shape=tuple(row_outs) + tuple(acc_outs), compiler_params=_cparams(1),
    )(*row_ins, *const_ins)


def _sds(shape, dtype):
    return jax.ShapeDtypeStruct(tuple(shape), dtype)


def _matmul_tn(name, a, b, tm, tn, tk, stacked=False):
    k_dim, m_dim = a.shape
    n_dim = b.shape[1]
    assert m_dim % tm == 0 and n_dim % tn == 0 and k_dim % tk == 0

    def kern(a_ref, b_ref, o_ref):
        @pl.when(pl.program_id(2) == 0)
        def _():
            o_ref[...] = jnp.zeros_like(o_ref)

        o_ref[...] += _dot_tn(a_ref[...], b_ref[...])

    if stacked:
        assert tm == m_dim
        out_shape = _sds((n_dim // tn, m_dim, tn), F32)
        out_spec = pl.BlockSpec((None, tm, tn), lambda i, j, k: (j, i, 0))
    else:
        out_shape = _sds((m_dim, n_dim), F32)
        out_spec = pl.BlockSpec((tm, tn), lambda i, j, k: (i, j))
    return pl.pallas_call(
        kern, name=name, grid=(m_dim // tm, n_dim // tn, k_dim // tk),
        in_specs=[pl.BlockSpec((tk, tm), lambda i, j, k: (k, i)), pl.BlockSpec((tk, tn), lambda i, j, k: (k, j))],
        out_specs=out_spec, out_shape=out_shape, compiler_params=_cparams(3),
    )(a, b)


def _inproj_fwd(x, nw, w_uv, w_xbc, w_z, w_dt, tm=256):
    n_tok = x.shape[0]

    def body(x_ref, nw_ref, wuv_ref, wxbc_ref, wz_ref, wdt_ref, puv_ref, pxbc_ref, pz_ref, pdt_ref):
        h, _ = _rms_fwd(x_ref[...], nw_ref[...])
        h = h.astype(BF16)
        puv_ref[...] = jnp.dot(h, wuv_ref[...], preferred_element_type=F32)
        pxbc_ref[...] = jnp.dot(h, wxbc_ref[...], preferred_element_type=F32)
        pz_ref[...] = jnp.dot(h, wz_ref[...], preferred_element_type=F32)
        pdt_ref[...] = jnp.dot(h, wdt_ref[...], preferred_element_type=F32)

    return _rows_call(
        "inproj_fwd", body, tm, [x], [nw, w_uv, w_xbc, w_z, w_dt],
        [_sds((n_tok, 2 * GM_WIDTH), F32), _sds((n_tok, CONV_CH), F32), _sds((n_tok, SSM_WIDTH), F32),
         _sds((n_tok, DT_PAD), F32)])


def _head_lane_mask(width, head):
    lane = lax.broadcasted_iota(jnp.int32, (1, width), 1)
    return (lane // HEAD_DIM) == head


def _gmlp_common(puv, lnw, lnb, gavg):
    u = puv[:, :GM_WIDTH]
    v = puv[:, GM_WIDTH:]
    gu, tu = _gelu(u)
    gv, tv = _gelu(v)
    mu = _dot_hi(gv, gavg)
    xc = gv - mu
    var = _dot_hi(xc * xc, gavg)
    rstd = lax.rsqrt(var + EPS)
    xhat = xc * rstd
    vn = xhat * lnw + lnb
    return u, v, gu, tu, tv, rstd, xhat, vn


def _tril_mask():
    r = lax.broadcasted_iota(jnp.int32, (CHUNK, CHUNK), 0)
    c = lax.broadcasted_iota(jnp.int32, (CHUNK, CHUNK), 1)
    return r >= c


def _gmlp_fwd(p_uv, lnw, lnb, gavg, w_s, bmap):
    n_tok = p_uv.shape[0]

    def body(puv_ref, lnw_ref, lnb_ref, g_ref, ws_ref, bmap_ref, ya_ref):
        _, _, gu, _, _, _, _, vn = _gmlp_common(puv_ref[...], lnw_ref[...], lnb_ref[...], g_ref[...])
        tri = _tril_mask()
        vnb = vn.astype(BF16)
        mixed = bmap_ref[...]
        for h in range(N_HEADS):
            wh = jnp.where(tri, ws_ref[h], 0.0).astype(BF16)
            full = jnp.dot(wh, vnb, preferred_element_type=F32)
            mixed = mixed + jnp.where(_head_lane_mask(GM_WIDTH, h), full, 0.0)
        ya_ref[...] = (gu * mixed).astype(BF16)

    (ya,) = _rows_call("gmlp_fwd", body, CHUNK, [p_uv], [lnw, lnb, gavg, w_s, bmap], [_sds((n_tok, GM_WIDTH), BF16)])
    return ya


def _ssd_pre(xr, prev, cw_ref, cb, pdt, dtb, alog, emap):
    rowi = lax.broadcasted_iota(jnp.int32, (CHUNK, 1), 0)

    def down(s):
        return jnp.where(rowi < s, pltpu.roll(prev, s, 0), pltpu.roll(xr, s, 0))

    shifted = [down(3), down(2), down(1), xr]
    xc = cb
    for k in range(CONV_K):
        xc = xc + cw_ref[k] * shifted[k]
    sg = _sigmoid(xc)
    xa = xc * sg
    pre = pdt + dtb
    dt = jnp.maximum(pre, 0.0) + jnp.log(1.0 + jnp.exp(-jnp.abs(pre)))
    a_neg = -jnp.exp(alog)
    ltri = _tril_mask().astype(F32)
    a_cs = _dot_hi(ltri, dt * a_neg)
    acs_map = _dot_hi(a_cs, emap)
    dt_map = _dot_hi(dt, emap)
    return dict(shifted=shifted, xc=xc, sg=sg, xa=xa, pre=pre, dt=dt, a_neg=a_neg, ltri=ltri, a_cs=a_cs,
                acs_map=acs_map, dt_map=dt_map, rowi=rowi)


def _ssd_maps(p):
    last = p["rowi"] == CHUNK - 1
    aq_map = jnp.sum(jnp.where(last, p["acs_map"], 0.0), axis=0, keepdims=True)
    e_exp = jnp.exp(p["acs_map"])
    dte = jnp.exp(aq_map - p["acs_map"])
    cd = jnp.exp(aq_map)
    return last, e_exp, dte, cd


def _head_decay(a_cs, a_cs_t, head, tri):
    lane = lax.broadcasted_iota(jnp.int32, (1, DT_PAD), 1)
    sub = lax.broadcasted_iota(jnp.int32, (DT_PAD, 1), 0)
    col = jnp.sum(jnp.where(lane == head, a_cs, 0.0), axis=1, keepdims=True)
    row = jnp.sum(jnp.where(sub == head, a_cs_t, 0.0), axis=0, keepdims=True)
    return jnp.exp(jnp.where(tri, col - row, -1e30))


def _gate_fwd(y, z, nw):
    sz = _sigmoid(z)
    zg = z * sz
    yg = y * zg
    outs, rs = [], []
    for g in range(SSM_GROUPS):
        gs = slice(g * GROUP_W, (g + 1) * GROUP_W)
        o, r = _rms_fwd(yg[:, gs], nw[:, gs])
        outs.append(o)
        rs.append(r)
    return sz, zg, yg, outs, rs


def _ssd_const_specs():
    def whole(shape):
        nd = len(shape)
        return pl.BlockSpec(tuple(shape), lambda b, c: (0,) * nd)
    return [whole((CONV_K, 1, CONV_CH)), whole((1, CONV_CH)), whole((1, DT_PAD)), whole((1, DT_PAD)),
            whole((1, SSM_WIDTH)), whole((1, SSM_WIDTH)), whole((DT_PAD, SSM_WIDTH))]


def _ssd_fwd(p_xbc, p_z, p_dt, conv_w, conv_b, dt_bias, a_log, dskip_map, norm_w, emap, n_seq):
    n_tok = p_xbc.shape[0]
    nc = n_tok // n_seq // CHUNK

    def body(xr_ref, z_ref, pdt_ref, cw_ref, cb_ref, dtb_ref, alog_ref, dsk_ref, nw_ref, e_ref,
             yb_ref, yssd_ref, sprev_ref, prev_scr, s_scr):
        @pl.when(pl.program_id(1) == 0)
        def _():
            prev_scr[...] = jnp.zeros_like(prev_scr)
            s_scr[...] = jnp.zeros_like(s_scr)

        xr = xr_ref[...]
        p = _ssd_pre(xr, prev_scr[...], cw_ref, cb_ref[...], pdt_ref[...], dtb_ref[...], alog_ref[...], e_ref[...])
        _, e_exp, dte, cd = _ssd_maps(p)
        xs = p["xa"][:, :SSM_WIDTH]
        xd = xs * p["dt_map"]
        a_cs_t = p["a_cs"].T
        tri = _tril_mask()
        s_old = s_scr[...]
        sprev_ref[...] = s_old
        for g in range(SSM_GROUPS):
            gs = slice(g * GROUP_W, (g + 1) * GROUP_W)
            bm = p["xa"][:, SSM_WIDTH + g * SSM_STATE: SSM_WIDTH + (g + 1) * SSM_STATE].astype(BF16)
            cm = p["xa"][:, SSM_WIDTH + (SSM_GROUPS + g) * SSM_STATE: SSM_WIDTH + (SSM_GROUPS + g + 1) * SSM_STATE].astype(BF16)
            cb_mat = _dot_nt(cm, bm)
            xdg = xd[:, gs].astype(BF16)
            y_g = _dot(cm, s_old[:, gs]) * e_exp[:, gs] + dsk_ref[:, gs] * xs[:, gs]
            for r in range(SSM_GROUPS * 2):
                dm = _head_decay(p["a_cs"], a_cs_t, g * 4 + r, tri)
                full = jnp.dot((cb_mat * dm).astype(BF16), xdg, preferred_element_type=F32)
                y_g = y_g + jnp.where(_head_lane_mask(GROUP_W, r), full, 0.0)
            yssd_ref[:, gs] = y_g
            s_scr[:, gs] = cd[:, gs] * s_old[:, gs] + _dot_tn(bm, xd[:, gs] * dte[:, gs])
        _, _, _, outs, _ = _gate_fwd(yssd_ref[...], z_ref[...], nw_ref[...])
        for g in range(SSM_GROUPS):
            yb_ref[:, g * GROUP_W:(g + 1) * GROUP_W] = outs[g].astype(BF16)
        prev_scr[...] = xr

    def rows(width):
        return pl.BlockSpec((CHUNK, width), lambda b, c: (b * nc + c, 0))

    return pl.pallas_call(
        body, name="ssd_fwd", grid=(n_seq, nc),
        in_specs=[rows(CONV_CH), rows(SSM_WIDTH), rows(DT_PAD)] + _ssd_const_specs(),
        out_specs=[rows(SSM_WIDTH), rows(SSM_WIDTH), rows(SSM_WIDTH)],
        out_shape=(_sds((n_tok, SSM_WIDTH), BF16), _sds((n_tok, SSM_WIDTH), F32), _sds((n_tok, SSM_WIDTH), F32)),
        scratch_shapes=[pltpu.VMEM((CHUNK, CONV_CH), F32), pltpu.VMEM((SSM_STATE, SSM_WIDTH), F32)],
        compiler_params=_cparams(2),
    )(p_xbc, p_z, p_dt, conv_w, conv_b, dt_bias, a_log, dskip_map, norm_w, emap)


def _outproj_fwd(ya, yb, x, w_out, nw_post, nw_pre2, tm=256):
    n_tok = x.shape[0]

    def body(ya_ref, yb_ref, x_ref, wo_ref, nwa_ref, nwb_ref, o_ref, x1_ref, h2_ref):
        o = jnp.dot(ya_ref[...], wo_ref[:GM_WIDTH, :], preferred_element_type=F32)
        o = o + jnp.dot(yb_ref[...], wo_ref[GM_WIDTH:, :], preferred_element_type=F32)
        on, _ = _rms_fwd(o, nwa_ref[...])
        x1 = x_ref[...] + on
        h2, _ = _rms_fwd(x1, nwb_ref[...])
        o_ref[...] = o
        x1_ref[...] = x1
        h2_ref[...] = h2.astype(BF16)

    return _rows_call("outproj_fwd", body, tm, [ya, yb, x], [w_out, nw_post, nw_pre2],
                      [_sds((n_tok, D_MODEL), F32), _sds((n_tok, D_MODEL), F32), _sds((n_tok, D_MODEL), BF16)])


def _mlp_fwd(h2, x1, tgt, w_up, w_down, nw, tm=256):
    n_tok = x1.shape[0]

    def body(h2_ref, x1_ref, tgt_ref, wup_ref, wdn_ref, nw_ref, up_ref, f_ref, dd_ref, dy_ref, loss_ref, dnw_ref):
        h2v = h2_ref[...]
        acc = jnp.zeros((tm, D_MODEL), F32)
        for j in range(D_FF // FF_TILE):
            cs = slice(j * FF_TILE, (j + 1) * FF_TILE)
            u = jnp.dot(h2v, wup_ref[:, cs], preferred_element_type=F32)
            up_ref[:, cs] = u
            f = jnp.square(jnp.maximum(u, 0.0)).astype(BF16)
            f_ref[:, cs] = f
            acc = acc + jnp.dot(f, wdn_ref[cs, :], preferred_element_type=F32)
        dn, r = _rms_fwd(acc, nw_ref[...])
        e = x1_ref[...] + dn - tgt_ref[...]
        loss_ref[...] += jnp.full(loss_ref.shape, (0.5 / D_MODEL) * jnp.sum(e * e), F32)
        dy = e * (1.0 / D_MODEL)
        dd, dnw = _rms_bwd(acc, r, nw_ref[...], dy)
        dy_ref[...] = dy
        dd_ref[...] = dd.astype(BF16)
        dnw_ref[...] += dnw

    return _rows_call(
        "mlp_fwd", body, tm, [h2, x1, tgt], [w_up, w_down, nw],
        [_sds((n_tok, D_FF), F32), _sds((n_tok, D_FF), BF16), _sds((n_tok, D_MODEL), BF16), _sds((n_tok, D_MODEL), F32)],
        [_sds((8, 128), F32), _sds((1, D_MODEL), F32)])


def _mlp_bwd(dd, up, x1, dy, w_down, w_up, nw, tm=256):
    n_tok = x1.shape[0]

    def body(dd_ref, up_ref, x1_ref, dy_ref, wdn_ref, wup_ref, nw_ref, dup_ref, dx1_ref, dnw_ref):
        ddv = dd_ref[...]
        acc = jnp.zeros((tm, D_MODEL), F32)
        for j in range(D_FF // FF_TILE):
            cs = slice(j * FF_TILE, (j + 1) * FF_TILE)
            df = _dot_nt(ddv, wdn_ref[cs, :])
            du = (df * (2.0 * jnp.maximum(up_ref[:, cs], 0.0))).astype(BF16)
            dup_ref[:, cs] = du
            acc = acc + _dot_nt(du, wup_ref[:, cs])
        x1v = x1_ref[...]
        _, r = _rms_fwd(x1v, nw_ref[...])
        dx, dnw = _rms_bwd(x1v, r, nw_ref[...], acc)
        dx1_ref[...] = dy_ref[...] + dx
        dnw_ref[...] += dnw

    return _rows_call("mlp_bwd", body, tm, [dd, up, x1, dy], [w_down, w_up, nw],
                      [_sds((n_tok, D_FF), BF16), _sds((n_tok, D_MODEL), F32)], [_sds((1, D_MODEL), F32)])


def _outproj_bwd(dx1, o, w_out, nw, tm=256):
    n_tok = dx1.shape[0]

    def body(dx1_ref, o_ref, wo_ref, nw_ref, do_ref, dya_ref, dyb_ref, dnw_ref):
        ov = o_ref[...]
        _, r = _rms_fwd(ov, nw_ref[...])
        do, dnw = _rms_bwd(ov, r, nw_ref[...], dx1_ref[...])
        dob = do.astype(BF16)
        do_ref[...] = dob
        dya_ref[...] = _dot_nt(dob, wo_ref[:GM_WIDTH, :])
        dyb_ref[...] = _dot_nt(dob, wo_ref[GM_WIDTH:, :])
        dnw_ref[...] += dnw

    return _rows_call("outproj_bwd", body, tm, [dx1, o], [w_out, nw],
                      [_sds((n_tok, D_MODEL), BF16), _sds((n_tok, GM_WIDTH), F32), _sds((n_tok, SSM_WIDTH), F32)],
                      [_sds((1, D_MODEL), F32)])


def _gmlp_bwd(p_uv, dya, lnw, lnb, gavg, w_s, bmap, emap):
    n_tok = p_uv.shape[0]

    def body(puv_ref, dya_ref, lnw_ref, lnb_ref, g_ref, ws_ref, bmap_ref, e_ref,
             dpuv_ref, dws_ref, dbs_ref, dlnw_ref, dlnb_ref):
        gavg_v = g_ref[...]
        lnw_v = lnw_ref[...]
        u, v, gu, tu, tv, rstd, xhat, vn = _gmlp_common(puv_ref[...], lnw_v, lnb_ref[...], gavg_v)
        tri = _tril_mask()
        vnb = vn.astype(BF16)
        whs = [jnp.where(tri, ws_ref[h], 0.0).astype(BF16) for h in range(N_HEADS)]
        mixed = bmap_ref[...]
        for h in range(N_HEADS):
            full = jnp.dot(whs[h], vnb, preferred_element_type=F32)
            mixed = mixed + jnp.where(_head_lane_mask(GM_WIDTH, h), full, 0.0)
        dy = dya_ref[...]
        du = dy * mixed * _gelu_grad(u, tu)
        dmixed = dy * gu
        dbs_ref[...] += _dot_nt_hi(dmixed, e_ref[...])
        dvn = jnp.zeros((CHUNK, GM_WIDTH), F32)
        for h in range(N_HEADS):
            mask = _head_lane_mask(GM_WIDTH, h)
            dmh = jnp.where(mask, dmixed, 0.0).astype(BF16)
            dvn = dvn + jnp.where(mask, _dot_tn(whs[h], dmh), 0.0)
            dws_ref[h] += jnp.where(tri, _dot_nt(dmh, vnb), 0.0)
        dlnw_ref[...] += jnp.sum(dvn * xhat, axis=0, keepdims=True)
        dlnb_ref[...] += jnp.sum(dvn, axis=0, keepdims=True)
        dxh = dvn * lnw_v
        dgv = rstd * (dxh - _dot_hi(dxh, gavg_v) - xhat * _dot_hi(dxh * xhat, gavg_v))
        dv = dgv * _gelu_grad(v, tv)
        dpuv_ref[:, :GM_WIDTH] = du.astype(BF16)
        dpuv_ref[:, GM_WIDTH:] = dv.astype(BF16)

    return _rows_call(
        "gmlp_bwd", body, CHUNK, [p_uv, dya], [lnw, lnb, gavg, w_s, bmap, emap], [_sds((n_tok, 2 * GM_WIDTH), BF16)],
        [_sds((N_HEADS, CHUNK, CHUNK), F32), _sds((CHUNK, DT_PAD), F32), _sds((1, GM_WIDTH), F32),
         _sds((1, GM_WIDTH), F32)])


def _ssd_bwd(p_xbc, p_z, p_dt, yssd, sprev, dyb, conv_w, conv_b, dt_bias, a_log, dskip_map, norm_w, emap, n_seq):
    n_tok = p_xbc.shape[0]
    nc = n_tok // n_seq // CHUNK

    def body(xr_ref, xprev_ref, z_ref, pdt_ref, yssd_ref, sprev_ref, dyb_ref,
             cw_ref, cb_ref, dtb_ref, alog_ref, dsk_ref, nw_ref, e_ref,
             dpxbc_ref, dpz_ref, dpdt_ref, dcw_ref, dcb_ref, ddtb_ref, dalog_ref, ddsk_ref, dnw_ref,
             ds_scr, nxt_scr, dxa_scr):
        step = pl.program_id(1)
        first = jnp.logical_and(pl.program_id(0) == 0, step == 0)

        @pl.when(first)
        def _():
            for a in (dcw_ref, dcb_ref, ddtb_ref, dalog_ref, ddsk_ref, dnw_ref):
                a[...] = jnp.zeros_like(a)

        @pl.when(step == 0)
        def _():
            ds_scr[...] = jnp.zeros_like(ds_scr)
            nxt_scr[...] = jnp.zeros_like(nxt_scr)

        chunk = nc - 1 - step
        xr = xr_ref[...]
        prev = jnp.where(chunk == 0, 0.0, xprev_ref[...])
        emap_v = e_ref[...]
        p = _ssd_pre(xr, prev, cw_ref, cb_ref[...], pdt_ref[...], dtb_ref[...], alog_ref[...], emap_v)
        last, e_exp, dte, cd = _ssd_maps(p)
        rowi = p["rowi"]
        xs = p["xa"][:, :SSM_WIDTH]
        xd = xs * p["dt_map"]
        a_cs_t = p["a_cs"].T
        tri = _tril_mask()
        dsk = dsk_ref[...]
        nw_v = nw_ref[...]

        yv = yssd_ref[...]
        zv = z_ref[...]
        sz, zg, yg, _, rs = _gate_fwd(yv, zv, nw_v)
        dout = dyb_ref[...]
        for g in range(SSM_GROUPS):
            gs = slice(g * GROUP_W, (g + 1) * GROUP_W)
            dyg_g, dnw_g = _rms_bwd(yg[:, gs], rs[g], nw_v[:, gs], dout[:, gs])
            dnw_ref[:, gs] += dnw_g
            dxa_scr[:, gs] = dyg_g
        dyg = dxa_scr[:, :SSM_WIDTH]
        d_y = dyg * zg
        dpz_ref[...] = (dyg * yv * (sz + zv * sz * (1.0 - sz))).astype(BF16)

        s_prev = sprev_ref[...]
        ds_next = ds_scr[...]
        lane_dt = lax.broadcasted_iota(jnp.int32, (1, DT_PAD), 1)
        da_cols = jnp.zeros((CHUNK, DT_PAD), F32)
        for g in range(SSM_GROUPS):
            gs = slice(g * GROUP_W, (g + 1) * GROUP_W)
            b_off = SSM_WIDTH + g * SSM_STATE
            c_off = SSM_WIDTH + (SSM_GROUPS + g) * SSM_STATE
            bm = p["xa"][:, b_off:b_off + SSM_STATE].astype(BF16)
            cm = p["xa"][:, c_off:c_off + SSM_STATE].astype(BF16)
            cb_mat = _dot_nt(cm, bm)
            d_yg = d_y[:, gs]
            d_ygb = d_yg.astype(BF16)
            xdg = xd[:, gs]
            xdgb = xdg.astype(BF16)
            ds_g = ds_next[:, gs]
            sp_g = s_prev[:, gs]
            bds = _dot(bm, ds_g)
            dcs = d_yg * e_exp[:, gs]
            d_c = _dot_nt(dcs, sp_g)
            ds_scr[:, gs] = cd[:, gs] * ds_g + _dot_tn(cm, dcs)
            d_b = _dot_nt(xdg * dte[:, gs], ds_g)
            dxd_g = bds * dte[:, gs]
            sum_dcb = jnp.zeros((CHUNK, CHUNK), F32)
            for r in range(SSM_GROUPS * 2):
                head = g * 4 + r
                mask = _head_lane_mask(GROUP_W, r)
                dm = _head_decay(p["a_cs"], a_cs_t, head, tri)
                m_mat = cb_mat * dm
                g_mat = _dot_nt(jnp.where(mask, d_yg, 0.0), xdgb)
                w_mat = g_mat * m_mat
                sum_dcb = sum_dcb + g_mat * dm
                dxd_g = dxd_g + jnp.where(mask, _dot_tn(m_mat, d_ygb), 0.0)
                da_h = jnp.sum(w_mat - w_mat.T, axis=1, keepdims=True)
                da_cols = da_cols + jnp.where(lane_dt == head, da_h, 0.0)
            d_c = d_c + _dot(sum_dcb, bm)
            d_b = d_b + _dot_tn(sum_dcb, cm)
            dxa_scr[:, b_off:b_off + SSM_STATE] = d_b
            dxa_scr[:, c_off:c_off + SSM_STATE] = d_c
            y_off_g = _dot(cm, sp_g) * e_exp[:, gs]
            t3 = bds * xdg * dte[:, gs]
            tail = jnp.sum(t3, axis=0, keepdims=True) + jnp.sum(ds_g * sp_g, axis=0, keepdims=True) * cd[:, gs]
            pre_g = d_yg * y_off_g - t3 + jnp.where(last, tail, 0.0)
            da_cols = da_cols + _dot_nt_hi(pre_g, emap_v[:, gs])
            ddt_g = _dot_nt_hi(dxd_g * xs[:, gs], emap_v[:, gs])
            ddsk_ref[...] += jnp.sum(_dot_nt_hi(d_yg * xs[:, gs], emap_v[:, gs]), axis=0, keepdims=True)
            dxa_scr[:, gs] = dxd_g * p["dt_map"][:, gs] + dsk[:, gs] * d_yg
            if g == 0:
                ddt = ddt_g
            else:
                ddt = ddt + ddt_g
        ddta = _dot_tn_hi(p["ltri"], da_cols)
        ddt = ddt + ddta * p["a_neg"]
        dalog_ref[...] += jnp.sum(ddta * p["dt"], axis=0, keepdims=True) * p["a_neg"]
        draw = ddt * _sigmoid(p["pre"])
        ddtb_ref[...] += jnp.sum(draw, axis=0, keepdims=True)
        dpdt_ref[...] = draw.astype(BF16)

        xc = p["xc"]
        sg = p["sg"]
        dxc = dxa_scr[...] * (sg + xc * sg * (1.0 - sg))
        dcb_ref[...] += jnp.sum(dxc, axis=0, keepdims=True)
        for k in range(CONV_K):
            dcw_ref[k] += jnp.sum(dxc * p["shifted"][k], axis=0, keepdims=True)
        nxt = nxt_scr[...]

        def up(s):
            return jnp.where(rowi >= CHUNK - s, pltpu.roll(nxt, CHUNK - s, 0), pltpu.roll(dxc, CHUNK - s, 0))

        dxr = cw_ref[3] * dxc + cw_ref[2] * up(1) + cw_ref[1] * up(2) + cw_ref[0] * up(3)
        dpxbc_ref[...] = dxr.astype(BF16)
        nxt_scr[...] = dxc

    def rows(width):
        return pl.BlockSpec((CHUNK, width), lambda b, s: (b * nc + nc - 1 - s, 0))

    prev_rows = pl.BlockSpec((CHUNK, CONV_CH), lambda b, s: (b * nc + jnp.maximum(nc - 2 - s, 0), 0))

    def whole(shape):
        nd = len(shape)
        return pl.BlockSpec(tuple(shape), lambda b, s: (0,) * nd)

    acc_shapes = [(CONV_K, 1, CONV_CH), (1, CONV_CH), (1, DT_PAD), (1, DT_PAD), (1, DT_PAD), (1, SSM_WIDTH)]
    return pl.pallas_call(
        body, name="ssd_bwd", grid=(n_seq, nc),
        in_specs=[rows(CONV_CH), prev_rows, rows(SSM_WIDTH), rows(DT_PAD), rows(SSM_WIDTH), rows(SSM_WIDTH),
                  rows(SSM_WIDTH)] + _ssd_const_specs(),
        out_specs=[rows(CONV_CH), rows(SSM_WIDTH), rows(DT_PAD)] + [whole(s) for s in acc_shapes],
        out_shape=tuple([_sds((n_tok, CONV_CH), BF16), _sds((n_tok, SSM_WIDTH), BF16), _sds((n_tok, DT_PAD), BF16)]
                        + [_sds(s, F32) for s in acc_shapes]),
        scratch_shapes=[pltpu.VMEM((SSM_STATE, SSM_WIDTH), F32), pltpu.VMEM((CHUNK, CONV_CH), F32),
                        pltpu.VMEM((CHUNK, CONV_CH), F32)],
        compiler_params=_cparams(2),
    )(p_xbc, p_xbc, p_z, p_dt, yssd, sprev, dyb, conv_w, conv_b, dt_bias, a_log, dskip_map, norm_w, emap)


def _inproj_bwd(dp_uv, dp_xbc, dp_z, dp_dt, x, dx1, w_uv, w_xbc, w_z, w_dt, nw, tm=256):
    n_tok = x.shape[0]

    def body(duv_ref, dxbc_ref, dz_ref, ddt_ref, x_ref, dx1_ref, wuv_ref, wxbc_ref, wz_ref, wdt_ref, nw_ref,
             gx_ref, h_ref, dnw_ref):
        dh = _dot_nt(duv_ref[...], wuv_ref[...]) + _dot_nt(dxbc_ref[...], wxbc_ref[...])
        dh = dh + _dot_nt(dz_ref[...], wz_ref[...]) + _dot_nt(ddt_ref[...], wdt_ref[...])
        xv = x_ref[...]
        h, r = _rms_fwd(xv, nw_ref[...])
        dx, dnw = _rms_bwd(xv, r, nw_ref[...], dh)
        gx_ref[...] = dx1_ref[...] + dx
        h_ref[...] = h.astype(BF16)
        dnw_ref[...] += dnw

    return _rows_call("inproj_bwd", body, tm, [dp_uv, dp_xbc, dp_z, dp_dt, x, dx1], [w_uv, w_xbc, w_z, w_dt, nw],
                      [_sds((n_tok, D_MODEL), F32), _sds((n_tok, D_MODEL), BF16)], [_sds((1, D_MODEL), F32)])


def _const_maps():
    lane = jnp.arange(SSM_WIDTH) // HEAD_DIM
    emap = (jnp.arange(DT_PAD)[:, None] == lane[None, :]).astype(F32)
    gavg = (lane[:, None] == lane[None, :]).astype(F32) / HEAD_DIM
    return emap, gavg


def _pad_lanes(v, width):
    return jnp.pad(v, ((0, 0), (0, width - v.shape[1])))


def _local_grads(x, tgt, w_in_b, w_out_b, w_up_b, w_down_b, conv_w, small):
    n_seq, seq_len, _ = x.shape
    n_tok = n_seq * seq_len
    x2 = x.reshape(n_tok, D_MODEL)
    tgt2 = tgt.reshape(n_tok, D_MODEL)
    emap, gavg = _const_maps()

    w_uv = w_in_b[:, :2 * GM_WIDTH]
    w_z = w_in_b[:, 2 * GM_WIDTH:2 * GM_WIDTH + SSM_WIDTH]
    w_xbc = w_in_b[:, 2 * GM_WIDTH + SSM_WIDTH:2 * GM_WIDTH + SSM_WIDTH + CONV_CH]
    w_dt = _pad_lanes(w_in_b[:, 2 * GM_WIDTH + SSM_WIDTH + CONV_CH:], DT_PAD)

    nw_pre = small["norm_mix_pre"]
    lnw = small["gm_ln_w"].reshape(1, GM_WIDTH)
    lnb = small["gm_ln_b"].reshape(1, GM_WIDTH)
    w_s = small["gm_w_s"]
    bmap = jnp.repeat(small["gm_b_s"].T, HEAD_DIM, axis=1)
    cw3 = conv_w.reshape(CONV_K, 1, CONV_CH)
    conv_b = small["conv_b"]
    dt_bias = _pad_lanes(small["dt_bias"], DT_PAD)
    a_log = _pad_lanes(small["a_log"], DT_PAD)
    dskip_map = jnp.repeat(small["d_skip"], HEAD_DIM, axis=1)
    ssm_nw = small["ssm_norm_w"]

    p_uv, p_xbc, p_z, p_dt = _inproj_fwd(x2, nw_pre, w_uv, w_xbc, w_z, w_dt)
    ya = _gmlp_fwd(p_uv, lnw, lnb, gavg, w_s, bmap)
    ssd_consts = (cw3, conv_b, dt_bias, a_log, dskip_map, ssm_nw, emap)
    yb, yssd, sprev = _ssd_fwd(p_xbc, p_z, p_dt, *ssd_consts, n_seq)
    o, x1, h2 = _outproj_fwd(ya, yb, x2, w_out_b, small["norm_mix_post"], small["norm_ffn_pre"])
    up, f, dd, dy, loss_acc, d_nffn_post = _mlp_fwd(h2, x1, tgt2, w_up_b, w_down_b, small["norm_ffn_post"])

    dup, dx1, d_nffn_pre = _mlp_bwd(dd, up, x1, dy, w_down_b, w_up_b, small["norm_ffn_pre"])
    g_up = _matmul_tn("dw_up", h2, dup, D_MODEL, D_MODEL, 512, stacked=True)
    g_down = _matmul_tn("dw_down", f, dd, 1024, D_MODEL, 512)
    do, dya, dyb, d_nmix_post = _outproj_bwd(dx1, o, w_out_b, small["norm_mix_post"])
    g_out_a = _matmul_tn("dw_out_a", ya, do, GM_WIDTH, D_MODEL, 512)
    g_out_b = _matmul_tn("dw_out_b", yb, do, SSM_WIDTH, D_MODEL, 512)
    dp_uv, d_ws, d_bs_t, d_lnw, d_lnb = _gmlp_bwd(p_uv, dya, lnw, lnb, gavg, w_s, bmap, emap)
    (dp_xbc, dp_z, dp_dt, d_cw, d_cb, d_dtb, d_alog, d_dsk, d_ssm_nw) = _ssd_bwd(
        p_xbc, p_z, p_dt, yssd, sprev, dyb, *ssd_consts, n_seq)
    gx, h, d_nmix_pre = _inproj_bwd(dp_uv, dp_xbc, dp_z, dp_dt, x2, dx1, w_uv, w_xbc, w_z, w_dt, nw_pre)
    g_uv = _matmul_tn("dw_in_uv", h, dp_uv, D_MODEL, 2 * GM_WIDTH, 512)
    g_xbc = _matmul_tn("dw_in_xbc", h, dp_xbc, D_MODEL, CONV_CH, 512)
    g_z = _matmul_tn("dw_in_z", h, dp_z, D_MODEL, SSM_WIDTH, 512)
    g_dt = _matmul_tn("dw_in_dt", h, dp_dt, D_MODEL, DT_PAD, 512)

    g_in = jnp.concatenate([g_uv, g_z, g_xbc, g_dt[:, :N_HEADS]], axis=1)
    g_out = jnp.concatenate([g_out_a, g_out_b], axis=0)
    small_grads = {
        "norm_mix_pre": d_nmix_pre, "gm_ln_w": d_lnw.reshape(N_HEADS, HEAD_DIM), "gm_ln_b": d_lnb.reshape(N_HEADS, HEAD_DIM),
        "gm_w_s": d_ws, "gm_b_s": d_bs_t[:, :N_HEADS].T, "conv_w": d_cw.reshape(CONV_K, CONV_CH), "conv_b": d_cb,
        "dt_bias": d_dtb[:, :N_HEADS], "a_log": d_alog[:, :N_HEADS], "d_skip": d_dsk[:, :N_HEADS],
        "ssm_norm_w": d_ssm_nw, "norm_mix_post": d_nmix_post, "norm_ffn_pre": d_nffn_pre,
        "norm_ffn_post": d_nffn_post,
    }
    return loss_acc[0, 0], gx.reshape(x.shape), g_in, g_out, g_up, g_down, small_grads


_HBM = pl.BlockSpec(memory_space=pltpu.HBM)


def _position():
    x, y, c = lax.axis_index("x"), lax.axis_index("y"), lax.axis_index("c")
    chips = [(1 - x, y), (x, 1 - y), (1 - x, 1 - y)]
    return x, y, c, chips


def _allgather_chips(arrs):
    n = len(arrs)

    def body(*refs):
        ins, outs = refs[:n], refs[n:2 * n]
        send_sems, recv_sems, local_sems = refs[2 * n:]
        x, y, c, chips = _position()
        me = 2 * x + y
        sibling = (x, y, 1 - c)

        def copy(a, k, src, dst, to):
            return pltpu.make_async_remote_copy(src_ref=src, dst_ref=dst, send_sem=send_sems.at[a, k],
                                                recv_sem=recv_sems.at[a, k], device_id=to, device_id_type=MESH)

        def half_rows(a, pc):
            half = ins[a].shape[0] // 2
            return pl.ds(pc * half, half)

        started = []
        for a in range(n):
            local = pltpu.make_async_copy(ins[a], outs[a].at[me], local_sems.at[a])
            local.start()
            started.append(local)
        sends = []
        for a in range(n):
            mine = half_rows(a, c)
            for k, (px, py) in enumerate(chips):
                cp = copy(a, k, ins[a].at[mine], outs[a].at[me, mine], (px, py, c))
                cp.start()
                sends.append(cp)
        for a in range(n):
            for k, (px, py) in enumerate(chips):
                blk = outs[a].at[2 * px + py, half_rows(a, c)]
                copy(a, k, blk, blk, (px, py, c)).wait_recv()
                fw = copy(a, 3 + k, blk, blk, sibling)
                fw.start()
                sends.append(fw)
        for a in range(n):
            for k, (px, py) in enumerate(chips):
                blk = outs[a].at[2 * px + py, half_rows(a, 1 - c)]
                copy(a, 3 + k, blk, blk, sibling).wait_recv()
        for cp in sends:
            cp.wait_send()
        for local in started:
            local.wait()

    return pl.pallas_call(
        body, name="allgather_weights",
        out_shape=tuple(_sds((N_CHIPS,) + a.shape, a.dtype) for a in arrs),
        in_specs=[_HBM] * n, out_specs=tuple([_HBM] * n),
        scratch_shapes=[pltpu.SemaphoreType.DMA((n, 6)), pltpu.SemaphoreType.DMA((n, 6)), pltpu.SemaphoreType.DMA((n,))],
    )(*arrs)


def _pair_exchange(grads):
    n = len(grads)

    def body(*refs):
        ins, kept, got = refs[:n], refs[n:2 * n], refs[2 * n:3 * n]
        send_sems, recv_sems, local_sems = refs[3 * n:]
        x, y, c, _ = _position()
        sibling = (x, y, 1 - c)
        cps = []
        for a in range(n):
            half = ins[a].shape[1] // 2
            local = pltpu.make_async_copy(ins[a].at[:, pl.ds(c * half, half), :], kept[a], local_sems.at[a])
            local.start()
            cp = pltpu.make_async_remote_copy(
                src_ref=ins[a].at[:, pl.ds((1 - c) * half, half), :], dst_ref=got[a], send_sem=send_sems.at[a],
                recv_sem=recv_sems.at[a], device_id=sibling, device_id_type=MESH)
            cp.start()
            cps.append((local, cp))
        for local, cp in cps:
            cp.wait()
            local.wait()

    halves = tuple(_sds((N_CHIPS, g.shape[1] // 2, g.shape[2]), g.dtype) for g in grads)
    outs = pl.pallas_call(
        body, name="grad_pair_exchange", out_shape=halves + halves,
        in_specs=[_HBM] * n, out_specs=tuple([_HBM] * (2 * n)),
        scratch_shapes=[pltpu.SemaphoreType.DMA((n,)), pltpu.SemaphoreType.DMA((n,)), pltpu.SemaphoreType.DMA((n,))],
    )(*grads)
    return outs[:n], outs[n:]


def _chip_exchange(hsums):
    n = len(hsums)

    def body(*refs):
        ins, outs = refs[:n], refs[n:2 * n]
        send_sems, recv_sems, local_sems = refs[2 * n:]
        x, y, c, chips = _position()
        me = 2 * x + y
        cps = []
        for a in range(n):
            local = pltpu.make_async_copy(ins[a].at[me], outs[a].at[me], local_sems.at[a])
            local.start()
            cps.append(local)
            for k, (px, py) in enumerate(chips):
                cp = pltpu.make_async_remote_copy(
                    src_ref=ins[a].at[2 * px + py], dst_ref=outs[a].at[me], send_sem=send_sems.at[a, k],
                    recv_sem=recv_sems.at[a, k], device_id=(px, py, c), device_id_type=MESH)
                cp.start()
                cps.append(cp)
        for cp in cps:
            cp.wait()

    return pl.pallas_call(
        body, name="grad_chip_exchange", out_shape=tuple(_sds(h.shape, h.dtype) for h in hsums),
        in_specs=[_HBM] * n, out_specs=tuple([_HBM] * n),
        scratch_shapes=[pltpu.SemaphoreType.DMA((n, 3)), pltpu.SemaphoreType.DMA((n, 3)), pltpu.SemaphoreType.DMA((n,))],
    )(*hsums)


def _pair_gather(reds):
    n = len(reds)

    def body(*refs):
        ins, outs = refs[:n], refs[n:2 * n]
        send_sems, recv_sems, local_sems = refs[2 * n:]
        x, y, c, _ = _position()
        sibling = (x, y, 1 - c)
        cps = []
        for a in range(n):
            half = ins[a].shape[0]
            rows = outs[a].at[pl.ds(c * half, half), :]
            local = pltpu.make_async_copy(ins[a], rows, local_sems.at[a])
            local.start()
            cp = pltpu.make_async_remote_copy(src_ref=ins[a], dst_ref=rows, send_sem=send_sems.at[a],
                                              recv_sem=recv_sems.at[a], device_id=sibling, device_id_type=MESH)
            cp.start()
            cps.append((local, cp))
        for local, cp in cps:
            cp.wait()
            local.wait()

    return pl.pallas_call(
        body, name="grad_pair_gather", out_shape=tuple(_sds((2 * r.shape[0], r.shape[1]), r.dtype) for r in reds),
        in_specs=[_HBM] * n, out_specs=tuple([_HBM] * n),
        scratch_shapes=[pltpu.SemaphoreType.DMA((n,)), pltpu.SemaphoreType.DMA((n,)), pltpu.SemaphoreType.DMA((n,))],
    )(*reds)


def _small_allreduce(packed):
    m_per, n_cols = packed.shape

    def body(x_ref, sum_ref, all_ref, send_sems, recv_sems, local_sem):
        x, y, c, chips = _position()
        me, sibling = (x, y, c), (x, y, 1 - c)

        def rows(px, py, pc):
            return all_ref.at[pl.ds((4 * px + 2 * py + pc) * m_per, m_per), :]

        def copy(k, block, to, src=None):
            return pltpu.make_async_remote_copy(
                src_ref=rows(*block) if src is None else src, dst_ref=rows(*block), send_sem=send_sems.at[k],
                recv_sem=recv_sems.at[k], device_id=to, device_id_type=MESH)

        mine = pltpu.make_async_copy(x_ref, rows(*me), local_sem)
        mine.start()
        first = [copy(0, me, sibling, src=x_ref)]
        first += [copy(1 + j, me, (*chip, c), src=x_ref) for j, chip in enumerate(chips)]
        for cp in first:
            cp.start()
        passed = [copy(4 + j, (*chip, c), sibling) for j, chip in enumerate(chips)]
        for j, chip in enumerate(chips):
            copy(1 + j, (*chip, c), me).wait_recv()
            passed[j].start()
        copy(0, sibling, me).wait_recv()
        for j, chip in enumerate(chips):
            copy(4 + j, (*chip, 1 - c), me).wait_recv()
        for cp in first + passed:
            cp.wait_send()
        mine.wait()
        acc = all_ref[0:m_per, :]
        for d in range(1, N_DEV):
            acc = acc + all_ref[d * m_per:(d + 1) * m_per, :]
        sum_ref[...] = acc

    vmem = pl.BlockSpec(memory_space=pltpu.VMEM)
    total, _ = pl.pallas_call(
        body, name="small_allreduce",
        out_shape=(_sds((m_per, n_cols), F32), _sds((N_DEV * m_per, n_cols), F32)),
        in_specs=[vmem], out_specs=(vmem, vmem),
        scratch_shapes=[pltpu.SemaphoreType.DMA((7,)), pltpu.SemaphoreType.DMA((7,)), pltpu.SemaphoreType.DMA],
    )(packed)
    return total


def _flat_rows(shape):
    rows = 1
    for s in shape[:-1]:
        rows *= s
    return rows


def _pair_sum(kept, got, tm=256):
    shape = kept.shape
    k2 = kept.reshape(_flat_rows(shape), shape[-1])
    g2 = got.reshape(k2.shape)

    def body(k_ref, g_ref, o_ref):
        o_ref[...] = (k_ref[...] + g_ref[...]).astype(BF16)

    (out,) = _rows_call("grad_pair_sum", body, tm, [k2, g2], [], [_sds(k2.shape, BF16)])
    return out.reshape(shape)


def _chip_sum(slabs, tm=256):
    _, rows, cols = slabs.shape

    def body(s_ref, o_ref):
        acc = s_ref[0].astype(F32)
        for k in range(1, N_CHIPS):
            acc = acc + s_ref[k].astype(F32)
        o_ref[...] = acc

    return pl.pallas_call(
        body, name="grad_chip_sum", grid=(rows // tm,),
        in_specs=[pl.BlockSpec((N_CHIPS, tm, cols), lambda i: (0, i, 0))],
        out_specs=pl.BlockSpec((tm, cols), lambda i: (i, 0)), out_shape=_sds((rows, cols), F32),
        compiler_params=_cparams(1),
    )(slabs)


def _adam_math(w, g, m, v):
    m2 = ADAM_B1 * m + (1.0 - ADAM_B1) * g
    v2 = ADAM_B2 * v + (1.0 - ADAM_B2) * (g * g)
    m_hat = m2 / (1.0 - ADAM_B1 ** ADAM_STEP)
    v_hat = v2 / (1.0 - ADAM_B2 ** ADAM_STEP)
    delta = -ADAM_LR * (m_hat / (jnp.sqrt(v_hat) + ADAM_EPS) + ADAM_WD * w)
    return delta, m2, v2


def _adamw(name, w, g, m, v, tm):
    def body(w_ref, g_ref, m_ref, v_ref, d_ref, m2_ref, v2_ref):
        d, m2, v2 = _adam_math(w_ref[...], g_ref[...], m_ref[...], v_ref[...])
        d_ref[...] = d
        m2_ref[...] = m2
        v2_ref[...] = v2

    return _rows_call(name, body, tm, [w, g, m, v], [], [_sds(w.shape, F32)] * 3)


_SMALL_NAMES = ("norm_mix_pre", "gm_ln_w", "gm_ln_b", "gm_w_s", "gm_b_s", "conv_w", "conv_b", "dt_bias", "a_log",
                "d_skip", "ssm_norm_w", "norm_mix_post", "norm_ffn_pre", "norm_ffn_post")
_PACK_COLS = 1024


def _pack(parts):
    flat = jnp.concatenate([parts[n].reshape(-1) for n in _SMALL_NAMES])
    rows = -(-flat.shape[0] // (8 * _PACK_COLS)) * 8
    flat = jnp.pad(flat, (0, rows * _PACK_COLS - flat.shape[0]))
    return flat.reshape(rows, _PACK_COLS)


def _unpack(packed, shapes):
    flat = packed.reshape(-1)
    out, off = {}, 0
    for n in _SMALL_NAMES:
        size = 1
        for s in shapes[n]:
            size *= s
        out[n] = flat[off:off + size].reshape(shapes[n])
        off += size
    return out


def kernel(x, norm_mix_pre, w_in, gm_ln_w, gm_ln_b, gm_w_s, gm_b_s, conv_w, conv_b, dt_bias, a_log, d_skip, ssm_norm_w, w_out, norm_mix_post, norm_ffn_pre, w_up, w_down, norm_ffn_post, loss_target, m_norm_mix_pre, m_w_in, m_gm_ln_w, m_gm_ln_b, m_gm_w_s, m_gm_b_s, m_conv_w, m_conv_b, m_dt_bias, m_a_log, m_d_skip, m_ssm_norm_w, m_w_out, m_norm_mix_post, m_norm_ffn_pre, m_w_up, m_w_down, m_norm_ffn_post, v_norm_mix_pre, v_w_in, v_gm_ln_w, v_gm_ln_b, v_gm_w_s, v_gm_b_s, v_conv_w, v_conv_b, v_dt_bias, v_a_log, v_d_skip, v_ssm_norm_w, v_w_out, v_norm_mix_post, v_norm_ffn_pre, v_w_up, v_w_down, v_norm_ffn_post):
    params = dict(norm_mix_pre=norm_mix_pre, w_in=w_in, gm_ln_w=gm_ln_w, gm_ln_b=gm_ln_b, gm_w_s=gm_w_s, gm_b_s=gm_b_s,
                  conv_w=conv_w, conv_b=conv_b, dt_bias=dt_bias, a_log=a_log, d_skip=d_skip, ssm_norm_w=ssm_norm_w,
                  w_out=w_out, norm_mix_post=norm_mix_post, norm_ffn_pre=norm_ffn_pre, w_up=w_up, w_down=w_down,
                  norm_ffn_post=norm_ffn_post)
    mom1 = dict(norm_mix_pre=m_norm_mix_pre, w_in=m_w_in, gm_ln_w=m_gm_ln_w, gm_ln_b=m_gm_ln_b, gm_w_s=m_gm_w_s,
                gm_b_s=m_gm_b_s, conv_w=m_conv_w, conv_b=m_conv_b, dt_bias=m_dt_bias, a_log=m_a_log, d_skip=m_d_skip,
                ssm_norm_w=m_ssm_norm_w, w_out=m_w_out, norm_mix_post=m_norm_mix_post, norm_ffn_pre=m_norm_ffn_pre,
                w_up=m_w_up, w_down=m_w_down, norm_ffn_post=m_norm_ffn_post)
    mom2 = dict(norm_mix_pre=v_norm_mix_pre, w_in=v_w_in, gm_ln_w=v_gm_ln_w, gm_ln_b=v_gm_ln_b, gm_w_s=v_gm_w_s,
                gm_b_s=v_gm_b_s, conv_w=v_conv_w, conv_b=v_conv_b, dt_bias=v_dt_bias, a_log=v_a_log, d_skip=v_d_skip,
                ssm_norm_w=v_ssm_norm_w, w_out=v_w_out, norm_mix_post=v_norm_mix_post, norm_ffn_pre=v_norm_ffn_pre,
                w_up=v_w_up, w_down=v_w_down, norm_ffn_post=v_norm_ffn_post)
    names = list(params)
    big = ("w_in", "w_out", "w_up", "w_down")
    chip = 2 * lax.axis_index("x") + lax.axis_index("y")

    shards = [params[n][0].astype(BF16) for n in big]
    conv_shard = jnp.pad(conv_w[0], ((0, 16 - CONV_K), (0, 0)))
    g_in4, g_out4, g_up4, g_down4, g_conv4 = _allgather_chips(shards + [conv_shard])
    w_in_b = jnp.transpose(g_in4, (1, 0, 2)).reshape(D_MODEL, IN_COLS)
    w_out_b = g_out4.reshape(D_MODEL, D_MODEL)
    w_up_b = jnp.transpose(g_up4, (1, 0, 2)).reshape(D_MODEL, D_FF)
    w_down_b = g_down4.reshape(D_FF, D_MODEL)
    conv_full = jnp.transpose(g_conv4[:, :CONV_K, :], (1, 0, 2)).reshape(CONV_K, CONV_CH)

    small = {n: params[n][0] if params[n].ndim >= 3 else params[n] for n in _SMALL_NAMES if n != "conv_w"}
    loss_part, grad_x, g_in, g_out, g_up, g_down, small_grads = _local_grads(
        x, loss_target, w_in_b, w_out_b, w_up_b, w_down_b, conv_full, small)
    loss = lax.psum(loss_part, ("x", "y", "c"))

    stacked = [jnp.transpose(g_in.reshape(D_MODEL, N_CHIPS, IN_COLS // N_CHIPS), (1, 0, 2)),
               g_out.reshape(N_CHIPS, D_MODEL // N_CHIPS, D_MODEL), g_up,
               g_down.reshape(N_CHIPS, D_FF // N_CHIPS, D_MODEL)]
    kept, got = _pair_exchange(stacked)
    hsums = [_pair_sum(k, g, tm) for k, g, tm in zip(kept, got, (256, 128, 256, 256))]
    slabs = _chip_exchange(hsums)
    reds = [_chip_sum(s, tm) for s, tm in zip(slabs, (256, 128, 256, 256))]
    big_grads = dict(zip(big, _pair_gather(reds)))

    small_shapes = {n: (small_grads[n].shape) for n in _SMALL_NAMES}
    small_sum = _unpack(_small_allreduce(_pack(small_grads)), small_shapes)
    small_sum["conv_w"] = lax.dynamic_slice_in_dim(small_sum["conv_w"], chip * (CONV_CH // N_CHIPS), CONV_CH // N_CHIPS, axis=1)

    grads, delta, new_m, new_v = {}, {}, {}, {}
    for n, tm in zip(big, (256, 128, 256, 256)):
        g = big_grads[n]
        d, m2, v2 = _adamw("adamw_" + n, params[n][0], g, mom1[n][0], mom2[n][0], tm)
        grads[n], delta[n], new_m[n], new_v[n] = g[None], d[None], m2[None], v2[None]
    local_shapes = {n: params[n].shape[1:] if params[n].ndim >= 3 else params[n].shape for n in _SMALL_NAMES}
    flat = lambda tree: {n: tree[n].reshape(local_shapes[n]) for n in _SMALL_NAMES}
    packed = [_pack(flat(t)) for t in (params, small_sum, mom1, mom2)]
    d_p, m_p, v_p = _adamw("adamw_small", *packed, packed[0].shape[0])
    for src, dst in ((d_p, delta), (m_p, new_m), (v_p, new_v)):
        for n, val in _unpack(src, local_shapes).items():
            dst[n] = val.reshape(params[n].shape)
    for n in _SMALL_NAMES:
        grads[n] = small_sum[n].reshape(params[n].shape)

    out = [loss, grad_x]
    for tree in (grads, delta, new_m, new_v):
        out += [tree[n] for n in names]
    return tuple(out)
```

```python
import functools

import jax
import jax.numpy as jnp
from jax import lax
from jax.experimental import pallas as pl
from jax.experimental.pallas import tpu as pltpu

F32 = jnp.float32
BF16 = jnp.bfloat16
HI = lax.Precision.HIGHEST
MESH = pl.DeviceIdType.MESH

EPS = 1e-6
D_MODEL = 1024
GM_WIDTH = 512
SSM_WIDTH = 512
N_HEADS = 8
HEAD_DIM = 64
CHUNK = 128
SSM_GROUPS = 2
GROUP_W = SSM_WIDTH // SSM_GROUPS
SSM_STATE = 128
CONV_K = 4
CONV_CH = 1024
D_FF = 4096
IN_COLS = 2568
DT_PAD = 128
N_CHIPS = 4
N_DEV = 8

ADAM_LR = 0.001
ADAM_B1 = 0.9
ADAM_B2 = 0.999
ADAM_EPS = 1e-08
ADAM_WD = 0.01
ADAM_STEP = 10

VMEM_LIMIT_BYTES = 56 * 1024 * 1024
FF_TILE = 512


def _cparams(n_axes):
    return pltpu.CompilerParams(dimension_semantics=("arbitrary",) * n_axes, vmem_limit_bytes=VMEM_LIMIT_BYTES)


def _dot(a, b):
    return jnp.dot(a.astype(BF16), b.astype(BF16), preferred_element_type=F32)


def _dot_nt(a, b):
    return lax.dot_general(a.astype(BF16), b.astype(BF16), (((1,), (1,)), ((), ())), preferred_element_type=F32)


def _dot_tn(a, b):
    return lax.dot_general(a.astype(BF16), b.astype(BF16), (((0,), (0,)), ((), ())), preferred_element_type=F32)


def _dot_hi(a, b):
    return jnp.dot(a, b, precision=HI, preferred_element_type=F32)


def _dot_nt_hi(a, b):
    return lax.dot_general(a, b, (((1,), (1,)), ((), ())), precision=HI, preferred_element_type=F32)


def _dot_tn_hi(a, b):
    return lax.dot_general(a, b, (((0,), (0,)), ((), ())), precision=HI, preferred_element_type=F32)


def _sigmoid(x):
    return 1.0 / (1.0 + jnp.exp(-x))


_GELU_C = 0.7978845608028654
_GELU_A = 0.044715


def _gelu(x):
    t = jnp.tanh(_GELU_C * (x + _GELU_A * (x * x * x)))
    return 0.5 * x * (1.0 + t), t


def _gelu_grad(x, t):
    return 0.5 * (1.0 + t) + 0.5 * x * (1.0 - t * t) * (_GELU_C * (1.0 + 3.0 * _GELU_A * x * x))


def _rms_fwd(x, w):
    r = lax.rsqrt(jnp.mean(x * x, axis=-1, keepdims=True) + EPS)
    return x * r * w, r


def _rms_bwd(x, r, w, dy):
    g = dy * w
    dx = r * g - x * (r * r * r) * jnp.mean(g * x, axis=-1, keepdims=True)
    dw = jnp.sum(dy * x * r, axis=0, keepdims=True)
    return dx, dw


def _rows_call(name, body, tm, row_ins, const_ins, row_outs, acc_outs=()):
    n_rows = row_ins[0].shape[0]
    assert n_rows % tm == 0
    n_in = len(row_ins) + len(const_ins)
    n_ro = len(row_outs)

    def kern(*refs):
        accs = refs[n_in + n_ro:]

        @pl.when(pl.program_id(0) == 0)
        def _():
            for a in accs:
                a[...] = jnp.zeros_like(a)

        body(*refs)

    def whole(shape):
        nd = len(shape)
        return pl.BlockSpec(tuple(shape), lambda i: (0,) * nd)

    in_specs = [pl.BlockSpec((tm, a.shape[1]), lambda i: (i, 0)) for a in row_ins]
    in_specs += [whole(a.shape) for a in const_ins]
    out_specs = [pl.BlockSpec((tm, s.shape[1]), lambda i: (i, 0)) for s in row_outs]
    out_specs += [whole(s.shape) for s in acc_outs]
    return pl.pallas_call(
        kern, name=name, grid=(n_rows // tm,), in_specs=in_specs, out_specs=out_specs,
        out_shape=tuple(row_outs) + tuple(acc_outs), compiler_params=_cparams(1),
    )(*row_ins, *const_ins)


def _sds(shape, dtype):
    return jax.ShapeDtypeStruct(tuple(shape), dtype)


def _matmul_tn(name, a, b, tm, tn, tk, stacked=False):
    k_dim, m_dim = a.shape
    n_dim = b.shape[1]
    assert m_dim % tm == 0 and n_dim % tn == 0 and k_dim % tk == 0

    def kern(a_ref, b_ref, o_ref):
        @pl.when(pl.program_id(2) == 0)
        def _():
            o_ref[...] = jnp.zeros_like(o_ref)

        o_ref[...] += _dot_tn(a_ref[...], b_ref[...])

    if stacked:
        assert tm == m_dim
        out_shape = _sds((n_dim // tn, m_dim, tn), F32)
        out_spec = pl.BlockSpec((None, tm, tn), lambda i, j, k: (j, i, 0))
    else:
        out_shape = _sds((m_dim, n_dim), F32)
        out_spec = pl.BlockSpec((tm, tn), lambda i, j, k: (i, j))
    return pl.pallas_call(
        kern, name=name, grid=(m_dim // tm, n_dim // tn, k_dim // tk),
        in_specs=[pl.BlockSpec((tk, tm), lambda i, j, k: (k, i)), pl.BlockSpec((tk, tn), lambda i, j, k: (k, j))],
        out_specs=out_spec, out_shape=out_shape, compiler_params=_cparams(3),
    )(a, b)


def _inproj_fwd(x, nw, w_uv, w_xbc, w_z, w_dt, tm=256):
    n_tok = x.shape[0]

    def body(x_ref, nw_ref, wuv_ref, wxbc_ref, wz_ref, wdt_ref, puv_ref, pxbc_ref, pz_ref, pdt_ref):
        h, _ = _rms_fwd(x_ref[...], nw_ref[...])
        h = h.astype(BF16)
        puv_ref[...] = jnp.dot(h, wuv_ref[...], preferred_element_type=F32)
        pxbc_ref[...] = jnp.dot(h, wxbc_ref[...], preferred_element_type=F32)
        pz_ref[...] = jnp.dot(h, wz_ref[...], preferred_element_type=F32)
        pdt_ref[...] = jnp.dot(h, wdt_ref[...], preferred_element_type=F32)

    return _rows_call(
        "inproj_fwd", body, tm, [x], [nw, w_uv, w_xbc, w_z, w_dt],
        [_sds((n_tok, 2 * GM_WIDTH), F32), _sds((n_tok, CONV_CH), F32), _sds((n_tok, SSM_WIDTH), F32),
         _sds((n_tok, DT_PAD), F32)])


def _head_lane_mask(width, head):
    lane = lax.broadcasted_iota(jnp.int32, (1, width), 1)
    return (lane // HEAD_DIM) == head


def _gmlp_common(puv, lnw, lnb, gavg):
    u = puv[:, :GM_WIDTH]
    v = puv[:, GM_WIDTH:]
    gu, tu = _gelu(u)
    gv, tv = _gelu(v)
    mu = _dot_hi(gv, gavg)
    xc = gv - mu
    var = _dot_hi(xc * xc, gavg)
    rstd = lax.rsqrt(var + EPS)
    xhat = xc * rstd
    vn = xhat * lnw + lnb
    return u, v, gu, tu, tv, rstd, xhat, vn


def _tril_mask():
    r = lax.broadcasted_iota(jnp.int32, (CHUNK, CHUNK), 0)
    c = lax.broadcasted_iota(jnp.int32, (CHUNK, CHUNK), 1)
    return r >= c


def _gmlp_fwd(p_uv, lnw, lnb, gavg, w_s, bmap):
    n_tok = p_uv.shape[0]

    def body(puv_ref, lnw_ref, lnb_ref, g_ref, ws_ref, bmap_ref, ya_ref):
        _, _, gu, _, _, _, _, vn = _gmlp_common(puv_ref[...], lnw_ref[...], lnb_ref[...], g_ref[...])
        tri = _tril_mask()
        vnb = vn.astype(BF16)
        mixed = bmap_ref[...]
        for h in range(N_HEADS):
            wh = jnp.where(tri, ws_ref[h], 0.0).astype(BF16)
            full = jnp.dot(wh, vnb, preferred_element_type=F32)
            mixed = mixed + jnp.where(_head_lane_mask(GM_WIDTH, h), full, 0.0)
        ya_ref[...] = (gu * mixed).astype(BF16)

    (ya,) = _rows_call("gmlp_fwd", body, CHUNK, [p_uv], [lnw, lnb, gavg, w_s, bmap], [_sds((n_tok, GM_WIDTH), BF16)])
    return ya


def _ssd_pre(xr, prev, cw_ref, cb, pdt, dtb, alog, emap):
    rowi = lax.broadcasted_iota(jnp.int32, (CHUNK, 1), 0)

    def down(s):
        return jnp.where(rowi < s, pltpu.roll(prev, s, 0), pltpu.roll(xr, s, 0))

    shifted = [down(3), down(2), down(1), xr]
    xc = cb
    for k in range(CONV_K):
        xc = xc + cw_ref[k] * shifted[k]
    sg = _sigmoid(xc)
    xa = xc * sg
    pre = pdt + dtb
    dt = jnp.maximum(pre, 0.0) + jnp.log(1.0 + jnp.exp(-jnp.abs(pre)))
    a_neg = -jnp.exp(alog)
    ltri = _tril_mask().astype(F32)
    a_cs = _dot_hi(ltri, dt * a_neg)
    acs_map = _dot_hi(a_cs, emap)
    dt_map = _dot_hi(dt, emap)
    return dict(shifted=shifted, xc=xc, sg=sg, xa=xa, pre=pre, dt=dt, a_neg=a_neg, ltri=ltri, a_cs=a_cs,
                acs_map=acs_map, dt_map=dt_map, rowi=rowi)


def _ssd_maps(p):
    last = p["rowi"] == CHUNK - 1
    aq_map = jnp.sum(jnp.where(last, p["acs_map"], 0.0), axis=0, keepdims=True)
    e_exp = jnp.exp(p["acs_map"])
    dte = jnp.exp(aq_map - p["acs_map"])
    cd = jnp.exp(aq_map)
    return last, e_exp, dte, cd


def _head_decay(a_cs, a_cs_t, head, tri):
    lane = lax.broadcasted_iota(jnp.int32, (1, DT_PAD), 1)
    sub = lax.broadcasted_iota(jnp.int32, (DT_PAD, 1), 0)
    col = jnp.sum(jnp.where(lane == head, a_cs, 0.0), axis=1, keepdims=True)
    row = jnp.sum(jnp.where(sub == head, a_cs_t, 0.0), axis=0, keepdims=True)
    return jnp.exp(jnp.where(tri, col - row, -1e30))


def _gate_fwd(y, z, nw):
    sz = _sigmoid(z)
    zg = z * sz
    yg = y * zg
    outs, rs = [], []
    for g in range(SSM_GROUPS):
        gs = slice(g * GROUP_W, (g + 1) * GROUP_W)
        o, r = _rms_fwd(yg[:, gs], nw[:, gs])
        outs.append(o)
        rs.append(r)
    return sz, zg, yg, outs, rs


def _ssd_const_specs():
    def whole(shape):
        nd = len(shape)
        return pl.BlockSpec(tuple(shape), lambda b, c: (0,) * nd)
    return [whole((CONV_K, 1, CONV_CH)), whole((1, CONV_CH)), whole((1, DT_PAD)), whole((1, DT_PAD)),
            whole((1, SSM_WIDTH)), whole((1, SSM_WIDTH)), whole((DT_PAD, SSM_WIDTH))]


def _ssd_fwd(p_xbc, p_z, p_dt, conv_w, conv_b, dt_bias, a_log, dskip_map, norm_w, emap, n_seq):
    n_tok = p_xbc.shape[0]
    nc = n_tok // n_seq // CHUNK

    def body(xr_ref, z_ref, pdt_ref, cw_ref, cb_ref, dtb_ref, alog_ref, dsk_ref, nw_ref, e_ref,
             yb_ref, yssd_ref, sprev_ref, prev_scr, s_scr):
        @pl.when(pl.program_id(1) == 0)
        def _():
            prev_scr[...] = jnp.zeros_like(prev_scr)
            s_scr[...] = jnp.zeros_like(s_scr)

        xr = xr_ref[...]
        p = _ssd_pre(xr, prev_scr[...], cw_ref, cb_ref[...], pdt_ref[...], dtb_ref[...], alog_ref[...], e_ref[...])
        _, e_exp, dte, cd = _ssd_maps(p)
        xs = p["xa"][:, :SSM_WIDTH]
        xd = xs * p["dt_map"]
        a_cs_t = p["a_cs"].T
        tri = _tril_mask()
        s_old = s_scr[...]
        sprev_ref[...] = s_old
        for g in range(SSM_GROUPS):
            gs = slice(g * GROUP_W, (g + 1) * GROUP_W)
            bm = p["xa"][:, SSM_WIDTH + g * SSM_STATE: SSM_WIDTH + (g + 1) * SSM_STATE].astype(BF16)
            cm = p["xa"][:, SSM_WIDTH + (SSM_GROUPS + g) * SSM_STATE: SSM_WIDTH + (SSM_GROUPS + g + 1) * SSM_STATE].astype(BF16)
            cb_mat = _dot_nt(cm, bm)
            xdg = xd[:, gs].astype(BF16)
            y_g = _dot(cm, s_old[:, gs]) * e_exp[:, gs] + dsk_ref[:, gs] * xs[:, gs]
            for r in range(SSM_GROUPS * 2):
                dm = _head_decay(p["a_cs"], a_cs_t, g * 4 + r, tri)
                full = jnp.dot((cb_mat * dm).astype(BF16), xdg, preferred_element_type=F32)
                y_g = y_g + jnp.where(_head_lane_mask(GROUP_W, r), full, 0.0)
            yssd_ref[:, gs] = y_g
            s_scr[:, gs] = cd[:, gs] * s_old[:, gs] + _dot_tn(bm, xd[:, gs] * dte[:, gs])
        _, _, _, outs, _ = _gate_fwd(yssd_ref[...], z_ref[...], nw_ref[...])
        for g in range(SSM_GROUPS):
            yb_ref[:, g * GROUP_W:(g + 1) * GROUP_W] = outs[g].astype(BF16)
        prev_scr[...] = xr

    def rows(width):
        return pl.BlockSpec((CHUNK, width), lambda b, c: (b * nc + c, 0))

    return pl.pallas_call(
        body, name="ssd_fwd", grid=(n_seq, nc),
        in_specs=[rows(CONV_CH), rows(SSM_WIDTH), rows(DT_PAD)] + _ssd_const_specs(),
        out_specs=[rows(SSM_WIDTH), rows(SSM_WIDTH), rows(SSM_WIDTH)],
        out_shape=(_sds((n_tok, SSM_WIDTH), BF16), _sds((n_tok, SSM_WIDTH), F32), _sds((n_tok, SSM_WIDTH), F32)),
        scratch_shapes=[pltpu.VMEM((CHUNK, CONV_CH), F32), pltpu.VMEM((SSM_STATE, SSM_WIDTH), F32)],
        compiler_params=_cparams(2),
    )(p_xbc, p_z, p_dt, conv_w, conv_b, dt_bias, a_log, dskip_map, norm_w, emap)


def _outproj_fwd(ya, yb, x, w_out, nw_post, nw_pre2, tm=256):
    n_tok = x.shape[0]

    def body(ya_ref, yb_ref, x_ref, wo_ref, nwa_ref, nwb_ref, o_ref, x1_ref, h2_ref):
        o = jnp.dot(ya_ref[...], wo_ref[:GM_WIDTH, :], preferred_element_type=F32)
        o = o + jnp.dot(yb_ref[...], wo_ref[GM_WIDTH:, :], preferred_element_type=F32)
        on, _ = _rms_fwd(o, nwa_ref[...])
        x1 = x_ref[...] + on
        h2, _ = _rms_fwd(x1, nwb_ref[...])
        o_ref[...] = o
        x1_ref[...] = x1
        h2_ref[...] = h2.astype(BF16)

    return _rows_call("outproj_fwd", body, tm, [ya, yb, x], [w_out, nw_post, nw_pre2],
                      [_sds((n_tok, D_MODEL), F32), _sds((n_tok, D_MODEL), F32), _sds((n_tok, D_MODEL), BF16)])


def _mlp_fwd(h2, x1, tgt, w_up, w_down, nw, tm=256):
    n_tok = x1.shape[0]

    def body(h2_ref, x1_ref, tgt_ref, wup_ref, wdn_ref, nw_ref, up_ref, f_ref, dd_ref, dy_ref, loss_ref, dnw_ref):
        h2v = h2_ref[...]
        acc = jnp.zeros((tm, D_MODEL), F32)
        for j in range(D_FF // FF_TILE):
            cs = slice(j * FF_TILE, (j + 1) * FF_TILE)
            u = jnp.dot(h2v, wup_ref[:, cs], preferred_element_type=F32)
            up_ref[:, cs] = u
            f = jnp.square(jnp.maximum(u, 0.0)).astype(BF16)
            f_ref[:, cs] = f
            acc = acc + jnp.dot(f, wdn_ref[cs, :], preferred_element_type=F32)
        dn, r = _rms_fwd(acc, nw_ref[...])
        e = x1_ref[...] + dn - tgt_ref[...]
        loss_ref[...] += jnp.full(loss_ref.shape, (0.5 / D_MODEL) * jnp.sum(e * e), F32)
        dy = e * (1.0 / D_MODEL)
        dd, dnw = _rms_bwd(acc, r, nw_ref[...], dy)
        dy_ref[...] = dy
        dd_ref[...] = dd.astype(BF16)
        dnw_ref[...] += dnw

    return _rows_call(
        "mlp_fwd", body, tm, [h2, x1, tgt], [w_up, w_down, nw],
        [_sds((n_tok, D_FF), F32), _sds((n_tok, D_FF), BF16), _sds((n_tok, D_MODEL), BF16), _sds((n_tok, D_MODEL), F32)],
        [_sds((8, 128), F32), _sds((1, D_MODEL), F32)])


def _mlp_bwd(dd, up, x1, dy, w_down, w_up, nw, tm=256):
    n_tok = x1.shape[0]

    def body(dd_ref, up_ref, x1_ref, dy_ref, wdn_ref, wup_ref, nw_ref, dup_ref, dx1_ref, dnw_ref):
        ddv = dd_ref[...]
        acc = jnp.zeros((tm, D_MODEL), F32)
        for j in range(D_FF // FF_TILE):
            cs = slice(j * FF_TILE, (j + 1) * FF_TILE)
            df = _dot_nt(ddv, wdn_ref[cs, :])
            du = (df * (2.0 * jnp.maximum(up_ref[:, cs], 0.0))).astype(BF16)
            dup_ref[:, cs] = du
            acc = acc + _dot_nt(du, wup_ref[:, cs])
        x1v = x1_ref[...]
        _, r = _rms_fwd(x1v, nw_ref[...])
        dx, dnw = _rms_bwd(x1v, r, nw_ref[...], acc)
        dx1_ref[...] = dy_ref[...] + dx
        dnw_ref[...] += dnw

    return _rows_call("mlp_bwd", body, tm, [dd, up, x1, dy], [w_down, w_up, nw],
                      [_sds((n_tok, D_FF), BF16), _sds((n_tok, D_MODEL), F32)], [_sds((1, D_MODEL), F32)])


def _outproj_bwd(dx1, o, w_out, nw, tm=256):
    n_tok = dx1.shape[0]

    def body(dx1_ref, o_ref, wo_ref, nw_ref, do_ref, dya_ref, dyb_ref, dnw_ref):
        ov = o_ref[...]
        _, r = _rms_fwd(ov, nw_ref[...])
        do, dnw = _rms_bwd(ov, r, nw_ref[...], dx1_ref[...])
        dob = do.astype(BF16)
        do_ref[...] = dob
        dya_ref[...] = _dot_nt(dob, wo_ref[:GM_WIDTH, :])
        dyb_ref[...] = _dot_nt(dob, wo_ref[GM_WIDTH:, :])
        dnw_ref[...] += dnw

    return _rows_call("outproj_bwd", body, tm, [dx1, o], [w_out, nw],
                      [_sds((n_tok, D_MODEL), BF16), _sds((n_tok, GM_WIDTH), F32), _sds((n_tok, SSM_WIDTH), F32)],
                      [_sds((1, D_MODEL), F32)])


def _gmlp_bwd(p_uv, dya, lnw, lnb, gavg, w_s, bmap, emap):
    n_tok = p_uv.shape[0]

    def body(puv_ref, dya_ref, lnw_ref, lnb_ref, g_ref, ws_ref, bmap_ref, e_ref,
             dpuv_ref, dws_ref, dbs_ref, dlnw_ref, dlnb_ref):
        gavg_v = g_ref[...]
        lnw_v = lnw_ref[...]
        u, v, gu, tu, tv, rstd, xhat, vn = _gmlp_common(puv_ref[...], lnw_v, lnb_ref[...], gavg_v)
        tri = _tril_mask()
        vnb = vn.astype(BF16)
        whs = [jnp.where(tri, ws_ref[h], 0.0).astype(BF16) for h in range(N_HEADS)]
        mixed = bmap_ref[...]
        for h in range(N_HEADS):
            full = jnp.dot(whs[h], vnb, preferred_element_type=F32)
            mixed = mixed + jnp.where(_head_lane_mask(GM_WIDTH, h), full, 0.0)
        dy = dya_ref[...]
        du = dy * mixed * _gelu_grad(u, tu)
        dmixed = dy * gu
        dbs_ref[...] += _dot_nt_hi(dmixed, e_ref[...])
        dvn = jnp.zeros((CHUNK, GM_WIDTH), F32)
        for h in range(N_HEADS):
            mask = _head_lane_mask(GM_WIDTH, h)
            dmh = jnp.where(mask, dmixed, 0.0).astype(BF16)
            dvn = dvn + jnp.where(mask, _dot_tn(whs[h], dmh), 0.0)
            dws_ref[h] += jnp.where(tri, _dot_nt(dmh, vnb), 0.0)
        dlnw_ref[...] += jnp.sum(dvn * xhat, axis=0, keepdims=True)
        dlnb_ref[...] += jnp.sum(dvn, axis=0, keepdims=True)
        dxh = dvn * lnw_v
        dgv = rstd * (dxh - _dot_hi(dxh, gavg_v) - xhat * _dot_hi(dxh * xhat, gavg_v))
        dv = dgv * _gelu_grad(v, tv)
        dpuv_ref[:, :GM_WIDTH] = du.astype(BF16)
        dpuv_ref[:, GM_WIDTH:] = dv.astype(BF16)

    return _rows_call(
        "gmlp_bwd", body, CHUNK, [p_uv, dya], [lnw, lnb, gavg, w_s, bmap, emap], [_sds((n_tok, 2 * GM_WIDTH), BF16)],
        [_sds((N_HEADS, CHUNK, CHUNK), F32), _sds((CHUNK, DT_PAD), F32), _sds((1, GM_WIDTH), F32),
         _sds((1, GM_WIDTH), F32)])


def _ssd_bwd(p_xbc, p_z, p_dt, yssd, sprev, dyb, conv_w, conv_b, dt_bias, a_log, dskip_map, norm_w, emap, n_seq):
    n_tok = p_xbc.shape[0]
    nc = n_tok // n_seq // CHUNK

    def body(xr_ref, xprev_ref, z_ref, pdt_ref, yssd_ref, sprev_ref, dyb_ref,
             cw_ref, cb_ref, dtb_ref, alog_ref, dsk_ref, nw_ref, e_ref,
             dpxbc_ref, dpz_ref, dpdt_ref, dcw_ref, dcb_ref, ddtb_ref, dalog_ref, ddsk_ref, dnw_ref,
             ds_scr, nxt_scr, dxa_scr):
        step = pl.program_id(1)
        first = jnp.logical_and(pl.program_id(0) == 0, step == 0)

        @pl.when(first)
        def _():
            for a in (dcw_ref, dcb_ref, ddtb_ref, dalog_ref, ddsk_ref, dnw_ref):
                a[...] = jnp.zeros_like(a)

        @pl.when(step == 0)
        def _():
            ds_scr[...] = jnp.zeros_like(ds_scr)
            nxt_scr[...] = jnp.zeros_like(nxt_scr)

        chunk = nc - 1 - step
        xr = xr_ref[...]
        prev = jnp.where(chunk == 0, 0.0, xprev_ref[...])
        emap_v = e_ref[...]
        p = _ssd_pre(xr, prev, cw_ref, cb_ref[...], pdt_ref[...], dtb_ref[...], alog_ref[...], emap_v)
        last, e_exp, dte, cd = _ssd_maps(p)
        rowi = p["rowi"]
        xs = p["xa"][:, :SSM_WIDTH]
        xd = xs * p["dt_map"]
        a_cs_t = p["a_cs"].T
        tri = _tril_mask()
        dsk = dsk_ref[...]
        nw_v = nw_ref[...]

        yv = yssd_ref[...]
        zv = z_ref[...]
        sz, zg, yg, _, rs = _gate_fwd(yv, zv, nw_v)
        dout = dyb_ref[...]
        for g in range(SSM_GROUPS):
            gs = slice(g * GROUP_W, (g + 1) * GROUP_W)
            dyg_g, dnw_g = _rms_bwd(yg[:, gs], rs[g], nw_v[:, gs], dout[:, gs])
            dnw_ref[:, gs] += dnw_g
            dxa_scr[:, gs] = dyg_g
        dyg = dxa_scr[:, :SSM_WIDTH]
        d_y = dyg * zg
        dpz_ref[...] = (dyg * yv * (sz + zv * sz * (1.0 - sz))).astype(BF16)

        s_prev = sprev_ref[...]
        ds_next = ds_scr[...]
        lane_dt = lax.broadcasted_iota(jnp.int32, (1, DT_PAD), 1)
        da_cols = jnp.zeros((CHUNK, DT_PAD), F32)
        for g in range(SSM_GROUPS):
            gs = slice(g * GROUP_W, (g + 1) * GROUP_W)
            b_off = SSM_WIDTH + g * SSM_STATE
            c_off = SSM_WIDTH + (SSM_GROUPS + g) * SSM_STATE
            bm = p["xa"][:, b_off:b_off + SSM_STATE].astype(BF16)
            cm = p["xa"][:, c_off:c_off + SSM_STATE].astype(BF16)
            cb_mat = _dot_nt(cm, bm)
            d_yg = d_y[:, gs]
            d_ygb = d_yg.astype(BF16)
            xdg = xd[:, gs]
            xdgb = xdg.astype(BF16)
            ds_g = ds_next[:, gs]
            sp_g = s_prev[:, gs]
            bds = _dot(bm, ds_g)
            dcs = d_yg * e_exp[:, gs]
            d_c = _dot_nt(dcs, sp_g)
            ds_scr[:, gs] = cd[:, gs] * ds_g + _dot_tn(cm, dcs)
            d_b = _dot_nt(xdg * dte[:, gs], ds_g)
            dxd_g = bds * dte[:, gs]
            sum_dcb = jnp.zeros((CHUNK, CHUNK), F32)
            for r in range(SSM_GROUPS * 2):
                head = g * 4 + r
                mask = _head_lane_mask(GROUP_W, r)
                dm = _head_decay(p["a_cs"], a_cs_t, head, tri)
                m_mat = cb_mat * dm
                g_mat = _dot_nt(jnp.where(mask, d_yg, 0.0), xdgb)
                w_mat = g_mat * m_mat
                sum_dcb = sum_dcb + g_mat * dm
                dxd_g = dxd_g + jnp.where(mask, _dot_tn(m_mat, d_ygb), 0.0)
                da_h = jnp.sum(w_mat - w_mat.T, axis=1, keepdims=True)
                da_cols = da_cols + jnp.where(lane_dt == head, da_h, 0.0)
            d_c = d_c + _dot(sum_dcb, bm)
            d_b = d_b + _dot_tn(sum_dcb, cm)
            dxa_scr[:, b_off:b_off + SSM_STATE] = d_b
            dxa_scr[:, c_off:c_off + SSM_STATE] = d_c
            y_off_g = _dot(cm, sp_g) * e_exp[:, gs]
            t3 = bds * xdg * dte[:, gs]
            tail = jnp.sum(t3, axis=0, keepdims=True) + jnp.sum(ds_g * sp_g, axis=0, keepdims=True) * cd[:, gs]
            pre_g = d_yg * y_off_g - t3 + jnp.where(last, tail, 0.0)
            da_cols = da_cols + _dot_nt_hi(pre_g, emap_v[:, gs])
            ddt_g = _dot_nt_hi(dxd_g * xs[:, gs], emap_v[:, gs])
            ddsk_ref[...] += jnp.sum(_dot_nt_hi(d_yg * xs[:, gs], emap_v[:, gs]), axis=0, keepdims=True)
            dxa_scr[:, gs] = dxd_g * p["dt_map"][:, gs] + dsk[:, gs] * d_yg
            if g == 0:
                ddt = ddt_g
            else:
                ddt = ddt + ddt_g
        ddta = _dot_tn_hi(p["ltri"], da_cols)
        ddt = ddt + ddta * p["a_neg"]
        dalog_ref[...] += jnp.sum(ddta * p["dt"], axis=0, keepdims=True) * p["a_neg"]
        draw = ddt * _sigmoid(p["pre"])
        ddtb_ref[...] += jnp.sum(draw, axis=0, keepdims=True)
        dpdt_ref[...] = draw.astype(BF16)

        xc = p["xc"]
        sg = p["sg"]
        dxc = dxa_scr[...] * (sg + xc * sg * (1.0 - sg))
        dcb_ref[...] += jnp.sum(dxc, axis=0, keepdims=True)
        for k in range(CONV_K):
            dcw_ref[k] += jnp.sum(dxc * p["shifted"][k], axis=0, keepdims=True)
        nxt = nxt_scr[...]

        def up(s):
            return jnp.where(rowi >= CHUNK - s, pltpu.roll(nxt, CHUNK - s, 0), pltpu.roll(dxc, CHUNK - s, 0))

        dxr = cw_ref[3] * dxc + cw_ref[2] * up(1) + cw_ref[1] * up(2) + cw_ref[0] * up(3)
        dpxbc_ref[...] = dxr.astype(BF16)
        nxt_scr[...] = dxc

    def rows(width):
        return pl.BlockSpec((CHUNK, width), lambda b, s: (b * nc + nc - 1 - s, 0))

    prev_rows = pl.BlockSpec((CHUNK, CONV_CH), lambda b, s: (b * nc + jnp.maximum(nc - 2 - s, 0), 0))

    def whole(shape):
        nd = len(shape)
        return pl.BlockSpec(tuple(shape), lambda b, s: (0,) * nd)

    acc_shapes = [(CONV_K, 1, CONV_CH), (1, CONV_CH), (1, DT_PAD), (1, DT_PAD), (1, DT_PAD), (1, SSM_WIDTH)]
    return pl.pallas_call(
        body, name="ssd_bwd", grid=(n_seq, nc),
        in_specs=[rows(CONV_CH), prev_rows, rows(SSM_WIDTH), rows(DT_PAD), rows(SSM_WIDTH), rows(SSM_WIDTH),
                  rows(SSM_WIDTH)] + _ssd_const_specs(),
        out_specs=[rows(CONV_CH), rows(SSM_WIDTH), rows(DT_PAD)] + [whole(s) for s in acc_shapes],
        out_shape=tuple([_sds((n_tok, CONV_CH), BF16), _sds((n_tok, SSM_WIDTH), BF16), _sds((n_tok, DT_PAD), BF16)]
                        + [_sds(s, F32) for s in acc_shapes]),
        scratch_shapes=[pltpu.VMEM((SSM_STATE, SSM_WIDTH), F32), pltpu.VMEM((CHUNK, CONV_CH), F32),
                        pltpu.VMEM((CHUNK, CONV_CH), F32)],
        compiler_params=_cparams(2),
    )(p_xbc, p_xbc, p_z, p_dt, yssd, sprev, dyb, conv_w, conv_b, dt_bias, a_log, dskip_map, norm_w, emap)


def _inproj_bwd(dp_uv, dp_xbc, dp_z, dp_dt, x, dx1, w_uv, w_xbc, w_z, w_dt, nw, tm=256):
    n_tok = x.shape[0]

    def body(duv_ref, dxbc_ref, dz_ref, ddt_ref, x_ref, dx1_ref, wuv_ref, wxbc_ref, wz_ref, wdt_ref, nw_ref,
             gx_ref, h_ref, dnw_ref):
        dh = _dot_nt(duv_ref[...], wuv_ref[...]) + _dot_nt(dxbc_ref[...], wxbc_ref[...])
        dh = dh + _dot_nt(dz_ref[...], wz_ref[...]) + _dot_nt(ddt_ref[...], wdt_ref[...])
        xv = x_ref[...]
        h, r = _rms_fwd(xv, nw_ref[...])
        dx, dnw = _rms_bwd(xv, r, nw_ref[...], dh)
        gx_ref[...] = dx1_ref[...] + dx
        h_ref[...] = h.astype(BF16)
        dnw_ref[...] += dnw

    return _rows_call("inproj_bwd", body, tm, [dp_uv, dp_xbc, dp_z, dp_dt, x, dx1], [w_uv, w_xbc, w_z, w_dt, nw],
                      [_sds((n_tok, D_MODEL), F32), _sds((n_tok, D_MODEL), BF16)], [_sds((1, D_MODEL), F32)])


def _const_maps():
    lane = jnp.arange(SSM_WIDTH) // HEAD_DIM
    emap = (jnp.arange(DT_PAD)[:, None] == lane[None, :]).astype(F32)
    gavg = (lane[:, None] == lane[None, :]).astype(F32) / HEAD_DIM
    return emap, gavg


def _pad_lanes(v, width):
    return jnp.pad(v, ((0, 0), (0, width - v.shape[1])))


def _local_grads(x, tgt, w_in_b, w_out_b, w_up_b, w_down_b, conv_w, small):
    n_seq, seq_len, _ = x.shape
    n_tok = n_seq * seq_len
    x2 = x.reshape(n_tok, D_MODEL)
    tgt2 = tgt.reshape(n_tok, D_MODEL)
    emap, gavg = _const_maps()

    w_uv = w_in_b[:, :2 * GM_WIDTH]
    w_z = w_in_b[:, 2 * GM_WIDTH:2 * GM_WIDTH + SSM_WIDTH]
    w_xbc = w_in_b[:, 2 * GM_WIDTH + SSM_WIDTH:2 * GM_WIDTH + SSM_WIDTH + CONV_CH]
    w_dt = _pad_lanes(w_in_b[:, 2 * GM_WIDTH + SSM_WIDTH + CONV_CH:], DT_PAD)

    nw_pre = small["norm_mix_pre"]
    lnw = small["gm_ln_w"].reshape(1, GM_WIDTH)
    lnb = small["gm_ln_b"].reshape(1, GM_WIDTH)
    w_s = small["gm_w_s"]
    bmap = jnp.repeat(small["gm_b_s"].T, HEAD_DIM, axis=1)
    cw3 = conv_w.reshape(CONV_K, 1, CONV_CH)
    conv_b = small["conv_b"]
    dt_bias = _pad_lanes(small["dt_bias"], DT_PAD)
    a_log = _pad_lanes(small["a_log"], DT_PAD)
    dskip_map = jnp.repeat(small["d_skip"], HEAD_DIM, axis=1)
    ssm_nw = small["ssm_norm_w"]

    p_uv, p_xbc, p_z, p_dt = _inproj_fwd(x2, nw_pre, w_uv, w_xbc, w_z, w_dt)
    ya = _gmlp_fwd(p_uv, lnw, lnb, gavg, w_s, bmap)
    ssd_consts = (cw3, conv_b, dt_bias, a_log, dskip_map, ssm_nw, emap)
    yb, yssd, sprev = _ssd_fwd(p_xbc, p_z, p_dt, *ssd_consts, n_seq)
    o, x1, h2 = _outproj_fwd(ya, yb, x2, w_out_b, small["norm_mix_post"], small["norm_ffn_pre"])
    up, f, dd, dy, loss_acc, d_nffn_post = _mlp_fwd(h2, x1, tgt2, w_up_b, w_down_b, small["norm_ffn_post"])

    dup, dx1, d_nffn_pre = _mlp_bwd(dd, up, x1, dy, w_down_b, w_up_b, small["norm_ffn_pre"])
    g_up = _matmul_tn("dw_up", h2, dup, D_MODEL, D_MODEL, 512, stacked=True)
    g_down = _matmul_tn("dw_down", f, dd, 1024, D_MODEL, 512)
    do, dya, dyb, d_nmix_post = _outproj_bwd(dx1, o, w_out_b, small["norm_mix_post"])
    g_out_a = _matmul_tn("dw_out_a", ya, do, GM_WIDTH, D_MODEL, 512)
    g_out_b = _matmul_tn("dw_out_b", yb, do, SSM_WIDTH, D_MODEL, 512)
    dp_uv, d_ws, d_bs_t, d_lnw, d_lnb = _gmlp_bwd(p_uv, dya, lnw, lnb, gavg, w_s, bmap, emap)
    (dp_xbc, dp_z, dp_dt, d_cw, d_cb, d_dtb, d_alog, d_dsk, d_ssm_nw) = _ssd_bwd(
        p_xbc, p_z, p_dt, yssd, sprev, dyb, *ssd_consts, n_seq)
    gx, h, d_nmix_pre = _inproj_bwd(dp_uv, dp_xbc, dp_z, dp_dt, x2, dx1, w_uv, w_xbc, w_z, w_dt, nw_pre)
    g_uv = _matmul_tn("dw_in_uv", h, dp_uv, D_MODEL, 2 * GM_WIDTH, 512)
    g_xbc = _matmul_tn("dw_in_xbc", h, dp_xbc, D_MODEL, CONV_CH, 512)
    g_z = _matmul_tn("dw_in_z", h, dp_z, D_MODEL, SSM_WIDTH, 512)
    g_dt = _matmul_tn("dw_in_dt", h, dp_dt, D_MODEL, DT_PAD, 512)

    g_in = jnp.concatenate([g_uv, g_z, g_xbc, g_dt[:, :N_HEADS]], axis=1)
    g_out = jnp.concatenate([g_out_a, g_out_b], axis=0)
    small_grads = {
        "norm_mix_pre": d_nmix_pre, "gm_ln_w": d_lnw.reshape(N_HEADS, HEAD_DIM), "gm_ln_b": d_lnb.reshape(N_HEADS, HEAD_DIM),
        "gm_w_s": d_ws, "gm_b_s": d_bs_t[:, :N_HEADS].T, "conv_w": d_cw.reshape(CONV_K, CONV_CH), "conv_b": d_cb,
        "dt_bias": d_dtb[:, :N_HEADS], "a_log": d_alog[:, :N_HEADS], "d_skip": d_dsk[:, :N_HEADS],
        "ssm_norm_w": d_ssm_nw, "norm_mix_post": d_nmix_post, "norm_ffn_pre": d_nffn_pre,
        "norm_ffn_post": d_nffn_post,
    }
    return loss_acc[0, 0], gx.reshape(x.shape), g_in, g_out, g_up, g_down, small_grads


_HBM = pl.BlockSpec(memory_space=pltpu.HBM)


D2D_CHUNKS = 8
ROW_ALIGN = 16


def _row_chunks(rows, n_chunks):
    size = min(max(rows // n_chunks, ROW_ALIGN), rows)
    assert rows % size == 0
    return [(start, size) for start in range(0, rows, size)]


def _position():
    x, y, c = lax.axis_index("x"), lax.axis_index("y"), lax.axis_index("c")
    chips = [(1 - x, y), (x, 1 - y), (1 - x, 1 - y)]
    return x, y, c, chips


def _allgather_chips(arrs):
    n = len(arrs)

    def body(*refs):
        ins, outs = refs[:n], refs[n:2 * n]
        send_sems, recv_sems, local_sems = refs[2 * n:]
        x, y, c, chips = _position()
        me = 2 * x + y
        sibling = (x, y, 1 - c)

        def copy(a, k, src, dst, to):
            return pltpu.make_async_remote_copy(src_ref=src, dst_ref=dst, send_sem=send_sems.at[a, k],
                                                recv_sem=recv_sems.at[a, k], device_id=to, device_id_type=MESH)

        def half_rows(a, pc):
            half = ins[a].shape[0] // 2
            return pl.ds(pc * half, half)

        started = []
        for a in range(n):
            local = pltpu.make_async_copy(ins[a], outs[a].at[me], local_sems.at[a])
            local.start()
            started.append(local)
        sends = []
        for a in range(n):
            mine = half_rows(a, c)
            for k, (px, py) in enumerate(chips):
                cp = copy(a, k, ins[a].at[mine], outs[a].at[me, mine], (px, py, c))
                cp.start()
                sends.append(cp)
        for a in range(n):
            half = ins[a].shape[0] // 2
            for k, (px, py) in enumerate(chips):
                blk = outs[a].at[2 * px + py, half_rows(a, c)]
                copy(a, k, blk, blk, (px, py, c)).wait_recv()
                for start, size in _row_chunks(half, D2D_CHUNKS):
                    piece = outs[a].at[2 * px + py, pl.ds(c * half + start, size)]
                    copy(a, 3 + k, piece, piece, sibling).start()
                sends.append(copy(a, 3 + k, blk, blk, sibling))
        for a in range(n):
            for k, (px, py) in enumerate(chips):
                blk = outs[a].at[2 * px + py, half_rows(a, 1 - c)]
                copy(a, 3 + k, blk, blk, sibling).wait_recv()
        for cp in sends:
            cp.wait_send()
        for local in started:
            local.wait()

    return pl.pallas_call(
        body, name="allgather_weights",
        out_shape=tuple(_sds((N_CHIPS,) + a.shape, a.dtype) for a in arrs),
        in_specs=[_HBM] * n, out_specs=tuple([_HBM] * n),
        scratch_shapes=[pltpu.SemaphoreType.DMA((n, 6)), pltpu.SemaphoreType.DMA((n, 6)), pltpu.SemaphoreType.DMA((n,))],
    )(*arrs)


def _pair_exchange(grads):
    n = len(grads)

    def body(*refs):
        ins, got = refs[:n], refs[n:2 * n]
        send_sems, recv_sems = refs[2 * n:]
        x, y, c, _ = _position()
        sibling = (x, y, 1 - c)

        def copy(a, src, dst):
            return pltpu.make_async_remote_copy(src_ref=src, dst_ref=dst, send_sem=send_sems.at[a],
                                                recv_sem=recv_sems.at[a], device_id=sibling, device_id_type=MESH)

        for a in range(n):
            half = ins[a].shape[1] // 2
            for slab in range(N_CHIPS):
                for start, size in _row_chunks(half, D2D_CHUNKS):
                    copy(a, ins[a].at[slab, pl.ds((1 - c) * half + start, size), :],
                         got[a].at[slab, pl.ds(start, size), :]).start()
        for a in range(n):
            half = ins[a].shape[1] // 2
            copy(a, ins[a].at[:, pl.ds((1 - c) * half, half), :], got[a]).wait()

    halves = tuple(_sds((N_CHIPS, g.shape[1] // 2, g.shape[2]), g.dtype) for g in grads)
    return pl.pallas_call(
        body, name="grad_pair_exchange", out_shape=halves, in_specs=[_HBM] * n, out_specs=tuple([_HBM] * n),
        scratch_shapes=[pltpu.SemaphoreType.DMA((n,)), pltpu.SemaphoreType.DMA((n,))],
    )(*grads)


def _chip_exchange(hsums):
    n = len(hsums)

    def body(*refs):
        ins, outs = refs[:n], refs[n:2 * n]
        send_sems, recv_sems, local_sems = refs[2 * n:]
        x, y, c, chips = _position()
        me = 2 * x + y
        cps = []
        for a in range(n):
            local = pltpu.make_async_copy(ins[a].at[me], outs[a].at[me], local_sems.at[a])
            local.start()
            cps.append(local)
            for k, (px, py) in enumerate(chips):
                cp = pltpu.make_async_remote_copy(
                    src_ref=ins[a].at[2 * px + py], dst_ref=outs[a].at[me], send_sem=send_sems.at[a, k],
                    recv_sem=recv_sems.at[a, k], device_id=(px, py, c), device_id_type=MESH)
                cp.start()
                cps.append(cp)
        for cp in cps:
            cp.wait()

    return pl.pallas_call(
        body, name="grad_chip_exchange", out_shape=tuple(_sds(h.shape, h.dtype) for h in hsums),
        in_specs=[_HBM] * n, out_specs=tuple([_HBM] * n),
        scratch_shapes=[pltpu.SemaphoreType.DMA((n, 3)), pltpu.SemaphoreType.DMA((n, 3)), pltpu.SemaphoreType.DMA((n,))],
    )(*hsums)


def _pair_gather(bufs):
    n = len(bufs)

    def body(*refs):
        outs = refs[n:2 * n]
        send_sems, recv_sems = refs[2 * n:]
        x, y, c, _ = _position()
        sibling = (x, y, 1 - c)

        def copy(a, rows):
            return pltpu.make_async_remote_copy(src_ref=rows, dst_ref=rows, send_sem=send_sems.at[a],
                                                recv_sem=recv_sems.at[a], device_id=sibling, device_id_type=MESH)

        for a in range(n):
            half = outs[a].shape[0] // 2
            for start, size in _row_chunks(half, 2 * D2D_CHUNKS):
                copy(a, outs[a].at[pl.ds(c * half + start, size), :]).start()
        for a in range(n):
            half = outs[a].shape[0] // 2
            copy(a, outs[a].at[pl.ds(c * half, half), :]).wait_send()
            copy(a, outs[a].at[pl.ds((1 - c) * half, half), :]).wait_recv()

    return pl.pallas_call(
        body, name="grad_pair_gather", out_shape=tuple(_sds(b.shape, b.dtype) for b in bufs),
        in_specs=[_HBM] * n, out_specs=tuple([_HBM] * n), input_output_aliases={a: a for a in range(n)},
        scratch_shapes=[pltpu.SemaphoreType.DMA((n,)), pltpu.SemaphoreType.DMA((n,))],
    )(*bufs)


def _small_allreduce(packed):
    m_per, n_cols = packed.shape

    def body(x_ref, sum_ref, all_ref, send_sems, recv_sems, local_sem):
        x, y, c, chips = _position()
        me, sibling = (x, y, c), (x, y, 1 - c)

        def rows(px, py, pc):
            return all_ref.at[pl.ds((4 * px + 2 * py + pc) * m_per, m_per), :]

        def copy(k, block, to, src=None):
            return pltpu.make_async_remote_copy(
                src_ref=rows(*block) if src is None else src, dst_ref=rows(*block), send_sem=send_sems.at[k],
                recv_sem=recv_sems.at[k], device_id=to, device_id_type=MESH)

        mine = pltpu.make_async_copy(x_ref, rows(*me), local_sem)
        mine.start()
        first = [copy(0, me, sibling, src=x_ref)]
        first += [copy(1 + j, me, (*chip, c), src=x_ref) for j, chip in enumerate(chips)]
        for cp in first:
            cp.start()
        passed = [copy(4 + j, (*chip, c), sibling) for j, chip in enumerate(chips)]
        for j, chip in enumerate(chips):
            copy(1 + j, (*chip, c), me).wait_recv()
            passed[j].start()
        copy(0, sibling, me).wait_recv()
        for j, chip in enumerate(chips):
            copy(4 + j, (*chip, 1 - c), me).wait_recv()
        for cp in first + passed:
            cp.wait_send()
        mine.wait()
        acc = all_ref[0:m_per, :]
        for d in range(1, N_DEV):
            acc = acc + all_ref[d * m_per:(d + 1) * m_per, :]
        sum_ref[...] = acc

    vmem = pl.BlockSpec(memory_space=pltpu.VMEM)
    total, _ = pl.pallas_call(
        body, name="small_allreduce",
        out_shape=(_sds((m_per, n_cols), F32), _sds((N_DEV * m_per, n_cols), F32)),
        in_specs=[vmem], out_specs=(vmem, vmem),
        scratch_shapes=[pltpu.SemaphoreType.DMA((7,)), pltpu.SemaphoreType.DMA((7,)), pltpu.SemaphoreType.DMA],
    )(packed)
    return total


def _pair_sum(core, own, got, tm):
    _, half, cols = got.shape
    nb = half // tm

    def body(c_ref, a_ref, b_ref, o_ref):
        o_ref[...] = (a_ref[...] + b_ref[...]).astype(BF16)

    return pl.pallas_call(
        body, name="grad_pair_sum", out_shape=_sds(got.shape, BF16),
        grid_spec=pltpu.PrefetchScalarGridSpec(
            num_scalar_prefetch=1, grid=(N_CHIPS, nb),
            in_specs=[pl.BlockSpec((None, tm, cols), lambda s, i, c_ref: (s, c_ref[0] * nb + i, 0)),
                      pl.BlockSpec((None, tm, cols), lambda s, i, c_ref: (s, i, 0))],
            out_specs=pl.BlockSpec((None, tm, cols), lambda s, i, c_ref: (s, i, 0))),
        compiler_params=_cparams(2),
    )(core, own, got)


def _chip_sum(core, slabs, tm):
    _, half, cols = slabs.shape
    nb = half // tm

    def body(c_ref, s_ref, o_ref):
        acc = s_ref[0].astype(F32)
        for k in range(1, N_CHIPS):
            acc = acc + s_ref[k].astype(F32)
        o_ref[...] = acc

    return pl.pallas_call(
        body, name="grad_chip_sum", out_shape=_sds((2 * half, cols), F32),
        grid_spec=pltpu.PrefetchScalarGridSpec(
            num_scalar_prefetch=1, grid=(nb,),
            in_specs=[pl.BlockSpec((N_CHIPS, tm, cols), lambda i, c_ref: (0, i, 0))],
            out_specs=pl.BlockSpec((tm, cols), lambda i, c_ref: (c_ref[0] * nb + i, 0))),
        compiler_params=_cparams(1),
    )(core, slabs)


def _adam_math(w, g, m, v):
    m2 = ADAM_B1 * m + (1.0 - ADAM_B1) * g
    v2 = ADAM_B2 * v + (1.0 - ADAM_B2) * (g * g)
    m_hat = m2 / (1.0 - ADAM_B1 ** ADAM_STEP)
    v_hat = v2 / (1.0 - ADAM_B2 ** ADAM_STEP)
    delta = -ADAM_LR * (m_hat / (jnp.sqrt(v_hat) + ADAM_EPS) + ADAM_WD * w)
    return delta, m2, v2


def _adamw(name, w, g, m, v, tm):
    def body(w_ref, g_ref, m_ref, v_ref, d_ref, m2_ref, v2_ref):
        d, m2, v2 = _adam_math(w_ref[...], g_ref[...], m_ref[...], v_ref[...])
        d_ref[...] = d
        m2_ref[...] = m2
        v2_ref[...] = v2

    return _rows_call(name, body, tm, [w, g, m, v], [], [_sds(w.shape, F32)] * 3)


_SMALL_NAMES = ("norm_mix_pre", "gm_ln_w", "gm_ln_b", "gm_w_s", "gm_b_s", "conv_w", "conv_b", "dt_bias", "a_log",
                "d_skip", "ssm_norm_w", "norm_mix_post", "norm_ffn_pre", "norm_ffn_post")
_PACK_COLS = 1024


def _pack(parts):
    flat = jnp.concatenate([parts[n].reshape(-1) for n in _SMALL_NAMES])
    rows = -(-flat.shape[0] // (8 * _PACK_COLS)) * 8
    flat = jnp.pad(flat, (0, rows * _PACK_COLS - flat.shape[0]))
    return flat.reshape(rows, _PACK_COLS)


def _unpack(packed, shapes):
    flat = packed.reshape(-1)
    out, off = {}, 0
    for n in _SMALL_NAMES:
        size = 1
        for s in shapes[n]:
            size *= s
        out[n] = flat[off:off + size].reshape(shapes[n])
        off += size
    return out


def kernel(x, norm_mix_pre, w_in, gm_ln_w, gm_ln_b, gm_w_s, gm_b_s, conv_w, conv_b, dt_bias, a_log, d_skip, ssm_norm_w, w_out, norm_mix_post, norm_ffn_pre, w_up, w_down, norm_ffn_post, loss_target, m_norm_mix_pre, m_w_in, m_gm_ln_w, m_gm_ln_b, m_gm_w_s, m_gm_b_s, m_conv_w, m_conv_b, m_dt_bias, m_a_log, m_d_skip, m_ssm_norm_w, m_w_out, m_norm_mix_post, m_norm_ffn_pre, m_w_up, m_w_down, m_norm_ffn_post, v_norm_mix_pre, v_w_in, v_gm_ln_w, v_gm_ln_b, v_gm_w_s, v_gm_b_s, v_conv_w, v_conv_b, v_dt_bias, v_a_log, v_d_skip, v_ssm_norm_w, v_w_out, v_norm_mix_post, v_norm_ffn_pre, v_w_up, v_w_down, v_norm_ffn_post):
    params = dict(norm_mix_pre=norm_mix_pre, w_in=w_in, gm_ln_w=gm_ln_w, gm_ln_b=gm_ln_b, gm_w_s=gm_w_s, gm_b_s=gm_b_s,
                  conv_w=conv_w, conv_b=conv_b, dt_bias=dt_bias, a_log=a_log, d_skip=d_skip, ssm_norm_w=ssm_norm_w,
                  w_out=w_out, norm_mix_post=norm_mix_post, norm_ffn_pre=norm_ffn_pre, w_up=w_up, w_down=w_down,
                  norm_ffn_post=norm_ffn_post)
    mom1 = dict(norm_mix_pre=m_norm_mix_pre, w_in=m_w_in, gm_ln_w=m_gm_ln_w, gm_ln_b=m_gm_ln_b, gm_w_s=m_gm_w_s,
                gm_b_s=m_gm_b_s, conv_w=m_conv_w, conv_b=m_conv_b, dt_bias=m_dt_bias, a_log=m_a_log, d_skip=m_d_skip,
                ssm_norm_w=m_ssm_norm_w, w_out=m_w_out, norm_mix_post=m_norm_mix_post, norm_ffn_pre=m_norm_ffn_pre,
                w_up=m_w_up, w_down=m_w_down, norm_ffn_post=m_norm_ffn_post)
    mom2 = dict(norm_mix_pre=v_norm_mix_pre, w_in=v_w_in, gm_ln_w=v_gm_ln_w, gm_ln_b=v_gm_ln_b, gm_w_s=v_gm_w_s,
                gm_b_s=v_gm_b_s, conv_w=v_conv_w, conv_b=v_conv_b, dt_bias=v_dt_bias, a_log=v_a_log, d_skip=v_d_skip,
                ssm_norm_w=v_ssm_norm_w, w_out=v_w_out, norm_mix_post=v_norm_mix_post, norm_ffn_pre=v_norm_ffn_pre,
                w_up=v_w_up, w_down=v_w_down, norm_ffn_post=v_norm_ffn_post)
    names = list(params)
    big = ("w_in", "w_out", "w_up", "w_down")
    chip = 2 * lax.axis_index("x") + lax.axis_index("y")

    shards = [params[n][0].astype(BF16) for n in big]
    conv_shard = jnp.pad(conv_w[0], ((0, 16 - CONV_K), (0, 0)))
    g_in4, g_out4, g_up4, g_down4, g_conv4 = _allgather_chips(shards + [conv_shard])
    w_in_b = jnp.transpose(g_in4, (1, 0, 2)).reshape(D_MODEL, IN_COLS)
    w_out_b = g_out4.reshape(D_MODEL, D_MODEL)
    w_up_b = jnp.transpose(g_up4, (1, 0, 2)).reshape(D_MODEL, D_FF)
    w_down_b = g_down4.reshape(D_FF, D_MODEL)
    conv_full = jnp.transpose(g_conv4[:, :CONV_K, :], (1, 0, 2)).reshape(CONV_K, CONV_CH)

    small = {n: params[n][0] if params[n].ndim >= 3 else params[n] for n in _SMALL_NAMES if n != "conv_w"}
    loss_part, grad_x, g_in, g_out, g_up, g_down, small_grads = _local_grads(
        x, loss_target, w_in_b, w_out_b, w_up_b, w_down_b, conv_full, small)
    loss = lax.psum(loss_part, ("x", "y", "c"))

    stacked = [jnp.transpose(g_in.reshape(D_MODEL, N_CHIPS, IN_COLS // N_CHIPS), (1, 0, 2)),
               g_out.reshape(N_CHIPS, D_MODEL // N_CHIPS, D_MODEL), g_up,
               g_down.reshape(N_CHIPS, D_FF // N_CHIPS, D_MODEL)]
    core = lax.axis_index("c").astype(jnp.int32).reshape(1)
    got = _pair_exchange(stacked)
    hsums = [_pair_sum(core, s, g, tm) for s, g, tm in zip(stacked, got, (256, 128, 256, 256))]
    slabs = _chip_exchange(hsums)
    reds = [_chip_sum(core, s, tm) for s, tm in zip(slabs, (256, 128, 256, 256))]
    big_grads = dict(zip(big, _pair_gather(reds)))

    small_shapes = {n: (small_grads[n].shape) for n in _SMALL_NAMES}
    small_sum = _unpack(_small_allreduce(_pack(small_grads)), small_shapes)
    small_sum["conv_w"] = lax.dynamic_slice_in_dim(small_sum["conv_w"], chip * (CONV_CH // N_CHIPS), CONV_CH // N_CHIPS, axis=1)

    grads, delta, new_m, new_v = {}, {}, {}, {}
    for n, tm in zip(big, (256, 128, 256, 256)):
        g = big_grads[n]
        d, m2, v2 = _adamw("adamw_" + n, params[n][0], g, mom1[n][0], mom2[n][0], tm)
        grads[n], delta[n], new_m[n], new_v[n] = g[None], d[None], m2[None], v2[None]
    local_shapes = {n: params[n].shape[1:] if params[n].ndim >= 3 else params[n].shape for n in _SMALL_NAMES}
    flat = lambda tree: {n: tree[n].reshape(local_shapes[n]) for n in _SMALL_NAMES}
    packed = [_pack(flat(t)) for t in (params, small_sum, mom1, mom2)]
    d_p, m_p, v_p = _adamw("adamw_small", *packed, packed[0].shape[0])
    for src, dst in ((d_p, delta), (m_p, new_m), (v_p, new_v)):
        for n, val in _unpack(src, local_shapes).items():
            dst[n] = val.reshape(params[n].shape)
    for n in _SMALL_NAMES:
        grads[n] = small_sum[n].reshape(params[n].shape)

    out = [loss, grad_x]
    for tree in (grads, delta, new_m, new_v):
        out += [tree[n] for n in names]
    return tuple(out)
```

```python
import functools

import jax
import jax.numpy as jnp
from jax import lax
from jax.experimental import pallas as pl
from jax.experimental.pallas import tpu as pltpu

F32 = jnp.float32
BF16 = jnp.bfloat16
HI = lax.Precision.HIGHEST
MESH = pl.DeviceIdType.MESH

EPS = 1e-6
D_MODEL = 1024
GM_WIDTH = 512
SSM_WIDTH = 512
N_HEADS = 8
HEAD_DIM = 64
CHUNK = 128
SSM_GROUPS = 2
GROUP_W = SSM_WIDTH // SSM_GROUPS
SSM_STATE = 128
CONV_K = 4
CONV_CH = 1024
D_FF = 4096
IN_COLS = 2568
DT_PAD = 128
N_CHIPS = 4
N_DEV = 8

ADAM_LR = 0.001
ADAM_B1 = 0.9
ADAM_B2 = 0.999
ADAM_EPS = 1e-08
ADAM_WD = 0.01
ADAM_STEP = 10

VMEM_LIMIT_BYTES = 56 * 1024 * 1024
FF_TILE = 512


def _cparams(n_axes):
    return pltpu.CompilerParams(dimension_semantics=("arbitrary",) * n_axes, vmem_limit_bytes=VMEM_LIMIT_BYTES)


def _dot(a, b):
    return jnp.dot(a.astype(BF16), b.astype(BF16), preferred_element_type=F32)


def _dot_nt(a, b):
    return lax.dot_general(a.astype(BF16), b.astype(BF16), (((1,), (1,)), ((), ())), preferred_element_type=F32)


def _dot_tn(a, b):
    return lax.dot_general(a.astype(BF16), b.astype(BF16), (((0,), (0,)), ((), ())), preferred_element_type=F32)


def _sigmoid(x):
    return 1.0 / (1.0 + jnp.exp(-x))


_GELU_C = 0.7978845608028654
_GELU_A = 0.044715


def _gelu(x):
    t = jnp.tanh(_GELU_C * (x + _GELU_A * (x * x * x)))
    return 0.5 * x * (1.0 + t), t


def _gelu_grad(x, t):
    return 0.5 * (1.0 + t) + 0.5 * x * (1.0 - t * t) * (_GELU_C * (1.0 + 3.0 * _GELU_A * x * x))


def _rms_fwd(x, w):
    r = lax.rsqrt(jnp.mean(x * x, axis=-1, keepdims=True) + EPS)
    return x * r * w, r


def _rms_bwd(x, r, w, dy):
    g = dy * w
    dx = r * g - x * (r * r * r) * jnp.mean(g * x, axis=-1, keepdims=True)
    dw = jnp.sum(dy * x * r, axis=0, keepdims=True)
    return dx, dw


def _rows_call(name, body, tm, row_ins, const_ins, row_outs, acc_outs=(), scratch=()):
    n_rows = row_ins[0].shape[0]
    assert n_rows % tm == 0
    n_in = len(row_ins) + len(const_ins)
    n_ro = len(row_outs)
    n_acc = len(acc_outs)

    def kern(*refs):
        accs = refs[n_in + n_ro:n_in + n_ro + n_acc]

        @pl.when(pl.program_id(0) == 0)
        def _():
            for a in accs:
                a[...] = jnp.zeros_like(a)

        body(*refs)

    def whole(shape):
        nd = len(shape)
        return pl.BlockSpec(tuple(shape), lambda i: (0,) * nd)

    in_specs = [pl.BlockSpec((tm, a.shape[1]), lambda i: (i, 0)) for a in row_ins]
    in_specs += [whole(a.shape) for a in const_ins]
    out_specs = [pl.BlockSpec((tm, s.shape[1]), lambda i: (i, 0)) for s in row_outs]
    out_specs += [whole(s.shape) for s in acc_outs]
    return pl.pallas_call(
        kern, name=name, grid=(n_rows // tm,), in_specs=in_specs, out_specs=out_specs,
        out_shape=tuple(row_outs) + tuple(acc_outs), scratch_shapes=list(scratch), compiler_params=_cparams(1),
    )(*row_ins, *const_ins)


def _sds(shape, dtype):
    return jax.ShapeDtypeStruct(tuple(shape), dtype)


def _matmul_tn(name, a, b, tm, tn, tk, stacked=False):
    k_dim, m_dim = a.shape
    n_dim = b.shape[1]
    assert m_dim % tm == 0 and n_dim % tn == 0 and k_dim % tk == 0

    def kern(a_ref, b_ref, o_ref):
        @pl.when(pl.program_id(2) == 0)
        def _():
            o_ref[...] = jnp.zeros_like(o_ref)

        o_ref[...] += _dot_tn(a_ref[...], b_ref[...])

    if stacked:
        assert tm == m_dim
        out_shape = _sds((n_dim // tn, m_dim, tn), F32)
        out_spec = pl.BlockSpec((None, tm, tn), lambda i, j, k: (j, i, 0))
    else:
        out_shape = _sds((m_dim, n_dim), F32)
        out_spec = pl.BlockSpec((tm, tn), lambda i, j, k: (i, j))
    return pl.pallas_call(
        kern, name=name, grid=(m_dim // tm, n_dim // tn, k_dim // tk),
        in_specs=[pl.BlockSpec((tk, tm), lambda i, j, k: (k, i)), pl.BlockSpec((tk, tn), lambda i, j, k: (k, j))],
        out_specs=out_spec, out_shape=out_shape, compiler_params=_cparams(3),
    )(a, b)


def _inproj_fwd(x, nw, w_uv, w_xbc, w_z, w_dt, tm=256):
    n_tok = x.shape[0]

    def body(x_ref, nw_ref, wuv_ref, wxbc_ref, wz_ref, wdt_ref, puv_ref, pxbc_ref, pz_ref, pdt_ref):
        h, _ = _rms_fwd(x_ref[...], nw_ref[...])
        h = h.astype(BF16)
        puv_ref[...] = jnp.dot(h, wuv_ref[...], preferred_element_type=F32)
        pxbc_ref[...] = jnp.dot(h, wxbc_ref[...], preferred_element_type=F32)
        pz_ref[...] = jnp.dot(h, wz_ref[...], preferred_element_type=F32)
        pdt_ref[...] = jnp.dot(h, wdt_ref[...], preferred_element_type=F32)

    return _rows_call(
        "inproj_fwd", body, tm, [x], [nw, w_uv, w_xbc, w_z, w_dt],
        [_sds((n_tok, 2 * GM_WIDTH), F32), _sds((n_tok, CONV_CH), F32), _sds((n_tok, SSM_WIDTH), F32),
         _sds((n_tok, DT_PAD), F32)])


def _head_lane_mask(width, head):
    lane = lax.broadcasted_iota(jnp.int32, (1, width), 1)
    return (lane // HEAD_DIM) == head


def _split_terms(x, terms):
    parts = []
    for _ in range(terms):
        p = x.astype(BF16)
        parts.append(p)
        x = x - p.astype(F32)
    return parts


def _seg_dots(vals, ind, terms=2):
    m = vals[0].shape[0]
    parts = []
    for v in vals:
        parts += _split_terms(v, terms)
    red = jnp.dot(jnp.concatenate(parts, axis=0), ind, preferred_element_type=F32)
    outs = []
    for i in range(len(vals)):
        acc = red[i * terms * m:(i * terms + 1) * m]
        for t in range(1, terms):
            acc = acc + red[(i * terms + t) * m:(i * terms + t + 1) * m]
        outs.append(acc)
    return outs


def _tri_dot(mask, x, terms=3):
    n = x.shape[1]
    red = jnp.dot(mask.astype(BF16), jnp.concatenate(_split_terms(x, terms), axis=1), preferred_element_type=F32)
    acc = red[:, :n]
    for t in range(1, terms):
        acc = acc + red[:, t * n:(t + 1) * n]
    return acc


def _gmlp_common(puv, lnw, lnb, e_bf, et_bf):
    u = puv[:, :GM_WIDTH]
    v = puv[:, GM_WIDTH:]
    gu, tu = _gelu(u)
    gv, tv = _gelu(v)
    (s1,) = _seg_dots([gv], et_bf)
    (mu,) = _seg_dots([s1 * (1.0 / HEAD_DIM)], e_bf)
    xc = gv - mu
    (s2,) = _seg_dots([xc * xc], et_bf)
    (rstd,) = _seg_dots([lax.rsqrt(s2 * (1.0 / HEAD_DIM) + EPS)], e_bf)
    xhat = xc * rstd
    vn = xhat * lnw + lnb
    return u, v, gu, tu, tv, rstd, xhat, vn


def _tril_mask():
    r = lax.broadcasted_iota(jnp.int32, (CHUNK, CHUNK), 0)
    c = lax.broadcasted_iota(jnp.int32, (CHUNK, CHUNK), 1)
    return r >= c


def _head_blocks(v):
    return jnp.concatenate([jnp.where(_head_lane_mask(GM_WIDTH, h), v, jnp.zeros_like(v)) for h in range(N_HEADS)], axis=0)


def _gmlp_fwd(p_uv, lnw, lnb, e_bf, et_bf, w_cat, bmap):
    n_tok = p_uv.shape[0]

    def body(puv_ref, lnw_ref, lnb_ref, e_ref, et_ref, wcat_ref, bmap_ref, ya_ref, wm_scr):
        @pl.when(pl.program_id(0) == 0)
        def _():
            t = lax.broadcasted_iota(jnp.int32, (CHUNK, N_HEADS * CHUNK), 0)
            s = lax.broadcasted_iota(jnp.int32, (CHUNK, N_HEADS * CHUNK), 1) % CHUNK
            wm_scr[...] = jnp.where(t >= s, wcat_ref[...], 0.0).astype(BF16)

        _, _, gu, _, _, _, _, vn = _gmlp_common(puv_ref[...], lnw_ref[...], lnb_ref[...], e_ref[...], et_ref[...])
        mixed = jnp.dot(wm_scr[...], _head_blocks(vn.astype(BF16)), preferred_element_type=F32) + bmap_ref[...]
        ya_ref[...] = (gu * mixed).astype(BF16)

    (ya,) = _rows_call("gmlp_fwd", body, CHUNK, [p_uv], [lnw, lnb, e_bf, et_bf, w_cat, bmap],
                       [_sds((n_tok, GM_WIDTH), BF16)], scratch=[pltpu.VMEM((CHUNK, N_HEADS * CHUNK), BF16)])
    return ya


def _ssd_pre(xr, prev, cw_ref, cb, pdt, dtb, alog, emap):
    rowi = lax.broadcasted_iota(jnp.int32, (CHUNK, 1), 0)

    def down(s):
        return jnp.where(rowi < s, pltpu.roll(prev, s, 0), pltpu.roll(xr, s, 0))

    shifted = [down(3), down(2), down(1), xr]
    xc = cb
    for k in range(CONV_K):
        xc = xc + cw_ref[k] * shifted[k]
    sg = _sigmoid(xc)
    xa = xc * sg
    pre = pdt + dtb
    dt = jnp.maximum(pre, 0.0) + jnp.log(1.0 + jnp.exp(-jnp.abs(pre)))
    a_neg = -jnp.exp(alog)
    a_cs = _tri_dot(_tril_mask(), dt * a_neg)
    acs_map, dt_map = _seg_dots([a_cs, dt], emap, terms=3)
    return dict(shifted=shifted, xc=xc, sg=sg, xa=xa, pre=pre, dt=dt, a_neg=a_neg, a_cs=a_cs,
                acs_map=acs_map, dt_map=dt_map, rowi=rowi)


def _ssd_maps(p):
    last = p["rowi"] == CHUNK - 1
    aq_map = jnp.sum(jnp.where(last, p["acs_map"], 0.0), axis=0, keepdims=True)
    e_exp = jnp.exp(p["acs_map"])
    dte = jnp.exp(aq_map - p["acs_map"])
    cd = jnp.exp(aq_map)
    return last, e_exp, dte, cd


def _head_decay(a_cs, a_cs_t, head, tri):
    lane = lax.broadcasted_iota(jnp.int32, (1, DT_PAD), 1)
    sub = lax.broadcasted_iota(jnp.int32, (DT_PAD, 1), 0)
    col = jnp.sum(jnp.where(lane == head, a_cs, 0.0), axis=1, keepdims=True)
    row = jnp.sum(jnp.where(sub == head, a_cs_t, 0.0), axis=0, keepdims=True)
    return jnp.exp(jnp.where(tri, col - row, -1e30))


def _gate_fwd(y, z, nw):
    sz = _sigmoid(z)
    zg = z * sz
    yg = y * zg
    outs, rs = [], []
    for g in range(SSM_GROUPS):
        gs = slice(g * GROUP_W, (g + 1) * GROUP_W)
        o, r = _rms_fwd(yg[:, gs], nw[:, gs])
        outs.append(o)
        rs.append(r)
    return sz, zg, yg, outs, rs


def _ssd_const_specs():
    def whole(shape):
        nd = len(shape)
        return pl.BlockSpec(tuple(shape), lambda b, c: (0,) * nd)
    return [whole((CONV_K, 1, CONV_CH)), whole((1, CONV_CH)), whole((1, DT_PAD)), whole((1, DT_PAD)),
            whole((1, SSM_WIDTH)), whole((1, SSM_WIDTH)), whole((DT_PAD, SSM_WIDTH)), whole((SSM_WIDTH, DT_PAD))]


def _ssd_fwd(p_xbc, p_z, p_dt, conv_w, conv_b, dt_bias, a_log, dskip_map, norm_w, e_bf, et_bf, n_seq):
    n_tok = p_xbc.shape[0]
    nc = n_tok // n_seq // CHUNK

    def body(xr_ref, z_ref, pdt_ref, cw_ref, cb_ref, dtb_ref, alog_ref, dsk_ref, nw_ref, e_ref, et_ref,
             yb_ref, yssd_ref, sprev_ref, prev_scr, s_scr):
        @pl.when(pl.program_id(1) == 0)
        def _():
            prev_scr[...] = jnp.zeros_like(prev_scr)
            s_scr[...] = jnp.zeros_like(s_scr)

        xr = xr_ref[...]
        p = _ssd_pre(xr, prev_scr[...], cw_ref, cb_ref[...], pdt_ref[...], dtb_ref[...], alog_ref[...], e_ref[...])
        _, e_exp, dte, cd = _ssd_maps(p)
        xs = p["xa"][:, :SSM_WIDTH]
        xd = xs * p["dt_map"]
        a_cs_t = p["a_cs"].T
        tri = _tril_mask()
        s_old = s_scr[...]
        sprev_ref[...] = s_old
        for g in range(SSM_GROUPS):
            gs = slice(g * GROUP_W, (g + 1) * GROUP_W)
            bm = p["xa"][:, SSM_WIDTH + g * SSM_STATE: SSM_WIDTH + (g + 1) * SSM_STATE].astype(BF16)
            cm = p["xa"][:, SSM_WIDTH + (SSM_GROUPS + g) * SSM_STATE: SSM_WIDTH + (SSM_GROUPS + g + 1) * SSM_STATE].astype(BF16)
            cb_mat = _dot_nt(cm, bm)
            xdg = xd[:, gs].astype(BF16)
            y_g = _dot(cm, s_old[:, gs]) * e_exp[:, gs] + dsk_ref[:, gs] * xs[:, gs]
            for r in range(SSM_GROUPS * 2):
                dm = _head_decay(p["a_cs"], a_cs_t, g * 4 + r, tri)
                full = jnp.dot((cb_mat * dm).astype(BF16), xdg, preferred_element_type=F32)
                y_g = y_g + jnp.where(_head_lane_mask(GROUP_W, r), full, 0.0)
            yssd_ref[:, gs] = y_g
            s_scr[:, gs] = cd[:, gs] * s_old[:, gs] + _dot_tn(bm, xd[:, gs] * dte[:, gs])
        _, _, _, outs, _ = _gate_fwd(yssd_ref[...], z_ref[...], nw_ref[...])
        for g in range(SSM_GROUPS):
            yb_ref[:, g * GROUP_W:(g + 1) * GROUP_W] = outs[g].astype(BF16)
        prev_scr[...] = xr

    def rows(width):
        return pl.BlockSpec((CHUNK, width), lambda b, c: (b * nc + c, 0))

    return pl.pallas_call(
        body, name="ssd_fwd", grid=(n_seq, nc),
        in_specs=[rows(CONV_CH), rows(SSM_WIDTH), rows(DT_PAD)] + _ssd_const_specs(),
        out_specs=[rows(SSM_WIDTH), rows(SSM_WIDTH), rows(SSM_WIDTH)],
        out_shape=(_sds((n_tok, SSM_WIDTH), BF16), _sds((n_tok, SSM_WIDTH), F32), _sds((n_tok, SSM_WIDTH), F32)),
        scratch_shapes=[pltpu.VMEM((CHUNK, CONV_CH), F32), pltpu.VMEM((SSM_STATE, SSM_WIDTH), F32)],
        compiler_params=_cparams(2),
    )(p_xbc, p_z, p_dt, conv_w, conv_b, dt_bias, a_log, dskip_map, norm_w, e_bf, et_bf)


def _outproj_fwd(ya, yb, x, w_out, nw_post, nw_pre2, tm=256):
    n_tok = x.shape[0]

    def body(ya_ref, yb_ref, x_ref, wo_ref, nwa_ref, nwb_ref, o_ref, x1_ref, h2_ref):
        o = jnp.dot(ya_ref[...], wo_ref[:GM_WIDTH, :], preferred_element_type=F32)
        o = o + jnp.dot(yb_ref[...], wo_ref[GM_WIDTH:, :], preferred_element_type=F32)
        on, _ = _rms_fwd(o, nwa_ref[...])
        x1 = x_ref[...] + on
        h2, _ = _rms_fwd(x1, nwb_ref[...])
        o_ref[...] = o
        x1_ref[...] = x1
        h2_ref[...] = h2.astype(BF16)

    return _rows_call("outproj_fwd", body, tm, [ya, yb, x], [w_out, nw_post, nw_pre2],
                      [_sds((n_tok, D_MODEL), F32), _sds((n_tok, D_MODEL), F32), _sds((n_tok, D_MODEL), BF16)])


def _up_cols(wup_ref, j):
    per = (D_FF // N_CHIPS) // FF_TILE
    return wup_ref[j // per, :, (j % per) * FF_TILE:(j % per + 1) * FF_TILE]


def _mlp_fwd(h2, x1, tgt, w_up, w_down, nw, tm=256):
    n_tok = x1.shape[0]

    def body(h2_ref, x1_ref, tgt_ref, wup_ref, wdn_ref, nw_ref, up_ref, f_ref, dd_ref, dy_ref, loss_ref, dnw_ref):
        h2v = h2_ref[...]
        acc = jnp.zeros((tm, D_MODEL), F32)
        for j in range(D_FF // FF_TILE):
            cs = slice(j * FF_TILE, (j + 1) * FF_TILE)
            u = jnp.dot(h2v, _up_cols(wup_ref, j), preferred_element_type=F32)
            up_ref[:, cs] = u
            f = jnp.square(jnp.maximum(u, 0.0)).astype(BF16)
            f_ref[:, cs] = f
            acc = acc + jnp.dot(f, wdn_ref[cs, :], preferred_element_type=F32)
        dn, r = _rms_fwd(acc, nw_ref[...])
        e = x1_ref[...] + dn - tgt_ref[...]
        loss_ref[...] += jnp.full(loss_ref.shape, (0.5 / D_MODEL) * jnp.sum(e * e), F32)
        dy = e * (1.0 / D_MODEL)
        dd, dnw = _rms_bwd(acc, r, nw_ref[...], dy)
        dy_ref[...] = dy
        dd_ref[...] = dd.astype(BF16)
        dnw_ref[...] += dnw

    return _rows_call(
        "mlp_fwd", body, tm, [h2, x1, tgt], [w_up, w_down, nw],
        [_sds((n_tok, D_FF), F32), _sds((n_tok, D_FF), BF16), _sds((n_tok, D_MODEL), BF16), _sds((n_tok, D_MODEL), F32)],
        [_sds((8, 128), F32), _sds((1, D_MODEL), F32)])


def _mlp_bwd(dd, up, x1, dy, w_down, w_up, nw, tm=256):
    n_tok = x1.shape[0]

    def body(dd_ref, up_ref, x1_ref, dy_ref, wdn_ref, wup_ref, nw_ref, dup_ref, dx1_ref, dnw_ref):
        ddv = dd_ref[...]
        acc = jnp.zeros((tm, D_MODEL), F32)
        for j in range(D_FF // FF_TILE):
            cs = slice(j * FF_TILE, (j + 1) * FF_TILE)
            df = _dot_nt(ddv, wdn_ref[cs, :])
            du = (df * (2.0 * jnp.maximum(up_ref[:, cs], 0.0))).astype(BF16)
            dup_ref[:, cs] = du
            acc = acc + _dot_nt(du, _up_cols(wup_ref, j))
        x1v = x1_ref[...]
        _, r = _rms_fwd(x1v, nw_ref[...])
        dx, dnw = _rms_bwd(x1v, r, nw_ref[...], acc)
        dx1_ref[...] = dy_ref[...] + dx
        dnw_ref[...] += dnw

    return _rows_call("mlp_bwd", body, tm, [dd, up, x1, dy], [w_down, w_up, nw],
                      [_sds((n_tok, D_FF), BF16), _sds((n_tok, D_MODEL), F32)], [_sds((1, D_MODEL), F32)])


def _outproj_bwd(dx1, o, w_out, nw, tm=256):
    n_tok = dx1.shape[0]

    def body(dx1_ref, o_ref, wo_ref, nw_ref, do_ref, dya_ref, dyb_ref, dnw_ref):
        ov = o_ref[...]
        _, r = _rms_fwd(ov, nw_ref[...])
        do, dnw = _rms_bwd(ov, r, nw_ref[...], dx1_ref[...])
        dob = do.astype(BF16)
        do_ref[...] = dob
        dya_ref[...] = _dot_nt(dob, wo_ref[:GM_WIDTH, :])
        dyb_ref[...] = _dot_nt(dob, wo_ref[GM_WIDTH:, :])
        dnw_ref[...] += dnw

    return _rows_call("outproj_bwd", body, tm, [dx1, o], [w_out, nw],
                      [_sds((n_tok, D_MODEL), BF16), _sds((n_tok, GM_WIDTH), F32), _sds((n_tok, SSM_WIDTH), F32)],
                      [_sds((1, D_MODEL), F32)])


def _gmlp_bwd(p_uv, dya, lnw, lnb, e_bf, et_bf, w_cat, w_stack, bmap):
    n_tok = p_uv.shape[0]

    def body(puv_ref, dya_ref, lnw_ref, lnb_ref, e_ref, et_ref, wcat_ref, wstack_ref, bmap_ref,
             dpuv_ref, dws_ref, dbs_ref, dlnw_ref, dlnb_ref, wm_scr, wsm_scr):
        t_cat = lax.broadcasted_iota(jnp.int32, (CHUNK, N_HEADS * CHUNK), 0)
        s_cat = lax.broadcasted_iota(jnp.int32, (CHUNK, N_HEADS * CHUNK), 1) % CHUNK
        t_stk = lax.broadcasted_iota(jnp.int32, (N_HEADS * CHUNK, CHUNK), 0) % CHUNK
        s_stk = lax.broadcasted_iota(jnp.int32, (N_HEADS * CHUNK, CHUNK), 1)

        @pl.when(pl.program_id(0) == 0)
        def _():
            wm_scr[...] = jnp.where(t_cat >= s_cat, wcat_ref[...], 0.0).astype(BF16)
            wsm_scr[...] = jnp.where(t_stk >= s_stk, wstack_ref[...], 0.0).astype(BF16)

        lnw_v = lnw_ref[...]
        e_v, et_v = e_ref[...], et_ref[...]
        u, v, gu, tu, tv, rstd, xhat, vn = _gmlp_common(puv_ref[...], lnw_v, lnb_ref[...], e_v, et_v)
        vnb = vn.astype(BF16)
        mixed = jnp.dot(wm_scr[...], _head_blocks(vnb), preferred_element_type=F32) + bmap_ref[...]
        dy = dya_ref[...]
        du = dy * mixed * _gelu_grad(u, tu)
        dmixed = dy * gu
        (dbs,) = _seg_dots([dmixed], et_v)
        dbs_ref[...] += dbs
        dblocks = _head_blocks(dmixed.astype(BF16))
        dvn = lax.dot_general(wsm_scr[...], dblocks, (((0,), (0,)), ((), ())), preferred_element_type=F32)
        dws = lax.dot_general(dblocks, vnb, (((1,), (1,)), ((), ())), preferred_element_type=F32)
        dws_ref[...] += jnp.where(t_stk >= s_stk, dws, 0.0)
        dlnw_ref[...] += jnp.sum(dvn * xhat, axis=0, keepdims=True)
        dlnb_ref[...] += jnp.sum(dvn, axis=0, keepdims=True)
        dxh = dvn * lnw_v
        m1, m2 = _seg_dots([dxh, dxh * xhat], et_v)
        m1, m2 = _seg_dots([m1 * (1.0 / HEAD_DIM), m2 * (1.0 / HEAD_DIM)], e_v)
        dgv = rstd * (dxh - m1 - xhat * m2)
        dv = dgv * _gelu_grad(v, tv)
        dpuv_ref[:, :GM_WIDTH] = du.astype(BF16)
        dpuv_ref[:, GM_WIDTH:] = dv.astype(BF16)

    return _rows_call(
        "gmlp_bwd", body, CHUNK, [p_uv, dya], [lnw, lnb, e_bf, et_bf, w_cat, w_stack, bmap],
        [_sds((n_tok, 2 * GM_WIDTH), BF16)],
        [_sds((N_HEADS * CHUNK, CHUNK), F32), _sds((CHUNK, DT_PAD), F32), _sds((1, GM_WIDTH), F32),
         _sds((1, GM_WIDTH), F32)],
        scratch=[pltpu.VMEM((CHUNK, N_HEADS * CHUNK), BF16), pltpu.VMEM((N_HEADS * CHUNK, CHUNK), BF16)])


def _ssd_bwd(p_xbc, p_z, p_dt, yssd, sprev, dyb, conv_w, conv_b, dt_bias, a_log, dskip_map, norm_w, e_bf, et_bf, n_seq):
    n_tok = p_xbc.shape[0]
    nc = n_tok // n_seq // CHUNK

    def body(xr_ref, xprev_ref, z_ref, pdt_ref, yssd_ref, sprev_ref, dyb_ref,
             cw_ref, cb_ref, dtb_ref, alog_ref, dsk_ref, nw_ref, e_ref, et_ref,
             dpxbc_ref, dpz_ref, dpdt_ref, dcw_ref, dcb_ref, ddtb_ref, dalog_ref, ddsk_ref, dnw_ref,
             ds_scr, nxt_scr, dxa_scr):
        step = pl.program_id(1)
        first = jnp.logical_and(pl.program_id(0) == 0, step == 0)

        @pl.when(first)
        def _():
            for a in (dcw_ref, dcb_ref, ddtb_ref, dalog_ref, ddsk_ref, dnw_ref):
                a[...] = jnp.zeros_like(a)

        @pl.when(step == 0)
        def _():
            ds_scr[...] = jnp.zeros_like(ds_scr)
            nxt_scr[...] = jnp.zeros_like(nxt_scr)

        chunk = nc - 1 - step
        xr = xr_ref[...]
        prev = jnp.where(chunk == 0, 0.0, xprev_ref[...])
        et_v = et_ref[...]
        p = _ssd_pre(xr, prev, cw_ref, cb_ref[...], pdt_ref[...], dtb_ref[...], alog_ref[...], e_ref[...])
        last, e_exp, dte, cd = _ssd_maps(p)
        rowi = p["rowi"]
        xs = p["xa"][:, :SSM_WIDTH]
        xd = xs * p["dt_map"]
        a_cs_t = p["a_cs"].T
        tri = _tril_mask()
        dsk = dsk_ref[...]
        nw_v = nw_ref[...]

        yv = yssd_ref[...]
        zv = z_ref[...]
        sz, zg, yg, _, rs = _gate_fwd(yv, zv, nw_v)
        dout = dyb_ref[...]
        for g in range(SSM_GROUPS):
            gs = slice(g * GROUP_W, (g + 1) * GROUP_W)
            dyg_g, dnw_g = _rms_bwd(yg[:, gs], rs[g], nw_v[:, gs], dout[:, gs])
            dnw_ref[:, gs] += dnw_g
            dxa_scr[:, gs] = dyg_g
        dyg = dxa_scr[:, :SSM_WIDTH]
        d_y = dyg * zg
        dpz_ref[...] = (dyg * yv * (sz + zv * sz * (1.0 - sz))).astype(BF16)

        s_prev = sprev_ref[...]
        ds_next = ds_scr[...]
        lane_dt = lax.broadcasted_iota(jnp.int32, (1, DT_PAD), 1)
        da_cols = jnp.zeros((CHUNK, DT_PAD), F32)
        for g in range(SSM_GROUPS):
            gs = slice(g * GROUP_W, (g + 1) * GROUP_W)
            b_off = SSM_WIDTH + g * SSM_STATE
            c_off = SSM_WIDTH + (SSM_GROUPS + g) * SSM_STATE
            bm = p["xa"][:, b_off:b_off + SSM_STATE].astype(BF16)
            cm = p["xa"][:, c_off:c_off + SSM_STATE].astype(BF16)
            cb_mat = _dot_nt(cm, bm)
            d_yg = d_y[:, gs]
            d_ygb = d_yg.astype(BF16)
            xdg = xd[:, gs]
            xdgb = xdg.astype(BF16)
            ds_g = ds_next[:, gs]
            sp_g = s_prev[:, gs]
            bds = _dot(bm, ds_g)
            dcs = d_yg * e_exp[:, gs]
            d_c = _dot_nt(dcs, sp_g)
            ds_scr[:, gs] = cd[:, gs] * ds_g + _dot_tn(cm, dcs)
            d_b = _dot_nt(xdg * dte[:, gs], ds_g)
            dxd_g = bds * dte[:, gs]
            sum_dcb = jnp.zeros((CHUNK, CHUNK), F32)
            for r in range(SSM_GROUPS * 2):
                head = g * 4 + r
                mask = _head_lane_mask(GROUP_W, r)
                dm = _head_decay(p["a_cs"], a_cs_t, head, tri)
                m_mat = cb_mat * dm
                g_mat = _dot_nt(jnp.where(mask, d_yg, 0.0), xdgb)
                w_mat = g_mat * m_mat
                sum_dcb = sum_dcb + g_mat * dm
                dxd_g = dxd_g + jnp.where(mask, _dot_tn(m_mat, d_ygb), 0.0)
                da_h = jnp.sum(w_mat - w_mat.T, axis=1, keepdims=True)
                da_cols = da_cols + jnp.where(lane_dt == head, da_h, 0.0)
            d_c = d_c + _dot(sum_dcb, bm)
            d_b = d_b + _dot_tn(sum_dcb, cm)
            dxa_scr[:, b_off:b_off + SSM_STATE] = d_b
            dxa_scr[:, c_off:c_off + SSM_STATE] = d_c
            y_off_g = _dot(cm, sp_g) * e_exp[:, gs]
            t3 = bds * xdg * dte[:, gs]
            tail = jnp.sum(t3, axis=0, keepdims=True) + jnp.sum(ds_g * sp_g, axis=0, keepdims=True) * cd[:, gs]
            pre_g = d_yg * y_off_g - t3 + jnp.where(last, tail, 0.0)
            s_pre, ddt_g, s_dsk = _seg_dots([pre_g, dxd_g * xs[:, gs], d_yg * xs[:, gs]], et_v[gs, :])
            da_cols = da_cols + s_pre
            ddsk_ref[...] += jnp.sum(s_dsk, axis=0, keepdims=True)
            dxa_scr[:, gs] = dxd_g * p["dt_map"][:, gs] + dsk[:, gs] * d_yg
            if g == 0:
                ddt = ddt_g
            else:
                ddt = ddt + ddt_g
        r_i = lax.broadcasted_iota(jnp.int32, (CHUNK, CHUNK), 0)
        c_i = lax.broadcasted_iota(jnp.int32, (CHUNK, CHUNK), 1)
        ddta = _tri_dot(r_i <= c_i, da_cols, terms=2)
        ddt = ddt + ddta * p["a_neg"]
        dalog_ref[...] += jnp.sum(ddta * p["dt"], axis=0, keepdims=True) * p["a_neg"]
        draw = ddt * _sigmoid(p["pre"])
        ddtb_ref[...] += jnp.sum(draw, axis=0, keepdims=True)
        dpdt_ref[...] = draw.astype(BF16)

        xc = p["xc"]
        sg = p["sg"]
        dxc = dxa_scr[...] * (sg + xc * sg * (1.0 - sg))
        dcb_ref[...] += jnp.sum(dxc, axis=0, keepdims=True)
        for k in range(CONV_K):
            dcw_ref[k] += jnp.sum(dxc * p["shifted"][k], axis=0, keepdims=True)
        nxt = nxt_scr[...]

        def up(s):
            return jnp.where(rowi >= CHUNK - s, pltpu.roll(nxt, CHUNK - s, 0), pltpu.roll(dxc, CHUNK - s, 0))

        dxr = cw_ref[3] * dxc + cw_ref[2] * up(1) + cw_ref[1] * up(2) + cw_ref[0] * up(3)
        dpxbc_ref[...] = dxr.astype(BF16)
        nxt_scr[...] = dxc

    def rows(width):
        return pl.BlockSpec((CHUNK, width), lambda b, s: (b * nc + nc - 1 - s, 0))

    prev_rows = pl.BlockSpec((CHUNK, CONV_CH), lambda b, s: (b * nc + jnp.maximum(nc - 2 - s, 0), 0))

    def whole(shape):
        nd = len(shape)
        return pl.BlockSpec(tuple(shape), lambda b, s: (0,) * nd)

    acc_shapes = [(CONV_K, 1, CONV_CH), (1, CONV_CH), (1, DT_PAD), (1, DT_PAD), (1, DT_PAD), (1, SSM_WIDTH)]
    return pl.pallas_call(
        body, name="ssd_bwd", grid=(n_seq, nc),
        in_specs=[rows(CONV_CH), prev_rows, rows(SSM_WIDTH), rows(DT_PAD), rows(SSM_WIDTH), rows(SSM_WIDTH),
                  rows(SSM_WIDTH)] + _ssd_const_specs(),
        out_specs=[rows(CONV_CH), rows(SSM_WIDTH), rows(DT_PAD)] + [whole(s) for s in acc_shapes],
        out_shape=tuple([_sds((n_tok, CONV_CH), BF16), _sds((n_tok, SSM_WIDTH), BF16), _sds((n_tok, DT_PAD), BF16)]
                        + [_sds(s, F32) for s in acc_shapes]),
        scratch_shapes=[pltpu.VMEM((SSM_STATE, SSM_WIDTH), F32), pltpu.VMEM((CHUNK, CONV_CH), F32),
                        pltpu.VMEM((CHUNK, CONV_CH), F32)],
        compiler_params=_cparams(2),
    )(p_xbc, p_xbc, p_z, p_dt, yssd, sprev, dyb, conv_w, conv_b, dt_bias, a_log, dskip_map, norm_w, e_bf, et_bf)


def _inproj_bwd(dp_uv, dp_xbc, dp_z, dp_dt, x, dx1, w_uv, w_xbc, w_z, w_dt, nw, tm=256):
    n_tok = x.shape[0]

    def body(duv_ref, dxbc_ref, dz_ref, ddt_ref, x_ref, dx1_ref, wuv_ref, wxbc_ref, wz_ref, wdt_ref, nw_ref,
             gx_ref, h_ref, dnw_ref):
        dh = _dot_nt(duv_ref[...], wuv_ref[...]) + _dot_nt(dxbc_ref[...], wxbc_ref[...])
        dh = dh + _dot_nt(dz_ref[...], wz_ref[...]) + _dot_nt(ddt_ref[...], wdt_ref[...])
        xv = x_ref[...]
        h, r = _rms_fwd(xv, nw_ref[...])
        dx, dnw = _rms_bwd(xv, r, nw_ref[...], dh)
        gx_ref[...] = dx1_ref[...] + dx
        h_ref[...] = h.astype(BF16)
        dnw_ref[...] += dnw

    return _rows_call("inproj_bwd", body, tm, [dp_uv, dp_xbc, dp_z, dp_dt, x, dx1], [w_uv, w_xbc, w_z, w_dt, nw],
                      [_sds((n_tok, D_MODEL), F32), _sds((n_tok, D_MODEL), BF16)], [_sds((1, D_MODEL), F32)])


def _const_maps():
    lane = jnp.arange(SSM_WIDTH) // HEAD_DIM
    e_bf = (jnp.arange(DT_PAD)[:, None] == lane[None, :]).astype(BF16)
    return e_bf, e_bf.T


def _pad_lanes(v, width):
    return jnp.pad(v, ((0, 0), (0, width - v.shape[1])))


SHARD_COLS = IN_COLS // N_CHIPS
_UV_END = 2 * GM_WIDTH
_Z_END = _UV_END + SSM_WIDTH
_XBC_END = _Z_END + CONV_CH


def _cols_from_shards(w4, lo, hi):
    pieces = []
    for j in range(N_CHIPS):
        a, b = max(lo, j * SHARD_COLS), min(hi, (j + 1) * SHARD_COLS)
        if a < b:
            pieces.append(w4[j][:, a - j * SHARD_COLS:b - j * SHARD_COLS])
    return pieces[0] if len(pieces) == 1 else jnp.concatenate(pieces, axis=1)


def _shards_from_cols(blocks):
    shards = []
    for j in range(N_CHIPS):
        pieces = []
        for arr, lo, hi in blocks:
            a, b = max(lo, j * SHARD_COLS), min(hi, (j + 1) * SHARD_COLS)
            if a < b:
                pieces.append(arr[:, a - lo:b - lo])
        shards.append(pieces[0] if len(pieces) == 1 else jnp.concatenate(pieces, axis=1))
    return jnp.stack(shards)


def _local_grads(x, tgt, w_in4, w_out_b, w_up4, w_down_b, conv_w, small):
    n_seq, seq_len, _ = x.shape
    n_tok = n_seq * seq_len
    x2 = x.reshape(n_tok, D_MODEL)
    tgt2 = tgt.reshape(n_tok, D_MODEL)
    e_bf, et_bf = _const_maps()

    w_uv = _cols_from_shards(w_in4, 0, _UV_END)
    w_z = _cols_from_shards(w_in4, _UV_END, _Z_END)
    w_xbc = _cols_from_shards(w_in4, _Z_END, _XBC_END)
    w_dt = _pad_lanes(_cols_from_shards(w_in4, _XBC_END, IN_COLS), DT_PAD)

    nw_pre = small["norm_mix_pre"]
    lnw = small["gm_ln_w"].reshape(1, GM_WIDTH)
    lnb = small["gm_ln_b"].reshape(1, GM_WIDTH)
    w_stack = small["gm_w_s"].reshape(N_HEADS * CHUNK, CHUNK)
    w_cat = jnp.transpose(small["gm_w_s"], (1, 0, 2)).reshape(CHUNK, N_HEADS * CHUNK)
    bmap = jnp.repeat(small["gm_b_s"].T, HEAD_DIM, axis=1)
    cw3 = conv_w.reshape(CONV_K, 1, CONV_CH)
    conv_b = small["conv_b"]
    dt_bias = _pad_lanes(small["dt_bias"], DT_PAD)
    a_log = _pad_lanes(small["a_log"], DT_PAD)
    dskip_map = jnp.repeat(small["d_skip"], HEAD_DIM, axis=1)
    ssm_nw = small["ssm_norm_w"]

    p_uv, p_xbc, p_z, p_dt = _inproj_fwd(x2, nw_pre, w_uv, w_xbc, w_z, w_dt)
    ya = _gmlp_fwd(p_uv, lnw, lnb, e_bf, et_bf, w_cat, bmap)
    ssd_consts = (cw3, conv_b, dt_bias, a_log, dskip_map, ssm_nw, e_bf, et_bf)
    yb, yssd, sprev = _ssd_fwd(p_xbc, p_z, p_dt, *ssd_consts, n_seq)
    o, x1, h2 = _outproj_fwd(ya, yb, x2, w_out_b, small["norm_mix_post"], small["norm_ffn_pre"])
    up, f, dd, dy, loss_acc, d_nffn_post = _mlp_fwd(h2, x1, tgt2, w_up4, w_down_b, small["norm_ffn_post"])

    dup, dx1, d_nffn_pre = _mlp_bwd(dd, up, x1, dy, w_down_b, w_up4, small["norm_ffn_pre"])
    g_up = _matmul_tn("dw_up", h2, dup, D_MODEL, D_MODEL, 512, stacked=True)
    g_down = _matmul_tn("dw_down", f, dd, 1024, D_MODEL, 512)
    do, dya, dyb, d_nmix_post = _outproj_bwd(dx1, o, w_out_b, small["norm_mix_post"])
    g_out_a = _matmul_tn("dw_out_a", ya, do, GM_WIDTH, D_MODEL, 512)
    g_out_b = _matmul_tn("dw_out_b", yb, do, SSM_WIDTH, D_MODEL, 512)
    dp_uv, d_ws, d_bs_t, d_lnw, d_lnb = _gmlp_bwd(p_uv, dya, lnw, lnb, e_bf, et_bf, w_cat, w_stack, bmap)
    (dp_xbc, dp_z, dp_dt, d_cw, d_cb, d_dtb, d_alog, d_dsk, d_ssm_nw) = _ssd_bwd(
        p_xbc, p_z, p_dt, yssd, sprev, dyb, *ssd_consts, n_seq)
    gx, h, d_nmix_pre = _inproj_bwd(dp_uv, dp_xbc, dp_z, dp_dt, x2, dx1, w_uv, w_xbc, w_z, w_dt, nw_pre)
    g_uv = _matmul_tn("dw_in_uv", h, dp_uv, D_MODEL, 2 * GM_WIDTH, 512)
    g_xbc = _matmul_tn("dw_in_xbc", h, dp_xbc, D_MODEL, CONV_CH, 512)
    g_z = _matmul_tn("dw_in_z", h, dp_z, D_MODEL, SSM_WIDTH, 512)
    g_dt = _matmul_tn("dw_in_dt", h, dp_dt, D_MODEL, DT_PAD, 512)

    g_in = _shards_from_cols([(g_uv, 0, _UV_END), (g_z, _UV_END, _Z_END), (g_xbc, _Z_END, _XBC_END),
                              (g_dt, _XBC_END, IN_COLS)])
    g_out = jnp.concatenate([g_out_a, g_out_b], axis=0)
    small_grads = {
        "norm_mix_pre": d_nmix_pre, "gm_ln_w": d_lnw.reshape(N_HEADS, HEAD_DIM), "gm_ln_b": d_lnb.reshape(N_HEADS, HEAD_DIM),
        "gm_w_s": d_ws.reshape(N_HEADS, CHUNK, CHUNK), "gm_b_s": d_bs_t[:, :N_HEADS].T, "conv_w": d_cw.reshape(CONV_K, CONV_CH), "conv_b": d_cb,
        "dt_bias": d_dtb[:, :N_HEADS], "a_log": d_alog[:, :N_HEADS], "d_skip": d_dsk[:, :N_HEADS],
        "ssm_norm_w": d_ssm_nw, "norm_mix_post": d_nmix_post, "norm_ffn_pre": d_nffn_pre,
        "norm_ffn_post": d_nffn_post,
    }
    return loss_acc[0, 0], gx.reshape(x.shape), g_in, g_out, g_up, g_down, small_grads


_HBM = pl.BlockSpec(memory_space=pltpu.HBM)


D2D_CHUNKS = 8
ROW_ALIGN = 16


def _row_chunks(rows, n_chunks):
    size = min(max(rows // n_chunks, ROW_ALIGN), rows)
    assert rows % size == 0
    return [(start, size) for start in range(0, rows, size)]


def _position():
    x, y, c = lax.axis_index("x"), lax.axis_index("y"), lax.axis_index("c")
    chips = [(1 - x, y), (x, 1 - y), (1 - x, 1 - y)]
    return x, y, c, chips


def _allgather_chips(arrs):
    n = len(arrs)

    def body(*refs):
        ins, outs = refs[:n], refs[n:2 * n]
        send_sems, recv_sems, local_sems = refs[2 * n:]
        x, y, c, chips = _position()
        me = 2 * x + y
        sibling = (x, y, 1 - c)

        def copy(a, k, src, dst, to):
            return pltpu.make_async_remote_copy(src_ref=src, dst_ref=dst, send_sem=send_sems.at[a, k],
                                                recv_sem=recv_sems.at[a, k], device_id=to, device_id_type=MESH)

        def half_rows(a, pc):
            half = ins[a].shape[0] // 2
            return pl.ds(pc * half, half)

        started = []
        for a in range(n):
            local = pltpu.make_async_copy(ins[a], outs[a].at[me], local_sems.at[a])
            local.start()
            started.append(local)
        sends = []
        for a in range(n):
            mine = half_rows(a, c)
            for k, (px, py) in enumerate(chips):
                cp = copy(a, k, ins[a].at[mine], outs[a].at[me, mine], (px, py, c))
                cp.start()
                sends.append(cp)
        for a in range(n):
            half = ins[a].shape[0] // 2
            for k, (px, py) in enumerate(chips):
                blk = outs[a].at[2 * px + py, half_rows(a, c)]
                copy(a, k, blk, blk, (px, py, c)).wait_recv()
                for start, size in _row_chunks(half, D2D_CHUNKS):
                    piece = outs[a].at[2 * px + py, pl.ds(c * half + start, size)]
                    copy(a, 3 + k, piece, piece, sibling).start()
                sends.append(copy(a, 3 + k, blk, blk, sibling))
        for a in range(n):
            for k, (px, py) in enumerate(chips):
                blk = outs[a].at[2 * px + py, half_rows(a, 1 - c)]
                copy(a, 3 + k, blk, blk, sibling).wait_recv()
        for cp in sends:
            cp.wait_send()
        for local in started:
            local.wait()

    return pl.pallas_call(
        body, name="allgather_weights",
        out_shape=tuple(_sds((N_CHIPS,) + a.shape, a.dtype) for a in arrs),
        in_specs=[_HBM] * n, out_specs=tuple([_HBM] * n),
        scratch_shapes=[pltpu.SemaphoreType.DMA((n, 6)), pltpu.SemaphoreType.DMA((n, 6)), pltpu.SemaphoreType.DMA((n,))],
    )(*arrs)


def _pair_exchange(grads):
    n = len(grads)

    def body(*refs):
        ins, got = refs[:n], refs[n:2 * n]
        send_sems, recv_sems = refs[2 * n:]
        x, y, c, _ = _position()
        sibling = (x, y, 1 - c)

        def copy(a, src, dst):
            return pltpu.make_async_remote_copy(src_ref=src, dst_ref=dst, send_sem=send_sems.at[a],
                                                recv_sem=recv_sems.at[a], device_id=sibling, device_id_type=MESH)

        for a in range(n):
            half = ins[a].shape[1] // 2
            for slab in range(N_CHIPS):
                for start, size in _row_chunks(half, D2D_CHUNKS):
                    copy(a, ins[a].at[slab, pl.ds((1 - c) * half + start, size), :],
                         got[a].at[slab, pl.ds(start, size), :]).start()
        for a in range(n):
            half = ins[a].shape[1] // 2
            copy(a, ins[a].at[:, pl.ds((1 - c) * half, half), :], got[a]).wait()

    halves = tuple(_sds((N_CHIPS, g.shape[1] // 2, g.shape[2]), g.dtype) for g in grads)
    return pl.pallas_call(
        body, name="grad_pair_exchange", out_shape=halves, in_specs=[_HBM] * n, out_specs=tuple([_HBM] * n),
        scratch_shapes=[pltpu.SemaphoreType.DMA((n,)), pltpu.SemaphoreType.DMA((n,))],
    )(*grads)


def _chip_exchange(hsums):
    n = len(hsums)

    def body(*refs):
        ins, outs = refs[:n], refs[n:2 * n]
        send_sems, recv_sems, local_sems = refs[2 * n:]
        x, y, c, chips = _position()
        me = 2 * x + y
        cps = []
        for a in range(n):
            local = pltpu.make_async_copy(ins[a].at[me], outs[a].at[me], local_sems.at[a])
            local.start()
            cps.append(local)
            for k, (px, py) in enumerate(chips):
                cp = pltpu.make_async_remote_copy(
                    src_ref=ins[a].at[2 * px + py], dst_ref=outs[a].at[me], send_sem=send_sems.at[a, k],
                    recv_sem=recv_sems.at[a, k], device_id=(px, py, c), device_id_type=MESH)
                cp.start()
                cps.append(cp)
        for cp in cps:
            cp.wait()

    return pl.pallas_call(
        body, name="grad_chip_exchange", out_shape=tuple(_sds(h.shape, h.dtype) for h in hsums),
        in_specs=[_HBM] * n, out_specs=tuple([_HBM] * n),
        scratch_shapes=[pltpu.SemaphoreType.DMA((n, 3)), pltpu.SemaphoreType.DMA((n, 3)), pltpu.SemaphoreType.DMA((n,))],
    )(*hsums)


def _pair_gather(bufs):
    n = len(bufs)

    def body(*refs):
        outs = refs[n:2 * n]
        send_sems, recv_sems = refs[2 * n:]
        x, y, c, _ = _position()
        sibling = (x, y, 1 - c)

        def copy(a, rows):
            return pltpu.make_async_remote_copy(src_ref=rows, dst_ref=rows, send_sem=send_sems.at[a],
                                                recv_sem=recv_sems.at[a], device_id=sibling, device_id_type=MESH)

        for a in range(n):
            half = outs[a].shape[0] // 2
            for start, size in _row_chunks(half, 2 * D2D_CHUNKS):
                copy(a, outs[a].at[pl.ds(c * half + start, size), :]).start()
        for a in range(n):
            half = outs[a].shape[0] // 2
            copy(a, outs[a].at[pl.ds(c * half, half), :]).wait_send()
            copy(a, outs[a].at[pl.ds((1 - c) * half, half), :]).wait_recv()

    return pl.pallas_call(
        body, name="grad_pair_gather", out_shape=tuple(_sds(b.shape, b.dtype) for b in bufs),
        in_specs=[_HBM] * n, out_specs=tuple([_HBM] * n), input_output_aliases={a: a for a in range(n)},
        scratch_shapes=[pltpu.SemaphoreType.DMA((n,)), pltpu.SemaphoreType.DMA((n,))],
    )(*bufs)


def _small_allreduce(packed):
    m_per, n_cols = packed.shape

    def body(x_ref, sum_ref, all_ref, send_sems, recv_sems, local_sem):
        x, y, c, chips = _position()
        me, sibling = (x, y, c), (x, y, 1 - c)

        def rows(px, py, pc):
            return all_ref.at[pl.ds((4 * px + 2 * py + pc) * m_per, m_per), :]

        def copy(k, block, to, src=None):
            return pltpu.make_async_remote_copy(
                src_ref=rows(*block) if src is None else src, dst_ref=rows(*block), send_sem=send_sems.at[k],
                recv_sem=recv_sems.at[k], device_id=to, device_id_type=MESH)

        mine = pltpu.make_async_copy(x_ref, rows(*me), local_sem)
        mine.start()
        first = [copy(0, me, sibling, src=x_ref)]
        first += [copy(1 + j, me, (*chip, c), src=x_ref) for j, chip in enumerate(chips)]
        for cp in first:
            cp.start()
        passed = [copy(4 + j, (*chip, c), sibling) for j, chip in enumerate(chips)]
        for j, chip in enumerate(chips):
            copy(1 + j, (*chip, c), me).wait_recv()
            passed[j].start()
        copy(0, sibling, me).wait_recv()
        for j, chip in enumerate(chips):
            copy(4 + j, (*chip, 1 - c), me).wait_recv()
        for cp in first + passed:
            cp.wait_send()
        mine.wait()
        acc = all_ref[0:m_per, :]
        for d in range(1, N_DEV):
            acc = acc + all_ref[d * m_per:(d + 1) * m_per, :]
        sum_ref[...] = acc

    vmem = pl.BlockSpec(memory_space=pltpu.VMEM)
    total, _ = pl.pallas_call(
        body, name="small_allreduce",
        out_shape=(_sds((m_per, n_cols), F32), _sds((N_DEV * m_per, n_cols), F32)),
        in_specs=[vmem], out_specs=(vmem, vmem),
        scratch_shapes=[pltpu.SemaphoreType.DMA((7,)), pltpu.SemaphoreType.DMA((7,)), pltpu.SemaphoreType.DMA],
    )(packed)
    return total


def _pair_sum(core, own, got, tm):
    _, half, cols = got.shape
    nb = half // tm

    def body(c_ref, a_ref, b_ref, o_ref):
        o_ref[...] = (a_ref[...] + b_ref[...]).astype(BF16)

    return pl.pallas_call(
        body, name="grad_pair_sum", out_shape=_sds(got.shape, BF16),
        grid_spec=pltpu.PrefetchScalarGridSpec(
            num_scalar_prefetch=1, grid=(N_CHIPS, nb),
            in_specs=[pl.BlockSpec((None, tm, cols), lambda s, i, c_ref: (s, c_ref[0] * nb + i, 0)),
                      pl.BlockSpec((None, tm, cols), lambda s, i, c_ref: (s, i, 0))],
            out_specs=pl.BlockSpec((None, tm, cols), lambda s, i, c_ref: (s, i, 0))),
        compiler_params=_cparams(2),
    )(core, own, got)


def _chip_sum(core, slabs, tm):
    _, half, cols = slabs.shape
    nb = half // tm

    def body(c_ref, s_ref, o_ref):
        acc = s_ref[0].astype(F32)
        for k in range(1, N_CHIPS):
            acc = acc + s_ref[k].astype(F32)
        o_ref[...] = acc

    return pl.pallas_call(
        body, name="grad_chip_sum", out_shape=_sds((2 * half, cols), F32),
        grid_spec=pltpu.PrefetchScalarGridSpec(
            num_scalar_prefetch=1, grid=(nb,),
            in_specs=[pl.BlockSpec((N_CHIPS, tm, cols), lambda i, c_ref: (0, i, 0))],
            out_specs=pl.BlockSpec((tm, cols), lambda i, c_ref: (c_ref[0] * nb + i, 0))),
        compiler_params=_cparams(1),
    )(core, slabs)


def _adam_math(w, g, m, v):
    m2 = ADAM_B1 * m + (1.0 - ADAM_B1) * g
    v2 = ADAM_B2 * v + (1.0 - ADAM_B2) * (g * g)
    m_hat = m2 / (1.0 - ADAM_B1 ** ADAM_STEP)
    v_hat = v2 / (1.0 - ADAM_B2 ** ADAM_STEP)
    delta = -ADAM_LR * (m_hat / (jnp.sqrt(v_hat) + ADAM_EPS) + ADAM_WD * w)
    return delta, m2, v2


def _adamw(name, w, g, m, v, tm):
    def body(w_ref, g_ref, m_ref, v_ref, d_ref, m2_ref, v2_ref):
        d, m2, v2 = _adam_math(w_ref[...], g_ref[...], m_ref[...], v_ref[...])
        d_ref[...] = d
        m2_ref[...] = m2
        v2_ref[...] = v2

    return _rows_call(name, body, tm, [w, g, m, v], [], [_sds(w.shape, F32)] * 3)


_SMALL_NAMES = ("norm_mix_pre", "gm_ln_w", "gm_ln_b", "gm_w_s", "gm_b_s", "conv_w", "conv_b", "dt_bias", "a_log",
                "d_skip", "ssm_norm_w", "norm_mix_post", "norm_ffn_pre", "norm_ffn_post")
_PACK_COLS = 1024


def _pack(parts, tail=None):
    pieces = [parts[n].reshape(-1) for n in _SMALL_NAMES]
    flat = jnp.concatenate(pieces if tail is None else pieces + [tail])
    rows = -(-flat.shape[0] // (8 * _PACK_COLS)) * 8
    flat = jnp.pad(flat, (0, rows * _PACK_COLS - flat.shape[0]))
    return flat.reshape(rows, _PACK_COLS)


def _unpack(packed, shapes):
    flat = packed.reshape(-1)
    out, off = {}, 0
    for n in _SMALL_NAMES:
        size = 1
        for s in shapes[n]:
            size *= s
        out[n] = flat[off:off + size].reshape(shapes[n])
        off += size
    return out


def kernel(x, norm_mix_pre, w_in, gm_ln_w, gm_ln_b, gm_w_s, gm_b_s, conv_w, conv_b, dt_bias, a_log, d_skip, ssm_norm_w, w_out, norm_mix_post, norm_ffn_pre, w_up, w_down, norm_ffn_post, loss_target, m_norm_mix_pre, m_w_in, m_gm_ln_w, m_gm_ln_b, m_gm_w_s, m_gm_b_s, m_conv_w, m_conv_b, m_dt_bias, m_a_log, m_d_skip, m_ssm_norm_w, m_w_out, m_norm_mix_post, m_norm_ffn_pre, m_w_up, m_w_down, m_norm_ffn_post, v_norm_mix_pre, v_w_in, v_gm_ln_w, v_gm_ln_b, v_gm_w_s, v_gm_b_s, v_conv_w, v_conv_b, v_dt_bias, v_a_log, v_d_skip, v_ssm_norm_w, v_w_out, v_norm_mix_post, v_norm_ffn_pre, v_w_up, v_w_down, v_norm_ffn_post):
    params = dict(norm_mix_pre=norm_mix_pre, w_in=w_in, gm_ln_w=gm_ln_w, gm_ln_b=gm_ln_b, gm_w_s=gm_w_s, gm_b_s=gm_b_s,
                  conv_w=conv_w, conv_b=conv_b, dt_bias=dt_bias, a_log=a_log, d_skip=d_skip, ssm_norm_w=ssm_norm_w,
                  w_out=w_out, norm_mix_post=norm_mix_post, norm_ffn_pre=norm_ffn_pre, w_up=w_up, w_down=w_down,
                  norm_ffn_post=norm_ffn_post)
    mom1 = dict(norm_mix_pre=m_norm_mix_pre, w_in=m_w_in, gm_ln_w=m_gm_ln_w, gm_ln_b=m_gm_ln_b, gm_w_s=m_gm_w_s,
                gm_b_s=m_gm_b_s, conv_w=m_conv_w, conv_b=m_conv_b, dt_bias=m_dt_bias, a_log=m_a_log, d_skip=m_d_skip,
                ssm_norm_w=m_ssm_norm_w, w_out=m_w_out, norm_mix_post=m_norm_mix_post, norm_ffn_pre=m_norm_ffn_pre,
                w_up=m_w_up, w_down=m_w_down, norm_ffn_post=m_norm_ffn_post)
    mom2 = dict(norm_mix_pre=v_norm_mix_pre, w_in=v_w_in, gm_ln_w=v_gm_ln_w, gm_ln_b=v_gm_ln_b, gm_w_s=v_gm_w_s,
                gm_b_s=v_gm_b_s, conv_w=v_conv_w, conv_b=v_conv_b, dt_bias=v_dt_bias, a_log=v_a_log, d_skip=v_d_skip,
                ssm_norm_w=v_ssm_norm_w, w_out=v_w_out, norm_mix_post=v_norm_mix_post, norm_ffn_pre=v_norm_ffn_pre,
                w_up=v_w_up, w_down=v_w_down, norm_ffn_post=v_norm_ffn_post)
    names = list(params)
    big = ("w_in", "w_out", "w_up", "w_down")
    chip = 2 * lax.axis_index("x") + lax.axis_index("y")

    shards = [params[n][0].astype(BF16) for n in big]
    conv_shard = jnp.pad(conv_w[0], ((0, 16 - CONV_K), (0, 0)))
    g_in4, g_out4, g_up4, g_down4, g_conv4 = _allgather_chips(shards + [conv_shard])
    w_out_b = g_out4.reshape(D_MODEL, D_MODEL)
    w_down_b = g_down4.reshape(D_FF, D_MODEL)
    conv_full = jnp.transpose(g_conv4[:, :CONV_K, :], (1, 0, 2)).reshape(CONV_K, CONV_CH)

    small = {n: params[n][0] if params[n].ndim >= 3 else params[n] for n in _SMALL_NAMES if n != "conv_w"}
    loss_part, grad_x, g_in, g_out, g_up, g_down, small_grads = _local_grads(
        x, loss_target, g_in4, w_out_b, g_up4, w_down_b, conv_full, small)

    stacked = [g_in, g_out.reshape(N_CHIPS, D_MODEL // N_CHIPS, D_MODEL), g_up,
               g_down.reshape(N_CHIPS, D_FF // N_CHIPS, D_MODEL)]
    core = lax.axis_index("c").astype(jnp.int32).reshape(1)
    got = _pair_exchange(stacked)
    hsums = [_pair_sum(core, s, g, tm) for s, g, tm in zip(stacked, got, (256, 128, 256, 256))]
    slabs = _chip_exchange(hsums)
    reds = [_chip_sum(core, s, tm) for s, tm in zip(slabs, (256, 128, 256, 256))]
    big_grads = dict(zip(big, _pair_gather(reds)))

    small_shapes = {n: (small_grads[n].shape) for n in _SMALL_NAMES}
    small_total = _small_allreduce(_pack(small_grads, tail=loss_part.reshape(1)))
    small_sum = _unpack(small_total, small_shapes)
    loss = small_total.reshape(-1)[sum(small_grads[n].size for n in _SMALL_NAMES)]
    small_sum["conv_w"] = lax.dynamic_slice_in_dim(small_sum["conv_w"], chip * (CONV_CH // N_CHIPS), CONV_CH // N_CHIPS, axis=1)

    grads, delta, new_m, new_v = {}, {}, {}, {}
    for n, tm in zip(big, (256, 128, 256, 256)):
        g = big_grads[n]
        d, m2, v2 = _adamw("adamw_" + n, params[n][0], g, mom1[n][0], mom2[n][0], tm)
        grads[n], delta[n], new_m[n], new_v[n] = g[None], d[None], m2[None], v2[None]
    local_shapes = {n: params[n].shape[1:] if params[n].ndim >= 3 else params[n].shape for n in _SMALL_NAMES}
    flat = lambda tree: {n: tree[n].reshape(local_shapes[n]) for n in _SMALL_NAMES}
    packed = [_pack(flat(t)) for t in (params, small_sum, mom1, mom2)]
    d_p, m_p, v_p = _adamw("adamw_small", *packed, packed[0].shape[0])
    for src, dst in ((d_p, delta), (m_p, new_m), (v_p, new_v)):
        for n, val in _unpack(src, local_shapes).items():
            dst[n] = val.reshape(params[n].shape)
    for n in _SMALL_NAMES:
        grads[n] = small_sum[n].reshape(params[n].shape)

    out = [loss, grad_x]
    for tree in (grads, delta, new_m, new_v):
        out += [tree[n] for n in names]
    return tuple(out)
```

```python
import functools

import jax
import jax.numpy as jnp
from jax import lax
from jax.experimental import pallas as pl
from jax.experimental.pallas import tpu as pltpu

F32 = jnp.float32
BF16 = jnp.bfloat16
HI = lax.Precision.HIGHEST
MESH = pl.DeviceIdType.MESH

EPS = 1e-6
D_MODEL = 1024
GM_WIDTH = 512
SSM_WIDTH = 512
N_HEADS = 8
HEAD_DIM = 64
CHUNK = 128
SSM_GROUPS = 2
GROUP_W = SSM_WIDTH // SSM_GROUPS
SSM_STATE = 128
CONV_K = 4
CONV_CH = 1024
D_FF = 4096
IN_COLS = 2568
DT_PAD = 128
N_CHIPS = 4
N_DEV = 8

ADAM_LR = 0.001
ADAM_B1 = 0.9
ADAM_B2 = 0.999
ADAM_EPS = 1e-08
ADAM_WD = 0.01
ADAM_STEP = 10

VMEM_LIMIT_BYTES = 56 * 1024 * 1024
FF_TILE = 512


def _cparams(n_axes):
    return pltpu.CompilerParams(dimension_semantics=("arbitrary",) * n_axes, vmem_limit_bytes=VMEM_LIMIT_BYTES)


def _dot(a, b):
    return jnp.dot(a.astype(BF16), b.astype(BF16), preferred_element_type=F32)


def _dot_nt(a, b):
    return lax.dot_general(a.astype(BF16), b.astype(BF16), (((1,), (1,)), ((), ())), preferred_element_type=F32)


def _dot_tn(a, b):
    return lax.dot_general(a.astype(BF16), b.astype(BF16), (((0,), (0,)), ((), ())), preferred_element_type=F32)


def _sigmoid(x):
    return 1.0 / (1.0 + jnp.exp(-x))


_GELU_C = 0.7978845608028654
_GELU_A = 0.044715


def _gelu(x):
    t = jnp.tanh(_GELU_C * (x + _GELU_A * (x * x * x)))
    return 0.5 * x * (1.0 + t), t


def _gelu_grad(x, t):
    return 0.5 * (1.0 + t) + 0.5 * x * (1.0 - t * t) * (_GELU_C * (1.0 + 3.0 * _GELU_A * x * x))


def _rms_fwd(x, w):
    r = lax.rsqrt(jnp.mean(x * x, axis=-1, keepdims=True) + EPS)
    return x * r * w, r


def _rms_bwd(x, r, w, dy):
    g = dy * w
    dx = r * g - x * (r * r * r) * jnp.mean(g * x, axis=-1, keepdims=True)
    dw = jnp.sum(dy * x * r, axis=0, keepdims=True)
    return dx, dw


class _Carried:
    def __init__(self, ins, out_shapes, sems, start, finish):
        self.ins, self.out_shapes, self.sems, self.start, self.finish = list(ins), list(out_shapes), list(sems), start, finish


def _split_carried(refs, n_in, n_out, n_scratch, carried):
    n_ci, n_co, n_cs = len(carried.ins), len(carried.out_shapes), len(carried.sems)
    ins, rest = refs[:n_in], refs[n_in:]
    c_ins, rest = rest[:n_ci], rest[n_ci:]
    outs, rest = rest[:n_out], rest[n_out:]
    c_outs, rest = rest[:n_co], rest[n_co:]
    scr, c_sems = rest[:n_scratch], rest[n_scratch:]
    assert len(c_sems) == n_cs
    return tuple(ins) + tuple(outs) + tuple(scr), c_ins, c_outs, c_sems


def _rows_call(name, body, tm, row_ins, const_ins, row_outs, acc_outs=(), scratch=(), carried=None):
    n_rows = row_ins[0].shape[0]
    assert n_rows % tm == 0
    n_steps = n_rows // tm
    n_in = len(row_ins) + len(const_ins)
    n_ro = len(row_outs)
    n_acc = len(acc_outs)

    def kern(*refs):
        accs = refs[n_in + n_ro:n_in + n_ro + n_acc]

        @pl.when(pl.program_id(0) == 0)
        def _():
            for a in accs:
                a[...] = jnp.zeros_like(a)

        body(*refs)

    def whole(shape):
        nd = len(shape)
        return pl.BlockSpec(tuple(shape), lambda i: (0,) * nd)

    in_specs = [pl.BlockSpec((tm, a.shape[1]), lambda i: (i, 0)) for a in row_ins]
    in_specs += [whole(a.shape) for a in const_ins]
    out_specs = [pl.BlockSpec((tm, s.shape[1]), lambda i: (i, 0)) for s in row_outs]
    out_specs += [whole(s.shape) for s in acc_outs]
    return _call_carrying(
        kern, carried, name=name, grid=(n_steps,), in_specs=in_specs, out_specs=out_specs,
        out_shape=tuple(row_outs) + tuple(acc_outs), scratch_shapes=list(scratch), operands=list(row_ins) + list(const_ins))


def _call_carrying(body, carried, *, name, grid, in_specs, out_specs, out_shape, scratch_shapes, operands):
    n_in, n_out, n_scratch = len(in_specs), len(out_specs), len(scratch_shapes)
    kern = body
    if carried is not None:
        def kern(*refs):
            plain, c_ins, c_outs, c_sems = _split_carried(refs, n_in, n_out, n_scratch, carried)
            first, last = True, True
            for d, size in enumerate(grid):
                first = jnp.logical_and(first, pl.program_id(d) == 0)
                last = jnp.logical_and(last, pl.program_id(d) == size - 1)

            @pl.when(first)
            def _():
                carried.start(c_ins, c_outs, *c_sems)

            body(*plain)

            @pl.when(last)
            def _():
                carried.finish(c_ins, c_outs, *c_sems)

        in_specs = list(in_specs) + [_HBM] * len(carried.ins)
        out_specs = list(out_specs) + [_HBM] * len(carried.out_shapes)
        out_shape = tuple(out_shape) + tuple(carried.out_shapes)
        operands = list(operands) + carried.ins
        scratch_shapes = list(scratch_shapes) + carried.sems
    return pl.pallas_call(
        kern, name=name, grid=grid, in_specs=in_specs, out_specs=out_specs, out_shape=out_shape,
        scratch_shapes=scratch_shapes, compiler_params=_cparams(len(grid)),
    )(*operands)


def _sds(shape, dtype):
    return jax.ShapeDtypeStruct(tuple(shape), dtype)


def _matmul_tn(name, a, b, tm, tn, tk, stacked=False):
    k_dim, m_dim = a.shape
    n_dim = b.shape[1]
    assert m_dim % tm == 0 and n_dim % tn == 0 and k_dim % tk == 0

    def kern(a_ref, b_ref, o_ref):
        @pl.when(pl.program_id(2) == 0)
        def _():
            o_ref[...] = jnp.zeros_like(o_ref)

        o_ref[...] += _dot_tn(a_ref[...], b_ref[...])

    if stacked:
        assert tm == m_dim
        out_shape = _sds((n_dim // tn, m_dim, tn), F32)
        out_spec = pl.BlockSpec((None, tm, tn), lambda i, j, k: (j, i, 0))
    else:
        out_shape = _sds((m_dim, n_dim), F32)
        out_spec = pl.BlockSpec((tm, tn), lambda i, j, k: (i, j))
    return pl.pallas_call(
        kern, name=name, grid=(m_dim // tm, n_dim // tn, k_dim // tk),
        in_specs=[pl.BlockSpec((tk, tm), lambda i, j, k: (k, i)), pl.BlockSpec((tk, tn), lambda i, j, k: (k, j))],
        out_specs=out_spec, out_shape=out_shape, compiler_params=_cparams(3),
    )(a, b)


def _inproj_fwd(x, nw, w_uv, w_xbc, w_z, w_dt, tm=256):
    n_tok = x.shape[0]

    def body(x_ref, nw_ref, wuv_ref, wxbc_ref, wz_ref, wdt_ref, puv_ref, pxbc_ref, pz_ref, pdt_ref):
        h, _ = _rms_fwd(x_ref[...], nw_ref[...])
        h = h.astype(BF16)
        puv_ref[...] = jnp.dot(h, wuv_ref[...], preferred_element_type=F32)
        pxbc_ref[...] = jnp.dot(h, wxbc_ref[...], preferred_element_type=F32)
        pz_ref[...] = jnp.dot(h, wz_ref[...], preferred_element_type=F32)
        pdt_ref[...] = jnp.dot(h, wdt_ref[...], preferred_element_type=F32)

    return _rows_call(
        "inproj_fwd", body, tm, [x], [nw, w_uv, w_xbc, w_z, w_dt],
        [_sds((n_tok, 2 * GM_WIDTH), F32), _sds((n_tok, CONV_CH), F32), _sds((n_tok, SSM_WIDTH), F32),
         _sds((n_tok, DT_PAD), F32)])


def _head_lane_mask(width, head):
    lane = lax.broadcasted_iota(jnp.int32, (1, width), 1)
    return (lane // HEAD_DIM) == head


def _split_terms(x, terms):
    parts = []
    for _ in range(terms):
        p = x.astype(BF16)
        parts.append(p)
        x = x - p.astype(F32)
    return parts


def _seg_dots(vals, ind, terms=2):
    m = vals[0].shape[0]
    parts = []
    for v in vals:
        parts += _split_terms(v, terms)
    red = jnp.dot(jnp.concatenate(parts, axis=0), ind, preferred_element_type=F32)
    outs = []
    for i in range(len(vals)):
        acc = red[i * terms * m:(i * terms + 1) * m]
        for t in range(1, terms):
            acc = acc + red[(i * terms + t) * m:(i * terms + t + 1) * m]
        outs.append(acc)
    return outs


def _tri_dot(mask, x, terms=3):
    n = x.shape[1]
    red = jnp.dot(mask.astype(BF16), jnp.concatenate(_split_terms(x, terms), axis=1), preferred_element_type=F32)
    acc = red[:, :n]
    for t in range(1, terms):
        acc = acc + red[:, t * n:(t + 1) * n]
    return acc


def _gmlp_common(puv, lnw, lnb, e_bf, et_bf):
    u = puv[:, :GM_WIDTH]
    v = puv[:, GM_WIDTH:]
    gu, tu = _gelu(u)
    gv, tv = _gelu(v)
    (s1,) = _seg_dots([gv], et_bf)
    (mu,) = _seg_dots([s1 * (1.0 / HEAD_DIM)], e_bf)
    xc = gv - mu
    (s2,) = _seg_dots([xc * xc], et_bf)
    (rstd,) = _seg_dots([lax.rsqrt(s2 * (1.0 / HEAD_DIM) + EPS)], e_bf)
    xhat = xc * rstd
    vn = xhat * lnw + lnb
    return u, v, gu, tu, tv, rstd, xhat, vn


def _tril_mask():
    r = lax.broadcasted_iota(jnp.int32, (CHUNK, CHUNK), 0)
    c = lax.broadcasted_iota(jnp.int32, (CHUNK, CHUNK), 1)
    return r >= c


def _head_blocks(v):
    return jnp.concatenate([jnp.where(_head_lane_mask(GM_WIDTH, h), v, jnp.zeros_like(v)) for h in range(N_HEADS)], axis=0)


def _gmlp_fwd(p_uv, lnw, lnb, e_bf, et_bf, w_cat, bmap, carried=None):
    n_tok = p_uv.shape[0]

    def body(puv_ref, lnw_ref, lnb_ref, e_ref, et_ref, wcat_ref, bmap_ref, ya_ref, wm_scr):
        @pl.when(pl.program_id(0) == 0)
        def _():
            t = lax.broadcasted_iota(jnp.int32, (CHUNK, N_HEADS * CHUNK), 0)
            s = lax.broadcasted_iota(jnp.int32, (CHUNK, N_HEADS * CHUNK), 1) % CHUNK
            wm_scr[...] = jnp.where(t >= s, wcat_ref[...], 0.0).astype(BF16)

        _, _, gu, _, _, _, _, vn = _gmlp_common(puv_ref[...], lnw_ref[...], lnb_ref[...], e_ref[...], et_ref[...])
        mixed = jnp.dot(wm_scr[...], _head_blocks(vn.astype(BF16)), preferred_element_type=F32) + bmap_ref[...]
        ya_ref[...] = (gu * mixed).astype(BF16)

    return _rows_call("gmlp_fwd", body, CHUNK, [p_uv], [lnw, lnb, e_bf, et_bf, w_cat, bmap],
                      [_sds((n_tok, GM_WIDTH), BF16)], scratch=[pltpu.VMEM((CHUNK, N_HEADS * CHUNK), BF16)],
                      carried=carried)


def _ssd_pre(xr, prev, cw_ref, cb, pdt, dtb, alog, emap):
    rowi = lax.broadcasted_iota(jnp.int32, (CHUNK, 1), 0)

    def down(s):
        return jnp.where(rowi < s, pltpu.roll(prev, s, 0), pltpu.roll(xr, s, 0))

    shifted = [down(3), down(2), down(1), xr]
    xc = cb
    for k in range(CONV_K):
        xc = xc + cw_ref[k] * shifted[k]
    sg = _sigmoid(xc)
    xa = xc * sg
    pre = pdt + dtb
    dt = jnp.maximum(pre, 0.0) + jnp.log(1.0 + jnp.exp(-jnp.abs(pre)))
    a_neg = -jnp.exp(alog)
    a_cs = _tri_dot(_tril_mask(), dt * a_neg)
    acs_map, dt_map = _seg_dots([a_cs, dt], emap, terms=3)
    return dict(shifted=shifted, xc=xc, sg=sg, xa=xa, pre=pre, dt=dt, a_neg=a_neg, a_cs=a_cs,
                acs_map=acs_map, dt_map=dt_map, rowi=rowi)


def _ssd_maps(p):
    last = p["rowi"] == CHUNK - 1
    aq_map = jnp.sum(jnp.where(last, p["acs_map"], 0.0), axis=0, keepdims=True)
    e_exp = jnp.exp(p["acs_map"])
    dte = jnp.exp(aq_map - p["acs_map"])
    cd = jnp.exp(aq_map)
    return last, e_exp, dte, cd


def _head_decay(a_cs, a_cs_t, head, tri):
    lane = lax.broadcasted_iota(jnp.int32, (1, DT_PAD), 1)
    sub = lax.broadcasted_iota(jnp.int32, (DT_PAD, 1), 0)
    col = jnp.sum(jnp.where(lane == head, a_cs, 0.0), axis=1, keepdims=True)
    row = jnp.sum(jnp.where(sub == head, a_cs_t, 0.0), axis=0, keepdims=True)
    return jnp.exp(jnp.where(tri, col - row, -1e30))


def _gate_fwd(y, z, nw):
    sz = _sigmoid(z)
    zg = z * sz
    yg = y * zg
    outs, rs = [], []
    for g in range(SSM_GROUPS):
        gs = slice(g * GROUP_W, (g + 1) * GROUP_W)
        o, r = _rms_fwd(yg[:, gs], nw[:, gs])
        outs.append(o)
        rs.append(r)
    return sz, zg, yg, outs, rs


def _ssd_const_specs():
    def whole(shape):
        nd = len(shape)
        return pl.BlockSpec(tuple(shape), lambda b, c: (0,) * nd)
    return [whole((CONV_K, 1, CONV_CH)), whole((1, CONV_CH)), whole((1, DT_PAD)), whole((1, DT_PAD)),
            whole((1, SSM_WIDTH)), whole((1, SSM_WIDTH)), whole((DT_PAD, SSM_WIDTH)), whole((SSM_WIDTH, DT_PAD))]


def _ssd_fwd(p_xbc, p_z, p_dt, conv_w, conv_b, dt_bias, a_log, dskip_map, norm_w, e_bf, et_bf, n_seq, carried=None):
    n_tok = p_xbc.shape[0]
    nc = n_tok // n_seq // CHUNK

    def body(xr_ref, z_ref, pdt_ref, cw_ref, cb_ref, dtb_ref, alog_ref, dsk_ref, nw_ref, e_ref, et_ref,
             yb_ref, yssd_ref, sprev_ref, prev_scr, s_scr):
        @pl.when(pl.program_id(1) == 0)
        def _():
            prev_scr[...] = jnp.zeros_like(prev_scr)
            s_scr[...] = jnp.zeros_like(s_scr)

        xr = xr_ref[...]
        p = _ssd_pre(xr, prev_scr[...], cw_ref, cb_ref[...], pdt_ref[...], dtb_ref[...], alog_ref[...], e_ref[...])
        _, e_exp, dte, cd = _ssd_maps(p)
        xs = p["xa"][:, :SSM_WIDTH]
        xd = xs * p["dt_map"]
        a_cs_t = p["a_cs"].T
        tri = _tril_mask()
        s_old = s_scr[...]
        sprev_ref[...] = s_old
        for g in range(SSM_GROUPS):
            gs = slice(g * GROUP_W, (g + 1) * GROUP_W)
            bm = p["xa"][:, SSM_WIDTH + g * SSM_STATE: SSM_WIDTH + (g + 1) * SSM_STATE].astype(BF16)
            cm = p["xa"][:, SSM_WIDTH + (SSM_GROUPS + g) * SSM_STATE: SSM_WIDTH + (SSM_GROUPS + g + 1) * SSM_STATE].astype(BF16)
            cb_mat = _dot_nt(cm, bm)
            xdg = xd[:, gs].astype(BF16)
            y_g = _dot(cm, s_old[:, gs]) * e_exp[:, gs] + dsk_ref[:, gs] * xs[:, gs]
            for r in range(SSM_GROUPS * 2):
                dm = _head_decay(p["a_cs"], a_cs_t, g * 4 + r, tri)
                full = jnp.dot((cb_mat * dm).astype(BF16), xdg, preferred_element_type=F32)
                y_g = y_g + jnp.where(_head_lane_mask(GROUP_W, r), full, 0.0)
            yssd_ref[:, gs] = y_g
            s_scr[:, gs] = cd[:, gs] * s_old[:, gs] + _dot_tn(bm, xd[:, gs] * dte[:, gs])
        _, _, _, outs, _ = _gate_fwd(yssd_ref[...], z_ref[...], nw_ref[...])
        for g in range(SSM_GROUPS):
            yb_ref[:, g * GROUP_W:(g + 1) * GROUP_W] = outs[g].astype(BF16)
        prev_scr[...] = xr

    def rows(width):
        return pl.BlockSpec((CHUNK, width), lambda b, c: (b * nc + c, 0))

    return _call_carrying(
        body, carried, name="ssd_fwd", grid=(n_seq, nc),
        in_specs=[rows(CONV_CH), rows(SSM_WIDTH), rows(DT_PAD)] + _ssd_const_specs(),
        out_specs=[rows(SSM_WIDTH), rows(SSM_WIDTH), rows(SSM_WIDTH)],
        out_shape=(_sds((n_tok, SSM_WIDTH), BF16), _sds((n_tok, SSM_WIDTH), F32), _sds((n_tok, SSM_WIDTH), F32)),
        scratch_shapes=[pltpu.VMEM((CHUNK, CONV_CH), F32), pltpu.VMEM((SSM_STATE, SSM_WIDTH), F32)],
        operands=[p_xbc, p_z, p_dt, conv_w, conv_b, dt_bias, a_log, dskip_map, norm_w, e_bf, et_bf])


def _outproj_fwd(ya, yb, x, w_out, nw_post, nw_pre2, tm=256):
    n_tok = x.shape[0]

    def body(ya_ref, yb_ref, x_ref, wo_ref, nwa_ref, nwb_ref, o_ref, x1_ref, h2_ref):
        o = jnp.dot(ya_ref[...], wo_ref[:GM_WIDTH, :], preferred_element_type=F32)
        o = o + jnp.dot(yb_ref[...], wo_ref[GM_WIDTH:, :], preferred_element_type=F32)
        on, _ = _rms_fwd(o, nwa_ref[...])
        x1 = x_ref[...] + on
        h2, _ = _rms_fwd(x1, nwb_ref[...])
        o_ref[...] = o
        x1_ref[...] = x1
        h2_ref[...] = h2.astype(BF16)

    return _rows_call("outproj_fwd", body, tm, [ya, yb, x], [w_out, nw_post, nw_pre2],
                      [_sds((n_tok, D_MODEL), F32), _sds((n_tok, D_MODEL), F32), _sds((n_tok, D_MODEL), BF16)])


def _up_cols(wup_ref, j):
    per = (D_FF // N_CHIPS) // FF_TILE
    return wup_ref[j // per, :, (j % per) * FF_TILE:(j % per + 1) * FF_TILE]


def _mlp_fwd(h2, x1, tgt, w_up, w_down, nw, tm=256):
    n_tok = x1.shape[0]

    def body(h2_ref, x1_ref, tgt_ref, wup_ref, wdn_ref, nw_ref, f_ref, dd_ref, dy_ref, loss_ref, dnw_ref):
        h2v = h2_ref[...]
        acc = jnp.zeros((tm, D_MODEL), F32)
        for j in range(D_FF // FF_TILE):
            cs = slice(j * FF_TILE, (j + 1) * FF_TILE)
            u = jnp.dot(h2v, _up_cols(wup_ref, j), preferred_element_type=F32)
            f = jnp.square(jnp.maximum(u, 0.0)).astype(BF16)
            f_ref[:, cs] = f
            acc = acc + jnp.dot(f, wdn_ref[cs, :], preferred_element_type=F32)
        dn, r = _rms_fwd(acc, nw_ref[...])
        e = x1_ref[...] + dn - tgt_ref[...]
        loss_ref[...] += jnp.full(loss_ref.shape, (0.5 / D_MODEL) * jnp.sum(e * e), F32)
        dy = e * (1.0 / D_MODEL)
        dd, dnw = _rms_bwd(acc, r, nw_ref[...], dy)
        dy_ref[...] = dy
        dd_ref[...] = dd.astype(BF16)
        dnw_ref[...] += dnw

    return _rows_call(
        "mlp_fwd", body, tm, [h2, x1, tgt], [w_up, w_down, nw],
        [_sds((n_tok, D_FF), BF16), _sds((n_tok, D_MODEL), BF16), _sds((n_tok, D_MODEL), F32)],
        [_sds((8, 128), F32), _sds((1, D_MODEL), F32)])


def _mlp_bwd(dd, f, x1, dy, w_down, w_up, nw, tm=256):
    n_tok = x1.shape[0]

    def body(dd_ref, f_ref, x1_ref, dy_ref, wdn_ref, wup_ref, nw_ref, dup_ref, dx1_ref, dnw_ref):
        ddv = dd_ref[...]
        acc = jnp.zeros((tm, D_MODEL), F32)
        for j in range(D_FF // FF_TILE):
            cs = slice(j * FF_TILE, (j + 1) * FF_TILE)
            df = _dot_nt(ddv, wdn_ref[cs, :])
            du = (df * (2.0 * jnp.sqrt(f_ref[:, cs].astype(F32)))).astype(BF16)
            dup_ref[:, cs] = du
            acc = acc + _dot_nt(du, _up_cols(wup_ref, j))
        x1v = x1_ref[...]
        _, r = _rms_fwd(x1v, nw_ref[...])
        dx, dnw = _rms_bwd(x1v, r, nw_ref[...], acc)
        dx1_ref[...] = dy_ref[...] + dx
        dnw_ref[...] += dnw

    return _rows_call("mlp_bwd", body, tm, [dd, f, x1, dy], [w_down, w_up, nw],
                      [_sds((n_tok, D_FF), BF16), _sds((n_tok, D_MODEL), F32)], [_sds((1, D_MODEL), F32)])


def _outproj_bwd(dx1, o, w_out, nw, tm=256):
    n_tok = dx1.shape[0]

    def body(dx1_ref, o_ref, wo_ref, nw_ref, do_ref, dya_ref, dyb_ref, dnw_ref):
        ov = o_ref[...]
        _, r = _rms_fwd(ov, nw_ref[...])
        do, dnw = _rms_bwd(ov, r, nw_ref[...], dx1_ref[...])
        dob = do.astype(BF16)
        do_ref[...] = dob
        dya_ref[...] = _dot_nt(dob, wo_ref[:GM_WIDTH, :])
        dyb_ref[...] = _dot_nt(dob, wo_ref[GM_WIDTH:, :])
        dnw_ref[...] += dnw

    return _rows_call("outproj_bwd", body, tm, [dx1, o], [w_out, nw],
                      [_sds((n_tok, D_MODEL), BF16), _sds((n_tok, GM_WIDTH), F32), _sds((n_tok, SSM_WIDTH), F32)],
                      [_sds((1, D_MODEL), F32)])


def _gmlp_bwd(p_uv, dya, lnw, lnb, e_bf, et_bf, w_cat, w_stack, bmap, carried=None):
    n_tok = p_uv.shape[0]

    def body(puv_ref, dya_ref, lnw_ref, lnb_ref, e_ref, et_ref, wcat_ref, wstack_ref, bmap_ref,
             dpuv_ref, dws_ref, dbs_ref, dlnw_ref, dlnb_ref, wm_scr, wsm_scr):
        t_cat = lax.broadcasted_iota(jnp.int32, (CHUNK, N_HEADS * CHUNK), 0)
        s_cat = lax.broadcasted_iota(jnp.int32, (CHUNK, N_HEADS * CHUNK), 1) % CHUNK
        t_stk = lax.broadcasted_iota(jnp.int32, (N_HEADS * CHUNK, CHUNK), 0) % CHUNK
        s_stk = lax.broadcasted_iota(jnp.int32, (N_HEADS * CHUNK, CHUNK), 1)

        @pl.when(pl.program_id(0) == 0)
        def _():
            wm_scr[...] = jnp.where(t_cat >= s_cat, wcat_ref[...], 0.0).astype(BF16)
            wsm_scr[...] = jnp.where(t_stk >= s_stk, wstack_ref[...], 0.0).astype(BF16)

        lnw_v = lnw_ref[...]
        e_v, et_v = e_ref[...], et_ref[...]
        u, v, gu, tu, tv, rstd, xhat, vn = _gmlp_common(puv_ref[...], lnw_v, lnb_ref[...], e_v, et_v)
        vnb = vn.astype(BF16)
        mixed = jnp.dot(wm_scr[...], _head_blocks(vnb), preferred_element_type=F32) + bmap_ref[...]
        dy = dya_ref[...]
        du = dy * mixed * _gelu_grad(u, tu)
        dmixed = dy * gu
        (dbs,) = _seg_dots([dmixed], et_v)
        dbs_ref[...] += dbs
        dblocks = _head_blocks(dmixed.astype(BF16))
        dvn = lax.dot_general(wsm_scr[...], dblocks, (((0,), (0,)), ((), ())), preferred_element_type=F32)
        dws = lax.dot_general(dblocks, vnb, (((1,), (1,)), ((), ())), preferred_element_type=F32)
        dws_ref[...] += jnp.where(t_stk >= s_stk, dws, 0.0)
        dlnw_ref[...] += jnp.sum(dvn * xhat, axis=0, keepdims=True)
        dlnb_ref[...] += jnp.sum(dvn, axis=0, keepdims=True)
        dxh = dvn * lnw_v
        m1, m2 = _seg_dots([dxh, dxh * xhat], et_v)
        m1, m2 = _seg_dots([m1 * (1.0 / HEAD_DIM), m2 * (1.0 / HEAD_DIM)], e_v)
        dgv = rstd * (dxh - m1 - xhat * m2)
        dv = dgv * _gelu_grad(v, tv)
        dpuv_ref[:, :GM_WIDTH] = du.astype(BF16)
        dpuv_ref[:, GM_WIDTH:] = dv.astype(BF16)

    return _rows_call(
        "gmlp_bwd", body, CHUNK, [p_uv, dya], [lnw, lnb, e_bf, et_bf, w_cat, w_stack, bmap],
        [_sds((n_tok, 2 * GM_WIDTH), BF16)],
        [_sds((N_HEADS * CHUNK, CHUNK), F32), _sds((CHUNK, DT_PAD), F32), _sds((1, GM_WIDTH), F32),
         _sds((1, GM_WIDTH), F32)],
        scratch=[pltpu.VMEM((CHUNK, N_HEADS * CHUNK), BF16), pltpu.VMEM((N_HEADS * CHUNK, CHUNK), BF16)],
        carried=carried)


def _ssd_bwd(p_xbc, p_z, p_dt, yssd, sprev, dyb, conv_w, conv_b, dt_bias, a_log, dskip_map, norm_w, e_bf, et_bf, n_seq,
             carried=None):
    n_tok = p_xbc.shape[0]
    nc = n_tok // n_seq // CHUNK

    def body(xr_ref, xprev_ref, z_ref, pdt_ref, yssd_ref, sprev_ref, dyb_ref,
             cw_ref, cb_ref, dtb_ref, alog_ref, dsk_ref, nw_ref, e_ref, et_ref,
             dpxbc_ref, dpz_ref, dpdt_ref, dcw_ref, dcb_ref, ddtb_ref, dalog_ref, ddsk_ref, dnw_ref,
             ds_scr, nxt_scr, dxa_scr):
        step = pl.program_id(1)
        first = jnp.logical_and(pl.program_id(0) == 0, step == 0)

        @pl.when(first)
        def _():
            for a in (dcw_ref, dcb_ref, ddtb_ref, dalog_ref, ddsk_ref, dnw_ref):
                a[...] = jnp.zeros_like(a)

        @pl.when(step == 0)
        def _():
            ds_scr[...] = jnp.zeros_like(ds_scr)
            nxt_scr[...] = jnp.zeros_like(nxt_scr)

        chunk = nc - 1 - step
        xr = xr_ref[...]
        prev = jnp.where(chunk == 0, 0.0, xprev_ref[...])
        et_v = et_ref[...]
        p = _ssd_pre(xr, prev, cw_ref, cb_ref[...], pdt_ref[...], dtb_ref[...], alog_ref[...], e_ref[...])
        last, e_exp, dte, cd = _ssd_maps(p)
        rowi = p["rowi"]
        xs = p["xa"][:, :SSM_WIDTH]
        xd = xs * p["dt_map"]
        a_cs_t = p["a_cs"].T
        tri = _tril_mask()
        dsk = dsk_ref[...]
        nw_v = nw_ref[...]

        yv = yssd_ref[...]
        zv = z_ref[...]
        sz, zg, yg, _, rs = _gate_fwd(yv, zv, nw_v)
        dout = dyb_ref[...]
        for g in range(SSM_GROUPS):
            gs = slice(g * GROUP_W, (g + 1) * GROUP_W)
            dyg_g, dnw_g = _rms_bwd(yg[:, gs], rs[g], nw_v[:, gs], dout[:, gs])
            dnw_ref[:, gs] += dnw_g
            dxa_scr[:, gs] = dyg_g
        dyg = dxa_scr[:, :SSM_WIDTH]
        d_y = dyg * zg
        dpz_ref[...] = (dyg * yv * (sz + zv * sz * (1.0 - sz))).astype(BF16)

        s_prev = sprev_ref[...]
        ds_next = ds_scr[...]
        lane_dt = lax.broadcasted_iota(jnp.int32, (1, DT_PAD), 1)
        da_cols = jnp.zeros((CHUNK, DT_PAD), F32)
        for g in range(SSM_GROUPS):
            gs = slice(g * GROUP_W, (g + 1) * GROUP_W)
            b_off = SSM_WIDTH + g * SSM_STATE
            c_off = SSM_WIDTH + (SSM_GROUPS + g) * SSM_STATE
            bm = p["xa"][:, b_off:b_off + SSM_STATE].astype(BF16)
            cm = p["xa"][:, c_off:c_off + SSM_STATE].astype(BF16)
            cb_mat = _dot_nt(cm, bm)
            d_yg = d_y[:, gs]
            d_ygb = d_yg.astype(BF16)
            xdg = xd[:, gs]
            xdgb = xdg.astype(BF16)
            ds_g = ds_next[:, gs]
            sp_g = s_prev[:, gs]
            bds = _dot(bm, ds_g)
            dcs = d_yg * e_exp[:, gs]
            d_c = _dot_nt(dcs, sp_g)
            ds_scr[:, gs] = cd[:, gs] * ds_g + _dot_tn(cm, dcs)
            d_b = _dot_nt(xdg * dte[:, gs], ds_g)
            dxd_g = bds * dte[:, gs]
            sum_dcb = jnp.zeros((CHUNK, CHUNK), F32)
            for r in range(SSM_GROUPS * 2):
                head = g * 4 + r
                mask = _head_lane_mask(GROUP_W, r)
                dm = _head_decay(p["a_cs"], a_cs_t, head, tri)
                m_mat = cb_mat * dm
                g_mat = _dot_nt(jnp.where(mask, d_yg, 0.0), xdgb)
                w_mat = g_mat * m_mat
                sum_dcb = sum_dcb + g_mat * dm
                dxd_g = dxd_g + jnp.where(mask, _dot_tn(m_mat, d_ygb), 0.0)
                da_h = jnp.sum(w_mat - w_mat.T, axis=1, keepdims=True)
                da_cols = da_cols + jnp.where(lane_dt == head, da_h, 0.0)
            d_c = d_c + _dot(sum_dcb, bm)
            d_b = d_b + _dot_tn(sum_dcb, cm)
            dxa_scr[:, b_off:b_off + SSM_STATE] = d_b
            dxa_scr[:, c_off:c_off + SSM_STATE] = d_c
            y_off_g = _dot(cm, sp_g) * e_exp[:, gs]
            t3 = bds * xdg * dte[:, gs]
            tail = jnp.sum(t3, axis=0, keepdims=True) + jnp.sum(ds_g * sp_g, axis=0, keepdims=True) * cd[:, gs]
            pre_g = d_yg * y_off_g - t3 + jnp.where(last, tail, 0.0)
            s_pre, ddt_g, s_dsk = _seg_dots([pre_g, dxd_g * xs[:, gs], d_yg * xs[:, gs]], et_v[gs, :])
            da_cols = da_cols + s_pre
            ddsk_ref[...] += jnp.sum(s_dsk, axis=0, keepdims=True)
            dxa_scr[:, gs] = dxd_g * p["dt_map"][:, gs] + dsk[:, gs] * d_yg
            if g == 0:
                ddt = ddt_g
            else:
                ddt = ddt + ddt_g
        r_i = lax.broadcasted_iota(jnp.int32, (CHUNK, CHUNK), 0)
        c_i = lax.broadcasted_iota(jnp.int32, (CHUNK, CHUNK), 1)
        ddta = _tri_dot(r_i <= c_i, da_cols, terms=2)
        ddt = ddt + ddta * p["a_neg"]
        dalog_ref[...] += jnp.sum(ddta * p["dt"], axis=0, keepdims=True) * p["a_neg"]
        draw = ddt * _sigmoid(p["pre"])
        ddtb_ref[...] += jnp.sum(draw, axis=0, keepdims=True)
        dpdt_ref[...] = draw.astype(BF16)

        xc = p["xc"]
        sg = p["sg"]
        dxc = dxa_scr[...] * (sg + xc * sg * (1.0 - sg))
        dcb_ref[...] += jnp.sum(dxc, axis=0, keepdims=True)
        for k in range(CONV_K):
            dcw_ref[k] += jnp.sum(dxc * p["shifted"][k], axis=0, keepdims=True)
        nxt = nxt_scr[...]

        def up(s):
            return jnp.where(rowi >= CHUNK - s, pltpu.roll(nxt, CHUNK - s, 0), pltpu.roll(dxc, CHUNK - s, 0))

        dxr = cw_ref[3] * dxc + cw_ref[2] * up(1) + cw_ref[1] * up(2) + cw_ref[0] * up(3)
        dpxbc_ref[...] = dxr.astype(BF16)
        nxt_scr[...] = dxc

    def rows(width):
        return pl.BlockSpec((CHUNK, width), lambda b, s: (b * nc + nc - 1 - s, 0))

    prev_rows = pl.BlockSpec((CHUNK, CONV_CH), lambda b, s: (b * nc + jnp.maximum(nc - 2 - s, 0), 0))

    def whole(shape):
        nd = len(shape)
        return pl.BlockSpec(tuple(shape), lambda b, s: (0,) * nd)

    acc_shapes = [(CONV_K, 1, CONV_CH), (1, CONV_CH), (1, DT_PAD), (1, DT_PAD), (1, DT_PAD), (1, SSM_WIDTH)]
    return _call_carrying(
        body, carried, name="ssd_bwd", grid=(n_seq, nc),
        in_specs=[rows(CONV_CH), prev_rows, rows(SSM_WIDTH), rows(DT_PAD), rows(SSM_WIDTH), rows(SSM_WIDTH),
                  rows(SSM_WIDTH)] + _ssd_const_specs(),
        out_specs=[rows(CONV_CH), rows(SSM_WIDTH), rows(DT_PAD)] + [whole(s) for s in acc_shapes],
        out_shape=tuple([_sds((n_tok, CONV_CH), BF16), _sds((n_tok, SSM_WIDTH), BF16), _sds((n_tok, DT_PAD), BF16)]
                        + [_sds(s, F32) for s in acc_shapes]),
        scratch_shapes=[pltpu.VMEM((SSM_STATE, SSM_WIDTH), F32), pltpu.VMEM((CHUNK, CONV_CH), F32),
                        pltpu.VMEM((CHUNK, CONV_CH), F32)],
        operands=[p_xbc, p_xbc, p_z, p_dt, yssd, sprev, dyb, conv_w, conv_b, dt_bias, a_log, dskip_map, norm_w, e_bf,
                  et_bf])


def _inproj_bwd(dp_uv, dp_xbc, dp_z, dp_dt, x, dx1, w_uv, w_xbc, w_z, w_dt, nw, tm=256):
    n_tok = x.shape[0]

    def body(duv_ref, dxbc_ref, dz_ref, ddt_ref, x_ref, dx1_ref, wuv_ref, wxbc_ref, wz_ref, wdt_ref, nw_ref,
             gx_ref, h_ref, dnw_ref):
        dh = _dot_nt(duv_ref[...], wuv_ref[...]) + _dot_nt(dxbc_ref[...], wxbc_ref[...])
        dh = dh + _dot_nt(dz_ref[...], wz_ref[...]) + _dot_nt(ddt_ref[...], wdt_ref[...])
        xv = x_ref[...]
        h, r = _rms_fwd(xv, nw_ref[...])
        dx, dnw = _rms_bwd(xv, r, nw_ref[...], dh)
        gx_ref[...] = dx1_ref[...] + dx
        h_ref[...] = h.astype(BF16)
        dnw_ref[...] += dnw

    return _rows_call("inproj_bwd", body, tm, [dp_uv, dp_xbc, dp_z, dp_dt, x, dx1], [w_uv, w_xbc, w_z, w_dt, nw],
                      [_sds((n_tok, D_MODEL), F32), _sds((n_tok, D_MODEL), BF16)], [_sds((1, D_MODEL), F32)])


def _const_maps():
    lane = jnp.arange(SSM_WIDTH) // HEAD_DIM
    e_bf = (jnp.arange(DT_PAD)[:, None] == lane[None, :]).astype(BF16)
    return e_bf, e_bf.T


def _pad_lanes(v, width):
    return jnp.pad(v, ((0, 0), (0, width - v.shape[1])))


SHARD_COLS = IN_COLS // N_CHIPS
_UV_END = 2 * GM_WIDTH
_Z_END = _UV_END + SSM_WIDTH
_XBC_END = _Z_END + CONV_CH


def _cols_from_shards(w4, lo, hi):
    pieces = []
    for j in range(N_CHIPS):
        a, b = max(lo, j * SHARD_COLS), min(hi, (j + 1) * SHARD_COLS)
        if a < b:
            pieces.append(w4[j][:, a - j * SHARD_COLS:b - j * SHARD_COLS])
    return pieces[0] if len(pieces) == 1 else jnp.concatenate(pieces, axis=1)


def _shards_from_cols(blocks):
    shards = []
    for j in range(N_CHIPS):
        pieces = []
        for arr, lo, hi in blocks:
            a, b = max(lo, j * SHARD_COLS), min(hi, (j + 1) * SHARD_COLS)
            if a < b:
                pieces.append(arr[:, a - lo:b - lo])
        shards.append(pieces[0] if len(pieces) == 1 else jnp.concatenate(pieces, axis=1))
    return jnp.stack(shards)


def _forward_backward(x, tgt, w_in4, conv_w, small, out_shard, up_shard, down_shard, core):
    n_seq, seq_len, _ = x.shape
    n_tok = n_seq * seq_len
    x2 = x.reshape(n_tok, D_MODEL)
    tgt2 = tgt.reshape(n_tok, D_MODEL)
    e_bf, et_bf = _const_maps()

    w_uv = _cols_from_shards(w_in4, 0, _UV_END)
    w_z = _cols_from_shards(w_in4, _UV_END, _Z_END)
    w_xbc = _cols_from_shards(w_in4, _Z_END, _XBC_END)
    w_dt = _pad_lanes(_cols_from_shards(w_in4, _XBC_END, IN_COLS), DT_PAD)

    nw_pre = small["norm_mix_pre"]
    lnw = small["gm_ln_w"].reshape(1, GM_WIDTH)
    lnb = small["gm_ln_b"].reshape(1, GM_WIDTH)
    w_stack = small["gm_w_s"].reshape(N_HEADS * CHUNK, CHUNK)
    w_cat = jnp.transpose(small["gm_w_s"], (1, 0, 2)).reshape(CHUNK, N_HEADS * CHUNK)
    bmap = jnp.repeat(small["gm_b_s"].T, HEAD_DIM, axis=1)
    cw3 = conv_w.reshape(CONV_K, 1, CONV_CH)
    conv_b = small["conv_b"]
    dt_bias = _pad_lanes(small["dt_bias"], DT_PAD)
    a_log = _pad_lanes(small["a_log"], DT_PAD)
    dskip_map = jnp.repeat(small["d_skip"], HEAD_DIM, axis=1)
    ssm_nw = small["ssm_norm_w"]

    p_uv, p_xbc, p_z, p_dt = _inproj_fwd(x2, nw_pre, w_uv, w_xbc, w_z, w_dt)
    ya, w_up4 = _gmlp_fwd(p_uv, lnw, lnb, e_bf, et_bf, w_cat, bmap, carried=_allgather_exchange([up_shard]))
    ssd_consts = (cw3, conv_b, dt_bias, a_log, dskip_map, ssm_nw, e_bf, et_bf)
    yb, yssd, sprev, w_down4, w_out4 = _ssd_fwd(p_xbc, p_z, p_dt, *ssd_consts, n_seq,
                                                carried=_allgather_exchange([down_shard, out_shard]))
    w_down_b = w_down4.reshape(D_FF, D_MODEL)
    w_out_b = w_out4.reshape(D_MODEL, D_MODEL)
    o, x1, h2 = _outproj_fwd(ya, yb, x2, w_out_b, small["norm_mix_post"], small["norm_ffn_pre"])
    f, dd, dy, loss_acc, d_nffn_post = _mlp_fwd(h2, x1, tgt2, w_up4, w_down_b, small["norm_ffn_post"])

    dup, dx1, d_nffn_pre = _mlp_bwd(dd, f, x1, dy, w_down_b, w_up4, small["norm_ffn_pre"])
    g_up = _matmul_tn("dw_up", h2, dup, D_MODEL, D_MODEL, 512, stacked=True)
    g_down = _matmul_tn("dw_down", f, dd, 1024, D_MODEL, 512).reshape(N_CHIPS, D_FF // N_CHIPS, D_MODEL)
    got_up, got_down = _pair_exchange("grad_pair_exchange_mlp", [g_up, g_down])
    h_up = _pair_sum(core, g_up, got_up, 256)
    h_down = _pair_sum(core, g_down, got_down, 256)
    do, dya, dyb, d_nmix_post = _outproj_bwd(dx1, o, w_out_b, small["norm_mix_post"])
    g_out_a = _matmul_tn("dw_out_a", ya, do, GM_WIDTH, D_MODEL, 512)
    g_out_b = _matmul_tn("dw_out_b", yb, do, SSM_WIDTH, D_MODEL, 512)
    dp_uv, d_ws, d_bs_t, d_lnw, d_lnb, slab_up = _gmlp_bwd(
        p_uv, dya, lnw, lnb, e_bf, et_bf, w_cat, w_stack, bmap, carried=_chip_exchange([h_up]))
    (dp_xbc, dp_z, dp_dt, d_cw, d_cb, d_dtb, d_alog, d_dsk, d_ssm_nw, slab_down) = _ssd_bwd(
        p_xbc, p_z, p_dt, yssd, sprev, dyb, *ssd_consts, n_seq, carried=_chip_exchange([h_down]))
    gx, h, d_nmix_pre = _inproj_bwd(dp_uv, dp_xbc, dp_z, dp_dt, x2, dx1, w_uv, w_xbc, w_z, w_dt, nw_pre)
    g_uv = _matmul_tn("dw_in_uv", h, dp_uv, D_MODEL, 2 * GM_WIDTH, 512)
    g_xbc = _matmul_tn("dw_in_xbc", h, dp_xbc, D_MODEL, CONV_CH, 512)
    g_z = _matmul_tn("dw_in_z", h, dp_z, D_MODEL, SSM_WIDTH, 512)
    g_dt = _matmul_tn("dw_in_dt", h, dp_dt, D_MODEL, DT_PAD, 512)

    g_in = _shards_from_cols([(g_uv, 0, _UV_END), (g_z, _UV_END, _Z_END), (g_xbc, _Z_END, _XBC_END),
                              (g_dt, _XBC_END, IN_COLS)])
    g_out = jnp.concatenate([g_out_a, g_out_b], axis=0).reshape(N_CHIPS, D_MODEL // N_CHIPS, D_MODEL)

    got_in, got_out = _pair_exchange("grad_pair_exchange_mix", [g_in, g_out])
    h_in = _pair_sum(core, g_in, got_in, 256)
    h_out = _pair_sum(core, g_out, got_out, 128)
    slab_in, slab_out = _run_exchange("grad_chip_exchange", _chip_exchange([h_in, h_out]))
    reds = [_chip_sum(core, s, tm) for s, tm in ((slab_in, 256), (slab_out, 128), (slab_up, 256), (slab_down, 256))]
    big_grads = dict(zip(("w_in", "w_out", "w_up", "w_down"), _pair_gather(reds)))

    small_grads = {
        "norm_mix_pre": d_nmix_pre, "gm_ln_w": d_lnw.reshape(N_HEADS, HEAD_DIM), "gm_ln_b": d_lnb.reshape(N_HEADS, HEAD_DIM),
        "gm_w_s": d_ws.reshape(N_HEADS, CHUNK, CHUNK), "gm_b_s": d_bs_t[:, :N_HEADS].T, "conv_w": d_cw.reshape(CONV_K, CONV_CH), "conv_b": d_cb,
        "dt_bias": d_dtb[:, :N_HEADS], "a_log": d_alog[:, :N_HEADS], "d_skip": d_dsk[:, :N_HEADS],
        "ssm_norm_w": d_ssm_nw, "norm_mix_post": d_nmix_post, "norm_ffn_pre": d_nffn_pre,
        "norm_ffn_post": d_nffn_post,
    }
    return loss_acc[0, 0], gx.reshape(x.shape), big_grads, small_grads


_HBM = pl.BlockSpec(memory_space=pltpu.HBM)


D2D_CHUNKS = 8
ROW_ALIGN = 16


def _row_chunks(rows, n_chunks):
    size = min(max(rows // n_chunks, ROW_ALIGN), rows)
    assert rows % size == 0
    return [(start, size) for start in range(0, rows, size)]


def _position():
    x, y, c = lax.axis_index("x"), lax.axis_index("y"), lax.axis_index("c")
    chips = [(1 - x, y), (x, 1 - y), (1 - x, 1 - y)]
    return x, y, c, chips


def _allgather_exchange(arrs):
    n = len(arrs)

    def copies(ins, outs, send_sems, recv_sems, local_sems):
        x, y, c, chips = _position()
        me = 2 * x + y
        sibling = (x, y, 1 - c)

        def copy(a, k, src, dst, to):
            return pltpu.make_async_remote_copy(src_ref=src, dst_ref=dst, send_sem=send_sems.at[a, k],
                                                recv_sem=recv_sems.at[a, k], device_id=to, device_id_type=MESH)

        def half_rows(a, pc):
            half = ins[a].shape[0] // 2
            return pl.ds(pc * half, half)

        local = [pltpu.make_async_copy(ins[a], outs[a].at[me], local_sems.at[a]) for a in range(n)]
        ici_out = [[copy(a, k, ins[a].at[half_rows(a, c)], outs[a].at[me, half_rows(a, c)], (px, py, c))
                    for k, (px, py) in enumerate(chips)] for a in range(n)]
        return c, chips, sibling, copy, half_rows, local, ici_out

    def start(ins, outs, send_sems, recv_sems, local_sems):
        _, _, _, _, _, local, ici_out = copies(ins, outs, send_sems, recv_sems, local_sems)
        for cp in local:
            cp.start()
        for a in range(n):
            for cp in ici_out[a]:
                cp.start()

    def finish(ins, outs, send_sems, recv_sems, local_sems):
        c, chips, sibling, copy, half_rows, local, ici_out = copies(ins, outs, send_sems, recv_sems, local_sems)
        passed = []
        for a in range(n):
            half = ins[a].shape[0] // 2
            for k, (px, py) in enumerate(chips):
                blk = outs[a].at[2 * px + py, half_rows(a, c)]
                copy(a, k, blk, blk, (px, py, c)).wait_recv()
                for first, size in _row_chunks(half, D2D_CHUNKS):
                    piece = outs[a].at[2 * px + py, pl.ds(c * half + first, size)]
                    copy(a, 3 + k, piece, piece, sibling).start()
                passed.append(copy(a, 3 + k, blk, blk, sibling))
        for a in range(n):
            for k, (px, py) in enumerate(chips):
                blk = outs[a].at[2 * px + py, half_rows(a, 1 - c)]
                copy(a, 3 + k, blk, blk, sibling).wait_recv()
        for a in range(n):
            for cp in ici_out[a]:
                cp.wait_send()
        for cp in passed:
            cp.wait_send()
        for cp in local:
            cp.wait()

    return _Carried(arrs, [_sds((N_CHIPS,) + a.shape, a.dtype) for a in arrs],
                    [pltpu.SemaphoreType.DMA((n, 6)), pltpu.SemaphoreType.DMA((n, 6)), pltpu.SemaphoreType.DMA((n,))],
                    start, finish)


def _run_exchange(name, exchange):
    n_in, n_out = len(exchange.ins), len(exchange.out_shapes)

    def body(*refs):
        ins, outs, sems = refs[:n_in], refs[n_in:n_in + n_out], refs[n_in + n_out:]
        exchange.start(ins, outs, *sems)
        exchange.finish(ins, outs, *sems)

    return pl.pallas_call(
        body, name=name, out_shape=tuple(exchange.out_shapes), in_specs=[_HBM] * n_in,
        out_specs=tuple([_HBM] * n_out), scratch_shapes=exchange.sems,
    )(*exchange.ins)


def _pair_exchange(name, grads):
    n = len(grads)

    def body(*refs):
        ins, got = refs[:n], refs[n:2 * n]
        send_sems, recv_sems = refs[2 * n:]
        x, y, c, _ = _position()
        sibling = (x, y, 1 - c)

        def copy(a, src, dst):
            return pltpu.make_async_remote_copy(src_ref=src, dst_ref=dst, send_sem=send_sems.at[a],
                                                recv_sem=recv_sems.at[a], device_id=sibling, device_id_type=MESH)

        for a in range(n):
            half = ins[a].shape[1] // 2
            for slab in range(N_CHIPS):
                for start, size in _row_chunks(half, D2D_CHUNKS):
                    copy(a, ins[a].at[slab, pl.ds((1 - c) * half + start, size), :],
                         got[a].at[slab, pl.ds(start, size), :]).start()
        for a in range(n):
            half = ins[a].shape[1] // 2
            copy(a, ins[a].at[:, pl.ds((1 - c) * half, half), :], got[a]).wait()

    halves = tuple(_sds((N_CHIPS, g.shape[1] // 2, g.shape[2]), g.dtype) for g in grads)
    return pl.pallas_call(
        body, name=name, out_shape=halves, in_specs=[_HBM] * n, out_specs=tuple([_HBM] * n),
        scratch_shapes=[pltpu.SemaphoreType.DMA((n,)), pltpu.SemaphoreType.DMA((n,))],
    )(*grads)


def _chip_exchange(hsums):
    n = len(hsums)

    def copies(ins, outs, send_sems, recv_sems, local_sems):
        x, y, c, chips = _position()
        me = 2 * x + y
        cps = []
        for a in range(n):
            cps.append(pltpu.make_async_copy(ins[a].at[me], outs[a].at[me], local_sems.at[a]))
            for k, (px, py) in enumerate(chips):
                cps.append(pltpu.make_async_remote_copy(
                    src_ref=ins[a].at[2 * px + py], dst_ref=outs[a].at[me], send_sem=send_sems.at[a, k],
                    recv_sem=recv_sems.at[a, k], device_id=(px, py, c), device_id_type=MESH))
        return cps

    def start(*refs):
        for cp in copies(*refs):
            cp.start()

    def finish(*refs):
        for cp in copies(*refs):
            cp.wait()

    return _Carried(hsums, [_sds(h.shape, h.dtype) for h in hsums],
                    [pltpu.SemaphoreType.DMA((n, 3)), pltpu.SemaphoreType.DMA((n, 3)), pltpu.SemaphoreType.DMA((n,))],
                    start, finish)


def _pair_gather(bufs):
    n = len(bufs)

    def body(*refs):
        outs = refs[n:2 * n]
        send_sems, recv_sems = refs[2 * n:]
        x, y, c, _ = _position()
        sibling = (x, y, 1 - c)

        def copy(a, rows):
            return pltpu.make_async_remote_copy(src_ref=rows, dst_ref=rows, send_sem=send_sems.at[a],
                                                recv_sem=recv_sems.at[a], device_id=sibling, device_id_type=MESH)

        for a in range(n):
            half = outs[a].shape[0] // 2
            for start, size in _row_chunks(half, 2 * D2D_CHUNKS):
                copy(a, outs[a].at[pl.ds(c * half + start, size), :]).start()
        for a in range(n):
            half = outs[a].shape[0] // 2
            copy(a, outs[a].at[pl.ds(c * half, half), :]).wait_send()
            copy(a, outs[a].at[pl.ds((1 - c) * half, half), :]).wait_recv()

    return pl.pallas_call(
        body, name="grad_pair_gather", out_shape=tuple(_sds(b.shape, b.dtype) for b in bufs),
        in_specs=[_HBM] * n, out_specs=tuple([_HBM] * n), input_output_aliases={a: a for a in range(n)},
        scratch_shapes=[pltpu.SemaphoreType.DMA((n,)), pltpu.SemaphoreType.DMA((n,))],
    )(*bufs)


def _small_allreduce(packed):
    m_per, n_cols = packed.shape

    def body(x_ref, sum_ref, all_ref, send_sems, recv_sems, local_sem):
        x, y, c, chips = _position()
        me, sibling = (x, y, c), (x, y, 1 - c)

        def rows(px, py, pc):
            return all_ref.at[pl.ds((4 * px + 2 * py + pc) * m_per, m_per), :]

        def copy(k, block, to, src=None):
            return pltpu.make_async_remote_copy(
                src_ref=rows(*block) if src is None else src, dst_ref=rows(*block), send_sem=send_sems.at[k],
                recv_sem=recv_sems.at[k], device_id=to, device_id_type=MESH)

        mine = pltpu.make_async_copy(x_ref, rows(*me), local_sem)
        mine.start()
        first = [copy(0, me, sibling, src=x_ref)]
        first += [copy(1 + j, me, (*chip, c), src=x_ref) for j, chip in enumerate(chips)]
        for cp in first:
            cp.start()
        passed = [copy(4 + j, (*chip, c), sibling) for j, chip in enumerate(chips)]
        for j, chip in enumerate(chips):
            copy(1 + j, (*chip, c), me).wait_recv()
            passed[j].start()
        copy(0, sibling, me).wait_recv()
        for j, chip in enumerate(chips):
            copy(4 + j, (*chip, 1 - c), me).wait_recv()
        for cp in first + passed:
            cp.wait_send()
        mine.wait()
        acc = all_ref[0:m_per, :]
        for d in range(1, N_DEV):
            acc = acc + all_ref[d * m_per:(d + 1) * m_per, :]
        sum_ref[...] = acc

    vmem = pl.BlockSpec(memory_space=pltpu.VMEM)
    total, _ = pl.pallas_call(
        body, name="small_allreduce",
        out_shape=(_sds((m_per, n_cols), F32), _sds((N_DEV * m_per, n_cols), F32)),
        in_specs=[vmem], out_specs=(vmem, vmem),
        scratch_shapes=[pltpu.SemaphoreType.DMA((7,)), pltpu.SemaphoreType.DMA((7,)), pltpu.SemaphoreType.DMA],
    )(packed)
    return total


def _pair_sum(core, own, got, tm):
    _, half, cols = got.shape
    nb = half // tm

    def body(c_ref, a_ref, b_ref, o_ref):
        o_ref[...] = (a_ref[...] + b_ref[...]).astype(BF16)

    return pl.pallas_call(
        body, name="grad_pair_sum", out_shape=_sds(got.shape, BF16),
        grid_spec=pltpu.PrefetchScalarGridSpec(
            num_scalar_prefetch=1, grid=(N_CHIPS, nb),
            in_specs=[pl.BlockSpec((None, tm, cols), lambda s, i, c_ref: (s, c_ref[0] * nb + i, 0)),
                      pl.BlockSpec((None, tm, cols), lambda s, i, c_ref: (s, i, 0))],
            out_specs=pl.BlockSpec((None, tm, cols), lambda s, i, c_ref: (s, i, 0))),
        compiler_params=_cparams(2),
    )(core, own, got)


def _chip_sum(core, slabs, tm):
    _, half, cols = slabs.shape
    nb = half // tm

    def body(c_ref, s_ref, o_ref):
        acc = s_ref[0].astype(F32)
        for k in range(1, N_CHIPS):
            acc = acc + s_ref[k].astype(F32)
        o_ref[...] = acc

    return pl.pallas_call(
        body, name="grad_chip_sum", out_shape=_sds((2 * half, cols), F32),
        grid_spec=pltpu.PrefetchScalarGridSpec(
            num_scalar_prefetch=1, grid=(nb,),
            in_specs=[pl.BlockSpec((N_CHIPS, tm, cols), lambda i, c_ref: (0, i, 0))],
            out_specs=pl.BlockSpec((tm, cols), lambda i, c_ref: (c_ref[0] * nb + i, 0))),
        compiler_params=_cparams(1),
    )(core, slabs)


def _adam_math(w, g, m, v):
    m2 = ADAM_B1 * m + (1.0 - ADAM_B1) * g
    v2 = ADAM_B2 * v + (1.0 - ADAM_B2) * (g * g)
    m_hat = m2 / (1.0 - ADAM_B1 ** ADAM_STEP)
    v_hat = v2 / (1.0 - ADAM_B2 ** ADAM_STEP)
    delta = -ADAM_LR * (m_hat / (jnp.sqrt(v_hat) + ADAM_EPS) + ADAM_WD * w)
    return delta, m2, v2


def _adamw(name, w, g, m, v, tm):
    def body(w_ref, g_ref, m_ref, v_ref, d_ref, m2_ref, v2_ref):
        d, m2, v2 = _adam_math(w_ref[...], g_ref[...], m_ref[...], v_ref[...])
        d_ref[...] = d
        m2_ref[...] = m2
        v2_ref[...] = v2

    return _rows_call(name, body, tm, [w, g, m, v], [], [_sds(w.shape, F32)] * 3)


_SMALL_NAMES = ("norm_mix_pre", "gm_ln_w", "gm_ln_b", "gm_w_s", "gm_b_s", "conv_w", "conv_b", "dt_bias", "a_log",
                "d_skip", "ssm_norm_w", "norm_mix_post", "norm_ffn_pre", "norm_ffn_post")
_PACK_COLS = 1024


def _pack(parts, tail=None):
    pieces = [parts[n].reshape(-1) for n in _SMALL_NAMES]
    flat = jnp.concatenate(pieces if tail is None else pieces + [tail])
    rows = -(-flat.shape[0] // (8 * _PACK_COLS)) * 8
    flat = jnp.pad(flat, (0, rows * _PACK_COLS - flat.shape[0]))
    return flat.reshape(rows, _PACK_COLS)


def _unpack(packed, shapes):
    flat = packed.reshape(-1)
    out, off = {}, 0
    for n in _SMALL_NAMES:
        size = 1
        for s in shapes[n]:
            size *= s
        out[n] = flat[off:off + size].reshape(shapes[n])
        off += size
    return out


def kernel(x, norm_mix_pre, w_in, gm_ln_w, gm_ln_b, gm_w_s, gm_b_s, conv_w, conv_b, dt_bias, a_log, d_skip, ssm_norm_w, w_out, norm_mix_post, norm_ffn_pre, w_up, w_down, norm_ffn_post, loss_target, m_norm_mix_pre, m_w_in, m_gm_ln_w, m_gm_ln_b, m_gm_w_s, m_gm_b_s, m_conv_w, m_conv_b, m_dt_bias, m_a_log, m_d_skip, m_ssm_norm_w, m_w_out, m_norm_mix_post, m_norm_ffn_pre, m_w_up, m_w_down, m_norm_ffn_post, v_norm_mix_pre, v_w_in, v_gm_ln_w, v_gm_ln_b, v_gm_w_s, v_gm_b_s, v_conv_w, v_conv_b, v_dt_bias, v_a_log, v_d_skip, v_ssm_norm_w, v_w_out, v_norm_mix_post, v_norm_ffn_pre, v_w_up, v_w_down, v_norm_ffn_post):
    params = dict(norm_mix_pre=norm_mix_pre, w_in=w_in, gm_ln_w=gm_ln_w, gm_ln_b=gm_ln_b, gm_w_s=gm_w_s, gm_b_s=gm_b_s,
                  conv_w=conv_w, conv_b=conv_b, dt_bias=dt_bias, a_log=a_log, d_skip=d_skip, ssm_norm_w=ssm_norm_w,
                  w_out=w_out, norm_mix_post=norm_mix_post, norm_ffn_pre=norm_ffn_pre, w_up=w_up, w_down=w_down,
                  norm_ffn_post=norm_ffn_post)
    mom1 = dict(norm_mix_pre=m_norm_mix_pre, w_in=m_w_in, gm_ln_w=m_gm_ln_w, gm_ln_b=m_gm_ln_b, gm_w_s=m_gm_w_s,
                gm_b_s=m_gm_b_s, conv_w=m_conv_w, conv_b=m_conv_b, dt_bias=m_dt_bias, a_log=m_a_log, d_skip=m_d_skip,
                ssm_norm_w=m_ssm_norm_w, w_out=m_w_out, norm_mix_post=m_norm_mix_post, norm_ffn_pre=m_norm_ffn_pre,
                w_up=m_w_up, w_down=m_w_down, norm_ffn_post=m_norm_ffn_post)
    mom2 = dict(norm_mix_pre=v_norm_mix_pre, w_in=v_w_in, gm_ln_w=v_gm_ln_w, gm_ln_b=v_gm_ln_b, gm_w_s=v_gm_w_s,
                gm_b_s=v_gm_b_s, conv_w=v_conv_w, conv_b=v_conv_b, dt_bias=v_dt_bias, a_log=v_a_log, d_skip=v_d_skip,
                ssm_norm_w=v_ssm_norm_w, w_out=v_w_out, norm_mix_post=v_norm_mix_post, norm_ffn_pre=v_norm_ffn_pre,
                w_up=v_w_up, w_down=v_w_down, norm_ffn_post=v_norm_ffn_post)
    names = list(params)
    big = ("w_in", "w_out", "w_up", "w_down")
    chip = 2 * lax.axis_index("x") + lax.axis_index("y")

    shards = {n: params[n][0].astype(BF16) for n in big}
    conv_shard = jnp.pad(conv_w[0], ((0, 16 - CONV_K), (0, 0)))
    g_in4, g_conv4 = _run_exchange("allgather_w_in", _allgather_exchange([shards["w_in"], conv_shard]))
    conv_full = jnp.transpose(g_conv4[:, :CONV_K, :], (1, 0, 2)).reshape(CONV_K, CONV_CH)

    small = {n: params[n][0] if params[n].ndim >= 3 else params[n] for n in _SMALL_NAMES if n != "conv_w"}
    core = lax.axis_index("c").astype(jnp.int32).reshape(1)
    loss_part, grad_x, big_grads, small_grads = _forward_backward(
        x, loss_target, g_in4, conv_full, small, shards["w_out"], shards["w_up"], shards["w_down"], core)

    small_shapes = {n: (small_grads[n].shape) for n in _SMALL_NAMES}
    small_total = _small_allreduce(_pack(small_grads, tail=loss_part.reshape(1)))
    small_sum = _unpack(small_total, small_shapes)
    loss = small_total.reshape(-1)[sum(small_grads[n].size for n in _SMALL_NAMES)]
    small_sum["conv_w"] = lax.dynamic_slice_in_dim(small_sum["conv_w"], chip * (CONV_CH // N_CHIPS), CONV_CH // N_CHIPS, axis=1)

    grads, delta, new_m, new_v = {}, {}, {}, {}
    for n, tm in zip(big, (256, 128, 256, 256)):
        g = big_grads[n]
        d, m2, v2 = _adamw("adamw_" + n, params[n][0], g, mom1[n][0], mom2[n][0], tm)
        grads[n], delta[n], new_m[n], new_v[n] = g[None], d[None], m2[None], v2[None]
    local_shapes = {n: params[n].shape[1:] if params[n].ndim >= 3 else params[n].shape for n in _SMALL_NAMES}
    flat = lambda tree: {n: tree[n].reshape(local_shapes[n]) for n in _SMALL_NAMES}
    packed = [_pack(flat(t)) for t in (params, small_sum, mom1, mom2)]
    d_p, m_p, v_p = _adamw("adamw_small", *packed, packed[0].shape[0])
    for src, dst in ((d_p, delta), (m_p, new_m), (v_p, new_v)):
        for n, val in _unpack(src, local_shapes).items():
            dst[n] = val.reshape(params[n].shape)
    for n in _SMALL_NAMES:
        grads[n] = small_sum[n].reshape(params[n].shape)

    out = [loss, grad_x]
    for tree in (grads, delta, new_m, new_v):
        out += [tree[n] for n in names]
    return tuple(out)
```

```python
import functools

import jax
import jax.numpy as jnp
from jax import lax
from jax.experimental import pallas as pl
from jax.experimental.pallas import tpu as pltpu

F32 = jnp.float32
BF16 = jnp.bfloat16
HI = lax.Precision.HIGHEST
MESH = pl.DeviceIdType.MESH

EPS = 1e-6
D_MODEL = 1024
GM_WIDTH = 512
SSM_WIDTH = 512
N_HEADS = 8
HEAD_DIM = 64
CHUNK = 128
SSM_GROUPS = 2
GROUP_W = SSM_WIDTH // SSM_GROUPS
SSM_STATE = 128
CONV_K = 4
CONV_CH = 1024
D_FF = 4096
IN_COLS = 2568
DT_PAD = 128
N_CHIPS = 4
N_DEV = 8

ADAM_LR = 0.001
ADAM_B1 = 0.9
ADAM_B2 = 0.999
ADAM_EPS = 1e-08
ADAM_WD = 0.01
ADAM_STEP = 10

VMEM_LIMIT_BYTES = 56 * 1024 * 1024
FF_TILE = 512
DW_TOKENS_PER_STEP = 2048


def _cparams(n_axes):
    return pltpu.CompilerParams(dimension_semantics=("arbitrary",) * n_axes, vmem_limit_bytes=VMEM_LIMIT_BYTES)


def _dot(a, b):
    return jnp.dot(a.astype(BF16), b.astype(BF16), preferred_element_type=F32)


def _dot_nt(a, b):
    return lax.dot_general(a.astype(BF16), b.astype(BF16), (((1,), (1,)), ((), ())), preferred_element_type=F32)


def _dot_tn(a, b):
    return lax.dot_general(a.astype(BF16), b.astype(BF16), (((0,), (0,)), ((), ())), preferred_element_type=F32)


def _sigmoid(x):
    return 1.0 / (1.0 + jnp.exp(-x))


_GELU_C = 0.7978845608028654
_GELU_A = 0.044715


def _gelu(x):
    t = jnp.tanh(_GELU_C * (x + _GELU_A * (x * x * x)))
    return 0.5 * x * (1.0 + t), t


def _gelu_grad(x, t):
    return 0.5 * (1.0 + t) + 0.5 * x * (1.0 - t * t) * (_GELU_C * (1.0 + 3.0 * _GELU_A * x * x))


def _rms_fwd(x, w):
    r = lax.rsqrt(jnp.mean(x * x, axis=-1, keepdims=True) + EPS)
    return x * r * w, r


def _rms_bwd(x, r, w, dy):
    g = dy * w
    dx = r * g - x * (r * r * r) * jnp.mean(g * x, axis=-1, keepdims=True)
    dw = jnp.sum(dy * x * r, axis=0, keepdims=True)
    return dx, dw


class _Carried:
    def __init__(self, ins, out_shapes, sems, start, finish):
        self.ins, self.out_shapes, self.sems, self.start, self.finish = list(ins), list(out_shapes), list(sems), start, finish


def _split_carried(refs, n_in, n_out, n_scratch, carried):
    n_ci, n_co, n_cs = len(carried.ins), len(carried.out_shapes), len(carried.sems)
    ins, rest = refs[:n_in], refs[n_in:]
    c_ins, rest = rest[:n_ci], rest[n_ci:]
    outs, rest = rest[:n_out], rest[n_out:]
    c_outs, rest = rest[:n_co], rest[n_co:]
    scr, c_sems = rest[:n_scratch], rest[n_scratch:]
    assert len(c_sems) == n_cs
    return tuple(ins) + tuple(outs) + tuple(scr), c_ins, c_outs, c_sems


def _rows_call(name, body, tm, row_ins, const_ins, row_outs, acc_outs=(), scratch=(), carried=None):
    n_rows = row_ins[0].shape[0]
    assert n_rows % tm == 0
    n_steps = n_rows // tm
    n_in = len(row_ins) + len(const_ins)
    n_ro = len(row_outs)
    n_acc = len(acc_outs)

    def kern(*refs):
        accs = refs[n_in + n_ro:n_in + n_ro + n_acc]

        @pl.when(pl.program_id(0) == 0)
        def _():
            for a in accs:
                a[...] = jnp.zeros_like(a)

        body(*refs)

    def whole(shape):
        nd = len(shape)
        return pl.BlockSpec(tuple(shape), lambda i: (0,) * nd)

    in_specs = [pl.BlockSpec((tm, a.shape[1]), lambda i: (i, 0)) for a in row_ins]
    in_specs += [whole(a.shape) for a in const_ins]
    out_specs = [pl.BlockSpec((tm, s.shape[1]), lambda i: (i, 0)) for s in row_outs]
    out_specs += [whole(s.shape) for s in acc_outs]
    return _call_carrying(
        kern, carried, name=name, grid=(n_steps,), in_specs=in_specs, out_specs=out_specs,
        out_shape=tuple(row_outs) + tuple(acc_outs), scratch_shapes=list(scratch), operands=list(row_ins) + list(const_ins))


def _call_carrying(body, carried, *, name, grid, in_specs, out_specs, out_shape, scratch_shapes, operands):
    n_in, n_out, n_scratch = len(in_specs), len(out_specs), len(scratch_shapes)
    kern = body
    if carried is not None:
        def kern(*refs):
            plain, c_ins, c_outs, c_sems = _split_carried(refs, n_in, n_out, n_scratch, carried)
            first, last = True, True
            for d, size in enumerate(grid):
                first = jnp.logical_and(first, pl.program_id(d) == 0)
                last = jnp.logical_and(last, pl.program_id(d) == size - 1)

            @pl.when(first)
            def _():
                carried.start(c_ins, c_outs, *c_sems)

            body(*plain)

            @pl.when(last)
            def _():
                carried.finish(c_ins, c_outs, *c_sems)

        in_specs = list(in_specs) + [_HBM] * len(carried.ins)
        out_specs = list(out_specs) + [_HBM] * len(carried.out_shapes)
        out_shape = tuple(out_shape) + tuple(carried.out_shapes)
        operands = list(operands) + carried.ins
        scratch_shapes = list(scratch_shapes) + carried.sems
    return pl.pallas_call(
        kern, name=name, grid=grid, in_specs=in_specs, out_specs=out_specs, out_shape=out_shape,
        scratch_shapes=scratch_shapes, compiler_params=_cparams(len(grid)),
    )(*operands)


def _sds(shape, dtype):
    return jax.ShapeDtypeStruct(tuple(shape), dtype)


def _matmul_tn(name, a, b, tm, tn, tk, stacked=False, carried=None):
    k_dim, m_dim = a.shape
    n_dim = b.shape[1]
    assert m_dim % tm == 0 and n_dim % tn == 0 and k_dim % tk == 0

    def kern(a_ref, b_ref, o_ref):
        @pl.when(pl.program_id(2) == 0)
        def _():
            o_ref[...] = jnp.zeros_like(o_ref)

        o_ref[...] += _dot_tn(a_ref[...], b_ref[...])

    if stacked:
        assert tm == m_dim
        out_shape = _sds((n_dim // tn, m_dim, tn), F32)
        out_spec = pl.BlockSpec((None, tm, tn), lambda i, j, k: (j, i, 0))
    else:
        out_shape = _sds((m_dim, n_dim), F32)
        out_spec = pl.BlockSpec((tm, tn), lambda i, j, k: (i, j))
    outs = _call_carrying(
        kern, carried, name=name, grid=(m_dim // tm, n_dim // tn, k_dim // tk),
        in_specs=[pl.BlockSpec((tk, tm), lambda i, j, k: (k, i)), pl.BlockSpec((tk, tn), lambda i, j, k: (k, j))],
        out_specs=[out_spec], out_shape=(out_shape,), scratch_shapes=[], operands=[a, b])
    return outs[0] if carried is None else outs


def _inproj_fwd(x, nw, w_uv, w_xbc, w_z, w_dt, tm=256, carried=None):
    n_tok = x.shape[0]

    def body(x_ref, nw_ref, wuv_ref, wxbc_ref, wz_ref, wdt_ref, puv_ref, pxbc_ref, pz_ref, pdt_ref):
        h, _ = _rms_fwd(x_ref[...], nw_ref[...])
        h = h.astype(BF16)
        puv_ref[...] = jnp.dot(h, wuv_ref[...], preferred_element_type=F32)
        pxbc_ref[...] = jnp.dot(h, wxbc_ref[...], preferred_element_type=F32)
        pz_ref[...] = jnp.dot(h, wz_ref[...], preferred_element_type=F32)
        pdt_ref[...] = jnp.dot(h, wdt_ref[...], preferred_element_type=F32)

    return _rows_call(
        "inproj_fwd", body, tm, [x], [nw, w_uv, w_xbc, w_z, w_dt],
        [_sds((n_tok, 2 * GM_WIDTH), F32), _sds((n_tok, CONV_CH), F32), _sds((n_tok, SSM_WIDTH), F32),
         _sds((n_tok, DT_PAD), F32)], carried=carried)


def _head_lane_mask(width, head):
    lane = lax.broadcasted_iota(jnp.int32, (1, width), 1)
    return (lane // HEAD_DIM) == head


def _split_terms(x, terms):
    parts = []
    for _ in range(terms):
        p = x.astype(BF16)
        parts.append(p)
        x = x - p.astype(F32)
    return parts


def _seg_dots(vals, ind, terms=2):
    m = vals[0].shape[0]
    parts = []
    for v in vals:
        parts += _split_terms(v, terms)
    red = jnp.dot(jnp.concatenate(parts, axis=0), ind, preferred_element_type=F32)
    outs = []
    for i in range(len(vals)):
        acc = red[i * terms * m:(i * terms + 1) * m]
        for t in range(1, terms):
            acc = acc + red[(i * terms + t) * m:(i * terms + t + 1) * m]
        outs.append(acc)
    return outs


def _tri_dot(mask, x, terms=3):
    n = x.shape[1]
    red = jnp.dot(mask.astype(BF16), jnp.concatenate(_split_terms(x, terms), axis=1), preferred_element_type=F32)
    acc = red[:, :n]
    for t in range(1, terms):
        acc = acc + red[:, t * n:(t + 1) * n]
    return acc


def _gmlp_common(puv, lnw, lnb, e_bf, et_bf):
    u = puv[:, :GM_WIDTH]
    v = puv[:, GM_WIDTH:]
    gu, tu = _gelu(u)
    gv, tv = _gelu(v)
    (s1,) = _seg_dots([gv], et_bf)
    (mu,) = _seg_dots([s1 * (1.0 / HEAD_DIM)], e_bf)
    xc = gv - mu
    (s2,) = _seg_dots([xc * xc], et_bf)
    (rstd,) = _seg_dots([lax.rsqrt(s2 * (1.0 / HEAD_DIM) + EPS)], e_bf)
    xhat = xc * rstd
    vn = xhat * lnw + lnb
    return u, v, gu, tu, tv, rstd, xhat, vn


def _tril_mask():
    r = lax.broadcasted_iota(jnp.int32, (CHUNK, CHUNK), 0)
    c = lax.broadcasted_iota(jnp.int32, (CHUNK, CHUNK), 1)
    return r >= c


def _head_blocks(v):
    return jnp.concatenate([jnp.where(_head_lane_mask(GM_WIDTH, h), v, jnp.zeros_like(v)) for h in range(N_HEADS)], axis=0)


def _causal_w_cat(w_cat):
    t = lax.broadcasted_iota(jnp.int32, (CHUNK, N_HEADS * CHUNK), 0)
    s = lax.broadcasted_iota(jnp.int32, (CHUNK, N_HEADS * CHUNK), 1) % CHUNK
    return jnp.where(t >= s, w_cat, 0.0).astype(BF16)


def _gmlp_chunk_fwd(puv, lnw, lnb, e_bf, et_bf, wm, bmap):
    _, _, gu, _, _, _, _, vn = _gmlp_common(puv, lnw, lnb, e_bf, et_bf)
    mixed = jnp.dot(wm, _head_blocks(vn.astype(BF16)), preferred_element_type=F32) + bmap
    return (gu * mixed).astype(BF16)


def _ssd_pre(xr, prev, cw_ref, cb, pdt, dtb, alog, emap):
    rowi = lax.broadcasted_iota(jnp.int32, (CHUNK, 1), 0)

    def down(s):
        return jnp.where(rowi < s, pltpu.roll(prev, s, 0), pltpu.roll(xr, s, 0))

    shifted = [down(3), down(2), down(1), xr]
    xc = cb
    for k in range(CONV_K):
        xc = xc + cw_ref[k] * shifted[k]
    sg = _sigmoid(xc)
    xa = xc * sg
    pre = pdt + dtb
    dt = jnp.maximum(pre, 0.0) + jnp.log(1.0 + jnp.exp(-jnp.abs(pre)))
    a_neg = -jnp.exp(alog)
    a_cs = _tri_dot(_tril_mask(), dt * a_neg)
    acs_map, dt_map = _seg_dots([a_cs, dt], emap, terms=3)
    return dict(shifted=shifted, xc=xc, sg=sg, xa=xa, pre=pre, dt=dt, a_neg=a_neg, a_cs=a_cs,
                acs_map=acs_map, dt_map=dt_map, rowi=rowi)


def _ssd_maps(p):
    last = p["rowi"] == CHUNK - 1
    aq_map = jnp.sum(jnp.where(last, p["acs_map"], 0.0), axis=0, keepdims=True)
    e_exp = jnp.exp(p["acs_map"])
    dte = jnp.exp(aq_map - p["acs_map"])
    cd = jnp.exp(aq_map)
    return last, e_exp, dte, cd


def _head_decay(a_cs, a_cs_t, head, tri):
    lane = lax.broadcasted_iota(jnp.int32, (1, DT_PAD), 1)
    sub = lax.broadcasted_iota(jnp.int32, (DT_PAD, 1), 0)
    col = jnp.sum(jnp.where(lane == head, a_cs, 0.0), axis=1, keepdims=True)
    row = jnp.sum(jnp.where(sub == head, a_cs_t, 0.0), axis=0, keepdims=True)
    return jnp.exp(jnp.where(tri, col - row, -1e30))


def _gate_fwd(y, z, nw):
    sz = _sigmoid(z)
    zg = z * sz
    yg = y * zg
    outs, rs = [], []
    for g in range(SSM_GROUPS):
        gs = slice(g * GROUP_W, (g + 1) * GROUP_W)
        o, r = _rms_fwd(yg[:, gs], nw[:, gs])
        outs.append(o)
        rs.append(r)
    return sz, zg, yg, outs, rs


def _ssd_const_specs():
    def whole(shape):
        nd = len(shape)
        return pl.BlockSpec(tuple(shape), lambda b, c: (0,) * nd)
    return [whole((CONV_K, 1, CONV_CH)), whole((1, CONV_CH)), whole((1, DT_PAD)), whole((1, DT_PAD)),
            whole((1, SSM_WIDTH)), whole((1, SSM_WIDTH)), whole((DT_PAD, SSM_WIDTH)), whole((SSM_WIDTH, DT_PAD))]


def _mixer_fwd(p_uv, p_xbc, p_z, p_dt, lnw, lnb, w_cat, bmap, conv_w, conv_b, dt_bias, a_log, dskip_map, norm_w,
               e_bf, et_bf, n_seq, carried=None):
    n_tok = p_xbc.shape[0]
    nc = n_tok // n_seq // CHUNK

    def body(puv_ref, xr_ref, z_ref, pdt_ref, lnw_ref, lnb_ref, wcat_ref, bmap_ref,
             cw_ref, cb_ref, dtb_ref, alog_ref, dsk_ref, nw_ref, e_ref, et_ref,
             ya_ref, yb_ref, yssd_ref, sprev_ref, wm_scr, prev_scr, s_scr):
        @pl.when(jnp.logical_and(pl.program_id(0) == 0, pl.program_id(1) == 0))
        def _():
            wm_scr[...] = _causal_w_cat(wcat_ref[...])

        @pl.when(pl.program_id(1) == 0)
        def _():
            prev_scr[...] = jnp.zeros_like(prev_scr)
            s_scr[...] = jnp.zeros_like(s_scr)

        ya_ref[...] = _gmlp_chunk_fwd(puv_ref[...], lnw_ref[...], lnb_ref[...], e_ref[...], et_ref[...], wm_scr[...],
                                      bmap_ref[...])
        xr = xr_ref[...]
        p = _ssd_pre(xr, prev_scr[...], cw_ref, cb_ref[...], pdt_ref[...], dtb_ref[...], alog_ref[...], e_ref[...])
        _, e_exp, dte, cd = _ssd_maps(p)
        xs = p["xa"][:, :SSM_WIDTH]
        xd = xs * p["dt_map"]
        a_cs_t = p["a_cs"].T
        tri = _tril_mask()
        s_old = s_scr[...]
        sprev_ref[...] = s_old
        for g in range(SSM_GROUPS):
            gs = slice(g * GROUP_W, (g + 1) * GROUP_W)
            bm = p["xa"][:, SSM_WIDTH + g * SSM_STATE: SSM_WIDTH + (g + 1) * SSM_STATE].astype(BF16)
            cm = p["xa"][:, SSM_WIDTH + (SSM_GROUPS + g) * SSM_STATE: SSM_WIDTH + (SSM_GROUPS + g + 1) * SSM_STATE].astype(BF16)
            cb_mat = _dot_nt(cm, bm)
            xdg = xd[:, gs].astype(BF16)
            y_g = _dot(cm, s_old[:, gs]) * e_exp[:, gs] + dsk_ref[:, gs] * xs[:, gs]
            for r in range(SSM_GROUPS * 2):
                dm = _head_decay(p["a_cs"], a_cs_t, g * 4 + r, tri)
                full = jnp.dot((cb_mat * dm).astype(BF16), xdg, preferred_element_type=F32)
                y_g = y_g + jnp.where(_head_lane_mask(GROUP_W, r), full, 0.0)
            yssd_ref[:, gs] = y_g
            s_scr[:, gs] = cd[:, gs] * s_old[:, gs] + _dot_tn(bm, xd[:, gs] * dte[:, gs])
        _, _, _, outs, _ = _gate_fwd(yssd_ref[...], z_ref[...], nw_ref[...])
        for g in range(SSM_GROUPS):
            yb_ref[:, g * GROUP_W:(g + 1) * GROUP_W] = outs[g].astype(BF16)
        prev_scr[...] = xr

    def rows(width):
        return pl.BlockSpec((CHUNK, width), lambda b, c: (b * nc + c, 0))

    def whole(shape):
        nd = len(shape)
        return pl.BlockSpec(tuple(shape), lambda b, c: (0,) * nd)

    return _call_carrying(
        body, carried, name="mixer_fwd", grid=(n_seq, nc),
        in_specs=[rows(2 * GM_WIDTH), rows(CONV_CH), rows(SSM_WIDTH), rows(DT_PAD), whole(lnw.shape), whole(lnb.shape),
                  whole(w_cat.shape), whole(bmap.shape)] + _ssd_const_specs(),
        out_specs=[rows(GM_WIDTH), rows(SSM_WIDTH), rows(SSM_WIDTH), rows(SSM_WIDTH)],
        out_shape=(_sds((n_tok, GM_WIDTH), BF16), _sds((n_tok, SSM_WIDTH), BF16), _sds((n_tok, SSM_WIDTH), F32),
                   _sds((n_tok, SSM_WIDTH), F32)),
        scratch_shapes=[pltpu.VMEM((CHUNK, N_HEADS * CHUNK), BF16), pltpu.VMEM((CHUNK, CONV_CH), F32),
                        pltpu.VMEM((SSM_STATE, SSM_WIDTH), F32)],
        operands=[p_uv, p_xbc, p_z, p_dt, lnw, lnb, w_cat, bmap, conv_w, conv_b, dt_bias, a_log, dskip_map, norm_w, e_bf,
                  et_bf])


def _outproj_fwd(ya, yb, x, w_out, nw_post, nw_pre2, tm=256):
    n_tok = x.shape[0]

    def body(ya_ref, yb_ref, x_ref, wo_ref, nwa_ref, nwb_ref, o_ref, x1_ref, h2_ref):
        o = jnp.dot(ya_ref[...], wo_ref[:GM_WIDTH, :], preferred_element_type=F32)
        o = o + jnp.dot(yb_ref[...], wo_ref[GM_WIDTH:, :], preferred_element_type=F32)
        on, _ = _rms_fwd(o, nwa_ref[...])
        x1 = x_ref[...] + on
        h2, _ = _rms_fwd(x1, nwb_ref[...])
        o_ref[...] = o
        x1_ref[...] = x1
        h2_ref[...] = h2.astype(BF16)

    return _rows_call("outproj_fwd", body, tm, [ya, yb, x], [w_out, nw_post, nw_pre2],
                      [_sds((n_tok, D_MODEL), F32), _sds((n_tok, D_MODEL), F32), _sds((n_tok, D_MODEL), BF16)])


def _up_cols(wup_ref, j):
    per = (D_FF // N_CHIPS) // FF_TILE
    return wup_ref[j // per, :, (j % per) * FF_TILE:(j % per + 1) * FF_TILE]


def _mlp_fwd(h2, x1, tgt, w_up, w_down, nw, tm=256):
    n_tok = x1.shape[0]

    def body(h2_ref, x1_ref, tgt_ref, wup_ref, wdn_ref, nw_ref, f_ref, dd_ref, dy_ref, loss_ref, dnw_ref):
        h2v = h2_ref[...]
        acc = jnp.zeros((tm, D_MODEL), F32)
        for j in range(D_FF // FF_TILE):
            cs = slice(j * FF_TILE, (j + 1) * FF_TILE)
            u = jnp.dot(h2v, _up_cols(wup_ref, j), preferred_element_type=F32)
            f = jnp.square(jnp.maximum(u, 0.0)).astype(BF16)
            f_ref[:, cs] = f
            acc = acc + jnp.dot(f, wdn_ref[cs, :], preferred_element_type=F32)
        dn, r = _rms_fwd(acc, nw_ref[...])
        e = x1_ref[...] + dn - tgt_ref[...]
        loss_ref[...] += jnp.full(loss_ref.shape, (0.5 / D_MODEL) * jnp.sum(e * e), F32)
        dy = e * (1.0 / D_MODEL)
        dd, dnw = _rms_bwd(acc, r, nw_ref[...], dy)
        dy_ref[...] = dy
        dd_ref[...] = dd.astype(BF16)
        dnw_ref[...] += dnw

    return _rows_call(
        "mlp_fwd", body, tm, [h2, x1, tgt], [w_up, w_down, nw],
        [_sds((n_tok, D_FF), BF16), _sds((n_tok, D_MODEL), BF16), _sds((n_tok, D_MODEL), F32)],
        [_sds((8, 128), F32), _sds((1, D_MODEL), F32)])


def _mlp_bwd(dd, f, x1, dy, w_down, w_up, nw, tm=256):
    n_tok = x1.shape[0]

    def body(dd_ref, f_ref, x1_ref, dy_ref, wdn_ref, wup_ref, nw_ref, dup_ref, dx1_ref, dnw_ref):
        ddv = dd_ref[...]
        acc = jnp.zeros((tm, D_MODEL), F32)
        for j in range(D_FF // FF_TILE):
            cs = slice(j * FF_TILE, (j + 1) * FF_TILE)
            df = _dot_nt(ddv, wdn_ref[cs, :])
            du = (df * (2.0 * jnp.sqrt(f_ref[:, cs].astype(F32)))).astype(BF16)
            dup_ref[:, cs] = du
            acc = acc + _dot_nt(du, _up_cols(wup_ref, j))
        x1v = x1_ref[...]
        _, r = _rms_fwd(x1v, nw_ref[...])
        dx, dnw = _rms_bwd(x1v, r, nw_ref[...], acc)
        dx1_ref[...] = dy_ref[...] + dx
        dnw_ref[...] += dnw

    return _rows_call("mlp_bwd", body, tm, [dd, f, x1, dy], [w_down, w_up, nw],
                      [_sds((n_tok, D_FF), BF16), _sds((n_tok, D_MODEL), F32)], [_sds((1, D_MODEL), F32)])


def _outproj_bwd(dx1, o, w_out, nw, tm=256):
    n_tok = dx1.shape[0]

    def body(dx1_ref, o_ref, wo_ref, nw_ref, do_ref, dya_ref, dyb_ref, dnw_ref):
        ov = o_ref[...]
        _, r = _rms_fwd(ov, nw_ref[...])
        do, dnw = _rms_bwd(ov, r, nw_ref[...], dx1_ref[...])
        dob = do.astype(BF16)
        do_ref[...] = dob
        dya_ref[...] = _dot_nt(dob, wo_ref[:GM_WIDTH, :])
        dyb_ref[...] = _dot_nt(dob, wo_ref[GM_WIDTH:, :])
        dnw_ref[...] += dnw

    return _rows_call("outproj_bwd", body, tm, [dx1, o], [w_out, nw],
                      [_sds((n_tok, D_MODEL), BF16), _sds((n_tok, GM_WIDTH), F32), _sds((n_tok, SSM_WIDTH), F32)],
                      [_sds((1, D_MODEL), F32)])


def _gmlp_bwd(p_uv, dya, lnw, lnb, e_bf, et_bf, w_cat, w_stack, bmap, carried=None):
    n_tok = p_uv.shape[0]

    def body(puv_ref, dya_ref, lnw_ref, lnb_ref, e_ref, et_ref, wcat_ref, wstack_ref, bmap_ref,
             dpuv_ref, dws_ref, dbs_ref, dlnw_ref, dlnb_ref, wm_scr, wsm_scr):
        t_stk = lax.broadcasted_iota(jnp.int32, (N_HEADS * CHUNK, CHUNK), 0) % CHUNK
        s_stk = lax.broadcasted_iota(jnp.int32, (N_HEADS * CHUNK, CHUNK), 1)

        @pl.when(pl.program_id(0) == 0)
        def _():
            wm_scr[...] = _causal_w_cat(wcat_ref[...])
            wsm_scr[...] = jnp.where(t_stk >= s_stk, wstack_ref[...], 0.0).astype(BF16)

        lnw_v = lnw_ref[...]
        e_v, et_v = e_ref[...], et_ref[...]
        u, v, gu, tu, tv, rstd, xhat, vn = _gmlp_common(puv_ref[...], lnw_v, lnb_ref[...], e_v, et_v)
        vnb = vn.astype(BF16)
        mixed = jnp.dot(wm_scr[...], _head_blocks(vnb), preferred_element_type=F32) + bmap_ref[...]
        dy = dya_ref[...]
        du = dy * mixed * _gelu_grad(u, tu)
        dmixed = dy * gu
        (dbs,) = _seg_dots([dmixed], et_v)
        dbs_ref[...] += dbs
        dblocks = _head_blocks(dmixed.astype(BF16))
        dvn = lax.dot_general(wsm_scr[...], dblocks, (((0,), (0,)), ((), ())), preferred_element_type=F32)
        dws = lax.dot_general(dblocks, vnb, (((1,), (1,)), ((), ())), preferred_element_type=F32)
        dws_ref[...] += jnp.where(t_stk >= s_stk, dws, 0.0)
        dlnw_ref[...] += jnp.sum(dvn * xhat, axis=0, keepdims=True)
        dlnb_ref[...] += jnp.sum(dvn, axis=0, keepdims=True)
        dxh = dvn * lnw_v
        m1, m2 = _seg_dots([dxh, dxh * xhat], et_v)
        m1, m2 = _seg_dots([m1 * (1.0 / HEAD_DIM), m2 * (1.0 / HEAD_DIM)], e_v)
        dgv = rstd * (dxh - m1 - xhat * m2)
        dv = dgv * _gelu_grad(v, tv)
        dpuv_ref[:, :GM_WIDTH] = du.astype(BF16)
        dpuv_ref[:, GM_WIDTH:] = dv.astype(BF16)

    return _rows_call(
        "gmlp_bwd", body, CHUNK, [p_uv, dya], [lnw, lnb, e_bf, et_bf, w_cat, w_stack, bmap],
        [_sds((n_tok, 2 * GM_WIDTH), BF16)],
        [_sds((N_HEADS * CHUNK, CHUNK), F32), _sds((CHUNK, DT_PAD), F32), _sds((1, GM_WIDTH), F32),
         _sds((1, GM_WIDTH), F32)],
        scratch=[pltpu.VMEM((CHUNK, N_HEADS * CHUNK), BF16), pltpu.VMEM((N_HEADS * CHUNK, CHUNK), BF16)],
        carried=carried)


def _ssd_bwd(p_xbc, p_z, p_dt, yssd, sprev, dyb, conv_w, conv_b, dt_bias, a_log, dskip_map, norm_w, e_bf, et_bf, n_seq,
             carried=None):
    n_tok = p_xbc.shape[0]
    nc = n_tok // n_seq // CHUNK

    def body(xr_ref, xprev_ref, z_ref, pdt_ref, yssd_ref, sprev_ref, dyb_ref,
             cw_ref, cb_ref, dtb_ref, alog_ref, dsk_ref, nw_ref, e_ref, et_ref,
             dpxbc_ref, dpz_ref, dpdt_ref, dcw_ref, dcb_ref, ddtb_ref, dalog_ref, ddsk_ref, dnw_ref,
             ds_scr, nxt_scr, dxa_scr):
        step = pl.program_id(1)
        first = jnp.logical_and(pl.program_id(0) == 0, step == 0)

        @pl.when(first)
        def _():
            for a in (dcw_ref, dcb_ref, ddtb_ref, dalog_ref, ddsk_ref, dnw_ref):
                a[...] = jnp.zeros_like(a)

        @pl.when(step == 0)
        def _():
            ds_scr[...] = jnp.zeros_like(ds_scr)
            nxt_scr[...] = jnp.zeros_like(nxt_scr)

        chunk = nc - 1 - step
        xr = xr_ref[...]
        prev = jnp.where(chunk == 0, 0.0, xprev_ref[...])
        et_v = et_ref[...]
        p = _ssd_pre(xr, prev, cw_ref, cb_ref[...], pdt_ref[...], dtb_ref[...], alog_ref[...], e_ref[...])
        last, e_exp, dte, cd = _ssd_maps(p)
        rowi = p["rowi"]
        xs = p["xa"][:, :SSM_WIDTH]
        xd = xs * p["dt_map"]
        a_cs_t = p["a_cs"].T
        tri = _tril_mask()
        dsk = dsk_ref[...]
        nw_v = nw_ref[...]

        yv = yssd_ref[...]
        zv = z_ref[...]
        sz, zg, yg, _, rs = _gate_fwd(yv, zv, nw_v)
        dout = dyb_ref[...]
        for g in range(SSM_GROUPS):
            gs = slice(g * GROUP_W, (g + 1) * GROUP_W)
            dyg_g, dnw_g = _rms_bwd(yg[:, gs], rs[g], nw_v[:, gs], dout[:, gs])
            dnw_ref[:, gs] += dnw_g
            dxa_scr[:, gs] = dyg_g
        dyg = dxa_scr[:, :SSM_WIDTH]
        d_y = dyg * zg
        dpz_ref[...] = (dyg * yv * (sz + zv * sz * (1.0 - sz))).astype(BF16)

        s_prev = sprev_ref[...]
        ds_next = ds_scr[...]
        lane_dt = lax.broadcasted_iota(jnp.int32, (1, DT_PAD), 1)
        da_cols = jnp.zeros((CHUNK, DT_PAD), F32)
        for g in range(SSM_GROUPS):
            gs = slice(g * GROUP_W, (g + 1) * GROUP_W)
            b_off = SSM_WIDTH + g * SSM_STATE
            c_off = SSM_WIDTH + (SSM_GROUPS + g) * SSM_STATE
            bm = p["xa"][:, b_off:b_off + SSM_STATE].astype(BF16)
            cm = p["xa"][:, c_off:c_off + SSM_STATE].astype(BF16)
            cb_mat = _dot_nt(cm, bm)
            d_yg = d_y[:, gs]
            d_ygb = d_yg.astype(BF16)
            xdg = xd[:, gs]
            xdgb = xdg.astype(BF16)
            ds_g = ds_next[:, gs]
            sp_g = s_prev[:, gs]
            bds = _dot(bm, ds_g)
            dcs = d_yg * e_exp[:, gs]
            d_c = _dot_nt(dcs, sp_g)
            ds_scr[:, gs] = cd[:, gs] * ds_g + _dot_tn(cm, dcs)
            d_b = _dot_nt(xdg * dte[:, gs], ds_g)
            dxd_g = bds * dte[:, gs]
            sum_dcb = jnp.zeros((CHUNK, CHUNK), F32)
            for r in range(SSM_GROUPS * 2):
                head = g * 4 + r
                mask = _head_lane_mask(GROUP_W, r)
                dm = _head_decay(p["a_cs"], a_cs_t, head, tri)
                m_mat = cb_mat * dm
                g_mat = _dot_nt(jnp.where(mask, d_yg, 0.0), xdgb)
                w_mat = g_mat * m_mat
                sum_dcb = sum_dcb + g_mat * dm
                dxd_g = dxd_g + jnp.where(mask, _dot_tn(m_mat, d_ygb), 0.0)
                da_h = jnp.sum(w_mat - w_mat.T, axis=1, keepdims=True)
                da_cols = da_cols + jnp.where(lane_dt == head, da_h, 0.0)
            d_c = d_c + _dot(sum_dcb, bm)
            d_b = d_b + _dot_tn(sum_dcb, cm)
            dxa_scr[:, b_off:b_off + SSM_STATE] = d_b
            dxa_scr[:, c_off:c_off + SSM_STATE] = d_c
            y_off_g = _dot(cm, sp_g) * e_exp[:, gs]
            t3 = bds * xdg * dte[:, gs]
            tail = jnp.sum(t3, axis=0, keepdims=True) + jnp.sum(ds_g * sp_g, axis=0, keepdims=True) * cd[:, gs]
            pre_g = d_yg * y_off_g - t3 + jnp.where(last, tail, 0.0)
            s_pre, ddt_g, s_dsk = _seg_dots([pre_g, dxd_g * xs[:, gs], d_yg * xs[:, gs]], et_v[gs, :])
            da_cols = da_cols + s_pre
            ddsk_ref[...] += jnp.sum(s_dsk, axis=0, keepdims=True)
            dxa_scr[:, gs] = dxd_g * p["dt_map"][:, gs] + dsk[:, gs] * d_yg
            if g == 0:
                ddt = ddt_g
            else:
                ddt = ddt + ddt_g
        r_i = lax.broadcasted_iota(jnp.int32, (CHUNK, CHUNK), 0)
        c_i = lax.broadcasted_iota(jnp.int32, (CHUNK, CHUNK), 1)
        ddta = _tri_dot(r_i <= c_i, da_cols, terms=2)
        ddt = ddt + ddta * p["a_neg"]
        dalog_ref[...] += jnp.sum(ddta * p["dt"], axis=0, keepdims=True) * p["a_neg"]
        draw = ddt * _sigmoid(p["pre"])
        ddtb_ref[...] += jnp.sum(draw, axis=0, keepdims=True)
        dpdt_ref[...] = draw.astype(BF16)

        xc = p["xc"]
        sg = p["sg"]
        dxc = dxa_scr[...] * (sg + xc * sg * (1.0 - sg))
        dcb_ref[...] += jnp.sum(dxc, axis=0, keepdims=True)
        for k in range(CONV_K):
            dcw_ref[k] += jnp.sum(dxc * p["shifted"][k], axis=0, keepdims=True)
        nxt = nxt_scr[...]

        def up(s):
            return jnp.where(rowi >= CHUNK - s, pltpu.roll(nxt, CHUNK - s, 0), pltpu.roll(dxc, CHUNK - s, 0))

        dxr = cw_ref[3] * dxc + cw_ref[2] * up(1) + cw_ref[1] * up(2) + cw_ref[0] * up(3)
        dpxbc_ref[...] = dxr.astype(BF16)
        nxt_scr[...] = dxc

    def rows(width):
        return pl.BlockSpec((CHUNK, width), lambda b, s: (b * nc + nc - 1 - s, 0))

    prev_rows = pl.BlockSpec((CHUNK, CONV_CH), lambda b, s: (b * nc + jnp.maximum(nc - 2 - s, 0), 0))

    def whole(shape):
        nd = len(shape)
        return pl.BlockSpec(tuple(shape), lambda b, s: (0,) * nd)

    acc_shapes = [(CONV_K, 1, CONV_CH), (1, CONV_CH), (1, DT_PAD), (1, DT_PAD), (1, DT_PAD), (1, SSM_WIDTH)]
    return _call_carrying(
        body, carried, name="ssd_bwd", grid=(n_seq, nc),
        in_specs=[rows(CONV_CH), prev_rows, rows(SSM_WIDTH), rows(DT_PAD), rows(SSM_WIDTH), rows(SSM_WIDTH),
                  rows(SSM_WIDTH)] + _ssd_const_specs(),
        out_specs=[rows(CONV_CH), rows(SSM_WIDTH), rows(DT_PAD)] + [whole(s) for s in acc_shapes],
        out_shape=tuple([_sds((n_tok, CONV_CH), BF16), _sds((n_tok, SSM_WIDTH), BF16), _sds((n_tok, DT_PAD), BF16)]
                        + [_sds(s, F32) for s in acc_shapes]),
        scratch_shapes=[pltpu.VMEM((SSM_STATE, SSM_WIDTH), F32), pltpu.VMEM((CHUNK, CONV_CH), F32),
                        pltpu.VMEM((CHUNK, CONV_CH), F32)],
        operands=[p_xbc, p_xbc, p_z, p_dt, yssd, sprev, dyb, conv_w, conv_b, dt_bias, a_log, dskip_map, norm_w, e_bf,
                  et_bf])


def _inproj_bwd(dp_uv, dp_xbc, dp_z, dp_dt, x, dx1, w_uv, w_xbc, w_z, w_dt, nw, tm=256, carried=None):
    n_tok = x.shape[0]

    def body(duv_ref, dxbc_ref, dz_ref, ddt_ref, x_ref, dx1_ref, wuv_ref, wxbc_ref, wz_ref, wdt_ref, nw_ref,
             gx_ref, h_ref, dnw_ref):
        dh = _dot_nt(duv_ref[...], wuv_ref[...]) + _dot_nt(dxbc_ref[...], wxbc_ref[...])
        dh = dh + _dot_nt(dz_ref[...], wz_ref[...]) + _dot_nt(ddt_ref[...], wdt_ref[...])
        xv = x_ref[...]
        h, r = _rms_fwd(xv, nw_ref[...])
        dx, dnw = _rms_bwd(xv, r, nw_ref[...], dh)
        gx_ref[...] = dx1_ref[...] + dx
        h_ref[...] = h.astype(BF16)
        dnw_ref[...] += dnw

    return _rows_call("inproj_bwd", body, tm, [dp_uv, dp_xbc, dp_z, dp_dt, x, dx1], [w_uv, w_xbc, w_z, w_dt, nw],
                      [_sds((n_tok, D_MODEL), F32), _sds((n_tok, D_MODEL), BF16)], [_sds((1, D_MODEL), F32)],
                      carried=carried)


def _const_maps():
    lane = jnp.arange(SSM_WIDTH) // HEAD_DIM
    e_bf = (jnp.arange(DT_PAD)[:, None] == lane[None, :]).astype(BF16)
    return e_bf, e_bf.T


def _pad_lanes(v, width):
    return jnp.pad(v, ((0, 0), (0, width - v.shape[1])))


SHARD_COLS = IN_COLS // N_CHIPS
_UV_END = 2 * GM_WIDTH
_Z_END = _UV_END + SSM_WIDTH
_XBC_END = _Z_END + CONV_CH


def _cols_from_shards(w4, lo, hi):
    pieces = []
    for j in range(N_CHIPS):
        a, b = max(lo, j * SHARD_COLS), min(hi, (j + 1) * SHARD_COLS)
        if a < b:
            pieces.append(w4[j][:, a - j * SHARD_COLS:b - j * SHARD_COLS])
    return pieces[0] if len(pieces) == 1 else jnp.concatenate(pieces, axis=1)


def _shards_from_cols(blocks):
    shards = []
    for j in range(N_CHIPS):
        pieces = []
        for arr, lo, hi in blocks:
            a, b = max(lo, j * SHARD_COLS), min(hi, (j + 1) * SHARD_COLS)
            if a < b:
                pieces.append(arr[:, a - lo:b - lo])
        shards.append(pieces[0] if len(pieces) == 1 else jnp.concatenate(pieces, axis=1))
    return jnp.stack(shards)


def _forward_backward(x, tgt, w_in4, conv_w, small, out_shard, up_shard, down_shard, core):
    n_seq, seq_len, _ = x.shape
    n_tok = n_seq * seq_len
    x2 = x.reshape(n_tok, D_MODEL)
    tgt2 = tgt.reshape(n_tok, D_MODEL)
    e_bf, et_bf = _const_maps()

    w_uv = _cols_from_shards(w_in4, 0, _UV_END)
    w_z = _cols_from_shards(w_in4, _UV_END, _Z_END)
    w_xbc = _cols_from_shards(w_in4, _Z_END, _XBC_END)
    w_dt = _pad_lanes(_cols_from_shards(w_in4, _XBC_END, IN_COLS), DT_PAD)

    nw_pre = small["norm_mix_pre"]
    lnw = small["gm_ln_w"].reshape(1, GM_WIDTH)
    lnb = small["gm_ln_b"].reshape(1, GM_WIDTH)
    w_stack = small["gm_w_s"].reshape(N_HEADS * CHUNK, CHUNK)
    w_cat = jnp.transpose(small["gm_w_s"], (1, 0, 2)).reshape(CHUNK, N_HEADS * CHUNK)
    bmap = jnp.repeat(small["gm_b_s"].T, HEAD_DIM, axis=1)
    cw3 = conv_w.reshape(CONV_K, 1, CONV_CH)
    conv_b = small["conv_b"]
    dt_bias = _pad_lanes(small["dt_bias"], DT_PAD)
    a_log = _pad_lanes(small["a_log"], DT_PAD)
    dskip_map = jnp.repeat(small["d_skip"], HEAD_DIM, axis=1)
    ssm_nw = small["ssm_norm_w"]

    p_uv, p_xbc, p_z, p_dt, w_out4 = _inproj_fwd(x2, nw_pre, w_uv, w_xbc, w_z, w_dt,
                                                 carried=_allgather_exchange([out_shard]))
    ssd_consts = (cw3, conv_b, dt_bias, a_log, dskip_map, ssm_nw, e_bf, et_bf)
    ya, yb, yssd, sprev, w_up4, w_down4 = _mixer_fwd(
        p_uv, p_xbc, p_z, p_dt, lnw, lnb, w_cat, bmap, *ssd_consts, n_seq,
        carried=_allgather_exchange([up_shard, down_shard]))
    w_down_b = w_down4.reshape(D_FF, D_MODEL)
    w_out_b = w_out4.reshape(D_MODEL, D_MODEL)
    o, x1, h2 = _outproj_fwd(ya, yb, x2, w_out_b, small["norm_mix_post"], small["norm_ffn_pre"])
    f, dd, dy, loss_acc, d_nffn_post = _mlp_fwd(h2, x1, tgt2, w_up4, w_down_b, small["norm_ffn_post"])

    dup, dx1, d_nffn_pre = _mlp_bwd(dd, f, x1, dy, w_down_b, w_up4, small["norm_ffn_pre"])
    tk = min(DW_TOKENS_PER_STEP, n_tok)
    g_up = _matmul_tn("dw_up", h2, dup, D_MODEL, D_MODEL, tk, stacked=True)
    g_down = _matmul_tn("dw_down", f, dd, 1024, D_MODEL, tk).reshape(N_CHIPS, D_FF // N_CHIPS, D_MODEL)
    got_up, got_down = _pair_exchange("grad_pair_exchange_mlp", [g_up, g_down])
    h_up = _pair_sum(core, g_up, got_up, 256)
    h_down = _pair_sum(core, g_down, got_down, 256)
    do, dya, dyb, d_nmix_post = _outproj_bwd(dx1, o, w_out_b, small["norm_mix_post"])
    g_out_a = _matmul_tn("dw_out_a", ya, do, GM_WIDTH, D_MODEL, tk)
    g_out_b = _matmul_tn("dw_out_b", yb, do, SSM_WIDTH, D_MODEL, tk)
    dp_uv, d_ws, d_bs_t, d_lnw, d_lnb, slab_up = _gmlp_bwd(
        p_uv, dya, lnw, lnb, e_bf, et_bf, w_cat, w_stack, bmap, carried=_chip_exchange([h_up]))
    (dp_xbc, dp_z, dp_dt, d_cw, d_cb, d_dtb, d_alog, d_dsk, d_ssm_nw, slab_down) = _ssd_bwd(
        p_xbc, p_z, p_dt, yssd, sprev, dyb, *ssd_consts, n_seq, carried=_chip_exchange([h_down]))

    early = {
        "gm_ln_w": d_lnw.reshape(N_HEADS, HEAD_DIM), "gm_ln_b": d_lnb.reshape(N_HEADS, HEAD_DIM),
        "gm_w_s": d_ws.reshape(N_HEADS, CHUNK, CHUNK), "gm_b_s": d_bs_t[:, :N_HEADS].T,
        "conv_w": d_cw.reshape(CONV_K, CONV_CH), "conv_b": d_cb, "dt_bias": d_dtb[:, :N_HEADS],
        "a_log": d_alog[:, :N_HEADS], "d_skip": d_dsk[:, :N_HEADS], "ssm_norm_w": d_ssm_nw,
        "norm_mix_post": d_nmix_post, "norm_ffn_pre": d_nffn_pre, "norm_ffn_post": d_nffn_post,
    }
    early_names = tuple(n for n in _SMALL_NAMES if n != "norm_mix_pre")
    packed_early = _pack(early, early_names, tail=loss_acc[0, 0].reshape(1))
    gx, h, d_nmix_pre, all_early = _inproj_bwd(dp_uv, dp_xbc, dp_z, dp_dt, x2, dx1, w_uv, w_xbc, w_z, w_dt, nw_pre,
                                               carried=_device_gather_exchange(packed_early))
    g_uv, all_pre = _matmul_tn("dw_in_uv", h, dp_uv, D_MODEL, 2 * GM_WIDTH, tk,
                               carried=_device_gather_exchange(jnp.pad(d_nmix_pre, ((0, 7), (0, 0)))))
    sum_early = _ordered_sum("small_sum", all_early)
    small_sum = _unpack(sum_early, {n: early[n].shape for n in early_names}, early_names)
    small_sum["norm_mix_pre"] = _ordered_sum("small_sum_pre", all_pre)[:1]
    loss = sum_early.reshape(-1)[sum(early[n].size for n in early_names)]
    g_xbc = _matmul_tn("dw_in_xbc", h, dp_xbc, D_MODEL, CONV_CH, tk)
    g_z = _matmul_tn("dw_in_z", h, dp_z, D_MODEL, SSM_WIDTH, tk)
    g_dt = _matmul_tn("dw_in_dt", h, dp_dt, D_MODEL, DT_PAD, tk)

    g_in = _shards_from_cols([(g_uv, 0, _UV_END), (g_z, _UV_END, _Z_END), (g_xbc, _Z_END, _XBC_END),
                              (g_dt, _XBC_END, IN_COLS)])
    g_out = jnp.concatenate([g_out_a, g_out_b], axis=0).reshape(N_CHIPS, D_MODEL // N_CHIPS, D_MODEL)

    got_in, got_out = _pair_exchange("grad_pair_exchange_mix", [g_in, g_out])
    h_in = _pair_sum(core, g_in, got_in, 256)
    h_out = _pair_sum(core, g_out, got_out, 128)
    slab_in, slab_out = _run_exchange("grad_chip_exchange", _chip_exchange([h_in, h_out]))
    reds = [_chip_sum(core, s, tm) for s, tm in ((slab_in, 256), (slab_out, 128), (slab_up, 256), (slab_down, 256))]
    big_grads = dict(zip(("w_in", "w_out", "w_up", "w_down"), _pair_gather(reds)))

    return loss, gx.reshape(x.shape), big_grads, small_sum


_HBM = pl.BlockSpec(memory_space=pltpu.HBM)


D2D_CHUNKS = 8
ROW_ALIGN = 16


def _row_chunks(rows, n_chunks):
    size = min(max(rows // n_chunks, ROW_ALIGN), rows)
    assert rows % size == 0
    return [(start, size) for start in range(0, rows, size)]


def _position():
    x, y, c = lax.axis_index("x"), lax.axis_index("y"), lax.axis_index("c")
    chips = [(1 - x, y), (x, 1 - y), (1 - x, 1 - y)]
    return x, y, c, chips


def _allgather_exchange(arrs):
    n = len(arrs)

    def copies(ins, outs, send_sems, recv_sems, local_sems):
        x, y, c, chips = _position()
        me = 2 * x + y
        sibling = (x, y, 1 - c)

        def copy(a, k, src, dst, to):
            return pltpu.make_async_remote_copy(src_ref=src, dst_ref=dst, send_sem=send_sems.at[a, k],
                                                recv_sem=recv_sems.at[a, k], device_id=to, device_id_type=MESH)

        def half_rows(a, pc):
            half = ins[a].shape[0] // 2
            return pl.ds(pc * half, half)

        local = [pltpu.make_async_copy(ins[a], outs[a].at[me], local_sems.at[a]) for a in range(n)]
        ici_out = [[copy(a, k, ins[a].at[half_rows(a, c)], outs[a].at[me, half_rows(a, c)], (px, py, c))
                    for k, (px, py) in enumerate(chips)] for a in range(n)]
        return c, chips, sibling, copy, half_rows, local, ici_out

    def start(ins, outs, send_sems, recv_sems, local_sems):
        _, _, _, _, _, local, ici_out = copies(ins, outs, send_sems, recv_sems, local_sems)
        for cp in local:
            cp.start()
        for a in range(n):
            for cp in ici_out[a]:
                cp.start()

    def finish(ins, outs, send_sems, recv_sems, local_sems):
        c, chips, sibling, copy, half_rows, local, ici_out = copies(ins, outs, send_sems, recv_sems, local_sems)
        passed = []
        for a in range(n):
            half = ins[a].shape[0] // 2
            for k, (px, py) in enumerate(chips):
                blk = outs[a].at[2 * px + py, half_rows(a, c)]
                copy(a, k, blk, blk, (px, py, c)).wait_recv()
                for first, size in _row_chunks(half, D2D_CHUNKS):
                    piece = outs[a].at[2 * px + py, pl.ds(c * half + first, size)]
                    copy(a, 3 + k, piece, piece, sibling).start()
                passed.append(copy(a, 3 + k, blk, blk, sibling))
        for a in range(n):
            for k, (px, py) in enumerate(chips):
                blk = outs[a].at[2 * px + py, half_rows(a, 1 - c)]
                copy(a, 3 + k, blk, blk, sibling).wait_recv()
        for a in range(n):
            for cp in ici_out[a]:
                cp.wait_send()
        for cp in passed:
            cp.wait_send()
        for cp in local:
            cp.wait()

    return _Carried(arrs, [_sds((N_CHIPS,) + a.shape, a.dtype) for a in arrs],
                    [pltpu.SemaphoreType.DMA((n, 6)), pltpu.SemaphoreType.DMA((n, 6)), pltpu.SemaphoreType.DMA((n,))],
                    start, finish)


def _run_exchange(name, exchange):
    n_in, n_out = len(exchange.ins), len(exchange.out_shapes)

    def body(*refs):
        ins, outs, sems = refs[:n_in], refs[n_in:n_in + n_out], refs[n_in + n_out:]
        exchange.start(ins, outs, *sems)
        exchange.finish(ins, outs, *sems)

    return pl.pallas_call(
        body, name=name, out_shape=tuple(exchange.out_shapes), in_specs=[_HBM] * n_in,
        out_specs=tuple([_HBM] * n_out), scratch_shapes=exchange.sems,
    )(*exchange.ins)


def _pair_exchange(name, grads):
    n = len(grads)

    def body(*refs):
        ins, got = refs[:n], refs[n:2 * n]
        send_sems, recv_sems = refs[2 * n:]
        x, y, c, _ = _position()
        sibling = (x, y, 1 - c)

        def copy(a, src, dst):
            return pltpu.make_async_remote_copy(src_ref=src, dst_ref=dst, send_sem=send_sems.at[a],
                                                recv_sem=recv_sems.at[a], device_id=sibling, device_id_type=MESH)

        for a in range(n):
            half = ins[a].shape[1] // 2
            for slab in range(N_CHIPS):
                for start, size in _row_chunks(half, D2D_CHUNKS):
                    copy(a, ins[a].at[slab, pl.ds((1 - c) * half + start, size), :],
                         got[a].at[slab, pl.ds(start, size), :]).start()
        for a in range(n):
            half = ins[a].shape[1] // 2
            copy(a, ins[a].at[:, pl.ds((1 - c) * half, half), :], got[a]).wait()

    halves = tuple(_sds((N_CHIPS, g.shape[1] // 2, g.shape[2]), g.dtype) for g in grads)
    return pl.pallas_call(
        body, name=name, out_shape=halves, in_specs=[_HBM] * n, out_specs=tuple([_HBM] * n),
        scratch_shapes=[pltpu.SemaphoreType.DMA((n,)), pltpu.SemaphoreType.DMA((n,))],
    )(*grads)


def _chip_exchange(hsums):
    n = len(hsums)

    def copies(ins, outs, send_sems, recv_sems, local_sems):
        x, y, c, chips = _position()
        me = 2 * x + y
        cps = []
        for a in range(n):
            cps.append(pltpu.make_async_copy(ins[a].at[me], outs[a].at[me], local_sems.at[a]))
            for k, (px, py) in enumerate(chips):
                cps.append(pltpu.make_async_remote_copy(
                    src_ref=ins[a].at[2 * px + py], dst_ref=outs[a].at[me], send_sem=send_sems.at[a, k],
                    recv_sem=recv_sems.at[a, k], device_id=(px, py, c), device_id_type=MESH))
        return cps

    def start(*refs):
        for cp in copies(*refs):
            cp.start()

    def finish(*refs):
        for cp in copies(*refs):
            cp.wait()

    return _Carried(hsums, [_sds(h.shape, h.dtype) for h in hsums],
                    [pltpu.SemaphoreType.DMA((n, 3)), pltpu.SemaphoreType.DMA((n, 3)), pltpu.SemaphoreType.DMA((n,))],
                    start, finish)


def _pair_gather(bufs):
    n = len(bufs)

    def body(*refs):
        outs = refs[n:2 * n]
        send_sems, recv_sems = refs[2 * n:]
        x, y, c, _ = _position()
        sibling = (x, y, 1 - c)

        def copy(a, rows):
            return pltpu.make_async_remote_copy(src_ref=rows, dst_ref=rows, send_sem=send_sems.at[a],
                                                recv_sem=recv_sems.at[a], device_id=sibling, device_id_type=MESH)

        for a in range(n):
            half = outs[a].shape[0] // 2
            for start, size in _row_chunks(half, 2 * D2D_CHUNKS):
                copy(a, outs[a].at[pl.ds(c * half + start, size), :]).start()
        for a in range(n):
            half = outs[a].shape[0] // 2
            copy(a, outs[a].at[pl.ds(c * half, half), :]).wait_send()
            copy(a, outs[a].at[pl.ds((1 - c) * half, half), :]).wait_recv()

    return pl.pallas_call(
        body, name="grad_pair_gather", out_shape=tuple(_sds(b.shape, b.dtype) for b in bufs),
        in_specs=[_HBM] * n, out_specs=tuple([_HBM] * n), input_output_aliases={a: a for a in range(n)},
        scratch_shapes=[pltpu.SemaphoreType.DMA((n,)), pltpu.SemaphoreType.DMA((n,))],
    )(*bufs)


def _device_gather_exchange(packed):
    def copies(ins, outs, send_sems, recv_sems, local_sem):
        (x_ref,), (all_ref,) = ins, outs
        x, y, c, chips = _position()
        me, sibling = (x, y, c), (x, y, 1 - c)

        def slab(px, py, pc):
            return all_ref.at[4 * px + 2 * py + pc]

        def copy(k, block, to, src=None):
            return pltpu.make_async_remote_copy(
                src_ref=slab(*block) if src is None else src, dst_ref=slab(*block), send_sem=send_sems.at[k],
                recv_sem=recv_sems.at[k], device_id=to, device_id_type=MESH)

        mine = pltpu.make_async_copy(x_ref, slab(*me), local_sem)
        first = [copy(0, me, sibling, src=x_ref)]
        first += [copy(1 + j, me, (*chip, c), src=x_ref) for j, chip in enumerate(chips)]
        passed = [copy(4 + j, (*chip, c), sibling) for j, chip in enumerate(chips)]
        return c, chips, me, sibling, copy, mine, first, passed

    def start(ins, outs, send_sems, recv_sems, local_sem):
        _, _, _, _, _, mine, first, _ = copies(ins, outs, send_sems, recv_sems, local_sem)
        mine.start()
        for cp in first:
            cp.start()

    def finish(ins, outs, send_sems, recv_sems, local_sem):
        c, chips, me, sibling, copy, mine, first, passed = copies(ins, outs, send_sems, recv_sems, local_sem)
        for j, chip in enumerate(chips):
            copy(1 + j, (*chip, c), me).wait_recv()
            passed[j].start()
        copy(0, sibling, me).wait_recv()
        for j, chip in enumerate(chips):
            copy(4 + j, (*chip, 1 - c), me).wait_recv()
        for cp in first + passed:
            cp.wait_send()
        mine.wait()

    return _Carried([packed], [_sds((N_DEV,) + packed.shape, F32)],
                    [pltpu.SemaphoreType.DMA((7,)), pltpu.SemaphoreType.DMA((7,)), pltpu.SemaphoreType.DMA],
                    start, finish)


def _ordered_sum(name, slabs):
    _, m_per, n_cols = slabs.shape

    def body(s_ref, o_ref):
        acc = s_ref[0]
        for d in range(1, N_DEV):
            acc = acc + s_ref[d]
        o_ref[...] = acc

    vmem = pl.BlockSpec(memory_space=pltpu.VMEM)
    return pl.pallas_call(body, name=name, out_shape=_sds((m_per, n_cols), F32), in_specs=[vmem], out_specs=vmem)(slabs)


def _pair_sum(core, own, got, tm):
    _, half, cols = got.shape
    nb = half // tm

    def body(c_ref, a_ref, b_ref, o_ref):
        o_ref[...] = (a_ref[...] + b_ref[...]).astype(BF16)

    return pl.pallas_call(
        body, name="grad_pair_sum", out_shape=_sds(got.shape, BF16),
        grid_spec=pltpu.PrefetchScalarGridSpec(
            num_scalar_prefetch=1, grid=(N_CHIPS, nb),
            in_specs=[pl.BlockSpec((None, tm, cols), lambda s, i, c_ref: (s, c_ref[0] * nb + i, 0)),
                      pl.BlockSpec((None, tm, cols), lambda s, i, c_ref: (s, i, 0))],
            out_specs=pl.BlockSpec((None, tm, cols), lambda s, i, c_ref: (s, i, 0))),
        compiler_params=_cparams(2),
    )(core, own, got)


def _chip_sum(core, slabs, tm):
    _, half, cols = slabs.shape
    nb = half // tm

    def body(c_ref, s_ref, o_ref):
        acc = s_ref[0].astype(F32)
        for k in range(1, N_CHIPS):
            acc = acc + s_ref[k].astype(F32)
        o_ref[...] = acc

    return pl.pallas_call(
        body, name="grad_chip_sum", out_shape=_sds((2 * half, cols), F32),
        grid_spec=pltpu.PrefetchScalarGridSpec(
            num_scalar_prefetch=1, grid=(nb,),
            in_specs=[pl.BlockSpec((N_CHIPS, tm, cols), lambda i, c_ref: (0, i, 0))],
            out_specs=pl.BlockSpec((tm, cols), lambda i, c_ref: (c_ref[0] * nb + i, 0))),
        compiler_params=_cparams(1),
    )(core, slabs)


def _adam_math(w, g, m, v):
    m2 = ADAM_B1 * m + (1.0 - ADAM_B1) * g
    v2 = ADAM_B2 * v + (1.0 - ADAM_B2) * (g * g)
    m_hat = m2 / (1.0 - ADAM_B1 ** ADAM_STEP)
    v_hat = v2 / (1.0 - ADAM_B2 ** ADAM_STEP)
    delta = -ADAM_LR * (m_hat / (jnp.sqrt(v_hat) + ADAM_EPS) + ADAM_WD * w)
    return delta, m2, v2


def _adamw(name, w, g, m, v, tm):
    def body(w_ref, g_ref, m_ref, v_ref, d_ref, m2_ref, v2_ref):
        d, m2, v2 = _adam_math(w_ref[...], g_ref[...], m_ref[...], v_ref[...])
        d_ref[...] = d
        m2_ref[...] = m2
        v2_ref[...] = v2

    return _rows_call(name, body, tm, [w, g, m, v], [], [_sds(w.shape, F32)] * 3)


_SMALL_NAMES = ("norm_mix_pre", "gm_ln_w", "gm_ln_b", "gm_w_s", "gm_b_s", "conv_w", "conv_b", "dt_bias", "a_log",
                "d_skip", "ssm_norm_w", "norm_mix_post", "norm_ffn_pre", "norm_ffn_post")
_PACK_COLS = 1024


def _pack(parts, names=_SMALL_NAMES, tail=None):
    pieces = [parts[n].reshape(-1) for n in names]
    flat = jnp.concatenate(pieces if tail is None else pieces + [tail])
    rows = -(-flat.shape[0] // (8 * _PACK_COLS)) * 8
    flat = jnp.pad(flat, (0, rows * _PACK_COLS - flat.shape[0]))
    return flat.reshape(rows, _PACK_COLS)


def _unpack(packed, shapes, names=_SMALL_NAMES):
    flat = packed.reshape(-1)
    out, off = {}, 0
    for n in names:
        size = 1
        for s in shapes[n]:
            size *= s
        out[n] = flat[off:off + size].reshape(shapes[n])
        off += size
    return out


def kernel(x, norm_mix_pre, w_in, gm_ln_w, gm_ln_b, gm_w_s, gm_b_s, conv_w, conv_b, dt_bias, a_log, d_skip, ssm_norm_w, w_out, norm_mix_post, norm_ffn_pre, w_up, w_down, norm_ffn_post, loss_target, m_norm_mix_pre, m_w_in, m_gm_ln_w, m_gm_ln_b, m_gm_w_s, m_gm_b_s, m_conv_w, m_conv_b, m_dt_bias, m_a_log, m_d_skip, m_ssm_norm_w, m_w_out, m_norm_mix_post, m_norm_ffn_pre, m_w_up, m_w_down, m_norm_ffn_post, v_norm_mix_pre, v_w_in, v_gm_ln_w, v_gm_ln_b, v_gm_w_s, v_gm_b_s, v_conv_w, v_conv_b, v_dt_bias, v_a_log, v_d_skip, v_ssm_norm_w, v_w_out, v_norm_mix_post, v_norm_ffn_pre, v_w_up, v_w_down, v_norm_ffn_post):
    params = dict(norm_mix_pre=norm_mix_pre, w_in=w_in, gm_ln_w=gm_ln_w, gm_ln_b=gm_ln_b, gm_w_s=gm_w_s, gm_b_s=gm_b_s,
                  conv_w=conv_w, conv_b=conv_b, dt_bias=dt_bias, a_log=a_log, d_skip=d_skip, ssm_norm_w=ssm_norm_w,
                  w_out=w_out, norm_mix_post=norm_mix_post, norm_ffn_pre=norm_ffn_pre, w_up=w_up, w_down=w_down,
                  norm_ffn_post=norm_ffn_post)
    mom1 = dict(norm_mix_pre=m_norm_mix_pre, w_in=m_w_in, gm_ln_w=m_gm_ln_w, gm_ln_b=m_gm_ln_b, gm_w_s=m_gm_w_s,
                gm_b_s=m_gm_b_s, conv_w=m_conv_w, conv_b=m_conv_b, dt_bias=m_dt_bias, a_log=m_a_log, d_skip=m_d_skip,
                ssm_norm_w=m_ssm_norm_w, w_out=m_w_out, norm_mix_post=m_norm_mix_post, norm_ffn_pre=m_norm_ffn_pre,
                w_up=m_w_up, w_down=m_w_down, norm_ffn_post=m_norm_ffn_post)
    mom2 = dict(norm_mix_pre=v_norm_mix_pre, w_in=v_w_in, gm_ln_w=v_gm_ln_w, gm_ln_b=v_gm_ln_b, gm_w_s=v_gm_w_s,
                gm_b_s=v_gm_b_s, conv_w=v_conv_w, conv_b=v_conv_b, dt_bias=v_dt_bias, a_log=v_a_log, d_skip=v_d_skip,
                ssm_norm_w=v_ssm_norm_w, w_out=v_w_out, norm_mix_post=v_norm_mix_post, norm_ffn_pre=v_norm_ffn_pre,
                w_up=v_w_up, w_down=v_w_down, norm_ffn_post=v_norm_ffn_post)
    names = list(params)
    big = ("w_in", "w_out", "w_up", "w_down")
    chip = 2 * lax.axis_index("x") + lax.axis_index("y")

    shards = {n: params[n][0].astype(BF16) for n in big}
    conv_shard = jnp.pad(conv_w[0], ((0, 16 - CONV_K), (0, 0)))
    g_in4, g_conv4 = _run_exchange("allgather_w_in", _allgather_exchange([shards["w_in"], conv_shard]))
    conv_full = jnp.transpose(g_conv4[:, :CONV_K, :], (1, 0, 2)).reshape(CONV_K, CONV_CH)

    small = {n: params[n][0] if params[n].ndim >= 3 else params[n] for n in _SMALL_NAMES if n != "conv_w"}
    core = lax.axis_index("c").astype(jnp.int32).reshape(1)
    loss, grad_x, big_grads, small_sum = _forward_backward(
        x, loss_target, g_in4, conv_full, small, shards["w_out"], shards["w_up"], shards["w_down"], core)

    small_sum["conv_w"] = lax.dynamic_slice_in_dim(small_sum["conv_w"], chip * (CONV_CH // N_CHIPS), CONV_CH // N_CHIPS, axis=1)

    grads, delta, new_m, new_v = {}, {}, {}, {}
    for n, tm in zip(big, (256, 128, 256, 256)):
        g = big_grads[n]
        d, m2, v2 = _adamw("adamw_" + n, params[n][0], g, mom1[n][0], mom2[n][0], tm)
        grads[n], delta[n], new_m[n], new_v[n] = g[None], d[None], m2[None], v2[None]
    local_shapes = {n: params[n].shape[1:] if params[n].ndim >= 3 else params[n].shape for n in _SMALL_NAMES}
    flat = lambda tree: {n: tree[n].reshape(local_shapes[n]) for n in _SMALL_NAMES}
    packed = [_pack(flat(t)) for t in (params, small_sum, mom1, mom2)]
    d_p, m_p, v_p = _adamw("adamw_small", *packed, packed[0].shape[0])
    for src, dst in ((d_p, delta), (m_p, new_m), (v_p, new_v)):
        for n, val in _unpack(src, local_shapes).items():
            dst[n] = val.reshape(params[n].shape)
    for n in _SMALL_NAMES:
        grads[n] = small_sum[n].reshape(params[n].shape)

    out = [loss, grad_x]
    for tree in (grads, delta, new_m, new_v):
        out += [tree[n] for n in names]
    return tuple(out)
```

```python
import functools

import jax
import jax.numpy as jnp
from jax import lax
from jax.experimental import pallas as pl
from jax.experimental.pallas import tpu as pltpu

F32 = jnp.float32
BF16 = jnp.bfloat16
HI = lax.Precision.HIGHEST
MESH = pl.DeviceIdType.MESH

EPS = 1e-6
D_MODEL = 1024
GM_WIDTH = 512
SSM_WIDTH = 512
N_HEADS = 8
HEAD_DIM = 64
CHUNK = 128
SSM_GROUPS = 2
GROUP_W = SSM_WIDTH // SSM_GROUPS
SSM_STATE = 128
CONV_K = 4
CONV_CH = 1024
D_FF = 4096
IN_COLS = 2568
DT_PAD = 128
N_CHIPS = 4
N_DEV = 8

ADAM_LR = 0.001
ADAM_B1 = 0.9
ADAM_B2 = 0.999
ADAM_EPS = 1e-08
ADAM_WD = 0.01
ADAM_STEP = 10

VMEM_LIMIT_BYTES = 56 * 1024 * 1024
FF_TILE = 512
DW_TOKENS_PER_STEP = 2048


def _cparams(n_axes):
    return pltpu.CompilerParams(dimension_semantics=("arbitrary",) * n_axes, vmem_limit_bytes=VMEM_LIMIT_BYTES)


def _dot(a, b):
    return jnp.dot(a.astype(BF16), b.astype(BF16), preferred_element_type=F32)


def _dot_nt(a, b):
    return lax.dot_general(a.astype(BF16), b.astype(BF16), (((1,), (1,)), ((), ())), preferred_element_type=F32)


def _dot_tn(a, b):
    return lax.dot_general(a.astype(BF16), b.astype(BF16), (((0,), (0,)), ((), ())), preferred_element_type=F32)


def _sigmoid(x):
    return 1.0 / (1.0 + jnp.exp(-x))


_GELU_C = 0.7978845608028654
_GELU_A = 0.044715


def _gelu(x):
    t = jnp.tanh(_GELU_C * (x + _GELU_A * (x * x * x)))
    return 0.5 * x * (1.0 + t), t


def _gelu_grad(x, t):
    return 0.5 * (1.0 + t) + 0.5 * x * (1.0 - t * t) * (_GELU_C * (1.0 + 3.0 * _GELU_A * x * x))


def _rms_fwd(x, w):
    r = lax.rsqrt(jnp.mean(x * x, axis=-1, keepdims=True) + EPS)
    return x * r * w, r


def _rms_bwd(x, r, w, dy):
    g = dy * w
    dx = r * g - x * (r * r * r) * jnp.mean(g * x, axis=-1, keepdims=True)
    dw = jnp.sum(dy * x * r, axis=0, keepdims=True)
    return dx, dw


class _Carried:
    def __init__(self, ins, out_shapes, sems, start, finish):
        self.ins, self.out_shapes, self.sems, self.start, self.finish = list(ins), list(out_shapes), list(sems), start, finish


def _split_carried(refs, n_in, n_out, n_scratch, carried):
    n_ci, n_co, n_cs = len(carried.ins), len(carried.out_shapes), len(carried.sems)
    ins, rest = refs[:n_in], refs[n_in:]
    c_ins, rest = rest[:n_ci], rest[n_ci:]
    outs, rest = rest[:n_out], rest[n_out:]
    c_outs, rest = rest[:n_co], rest[n_co:]
    scr, c_sems = rest[:n_scratch], rest[n_scratch:]
    assert len(c_sems) == n_cs
    return tuple(ins) + tuple(outs) + tuple(scr), c_ins, c_outs, c_sems


def _rows_call(name, body, tm, row_ins, const_ins, row_outs, acc_outs=(), scratch=(), carried=None):
    n_rows = row_ins[0].shape[0]
    assert n_rows % tm == 0
    n_steps = n_rows // tm
    n_in = len(row_ins) + len(const_ins)
    n_ro = len(row_outs)
    n_acc = len(acc_outs)

    def kern(*refs):
        accs = refs[n_in + n_ro:n_in + n_ro + n_acc]

        @pl.when(pl.program_id(0) == 0)
        def _():
            for a in accs:
                a[...] = jnp.zeros_like(a)

        body(*refs)

    def whole(shape):
        nd = len(shape)
        return pl.BlockSpec(tuple(shape), lambda i: (0,) * nd)

    in_specs = [pl.BlockSpec((tm, a.shape[1]), lambda i: (i, 0)) for a in row_ins]
    in_specs += [whole(a.shape) for a in const_ins]
    out_specs = [pl.BlockSpec((tm, s.shape[1]), lambda i: (i, 0)) for s in row_outs]
    out_specs += [whole(s.shape) for s in acc_outs]
    return _call_carrying(
        kern, carried, name=name, grid=(n_steps,), in_specs=in_specs, out_specs=out_specs,
        out_shape=tuple(row_outs) + tuple(acc_outs), scratch_shapes=list(scratch), operands=list(row_ins) + list(const_ins))


def _call_carrying(body, carried, *, name, grid, in_specs, out_specs, out_shape, scratch_shapes, operands):
    n_in, n_out, n_scratch = len(in_specs), len(out_specs), len(scratch_shapes)
    kern = body
    if carried is not None:
        def kern(*refs):
            plain, c_ins, c_outs, c_sems = _split_carried(refs, n_in, n_out, n_scratch, carried)
            first, last = True, True
            for d, size in enumerate(grid):
                first = jnp.logical_and(first, pl.program_id(d) == 0)
                last = jnp.logical_and(last, pl.program_id(d) == size - 1)

            @pl.when(first)
            def _():
                carried.start(c_ins, c_outs, *c_sems)

            body(*plain)

            @pl.when(last)
            def _():
                carried.finish(c_ins, c_outs, *c_sems)

        in_specs = list(in_specs) + [_HBM] * len(carried.ins)
        out_specs = list(out_specs) + [_HBM] * len(carried.out_shapes)
        out_shape = tuple(out_shape) + tuple(carried.out_shapes)
        operands = list(operands) + carried.ins
        scratch_shapes = list(scratch_shapes) + carried.sems
    return pl.pallas_call(
        kern, name=name, grid=grid, in_specs=in_specs, out_specs=out_specs, out_shape=out_shape,
        scratch_shapes=scratch_shapes, compiler_params=_cparams(len(grid)),
    )(*operands)


def _sds(shape, dtype):
    return jax.ShapeDtypeStruct(tuple(shape), dtype)


def _matmul_tn(name, a, b, tm, tn, tk, stacked=False, carried=None):
    k_dim, m_dim = a.shape
    n_dim = b.shape[1]
    assert m_dim % tm == 0 and n_dim % tn == 0 and k_dim % tk == 0

    def kern(a_ref, b_ref, o_ref):
        @pl.when(pl.program_id(2) == 0)
        def _():
            o_ref[...] = jnp.zeros_like(o_ref)

        o_ref[...] += _dot_tn(a_ref[...], b_ref[...])

    if stacked:
        assert tm == m_dim
        out_shape = _sds((n_dim // tn, m_dim, tn), F32)
        out_spec = pl.BlockSpec((None, tm, tn), lambda i, j, k: (j, i, 0))
    else:
        out_shape = _sds((m_dim, n_dim), F32)
        out_spec = pl.BlockSpec((tm, tn), lambda i, j, k: (i, j))
    outs = _call_carrying(
        kern, carried, name=name, grid=(m_dim // tm, n_dim // tn, k_dim // tk),
        in_specs=[pl.BlockSpec((tk, tm), lambda i, j, k: (k, i)), pl.BlockSpec((tk, tn), lambda i, j, k: (k, j))],
        out_specs=[out_spec], out_shape=(out_shape,), scratch_shapes=[], operands=[a, b])
    return outs[0] if carried is None else outs


def _inproj_fwd(x, nw, w_uv, w_xbc, w_z, w_dt, tm=256, carried=None):
    n_tok = x.shape[0]

    def body(x_ref, nw_ref, wuv_ref, wxbc_ref, wz_ref, wdt_ref, puv_ref, pxbc_ref, pz_ref, pdt_ref):
        h, _ = _rms_fwd(x_ref[...], nw_ref[...])
        h = h.astype(BF16)
        puv_ref[...] = jnp.dot(h, wuv_ref[...], preferred_element_type=F32)
        pxbc_ref[...] = jnp.dot(h, wxbc_ref[...], preferred_element_type=F32)
        pz_ref[...] = jnp.dot(h, wz_ref[...], preferred_element_type=F32)
        pdt_ref[...] = jnp.dot(h, wdt_ref[...], preferred_element_type=F32)

    return _rows_call(
        "inproj_fwd", body, tm, [x], [nw, w_uv, w_xbc, w_z, w_dt],
        [_sds((n_tok, 2 * GM_WIDTH), F32), _sds((n_tok, CONV_CH), F32), _sds((n_tok, SSM_WIDTH), F32),
         _sds((n_tok, DT_PAD), F32)], carried=carried)


def _head_lane_mask(width, head):
    lane = lax.broadcasted_iota(jnp.int32, (1, width), 1)
    return (lane // HEAD_DIM) == head


def _split_terms(x, terms):
    parts = []
    for _ in range(terms):
        p = x.astype(BF16)
        parts.append(p)
        x = x - p.astype(F32)
    return parts


def _seg_dots(vals, ind, terms=2):
    m = vals[0].shape[0]
    parts = []
    for v in vals:
        parts += _split_terms(v, terms)
    red = jnp.dot(jnp.concatenate(parts, axis=0), ind, preferred_element_type=F32)
    outs = []
    for i in range(len(vals)):
        acc = red[i * terms * m:(i * terms + 1) * m]
        for t in range(1, terms):
            acc = acc + red[(i * terms + t) * m:(i * terms + t + 1) * m]
        outs.append(acc)
    return outs


def _tri_dot(mask, x, terms=3):
    n = x.shape[1]
    red = jnp.dot(mask.astype(BF16), jnp.concatenate(_split_terms(x, terms), axis=1), preferred_element_type=F32)
    acc = red[:, :n]
    for t in range(1, terms):
        acc = acc + red[:, t * n:(t + 1) * n]
    return acc


def _gmlp_common(puv, lnw, lnb, e_bf, et_bf):
    u = puv[:, :GM_WIDTH]
    v = puv[:, GM_WIDTH:]
    gu, tu = _gelu(u)
    gv, tv = _gelu(v)
    (s1,) = _seg_dots([gv], et_bf)
    (mu,) = _seg_dots([s1 * (1.0 / HEAD_DIM)], e_bf)
    xc = gv - mu
    (s2,) = _seg_dots([xc * xc], et_bf)
    (rstd,) = _seg_dots([lax.rsqrt(s2 * (1.0 / HEAD_DIM) + EPS)], e_bf)
    xhat = xc * rstd
    vn = xhat * lnw + lnb
    return u, v, gu, tu, tv, rstd, xhat, vn


def _tril_mask():
    r = lax.broadcasted_iota(jnp.int32, (CHUNK, CHUNK), 0)
    c = lax.broadcasted_iota(jnp.int32, (CHUNK, CHUNK), 1)
    return r >= c


def _head_blocks(v):
    return jnp.concatenate([jnp.where(_head_lane_mask(GM_WIDTH, h), v, jnp.zeros_like(v)) for h in range(N_HEADS)], axis=0)


def _causal_w_cat(w_cat):
    t = lax.broadcasted_iota(jnp.int32, (CHUNK, N_HEADS * CHUNK), 0)
    s = lax.broadcasted_iota(jnp.int32, (CHUNK, N_HEADS * CHUNK), 1) % CHUNK
    return jnp.where(t >= s, w_cat, 0.0).astype(BF16)


def _gmlp_chunk_fwd(puv, lnw, lnb, e_bf, et_bf, wm, bmap):
    _, _, gu, _, _, _, _, vn = _gmlp_common(puv, lnw, lnb, e_bf, et_bf)
    mixed = jnp.dot(wm, _head_blocks(vn.astype(BF16)), preferred_element_type=F32) + bmap
    return (gu * mixed).astype(BF16)


def _ssd_pre(xr, prev, cw_ref, cb, pdt, dtb, alog, emap):
    rowi = lax.broadcasted_iota(jnp.int32, (CHUNK, 1), 0)

    def down(s):
        return jnp.where(rowi < s, pltpu.roll(prev, s, 0), pltpu.roll(xr, s, 0))

    shifted = [down(3), down(2), down(1), xr]
    xc = cb
    for k in range(CONV_K):
        xc = xc + cw_ref[k] * shifted[k]
    sg = _sigmoid(xc)
    xa = xc * sg
    pre = pdt + dtb
    dt = jnp.maximum(pre, 0.0) + jnp.log(1.0 + jnp.exp(-jnp.abs(pre)))
    a_neg = -jnp.exp(alog)
    a_cs = _tri_dot(_tril_mask(), dt * a_neg)
    acs_map, dt_map = _seg_dots([a_cs, dt], emap, terms=3)
    return dict(shifted=shifted, xc=xc, sg=sg, xa=xa, pre=pre, dt=dt, a_neg=a_neg, a_cs=a_cs,
                acs_map=acs_map, dt_map=dt_map, rowi=rowi)


def _ssd_maps(p):
    last = p["rowi"] == CHUNK - 1
    aq_map = jnp.sum(jnp.where(last, p["acs_map"], 0.0), axis=0, keepdims=True)
    e_exp = jnp.exp(p["acs_map"])
    dte = jnp.exp(aq_map - p["acs_map"])
    cd = jnp.exp(aq_map)
    return last, e_exp, dte, cd


def _head_decay(a_cs, a_cs_t, head, tri):
    lane = lax.broadcasted_iota(jnp.int32, (1, DT_PAD), 1)
    sub = lax.broadcasted_iota(jnp.int32, (DT_PAD, 1), 0)
    col = jnp.sum(jnp.where(lane == head, a_cs, 0.0), axis=1, keepdims=True)
    row = jnp.sum(jnp.where(sub == head, a_cs_t, 0.0), axis=0, keepdims=True)
    return jnp.exp(jnp.where(tri, col - row, -1e30))


def _gate_fwd(y, z, nw):
    sz = _sigmoid(z)
    zg = z * sz
    yg = y * zg
    outs, rs = [], []
    for g in range(SSM_GROUPS):
        gs = slice(g * GROUP_W, (g + 1) * GROUP_W)
        o, r = _rms_fwd(yg[:, gs], nw[:, gs])
        outs.append(o)
        rs.append(r)
    return sz, zg, yg, outs, rs


def _ssd_const_specs():
    def whole(shape):
        nd = len(shape)
        return pl.BlockSpec(tuple(shape), lambda b, c: (0,) * nd)
    return [whole((CONV_K, 1, CONV_CH)), whole((1, CONV_CH)), whole((1, DT_PAD)), whole((1, DT_PAD)),
            whole((1, SSM_WIDTH)), whole((1, SSM_WIDTH)), whole((DT_PAD, SSM_WIDTH)), whole((SSM_WIDTH, DT_PAD))]


def _mixer_fwd(p_uv, p_xbc, p_z, p_dt, lnw, lnb, w_cat, bmap, conv_w, conv_b, dt_bias, a_log, dskip_map, norm_w,
               e_bf, et_bf, n_seq, carried=None):
    n_tok = p_xbc.shape[0]
    nc = n_tok // n_seq // CHUNK

    def body(puv_ref, xr_ref, z_ref, pdt_ref, lnw_ref, lnb_ref, wcat_ref, bmap_ref,
             cw_ref, cb_ref, dtb_ref, alog_ref, dsk_ref, nw_ref, e_ref, et_ref,
             ya_ref, yb_ref, yssd_ref, sprev_ref, wm_scr, prev_scr, s_scr):
        @pl.when(jnp.logical_and(pl.program_id(0) == 0, pl.program_id(1) == 0))
        def _():
            wm_scr[...] = _causal_w_cat(wcat_ref[...])

        @pl.when(pl.program_id(1) == 0)
        def _():
            prev_scr[...] = jnp.zeros_like(prev_scr)
            s_scr[...] = jnp.zeros_like(s_scr)

        ya_ref[...] = _gmlp_chunk_fwd(puv_ref[...], lnw_ref[...], lnb_ref[...], e_ref[...], et_ref[...], wm_scr[...],
                                      bmap_ref[...])
        xr = xr_ref[...]
        p = _ssd_pre(xr, prev_scr[...], cw_ref, cb_ref[...], pdt_ref[...], dtb_ref[...], alog_ref[...], e_ref[...])
        _, e_exp, dte, cd = _ssd_maps(p)
        xs = p["xa"][:, :SSM_WIDTH]
        xd = xs * p["dt_map"]
        a_cs_t = p["a_cs"].T
        tri = _tril_mask()
        s_old = s_scr[...]
        sprev_ref[...] = s_old
        for g in range(SSM_GROUPS):
            gs = slice(g * GROUP_W, (g + 1) * GROUP_W)
            bm = p["xa"][:, SSM_WIDTH + g * SSM_STATE: SSM_WIDTH + (g + 1) * SSM_STATE].astype(BF16)
            cm = p["xa"][:, SSM_WIDTH + (SSM_GROUPS + g) * SSM_STATE: SSM_WIDTH + (SSM_GROUPS + g + 1) * SSM_STATE].astype(BF16)
            cb_mat = _dot_nt(cm, bm)
            xdg = xd[:, gs].astype(BF16)
            y_g = _dot(cm, s_old[:, gs]) * e_exp[:, gs] + dsk_ref[:, gs] * xs[:, gs]
            for r in range(SSM_GROUPS * 2):
                dm = _head_decay(p["a_cs"], a_cs_t, g * 4 + r, tri)
                full = jnp.dot((cb_mat * dm).astype(BF16), xdg, preferred_element_type=F32)
                y_g = y_g + jnp.where(_head_lane_mask(GROUP_W, r), full, 0.0)
            yssd_ref[:, gs] = y_g
            s_scr[:, gs] = cd[:, gs] * s_old[:, gs] + _dot_tn(bm, xd[:, gs] * dte[:, gs])
        _, _, _, outs, _ = _gate_fwd(yssd_ref[...], z_ref[...], nw_ref[...])
        for g in range(SSM_GROUPS):
            yb_ref[:, g * GROUP_W:(g + 1) * GROUP_W] = outs[g].astype(BF16)
        prev_scr[...] = xr

    def rows(width):
        return pl.BlockSpec((CHUNK, width), lambda b, c: (b * nc + c, 0))

    def whole(shape):
        nd = len(shape)
        return pl.BlockSpec(tuple(shape), lambda b, c: (0,) * nd)

    return _call_carrying(
        body, carried, name="mixer_fwd", grid=(n_seq, nc),
        in_specs=[rows(2 * GM_WIDTH), rows(CONV_CH), rows(SSM_WIDTH), rows(DT_PAD), whole(lnw.shape), whole(lnb.shape),
                  whole(w_cat.shape), whole(bmap.shape)] + _ssd_const_specs(),
        out_specs=[rows(GM_WIDTH), rows(SSM_WIDTH), rows(SSM_WIDTH), rows(SSM_WIDTH)],
        out_shape=(_sds((n_tok, GM_WIDTH), BF16), _sds((n_tok, SSM_WIDTH), BF16), _sds((n_tok, SSM_WIDTH), F32),
                   _sds((n_tok, SSM_WIDTH), F32)),
        scratch_shapes=[pltpu.VMEM((CHUNK, N_HEADS * CHUNK), BF16), pltpu.VMEM((CHUNK, CONV_CH), F32),
                        pltpu.VMEM((SSM_STATE, SSM_WIDTH), F32)],
        operands=[p_uv, p_xbc, p_z, p_dt, lnw, lnb, w_cat, bmap, conv_w, conv_b, dt_bias, a_log, dskip_map, norm_w, e_bf,
                  et_bf])


def _outproj_fwd(ya, yb, x, w_out, nw_post, nw_pre2, tm=256):
    n_tok = x.shape[0]

    def body(ya_ref, yb_ref, x_ref, wo_ref, nwa_ref, nwb_ref, o_ref, x1_ref, h2_ref):
        o = jnp.dot(ya_ref[...], wo_ref[:GM_WIDTH, :], preferred_element_type=F32)
        o = o + jnp.dot(yb_ref[...], wo_ref[GM_WIDTH:, :], preferred_element_type=F32)
        on, _ = _rms_fwd(o, nwa_ref[...])
        x1 = x_ref[...] + on
        h2, _ = _rms_fwd(x1, nwb_ref[...])
        o_ref[...] = o
        x1_ref[...] = x1
        h2_ref[...] = h2.astype(BF16)

    return _rows_call("outproj_fwd", body, tm, [ya, yb, x], [w_out, nw_post, nw_pre2],
                      [_sds((n_tok, D_MODEL), F32), _sds((n_tok, D_MODEL), F32), _sds((n_tok, D_MODEL), BF16)])


def _up_cols(wup_ref, j):
    per = (D_FF // N_CHIPS) // FF_TILE
    return wup_ref[j // per, :, (j % per) * FF_TILE:(j % per + 1) * FF_TILE]


def _mlp_fwd(h2, x1, tgt, w_up, w_down, nw, tm=256):
    n_tok = x1.shape[0]

    def body(h2_ref, x1_ref, tgt_ref, wup_ref, wdn_ref, nw_ref, f_ref, dd_ref, dy_ref, loss_ref, dnw_ref):
        h2v = h2_ref[...]
        acc = jnp.zeros((tm, D_MODEL), F32)
        for j in range(D_FF // FF_TILE):
            cs = slice(j * FF_TILE, (j + 1) * FF_TILE)
            u = jnp.dot(h2v, _up_cols(wup_ref, j), preferred_element_type=F32)
            f = jnp.square(jnp.maximum(u, 0.0)).astype(BF16)
            f_ref[:, cs] = f
            acc = acc + jnp.dot(f, wdn_ref[cs, :], preferred_element_type=F32)
        dn, r = _rms_fwd(acc, nw_ref[...])
        e = x1_ref[...] + dn - tgt_ref[...]
        loss_ref[...] += jnp.full(loss_ref.shape, (0.5 / D_MODEL) * jnp.sum(e * e), F32)
        dy = e * (1.0 / D_MODEL)
        dd, dnw = _rms_bwd(acc, r, nw_ref[...], dy)
        dy_ref[...] = dy
        dd_ref[...] = dd.astype(BF16)
        dnw_ref[...] += dnw

    return _rows_call(
        "mlp_fwd", body, tm, [h2, x1, tgt], [w_up, w_down, nw],
        [_sds((n_tok, D_FF), BF16), _sds((n_tok, D_MODEL), BF16), _sds((n_tok, D_MODEL), F32)],
        [_sds((8, 128), F32), _sds((1, D_MODEL), F32)])


def _mlp_bwd(dd, f, x1, dy, w_down, w_up, nw, tm=256):
    n_tok = x1.shape[0]

    def body(dd_ref, f_ref, x1_ref, dy_ref, wdn_ref, wup_ref, nw_ref, dup_ref, dx1_ref, dnw_ref):
        ddv = dd_ref[...]
        acc = jnp.zeros((tm, D_MODEL), F32)
        for j in range(D_FF // FF_TILE):
            cs = slice(j * FF_TILE, (j + 1) * FF_TILE)
            df = _dot_nt(ddv, wdn_ref[cs, :])
            du = (df * (2.0 * jnp.sqrt(f_ref[:, cs].astype(F32)))).astype(BF16)
            dup_ref[:, cs] = du
            acc = acc + _dot_nt(du, _up_cols(wup_ref, j))
        x1v = x1_ref[...]
        _, r = _rms_fwd(x1v, nw_ref[...])
        dx, dnw = _rms_bwd(x1v, r, nw_ref[...], acc)
        dx1_ref[...] = dy_ref[...] + dx
        dnw_ref[...] += dnw

    return _rows_call("mlp_bwd", body, tm, [dd, f, x1, dy], [w_down, w_up, nw],
                      [_sds((n_tok, D_FF), BF16), _sds((n_tok, D_MODEL), F32)], [_sds((1, D_MODEL), F32)])


def _outproj_bwd(dx1, o, w_out, nw, tm=256):
    n_tok = dx1.shape[0]

    def body(dx1_ref, o_ref, wo_ref, nw_ref, do_ref, dya_ref, dyb_ref, dnw_ref):
        ov = o_ref[...]
        _, r = _rms_fwd(ov, nw_ref[...])
        do, dnw = _rms_bwd(ov, r, nw_ref[...], dx1_ref[...])
        dob = do.astype(BF16)
        do_ref[...] = dob
        dya_ref[...] = _dot_nt(dob, wo_ref[:GM_WIDTH, :])
        dyb_ref[...] = _dot_nt(dob, wo_ref[GM_WIDTH:, :])
        dnw_ref[...] += dnw

    return _rows_call("outproj_bwd", body, tm, [dx1, o], [w_out, nw],
                      [_sds((n_tok, D_MODEL), BF16), _sds((n_tok, GM_WIDTH), F32), _sds((n_tok, SSM_WIDTH), F32)],
                      [_sds((1, D_MODEL), F32)])


def _gmlp_bwd(p_uv, dya, lnw, lnb, e_bf, et_bf, w_cat, w_stack, bmap, carried=None):
    n_tok = p_uv.shape[0]

    def body(puv_ref, dya_ref, lnw_ref, lnb_ref, e_ref, et_ref, wcat_ref, wstack_ref, bmap_ref,
             dpuv_ref, dws_ref, dbs_ref, dlnw_ref, dlnb_ref, wm_scr, wsm_scr):
        t_stk = lax.broadcasted_iota(jnp.int32, (N_HEADS * CHUNK, CHUNK), 0) % CHUNK
        s_stk = lax.broadcasted_iota(jnp.int32, (N_HEADS * CHUNK, CHUNK), 1)

        @pl.when(pl.program_id(0) == 0)
        def _():
            wm_scr[...] = _causal_w_cat(wcat_ref[...])
            wsm_scr[...] = jnp.where(t_stk >= s_stk, wstack_ref[...], 0.0).astype(BF16)

        lnw_v = lnw_ref[...]
        e_v, et_v = e_ref[...], et_ref[...]
        u, v, gu, tu, tv, rstd, xhat, vn = _gmlp_common(puv_ref[...], lnw_v, lnb_ref[...], e_v, et_v)
        vnb = vn.astype(BF16)
        mixed = jnp.dot(wm_scr[...], _head_blocks(vnb), preferred_element_type=F32) + bmap_ref[...]
        dy = dya_ref[...]
        du = dy * mixed * _gelu_grad(u, tu)
        dmixed = dy * gu
        (dbs,) = _seg_dots([dmixed], et_v)
        dbs_ref[...] += dbs
        dblocks = _head_blocks(dmixed.astype(BF16))
        dvn = lax.dot_general(wsm_scr[...], dblocks, (((0,), (0,)), ((), ())), preferred_element_type=F32)
        dws = lax.dot_general(dblocks, vnb, (((1,), (1,)), ((), ())), preferred_element_type=F32)
        dws_ref[...] += jnp.where(t_stk >= s_stk, dws, 0.0)
        dlnw_ref[...] += jnp.sum(dvn * xhat, axis=0, keepdims=True)
        dlnb_ref[...] += jnp.sum(dvn, axis=0, keepdims=True)
        dxh = dvn * lnw_v
        m1, m2 = _seg_dots([dxh, dxh * xhat], et_v)
        m1, m2 = _seg_dots([m1 * (1.0 / HEAD_DIM), m2 * (1.0 / HEAD_DIM)], e_v)
        dgv = rstd * (dxh - m1 - xhat * m2)
        dv = dgv * _gelu_grad(v, tv)
        dpuv_ref[:, :GM_WIDTH] = du.astype(BF16)
        dpuv_ref[:, GM_WIDTH:] = dv.astype(BF16)

    return _rows_call(
        "gmlp_bwd", body, CHUNK, [p_uv, dya], [lnw, lnb, e_bf, et_bf, w_cat, w_stack, bmap],
        [_sds((n_tok, 2 * GM_WIDTH), BF16)],
        [_sds((N_HEADS * CHUNK, CHUNK), F32), _sds((CHUNK, DT_PAD), F32), _sds((1, GM_WIDTH), F32),
         _sds((1, GM_WIDTH), F32)],
        scratch=[pltpu.VMEM((CHUNK, N_HEADS * CHUNK), BF16), pltpu.VMEM((N_HEADS * CHUNK, CHUNK), BF16)],
        carried=carried)


def _ssd_bwd(p_xbc, p_z, p_dt, yssd, sprev, dyb, conv_w, conv_b, dt_bias, a_log, dskip_map, norm_w, e_bf, et_bf, n_seq,
             carried=None):
    n_tok = p_xbc.shape[0]
    nc = n_tok // n_seq // CHUNK

    def body(xr_ref, xprev_ref, z_ref, pdt_ref, yssd_ref, sprev_ref, dyb_ref,
             cw_ref, cb_ref, dtb_ref, alog_ref, dsk_ref, nw_ref, e_ref, et_ref,
             dpxbc_ref, dpz_ref, dpdt_ref, dcw_ref, dcb_ref, ddtb_ref, dalog_ref, ddsk_ref, dnw_ref,
             ds_scr, nxt_scr, dxa_scr):
        step = pl.program_id(1)
        first = jnp.logical_and(pl.program_id(0) == 0, step == 0)

        @pl.when(first)
        def _():
            for a in (dcw_ref, dcb_ref, ddtb_ref, dalog_ref, ddsk_ref, dnw_ref):
                a[...] = jnp.zeros_like(a)

        @pl.when(step == 0)
        def _():
            ds_scr[...] = jnp.zeros_like(ds_scr)
            nxt_scr[...] = jnp.zeros_like(nxt_scr)

        chunk = nc - 1 - step
        xr = xr_ref[...]
        prev = jnp.where(chunk == 0, 0.0, xprev_ref[...])
        et_v = et_ref[...]
        p = _ssd_pre(xr, prev, cw_ref, cb_ref[...], pdt_ref[...], dtb_ref[...], alog_ref[...], e_ref[...])
        last, e_exp, dte, cd = _ssd_maps(p)
        rowi = p["rowi"]
        xs = p["xa"][:, :SSM_WIDTH]
        xd = xs * p["dt_map"]
        a_cs_t = p["a_cs"].T
        tri = _tril_mask()
        dsk = dsk_ref[...]
        nw_v = nw_ref[...]

        yv = yssd_ref[...]
        zv = z_ref[...]
        sz, zg, yg, _, rs = _gate_fwd(yv, zv, nw_v)
        dout = dyb_ref[...]
        for g in range(SSM_GROUPS):
            gs = slice(g * GROUP_W, (g + 1) * GROUP_W)
            dyg_g, dnw_g = _rms_bwd(yg[:, gs], rs[g], nw_v[:, gs], dout[:, gs])
            dnw_ref[:, gs] += dnw_g
            dxa_scr[:, gs] = dyg_g
        dyg = dxa_scr[:, :SSM_WIDTH]
        d_y = dyg * zg
        dpz_ref[...] = (dyg * yv * (sz + zv * sz * (1.0 - sz))).astype(BF16)

        s_prev = sprev_ref[...]
        ds_next = ds_scr[...]
        lane_dt = lax.broadcasted_iota(jnp.int32, (1, DT_PAD), 1)
        da_cols = jnp.zeros((CHUNK, DT_PAD), F32)
        for g in range(SSM_GROUPS):
            gs = slice(g * GROUP_W, (g + 1) * GROUP_W)
            b_off = SSM_WIDTH + g * SSM_STATE
            c_off = SSM_WIDTH + (SSM_GROUPS + g) * SSM_STATE
            bm = p["xa"][:, b_off:b_off + SSM_STATE].astype(BF16)
            cm = p["xa"][:, c_off:c_off + SSM_STATE].astype(BF16)
            cb_mat = _dot_nt(cm, bm)
            d_yg = d_y[:, gs]
            d_ygb = d_yg.astype(BF16)
            xdg = xd[:, gs]
            xdgb = xdg.astype(BF16)
            ds_g = ds_next[:, gs]
            sp_g = s_prev[:, gs]
            bds = _dot(bm, ds_g)
            dcs = d_yg * e_exp[:, gs]
            d_c = _dot_nt(dcs, sp_g)
            ds_scr[:, gs] = cd[:, gs] * ds_g + _dot_tn(cm, dcs)
            d_b = _dot_nt(xdg * dte[:, gs], ds_g)
            dxd_g = bds * dte[:, gs]
            sum_dcb = jnp.zeros((CHUNK, CHUNK), F32)
            for r in range(SSM_GROUPS * 2):
                head = g * 4 + r
                mask = _head_lane_mask(GROUP_W, r)
                dm = _head_decay(p["a_cs"], a_cs_t, head, tri)
                m_mat = cb_mat * dm
                g_mat = _dot_nt(jnp.where(mask, d_yg, 0.0), xdgb)
                w_mat = g_mat * m_mat
                sum_dcb = sum_dcb + g_mat * dm
                dxd_g = dxd_g + jnp.where(mask, _dot_tn(m_mat, d_ygb), 0.0)
                da_h = jnp.sum(w_mat - w_mat.T, axis=1, keepdims=True)
                da_cols = da_cols + jnp.where(lane_dt == head, da_h, 0.0)
            d_c = d_c + _dot(sum_dcb, bm)
            d_b = d_b + _dot_tn(sum_dcb, cm)
            dxa_scr[:, b_off:b_off + SSM_STATE] = d_b
            dxa_scr[:, c_off:c_off + SSM_STATE] = d_c
            y_off_g = _dot(cm, sp_g) * e_exp[:, gs]
            t3 = bds * xdg * dte[:, gs]
            tail = jnp.sum(t3, axis=0, keepdims=True) + jnp.sum(ds_g * sp_g, axis=0, keepdims=True) * cd[:, gs]
            pre_g = d_yg * y_off_g - t3 + jnp.where(last, tail, 0.0)
            s_pre, ddt_g, s_dsk = _seg_dots([pre_g, dxd_g * xs[:, gs], d_yg * xs[:, gs]], et_v[gs, :])
            da_cols = da_cols + s_pre
            ddsk_ref[...] += jnp.sum(s_dsk, axis=0, keepdims=True)
            dxa_scr[:, gs] = dxd_g * p["dt_map"][:, gs] + dsk[:, gs] * d_yg
            if g == 0:
                ddt = ddt_g
            else:
                ddt = ddt + ddt_g
        r_i = lax.broadcasted_iota(jnp.int32, (CHUNK, CHUNK), 0)
        c_i = lax.broadcasted_iota(jnp.int32, (CHUNK, CHUNK), 1)
        ddta = _tri_dot(r_i <= c_i, da_cols, terms=2)
        ddt = ddt + ddta * p["a_neg"]
        dalog_ref[...] += jnp.sum(ddta * p["dt"], axis=0, keepdims=True) * p["a_neg"]
        draw = ddt * _sigmoid(p["pre"])
        ddtb_ref[...] += jnp.sum(draw, axis=0, keepdims=True)
        dpdt_ref[...] = draw.astype(BF16)

        xc = p["xc"]
        sg = p["sg"]
        dxc = dxa_scr[...] * (sg + xc * sg * (1.0 - sg))
        dcb_ref[...] += jnp.sum(dxc, axis=0, keepdims=True)
        for k in range(CONV_K):
            dcw_ref[k] += jnp.sum(dxc * p["shifted"][k], axis=0, keepdims=True)
        nxt = nxt_scr[...]

        def up(s):
            return jnp.where(rowi >= CHUNK - s, pltpu.roll(nxt, CHUNK - s, 0), pltpu.roll(dxc, CHUNK - s, 0))

        dxr = cw_ref[3] * dxc + cw_ref[2] * up(1) + cw_ref[1] * up(2) + cw_ref[0] * up(3)
        dpxbc_ref[...] = dxr.astype(BF16)
        nxt_scr[...] = dxc

    def rows(width):
        return pl.BlockSpec((CHUNK, width), lambda b, s: (b * nc + nc - 1 - s, 0))

    prev_rows = pl.BlockSpec((CHUNK, CONV_CH), lambda b, s: (b * nc + jnp.maximum(nc - 2 - s, 0), 0))

    def whole(shape):
        nd = len(shape)
        return pl.BlockSpec(tuple(shape), lambda b, s: (0,) * nd)

    acc_shapes = [(CONV_K, 1, CONV_CH), (1, CONV_CH), (1, DT_PAD), (1, DT_PAD), (1, DT_PAD), (1, SSM_WIDTH)]
    return _call_carrying(
        body, carried, name="ssd_bwd", grid=(n_seq, nc),
        in_specs=[rows(CONV_CH), prev_rows, rows(SSM_WIDTH), rows(DT_PAD), rows(SSM_WIDTH), rows(SSM_WIDTH),
                  rows(SSM_WIDTH)] + _ssd_const_specs(),
        out_specs=[rows(CONV_CH), rows(SSM_WIDTH), rows(DT_PAD)] + [whole(s) for s in acc_shapes],
        out_shape=tuple([_sds((n_tok, CONV_CH), BF16), _sds((n_tok, SSM_WIDTH), BF16), _sds((n_tok, DT_PAD), BF16)]
                        + [_sds(s, F32) for s in acc_shapes]),
        scratch_shapes=[pltpu.VMEM((SSM_STATE, SSM_WIDTH), F32), pltpu.VMEM((CHUNK, CONV_CH), F32),
                        pltpu.VMEM((CHUNK, CONV_CH), F32)],
        operands=[p_xbc, p_xbc, p_z, p_dt, yssd, sprev, dyb, conv_w, conv_b, dt_bias, a_log, dskip_map, norm_w, e_bf,
                  et_bf])


def _inproj_bwd(dp_uv, dp_xbc, dp_z, dp_dt, x, dx1, w_uv, w_xbc, w_z, w_dt, nw, tm=256, carried=None):
    n_tok = x.shape[0]

    def body(duv_ref, dxbc_ref, dz_ref, ddt_ref, x_ref, dx1_ref, wuv_ref, wxbc_ref, wz_ref, wdt_ref, nw_ref,
             gx_ref, h_ref, dnw_ref):
        dh = _dot_nt(duv_ref[...], wuv_ref[...]) + _dot_nt(dxbc_ref[...], wxbc_ref[...])
        dh = dh + _dot_nt(dz_ref[...], wz_ref[...]) + _dot_nt(ddt_ref[...], wdt_ref[...])
        xv = x_ref[...]
        h, r = _rms_fwd(xv, nw_ref[...])
        dx, dnw = _rms_bwd(xv, r, nw_ref[...], dh)
        gx_ref[...] = dx1_ref[...] + dx
        h_ref[...] = h.astype(BF16)
        dnw_ref[...] += dnw

    return _rows_call("inproj_bwd", body, tm, [dp_uv, dp_xbc, dp_z, dp_dt, x, dx1], [w_uv, w_xbc, w_z, w_dt, nw],
                      [_sds((n_tok, D_MODEL), F32), _sds((n_tok, D_MODEL), BF16)], [_sds((1, D_MODEL), F32)],
                      carried=carried)


def _const_maps():
    lane = jnp.arange(SSM_WIDTH) // HEAD_DIM
    e_bf = (jnp.arange(DT_PAD)[:, None] == lane[None, :]).astype(BF16)
    return e_bf, e_bf.T


def _pad_lanes(v, width):
    return jnp.pad(v, ((0, 0), (0, width - v.shape[1])))


SHARD_COLS = IN_COLS // N_CHIPS
_UV_END = 2 * GM_WIDTH
_Z_END = _UV_END + SSM_WIDTH
_XBC_END = _Z_END + CONV_CH


def _cols_from_shards(w4, lo, hi):
    pieces = []
    for j in range(N_CHIPS):
        a, b = max(lo, j * SHARD_COLS), min(hi, (j + 1) * SHARD_COLS)
        if a < b:
            pieces.append(w4[j][:, a - j * SHARD_COLS:b - j * SHARD_COLS])
    return pieces[0] if len(pieces) == 1 else jnp.concatenate(pieces, axis=1)


def _shards_from_cols(blocks):
    shards = []
    for j in range(N_CHIPS):
        pieces = []
        for arr, lo, hi in blocks:
            a, b = max(lo, j * SHARD_COLS), min(hi, (j + 1) * SHARD_COLS)
            if a < b:
                pieces.append(arr[:, a - lo:b - lo])
        shards.append(pieces[0] if len(pieces) == 1 else jnp.concatenate(pieces, axis=1))
    return jnp.stack(shards)


def _forward_backward(x, tgt, w_in4, conv_w, small, out_shard, up_shard, down_shard, core):
    n_seq, seq_len, _ = x.shape
    n_tok = n_seq * seq_len
    x2 = x.reshape(n_tok, D_MODEL)
    tgt2 = tgt.reshape(n_tok, D_MODEL)
    e_bf, et_bf = _const_maps()

    w_uv = _cols_from_shards(w_in4, 0, _UV_END)
    w_z = _cols_from_shards(w_in4, _UV_END, _Z_END)
    w_xbc = _cols_from_shards(w_in4, _Z_END, _XBC_END)
    w_dt = _pad_lanes(_cols_from_shards(w_in4, _XBC_END, IN_COLS), DT_PAD)

    nw_pre = small["norm_mix_pre"]
    lnw = small["gm_ln_w"].reshape(1, GM_WIDTH)
    lnb = small["gm_ln_b"].reshape(1, GM_WIDTH)
    w_stack = small["gm_w_s"].reshape(N_HEADS * CHUNK, CHUNK)
    w_cat = jnp.transpose(small["gm_w_s"], (1, 0, 2)).reshape(CHUNK, N_HEADS * CHUNK)
    bmap = jnp.repeat(small["gm_b_s"].T, HEAD_DIM, axis=1)
    cw3 = conv_w.reshape(CONV_K, 1, CONV_CH)
    conv_b = small["conv_b"]
    dt_bias = _pad_lanes(small["dt_bias"], DT_PAD)
    a_log = _pad_lanes(small["a_log"], DT_PAD)
    dskip_map = jnp.repeat(small["d_skip"], HEAD_DIM, axis=1)
    ssm_nw = small["ssm_norm_w"]

    p_uv, p_xbc, p_z, p_dt, w_out4, w_up4 = _inproj_fwd(x2, nw_pre, w_uv, w_xbc, w_z, w_dt,
                                                        carried=_allgather_exchange([out_shard, up_shard]))
    ssd_consts = (cw3, conv_b, dt_bias, a_log, dskip_map, ssm_nw, e_bf, et_bf)
    ya, yb, yssd, sprev, w_down4 = _mixer_fwd(
        p_uv, p_xbc, p_z, p_dt, lnw, lnb, w_cat, bmap, *ssd_consts, n_seq, carried=_allgather_exchange([down_shard]))
    w_down_b = w_down4.reshape(D_FF, D_MODEL)
    w_out_b = w_out4.reshape(D_MODEL, D_MODEL)
    o, x1, h2 = _outproj_fwd(ya, yb, x2, w_out_b, small["norm_mix_post"], small["norm_ffn_pre"])
    f, dd, dy, loss_acc, d_nffn_post = _mlp_fwd(h2, x1, tgt2, w_up4, w_down_b, small["norm_ffn_post"])

    dup, dx1, d_nffn_pre = _mlp_bwd(dd, f, x1, dy, w_down_b, w_up4, small["norm_ffn_pre"])
    tk = min(DW_TOKENS_PER_STEP, n_tok)
    g_up = _matmul_tn("dw_up", h2, dup, D_MODEL, D_MODEL, tk, stacked=True)
    g_down = _matmul_tn("dw_down", f, dd, 1024, D_MODEL, tk).reshape(N_CHIPS, D_FF // N_CHIPS, D_MODEL)
    do, dya, dyb, d_nmix_post = _outproj_bwd(dx1, o, w_out_b, small["norm_mix_post"])
    g_out_a = _matmul_tn("dw_out_a", ya, do, GM_WIDTH, D_MODEL, tk)
    g_out_b = _matmul_tn("dw_out_b", yb, do, SSM_WIDTH, D_MODEL, tk)
    g_out = jnp.concatenate([g_out_a, g_out_b], axis=0).reshape(N_CHIPS, D_MODEL // N_CHIPS, D_MODEL)
    got_up, got_down, got_out = _pair_exchange("grad_pair_exchange_mlp", [g_up, g_down, g_out])
    h_up = _pair_sum(core, g_up, got_up, 256)
    h_down = _pair_sum(core, g_down, got_down, 256)
    h_out = _pair_sum(core, g_out, got_out, 128)
    dp_uv, d_ws, d_bs_t, d_lnw, d_lnb, slab_up = _gmlp_bwd(
        p_uv, dya, lnw, lnb, e_bf, et_bf, w_cat, w_stack, bmap, carried=_chip_exchange([h_up]))
    (dp_xbc, dp_z, dp_dt, d_cw, d_cb, d_dtb, d_alog, d_dsk, d_ssm_nw, slab_down, slab_out) = _ssd_bwd(
        p_xbc, p_z, p_dt, yssd, sprev, dyb, *ssd_consts, n_seq, carried=_chip_exchange([h_down, h_out]))

    early = {
        "gm_ln_w": d_lnw.reshape(N_HEADS, HEAD_DIM), "gm_ln_b": d_lnb.reshape(N_HEADS, HEAD_DIM),
        "gm_w_s": d_ws.reshape(N_HEADS, CHUNK, CHUNK), "gm_b_s": d_bs_t[:, :N_HEADS].T,
        "conv_w": d_cw.reshape(CONV_K, CONV_CH), "conv_b": d_cb, "dt_bias": d_dtb[:, :N_HEADS],
        "a_log": d_alog[:, :N_HEADS], "d_skip": d_dsk[:, :N_HEADS], "ssm_norm_w": d_ssm_nw,
        "norm_mix_post": d_nmix_post, "norm_ffn_pre": d_nffn_pre, "norm_ffn_post": d_nffn_post,
    }
    early_names = tuple(n for n in _SMALL_NAMES if n != "norm_mix_pre")
    packed_early = _pack(early, early_names, tail=loss_acc[0, 0].reshape(1))
    gx, h, d_nmix_pre, all_early = _inproj_bwd(dp_uv, dp_xbc, dp_z, dp_dt, x2, dx1, w_uv, w_xbc, w_z, w_dt, nw_pre,
                                               carried=_device_gather_exchange(packed_early))
    g_uv, all_pre = _matmul_tn("dw_in_uv", h, dp_uv, D_MODEL, 2 * GM_WIDTH, tk,
                               carried=_device_gather_exchange(jnp.pad(d_nmix_pre, ((0, 7), (0, 0)))))
    sum_early = _ordered_sum("small_sum", all_early)
    small_sum = _unpack(sum_early, {n: early[n].shape for n in early_names}, early_names)
    small_sum["norm_mix_pre"] = _ordered_sum("small_sum_pre", all_pre)[:1]
    loss = sum_early.reshape(-1)[sum(early[n].size for n in early_names)]
    g_xbc = _matmul_tn("dw_in_xbc", h, dp_xbc, D_MODEL, CONV_CH, tk)
    g_z = _matmul_tn("dw_in_z", h, dp_z, D_MODEL, SSM_WIDTH, tk)
    g_dt = _matmul_tn("dw_in_dt", h, dp_dt, D_MODEL, DT_PAD, tk)

    g_in = _shards_from_cols([(g_uv, 0, _UV_END), (g_z, _UV_END, _Z_END), (g_xbc, _Z_END, _XBC_END),
                              (g_dt, _XBC_END, IN_COLS)])

    (got_in,) = _pair_exchange("grad_pair_exchange_in", [g_in])
    h_in = _pair_sum(core, g_in, got_in, 256)
    (slab_in,) = _run_exchange("grad_chip_exchange", _chip_exchange([h_in]))
    reds = [_chip_sum(core, s, tm) for s, tm in ((slab_in, 256), (slab_out, 128), (slab_up, 256), (slab_down, 256))]
    big_grads = dict(zip(("w_in", "w_out", "w_up", "w_down"), _pair_gather(reds)))

    return loss, gx.reshape(x.shape), big_grads, small_sum


_HBM = pl.BlockSpec(memory_space=pltpu.HBM)


D2D_CHUNKS = 8
ROW_ALIGN = 16


def _row_chunks(rows, n_chunks):
    size = min(max(rows // n_chunks, ROW_ALIGN), rows)
    assert rows % size == 0
    return [(start, size) for start in range(0, rows, size)]


def _position():
    x, y, c = lax.axis_index("x"), lax.axis_index("y"), lax.axis_index("c")
    chips = [(1 - x, y), (x, 1 - y), (1 - x, 1 - y)]
    return x, y, c, chips


def _allgather_exchange(arrs):
    n = len(arrs)

    def copies(ins, outs, send_sems, recv_sems, local_sems):
        x, y, c, chips = _position()
        me = 2 * x + y
        sibling = (x, y, 1 - c)

        def copy(a, k, src, dst, to):
            return pltpu.make_async_remote_copy(src_ref=src, dst_ref=dst, send_sem=send_sems.at[a, k],
                                                recv_sem=recv_sems.at[a, k], device_id=to, device_id_type=MESH)

        def half_rows(a, pc):
            half = ins[a].shape[0] // 2
            return pl.ds(pc * half, half)

        local = [pltpu.make_async_copy(ins[a], outs[a].at[me], local_sems.at[a]) for a in range(n)]
        ici_out = [[copy(a, k, ins[a].at[half_rows(a, c)], outs[a].at[me, half_rows(a, c)], (px, py, c))
                    for k, (px, py) in enumerate(chips)] for a in range(n)]
        return c, chips, sibling, copy, half_rows, local, ici_out

    def start(ins, outs, send_sems, recv_sems, local_sems):
        _, _, _, _, _, local, ici_out = copies(ins, outs, send_sems, recv_sems, local_sems)
        for cp in local:
            cp.start()
        for a in range(n):
            for cp in ici_out[a]:
                cp.start()

    def finish(ins, outs, send_sems, recv_sems, local_sems):
        c, chips, sibling, copy, half_rows, local, ici_out = copies(ins, outs, send_sems, recv_sems, local_sems)
        passed = []
        for a in range(n):
            half = ins[a].shape[0] // 2
            for k, (px, py) in enumerate(chips):
                blk = outs[a].at[2 * px + py, half_rows(a, c)]
                copy(a, k, blk, blk, (px, py, c)).wait_recv()
                for first, size in _row_chunks(half, D2D_CHUNKS):
                    piece = outs[a].at[2 * px + py, pl.ds(c * half + first, size)]
                    copy(a, 3 + k, piece, piece, sibling).start()
                passed.append(copy(a, 3 + k, blk, blk, sibling))
        for a in range(n):
            for k, (px, py) in enumerate(chips):
                blk = outs[a].at[2 * px + py, half_rows(a, 1 - c)]
                copy(a, 3 + k, blk, blk, sibling).wait_recv()
        for a in range(n):
            for cp in ici_out[a]:
                cp.wait_send()
        for cp in passed:
            cp.wait_send()
        for cp in local:
            cp.wait()

    return _Carried(arrs, [_sds((N_CHIPS,) + a.shape, a.dtype) for a in arrs],
                    [pltpu.SemaphoreType.DMA((n, 6)), pltpu.SemaphoreType.DMA((n, 6)), pltpu.SemaphoreType.DMA((n,))],
                    start, finish)


def _run_exchange(name, exchange):
    n_in, n_out = len(exchange.ins), len(exchange.out_shapes)

    def body(*refs):
        ins, outs, sems = refs[:n_in], refs[n_in:n_in + n_out], refs[n_in + n_out:]
        exchange.start(ins, outs, *sems)
        exchange.finish(ins, outs, *sems)

    return pl.pallas_call(
        body, name=name, out_shape=tuple(exchange.out_shapes), in_specs=[_HBM] * n_in,
        out_specs=tuple([_HBM] * n_out), scratch_shapes=exchange.sems,
    )(*exchange.ins)


def _pair_exchange(name, grads):
    n = len(grads)

    def body(*refs):
        ins, got = refs[:n], refs[n:2 * n]
        send_sems, recv_sems = refs[2 * n:]
        x, y, c, _ = _position()
        sibling = (x, y, 1 - c)

        def copy(a, src, dst):
            return pltpu.make_async_remote_copy(src_ref=src, dst_ref=dst, send_sem=send_sems.at[a],
                                                recv_sem=recv_sems.at[a], device_id=sibling, device_id_type=MESH)

        for a in range(n):
            half = ins[a].shape[1] // 2
            for slab in range(N_CHIPS):
                for start, size in _row_chunks(half, D2D_CHUNKS):
                    copy(a, ins[a].at[slab, pl.ds((1 - c) * half + start, size), :],
                         got[a].at[slab, pl.ds(start, size), :]).start()
        for a in range(n):
            half = ins[a].shape[1] // 2
            copy(a, ins[a].at[:, pl.ds((1 - c) * half, half), :], got[a]).wait()

    halves = tuple(_sds((N_CHIPS, g.shape[1] // 2, g.shape[2]), g.dtype) for g in grads)
    return pl.pallas_call(
        body, name=name, out_shape=halves, in_specs=[_HBM] * n, out_specs=tuple([_HBM] * n),
        scratch_shapes=[pltpu.SemaphoreType.DMA((n,)), pltpu.SemaphoreType.DMA((n,))],
    )(*grads)


def _chip_exchange(hsums):
    n = len(hsums)

    def copies(ins, outs, send_sems, recv_sems, local_sems):
        x, y, c, chips = _position()
        me = 2 * x + y
        cps = []
        for a in range(n):
            cps.append(pltpu.make_async_copy(ins[a].at[me], outs[a].at[me], local_sems.at[a]))
            for k, (px, py) in enumerate(chips):
                cps.append(pltpu.make_async_remote_copy(
                    src_ref=ins[a].at[2 * px + py], dst_ref=outs[a].at[me], send_sem=send_sems.at[a, k],
                    recv_sem=recv_sems.at[a, k], device_id=(px, py, c), device_id_type=MESH))
        return cps

    def start(*refs):
        for cp in copies(*refs):
            cp.start()

    def finish(*refs):
        for cp in copies(*refs):
            cp.wait()

    return _Carried(hsums, [_sds(h.shape, h.dtype) for h in hsums],
                    [pltpu.SemaphoreType.DMA((n, 3)), pltpu.SemaphoreType.DMA((n, 3)), pltpu.SemaphoreType.DMA((n,))],
                    start, finish)


def _pair_gather(bufs):
    n = len(bufs)

    def body(*refs):
        outs = refs[n:2 * n]
        send_sems, recv_sems = refs[2 * n:]
        x, y, c, _ = _position()
        sibling = (x, y, 1 - c)

        def copy(a, rows):
            return pltpu.make_async_remote_copy(src_ref=rows, dst_ref=rows, send_sem=send_sems.at[a],
                                                recv_sem=recv_sems.at[a], device_id=sibling, device_id_type=MESH)

        for a in range(n):
            half = outs[a].shape[0] // 2
            for start, size in _row_chunks(half, 2 * D2D_CHUNKS):
                copy(a, outs[a].at[pl.ds(c * half + start, size), :]).start()
        for a in range(n):
            half = outs[a].shape[0] // 2
            copy(a, outs[a].at[pl.ds(c * half, half), :]).wait_send()
            copy(a, outs[a].at[pl.ds((1 - c) * half, half), :]).wait_recv()

    return pl.pallas_call(
        body, name="grad_pair_gather", out_shape=tuple(_sds(b.shape, b.dtype) for b in bufs),
        in_specs=[_HBM] * n, out_specs=tuple([_HBM] * n), input_output_aliases={a: a for a in range(n)},
        scratch_shapes=[pltpu.SemaphoreType.DMA((n,)), pltpu.SemaphoreType.DMA((n,))],
    )(*bufs)


def _device_gather_exchange(packed):
    def copies(ins, outs, send_sems, recv_sems, local_sem):
        (x_ref,), (all_ref,) = ins, outs
        x, y, c, chips = _position()
        me, sibling = (x, y, c), (x, y, 1 - c)

        def slab(px, py, pc):
            return all_ref.at[4 * px + 2 * py + pc]

        def copy(k, block, to, src=None):
            return pltpu.make_async_remote_copy(
                src_ref=slab(*block) if src is None else src, dst_ref=slab(*block), send_sem=send_sems.at[k],
                recv_sem=recv_sems.at[k], device_id=to, device_id_type=MESH)

        mine = pltpu.make_async_copy(x_ref, slab(*me), local_sem)
        first = [copy(0, me, sibling, src=x_ref)]
        first += [copy(1 + j, me, (*chip, c), src=x_ref) for j, chip in enumerate(chips)]
        passed = [copy(4 + j, (*chip, c), sibling) for j, chip in enumerate(chips)]
        return c, chips, me, sibling, copy, mine, first, passed

    def start(ins, outs, send_sems, recv_sems, local_sem):
        _, _, _, _, _, mine, first, _ = copies(ins, outs, send_sems, recv_sems, local_sem)
        mine.start()
        for cp in first:
            cp.start()

    def finish(ins, outs, send_sems, recv_sems, local_sem):
        c, chips, me, sibling, copy, mine, first, passed = copies(ins, outs, send_sems, recv_sems, local_sem)
        for j, chip in enumerate(chips):
            copy(1 + j, (*chip, c), me).wait_recv()
            passed[j].start()
        copy(0, sibling, me).wait_recv()
        for j, chip in enumerate(chips):
            copy(4 + j, (*chip, 1 - c), me).wait_recv()
        for cp in first + passed:
            cp.wait_send()
        mine.wait()

    return _Carried([packed], [_sds((N_DEV,) + packed.shape, F32)],
                    [pltpu.SemaphoreType.DMA((7,)), pltpu.SemaphoreType.DMA((7,)), pltpu.SemaphoreType.DMA],
                    start, finish)


def _ordered_sum(name, slabs):
    _, m_per, n_cols = slabs.shape

    def body(s_ref, o_ref):
        acc = s_ref[0]
        for d in range(1, N_DEV):
            acc = acc + s_ref[d]
        o_ref[...] = acc

    vmem = pl.BlockSpec(memory_space=pltpu.VMEM)
    return pl.pallas_call(body, name=name, out_shape=_sds((m_per, n_cols), F32), in_specs=[vmem], out_specs=vmem)(slabs)


def _pair_sum(core, own, got, tm):
    _, half, cols = got.shape
    nb = half // tm

    def body(c_ref, a_ref, b_ref, o_ref):
        o_ref[...] = (a_ref[...] + b_ref[...]).astype(BF16)

    return pl.pallas_call(
        body, name="grad_pair_sum", out_shape=_sds(got.shape, BF16),
        grid_spec=pltpu.PrefetchScalarGridSpec(
            num_scalar_prefetch=1, grid=(N_CHIPS, nb),
            in_specs=[pl.BlockSpec((None, tm, cols), lambda s, i, c_ref: (s, c_ref[0] * nb + i, 0)),
                      pl.BlockSpec((None, tm, cols), lambda s, i, c_ref: (s, i, 0))],
            out_specs=pl.BlockSpec((None, tm, cols), lambda s, i, c_ref: (s, i, 0))),
        compiler_params=_cparams(2),
    )(core, own, got)


def _chip_sum(core, slabs, tm):
    _, half, cols = slabs.shape
    nb = half // tm

    def body(c_ref, s_ref, o_ref):
        acc = s_ref[0].astype(F32)
        for k in range(1, N_CHIPS):
            acc = acc + s_ref[k].astype(F32)
        o_ref[...] = acc

    return pl.pallas_call(
        body, name="grad_chip_sum", out_shape=_sds((2 * half, cols), F32),
        grid_spec=pltpu.PrefetchScalarGridSpec(
            num_scalar_prefetch=1, grid=(nb,),
            in_specs=[pl.BlockSpec((N_CHIPS, tm, cols), lambda i, c_ref: (0, i, 0))],
            out_specs=pl.BlockSpec((tm, cols), lambda i, c_ref: (c_ref[0] * nb + i, 0))),
        compiler_params=_cparams(1),
    )(core, slabs)


def _adam_math(w, g, m, v):
    m2 = ADAM_B1 * m + (1.0 - ADAM_B1) * g
    v2 = ADAM_B2 * v + (1.0 - ADAM_B2) * (g * g)
    m_hat = m2 / (1.0 - ADAM_B1 ** ADAM_STEP)
    v_hat = v2 / (1.0 - ADAM_B2 ** ADAM_STEP)
    delta = -ADAM_LR * (m_hat / (jnp.sqrt(v_hat) + ADAM_EPS) + ADAM_WD * w)
    return delta, m2, v2


def _adamw(name, w, g, m, v, tm):
    def body(w_ref, g_ref, m_ref, v_ref, gout_ref, d_ref, m2_ref, v2_ref):
        gv = g_ref[...]
        d, m2, v2 = _adam_math(w_ref[...], gv, m_ref[...], v_ref[...])
        gout_ref[...] = gv
        d_ref[...] = d
        m2_ref[...] = m2
        v2_ref[...] = v2

    return _rows_call(name, body, tm, [w, g, m, v], [], [_sds(w.shape, F32)] * 4)


_SMALL_NAMES = ("norm_mix_pre", "gm_ln_w", "gm_ln_b", "gm_w_s", "gm_b_s", "conv_w", "conv_b", "dt_bias", "a_log",
                "d_skip", "ssm_norm_w", "norm_mix_post", "norm_ffn_pre", "norm_ffn_post")
_PACK_COLS = 1024


def _pack(parts, names=_SMALL_NAMES, tail=None):
    pieces = [parts[n].reshape(-1) for n in names]
    flat = jnp.concatenate(pieces if tail is None else pieces + [tail])
    rows = -(-flat.shape[0] // (8 * _PACK_COLS)) * 8
    flat = jnp.pad(flat, (0, rows * _PACK_COLS - flat.shape[0]))
    return flat.reshape(rows, _PACK_COLS)


def _unpack(packed, shapes, names=_SMALL_NAMES):
    flat = packed.reshape(-1)
    out, off = {}, 0
    for n in names:
        size = 1
        for s in shapes[n]:
            size *= s
        out[n] = flat[off:off + size].reshape(shapes[n])
        off += size
    return out


def kernel(x, norm_mix_pre, w_in, gm_ln_w, gm_ln_b, gm_w_s, gm_b_s, conv_w, conv_b, dt_bias, a_log, d_skip, ssm_norm_w, w_out, norm_mix_post, norm_ffn_pre, w_up, w_down, norm_ffn_post, loss_target, m_norm_mix_pre, m_w_in, m_gm_ln_w, m_gm_ln_b, m_gm_w_s, m_gm_b_s, m_conv_w, m_conv_b, m_dt_bias, m_a_log, m_d_skip, m_ssm_norm_w, m_w_out, m_norm_mix_post, m_norm_ffn_pre, m_w_up, m_w_down, m_norm_ffn_post, v_norm_mix_pre, v_w_in, v_gm_ln_w, v_gm_ln_b, v_gm_w_s, v_gm_b_s, v_conv_w, v_conv_b, v_dt_bias, v_a_log, v_d_skip, v_ssm_norm_w, v_w_out, v_norm_mix_post, v_norm_ffn_pre, v_w_up, v_w_down, v_norm_ffn_post):
    params = dict(norm_mix_pre=norm_mix_pre, w_in=w_in, gm_ln_w=gm_ln_w, gm_ln_b=gm_ln_b, gm_w_s=gm_w_s, gm_b_s=gm_b_s,
                  conv_w=conv_w, conv_b=conv_b, dt_bias=dt_bias, a_log=a_log, d_skip=d_skip, ssm_norm_w=ssm_norm_w,
                  w_out=w_out, norm_mix_post=norm_mix_post, norm_ffn_pre=norm_ffn_pre, w_up=w_up, w_down=w_down,
                  norm_ffn_post=norm_ffn_post)
    mom1 = dict(norm_mix_pre=m_norm_mix_pre, w_in=m_w_in, gm_ln_w=m_gm_ln_w, gm_ln_b=m_gm_ln_b, gm_w_s=m_gm_w_s,
                gm_b_s=m_gm_b_s, conv_w=m_conv_w, conv_b=m_conv_b, dt_bias=m_dt_bias, a_log=m_a_log, d_skip=m_d_skip,
                ssm_norm_w=m_ssm_norm_w, w_out=m_w_out, norm_mix_post=m_norm_mix_post, norm_ffn_pre=m_norm_ffn_pre,
                w_up=m_w_up, w_down=m_w_down, norm_ffn_post=m_norm_ffn_post)
    mom2 = dict(norm_mix_pre=v_norm_mix_pre, w_in=v_w_in, gm_ln_w=v_gm_ln_w, gm_ln_b=v_gm_ln_b, gm_w_s=v_gm_w_s,
                gm_b_s=v_gm_b_s, conv_w=v_conv_w, conv_b=v_conv_b, dt_bias=v_dt_bias, a_log=v_a_log, d_skip=v_d_skip,
                ssm_norm_w=v_ssm_norm_w, w_out=v_w_out, norm_mix_post=v_norm_mix_post, norm_ffn_pre=v_norm_ffn_pre,
                w_up=v_w_up, w_down=v_w_down, norm_ffn_post=v_norm_ffn_post)
    names = list(params)
    big = ("w_in", "w_out", "w_up", "w_down")
    chip = 2 * lax.axis_index("x") + lax.axis_index("y")

    shards = {n: params[n][0].astype(BF16) for n in big}
    conv_shard = jnp.pad(conv_w[0], ((0, 16 - CONV_K), (0, 0)))
    g_in4, g_conv4 = _run_exchange("allgather_w_in", _allgather_exchange([shards["w_in"], conv_shard]))
    conv_full = jnp.transpose(g_conv4[:, :CONV_K, :], (1, 0, 2)).reshape(CONV_K, CONV_CH)

    small = {n: params[n][0] if params[n].ndim >= 3 else params[n] for n in _SMALL_NAMES if n != "conv_w"}
    core = lax.axis_index("c").astype(jnp.int32).reshape(1)
    loss, grad_x, big_grads, small_sum = _forward_backward(
        x, loss_target, g_in4, conv_full, small, shards["w_out"], shards["w_up"], shards["w_down"], core)

    small_sum["conv_w"] = lax.dynamic_slice_in_dim(small_sum["conv_w"], chip * (CONV_CH // N_CHIPS), CONV_CH // N_CHIPS, axis=1)

    grads, delta, new_m, new_v = {}, {}, {}, {}
    for n, tm in zip(big, (256, 128, 256, 256)):
        g, d, m2, v2 = _adamw("adamw_" + n, params[n][0], big_grads[n], mom1[n][0], mom2[n][0], tm)
        grads[n], delta[n], new_m[n], new_v[n] = g[None], d[None], m2[None], v2[None]
    local_shapes = {n: params[n].shape[1:] if params[n].ndim >= 3 else params[n].shape for n in _SMALL_NAMES}
    flat = lambda tree: {n: tree[n].reshape(local_shapes[n]) for n in _SMALL_NAMES}
    packed = [_pack(flat(t)) for t in (params, small_sum, mom1, mom2)]
    _, d_p, m_p, v_p = _adamw("adamw_small", *packed, packed[0].shape[0])
    for src, dst in ((d_p, delta), (m_p, new_m), (v_p, new_v)):
        for n, val in _unpack(src, local_shapes).items():
            dst[n] = val.reshape(params[n].shape)
    for n in _SMALL_NAMES:
        grads[n] = small_sum[n].reshape(params[n].shape)

    out = [loss, grad_x]
    for tree in (grads, delta, new_m, new_v):
        out += [tree[n] for n in names]
    return tuple(out)
```

```python
import functools

import jax
import jax.numpy as jnp
from jax import lax
from jax.experimental import pallas as pl
from jax.experimental.pallas import tpu as pltpu

F32 = jnp.float32
BF16 = jnp.bfloat16
HI = lax.Precision.HIGHEST
MESH = pl.DeviceIdType.MESH

EPS = 1e-6
D_MODEL = 1024
GM_WIDTH = 512
SSM_WIDTH = 512
N_HEADS = 8
HEAD_DIM = 64
CHUNK = 128
SSM_GROUPS = 2
GROUP_W = SSM_WIDTH // SSM_GROUPS
SSM_STATE = 128
CONV_K = 4
CONV_CH = 1024
D_FF = 4096
IN_COLS = 2568
DT_PAD = 128
N_CHIPS = 4
N_DEV = 8

ADAM_LR = 0.001
ADAM_B1 = 0.9
ADAM_B2 = 0.999
ADAM_EPS = 1e-08
ADAM_WD = 0.01
ADAM_STEP = 10

VMEM_LIMIT_BYTES = 56 * 1024 * 1024
FF_TILE = 512
DW_TOKENS_PER_STEP = 2048


def _cparams(n_axes):
    return pltpu.CompilerParams(dimension_semantics=("arbitrary",) * n_axes, vmem_limit_bytes=VMEM_LIMIT_BYTES)


def _dot(a, b):
    return jnp.dot(a.astype(BF16), b.astype(BF16), preferred_element_type=F32)


def _dot_nt(a, b):
    return lax.dot_general(a.astype(BF16), b.astype(BF16), (((1,), (1,)), ((), ())), preferred_element_type=F32)


def _dot_tn(a, b):
    return lax.dot_general(a.astype(BF16), b.astype(BF16), (((0,), (0,)), ((), ())), preferred_element_type=F32)


def _sigmoid(x):
    return 1.0 / (1.0 + jnp.exp(-x))


_GELU_C = 0.7978845608028654
_GELU_A = 0.044715


def _gelu(x):
    t = jnp.tanh(_GELU_C * (x + _GELU_A * (x * x * x)))
    return 0.5 * x * (1.0 + t), t


def _gelu_grad(x, t):
    return 0.5 * (1.0 + t) + 0.5 * x * (1.0 - t * t) * (_GELU_C * (1.0 + 3.0 * _GELU_A * x * x))


def _rms_fwd(x, w):
    r = lax.rsqrt(jnp.mean(x * x, axis=-1, keepdims=True) + EPS)
    return x * r * w, r


def _rms_bwd(x, r, w, dy):
    g = dy * w
    dx = r * g - x * (r * r * r) * jnp.mean(g * x, axis=-1, keepdims=True)
    dw = jnp.sum(dy * x * r, axis=0, keepdims=True)
    return dx, dw


class _Carried:
    def __init__(self, ins, out_shapes, sems, start, finish):
        self.ins, self.out_shapes, self.sems, self.start, self.finish = list(ins), list(out_shapes), list(sems), start, finish


def _both(first, second):
    n_i, n_o, n_s = len(first.ins), len(first.out_shapes), len(first.sems)

    def split(ins, outs, sems):
        return (ins[:n_i], outs[:n_o], sems[:n_s]), (ins[n_i:], outs[n_o:], sems[n_s:])

    def start(ins, outs, *sems):
        (i1, o1, s1), (i2, o2, s2) = split(ins, outs, sems)
        first.start(i1, o1, *s1)
        second.start(i2, o2, *s2)

    def finish(ins, outs, *sems):
        (i1, o1, s1), (i2, o2, s2) = split(ins, outs, sems)
        first.finish(i1, o1, *s1)
        second.finish(i2, o2, *s2)

    return _Carried(first.ins + second.ins, first.out_shapes + second.out_shapes, first.sems + second.sems, start, finish)


def _split_carried(refs, n_in, n_out, n_scratch, carried):
    n_ci, n_co, n_cs = len(carried.ins), len(carried.out_shapes), len(carried.sems)
    ins, rest = refs[:n_in], refs[n_in:]
    c_ins, rest = rest[:n_ci], rest[n_ci:]
    outs, rest = rest[:n_out], rest[n_out:]
    c_outs, rest = rest[:n_co], rest[n_co:]
    scr, c_sems = rest[:n_scratch], rest[n_scratch:]
    assert len(c_sems) == n_cs
    return tuple(ins) + tuple(outs) + tuple(scr), c_ins, c_outs, c_sems


def _rows_call(name, body, tm, row_ins, const_ins, row_outs, acc_outs=(), scratch=(), carried=None):
    n_rows = row_ins[0].shape[0]
    assert n_rows % tm == 0
    n_steps = n_rows // tm
    n_in = len(row_ins) + len(const_ins)
    n_ro = len(row_outs)
    n_acc = len(acc_outs)

    def kern(*refs):
        accs = refs[n_in + n_ro:n_in + n_ro + n_acc]

        @pl.when(pl.program_id(0) == 0)
        def _():
            for a in accs:
                a[...] = jnp.zeros_like(a)

        body(*refs)

    def whole(shape):
        nd = len(shape)
        return pl.BlockSpec(tuple(shape), lambda i: (0,) * nd)

    in_specs = [pl.BlockSpec((tm, a.shape[1]), lambda i: (i, 0)) for a in row_ins]
    in_specs += [whole(a.shape) for a in const_ins]
    out_specs = [pl.BlockSpec((tm, s.shape[1]), lambda i: (i, 0)) for s in row_outs]
    out_specs += [whole(s.shape) for s in acc_outs]
    return _call_carrying(
        kern, carried, name=name, grid=(n_steps,), in_specs=in_specs, out_specs=out_specs,
        out_shape=tuple(row_outs) + tuple(acc_outs), scratch_shapes=list(scratch), operands=list(row_ins) + list(const_ins))


def _call_carrying(body, carried, *, name, grid, in_specs, out_specs, out_shape, scratch_shapes, operands):
    n_in, n_out, n_scratch = len(in_specs), len(out_specs), len(scratch_shapes)
    kern = body
    if carried is not None:
        def kern(*refs):
            plain, c_ins, c_outs, c_sems = _split_carried(refs, n_in, n_out, n_scratch, carried)
            first, last = True, True
            for d, size in enumerate(grid):
                first = jnp.logical_and(first, pl.program_id(d) == 0)
                last = jnp.logical_and(last, pl.program_id(d) == size - 1)

            @pl.when(first)
            def _():
                carried.start(c_ins, c_outs, *c_sems)

            body(*plain)

            @pl.when(last)
            def _():
                carried.finish(c_ins, c_outs, *c_sems)

        in_specs = list(in_specs) + [_HBM] * len(carried.ins)
        out_specs = list(out_specs) + [_HBM] * len(carried.out_shapes)
        out_shape = tuple(out_shape) + tuple(carried.out_shapes)
        operands = list(operands) + carried.ins
        scratch_shapes = list(scratch_shapes) + carried.sems
    return pl.pallas_call(
        kern, name=name, grid=grid, in_specs=in_specs, out_specs=out_specs, out_shape=out_shape,
        scratch_shapes=scratch_shapes, compiler_params=_cparams(len(grid)),
    )(*operands)


def _sds(shape, dtype):
    return jax.ShapeDtypeStruct(tuple(shape), dtype)


def _matmul_tn(name, a, b, tm, tn, tk, stacked=False, carried=None):
    k_dim, m_dim = a.shape
    n_dim = b.shape[1]
    assert m_dim % tm == 0 and n_dim % tn == 0 and k_dim % tk == 0

    def kern(a_ref, b_ref, o_ref):
        @pl.when(pl.program_id(2) == 0)
        def _():
            o_ref[...] = jnp.zeros_like(o_ref)

        o_ref[...] += _dot_tn(a_ref[...], b_ref[...])

    if stacked:
        assert tm == m_dim
        out_shape = _sds((n_dim // tn, m_dim, tn), F32)
        out_spec = pl.BlockSpec((None, tm, tn), lambda i, j, k: (j, i, 0))
    else:
        out_shape = _sds((m_dim, n_dim), F32)
        out_spec = pl.BlockSpec((tm, tn), lambda i, j, k: (i, j))
    outs = _call_carrying(
        kern, carried, name=name, grid=(m_dim // tm, n_dim // tn, k_dim // tk),
        in_specs=[pl.BlockSpec((tk, tm), lambda i, j, k: (k, i)), pl.BlockSpec((tk, tn), lambda i, j, k: (k, j))],
        out_specs=[out_spec], out_shape=(out_shape,), scratch_shapes=[], operands=[a, b])
    return outs[0] if carried is None else outs


def _inproj_fwd(x, nw, w_uv, w_xbc, w_z, w_dt, tm=256, carried=None):
    n_tok = x.shape[0]

    def body(x_ref, nw_ref, wuv_ref, wxbc_ref, wz_ref, wdt_ref, puv_ref, pxbc_ref, pz_ref, pdt_ref):
        h, _ = _rms_fwd(x_ref[...], nw_ref[...])
        h = h.astype(BF16)
        puv_ref[...] = jnp.dot(h, wuv_ref[...], preferred_element_type=F32)
        pxbc_ref[...] = jnp.dot(h, wxbc_ref[...], preferred_element_type=F32)
        pz_ref[...] = jnp.dot(h, wz_ref[...], preferred_element_type=F32)
        pdt_ref[...] = jnp.dot(h, wdt_ref[...], preferred_element_type=F32)

    return _rows_call(
        "inproj_fwd", body, tm, [x], [nw, w_uv, w_xbc, w_z, w_dt],
        [_sds((n_tok, 2 * GM_WIDTH), F32), _sds((n_tok, CONV_CH), F32), _sds((n_tok, SSM_WIDTH), F32),
         _sds((n_tok, DT_PAD), F32)], carried=carried)


def _head_lane_mask(width, head):
    lane = lax.broadcasted_iota(jnp.int32, (1, width), 1)
    return (lane // HEAD_DIM) == head


def _split_terms(x, terms):
    parts = []
    for _ in range(terms):
        p = x.astype(BF16)
        parts.append(p)
        x = x - p.astype(F32)
    return parts


def _seg_dots(vals, ind, terms=2):
    m = vals[0].shape[0]
    parts = []
    for v in vals:
        parts += _split_terms(v, terms)
    red = jnp.dot(jnp.concatenate(parts, axis=0), ind, preferred_element_type=F32)
    outs = []
    for i in range(len(vals)):
        acc = red[i * terms * m:(i * terms + 1) * m]
        for t in range(1, terms):
            acc = acc + red[(i * terms + t) * m:(i * terms + t + 1) * m]
        outs.append(acc)
    return outs


def _tri_dot(mask, x, terms=3):
    n = x.shape[1]
    red = jnp.dot(mask.astype(BF16), jnp.concatenate(_split_terms(x, terms), axis=1), preferred_element_type=F32)
    acc = red[:, :n]
    for t in range(1, terms):
        acc = acc + red[:, t * n:(t + 1) * n]
    return acc


def _gmlp_common(puv, lnw, lnb, e_bf, et_bf):
    u = puv[:, :GM_WIDTH]
    v = puv[:, GM_WIDTH:]
    gu, tu = _gelu(u)
    gv, tv = _gelu(v)
    (s1,) = _seg_dots([gv], et_bf)
    (mu,) = _seg_dots([s1 * (1.0 / HEAD_DIM)], e_bf)
    xc = gv - mu
    (s2,) = _seg_dots([xc * xc], et_bf)
    (rstd,) = _seg_dots([lax.rsqrt(s2 * (1.0 / HEAD_DIM) + EPS)], e_bf)
    xhat = xc * rstd
    vn = xhat * lnw + lnb
    return u, v, gu, tu, tv, rstd, xhat, vn


def _tril_mask():
    r = lax.broadcasted_iota(jnp.int32, (CHUNK, CHUNK), 0)
    c = lax.broadcasted_iota(jnp.int32, (CHUNK, CHUNK), 1)
    return r >= c


def _head_blocks(v):
    return jnp.concatenate([jnp.where(_head_lane_mask(GM_WIDTH, h), v, jnp.zeros_like(v)) for h in range(N_HEADS)], axis=0)


def _causal_w_cat(w_cat):
    t = lax.broadcasted_iota(jnp.int32, (CHUNK, N_HEADS * CHUNK), 0)
    s = lax.broadcasted_iota(jnp.int32, (CHUNK, N_HEADS * CHUNK), 1) % CHUNK
    return jnp.where(t >= s, w_cat, 0.0).astype(BF16)


def _gmlp_chunk_fwd(puv, lnw, lnb, e_bf, et_bf, wm, bmap):
    _, _, gu, _, _, _, _, vn = _gmlp_common(puv, lnw, lnb, e_bf, et_bf)
    mixed = jnp.dot(wm, _head_blocks(vn.astype(BF16)), preferred_element_type=F32) + bmap
    return (gu * mixed).astype(BF16)


def _ssd_pre(xr, prev, cw_ref, cb, pdt, dtb, alog, emap):
    rowi = lax.broadcasted_iota(jnp.int32, (CHUNK, 1), 0)

    def down(s):
        return jnp.where(rowi < s, pltpu.roll(prev, s, 0), pltpu.roll(xr, s, 0))

    shifted = [down(3), down(2), down(1), xr]
    xc = cb
    for k in range(CONV_K):
        xc = xc + cw_ref[k] * shifted[k]
    sg = _sigmoid(xc)
    xa = xc * sg
    pre = pdt + dtb
    dt = jnp.maximum(pre, 0.0) + jnp.log(1.0 + jnp.exp(-jnp.abs(pre)))
    a_neg = -jnp.exp(alog)
    a_cs = _tri_dot(_tril_mask(), dt * a_neg)
    acs_map, dt_map = _seg_dots([a_cs, dt], emap, terms=3)
    return dict(shifted=shifted, xc=xc, sg=sg, xa=xa, pre=pre, dt=dt, a_neg=a_neg, a_cs=a_cs,
                acs_map=acs_map, dt_map=dt_map, rowi=rowi)


def _ssd_maps(p):
    last = p["rowi"] == CHUNK - 1
    aq_map = jnp.sum(jnp.where(last, p["acs_map"], 0.0), axis=0, keepdims=True)
    e_exp = jnp.exp(p["acs_map"])
    dte = jnp.exp(aq_map - p["acs_map"])
    cd = jnp.exp(aq_map)
    return last, e_exp, dte, cd


def _head_decay(a_cs, a_cs_t, head, tri):
    lane = lax.broadcasted_iota(jnp.int32, (1, DT_PAD), 1)
    sub = lax.broadcasted_iota(jnp.int32, (DT_PAD, 1), 0)
    col = jnp.sum(jnp.where(lane == head, a_cs, 0.0), axis=1, keepdims=True)
    row = jnp.sum(jnp.where(sub == head, a_cs_t, 0.0), axis=0, keepdims=True)
    return jnp.exp(jnp.where(tri, col - row, -1e30))


def _gate_fwd(y, z, nw):
    sz = _sigmoid(z)
    zg = z * sz
    yg = y * zg
    outs, rs = [], []
    for g in range(SSM_GROUPS):
        gs = slice(g * GROUP_W, (g + 1) * GROUP_W)
        o, r = _rms_fwd(yg[:, gs], nw[:, gs])
        outs.append(o)
        rs.append(r)
    return sz, zg, yg, outs, rs


def _ssd_const_specs():
    def whole(shape):
        nd = len(shape)
        return pl.BlockSpec(tuple(shape), lambda b, c: (0,) * nd)
    return [whole((CONV_K, 1, CONV_CH)), whole((1, CONV_CH)), whole((1, DT_PAD)), whole((1, DT_PAD)),
            whole((1, SSM_WIDTH)), whole((1, SSM_WIDTH)), whole((DT_PAD, SSM_WIDTH)), whole((SSM_WIDTH, DT_PAD))]


def _mixer_fwd(p_uv, p_xbc, p_z, p_dt, lnw, lnb, w_cat, bmap, conv_w, conv_b, dt_bias, a_log, dskip_map, norm_w,
               e_bf, et_bf, n_seq, carried=None):
    n_tok = p_xbc.shape[0]
    nc = n_tok // n_seq // CHUNK

    def body(puv_ref, xr_ref, z_ref, pdt_ref, lnw_ref, lnb_ref, wcat_ref, bmap_ref,
             cw_ref, cb_ref, dtb_ref, alog_ref, dsk_ref, nw_ref, e_ref, et_ref,
             ya_ref, yb_ref, yssd_ref, sprev_ref, wm_scr, prev_scr, s_scr):
        @pl.when(jnp.logical_and(pl.program_id(0) == 0, pl.program_id(1) == 0))
        def _():
            wm_scr[...] = _causal_w_cat(wcat_ref[...])

        @pl.when(pl.program_id(1) == 0)
        def _():
            prev_scr[...] = jnp.zeros_like(prev_scr)
            s_scr[...] = jnp.zeros_like(s_scr)

        ya_ref[...] = _gmlp_chunk_fwd(puv_ref[...], lnw_ref[...], lnb_ref[...], e_ref[...], et_ref[...], wm_scr[...],
                                      bmap_ref[...])
        xr = xr_ref[...]
        p = _ssd_pre(xr, prev_scr[...], cw_ref, cb_ref[...], pdt_ref[...], dtb_ref[...], alog_ref[...], e_ref[...])
        _, e_exp, dte, cd = _ssd_maps(p)
        xs = p["xa"][:, :SSM_WIDTH]
        xd = xs * p["dt_map"]
        a_cs_t = p["a_cs"].T
        tri = _tril_mask()
        s_old = s_scr[...]
        sprev_ref[...] = s_old
        for g in range(SSM_GROUPS):
            gs = slice(g * GROUP_W, (g + 1) * GROUP_W)
            bm = p["xa"][:, SSM_WIDTH + g * SSM_STATE: SSM_WIDTH + (g + 1) * SSM_STATE].astype(BF16)
            cm = p["xa"][:, SSM_WIDTH + (SSM_GROUPS + g) * SSM_STATE: SSM_WIDTH + (SSM_GROUPS + g + 1) * SSM_STATE].astype(BF16)
            cb_mat = _dot_nt(cm, bm)
            xdg = xd[:, gs].astype(BF16)
            y_g = _dot(cm, s_old[:, gs]) * e_exp[:, gs] + dsk_ref[:, gs] * xs[:, gs]
            for r in range(SSM_GROUPS * 2):
                dm = _head_decay(p["a_cs"], a_cs_t, g * 4 + r, tri)
                full = jnp.dot((cb_mat * dm).astype(BF16), xdg, preferred_element_type=F32)
                y_g = y_g + jnp.where(_head_lane_mask(GROUP_W, r), full, 0.0)
            yssd_ref[:, gs] = y_g
            s_scr[:, gs] = cd[:, gs] * s_old[:, gs] + _dot_tn(bm, xd[:, gs] * dte[:, gs])
        _, _, _, outs, _ = _gate_fwd(yssd_ref[...], z_ref[...], nw_ref[...])
        for g in range(SSM_GROUPS):
            yb_ref[:, g * GROUP_W:(g + 1) * GROUP_W] = outs[g].astype(BF16)
        prev_scr[...] = xr

    def rows(width):
        return pl.BlockSpec((CHUNK, width), lambda b, c: (b * nc + c, 0))

    def whole(shape):
        nd = len(shape)
        return pl.BlockSpec(tuple(shape), lambda b, c: (0,) * nd)

    return _call_carrying(
        body, carried, name="mixer_fwd", grid=(n_seq, nc),
        in_specs=[rows(2 * GM_WIDTH), rows(CONV_CH), rows(SSM_WIDTH), rows(DT_PAD), whole(lnw.shape), whole(lnb.shape),
                  whole(w_cat.shape), whole(bmap.shape)] + _ssd_const_specs(),
        out_specs=[rows(GM_WIDTH), rows(SSM_WIDTH), rows(SSM_WIDTH), rows(SSM_WIDTH)],
        out_shape=(_sds((n_tok, GM_WIDTH), BF16), _sds((n_tok, SSM_WIDTH), BF16), _sds((n_tok, SSM_WIDTH), F32),
                   _sds((n_tok, SSM_WIDTH), F32)),
        scratch_shapes=[pltpu.VMEM((CHUNK, N_HEADS * CHUNK), BF16), pltpu.VMEM((CHUNK, CONV_CH), F32),
                        pltpu.VMEM((SSM_STATE, SSM_WIDTH), F32)],
        operands=[p_uv, p_xbc, p_z, p_dt, lnw, lnb, w_cat, bmap, conv_w, conv_b, dt_bias, a_log, dskip_map, norm_w, e_bf,
                  et_bf])


def _outproj_fwd(ya, yb, x, w_out, nw_post, nw_pre2, tm=256):
    n_tok = x.shape[0]

    def body(ya_ref, yb_ref, x_ref, wo_ref, nwa_ref, nwb_ref, o_ref, x1_ref, h2_ref):
        o = jnp.dot(ya_ref[...], wo_ref[:GM_WIDTH, :], preferred_element_type=F32)
        o = o + jnp.dot(yb_ref[...], wo_ref[GM_WIDTH:, :], preferred_element_type=F32)
        on, _ = _rms_fwd(o, nwa_ref[...])
        x1 = x_ref[...] + on
        h2, _ = _rms_fwd(x1, nwb_ref[...])
        o_ref[...] = o
        x1_ref[...] = x1
        h2_ref[...] = h2.astype(BF16)

    return _rows_call("outproj_fwd", body, tm, [ya, yb, x], [w_out, nw_post, nw_pre2],
                      [_sds((n_tok, D_MODEL), F32), _sds((n_tok, D_MODEL), F32), _sds((n_tok, D_MODEL), BF16)])


def _up_cols(wup_ref, j):
    per = (D_FF // N_CHIPS) // FF_TILE
    return wup_ref[j // per, :, (j % per) * FF_TILE:(j % per + 1) * FF_TILE]


def _down_rows(wda_ref, wdb_ref, j):
    assert 2 * FF_TILE == D_FF // N_CHIPS
    return (wda_ref if j % 2 == 0 else wdb_ref)[j // 2]


def _mlp_fwd(h2, x1, tgt, w_up, w_down_a, w_down_b, nw, tm=256):
    n_tok = x1.shape[0]

    def body(h2_ref, x1_ref, tgt_ref, wup_ref, wda_ref, wdb_ref, nw_ref, f_ref, dd_ref, dy_ref, loss_ref, dnw_ref):
        h2v = h2_ref[...]
        acc = jnp.zeros((tm, D_MODEL), F32)
        for j in range(D_FF // FF_TILE):
            cs = slice(j * FF_TILE, (j + 1) * FF_TILE)
            u = jnp.dot(h2v, _up_cols(wup_ref, j), preferred_element_type=F32)
            f = jnp.square(jnp.maximum(u, 0.0)).astype(BF16)
            f_ref[:, cs] = f
            acc = acc + jnp.dot(f, _down_rows(wda_ref, wdb_ref, j), preferred_element_type=F32)
        dn, r = _rms_fwd(acc, nw_ref[...])
        e = x1_ref[...] + dn - tgt_ref[...]
        loss_ref[...] += jnp.full(loss_ref.shape, (0.5 / D_MODEL) * jnp.sum(e * e), F32)
        dy = e * (1.0 / D_MODEL)
        dd, dnw = _rms_bwd(acc, r, nw_ref[...], dy)
        dy_ref[...] = dy
        dd_ref[...] = dd.astype(BF16)
        dnw_ref[...] += dnw

    return _rows_call(
        "mlp_fwd", body, tm, [h2, x1, tgt], [w_up, w_down_a, w_down_b, nw],
        [_sds((n_tok, D_FF), BF16), _sds((n_tok, D_MODEL), BF16), _sds((n_tok, D_MODEL), F32)],
        [_sds((8, 128), F32), _sds((1, D_MODEL), F32)])


def _mlp_bwd(dd, f, x1, dy, w_down_a, w_down_b, w_up, nw, tm=256):
    n_tok = x1.shape[0]

    def body(dd_ref, f_ref, x1_ref, dy_ref, wda_ref, wdb_ref, wup_ref, nw_ref, dup_ref, dx1_ref, dnw_ref):
        ddv = dd_ref[...]
        acc = jnp.zeros((tm, D_MODEL), F32)
        for j in range(D_FF // FF_TILE):
            cs = slice(j * FF_TILE, (j + 1) * FF_TILE)
            df = _dot_nt(ddv, _down_rows(wda_ref, wdb_ref, j))
            du = (df * (2.0 * jnp.sqrt(f_ref[:, cs].astype(F32)))).astype(BF16)
            dup_ref[:, cs] = du
            acc = acc + _dot_nt(du, _up_cols(wup_ref, j))
        x1v = x1_ref[...]
        _, r = _rms_fwd(x1v, nw_ref[...])
        dx, dnw = _rms_bwd(x1v, r, nw_ref[...], acc)
        dx1_ref[...] = dy_ref[...] + dx
        dnw_ref[...] += dnw

    return _rows_call("mlp_bwd", body, tm, [dd, f, x1, dy], [w_down_a, w_down_b, w_up, nw],
                      [_sds((n_tok, D_FF), BF16), _sds((n_tok, D_MODEL), F32)], [_sds((1, D_MODEL), F32)])


def _outproj_bwd(dx1, o, w_out, nw, tm=256):
    n_tok = dx1.shape[0]

    def body(dx1_ref, o_ref, wo_ref, nw_ref, do_ref, dya_ref, dyb_ref, dnw_ref):
        ov = o_ref[...]
        _, r = _rms_fwd(ov, nw_ref[...])
        do, dnw = _rms_bwd(ov, r, nw_ref[...], dx1_ref[...])
        dob = do.astype(BF16)
        do_ref[...] = dob
        dya_ref[...] = _dot_nt(dob, wo_ref[:GM_WIDTH, :])
        dyb_ref[...] = _dot_nt(dob, wo_ref[GM_WIDTH:, :])
        dnw_ref[...] += dnw

    return _rows_call("outproj_bwd", body, tm, [dx1, o], [w_out, nw],
                      [_sds((n_tok, D_MODEL), BF16), _sds((n_tok, GM_WIDTH), F32), _sds((n_tok, SSM_WIDTH), F32)],
                      [_sds((1, D_MODEL), F32)])


def _gmlp_bwd(p_uv, dya, lnw, lnb, e_bf, et_bf, w_cat, w_stack, bmap, carried=None):
    n_tok = p_uv.shape[0]

    def body(puv_ref, dya_ref, lnw_ref, lnb_ref, e_ref, et_ref, wcat_ref, wstack_ref, bmap_ref,
             dpuv_ref, dws_ref, dbs_ref, dlnw_ref, dlnb_ref, wm_scr, wsm_scr):
        t_stk = lax.broadcasted_iota(jnp.int32, (N_HEADS * CHUNK, CHUNK), 0) % CHUNK
        s_stk = lax.broadcasted_iota(jnp.int32, (N_HEADS * CHUNK, CHUNK), 1)

        @pl.when(pl.program_id(0) == 0)
        def _():
            wm_scr[...] = _causal_w_cat(wcat_ref[...])
            wsm_scr[...] = jnp.where(t_stk >= s_stk, wstack_ref[...], 0.0).astype(BF16)

        lnw_v = lnw_ref[...]
        e_v, et_v = e_ref[...], et_ref[...]
        u, v, gu, tu, tv, rstd, xhat, vn = _gmlp_common(puv_ref[...], lnw_v, lnb_ref[...], e_v, et_v)
        vnb = vn.astype(BF16)
        mixed = jnp.dot(wm_scr[...], _head_blocks(vnb), preferred_element_type=F32) + bmap_ref[...]
        dy = dya_ref[...]
        du = dy * mixed * _gelu_grad(u, tu)
        dmixed = dy * gu
        (dbs,) = _seg_dots([dmixed], et_v)
        dbs_ref[...] += dbs
        dblocks = _head_blocks(dmixed.astype(BF16))
        dvn = lax.dot_general(wsm_scr[...], dblocks, (((0,), (0,)), ((), ())), preferred_element_type=F32)
        dws = lax.dot_general(dblocks, vnb, (((1,), (1,)), ((), ())), preferred_element_type=F32)
        dws_ref[...] += jnp.where(t_stk >= s_stk, dws, 0.0)
        dlnw_ref[...] += jnp.sum(dvn * xhat, axis=0, keepdims=True)
        dlnb_ref[...] += jnp.sum(dvn, axis=0, keepdims=True)
        dxh = dvn * lnw_v
        m1, m2 = _seg_dots([dxh, dxh * xhat], et_v)
        m1, m2 = _seg_dots([m1 * (1.0 / HEAD_DIM), m2 * (1.0 / HEAD_DIM)], e_v)
        dgv = rstd * (dxh - m1 - xhat * m2)
        dv = dgv * _gelu_grad(v, tv)
        dpuv_ref[:, :GM_WIDTH] = du.astype(BF16)
        dpuv_ref[:, GM_WIDTH:] = dv.astype(BF16)

    return _rows_call(
        "gmlp_bwd", body, CHUNK, [p_uv, dya], [lnw, lnb, e_bf, et_bf, w_cat, w_stack, bmap],
        [_sds((n_tok, 2 * GM_WIDTH), BF16)],
        [_sds((N_HEADS * CHUNK, CHUNK), F32), _sds((CHUNK, DT_PAD), F32), _sds((1, GM_WIDTH), F32),
         _sds((1, GM_WIDTH), F32)],
        scratch=[pltpu.VMEM((CHUNK, N_HEADS * CHUNK), BF16), pltpu.VMEM((N_HEADS * CHUNK, CHUNK), BF16)],
        carried=carried)


def _ssd_bwd(p_xbc, p_z, p_dt, yssd, sprev, dyb, conv_w, conv_b, dt_bias, a_log, dskip_map, norm_w, e_bf, et_bf, n_seq,
             carried=None):
    n_tok = p_xbc.shape[0]
    nc = n_tok // n_seq // CHUNK

    def body(xr_ref, xprev_ref, z_ref, pdt_ref, yssd_ref, sprev_ref, dyb_ref,
             cw_ref, cb_ref, dtb_ref, alog_ref, dsk_ref, nw_ref, e_ref, et_ref,
             dpxbc_ref, dpz_ref, dpdt_ref, dcw_ref, dcb_ref, ddtb_ref, dalog_ref, ddsk_ref, dnw_ref,
             ds_scr, nxt_scr, dxa_scr):
        step = pl.program_id(1)
        first = jnp.logical_and(pl.program_id(0) == 0, step == 0)

        @pl.when(first)
        def _():
            for a in (dcw_ref, dcb_ref, ddtb_ref, dalog_ref, ddsk_ref, dnw_ref):
                a[...] = jnp.zeros_like(a)

        @pl.when(step == 0)
        def _():
            ds_scr[...] = jnp.zeros_like(ds_scr)
            nxt_scr[...] = jnp.zeros_like(nxt_scr)

        chunk = nc - 1 - step
        xr = xr_ref[...]
        prev = jnp.where(chunk == 0, 0.0, xprev_ref[...])
        et_v = et_ref[...]
        p = _ssd_pre(xr, prev, cw_ref, cb_ref[...], pdt_ref[...], dtb_ref[...], alog_ref[...], e_ref[...])
        last, e_exp, dte, cd = _ssd_maps(p)
        rowi = p["rowi"]
        xs = p["xa"][:, :SSM_WIDTH]
        xd = xs * p["dt_map"]
        a_cs_t = p["a_cs"].T
        tri = _tril_mask()
        dsk = dsk_ref[...]
        nw_v = nw_ref[...]

        yv = yssd_ref[...]
        zv = z_ref[...]
        sz, zg, yg, _, rs = _gate_fwd(yv, zv, nw_v)
        dout = dyb_ref[...]
        for g in range(SSM_GROUPS):
            gs = slice(g * GROUP_W, (g + 1) * GROUP_W)
            dyg_g, dnw_g = _rms_bwd(yg[:, gs], rs[g], nw_v[:, gs], dout[:, gs])
            dnw_ref[:, gs] += dnw_g
            dxa_scr[:, gs] = dyg_g
        dyg = dxa_scr[:, :SSM_WIDTH]
        d_y = dyg * zg
        dpz_ref[...] = (dyg * yv * (sz + zv * sz * (1.0 - sz))).astype(BF16)

        s_prev = sprev_ref[...]
        ds_next = ds_scr[...]
        lane_dt = lax.broadcasted_iota(jnp.int32, (1, DT_PAD), 1)
        da_cols = jnp.zeros((CHUNK, DT_PAD), F32)
        for g in range(SSM_GROUPS):
            gs = slice(g * GROUP_W, (g + 1) * GROUP_W)
            b_off = SSM_WIDTH + g * SSM_STATE
            c_off = SSM_WIDTH + (SSM_GROUPS + g) * SSM_STATE
            bm = p["xa"][:, b_off:b_off + SSM_STATE].astype(BF16)
            cm = p["xa"][:, c_off:c_off + SSM_STATE].astype(BF16)
            cb_mat = _dot_nt(cm, bm)
            d_yg = d_y[:, gs]
            d_ygb = d_yg.astype(BF16)
            xdg = xd[:, gs]
            xdgb = xdg.astype(BF16)
            ds_g = ds_next[:, gs]
            sp_g = s_prev[:, gs]
            bds = _dot(bm, ds_g)
            dcs = d_yg * e_exp[:, gs]
            d_c = _dot_nt(dcs, sp_g)
            ds_scr[:, gs] = cd[:, gs] * ds_g + _dot_tn(cm, dcs)
            d_b = _dot_nt(xdg * dte[:, gs], ds_g)
            dxd_g = bds * dte[:, gs]
            sum_dcb = jnp.zeros((CHUNK, CHUNK), F32)
            for r in range(SSM_GROUPS * 2):
                head = g * 4 + r
                mask = _head_lane_mask(GROUP_W, r)
                dm = _head_decay(p["a_cs"], a_cs_t, head, tri)
                m_mat = cb_mat * dm
                g_mat = _dot_nt(jnp.where(mask, d_yg, 0.0), xdgb)
                w_mat = g_mat * m_mat
                sum_dcb = sum_dcb + g_mat * dm
                dxd_g = dxd_g + jnp.where(mask, _dot_tn(m_mat, d_ygb), 0.0)
                da_h = jnp.sum(w_mat - w_mat.T, axis=1, keepdims=True)
                da_cols = da_cols + jnp.where(lane_dt == head, da_h, 0.0)
            d_c = d_c + _dot(sum_dcb, bm)
            d_b = d_b + _dot_tn(sum_dcb, cm)
            dxa_scr[:, b_off:b_off + SSM_STATE] = d_b
            dxa_scr[:, c_off:c_off + SSM_STATE] = d_c
            y_off_g = _dot(cm, sp_g) * e_exp[:, gs]
            t3 = bds * xdg * dte[:, gs]
            tail = jnp.sum(t3, axis=0, keepdims=True) + jnp.sum(ds_g * sp_g, axis=0, keepdims=True) * cd[:, gs]
            pre_g = d_yg * y_off_g - t3 + jnp.where(last, tail, 0.0)
            s_pre, ddt_g, s_dsk = _seg_dots([pre_g, dxd_g * xs[:, gs], d_yg * xs[:, gs]], et_v[gs, :])
            da_cols = da_cols + s_pre
            ddsk_ref[...] += jnp.sum(s_dsk, axis=0, keepdims=True)
            dxa_scr[:, gs] = dxd_g * p["dt_map"][:, gs] + dsk[:, gs] * d_yg
            if g == 0:
                ddt = ddt_g
            else:
                ddt = ddt + ddt_g
        r_i = lax.broadcasted_iota(jnp.int32, (CHUNK, CHUNK), 0)
        c_i = lax.broadcasted_iota(jnp.int32, (CHUNK, CHUNK), 1)
        ddta = _tri_dot(r_i <= c_i, da_cols, terms=2)
        ddt = ddt + ddta * p["a_neg"]
        dalog_ref[...] += jnp.sum(ddta * p["dt"], axis=0, keepdims=True) * p["a_neg"]
        draw = ddt * _sigmoid(p["pre"])
        ddtb_ref[...] += jnp.sum(draw, axis=0, keepdims=True)
        dpdt_ref[...] = draw.astype(BF16)

        xc = p["xc"]
        sg = p["sg"]
        dxc = dxa_scr[...] * (sg + xc * sg * (1.0 - sg))
        dcb_ref[...] += jnp.sum(dxc, axis=0, keepdims=True)
        for k in range(CONV_K):
            dcw_ref[k] += jnp.sum(dxc * p["shifted"][k], axis=0, keepdims=True)
        nxt = nxt_scr[...]

        def up(s):
            return jnp.where(rowi >= CHUNK - s, pltpu.roll(nxt, CHUNK - s, 0), pltpu.roll(dxc, CHUNK - s, 0))

        dxr = cw_ref[3] * dxc + cw_ref[2] * up(1) + cw_ref[1] * up(2) + cw_ref[0] * up(3)
        dpxbc_ref[...] = dxr.astype(BF16)
        nxt_scr[...] = dxc

    def rows(width):
        return pl.BlockSpec((CHUNK, width), lambda b, s: (b * nc + nc - 1 - s, 0))

    prev_rows = pl.BlockSpec((CHUNK, CONV_CH), lambda b, s: (b * nc + jnp.maximum(nc - 2 - s, 0), 0))

    def whole(shape):
        nd = len(shape)
        return pl.BlockSpec(tuple(shape), lambda b, s: (0,) * nd)

    acc_shapes = [(CONV_K, 1, CONV_CH), (1, CONV_CH), (1, DT_PAD), (1, DT_PAD), (1, DT_PAD), (1, SSM_WIDTH)]
    return _call_carrying(
        body, carried, name="ssd_bwd", grid=(n_seq, nc),
        in_specs=[rows(CONV_CH), prev_rows, rows(SSM_WIDTH), rows(DT_PAD), rows(SSM_WIDTH), rows(SSM_WIDTH),
                  rows(SSM_WIDTH)] + _ssd_const_specs(),
        out_specs=[rows(CONV_CH), rows(SSM_WIDTH), rows(DT_PAD)] + [whole(s) for s in acc_shapes],
        out_shape=tuple([_sds((n_tok, CONV_CH), BF16), _sds((n_tok, SSM_WIDTH), BF16), _sds((n_tok, DT_PAD), BF16)]
                        + [_sds(s, F32) for s in acc_shapes]),
        scratch_shapes=[pltpu.VMEM((SSM_STATE, SSM_WIDTH), F32), pltpu.VMEM((CHUNK, CONV_CH), F32),
                        pltpu.VMEM((CHUNK, CONV_CH), F32)],
        operands=[p_xbc, p_xbc, p_z, p_dt, yssd, sprev, dyb, conv_w, conv_b, dt_bias, a_log, dskip_map, norm_w, e_bf,
                  et_bf])


def _inproj_bwd(dp_uv, dp_xbc, dp_z, dp_dt, x, dx1, w_uv, w_xbc, w_z, w_dt, nw, tm=256, carried=None):
    n_tok = x.shape[0]

    def body(duv_ref, dxbc_ref, dz_ref, ddt_ref, x_ref, dx1_ref, wuv_ref, wxbc_ref, wz_ref, wdt_ref, nw_ref,
             gx_ref, h_ref, dnw_ref):
        dh = _dot_nt(duv_ref[...], wuv_ref[...]) + _dot_nt(dxbc_ref[...], wxbc_ref[...])
        dh = dh + _dot_nt(dz_ref[...], wz_ref[...]) + _dot_nt(ddt_ref[...], wdt_ref[...])
        xv = x_ref[...]
        h, r = _rms_fwd(xv, nw_ref[...])
        dx, dnw = _rms_bwd(xv, r, nw_ref[...], dh)
        gx_ref[...] = dx1_ref[...] + dx
        h_ref[...] = h.astype(BF16)
        dnw_ref[...] += dnw

    return _rows_call("inproj_bwd", body, tm, [dp_uv, dp_xbc, dp_z, dp_dt, x, dx1], [w_uv, w_xbc, w_z, w_dt, nw],
                      [_sds((n_tok, D_MODEL), F32), _sds((n_tok, D_MODEL), BF16)], [_sds((1, D_MODEL), F32)],
                      carried=carried)


def _const_maps():
    lane = jnp.arange(SSM_WIDTH) // HEAD_DIM
    e_bf = (jnp.arange(DT_PAD)[:, None] == lane[None, :]).astype(BF16)
    return e_bf, e_bf.T


def _pad_lanes(v, width):
    return jnp.pad(v, ((0, 0), (0, width - v.shape[1])))


SHARD_COLS = IN_COLS // N_CHIPS
_UV_END = 2 * GM_WIDTH
_Z_END = _UV_END + SSM_WIDTH
_XBC_END = _Z_END + CONV_CH


def _cols_from_shards(w4, lo, hi):
    pieces = []
    for j in range(N_CHIPS):
        a, b = max(lo, j * SHARD_COLS), min(hi, (j + 1) * SHARD_COLS)
        if a < b:
            pieces.append(w4[j][:, a - j * SHARD_COLS:b - j * SHARD_COLS])
    return pieces[0] if len(pieces) == 1 else jnp.concatenate(pieces, axis=1)


def _shards_from_cols(blocks):
    shards = []
    for j in range(N_CHIPS):
        pieces = []
        for arr, lo, hi in blocks:
            a, b = max(lo, j * SHARD_COLS), min(hi, (j + 1) * SHARD_COLS)
            if a < b:
                pieces.append(arr[:, a - lo:b - lo])
        shards.append(pieces[0] if len(pieces) == 1 else jnp.concatenate(pieces, axis=1))
    return jnp.stack(shards)


def _forward_backward(x, tgt, w_in4, conv_w, small, out_shard, up_shard, down_shard, core):
    n_seq, seq_len, _ = x.shape
    n_tok = n_seq * seq_len
    x2 = x.reshape(n_tok, D_MODEL)
    tgt2 = tgt.reshape(n_tok, D_MODEL)
    e_bf, et_bf = _const_maps()

    w_uv = _cols_from_shards(w_in4, 0, _UV_END)
    w_z = _cols_from_shards(w_in4, _UV_END, _Z_END)
    w_xbc = _cols_from_shards(w_in4, _Z_END, _XBC_END)
    w_dt = _pad_lanes(_cols_from_shards(w_in4, _XBC_END, IN_COLS), DT_PAD)

    nw_pre = small["norm_mix_pre"]
    lnw = small["gm_ln_w"].reshape(1, GM_WIDTH)
    lnb = small["gm_ln_b"].reshape(1, GM_WIDTH)
    w_stack = small["gm_w_s"].reshape(N_HEADS * CHUNK, CHUNK)
    w_cat = jnp.transpose(small["gm_w_s"], (1, 0, 2)).reshape(CHUNK, N_HEADS * CHUNK)
    bmap = jnp.repeat(small["gm_b_s"].T, HEAD_DIM, axis=1)
    cw3 = conv_w.reshape(CONV_K, 1, CONV_CH)
    conv_b = small["conv_b"]
    dt_bias = _pad_lanes(small["dt_bias"], DT_PAD)
    a_log = _pad_lanes(small["a_log"], DT_PAD)
    dskip_map = jnp.repeat(small["d_skip"], HEAD_DIM, axis=1)
    ssm_nw = small["ssm_norm_w"]

    half = down_shard.shape[0] // 2
    p_uv, p_xbc, p_z, p_dt, w_out4, w_down_a = _inproj_fwd(
        x2, nw_pre, w_uv, w_xbc, w_z, w_dt, carried=_allgather_exchange([out_shard, down_shard[:half]]))
    ssd_consts = (cw3, conv_b, dt_bias, a_log, dskip_map, ssm_nw, e_bf, et_bf)
    ya, yb, yssd, sprev, w_up4, w_down_b = _mixer_fwd(
        p_uv, p_xbc, p_z, p_dt, lnw, lnb, w_cat, bmap, *ssd_consts, n_seq,
        carried=_allgather_exchange([up_shard, down_shard[half:]]))
    w_out_b = w_out4.reshape(D_MODEL, D_MODEL)
    o, x1, h2 = _outproj_fwd(ya, yb, x2, w_out_b, small["norm_mix_post"], small["norm_ffn_pre"])
    f, dd, dy, loss_acc, d_nffn_post = _mlp_fwd(h2, x1, tgt2, w_up4, w_down_a, w_down_b, small["norm_ffn_post"])

    dup, dx1, d_nffn_pre = _mlp_bwd(dd, f, x1, dy, w_down_a, w_down_b, w_up4, small["norm_ffn_pre"])
    tk = min(DW_TOKENS_PER_STEP, n_tok)
    g_up = _matmul_tn("dw_up", h2, dup, D_MODEL, D_MODEL, tk, stacked=True)
    g_down = _matmul_tn("dw_down", f, dd, 1024, D_MODEL, tk).reshape(N_CHIPS, D_FF // N_CHIPS, D_MODEL)
    do, dya, dyb, d_nmix_post = _outproj_bwd(dx1, o, w_out_b, small["norm_mix_post"])
    g_out_a = _matmul_tn("dw_out_a", ya, do, GM_WIDTH, D_MODEL, tk)
    g_out_b = _matmul_tn("dw_out_b", yb, do, SSM_WIDTH, D_MODEL, tk)
    g_out = jnp.concatenate([g_out_a, g_out_b], axis=0).reshape(N_CHIPS, D_MODEL // N_CHIPS, D_MODEL)
    got_up, got_down, got_out = _pair_exchange("grad_pair_exchange_mlp", [g_up, g_down, g_out])
    h_up = _pair_sum(core, g_up, got_up, 256)
    h_down = _pair_sum(core, g_down, got_down, 256)
    h_out = _pair_sum(core, g_out, got_out, 128)
    dp_uv, d_ws, d_bs_t, d_lnw, d_lnb, slab_up = _gmlp_bwd(
        p_uv, dya, lnw, lnb, e_bf, et_bf, w_cat, w_stack, bmap, carried=_chip_exchange([h_up]))
    early = {
        "gm_ln_w": d_lnw.reshape(N_HEADS, HEAD_DIM), "gm_ln_b": d_lnb.reshape(N_HEADS, HEAD_DIM),
        "gm_w_s": d_ws.reshape(N_HEADS, CHUNK, CHUNK), "gm_b_s": d_bs_t[:, :N_HEADS].T,
        "norm_mix_post": d_nmix_post, "norm_ffn_pre": d_nffn_pre, "norm_ffn_post": d_nffn_post,
    }
    packed_early = _pack(early, tuple(early), tail=loss_acc[0, 0].reshape(1))
    (dp_xbc, dp_z, dp_dt, d_cw, d_cb, d_dtb, d_alog, d_dsk, d_ssm_nw, slab_down, slab_out, all_early) = _ssd_bwd(
        p_xbc, p_z, p_dt, yssd, sprev, dyb, *ssd_consts, n_seq,
        carried=_both(_chip_exchange([h_down, h_out]), _device_gather_exchange(packed_early)))
    gx, h, d_nmix_pre = _inproj_bwd(dp_uv, dp_xbc, dp_z, dp_dt, x2, dx1, w_uv, w_xbc, w_z, w_dt, nw_pre)
    late = {
        "norm_mix_pre": d_nmix_pre, "conv_w": d_cw.reshape(CONV_K, CONV_CH), "conv_b": d_cb,
        "dt_bias": d_dtb[:, :N_HEADS], "a_log": d_alog[:, :N_HEADS], "d_skip": d_dsk[:, :N_HEADS],
        "ssm_norm_w": d_ssm_nw,
    }
    g_uv, all_late = _matmul_tn("dw_in_uv", h, dp_uv, D_MODEL, 2 * GM_WIDTH, tk,
                                carried=_device_gather_exchange(_pack(late, tuple(late))))
    sum_early = _ordered_sum("small_sum_early", all_early)
    small_sum = _unpack(sum_early, {n: v.shape for n, v in early.items()}, tuple(early))
    small_sum.update(_unpack(_ordered_sum("small_sum_late", all_late), {n: v.shape for n, v in late.items()}, tuple(late)))
    loss = sum_early.reshape(-1)[sum(v.size for v in early.values())]
    g_xbc = _matmul_tn("dw_in_xbc", h, dp_xbc, D_MODEL, CONV_CH, tk)
    g_z = _matmul_tn("dw_in_z", h, dp_z, D_MODEL, SSM_WIDTH, tk)
    g_dt = _matmul_tn("dw_in_dt", h, dp_dt, D_MODEL, DT_PAD, tk)

    g_in = _shards_from_cols([(g_uv, 0, _UV_END), (g_z, _UV_END, _Z_END), (g_xbc, _Z_END, _XBC_END),
                              (g_dt, _XBC_END, IN_COLS)])

    (got_in,) = _pair_exchange("grad_pair_exchange_in", [g_in])
    h_in = _pair_sum(core, g_in, got_in, 256)
    (slab_in,) = _run_exchange("grad_chip_exchange", _chip_exchange([h_in]))
    reds = [_chip_sum(core, s, tm) for s, tm in ((slab_in, 256), (slab_out, 128), (slab_up, 256), (slab_down, 256))]
    big_grads = dict(zip(("w_in", "w_out", "w_up", "w_down"), _pair_gather(reds)))

    return loss, gx.reshape(x.shape), big_grads, small_sum


_HBM = pl.BlockSpec(memory_space=pltpu.HBM)


D2D_CHUNKS = 8
ROW_ALIGN = 16


def _row_chunks(rows, n_chunks):
    size = min(max(rows // n_chunks, ROW_ALIGN), rows)
    assert rows % size == 0
    return [(start, size) for start in range(0, rows, size)]


def _position():
    x, y, c = lax.axis_index("x"), lax.axis_index("y"), lax.axis_index("c")
    chips = [(1 - x, y), (x, 1 - y), (1 - x, 1 - y)]
    return x, y, c, chips


def _allgather_exchange(arrs):
    n = len(arrs)

    def copies(ins, outs, send_sems, recv_sems, local_sems):
        x, y, c, chips = _position()
        me = 2 * x + y
        sibling = (x, y, 1 - c)

        def copy(a, k, src, dst, to):
            return pltpu.make_async_remote_copy(src_ref=src, dst_ref=dst, send_sem=send_sems.at[a, k],
                                                recv_sem=recv_sems.at[a, k], device_id=to, device_id_type=MESH)

        def half_rows(a, pc):
            half = ins[a].shape[0] // 2
            return pl.ds(pc * half, half)

        local = [pltpu.make_async_copy(ins[a], outs[a].at[me], local_sems.at[a]) for a in range(n)]
        ici_out = [[copy(a, k, ins[a].at[half_rows(a, c)], outs[a].at[me, half_rows(a, c)], (px, py, c))
                    for k, (px, py) in enumerate(chips)] for a in range(n)]
        return c, chips, sibling, copy, half_rows, local, ici_out

    def start(ins, outs, send_sems, recv_sems, local_sems):
        _, _, _, _, _, local, ici_out = copies(ins, outs, send_sems, recv_sems, local_sems)
        for cp in local:
            cp.start()
        for a in range(n):
            for cp in ici_out[a]:
                cp.start()

    def finish(ins, outs, send_sems, recv_sems, local_sems):
        c, chips, sibling, copy, half_rows, local, ici_out = copies(ins, outs, send_sems, recv_sems, local_sems)
        passed = []
        for a in range(n):
            half = ins[a].shape[0] // 2
            for k, (px, py) in enumerate(chips):
                blk = outs[a].at[2 * px + py, half_rows(a, c)]
                copy(a, k, blk, blk, (px, py, c)).wait_recv()
                for first, size in _row_chunks(half, D2D_CHUNKS):
                    piece = outs[a].at[2 * px + py, pl.ds(c * half + first, size)]
                    copy(a, 3 + k, piece, piece, sibling).start()
                passed.append(copy(a, 3 + k, blk, blk, sibling))
        for a in range(n):
            for k, (px, py) in enumerate(chips):
                blk = outs[a].at[2 * px + py, half_rows(a, 1 - c)]
                copy(a, 3 + k, blk, blk, sibling).wait_recv()
        for a in range(n):
            for cp in ici_out[a]:
                cp.wait_send()
        for cp in passed:
            cp.wait_send()
        for cp in local:
            cp.wait()

    return _Carried(arrs, [_sds((N_CHIPS,) + a.shape, a.dtype) for a in arrs],
                    [pltpu.SemaphoreType.DMA((n, 6)), pltpu.SemaphoreType.DMA((n, 6)), pltpu.SemaphoreType.DMA((n,))],
                    start, finish)


def _run_exchange(name, exchange):
    n_in, n_out = len(exchange.ins), len(exchange.out_shapes)

    def body(*refs):
        ins, outs, sems = refs[:n_in], refs[n_in:n_in + n_out], refs[n_in + n_out:]
        exchange.start(ins, outs, *sems)
        exchange.finish(ins, outs, *sems)

    return pl.pallas_call(
        body, name=name, out_shape=tuple(exchange.out_shapes), in_specs=[_HBM] * n_in,
        out_specs=tuple([_HBM] * n_out), scratch_shapes=exchange.sems,
    )(*exchange.ins)


def _pair_exchange(name, grads):
    n = len(grads)

    def body(*refs):
        ins, got = refs[:n], refs[n:2 * n]
        send_sems, recv_sems = refs[2 * n:]
        x, y, c, _ = _position()
        sibling = (x, y, 1 - c)

        def copy(a, src, dst):
            return pltpu.make_async_remote_copy(src_ref=src, dst_ref=dst, send_sem=send_sems.at[a],
                                                recv_sem=recv_sems.at[a], device_id=sibling, device_id_type=MESH)

        for a in range(n):
            half = ins[a].shape[1] // 2
            for slab in range(N_CHIPS):
                for start, size in _row_chunks(half, D2D_CHUNKS):
                    copy(a, ins[a].at[slab, pl.ds((1 - c) * half + start, size), :],
                         got[a].at[slab, pl.ds(start, size), :]).start()
        for a in range(n):
            half = ins[a].shape[1] // 2
            copy(a, ins[a].at[:, pl.ds((1 - c) * half, half), :], got[a]).wait()

    halves = tuple(_sds((N_CHIPS, g.shape[1] // 2, g.shape[2]), g.dtype) for g in grads)
    return pl.pallas_call(
        body, name=name, out_shape=halves, in_specs=[_HBM] * n, out_specs=tuple([_HBM] * n),
        scratch_shapes=[pltpu.SemaphoreType.DMA((n,)), pltpu.SemaphoreType.DMA((n,))],
    )(*grads)


def _chip_exchange(hsums):
    n = len(hsums)

    def copies(ins, outs, send_sems, recv_sems, local_sems):
        x, y, c, chips = _position()
        me = 2 * x + y
        cps = []
        for a in range(n):
            cps.append(pltpu.make_async_copy(ins[a].at[me], outs[a].at[me], local_sems.at[a]))
            for k, (px, py) in enumerate(chips):
                cps.append(pltpu.make_async_remote_copy(
                    src_ref=ins[a].at[2 * px + py], dst_ref=outs[a].at[me], send_sem=send_sems.at[a, k],
                    recv_sem=recv_sems.at[a, k], device_id=(px, py, c), device_id_type=MESH))
        return cps

    def start(*refs):
        for cp in copies(*refs):
            cp.start()

    def finish(*refs):
        for cp in copies(*refs):
            cp.wait()

    return _Carried(hsums, [_sds(h.shape, h.dtype) for h in hsums],
                    [pltpu.SemaphoreType.DMA((n, 3)), pltpu.SemaphoreType.DMA((n, 3)), pltpu.SemaphoreType.DMA((n,))],
                    start, finish)


def _pair_gather(bufs):
    n = len(bufs)

    def body(*refs):
        outs = refs[n:2 * n]
        send_sems, recv_sems = refs[2 * n:]
        x, y, c, _ = _position()
        sibling = (x, y, 1 - c)

        def copy(a, rows):
            return pltpu.make_async_remote_copy(src_ref=rows, dst_ref=rows, send_sem=send_sems.at[a],
                                                recv_sem=recv_sems.at[a], device_id=sibling, device_id_type=MESH)

        for a in range(n):
            half = outs[a].shape[0] // 2
            for start, size in _row_chunks(half, 2 * D2D_CHUNKS):
                copy(a, outs[a].at[pl.ds(c * half + start, size), :]).start()
        for a in range(n):
            half = outs[a].shape[0] // 2
            copy(a, outs[a].at[pl.ds(c * half, half), :]).wait_send()
            copy(a, outs[a].at[pl.ds((1 - c) * half, half), :]).wait_recv()

    return pl.pallas_call(
        body, name="grad_pair_gather", out_shape=tuple(_sds(b.shape, b.dtype) for b in bufs),
        in_specs=[_HBM] * n, out_specs=tuple([_HBM] * n), input_output_aliases={a: a for a in range(n)},
        scratch_shapes=[pltpu.SemaphoreType.DMA((n,)), pltpu.SemaphoreType.DMA((n,))],
    )(*bufs)


def _device_gather_exchange(packed):
    def copies(ins, outs, send_sems, recv_sems, local_sem):
        (x_ref,), (all_ref,) = ins, outs
        x, y, c, chips = _position()
        me, sibling = (x, y, c), (x, y, 1 - c)

        def slab(px, py, pc):
            return all_ref.at[4 * px + 2 * py + pc]

        def copy(k, block, to, src=None):
            return pltpu.make_async_remote_copy(
                src_ref=slab(*block) if src is None else src, dst_ref=slab(*block), send_sem=send_sems.at[k],
                recv_sem=recv_sems.at[k], device_id=to, device_id_type=MESH)

        mine = pltpu.make_async_copy(x_ref, slab(*me), local_sem)
        first = [copy(0, me, sibling, src=x_ref)]
        first += [copy(1 + j, me, (*chip, c), src=x_ref) for j, chip in enumerate(chips)]
        passed = [copy(4 + j, (*chip, c), sibling) for j, chip in enumerate(chips)]
        return c, chips, me, sibling, copy, mine, first, passed

    def start(ins, outs, send_sems, recv_sems, local_sem):
        _, _, _, _, _, mine, first, _ = copies(ins, outs, send_sems, recv_sems, local_sem)
        mine.start()
        for cp in first:
            cp.start()

    def finish(ins, outs, send_sems, recv_sems, local_sem):
        c, chips, me, sibling, copy, mine, first, passed = copies(ins, outs, send_sems, recv_sems, local_sem)
        for j, chip in enumerate(chips):
            copy(1 + j, (*chip, c), me).wait_recv()
            passed[j].start()
        copy(0, sibling, me).wait_recv()
        for j, chip in enumerate(chips):
            copy(4 + j, (*chip, 1 - c), me).wait_recv()
        for cp in first + passed:
            cp.wait_send()
        mine.wait()

    return _Carried([packed], [_sds((N_DEV,) + packed.shape, F32)],
                    [pltpu.SemaphoreType.DMA((7,)), pltpu.SemaphoreType.DMA((7,)), pltpu.SemaphoreType.DMA],
                    start, finish)


def _ordered_sum(name, slabs):
    _, m_per, n_cols = slabs.shape

    def body(s_ref, o_ref):
        acc = s_ref[0]
        for d in range(1, N_DEV):
            acc = acc + s_ref[d]
        o_ref[...] = acc

    vmem = pl.BlockSpec(memory_space=pltpu.VMEM)
    return pl.pallas_call(body, name=name, out_shape=_sds((m_per, n_cols), F32), in_specs=[vmem], out_specs=vmem)(slabs)


def _pair_sum(core, own, got, tm):
    _, half, cols = got.shape
    nb = half // tm

    def body(c_ref, a_ref, b_ref, o_ref):
        o_ref[...] = (a_ref[...] + b_ref[...]).astype(BF16)

    return pl.pallas_call(
        body, name="grad_pair_sum", out_shape=_sds(got.shape, BF16),
        grid_spec=pltpu.PrefetchScalarGridSpec(
            num_scalar_prefetch=1, grid=(N_CHIPS, nb),
            in_specs=[pl.BlockSpec((None, tm, cols), lambda s, i, c_ref: (s, c_ref[0] * nb + i, 0)),
                      pl.BlockSpec((None, tm, cols), lambda s, i, c_ref: (s, i, 0))],
            out_specs=pl.BlockSpec((None, tm, cols), lambda s, i, c_ref: (s, i, 0))),
        compiler_params=_cparams(2),
    )(core, own, got)


def _chip_sum(core, slabs, tm):
    _, half, cols = slabs.shape
    nb = half // tm

    def body(c_ref, s_ref, o_ref):
        acc = s_ref[0].astype(F32)
        for k in range(1, N_CHIPS):
            acc = acc + s_ref[k].astype(F32)
        o_ref[...] = acc

    return pl.pallas_call(
        body, name="grad_chip_sum", out_shape=_sds((2 * half, cols), F32),
        grid_spec=pltpu.PrefetchScalarGridSpec(
            num_scalar_prefetch=1, grid=(nb,),
            in_specs=[pl.BlockSpec((N_CHIPS, tm, cols), lambda i, c_ref: (0, i, 0))],
            out_specs=pl.BlockSpec((tm, cols), lambda i, c_ref: (c_ref[0] * nb + i, 0))),
        compiler_params=_cparams(1),
    )(core, slabs)


def _adam_math(w, g, m, v):
    m2 = ADAM_B1 * m + (1.0 - ADAM_B1) * g
    v2 = ADAM_B2 * v + (1.0 - ADAM_B2) * (g * g)
    m_hat = m2 / (1.0 - ADAM_B1 ** ADAM_STEP)
    v_hat = v2 / (1.0 - ADAM_B2 ** ADAM_STEP)
    delta = -ADAM_LR * (m_hat / (jnp.sqrt(v_hat) + ADAM_EPS) + ADAM_WD * w)
    return delta, m2, v2


def _adamw(name, w, g, m, v, tm):
    def body(w_ref, g_ref, m_ref, v_ref, gout_ref, d_ref, m2_ref, v2_ref):
        gv = g_ref[...]
        d, m2, v2 = _adam_math(w_ref[...], gv, m_ref[...], v_ref[...])
        gout_ref[...] = gv
        d_ref[...] = d
        m2_ref[...] = m2
        v2_ref[...] = v2

    return _rows_call(name, body, tm, [w, g, m, v], [], [_sds(w.shape, F32)] * 4)


_SMALL_NAMES = ("norm_mix_pre", "gm_ln_w", "gm_ln_b", "gm_w_s", "gm_b_s", "conv_w", "conv_b", "dt_bias", "a_log",
                "d_skip", "ssm_norm_w", "norm_mix_post", "norm_ffn_pre", "norm_ffn_post")
_PACK_COLS = 1024


def _pack(parts, names=_SMALL_NAMES, tail=None):
    pieces = [parts[n].reshape(-1) for n in names]
    flat = jnp.concatenate(pieces if tail is None else pieces + [tail])
    rows = -(-flat.shape[0] // (8 * _PACK_COLS)) * 8
    flat = jnp.pad(flat, (0, rows * _PACK_COLS - flat.shape[0]))
    return flat.reshape(rows, _PACK_COLS)


def _unpack(packed, shapes, names=_SMALL_NAMES):
    flat = packed.reshape(-1)
    out, off = {}, 0
    for n in names:
        size = 1
        for s in shapes[n]:
            size *= s
        out[n] = flat[off:off + size].reshape(shapes[n])
        off += size
    return out


def kernel(x, norm_mix_pre, w_in, gm_ln_w, gm_ln_b, gm_w_s, gm_b_s, conv_w, conv_b, dt_bias, a_log, d_skip, ssm_norm_w, w_out, norm_mix_post, norm_ffn_pre, w_up, w_down, norm_ffn_post, loss_target, m_norm_mix_pre, m_w_in, m_gm_ln_w, m_gm_ln_b, m_gm_w_s, m_gm_b_s, m_conv_w, m_conv_b, m_dt_bias, m_a_log, m_d_skip, m_ssm_norm_w, m_w_out, m_norm_mix_post, m_norm_ffn_pre, m_w_up, m_w_down, m_norm_ffn_post, v_norm_mix_pre, v_w_in, v_gm_ln_w, v_gm_ln_b, v_gm_w_s, v_gm_b_s, v_conv_w, v_conv_b, v_dt_bias, v_a_log, v_d_skip, v_ssm_norm_w, v_w_out, v_norm_mix_post, v_norm_ffn_pre, v_w_up, v_w_down, v_norm_ffn_post):
    params = dict(norm_mix_pre=norm_mix_pre, w_in=w_in, gm_ln_w=gm_ln_w, gm_ln_b=gm_ln_b, gm_w_s=gm_w_s, gm_b_s=gm_b_s,
                  conv_w=conv_w, conv_b=conv_b, dt_bias=dt_bias, a_log=a_log, d_skip=d_skip, ssm_norm_w=ssm_norm_w,
                  w_out=w_out, norm_mix_post=norm_mix_post, norm_ffn_pre=norm_ffn_pre, w_up=w_up, w_down=w_down,
                  norm_ffn_post=norm_ffn_post)
    mom1 = dict(norm_mix_pre=m_norm_mix_pre, w_in=m_w_in, gm_ln_w=m_gm_ln_w, gm_ln_b=m_gm_ln_b, gm_w_s=m_gm_w_s,
                gm_b_s=m_gm_b_s, conv_w=m_conv_w, conv_b=m_conv_b, dt_bias=m_dt_bias, a_log=m_a_log, d_skip=m_d_skip,
                ssm_norm_w=m_ssm_norm_w, w_out=m_w_out, norm_mix_post=m_norm_mix_post, norm_ffn_pre=m_norm_ffn_pre,
                w_up=m_w_up, w_down=m_w_down, norm_ffn_post=m_norm_ffn_post)
    mom2 = dict(norm_mix_pre=v_norm_mix_pre, w_in=v_w_in, gm_ln_w=v_gm_ln_w, gm_ln_b=v_gm_ln_b, gm_w_s=v_gm_w_s,
                gm_b_s=v_gm_b_s, conv_w=v_conv_w, conv_b=v_conv_b, dt_bias=v_dt_bias, a_log=v_a_log, d_skip=v_d_skip,
                ssm_norm_w=v_ssm_norm_w, w_out=v_w_out, norm_mix_post=v_norm_mix_post, norm_ffn_pre=v_norm_ffn_pre,
                w_up=v_w_up, w_down=v_w_down, norm_ffn_post=v_norm_ffn_post)
    names = list(params)
    big = ("w_in", "w_out", "w_up", "w_down")
    chip = 2 * lax.axis_index("x") + lax.axis_index("y")

    shards = {n: params[n][0].astype(BF16) for n in big}
    conv_shard = jnp.pad(conv_w[0], ((0, 16 - CONV_K), (0, 0)))
    g_in4, g_conv4 = _run_exchange("allgather_w_in", _allgather_exchange([shards["w_in"], conv_shard]))
    conv_full = jnp.transpose(g_conv4[:, :CONV_K, :], (1, 0, 2)).reshape(CONV_K, CONV_CH)

    small = {n: params[n][0] if params[n].ndim >= 3 else params[n] for n in _SMALL_NAMES if n != "conv_w"}
    core = lax.axis_index("c").astype(jnp.int32).reshape(1)
    loss, grad_x, big_grads, small_sum = _forward_backward(
        x, loss_target, g_in4, conv_full, small, shards["w_out"], shards["w_up"], shards["w_down"], core)

    small_sum["conv_w"] = lax.dynamic_slice_in_dim(small_sum["conv_w"], chip * (CONV_CH // N_CHIPS), CONV_CH // N_CHIPS, axis=1)

    grads, delta, new_m, new_v = {}, {}, {}, {}
    for n, tm in zip(big, (256, 128, 256, 256)):
        g, d, m2, v2 = _adamw("adamw_" + n, params[n][0], big_grads[n], mom1[n][0], mom2[n][0], tm)
        grads[n], delta[n], new_m[n], new_v[n] = g[None], d[None], m2[None], v2[None]
    local_shapes = {n: params[n].shape[1:] if params[n].ndim >= 3 else params[n].shape for n in _SMALL_NAMES}
    flat = lambda tree: {n: tree[n].reshape(local_shapes[n]) for n in _SMALL_NAMES}
    packed = [_pack(flat(t)) for t in (params, small_sum, mom1, mom2)]
    _, d_p, m_p, v_p = _adamw("adamw_small", *packed, packed[0].shape[0])
    for src, dst in ((d_p, delta), (m_p, new_m), (v_p, new_v)):
        for n, val in _unpack(src, local_shapes).items():
            dst[n] = val.reshape(params[n].shape)
    for n in _SMALL_NAMES:
        grads[n] = small_sum[n].reshape(params[n].shape)

    out = [loss, grad_x]
    for tree in (grads, delta, new_m, new_v):
        out += [tree[n] for n in names]
    return tuple(out)
```

```python
import functools

import jax
import jax.numpy as jnp
from jax import lax
from jax.experimental import pallas as pl
from jax.experimental.pallas import tpu as pltpu

F32 = jnp.float32
BF16 = jnp.bfloat16
HI = lax.Precision.HIGHEST
MESH = pl.DeviceIdType.MESH

EPS = 1e-6
D_MODEL = 1024
GM_WIDTH = 512
SSM_WIDTH = 512
N_HEADS = 8
HEAD_DIM = 64
CHUNK = 128
SSM_GROUPS = 2
GROUP_W = SSM_WIDTH // SSM_GROUPS
SSM_STATE = 128
CONV_K = 4
CONV_CH = 1024
D_FF = 4096
IN_COLS = 2568
DT_PAD = 128
N_CHIPS = 4
N_DEV = 8

ADAM_LR = 0.001
ADAM_B1 = 0.9
ADAM_B2 = 0.999
ADAM_EPS = 1e-08
ADAM_WD = 0.01
ADAM_STEP = 10

VMEM_LIMIT_BYTES = 56 * 1024 * 1024
FF_TILE = 512
DW_TOKENS_PER_STEP = 2048


def _cparams(n_axes):
    return pltpu.CompilerParams(dimension_semantics=("arbitrary",) * n_axes, vmem_limit_bytes=VMEM_LIMIT_BYTES)


def _dot(a, b):
    return jnp.dot(a.astype(BF16), b.astype(BF16), preferred_element_type=F32)


def _dot_nt(a, b):
    return lax.dot_general(a.astype(BF16), b.astype(BF16), (((1,), (1,)), ((), ())), preferred_element_type=F32)


def _dot_tn(a, b):
    return lax.dot_general(a.astype(BF16), b.astype(BF16), (((0,), (0,)), ((), ())), preferred_element_type=F32)


def _sigmoid(x):
    return 1.0 / (1.0 + jnp.exp(-x))


_GELU_C = 0.7978845608028654
_GELU_A = 0.044715


def _gelu(x):
    t = jnp.tanh(_GELU_C * (x + _GELU_A * (x * x * x)))
    return 0.5 * x * (1.0 + t), t


def _gelu_grad(x, t):
    return 0.5 * (1.0 + t) + 0.5 * x * (1.0 - t * t) * (_GELU_C * (1.0 + 3.0 * _GELU_A * x * x))


def _rms_fwd(x, w):
    r = lax.rsqrt(jnp.mean(x * x, axis=-1, keepdims=True) + EPS)
    return x * r * w, r


def _rms_bwd(x, r, w, dy):
    g = dy * w
    dx = r * g - x * (r * r * r) * jnp.mean(g * x, axis=-1, keepdims=True)
    dw = jnp.sum(dy * x * r, axis=0, keepdims=True)
    return dx, dw


class _Carried:
    def __init__(self, ins, out_shapes, sems, start, finish):
        self.ins, self.out_shapes, self.sems, self.start, self.finish = list(ins), list(out_shapes), list(sems), start, finish


def _both(first, second):
    n_i, n_o, n_s = len(first.ins), len(first.out_shapes), len(first.sems)

    def split(ins, outs, sems):
        return (ins[:n_i], outs[:n_o], sems[:n_s]), (ins[n_i:], outs[n_o:], sems[n_s:])

    def start(ins, outs, *sems):
        (i1, o1, s1), (i2, o2, s2) = split(ins, outs, sems)
        first.start(i1, o1, *s1)
        second.start(i2, o2, *s2)

    def finish(ins, outs, *sems):
        (i1, o1, s1), (i2, o2, s2) = split(ins, outs, sems)
        first.finish(i1, o1, *s1)
        second.finish(i2, o2, *s2)

    return _Carried(first.ins + second.ins, first.out_shapes + second.out_shapes, first.sems + second.sems, start, finish)


def _split_carried(refs, n_in, n_out, n_scratch, carried):
    n_ci, n_co, n_cs = len(carried.ins), len(carried.out_shapes), len(carried.sems)
    ins, rest = refs[:n_in], refs[n_in:]
    c_ins, rest = rest[:n_ci], rest[n_ci:]
    outs, rest = rest[:n_out], rest[n_out:]
    c_outs, rest = rest[:n_co], rest[n_co:]
    scr, c_sems = rest[:n_scratch], rest[n_scratch:]
    assert len(c_sems) == n_cs
    return tuple(ins) + tuple(outs) + tuple(scr), c_ins, c_outs, c_sems


def _rows_call(name, body, tm, row_ins, const_ins, row_outs, acc_outs=(), scratch=(), carried=None):
    n_rows = row_ins[0].shape[0]
    assert n_rows % tm == 0
    n_steps = n_rows // tm
    n_in = len(row_ins) + len(const_ins)
    n_ro = len(row_outs)
    n_acc = len(acc_outs)

    def kern(*refs):
        accs = refs[n_in + n_ro:n_in + n_ro + n_acc]

        @pl.when(pl.program_id(0) == 0)
        def _():
            for a in accs:
                a[...] = jnp.zeros_like(a)

        body(*refs)

    def whole(shape):
        nd = len(shape)
        return pl.BlockSpec(tuple(shape), lambda i: (0,) * nd)

    in_specs = [pl.BlockSpec((tm, a.shape[1]), lambda i: (i, 0)) for a in row_ins]
    in_specs += [whole(a.shape) for a in const_ins]
    out_specs = [pl.BlockSpec((tm, s.shape[1]), lambda i: (i, 0)) for s in row_outs]
    out_specs += [whole(s.shape) for s in acc_outs]
    return _call_carrying(
        kern, carried, name=name, grid=(n_steps,), in_specs=in_specs, out_specs=out_specs,
        out_shape=tuple(row_outs) + tuple(acc_outs), scratch_shapes=list(scratch), operands=list(row_ins) + list(const_ins))


def _call_carrying(body, carried, *, name, grid, in_specs, out_specs, out_shape, scratch_shapes, operands):
    n_in, n_out, n_scratch = len(in_specs), len(out_specs), len(scratch_shapes)
    kern = body
    if carried is not None:
        def kern(*refs):
            plain, c_ins, c_outs, c_sems = _split_carried(refs, n_in, n_out, n_scratch, carried)
            first, last = True, True
            for d, size in enumerate(grid):
                first = jnp.logical_and(first, pl.program_id(d) == 0)
                last = jnp.logical_and(last, pl.program_id(d) == size - 1)

            @pl.when(first)
            def _():
                carried.start(c_ins, c_outs, *c_sems)

            body(*plain)

            @pl.when(last)
            def _():
                carried.finish(c_ins, c_outs, *c_sems)

        in_specs = list(in_specs) + [_HBM] * len(carried.ins)
        out_specs = list(out_specs) + [_HBM] * len(carried.out_shapes)
        out_shape = tuple(out_shape) + tuple(carried.out_shapes)
        operands = list(operands) + carried.ins
        scratch_shapes = list(scratch_shapes) + carried.sems
    return pl.pallas_call(
        kern, name=name, grid=grid, in_specs=in_specs, out_specs=out_specs, out_shape=out_shape,
        scratch_shapes=scratch_shapes, compiler_params=_cparams(len(grid)),
    )(*operands)


def _sds(shape, dtype):
    return jax.ShapeDtypeStruct(tuple(shape), dtype)


def _matmul_tn(name, a, b, tm, tn, tk, stacked=False, carried=None):
    k_dim, m_dim = a.shape
    n_dim = b.shape[1]
    assert m_dim % tm == 0 and n_dim % tn == 0 and k_dim % tk == 0

    def kern(a_ref, b_ref, o_ref):
        @pl.when(pl.program_id(2) == 0)
        def _():
            o_ref[...] = jnp.zeros_like(o_ref)

        o_ref[...] += _dot_tn(a_ref[...], b_ref[...])

    if stacked:
        assert tm == m_dim
        out_shape = _sds((n_dim // tn, m_dim, tn), F32)
        out_spec = pl.BlockSpec((None, tm, tn), lambda i, j, k: (j, i, 0))
    else:
        out_shape = _sds((m_dim, n_dim), F32)
        out_spec = pl.BlockSpec((tm, tn), lambda i, j, k: (i, j))
    outs = _call_carrying(
        kern, carried, name=name, grid=(m_dim // tm, n_dim // tn, k_dim // tk),
        in_specs=[pl.BlockSpec((tk, tm), lambda i, j, k: (k, i)), pl.BlockSpec((tk, tn), lambda i, j, k: (k, j))],
        out_specs=[out_spec], out_shape=(out_shape,), scratch_shapes=[], operands=[a, b])
    return outs[0] if carried is None else outs


def _inproj_fwd(x, nw, w_uv, w_xbc, w_z, w_dt, tm=256, carried=None):
    n_tok = x.shape[0]

    def body(x_ref, nw_ref, wuv_ref, wxbc_ref, wz_ref, wdt_ref, puv_ref, pxbc_ref, pz_ref, pdt_ref):
        h, _ = _rms_fwd(x_ref[...], nw_ref[...])
        h = h.astype(BF16)
        puv_ref[...] = jnp.dot(h, wuv_ref[...], preferred_element_type=F32)
        pxbc_ref[...] = jnp.dot(h, wxbc_ref[...], preferred_element_type=F32)
        pz_ref[...] = jnp.dot(h, wz_ref[...], preferred_element_type=F32)
        pdt_ref[...] = jnp.dot(h, wdt_ref[...], preferred_element_type=F32)

    return _rows_call(
        "inproj_fwd", body, tm, [x], [nw, w_uv, w_xbc, w_z, w_dt],
        [_sds((n_tok, 2 * GM_WIDTH), F32), _sds((n_tok, CONV_CH), F32), _sds((n_tok, SSM_WIDTH), F32),
         _sds((n_tok, DT_PAD), F32)], carried=carried)


def _head_lane_mask(width, head):
    lane = lax.broadcasted_iota(jnp.int32, (1, width), 1)
    return (lane // HEAD_DIM) == head


def _split_terms(x, terms):
    parts = []
    for _ in range(terms):
        p = x.astype(BF16)
        parts.append(p)
        x = x - p.astype(F32)
    return parts


def _seg_dots(vals, ind, terms=2):
    m = vals[0].shape[0]
    parts = []
    for v in vals:
        parts += _split_terms(v, terms)
    red = jnp.dot(jnp.concatenate(parts, axis=0), ind, preferred_element_type=F32)
    outs = []
    for i in range(len(vals)):
        acc = red[i * terms * m:(i * terms + 1) * m]
        for t in range(1, terms):
            acc = acc + red[(i * terms + t) * m:(i * terms + t + 1) * m]
        outs.append(acc)
    return outs


def _tri_dot(mask, x, terms=3):
    n = x.shape[1]
    red = jnp.dot(mask.astype(BF16), jnp.concatenate(_split_terms(x, terms), axis=1), preferred_element_type=F32)
    acc = red[:, :n]
    for t in range(1, terms):
        acc = acc + red[:, t * n:(t + 1) * n]
    return acc


def _gmlp_common(puv, lnw, lnb, e_bf, et_bf):
    u = puv[:, :GM_WIDTH]
    v = puv[:, GM_WIDTH:]
    gu, tu = _gelu(u)
    gv, tv = _gelu(v)
    (s1,) = _seg_dots([gv], et_bf)
    (mu,) = _seg_dots([s1 * (1.0 / HEAD_DIM)], e_bf)
    xc = gv - mu
    (s2,) = _seg_dots([xc * xc], et_bf)
    (rstd,) = _seg_dots([lax.rsqrt(s2 * (1.0 / HEAD_DIM) + EPS)], e_bf)
    xhat = xc * rstd
    vn = xhat * lnw + lnb
    return u, v, gu, tu, tv, rstd, xhat, vn


def _tril_mask():
    r = lax.broadcasted_iota(jnp.int32, (CHUNK, CHUNK), 0)
    c = lax.broadcasted_iota(jnp.int32, (CHUNK, CHUNK), 1)
    return r >= c


def _head_blocks(v):
    return jnp.concatenate([jnp.where(_head_lane_mask(GM_WIDTH, h), v, jnp.zeros_like(v)) for h in range(N_HEADS)], axis=0)


def _causal_w_cat(w_cat):
    t = lax.broadcasted_iota(jnp.int32, (CHUNK, N_HEADS * CHUNK), 0)
    s = lax.broadcasted_iota(jnp.int32, (CHUNK, N_HEADS * CHUNK), 1) % CHUNK
    return jnp.where(t >= s, w_cat, 0.0).astype(BF16)


def _gmlp_chunk_fwd(puv, lnw, lnb, e_bf, et_bf, wm, bmap):
    _, _, gu, _, _, _, _, vn = _gmlp_common(puv, lnw, lnb, e_bf, et_bf)
    mixed = jnp.dot(wm, _head_blocks(vn.astype(BF16)), preferred_element_type=F32) + bmap
    return (gu * mixed).astype(BF16)


SUBLANES = 8


def _shift_down(x, tail, s):
    main = pltpu.roll(x, s, 0)
    row = lax.broadcasted_iota(jnp.int32, (SUBLANES, 1), 0)
    head = jnp.where(row < s, pltpu.roll(tail, s, 0), main[:SUBLANES])
    return jnp.concatenate([head, main[SUBLANES:]], axis=0)


def _shift_up(x, head_next, s):
    n = x.shape[0]
    main = pltpu.roll(x, n - s, 0)
    row = lax.broadcasted_iota(jnp.int32, (SUBLANES, 1), 0)
    last = jnp.where(row >= SUBLANES - s, pltpu.roll(head_next, SUBLANES - s, 0), main[n - SUBLANES:])
    return jnp.concatenate([main[:n - SUBLANES], last], axis=0)


def _ssd_pre(xr, tail, cw_ref, cb, pdt, dtb, alog, emap):
    rowi = lax.broadcasted_iota(jnp.int32, (CHUNK, 1), 0)
    shifted = [_shift_down(xr, tail, 3), _shift_down(xr, tail, 2), _shift_down(xr, tail, 1), xr]
    xc = cb
    for k in range(CONV_K):
        xc = xc + cw_ref[k] * shifted[k]
    sg = _sigmoid(xc)
    xa = xc * sg
    pre = pdt + dtb
    dt = jnp.maximum(pre, 0.0) + jnp.log(1.0 + jnp.exp(-jnp.abs(pre)))
    a_neg = -jnp.exp(alog)
    a_cs = _tri_dot(_tril_mask(), dt * a_neg)
    acs_map, dt_map = _seg_dots([a_cs, dt], emap, terms=3)
    return dict(shifted=shifted, xc=xc, sg=sg, xa=xa, pre=pre, dt=dt, a_neg=a_neg, a_cs=a_cs,
                acs_map=acs_map, dt_map=dt_map, rowi=rowi)


def _ssd_maps(p):
    last = p["rowi"] == CHUNK - 1
    aq_map = jnp.sum(jnp.where(last, p["acs_map"], 0.0), axis=0, keepdims=True)
    e_exp = jnp.exp(p["acs_map"])
    dte = jnp.exp(aq_map - p["acs_map"])
    cd = jnp.exp(aq_map)
    return last, e_exp, dte, cd


def _head_decay(a_cs, a_cs_t, head, tri):
    lane = lax.broadcasted_iota(jnp.int32, (1, DT_PAD), 1)
    sub = lax.broadcasted_iota(jnp.int32, (DT_PAD, 1), 0)
    col = jnp.sum(jnp.where(lane == head, a_cs, 0.0), axis=1, keepdims=True)
    row = jnp.sum(jnp.where(sub == head, a_cs_t, 0.0), axis=0, keepdims=True)
    return jnp.exp(jnp.where(tri, col - row, -1e30))


def _gate_fwd(y, z, nw):
    sz = _sigmoid(z)
    zg = z * sz
    yg = y * zg
    outs, rs = [], []
    for g in range(SSM_GROUPS):
        gs = slice(g * GROUP_W, (g + 1) * GROUP_W)
        o, r = _rms_fwd(yg[:, gs], nw[:, gs])
        outs.append(o)
        rs.append(r)
    return sz, zg, yg, outs, rs


def _ssd_const_specs():
    def whole(shape):
        nd = len(shape)
        return pl.BlockSpec(tuple(shape), lambda b, c: (0,) * nd)
    return [whole((CONV_K, 1, CONV_CH)), whole((1, CONV_CH)), whole((1, DT_PAD)), whole((1, DT_PAD)),
            whole((1, SSM_WIDTH)), whole((1, SSM_WIDTH)), whole((DT_PAD, SSM_WIDTH)), whole((SSM_WIDTH, DT_PAD))]


def _mixer_fwd(p_uv, p_xbc, p_z, p_dt, lnw, lnb, w_cat, bmap, conv_w, conv_b, dt_bias, a_log, dskip_map, norm_w,
               e_bf, et_bf, n_seq, carried=None):
    n_tok = p_xbc.shape[0]
    nc = n_tok // n_seq // CHUNK

    def body(puv_ref, xr_ref, z_ref, pdt_ref, lnw_ref, lnb_ref, wcat_ref, bmap_ref,
             cw_ref, cb_ref, dtb_ref, alog_ref, dsk_ref, nw_ref, e_ref, et_ref,
             ya_ref, yb_ref, yssd_ref, sprev_ref, wm_scr, prev_scr, s_scr):
        @pl.when(jnp.logical_and(pl.program_id(0) == 0, pl.program_id(1) == 0))
        def _():
            wm_scr[...] = _causal_w_cat(wcat_ref[...])

        @pl.when(pl.program_id(1) == 0)
        def _():
            prev_scr[...] = jnp.zeros_like(prev_scr)
            s_scr[...] = jnp.zeros_like(s_scr)

        ya_ref[...] = _gmlp_chunk_fwd(puv_ref[...], lnw_ref[...], lnb_ref[...], e_ref[...], et_ref[...], wm_scr[...],
                                      bmap_ref[...])
        xr = xr_ref[...]
        p = _ssd_pre(xr, prev_scr[...], cw_ref, cb_ref[...], pdt_ref[...], dtb_ref[...], alog_ref[...], e_ref[...])
        _, e_exp, dte, cd = _ssd_maps(p)
        xs = p["xa"][:, :SSM_WIDTH]
        xd = xs * p["dt_map"]
        a_cs_t = p["a_cs"].T
        tri = _tril_mask()
        s_old = s_scr[...]
        sprev_ref[...] = s_old
        for g in range(SSM_GROUPS):
            gs = slice(g * GROUP_W, (g + 1) * GROUP_W)
            bm = p["xa"][:, SSM_WIDTH + g * SSM_STATE: SSM_WIDTH + (g + 1) * SSM_STATE].astype(BF16)
            cm = p["xa"][:, SSM_WIDTH + (SSM_GROUPS + g) * SSM_STATE: SSM_WIDTH + (SSM_GROUPS + g + 1) * SSM_STATE].astype(BF16)
            cb_mat = _dot_nt(cm, bm)
            xdg = xd[:, gs].astype(BF16)
            y_g = _dot(cm, s_old[:, gs]) * e_exp[:, gs] + dsk_ref[:, gs] * xs[:, gs]
            for r in range(SSM_GROUPS * 2):
                dm = _head_decay(p["a_cs"], a_cs_t, g * 4 + r, tri)
                full = jnp.dot((cb_mat * dm).astype(BF16), xdg, preferred_element_type=F32)
                y_g = y_g + jnp.where(_head_lane_mask(GROUP_W, r), full, 0.0)
            yssd_ref[:, gs] = y_g
            s_scr[:, gs] = cd[:, gs] * s_old[:, gs] + _dot_tn(bm, xd[:, gs] * dte[:, gs])
        _, _, _, outs, _ = _gate_fwd(yssd_ref[...], z_ref[...], nw_ref[...])
        for g in range(SSM_GROUPS):
            yb_ref[:, g * GROUP_W:(g + 1) * GROUP_W] = outs[g].astype(BF16)
        prev_scr[...] = xr[CHUNK - SUBLANES:, :]

    def rows(width):
        return pl.BlockSpec((CHUNK, width), lambda b, c: (b * nc + c, 0))

    def whole(shape):
        nd = len(shape)
        return pl.BlockSpec(tuple(shape), lambda b, c: (0,) * nd)

    return _call_carrying(
        body, carried, name="mixer_fwd", grid=(n_seq, nc),
        in_specs=[rows(2 * GM_WIDTH), rows(CONV_CH), rows(SSM_WIDTH), rows(DT_PAD), whole(lnw.shape), whole(lnb.shape),
                  whole(w_cat.shape), whole(bmap.shape)] + _ssd_const_specs(),
        out_specs=[rows(GM_WIDTH), rows(SSM_WIDTH), rows(SSM_WIDTH), rows(SSM_WIDTH)],
        out_shape=(_sds((n_tok, GM_WIDTH), BF16), _sds((n_tok, SSM_WIDTH), BF16), _sds((n_tok, SSM_WIDTH), F32),
                   _sds((n_tok, SSM_WIDTH), F32)),
        scratch_shapes=[pltpu.VMEM((CHUNK, N_HEADS * CHUNK), BF16), pltpu.VMEM((SUBLANES, CONV_CH), F32),
                        pltpu.VMEM((SSM_STATE, SSM_WIDTH), F32)],
        operands=[p_uv, p_xbc, p_z, p_dt, lnw, lnb, w_cat, bmap, conv_w, conv_b, dt_bias, a_log, dskip_map, norm_w, e_bf,
                  et_bf])


def _outproj_fwd(ya, yb, x, w_out, nw_post, nw_pre2, tm=256):
    n_tok = x.shape[0]

    def body(ya_ref, yb_ref, x_ref, wo_ref, nwa_ref, nwb_ref, o_ref, x1_ref, h2_ref):
        o = jnp.dot(ya_ref[...], wo_ref[:GM_WIDTH, :], preferred_element_type=F32)
        o = o + jnp.dot(yb_ref[...], wo_ref[GM_WIDTH:, :], preferred_element_type=F32)
        on, _ = _rms_fwd(o, nwa_ref[...])
        x1 = x_ref[...] + on
        h2, _ = _rms_fwd(x1, nwb_ref[...])
        o_ref[...] = o
        x1_ref[...] = x1
        h2_ref[...] = h2.astype(BF16)

    return _rows_call("outproj_fwd", body, tm, [ya, yb, x], [w_out, nw_post, nw_pre2],
                      [_sds((n_tok, D_MODEL), F32), _sds((n_tok, D_MODEL), F32), _sds((n_tok, D_MODEL), BF16)])


def _up_cols(wup_ref, j):
    per = (D_FF // N_CHIPS) // FF_TILE
    return wup_ref[j // per, :, (j % per) * FF_TILE:(j % per + 1) * FF_TILE]


def _down_rows(wda_ref, wdb_ref, j):
    assert 2 * FF_TILE == D_FF // N_CHIPS
    return (wda_ref if j % 2 == 0 else wdb_ref)[j // 2]


def _skewed_rows_call(name, main, tail, tm, lead_ins, lag_ins, const_ins, lead_outs, lag_outs, acc_outs, carry):
    n_rows = lead_ins[0].shape[0]
    assert n_rows % tm == 0
    n = n_rows // tm
    counts = [len(lead_ins), len(lag_ins), len(const_ins), len(lead_outs), len(lag_outs), len(acc_outs)]

    def kern(*refs):
        groups, pos = [], 0
        for cnt in counts:
            groups.append(refs[pos:pos + cnt])
            pos += cnt
        lead_i, lag_i, consts, lead_o, lag_o, accs = groups
        carry_scr = refs[pos]
        i = pl.program_id(0)

        @pl.when(i == 0)
        def _():
            for a in accs:
                a[...] = jnp.zeros_like(a)
            carry_scr[...] = main(lead_i, consts, lead_o)

        @pl.when(jnp.logical_and(i > 0, i < n))
        def _():
            previous = carry_scr[...]
            carry_scr[...] = main(lead_i, consts, lead_o)
            tail(previous, lag_i, consts, lag_o, accs)

        @pl.when(i == n)
        def _():
            tail(carry_scr[...], lag_i, consts, lag_o, accs)

    def lead(width):
        return pl.BlockSpec((tm, width), lambda i: (jnp.minimum(i, n - 1), 0))

    def lag(width):
        return pl.BlockSpec((tm, width), lambda i: (jnp.maximum(i - 1, 0), 0))

    def whole(shape, **kw):
        nd = len(shape)
        return pl.BlockSpec(tuple(shape), lambda i: (0,) * nd, **kw)

    const_specs = [whole(a.shape, pipeline_mode=pl.Buffered(1)) for a in const_ins]
    return pl.pallas_call(
        kern, name=name, grid=(n + 1,),
        in_specs=[lead(a.shape[1]) for a in lead_ins] + [lag(a.shape[1]) for a in lag_ins] + const_specs,
        out_specs=[lead(s.shape[1]) for s in lead_outs] + [lag(s.shape[1]) for s in lag_outs] + [whole(s.shape) for s in acc_outs],
        out_shape=tuple(lead_outs) + tuple(lag_outs) + tuple(acc_outs),
        scratch_shapes=[pltpu.VMEM(carry, F32)], compiler_params=_cparams(1),
    )(*lead_ins, *lag_ins, *const_ins)


def _mlp_fwd(h2, x1, tgt, w_up, w_down_a, w_down_b, nw, tm=512):
    n_tok = x1.shape[0]

    def main(lead_i, consts, lead_o):
        (h2_ref,), (wup_ref, wda_ref, wdb_ref, _), (f_ref,) = lead_i, consts, lead_o
        h2v = h2_ref[...]
        acc = jnp.zeros((tm, D_MODEL), F32)
        for j in range(D_FF // FF_TILE):
            cs = slice(j * FF_TILE, (j + 1) * FF_TILE)
            u = jnp.dot(h2v, _up_cols(wup_ref, j), preferred_element_type=F32)
            f = jnp.square(jnp.maximum(u, 0.0)).astype(BF16)
            f_ref[:, cs] = f
            acc = acc + jnp.dot(f, _down_rows(wda_ref, wdb_ref, j), preferred_element_type=F32)
        return acc

    def tail(acc, lag_i, consts, lag_o, accs):
        (x1_ref, tgt_ref), nw_ref, (dd_ref, dy_ref), (loss_ref, dnw_ref) = lag_i, consts[3], lag_o, accs
        dn, r = _rms_fwd(acc, nw_ref[...])
        e = x1_ref[...] + dn - tgt_ref[...]
        loss_ref[...] += jnp.full(loss_ref.shape, (0.5 / D_MODEL) * jnp.sum(e * e), F32)
        dy = e * (1.0 / D_MODEL)
        dd, dnw = _rms_bwd(acc, r, nw_ref[...], dy)
        dy_ref[...] = dy
        dd_ref[...] = dd.astype(BF16)
        dnw_ref[...] += dnw

    return _skewed_rows_call(
        "mlp_fwd", main, tail, tm, [h2], [x1, tgt], [w_up, w_down_a, w_down_b, nw],
        [_sds((n_tok, D_FF), BF16)], [_sds((n_tok, D_MODEL), BF16), _sds((n_tok, D_MODEL), F32)],
        [_sds((8, 128), F32), _sds((1, D_MODEL), F32)], carry=(tm, D_MODEL))


def _mlp_bwd(dd, f, x1, dy, w_down_a, w_down_b, w_up, nw, tm=512):
    n_tok = x1.shape[0]

    def main(lead_i, consts, lead_o):
        (dd_ref, f_ref), (wda_ref, wdb_ref, wup_ref, _), (dup_ref,) = lead_i, consts, lead_o
        ddv = dd_ref[...]
        acc = jnp.zeros((tm, D_MODEL), F32)
        for j in range(D_FF // FF_TILE):
            cs = slice(j * FF_TILE, (j + 1) * FF_TILE)
            df = _dot_nt(ddv, _down_rows(wda_ref, wdb_ref, j))
            du = (df * (2.0 * jnp.sqrt(f_ref[:, cs].astype(F32)))).astype(BF16)
            dup_ref[:, cs] = du
            acc = acc + _dot_nt(du, _up_cols(wup_ref, j))
        return acc

    def tail(acc, lag_i, consts, lag_o, accs):
        (x1_ref, dy_ref), nw_ref, (dx1_ref,), (dnw_ref,) = lag_i, consts[3], lag_o, accs
        x1v = x1_ref[...]
        _, r = _rms_fwd(x1v, nw_ref[...])
        dx, dnw = _rms_bwd(x1v, r, nw_ref[...], acc)
        dx1_ref[...] = dy_ref[...] + dx
        dnw_ref[...] += dnw

    return _skewed_rows_call(
        "mlp_bwd", main, tail, tm, [dd, f], [x1, dy], [w_down_a, w_down_b, w_up, nw],
        [_sds((n_tok, D_FF), BF16)], [_sds((n_tok, D_MODEL), F32)], [_sds((1, D_MODEL), F32)], carry=(tm, D_MODEL))


def _outproj_bwd(dx1, o, w_out, nw, tm=256, carried=None):
    n_tok = dx1.shape[0]

    def body(dx1_ref, o_ref, wo_ref, nw_ref, do_ref, dya_ref, dyb_ref, dnw_ref):
        ov = o_ref[...]
        _, r = _rms_fwd(ov, nw_ref[...])
        do, dnw = _rms_bwd(ov, r, nw_ref[...], dx1_ref[...])
        dob = do.astype(BF16)
        do_ref[...] = dob
        dya_ref[...] = _dot_nt(dob, wo_ref[:GM_WIDTH, :])
        dyb_ref[...] = _dot_nt(dob, wo_ref[GM_WIDTH:, :])
        dnw_ref[...] += dnw

    return _rows_call("outproj_bwd", body, tm, [dx1, o], [w_out, nw],
                      [_sds((n_tok, D_MODEL), BF16), _sds((n_tok, GM_WIDTH), F32), _sds((n_tok, SSM_WIDTH), F32)],
                      [_sds((1, D_MODEL), F32)], carried=carried)


def _gmlp_bwd(p_uv, dya, lnw, lnb, e_bf, et_bf, w_cat, w_stack, bmap, carried=None):
    n_tok = p_uv.shape[0]

    def body(puv_ref, dya_ref, lnw_ref, lnb_ref, e_ref, et_ref, wcat_ref, wstack_ref, bmap_ref,
             dpuv_ref, dws_ref, dbs_ref, dlnw_ref, dlnb_ref, wm_scr, wsm_scr):
        t_stk = lax.broadcasted_iota(jnp.int32, (N_HEADS * CHUNK, CHUNK), 0) % CHUNK
        s_stk = lax.broadcasted_iota(jnp.int32, (N_HEADS * CHUNK, CHUNK), 1)

        @pl.when(pl.program_id(0) == 0)
        def _():
            wm_scr[...] = _causal_w_cat(wcat_ref[...])
            wsm_scr[...] = jnp.where(t_stk >= s_stk, wstack_ref[...], 0.0).astype(BF16)

        lnw_v = lnw_ref[...]
        e_v, et_v = e_ref[...], et_ref[...]
        u, v, gu, tu, tv, rstd, xhat, vn = _gmlp_common(puv_ref[...], lnw_v, lnb_ref[...], e_v, et_v)
        vnb = vn.astype(BF16)
        mixed = jnp.dot(wm_scr[...], _head_blocks(vnb), preferred_element_type=F32) + bmap_ref[...]
        dy = dya_ref[...]
        du = dy * mixed * _gelu_grad(u, tu)
        dmixed = dy * gu
        (dbs,) = _seg_dots([dmixed], et_v)
        dbs_ref[...] += dbs
        dblocks = _head_blocks(dmixed.astype(BF16))
        dvn = lax.dot_general(wsm_scr[...], dblocks, (((0,), (0,)), ((), ())), preferred_element_type=F32)
        dws = lax.dot_general(dblocks, vnb, (((1,), (1,)), ((), ())), preferred_element_type=F32)
        dws_ref[...] += jnp.where(t_stk >= s_stk, dws, 0.0)
        dlnw_ref[...] += jnp.sum(dvn * xhat, axis=0, keepdims=True)
        dlnb_ref[...] += jnp.sum(dvn, axis=0, keepdims=True)
        dxh = dvn * lnw_v
        m1, m2 = _seg_dots([dxh, dxh * xhat], et_v)
        m1, m2 = _seg_dots([m1 * (1.0 / HEAD_DIM), m2 * (1.0 / HEAD_DIM)], e_v)
        dgv = rstd * (dxh - m1 - xhat * m2)
        dv = dgv * _gelu_grad(v, tv)
        dpuv_ref[:, :GM_WIDTH] = du.astype(BF16)
        dpuv_ref[:, GM_WIDTH:] = dv.astype(BF16)

    return _rows_call(
        "gmlp_bwd", body, CHUNK, [p_uv, dya], [lnw, lnb, e_bf, et_bf, w_cat, w_stack, bmap],
        [_sds((n_tok, 2 * GM_WIDTH), BF16)],
        [_sds((N_HEADS * CHUNK, CHUNK), F32), _sds((CHUNK, DT_PAD), F32), _sds((1, GM_WIDTH), F32),
         _sds((1, GM_WIDTH), F32)],
        scratch=[pltpu.VMEM((CHUNK, N_HEADS * CHUNK), BF16), pltpu.VMEM((N_HEADS * CHUNK, CHUNK), BF16)],
        carried=carried)


def _ssd_bwd(p_xbc, p_z, p_dt, yssd, sprev, dyb, conv_w, conv_b, dt_bias, a_log, dskip_map, norm_w, e_bf, et_bf, n_seq,
             carried=None):
    n_tok = p_xbc.shape[0]
    nc = n_tok // n_seq // CHUNK

    def body(xr_ref, xprev_ref, z_ref, pdt_ref, yssd_ref, sprev_ref, dyb_ref,
             cw_ref, cb_ref, dtb_ref, alog_ref, dsk_ref, nw_ref, e_ref, et_ref,
             dpxbc_ref, dpz_ref, dpdt_ref, dcw_ref, dcb_ref, ddtb_ref, dalog_ref, ddsk_ref, dnw_ref,
             ds_scr, nxt_scr, dxa_scr):
        step = pl.program_id(1)
        first = jnp.logical_and(pl.program_id(0) == 0, step == 0)

        @pl.when(first)
        def _():
            for a in (dcw_ref, dcb_ref, ddtb_ref, dalog_ref, ddsk_ref, dnw_ref):
                a[...] = jnp.zeros_like(a)

        @pl.when(step == 0)
        def _():
            ds_scr[...] = jnp.zeros_like(ds_scr)
            nxt_scr[...] = jnp.zeros_like(nxt_scr)

        chunk = nc - 1 - step
        xr = xr_ref[...]
        prev = jnp.where(chunk == 0, 0.0, xprev_ref[...])
        et_v = et_ref[...]
        p = _ssd_pre(xr, prev, cw_ref, cb_ref[...], pdt_ref[...], dtb_ref[...], alog_ref[...], e_ref[...])
        last, e_exp, dte, cd = _ssd_maps(p)
        rowi = p["rowi"]
        xs = p["xa"][:, :SSM_WIDTH]
        xd = xs * p["dt_map"]
        a_cs_t = p["a_cs"].T
        tri = _tril_mask()
        dsk = dsk_ref[...]
        nw_v = nw_ref[...]

        yv = yssd_ref[...]
        zv = z_ref[...]
        sz, zg, yg, _, rs = _gate_fwd(yv, zv, nw_v)
        dout = dyb_ref[...]
        for g in range(SSM_GROUPS):
            gs = slice(g * GROUP_W, (g + 1) * GROUP_W)
            dyg_g, dnw_g = _rms_bwd(yg[:, gs], rs[g], nw_v[:, gs], dout[:, gs])
            dnw_ref[:, gs] += dnw_g
            dxa_scr[:, gs] = dyg_g
        dyg = dxa_scr[:, :SSM_WIDTH]
        d_y = dyg * zg
        dpz_ref[...] = (dyg * yv * (sz + zv * sz * (1.0 - sz))).astype(BF16)

        s_prev = sprev_ref[...]
        ds_next = ds_scr[...]
        lane_dt = lax.broadcasted_iota(jnp.int32, (1, DT_PAD), 1)
        da_cols = jnp.zeros((CHUNK, DT_PAD), F32)
        for g in range(SSM_GROUPS):
            gs = slice(g * GROUP_W, (g + 1) * GROUP_W)
            b_off = SSM_WIDTH + g * SSM_STATE
            c_off = SSM_WIDTH + (SSM_GROUPS + g) * SSM_STATE
            bm = p["xa"][:, b_off:b_off + SSM_STATE].astype(BF16)
            cm = p["xa"][:, c_off:c_off + SSM_STATE].astype(BF16)
            cb_mat = _dot_nt(cm, bm)
            d_yg = d_y[:, gs]
            d_ygb = d_yg.astype(BF16)
            xdg = xd[:, gs]
            xdgb = xdg.astype(BF16)
            ds_g = ds_next[:, gs]
            sp_g = s_prev[:, gs]
            bds = _dot(bm, ds_g)
            dcs = d_yg * e_exp[:, gs]
            d_c = _dot_nt(dcs, sp_g)
            ds_scr[:, gs] = cd[:, gs] * ds_g + _dot_tn(cm, dcs)
            d_b = _dot_nt(xdg * dte[:, gs], ds_g)
            dxd_g = bds * dte[:, gs]
            sum_dcb = jnp.zeros((CHUNK, CHUNK), F32)
            for r in range(SSM_GROUPS * 2):
                head = g * 4 + r
                mask = _head_lane_mask(GROUP_W, r)
                dm = _head_decay(p["a_cs"], a_cs_t, head, tri)
                m_mat = cb_mat * dm
                g_mat = _dot_nt(jnp.where(mask, d_yg, 0.0), xdgb)
                w_mat = g_mat * m_mat
                sum_dcb = sum_dcb + g_mat * dm
                dxd_g = dxd_g + jnp.where(mask, _dot_tn(m_mat, d_ygb), 0.0)
                da_h = jnp.sum(w_mat - w_mat.T, axis=1, keepdims=True)
                da_cols = da_cols + jnp.where(lane_dt == head, da_h, 0.0)
            d_c = d_c + _dot(sum_dcb, bm)
            d_b = d_b + _dot_tn(sum_dcb, cm)
            dxa_scr[:, b_off:b_off + SSM_STATE] = d_b
            dxa_scr[:, c_off:c_off + SSM_STATE] = d_c
            y_off_g = _dot(cm, sp_g) * e_exp[:, gs]
            t3 = bds * xdg * dte[:, gs]
            tail = jnp.sum(t3, axis=0, keepdims=True) + jnp.sum(ds_g * sp_g, axis=0, keepdims=True) * cd[:, gs]
            pre_g = d_yg * y_off_g - t3 + jnp.where(last, tail, 0.0)
            s_pre, ddt_g, s_dsk = _seg_dots([pre_g, dxd_g * xs[:, gs], d_yg * xs[:, gs]], et_v[gs, :])
            da_cols = da_cols + s_pre
            ddsk_ref[...] += jnp.sum(s_dsk, axis=0, keepdims=True)
            dxa_scr[:, gs] = dxd_g * p["dt_map"][:, gs] + dsk[:, gs] * d_yg
            if g == 0:
                ddt = ddt_g
            else:
                ddt = ddt + ddt_g
        r_i = lax.broadcasted_iota(jnp.int32, (CHUNK, CHUNK), 0)
        c_i = lax.broadcasted_iota(jnp.int32, (CHUNK, CHUNK), 1)
        ddta = _tri_dot(r_i <= c_i, da_cols, terms=2)
        ddt = ddt + ddta * p["a_neg"]
        dalog_ref[...] += jnp.sum(ddta * p["dt"], axis=0, keepdims=True) * p["a_neg"]
        draw = ddt * _sigmoid(p["pre"])
        ddtb_ref[...] += jnp.sum(draw, axis=0, keepdims=True)
        dpdt_ref[...] = draw.astype(BF16)

        xc = p["xc"]
        sg = p["sg"]
        dxc = dxa_scr[...] * (sg + xc * sg * (1.0 - sg))
        dcb_ref[...] += jnp.sum(dxc, axis=0, keepdims=True)
        for k in range(CONV_K):
            dcw_ref[k] += jnp.sum(dxc * p["shifted"][k], axis=0, keepdims=True)
        nxt = nxt_scr[...]
        dxr = cw_ref[3] * dxc
        for s in range(1, CONV_K):
            dxr = dxr + cw_ref[CONV_K - 1 - s] * _shift_up(dxc, nxt, s)
        dpxbc_ref[...] = dxr.astype(BF16)
        nxt_scr[...] = dxc[:SUBLANES, :]

    def rows(width):
        return pl.BlockSpec((CHUNK, width), lambda b, s: (b * nc + nc - 1 - s, 0))

    tiles = CHUNK // SUBLANES
    prev_rows = pl.BlockSpec((SUBLANES, CONV_CH), lambda b, s: (jnp.maximum((b * nc + nc - 1 - s) * tiles - 1, 0), 0))

    def whole(shape):
        nd = len(shape)
        return pl.BlockSpec(tuple(shape), lambda b, s: (0,) * nd)

    acc_shapes = [(CONV_K, 1, CONV_CH), (1, CONV_CH), (1, DT_PAD), (1, DT_PAD), (1, DT_PAD), (1, SSM_WIDTH)]
    return _call_carrying(
        body, carried, name="ssd_bwd", grid=(n_seq, nc),
        in_specs=[rows(CONV_CH), prev_rows, rows(SSM_WIDTH), rows(DT_PAD), rows(SSM_WIDTH), rows(SSM_WIDTH),
                  rows(SSM_WIDTH)] + _ssd_const_specs(),
        out_specs=[rows(CONV_CH), rows(SSM_WIDTH), rows(DT_PAD)] + [whole(s) for s in acc_shapes],
        out_shape=tuple([_sds((n_tok, CONV_CH), BF16), _sds((n_tok, SSM_WIDTH), BF16), _sds((n_tok, DT_PAD), BF16)]
                        + [_sds(s, F32) for s in acc_shapes]),
        scratch_shapes=[pltpu.VMEM((SSM_STATE, SSM_WIDTH), F32), pltpu.VMEM((SUBLANES, CONV_CH), F32),
                        pltpu.VMEM((CHUNK, CONV_CH), F32)],
        operands=[p_xbc, p_xbc, p_z, p_dt, yssd, sprev, dyb, conv_w, conv_b, dt_bias, a_log, dskip_map, norm_w, e_bf,
                  et_bf])


def _inproj_bwd(dp_uv, dp_xbc, dp_z, dp_dt, x, dx1, w_uv, w_xbc, w_z, w_dt, nw, tm=256, carried=None):
    n_tok = x.shape[0]

    def body(duv_ref, dxbc_ref, dz_ref, ddt_ref, x_ref, dx1_ref, wuv_ref, wxbc_ref, wz_ref, wdt_ref, nw_ref,
             gx_ref, h_ref, dnw_ref):
        dh = _dot_nt(duv_ref[...], wuv_ref[...]) + _dot_nt(dxbc_ref[...], wxbc_ref[...])
        dh = dh + _dot_nt(dz_ref[...], wz_ref[...]) + _dot_nt(ddt_ref[...], wdt_ref[...])
        xv = x_ref[...]
        h, r = _rms_fwd(xv, nw_ref[...])
        dx, dnw = _rms_bwd(xv, r, nw_ref[...], dh)
        gx_ref[...] = dx1_ref[...] + dx
        h_ref[...] = h.astype(BF16)
        dnw_ref[...] += dnw

    return _rows_call("inproj_bwd", body, tm, [dp_uv, dp_xbc, dp_z, dp_dt, x, dx1], [w_uv, w_xbc, w_z, w_dt, nw],
                      [_sds((n_tok, D_MODEL), F32), _sds((n_tok, D_MODEL), BF16)], [_sds((1, D_MODEL), F32)],
                      carried=carried)


def _const_maps():
    lane = jnp.arange(SSM_WIDTH) // HEAD_DIM
    e_bf = (jnp.arange(DT_PAD)[:, None] == lane[None, :]).astype(BF16)
    return e_bf, e_bf.T


def _pad_lanes(v, width):
    return jnp.pad(v, ((0, 0), (0, width - v.shape[1])))


SHARD_COLS = IN_COLS // N_CHIPS
_UV_END = 2 * GM_WIDTH
_Z_END = _UV_END + SSM_WIDTH
_XBC_END = _Z_END + CONV_CH


def _cols_from_shards(w4, lo, hi):
    pieces = []
    for j in range(N_CHIPS):
        a, b = max(lo, j * SHARD_COLS), min(hi, (j + 1) * SHARD_COLS)
        if a < b:
            pieces.append(w4[j][:, a - j * SHARD_COLS:b - j * SHARD_COLS])
    return pieces[0] if len(pieces) == 1 else jnp.concatenate(pieces, axis=1)


def _shards_from_cols(blocks):
    shards = []
    for j in range(N_CHIPS):
        pieces = []
        for arr, lo, hi in blocks:
            a, b = max(lo, j * SHARD_COLS), min(hi, (j + 1) * SHARD_COLS)
            if a < b:
                pieces.append(arr[:, a - lo:b - lo])
        shards.append(pieces[0] if len(pieces) == 1 else jnp.concatenate(pieces, axis=1))
    return jnp.stack(shards)


def _forward_backward(x, tgt, w_in4, conv_w, small, out_shard, up_shard, down_shard, core):
    n_seq, seq_len, _ = x.shape
    n_tok = n_seq * seq_len
    x2 = x.reshape(n_tok, D_MODEL)
    tgt2 = tgt.reshape(n_tok, D_MODEL)
    e_bf, et_bf = _const_maps()

    w_uv = _cols_from_shards(w_in4, 0, _UV_END)
    w_z = _cols_from_shards(w_in4, _UV_END, _Z_END)
    w_xbc = _cols_from_shards(w_in4, _Z_END, _XBC_END)
    w_dt = _pad_lanes(_cols_from_shards(w_in4, _XBC_END, IN_COLS), DT_PAD)

    nw_pre = small["norm_mix_pre"]
    lnw = small["gm_ln_w"].reshape(1, GM_WIDTH)
    lnb = small["gm_ln_b"].reshape(1, GM_WIDTH)
    w_stack = small["gm_w_s"].reshape(N_HEADS * CHUNK, CHUNK)
    w_cat = jnp.transpose(small["gm_w_s"], (1, 0, 2)).reshape(CHUNK, N_HEADS * CHUNK)
    bmap = jnp.repeat(small["gm_b_s"].T, HEAD_DIM, axis=1)
    cw3 = conv_w.reshape(CONV_K, 1, CONV_CH)
    conv_b = small["conv_b"]
    dt_bias = _pad_lanes(small["dt_bias"], DT_PAD)
    a_log = _pad_lanes(small["a_log"], DT_PAD)
    dskip_map = jnp.repeat(small["d_skip"], HEAD_DIM, axis=1)
    ssm_nw = small["ssm_norm_w"]

    half = down_shard.shape[0] // 2
    p_uv, p_xbc, p_z, p_dt, w_out4, w_down_a = _inproj_fwd(
        x2, nw_pre, w_uv, w_xbc, w_z, w_dt, carried=_allgather_exchange([out_shard, down_shard[:half]]))
    ssd_consts = (cw3, conv_b, dt_bias, a_log, dskip_map, ssm_nw, e_bf, et_bf)
    ya, yb, yssd, sprev, w_up4, w_down_b = _mixer_fwd(
        p_uv, p_xbc, p_z, p_dt, lnw, lnb, w_cat, bmap, *ssd_consts, n_seq,
        carried=_allgather_exchange([up_shard, down_shard[half:]]))
    w_out_b = w_out4.reshape(D_MODEL, D_MODEL)
    o, x1, h2 = _outproj_fwd(ya, yb, x2, w_out_b, small["norm_mix_post"], small["norm_ffn_pre"])
    f, dd, dy, loss_acc, d_nffn_post = _mlp_fwd(h2, x1, tgt2, w_up4, w_down_a, w_down_b, small["norm_ffn_post"])

    dup, dx1, d_nffn_pre = _mlp_bwd(dd, f, x1, dy, w_down_a, w_down_b, w_up4, small["norm_ffn_pre"])
    tk = min(DW_TOKENS_PER_STEP, n_tok)
    g_up = _matmul_tn("dw_up", h2, dup, D_MODEL, D_MODEL, tk, stacked=True)
    g_down = _matmul_tn("dw_down", f, dd, 1024, D_MODEL, tk).reshape(N_CHIPS, D_FF // N_CHIPS, D_MODEL)
    do, dya, dyb, d_nmix_post, got_up, got_down = _outproj_bwd(
        dx1, o, w_out_b, small["norm_mix_post"], carried=_pair_exchange([g_up, g_down]))
    h_up = _pair_sum(core, g_up, got_up, 256)
    h_down = _pair_sum(core, g_down, got_down, 256)
    g_out_a = _matmul_tn("dw_out_a", ya, do, GM_WIDTH, D_MODEL, tk)
    g_out_b = _matmul_tn("dw_out_b", yb, do, SSM_WIDTH, D_MODEL, tk)
    g_out = jnp.concatenate([g_out_a, g_out_b], axis=0).reshape(N_CHIPS, D_MODEL // N_CHIPS, D_MODEL)
    dp_uv, d_ws, d_bs_t, d_lnw, d_lnb, slab_up, got_out = _gmlp_bwd(
        p_uv, dya, lnw, lnb, e_bf, et_bf, w_cat, w_stack, bmap,
        carried=_both(_chip_exchange([h_up]), _pair_exchange([g_out])))
    h_out = _pair_sum(core, g_out, got_out, 128)
    early = {
        "gm_ln_w": d_lnw.reshape(N_HEADS, HEAD_DIM), "gm_ln_b": d_lnb.reshape(N_HEADS, HEAD_DIM),
        "gm_w_s": d_ws.reshape(N_HEADS, CHUNK, CHUNK), "gm_b_s": d_bs_t[:, :N_HEADS].T,
        "norm_mix_post": d_nmix_post, "norm_ffn_pre": d_nffn_pre, "norm_ffn_post": d_nffn_post,
    }
    packed_early = _pack(early, tuple(early), tail=loss_acc[0, 0].reshape(1))
    (dp_xbc, dp_z, dp_dt, d_cw, d_cb, d_dtb, d_alog, d_dsk, d_ssm_nw, slab_down, slab_out, all_early) = _ssd_bwd(
        p_xbc, p_z, p_dt, yssd, sprev, dyb, *ssd_consts, n_seq,
        carried=_both(_chip_exchange([h_down, h_out]), _device_gather_exchange(packed_early)))
    gx, h, d_nmix_pre = _inproj_bwd(dp_uv, dp_xbc, dp_z, dp_dt, x2, dx1, w_uv, w_xbc, w_z, w_dt, nw_pre)
    late = {
        "norm_mix_pre": d_nmix_pre, "conv_w": d_cw.reshape(CONV_K, CONV_CH), "conv_b": d_cb,
        "dt_bias": d_dtb[:, :N_HEADS], "a_log": d_alog[:, :N_HEADS], "d_skip": d_dsk[:, :N_HEADS],
        "ssm_norm_w": d_ssm_nw,
    }
    g_uv, all_late = _matmul_tn("dw_in_uv", h, dp_uv, D_MODEL, 2 * GM_WIDTH, tk,
                                carried=_device_gather_exchange(_pack(late, tuple(late))))
    sum_early = _ordered_sum("small_sum_early", all_early)
    small_sum = _unpack(sum_early, {n: v.shape for n, v in early.items()}, tuple(early))
    small_sum.update(_unpack(_ordered_sum("small_sum_late", all_late), {n: v.shape for n, v in late.items()}, tuple(late)))
    loss = sum_early.reshape(-1)[sum(v.size for v in early.values())]
    g_xbc = _matmul_tn("dw_in_xbc", h, dp_xbc, D_MODEL, CONV_CH, tk)
    g_z = _matmul_tn("dw_in_z", h, dp_z, D_MODEL, SSM_WIDTH, tk)
    g_dt = _matmul_tn("dw_in_dt", h, dp_dt, D_MODEL, DT_PAD, tk)

    g_in = _shards_from_cols([(g_uv, 0, _UV_END), (g_z, _UV_END, _Z_END), (g_xbc, _Z_END, _XBC_END),
                              (g_dt, _XBC_END, IN_COLS)])

    (got_in,) = _run_exchange("grad_pair_exchange_in", _pair_exchange([g_in]))
    h_in = _pair_sum(core, g_in, got_in, 256)
    (slab_in,) = _run_exchange("grad_chip_exchange", _chip_exchange([h_in]))
    reds = [_chip_sum(core, s, tm) for s, tm in ((slab_in, 256), (slab_out, 128), (slab_up, 256), (slab_down, 256))]
    big_grads = dict(zip(("w_in", "w_out", "w_up", "w_down"), _pair_gather(reds)))

    return loss, gx.reshape(x.shape), big_grads, small_sum


_HBM = pl.BlockSpec(memory_space=pltpu.HBM)


D2D_CHUNKS = 8
ROW_ALIGN = 16


def _row_chunks(rows, n_chunks):
    size = min(max(rows // n_chunks, ROW_ALIGN), rows)
    assert rows % size == 0
    return [(start, size) for start in range(0, rows, size)]


def _position():
    x, y, c = lax.axis_index("x"), lax.axis_index("y"), lax.axis_index("c")
    chips = [(1 - x, y), (x, 1 - y), (1 - x, 1 - y)]
    return x, y, c, chips


def _allgather_exchange(arrs):
    n = len(arrs)

    def copies(ins, outs, send_sems, recv_sems, local_sems):
        x, y, c, chips = _position()
        me = 2 * x + y
        sibling = (x, y, 1 - c)

        def copy(a, k, src, dst, to):
            return pltpu.make_async_remote_copy(src_ref=src, dst_ref=dst, send_sem=send_sems.at[a, k],
                                                recv_sem=recv_sems.at[a, k], device_id=to, device_id_type=MESH)

        def half_rows(a, pc):
            half = ins[a].shape[0] // 2
            return pl.ds(pc * half, half)

        local = [pltpu.make_async_copy(ins[a], outs[a].at[me], local_sems.at[a]) for a in range(n)]
        ici_out = [[copy(a, k, ins[a].at[half_rows(a, c)], outs[a].at[me, half_rows(a, c)], (px, py, c))
                    for k, (px, py) in enumerate(chips)] for a in range(n)]
        return c, chips, sibling, copy, half_rows, local, ici_out

    def start(ins, outs, send_sems, recv_sems, local_sems):
        _, _, _, _, _, local, ici_out = copies(ins, outs, send_sems, recv_sems, local_sems)
        for cp in local:
            cp.start()
        for a in range(n):
            for cp in ici_out[a]:
                cp.start()

    def finish(ins, outs, send_sems, recv_sems, local_sems):
        c, chips, sibling, copy, half_rows, local, ici_out = copies(ins, outs, send_sems, recv_sems, local_sems)
        passed = []
        for a in range(n):
            half = ins[a].shape[0] // 2
            for k, (px, py) in enumerate(chips):
                blk = outs[a].at[2 * px + py, half_rows(a, c)]
                copy(a, k, blk, blk, (px, py, c)).wait_recv()
                for first, size in _row_chunks(half, D2D_CHUNKS):
                    piece = outs[a].at[2 * px + py, pl.ds(c * half + first, size)]
                    copy(a, 3 + k, piece, piece, sibling).start()
                passed.append(copy(a, 3 + k, blk, blk, sibling))
        for a in range(n):
            for k, (px, py) in enumerate(chips):
                blk = outs[a].at[2 * px + py, half_rows(a, 1 - c)]
                copy(a, 3 + k, blk, blk, sibling).wait_recv()
        for a in range(n):
            for cp in ici_out[a]:
                cp.wait_send()
        for cp in passed:
            cp.wait_send()
        for cp in local:
            cp.wait()

    return _Carried(arrs, [_sds((N_CHIPS,) + a.shape, a.dtype) for a in arrs],
                    [pltpu.SemaphoreType.DMA((n, 6)), pltpu.SemaphoreType.DMA((n, 6)), pltpu.SemaphoreType.DMA((n,))],
                    start, finish)


def _run_exchange(name, exchange):
    n_in, n_out = len(exchange.ins), len(exchange.out_shapes)

    def body(*refs):
        ins, outs, sems = refs[:n_in], refs[n_in:n_in + n_out], refs[n_in + n_out:]
        exchange.start(ins, outs, *sems)
        exchange.finish(ins, outs, *sems)

    return pl.pallas_call(
        body, name=name, out_shape=tuple(exchange.out_shapes), in_specs=[_HBM] * n_in,
        out_specs=tuple([_HBM] * n_out), scratch_shapes=exchange.sems,
    )(*exchange.ins)


def _pair_exchange(grads):
    n = len(grads)

    def copier(send_sems, recv_sems):
        x, y, c, _ = _position()

        def copy(a, src, dst):
            return pltpu.make_async_remote_copy(src_ref=src, dst_ref=dst, send_sem=send_sems.at[a],
                                                recv_sem=recv_sems.at[a], device_id=(x, y, 1 - c), device_id_type=MESH)
        return c, copy

    def start(ins, got, send_sems, recv_sems):
        c, copy = copier(send_sems, recv_sems)
        for a in range(n):
            half = ins[a].shape[1] // 2
            for slab in range(N_CHIPS):
                for first, size in _row_chunks(half, D2D_CHUNKS):
                    copy(a, ins[a].at[slab, pl.ds((1 - c) * half + first, size), :],
                         got[a].at[slab, pl.ds(first, size), :]).start()

    def finish(ins, got, send_sems, recv_sems):
        c, copy = copier(send_sems, recv_sems)
        for a in range(n):
            half = ins[a].shape[1] // 2
            copy(a, ins[a].at[:, pl.ds((1 - c) * half, half), :], got[a]).wait()

    return _Carried(grads, [_sds((N_CHIPS, g.shape[1] // 2, g.shape[2]), g.dtype) for g in grads],
                    [pltpu.SemaphoreType.DMA((n,)), pltpu.SemaphoreType.DMA((n,))], start, finish)


def _chip_exchange(hsums):
    n = len(hsums)

    def copies(ins, outs, send_sems, recv_sems, local_sems):
        x, y, c, chips = _position()
        me = 2 * x + y
        cps = []
        for a in range(n):
            cps.append(pltpu.make_async_copy(ins[a].at[me], outs[a].at[me], local_sems.at[a]))
            for k, (px, py) in enumerate(chips):
                cps.append(pltpu.make_async_remote_copy(
                    src_ref=ins[a].at[2 * px + py], dst_ref=outs[a].at[me], send_sem=send_sems.at[a, k],
                    recv_sem=recv_sems.at[a, k], device_id=(px, py, c), device_id_type=MESH))
        return cps

    def start(*refs):
        for cp in copies(*refs):
            cp.start()

    def finish(*refs):
        for cp in copies(*refs):
            cp.wait()

    return _Carried(hsums, [_sds(h.shape, h.dtype) for h in hsums],
                    [pltpu.SemaphoreType.DMA((n, 3)), pltpu.SemaphoreType.DMA((n, 3)), pltpu.SemaphoreType.DMA((n,))],
                    start, finish)


def _pair_gather(bufs):
    n = len(bufs)

    def body(*refs):
        outs = refs[n:2 * n]
        send_sems, recv_sems = refs[2 * n:]
        x, y, c, _ = _position()
        sibling = (x, y, 1 - c)

        def copy(a, rows):
            return pltpu.make_async_remote_copy(src_ref=rows, dst_ref=rows, send_sem=send_sems.at[a],
                                                recv_sem=recv_sems.at[a], device_id=sibling, device_id_type=MESH)

        for a in range(n):
            half = outs[a].shape[0] // 2
            for start, size in _row_chunks(half, 2 * D2D_CHUNKS):
                copy(a, outs[a].at[pl.ds(c * half + start, size), :]).start()
        for a in range(n):
            half = outs[a].shape[0] // 2
            copy(a, outs[a].at[pl.ds(c * half, half), :]).wait_send()
            copy(a, outs[a].at[pl.ds((1 - c) * half, half), :]).wait_recv()

    return pl.pallas_call(
        body, name="grad_pair_gather", out_shape=tuple(_sds(b.shape, b.dtype) for b in bufs),
        in_specs=[_HBM] * n, out_specs=tuple([_HBM] * n), input_output_aliases={a: a for a in range(n)},
        scratch_shapes=[pltpu.SemaphoreType.DMA((n,)), pltpu.SemaphoreType.DMA((n,))],
    )(*bufs)


def _device_gather_exchange(packed):
    def copies(ins, outs, send_sems, recv_sems, local_sem):
        (x_ref,), (all_ref,) = ins, outs
        x, y, c, chips = _position()
        me, sibling = (x, y, c), (x, y, 1 - c)

        def slab(px, py, pc):
            return all_ref.at[4 * px + 2 * py + pc]

        def copy(k, block, to, src=None):
            return pltpu.make_async_remote_copy(
                src_ref=slab(*block) if src is None else src, dst_ref=slab(*block), send_sem=send_sems.at[k],
                recv_sem=recv_sems.at[k], device_id=to, device_id_type=MESH)

        mine = pltpu.make_async_copy(x_ref, slab(*me), local_sem)
        first = [copy(0, me, sibling, src=x_ref)]
        first += [copy(1 + j, me, (*chip, c), src=x_ref) for j, chip in enumerate(chips)]
        passed = [copy(4 + j, (*chip, c), sibling) for j, chip in enumerate(chips)]
        return c, chips, me, sibling, copy, mine, first, passed

    def start(ins, outs, send_sems, recv_sems, local_sem):
        _, _, _, _, _, mine, first, _ = copies(ins, outs, send_sems, recv_sems, local_sem)
        mine.start()
        for cp in first:
            cp.start()

    def finish(ins, outs, send_sems, recv_sems, local_sem):
        c, chips, me, sibling, copy, mine, first, passed = copies(ins, outs, send_sems, recv_sems, local_sem)
        for j, chip in enumerate(chips):
            copy(1 + j, (*chip, c), me).wait_recv()
            passed[j].start()
        copy(0, sibling, me).wait_recv()
        for j, chip in enumerate(chips):
            copy(4 + j, (*chip, 1 - c), me).wait_recv()
        for cp in first + passed:
            cp.wait_send()
        mine.wait()

    return _Carried([packed], [_sds((N_DEV,) + packed.shape, F32)],
                    [pltpu.SemaphoreType.DMA((7,)), pltpu.SemaphoreType.DMA((7,)), pltpu.SemaphoreType.DMA],
                    start, finish)


def _ordered_sum(name, slabs):
    _, m_per, n_cols = slabs.shape

    def body(s_ref, o_ref):
        acc = s_ref[0]
        for d in range(1, N_DEV):
            acc = acc + s_ref[d]
        o_ref[...] = acc

    vmem = pl.BlockSpec(memory_space=pltpu.VMEM)
    return pl.pallas_call(body, name=name, out_shape=_sds((m_per, n_cols), F32), in_specs=[vmem], out_specs=vmem)(slabs)


def _pair_sum(core, own, got, tm):
    _, half, cols = got.shape
    nb = half // tm

    def body(c_ref, a_ref, b_ref, o_ref):
        o_ref[...] = (a_ref[...] + b_ref[...]).astype(BF16)

    return pl.pallas_call(
        body, name="grad_pair_sum", out_shape=_sds(got.shape, BF16),
        grid_spec=pltpu.PrefetchScalarGridSpec(
            num_scalar_prefetch=1, grid=(N_CHIPS, nb),
            in_specs=[pl.BlockSpec((None, tm, cols), lambda s, i, c_ref: (s, c_ref[0] * nb + i, 0)),
                      pl.BlockSpec((None, tm, cols), lambda s, i, c_ref: (s, i, 0))],
            out_specs=pl.BlockSpec((None, tm, cols), lambda s, i, c_ref: (s, i, 0))),
        compiler_params=_cparams(2),
    )(core, own, got)


def _chip_sum(core, slabs, tm):
    _, half, cols = slabs.shape
    nb = half // tm

    def body(c_ref, s_ref, o_ref):
        acc = s_ref[0].astype(F32)
        for k in range(1, N_CHIPS):
            acc = acc + s_ref[k].astype(F32)
        o_ref[...] = acc

    return pl.pallas_call(
        body, name="grad_chip_sum", out_shape=_sds((2 * half, cols), F32),
        grid_spec=pltpu.PrefetchScalarGridSpec(
            num_scalar_prefetch=1, grid=(nb,),
            in_specs=[pl.BlockSpec((N_CHIPS, tm, cols), lambda i, c_ref: (0, i, 0))],
            out_specs=pl.BlockSpec((tm, cols), lambda i, c_ref: (c_ref[0] * nb + i, 0))),
        compiler_params=_cparams(1),
    )(core, slabs)


def _adam_math(w, g, m, v):
    m2 = ADAM_B1 * m + (1.0 - ADAM_B1) * g
    v2 = ADAM_B2 * v + (1.0 - ADAM_B2) * (g * g)
    m_hat = m2 / (1.0 - ADAM_B1 ** ADAM_STEP)
    v_hat = v2 / (1.0 - ADAM_B2 ** ADAM_STEP)
    delta = -ADAM_LR * (m_hat / (jnp.sqrt(v_hat) + ADAM_EPS) + ADAM_WD * w)
    return delta, m2, v2


def _adamw(name, w, g, m, v, tm):
    def body(w_ref, g_ref, m_ref, v_ref, gout_ref, d_ref, m2_ref, v2_ref):
        gv = g_ref[...]
        d, m2, v2 = _adam_math(w_ref[...], gv, m_ref[...], v_ref[...])
        gout_ref[...] = gv
        d_ref[...] = d
        m2_ref[...] = m2
        v2_ref[...] = v2

    return _rows_call(name, body, tm, [w, g, m, v], [], [_sds(w.shape, F32)] * 4)


_SMALL_NAMES = ("norm_mix_pre", "gm_ln_w", "gm_ln_b", "gm_w_s", "gm_b_s", "conv_w", "conv_b", "dt_bias", "a_log",
                "d_skip", "ssm_norm_w", "norm_mix_post", "norm_ffn_pre", "norm_ffn_post")
_PACK_COLS = 1024


def _pack(parts, names=_SMALL_NAMES, tail=None):
    pieces = [parts[n].reshape(-1) for n in names]
    flat = jnp.concatenate(pieces if tail is None else pieces + [tail])
    rows = -(-flat.shape[0] // (8 * _PACK_COLS)) * 8
    flat = jnp.pad(flat, (0, rows * _PACK_COLS - flat.shape[0]))
    return flat.reshape(rows, _PACK_COLS)


def _unpack(packed, shapes, names=_SMALL_NAMES):
    flat = packed.reshape(-1)
    out, off = {}, 0
    for n in names:
        size = 1
        for s in shapes[n]:
            size *= s
        out[n] = flat[off:off + size].reshape(shapes[n])
        off += size
    return out


def kernel(x, norm_mix_pre, w_in, gm_ln_w, gm_ln_b, gm_w_s, gm_b_s, conv_w, conv_b, dt_bias, a_log, d_skip, ssm_norm_w, w_out, norm_mix_post, norm_ffn_pre, w_up, w_down, norm_ffn_post, loss_target, m_norm_mix_pre, m_w_in, m_gm_ln_w, m_gm_ln_b, m_gm_w_s, m_gm_b_s, m_conv_w, m_conv_b, m_dt_bias, m_a_log, m_d_skip, m_ssm_norm_w, m_w_out, m_norm_mix_post, m_norm_ffn_pre, m_w_up, m_w_down, m_norm_ffn_post, v_norm_mix_pre, v_w_in, v_gm_ln_w, v_gm_ln_b, v_gm_w_s, v_gm_b_s, v_conv_w, v_conv_b, v_dt_bias, v_a_log, v_d_skip, v_ssm_norm_w, v_w_out, v_norm_mix_post, v_norm_ffn_pre, v_w_up, v_w_down, v_norm_ffn_post):
    params = dict(norm_mix_pre=norm_mix_pre, w_in=w_in, gm_ln_w=gm_ln_w, gm_ln_b=gm_ln_b, gm_w_s=gm_w_s, gm_b_s=gm_b_s,
                  conv_w=conv_w, conv_b=conv_b, dt_bias=dt_bias, a_log=a_log, d_skip=d_skip, ssm_norm_w=ssm_norm_w,
                  w_out=w_out, norm_mix_post=norm_mix_post, norm_ffn_pre=norm_ffn_pre, w_up=w_up, w_down=w_down,
                  norm_ffn_post=norm_ffn_post)
    mom1 = dict(norm_mix_pre=m_norm_mix_pre, w_in=m_w_in, gm_ln_w=m_gm_ln_w, gm_ln_b=m_gm_ln_b, gm_w_s=m_gm_w_s,
                gm_b_s=m_gm_b_s, conv_w=m_conv_w, conv_b=m_conv_b, dt_bias=m_dt_bias, a_log=m_a_log, d_skip=m_d_skip,
                ssm_norm_w=m_ssm_norm_w, w_out=m_w_out, norm_mix_post=m_norm_mix_post, norm_ffn_pre=m_norm_ffn_pre,
                w_up=m_w_up, w_down=m_w_down, norm_ffn_post=m_norm_ffn_post)
    mom2 = dict(norm_mix_pre=v_norm_mix_pre, w_in=v_w_in, gm_ln_w=v_gm_ln_w, gm_ln_b=v_gm_ln_b, gm_w_s=v_gm_w_s,
                gm_b_s=v_gm_b_s, conv_w=v_conv_w, conv_b=v_conv_b, dt_bias=v_dt_bias, a_log=v_a_log, d_skip=v_d_skip,
                ssm_norm_w=v_ssm_norm_w, w_out=v_w_out, norm_mix_post=v_norm_mix_post, norm_ffn_pre=v_norm_ffn_pre,
                w_up=v_w_up, w_down=v_w_down, norm_ffn_post=v_norm_ffn_post)
    names = list(params)
    big = ("w_in", "w_out", "w_up", "w_down")
    chip = 2 * lax.axis_index("x") + lax.axis_index("y")

    shards = {n: params[n][0].astype(BF16) for n in big}
    conv_shard = jnp.pad(conv_w[0], ((0, 16 - CONV_K), (0, 0)))
    g_in4, g_conv4 = _run_exchange("allgather_w_in", _allgather_exchange([shards["w_in"], conv_shard]))
    conv_full = jnp.transpose(g_conv4[:, :CONV_K, :], (1, 0, 2)).reshape(CONV_K, CONV_CH)

    small = {n: params[n][0] if params[n].ndim >= 3 else params[n] for n in _SMALL_NAMES if n != "conv_w"}
    core = lax.axis_index("c").astype(jnp.int32).reshape(1)
    loss, grad_x, big_grads, small_sum = _forward_backward(
        x, loss_target, g_in4, conv_full, small, shards["w_out"], shards["w_up"], shards["w_down"], core)

    small_sum["conv_w"] = lax.dynamic_slice_in_dim(small_sum["conv_w"], chip * (CONV_CH // N_CHIPS), CONV_CH // N_CHIPS, axis=1)

    grads, delta, new_m, new_v = {}, {}, {}, {}
    for n, tm in zip(big, (256, 128, 256, 256)):
        g, d, m2, v2 = _adamw("adamw_" + n, params[n][0], big_grads[n], mom1[n][0], mom2[n][0], tm)
        grads[n], delta[n], new_m[n], new_v[n] = g[None], d[None], m2[None], v2[None]
    local_shapes = {n: params[n].shape[1:] if params[n].ndim >= 3 else params[n].shape for n in _SMALL_NAMES}
    flat = lambda tree: {n: tree[n].reshape(local_shapes[n]) for n in _SMALL_NAMES}
    packed = [_pack(flat(t)) for t in (params, small_sum, mom1, mom2)]
    _, d_p, m_p, v_p = _adamw("adamw_small", *packed, packed[0].shape[0])
    for src, dst in ((d_p, delta), (m_p, new_m), (v_p, new_v)):
        for n, val in _unpack(src, local_shapes).items():
            dst[n] = val.reshape(params[n].shape)
    for n in _SMALL_NAMES:
        grads[n] = small_sum[n].reshape(params[n].shape)

    out = [loss, grad_x]
    for tree in (grads, delta, new_m, new_v):
        out += [tree[n] for n in names]
    return tuple(out)
```

```python
import functools

import jax
import jax.numpy as jnp
from jax import lax
from jax.experimental import pallas as pl
from jax.experimental.pallas import tpu as pltpu

F32 = jnp.float32
BF16 = jnp.bfloat16
HI = lax.Precision.HIGHEST
MESH = pl.DeviceIdType.MESH

EPS = 1e-6
D_MODEL = 1024
GM_WIDTH = 512
SSM_WIDTH = 512
N_HEADS = 8
HEAD_DIM = 64
CHUNK = 128
SSM_GROUPS = 2
GROUP_W = SSM_WIDTH // SSM_GROUPS
SSM_STATE = 128
CONV_K = 4
CONV_CH = 1024
D_FF = 4096
IN_COLS = 2568
DT_PAD = 128
N_CHIPS = 4
N_DEV = 8

ADAM_LR = 0.001
ADAM_B1 = 0.9
ADAM_B2 = 0.999
ADAM_EPS = 1e-08
ADAM_WD = 0.01
ADAM_STEP = 10

VMEM_LIMIT_BYTES = 56 * 1024 * 1024
FF_TILE = 512
DW_TOKENS_PER_STEP = 2048


def _cparams(n_axes):
    return pltpu.CompilerParams(dimension_semantics=("arbitrary",) * n_axes, vmem_limit_bytes=VMEM_LIMIT_BYTES)


def _dot(a, b):
    return jnp.dot(a.astype(BF16), b.astype(BF16), preferred_element_type=F32)


def _dot_nt(a, b):
    return lax.dot_general(a.astype(BF16), b.astype(BF16), (((1,), (1,)), ((), ())), preferred_element_type=F32)


def _dot_tn(a, b):
    return lax.dot_general(a.astype(BF16), b.astype(BF16), (((0,), (0,)), ((), ())), preferred_element_type=F32)


def _sigmoid(x):
    return 1.0 / (1.0 + jnp.exp(-x))


_GELU_C = 0.7978845608028654
_GELU_A = 0.044715


def _gelu(x):
    t = jnp.tanh(_GELU_C * (x + _GELU_A * (x * x * x)))
    return 0.5 * x * (1.0 + t), t


def _gelu_grad(x, t):
    return 0.5 * (1.0 + t) + 0.5 * x * (1.0 - t * t) * (_GELU_C * (1.0 + 3.0 * _GELU_A * x * x))


def _rms_fwd(x, w):
    r = lax.rsqrt(jnp.mean(x * x, axis=-1, keepdims=True) + EPS)
    return x * r * w, r


def _rms_bwd(x, r, w, dy):
    g = dy * w
    dx = r * g - x * (r * r * r) * jnp.mean(g * x, axis=-1, keepdims=True)
    dw = jnp.sum(dy * x * r, axis=0, keepdims=True)
    return dx, dw


class _Carried:
    def __init__(self, ins, out_shapes, sems, start, finish):
        self.ins, self.out_shapes, self.sems, self.start, self.finish = list(ins), list(out_shapes), list(sems), start, finish


def _both(first, second):
    n_i, n_o, n_s = len(first.ins), len(first.out_shapes), len(first.sems)

    def split(ins, outs, sems):
        return (ins[:n_i], outs[:n_o], sems[:n_s]), (ins[n_i:], outs[n_o:], sems[n_s:])

    def start(ins, outs, *sems):
        (i1, o1, s1), (i2, o2, s2) = split(ins, outs, sems)
        first.start(i1, o1, *s1)
        second.start(i2, o2, *s2)

    def finish(ins, outs, *sems):
        (i1, o1, s1), (i2, o2, s2) = split(ins, outs, sems)
        first.finish(i1, o1, *s1)
        second.finish(i2, o2, *s2)

    return _Carried(first.ins + second.ins, first.out_shapes + second.out_shapes, first.sems + second.sems, start, finish)


def _split_carried(refs, n_in, n_out, n_scratch, carried):
    n_ci, n_co, n_cs = len(carried.ins), len(carried.out_shapes), len(carried.sems)
    ins, rest = refs[:n_in], refs[n_in:]
    c_ins, rest = rest[:n_ci], rest[n_ci:]
    outs, rest = rest[:n_out], rest[n_out:]
    c_outs, rest = rest[:n_co], rest[n_co:]
    scr, c_sems = rest[:n_scratch], rest[n_scratch:]
    assert len(c_sems) == n_cs
    return tuple(ins) + tuple(outs) + tuple(scr), c_ins, c_outs, c_sems


def _rows_call(name, body, tm, row_ins, const_ins, row_outs, acc_outs=(), scratch=(), carried=None):
    n_rows = row_ins[0].shape[0]
    assert n_rows % tm == 0
    n_steps = n_rows // tm
    n_in = len(row_ins) + len(const_ins)
    n_ro = len(row_outs)
    n_acc = len(acc_outs)

    def kern(*refs):
        accs = refs[n_in + n_ro:n_in + n_ro + n_acc]

        @pl.when(pl.program_id(0) == 0)
        def _():
            for a in accs:
                a[...] = jnp.zeros_like(a)

        body(*refs)

    def whole(shape):
        nd = len(shape)
        return pl.BlockSpec(tuple(shape), lambda i: (0,) * nd)

    in_specs = [pl.BlockSpec((tm, a.shape[1]), lambda i: (i, 0)) for a in row_ins]
    in_specs += [whole(a.shape) for a in const_ins]
    out_specs = [pl.BlockSpec((tm, s.shape[1]), lambda i: (i, 0)) for s in row_outs]
    out_specs += [whole(s.shape) for s in acc_outs]
    return _call_carrying(
        kern, carried, name=name, grid=(n_steps,), in_specs=in_specs, out_specs=out_specs,
        out_shape=tuple(row_outs) + tuple(acc_outs), scratch_shapes=list(scratch), operands=list(row_ins) + list(const_ins))


def _call_carrying(body, carried, *, name, grid, in_specs, out_specs, out_shape, scratch_shapes, operands):
    n_in, n_out, n_scratch = len(in_specs), len(out_specs), len(scratch_shapes)
    kern = body
    if carried is not None:
        def kern(*refs):
            plain, c_ins, c_outs, c_sems = _split_carried(refs, n_in, n_out, n_scratch, carried)
            first, last = True, True
            for d, size in enumerate(grid):
                first = jnp.logical_and(first, pl.program_id(d) == 0)
                last = jnp.logical_and(last, pl.program_id(d) == size - 1)

            @pl.when(first)
            def _():
                carried.start(c_ins, c_outs, *c_sems)

            body(*plain)

            @pl.when(last)
            def _():
                carried.finish(c_ins, c_outs, *c_sems)

        in_specs = list(in_specs) + [_HBM] * len(carried.ins)
        out_specs = list(out_specs) + [_HBM] * len(carried.out_shapes)
        out_shape = tuple(out_shape) + tuple(carried.out_shapes)
        operands = list(operands) + carried.ins
        scratch_shapes = list(scratch_shapes) + carried.sems
    return pl.pallas_call(
        kern, name=name, grid=grid, in_specs=in_specs, out_specs=out_specs, out_shape=out_shape,
        scratch_shapes=scratch_shapes, compiler_params=_cparams(len(grid)),
    )(*operands)


def _sds(shape, dtype):
    return jax.ShapeDtypeStruct(tuple(shape), dtype)


def _matmul_tn(name, a, b, tm, tn, tk, stacked=False, carried=None):
    k_dim, m_dim = a.shape
    n_dim = b.shape[1]
    assert m_dim % tm == 0 and n_dim % tn == 0 and k_dim % tk == 0

    def kern(a_ref, b_ref, o_ref):
        @pl.when(pl.program_id(2) == 0)
        def _():
            o_ref[...] = jnp.zeros_like(o_ref)

        o_ref[...] += _dot_tn(a_ref[...], b_ref[...])

    if stacked:
        assert tm == m_dim
        out_shape = _sds((n_dim // tn, m_dim, tn), F32)
        out_spec = pl.BlockSpec((None, tm, tn), lambda i, j, k: (j, i, 0))
    else:
        out_shape = _sds((m_dim, n_dim), F32)
        out_spec = pl.BlockSpec((tm, tn), lambda i, j, k: (i, j))
    outs = _call_carrying(
        kern, carried, name=name, grid=(m_dim // tm, n_dim // tn, k_dim // tk),
        in_specs=[pl.BlockSpec((tk, tm), lambda i, j, k: (k, i)), pl.BlockSpec((tk, tn), lambda i, j, k: (k, j))],
        out_specs=[out_spec], out_shape=(out_shape,), scratch_shapes=[], operands=[a, b])
    return outs[0] if carried is None else outs


def _inproj_fwd(x, nw, w_uv, w_xbc, w_z, w_dt, tm=256, carried=None):
    n_tok = x.shape[0]

    def body(x_ref, nw_ref, wuv_ref, wxbc_ref, wz_ref, wdt_ref, puv_ref, pxbc_ref, pz_ref, pdt_ref):
        h, _ = _rms_fwd(x_ref[...], nw_ref[...])
        h = h.astype(BF16)
        puv_ref[...] = jnp.dot(h, wuv_ref[...], preferred_element_type=F32)
        pxbc_ref[...] = jnp.dot(h, wxbc_ref[...], preferred_element_type=F32)
        pz_ref[...] = jnp.dot(h, wz_ref[...], preferred_element_type=F32)
        pdt_ref[...] = jnp.dot(h, wdt_ref[...], preferred_element_type=F32)

    return _rows_call(
        "inproj_fwd", body, tm, [x], [nw, w_uv, w_xbc, w_z, w_dt],
        [_sds((n_tok, 2 * GM_WIDTH), F32), _sds((n_tok, CONV_CH), F32), _sds((n_tok, SSM_WIDTH), F32),
         _sds((n_tok, DT_PAD), F32)], carried=carried)


def _head_lane_mask(width, head):
    lane = lax.broadcasted_iota(jnp.int32, (1, width), 1)
    return (lane // HEAD_DIM) == head


def _split_terms(x, terms):
    parts = []
    for _ in range(terms):
        p = x.astype(BF16)
        parts.append(p)
        x = x - p.astype(F32)
    return parts


def _seg_dots(vals, ind, terms=2):
    m = vals[0].shape[0]
    parts = []
    for v in vals:
        parts += _split_terms(v, terms)
    red = jnp.dot(jnp.concatenate(parts, axis=0), ind, preferred_element_type=F32)
    outs = []
    for i in range(len(vals)):
        acc = red[i * terms * m:(i * terms + 1) * m]
        for t in range(1, terms):
            acc = acc + red[(i * terms + t) * m:(i * terms + t + 1) * m]
        outs.append(acc)
    return outs


def _tri_dot(mask, x, terms=3):
    n = x.shape[1]
    red = jnp.dot(mask.astype(BF16), jnp.concatenate(_split_terms(x, terms), axis=1), preferred_element_type=F32)
    acc = red[:, :n]
    for t in range(1, terms):
        acc = acc + red[:, t * n:(t + 1) * n]
    return acc


def _gmlp_common(puv, lnw, lnb, e_bf, et_bf):
    u = puv[:, :GM_WIDTH]
    v = puv[:, GM_WIDTH:]
    gu, tu = _gelu(u)
    gv, tv = _gelu(v)
    (s1,) = _seg_dots([gv], et_bf)
    (mu,) = _seg_dots([s1 * (1.0 / HEAD_DIM)], e_bf)
    xc = gv - mu
    (s2,) = _seg_dots([xc * xc], et_bf)
    (rstd,) = _seg_dots([lax.rsqrt(s2 * (1.0 / HEAD_DIM) + EPS)], e_bf)
    xhat = xc * rstd
    vn = xhat * lnw + lnb
    return u, v, gu, tu, tv, rstd, xhat, vn


def _tril_mask():
    r = lax.broadcasted_iota(jnp.int32, (CHUNK, CHUNK), 0)
    c = lax.broadcasted_iota(jnp.int32, (CHUNK, CHUNK), 1)
    return r >= c


def _head_blocks(v):
    return jnp.concatenate([jnp.where(_head_lane_mask(GM_WIDTH, h), v, jnp.zeros_like(v)) for h in range(N_HEADS)], axis=0)


def _causal_w_cat(w_cat):
    t = lax.broadcasted_iota(jnp.int32, (CHUNK, N_HEADS * CHUNK), 0)
    s = lax.broadcasted_iota(jnp.int32, (CHUNK, N_HEADS * CHUNK), 1) % CHUNK
    return jnp.where(t >= s, w_cat, 0.0).astype(BF16)


def _gmlp_chunk_fwd(puv, lnw, lnb, e_bf, et_bf, wm, bmap):
    _, _, gu, _, _, _, _, vn = _gmlp_common(puv, lnw, lnb, e_bf, et_bf)
    mixed = jnp.dot(wm, _head_blocks(vn.astype(BF16)), preferred_element_type=F32) + bmap
    return (gu * mixed).astype(BF16)


SUBLANES = 8


def _shift_down(x, tail, s):
    main = pltpu.roll(x, s, 0)
    row = lax.broadcasted_iota(jnp.int32, (SUBLANES, 1), 0)
    head = jnp.where(row < s, pltpu.roll(tail, s, 0), main[:SUBLANES])
    return jnp.concatenate([head, main[SUBLANES:]], axis=0)


def _shift_up(x, head_next, s):
    n = x.shape[0]
    main = pltpu.roll(x, n - s, 0)
    row = lax.broadcasted_iota(jnp.int32, (SUBLANES, 1), 0)
    last = jnp.where(row >= SUBLANES - s, pltpu.roll(head_next, SUBLANES - s, 0), main[n - SUBLANES:])
    return jnp.concatenate([main[:n - SUBLANES], last], axis=0)


def _ssd_pre(xr, tail, cw_ref, cb, pdt, dtb, alog, emap):
    rowi = lax.broadcasted_iota(jnp.int32, (CHUNK, 1), 0)
    shifted = [_shift_down(xr, tail, 3), _shift_down(xr, tail, 2), _shift_down(xr, tail, 1), xr]
    xc = cb
    for k in range(CONV_K):
        xc = xc + cw_ref[k] * shifted[k]
    sg = _sigmoid(xc)
    xa = xc * sg
    pre = pdt + dtb
    dt = jnp.maximum(pre, 0.0) + jnp.log(1.0 + jnp.exp(-jnp.abs(pre)))
    a_neg = -jnp.exp(alog)
    a_cs = _tri_dot(_tril_mask(), dt * a_neg)
    acs_map, dt_map = _seg_dots([a_cs, dt], emap, terms=3)
    return dict(shifted=shifted, xc=xc, sg=sg, xa=xa, pre=pre, dt=dt, a_neg=a_neg, a_cs=a_cs,
                acs_map=acs_map, dt_map=dt_map, rowi=rowi)


def _ssd_maps(p):
    last = p["rowi"] == CHUNK - 1
    aq_map = jnp.sum(jnp.where(last, p["acs_map"], 0.0), axis=0, keepdims=True)
    e_exp = jnp.exp(p["acs_map"])
    dte = jnp.exp(aq_map - p["acs_map"])
    cd = jnp.exp(aq_map)
    return last, e_exp, dte, cd


def _head_decay(a_cs, a_cs_t, head, tri):
    lane = lax.broadcasted_iota(jnp.int32, (1, DT_PAD), 1)
    sub = lax.broadcasted_iota(jnp.int32, (DT_PAD, 1), 0)
    col = jnp.sum(jnp.where(lane == head, a_cs, 0.0), axis=1, keepdims=True)
    row = jnp.sum(jnp.where(sub == head, a_cs_t, 0.0), axis=0, keepdims=True)
    return jnp.exp(jnp.where(tri, col - row, -1e30))


def _gate_fwd(y, z, nw):
    sz = _sigmoid(z)
    zg = z * sz
    yg = y * zg
    outs, rs = [], []
    for g in range(SSM_GROUPS):
        gs = slice(g * GROUP_W, (g + 1) * GROUP_W)
        o, r = _rms_fwd(yg[:, gs], nw[:, gs])
        outs.append(o)
        rs.append(r)
    return sz, zg, yg, outs, rs


def _ssd_const_specs():
    def whole(shape):
        nd = len(shape)
        return pl.BlockSpec(tuple(shape), lambda c: (0,) * nd)
    return [whole((CONV_K, 1, CONV_CH)), whole((1, CONV_CH)), whole((1, DT_PAD)), whole((1, DT_PAD)),
            whole((1, SSM_WIDTH)), whole((1, SSM_WIDTH)), whole((DT_PAD, SSM_WIDTH)), whole((SSM_WIDTH, DT_PAD))]


def _mixer_fwd(p_uv, p_xbc, p_z, p_dt, lnw, lnb, w_cat, bmap, conv_w, conv_b, dt_bias, a_log, dskip_map, norm_w,
               e_bf, et_bf, n_seq, carried=None):
    n_tok = p_xbc.shape[0]
    nc = n_tok // n_seq // CHUNK

    def body(puv3, xr3, z3, pdt3, lnw_ref, lnb_ref, wcat_ref, bmap_ref,
             cw_ref, cb_ref, dtb_ref, alog_ref, dsk_ref, nw_ref, e_ref, et_ref,
             ya3, yb3, yssd3, sprev3, wm_scr, prev3_scr, s3_scr):
        @pl.when(pl.program_id(0) == 0)
        def _():
            wm_scr[...] = _causal_w_cat(wcat_ref[...])
            prev3_scr[...] = jnp.zeros_like(prev3_scr)
            s3_scr[...] = jnp.zeros_like(s3_scr)

        for b in range(n_seq):
            one_sequence(puv3.at[b], xr3.at[b], z3.at[b], pdt3.at[b], lnw_ref, lnb_ref, bmap_ref,
                         cw_ref, cb_ref, dtb_ref, alog_ref, dsk_ref, nw_ref, e_ref, et_ref,
                         ya3.at[b], yb3.at[b], yssd3.at[b], sprev3.at[b], wm_scr, prev3_scr.at[b], s3_scr.at[b])

    def one_sequence(puv_ref, xr_ref, z_ref, pdt_ref, lnw_ref, lnb_ref, bmap_ref,
                     cw_ref, cb_ref, dtb_ref, alog_ref, dsk_ref, nw_ref, e_ref, et_ref,
                     ya_ref, yb_ref, yssd_ref, sprev_ref, wm_scr, prev_scr, s_scr):
        ya_ref[...] = _gmlp_chunk_fwd(puv_ref[...], lnw_ref[...], lnb_ref[...], e_ref[...], et_ref[...], wm_scr[...],
                                      bmap_ref[...])
        xr = xr_ref[...]
        p = _ssd_pre(xr, prev_scr[...], cw_ref, cb_ref[...], pdt_ref[...], dtb_ref[...], alog_ref[...], e_ref[...])
        _, e_exp, dte, cd = _ssd_maps(p)
        xs = p["xa"][:, :SSM_WIDTH]
        xd = xs * p["dt_map"]
        a_cs_t = p["a_cs"].T
        tri = _tril_mask()
        s_old = s_scr[...]
        sprev_ref[...] = s_old
        for g in range(SSM_GROUPS):
            gs = slice(g * GROUP_W, (g + 1) * GROUP_W)
            bm = p["xa"][:, SSM_WIDTH + g * SSM_STATE: SSM_WIDTH + (g + 1) * SSM_STATE].astype(BF16)
            cm = p["xa"][:, SSM_WIDTH + (SSM_GROUPS + g) * SSM_STATE: SSM_WIDTH + (SSM_GROUPS + g + 1) * SSM_STATE].astype(BF16)
            cb_mat = _dot_nt(cm, bm)
            xdg = xd[:, gs].astype(BF16)
            y_g = _dot(cm, s_old[:, gs]) * e_exp[:, gs] + dsk_ref[:, gs] * xs[:, gs]
            for r in range(SSM_GROUPS * 2):
                dm = _head_decay(p["a_cs"], a_cs_t, g * 4 + r, tri)
                full = jnp.dot((cb_mat * dm).astype(BF16), xdg, preferred_element_type=F32)
                y_g = y_g + jnp.where(_head_lane_mask(GROUP_W, r), full, 0.0)
            yssd_ref[:, gs] = y_g
            s_scr[:, gs] = cd[:, gs] * s_old[:, gs] + _dot_tn(bm, xd[:, gs] * dte[:, gs])
        _, _, _, outs, _ = _gate_fwd(yssd_ref[...], z_ref[...], nw_ref[...])
        for g in range(SSM_GROUPS):
            yb_ref[:, g * GROUP_W:(g + 1) * GROUP_W] = outs[g].astype(BF16)
        prev_scr[...] = xr[CHUNK - SUBLANES:, :]

    seq_len = n_tok // n_seq

    def rows(width):
        return pl.BlockSpec((n_seq, CHUNK, width), lambda c: (0, c, 0))

    def whole(shape):
        nd = len(shape)
        return pl.BlockSpec(tuple(shape), lambda c: (0,) * nd)

    def by_seq(a):
        return a.reshape(n_seq, seq_len, a.shape[-1])

    outs = _call_carrying(
        body, carried, name="mixer_fwd", grid=(nc,),
        in_specs=[rows(2 * GM_WIDTH), rows(CONV_CH), rows(SSM_WIDTH), rows(DT_PAD), whole(lnw.shape), whole(lnb.shape),
                  whole(w_cat.shape), whole(bmap.shape)] + _ssd_const_specs(),
        out_specs=[rows(GM_WIDTH), rows(SSM_WIDTH), rows(SSM_WIDTH), rows(SSM_WIDTH)],
        out_shape=(_sds((n_seq, seq_len, GM_WIDTH), BF16), _sds((n_seq, seq_len, SSM_WIDTH), BF16),
                   _sds((n_seq, seq_len, SSM_WIDTH), F32), _sds((n_seq, seq_len, SSM_WIDTH), F32)),
        scratch_shapes=[pltpu.VMEM((CHUNK, N_HEADS * CHUNK), BF16), pltpu.VMEM((n_seq, SUBLANES, CONV_CH), F32),
                        pltpu.VMEM((n_seq, SSM_STATE, SSM_WIDTH), F32)],
        operands=[by_seq(p_uv), by_seq(p_xbc), by_seq(p_z), by_seq(p_dt), lnw, lnb, w_cat, bmap, conv_w, conv_b, dt_bias,
                  a_log, dskip_map, norm_w, e_bf, et_bf])
    return tuple(o.reshape(n_tok, o.shape[-1]) for o in outs[:4]) + tuple(outs[4:])


def _outproj_fwd(ya, yb, x, w_out, nw_post, nw_pre2, tm=256):
    n_tok = x.shape[0]

    def body(ya_ref, yb_ref, x_ref, wo_ref, nwa_ref, nwb_ref, o_ref, x1_ref, h2_ref):
        o = jnp.dot(ya_ref[...], wo_ref[:GM_WIDTH, :], preferred_element_type=F32)
        o = o + jnp.dot(yb_ref[...], wo_ref[GM_WIDTH:, :], preferred_element_type=F32)
        on, _ = _rms_fwd(o, nwa_ref[...])
        x1 = x_ref[...] + on
        h2, _ = _rms_fwd(x1, nwb_ref[...])
        o_ref[...] = o
        x1_ref[...] = x1
        h2_ref[...] = h2.astype(BF16)

    return _rows_call("outproj_fwd", body, tm, [ya, yb, x], [w_out, nw_post, nw_pre2],
                      [_sds((n_tok, D_MODEL), F32), _sds((n_tok, D_MODEL), F32), _sds((n_tok, D_MODEL), BF16)])


def _up_cols(wup_ref, j):
    per = (D_FF // N_CHIPS) // FF_TILE
    return wup_ref[j // per, :, (j % per) * FF_TILE:(j % per + 1) * FF_TILE]


def _down_rows(wda_ref, wdb_ref, j):
    assert 2 * FF_TILE == D_FF // N_CHIPS
    return (wda_ref if j % 2 == 0 else wdb_ref)[j // 2]


def _skewed_rows_call(name, main, tail, tm, lead_ins, lag_ins, const_ins, lead_outs, lag_outs, acc_outs, carry):
    n_rows = lead_ins[0].shape[0]
    assert n_rows % tm == 0
    n = n_rows // tm
    counts = [len(lead_ins), len(lag_ins), len(const_ins), len(lead_outs), len(lag_outs), len(acc_outs)]

    def kern(*refs):
        groups, pos = [], 0
        for cnt in counts:
            groups.append(refs[pos:pos + cnt])
            pos += cnt
        lead_i, lag_i, consts, lead_o, lag_o, accs = groups
        carry_scr = refs[pos]
        i = pl.program_id(0)

        @pl.when(i == 0)
        def _():
            for a in accs:
                a[...] = jnp.zeros_like(a)
            carry_scr[...] = main(lead_i, consts, lead_o)

        @pl.when(jnp.logical_and(i > 0, i < n))
        def _():
            previous = carry_scr[...]
            carry_scr[...] = main(lead_i, consts, lead_o)
            tail(previous, lag_i, consts, lag_o, accs)

        @pl.when(i == n)
        def _():
            tail(carry_scr[...], lag_i, consts, lag_o, accs)

    def lead(width):
        return pl.BlockSpec((tm, width), lambda i: (jnp.minimum(i, n - 1), 0))

    def lag(width):
        return pl.BlockSpec((tm, width), lambda i: (jnp.maximum(i - 1, 0), 0))

    def whole(shape, **kw):
        nd = len(shape)
        return pl.BlockSpec(tuple(shape), lambda i: (0,) * nd, **kw)

    const_specs = [whole(a.shape, pipeline_mode=pl.Buffered(1)) for a in const_ins]
    return pl.pallas_call(
        kern, name=name, grid=(n + 1,),
        in_specs=[lead(a.shape[1]) for a in lead_ins] + [lag(a.shape[1]) for a in lag_ins] + const_specs,
        out_specs=[lead(s.shape[1]) for s in lead_outs] + [lag(s.shape[1]) for s in lag_outs] + [whole(s.shape) for s in acc_outs],
        out_shape=tuple(lead_outs) + tuple(lag_outs) + tuple(acc_outs),
        scratch_shapes=[pltpu.VMEM(carry, F32)], compiler_params=_cparams(1),
    )(*lead_ins, *lag_ins, *const_ins)


def _mlp_fwd(h2, x1, tgt, w_up, w_down_a, w_down_b, nw, tm=512):
    n_tok = x1.shape[0]

    def main(lead_i, consts, lead_o):
        (h2_ref,), (wup_ref, wda_ref, wdb_ref, _), (f_ref,) = lead_i, consts, lead_o
        h2v = h2_ref[...]
        acc = jnp.zeros((tm, D_MODEL), F32)
        for j in range(D_FF // FF_TILE):
            cs = slice(j * FF_TILE, (j + 1) * FF_TILE)
            u = jnp.dot(h2v, _up_cols(wup_ref, j), preferred_element_type=F32)
            f = jnp.square(jnp.maximum(u, 0.0)).astype(BF16)
            f_ref[:, cs] = f
            acc = acc + jnp.dot(f, _down_rows(wda_ref, wdb_ref, j), preferred_element_type=F32)
        return acc

    def tail(acc, lag_i, consts, lag_o, accs):
        (x1_ref, tgt_ref), nw_ref, (dd_ref, dy_ref), (loss_ref, dnw_ref) = lag_i, consts[3], lag_o, accs
        dn, r = _rms_fwd(acc, nw_ref[...])
        e = x1_ref[...] + dn - tgt_ref[...]
        loss_ref[...] += jnp.full(loss_ref.shape, (0.5 / D_MODEL) * jnp.sum(e * e), F32)
        dy = e * (1.0 / D_MODEL)
        dd, dnw = _rms_bwd(acc, r, nw_ref[...], dy)
        dy_ref[...] = dy
        dd_ref[...] = dd.astype(BF16)
        dnw_ref[...] += dnw

    return _skewed_rows_call(
        "mlp_fwd", main, tail, tm, [h2], [x1, tgt], [w_up, w_down_a, w_down_b, nw],
        [_sds((n_tok, D_FF), BF16)], [_sds((n_tok, D_MODEL), BF16), _sds((n_tok, D_MODEL), F32)],
        [_sds((8, 128), F32), _sds((1, D_MODEL), F32)], carry=(tm, D_MODEL))


def _mlp_bwd(dd, f, x1, dy, w_down_a, w_down_b, w_up, nw, tm=256):
    n_tok = x1.shape[0]

    def main(lead_i, consts, lead_o):
        (dd_ref, f_ref), (wda_ref, wdb_ref, wup_ref, _), (dup_ref,) = lead_i, consts, lead_o
        ddv = dd_ref[...]
        acc = jnp.zeros((tm, D_MODEL), F32)
        for j in range(D_FF // FF_TILE):
            cs = slice(j * FF_TILE, (j + 1) * FF_TILE)
            df = _dot_nt(ddv, _down_rows(wda_ref, wdb_ref, j))
            du = (df * (2.0 * jnp.sqrt(f_ref[:, cs].astype(F32)))).astype(BF16)
            dup_ref[:, cs] = du
            acc = acc + _dot_nt(du, _up_cols(wup_ref, j))
        return acc

    def tail(acc, lag_i, consts, lag_o, accs):
        (x1_ref, dy_ref), nw_ref, (dx1_ref,), (dnw_ref,) = lag_i, consts[3], lag_o, accs
        x1v = x1_ref[...]
        _, r = _rms_fwd(x1v, nw_ref[...])
        dx, dnw = _rms_bwd(x1v, r, nw_ref[...], acc)
        dx1_ref[...] = dy_ref[...] + dx
        dnw_ref[...] += dnw

    return _skewed_rows_call(
        "mlp_bwd", main, tail, tm, [dd, f], [x1, dy], [w_down_a, w_down_b, w_up, nw],
        [_sds((n_tok, D_FF), BF16)], [_sds((n_tok, D_MODEL), F32)], [_sds((1, D_MODEL), F32)], carry=(tm, D_MODEL))


def _outproj_bwd(dx1, o, w_out, nw, tm=256, carried=None):
    n_tok = dx1.shape[0]

    def body(dx1_ref, o_ref, wo_ref, nw_ref, do_ref, dya_ref, dyb_ref, dnw_ref):
        ov = o_ref[...]
        _, r = _rms_fwd(ov, nw_ref[...])
        do, dnw = _rms_bwd(ov, r, nw_ref[...], dx1_ref[...])
        dob = do.astype(BF16)
        do_ref[...] = dob
        dya_ref[...] = _dot_nt(dob, wo_ref[:GM_WIDTH, :])
        dyb_ref[...] = _dot_nt(dob, wo_ref[GM_WIDTH:, :])
        dnw_ref[...] += dnw

    return _rows_call("outproj_bwd", body, tm, [dx1, o], [w_out, nw],
                      [_sds((n_tok, D_MODEL), BF16), _sds((n_tok, GM_WIDTH), F32), _sds((n_tok, SSM_WIDTH), F32)],
                      [_sds((1, D_MODEL), F32)], carried=carried)


def _gmlp_bwd(p_uv, dya, lnw, lnb, e_bf, et_bf, w_cat, w_stack, bmap, carried=None):
    n_tok = p_uv.shape[0]
    chunks_per_step = 2

    def body(puv_ref, dya_ref, lnw_ref, lnb_ref, e_ref, et_ref, wcat_ref, wstack_ref, bmap_ref,
             dpuv_ref, dws_ref, dbs_ref, dlnw_ref, dlnb_ref, wm_scr, wsm_scr):
        t_stk = lax.broadcasted_iota(jnp.int32, (N_HEADS * CHUNK, CHUNK), 0) % CHUNK
        s_stk = lax.broadcasted_iota(jnp.int32, (N_HEADS * CHUNK, CHUNK), 1)

        @pl.when(pl.program_id(0) == 0)
        def _():
            wm_scr[...] = _causal_w_cat(wcat_ref[...])
            wsm_scr[...] = jnp.where(t_stk >= s_stk, wstack_ref[...], 0.0).astype(BF16)

        lnw_v = lnw_ref[...]
        e_v, et_v = e_ref[...], et_ref[...]

        def one_chunk(rows):
            u, v, gu, tu, tv, rstd, xhat, vn = _gmlp_common(puv_ref[rows, :], lnw_v, lnb_ref[...], e_v, et_v)
            vnb = vn.astype(BF16)
            mixed = jnp.dot(wm_scr[...], _head_blocks(vnb), preferred_element_type=F32) + bmap_ref[...]
            dy = dya_ref[rows, :]
            du = dy * mixed * _gelu_grad(u, tu)
            dmixed = dy * gu
            (dbs,) = _seg_dots([dmixed], et_v)
            dblocks = _head_blocks(dmixed.astype(BF16))
            dvn = lax.dot_general(wsm_scr[...], dblocks, (((0,), (0,)), ((), ())), preferred_element_type=F32)
            dws = lax.dot_general(dblocks, vnb, (((1,), (1,)), ((), ())), preferred_element_type=F32)
            dxh = dvn * lnw_v
            m1, m2 = _seg_dots([dxh, dxh * xhat], et_v)
            m1, m2 = _seg_dots([m1 * (1.0 / HEAD_DIM), m2 * (1.0 / HEAD_DIM)], e_v)
            dgv = rstd * (dxh - m1 - xhat * m2)
            dv = dgv * _gelu_grad(v, tv)
            dpuv_ref[rows, :GM_WIDTH] = du.astype(BF16)
            dpuv_ref[rows, GM_WIDTH:] = dv.astype(BF16)
            return dbs, dws, jnp.sum(dvn * xhat, axis=0, keepdims=True), jnp.sum(dvn, axis=0, keepdims=True)

        parts = [one_chunk(slice(k * CHUNK, (k + 1) * CHUNK)) for k in range(chunks_per_step)]
        dbs, dws, dlnw, dlnb = [functools.reduce(lambda a, b: a + b, vals) for vals in zip(*parts)]
        dbs_ref[...] += dbs
        dws_ref[...] += jnp.where(t_stk >= s_stk, dws, 0.0)
        dlnw_ref[...] += dlnw
        dlnb_ref[...] += dlnb

    return _rows_call(
        "gmlp_bwd", body, chunks_per_step * CHUNK, [p_uv, dya], [lnw, lnb, e_bf, et_bf, w_cat, w_stack, bmap],
        [_sds((n_tok, 2 * GM_WIDTH), BF16)],
        [_sds((N_HEADS * CHUNK, CHUNK), F32), _sds((CHUNK, DT_PAD), F32), _sds((1, GM_WIDTH), F32),
         _sds((1, GM_WIDTH), F32)],
        scratch=[pltpu.VMEM((CHUNK, N_HEADS * CHUNK), BF16), pltpu.VMEM((N_HEADS * CHUNK, CHUNK), BF16)],
        carried=carried)


def _ssd_bwd(p_xbc, p_z, p_dt, yssd, sprev, dyb, conv_w, conv_b, dt_bias, a_log, dskip_map, norm_w, e_bf, et_bf, n_seq,
             carried=None):
    n_tok = p_xbc.shape[0]
    nc = n_tok // n_seq // CHUNK

    def body(xr3, xprev3, z3, pdt3, yssd3, sprev3, dyb3,
             cw_ref, cb_ref, dtb_ref, alog_ref, dsk_ref, nw_ref, e_ref, et_ref,
             dpxbc3, dpz3, dpdt3, dcw_ref, dcb_ref, ddtb_ref, dalog_ref, ddsk_ref, dnw_ref,
             ds3_scr, nxt3_scr, dxa3_scr):
        @pl.when(pl.program_id(0) == 0)
        def _():
            for a in (dcw_ref, dcb_ref, ddtb_ref, dalog_ref, ddsk_ref, dnw_ref, ds3_scr, nxt3_scr):
                a[...] = jnp.zeros_like(a)

        for b in range(n_seq):
            one_sequence(xr3.at[b], xprev3.at[b], z3.at[b], pdt3.at[b], yssd3.at[b], sprev3.at[b], dyb3.at[b],
                         cw_ref, cb_ref, dtb_ref, alog_ref, dsk_ref, nw_ref, e_ref, et_ref,
                         dpxbc3.at[b], dpz3.at[b], dpdt3.at[b], dcw_ref, dcb_ref, ddtb_ref, dalog_ref, ddsk_ref, dnw_ref,
                         ds3_scr.at[b], nxt3_scr.at[b], dxa3_scr.at[b])

    def one_sequence(xr_ref, xprev_ref, z_ref, pdt_ref, yssd_ref, sprev_ref, dyb_ref,
                     cw_ref, cb_ref, dtb_ref, alog_ref, dsk_ref, nw_ref, e_ref, et_ref,
                     dpxbc_ref, dpz_ref, dpdt_ref, dcw_ref, dcb_ref, ddtb_ref, dalog_ref, ddsk_ref, dnw_ref,
                     ds_scr, nxt_scr, dxa_scr):
        chunk = nc - 1 - pl.program_id(0)
        xr = xr_ref[...]
        prev = jnp.where(chunk == 0, 0.0, xprev_ref[...])
        et_v = et_ref[...]
        p = _ssd_pre(xr, prev, cw_ref, cb_ref[...], pdt_ref[...], dtb_ref[...], alog_ref[...], e_ref[...])
        last, e_exp, dte, cd = _ssd_maps(p)
        rowi = p["rowi"]
        xs = p["xa"][:, :SSM_WIDTH]
        xd = xs * p["dt_map"]
        a_cs_t = p["a_cs"].T
        tri = _tril_mask()
        dsk = dsk_ref[...]
        nw_v = nw_ref[...]

        yv = yssd_ref[...]
        zv = z_ref[...]
        sz, zg, yg, _, rs = _gate_fwd(yv, zv, nw_v)
        dout = dyb_ref[...]
        for g in range(SSM_GROUPS):
            gs = slice(g * GROUP_W, (g + 1) * GROUP_W)
            dyg_g, dnw_g = _rms_bwd(yg[:, gs], rs[g], nw_v[:, gs], dout[:, gs])
            dnw_ref[:, gs] += dnw_g
            dxa_scr[:, gs] = dyg_g
        dyg = dxa_scr[:, :SSM_WIDTH]
        d_y = dyg * zg
        dpz_ref[...] = (dyg * yv * (sz + zv * sz * (1.0 - sz))).astype(BF16)

        s_prev = sprev_ref[...]
        ds_next = ds_scr[...]
        lane_dt = lax.broadcasted_iota(jnp.int32, (1, DT_PAD), 1)
        da_cols = jnp.zeros((CHUNK, DT_PAD), F32)
        for g in range(SSM_GROUPS):
            gs = slice(g * GROUP_W, (g + 1) * GROUP_W)
            b_off = SSM_WIDTH + g * SSM_STATE
            c_off = SSM_WIDTH + (SSM_GROUPS + g) * SSM_STATE
            bm = p["xa"][:, b_off:b_off + SSM_STATE].astype(BF16)
            cm = p["xa"][:, c_off:c_off + SSM_STATE].astype(BF16)
            cb_mat = _dot_nt(cm, bm)
            d_yg = d_y[:, gs]
            d_ygb = d_yg.astype(BF16)
            xdg = xd[:, gs]
            xdgb = xdg.astype(BF16)
            ds_g = ds_next[:, gs]
            sp_g = s_prev[:, gs]
            bds = _dot(bm, ds_g)
            dcs = d_yg * e_exp[:, gs]
            d_c = _dot_nt(dcs, sp_g)
            ds_scr[:, gs] = cd[:, gs] * ds_g + _dot_tn(cm, dcs)
            d_b = _dot_nt(xdg * dte[:, gs], ds_g)
            dxd_g = bds * dte[:, gs]
            sum_dcb = jnp.zeros((CHUNK, CHUNK), F32)
            for r in range(SSM_GROUPS * 2):
                head = g * 4 + r
                mask = _head_lane_mask(GROUP_W, r)
                dm = _head_decay(p["a_cs"], a_cs_t, head, tri)
                m_mat = cb_mat * dm
                g_mat = _dot_nt(jnp.where(mask, d_yg, 0.0), xdgb)
                w_mat = g_mat * m_mat
                sum_dcb = sum_dcb + g_mat * dm
                dxd_g = dxd_g + jnp.where(mask, _dot_tn(m_mat, d_ygb), 0.0)
                da_h = jnp.sum(w_mat - w_mat.T, axis=1, keepdims=True)
                da_cols = da_cols + jnp.where(lane_dt == head, da_h, 0.0)
            d_c = d_c + _dot(sum_dcb, bm)
            d_b = d_b + _dot_tn(sum_dcb, cm)
            dxa_scr[:, b_off:b_off + SSM_STATE] = d_b
            dxa_scr[:, c_off:c_off + SSM_STATE] = d_c
            y_off_g = _dot(cm, sp_g) * e_exp[:, gs]
            t3 = bds * xdg * dte[:, gs]
            tail = jnp.sum(t3, axis=0, keepdims=True) + jnp.sum(ds_g * sp_g, axis=0, keepdims=True) * cd[:, gs]
            pre_g = d_yg * y_off_g - t3 + jnp.where(last, tail, 0.0)
            s_pre, ddt_g, s_dsk = _seg_dots([pre_g, dxd_g * xs[:, gs], d_yg * xs[:, gs]], et_v[gs, :])
            da_cols = da_cols + s_pre
            ddsk_ref[...] += jnp.sum(s_dsk, axis=0, keepdims=True)
            dxa_scr[:, gs] = dxd_g * p["dt_map"][:, gs] + dsk[:, gs] * d_yg
            if g == 0:
                ddt = ddt_g
            else:
                ddt = ddt + ddt_g
        r_i = lax.broadcasted_iota(jnp.int32, (CHUNK, CHUNK), 0)
        c_i = lax.broadcasted_iota(jnp.int32, (CHUNK, CHUNK), 1)
        ddta = _tri_dot(r_i <= c_i, da_cols, terms=2)
        ddt = ddt + ddta * p["a_neg"]
        dalog_ref[...] += jnp.sum(ddta * p["dt"], axis=0, keepdims=True) * p["a_neg"]
        draw = ddt * _sigmoid(p["pre"])
        ddtb_ref[...] += jnp.sum(draw, axis=0, keepdims=True)
        dpdt_ref[...] = draw.astype(BF16)

        xc = p["xc"]
        sg = p["sg"]
        dxc = dxa_scr[...] * (sg + xc * sg * (1.0 - sg))
        dcb_ref[...] += jnp.sum(dxc, axis=0, keepdims=True)
        for k in range(CONV_K):
            dcw_ref[k] += jnp.sum(dxc * p["shifted"][k], axis=0, keepdims=True)
        nxt = nxt_scr[...]
        dxr = cw_ref[3] * dxc
        for s in range(1, CONV_K):
            dxr = dxr + cw_ref[CONV_K - 1 - s] * _shift_up(dxc, nxt, s)
        dpxbc_ref[...] = dxr.astype(BF16)
        nxt_scr[...] = dxc[:SUBLANES, :]

    seq_len = n_tok // n_seq

    def rows(width):
        return pl.BlockSpec((n_seq, CHUNK, width), lambda s: (0, nc - 1 - s, 0))

    tiles = CHUNK // SUBLANES
    prev_rows = pl.BlockSpec((n_seq, SUBLANES, CONV_CH), lambda s: (0, jnp.maximum((nc - 1 - s) * tiles - 1, 0), 0))

    def whole(shape):
        nd = len(shape)
        return pl.BlockSpec(tuple(shape), lambda s: (0,) * nd)

    def by_seq(a):
        return a.reshape(n_seq, seq_len, a.shape[-1])

    acc_shapes = [(CONV_K, 1, CONV_CH), (1, CONV_CH), (1, DT_PAD), (1, DT_PAD), (1, DT_PAD), (1, SSM_WIDTH)]
    xbc3 = by_seq(p_xbc)
    outs = _call_carrying(
        body, carried, name="ssd_bwd", grid=(nc,),
        in_specs=[rows(CONV_CH), prev_rows, rows(SSM_WIDTH), rows(DT_PAD), rows(SSM_WIDTH), rows(SSM_WIDTH),
                  rows(SSM_WIDTH)] + _ssd_const_specs(),
        out_specs=[rows(CONV_CH), rows(SSM_WIDTH), rows(DT_PAD)] + [whole(s) for s in acc_shapes],
        out_shape=tuple([_sds((n_seq, seq_len, CONV_CH), BF16), _sds((n_seq, seq_len, SSM_WIDTH), BF16),
                         _sds((n_seq, seq_len, DT_PAD), BF16)] + [_sds(s, F32) for s in acc_shapes]),
        scratch_shapes=[pltpu.VMEM((n_seq, SSM_STATE, SSM_WIDTH), F32), pltpu.VMEM((n_seq, SUBLANES, CONV_CH), F32),
                        pltpu.VMEM((n_seq, CHUNK, CONV_CH), F32)],
        operands=[xbc3, xbc3, by_seq(p_z), by_seq(p_dt), by_seq(yssd), by_seq(sprev), by_seq(dyb), conv_w, conv_b, dt_bias,
                  a_log, dskip_map, norm_w, e_bf, et_bf])
    return tuple(o.reshape(n_tok, o.shape[-1]) for o in outs[:3]) + tuple(outs[3:])


def _inproj_bwd(dp_uv, dp_xbc, dp_z, dp_dt, x, dx1, w_uv, w_xbc, w_z, w_dt, nw, tm=256, carried=None):
    n_tok = x.shape[0]

    def body(duv_ref, dxbc_ref, dz_ref, ddt_ref, x_ref, dx1_ref, wuv_ref, wxbc_ref, wz_ref, wdt_ref, nw_ref,
             gx_ref, h_ref, dnw_ref):
        dh = _dot_nt(duv_ref[...], wuv_ref[...]) + _dot_nt(dxbc_ref[...], wxbc_ref[...])
        dh = dh + _dot_nt(dz_ref[...], wz_ref[...]) + _dot_nt(ddt_ref[...], wdt_ref[...])
        xv = x_ref[...]
        h, r = _rms_fwd(xv, nw_ref[...])
        dx, dnw = _rms_bwd(xv, r, nw_ref[...], dh)
        gx_ref[...] = dx1_ref[...] + dx
        h_ref[...] = h.astype(BF16)
        dnw_ref[...] += dnw

    return _rows_call("inproj_bwd", body, tm, [dp_uv, dp_xbc, dp_z, dp_dt, x, dx1], [w_uv, w_xbc, w_z, w_dt, nw],
                      [_sds((n_tok, D_MODEL), F32), _sds((n_tok, D_MODEL), BF16)], [_sds((1, D_MODEL), F32)],
                      carried=carried)


def _const_maps():
    lane = jnp.arange(SSM_WIDTH) // HEAD_DIM
    e_bf = (jnp.arange(DT_PAD)[:, None] == lane[None, :]).astype(BF16)
    return e_bf, e_bf.T


def _pad_lanes(v, width):
    return jnp.pad(v, ((0, 0), (0, width - v.shape[1])))


SHARD_COLS = IN_COLS // N_CHIPS
_UV_END = 2 * GM_WIDTH
_Z_END = _UV_END + SSM_WIDTH
_XBC_END = _Z_END + CONV_CH


def _cols_from_shards(w4, lo, hi):
    pieces = []
    for j in range(N_CHIPS):
        a, b = max(lo, j * SHARD_COLS), min(hi, (j + 1) * SHARD_COLS)
        if a < b:
            pieces.append(w4[j][:, a - j * SHARD_COLS:b - j * SHARD_COLS])
    return pieces[0] if len(pieces) == 1 else jnp.concatenate(pieces, axis=1)


def _shards_from_cols(blocks):
    shards = []
    for j in range(N_CHIPS):
        pieces = []
        for arr, lo, hi in blocks:
            a, b = max(lo, j * SHARD_COLS), min(hi, (j + 1) * SHARD_COLS)
            if a < b:
                pieces.append(arr[:, a - lo:b - lo])
        shards.append(pieces[0] if len(pieces) == 1 else jnp.concatenate(pieces, axis=1))
    return jnp.stack(shards)


def _forward_backward(x, tgt, w_in4, conv_w, small, out_shard, up_shard, down_shard, core):
    n_seq, seq_len, _ = x.shape
    n_tok = n_seq * seq_len
    x2 = x.reshape(n_tok, D_MODEL)
    tgt2 = tgt.reshape(n_tok, D_MODEL)
    e_bf, et_bf = _const_maps()

    w_uv = _cols_from_shards(w_in4, 0, _UV_END)
    w_z = _cols_from_shards(w_in4, _UV_END, _Z_END)
    w_xbc = _cols_from_shards(w_in4, _Z_END, _XBC_END)
    w_dt = _pad_lanes(_cols_from_shards(w_in4, _XBC_END, IN_COLS), DT_PAD)

    nw_pre = small["norm_mix_pre"]
    lnw = small["gm_ln_w"].reshape(1, GM_WIDTH)
    lnb = small["gm_ln_b"].reshape(1, GM_WIDTH)
    w_stack = small["gm_w_s"].reshape(N_HEADS * CHUNK, CHUNK)
    w_cat = jnp.transpose(small["gm_w_s"], (1, 0, 2)).reshape(CHUNK, N_HEADS * CHUNK)
    bmap = jnp.repeat(small["gm_b_s"].T, HEAD_DIM, axis=1)
    cw3 = conv_w.reshape(CONV_K, 1, CONV_CH)
    conv_b = small["conv_b"]
    dt_bias = _pad_lanes(small["dt_bias"], DT_PAD)
    a_log = _pad_lanes(small["a_log"], DT_PAD)
    dskip_map = jnp.repeat(small["d_skip"], HEAD_DIM, axis=1)
    ssm_nw = small["ssm_norm_w"]

    half = down_shard.shape[0] // 2
    p_uv, p_xbc, p_z, p_dt, w_out4, w_down_a = _inproj_fwd(
        x2, nw_pre, w_uv, w_xbc, w_z, w_dt, carried=_allgather_exchange([out_shard, down_shard[:half]]))
    ssd_consts = (cw3, conv_b, dt_bias, a_log, dskip_map, ssm_nw, e_bf, et_bf)
    ya, yb, yssd, sprev, w_up4, w_down_b = _mixer_fwd(
        p_uv, p_xbc, p_z, p_dt, lnw, lnb, w_cat, bmap, *ssd_consts, n_seq,
        carried=_allgather_exchange([up_shard, down_shard[half:]]))
    w_out_b = w_out4.reshape(D_MODEL, D_MODEL)
    o, x1, h2 = _outproj_fwd(ya, yb, x2, w_out_b, small["norm_mix_post"], small["norm_ffn_pre"])
    f, dd, dy, loss_acc, d_nffn_post = _mlp_fwd(h2, x1, tgt2, w_up4, w_down_a, w_down_b, small["norm_ffn_post"])

    dup, dx1, d_nffn_pre = _mlp_bwd(dd, f, x1, dy, w_down_a, w_down_b, w_up4, small["norm_ffn_pre"])
    tk = min(DW_TOKENS_PER_STEP, n_tok)
    g_up = _matmul_tn("dw_up", h2, dup, D_MODEL, D_MODEL, tk, stacked=True)
    g_down = _matmul_tn("dw_down", f, dd, 1024, D_MODEL, tk).reshape(N_CHIPS, D_FF // N_CHIPS, D_MODEL)
    do, dya, dyb, d_nmix_post, got_up, got_down = _outproj_bwd(
        dx1, o, w_out_b, small["norm_mix_post"], carried=_pair_exchange([g_up, g_down]))
    h_up = _pair_sum(core, g_up, got_up, 256)
    h_down = _pair_sum(core, g_down, got_down, 256)
    g_out_a = _matmul_tn("dw_out_a", ya, do, GM_WIDTH, D_MODEL, tk)
    g_out_b = _matmul_tn("dw_out_b", yb, do, SSM_WIDTH, D_MODEL, tk)
    g_out = jnp.concatenate([g_out_a, g_out_b], axis=0).reshape(N_CHIPS, D_MODEL // N_CHIPS, D_MODEL)
    dp_uv, d_ws, d_bs_t, d_lnw, d_lnb, slab_up, got_out = _gmlp_bwd(
        p_uv, dya, lnw, lnb, e_bf, et_bf, w_cat, w_stack, bmap,
        carried=_both(_chip_exchange([h_up]), _pair_exchange([g_out])))
    h_out = _pair_sum(core, g_out, got_out, 128)
    early = {
        "gm_ln_w": d_lnw.reshape(N_HEADS, HEAD_DIM), "gm_ln_b": d_lnb.reshape(N_HEADS, HEAD_DIM),
        "gm_w_s": d_ws.reshape(N_HEADS, CHUNK, CHUNK), "gm_b_s": d_bs_t[:, :N_HEADS].T,
        "norm_mix_post": d_nmix_post, "norm_ffn_pre": d_nffn_pre, "norm_ffn_post": d_nffn_post,
    }
    packed_early = _pack(early, tuple(early), tail=loss_acc[0, 0].reshape(1))
    (dp_xbc, dp_z, dp_dt, d_cw, d_cb, d_dtb, d_alog, d_dsk, d_ssm_nw, slab_down, slab_out, all_early) = _ssd_bwd(
        p_xbc, p_z, p_dt, yssd, sprev, dyb, *ssd_consts, n_seq,
        carried=_both(_chip_exchange([h_down, h_out]), _device_gather_exchange(packed_early)))
    gx, h, d_nmix_pre = _inproj_bwd(dp_uv, dp_xbc, dp_z, dp_dt, x2, dx1, w_uv, w_xbc, w_z, w_dt, nw_pre)
    late = {
        "norm_mix_pre": d_nmix_pre, "conv_w": d_cw.reshape(CONV_K, CONV_CH), "conv_b": d_cb,
        "dt_bias": d_dtb[:, :N_HEADS], "a_log": d_alog[:, :N_HEADS], "d_skip": d_dsk[:, :N_HEADS],
        "ssm_norm_w": d_ssm_nw,
    }
    g_uv, all_late = _matmul_tn("dw_in_uv", h, dp_uv, D_MODEL, 2 * GM_WIDTH, tk,
                                carried=_device_gather_exchange(_pack(late, tuple(late))))
    sum_early = _ordered_sum("small_sum_early", all_early)
    small_sum = _unpack(sum_early, {n: v.shape for n, v in early.items()}, tuple(early))
    small_sum.update(_unpack(_ordered_sum("small_sum_late", all_late), {n: v.shape for n, v in late.items()}, tuple(late)))
    loss = sum_early.reshape(-1)[sum(v.size for v in early.values())]
    g_xbc = _matmul_tn("dw_in_xbc", h, dp_xbc, D_MODEL, CONV_CH, tk)
    g_z = _matmul_tn("dw_in_z", h, dp_z, D_MODEL, SSM_WIDTH, tk)
    g_dt = _matmul_tn("dw_in_dt", h, dp_dt, D_MODEL, DT_PAD, tk)

    g_in = _shards_from_cols([(g_uv, 0, _UV_END), (g_z, _UV_END, _Z_END), (g_xbc, _Z_END, _XBC_END),
                              (g_dt, _XBC_END, IN_COLS)])

    (got_in,) = _run_exchange("grad_pair_exchange_in", _pair_exchange([g_in]))
    h_in = _pair_sum(core, g_in, got_in, 256)
    (slab_in,) = _run_exchange("grad_chip_exchange", _chip_exchange([h_in]))
    reds = [_chip_sum(core, s, tm) for s, tm in ((slab_in, 256), (slab_out, 128), (slab_up, 256), (slab_down, 256))]
    big_grads = dict(zip(("w_in", "w_out", "w_up", "w_down"), _pair_gather(reds)))

    return loss, gx.reshape(x.shape), big_grads, small_sum


_HBM = pl.BlockSpec(memory_space=pltpu.HBM)


D2D_CHUNKS = 8
ROW_ALIGN = 16


def _row_chunks(rows, n_chunks):
    size = min(max(rows // n_chunks, ROW_ALIGN), rows)
    assert rows % size == 0
    return [(start, size) for start in range(0, rows, size)]


def _position():
    x, y, c = lax.axis_index("x"), lax.axis_index("y"), lax.axis_index("c")
    chips = [(1 - x, y), (x, 1 - y), (1 - x, 1 - y)]
    return x, y, c, chips


def _allgather_exchange(arrs):
    n = len(arrs)

    def copies(ins, outs, send_sems, recv_sems, local_sems):
        x, y, c, chips = _position()
        me = 2 * x + y
        sibling = (x, y, 1 - c)

        def copy(a, k, src, dst, to):
            return pltpu.make_async_remote_copy(src_ref=src, dst_ref=dst, send_sem=send_sems.at[a, k],
                                                recv_sem=recv_sems.at[a, k], device_id=to, device_id_type=MESH)

        def half_rows(a, pc):
            half = ins[a].shape[0] // 2
            return pl.ds(pc * half, half)

        local = [pltpu.make_async_copy(ins[a], outs[a].at[me], local_sems.at[a]) for a in range(n)]
        ici_out = [[copy(a, k, ins[a].at[half_rows(a, c)], outs[a].at[me, half_rows(a, c)], (px, py, c))
                    for k, (px, py) in enumerate(chips)] for a in range(n)]
        return c, chips, sibling, copy, half_rows, local, ici_out

    def start(ins, outs, send_sems, recv_sems, local_sems):
        _, _, _, _, _, local, ici_out = copies(ins, outs, send_sems, recv_sems, local_sems)
        for cp in local:
            cp.start()
        for a in range(n):
            for cp in ici_out[a]:
                cp.start()

    def finish(ins, outs, send_sems, recv_sems, local_sems):
        c, chips, sibling, copy, half_rows, local, ici_out = copies(ins, outs, send_sems, recv_sems, local_sems)
        passed = []
        for a in range(n):
            half = ins[a].shape[0] // 2
            for k, (px, py) in enumerate(chips):
                blk = outs[a].at[2 * px + py, half_rows(a, c)]
                copy(a, k, blk, blk, (px, py, c)).wait_recv()
                for first, size in _row_chunks(half, D2D_CHUNKS):
                    piece = outs[a].at[2 * px + py, pl.ds(c * half + first, size)]
                    copy(a, 3 + k, piece, piece, sibling).start()
                passed.append(copy(a, 3 + k, blk, blk, sibling))
        for a in range(n):
            for k, (px, py) in enumerate(chips):
                blk = outs[a].at[2 * px + py, half_rows(a, 1 - c)]
                copy(a, 3 + k, blk, blk, sibling).wait_recv()
        for a in range(n):
            for cp in ici_out[a]:
                cp.wait_send()
        for cp in passed:
            cp.wait_send()
        for cp in local:
            cp.wait()

    return _Carried(arrs, [_sds((N_CHIPS,) + a.shape, a.dtype) for a in arrs],
                    [pltpu.SemaphoreType.DMA((n, 6)), pltpu.SemaphoreType.DMA((n, 6)), pltpu.SemaphoreType.DMA((n,))],
                    start, finish)


def _run_exchange(name, exchange):
    n_in, n_out = len(exchange.ins), len(exchange.out_shapes)

    def body(*refs):
        ins, outs, sems = refs[:n_in], refs[n_in:n_in + n_out], refs[n_in + n_out:]
        exchange.start(ins, outs, *sems)
        exchange.finish(ins, outs, *sems)

    return pl.pallas_call(
        body, name=name, out_shape=tuple(exchange.out_shapes), in_specs=[_HBM] * n_in,
        out_specs=tuple([_HBM] * n_out), scratch_shapes=exchange.sems,
    )(*exchange.ins)


def _pair_exchange(grads):
    n = len(grads)

    def copier(send_sems, recv_sems):
        x, y, c, _ = _position()

        def copy(a, src, dst):
            return pltpu.make_async_remote_copy(src_ref=src, dst_ref=dst, send_sem=send_sems.at[a],
                                                recv_sem=recv_sems.at[a], device_id=(x, y, 1 - c), device_id_type=MESH)
        return c, copy

    def start(ins, got, send_sems, recv_sems):
        c, copy = copier(send_sems, recv_sems)
        for a in range(n):
            half = ins[a].shape[1] // 2
            for slab in range(N_CHIPS):
                for first, size in _row_chunks(half, D2D_CHUNKS):
                    copy(a, ins[a].at[slab, pl.ds((1 - c) * half + first, size), :],
                         got[a].at[slab, pl.ds(first, size), :]).start()

    def finish(ins, got, send_sems, recv_sems):
        c, copy = copier(send_sems, recv_sems)
        for a in range(n):
            half = ins[a].shape[1] // 2
            copy(a, ins[a].at[:, pl.ds((1 - c) * half, half), :], got[a]).wait()

    return _Carried(grads, [_sds((N_CHIPS, g.shape[1] // 2, g.shape[2]), g.dtype) for g in grads],
                    [pltpu.SemaphoreType.DMA((n,)), pltpu.SemaphoreType.DMA((n,))], start, finish)


def _chip_exchange(hsums):
    n = len(hsums)

    def copies(ins, outs, send_sems, recv_sems, local_sems):
        x, y, c, chips = _position()
        me = 2 * x + y
        cps = []
        for a in range(n):
            cps.append(pltpu.make_async_copy(ins[a].at[me], outs[a].at[me], local_sems.at[a]))
            for k, (px, py) in enumerate(chips):
                cps.append(pltpu.make_async_remote_copy(
                    src_ref=ins[a].at[2 * px + py], dst_ref=outs[a].at[me], send_sem=send_sems.at[a, k],
                    recv_sem=recv_sems.at[a, k], device_id=(px, py, c), device_id_type=MESH))
        return cps

    def start(*refs):
        for cp in copies(*refs):
            cp.start()

    def finish(*refs):
        for cp in copies(*refs):
            cp.wait()

    return _Carried(hsums, [_sds(h.shape, h.dtype) for h in hsums],
                    [pltpu.SemaphoreType.DMA((n, 3)), pltpu.SemaphoreType.DMA((n, 3)), pltpu.SemaphoreType.DMA((n,))],
                    start, finish)


def _pair_gather(bufs):
    n = len(bufs)

    def body(*refs):
        outs = refs[n:2 * n]
        send_sems, recv_sems = refs[2 * n:]
        x, y, c, _ = _position()
        sibling = (x, y, 1 - c)

        def copy(a, rows):
            return pltpu.make_async_remote_copy(src_ref=rows, dst_ref=rows, send_sem=send_sems.at[a],
                                                recv_sem=recv_sems.at[a], device_id=sibling, device_id_type=MESH)

        for a in range(n):
            half = outs[a].shape[0] // 2
            for start, size in _row_chunks(half, 2 * D2D_CHUNKS):
                copy(a, outs[a].at[pl.ds(c * half + start, size), :]).start()
        for a in range(n):
            half = outs[a].shape[0] // 2
            copy(a, outs[a].at[pl.ds(c * half, half), :]).wait_send()
            copy(a, outs[a].at[pl.ds((1 - c) * half, half), :]).wait_recv()

    return pl.pallas_call(
        body, name="grad_pair_gather", out_shape=tuple(_sds(b.shape, b.dtype) for b in bufs),
        in_specs=[_HBM] * n, out_specs=tuple([_HBM] * n), input_output_aliases={a: a for a in range(n)},
        scratch_shapes=[pltpu.SemaphoreType.DMA((n,)), pltpu.SemaphoreType.DMA((n,))],
    )(*bufs)


def _device_gather_exchange(packed):
    def copies(ins, outs, send_sems, recv_sems, local_sem):
        (x_ref,), (all_ref,) = ins, outs
        x, y, c, chips = _position()
        me, sibling = (x, y, c), (x, y, 1 - c)

        def slab(px, py, pc):
            return all_ref.at[4 * px + 2 * py + pc]

        def copy(k, block, to, src=None):
            return pltpu.make_async_remote_copy(
                src_ref=slab(*block) if src is None else src, dst_ref=slab(*block), send_sem=send_sems.at[k],
                recv_sem=recv_sems.at[k], device_id=to, device_id_type=MESH)

        mine = pltpu.make_async_copy(x_ref, slab(*me), local_sem)
        first = [copy(0, me, sibling, src=x_ref)]
        first += [copy(1 + j, me, (*chip, c), src=x_ref) for j, chip in enumerate(chips)]
        passed = [copy(4 + j, (*chip, c), sibling) for j, chip in enumerate(chips)]
        return c, chips, me, sibling, copy, mine, first, passed

    def start(ins, outs, send_sems, recv_sems, local_sem):
        _, _, _, _, _, mine, first, _ = copies(ins, outs, send_sems, recv_sems, local_sem)
        mine.start()
        for cp in first:
            cp.start()

    def finish(ins, outs, send_sems, recv_sems, local_sem):
        c, chips, me, sibling, copy, mine, first, passed = copies(ins, outs, send_sems, recv_sems, local_sem)
        for j, chip in enumerate(chips):
            copy(1 + j, (*chip, c), me).wait_recv()
            passed[j].start()
        copy(0, sibling, me).wait_recv()
        for j, chip in enumerate(chips):
            copy(4 + j, (*chip, 1 - c), me).wait_recv()
        for cp in first + passed:
            cp.wait_send()
        mine.wait()

    return _Carried([packed], [_sds((N_DEV,) + packed.shape, F32)],
                    [pltpu.SemaphoreType.DMA((7,)), pltpu.SemaphoreType.DMA((7,)), pltpu.SemaphoreType.DMA],
                    start, finish)


def _ordered_sum(name, slabs):
    _, m_per, n_cols = slabs.shape

    def body(s_ref, o_ref):
        acc = s_ref[0]
        for d in range(1, N_DEV):
            acc = acc + s_ref[d]
        o_ref[...] = acc

    vmem = pl.BlockSpec(memory_space=pltpu.VMEM)
    return pl.pallas_call(body, name=name, out_shape=_sds((m_per, n_cols), F32), in_specs=[vmem], out_specs=vmem)(slabs)


def _pair_sum(core, own, got, tm):
    _, half, cols = got.shape
    nb = half // tm

    def body(c_ref, a_ref, b_ref, o_ref):
        o_ref[...] = (a_ref[...] + b_ref[...]).astype(BF16)

    return pl.pallas_call(
        body, name="grad_pair_sum", out_shape=_sds(got.shape, BF16),
        grid_spec=pltpu.PrefetchScalarGridSpec(
            num_scalar_prefetch=1, grid=(N_CHIPS, nb),
            in_specs=[pl.BlockSpec((None, tm, cols), lambda s, i, c_ref: (s, c_ref[0] * nb + i, 0)),
                      pl.BlockSpec((None, tm, cols), lambda s, i, c_ref: (s, i, 0))],
            out_specs=pl.BlockSpec((None, tm, cols), lambda s, i, c_ref: (s, i, 0))),
        compiler_params=_cparams(2),
    )(core, own, got)


def _chip_sum(core, slabs, tm):
    _, half, cols = slabs.shape
    nb = half // tm

    def body(c_ref, s_ref, o_ref):
        acc = s_ref[0].astype(F32)
        for k in range(1, N_CHIPS):
            acc = acc + s_ref[k].astype(F32)
        o_ref[...] = acc

    return pl.pallas_call(
        body, name="grad_chip_sum", out_shape=_sds((2 * half, cols), F32),
        grid_spec=pltpu.PrefetchScalarGridSpec(
            num_scalar_prefetch=1, grid=(nb,),
            in_specs=[pl.BlockSpec((N_CHIPS, tm, cols), lambda i, c_ref: (0, i, 0))],
            out_specs=pl.BlockSpec((tm, cols), lambda i, c_ref: (c_ref[0] * nb + i, 0))),
        compiler_params=_cparams(1),
    )(core, slabs)


def _adam_math(w, g, m, v):
    m2 = ADAM_B1 * m + (1.0 - ADAM_B1) * g
    v2 = ADAM_B2 * v + (1.0 - ADAM_B2) * (g * g)
    m_hat = m2 / (1.0 - ADAM_B1 ** ADAM_STEP)
    v_hat = v2 / (1.0 - ADAM_B2 ** ADAM_STEP)
    delta = -ADAM_LR * (m_hat / (jnp.sqrt(v_hat) + ADAM_EPS) + ADAM_WD * w)
    return delta, m2, v2


def _adamw(name, w, g, m, v, tm):
    def body(w_ref, g_ref, m_ref, v_ref, gout_ref, d_ref, m2_ref, v2_ref):
        gv = g_ref[...]
        d, m2, v2 = _adam_math(w_ref[...], gv, m_ref[...], v_ref[...])
        gout_ref[...] = gv
        d_ref[...] = d
        m2_ref[...] = m2
        v2_ref[...] = v2

    return _rows_call(name, body, tm, [w, g, m, v], [], [_sds(w.shape, F32)] * 4)


_SMALL_NAMES = ("norm_mix_pre", "gm_ln_w", "gm_ln_b", "gm_w_s", "gm_b_s", "conv_w", "conv_b", "dt_bias", "a_log",
                "d_skip", "ssm_norm_w", "norm_mix_post", "norm_ffn_pre", "norm_ffn_post")
_PACK_COLS = 1024


def _pack(parts, names=_SMALL_NAMES, tail=None):
    pieces = [parts[n].reshape(-1) for n in names]
    flat = jnp.concatenate(pieces if tail is None else pieces + [tail])
    rows = -(-flat.shape[0] // (8 * _PACK_COLS)) * 8
    flat = jnp.pad(flat, (0, rows * _PACK_COLS - flat.shape[0]))
    return flat.reshape(rows, _PACK_COLS)


def _unpack(packed, shapes, names=_SMALL_NAMES):
    flat = packed.reshape(-1)
    out, off = {}, 0
    for n in names:
        size = 1
        for s in shapes[n]:
            size *= s
        out[n] = flat[off:off + size].reshape(shapes[n])
        off += size
    return out


def kernel(x, norm_mix_pre, w_in, gm_ln_w, gm_ln_b, gm_w_s, gm_b_s, conv_w, conv_b, dt_bias, a_log, d_skip, ssm_norm_w, w_out, norm_mix_post, norm_ffn_pre, w_up, w_down, norm_ffn_post, loss_target, m_norm_mix_pre, m_w_in, m_gm_ln_w, m_gm_ln_b, m_gm_w_s, m_gm_b_s, m_conv_w, m_conv_b, m_dt_bias, m_a_log, m_d_skip, m_ssm_norm_w, m_w_out, m_norm_mix_post, m_norm_ffn_pre, m_w_up, m_w_down, m_norm_ffn_post, v_norm_mix_pre, v_w_in, v_gm_ln_w, v_gm_ln_b, v_gm_w_s, v_gm_b_s, v_conv_w, v_conv_b, v_dt_bias, v_a_log, v_d_skip, v_ssm_norm_w, v_w_out, v_norm_mix_post, v_norm_ffn_pre, v_w_up, v_w_down, v_norm_ffn_post):
    params = dict(norm_mix_pre=norm_mix_pre, w_in=w_in, gm_ln_w=gm_ln_w, gm_ln_b=gm_ln_b, gm_w_s=gm_w_s, gm_b_s=gm_b_s,
                  conv_w=conv_w, conv_b=conv_b, dt_bias=dt_bias, a_log=a_log, d_skip=d_skip, ssm_norm_w=ssm_norm_w,
                  w_out=w_out, norm_mix_post=norm_mix_post, norm_ffn_pre=norm_ffn_pre, w_up=w_up, w_down=w_down,
                  norm_ffn_post=norm_ffn_post)
    mom1 = dict(norm_mix_pre=m_norm_mix_pre, w_in=m_w_in, gm_ln_w=m_gm_ln_w, gm_ln_b=m_gm_ln_b, gm_w_s=m_gm_w_s,
                gm_b_s=m_gm_b_s, conv_w=m_conv_w, conv_b=m_conv_b, dt_bias=m_dt_bias, a_log=m_a_log, d_skip=m_d_skip,
                ssm_norm_w=m_ssm_norm_w, w_out=m_w_out, norm_mix_post=m_norm_mix_post, norm_ffn_pre=m_norm_ffn_pre,
                w_up=m_w_up, w_down=m_w_down, norm_ffn_post=m_norm_ffn_post)
    mom2 = dict(norm_mix_pre=v_norm_mix_pre, w_in=v_w_in, gm_ln_w=v_gm_ln_w, gm_ln_b=v_gm_ln_b, gm_w_s=v_gm_w_s,
                gm_b_s=v_gm_b_s, conv_w=v_conv_w, conv_b=v_conv_b, dt_bias=v_dt_bias, a_log=v_a_log, d_skip=v_d_skip,
                ssm_norm_w=v_ssm_norm_w, w_out=v_w_out, norm_mix_post=v_norm_mix_post, norm_ffn_pre=v_norm_ffn_pre,
                w_up=v_w_up, w_down=v_w_down, norm_ffn_post=v_norm_ffn_post)
    names = list(params)
    big = ("w_in", "w_out", "w_up", "w_down")
    chip = 2 * lax.axis_index("x") + lax.axis_index("y")

    shards = {n: params[n][0].astype(BF16) for n in big}
    conv_shard = jnp.pad(conv_w[0], ((0, 16 - CONV_K), (0, 0)))
    g_in4, g_conv4 = _run_exchange("allgather_w_in", _allgather_exchange([shards["w_in"], conv_shard]))
    conv_full = jnp.transpose(g_conv4[:, :CONV_K, :], (1, 0, 2)).reshape(CONV_K, CONV_CH)

    small = {n: params[n][0] if params[n].ndim >= 3 else params[n] for n in _SMALL_NAMES if n != "conv_w"}
    core = lax.axis_index("c").astype(jnp.int32).reshape(1)
    loss, grad_x, big_grads, small_sum = _forward_backward(
        x, loss_target, g_in4, conv_full, small, shards["w_out"], shards["w_up"], shards["w_down"], core)

    small_sum["conv_w"] = lax.dynamic_slice_in_dim(small_sum["conv_w"], chip * (CONV_CH // N_CHIPS), CONV_CH // N_CHIPS, axis=1)

    grads, delta, new_m, new_v = {}, {}, {}, {}
    for n, tm in zip(big, (256, 128, 256, 256)):
        g, d, m2, v2 = _adamw("adamw_" + n, params[n][0], big_grads[n], mom1[n][0], mom2[n][0], tm)
        grads[n], delta[n], new_m[n], new_v[n] = g[None], d[None], m2[None], v2[None]
    local_shapes = {n: params[n].shape[1:] if params[n].ndim >= 3 else params[n].shape for n in _SMALL_NAMES}
    flat = lambda tree: {n: tree[n].reshape(local_shapes[n]) for n in _SMALL_NAMES}
    packed = [_pack(flat(t)) for t in (params, small_sum, mom1, mom2)]
    _, d_p, m_p, v_p = _adamw("adamw_small", *packed, packed[0].shape[0])
    for src, dst in ((d_p, delta), (m_p, new_m), (v_p, new_v)):
        for n, val in _unpack(src, local_shapes).items():
            dst[n] = val.reshape(params[n].shape)
    for n in _SMALL_NAMES:
        grads[n] = small_sum[n].reshape(params[n].shape)

    out = [loss, grad_x]
    for tree in (grads, delta, new_m, new_v):
        out += [tree[n] for n in names]
    return tuple(out)
```

```python
import functools

import jax
import jax.numpy as jnp
from jax import lax
from jax.experimental import pallas as pl
from jax.experimental.pallas import tpu as pltpu

F32 = jnp.float32
BF16 = jnp.bfloat16
HI = lax.Precision.HIGHEST
MESH = pl.DeviceIdType.MESH

EPS = 1e-6
D_MODEL = 1024
GM_WIDTH = 512
SSM_WIDTH = 512
N_HEADS = 8
HEAD_DIM = 64
CHUNK = 128
SSM_GROUPS = 2
GROUP_W = SSM_WIDTH // SSM_GROUPS
SSM_STATE = 128
CONV_K = 4
CONV_CH = 1024
D_FF = 4096
IN_COLS = 2568
DT_PAD = 128
N_CHIPS = 4
N_DEV = 8

ADAM_LR = 0.001
ADAM_B1 = 0.9
ADAM_B2 = 0.999
ADAM_EPS = 1e-08
ADAM_WD = 0.01
ADAM_STEP = 10

VMEM_LIMIT_BYTES = 56 * 1024 * 1024
FF_TILE = 512
DW_TOKENS_PER_STEP = 2048


def _cparams(n_axes):
    return pltpu.CompilerParams(dimension_semantics=("arbitrary",) * n_axes, vmem_limit_bytes=VMEM_LIMIT_BYTES)


def _dot(a, b):
    return jnp.dot(a.astype(BF16), b.astype(BF16), preferred_element_type=F32)


def _dot_nt(a, b):
    return lax.dot_general(a.astype(BF16), b.astype(BF16), (((1,), (1,)), ((), ())), preferred_element_type=F32)


def _dot_tn(a, b):
    return lax.dot_general(a.astype(BF16), b.astype(BF16), (((0,), (0,)), ((), ())), preferred_element_type=F32)


def _sigmoid(x):
    return 1.0 / (1.0 + jnp.exp(-x))


_GELU_C = 0.7978845608028654
_GELU_A = 0.044715


def _gelu(x):
    t = jnp.tanh(_GELU_C * (x + _GELU_A * (x * x * x)))
    return 0.5 * x * (1.0 + t), t


def _gelu_grad(x, t):
    return 0.5 * (1.0 + t) + 0.5 * x * (1.0 - t * t) * (_GELU_C * (1.0 + 3.0 * _GELU_A * x * x))


def _rms_fwd(x, w):
    r = lax.rsqrt(jnp.mean(x * x, axis=-1, keepdims=True) + EPS)
    return x * r * w, r


def _rms_bwd(x, r, w, dy):
    g = dy * w
    dx = r * g - x * (r * r * r) * jnp.mean(g * x, axis=-1, keepdims=True)
    dw = jnp.sum(dy * x * r, axis=0, keepdims=True)
    return dx, dw


class _Carried:
    def __init__(self, ins, out_shapes, sems, start, finish, relay=None):
        self.ins, self.out_shapes, self.sems = list(ins), list(out_shapes), list(sems)
        self.start, self.finish, self.relay = start, finish, relay


def _both(first, second):
    n_i, n_o, n_s = len(first.ins), len(first.out_shapes), len(first.sems)

    def split(ins, outs, sems):
        return (ins[:n_i], outs[:n_o], sems[:n_s]), (ins[n_i:], outs[n_o:], sems[n_s:])

    def start(ins, outs, *sems):
        (i1, o1, s1), (i2, o2, s2) = split(ins, outs, sems)
        first.start(i1, o1, *s1)
        second.start(i2, o2, *s2)

    def relay(ins, outs, *sems):
        (i1, o1, s1), (i2, o2, s2) = split(ins, outs, sems)
        if first.relay is not None:
            first.relay(i1, o1, *s1)
        if second.relay is not None:
            second.relay(i2, o2, *s2)

    def finish(ins, outs, *sems):
        (i1, o1, s1), (i2, o2, s2) = split(ins, outs, sems)
        first.finish(i1, o1, *s1)
        second.finish(i2, o2, *s2)

    has_relay = first.relay is not None or second.relay is not None
    return _Carried(first.ins + second.ins, first.out_shapes + second.out_shapes, first.sems + second.sems, start, finish,
                    relay if has_relay else None)


def _split_carried(refs, n_in, n_out, n_scratch, carried):
    n_ci, n_co, n_cs = len(carried.ins), len(carried.out_shapes), len(carried.sems)
    ins, rest = refs[:n_in], refs[n_in:]
    c_ins, rest = rest[:n_ci], rest[n_ci:]
    outs, rest = rest[:n_out], rest[n_out:]
    c_outs, rest = rest[:n_co], rest[n_co:]
    scr, c_sems = rest[:n_scratch], rest[n_scratch:]
    assert len(c_sems) == n_cs
    return tuple(ins) + tuple(outs) + tuple(scr), c_ins, c_outs, c_sems


def _rows_call(name, body, tm, row_ins, const_ins, row_outs, acc_outs=(), scratch=(), carried=None):
    n_rows = row_ins[0].shape[0]
    assert n_rows % tm == 0
    n_steps = n_rows // tm
    n_in = len(row_ins) + len(const_ins)
    n_ro = len(row_outs)
    n_acc = len(acc_outs)

    def kern(*refs):
        accs = refs[n_in + n_ro:n_in + n_ro + n_acc]

        @pl.when(pl.program_id(0) == 0)
        def _():
            for a in accs:
                a[...] = jnp.zeros_like(a)

        body(*refs)

    def whole(shape):
        nd = len(shape)
        return pl.BlockSpec(tuple(shape), lambda i: (0,) * nd)

    in_specs = [pl.BlockSpec((tm, a.shape[1]), lambda i: (i, 0)) for a in row_ins]
    in_specs += [whole(a.shape) for a in const_ins]
    out_specs = [pl.BlockSpec((tm, s.shape[1]), lambda i: (i, 0)) for s in row_outs]
    out_specs += [whole(s.shape) for s in acc_outs]
    return _call_carrying(
        kern, carried, name=name, grid=(n_steps,), in_specs=in_specs, out_specs=out_specs,
        out_shape=tuple(row_outs) + tuple(acc_outs), scratch_shapes=list(scratch), operands=list(row_ins) + list(const_ins))


def _call_carrying(body, carried, *, name, grid, in_specs, out_specs, out_shape, scratch_shapes, operands):
    n_in, n_out, n_scratch = len(in_specs), len(out_specs), len(scratch_shapes)
    kern = body
    if carried is not None:
        def kern(*refs):
            plain, c_ins, c_outs, c_sems = _split_carried(refs, n_in, n_out, n_scratch, carried)
            first, last, flat, total = True, True, 0, 1
            for d, size in enumerate(grid):
                first = jnp.logical_and(first, pl.program_id(d) == 0)
                last = jnp.logical_and(last, pl.program_id(d) == size - 1)
                flat = flat * size + pl.program_id(d)
                total *= size

            @pl.when(first)
            def _():
                carried.start(c_ins, c_outs, *c_sems)

            if carried.relay is not None:
                @pl.when(flat == min(total * RELAY_AT_NUM // RELAY_AT_DEN, total - 1))
                def _():
                    carried.relay(c_ins, c_outs, *c_sems)

            body(*plain)

            @pl.when(last)
            def _():
                carried.finish(c_ins, c_outs, *c_sems)

        in_specs = list(in_specs) + [_HBM] * len(carried.ins)
        out_specs = list(out_specs) + [_HBM] * len(carried.out_shapes)
        out_shape = tuple(out_shape) + tuple(carried.out_shapes)
        operands = list(operands) + carried.ins
        scratch_shapes = list(scratch_shapes) + carried.sems
    return pl.pallas_call(
        kern, name=name, grid=grid, in_specs=in_specs, out_specs=out_specs, out_shape=out_shape,
        scratch_shapes=scratch_shapes, compiler_params=_cparams(len(grid)),
    )(*operands)


def _sds(shape, dtype):
    return jax.ShapeDtypeStruct(tuple(shape), dtype)


def _matmul_tn(name, a, b, tm, tn, tk, stacked=False, carried=None):
    k_dim, m_dim = a.shape
    n_dim = b.shape[1]
    assert m_dim % tm == 0 and n_dim % tn == 0 and k_dim % tk == 0

    def kern(a_ref, b_ref, o_ref):
        @pl.when(pl.program_id(2) == 0)
        def _():
            o_ref[...] = jnp.zeros_like(o_ref)

        o_ref[...] += _dot_tn(a_ref[...], b_ref[...])

    if stacked:
        assert tm == m_dim
        out_shape = _sds((n_dim // tn, m_dim, tn), F32)
        out_spec = pl.BlockSpec((None, tm, tn), lambda i, j, k: (j, i, 0))
    else:
        out_shape = _sds((m_dim, n_dim), F32)
        out_spec = pl.BlockSpec((tm, tn), lambda i, j, k: (i, j))
    outs = _call_carrying(
        kern, carried, name=name, grid=(m_dim // tm, n_dim // tn, k_dim // tk),
        in_specs=[pl.BlockSpec((tk, tm), lambda i, j, k: (k, i)), pl.BlockSpec((tk, tn), lambda i, j, k: (k, j))],
        out_specs=[out_spec], out_shape=(out_shape,), scratch_shapes=[], operands=[a, b])
    return outs[0] if carried is None else outs


def _inproj_fwd(x, nw, w_uv, w_xbc, w_z, w_dt, tm=256, carried=None):
    n_tok = x.shape[0]

    def body(x_ref, nw_ref, wuv_ref, wxbc_ref, wz_ref, wdt_ref, puv_ref, pxbc_ref, pz_ref, pdt_ref):
        h, _ = _rms_fwd(x_ref[...], nw_ref[...])
        h = h.astype(BF16)
        puv_ref[...] = jnp.dot(h, wuv_ref[...], preferred_element_type=F32)
        pxbc_ref[...] = jnp.dot(h, wxbc_ref[...], preferred_element_type=F32)
        pz_ref[...] = jnp.dot(h, wz_ref[...], preferred_element_type=F32)
        pdt_ref[...] = jnp.dot(h, wdt_ref[...], preferred_element_type=F32)

    return _rows_call(
        "inproj_fwd", body, tm, [x], [nw, w_uv, w_xbc, w_z, w_dt],
        [_sds((n_tok, 2 * GM_WIDTH), F32), _sds((n_tok, CONV_CH), F32), _sds((n_tok, SSM_WIDTH), F32),
         _sds((n_tok, DT_PAD), F32)], carried=carried)


def _head_lane_mask(width, head):
    lane = lax.broadcasted_iota(jnp.int32, (1, width), 1)
    return (lane // HEAD_DIM) == head


def _split_terms(x, terms):
    parts = []
    for _ in range(terms):
        p = x.astype(BF16)
        parts.append(p)
        x = x - p.astype(F32)
    return parts


def _seg_dots(vals, ind, terms=2):
    m = vals[0].shape[0]
    parts = []
    for v in vals:
        parts += _split_terms(v, terms)
    red = jnp.dot(jnp.concatenate(parts, axis=0), ind, preferred_element_type=F32)
    outs = []
    for i in range(len(vals)):
        acc = red[i * terms * m:(i * terms + 1) * m]
        for t in range(1, terms):
            acc = acc + red[(i * terms + t) * m:(i * terms + t + 1) * m]
        outs.append(acc)
    return outs


def _tri_dot(mask, x, terms=3):
    n = x.shape[1]
    red = jnp.dot(mask.astype(BF16), jnp.concatenate(_split_terms(x, terms), axis=1), preferred_element_type=F32)
    acc = red[:, :n]
    for t in range(1, terms):
        acc = acc + red[:, t * n:(t + 1) * n]
    return acc


def _gmlp_common(puv, lnw, lnb, e_bf, et_bf):
    u = puv[:, :GM_WIDTH]
    v = puv[:, GM_WIDTH:]
    gu, tu = _gelu(u)
    gv, tv = _gelu(v)
    (s1,) = _seg_dots([gv], et_bf)
    (mu,) = _seg_dots([s1 * (1.0 / HEAD_DIM)], e_bf)
    xc = gv - mu
    (s2,) = _seg_dots([xc * xc], et_bf)
    (rstd,) = _seg_dots([lax.rsqrt(s2 * (1.0 / HEAD_DIM) + EPS)], e_bf)
    xhat = xc * rstd
    vn = xhat * lnw + lnb
    return u, v, gu, tu, tv, rstd, xhat, vn


def _tril_mask():
    r = lax.broadcasted_iota(jnp.int32, (CHUNK, CHUNK), 0)
    c = lax.broadcasted_iota(jnp.int32, (CHUNK, CHUNK), 1)
    return r >= c


def _head_blocks(v):
    return jnp.concatenate([jnp.where(_head_lane_mask(GM_WIDTH, h), v, jnp.zeros_like(v)) for h in range(N_HEADS)], axis=0)


def _causal_w_cat(w_cat):
    t = lax.broadcasted_iota(jnp.int32, (CHUNK, N_HEADS * CHUNK), 0)
    s = lax.broadcasted_iota(jnp.int32, (CHUNK, N_HEADS * CHUNK), 1) % CHUNK
    return jnp.where(t >= s, w_cat, 0.0).astype(BF16)


def _gmlp_chunk_fwd(puv, lnw, lnb, e_bf, et_bf, wm, bmap):
    _, _, gu, _, _, _, _, vn = _gmlp_common(puv, lnw, lnb, e_bf, et_bf)
    mixed = jnp.dot(wm, _head_blocks(vn.astype(BF16)), preferred_element_type=F32) + bmap
    return (gu * mixed).astype(BF16)


SUBLANES = 8


def _shift_down(x, tail, s):
    main = pltpu.roll(x, s, 0)
    row = lax.broadcasted_iota(jnp.int32, (SUBLANES, 1), 0)
    head = jnp.where(row < s, pltpu.roll(tail, s, 0), main[:SUBLANES])
    return jnp.concatenate([head, main[SUBLANES:]], axis=0)


def _shift_up(x, head_next, s):
    n = x.shape[0]
    main = pltpu.roll(x, n - s, 0)
    row = lax.broadcasted_iota(jnp.int32, (SUBLANES, 1), 0)
    last = jnp.where(row >= SUBLANES - s, pltpu.roll(head_next, SUBLANES - s, 0), main[n - SUBLANES:])
    return jnp.concatenate([main[:n - SUBLANES], last], axis=0)


def _ssd_pre(xr, tail, cw_ref, cb, pdt, dtb, alog, emap):
    rowi = lax.broadcasted_iota(jnp.int32, (CHUNK, 1), 0)
    shifted = [_shift_down(xr, tail, 3), _shift_down(xr, tail, 2), _shift_down(xr, tail, 1), xr]
    xc = cb
    for k in range(CONV_K):
        xc = xc + cw_ref[k] * shifted[k]
    sg = _sigmoid(xc)
    xa = xc * sg
    pre = pdt + dtb
    dt = jnp.maximum(pre, 0.0) + jnp.log(1.0 + jnp.exp(-jnp.abs(pre)))
    a_neg = -jnp.exp(alog)
    a_cs = _tri_dot(_tril_mask(), dt * a_neg)
    acs_map, dt_map = _seg_dots([a_cs, dt], emap, terms=3)
    return dict(shifted=shifted, xc=xc, sg=sg, xa=xa, pre=pre, dt=dt, a_neg=a_neg, a_cs=a_cs,
                acs_map=acs_map, dt_map=dt_map, rowi=rowi)


def _ssd_maps(p):
    last = p["rowi"] == CHUNK - 1
    aq_map = jnp.sum(jnp.where(last, p["acs_map"], 0.0), axis=0, keepdims=True)
    e_exp = jnp.exp(p["acs_map"])
    dte = jnp.exp(aq_map - p["acs_map"])
    cd = jnp.exp(aq_map)
    return last, e_exp, dte, cd


def _head_decay(a_cs, a_cs_t, head, tri):
    lane = lax.broadcasted_iota(jnp.int32, (1, DT_PAD), 1)
    sub = lax.broadcasted_iota(jnp.int32, (DT_PAD, 1), 0)
    col = jnp.sum(jnp.where(lane == head, a_cs, 0.0), axis=1, keepdims=True)
    row = jnp.sum(jnp.where(sub == head, a_cs_t, 0.0), axis=0, keepdims=True)
    return jnp.exp(jnp.where(tri, col - row, -1e30))


def _gate_fwd(y, z, nw):
    sz = _sigmoid(z)
    zg = z * sz
    yg = y * zg
    outs, rs = [], []
    for g in range(SSM_GROUPS):
        gs = slice(g * GROUP_W, (g + 1) * GROUP_W)
        o, r = _rms_fwd(yg[:, gs], nw[:, gs])
        outs.append(o)
        rs.append(r)
    return sz, zg, yg, outs, rs


def _ssd_const_specs():
    def whole(shape):
        nd = len(shape)
        return pl.BlockSpec(tuple(shape), lambda c: (0,) * nd)
    return [whole((CONV_K, 1, CONV_CH)), whole((1, CONV_CH)), whole((1, DT_PAD)), whole((1, DT_PAD)),
            whole((1, SSM_WIDTH)), whole((1, SSM_WIDTH)), whole((DT_PAD, SSM_WIDTH)), whole((SSM_WIDTH, DT_PAD))]


def _mixer_fwd(p_uv, p_xbc, p_z, p_dt, lnw, lnb, w_cat, bmap, conv_w, conv_b, dt_bias, a_log, dskip_map, norm_w,
               e_bf, et_bf, n_seq, carried=None):
    n_tok = p_xbc.shape[0]
    nc = n_tok // n_seq // CHUNK

    def body(puv3, xr3, z3, pdt3, lnw_ref, lnb_ref, wcat_ref, bmap_ref,
             cw_ref, cb_ref, dtb_ref, alog_ref, dsk_ref, nw_ref, e_ref, et_ref,
             ya3, yb3, yssd3, sprev3, wm_scr, prev3_scr, s3_scr):
        @pl.when(pl.program_id(0) == 0)
        def _():
            wm_scr[...] = _causal_w_cat(wcat_ref[...])
            prev3_scr[...] = jnp.zeros_like(prev3_scr)
            s3_scr[...] = jnp.zeros_like(s3_scr)

        for b in range(n_seq):
            one_sequence(puv3.at[b], xr3.at[b], z3.at[b], pdt3.at[b], lnw_ref, lnb_ref, bmap_ref,
                         cw_ref, cb_ref, dtb_ref, alog_ref, dsk_ref, nw_ref, e_ref, et_ref,
                         ya3.at[b], yb3.at[b], yssd3.at[b], sprev3.at[b], wm_scr, prev3_scr.at[b], s3_scr.at[b])

    def one_sequence(puv_ref, xr_ref, z_ref, pdt_ref, lnw_ref, lnb_ref, bmap_ref,
                     cw_ref, cb_ref, dtb_ref, alog_ref, dsk_ref, nw_ref, e_ref, et_ref,
                     ya_ref, yb_ref, yssd_ref, sprev_ref, wm_scr, prev_scr, s_scr):
        ya_ref[...] = _gmlp_chunk_fwd(puv_ref[...], lnw_ref[...], lnb_ref[...], e_ref[...], et_ref[...], wm_scr[...],
                                      bmap_ref[...])
        xr = xr_ref[...]
        p = _ssd_pre(xr, prev_scr[...], cw_ref, cb_ref[...], pdt_ref[...], dtb_ref[...], alog_ref[...], e_ref[...])
        _, e_exp, dte, cd = _ssd_maps(p)
        xs = p["xa"][:, :SSM_WIDTH]
        xd = xs * p["dt_map"]
        a_cs_t = p["a_cs"].T
        tri = _tril_mask()
        s_old = s_scr[...]
        sprev_ref[...] = s_old
        for g in range(SSM_GROUPS):
            gs = slice(g * GROUP_W, (g + 1) * GROUP_W)
            bm = p["xa"][:, SSM_WIDTH + g * SSM_STATE: SSM_WIDTH + (g + 1) * SSM_STATE].astype(BF16)
            cm = p["xa"][:, SSM_WIDTH + (SSM_GROUPS + g) * SSM_STATE: SSM_WIDTH + (SSM_GROUPS + g + 1) * SSM_STATE].astype(BF16)
            cb_mat = _dot_nt(cm, bm)
            xdg = xd[:, gs].astype(BF16)
            y_g = _dot(cm, s_old[:, gs]) * e_exp[:, gs] + dsk_ref[:, gs] * xs[:, gs]
            for r in range(SSM_GROUPS * 2):
                dm = _head_decay(p["a_cs"], a_cs_t, g * 4 + r, tri)
                full = jnp.dot((cb_mat * dm).astype(BF16), xdg, preferred_element_type=F32)
                y_g = y_g + jnp.where(_head_lane_mask(GROUP_W, r), full, 0.0)
            yssd_ref[:, gs] = y_g
            s_scr[:, gs] = cd[:, gs] * s_old[:, gs] + _dot_tn(bm, xd[:, gs] * dte[:, gs])
        _, _, _, outs, _ = _gate_fwd(yssd_ref[...], z_ref[...], nw_ref[...])
        for g in range(SSM_GROUPS):
            yb_ref[:, g * GROUP_W:(g + 1) * GROUP_W] = outs[g].astype(BF16)
        prev_scr[...] = xr[CHUNK - SUBLANES:, :]

    seq_len = n_tok // n_seq

    def rows(width):
        return pl.BlockSpec((n_seq, CHUNK, width), lambda c: (0, c, 0))

    def whole(shape):
        nd = len(shape)
        return pl.BlockSpec(tuple(shape), lambda c: (0,) * nd)

    def by_seq(a):
        return a.reshape(n_seq, seq_len, a.shape[-1])

    outs = _call_carrying(
        body, carried, name="mixer_fwd", grid=(nc,),
        in_specs=[rows(2 * GM_WIDTH), rows(CONV_CH), rows(SSM_WIDTH), rows(DT_PAD), whole(lnw.shape), whole(lnb.shape),
                  whole(w_cat.shape), whole(bmap.shape)] + _ssd_const_specs(),
        out_specs=[rows(GM_WIDTH), rows(SSM_WIDTH), rows(SSM_WIDTH), rows(SSM_WIDTH)],
        out_shape=(_sds((n_seq, seq_len, GM_WIDTH), BF16), _sds((n_seq, seq_len, SSM_WIDTH), BF16),
                   _sds((n_seq, seq_len, SSM_WIDTH), F32), _sds((n_seq, seq_len, SSM_WIDTH), F32)),
        scratch_shapes=[pltpu.VMEM((CHUNK, N_HEADS * CHUNK), BF16), pltpu.VMEM((n_seq, SUBLANES, CONV_CH), F32),
                        pltpu.VMEM((n_seq, SSM_STATE, SSM_WIDTH), F32)],
        operands=[by_seq(p_uv), by_seq(p_xbc), by_seq(p_z), by_seq(p_dt), lnw, lnb, w_cat, bmap, conv_w, conv_b, dt_bias,
                  a_log, dskip_map, norm_w, e_bf, et_bf])
    return tuple(o.reshape(n_tok, o.shape[-1]) for o in outs[:4]) + tuple(outs[4:])


def _outproj_fwd(ya, yb, x, w_out, nw_post, nw_pre2, tm=256):
    n_tok = x.shape[0]

    def body(ya_ref, yb_ref, x_ref, wo_ref, nwa_ref, nwb_ref, o_ref, x1_ref, h2_ref):
        o = jnp.dot(ya_ref[...], wo_ref[:GM_WIDTH, :], preferred_element_type=F32)
        o = o + jnp.dot(yb_ref[...], wo_ref[GM_WIDTH:, :], preferred_element_type=F32)
        on, _ = _rms_fwd(o, nwa_ref[...])
        x1 = x_ref[...] + on
        h2, _ = _rms_fwd(x1, nwb_ref[...])
        o_ref[...] = o
        x1_ref[...] = x1
        h2_ref[...] = h2.astype(BF16)

    return _rows_call("outproj_fwd", body, tm, [ya, yb, x], [w_out, nw_post, nw_pre2],
                      [_sds((n_tok, D_MODEL), F32), _sds((n_tok, D_MODEL), F32), _sds((n_tok, D_MODEL), BF16)])


def _up_cols(wup_ref, j):
    per = (D_FF // N_CHIPS) // FF_TILE
    return wup_ref[j // per, :, (j % per) * FF_TILE:(j % per + 1) * FF_TILE]


def _down_rows(wda_ref, wdb_ref, j):
    assert 2 * FF_TILE == D_FF // N_CHIPS
    return (wda_ref if j % 2 == 0 else wdb_ref)[j // 2]


def _skewed_rows_call(name, main, tail, tm, lead_ins, lag_ins, const_ins, lead_outs, lag_outs, acc_outs, carry):
    n_rows = lead_ins[0].shape[0]
    assert n_rows % tm == 0
    n = n_rows // tm
    counts = [len(lead_ins), len(lag_ins), len(const_ins), len(lead_outs), len(lag_outs), len(acc_outs)]

    def kern(*refs):
        groups, pos = [], 0
        for cnt in counts:
            groups.append(refs[pos:pos + cnt])
            pos += cnt
        lead_i, lag_i, consts, lead_o, lag_o, accs = groups
        carry_scr = refs[pos]
        i = pl.program_id(0)

        @pl.when(i == 0)
        def _():
            for a in accs:
                a[...] = jnp.zeros_like(a)
            carry_scr[...] = main(lead_i, consts, lead_o)

        @pl.when(jnp.logical_and(i > 0, i < n))
        def _():
            previous = carry_scr[...]
            carry_scr[...] = main(lead_i, consts, lead_o)
            tail(previous, lag_i, consts, lag_o, accs)

        @pl.when(i == n)
        def _():
            tail(carry_scr[...], lag_i, consts, lag_o, accs)

    def lead(width):
        return pl.BlockSpec((tm, width), lambda i: (jnp.minimum(i, n - 1), 0))

    def lag(width):
        return pl.BlockSpec((tm, width), lambda i: (jnp.maximum(i - 1, 0), 0))

    def whole(shape, **kw):
        nd = len(shape)
        return pl.BlockSpec(tuple(shape), lambda i: (0,) * nd, **kw)

    const_specs = [whole(a.shape, pipeline_mode=pl.Buffered(1)) for a in const_ins]
    return pl.pallas_call(
        kern, name=name, grid=(n + 1,),
        in_specs=[lead(a.shape[1]) for a in lead_ins] + [lag(a.shape[1]) for a in lag_ins] + const_specs,
        out_specs=[lead(s.shape[1]) for s in lead_outs] + [lag(s.shape[1]) for s in lag_outs] + [whole(s.shape) for s in acc_outs],
        out_shape=tuple(lead_outs) + tuple(lag_outs) + tuple(acc_outs),
        scratch_shapes=[pltpu.VMEM(carry, F32)], compiler_params=_cparams(1),
    )(*lead_ins, *lag_ins, *const_ins)


def _mlp_fwd(h2, x1, tgt, w_up, w_down_a, w_down_b, nw, tm=512):
    n_tok = x1.shape[0]

    def main(lead_i, consts, lead_o):
        (h2_ref,), (wup_ref, wda_ref, wdb_ref, _), (f_ref,) = lead_i, consts, lead_o
        h2v = h2_ref[...]
        acc = jnp.zeros((tm, D_MODEL), F32)
        for j in range(D_FF // FF_TILE):
            cs = slice(j * FF_TILE, (j + 1) * FF_TILE)
            u = jnp.dot(h2v, _up_cols(wup_ref, j), preferred_element_type=F32)
            f = jnp.square(jnp.maximum(u, 0.0)).astype(BF16)
            f_ref[:, cs] = f
            acc = acc + jnp.dot(f, _down_rows(wda_ref, wdb_ref, j), preferred_element_type=F32)
        return acc

    def tail(acc, lag_i, consts, lag_o, accs):
        (x1_ref, tgt_ref), nw_ref, (dd_ref, dy_ref), (loss_ref, dnw_ref) = lag_i, consts[3], lag_o, accs
        dn, r = _rms_fwd(acc, nw_ref[...])
        e = x1_ref[...] + dn - tgt_ref[...]
        loss_ref[...] += jnp.full(loss_ref.shape, (0.5 / D_MODEL) * jnp.sum(e * e), F32)
        dy = e * (1.0 / D_MODEL)
        dd, dnw = _rms_bwd(acc, r, nw_ref[...], dy)
        dy_ref[...] = dy
        dd_ref[...] = dd.astype(BF16)
        dnw_ref[...] += dnw

    return _skewed_rows_call(
        "mlp_fwd", main, tail, tm, [h2], [x1, tgt], [w_up, w_down_a, w_down_b, nw],
        [_sds((n_tok, D_FF), BF16)], [_sds((n_tok, D_MODEL), BF16), _sds((n_tok, D_MODEL), F32)],
        [_sds((8, 128), F32), _sds((1, D_MODEL), F32)], carry=(tm, D_MODEL))


def _mlp_bwd(dd, f, x1, dy, w_down_a, w_down_b, w_up, nw, tm=256):
    n_tok = x1.shape[0]

    def main(lead_i, consts, lead_o):
        (dd_ref, f_ref), (wda_ref, wdb_ref, wup_ref, _), (dup_ref,) = lead_i, consts, lead_o
        ddv = dd_ref[...]
        acc = jnp.zeros((tm, D_MODEL), F32)
        for j in range(D_FF // FF_TILE):
            cs = slice(j * FF_TILE, (j + 1) * FF_TILE)
            df = _dot_nt(ddv, _down_rows(wda_ref, wdb_ref, j))
            du = (df * (2.0 * jnp.sqrt(f_ref[:, cs].astype(F32)))).astype(BF16)
            dup_ref[:, cs] = du
            acc = acc + _dot_nt(du, _up_cols(wup_ref, j))
        return acc

    def tail(acc, lag_i, consts, lag_o, accs):
        (x1_ref, dy_ref), nw_ref, (dx1_ref,), (dnw_ref,) = lag_i, consts[3], lag_o, accs
        x1v = x1_ref[...]
        _, r = _rms_fwd(x1v, nw_ref[...])
        dx, dnw = _rms_bwd(x1v, r, nw_ref[...], acc)
        dx1_ref[...] = dy_ref[...] + dx
        dnw_ref[...] += dnw

    return _skewed_rows_call(
        "mlp_bwd", main, tail, tm, [dd, f], [x1, dy], [w_down_a, w_down_b, w_up, nw],
        [_sds((n_tok, D_FF), BF16)], [_sds((n_tok, D_MODEL), F32)], [_sds((1, D_MODEL), F32)], carry=(tm, D_MODEL))


def _outproj_bwd(dx1, o, w_out, nw, tm=256, carried=None):
    n_tok = dx1.shape[0]

    def body(dx1_ref, o_ref, wo_ref, nw_ref, do_ref, dya_ref, dyb_ref, dnw_ref):
        ov = o_ref[...]
        _, r = _rms_fwd(ov, nw_ref[...])
        do, dnw = _rms_bwd(ov, r, nw_ref[...], dx1_ref[...])
        dob = do.astype(BF16)
        do_ref[...] = dob
        dya_ref[...] = _dot_nt(dob, wo_ref[:GM_WIDTH, :])
        dyb_ref[...] = _dot_nt(dob, wo_ref[GM_WIDTH:, :])
        dnw_ref[...] += dnw

    return _rows_call("outproj_bwd", body, tm, [dx1, o], [w_out, nw],
                      [_sds((n_tok, D_MODEL), BF16), _sds((n_tok, GM_WIDTH), F32), _sds((n_tok, SSM_WIDTH), F32)],
                      [_sds((1, D_MODEL), F32)], carried=carried)


def _gmlp_bwd(p_uv, dya, lnw, lnb, e_bf, et_bf, w_cat, w_stack, bmap, carried=None):
    n_tok = p_uv.shape[0]
    chunks_per_step = 2

    def body(puv_ref, dya_ref, lnw_ref, lnb_ref, e_ref, et_ref, wcat_ref, wstack_ref, bmap_ref,
             dpuv_ref, dws_ref, dbs_ref, dlnw_ref, dlnb_ref, wm_scr, wsm_scr):
        t_stk = lax.broadcasted_iota(jnp.int32, (N_HEADS * CHUNK, CHUNK), 0) % CHUNK
        s_stk = lax.broadcasted_iota(jnp.int32, (N_HEADS * CHUNK, CHUNK), 1)

        @pl.when(pl.program_id(0) == 0)
        def _():
            wm_scr[...] = _causal_w_cat(wcat_ref[...])
            wsm_scr[...] = jnp.where(t_stk >= s_stk, wstack_ref[...], 0.0).astype(BF16)

        lnw_v = lnw_ref[...]
        e_v, et_v = e_ref[...], et_ref[...]

        def one_chunk(rows):
            u, v, gu, tu, tv, rstd, xhat, vn = _gmlp_common(puv_ref[rows, :], lnw_v, lnb_ref[...], e_v, et_v)
            vnb = vn.astype(BF16)
            mixed = jnp.dot(wm_scr[...], _head_blocks(vnb), preferred_element_type=F32) + bmap_ref[...]
            dy = dya_ref[rows, :]
            du = dy * mixed * _gelu_grad(u, tu)
            dmixed = dy * gu
            (dbs,) = _seg_dots([dmixed], et_v)
            dblocks = _head_blocks(dmixed.astype(BF16))
            dvn = lax.dot_general(wsm_scr[...], dblocks, (((0,), (0,)), ((), ())), preferred_element_type=F32)
            dws = lax.dot_general(dblocks, vnb, (((1,), (1,)), ((), ())), preferred_element_type=F32)
            dxh = dvn * lnw_v
            m1, m2 = _seg_dots([dxh, dxh * xhat], et_v)
            m1, m2 = _seg_dots([m1 * (1.0 / HEAD_DIM), m2 * (1.0 / HEAD_DIM)], e_v)
            dgv = rstd * (dxh - m1 - xhat * m2)
            dv = dgv * _gelu_grad(v, tv)
            dpuv_ref[rows, :GM_WIDTH] = du.astype(BF16)
            dpuv_ref[rows, GM_WIDTH:] = dv.astype(BF16)
            return dbs, dws, jnp.sum(dvn * xhat, axis=0, keepdims=True), jnp.sum(dvn, axis=0, keepdims=True)

        parts = [one_chunk(slice(k * CHUNK, (k + 1) * CHUNK)) for k in range(chunks_per_step)]
        dbs, dws, dlnw, dlnb = [functools.reduce(lambda a, b: a + b, vals) for vals in zip(*parts)]
        dbs_ref[...] += dbs
        dws_ref[...] += jnp.where(t_stk >= s_stk, dws, 0.0)
        dlnw_ref[...] += dlnw
        dlnb_ref[...] += dlnb

    return _rows_call(
        "gmlp_bwd", body, chunks_per_step * CHUNK, [p_uv, dya], [lnw, lnb, e_bf, et_bf, w_cat, w_stack, bmap],
        [_sds((n_tok, 2 * GM_WIDTH), BF16)],
        [_sds((N_HEADS * CHUNK, CHUNK), F32), _sds((CHUNK, DT_PAD), F32), _sds((1, GM_WIDTH), F32),
         _sds((1, GM_WIDTH), F32)],
        scratch=[pltpu.VMEM((CHUNK, N_HEADS * CHUNK), BF16), pltpu.VMEM((N_HEADS * CHUNK, CHUNK), BF16)],
        carried=carried)


def _ssd_bwd(p_xbc, p_z, p_dt, yssd, sprev, dyb, conv_w, conv_b, dt_bias, a_log, dskip_map, norm_w, e_bf, et_bf, n_seq,
             carried=None):
    n_tok = p_xbc.shape[0]
    nc = n_tok // n_seq // CHUNK

    def body(xr3, xprev3, z3, pdt3, yssd3, sprev3, dyb3,
             cw_ref, cb_ref, dtb_ref, alog_ref, dsk_ref, nw_ref, e_ref, et_ref,
             dpxbc3, dpz3, dpdt3, dcw_ref, dcb_ref, ddtb_ref, dalog_ref, ddsk_ref, dnw_ref,
             ds3_scr, nxt3_scr, dxa3_scr):
        @pl.when(pl.program_id(0) == 0)
        def _():
            for a in (dcw_ref, dcb_ref, ddtb_ref, dalog_ref, ddsk_ref, dnw_ref, ds3_scr, nxt3_scr):
                a[...] = jnp.zeros_like(a)

        for b in range(n_seq):
            one_sequence(xr3.at[b], xprev3.at[b], z3.at[b], pdt3.at[b], yssd3.at[b], sprev3.at[b], dyb3.at[b],
                         cw_ref, cb_ref, dtb_ref, alog_ref, dsk_ref, nw_ref, e_ref, et_ref,
                         dpxbc3.at[b], dpz3.at[b], dpdt3.at[b], dcw_ref, dcb_ref, ddtb_ref, dalog_ref, ddsk_ref, dnw_ref,
                         ds3_scr.at[b], nxt3_scr.at[b], dxa3_scr.at[b])

    def one_sequence(xr_ref, xprev_ref, z_ref, pdt_ref, yssd_ref, sprev_ref, dyb_ref,
                     cw_ref, cb_ref, dtb_ref, alog_ref, dsk_ref, nw_ref, e_ref, et_ref,
                     dpxbc_ref, dpz_ref, dpdt_ref, dcw_ref, dcb_ref, ddtb_ref, dalog_ref, ddsk_ref, dnw_ref,
                     ds_scr, nxt_scr, dxa_scr):
        chunk = nc - 1 - pl.program_id(0)
        xr = xr_ref[...]
        prev = jnp.where(chunk == 0, 0.0, xprev_ref[...])
        et_v = et_ref[...]
        p = _ssd_pre(xr, prev, cw_ref, cb_ref[...], pdt_ref[...], dtb_ref[...], alog_ref[...], e_ref[...])
        last, e_exp, dte, cd = _ssd_maps(p)
        rowi = p["rowi"]
        xs = p["xa"][:, :SSM_WIDTH]
        xd = xs * p["dt_map"]
        a_cs_t = p["a_cs"].T
        tri = _tril_mask()
        dsk = dsk_ref[...]
        nw_v = nw_ref[...]

        yv = yssd_ref[...]
        zv = z_ref[...]
        sz, zg, yg, _, rs = _gate_fwd(yv, zv, nw_v)
        dout = dyb_ref[...]
        for g in range(SSM_GROUPS):
            gs = slice(g * GROUP_W, (g + 1) * GROUP_W)
            dyg_g, dnw_g = _rms_bwd(yg[:, gs], rs[g], nw_v[:, gs], dout[:, gs])
            dnw_ref[:, gs] += dnw_g
            dxa_scr[:, gs] = dyg_g
        dyg = dxa_scr[:, :SSM_WIDTH]
        d_y = dyg * zg
        dpz_ref[...] = (dyg * yv * (sz + zv * sz * (1.0 - sz))).astype(BF16)

        s_prev = sprev_ref[...]
        ds_next = ds_scr[...]
        lane_dt = lax.broadcasted_iota(jnp.int32, (1, DT_PAD), 1)
        da_cols = jnp.zeros((CHUNK, DT_PAD), F32)
        for g in range(SSM_GROUPS):
            gs = slice(g * GROUP_W, (g + 1) * GROUP_W)
            b_off = SSM_WIDTH + g * SSM_STATE
            c_off = SSM_WIDTH + (SSM_GROUPS + g) * SSM_STATE
            bm = p["xa"][:, b_off:b_off + SSM_STATE].astype(BF16)
            cm = p["xa"][:, c_off:c_off + SSM_STATE].astype(BF16)
            cb_mat = _dot_nt(cm, bm)
            d_yg = d_y[:, gs]
            d_ygb = d_yg.astype(BF16)
            xdg = xd[:, gs]
            xdgb = xdg.astype(BF16)
            ds_g = ds_next[:, gs]
            sp_g = s_prev[:, gs]
            bds = _dot(bm, ds_g)
            dcs = d_yg * e_exp[:, gs]
            d_c = _dot_nt(dcs, sp_g)
            ds_scr[:, gs] = cd[:, gs] * ds_g + _dot_tn(cm, dcs)
            d_b = _dot_nt(xdg * dte[:, gs], ds_g)
            dxd_g = bds * dte[:, gs]
            sum_dcb = jnp.zeros((CHUNK, CHUNK), F32)
            for r in range(SSM_GROUPS * 2):
                head = g * 4 + r
                mask = _head_lane_mask(GROUP_W, r)
                dm = _head_decay(p["a_cs"], a_cs_t, head, tri)
                m_mat = cb_mat * dm
                g_mat = _dot_nt(jnp.where(mask, d_yg, 0.0), xdgb)
                w_mat = g_mat * m_mat
                sum_dcb = sum_dcb + g_mat * dm
                dxd_g = dxd_g + jnp.where(mask, _dot_tn(m_mat, d_ygb), 0.0)
                da_h = jnp.sum(w_mat - w_mat.T, axis=1, keepdims=True)
                da_cols = da_cols + jnp.where(lane_dt == head, da_h, 0.0)
            d_c = d_c + _dot(sum_dcb, bm)
            d_b = d_b + _dot_tn(sum_dcb, cm)
            dxa_scr[:, b_off:b_off + SSM_STATE] = d_b
            dxa_scr[:, c_off:c_off + SSM_STATE] = d_c
            y_off_g = _dot(cm, sp_g) * e_exp[:, gs]
            t3 = bds * xdg * dte[:, gs]
            tail = jnp.sum(t3, axis=0, keepdims=True) + jnp.sum(ds_g * sp_g, axis=0, keepdims=True) * cd[:, gs]
            pre_g = d_yg * y_off_g - t3 + jnp.where(last, tail, 0.0)
            s_pre, ddt_g, s_dsk = _seg_dots([pre_g, dxd_g * xs[:, gs], d_yg * xs[:, gs]], et_v[gs, :])
            da_cols = da_cols + s_pre
            ddsk_ref[...] += jnp.sum(s_dsk, axis=0, keepdims=True)
            dxa_scr[:, gs] = dxd_g * p["dt_map"][:, gs] + dsk[:, gs] * d_yg
            if g == 0:
                ddt = ddt_g
            else:
                ddt = ddt + ddt_g
        r_i = lax.broadcasted_iota(jnp.int32, (CHUNK, CHUNK), 0)
        c_i = lax.broadcasted_iota(jnp.int32, (CHUNK, CHUNK), 1)
        ddta = _tri_dot(r_i <= c_i, da_cols, terms=2)
        ddt = ddt + ddta * p["a_neg"]
        dalog_ref[...] += jnp.sum(ddta * p["dt"], axis=0, keepdims=True) * p["a_neg"]
        draw = ddt * _sigmoid(p["pre"])
        ddtb_ref[...] += jnp.sum(draw, axis=0, keepdims=True)
        dpdt_ref[...] = draw.astype(BF16)

        xc = p["xc"]
        sg = p["sg"]
        dxc = dxa_scr[...] * (sg + xc * sg * (1.0 - sg))
        dcb_ref[...] += jnp.sum(dxc, axis=0, keepdims=True)
        for k in range(CONV_K):
            dcw_ref[k] += jnp.sum(dxc * p["shifted"][k], axis=0, keepdims=True)
        nxt = nxt_scr[...]
        dxr = cw_ref[3] * dxc
        for s in range(1, CONV_K):
            dxr = dxr + cw_ref[CONV_K - 1 - s] * _shift_up(dxc, nxt, s)
        dpxbc_ref[...] = dxr.astype(BF16)
        nxt_scr[...] = dxc[:SUBLANES, :]

    seq_len = n_tok // n_seq

    def rows(width):
        return pl.BlockSpec((n_seq, CHUNK, width), lambda s: (0, nc - 1 - s, 0))

    tiles = CHUNK // SUBLANES
    prev_rows = pl.BlockSpec((n_seq, SUBLANES, CONV_CH), lambda s: (0, jnp.maximum((nc - 1 - s) * tiles - 1, 0), 0))

    def whole(shape):
        nd = len(shape)
        return pl.BlockSpec(tuple(shape), lambda s: (0,) * nd)

    def by_seq(a):
        return a.reshape(n_seq, seq_len, a.shape[-1])

    acc_shapes = [(CONV_K, 1, CONV_CH), (1, CONV_CH), (1, DT_PAD), (1, DT_PAD), (1, DT_PAD), (1, SSM_WIDTH)]
    xbc3 = by_seq(p_xbc)
    outs = _call_carrying(
        body, carried, name="ssd_bwd", grid=(nc,),
        in_specs=[rows(CONV_CH), prev_rows, rows(SSM_WIDTH), rows(DT_PAD), rows(SSM_WIDTH), rows(SSM_WIDTH),
                  rows(SSM_WIDTH)] + _ssd_const_specs(),
        out_specs=[rows(CONV_CH), rows(SSM_WIDTH), rows(DT_PAD)] + [whole(s) for s in acc_shapes],
        out_shape=tuple([_sds((n_seq, seq_len, CONV_CH), BF16), _sds((n_seq, seq_len, SSM_WIDTH), BF16),
                         _sds((n_seq, seq_len, DT_PAD), BF16)] + [_sds(s, F32) for s in acc_shapes]),
        scratch_shapes=[pltpu.VMEM((n_seq, SSM_STATE, SSM_WIDTH), F32), pltpu.VMEM((n_seq, SUBLANES, CONV_CH), F32),
                        pltpu.VMEM((n_seq, CHUNK, CONV_CH), F32)],
        operands=[xbc3, xbc3, by_seq(p_z), by_seq(p_dt), by_seq(yssd), by_seq(sprev), by_seq(dyb), conv_w, conv_b, dt_bias,
                  a_log, dskip_map, norm_w, e_bf, et_bf])
    return tuple(o.reshape(n_tok, o.shape[-1]) for o in outs[:3]) + tuple(outs[3:])


def _inproj_bwd(dp_uv, dp_xbc, dp_z, dp_dt, x, dx1, w_uv, w_xbc, w_z, w_dt, nw, tm=256, carried=None):
    n_tok = x.shape[0]

    def body(duv_ref, dxbc_ref, dz_ref, ddt_ref, x_ref, dx1_ref, wuv_ref, wxbc_ref, wz_ref, wdt_ref, nw_ref,
             gx_ref, h_ref, dnw_ref):
        dh = _dot_nt(duv_ref[...], wuv_ref[...]) + _dot_nt(dxbc_ref[...], wxbc_ref[...])
        dh = dh + _dot_nt(dz_ref[...], wz_ref[...]) + _dot_nt(ddt_ref[...], wdt_ref[...])
        xv = x_ref[...]
        h, r = _rms_fwd(xv, nw_ref[...])
        dx, dnw = _rms_bwd(xv, r, nw_ref[...], dh)
        gx_ref[...] = dx1_ref[...] + dx
        h_ref[...] = h.astype(BF16)
        dnw_ref[...] += dnw

    return _rows_call("inproj_bwd", body, tm, [dp_uv, dp_xbc, dp_z, dp_dt, x, dx1], [w_uv, w_xbc, w_z, w_dt, nw],
                      [_sds((n_tok, D_MODEL), F32), _sds((n_tok, D_MODEL), BF16)], [_sds((1, D_MODEL), F32)],
                      carried=carried)


def _const_maps():
    lane = jnp.arange(SSM_WIDTH) // HEAD_DIM
    e_bf = (jnp.arange(DT_PAD)[:, None] == lane[None, :]).astype(BF16)
    return e_bf, e_bf.T


def _pad_lanes(v, width):
    return jnp.pad(v, ((0, 0), (0, width - v.shape[1])))


SHARD_COLS = IN_COLS // N_CHIPS
_UV_END = 2 * GM_WIDTH
_Z_END = _UV_END + SSM_WIDTH
_XBC_END = _Z_END + CONV_CH


def _cols_from_shards(w4, lo, hi):
    pieces = []
    for j in range(N_CHIPS):
        a, b = max(lo, j * SHARD_COLS), min(hi, (j + 1) * SHARD_COLS)
        if a < b:
            pieces.append(w4[j][:, a - j * SHARD_COLS:b - j * SHARD_COLS])
    return pieces[0] if len(pieces) == 1 else jnp.concatenate(pieces, axis=1)


def _shards_from_cols(blocks):
    shards = []
    for j in range(N_CHIPS):
        pieces = []
        for arr, lo, hi in blocks:
            a, b = max(lo, j * SHARD_COLS), min(hi, (j + 1) * SHARD_COLS)
            if a < b:
                pieces.append(arr[:, a - lo:b - lo])
        shards.append(pieces[0] if len(pieces) == 1 else jnp.concatenate(pieces, axis=1))
    return jnp.stack(shards)


def _forward_backward(x, tgt, w_in4, conv_w, small, out_shard, up_shard, down_shard, core, adam_args):
    n_seq, seq_len, _ = x.shape
    n_tok = n_seq * seq_len
    x2 = x.reshape(n_tok, D_MODEL)
    tgt2 = tgt.reshape(n_tok, D_MODEL)
    e_bf, et_bf = _const_maps()

    w_uv = _cols_from_shards(w_in4, 0, _UV_END)
    w_z = _cols_from_shards(w_in4, _UV_END, _Z_END)
    w_xbc = _cols_from_shards(w_in4, _Z_END, _XBC_END)
    w_dt = _pad_lanes(_cols_from_shards(w_in4, _XBC_END, IN_COLS), DT_PAD)

    nw_pre = small["norm_mix_pre"]
    lnw = small["gm_ln_w"].reshape(1, GM_WIDTH)
    lnb = small["gm_ln_b"].reshape(1, GM_WIDTH)
    w_stack = small["gm_w_s"].reshape(N_HEADS * CHUNK, CHUNK)
    w_cat = jnp.transpose(small["gm_w_s"], (1, 0, 2)).reshape(CHUNK, N_HEADS * CHUNK)
    bmap = jnp.repeat(small["gm_b_s"].T, HEAD_DIM, axis=1)
    cw3 = conv_w.reshape(CONV_K, 1, CONV_CH)
    conv_b = small["conv_b"]
    dt_bias = _pad_lanes(small["dt_bias"], DT_PAD)
    a_log = _pad_lanes(small["a_log"], DT_PAD)
    dskip_map = jnp.repeat(small["d_skip"], HEAD_DIM, axis=1)
    ssm_nw = small["ssm_norm_w"]

    half = down_shard.shape[0] // 2
    p_uv, p_xbc, p_z, p_dt, w_out4, w_down_a = _inproj_fwd(
        x2, nw_pre, w_uv, w_xbc, w_z, w_dt, carried=_allgather_exchange([out_shard, down_shard[:half]]))
    ssd_consts = (cw3, conv_b, dt_bias, a_log, dskip_map, ssm_nw, e_bf, et_bf)
    ya, yb, yssd, sprev, w_up4, w_down_b = _mixer_fwd(
        p_uv, p_xbc, p_z, p_dt, lnw, lnb, w_cat, bmap, *ssd_consts, n_seq,
        carried=_allgather_exchange([up_shard, down_shard[half:]]))
    w_out_b = w_out4.reshape(D_MODEL, D_MODEL)
    o, x1, h2 = _outproj_fwd(ya, yb, x2, w_out_b, small["norm_mix_post"], small["norm_ffn_pre"])
    f, dd, dy, loss_acc, d_nffn_post = _mlp_fwd(h2, x1, tgt2, w_up4, w_down_a, w_down_b, small["norm_ffn_post"])

    dup, dx1, d_nffn_pre = _mlp_bwd(dd, f, x1, dy, w_down_a, w_down_b, w_up4, small["norm_ffn_pre"])
    tk = min(DW_TOKENS_PER_STEP, n_tok)
    g_up = _matmul_tn("dw_up", h2, dup, D_MODEL, D_MODEL, tk, stacked=True)
    g_down = _matmul_tn("dw_down", f, dd, 1024, D_MODEL, tk).reshape(N_CHIPS, D_FF // N_CHIPS, D_MODEL)
    do, dya, dyb, d_nmix_post, got_up, got_down = _outproj_bwd(
        dx1, o, w_out_b, small["norm_mix_post"], carried=_pair_exchange([g_up, g_down]))
    h_up = _pair_sum(core, g_up, got_up, 256)
    h_down = _pair_sum(core, g_down, got_down, 256)
    g_out_a = _matmul_tn("dw_out_a", ya, do, GM_WIDTH, D_MODEL, tk)
    g_out_b = _matmul_tn("dw_out_b", yb, do, SSM_WIDTH, D_MODEL, tk)
    g_out = jnp.concatenate([g_out_a, g_out_b], axis=0).reshape(N_CHIPS, D_MODEL // N_CHIPS, D_MODEL)
    dp_uv, d_ws, d_bs_t, d_lnw, d_lnb, slab_up, got_out = _gmlp_bwd(
        p_uv, dya, lnw, lnb, e_bf, et_bf, w_cat, w_stack, bmap,
        carried=_both(_chip_exchange([h_up]), _pair_exchange([g_out])))
    h_out = _pair_sum(core, g_out, got_out, 128)
    early = {
        "gm_ln_w": d_lnw.reshape(N_HEADS, HEAD_DIM), "gm_ln_b": d_lnb.reshape(N_HEADS, HEAD_DIM),
        "gm_w_s": d_ws.reshape(N_HEADS, CHUNK, CHUNK), "gm_b_s": d_bs_t[:, :N_HEADS].T,
        "norm_mix_post": d_nmix_post, "norm_ffn_pre": d_nffn_pre, "norm_ffn_post": d_nffn_post,
    }
    packed_early = _pack(early, tuple(early), tail=loss_acc[0, 0].reshape(1))
    (dp_xbc, dp_z, dp_dt, d_cw, d_cb, d_dtb, d_alog, d_dsk, d_ssm_nw, slab_down, slab_out, all_early) = _ssd_bwd(
        p_xbc, p_z, p_dt, yssd, sprev, dyb, *ssd_consts, n_seq,
        carried=_both(_chip_exchange([h_down, h_out]), _device_gather_exchange(packed_early)))
    gx, h, d_nmix_pre = _inproj_bwd(dp_uv, dp_xbc, dp_z, dp_dt, x2, dx1, w_uv, w_xbc, w_z, w_dt, nw_pre)
    late = {
        "norm_mix_pre": d_nmix_pre, "conv_w": d_cw.reshape(CONV_K, CONV_CH), "conv_b": d_cb,
        "dt_bias": d_dtb[:, :N_HEADS], "a_log": d_alog[:, :N_HEADS], "d_skip": d_dsk[:, :N_HEADS],
        "ssm_norm_w": d_ssm_nw,
    }
    g_uv, all_late = _matmul_tn("dw_in_uv", h, dp_uv, D_MODEL, 2 * GM_WIDTH, tk,
                                carried=_device_gather_exchange(_pack(late, tuple(late))))
    sum_early = _ordered_sum("small_sum_early", all_early)
    small_sum = _unpack(sum_early, {n: v.shape for n, v in early.items()}, tuple(early))
    small_sum.update(_unpack(_ordered_sum("small_sum_late", all_late), {n: v.shape for n, v in late.items()}, tuple(late)))
    loss = sum_early.reshape(-1)[sum(v.size for v in early.values())]
    g_xbc = _matmul_tn("dw_in_xbc", h, dp_xbc, D_MODEL, CONV_CH, tk)
    g_z = _matmul_tn("dw_in_z", h, dp_z, D_MODEL, SSM_WIDTH, tk)
    g_dt = _matmul_tn("dw_in_dt", h, dp_dt, D_MODEL, DT_PAD, tk)

    g_in = _shards_from_cols([(g_uv, 0, _UV_END), (g_z, _UV_END, _Z_END), (g_xbc, _Z_END, _XBC_END),
                              (g_dt, _XBC_END, IN_COLS)])

    red_up, red_down, red_out = _chip_sum(slab_up, 256), _chip_sum(slab_down, 256), _chip_sum(slab_out, 128)
    oth_up, oth_down, oth_out, got_in = _run_exchange(
        "grad_pair_swap", _both(_pair_swap([red_up, red_down, red_out]), _pair_exchange([g_in])))
    h_in = _pair_sum(core, g_in, got_in, 256)
    res = _adamw_halves("adamw_mlp", [(adam_args["w_up"][0], red_up, oth_up) + adam_args["w_up"][1:],
                                      (adam_args["w_down"][0], red_down, oth_down) + adam_args["w_down"][1:]],
                        256, carried=_chip_exchange([h_in]))
    big_out = {"w_up": res[0:4], "w_down": res[4:8]}
    big_out["w_out"] = _adamw_halves("adamw_w_out", [(adam_args["w_out"][0], red_out, oth_out) + adam_args["w_out"][1:]], 128)
    red_in = _chip_sum(res[8], 256)
    (oth_in,) = _run_exchange("grad_pair_swap_in", _pair_swap([red_in]))
    big_out["w_in"] = _adamw_halves("adamw_w_in", [(adam_args["w_in"][0], red_in, oth_in) + adam_args["w_in"][1:]], 256)

    return loss, gx.reshape(x.shape), big_out, small_sum


_HBM = pl.BlockSpec(memory_space=pltpu.HBM)


D2D_CHUNKS = 8
RELAY_AT_NUM, RELAY_AT_DEN = 7, 8
ROW_ALIGN = 16


def _row_chunks(rows, n_chunks):
    size = min(max(rows // n_chunks, ROW_ALIGN), rows)
    assert rows % size == 0
    return [(start, size) for start in range(0, rows, size)]


def _position():
    x, y, c = lax.axis_index("x"), lax.axis_index("y"), lax.axis_index("c")
    chips = [(1 - x, y), (x, 1 - y), (1 - x, 1 - y)]
    return x, y, c, chips


def _allgather_exchange(arrs):
    n = len(arrs)

    def copies(ins, outs, send_sems, recv_sems, local_sems):
        x, y, c, chips = _position()
        me = 2 * x + y
        sibling = (x, y, 1 - c)

        def copy(a, k, src, dst, to):
            return pltpu.make_async_remote_copy(src_ref=src, dst_ref=dst, send_sem=send_sems.at[a, k],
                                                recv_sem=recv_sems.at[a, k], device_id=to, device_id_type=MESH)

        def half_rows(a, pc):
            half = ins[a].shape[0] // 2
            return pl.ds(pc * half, half)

        local = [pltpu.make_async_copy(ins[a], outs[a].at[me], local_sems.at[a]) for a in range(n)]
        ici_out = [[copy(a, k, ins[a].at[half_rows(a, c)], outs[a].at[me, half_rows(a, c)], (px, py, c))
                    for k, (px, py) in enumerate(chips)] for a in range(n)]
        return c, chips, sibling, copy, half_rows, local, ici_out

    def start(ins, outs, send_sems, recv_sems, local_sems):
        _, _, _, _, _, local, ici_out = copies(ins, outs, send_sems, recv_sems, local_sems)
        for cp in local:
            cp.start()
        for a in range(n):
            for cp in ici_out[a]:
                cp.start()

    def relay(ins, outs, send_sems, recv_sems, local_sems):
        c, chips, sibling, copy, half_rows, _, _ = copies(ins, outs, send_sems, recv_sems, local_sems)
        for a in range(n):
            half = ins[a].shape[0] // 2
            for k, (px, py) in enumerate(chips):
                blk = outs[a].at[2 * px + py, half_rows(a, c)]
                copy(a, k, blk, blk, (px, py, c)).wait_recv()
                for first, size in _row_chunks(half, D2D_CHUNKS):
                    piece = outs[a].at[2 * px + py, pl.ds(c * half + first, size)]
                    copy(a, 3 + k, piece, piece, sibling).start()

    def finish(ins, outs, send_sems, recv_sems, local_sems):
        c, chips, sibling, copy, half_rows, local, ici_out = copies(ins, outs, send_sems, recv_sems, local_sems)
        for a in range(n):
            for k, (px, py) in enumerate(chips):
                theirs = outs[a].at[2 * px + py, half_rows(a, 1 - c)]
                copy(a, 3 + k, theirs, theirs, sibling).wait_recv()
                mine = outs[a].at[2 * px + py, half_rows(a, c)]
                copy(a, 3 + k, mine, mine, sibling).wait_send()
        for a in range(n):
            for cp in ici_out[a]:
                cp.wait_send()
        for cp in local:
            cp.wait()

    return _Carried(arrs, [_sds((N_CHIPS,) + a.shape, a.dtype) for a in arrs],
                    [pltpu.SemaphoreType.DMA((n, 6)), pltpu.SemaphoreType.DMA((n, 6)), pltpu.SemaphoreType.DMA((n,))],
                    start, finish, relay)


def _run_exchange(name, exchange):
    n_in, n_out = len(exchange.ins), len(exchange.out_shapes)

    def body(*refs):
        ins, outs, sems = refs[:n_in], refs[n_in:n_in + n_out], refs[n_in + n_out:]
        exchange.start(ins, outs, *sems)
        if exchange.relay is not None:
            exchange.relay(ins, outs, *sems)
        exchange.finish(ins, outs, *sems)

    return pl.pallas_call(
        body, name=name, out_shape=tuple(exchange.out_shapes), in_specs=[_HBM] * n_in,
        out_specs=tuple([_HBM] * n_out), scratch_shapes=exchange.sems,
    )(*exchange.ins)


def _pair_exchange(grads):
    n = len(grads)

    def copier(send_sems, recv_sems):
        x, y, c, _ = _position()

        def copy(a, src, dst):
            return pltpu.make_async_remote_copy(src_ref=src, dst_ref=dst, send_sem=send_sems.at[a],
                                                recv_sem=recv_sems.at[a], device_id=(x, y, 1 - c), device_id_type=MESH)
        return c, copy

    def start(ins, got, send_sems, recv_sems):
        c, copy = copier(send_sems, recv_sems)
        for a in range(n):
            half = ins[a].shape[1] // 2
            for slab in range(N_CHIPS):
                for first, size in _row_chunks(half, D2D_CHUNKS):
                    copy(a, ins[a].at[slab, pl.ds((1 - c) * half + first, size), :],
                         got[a].at[slab, pl.ds(first, size), :]).start()

    def finish(ins, got, send_sems, recv_sems):
        c, copy = copier(send_sems, recv_sems)
        for a in range(n):
            half = ins[a].shape[1] // 2
            copy(a, ins[a].at[:, pl.ds((1 - c) * half, half), :], got[a]).wait()

    return _Carried(grads, [_sds((N_CHIPS, g.shape[1] // 2, g.shape[2]), g.dtype) for g in grads],
                    [pltpu.SemaphoreType.DMA((n,)), pltpu.SemaphoreType.DMA((n,))], start, finish)


def _chip_exchange(hsums):
    n = len(hsums)

    def copies(ins, outs, send_sems, recv_sems, local_sems):
        x, y, c, chips = _position()
        me = 2 * x + y
        cps = []
        for a in range(n):
            cps.append(pltpu.make_async_copy(ins[a].at[me], outs[a].at[me], local_sems.at[a]))
            for k, (px, py) in enumerate(chips):
                cps.append(pltpu.make_async_remote_copy(
                    src_ref=ins[a].at[2 * px + py], dst_ref=outs[a].at[me], send_sem=send_sems.at[a, k],
                    recv_sem=recv_sems.at[a, k], device_id=(px, py, c), device_id_type=MESH))
        return cps

    def start(*refs):
        for cp in copies(*refs):
            cp.start()

    def finish(*refs):
        for cp in copies(*refs):
            cp.wait()

    return _Carried(hsums, [_sds(h.shape, h.dtype) for h in hsums],
                    [pltpu.SemaphoreType.DMA((n, 3)), pltpu.SemaphoreType.DMA((n, 3)), pltpu.SemaphoreType.DMA((n,))],
                    start, finish)


def _pair_swap(reds):
    n = len(reds)

    def copier(send_sems, recv_sems):
        x, y, c, _ = _position()

        def copy(a, src, dst):
            return pltpu.make_async_remote_copy(src_ref=src, dst_ref=dst, send_sem=send_sems.at[a],
                                                recv_sem=recv_sems.at[a], device_id=(x, y, 1 - c), device_id_type=MESH)
        return copy

    def start(ins, outs, send_sems, recv_sems):
        copy = copier(send_sems, recv_sems)
        for a in range(n):
            for first, size in _row_chunks(ins[a].shape[0], 2 * D2D_CHUNKS):
                copy(a, ins[a].at[pl.ds(first, size), :], outs[a].at[pl.ds(first, size), :]).start()

    def finish(ins, outs, send_sems, recv_sems):
        copy = copier(send_sems, recv_sems)
        for a in range(n):
            copy(a, ins[a], outs[a]).wait()

    return _Carried(reds, [_sds(r.shape, r.dtype) for r in reds],
                    [pltpu.SemaphoreType.DMA((n,)), pltpu.SemaphoreType.DMA((n,))], start, finish)


def _device_gather_exchange(packed):
    def copies(ins, outs, send_sems, recv_sems, local_sem):
        (x_ref,), (all_ref,) = ins, outs
        x, y, c, chips = _position()
        me, sibling = (x, y, c), (x, y, 1 - c)

        def slab(px, py, pc):
            return all_ref.at[4 * px + 2 * py + pc]

        def copy(k, block, to, src=None):
            return pltpu.make_async_remote_copy(
                src_ref=slab(*block) if src is None else src, dst_ref=slab(*block), send_sem=send_sems.at[k],
                recv_sem=recv_sems.at[k], device_id=to, device_id_type=MESH)

        mine = pltpu.make_async_copy(x_ref, slab(*me), local_sem)
        first = [copy(0, me, sibling, src=x_ref)]
        first += [copy(1 + j, me, (*chip, c), src=x_ref) for j, chip in enumerate(chips)]
        passed = [copy(4 + j, (*chip, c), sibling) for j, chip in enumerate(chips)]
        return c, chips, me, sibling, copy, mine, first, passed

    def start(ins, outs, send_sems, recv_sems, local_sem):
        _, _, _, _, _, mine, first, _ = copies(ins, outs, send_sems, recv_sems, local_sem)
        mine.start()
        for cp in first:
            cp.start()

    def relay(ins, outs, send_sems, recv_sems, local_sem):
        c, chips, me, _, copy, _, _, passed = copies(ins, outs, send_sems, recv_sems, local_sem)
        for j, chip in enumerate(chips):
            copy(1 + j, (*chip, c), me).wait_recv()
            passed[j].start()

    def finish(ins, outs, send_sems, recv_sems, local_sem):
        c, chips, me, sibling, copy, mine, first, passed = copies(ins, outs, send_sems, recv_sems, local_sem)
        copy(0, sibling, me).wait_recv()
        for j, chip in enumerate(chips):
            copy(4 + j, (*chip, 1 - c), me).wait_recv()
        for cp in first + passed:
            cp.wait_send()
        mine.wait()

    return _Carried([packed], [_sds((N_DEV,) + packed.shape, F32)],
                    [pltpu.SemaphoreType.DMA((7,)), pltpu.SemaphoreType.DMA((7,)), pltpu.SemaphoreType.DMA],
                    start, finish, relay)


def _ordered_sum(name, slabs):
    _, m_per, n_cols = slabs.shape

    def body(s_ref, o_ref):
        acc = s_ref[0]
        for d in range(1, N_DEV):
            acc = acc + s_ref[d]
        o_ref[...] = acc

    vmem = pl.BlockSpec(memory_space=pltpu.VMEM)
    return pl.pallas_call(body, name=name, out_shape=_sds((m_per, n_cols), F32), in_specs=[vmem], out_specs=vmem)(slabs)


def _pair_sum(core, own, got, tm):
    _, half, cols = got.shape
    nb = half // tm

    def body(c_ref, a_ref, b_ref, o_ref):
        o_ref[...] = (a_ref[...] + b_ref[...]).astype(BF16)

    return pl.pallas_call(
        body, name="grad_pair_sum", out_shape=_sds(got.shape, BF16),
        grid_spec=pltpu.PrefetchScalarGridSpec(
            num_scalar_prefetch=1, grid=(N_CHIPS, nb),
            in_specs=[pl.BlockSpec((None, tm, cols), lambda s, i, c_ref: (s, c_ref[0] * nb + i, 0)),
                      pl.BlockSpec((None, tm, cols), lambda s, i, c_ref: (s, i, 0))],
            out_specs=pl.BlockSpec((None, tm, cols), lambda s, i, c_ref: (s, i, 0))),
        compiler_params=_cparams(2),
    )(core, own, got)


def _chip_sum(slabs, tm):
    _, half, cols = slabs.shape

    def body(s_ref, o_ref):
        acc = s_ref[0].astype(F32)
        for k in range(1, N_CHIPS):
            acc = acc + s_ref[k].astype(F32)
        o_ref[...] = acc

    return pl.pallas_call(
        body, name="grad_chip_sum", out_shape=_sds((half, cols), F32), grid=(half // tm,),
        in_specs=[pl.BlockSpec((N_CHIPS, tm, cols), lambda i: (0, i, 0))],
        out_specs=pl.BlockSpec((tm, cols), lambda i: (i, 0)), compiler_params=_cparams(1),
    )(slabs)


def _adam_math(w, g, m, v):
    m2 = ADAM_B1 * m + (1.0 - ADAM_B1) * g
    v2 = ADAM_B2 * v + (1.0 - ADAM_B2) * (g * g)
    m_hat = m2 / (1.0 - ADAM_B1 ** ADAM_STEP)
    v_hat = v2 / (1.0 - ADAM_B2 ** ADAM_STEP)
    delta = -ADAM_LR * (m_hat / (jnp.sqrt(v_hat) + ADAM_EPS) + ADAM_WD * w)
    return delta, m2, v2


def _adamw_halves(name, items, tm, carried=None):
    rows, cols = items[0][0].shape
    nb = rows // 2 // tm
    n = len(items)

    def body(*refs):
        mine = (pl.program_id(0) // nb) == lax.axis_index("c")
        for k in range(n):
            w_ref, own_ref, oth_ref, m_ref, v_ref = refs[5 * k:5 * k + 5]
            g_ref, d_ref, m2_ref, v2_ref = refs[5 * n + 4 * k:5 * n + 4 * k + 4]
            g = jnp.where(mine, own_ref[...], oth_ref[...])
            d, m2, v2 = _adam_math(w_ref[...], g, m_ref[...], v_ref[...])
            g_ref[...] = g
            d_ref[...] = d
            m2_ref[...] = m2
            v2_ref[...] = v2

    full = pl.BlockSpec((tm, cols), lambda i: (i, 0))
    half = pl.BlockSpec((tm, cols), lambda i: (i % nb, 0))
    return _call_carrying(
        body, carried, name=name, grid=(rows // tm,), in_specs=[full, half, half, full, full] * n,
        out_specs=[full] * (4 * n), out_shape=tuple([_sds((rows, cols), F32)] * (4 * n)), scratch_shapes=[],
        operands=[a for item in items for a in item])


def _adamw(name, w, g, m, v, tm):
    def body(w_ref, g_ref, m_ref, v_ref, gout_ref, d_ref, m2_ref, v2_ref):
        gv = g_ref[...]
        d, m2, v2 = _adam_math(w_ref[...], gv, m_ref[...], v_ref[...])
        gout_ref[...] = gv
        d_ref[...] = d
        m2_ref[...] = m2
        v2_ref[...] = v2

    return _rows_call(name, body, tm, [w, g, m, v], [], [_sds(w.shape, F32)] * 4)


_SMALL_NAMES = ("norm_mix_pre", "gm_ln_w", "gm_ln_b", "gm_w_s", "gm_b_s", "conv_w", "conv_b", "dt_bias", "a_log",
                "d_skip", "ssm_norm_w", "norm_mix_post", "norm_ffn_pre", "norm_ffn_post")
_PACK_COLS = 1024


def _pack(parts, names=_SMALL_NAMES, tail=None):
    pieces = [parts[n].reshape(-1) for n in names]
    flat = jnp.concatenate(pieces if tail is None else pieces + [tail])
    rows = -(-flat.shape[0] // (8 * _PACK_COLS)) * 8
    flat = jnp.pad(flat, (0, rows * _PACK_COLS - flat.shape[0]))
    return flat.reshape(rows, _PACK_COLS)


def _unpack(packed, shapes, names=_SMALL_NAMES):
    flat = packed.reshape(-1)
    out, off = {}, 0
    for n in names:
        size = 1
        for s in shapes[n]:
            size *= s
        out[n] = flat[off:off + size].reshape(shapes[n])
        off += size
    return out


def kernel(x, norm_mix_pre, w_in, gm_ln_w, gm_ln_b, gm_w_s, gm_b_s, conv_w, conv_b, dt_bias, a_log, d_skip, ssm_norm_w, w_out, norm_mix_post, norm_ffn_pre, w_up, w_down, norm_ffn_post, loss_target, m_norm_mix_pre, m_w_in, m_gm_ln_w, m_gm_ln_b, m_gm_w_s, m_gm_b_s, m_conv_w, m_conv_b, m_dt_bias, m_a_log, m_d_skip, m_ssm_norm_w, m_w_out, m_norm_mix_post, m_norm_ffn_pre, m_w_up, m_w_down, m_norm_ffn_post, v_norm_mix_pre, v_w_in, v_gm_ln_w, v_gm_ln_b, v_gm_w_s, v_gm_b_s, v_conv_w, v_conv_b, v_dt_bias, v_a_log, v_d_skip, v_ssm_norm_w, v_w_out, v_norm_mix_post, v_norm_ffn_pre, v_w_up, v_w_down, v_norm_ffn_post):
    params = dict(norm_mix_pre=norm_mix_pre, w_in=w_in, gm_ln_w=gm_ln_w, gm_ln_b=gm_ln_b, gm_w_s=gm_w_s, gm_b_s=gm_b_s,
                  conv_w=conv_w, conv_b=conv_b, dt_bias=dt_bias, a_log=a_log, d_skip=d_skip, ssm_norm_w=ssm_norm_w,
                  w_out=w_out, norm_mix_post=norm_mix_post, norm_ffn_pre=norm_ffn_pre, w_up=w_up, w_down=w_down,
                  norm_ffn_post=norm_ffn_post)
    mom1 = dict(norm_mix_pre=m_norm_mix_pre, w_in=m_w_in, gm_ln_w=m_gm_ln_w, gm_ln_b=m_gm_ln_b, gm_w_s=m_gm_w_s,
                gm_b_s=m_gm_b_s, conv_w=m_conv_w, conv_b=m_conv_b, dt_bias=m_dt_bias, a_log=m_a_log, d_skip=m_d_skip,
                ssm_norm_w=m_ssm_norm_w, w_out=m_w_out, norm_mix_post=m_norm_mix_post, norm_ffn_pre=m_norm_ffn_pre,
                w_up=m_w_up, w_down=m_w_down, norm_ffn_post=m_norm_ffn_post)
    mom2 = dict(norm_mix_pre=v_norm_mix_pre, w_in=v_w_in, gm_ln_w=v_gm_ln_w, gm_ln_b=v_gm_ln_b, gm_w_s=v_gm_w_s,
                gm_b_s=v_gm_b_s, conv_w=v_conv_w, conv_b=v_conv_b, dt_bias=v_dt_bias, a_log=v_a_log, d_skip=v_d_skip,
                ssm_norm_w=v_ssm_norm_w, w_out=v_w_out, norm_mix_post=v_norm_mix_post, norm_ffn_pre=v_norm_ffn_pre,
                w_up=v_w_up, w_down=v_w_down, norm_ffn_post=v_norm_ffn_post)
    names = list(params)
    big = ("w_in", "w_out", "w_up", "w_down")
    chip = 2 * lax.axis_index("x") + lax.axis_index("y")

    shards = {n: params[n][0].astype(BF16) for n in big}
    conv_shard = jnp.pad(conv_w[0], ((0, 16 - CONV_K), (0, 0)))
    g_in4, g_conv4 = _run_exchange("allgather_w_in", _allgather_exchange([shards["w_in"], conv_shard]))
    conv_full = jnp.transpose(g_conv4[:, :CONV_K, :], (1, 0, 2)).reshape(CONV_K, CONV_CH)

    small = {n: params[n][0] if params[n].ndim >= 3 else params[n] for n in _SMALL_NAMES if n != "conv_w"}
    core = lax.axis_index("c").astype(jnp.int32).reshape(1)
    adam_args = {n: (params[n][0], mom1[n][0], mom2[n][0]) for n in big}
    loss, grad_x, big_out, small_sum = _forward_backward(
        x, loss_target, g_in4, conv_full, small, shards["w_out"], shards["w_up"], shards["w_down"], core, adam_args)
    grads, delta, new_m, new_v = {}, {}, {}, {}
    for n in big:
        grads[n], delta[n], new_m[n], new_v[n] = [a[None] for a in big_out[n]]

    small_sum["conv_w"] = lax.dynamic_slice_in_dim(small_sum["conv_w"], chip * (CONV_CH // N_CHIPS), CONV_CH // N_CHIPS, axis=1)

    local_shapes = {n: params[n].shape[1:] if params[n].ndim >= 3 else params[n].shape for n in _SMALL_NAMES}
    flat = lambda tree: {n: tree[n].reshape(local_shapes[n]) for n in _SMALL_NAMES}
    packed = [_pack(flat(t)) for t in (params, small_sum, mom1, mom2)]
    _, d_p, m_p, v_p = _adamw("adamw_small", *packed, packed[0].shape[0])
    for src, dst in ((d_p, delta), (m_p, new_m), (v_p, new_v)):
        for n, val in _unpack(src, local_shapes).items():
            dst[n] = val.reshape(params[n].shape)
    for n in _SMALL_NAMES:
        grads[n] = small_sum[n].reshape(params[n].shape)

    out = [loss, grad_x]
    for tree in (grads, delta, new_m, new_v):
        out += [tree[n] for n in names]
    return tuple(out)
```

```python
import functools

import jax
import jax.numpy as jnp
from jax import lax
from jax.experimental import pallas as pl
from jax.experimental.pallas import tpu as pltpu

F32 = jnp.float32
BF16 = jnp.bfloat16
HI = lax.Precision.HIGHEST
MESH = pl.DeviceIdType.MESH

EPS = 1e-6
D_MODEL = 1024
GM_WIDTH = 512
SSM_WIDTH = 512
N_HEADS = 8
HEAD_DIM = 64
CHUNK = 128
SSM_GROUPS = 2
GROUP_W = SSM_WIDTH // SSM_GROUPS
SSM_STATE = 128
CONV_K = 4
CONV_CH = 1024
D_FF = 4096
IN_COLS = 2568
DT_PAD = 128
N_CHIPS = 4
N_DEV = 8

ADAM_LR = 0.001
ADAM_B1 = 0.9
ADAM_B2 = 0.999
ADAM_EPS = 1e-08
ADAM_WD = 0.01
ADAM_STEP = 10

VMEM_LIMIT_BYTES = 56 * 1024 * 1024
FF_TILE = 512
DW_TOKENS_PER_STEP = 2048


def _cparams(n_axes):
    return pltpu.CompilerParams(dimension_semantics=("arbitrary",) * n_axes, vmem_limit_bytes=VMEM_LIMIT_BYTES)


def _dot(a, b):
    return jnp.dot(a.astype(BF16), b.astype(BF16), preferred_element_type=F32)


def _dot_nt(a, b):
    return lax.dot_general(a.astype(BF16), b.astype(BF16), (((1,), (1,)), ((), ())), preferred_element_type=F32)


def _dot_tn(a, b):
    return lax.dot_general(a.astype(BF16), b.astype(BF16), (((0,), (0,)), ((), ())), preferred_element_type=F32)


def _sigmoid(x):
    return 1.0 / (1.0 + jnp.exp(-x))


_GELU_C = 0.7978845608028654
_GELU_A = 0.044715


def _gelu(x):
    t = jnp.tanh(_GELU_C * (x + _GELU_A * (x * x * x)))
    return 0.5 * x * (1.0 + t), t


def _gelu_grad(x, t):
    return 0.5 * (1.0 + t) + 0.5 * x * (1.0 - t * t) * (_GELU_C * (1.0 + 3.0 * _GELU_A * x * x))


def _rms_fwd(x, w):
    r = lax.rsqrt(jnp.mean(x * x, axis=-1, keepdims=True) + EPS)
    return x * r * w, r


def _rms_bwd(x, r, w, dy):
    g = dy * w
    dx = r * g - x * (r * r * r) * jnp.mean(g * x, axis=-1, keepdims=True)
    dw = jnp.sum(dy * x * r, axis=0, keepdims=True)
    return dx, dw


class _Carried:
    def __init__(self, ins, out_shapes, sems, start, finish):
        self.ins, self.out_shapes, self.sems = list(ins), list(out_shapes), list(sems)
        self.start, self.finish = start, finish


def _both(first, second):
    n_i, n_o, n_s = len(first.ins), len(first.out_shapes), len(first.sems)

    def split(ins, outs, sems):
        return (ins[:n_i], outs[:n_o], sems[:n_s]), (ins[n_i:], outs[n_o:], sems[n_s:])

    def start(ins, outs, *sems):
        (i1, o1, s1), (i2, o2, s2) = split(ins, outs, sems)
        first.start(i1, o1, *s1)
        second.start(i2, o2, *s2)

    def finish(ins, outs, *sems):
        (i1, o1, s1), (i2, o2, s2) = split(ins, outs, sems)
        first.finish(i1, o1, *s1)
        second.finish(i2, o2, *s2)

    return _Carried(first.ins + second.ins, first.out_shapes + second.out_shapes, first.sems + second.sems, start, finish)


def _split_carried(refs, n_in, n_out, n_scratch, carried):
    n_ci, n_co, n_cs = len(carried.ins), len(carried.out_shapes), len(carried.sems)
    ins, rest = refs[:n_in], refs[n_in:]
    c_ins, rest = rest[:n_ci], rest[n_ci:]
    outs, rest = rest[:n_out], rest[n_out:]
    c_outs, rest = rest[:n_co], rest[n_co:]
    scr, c_sems = rest[:n_scratch], rest[n_scratch:]
    assert len(c_sems) == n_cs
    return tuple(ins) + tuple(outs) + tuple(scr), c_ins, c_outs, c_sems


def _rows_call(name, body, tm, row_ins, const_ins, row_outs, acc_outs=(), scratch=(), carried=None):
    n_rows = row_ins[0].shape[0]
    assert n_rows % tm == 0
    n_steps = n_rows // tm
    n_in = len(row_ins) + len(const_ins)
    n_ro = len(row_outs)
    n_acc = len(acc_outs)

    def kern(*refs):
        accs = refs[n_in + n_ro:n_in + n_ro + n_acc]

        @pl.when(pl.program_id(0) == 0)
        def _():
            for a in accs:
                a[...] = jnp.zeros_like(a)

        body(*refs)

    def whole(shape):
        nd = len(shape)
        return pl.BlockSpec(tuple(shape), lambda i: (0,) * nd)

    in_specs = [pl.BlockSpec((tm, a.shape[1]), lambda i: (i, 0)) for a in row_ins]
    in_specs += [whole(a.shape) for a in const_ins]
    out_specs = [pl.BlockSpec((tm, s.shape[1]), lambda i: (i, 0)) for s in row_outs]
    out_specs += [whole(s.shape) for s in acc_outs]
    return _call_carrying(
        kern, carried, name=name, grid=(n_steps,), in_specs=in_specs, out_specs=out_specs,
        out_shape=tuple(row_outs) + tuple(acc_outs), scratch_shapes=list(scratch), operands=list(row_ins) + list(const_ins))


def _call_carrying(body, carried, *, name, grid, in_specs, out_specs, out_shape, scratch_shapes, operands):
    n_in, n_out, n_scratch = len(in_specs), len(out_specs), len(scratch_shapes)
    kern = body
    if carried is not None:
        def kern(*refs):
            plain, c_ins, c_outs, c_sems = _split_carried(refs, n_in, n_out, n_scratch, carried)
            first, last = True, True
            for d, size in enumerate(grid):
                first = jnp.logical_and(first, pl.program_id(d) == 0)
                last = jnp.logical_and(last, pl.program_id(d) == size - 1)

            @pl.when(first)
            def _():
                carried.start(c_ins, c_outs, *c_sems)

            body(*plain)

            @pl.when(last)
            def _():
                carried.finish(c_ins, c_outs, *c_sems)

        in_specs = list(in_specs) + [_HBM] * len(carried.ins)
        out_specs = list(out_specs) + [_HBM] * len(carried.out_shapes)
        out_shape = tuple(out_shape) + tuple(carried.out_shapes)
        operands = list(operands) + carried.ins
        scratch_shapes = list(scratch_shapes) + carried.sems
    return pl.pallas_call(
        kern, name=name, grid=grid, in_specs=in_specs, out_specs=out_specs, out_shape=out_shape,
        scratch_shapes=scratch_shapes, compiler_params=_cparams(len(grid)),
    )(*operands)


def _sds(shape, dtype):
    return jax.ShapeDtypeStruct(tuple(shape), dtype)


def _matmul_tn(name, a, b, tm, tn, tk, stacked=False, carried=None):
    k_dim, m_dim = a.shape
    n_dim = b.shape[1]
    assert m_dim % tm == 0 and n_dim % tn == 0 and k_dim % tk == 0

    def kern(a_ref, b_ref, o_ref):
        @pl.when(pl.program_id(2) == 0)
        def _():
            o_ref[...] = jnp.zeros_like(o_ref)

        o_ref[...] += _dot_tn(a_ref[...], b_ref[...])

    if stacked:
        assert tm == m_dim
        out_shape = _sds((n_dim // tn, m_dim, tn), F32)
        out_spec = pl.BlockSpec((None, tm, tn), lambda i, j, k: (j, i, 0))
    else:
        out_shape = _sds((m_dim, n_dim), F32)
        out_spec = pl.BlockSpec((tm, tn), lambda i, j, k: (i, j))
    outs = _call_carrying(
        kern, carried, name=name, grid=(m_dim // tm, n_dim // tn, k_dim // tk),
        in_specs=[pl.BlockSpec((tk, tm), lambda i, j, k: (k, i)), pl.BlockSpec((tk, tn), lambda i, j, k: (k, j))],
        out_specs=[out_spec], out_shape=(out_shape,), scratch_shapes=[], operands=[a, b])
    return outs[0] if carried is None else outs


def _inproj_fwd(x, nw, w_uv, w_xbc, w_z, w_dt, tm=256, carried=None):
    n_tok = x.shape[0]

    def body(x_ref, nw_ref, wuv_ref, wxbc_ref, wz_ref, wdt_ref, puv_ref, pxbc_ref, pz_ref, pdt_ref):
        h, _ = _rms_fwd(x_ref[...], nw_ref[...])
        h = h.astype(BF16)
        puv_ref[...] = jnp.dot(h, wuv_ref[...], preferred_element_type=F32)
        pxbc_ref[...] = jnp.dot(h, wxbc_ref[...], preferred_element_type=F32)
        pz_ref[...] = jnp.dot(h, wz_ref[...], preferred_element_type=F32)
        pdt_ref[...] = jnp.dot(h, wdt_ref[...], preferred_element_type=F32)

    return _rows_call(
        "inproj_fwd", body, tm, [x], [nw, w_uv, w_xbc, w_z, w_dt],
        [_sds((n_tok, 2 * GM_WIDTH), F32), _sds((n_tok, CONV_CH), F32), _sds((n_tok, SSM_WIDTH), F32),
         _sds((n_tok, DT_PAD), F32)], carried=carried)


def _head_lane_mask(width, head):
    lane = lax.broadcasted_iota(jnp.int32, (1, width), 1)
    return (lane // HEAD_DIM) == head


def _split_terms(x, terms):
    parts = []
    for _ in range(terms):
        p = x.astype(BF16)
        parts.append(p)
        x = x - p.astype(F32)
    return parts


def _seg_dots(vals, ind, terms=2):
    m = vals[0].shape[0]
    parts = []
    for v in vals:
        parts += _split_terms(v, terms)
    red = jnp.dot(jnp.concatenate(parts, axis=0), ind, preferred_element_type=F32)
    outs = []
    for i in range(len(vals)):
        acc = red[i * terms * m:(i * terms + 1) * m]
        for t in range(1, terms):
            acc = acc + red[(i * terms + t) * m:(i * terms + t + 1) * m]
        outs.append(acc)
    return outs


def _tri_dot(mask, x, terms=3):
    n = x.shape[1]
    red = jnp.dot(mask.astype(BF16), jnp.concatenate(_split_terms(x, terms), axis=1), preferred_element_type=F32)
    acc = red[:, :n]
    for t in range(1, terms):
        acc = acc + red[:, t * n:(t + 1) * n]
    return acc


def _gmlp_common(puv, lnw, lnb, e_bf, et_bf):
    u = puv[:, :GM_WIDTH]
    v = puv[:, GM_WIDTH:]
    gu, tu = _gelu(u)
    gv, tv = _gelu(v)
    (s1,) = _seg_dots([gv], et_bf)
    (mu,) = _seg_dots([s1 * (1.0 / HEAD_DIM)], e_bf)
    xc = gv - mu
    (s2,) = _seg_dots([xc * xc], et_bf)
    (rstd,) = _seg_dots([lax.rsqrt(s2 * (1.0 / HEAD_DIM) + EPS)], e_bf)
    xhat = xc * rstd
    vn = xhat * lnw + lnb
    return u, v, gu, tu, tv, rstd, xhat, vn


def _tril_mask():
    r = lax.broadcasted_iota(jnp.int32, (CHUNK, CHUNK), 0)
    c = lax.broadcasted_iota(jnp.int32, (CHUNK, CHUNK), 1)
    return r >= c


def _head_blocks(v):
    return jnp.concatenate([jnp.where(_head_lane_mask(GM_WIDTH, h), v, jnp.zeros_like(v)) for h in range(N_HEADS)], axis=0)


def _causal_w_cat(w_cat):
    t = lax.broadcasted_iota(jnp.int32, (CHUNK, N_HEADS * CHUNK), 0)
    s = lax.broadcasted_iota(jnp.int32, (CHUNK, N_HEADS * CHUNK), 1) % CHUNK
    return jnp.where(t >= s, w_cat, 0.0).astype(BF16)


def _gmlp_chunk_fwd(puv, lnw, lnb, e_bf, et_bf, wm, bmap):
    _, _, gu, _, _, _, _, vn = _gmlp_common(puv, lnw, lnb, e_bf, et_bf)
    mixed = jnp.dot(wm, _head_blocks(vn.astype(BF16)), preferred_element_type=F32) + bmap
    return (gu * mixed).astype(BF16)


SUBLANES = 8


def _shift_down(x, tail, s):
    main = pltpu.roll(x, s, 0)
    row = lax.broadcasted_iota(jnp.int32, (SUBLANES, 1), 0)
    head = jnp.where(row < s, pltpu.roll(tail, s, 0), main[:SUBLANES])
    return jnp.concatenate([head, main[SUBLANES:]], axis=0)


def _shift_up(x, head_next, s):
    n = x.shape[0]
    main = pltpu.roll(x, n - s, 0)
    row = lax.broadcasted_iota(jnp.int32, (SUBLANES, 1), 0)
    last = jnp.where(row >= SUBLANES - s, pltpu.roll(head_next, SUBLANES - s, 0), main[n - SUBLANES:])
    return jnp.concatenate([main[:n - SUBLANES], last], axis=0)


def _ssd_pre(xr, tail, cw_ref, cb, pdt, dtb, alog, emap):
    rowi = lax.broadcasted_iota(jnp.int32, (CHUNK, 1), 0)
    shifted = [_shift_down(xr, tail, 3), _shift_down(xr, tail, 2), _shift_down(xr, tail, 1), xr]
    xc = cb
    for k in range(CONV_K):
        xc = xc + cw_ref[k] * shifted[k]
    sg = _sigmoid(xc)
    xa = xc * sg
    pre = pdt + dtb
    dt = jnp.maximum(pre, 0.0) + jnp.log(1.0 + jnp.exp(-jnp.abs(pre)))
    a_neg = -jnp.exp(alog)
    a_cs = _tri_dot(_tril_mask(), dt * a_neg)
    acs_map, dt_map = _seg_dots([a_cs, dt], emap, terms=3)
    return dict(shifted=shifted, xc=xc, sg=sg, xa=xa, pre=pre, dt=dt, a_neg=a_neg, a_cs=a_cs,
                acs_map=acs_map, dt_map=dt_map, rowi=rowi)


def _ssd_maps(p):
    last = p["rowi"] == CHUNK - 1
    aq_map = jnp.sum(jnp.where(last, p["acs_map"], 0.0), axis=0, keepdims=True)
    e_exp = jnp.exp(p["acs_map"])
    dte = jnp.exp(aq_map - p["acs_map"])
    cd = jnp.exp(aq_map)
    return last, e_exp, dte, cd


def _head_decay(a_cs, a_cs_t, head, tri):
    lane = lax.broadcasted_iota(jnp.int32, (1, DT_PAD), 1)
    sub = lax.broadcasted_iota(jnp.int32, (DT_PAD, 1), 0)
    col = jnp.sum(jnp.where(lane == head, a_cs, 0.0), axis=1, keepdims=True)
    row = jnp.sum(jnp.where(sub == head, a_cs_t, 0.0), axis=0, keepdims=True)
    return jnp.exp(jnp.where(tri, col - row, -1e30))


def _gate_fwd(y, z, nw):
    sz = _sigmoid(z)
    zg = z * sz
    yg = y * zg
    outs, rs = [], []
    for g in range(SSM_GROUPS):
        gs = slice(g * GROUP_W, (g + 1) * GROUP_W)
        o, r = _rms_fwd(yg[:, gs], nw[:, gs])
        outs.append(o)
        rs.append(r)
    return sz, zg, yg, outs, rs


def _ssd_const_specs():
    def whole(shape):
        nd = len(shape)
        return pl.BlockSpec(tuple(shape), lambda c: (0,) * nd)
    return [whole((CONV_K, 1, CONV_CH)), whole((1, CONV_CH)), whole((1, DT_PAD)), whole((1, DT_PAD)),
            whole((1, SSM_WIDTH)), whole((1, SSM_WIDTH)), whole((DT_PAD, SSM_WIDTH)), whole((SSM_WIDTH, DT_PAD))]


def _mixer_fwd(p_uv, p_xbc, p_z, p_dt, lnw, lnb, w_cat, bmap, conv_w, conv_b, dt_bias, a_log, dskip_map, norm_w,
               e_bf, et_bf, n_seq, carried=None):
    n_tok = p_xbc.shape[0]
    nc = n_tok // n_seq // CHUNK

    def body(puv3, xr3, z3, pdt3, lnw_ref, lnb_ref, wcat_ref, bmap_ref,
             cw_ref, cb_ref, dtb_ref, alog_ref, dsk_ref, nw_ref, e_ref, et_ref,
             ya3, yb3, yssd3, sprev3, wm_scr, prev3_scr, s3_scr):
        @pl.when(pl.program_id(0) == 0)
        def _():
            wm_scr[...] = _causal_w_cat(wcat_ref[...])
            prev3_scr[...] = jnp.zeros_like(prev3_scr)
            s3_scr[...] = jnp.zeros_like(s3_scr)

        for b in range(n_seq):
            one_sequence(puv3.at[b], xr3.at[b], z3.at[b], pdt3.at[b], lnw_ref, lnb_ref, bmap_ref,
                         cw_ref, cb_ref, dtb_ref, alog_ref, dsk_ref, nw_ref, e_ref, et_ref,
                         ya3.at[b], yb3.at[b], yssd3.at[b], sprev3.at[b], wm_scr, prev3_scr.at[b], s3_scr.at[b])

    def one_sequence(puv_ref, xr_ref, z_ref, pdt_ref, lnw_ref, lnb_ref, bmap_ref,
                     cw_ref, cb_ref, dtb_ref, alog_ref, dsk_ref, nw_ref, e_ref, et_ref,
                     ya_ref, yb_ref, yssd_ref, sprev_ref, wm_scr, prev_scr, s_scr):
        ya_ref[...] = _gmlp_chunk_fwd(puv_ref[...], lnw_ref[...], lnb_ref[...], e_ref[...], et_ref[...], wm_scr[...],
                                      bmap_ref[...])
        xr = xr_ref[...]
        p = _ssd_pre(xr, prev_scr[...], cw_ref, cb_ref[...], pdt_ref[...], dtb_ref[...], alog_ref[...], e_ref[...])
        _, e_exp, dte, cd = _ssd_maps(p)
        xs = p["xa"][:, :SSM_WIDTH]
        xd = xs * p["dt_map"]
        a_cs_t = p["a_cs"].T
        tri = _tril_mask()
        s_old = s_scr[...]
        sprev_ref[...] = s_old
        for g in range(SSM_GROUPS):
            gs = slice(g * GROUP_W, (g + 1) * GROUP_W)
            bm = p["xa"][:, SSM_WIDTH + g * SSM_STATE: SSM_WIDTH + (g + 1) * SSM_STATE].astype(BF16)
            cm = p["xa"][:, SSM_WIDTH + (SSM_GROUPS + g) * SSM_STATE: SSM_WIDTH + (SSM_GROUPS + g + 1) * SSM_STATE].astype(BF16)
            cb_mat = _dot_nt(cm, bm)
            xdg = xd[:, gs].astype(BF16)
            y_g = _dot(cm, s_old[:, gs]) * e_exp[:, gs] + dsk_ref[:, gs] * xs[:, gs]
            for r in range(SSM_GROUPS * 2):
                dm = _head_decay(p["a_cs"], a_cs_t, g * 4 + r, tri)
                full = jnp.dot((cb_mat * dm).astype(BF16), xdg, preferred_element_type=F32)
                y_g = y_g + jnp.where(_head_lane_mask(GROUP_W, r), full, 0.0)
            yssd_ref[:, gs] = y_g
            s_scr[:, gs] = cd[:, gs] * s_old[:, gs] + _dot_tn(bm, xd[:, gs] * dte[:, gs])
        _, _, _, outs, _ = _gate_fwd(yssd_ref[...], z_ref[...], nw_ref[...])
        for g in range(SSM_GROUPS):
            yb_ref[:, g * GROUP_W:(g + 1) * GROUP_W] = outs[g].astype(BF16)
        prev_scr[...] = xr[CHUNK - SUBLANES:, :]

    seq_len = n_tok // n_seq

    def rows(width):
        return pl.BlockSpec((n_seq, CHUNK, width), lambda c: (0, c, 0))

    def whole(shape):
        nd = len(shape)
        return pl.BlockSpec(tuple(shape), lambda c: (0,) * nd)

    def by_seq(a):
        return a.reshape(n_seq, seq_len, a.shape[-1])

    outs = _call_carrying(
        body, carried, name="mixer_fwd", grid=(nc,),
        in_specs=[rows(2 * GM_WIDTH), rows(CONV_CH), rows(SSM_WIDTH), rows(DT_PAD), whole(lnw.shape), whole(lnb.shape),
                  whole(w_cat.shape), whole(bmap.shape)] + _ssd_const_specs(),
        out_specs=[rows(GM_WIDTH), rows(SSM_WIDTH), rows(SSM_WIDTH), rows(SSM_WIDTH)],
        out_shape=(_sds((n_seq, seq_len, GM_WIDTH), BF16), _sds((n_seq, seq_len, SSM_WIDTH), BF16),
                   _sds((n_seq, seq_len, SSM_WIDTH), F32), _sds((n_seq, seq_len, SSM_WIDTH), F32)),
        scratch_shapes=[pltpu.VMEM((CHUNK, N_HEADS * CHUNK), BF16), pltpu.VMEM((n_seq, SUBLANES, CONV_CH), F32),
                        pltpu.VMEM((n_seq, SSM_STATE, SSM_WIDTH), F32)],
        operands=[by_seq(p_uv), by_seq(p_xbc), by_seq(p_z), by_seq(p_dt), lnw, lnb, w_cat, bmap, conv_w, conv_b, dt_bias,
                  a_log, dskip_map, norm_w, e_bf, et_bf])
    return tuple(o.reshape(n_tok, o.shape[-1]) for o in outs[:4]) + tuple(outs[4:])


def _outproj_fwd(ya, yb, x, w_out, nw_post, nw_pre2, tm=256):
    n_tok = x.shape[0]

    def body(ya_ref, yb_ref, x_ref, wo_ref, nwa_ref, nwb_ref, o_ref, x1_ref, h2_ref):
        o = jnp.dot(ya_ref[...], wo_ref[:GM_WIDTH, :], preferred_element_type=F32)
        o = o + jnp.dot(yb_ref[...], wo_ref[GM_WIDTH:, :], preferred_element_type=F32)
        on, _ = _rms_fwd(o, nwa_ref[...])
        x1 = x_ref[...] + on
        h2, _ = _rms_fwd(x1, nwb_ref[...])
        o_ref[...] = o
        x1_ref[...] = x1
        h2_ref[...] = h2.astype(BF16)

    return _rows_call("outproj_fwd", body, tm, [ya, yb, x], [w_out, nw_post, nw_pre2],
                      [_sds((n_tok, D_MODEL), F32), _sds((n_tok, D_MODEL), F32), _sds((n_tok, D_MODEL), BF16)])


def _up_cols(wup_ref, j):
    per = (D_FF // N_CHIPS) // FF_TILE
    return wup_ref[j // per, :, (j % per) * FF_TILE:(j % per + 1) * FF_TILE]


def _down_rows(wda_ref, wdb_ref, j):
    assert 2 * FF_TILE == D_FF // N_CHIPS
    return (wda_ref if j % 2 == 0 else wdb_ref)[j // 2]


def _skewed_rows_call(name, main, tail, tm, lead_ins, lag_ins, const_ins, lead_outs, lag_outs, acc_outs, carry):
    n_rows = lead_ins[0].shape[0]
    assert n_rows % tm == 0
    n = n_rows // tm
    counts = [len(lead_ins), len(lag_ins), len(const_ins), len(lead_outs), len(lag_outs), len(acc_outs)]

    def kern(*refs):
        groups, pos = [], 0
        for cnt in counts:
            groups.append(refs[pos:pos + cnt])
            pos += cnt
        lead_i, lag_i, consts, lead_o, lag_o, accs = groups
        carry_scr = refs[pos]
        i = pl.program_id(0)

        @pl.when(i == 0)
        def _():
            for a in accs:
                a[...] = jnp.zeros_like(a)
            carry_scr[...] = main(lead_i, consts, lead_o)

        @pl.when(jnp.logical_and(i > 0, i < n))
        def _():
            previous = carry_scr[...]
            carry_scr[...] = main(lead_i, consts, lead_o)
            tail(previous, lag_i, consts, lag_o, accs)

        @pl.when(i == n)
        def _():
            tail(carry_scr[...], lag_i, consts, lag_o, accs)

    def lead(width):
        return pl.BlockSpec((tm, width), lambda i: (jnp.minimum(i, n - 1), 0))

    def lag(width):
        return pl.BlockSpec((tm, width), lambda i: (jnp.maximum(i - 1, 0), 0))

    def whole(shape, **kw):
        nd = len(shape)
        return pl.BlockSpec(tuple(shape), lambda i: (0,) * nd, **kw)

    const_specs = [whole(a.shape, pipeline_mode=pl.Buffered(1)) for a in const_ins]
    return pl.pallas_call(
        kern, name=name, grid=(n + 1,),
        in_specs=[lead(a.shape[1]) for a in lead_ins] + [lag(a.shape[1]) for a in lag_ins] + const_specs,
        out_specs=[lead(s.shape[1]) for s in lead_outs] + [lag(s.shape[1]) for s in lag_outs] + [whole(s.shape) for s in acc_outs],
        out_shape=tuple(lead_outs) + tuple(lag_outs) + tuple(acc_outs),
        scratch_shapes=[pltpu.VMEM(carry, F32)], compiler_params=_cparams(1),
    )(*lead_ins, *lag_ins, *const_ins)


def _mlp_fwd(h2, x1, tgt, w_up, w_down_a, w_down_b, nw, tm=512):
    n_tok = x1.shape[0]

    def main(lead_i, consts, lead_o):
        (h2_ref,), (wup_ref, wda_ref, wdb_ref, _), (f_ref,) = lead_i, consts, lead_o
        h2v = h2_ref[...]
        acc = jnp.zeros((tm, D_MODEL), F32)
        for j in range(D_FF // FF_TILE):
            cs = slice(j * FF_TILE, (j + 1) * FF_TILE)
            u = jnp.dot(h2v, _up_cols(wup_ref, j), preferred_element_type=F32)
            f = jnp.square(jnp.maximum(u, 0.0)).astype(BF16)
            f_ref[:, cs] = f
            acc = acc + jnp.dot(f, _down_rows(wda_ref, wdb_ref, j), preferred_element_type=F32)
        return acc

    def tail(acc, lag_i, consts, lag_o, accs):
        (x1_ref, tgt_ref), nw_ref, (dd_ref, dy_ref), (loss_ref, dnw_ref) = lag_i, consts[3], lag_o, accs
        dn, r = _rms_fwd(acc, nw_ref[...])
        e = x1_ref[...] + dn - tgt_ref[...]
        loss_ref[...] += jnp.full(loss_ref.shape, (0.5 / D_MODEL) * jnp.sum(e * e), F32)
        dy = e * (1.0 / D_MODEL)
        dd, dnw = _rms_bwd(acc, r, nw_ref[...], dy)
        dy_ref[...] = dy
        dd_ref[...] = dd.astype(BF16)
        dnw_ref[...] += dnw

    return _skewed_rows_call(
        "mlp_fwd", main, tail, tm, [h2], [x1, tgt], [w_up, w_down_a, w_down_b, nw],
        [_sds((n_tok, D_FF), BF16)], [_sds((n_tok, D_MODEL), BF16), _sds((n_tok, D_MODEL), F32)],
        [_sds((8, 128), F32), _sds((1, D_MODEL), F32)], carry=(tm, D_MODEL))


def _mlp_bwd(dd, f, x1, dy, w_down_a, w_down_b, w_up, nw, tm=256):
    n_tok = x1.shape[0]

    def main(lead_i, consts, lead_o):
        (dd_ref, f_ref), (wda_ref, wdb_ref, wup_ref, _), (dup_ref,) = lead_i, consts, lead_o
        ddv = dd_ref[...]
        acc = jnp.zeros((tm, D_MODEL), F32)
        for j in range(D_FF // FF_TILE):
            cs = slice(j * FF_TILE, (j + 1) * FF_TILE)
            df = _dot_nt(ddv, _down_rows(wda_ref, wdb_ref, j))
            du = (df * (2.0 * jnp.sqrt(f_ref[:, cs].astype(F32)))).astype(BF16)
            dup_ref[:, cs] = du
            acc = acc + _dot_nt(du, _up_cols(wup_ref, j))
        return acc

    def tail(acc, lag_i, consts, lag_o, accs):
        (x1_ref, dy_ref), nw_ref, (dx1_ref,), (dnw_ref,) = lag_i, consts[3], lag_o, accs
        x1v = x1_ref[...]
        _, r = _rms_fwd(x1v, nw_ref[...])
        dx, dnw = _rms_bwd(x1v, r, nw_ref[...], acc)
        dx1_ref[...] = dy_ref[...] + dx
        dnw_ref[...] += dnw

    return _skewed_rows_call(
        "mlp_bwd", main, tail, tm, [dd, f], [x1, dy], [w_down_a, w_down_b, w_up, nw],
        [_sds((n_tok, D_FF), BF16)], [_sds((n_tok, D_MODEL), F32)], [_sds((1, D_MODEL), F32)], carry=(tm, D_MODEL))


def _outproj_bwd(dx1, o, w_out, nw, tm=256, carried=None):
    n_tok = dx1.shape[0]

    def body(dx1_ref, o_ref, wo_ref, nw_ref, do_ref, dya_ref, dyb_ref, dnw_ref):
        ov = o_ref[...]
        _, r = _rms_fwd(ov, nw_ref[...])
        do, dnw = _rms_bwd(ov, r, nw_ref[...], dx1_ref[...])
        dob = do.astype(BF16)
        do_ref[...] = dob
        dya_ref[...] = _dot_nt(dob, wo_ref[:GM_WIDTH, :])
        dyb_ref[...] = _dot_nt(dob, wo_ref[GM_WIDTH:, :])
        dnw_ref[...] += dnw

    return _rows_call("outproj_bwd", body, tm, [dx1, o], [w_out, nw],
                      [_sds((n_tok, D_MODEL), BF16), _sds((n_tok, GM_WIDTH), F32), _sds((n_tok, SSM_WIDTH), F32)],
                      [_sds((1, D_MODEL), F32)], carried=carried)


def _gmlp_bwd(p_uv, dya, lnw, lnb, e_bf, et_bf, w_cat, w_stack, bmap, carried=None):
    n_tok = p_uv.shape[0]
    chunks_per_step = 2

    def body(puv_ref, dya_ref, lnw_ref, lnb_ref, e_ref, et_ref, wcat_ref, wstack_ref, bmap_ref,
             dpuv_ref, dws_ref, dbs_ref, dlnw_ref, dlnb_ref, wm_scr, wsm_scr):
        t_stk = lax.broadcasted_iota(jnp.int32, (N_HEADS * CHUNK, CHUNK), 0) % CHUNK
        s_stk = lax.broadcasted_iota(jnp.int32, (N_HEADS * CHUNK, CHUNK), 1)

        @pl.when(pl.program_id(0) == 0)
        def _():
            wm_scr[...] = _causal_w_cat(wcat_ref[...])
            wsm_scr[...] = jnp.where(t_stk >= s_stk, wstack_ref[...], 0.0).astype(BF16)

        lnw_v = lnw_ref[...]
        e_v, et_v = e_ref[...], et_ref[...]

        def one_chunk(rows):
            u, v, gu, tu, tv, rstd, xhat, vn = _gmlp_common(puv_ref[rows, :], lnw_v, lnb_ref[...], e_v, et_v)
            vnb = vn.astype(BF16)
            mixed = jnp.dot(wm_scr[...], _head_blocks(vnb), preferred_element_type=F32) + bmap_ref[...]
            dy = dya_ref[rows, :]
            du = dy * mixed * _gelu_grad(u, tu)
            dmixed = dy * gu
            (dbs,) = _seg_dots([dmixed], et_v)
            dblocks = _head_blocks(dmixed.astype(BF16))
            dvn = lax.dot_general(wsm_scr[...], dblocks, (((0,), (0,)), ((), ())), preferred_element_type=F32)
            dws = lax.dot_general(dblocks, vnb, (((1,), (1,)), ((), ())), preferred_element_type=F32)
            dxh = dvn * lnw_v
            m1, m2 = _seg_dots([dxh, dxh * xhat], et_v)
            m1, m2 = _seg_dots([m1 * (1.0 / HEAD_DIM), m2 * (1.0 / HEAD_DIM)], e_v)
            dgv = rstd * (dxh - m1 - xhat * m2)
            dv = dgv * _gelu_grad(v, tv)
            dpuv_ref[rows, :GM_WIDTH] = du.astype(BF16)
            dpuv_ref[rows, GM_WIDTH:] = dv.astype(BF16)
            return dbs, dws, jnp.sum(dvn * xhat, axis=0, keepdims=True), jnp.sum(dvn, axis=0, keepdims=True)

        parts = [one_chunk(slice(k * CHUNK, (k + 1) * CHUNK)) for k in range(chunks_per_step)]
        dbs, dws, dlnw, dlnb = [functools.reduce(lambda a, b: a + b, vals) for vals in zip(*parts)]
        dbs_ref[...] += dbs
        dws_ref[...] += jnp.where(t_stk >= s_stk, dws, 0.0)
        dlnw_ref[...] += dlnw
        dlnb_ref[...] += dlnb

    return _rows_call(
        "gmlp_bwd", body, chunks_per_step * CHUNK, [p_uv, dya], [lnw, lnb, e_bf, et_bf, w_cat, w_stack, bmap],
        [_sds((n_tok, 2 * GM_WIDTH), BF16)],
        [_sds((N_HEADS * CHUNK, CHUNK), F32), _sds((CHUNK, DT_PAD), F32), _sds((1, GM_WIDTH), F32),
         _sds((1, GM_WIDTH), F32)],
        scratch=[pltpu.VMEM((CHUNK, N_HEADS * CHUNK), BF16), pltpu.VMEM((N_HEADS * CHUNK, CHUNK), BF16)],
        carried=carried)


def _ssd_bwd(p_xbc, p_z, p_dt, yssd, sprev, dyb, conv_w, conv_b, dt_bias, a_log, dskip_map, norm_w, e_bf, et_bf, n_seq,
             carried=None):
    n_tok = p_xbc.shape[0]
    nc = n_tok // n_seq // CHUNK

    def body(xr3, xprev3, z3, pdt3, yssd3, sprev3, dyb3,
             cw_ref, cb_ref, dtb_ref, alog_ref, dsk_ref, nw_ref, e_ref, et_ref,
             dpxbc3, dpz3, dpdt3, dcw_ref, dcb_ref, ddtb_ref, dalog_ref, ddsk_ref, dnw_ref,
             ds3_scr, nxt3_scr, dxa3_scr):
        @pl.when(pl.program_id(0) == 0)
        def _():
            for a in (dcw_ref, dcb_ref, ddtb_ref, dalog_ref, ddsk_ref, dnw_ref, ds3_scr, nxt3_scr):
                a[...] = jnp.zeros_like(a)

        for b in range(n_seq):
            one_sequence(xr3.at[b], xprev3.at[b], z3.at[b], pdt3.at[b], yssd3.at[b], sprev3.at[b], dyb3.at[b],
                         cw_ref, cb_ref, dtb_ref, alog_ref, dsk_ref, nw_ref, e_ref, et_ref,
                         dpxbc3.at[b], dpz3.at[b], dpdt3.at[b], dcw_ref, dcb_ref, ddtb_ref, dalog_ref, ddsk_ref, dnw_ref,
                         ds3_scr.at[b], nxt3_scr.at[b], dxa3_scr.at[b])

    def one_sequence(xr_ref, xprev_ref, z_ref, pdt_ref, yssd_ref, sprev_ref, dyb_ref,
                     cw_ref, cb_ref, dtb_ref, alog_ref, dsk_ref, nw_ref, e_ref, et_ref,
                     dpxbc_ref, dpz_ref, dpdt_ref, dcw_ref, dcb_ref, ddtb_ref, dalog_ref, ddsk_ref, dnw_ref,
                     ds_scr, nxt_scr, dxa_scr):
        chunk = nc - 1 - pl.program_id(0)
        xr = xr_ref[...]
        prev = jnp.where(chunk == 0, 0.0, xprev_ref[...])
        et_v = et_ref[...]
        p = _ssd_pre(xr, prev, cw_ref, cb_ref[...], pdt_ref[...], dtb_ref[...], alog_ref[...], e_ref[...])
        last, e_exp, dte, cd = _ssd_maps(p)
        rowi = p["rowi"]
        xs = p["xa"][:, :SSM_WIDTH]
        xd = xs * p["dt_map"]
        a_cs_t = p["a_cs"].T
        tri = _tril_mask()
        dsk = dsk_ref[...]
        nw_v = nw_ref[...]

        yv = yssd_ref[...]
        zv = z_ref[...]
        sz, zg, yg, _, rs = _gate_fwd(yv, zv, nw_v)
        dout = dyb_ref[...]
        for g in range(SSM_GROUPS):
            gs = slice(g * GROUP_W, (g + 1) * GROUP_W)
            dyg_g, dnw_g = _rms_bwd(yg[:, gs], rs[g], nw_v[:, gs], dout[:, gs])
            dnw_ref[:, gs] += dnw_g
            dxa_scr[:, gs] = dyg_g
        dyg = dxa_scr[:, :SSM_WIDTH]
        d_y = dyg * zg
        dpz_ref[...] = (dyg * yv * (sz + zv * sz * (1.0 - sz))).astype(BF16)

        s_prev = sprev_ref[...]
        ds_next = ds_scr[...]
        lane_dt = lax.broadcasted_iota(jnp.int32, (1, DT_PAD), 1)
        da_cols = jnp.zeros((CHUNK, DT_PAD), F32)
        for g in range(SSM_GROUPS):
            gs = slice(g * GROUP_W, (g + 1) * GROUP_W)
            b_off = SSM_WIDTH + g * SSM_STATE
            c_off = SSM_WIDTH + (SSM_GROUPS + g) * SSM_STATE
            bm = p["xa"][:, b_off:b_off + SSM_STATE].astype(BF16)
            cm = p["xa"][:, c_off:c_off + SSM_STATE].astype(BF16)
            cb_mat = _dot_nt(cm, bm)
            d_yg = d_y[:, gs]
            d_ygb = d_yg.astype(BF16)
            xdg = xd[:, gs]
            xdgb = xdg.astype(BF16)
            ds_g = ds_next[:, gs]
            sp_g = s_prev[:, gs]
            bds = _dot(bm, ds_g)
            dcs = d_yg * e_exp[:, gs]
            d_c = _dot_nt(dcs, sp_g)
            ds_scr[:, gs] = cd[:, gs] * ds_g + _dot_tn(cm, dcs)
            d_b = _dot_nt(xdg * dte[:, gs], ds_g)
            dxd_g = bds * dte[:, gs]
            sum_dcb = jnp.zeros((CHUNK, CHUNK), F32)
            for r in range(SSM_GROUPS * 2):
                head = g * 4 + r
                mask = _head_lane_mask(GROUP_W, r)
                dm = _head_decay(p["a_cs"], a_cs_t, head, tri)
                m_mat = cb_mat * dm
                g_mat = _dot_nt(jnp.where(mask, d_yg, 0.0), xdgb)
                w_mat = g_mat * m_mat
                sum_dcb = sum_dcb + g_mat * dm
                dxd_g = dxd_g + jnp.where(mask, _dot_tn(m_mat, d_ygb), 0.0)
                da_h = jnp.sum(w_mat - w_mat.T, axis=1, keepdims=True)
                da_cols = da_cols + jnp.where(lane_dt == head, da_h, 0.0)
            d_c = d_c + _dot(sum_dcb, bm)
            d_b = d_b + _dot_tn(sum_dcb, cm)
            dxa_scr[:, b_off:b_off + SSM_STATE] = d_b
            dxa_scr[:, c_off:c_off + SSM_STATE] = d_c
            y_off_g = _dot(cm, sp_g) * e_exp[:, gs]
            t3 = bds * xdg * dte[:, gs]
            tail = jnp.sum(t3, axis=0, keepdims=True) + jnp.sum(ds_g * sp_g, axis=0, keepdims=True) * cd[:, gs]
            pre_g = d_yg * y_off_g - t3 + jnp.where(last, tail, 0.0)
            s_pre, ddt_g, s_dsk = _seg_dots([pre_g, dxd_g * xs[:, gs], d_yg * xs[:, gs]], et_v[gs, :])
            da_cols = da_cols + s_pre
            ddsk_ref[...] += jnp.sum(s_dsk, axis=0, keepdims=True)
            dxa_scr[:, gs] = dxd_g * p["dt_map"][:, gs] + dsk[:, gs] * d_yg
            if g == 0:
                ddt = ddt_g
            else:
                ddt = ddt + ddt_g
        r_i = lax.broadcasted_iota(jnp.int32, (CHUNK, CHUNK), 0)
        c_i = lax.broadcasted_iota(jnp.int32, (CHUNK, CHUNK), 1)
        ddta = _tri_dot(r_i <= c_i, da_cols, terms=2)
        ddt = ddt + ddta * p["a_neg"]
        dalog_ref[...] += jnp.sum(ddta * p["dt"], axis=0, keepdims=True) * p["a_neg"]
        draw = ddt * _sigmoid(p["pre"])
        ddtb_ref[...] += jnp.sum(draw, axis=0, keepdims=True)
        dpdt_ref[...] = draw.astype(BF16)

        xc = p["xc"]
        sg = p["sg"]
        dxc = dxa_scr[...] * (sg + xc * sg * (1.0 - sg))
        dcb_ref[...] += jnp.sum(dxc, axis=0, keepdims=True)
        for k in range(CONV_K):
            dcw_ref[k] += jnp.sum(dxc * p["shifted"][k], axis=0, keepdims=True)
        nxt = nxt_scr[...]
        dxr = cw_ref[3] * dxc
        for s in range(1, CONV_K):
            dxr = dxr + cw_ref[CONV_K - 1 - s] * _shift_up(dxc, nxt, s)
        dpxbc_ref[...] = dxr.astype(BF16)
        nxt_scr[...] = dxc[:SUBLANES, :]

    seq_len = n_tok // n_seq

    def rows(width):
        return pl.BlockSpec((n_seq, CHUNK, width), lambda s: (0, nc - 1 - s, 0))

    tiles = CHUNK // SUBLANES
    prev_rows = pl.BlockSpec((n_seq, SUBLANES, CONV_CH), lambda s: (0, jnp.maximum((nc - 1 - s) * tiles - 1, 0), 0))

    def whole(shape):
        nd = len(shape)
        return pl.BlockSpec(tuple(shape), lambda s: (0,) * nd)

    def by_seq(a):
        return a.reshape(n_seq, seq_len, a.shape[-1])

    acc_shapes = [(CONV_K, 1, CONV_CH), (1, CONV_CH), (1, DT_PAD), (1, DT_PAD), (1, DT_PAD), (1, SSM_WIDTH)]
    xbc3 = by_seq(p_xbc)
    outs = _call_carrying(
        body, carried, name="ssd_bwd", grid=(nc,),
        in_specs=[rows(CONV_CH), prev_rows, rows(SSM_WIDTH), rows(DT_PAD), rows(SSM_WIDTH), rows(SSM_WIDTH),
                  rows(SSM_WIDTH)] + _ssd_const_specs(),
        out_specs=[rows(CONV_CH), rows(SSM_WIDTH), rows(DT_PAD)] + [whole(s) for s in acc_shapes],
        out_shape=tuple([_sds((n_seq, seq_len, CONV_CH), BF16), _sds((n_seq, seq_len, SSM_WIDTH), BF16),
                         _sds((n_seq, seq_len, DT_PAD), BF16)] + [_sds(s, F32) for s in acc_shapes]),
        scratch_shapes=[pltpu.VMEM((n_seq, SSM_STATE, SSM_WIDTH), F32), pltpu.VMEM((n_seq, SUBLANES, CONV_CH), F32),
                        pltpu.VMEM((n_seq, CHUNK, CONV_CH), F32)],
        operands=[xbc3, xbc3, by_seq(p_z), by_seq(p_dt), by_seq(yssd), by_seq(sprev), by_seq(dyb), conv_w, conv_b, dt_bias,
                  a_log, dskip_map, norm_w, e_bf, et_bf])
    return tuple(o.reshape(n_tok, o.shape[-1]) for o in outs[:3]) + tuple(outs[3:])


def _inproj_bwd(dp_uv, dp_xbc, dp_z, dp_dt, x, dx1, w_uv, w_xbc, w_z, w_dt, nw, tm=256, carried=None):
    n_tok = x.shape[0]

    def body(duv_ref, dxbc_ref, dz_ref, ddt_ref, x_ref, dx1_ref, wuv_ref, wxbc_ref, wz_ref, wdt_ref, nw_ref,
             gx_ref, h_ref, dnw_ref):
        dh = _dot_nt(duv_ref[...], wuv_ref[...]) + _dot_nt(dxbc_ref[...], wxbc_ref[...])
        dh = dh + _dot_nt(dz_ref[...], wz_ref[...]) + _dot_nt(ddt_ref[...], wdt_ref[...])
        xv = x_ref[...]
        h, r = _rms_fwd(xv, nw_ref[...])
        dx, dnw = _rms_bwd(xv, r, nw_ref[...], dh)
        gx_ref[...] = dx1_ref[...] + dx
        h_ref[...] = h.astype(BF16)
        dnw_ref[...] += dnw

    return _rows_call("inproj_bwd", body, tm, [dp_uv, dp_xbc, dp_z, dp_dt, x, dx1], [w_uv, w_xbc, w_z, w_dt, nw],
                      [_sds((n_tok, D_MODEL), F32), _sds((n_tok, D_MODEL), BF16)], [_sds((1, D_MODEL), F32)],
                      carried=carried)


def _const_maps():
    lane = jnp.arange(SSM_WIDTH) // HEAD_DIM
    e_bf = (jnp.arange(DT_PAD)[:, None] == lane[None, :]).astype(BF16)
    return e_bf, e_bf.T


def _pad_lanes(v, width):
    return jnp.pad(v, ((0, 0), (0, width - v.shape[1])))


SHARD_COLS = IN_COLS // N_CHIPS
_UV_END = 2 * GM_WIDTH
_Z_END = _UV_END + SSM_WIDTH
_XBC_END = _Z_END + CONV_CH


def _cols_from_shards(w4, lo, hi):
    pieces = []
    for j in range(N_CHIPS):
        a, b = max(lo, j * SHARD_COLS), min(hi, (j + 1) * SHARD_COLS)
        if a < b:
            pieces.append(w4[j][:, a - j * SHARD_COLS:b - j * SHARD_COLS])
    return pieces[0] if len(pieces) == 1 else jnp.concatenate(pieces, axis=1)


def _shards_from_cols(blocks):
    shards = []
    for j in range(N_CHIPS):
        pieces = []
        for arr, lo, hi in blocks:
            a, b = max(lo, j * SHARD_COLS), min(hi, (j + 1) * SHARD_COLS)
            if a < b:
                pieces.append(arr[:, a - lo:b - lo])
        shards.append(pieces[0] if len(pieces) == 1 else jnp.concatenate(pieces, axis=1))
    return jnp.stack(shards)


def _forward_backward(x, tgt, w_in4, conv_w, small, out_shard, up_shard, down_shard, core, adam_args):
    n_seq, seq_len, _ = x.shape
    n_tok = n_seq * seq_len
    x2 = x.reshape(n_tok, D_MODEL)
    tgt2 = tgt.reshape(n_tok, D_MODEL)
    e_bf, et_bf = _const_maps()

    w_uv = _cols_from_shards(w_in4, 0, _UV_END)
    w_z = _cols_from_shards(w_in4, _UV_END, _Z_END)
    w_xbc = _cols_from_shards(w_in4, _Z_END, _XBC_END)
    w_dt = _pad_lanes(_cols_from_shards(w_in4, _XBC_END, IN_COLS), DT_PAD)

    nw_pre = small["norm_mix_pre"]
    lnw = small["gm_ln_w"].reshape(1, GM_WIDTH)
    lnb = small["gm_ln_b"].reshape(1, GM_WIDTH)
    w_stack = small["gm_w_s"].reshape(N_HEADS * CHUNK, CHUNK)
    w_cat = jnp.transpose(small["gm_w_s"], (1, 0, 2)).reshape(CHUNK, N_HEADS * CHUNK)
    bmap = jnp.repeat(small["gm_b_s"].T, HEAD_DIM, axis=1)
    cw3 = conv_w.reshape(CONV_K, 1, CONV_CH)
    conv_b = small["conv_b"]
    dt_bias = _pad_lanes(small["dt_bias"], DT_PAD)
    a_log = _pad_lanes(small["a_log"], DT_PAD)
    dskip_map = jnp.repeat(small["d_skip"], HEAD_DIM, axis=1)
    ssm_nw = small["ssm_norm_w"]

    half = down_shard.shape[0] // 2
    p_uv, p_xbc, p_z, p_dt, w_out4, w_down_a = _inproj_fwd(
        x2, nw_pre, w_uv, w_xbc, w_z, w_dt, carried=_allgather_exchange([out_shard, down_shard[:half]]))
    ssd_consts = (cw3, conv_b, dt_bias, a_log, dskip_map, ssm_nw, e_bf, et_bf)
    ya, yb, yssd, sprev, w_up4, w_down_b = _mixer_fwd(
        p_uv, p_xbc, p_z, p_dt, lnw, lnb, w_cat, bmap, *ssd_consts, n_seq,
        carried=_allgather_exchange([up_shard, down_shard[half:]]))
    w_out_b = w_out4.reshape(D_MODEL, D_MODEL)
    o, x1, h2 = _outproj_fwd(ya, yb, x2, w_out_b, small["norm_mix_post"], small["norm_ffn_pre"])
    f, dd, dy, loss_acc, d_nffn_post = _mlp_fwd(h2, x1, tgt2, w_up4, w_down_a, w_down_b, small["norm_ffn_post"])

    dup, dx1, d_nffn_pre = _mlp_bwd(dd, f, x1, dy, w_down_a, w_down_b, w_up4, small["norm_ffn_pre"])
    tk = min(DW_TOKENS_PER_STEP, n_tok)
    g_up = _matmul_tn("dw_up", h2, dup, D_MODEL, D_MODEL, tk, stacked=True)
    g_down = _matmul_tn("dw_down", f, dd, 1024, D_MODEL, tk).reshape(N_CHIPS, D_FF // N_CHIPS, D_MODEL)
    do, dya, dyb, d_nmix_post, got_up, got_down = _outproj_bwd(
        dx1, o, w_out_b, small["norm_mix_post"], carried=_pair_exchange([g_up, g_down]))
    h_up = _pair_sum(core, g_up, got_up, 256)
    h_down = _pair_sum(core, g_down, got_down, 256)
    g_out_a = _matmul_tn("dw_out_a", ya, do, GM_WIDTH, D_MODEL, tk)
    g_out_b = _matmul_tn("dw_out_b", yb, do, SSM_WIDTH, D_MODEL, tk)
    g_out = jnp.concatenate([g_out_a, g_out_b], axis=0).reshape(N_CHIPS, D_MODEL // N_CHIPS, D_MODEL)
    dp_uv, d_ws, d_bs_t, d_lnw, d_lnb, slab_up, got_out = _gmlp_bwd(
        p_uv, dya, lnw, lnb, e_bf, et_bf, w_cat, w_stack, bmap,
        carried=_both(_chip_exchange([h_up]), _pair_exchange([g_out])))
    h_out = _pair_sum(core, g_out, got_out, 128)
    early = {
        "gm_ln_w": d_lnw.reshape(N_HEADS, HEAD_DIM), "gm_ln_b": d_lnb.reshape(N_HEADS, HEAD_DIM),
        "gm_w_s": d_ws.reshape(N_HEADS, CHUNK, CHUNK), "gm_b_s": d_bs_t[:, :N_HEADS].T,
        "norm_mix_post": d_nmix_post, "norm_ffn_pre": d_nffn_pre, "norm_ffn_post": d_nffn_post,
    }
    packed_early = _pack(early, tuple(early), tail=loss_acc[0, 0].reshape(1))
    (dp_xbc, dp_z, dp_dt, d_cw, d_cb, d_dtb, d_alog, d_dsk, d_ssm_nw, slab_down, slab_out, all_early) = _ssd_bwd(
        p_xbc, p_z, p_dt, yssd, sprev, dyb, *ssd_consts, n_seq,
        carried=_both(_chip_exchange([h_down, h_out]), _device_gather_exchange(packed_early)))
    gx, h, d_nmix_pre = _inproj_bwd(dp_uv, dp_xbc, dp_z, dp_dt, x2, dx1, w_uv, w_xbc, w_z, w_dt, nw_pre)
    late = {
        "norm_mix_pre": d_nmix_pre, "conv_w": d_cw.reshape(CONV_K, CONV_CH), "conv_b": d_cb,
        "dt_bias": d_dtb[:, :N_HEADS], "a_log": d_alog[:, :N_HEADS], "d_skip": d_dsk[:, :N_HEADS],
        "ssm_norm_w": d_ssm_nw,
    }
    g_uv, all_late = _matmul_tn("dw_in_uv", h, dp_uv, D_MODEL, 2 * GM_WIDTH, tk,
                                carried=_device_gather_exchange(_pack(late, tuple(late))))
    sum_early = _ordered_sum("small_sum_early", all_early)
    small_sum = _unpack(sum_early, {n: v.shape for n, v in early.items()}, tuple(early))
    small_sum.update(_unpack(_ordered_sum("small_sum_late", all_late), {n: v.shape for n, v in late.items()}, tuple(late)))
    loss = sum_early.reshape(-1)[sum(v.size for v in early.values())]
    g_xbc = _matmul_tn("dw_in_xbc", h, dp_xbc, D_MODEL, CONV_CH, tk)
    g_z = _matmul_tn("dw_in_z", h, dp_z, D_MODEL, SSM_WIDTH, tk)
    g_dt = _matmul_tn("dw_in_dt", h, dp_dt, D_MODEL, DT_PAD, tk)

    g_in = _shards_from_cols([(g_uv, 0, _UV_END), (g_z, _UV_END, _Z_END), (g_xbc, _Z_END, _XBC_END),
                              (g_dt, _XBC_END, IN_COLS)])

    red_up, red_down, red_out = _chip_sum(slab_up, 256), _chip_sum(slab_down, 256), _chip_sum(slab_out, 128)
    oth_up, oth_down, oth_out, got_in = _run_exchange(
        "grad_pair_swap", _both(_pair_swap([red_up, red_down, red_out]), _pair_exchange([g_in])))
    h_in = _pair_sum(core, g_in, got_in, 256)
    res = _adamw_halves("adamw_mlp", [(adam_args["w_up"][0], red_up, oth_up) + adam_args["w_up"][1:],
                                      (adam_args["w_down"][0], red_down, oth_down) + adam_args["w_down"][1:]],
                        256, carried=_chip_exchange([h_in]))
    big_out = {"w_up": res[0:4], "w_down": res[4:8]}
    big_out["w_out"] = _adamw_halves("adamw_w_out", [(adam_args["w_out"][0], red_out, oth_out) + adam_args["w_out"][1:]], 128)
    red_in = _chip_sum(res[8], 256)
    (oth_in,) = _run_exchange("grad_pair_swap_in", _pair_swap([red_in]))
    big_out["w_in"] = _adamw_halves("adamw_w_in", [(adam_args["w_in"][0], red_in, oth_in) + adam_args["w_in"][1:]], 256)

    return loss, gx.reshape(x.shape), big_out, small_sum


_HBM = pl.BlockSpec(memory_space=pltpu.HBM)


D2D_CHUNKS = 8
ICI_CHUNKS = 4
ROW_ALIGN = 16


def _row_chunks(rows, n_chunks):
    size = min(max(rows // n_chunks, ROW_ALIGN), rows)
    assert rows % size == 0
    return [(start, size) for start in range(0, rows, size)]


def _position():
    x, y, c = lax.axis_index("x"), lax.axis_index("y"), lax.axis_index("c")
    chips = [(1 - x, y), (x, 1 - y), (1 - x, 1 - y)]
    return x, y, c, chips


def _allgather_exchange(arrs):
    n = len(arrs)

    def copies(ins, outs, send_sems, recv_sems, local_sems):
        x, y, c, chips = _position()
        me = 2 * x + y
        sibling = (x, y, 1 - c)

        def copy(a, k, src, dst, to):
            return pltpu.make_async_remote_copy(src_ref=src, dst_ref=dst, send_sem=send_sems.at[a, k],
                                                recv_sem=recv_sems.at[a, k], device_id=to, device_id_type=MESH)

        def half_rows(a, pc):
            half = ins[a].shape[0] // 2
            return pl.ds(pc * half, half)

        local = [pltpu.make_async_copy(ins[a], outs[a].at[me], local_sems.at[a]) for a in range(n)]
        ici_out = [[copy(a, k, ins[a].at[half_rows(a, c)], outs[a].at[me, half_rows(a, c)], (px, py, c))
                    for k, (px, py) in enumerate(chips)] for a in range(n)]
        return c, chips, sibling, copy, half_rows, local, ici_out

    def start(ins, outs, send_sems, recv_sems, local_sems):
        c, chips, _, copy, _, local, _ = copies(ins, outs, send_sems, recv_sems, local_sems)
        x, y, _, _ = _position()
        me = 2 * x + y
        for cp in local:
            cp.start()
        for a in range(n):
            half = ins[a].shape[0] // 2
            for k, (px, py) in enumerate(chips):
                for first, size in _row_chunks(half, ICI_CHUNKS):
                    rows = pl.ds(c * half + first, size)
                    copy(a, k, ins[a].at[rows], outs[a].at[me, rows], (px, py, c)).start()

    def finish(ins, outs, send_sems, recv_sems, local_sems):
        c, chips, sibling, copy, half_rows, local, ici_out = copies(ins, outs, send_sems, recv_sems, local_sems)
        for a in range(n):
            half = ins[a].shape[0] // 2
            for k, (px, py) in enumerate(chips):
                blk = outs[a].at[2 * px + py, half_rows(a, c)]
                copy(a, k, blk, blk, (px, py, c)).wait_recv()
                for first, size in _row_chunks(half, D2D_CHUNKS):
                    piece = outs[a].at[2 * px + py, pl.ds(c * half + first, size)]
                    copy(a, 3 + k, piece, piece, sibling).start()
        for a in range(n):
            for k, (px, py) in enumerate(chips):
                theirs = outs[a].at[2 * px + py, half_rows(a, 1 - c)]
                copy(a, 3 + k, theirs, theirs, sibling).wait_recv()
                mine = outs[a].at[2 * px + py, half_rows(a, c)]
                copy(a, 3 + k, mine, mine, sibling).wait_send()
        for a in range(n):
            for cp in ici_out[a]:
                cp.wait_send()
        for cp in local:
            cp.wait()

    return _Carried(arrs, [_sds((N_CHIPS,) + a.shape, a.dtype) for a in arrs],
                    [pltpu.SemaphoreType.DMA((n, 6)), pltpu.SemaphoreType.DMA((n, 6)), pltpu.SemaphoreType.DMA((n,))],
                    start, finish)


def _run_exchange(name, exchange):
    n_in, n_out = len(exchange.ins), len(exchange.out_shapes)

    def body(*refs):
        ins, outs, sems = refs[:n_in], refs[n_in:n_in + n_out], refs[n_in + n_out:]
        exchange.start(ins, outs, *sems)
        exchange.finish(ins, outs, *sems)

    return pl.pallas_call(
        body, name=name, out_shape=tuple(exchange.out_shapes), in_specs=[_HBM] * n_in,
        out_specs=tuple([_HBM] * n_out), scratch_shapes=exchange.sems,
    )(*exchange.ins)


def _pair_exchange(grads):
    n = len(grads)

    def copier(send_sems, recv_sems):
        x, y, c, _ = _position()

        def copy(a, src, dst):
            return pltpu.make_async_remote_copy(src_ref=src, dst_ref=dst, send_sem=send_sems.at[a],
                                                recv_sem=recv_sems.at[a], device_id=(x, y, 1 - c), device_id_type=MESH)
        return c, copy

    def start(ins, got, send_sems, recv_sems):
        c, copy = copier(send_sems, recv_sems)
        for a in range(n):
            half = ins[a].shape[1] // 2
            for slab in range(N_CHIPS):
                for first, size in _row_chunks(half, D2D_CHUNKS):
                    copy(a, ins[a].at[slab, pl.ds((1 - c) * half + first, size), :],
                         got[a].at[slab, pl.ds(first, size), :]).start()

    def finish(ins, got, send_sems, recv_sems):
        c, copy = copier(send_sems, recv_sems)
        for a in range(n):
            half = ins[a].shape[1] // 2
            copy(a, ins[a].at[:, pl.ds((1 - c) * half, half), :], got[a]).wait()

    return _Carried(grads, [_sds((N_CHIPS, g.shape[1] // 2, g.shape[2]), g.dtype) for g in grads],
                    [pltpu.SemaphoreType.DMA((n,)), pltpu.SemaphoreType.DMA((n,))], start, finish)


def _chip_exchange(hsums):
    n = len(hsums)

    def copies(ins, outs, send_sems, recv_sems, local_sems, pieces):
        x, y, c, chips = _position()
        me = 2 * x + y
        cps = []
        for a in range(n):
            cps.append(pltpu.make_async_copy(ins[a].at[me], outs[a].at[me], local_sems.at[a]))
            rows = ins[a].shape[1]
            for k, (px, py) in enumerate(chips):
                for first, size in (_row_chunks(rows, ICI_CHUNKS) if pieces else [(0, rows)]):
                    cps.append(pltpu.make_async_remote_copy(
                        src_ref=ins[a].at[2 * px + py, pl.ds(first, size)], dst_ref=outs[a].at[me, pl.ds(first, size)],
                        send_sem=send_sems.at[a, k], recv_sem=recv_sems.at[a, k], device_id=(px, py, c),
                        device_id_type=MESH))
        return cps

    def start(*refs):
        for cp in copies(*refs, pieces=True):
            cp.start()

    def finish(*refs):
        for cp in copies(*refs, pieces=False):
            cp.wait()

    return _Carried(hsums, [_sds(h.shape, h.dtype) for h in hsums],
                    [pltpu.SemaphoreType.DMA((n, 3)), pltpu.SemaphoreType.DMA((n, 3)), pltpu.SemaphoreType.DMA((n,))],
                    start, finish)


def _pair_swap(reds):
    n = len(reds)

    def copier(send_sems, recv_sems):
        x, y, c, _ = _position()

        def copy(a, src, dst):
            return pltpu.make_async_remote_copy(src_ref=src, dst_ref=dst, send_sem=send_sems.at[a],
                                                recv_sem=recv_sems.at[a], device_id=(x, y, 1 - c), device_id_type=MESH)
        return copy

    def start(ins, outs, send_sems, recv_sems):
        copy = copier(send_sems, recv_sems)
        for a in range(n):
            for first, size in _row_chunks(ins[a].shape[0], 2 * D2D_CHUNKS):
                copy(a, ins[a].at[pl.ds(first, size), :], outs[a].at[pl.ds(first, size), :]).start()

    def finish(ins, outs, send_sems, recv_sems):
        copy = copier(send_sems, recv_sems)
        for a in range(n):
            copy(a, ins[a], outs[a]).wait()

    return _Carried(reds, [_sds(r.shape, r.dtype) for r in reds],
                    [pltpu.SemaphoreType.DMA((n,)), pltpu.SemaphoreType.DMA((n,))], start, finish)


def _device_gather_exchange(packed):
    def copies(ins, outs, send_sems, recv_sems, local_sem):
        (x_ref,), (all_ref,) = ins, outs
        x, y, c, chips = _position()
        me, sibling = (x, y, c), (x, y, 1 - c)

        def slab(px, py, pc):
            return all_ref.at[4 * px + 2 * py + pc]

        def copy(k, block, to, src=None):
            return pltpu.make_async_remote_copy(
                src_ref=slab(*block) if src is None else src, dst_ref=slab(*block), send_sem=send_sems.at[k],
                recv_sem=recv_sems.at[k], device_id=to, device_id_type=MESH)

        mine = pltpu.make_async_copy(x_ref, slab(*me), local_sem)
        first = [copy(0, me, sibling, src=x_ref)]
        first += [copy(1 + j, me, (*chip, c), src=x_ref) for j, chip in enumerate(chips)]
        passed = [copy(4 + j, (*chip, c), sibling) for j, chip in enumerate(chips)]
        return c, chips, me, sibling, copy, mine, first, passed

    def start(ins, outs, send_sems, recv_sems, local_sem):
        _, _, _, _, _, mine, first, _ = copies(ins, outs, send_sems, recv_sems, local_sem)
        mine.start()
        for cp in first:
            cp.start()

    def finish(ins, outs, send_sems, recv_sems, local_sem):
        c, chips, me, sibling, copy, mine, first, passed = copies(ins, outs, send_sems, recv_sems, local_sem)
        for j, chip in enumerate(chips):
            copy(1 + j, (*chip, c), me).wait_recv()
            passed[j].start()
        copy(0, sibling, me).wait_recv()
        for j, chip in enumerate(chips):
            copy(4 + j, (*chip, 1 - c), me).wait_recv()
        for cp in first + passed:
            cp.wait_send()
        mine.wait()

    return _Carried([packed], [_sds((N_DEV,) + packed.shape, F32)],
                    [pltpu.SemaphoreType.DMA((7,)), pltpu.SemaphoreType.DMA((7,)), pltpu.SemaphoreType.DMA],
                    start, finish)


def _ordered_sum(name, slabs):
    _, m_per, n_cols = slabs.shape

    def body(s_ref, o_ref):
        acc = s_ref[0]
        for d in range(1, N_DEV):
            acc = acc + s_ref[d]
        o_ref[...] = acc

    vmem = pl.BlockSpec(memory_space=pltpu.VMEM)
    return pl.pallas_call(body, name=name, out_shape=_sds((m_per, n_cols), F32), in_specs=[vmem], out_specs=vmem)(slabs)


def _pair_sum(core, own, got, tm):
    _, half, cols = got.shape
    nb = half // tm

    def body(c_ref, a_ref, b_ref, o_ref):
        o_ref[...] = (a_ref[...] + b_ref[...]).astype(BF16)

    return pl.pallas_call(
        body, name="grad_pair_sum", out_shape=_sds(got.shape, BF16),
        grid_spec=pltpu.PrefetchScalarGridSpec(
            num_scalar_prefetch=1, grid=(N_CHIPS, nb),
            in_specs=[pl.BlockSpec((None, tm, cols), lambda s, i, c_ref: (s, c_ref[0] * nb + i, 0)),
                      pl.BlockSpec((None, tm, cols), lambda s, i, c_ref: (s, i, 0))],
            out_specs=pl.BlockSpec((None, tm, cols), lambda s, i, c_ref: (s, i, 0))),
        compiler_params=_cparams(2),
    )(core, own, got)


def _chip_sum(slabs, tm):
    _, half, cols = slabs.shape

    def body(s_ref, o_ref):
        acc = s_ref[0].astype(F32)
        for k in range(1, N_CHIPS):
            acc = acc + s_ref[k].astype(F32)
        o_ref[...] = acc

    return pl.pallas_call(
        body, name="grad_chip_sum", out_shape=_sds((half, cols), F32), grid=(half // tm,),
        in_specs=[pl.BlockSpec((N_CHIPS, tm, cols), lambda i: (0, i, 0))],
        out_specs=pl.BlockSpec((tm, cols), lambda i: (i, 0)), compiler_params=_cparams(1),
    )(slabs)


def _adam_math(w, g, m, v):
    m2 = ADAM_B1 * m + (1.0 - ADAM_B1) * g
    v2 = ADAM_B2 * v + (1.0 - ADAM_B2) * (g * g)
    m_hat = m2 / (1.0 - ADAM_B1 ** ADAM_STEP)
    v_hat = v2 / (1.0 - ADAM_B2 ** ADAM_STEP)
    delta = -ADAM_LR * (m_hat / (jnp.sqrt(v_hat) + ADAM_EPS) + ADAM_WD * w)
    return delta, m2, v2


def _adamw_halves(name, items, tm, carried=None):
    rows, cols = items[0][0].shape
    nb = rows // 2 // tm
    n = len(items)

    def body(*refs):
        mine = (pl.program_id(0) // nb) == lax.axis_index("c")
        for k in range(n):
            w_ref, own_ref, oth_ref, m_ref, v_ref = refs[5 * k:5 * k + 5]
            g_ref, d_ref, m2_ref, v2_ref = refs[5 * n + 4 * k:5 * n + 4 * k + 4]
            g = jnp.where(mine, own_ref[...], oth_ref[...])
            d, m2, v2 = _adam_math(w_ref[...], g, m_ref[...], v_ref[...])
            g_ref[...] = g
            d_ref[...] = d
            m2_ref[...] = m2
            v2_ref[...] = v2

    full = pl.BlockSpec((tm, cols), lambda i: (i, 0))
    half = pl.BlockSpec((tm, cols), lambda i: (i % nb, 0))
    return _call_carrying(
        body, carried, name=name, grid=(rows // tm,), in_specs=[full, half, half, full, full] * n,
        out_specs=[full] * (4 * n), out_shape=tuple([_sds((rows, cols), F32)] * (4 * n)), scratch_shapes=[],
        operands=[a for item in items for a in item])


def _adamw(name, w, g, m, v, tm):
    def body(w_ref, g_ref, m_ref, v_ref, gout_ref, d_ref, m2_ref, v2_ref):
        gv = g_ref[...]
        d, m2, v2 = _adam_math(w_ref[...], gv, m_ref[...], v_ref[...])
        gout_ref[...] = gv
        d_ref[...] = d
        m2_ref[...] = m2
        v2_ref[...] = v2

    return _rows_call(name, body, tm, [w, g, m, v], [], [_sds(w.shape, F32)] * 4)


_SMALL_NAMES = ("norm_mix_pre", "gm_ln_w", "gm_ln_b", "gm_w_s", "gm_b_s", "conv_w", "conv_b", "dt_bias", "a_log",
                "d_skip", "ssm_norm_w", "norm_mix_post", "norm_ffn_pre", "norm_ffn_post")
_PACK_COLS = 1024


def _pack(parts, names=_SMALL_NAMES, tail=None):
    pieces = [parts[n].reshape(-1) for n in names]
    flat = jnp.concatenate(pieces if tail is None else pieces + [tail])
    rows = -(-flat.shape[0] // (8 * _PACK_COLS)) * 8
    flat = jnp.pad(flat, (0, rows * _PACK_COLS - flat.shape[0]))
    return flat.reshape(rows, _PACK_COLS)


def _unpack(packed, shapes, names=_SMALL_NAMES):
    flat = packed.reshape(-1)
    out, off = {}, 0
    for n in names:
        size = 1
        for s in shapes[n]:
            size *= s
        out[n] = flat[off:off + size].reshape(shapes[n])
        off += size
    return out


def kernel(x, norm_mix_pre, w_in, gm_ln_w, gm_ln_b, gm_w_s, gm_b_s, conv_w, conv_b, dt_bias, a_log, d_skip, ssm_norm_w, w_out, norm_mix_post, norm_ffn_pre, w_up, w_down, norm_ffn_post, loss_target, m_norm_mix_pre, m_w_in, m_gm_ln_w, m_gm_ln_b, m_gm_w_s, m_gm_b_s, m_conv_w, m_conv_b, m_dt_bias, m_a_log, m_d_skip, m_ssm_norm_w, m_w_out, m_norm_mix_post, m_norm_ffn_pre, m_w_up, m_w_down, m_norm_ffn_post, v_norm_mix_pre, v_w_in, v_gm_ln_w, v_gm_ln_b, v_gm_w_s, v_gm_b_s, v_conv_w, v_conv_b, v_dt_bias, v_a_log, v_d_skip, v_ssm_norm_w, v_w_out, v_norm_mix_post, v_norm_ffn_pre, v_w_up, v_w_down, v_norm_ffn_post):
    params = dict(norm_mix_pre=norm_mix_pre, w_in=w_in, gm_ln_w=gm_ln_w, gm_ln_b=gm_ln_b, gm_w_s=gm_w_s, gm_b_s=gm_b_s,
                  conv_w=conv_w, conv_b=conv_b, dt_bias=dt_bias, a_log=a_log, d_skip=d_skip, ssm_norm_w=ssm_norm_w,
                  w_out=w_out, norm_mix_post=norm_mix_post, norm_ffn_pre=norm_ffn_pre, w_up=w_up, w_down=w_down,
                  norm_ffn_post=norm_ffn_post)
    mom1 = dict(norm_mix_pre=m_norm_mix_pre, w_in=m_w_in, gm_ln_w=m_gm_ln_w, gm_ln_b=m_gm_ln_b, gm_w_s=m_gm_w_s,
                gm_b_s=m_gm_b_s, conv_w=m_conv_w, conv_b=m_conv_b, dt_bias=m_dt_bias, a_log=m_a_log, d_skip=m_d_skip,
                ssm_norm_w=m_ssm_norm_w, w_out=m_w_out, norm_mix_post=m_norm_mix_post, norm_ffn_pre=m_norm_ffn_pre,
                w_up=m_w_up, w_down=m_w_down, norm_ffn_post=m_norm_ffn_post)
    mom2 = dict(norm_mix_pre=v_norm_mix_pre, w_in=v_w_in, gm_ln_w=v_gm_ln_w, gm_ln_b=v_gm_ln_b, gm_w_s=v_gm_w_s,
                gm_b_s=v_gm_b_s, conv_w=v_conv_w, conv_b=v_conv_b, dt_bias=v_dt_bias, a_log=v_a_log, d_skip=v_d_skip,
                ssm_norm_w=v_ssm_norm_w, w_out=v_w_out, norm_mix_post=v_norm_mix_post, norm_ffn_pre=v_norm_ffn_pre,
                w_up=v_w_up, w_down=v_w_down, norm_ffn_post=v_norm_ffn_post)
    names = list(params)
    big = ("w_in", "w_out", "w_up", "w_down")
    chip = 2 * lax.axis_index("x") + lax.axis_index("y")

    shards = {n: params[n][0].astype(BF16) for n in big}
    conv_shard = jnp.pad(conv_w[0], ((0, 16 - CONV_K), (0, 0)))
    g_in4, g_conv4 = _run_exchange("allgather_w_in", _allgather_exchange([shards["w_in"], conv_shard]))
    conv_full = jnp.transpose(g_conv4[:, :CONV_K, :], (1, 0, 2)).reshape(CONV_K, CONV_CH)

    small = {n: params[n][0] if params[n].ndim >= 3 else params[n] for n in _SMALL_NAMES if n != "conv_w"}
    core = lax.axis_index("c").astype(jnp.int32).reshape(1)
    adam_args = {n: (params[n][0], mom1[n][0], mom2[n][0]) for n in big}
    loss, grad_x, big_out, small_sum = _forward_backward(
        x, loss_target, g_in4, conv_full, small, shards["w_out"], shards["w_up"], shards["w_down"], core, adam_args)
    grads, delta, new_m, new_v = {}, {}, {}, {}
    for n in big:
        grads[n], delta[n], new_m[n], new_v[n] = [a[None] for a in big_out[n]]

    small_sum["conv_w"] = lax.dynamic_slice_in_dim(small_sum["conv_w"], chip * (CONV_CH // N_CHIPS), CONV_CH // N_CHIPS, axis=1)

    local_shapes = {n: params[n].shape[1:] if params[n].ndim >= 3 else params[n].shape for n in _SMALL_NAMES}
    flat = lambda tree: {n: tree[n].reshape(local_shapes[n]) for n in _SMALL_NAMES}
    packed = [_pack(flat(t)) for t in (params, small_sum, mom1, mom2)]
    _, d_p, m_p, v_p = _adamw("adamw_small", *packed, packed[0].shape[0])
    for src, dst in ((d_p, delta), (m_p, new_m), (v_p, new_v)):
        for n, val in _unpack(src, local_shapes).items():
            dst[n] = val.reshape(params[n].shape)
    for n in _SMALL_NAMES:
        grads[n] = small_sum[n].reshape(params[n].shape)

    out = [loss, grad_x]
    for tree in (grads, delta, new_m, new_v):
        out += [tree[n] for n in names]
    return tuple(out)
```

```python
import functools

import jax
import jax.numpy as jnp
from jax import lax
from jax.experimental import pallas as pl
from jax.experimental.pallas import tpu as pltpu

F32 = jnp.float32
BF16 = jnp.bfloat16
HI = lax.Precision.HIGHEST
MESH = pl.DeviceIdType.MESH

EPS = 1e-6
D_MODEL = 1024
GM_WIDTH = 512
SSM_WIDTH = 512
N_HEADS = 8
HEAD_DIM = 64
CHUNK = 128
SSM_GROUPS = 2
GROUP_W = SSM_WIDTH // SSM_GROUPS
SSM_STATE = 128
CONV_K = 4
CONV_CH = 1024
D_FF = 4096
IN_COLS = 2568
DT_PAD = 128
N_CHIPS = 4
N_DEV = 8

ADAM_LR = 0.001
ADAM_B1 = 0.9
ADAM_B2 = 0.999
ADAM_EPS = 1e-08
ADAM_WD = 0.01
ADAM_STEP = 10

VMEM_LIMIT_BYTES = 56 * 1024 * 1024
FF_TILE = 512
DW_TOKENS_PER_STEP = 2048


def _cparams(n_axes):
    return pltpu.CompilerParams(dimension_semantics=("arbitrary",) * n_axes, vmem_limit_bytes=VMEM_LIMIT_BYTES)


def _dot(a, b):
    return jnp.dot(a.astype(BF16), b.astype(BF16), preferred_element_type=F32)


def _dot_nt(a, b):
    return lax.dot_general(a.astype(BF16), b.astype(BF16), (((1,), (1,)), ((), ())), preferred_element_type=F32)


def _dot_tn(a, b):
    return lax.dot_general(a.astype(BF16), b.astype(BF16), (((0,), (0,)), ((), ())), preferred_element_type=F32)


def _sigmoid(x):
    return 1.0 / (1.0 + jnp.exp(-x))


_GELU_C = 0.7978845608028654
_GELU_A = 0.044715


def _gelu(x):
    t = jnp.tanh(_GELU_C * (x + _GELU_A * (x * x * x)))
    return 0.5 * x * (1.0 + t), t


def _gelu_grad(x, t):
    return 0.5 * (1.0 + t) + 0.5 * x * (1.0 - t * t) * (_GELU_C * (1.0 + 3.0 * _GELU_A * x * x))


def _rms_fwd(x, w):
    r = lax.rsqrt(jnp.mean(x * x, axis=-1, keepdims=True) + EPS)
    return x * r * w, r


def _rms_bwd(x, r, w, dy):
    g = dy * w
    dx = r * g - x * (r * r * r) * jnp.mean(g * x, axis=-1, keepdims=True)
    dw = jnp.sum(dy * x * r, axis=0, keepdims=True)
    return dx, dw


class _Carried:
    def __init__(self, ins, out_shapes, sems, start, finish):
        self.ins, self.out_shapes, self.sems = list(ins), list(out_shapes), list(sems)
        self.start, self.finish = start, finish


def _both(first, second):
    n_i, n_o, n_s = len(first.ins), len(first.out_shapes), len(first.sems)

    def split(ins, outs, sems):
        return (ins[:n_i], outs[:n_o], sems[:n_s]), (ins[n_i:], outs[n_o:], sems[n_s:])

    def start(ins, outs, *sems):
        (i1, o1, s1), (i2, o2, s2) = split(ins, outs, sems)
        first.start(i1, o1, *s1)
        second.start(i2, o2, *s2)

    def finish(ins, outs, *sems):
        (i1, o1, s1), (i2, o2, s2) = split(ins, outs, sems)
        first.finish(i1, o1, *s1)
        second.finish(i2, o2, *s2)

    return _Carried(first.ins + second.ins, first.out_shapes + second.out_shapes, first.sems + second.sems, start, finish)


def _split_carried(refs, n_in, n_out, n_scratch, carried):
    n_ci, n_co, n_cs = len(carried.ins), len(carried.out_shapes), len(carried.sems)
    ins, rest = refs[:n_in], refs[n_in:]
    c_ins, rest = rest[:n_ci], rest[n_ci:]
    outs, rest = rest[:n_out], rest[n_out:]
    c_outs, rest = rest[:n_co], rest[n_co:]
    scr, c_sems = rest[:n_scratch], rest[n_scratch:]
    assert len(c_sems) == n_cs
    return tuple(ins) + tuple(outs) + tuple(scr), c_ins, c_outs, c_sems


def _rows_call(name, body, tm, row_ins, const_ins, row_outs, acc_outs=(), scratch=(), carried=None):
    n_rows = row_ins[0].shape[0]
    assert n_rows % tm == 0
    n_steps = n_rows // tm
    n_in = len(row_ins) + len(const_ins)
    n_ro = len(row_outs)
    n_acc = len(acc_outs)

    def kern(*refs):
        accs = refs[n_in + n_ro:n_in + n_ro + n_acc]

        @pl.when(pl.program_id(0) == 0)
        def _():
            for a in accs:
                a[...] = jnp.zeros_like(a)

        body(*refs)

    def whole(shape):
        nd = len(shape)
        return pl.BlockSpec(tuple(shape), lambda i: (0,) * nd)

    in_specs = [pl.BlockSpec((tm, a.shape[1]), lambda i: (i, 0)) for a in row_ins]
    in_specs += [whole(a.shape) for a in const_ins]
    out_specs = [pl.BlockSpec((tm, s.shape[1]), lambda i: (i, 0)) for s in row_outs]
    out_specs += [whole(s.shape) for s in acc_outs]
    return _call_carrying(
        kern, carried, name=name, grid=(n_steps,), in_specs=in_specs, out_specs=out_specs,
        out_shape=tuple(row_outs) + tuple(acc_outs), scratch_shapes=list(scratch), operands=list(row_ins) + list(const_ins))


def _call_carrying(body, carried, *, name, grid, in_specs, out_specs, out_shape, scratch_shapes, operands):
    n_in, n_out, n_scratch = len(in_specs), len(out_specs), len(scratch_shapes)
    kern = body
    if carried is not None:
        def kern(*refs):
            plain, c_ins, c_outs, c_sems = _split_carried(refs, n_in, n_out, n_scratch, carried)
            first, last = True, True
            for d, size in enumerate(grid):
                first = jnp.logical_and(first, pl.program_id(d) == 0)
                last = jnp.logical_and(last, pl.program_id(d) == size - 1)

            @pl.when(first)
            def _():
                carried.start(c_ins, c_outs, *c_sems)

            body(*plain)

            @pl.when(last)
            def _():
                carried.finish(c_ins, c_outs, *c_sems)

        in_specs = list(in_specs) + [_HBM] * len(carried.ins)
        out_specs = list(out_specs) + [_HBM] * len(carried.out_shapes)
        out_shape = tuple(out_shape) + tuple(carried.out_shapes)
        operands = list(operands) + carried.ins
        scratch_shapes = list(scratch_shapes) + carried.sems
    return pl.pallas_call(
        kern, name=name, grid=grid, in_specs=in_specs, out_specs=out_specs, out_shape=out_shape,
        scratch_shapes=scratch_shapes, compiler_params=_cparams(len(grid)),
    )(*operands)


def _sds(shape, dtype):
    return jax.ShapeDtypeStruct(tuple(shape), dtype)


def _matmul_tn(name, a, b, tm, tn, tk, stacked=False, carried=None):
    k_dim, m_dim = a.shape
    n_dim = b.shape[1]
    assert m_dim % tm == 0 and n_dim % tn == 0 and k_dim % tk == 0

    def kern(a_ref, b_ref, o_ref):
        @pl.when(pl.program_id(2) == 0)
        def _():
            o_ref[...] = jnp.zeros_like(o_ref)

        o_ref[...] += _dot_tn(a_ref[...], b_ref[...])

    if stacked:
        assert tm == m_dim
        out_shape = _sds((n_dim // tn, m_dim, tn), F32)
        out_spec = pl.BlockSpec((None, tm, tn), lambda i, j, k: (j, i, 0))
    else:
        out_shape = _sds((m_dim, n_dim), F32)
        out_spec = pl.BlockSpec((tm, tn), lambda i, j, k: (i, j))
    outs = _call_carrying(
        kern, carried, name=name, grid=(m_dim // tm, n_dim // tn, k_dim // tk),
        in_specs=[pl.BlockSpec((tk, tm), lambda i, j, k: (k, i)), pl.BlockSpec((tk, tn), lambda i, j, k: (k, j))],
        out_specs=[out_spec], out_shape=(out_shape,), scratch_shapes=[], operands=[a, b])
    return outs[0] if carried is None else outs


def _inproj_fwd(x, nw, w_uv, w_xbc, w_z, w_dt, tm=256, carried=None):
    n_tok = x.shape[0]

    def body(x_ref, nw_ref, wuv_ref, wxbc_ref, wz_ref, wdt_ref, puv_ref, pxbc_ref, pz_ref, pdt_ref):
        h, _ = _rms_fwd(x_ref[...], nw_ref[...])
        h = h.astype(BF16)
        puv_ref[...] = jnp.dot(h, wuv_ref[...], preferred_element_type=F32)
        pxbc_ref[...] = jnp.dot(h, wxbc_ref[...], preferred_element_type=F32)
        pz_ref[...] = jnp.dot(h, wz_ref[...], preferred_element_type=F32)
        pdt_ref[...] = jnp.dot(h, wdt_ref[...], preferred_element_type=F32)

    return _rows_call(
        "inproj_fwd", body, tm, [x], [nw, w_uv, w_xbc, w_z, w_dt],
        [_sds((n_tok, 2 * GM_WIDTH), F32), _sds((n_tok, CONV_CH), F32), _sds((n_tok, SSM_WIDTH), F32),
         _sds((n_tok, DT_PAD), F32)], carried=carried)


def _head_lane_mask(width, head):
    lane = lax.broadcasted_iota(jnp.int32, (1, width), 1)
    return (lane // HEAD_DIM) == head


def _split_terms(x, terms):
    parts = []
    for _ in range(terms):
        p = x.astype(BF16)
        parts.append(p)
        x = x - p.astype(F32)
    return parts


def _seg_dots(vals, ind, terms=2):
    m = vals[0].shape[0]
    parts = []
    for v in vals:
        parts += _split_terms(v, terms)
    red = jnp.dot(jnp.concatenate(parts, axis=0), ind, preferred_element_type=F32)
    outs = []
    for i in range(len(vals)):
        acc = red[i * terms * m:(i * terms + 1) * m]
        for t in range(1, terms):
            acc = acc + red[(i * terms + t) * m:(i * terms + t + 1) * m]
        outs.append(acc)
    return outs


def _tri_dot(mask, x, terms=3):
    n = x.shape[1]
    red = jnp.dot(mask.astype(BF16), jnp.concatenate(_split_terms(x, terms), axis=1), preferred_element_type=F32)
    acc = red[:, :n]
    for t in range(1, terms):
        acc = acc + red[:, t * n:(t + 1) * n]
    return acc


def _gmlp_common(puv, lnw, lnb, e_bf, et_bf):
    u = puv[:, :GM_WIDTH]
    v = puv[:, GM_WIDTH:]
    gu, tu = _gelu(u)
    gv, tv = _gelu(v)
    (s1,) = _seg_dots([gv], et_bf)
    (mu,) = _seg_dots([s1 * (1.0 / HEAD_DIM)], e_bf)
    xc = gv - mu
    (s2,) = _seg_dots([xc * xc], et_bf)
    (rstd,) = _seg_dots([lax.rsqrt(s2 * (1.0 / HEAD_DIM) + EPS)], e_bf)
    xhat = xc * rstd
    vn = xhat * lnw + lnb
    return u, v, gu, tu, tv, rstd, xhat, vn


def _tril_mask():
    r = lax.broadcasted_iota(jnp.int32, (CHUNK, CHUNK), 0)
    c = lax.broadcasted_iota(jnp.int32, (CHUNK, CHUNK), 1)
    return r >= c


def _head_blocks(v):
    return jnp.concatenate([jnp.where(_head_lane_mask(GM_WIDTH, h), v, jnp.zeros_like(v)) for h in range(N_HEADS)], axis=0)


def _causal_w_cat(w_cat):
    t = lax.broadcasted_iota(jnp.int32, (CHUNK, N_HEADS * CHUNK), 0)
    s = lax.broadcasted_iota(jnp.int32, (CHUNK, N_HEADS * CHUNK), 1) % CHUNK
    return jnp.where(t >= s, w_cat, 0.0).astype(BF16)


def _gmlp_chunk_fwd(puv, lnw, lnb, e_bf, et_bf, wm, bmap):
    _, _, gu, _, _, _, _, vn = _gmlp_common(puv, lnw, lnb, e_bf, et_bf)
    mixed = jnp.dot(wm, _head_blocks(vn.astype(BF16)), preferred_element_type=F32) + bmap
    return (gu * mixed).astype(BF16)


SUBLANES = 8


def _shift_down(x, tail, s):
    main = pltpu.roll(x, s, 0)
    row = lax.broadcasted_iota(jnp.int32, (SUBLANES, 1), 0)
    head = jnp.where(row < s, pltpu.roll(tail, s, 0), main[:SUBLANES])
    return jnp.concatenate([head, main[SUBLANES:]], axis=0)


def _shift_up(x, head_next, s):
    n = x.shape[0]
    main = pltpu.roll(x, n - s, 0)
    row = lax.broadcasted_iota(jnp.int32, (SUBLANES, 1), 0)
    last = jnp.where(row >= SUBLANES - s, pltpu.roll(head_next, SUBLANES - s, 0), main[n - SUBLANES:])
    return jnp.concatenate([main[:n - SUBLANES], last], axis=0)


def _ssd_pre(xr, tail, cw_ref, cb, pdt, dtb, alog, emap):
    rowi = lax.broadcasted_iota(jnp.int32, (CHUNK, 1), 0)
    shifted = [_shift_down(xr, tail, 3), _shift_down(xr, tail, 2), _shift_down(xr, tail, 1), xr]
    xc = cb
    for k in range(CONV_K):
        xc = xc + cw_ref[k] * shifted[k]
    sg = _sigmoid(xc)
    xa = xc * sg
    pre = pdt + dtb
    dt = jnp.maximum(pre, 0.0) + jnp.log(1.0 + jnp.exp(-jnp.abs(pre)))
    a_neg = -jnp.exp(alog)
    a_cs = _tri_dot(_tril_mask(), dt * a_neg)
    acs_map, dt_map = _seg_dots([a_cs, dt], emap, terms=3)
    return dict(shifted=shifted, xc=xc, sg=sg, xa=xa, pre=pre, dt=dt, a_neg=a_neg, a_cs=a_cs,
                acs_map=acs_map, dt_map=dt_map, rowi=rowi)


def _ssd_maps(p):
    last = p["rowi"] == CHUNK - 1
    aq_map = jnp.sum(jnp.where(last, p["acs_map"], 0.0), axis=0, keepdims=True)
    e_exp = jnp.exp(p["acs_map"])
    dte = jnp.exp(aq_map - p["acs_map"])
    cd = jnp.exp(aq_map)
    return last, e_exp, dte, cd


def _head_decay(a_cs, a_cs_t, head, tri):
    lane = lax.broadcasted_iota(jnp.int32, (1, DT_PAD), 1)
    sub = lax.broadcasted_iota(jnp.int32, (DT_PAD, 1), 0)
    col = jnp.sum(jnp.where(lane == head, a_cs, 0.0), axis=1, keepdims=True)
    row = jnp.sum(jnp.where(sub == head, a_cs_t, 0.0), axis=0, keepdims=True)
    return jnp.exp(jnp.where(tri, col - row, -1e30))


def _gate_fwd(y, z, nw):
    sz = _sigmoid(z)
    zg = z * sz
    yg = y * zg
    outs, rs = [], []
    for g in range(SSM_GROUPS):
        gs = slice(g * GROUP_W, (g + 1) * GROUP_W)
        o, r = _rms_fwd(yg[:, gs], nw[:, gs])
        outs.append(o)
        rs.append(r)
    return sz, zg, yg, outs, rs


def _ssd_const_specs():
    def whole(shape):
        nd = len(shape)
        return pl.BlockSpec(tuple(shape), lambda c: (0,) * nd)
    return [whole((CONV_K, 1, CONV_CH)), whole((1, CONV_CH)), whole((1, DT_PAD)), whole((1, DT_PAD)),
            whole((1, SSM_WIDTH)), whole((1, SSM_WIDTH)), whole((DT_PAD, SSM_WIDTH)), whole((SSM_WIDTH, DT_PAD))]


def _mixer_fwd(p_uv, p_xbc, p_z, p_dt, lnw, lnb, w_cat, bmap, conv_w, conv_b, dt_bias, a_log, dskip_map, norm_w,
               e_bf, et_bf, n_seq, carried=None):
    n_tok = p_xbc.shape[0]
    nc = n_tok // n_seq // CHUNK

    def body(puv3, xr3, z3, pdt3, lnw_ref, lnb_ref, wcat_ref, bmap_ref,
             cw_ref, cb_ref, dtb_ref, alog_ref, dsk_ref, nw_ref, e_ref, et_ref,
             ya3, yb3, yssd3, sprev3, wm_scr, prev3_scr, s3_scr):
        @pl.when(pl.program_id(0) == 0)
        def _():
            wm_scr[...] = _causal_w_cat(wcat_ref[...])
            prev3_scr[...] = jnp.zeros_like(prev3_scr)
            s3_scr[...] = jnp.zeros_like(s3_scr)

        for b in range(n_seq):
            one_sequence(puv3.at[b], xr3.at[b], z3.at[b], pdt3.at[b], lnw_ref, lnb_ref, bmap_ref,
                         cw_ref, cb_ref, dtb_ref, alog_ref, dsk_ref, nw_ref, e_ref, et_ref,
                         ya3.at[b], yb3.at[b], yssd3.at[b], sprev3.at[b], wm_scr, prev3_scr.at[b], s3_scr.at[b])

    def one_sequence(puv_ref, xr_ref, z_ref, pdt_ref, lnw_ref, lnb_ref, bmap_ref,
                     cw_ref, cb_ref, dtb_ref, alog_ref, dsk_ref, nw_ref, e_ref, et_ref,
                     ya_ref, yb_ref, yssd_ref, sprev_ref, wm_scr, prev_scr, s_scr):
        ya_ref[...] = _gmlp_chunk_fwd(puv_ref[...], lnw_ref[...], lnb_ref[...], e_ref[...], et_ref[...], wm_scr[...],
                                      bmap_ref[...])
        xr = xr_ref[...]
        p = _ssd_pre(xr, prev_scr[...], cw_ref, cb_ref[...], pdt_ref[...], dtb_ref[...], alog_ref[...], e_ref[...])
        _, e_exp, dte, cd = _ssd_maps(p)
        xs = p["xa"][:, :SSM_WIDTH]
        xd = xs * p["dt_map"]
        a_cs_t = p["a_cs"].T
        tri = _tril_mask()
        s_old = s_scr[...]
        sprev_ref[...] = s_old
        for g in range(SSM_GROUPS):
            gs = slice(g * GROUP_W, (g + 1) * GROUP_W)
            bm = p["xa"][:, SSM_WIDTH + g * SSM_STATE: SSM_WIDTH + (g + 1) * SSM_STATE].astype(BF16)
            cm = p["xa"][:, SSM_WIDTH + (SSM_GROUPS + g) * SSM_STATE: SSM_WIDTH + (SSM_GROUPS + g + 1) * SSM_STATE].astype(BF16)
            cb_mat = _dot_nt(cm, bm)
            xdg = xd[:, gs].astype(BF16)
            y_g = _dot(cm, s_old[:, gs]) * e_exp[:, gs] + dsk_ref[:, gs] * xs[:, gs]
            for r in range(SSM_GROUPS * 2):
                dm = _head_decay(p["a_cs"], a_cs_t, g * 4 + r, tri)
                full = jnp.dot((cb_mat * dm).astype(BF16), xdg, preferred_element_type=F32)
                y_g = y_g + jnp.where(_head_lane_mask(GROUP_W, r), full, 0.0)
            yssd_ref[:, gs] = y_g
            s_scr[:, gs] = cd[:, gs] * s_old[:, gs] + _dot_tn(bm, xd[:, gs] * dte[:, gs])
        _, _, _, outs, _ = _gate_fwd(yssd_ref[...], z_ref[...], nw_ref[...])
        for g in range(SSM_GROUPS):
            yb_ref[:, g * GROUP_W:(g + 1) * GROUP_W] = outs[g].astype(BF16)
        prev_scr[...] = xr[CHUNK - SUBLANES:, :]

    seq_len = n_tok // n_seq

    def rows(width):
        return pl.BlockSpec((n_seq, CHUNK, width), lambda c: (0, c, 0))

    def whole(shape):
        nd = len(shape)
        return pl.BlockSpec(tuple(shape), lambda c: (0,) * nd)

    def by_seq(a):
        return a.reshape(n_seq, seq_len, a.shape[-1])

    outs = _call_carrying(
        body, carried, name="mixer_fwd", grid=(nc,),
        in_specs=[rows(2 * GM_WIDTH), rows(CONV_CH), rows(SSM_WIDTH), rows(DT_PAD), whole(lnw.shape), whole(lnb.shape),
                  whole(w_cat.shape), whole(bmap.shape)] + _ssd_const_specs(),
        out_specs=[rows(GM_WIDTH), rows(SSM_WIDTH), rows(SSM_WIDTH), rows(SSM_WIDTH)],
        out_shape=(_sds((n_seq, seq_len, GM_WIDTH), BF16), _sds((n_seq, seq_len, SSM_WIDTH), BF16),
                   _sds((n_seq, seq_len, SSM_WIDTH), F32), _sds((n_seq, seq_len, SSM_WIDTH), F32)),
        scratch_shapes=[pltpu.VMEM((CHUNK, N_HEADS * CHUNK), BF16), pltpu.VMEM((n_seq, SUBLANES, CONV_CH), F32),
                        pltpu.VMEM((n_seq, SSM_STATE, SSM_WIDTH), F32)],
        operands=[by_seq(p_uv), by_seq(p_xbc), by_seq(p_z), by_seq(p_dt), lnw, lnb, w_cat, bmap, conv_w, conv_b, dt_bias,
                  a_log, dskip_map, norm_w, e_bf, et_bf])
    return tuple(o.reshape(n_tok, o.shape[-1]) for o in outs[:4]) + tuple(outs[4:])


def _outproj_fwd(ya, yb, x, w_out, nw_post, nw_pre2, tm=256):
    n_tok = x.shape[0]

    def body(ya_ref, yb_ref, x_ref, wo_ref, nwa_ref, nwb_ref, o_ref, x1_ref, h2_ref):
        o = jnp.dot(ya_ref[...], wo_ref[:GM_WIDTH, :], preferred_element_type=F32)
        o = o + jnp.dot(yb_ref[...], wo_ref[GM_WIDTH:, :], preferred_element_type=F32)
        on, _ = _rms_fwd(o, nwa_ref[...])
        x1 = x_ref[...] + on
        h2, _ = _rms_fwd(x1, nwb_ref[...])
        o_ref[...] = o
        x1_ref[...] = x1
        h2_ref[...] = h2.astype(BF16)

    return _rows_call("outproj_fwd", body, tm, [ya, yb, x], [w_out, nw_post, nw_pre2],
                      [_sds((n_tok, D_MODEL), F32), _sds((n_tok, D_MODEL), F32), _sds((n_tok, D_MODEL), BF16)])


def _up_cols(wup_ref, j):
    per = (D_FF // N_CHIPS) // FF_TILE
    return wup_ref[j // per, :, (j % per) * FF_TILE:(j % per + 1) * FF_TILE]


def _down_rows(wda_ref, wdb_ref, j):
    assert 2 * FF_TILE == D_FF // N_CHIPS
    return (wda_ref if j % 2 == 0 else wdb_ref)[j // 2]


def _skewed_rows_call(name, main, tail, tm, lead_ins, lag_ins, const_ins, lead_outs, lag_outs, acc_outs, carry):
    n_rows = lead_ins[0].shape[0]
    assert n_rows % tm == 0
    n = n_rows // tm
    counts = [len(lead_ins), len(lag_ins), len(const_ins), len(lead_outs), len(lag_outs), len(acc_outs)]

    def kern(*refs):
        groups, pos = [], 0
        for cnt in counts:
            groups.append(refs[pos:pos + cnt])
            pos += cnt
        lead_i, lag_i, consts, lead_o, lag_o, accs = groups
        carry_scr = refs[pos]
        i = pl.program_id(0)

        @pl.when(i == 0)
        def _():
            for a in accs:
                a[...] = jnp.zeros_like(a)
            carry_scr[...] = main(lead_i, consts, lead_o)

        @pl.when(jnp.logical_and(i > 0, i < n))
        def _():
            previous = carry_scr[...]
            carry_scr[...] = main(lead_i, consts, lead_o)
            tail(previous, lag_i, consts, lag_o, accs)

        @pl.when(i == n)
        def _():
            tail(carry_scr[...], lag_i, consts, lag_o, accs)

    def lead(width):
        return pl.BlockSpec((tm, width), lambda i: (jnp.minimum(i, n - 1), 0))

    def lag(width):
        return pl.BlockSpec((tm, width), lambda i: (jnp.maximum(i - 1, 0), 0))

    def whole(shape, **kw):
        nd = len(shape)
        return pl.BlockSpec(tuple(shape), lambda i: (0,) * nd, **kw)

    const_specs = [whole(a.shape, pipeline_mode=pl.Buffered(1)) for a in const_ins]
    return pl.pallas_call(
        kern, name=name, grid=(n + 1,),
        in_specs=[lead(a.shape[1]) for a in lead_ins] + [lag(a.shape[1]) for a in lag_ins] + const_specs,
        out_specs=[lead(s.shape[1]) for s in lead_outs] + [lag(s.shape[1]) for s in lag_outs] + [whole(s.shape) for s in acc_outs],
        out_shape=tuple(lead_outs) + tuple(lag_outs) + tuple(acc_outs),
        scratch_shapes=[pltpu.VMEM(carry, F32)], compiler_params=_cparams(1),
    )(*lead_ins, *lag_ins, *const_ins)


def _mlp_fwd(h2, x1, tgt, w_up, w_down_a, w_down_b, nw, tm=512):
    n_tok = x1.shape[0]

    def main(lead_i, consts, lead_o):
        (h2_ref,), (wup_ref, wda_ref, wdb_ref, _), (f_ref,) = lead_i, consts, lead_o
        h2v = h2_ref[...]
        acc = jnp.zeros((tm, D_MODEL), F32)
        for j in range(D_FF // FF_TILE):
            cs = slice(j * FF_TILE, (j + 1) * FF_TILE)
            u = jnp.dot(h2v, _up_cols(wup_ref, j), preferred_element_type=F32)
            f = jnp.square(jnp.maximum(u, 0.0)).astype(BF16)
            f_ref[:, cs] = f
            acc = acc + jnp.dot(f, _down_rows(wda_ref, wdb_ref, j), preferred_element_type=F32)
        return acc

    def tail(acc, lag_i, consts, lag_o, accs):
        (x1_ref, tgt_ref), nw_ref, (dd_ref, dy_ref), (loss_ref, dnw_ref) = lag_i, consts[3], lag_o, accs
        dn, r = _rms_fwd(acc, nw_ref[...])
        e = x1_ref[...] + dn - tgt_ref[...]
        loss_ref[...] += jnp.full(loss_ref.shape, (0.5 / D_MODEL) * jnp.sum(e * e), F32)
        dy = e * (1.0 / D_MODEL)
        dd, dnw = _rms_bwd(acc, r, nw_ref[...], dy)
        dy_ref[...] = dy
        dd_ref[...] = dd.astype(BF16)
        dnw_ref[...] += dnw

    return _skewed_rows_call(
        "mlp_fwd", main, tail, tm, [h2], [x1, tgt], [w_up, w_down_a, w_down_b, nw],
        [_sds((n_tok, D_FF), BF16)], [_sds((n_tok, D_MODEL), BF16), _sds((n_tok, D_MODEL), F32)],
        [_sds((8, 128), F32), _sds((1, D_MODEL), F32)], carry=(tm, D_MODEL))


def _mlp_bwd(dd, f, x1, dy, w_down_a, w_down_b, w_up, nw, tm=256):
    n_tok = x1.shape[0]

    def main(lead_i, consts, lead_o):
        (dd_ref, f_ref), (wda_ref, wdb_ref, wup_ref, _), (dup_ref,) = lead_i, consts, lead_o
        ddv = dd_ref[...]
        acc = jnp.zeros((tm, D_MODEL), F32)
        for j in range(D_FF // FF_TILE):
            cs = slice(j * FF_TILE, (j + 1) * FF_TILE)
            df = _dot_nt(ddv, _down_rows(wda_ref, wdb_ref, j))
            du = (df * (2.0 * jnp.sqrt(f_ref[:, cs].astype(F32)))).astype(BF16)
            dup_ref[:, cs] = du
            acc = acc + _dot_nt(du, _up_cols(wup_ref, j))
        return acc

    def tail(acc, lag_i, consts, lag_o, accs):
        (x1_ref, dy_ref), nw_ref, (dx1_ref,), (dnw_ref,) = lag_i, consts[3], lag_o, accs
        x1v = x1_ref[...]
        _, r = _rms_fwd(x1v, nw_ref[...])
        dx, dnw = _rms_bwd(x1v, r, nw_ref[...], acc)
        dx1_ref[...] = dy_ref[...] + dx
        dnw_ref[...] += dnw

    return _skewed_rows_call(
        "mlp_bwd", main, tail, tm, [dd, f], [x1, dy], [w_down_a, w_down_b, w_up, nw],
        [_sds((n_tok, D_FF), BF16)], [_sds((n_tok, D_MODEL), F32)], [_sds((1, D_MODEL), F32)], carry=(tm, D_MODEL))


def _outproj_bwd(dx1, o, w_out, nw, tm=256, carried=None):
    n_tok = dx1.shape[0]

    def body(dx1_ref, o_ref, wo_ref, nw_ref, do_ref, dya_ref, dyb_ref, dnw_ref):
        ov = o_ref[...]
        _, r = _rms_fwd(ov, nw_ref[...])
        do, dnw = _rms_bwd(ov, r, nw_ref[...], dx1_ref[...])
        dob = do.astype(BF16)
        do_ref[...] = dob
        dya_ref[...] = _dot_nt(dob, wo_ref[:GM_WIDTH, :])
        dyb_ref[...] = _dot_nt(dob, wo_ref[GM_WIDTH:, :])
        dnw_ref[...] += dnw

    return _rows_call("outproj_bwd", body, tm, [dx1, o], [w_out, nw],
                      [_sds((n_tok, D_MODEL), BF16), _sds((n_tok, GM_WIDTH), F32), _sds((n_tok, SSM_WIDTH), F32)],
                      [_sds((1, D_MODEL), F32)], carried=carried)


def _gmlp_bwd(p_uv, dya, lnw, lnb, e_bf, et_bf, w_cat, w_stack, bmap, carried=None):
    n_tok = p_uv.shape[0]
    chunks_per_step = 2

    def body(puv_ref, dya_ref, lnw_ref, lnb_ref, e_ref, et_ref, wcat_ref, wstack_ref, bmap_ref,
             dpuv_ref, dws_ref, dbs_ref, dlnw_ref, dlnb_ref, wm_scr, wsm_scr):
        t_stk = lax.broadcasted_iota(jnp.int32, (N_HEADS * CHUNK, CHUNK), 0) % CHUNK
        s_stk = lax.broadcasted_iota(jnp.int32, (N_HEADS * CHUNK, CHUNK), 1)

        @pl.when(pl.program_id(0) == 0)
        def _():
            wm_scr[...] = _causal_w_cat(wcat_ref[...])
            wsm_scr[...] = jnp.where(t_stk >= s_stk, wstack_ref[...], 0.0).astype(BF16)

        lnw_v = lnw_ref[...]
        e_v, et_v = e_ref[...], et_ref[...]

        def one_chunk(rows):
            u, v, gu, tu, tv, rstd, xhat, vn = _gmlp_common(puv_ref[rows, :], lnw_v, lnb_ref[...], e_v, et_v)
            vnb = vn.astype(BF16)
            mixed = jnp.dot(wm_scr[...], _head_blocks(vnb), preferred_element_type=F32) + bmap_ref[...]
            dy = dya_ref[rows, :]
            du = dy * mixed * _gelu_grad(u, tu)
            dmixed = dy * gu
            (dbs,) = _seg_dots([dmixed], et_v)
            dblocks = _head_blocks(dmixed.astype(BF16))
            dvn = lax.dot_general(wsm_scr[...], dblocks, (((0,), (0,)), ((), ())), preferred_element_type=F32)
            dws = lax.dot_general(dblocks, vnb, (((1,), (1,)), ((), ())), preferred_element_type=F32)
            dxh = dvn * lnw_v
            m1, m2 = _seg_dots([dxh, dxh * xhat], et_v)
            m1, m2 = _seg_dots([m1 * (1.0 / HEAD_DIM), m2 * (1.0 / HEAD_DIM)], e_v)
            dgv = rstd * (dxh - m1 - xhat * m2)
            dv = dgv * _gelu_grad(v, tv)
            dpuv_ref[rows, :GM_WIDTH] = du.astype(BF16)
            dpuv_ref[rows, GM_WIDTH:] = dv.astype(BF16)
            return dbs, dws, jnp.sum(dvn * xhat, axis=0, keepdims=True), jnp.sum(dvn, axis=0, keepdims=True)

        parts = [one_chunk(slice(k * CHUNK, (k + 1) * CHUNK)) for k in range(chunks_per_step)]
        dbs, dws, dlnw, dlnb = [functools.reduce(lambda a, b: a + b, vals) for vals in zip(*parts)]
        dbs_ref[...] += dbs
        dws_ref[...] += jnp.where(t_stk >= s_stk, dws, 0.0)
        dlnw_ref[...] += dlnw
        dlnb_ref[...] += dlnb

    return _rows_call(
        "gmlp_bwd", body, chunks_per_step * CHUNK, [p_uv, dya], [lnw, lnb, e_bf, et_bf, w_cat, w_stack, bmap],
        [_sds((n_tok, 2 * GM_WIDTH), BF16)],
        [_sds((N_HEADS * CHUNK, CHUNK), F32), _sds((CHUNK, DT_PAD), F32), _sds((1, GM_WIDTH), F32),
         _sds((1, GM_WIDTH), F32)],
        scratch=[pltpu.VMEM((CHUNK, N_HEADS * CHUNK), BF16), pltpu.VMEM((N_HEADS * CHUNK, CHUNK), BF16)],
        carried=carried)


def _ssd_bwd(p_xbc, p_z, p_dt, yssd, sprev, dyb, conv_w, conv_b, dt_bias, a_log, dskip_map, norm_w, e_bf, et_bf, n_seq,
             carried=None):
    n_tok = p_xbc.shape[0]
    nc = n_tok // n_seq // CHUNK

    def body(xr3, xprev3, z3, pdt3, yssd3, sprev3, dyb3,
             cw_ref, cb_ref, dtb_ref, alog_ref, dsk_ref, nw_ref, e_ref, et_ref,
             dpxbc3, dpz3, dpdt3, dcw_ref, dcb_ref, ddtb_ref, dalog_ref, ddsk_ref, dnw_ref,
             ds3_scr, nxt3_scr, dxa3_scr):
        @pl.when(pl.program_id(0) == 0)
        def _():
            for a in (dcw_ref, dcb_ref, ddtb_ref, dalog_ref, ddsk_ref, dnw_ref, ds3_scr, nxt3_scr):
                a[...] = jnp.zeros_like(a)

        for b in range(n_seq):
            one_sequence(xr3.at[b], xprev3.at[b], z3.at[b], pdt3.at[b], yssd3.at[b], sprev3.at[b], dyb3.at[b],
                         cw_ref, cb_ref, dtb_ref, alog_ref, dsk_ref, nw_ref, e_ref, et_ref,
                         dpxbc3.at[b], dpz3.at[b], dpdt3.at[b], dcw_ref, dcb_ref, ddtb_ref, dalog_ref, ddsk_ref, dnw_ref,
                         ds3_scr.at[b], nxt3_scr.at[b], dxa3_scr.at[b])

    def one_sequence(xr_ref, xprev_ref, z_ref, pdt_ref, yssd_ref, sprev_ref, dyb_ref,
                     cw_ref, cb_ref, dtb_ref, alog_ref, dsk_ref, nw_ref, e_ref, et_ref,
                     dpxbc_ref, dpz_ref, dpdt_ref, dcw_ref, dcb_ref, ddtb_ref, dalog_ref, ddsk_ref, dnw_ref,
                     ds_scr, nxt_scr, dxa_scr):
        chunk = nc - 1 - pl.program_id(0)
        xr = xr_ref[...]
        prev = jnp.where(chunk == 0, 0.0, xprev_ref[...])
        et_v = et_ref[...]
        p = _ssd_pre(xr, prev, cw_ref, cb_ref[...], pdt_ref[...], dtb_ref[...], alog_ref[...], e_ref[...])
        last, e_exp, dte, cd = _ssd_maps(p)
        rowi = p["rowi"]
        xs = p["xa"][:, :SSM_WIDTH]
        xd = xs * p["dt_map"]
        a_cs_t = p["a_cs"].T
        tri = _tril_mask()
        dsk = dsk_ref[...]
        nw_v = nw_ref[...]

        yv = yssd_ref[...]
        zv = z_ref[...]
        sz, zg, yg, _, rs = _gate_fwd(yv, zv, nw_v)
        dout = dyb_ref[...]
        for g in range(SSM_GROUPS):
            gs = slice(g * GROUP_W, (g + 1) * GROUP_W)
            dyg_g, dnw_g = _rms_bwd(yg[:, gs], rs[g], nw_v[:, gs], dout[:, gs])
            dnw_ref[:, gs] += dnw_g
            dxa_scr[:, gs] = dyg_g
        dyg = dxa_scr[:, :SSM_WIDTH]
        d_y = dyg * zg
        dpz_ref[...] = (dyg * yv * (sz + zv * sz * (1.0 - sz))).astype(BF16)

        s_prev = sprev_ref[...]
        ds_next = ds_scr[...]
        lane_dt = lax.broadcasted_iota(jnp.int32, (1, DT_PAD), 1)
        da_cols = jnp.zeros((CHUNK, DT_PAD), F32)
        for g in range(SSM_GROUPS):
            gs = slice(g * GROUP_W, (g + 1) * GROUP_W)
            b_off = SSM_WIDTH + g * SSM_STATE
            c_off = SSM_WIDTH + (SSM_GROUPS + g) * SSM_STATE
            bm = p["xa"][:, b_off:b_off + SSM_STATE].astype(BF16)
            cm = p["xa"][:, c_off:c_off + SSM_STATE].astype(BF16)
            cb_mat = _dot_nt(cm, bm)
            d_yg = d_y[:, gs]
            d_ygb = d_yg.astype(BF16)
            xdg = xd[:, gs]
            xdgb = xdg.astype(BF16)
            ds_g = ds_next[:, gs]
            sp_g = s_prev[:, gs]
            bds = _dot(bm, ds_g)
            dcs = d_yg * e_exp[:, gs]
            d_c = _dot_nt(dcs, sp_g)
            ds_scr[:, gs] = cd[:, gs] * ds_g + _dot_tn(cm, dcs)
            d_b = _dot_nt(xdg * dte[:, gs], ds_g)
            dxd_g = bds * dte[:, gs]
            sum_dcb = jnp.zeros((CHUNK, CHUNK), F32)
            for r in range(SSM_GROUPS * 2):
                head = g * 4 + r
                mask = _head_lane_mask(GROUP_W, r)
                dm = _head_decay(p["a_cs"], a_cs_t, head, tri)
                m_mat = cb_mat * dm
                g_mat = _dot_nt(jnp.where(mask, d_yg, 0.0), xdgb)
                w_mat = g_mat * m_mat
                sum_dcb = sum_dcb + g_mat * dm
                dxd_g = dxd_g + jnp.where(mask, _dot_tn(m_mat, d_ygb), 0.0)
                da_h = jnp.sum(w_mat - w_mat.T, axis=1, keepdims=True)
                da_cols = da_cols + jnp.where(lane_dt == head, da_h, 0.0)
            d_c = d_c + _dot(sum_dcb, bm)
            d_b = d_b + _dot_tn(sum_dcb, cm)
            dxa_scr[:, b_off:b_off + SSM_STATE] = d_b
            dxa_scr[:, c_off:c_off + SSM_STATE] = d_c
            y_off_g = _dot(cm, sp_g) * e_exp[:, gs]
            t3 = bds * xdg * dte[:, gs]
            tail = jnp.sum(t3, axis=0, keepdims=True) + jnp.sum(ds_g * sp_g, axis=0, keepdims=True) * cd[:, gs]
            pre_g = d_yg * y_off_g - t3 + jnp.where(last, tail, 0.0)
            s_pre, ddt_g, s_dsk = _seg_dots([pre_g, dxd_g * xs[:, gs], d_yg * xs[:, gs]], et_v[gs, :])
            da_cols = da_cols + s_pre
            ddsk_ref[...] += jnp.sum(s_dsk, axis=0, keepdims=True)
            dxa_scr[:, gs] = dxd_g * p["dt_map"][:, gs] + dsk[:, gs] * d_yg
            if g == 0:
                ddt = ddt_g
            else:
                ddt = ddt + ddt_g
        r_i = lax.broadcasted_iota(jnp.int32, (CHUNK, CHUNK), 0)
        c_i = lax.broadcasted_iota(jnp.int32, (CHUNK, CHUNK), 1)
        ddta = _tri_dot(r_i <= c_i, da_cols, terms=2)
        ddt = ddt + ddta * p["a_neg"]
        dalog_ref[...] += jnp.sum(ddta * p["dt"], axis=0, keepdims=True) * p["a_neg"]
        draw = ddt * _sigmoid(p["pre"])
        ddtb_ref[...] += jnp.sum(draw, axis=0, keepdims=True)
        dpdt_ref[...] = draw.astype(BF16)

        xc = p["xc"]
        sg = p["sg"]
        dxc = dxa_scr[...] * (sg + xc * sg * (1.0 - sg))
        dcb_ref[...] += jnp.sum(dxc, axis=0, keepdims=True)
        for k in range(CONV_K):
            dcw_ref[k] += jnp.sum(dxc * p["shifted"][k], axis=0, keepdims=True)
        nxt = nxt_scr[...]
        dxr = cw_ref[3] * dxc
        for s in range(1, CONV_K):
            dxr = dxr + cw_ref[CONV_K - 1 - s] * _shift_up(dxc, nxt, s)
        dpxbc_ref[...] = dxr.astype(BF16)
        nxt_scr[...] = dxc[:SUBLANES, :]

    seq_len = n_tok // n_seq

    def rows(width):
        return pl.BlockSpec((n_seq, CHUNK, width), lambda s: (0, nc - 1 - s, 0))

    tiles = CHUNK // SUBLANES
    prev_rows = pl.BlockSpec((n_seq, SUBLANES, CONV_CH), lambda s: (0, jnp.maximum((nc - 1 - s) * tiles - 1, 0), 0))

    def whole(shape):
        nd = len(shape)
        return pl.BlockSpec(tuple(shape), lambda s: (0,) * nd)

    def by_seq(a):
        return a.reshape(n_seq, seq_len, a.shape[-1])

    acc_shapes = [(CONV_K, 1, CONV_CH), (1, CONV_CH), (1, DT_PAD), (1, DT_PAD), (1, DT_PAD), (1, SSM_WIDTH)]
    xbc3 = by_seq(p_xbc)
    outs = _call_carrying(
        body, carried, name="ssd_bwd", grid=(nc,),
        in_specs=[rows(CONV_CH), prev_rows, rows(SSM_WIDTH), rows(DT_PAD), rows(SSM_WIDTH), rows(SSM_WIDTH),
                  rows(SSM_WIDTH)] + _ssd_const_specs(),
        out_specs=[rows(CONV_CH), rows(SSM_WIDTH), rows(DT_PAD)] + [whole(s) for s in acc_shapes],
        out_shape=tuple([_sds((n_seq, seq_len, CONV_CH), BF16), _sds((n_seq, seq_len, SSM_WIDTH), BF16),
                         _sds((n_seq, seq_len, DT_PAD), BF16)] + [_sds(s, F32) for s in acc_shapes]),
        scratch_shapes=[pltpu.VMEM((n_seq, SSM_STATE, SSM_WIDTH), F32), pltpu.VMEM((n_seq, SUBLANES, CONV_CH), F32),
                        pltpu.VMEM((n_seq, CHUNK, CONV_CH), F32)],
        operands=[xbc3, xbc3, by_seq(p_z), by_seq(p_dt), by_seq(yssd), by_seq(sprev), by_seq(dyb), conv_w, conv_b, dt_bias,
                  a_log, dskip_map, norm_w, e_bf, et_bf])
    return tuple(o.reshape(n_tok, o.shape[-1]) for o in outs[:3]) + tuple(outs[3:])


def _inproj_bwd(dp_uv, dp_xbc, dp_z, dp_dt, x, dx1, w_uv, w_xbc, w_z, w_dt, nw, tm=256, carried=None):
    n_tok = x.shape[0]

    def body(duv_ref, dxbc_ref, dz_ref, ddt_ref, x_ref, dx1_ref, wuv_ref, wxbc_ref, wz_ref, wdt_ref, nw_ref,
             gx_ref, h_ref, dnw_ref):
        dh = _dot_nt(duv_ref[...], wuv_ref[...]) + _dot_nt(dxbc_ref[...], wxbc_ref[...])
        dh = dh + _dot_nt(dz_ref[...], wz_ref[...]) + _dot_nt(ddt_ref[...], wdt_ref[...])
        xv = x_ref[...]
        h, r = _rms_fwd(xv, nw_ref[...])
        dx, dnw = _rms_bwd(xv, r, nw_ref[...], dh)
        gx_ref[...] = dx1_ref[...] + dx
        h_ref[...] = h.astype(BF16)
        dnw_ref[...] += dnw

    return _rows_call("inproj_bwd", body, tm, [dp_uv, dp_xbc, dp_z, dp_dt, x, dx1], [w_uv, w_xbc, w_z, w_dt, nw],
                      [_sds((n_tok, D_MODEL), F32), _sds((n_tok, D_MODEL), BF16)], [_sds((1, D_MODEL), F32)],
                      carried=carried)


def _const_maps():
    lane = jnp.arange(SSM_WIDTH) // HEAD_DIM
    e_bf = (jnp.arange(DT_PAD)[:, None] == lane[None, :]).astype(BF16)
    return e_bf, e_bf.T


def _pad_lanes(v, width):
    return jnp.pad(v, ((0, 0), (0, width - v.shape[1])))


SHARD_COLS = IN_COLS // N_CHIPS
_UV_END = 2 * GM_WIDTH
_Z_END = _UV_END + SSM_WIDTH
_XBC_END = _Z_END + CONV_CH


def _cols_from_shards(w4, lo, hi):
    pieces = []
    for j in range(N_CHIPS):
        a, b = max(lo, j * SHARD_COLS), min(hi, (j + 1) * SHARD_COLS)
        if a < b:
            pieces.append(w4[j][:, a - j * SHARD_COLS:b - j * SHARD_COLS])
    return pieces[0] if len(pieces) == 1 else jnp.concatenate(pieces, axis=1)


def _shards_from_cols(blocks):
    shards = []
    for j in range(N_CHIPS):
        pieces = []
        for arr, lo, hi in blocks:
            a, b = max(lo, j * SHARD_COLS), min(hi, (j + 1) * SHARD_COLS)
            if a < b:
                pieces.append(arr[:, a - lo:b - lo])
        shards.append(pieces[0] if len(pieces) == 1 else jnp.concatenate(pieces, axis=1))
    return jnp.stack(shards)


def _forward_backward(x, tgt, w_in4, conv_w, small, out_shard, up_shard, down_shard, core, adam_args):
    n_seq, seq_len, _ = x.shape
    n_tok = n_seq * seq_len
    x2 = x.reshape(n_tok, D_MODEL)
    tgt2 = tgt.reshape(n_tok, D_MODEL)
    e_bf, et_bf = _const_maps()

    w_uv = _cols_from_shards(w_in4, 0, _UV_END)
    w_z = _cols_from_shards(w_in4, _UV_END, _Z_END)
    w_xbc = _cols_from_shards(w_in4, _Z_END, _XBC_END)
    w_dt = _pad_lanes(_cols_from_shards(w_in4, _XBC_END, IN_COLS), DT_PAD)

    nw_pre = small["norm_mix_pre"]
    lnw = small["gm_ln_w"].reshape(1, GM_WIDTH)
    lnb = small["gm_ln_b"].reshape(1, GM_WIDTH)
    w_stack = small["gm_w_s"].reshape(N_HEADS * CHUNK, CHUNK)
    w_cat = jnp.transpose(small["gm_w_s"], (1, 0, 2)).reshape(CHUNK, N_HEADS * CHUNK)
    bmap = jnp.repeat(small["gm_b_s"].T, HEAD_DIM, axis=1)
    cw3 = conv_w.reshape(CONV_K, 1, CONV_CH)
    conv_b = small["conv_b"]
    dt_bias = _pad_lanes(small["dt_bias"], DT_PAD)
    a_log = _pad_lanes(small["a_log"], DT_PAD)
    dskip_map = jnp.repeat(small["d_skip"], HEAD_DIM, axis=1)
    ssm_nw = small["ssm_norm_w"]

    half = down_shard.shape[0] // 2
    p_uv, p_xbc, p_z, p_dt, w_out4, w_down_a = _inproj_fwd(
        x2, nw_pre, w_uv, w_xbc, w_z, w_dt, carried=_allgather_exchange([out_shard, down_shard[:half]]))
    ssd_consts = (cw3, conv_b, dt_bias, a_log, dskip_map, ssm_nw, e_bf, et_bf)
    ya, yb, yssd, sprev, w_up4, w_down_b = _mixer_fwd(
        p_uv, p_xbc, p_z, p_dt, lnw, lnb, w_cat, bmap, *ssd_consts, n_seq,
        carried=_allgather_exchange([up_shard, down_shard[half:]]))
    w_out_b = w_out4.reshape(D_MODEL, D_MODEL)
    o, x1, h2 = _outproj_fwd(ya, yb, x2, w_out_b, small["norm_mix_post"], small["norm_ffn_pre"])
    f, dd, dy, loss_acc, d_nffn_post = _mlp_fwd(h2, x1, tgt2, w_up4, w_down_a, w_down_b, small["norm_ffn_post"])

    dup, dx1, d_nffn_pre = _mlp_bwd(dd, f, x1, dy, w_down_a, w_down_b, w_up4, small["norm_ffn_pre"])
    tk = min(DW_TOKENS_PER_STEP, n_tok)
    g_up = _matmul_tn("dw_up", h2, dup, D_MODEL, D_MODEL, tk, stacked=True)
    g_down = _matmul_tn("dw_down", f, dd, 1024, D_MODEL, tk).reshape(N_CHIPS, D_FF // N_CHIPS, D_MODEL)
    do, dya, dyb, d_nmix_post, got_up, got_down = _outproj_bwd(
        dx1, o, w_out_b, small["norm_mix_post"], carried=_pair_exchange([g_up, g_down]))
    h_up = _pair_sum(core, g_up, got_up, 256)
    h_down = _pair_sum(core, g_down, got_down, 256)
    g_out_a = _matmul_tn("dw_out_a", ya, do, GM_WIDTH, D_MODEL, tk)
    g_out_b = _matmul_tn("dw_out_b", yb, do, SSM_WIDTH, D_MODEL, tk)
    g_out = jnp.concatenate([g_out_a, g_out_b], axis=0).reshape(N_CHIPS, D_MODEL // N_CHIPS, D_MODEL)
    dp_uv, d_ws, d_bs_t, d_lnw, d_lnb, slab_up, got_out = _gmlp_bwd(
        p_uv, dya, lnw, lnb, e_bf, et_bf, w_cat, w_stack, bmap,
        carried=_both(_chip_exchange([h_up]), _pair_exchange([g_out])))
    h_out = _pair_sum(core, g_out, got_out, 128)
    early = {
        "gm_ln_w": d_lnw.reshape(N_HEADS, HEAD_DIM), "gm_ln_b": d_lnb.reshape(N_HEADS, HEAD_DIM),
        "gm_w_s": d_ws.reshape(N_HEADS, CHUNK, CHUNK), "gm_b_s": d_bs_t[:, :N_HEADS].T,
        "norm_mix_post": d_nmix_post, "norm_ffn_pre": d_nffn_pre, "norm_ffn_post": d_nffn_post,
    }
    packed_early = _pack(early, tuple(early), tail=loss_acc[0, 0].reshape(1))
    (dp_xbc, dp_z, dp_dt, d_cw, d_cb, d_dtb, d_alog, d_dsk, d_ssm_nw, slab_down, slab_out, all_early) = _ssd_bwd(
        p_xbc, p_z, p_dt, yssd, sprev, dyb, *ssd_consts, n_seq,
        carried=_both(_chip_exchange([h_down, h_out]), _device_gather_exchange(packed_early)))
    gx, h, d_nmix_pre = _inproj_bwd(dp_uv, dp_xbc, dp_z, dp_dt, x2, dx1, w_uv, w_xbc, w_z, w_dt, nw_pre)
    late = {
        "norm_mix_pre": d_nmix_pre, "conv_w": d_cw.reshape(CONV_K, CONV_CH), "conv_b": d_cb,
        "dt_bias": d_dtb[:, :N_HEADS], "a_log": d_alog[:, :N_HEADS], "d_skip": d_dsk[:, :N_HEADS],
        "ssm_norm_w": d_ssm_nw,
    }
    g_uv, all_late = _matmul_tn("dw_in_uv", h, dp_uv, D_MODEL, 2 * GM_WIDTH, tk,
                                carried=_device_gather_exchange(_pack(late, tuple(late))))
    sum_early = _ordered_sum("small_sum_early", all_early)
    small_sum = _unpack(sum_early, {n: v.shape for n, v in early.items()}, tuple(early))
    small_sum.update(_unpack(_ordered_sum("small_sum_late", all_late), {n: v.shape for n, v in late.items()}, tuple(late)))
    loss = sum_early.reshape(-1)[sum(v.size for v in early.values())]
    g_xbc = _matmul_tn("dw_in_xbc", h, dp_xbc, D_MODEL, CONV_CH, tk)
    g_z = _matmul_tn("dw_in_z", h, dp_z, D_MODEL, SSM_WIDTH, tk)
    g_dt = _matmul_tn("dw_in_dt", h, dp_dt, D_MODEL, DT_PAD, tk)

    g_in = _shards_from_cols([(g_uv, 0, _UV_END), (g_z, _UV_END, _Z_END), (g_xbc, _Z_END, _XBC_END),
                              (g_dt, _XBC_END, IN_COLS)])

    red_up, red_down, red_out = _chip_sum(slab_up, 256), _chip_sum(slab_down, 256), _chip_sum(slab_out, 128)
    oth_up, oth_down, oth_out, got_in = _run_exchange(
        "grad_pair_swap", _both(_pair_swap([red_up, red_down, red_out]), _pair_exchange([g_in])))
    h_in = _pair_sum(core, g_in, got_in, 256)
    res = _adamw_halves("adamw_mlp", [(adam_args["w_up"][0], red_up, oth_up) + adam_args["w_up"][1:],
                                      (adam_args["w_down"][0], red_down, oth_down) + adam_args["w_down"][1:]],
                        256, carried=_chip_exchange([h_in]))
    big_out = {"w_up": res[0:4], "w_down": res[4:8]}
    big_out["w_out"] = _adamw_halves("adamw_w_out", [(adam_args["w_out"][0], red_out, oth_out) + adam_args["w_out"][1:]], 128)
    red_in = _chip_sum(res[8], 256)
    (oth_in,) = _run_exchange("grad_pair_swap_in", _pair_swap([red_in]))
    big_out["w_in"] = _adamw_halves("adamw_w_in", [(adam_args["w_in"][0], red_in, oth_in) + adam_args["w_in"][1:]], 256)

    return loss, gx.reshape(x.shape), big_out, small_sum


_HBM = pl.BlockSpec(memory_space=pltpu.HBM)


D2D_CHUNKS = 8
ICI_CHUNKS = 1
ROW_ALIGN = 16


def _row_chunks(rows, n_chunks):
    size = min(max(rows // n_chunks, ROW_ALIGN), rows)
    assert rows % size == 0
    return [(start, size) for start in range(0, rows, size)]


def _position():
    x, y, c = lax.axis_index("x"), lax.axis_index("y"), lax.axis_index("c")
    chips = [(1 - x, y), (x, 1 - y), (1 - x, 1 - y)]
    return x, y, c, chips


def _allgather_exchange(arrs):
    n = len(arrs)

    def copies(ins, outs, send_sems, recv_sems, local_sems):
        x, y, c, chips = _position()
        me = 2 * x + y
        sibling = (x, y, 1 - c)

        def copy(a, k, src, dst, to):
            return pltpu.make_async_remote_copy(src_ref=src, dst_ref=dst, send_sem=send_sems.at[a, k],
                                                recv_sem=recv_sems.at[a, k], device_id=to, device_id_type=MESH)

        def half_rows(a, pc):
            half = ins[a].shape[0] // 2
            return pl.ds(pc * half, half)

        local = [pltpu.make_async_copy(ins[a], outs[a].at[me], local_sems.at[a]) for a in range(n)]
        ici_out = [[copy(a, k, ins[a].at[half_rows(a, c)], outs[a].at[me, half_rows(a, c)], (px, py, c))
                    for k, (px, py) in enumerate(chips)] for a in range(n)]
        return c, chips, sibling, copy, half_rows, local, ici_out

    def start(ins, outs, send_sems, recv_sems, local_sems):
        c, chips, _, copy, _, local, _ = copies(ins, outs, send_sems, recv_sems, local_sems)
        x, y, _, _ = _position()
        me = 2 * x + y
        for cp in local:
            cp.start()
        for a in range(n):
            half = ins[a].shape[0] // 2
            for k, (px, py) in enumerate(chips):
                for first, size in _row_chunks(half, ICI_CHUNKS):
                    rows = pl.ds(c * half + first, size)
                    copy(a, k, ins[a].at[rows], outs[a].at[me, rows], (px, py, c)).start()

    def finish(ins, outs, send_sems, recv_sems, local_sems):
        c, chips, sibling, copy, half_rows, local, ici_out = copies(ins, outs, send_sems, recv_sems, local_sems)
        for a in range(n):
            half = ins[a].shape[0] // 2
            for k, (px, py) in enumerate(chips):
                blk = outs[a].at[2 * px + py, half_rows(a, c)]
                copy(a, k, blk, blk, (px, py, c)).wait_recv()
                for first, size in _row_chunks(half, D2D_CHUNKS):
                    piece = outs[a].at[2 * px + py, pl.ds(c * half + first, size)]
                    copy(a, 3 + k, piece, piece, sibling).start()
        for a in range(n):
            for k, (px, py) in enumerate(chips):
                theirs = outs[a].at[2 * px + py, half_rows(a, 1 - c)]
                copy(a, 3 + k, theirs, theirs, sibling).wait_recv()
                mine = outs[a].at[2 * px + py, half_rows(a, c)]
                copy(a, 3 + k, mine, mine, sibling).wait_send()
        for a in range(n):
            for cp in ici_out[a]:
                cp.wait_send()
        for cp in local:
            cp.wait()

    return _Carried(arrs, [_sds((N_CHIPS,) + a.shape, a.dtype) for a in arrs],
                    [pltpu.SemaphoreType.DMA((n, 6)), pltpu.SemaphoreType.DMA((n, 6)), pltpu.SemaphoreType.DMA((n,))],
                    start, finish)


def _run_exchange(name, exchange):
    n_in, n_out = len(exchange.ins), len(exchange.out_shapes)

    def body(*refs):
        ins, outs, sems = refs[:n_in], refs[n_in:n_in + n_out], refs[n_in + n_out:]
        exchange.start(ins, outs, *sems)
        exchange.finish(ins, outs, *sems)

    return pl.pallas_call(
        body, name=name, out_shape=tuple(exchange.out_shapes), in_specs=[_HBM] * n_in,
        out_specs=tuple([_HBM] * n_out), scratch_shapes=exchange.sems,
    )(*exchange.ins)


def _pair_exchange(grads):
    n = len(grads)

    def copier(send_sems, recv_sems):
        x, y, c, _ = _position()

        def copy(a, src, dst):
            return pltpu.make_async_remote_copy(src_ref=src, dst_ref=dst, send_sem=send_sems.at[a],
                                                recv_sem=recv_sems.at[a], device_id=(x, y, 1 - c), device_id_type=MESH)
        return c, copy

    def start(ins, got, send_sems, recv_sems):
        c, copy = copier(send_sems, recv_sems)
        for a in range(n):
            half = ins[a].shape[1] // 2
            for slab in range(N_CHIPS):
                for first, size in _row_chunks(half, D2D_CHUNKS):
                    copy(a, ins[a].at[slab, pl.ds((1 - c) * half + first, size), :],
                         got[a].at[slab, pl.ds(first, size), :]).start()

    def finish(ins, got, send_sems, recv_sems):
        c, copy = copier(send_sems, recv_sems)
        for a in range(n):
            half = ins[a].shape[1] // 2
            copy(a, ins[a].at[:, pl.ds((1 - c) * half, half), :], got[a]).wait()

    return _Carried(grads, [_sds((N_CHIPS, g.shape[1] // 2, g.shape[2]), g.dtype) for g in grads],
                    [pltpu.SemaphoreType.DMA((n,)), pltpu.SemaphoreType.DMA((n,))], start, finish)


def _chip_exchange(hsums):
    n = len(hsums)

    def copies(ins, outs, send_sems, recv_sems, local_sems, pieces):
        x, y, c, chips = _position()
        me = 2 * x + y
        cps = []
        for a in range(n):
            cps.append(pltpu.make_async_copy(ins[a].at[me], outs[a].at[me], local_sems.at[a]))
            rows = ins[a].shape[1]
            for k, (px, py) in enumerate(chips):
                for first, size in (_row_chunks(rows, ICI_CHUNKS) if pieces else [(0, rows)]):
                    cps.append(pltpu.make_async_remote_copy(
                        src_ref=ins[a].at[2 * px + py, pl.ds(first, size)], dst_ref=outs[a].at[me, pl.ds(first, size)],
                        send_sem=send_sems.at[a, k], recv_sem=recv_sems.at[a, k], device_id=(px, py, c),
                        device_id_type=MESH))
        return cps

    def start(*refs):
        for cp in copies(*refs, pieces=True):
            cp.start()

    def finish(*refs):
        for cp in copies(*refs, pieces=False):
            cp.wait()

    return _Carried(hsums, [_sds(h.shape, h.dtype) for h in hsums],
                    [pltpu.SemaphoreType.DMA((n, 3)), pltpu.SemaphoreType.DMA((n, 3)), pltpu.SemaphoreType.DMA((n,))],
                    start, finish)


def _pair_swap(reds):
    n = len(reds)

    def copier(send_sems, recv_sems):
        x, y, c, _ = _position()

        def copy(a, src, dst):
            return pltpu.make_async_remote_copy(src_ref=src, dst_ref=dst, send_sem=send_sems.at[a],
                                                recv_sem=recv_sems.at[a], device_id=(x, y, 1 - c), device_id_type=MESH)
        return copy

    def start(ins, outs, send_sems, recv_sems):
        copy = copier(send_sems, recv_sems)
        for a in range(n):
            for first, size in _row_chunks(ins[a].shape[0], 2 * D2D_CHUNKS):
                copy(a, ins[a].at[pl.ds(first, size), :], outs[a].at[pl.ds(first, size), :]).start()

    def finish(ins, outs, send_sems, recv_sems):
        copy = copier(send_sems, recv_sems)
        for a in range(n):
            copy(a, ins[a], outs[a]).wait()

    return _Carried(reds, [_sds(r.shape, r.dtype) for r in reds],
                    [pltpu.SemaphoreType.DMA((n,)), pltpu.SemaphoreType.DMA((n,))], start, finish)


def _device_gather_exchange(packed):
    def copies(ins, outs, send_sems, recv_sems, local_sem):
        (x_ref,), (all_ref,) = ins, outs
        x, y, c, chips = _position()
        me, sibling = (x, y, c), (x, y, 1 - c)

        def slab(px, py, pc):
            return all_ref.at[4 * px + 2 * py + pc]

        def copy(k, block, to, src=None):
            return pltpu.make_async_remote_copy(
                src_ref=slab(*block) if src is None else src, dst_ref=slab(*block), send_sem=send_sems.at[k],
                recv_sem=recv_sems.at[k], device_id=to, device_id_type=MESH)

        mine = pltpu.make_async_copy(x_ref, slab(*me), local_sem)
        first = [copy(0, me, sibling, src=x_ref)]
        first += [copy(1 + j, me, (*chip, c), src=x_ref) for j, chip in enumerate(chips)]
        passed = [copy(4 + j, (*chip, c), sibling) for j, chip in enumerate(chips)]
        return c, chips, me, sibling, copy, mine, first, passed

    def start(ins, outs, send_sems, recv_sems, local_sem):
        _, _, _, _, _, mine, first, _ = copies(ins, outs, send_sems, recv_sems, local_sem)
        mine.start()
        for cp in first:
            cp.start()

    def finish(ins, outs, send_sems, recv_sems, local_sem):
        c, chips, me, sibling, copy, mine, first, passed = copies(ins, outs, send_sems, recv_sems, local_sem)
        for j, chip in enumerate(chips):
            copy(1 + j, (*chip, c), me).wait_recv()
            passed[j].start()
        copy(0, sibling, me).wait_recv()
        for j, chip in enumerate(chips):
            copy(4 + j, (*chip, 1 - c), me).wait_recv()
        for cp in first + passed:
            cp.wait_send()
        mine.wait()

    return _Carried([packed], [_sds((N_DEV,) + packed.shape, F32)],
                    [pltpu.SemaphoreType.DMA((7,)), pltpu.SemaphoreType.DMA((7,)), pltpu.SemaphoreType.DMA],
                    start, finish)


def _ordered_sum(name, slabs):
    _, m_per, n_cols = slabs.shape

    def body(s_ref, o_ref):
        acc = s_ref[0]
        for d in range(1, N_DEV):
            acc = acc + s_ref[d]
        o_ref[...] = acc

    vmem = pl.BlockSpec(memory_space=pltpu.VMEM)
    return pl.pallas_call(body, name=name, out_shape=_sds((m_per, n_cols), F32), in_specs=[vmem], out_specs=vmem)(slabs)


def _pair_sum(core, own, got, tm):
    _, half, cols = got.shape
    nb = half // tm

    def body(c_ref, a_ref, b_ref, o_ref):
        o_ref[...] = (a_ref[...] + b_ref[...]).astype(BF16)

    return pl.pallas_call(
        body, name="grad_pair_sum", out_shape=_sds(got.shape, BF16),
        grid_spec=pltpu.PrefetchScalarGridSpec(
            num_scalar_prefetch=1, grid=(N_CHIPS, nb),
            in_specs=[pl.BlockSpec((None, tm, cols), lambda s, i, c_ref: (s, c_ref[0] * nb + i, 0)),
                      pl.BlockSpec((None, tm, cols), lambda s, i, c_ref: (s, i, 0))],
            out_specs=pl.BlockSpec((None, tm, cols), lambda s, i, c_ref: (s, i, 0))),
        compiler_params=_cparams(2),
    )(core, own, got)


def _chip_sum(slabs, tm):
    _, half, cols = slabs.shape

    def body(s_ref, o_ref):
        acc = s_ref[0].astype(F32)
        for k in range(1, N_CHIPS):
            acc = acc + s_ref[k].astype(F32)
        o_ref[...] = acc

    return pl.pallas_call(
        body, name="grad_chip_sum", out_shape=_sds((half, cols), F32), grid=(half // tm,),
        in_specs=[pl.BlockSpec((N_CHIPS, tm, cols), lambda i: (0, i, 0))],
        out_specs=pl.BlockSpec((tm, cols), lambda i: (i, 0)), compiler_params=_cparams(1),
    )(slabs)


def _adam_math(w, g, m, v):
    m2 = ADAM_B1 * m + (1.0 - ADAM_B1) * g
    v2 = ADAM_B2 * v + (1.0 - ADAM_B2) * (g * g)
    m_hat = m2 / (1.0 - ADAM_B1 ** ADAM_STEP)
    v_hat = v2 / (1.0 - ADAM_B2 ** ADAM_STEP)
    delta = -ADAM_LR * (m_hat / (jnp.sqrt(v_hat) + ADAM_EPS) + ADAM_WD * w)
    return delta, m2, v2


def _adamw_halves(name, items, tm, carried=None):
    rows, cols = items[0][0].shape
    nb = rows // 2 // tm
    n = len(items)

    def body(*refs):
        mine = (pl.program_id(0) // nb) == lax.axis_index("c")
        for k in range(n):
            w_ref, own_ref, oth_ref, m_ref, v_ref = refs[5 * k:5 * k + 5]
            g_ref, d_ref, m2_ref, v2_ref = refs[5 * n + 4 * k:5 * n + 4 * k + 4]
            g = jnp.where(mine, own_ref[...], oth_ref[...])
            d, m2, v2 = _adam_math(w_ref[...], g, m_ref[...], v_ref[...])
            g_ref[...] = g
            d_ref[...] = d
            m2_ref[...] = m2
            v2_ref[...] = v2

    full = pl.BlockSpec((tm, cols), lambda i: (i, 0))
    half = pl.BlockSpec((tm, cols), lambda i: (i % nb, 0))
    return _call_carrying(
        body, carried, name=name, grid=(rows // tm,), in_specs=[full, half, half, full, full] * n,
        out_specs=[full] * (4 * n), out_shape=tuple([_sds((rows, cols), F32)] * (4 * n)), scratch_shapes=[],
        operands=[a for item in items for a in item])


def _adamw(name, w, g, m, v, tm):
    def body(w_ref, g_ref, m_ref, v_ref, gout_ref, d_ref, m2_ref, v2_ref):
        gv = g_ref[...]
        d, m2, v2 = _adam_math(w_ref[...], gv, m_ref[...], v_ref[...])
        gout_ref[...] = gv
        d_ref[...] = d
        m2_ref[...] = m2
        v2_ref[...] = v2

    return _rows_call(name, body, tm, [w, g, m, v], [], [_sds(w.shape, F32)] * 4)


_SMALL_NAMES = ("norm_mix_pre", "gm_ln_w", "gm_ln_b", "gm_w_s", "gm_b_s", "conv_w", "conv_b", "dt_bias", "a_log",
                "d_skip", "ssm_norm_w", "norm_mix_post", "norm_ffn_pre", "norm_ffn_post")
_PACK_COLS = 1024


def _pack(parts, names=_SMALL_NAMES, tail=None):
    pieces = [parts[n].reshape(-1) for n in names]
    flat = jnp.concatenate(pieces if tail is None else pieces + [tail])
    rows = -(-flat.shape[0] // (8 * _PACK_COLS)) * 8
    flat = jnp.pad(flat, (0, rows * _PACK_COLS - flat.shape[0]))
    return flat.reshape(rows, _PACK_COLS)


def _unpack(packed, shapes, names=_SMALL_NAMES):
    flat = packed.reshape(-1)
    out, off = {}, 0
    for n in names:
        size = 1
        for s in shapes[n]:
            size *= s
        out[n] = flat[off:off + size].reshape(shapes[n])
        off += size
    return out


def kernel(x, norm_mix_pre, w_in, gm_ln_w, gm_ln_b, gm_w_s, gm_b_s, conv_w, conv_b, dt_bias, a_log, d_skip, ssm_norm_w, w_out, norm_mix_post, norm_ffn_pre, w_up, w_down, norm_ffn_post, loss_target, m_norm_mix_pre, m_w_in, m_gm_ln_w, m_gm_ln_b, m_gm_w_s, m_gm_b_s, m_conv_w, m_conv_b, m_dt_bias, m_a_log, m_d_skip, m_ssm_norm_w, m_w_out, m_norm_mix_post, m_norm_ffn_pre, m_w_up, m_w_down, m_norm_ffn_post, v_norm_mix_pre, v_w_in, v_gm_ln_w, v_gm_ln_b, v_gm_w_s, v_gm_b_s, v_conv_w, v_conv_b, v_dt_bias, v_a_log, v_d_skip, v_ssm_norm_w, v_w_out, v_norm_mix_post, v_norm_ffn_pre, v_w_up, v_w_down, v_norm_ffn_post):
    params = dict(norm_mix_pre=norm_mix_pre, w_in=w_in, gm_ln_w=gm_ln_w, gm_ln_b=gm_ln_b, gm_w_s=gm_w_s, gm_b_s=gm_b_s,
                  conv_w=conv_w, conv_b=conv_b, dt_bias=dt_bias, a_log=a_log, d_skip=d_skip, ssm_norm_w=ssm_norm_w,
                  w_out=w_out, norm_mix_post=norm_mix_post, norm_ffn_pre=norm_ffn_pre, w_up=w_up, w_down=w_down,
                  norm_ffn_post=norm_ffn_post)
    mom1 = dict(norm_mix_pre=m_norm_mix_pre, w_in=m_w_in, gm_ln_w=m_gm_ln_w, gm_ln_b=m_gm_ln_b, gm_w_s=m_gm_w_s,
                gm_b_s=m_gm_b_s, conv_w=m_conv_w, conv_b=m_conv_b, dt_bias=m_dt_bias, a_log=m_a_log, d_skip=m_d_skip,
                ssm_norm_w=m_ssm_norm_w, w_out=m_w_out, norm_mix_post=m_norm_mix_post, norm_ffn_pre=m_norm_ffn_pre,
                w_up=m_w_up, w_down=m_w_down, norm_ffn_post=m_norm_ffn_post)
    mom2 = dict(norm_mix_pre=v_norm_mix_pre, w_in=v_w_in, gm_ln_w=v_gm_ln_w, gm_ln_b=v_gm_ln_b, gm_w_s=v_gm_w_s,
                gm_b_s=v_gm_b_s, conv_w=v_conv_w, conv_b=v_conv_b, dt_bias=v_dt_bias, a_log=v_a_log, d_skip=v_d_skip,
                ssm_norm_w=v_ssm_norm_w, w_out=v_w_out, norm_mix_post=v_norm_mix_post, norm_ffn_pre=v_norm_ffn_pre,
                w_up=v_w_up, w_down=v_w_down, norm_ffn_post=v_norm_ffn_post)
    names = list(params)
    big = ("w_in", "w_out", "w_up", "w_down")
    chip = 2 * lax.axis_index("x") + lax.axis_index("y")

    shards = {n: params[n][0].astype(BF16) for n in big}
    conv_shard = jnp.pad(conv_w[0], ((0, 16 - CONV_K), (0, 0)))
    g_in4, g_conv4 = _run_exchange("allgather_w_in", _allgather_exchange([shards["w_in"], conv_shard]))
    conv_full = jnp.transpose(g_conv4[:, :CONV_K, :], (1, 0, 2)).reshape(CONV_K, CONV_CH)

    small = {n: params[n][0] if params[n].ndim >= 3 else params[n] for n in _SMALL_NAMES if n != "conv_w"}
    core = lax.axis_index("c").astype(jnp.int32).reshape(1)
    adam_args = {n: (params[n][0], mom1[n][0], mom2[n][0]) for n in big}
    loss, grad_x, big_out, small_sum = _forward_backward(
        x, loss_target, g_in4, conv_full, small, shards["w_out"], shards["w_up"], shards["w_down"], core, adam_args)
    grads, delta, new_m, new_v = {}, {}, {}, {}
    for n in big:
        grads[n], delta[n], new_m[n], new_v[n] = [a[None] for a in big_out[n]]

    small_sum["conv_w"] = lax.dynamic_slice_in_dim(small_sum["conv_w"], chip * (CONV_CH // N_CHIPS), CONV_CH // N_CHIPS, axis=1)

    local_shapes = {n: params[n].shape[1:] if params[n].ndim >= 3 else params[n].shape for n in _SMALL_NAMES}
    flat = lambda tree: {n: tree[n].reshape(local_shapes[n]) for n in _SMALL_NAMES}
    packed = [_pack(flat(t)) for t in (params, small_sum, mom1, mom2)]
    _, d_p, m_p, v_p = _adamw("adamw_small", *packed, packed[0].shape[0])
    for src, dst in ((d_p, delta), (m_p, new_m), (v_p, new_v)):
        for n, val in _unpack(src, local_shapes).items():
            dst[n] = val.reshape(params[n].shape)
    for n in _SMALL_NAMES:
        grads[n] = small_sum[n].reshape(params[n].shape)

    out = [loss, grad_x]
    for tree in (grads, delta, new_m, new_v):
        out += [tree[n] for n in names]
    return tuple(out)
```

```python
import functools

import jax
import jax.numpy as jnp
from jax import lax
from jax.experimental import pallas as pl
from jax.experimental.pallas import tpu as pltpu

F32 = jnp.float32
BF16 = jnp.bfloat16
HI = lax.Precision.HIGHEST
MESH = pl.DeviceIdType.MESH

EPS = 1e-6
D_MODEL = 1024
GM_WIDTH = 512
SSM_WIDTH = 512
N_HEADS = 8
HEAD_DIM = 64
CHUNK = 128
SSM_GROUPS = 2
GROUP_W = SSM_WIDTH // SSM_GROUPS
SSM_STATE = 128
CONV_K = 4
CONV_CH = 1024
D_FF = 4096
IN_COLS = 2568
DT_PAD = 128
N_CHIPS = 4
N_DEV = 8

ADAM_LR = 0.001
ADAM_B1 = 0.9
ADAM_B2 = 0.999
ADAM_EPS = 1e-08
ADAM_WD = 0.01
ADAM_STEP = 10

VMEM_LIMIT_BYTES = 56 * 1024 * 1024
FF_TILE = 512
DW_TOKENS_PER_STEP = 2048


def _cparams(n_axes):
    return pltpu.CompilerParams(dimension_semantics=("arbitrary",) * n_axes, vmem_limit_bytes=VMEM_LIMIT_BYTES)


def _dot(a, b):
    return jnp.dot(a.astype(BF16), b.astype(BF16), preferred_element_type=F32)


def _dot_nt(a, b):
    return lax.dot_general(a.astype(BF16), b.astype(BF16), (((1,), (1,)), ((), ())), preferred_element_type=F32)


def _dot_tn(a, b):
    return lax.dot_general(a.astype(BF16), b.astype(BF16), (((0,), (0,)), ((), ())), preferred_element_type=F32)


def _sigmoid(x):
    return 1.0 / (1.0 + jnp.exp(-x))


_GELU_C = 0.7978845608028654
_GELU_A = 0.044715


def _gelu(x):
    t = jnp.tanh(_GELU_C * (x + _GELU_A * (x * x * x)))
    return 0.5 * x * (1.0 + t), t


def _gelu_grad(x, t):
    return 0.5 * (1.0 + t) + 0.5 * x * (1.0 - t * t) * (_GELU_C * (1.0 + 3.0 * _GELU_A * x * x))


def _rms_fwd(x, w):
    r = lax.rsqrt(jnp.mean(x * x, axis=-1, keepdims=True) + EPS)
    return x * r * w, r


def _rms_bwd(x, r, w, dy):
    g = dy * w
    dx = r * g - x * (r * r * r) * jnp.mean(g * x, axis=-1, keepdims=True)
    dw = jnp.sum(dy * x * r, axis=0, keepdims=True)
    return dx, dw


class _Carried:
    def __init__(self, ins, out_shapes, sems, start, finish):
        self.ins, self.out_shapes, self.sems = list(ins), list(out_shapes), list(sems)
        self.start, self.finish = start, finish


def _both(first, second):
    n_i, n_o, n_s = len(first.ins), len(first.out_shapes), len(first.sems)

    def split(ins, outs, sems):
        return (ins[:n_i], outs[:n_o], sems[:n_s]), (ins[n_i:], outs[n_o:], sems[n_s:])

    def start(ins, outs, *sems):
        (i1, o1, s1), (i2, o2, s2) = split(ins, outs, sems)
        first.start(i1, o1, *s1)
        second.start(i2, o2, *s2)

    def finish(ins, outs, *sems):
        (i1, o1, s1), (i2, o2, s2) = split(ins, outs, sems)
        first.finish(i1, o1, *s1)
        second.finish(i2, o2, *s2)

    return _Carried(first.ins + second.ins, first.out_shapes + second.out_shapes, first.sems + second.sems, start, finish)


def _split_carried(refs, n_in, n_out, n_scratch, carried):
    n_ci, n_co, n_cs = len(carried.ins), len(carried.out_shapes), len(carried.sems)
    ins, rest = refs[:n_in], refs[n_in:]
    c_ins, rest = rest[:n_ci], rest[n_ci:]
    outs, rest = rest[:n_out], rest[n_out:]
    c_outs, rest = rest[:n_co], rest[n_co:]
    scr, c_sems = rest[:n_scratch], rest[n_scratch:]
    assert len(c_sems) == n_cs
    return tuple(ins) + tuple(outs) + tuple(scr), c_ins, c_outs, c_sems


def _rows_call(name, body, tm, row_ins, const_ins, row_outs, acc_outs=(), scratch=(), carried=None):
    n_rows = row_ins[0].shape[0]
    assert n_rows % tm == 0
    n_steps = n_rows // tm
    n_in = len(row_ins) + len(const_ins)
    n_ro = len(row_outs)
    n_acc = len(acc_outs)

    def kern(*refs):
        accs = refs[n_in + n_ro:n_in + n_ro + n_acc]

        @pl.when(pl.program_id(0) == 0)
        def _():
            for a in accs:
                a[...] = jnp.zeros_like(a)

        body(*refs)

    def whole(shape):
        nd = len(shape)
        return pl.BlockSpec(tuple(shape), lambda i: (0,) * nd)

    in_specs = [pl.BlockSpec((tm, a.shape[1]), lambda i: (i, 0)) for a in row_ins]
    in_specs += [whole(a.shape) for a in const_ins]
    out_specs = [pl.BlockSpec((tm, s.shape[1]), lambda i: (i, 0)) for s in row_outs]
    out_specs += [whole(s.shape) for s in acc_outs]
    return _call_carrying(
        kern, carried, name=name, grid=(n_steps,), in_specs=in_specs, out_specs=out_specs,
        out_shape=tuple(row_outs) + tuple(acc_outs), scratch_shapes=list(scratch), operands=list(row_ins) + list(const_ins))


def _call_carrying(body, carried, *, name, grid, in_specs, out_specs, out_shape, scratch_shapes, operands):
    n_in, n_out, n_scratch = len(in_specs), len(out_specs), len(scratch_shapes)
    kern = body
    if carried is not None:
        def kern(*refs):
            plain, c_ins, c_outs, c_sems = _split_carried(refs, n_in, n_out, n_scratch, carried)
            first, last = True, True
            for d, size in enumerate(grid):
                first = jnp.logical_and(first, pl.program_id(d) == 0)
                last = jnp.logical_and(last, pl.program_id(d) == size - 1)

            @pl.when(first)
            def _():
                carried.start(c_ins, c_outs, *c_sems)

            body(*plain)

            @pl.when(last)
            def _():
                carried.finish(c_ins, c_outs, *c_sems)

        in_specs = list(in_specs) + [_HBM] * len(carried.ins)
        out_specs = list(out_specs) + [_HBM] * len(carried.out_shapes)
        out_shape = tuple(out_shape) + tuple(carried.out_shapes)
        operands = list(operands) + carried.ins
        scratch_shapes = list(scratch_shapes) + carried.sems
    return pl.pallas_call(
        kern, name=name, grid=grid, in_specs=in_specs, out_specs=out_specs, out_shape=out_shape,
        scratch_shapes=scratch_shapes, compiler_params=_cparams(len(grid)),
    )(*operands)


def _sds(shape, dtype):
    return jax.ShapeDtypeStruct(tuple(shape), dtype)


def _matmul_tn(name, a, b, tm, tn, tk, stacked=False, carried=None):
    k_dim, m_dim = a.shape
    n_dim = b.shape[1]
    assert m_dim % tm == 0 and n_dim % tn == 0 and k_dim % tk == 0

    def kern(a_ref, b_ref, o_ref):
        @pl.when(pl.program_id(2) == 0)
        def _():
            o_ref[...] = jnp.zeros_like(o_ref)

        o_ref[...] += _dot_tn(a_ref[...], b_ref[...])

    if stacked:
        assert tm == m_dim
        out_shape = _sds((n_dim // tn, m_dim, tn), F32)
        out_spec = pl.BlockSpec((None, tm, tn), lambda i, j, k: (j, i, 0))
    else:
        out_shape = _sds((m_dim, n_dim), F32)
        out_spec = pl.BlockSpec((tm, tn), lambda i, j, k: (i, j))
    outs = _call_carrying(
        kern, carried, name=name, grid=(m_dim // tm, n_dim // tn, k_dim // tk),
        in_specs=[pl.BlockSpec((tk, tm), lambda i, j, k: (k, i)), pl.BlockSpec((tk, tn), lambda i, j, k: (k, j))],
        out_specs=[out_spec], out_shape=(out_shape,), scratch_shapes=[], operands=[a, b])
    return outs[0] if carried is None else outs


def _inproj_fwd(x, nw, w_uv, w_xbc, w_z, w_dt, tm=256, carried=None):
    n_tok = x.shape[0]

    def body(x_ref, nw_ref, wuv_ref, wxbc_ref, wz_ref, wdt_ref, puv_ref, pxbc_ref, pz_ref, pdt_ref):
        h, _ = _rms_fwd(x_ref[...], nw_ref[...])
        h = h.astype(BF16)
        puv_ref[...] = jnp.dot(h, wuv_ref[...], preferred_element_type=F32)
        pxbc_ref[...] = jnp.dot(h, wxbc_ref[...], preferred_element_type=F32)
        pz_ref[...] = jnp.dot(h, wz_ref[...], preferred_element_type=F32)
        pdt_ref[...] = jnp.dot(h, wdt_ref[...], preferred_element_type=F32)

    return _rows_call(
        "inproj_fwd", body, tm, [x], [nw, w_uv, w_xbc, w_z, w_dt],
        [_sds((n_tok, 2 * GM_WIDTH), F32), _sds((n_tok, CONV_CH), F32), _sds((n_tok, SSM_WIDTH), F32),
         _sds((n_tok, DT_PAD), F32)], carried=carried)


def _head_lane_mask(width, head):
    lane = lax.broadcasted_iota(jnp.int32, (1, width), 1)
    return (lane // HEAD_DIM) == head


def _split_terms(x, terms):
    parts = []
    for _ in range(terms):
        p = x.astype(BF16)
        parts.append(p)
        x = x - p.astype(F32)
    return parts


def _seg_dots(vals, ind, terms=2):
    m = vals[0].shape[0]
    parts = []
    for v in vals:
        parts += _split_terms(v, terms)
    red = jnp.dot(jnp.concatenate(parts, axis=0), ind, preferred_element_type=F32)
    outs = []
    for i in range(len(vals)):
        acc = red[i * terms * m:(i * terms + 1) * m]
        for t in range(1, terms):
            acc = acc + red[(i * terms + t) * m:(i * terms + t + 1) * m]
        outs.append(acc)
    return outs


def _tri_dot(mask, x, terms=3):
    n = x.shape[1]
    red = jnp.dot(mask.astype(BF16), jnp.concatenate(_split_terms(x, terms), axis=1), preferred_element_type=F32)
    acc = red[:, :n]
    for t in range(1, terms):
        acc = acc + red[:, t * n:(t + 1) * n]
    return acc


def _gmlp_common(puv, lnw, lnb, e_bf, et_bf):
    u = puv[:, :GM_WIDTH]
    v = puv[:, GM_WIDTH:]
    gu, tu = _gelu(u)
    gv, tv = _gelu(v)
    (s1,) = _seg_dots([gv], et_bf)
    (mu,) = _seg_dots([s1 * (1.0 / HEAD_DIM)], e_bf)
    xc = gv - mu
    (s2,) = _seg_dots([xc * xc], et_bf)
    (rstd,) = _seg_dots([lax.rsqrt(s2 * (1.0 / HEAD_DIM) + EPS)], e_bf)
    xhat = xc * rstd
    vn = xhat * lnw + lnb
    return u, v, gu, tu, tv, rstd, xhat, vn


def _tril_mask():
    r = lax.broadcasted_iota(jnp.int32, (CHUNK, CHUNK), 0)
    c = lax.broadcasted_iota(jnp.int32, (CHUNK, CHUNK), 1)
    return r >= c


def _head_blocks(v):
    return jnp.concatenate([jnp.where(_head_lane_mask(GM_WIDTH, h), v, jnp.zeros_like(v)) for h in range(N_HEADS)], axis=0)


def _causal_w_cat(w_cat):
    t = lax.broadcasted_iota(jnp.int32, (CHUNK, N_HEADS * CHUNK), 0)
    s = lax.broadcasted_iota(jnp.int32, (CHUNK, N_HEADS * CHUNK), 1) % CHUNK
    return jnp.where(t >= s, w_cat, 0.0).astype(BF16)


def _gmlp_chunk_fwd(puv, lnw, lnb, e_bf, et_bf, wm, bmap):
    _, _, gu, _, _, _, _, vn = _gmlp_common(puv, lnw, lnb, e_bf, et_bf)
    mixed = jnp.dot(wm, _head_blocks(vn.astype(BF16)), preferred_element_type=F32) + bmap
    return (gu * mixed).astype(BF16)


SUBLANES = 8


def _shift_down(x, tail, s):
    main = pltpu.roll(x, s, 0)
    row = lax.broadcasted_iota(jnp.int32, (SUBLANES, 1), 0)
    head = jnp.where(row < s, pltpu.roll(tail, s, 0), main[:SUBLANES])
    return jnp.concatenate([head, main[SUBLANES:]], axis=0)


def _shift_up(x, head_next, s):
    n = x.shape[0]
    main = pltpu.roll(x, n - s, 0)
    row = lax.broadcasted_iota(jnp.int32, (SUBLANES, 1), 0)
    last = jnp.where(row >= SUBLANES - s, pltpu.roll(head_next, SUBLANES - s, 0), main[n - SUBLANES:])
    return jnp.concatenate([main[:n - SUBLANES], last], axis=0)


def _ssd_pre(xr, tail, cw_ref, cb, pdt, dtb, alog, emap):
    rowi = lax.broadcasted_iota(jnp.int32, (CHUNK, 1), 0)
    shifted = [_shift_down(xr, tail, 3), _shift_down(xr, tail, 2), _shift_down(xr, tail, 1), xr]
    xc = cb
    for k in range(CONV_K):
        xc = xc + cw_ref[k] * shifted[k]
    sg = _sigmoid(xc)
    xa = xc * sg
    pre = pdt + dtb
    dt = jnp.maximum(pre, 0.0) + jnp.log(1.0 + jnp.exp(-jnp.abs(pre)))
    a_neg = -jnp.exp(alog)
    a_cs = _tri_dot(_tril_mask(), dt * a_neg)
    acs_map, dt_map = _seg_dots([a_cs, dt], emap, terms=3)
    return dict(shifted=shifted, xc=xc, sg=sg, xa=xa, pre=pre, dt=dt, a_neg=a_neg, a_cs=a_cs,
                acs_map=acs_map, dt_map=dt_map, rowi=rowi)


def _ssd_maps(p):
    last = p["rowi"] == CHUNK - 1
    aq_map = jnp.sum(jnp.where(last, p["acs_map"], 0.0), axis=0, keepdims=True)
    e_exp = jnp.exp(p["acs_map"])
    dte = jnp.exp(aq_map - p["acs_map"])
    cd = jnp.exp(aq_map)
    return last, e_exp, dte, cd


def _head_decay(a_cs, a_cs_t, head, tri):
    lane = lax.broadcasted_iota(jnp.int32, (1, DT_PAD), 1)
    sub = lax.broadcasted_iota(jnp.int32, (DT_PAD, 1), 0)
    col = jnp.sum(jnp.where(lane == head, a_cs, 0.0), axis=1, keepdims=True)
    row = jnp.sum(jnp.where(sub == head, a_cs_t, 0.0), axis=0, keepdims=True)
    return jnp.exp(jnp.where(tri, col - row, -1e30))


def _gate_fwd(y, z, nw):
    sz = _sigmoid(z)
    zg = z * sz
    yg = y * zg
    outs, rs = [], []
    for g in range(SSM_GROUPS):
        gs = slice(g * GROUP_W, (g + 1) * GROUP_W)
        o, r = _rms_fwd(yg[:, gs], nw[:, gs])
        outs.append(o)
        rs.append(r)
    return sz, zg, yg, outs, rs


def _ssd_const_specs():
    def whole(shape):
        nd = len(shape)
        return pl.BlockSpec(tuple(shape), lambda c: (0,) * nd)
    return [whole((CONV_K, 1, CONV_CH)), whole((1, CONV_CH)), whole((1, DT_PAD)), whole((1, DT_PAD)),
            whole((1, SSM_WIDTH)), whole((1, SSM_WIDTH)), whole((DT_PAD, SSM_WIDTH)), whole((SSM_WIDTH, DT_PAD))]


def _mixer_fwd(p_uv, p_xbc, p_z, p_dt, lnw, lnb, w_cat, bmap, conv_w, conv_b, dt_bias, a_log, dskip_map, norm_w,
               e_bf, et_bf, n_seq, carried=None):
    n_tok = p_xbc.shape[0]
    nc = n_tok // n_seq // CHUNK

    def body(puv3, xr3, z3, pdt3, lnw_ref, lnb_ref, wcat_ref, bmap_ref,
             cw_ref, cb_ref, dtb_ref, alog_ref, dsk_ref, nw_ref, e_ref, et_ref,
             ya3, yb3, yssd3, sprev3, wm_scr, prev3_scr, s3_scr):
        @pl.when(pl.program_id(0) == 0)
        def _():
            wm_scr[...] = _causal_w_cat(wcat_ref[...])
            prev3_scr[...] = jnp.zeros_like(prev3_scr)
            s3_scr[...] = jnp.zeros_like(s3_scr)

        for b in range(n_seq):
            one_sequence(puv3.at[b], xr3.at[b], z3.at[b], pdt3.at[b], lnw_ref, lnb_ref, bmap_ref,
                         cw_ref, cb_ref, dtb_ref, alog_ref, dsk_ref, nw_ref, e_ref, et_ref,
                         ya3.at[b], yb3.at[b], yssd3.at[b], sprev3.at[b], wm_scr, prev3_scr.at[b], s3_scr.at[b])

    def one_sequence(puv_ref, xr_ref, z_ref, pdt_ref, lnw_ref, lnb_ref, bmap_ref,
                     cw_ref, cb_ref, dtb_ref, alog_ref, dsk_ref, nw_ref, e_ref, et_ref,
                     ya_ref, yb_ref, yssd_ref, sprev_ref, wm_scr, prev_scr, s_scr):
        ya_ref[...] = _gmlp_chunk_fwd(puv_ref[...], lnw_ref[...], lnb_ref[...], e_ref[...], et_ref[...], wm_scr[...],
                                      bmap_ref[...])
        xr = xr_ref[...]
        p = _ssd_pre(xr, prev_scr[...], cw_ref, cb_ref[...], pdt_ref[...], dtb_ref[...], alog_ref[...], e_ref[...])
        _, e_exp, dte, cd = _ssd_maps(p)
        xs = p["xa"][:, :SSM_WIDTH]
        xd = xs * p["dt_map"]
        a_cs_t = p["a_cs"].T
        tri = _tril_mask()
        s_old = s_scr[...]
        sprev_ref[...] = s_old
        for g in range(SSM_GROUPS):
            gs = slice(g * GROUP_W, (g + 1) * GROUP_W)
            bm = p["xa"][:, SSM_WIDTH + g * SSM_STATE: SSM_WIDTH + (g + 1) * SSM_STATE].astype(BF16)
            cm = p["xa"][:, SSM_WIDTH + (SSM_GROUPS + g) * SSM_STATE: SSM_WIDTH + (SSM_GROUPS + g + 1) * SSM_STATE].astype(BF16)
            cb_mat = _dot_nt(cm, bm)
            xdg = xd[:, gs].astype(BF16)
            y_g = _dot(cm, s_old[:, gs]) * e_exp[:, gs] + dsk_ref[:, gs] * xs[:, gs]
            for r in range(SSM_GROUPS * 2):
                dm = _head_decay(p["a_cs"], a_cs_t, g * 4 + r, tri)
                full = jnp.dot((cb_mat * dm).astype(BF16), xdg, preferred_element_type=F32)
                y_g = y_g + jnp.where(_head_lane_mask(GROUP_W, r), full, 0.0)
            yssd_ref[:, gs] = y_g
            s_scr[:, gs] = cd[:, gs] * s_old[:, gs] + _dot_tn(bm, xd[:, gs] * dte[:, gs])
        _, _, _, outs, _ = _gate_fwd(yssd_ref[...], z_ref[...], nw_ref[...])
        for g in range(SSM_GROUPS):
            yb_ref[:, g * GROUP_W:(g + 1) * GROUP_W] = outs[g].astype(BF16)
        prev_scr[...] = xr[CHUNK - SUBLANES:, :]

    seq_len = n_tok // n_seq

    def rows(width):
        return pl.BlockSpec((n_seq, CHUNK, width), lambda c: (0, c, 0))

    def whole(shape):
        nd = len(shape)
        return pl.BlockSpec(tuple(shape), lambda c: (0,) * nd)

    def by_seq(a):
        return a.reshape(n_seq, seq_len, a.shape[-1])

    outs = _call_carrying(
        body, carried, name="mixer_fwd", grid=(nc,),
        in_specs=[rows(2 * GM_WIDTH), rows(CONV_CH), rows(SSM_WIDTH), rows(DT_PAD), whole(lnw.shape), whole(lnb.shape),
                  whole(w_cat.shape), whole(bmap.shape)] + _ssd_const_specs(),
        out_specs=[rows(GM_WIDTH), rows(SSM_WIDTH), rows(SSM_WIDTH), rows(SSM_WIDTH)],
        out_shape=(_sds((n_seq, seq_len, GM_WIDTH), BF16), _sds((n_seq, seq_len, SSM_WIDTH), BF16),
                   _sds((n_seq, seq_len, SSM_WIDTH), F32), _sds((n_seq, seq_len, SSM_WIDTH), F32)),
        scratch_shapes=[pltpu.VMEM((CHUNK, N_HEADS * CHUNK), BF16), pltpu.VMEM((n_seq, SUBLANES, CONV_CH), F32),
                        pltpu.VMEM((n_seq, SSM_STATE, SSM_WIDTH), F32)],
        operands=[by_seq(p_uv), by_seq(p_xbc), by_seq(p_z), by_seq(p_dt), lnw, lnb, w_cat, bmap, conv_w, conv_b, dt_bias,
                  a_log, dskip_map, norm_w, e_bf, et_bf])
    return tuple(o.reshape(n_tok, o.shape[-1]) for o in outs[:4]) + tuple(outs[4:])


def _outproj_fwd(ya, yb, x, w_out, nw_post, nw_pre2, tm=256):
    n_tok = x.shape[0]

    def body(ya_ref, yb_ref, x_ref, wo_ref, nwa_ref, nwb_ref, o_ref, x1_ref, h2_ref):
        o = jnp.dot(ya_ref[...], wo_ref[:GM_WIDTH, :], preferred_element_type=F32)
        o = o + jnp.dot(yb_ref[...], wo_ref[GM_WIDTH:, :], preferred_element_type=F32)
        on, _ = _rms_fwd(o, nwa_ref[...])
        x1 = x_ref[...] + on
        h2, _ = _rms_fwd(x1, nwb_ref[...])
        o_ref[...] = o
        x1_ref[...] = x1
        h2_ref[...] = h2.astype(BF16)

    return _rows_call("outproj_fwd", body, tm, [ya, yb, x], [w_out, nw_post, nw_pre2],
                      [_sds((n_tok, D_MODEL), F32), _sds((n_tok, D_MODEL), F32), _sds((n_tok, D_MODEL), BF16)])


def _up_cols(wup_ref, j):
    per = (D_FF // N_CHIPS) // FF_TILE
    return wup_ref[j // per, :, (j % per) * FF_TILE:(j % per + 1) * FF_TILE]


def _down_rows(wda_ref, wdb_ref, j):
    assert 2 * FF_TILE == D_FF // N_CHIPS
    return (wda_ref if j % 2 == 0 else wdb_ref)[j // 2]


def _skewed_rows_call(name, main, tail, tm, lead_ins, lag_ins, const_ins, lead_outs, lag_outs, acc_outs, carry):
    n_rows = lead_ins[0].shape[0]
    assert n_rows % tm == 0
    n = n_rows // tm
    counts = [len(lead_ins), len(lag_ins), len(const_ins), len(lead_outs), len(lag_outs), len(acc_outs)]

    def kern(*refs):
        groups, pos = [], 0
        for cnt in counts:
            groups.append(refs[pos:pos + cnt])
            pos += cnt
        lead_i, lag_i, consts, lead_o, lag_o, accs = groups
        carry_scr = refs[pos]
        i = pl.program_id(0)

        @pl.when(i == 0)
        def _():
            for a in accs:
                a[...] = jnp.zeros_like(a)
            carry_scr[...] = main(lead_i, consts, lead_o)

        @pl.when(jnp.logical_and(i > 0, i < n))
        def _():
            previous = carry_scr[...]
            carry_scr[...] = main(lead_i, consts, lead_o)
            tail(previous, lag_i, consts, lag_o, accs)

        @pl.when(i == n)
        def _():
            tail(carry_scr[...], lag_i, consts, lag_o, accs)

    def lead(width):
        return pl.BlockSpec((tm, width), lambda i: (jnp.minimum(i, n - 1), 0))

    def lag(width):
        return pl.BlockSpec((tm, width), lambda i: (jnp.maximum(i - 1, 0), 0))

    def whole(shape, **kw):
        nd = len(shape)
        return pl.BlockSpec(tuple(shape), lambda i: (0,) * nd, **kw)

    const_specs = [whole(a.shape, pipeline_mode=pl.Buffered(1)) for a in const_ins]
    return pl.pallas_call(
        kern, name=name, grid=(n + 1,),
        in_specs=[lead(a.shape[1]) for a in lead_ins] + [lag(a.shape[1]) for a in lag_ins] + const_specs,
        out_specs=[lead(s.shape[1]) for s in lead_outs] + [lag(s.shape[1]) for s in lag_outs] + [whole(s.shape) for s in acc_outs],
        out_shape=tuple(lead_outs) + tuple(lag_outs) + tuple(acc_outs),
        scratch_shapes=[pltpu.VMEM(carry, F32)], compiler_params=_cparams(1),
    )(*lead_ins, *lag_ins, *const_ins)


def _mlp_fwd(h2, x1, tgt, w_up, w_down_a, w_down_b, nw, tm=512):
    n_tok = x1.shape[0]

    def main(lead_i, consts, lead_o):
        (h2_ref,), (wup_ref, wda_ref, wdb_ref, _), (f_ref,) = lead_i, consts, lead_o
        h2v = h2_ref[...]
        acc = jnp.zeros((tm, D_MODEL), F32)
        for j in range(D_FF // FF_TILE):
            cs = slice(j * FF_TILE, (j + 1) * FF_TILE)
            u = jnp.dot(h2v, _up_cols(wup_ref, j), preferred_element_type=F32)
            f = jnp.square(jnp.maximum(u, 0.0)).astype(BF16)
            f_ref[:, cs] = f
            acc = acc + jnp.dot(f, _down_rows(wda_ref, wdb_ref, j), preferred_element_type=F32)
        return acc

    def tail(acc, lag_i, consts, lag_o, accs):
        (x1_ref, tgt_ref), nw_ref, (dd_ref, dy_ref), (loss_ref, dnw_ref) = lag_i, consts[3], lag_o, accs
        dn, r = _rms_fwd(acc, nw_ref[...])
        e = x1_ref[...] + dn - tgt_ref[...]
        loss_ref[...] += jnp.full(loss_ref.shape, (0.5 / D_MODEL) * jnp.sum(e * e), F32)
        dy = e * (1.0 / D_MODEL)
        dd, dnw = _rms_bwd(acc, r, nw_ref[...], dy)
        dy_ref[...] = dy
        dd_ref[...] = dd.astype(BF16)
        dnw_ref[...] += dnw

    return _skewed_rows_call(
        "mlp_fwd", main, tail, tm, [h2], [x1, tgt], [w_up, w_down_a, w_down_b, nw],
        [_sds((n_tok, D_FF), BF16)], [_sds((n_tok, D_MODEL), BF16), _sds((n_tok, D_MODEL), F32)],
        [_sds((8, 128), F32), _sds((1, D_MODEL), F32)], carry=(tm, D_MODEL))


def _mlp_bwd(dd, f, x1, dy, w_down_a, w_down_b, w_up, nw, tm=256):
    n_tok = x1.shape[0]

    def main(lead_i, consts, lead_o):
        (dd_ref, f_ref), (wda_ref, wdb_ref, wup_ref, _), (dup_ref,) = lead_i, consts, lead_o
        ddv = dd_ref[...]
        acc = jnp.zeros((tm, D_MODEL), F32)
        for j in range(D_FF // FF_TILE):
            cs = slice(j * FF_TILE, (j + 1) * FF_TILE)
            df = _dot_nt(ddv, _down_rows(wda_ref, wdb_ref, j))
            du = (df * (2.0 * jnp.sqrt(f_ref[:, cs].astype(F32)))).astype(BF16)
            dup_ref[:, cs] = du
            acc = acc + _dot_nt(du, _up_cols(wup_ref, j))
        return acc

    def tail(acc, lag_i, consts, lag_o, accs):
        (x1_ref, dy_ref), nw_ref, (dx1_ref,), (dnw_ref,) = lag_i, consts[3], lag_o, accs
        x1v = x1_ref[...]
        _, r = _rms_fwd(x1v, nw_ref[...])
        dx, dnw = _rms_bwd(x1v, r, nw_ref[...], acc)
        dx1_ref[...] = dy_ref[...] + dx
        dnw_ref[...] += dnw

    return _skewed_rows_call(
        "mlp_bwd", main, tail, tm, [dd, f], [x1, dy], [w_down_a, w_down_b, w_up, nw],
        [_sds((n_tok, D_FF), BF16)], [_sds((n_tok, D_MODEL), F32)], [_sds((1, D_MODEL), F32)], carry=(tm, D_MODEL))


def _outproj_bwd(dx1, o, w_out, nw, tm=256, carried=None):
    n_tok = dx1.shape[0]

    def body(dx1_ref, o_ref, wo_ref, nw_ref, do_ref, dya_ref, dyb_ref, dnw_ref):
        ov = o_ref[...]
        _, r = _rms_fwd(ov, nw_ref[...])
        do, dnw = _rms_bwd(ov, r, nw_ref[...], dx1_ref[...])
        dob = do.astype(BF16)
        do_ref[...] = dob
        dya_ref[...] = _dot_nt(dob, wo_ref[:GM_WIDTH, :])
        dyb_ref[...] = _dot_nt(dob, wo_ref[GM_WIDTH:, :])
        dnw_ref[...] += dnw

    return _rows_call("outproj_bwd", body, tm, [dx1, o], [w_out, nw],
                      [_sds((n_tok, D_MODEL), BF16), _sds((n_tok, GM_WIDTH), F32), _sds((n_tok, SSM_WIDTH), F32)],
                      [_sds((1, D_MODEL), F32)], carried=carried)


def _gmlp_bwd(p_uv, dya, lnw, lnb, e_bf, et_bf, w_cat, w_stack, bmap, carried=None):
    n_tok = p_uv.shape[0]
    chunks_per_step = 2

    def body(puv_ref, dya_ref, lnw_ref, lnb_ref, e_ref, et_ref, wcat_ref, wstack_ref, bmap_ref,
             dpuv_ref, dws_ref, dbs_ref, dlnw_ref, dlnb_ref, wm_scr, wsm_scr):
        t_stk = lax.broadcasted_iota(jnp.int32, (N_HEADS * CHUNK, CHUNK), 0) % CHUNK
        s_stk = lax.broadcasted_iota(jnp.int32, (N_HEADS * CHUNK, CHUNK), 1)

        @pl.when(pl.program_id(0) == 0)
        def _():
            wm_scr[...] = _causal_w_cat(wcat_ref[...])
            wsm_scr[...] = jnp.where(t_stk >= s_stk, wstack_ref[...], 0.0).astype(BF16)

        lnw_v = lnw_ref[...]
        e_v, et_v = e_ref[...], et_ref[...]

        def one_chunk(rows):
            u, v, gu, tu, tv, rstd, xhat, vn = _gmlp_common(puv_ref[rows, :], lnw_v, lnb_ref[...], e_v, et_v)
            vnb = vn.astype(BF16)
            mixed = jnp.dot(wm_scr[...], _head_blocks(vnb), preferred_element_type=F32) + bmap_ref[...]
            dy = dya_ref[rows, :]
            du = dy * mixed * _gelu_grad(u, tu)
            dmixed = dy * gu
            (dbs,) = _seg_dots([dmixed], et_v)
            dblocks = _head_blocks(dmixed.astype(BF16))
            dvn = lax.dot_general(wsm_scr[...], dblocks, (((0,), (0,)), ((), ())), preferred_element_type=F32)
            dws = lax.dot_general(dblocks, vnb, (((1,), (1,)), ((), ())), preferred_element_type=F32)
            dxh = dvn * lnw_v
            m1, m2 = _seg_dots([dxh, dxh * xhat], et_v)
            m1, m2 = _seg_dots([m1 * (1.0 / HEAD_DIM), m2 * (1.0 / HEAD_DIM)], e_v)
            dgv = rstd * (dxh - m1 - xhat * m2)
            dv = dgv * _gelu_grad(v, tv)
            dpuv_ref[rows, :GM_WIDTH] = du.astype(BF16)
            dpuv_ref[rows, GM_WIDTH:] = dv.astype(BF16)
            return dbs, dws, jnp.sum(dvn * xhat, axis=0, keepdims=True), jnp.sum(dvn, axis=0, keepdims=True)

        parts = [one_chunk(slice(k * CHUNK, (k + 1) * CHUNK)) for k in range(chunks_per_step)]
        dbs, dws, dlnw, dlnb = [functools.reduce(lambda a, b: a + b, vals) for vals in zip(*parts)]
        dbs_ref[...] += dbs
        dws_ref[...] += jnp.where(t_stk >= s_stk, dws, 0.0)
        dlnw_ref[...] += dlnw
        dlnb_ref[...] += dlnb

    return _rows_call(
        "gmlp_bwd", body, chunks_per_step * CHUNK, [p_uv, dya], [lnw, lnb, e_bf, et_bf, w_cat, w_stack, bmap],
        [_sds((n_tok, 2 * GM_WIDTH), BF16)],
        [_sds((N_HEADS * CHUNK, CHUNK), F32), _sds((CHUNK, DT_PAD), F32), _sds((1, GM_WIDTH), F32),
         _sds((1, GM_WIDTH), F32)],
        scratch=[pltpu.VMEM((CHUNK, N_HEADS * CHUNK), BF16), pltpu.VMEM((N_HEADS * CHUNK, CHUNK), BF16)],
        carried=carried)


def _ssd_bwd(p_xbc, p_z, p_dt, yssd, sprev, dyb, conv_w, conv_b, dt_bias, a_log, dskip_map, norm_w, e_bf, et_bf, n_seq,
             carried=None):
    n_tok = p_xbc.shape[0]
    nc = n_tok // n_seq // CHUNK

    def body(xr3, xprev3, z3, pdt3, yssd3, sprev3, dyb3,
             cw_ref, cb_ref, dtb_ref, alog_ref, dsk_ref, nw_ref, e_ref, et_ref,
             dpxbc3, dpz3, dpdt3, dcw_ref, dcb_ref, ddtb_ref, dalog_ref, ddsk_ref, dnw_ref,
             ds3_scr, nxt3_scr, dxa3_scr):
        @pl.when(pl.program_id(0) == 0)
        def _():
            for a in (dcw_ref, dcb_ref, ddtb_ref, dalog_ref, ddsk_ref, dnw_ref, ds3_scr, nxt3_scr):
                a[...] = jnp.zeros_like(a)

        for b in range(n_seq):
            one_sequence(xr3.at[b], xprev3.at[b], z3.at[b], pdt3.at[b], yssd3.at[b], sprev3.at[b], dyb3.at[b],
                         cw_ref, cb_ref, dtb_ref, alog_ref, dsk_ref, nw_ref, e_ref, et_ref,
                         dpxbc3.at[b], dpz3.at[b], dpdt3.at[b], dcw_ref, dcb_ref, ddtb_ref, dalog_ref, ddsk_ref, dnw_ref,
                         ds3_scr.at[b], nxt3_scr.at[b], dxa3_scr.at[b])

    def one_sequence(xr_ref, xprev_ref, z_ref, pdt_ref, yssd_ref, sprev_ref, dyb_ref,
                     cw_ref, cb_ref, dtb_ref, alog_ref, dsk_ref, nw_ref, e_ref, et_ref,
                     dpxbc_ref, dpz_ref, dpdt_ref, dcw_ref, dcb_ref, ddtb_ref, dalog_ref, ddsk_ref, dnw_ref,
                     ds_scr, nxt_scr, dxa_scr):
        chunk = nc - 1 - pl.program_id(0)
        xr = xr_ref[...]
        prev = jnp.where(chunk == 0, 0.0, xprev_ref[...])
        et_v = et_ref[...]
        p = _ssd_pre(xr, prev, cw_ref, cb_ref[...], pdt_ref[...], dtb_ref[...], alog_ref[...], e_ref[...])
        last, e_exp, dte, cd = _ssd_maps(p)
        rowi = p["rowi"]
        xs = p["xa"][:, :SSM_WIDTH]
        xd = xs * p["dt_map"]
        a_cs_t = p["a_cs"].T
        tri = _tril_mask()
        dsk = dsk_ref[...]
        nw_v = nw_ref[...]

        yv = yssd_ref[...]
        zv = z_ref[...]
        sz, zg, yg, _, rs = _gate_fwd(yv, zv, nw_v)
        dout = dyb_ref[...]
        for g in range(SSM_GROUPS):
            gs = slice(g * GROUP_W, (g + 1) * GROUP_W)
            dyg_g, dnw_g = _rms_bwd(yg[:, gs], rs[g], nw_v[:, gs], dout[:, gs])
            dnw_ref[:, gs] += dnw_g
            dxa_scr[:, gs] = dyg_g
        dyg = dxa_scr[:, :SSM_WIDTH]
        d_y = dyg * zg
        dpz_ref[...] = (dyg * yv * (sz + zv * sz * (1.0 - sz))).astype(BF16)

        s_prev = sprev_ref[...]
        ds_next = ds_scr[...]
        lane_dt = lax.broadcasted_iota(jnp.int32, (1, DT_PAD), 1)
        da_cols = jnp.zeros((CHUNK, DT_PAD), F32)
        for g in range(SSM_GROUPS):
            gs = slice(g * GROUP_W, (g + 1) * GROUP_W)
            b_off = SSM_WIDTH + g * SSM_STATE
            c_off = SSM_WIDTH + (SSM_GROUPS + g) * SSM_STATE
            bm = p["xa"][:, b_off:b_off + SSM_STATE].astype(BF16)
            cm = p["xa"][:, c_off:c_off + SSM_STATE].astype(BF16)
            cb_mat = _dot_nt(cm, bm)
            d_yg = d_y[:, gs]
            d_ygb = d_yg.astype(BF16)
            xdg = xd[:, gs]
            xdgb = xdg.astype(BF16)
            ds_g = ds_next[:, gs]
            sp_g = s_prev[:, gs]
            bds = _dot(bm, ds_g)
            dcs = d_yg * e_exp[:, gs]
            d_c = _dot_nt(dcs, sp_g)
            ds_scr[:, gs] = cd[:, gs] * ds_g + _dot_tn(cm, dcs)
            d_b = _dot_nt(xdg * dte[:, gs], ds_g)
            dxd_g = bds * dte[:, gs]
            sum_dcb = jnp.zeros((CHUNK, CHUNK), F32)
            for r in range(SSM_GROUPS * 2):
                head = g * 4 + r
                mask = _head_lane_mask(GROUP_W, r)
                dm = _head_decay(p["a_cs"], a_cs_t, head, tri)
                m_mat = cb_mat * dm
                g_mat = _dot_nt(jnp.where(mask, d_yg, 0.0), xdgb)
                w_mat = g_mat * m_mat
                sum_dcb = sum_dcb + g_mat * dm
                dxd_g = dxd_g + jnp.where(mask, _dot_tn(m_mat, d_ygb), 0.0)
                da_h = jnp.sum(w_mat - w_mat.T, axis=1, keepdims=True)
                da_cols = da_cols + jnp.where(lane_dt == head, da_h, 0.0)
            d_c = d_c + _dot(sum_dcb, bm)
            d_b = d_b + _dot_tn(sum_dcb, cm)
            dxa_scr[:, b_off:b_off + SSM_STATE] = d_b
            dxa_scr[:, c_off:c_off + SSM_STATE] = d_c
            y_off_g = _dot(cm, sp_g) * e_exp[:, gs]
            t3 = bds * xdg * dte[:, gs]
            tail = jnp.sum(t3, axis=0, keepdims=True) + jnp.sum(ds_g * sp_g, axis=0, keepdims=True) * cd[:, gs]
            pre_g = d_yg * y_off_g - t3 + jnp.where(last, tail, 0.0)
            s_pre, ddt_g, s_dsk = _seg_dots([pre_g, dxd_g * xs[:, gs], d_yg * xs[:, gs]], et_v[gs, :])
            da_cols = da_cols + s_pre
            ddsk_ref[...] += jnp.sum(s_dsk, axis=0, keepdims=True)
            dxa_scr[:, gs] = dxd_g * p["dt_map"][:, gs] + dsk[:, gs] * d_yg
            if g == 0:
                ddt = ddt_g
            else:
                ddt = ddt + ddt_g
        r_i = lax.broadcasted_iota(jnp.int32, (CHUNK, CHUNK), 0)
        c_i = lax.broadcasted_iota(jnp.int32, (CHUNK, CHUNK), 1)
        ddta = _tri_dot(r_i <= c_i, da_cols, terms=2)
        ddt = ddt + ddta * p["a_neg"]
        dalog_ref[...] += jnp.sum(ddta * p["dt"], axis=0, keepdims=True) * p["a_neg"]
        draw = ddt * _sigmoid(p["pre"])
        ddtb_ref[...] += jnp.sum(draw, axis=0, keepdims=True)
        dpdt_ref[...] = draw.astype(BF16)

        xc = p["xc"]
        sg = p["sg"]
        dxc = dxa_scr[...] * (sg + xc * sg * (1.0 - sg))
        dcb_ref[...] += jnp.sum(dxc, axis=0, keepdims=True)
        for k in range(CONV_K):
            dcw_ref[k] += jnp.sum(dxc * p["shifted"][k], axis=0, keepdims=True)
        nxt = nxt_scr[...]
        dxr = cw_ref[3] * dxc
        for s in range(1, CONV_K):
            dxr = dxr + cw_ref[CONV_K - 1 - s] * _shift_up(dxc, nxt, s)
        dpxbc_ref[...] = dxr.astype(BF16)
        nxt_scr[...] = dxc[:SUBLANES, :]

    seq_len = n_tok // n_seq

    def rows(width):
        return pl.BlockSpec((n_seq, CHUNK, width), lambda s: (0, nc - 1 - s, 0))

    tiles = CHUNK // SUBLANES
    prev_rows = pl.BlockSpec((n_seq, SUBLANES, CONV_CH), lambda s: (0, jnp.maximum((nc - 1 - s) * tiles - 1, 0), 0))

    def whole(shape):
        nd = len(shape)
        return pl.BlockSpec(tuple(shape), lambda s: (0,) * nd)

    def by_seq(a):
        return a.reshape(n_seq, seq_len, a.shape[-1])

    acc_shapes = [(CONV_K, 1, CONV_CH), (1, CONV_CH), (1, DT_PAD), (1, DT_PAD), (1, DT_PAD), (1, SSM_WIDTH)]
    xbc3 = by_seq(p_xbc)
    outs = _call_carrying(
        body, carried, name="ssd_bwd", grid=(nc,),
        in_specs=[rows(CONV_CH), prev_rows, rows(SSM_WIDTH), rows(DT_PAD), rows(SSM_WIDTH), rows(SSM_WIDTH),
                  rows(SSM_WIDTH)] + _ssd_const_specs(),
        out_specs=[rows(CONV_CH), rows(SSM_WIDTH), rows(DT_PAD)] + [whole(s) for s in acc_shapes],
        out_shape=tuple([_sds((n_seq, seq_len, CONV_CH), BF16), _sds((n_seq, seq_len, SSM_WIDTH), BF16),
                         _sds((n_seq, seq_len, DT_PAD), BF16)] + [_sds(s, F32) for s in acc_shapes]),
        scratch_shapes=[pltpu.VMEM((n_seq, SSM_STATE, SSM_WIDTH), F32), pltpu.VMEM((n_seq, SUBLANES, CONV_CH), F32),
                        pltpu.VMEM((n_seq, CHUNK, CONV_CH), F32)],
        operands=[xbc3, xbc3, by_seq(p_z), by_seq(p_dt), by_seq(yssd), by_seq(sprev), by_seq(dyb), conv_w, conv_b, dt_bias,
                  a_log, dskip_map, norm_w, e_bf, et_bf])
    return tuple(o.reshape(n_tok, o.shape[-1]) for o in outs[:3]) + tuple(outs[3:])


def _inproj_bwd(dp_uv, dp_xbc, dp_z, dp_dt, x, dx1, w_uv, w_xbc, w_z, w_dt, nw, tm=256, carried=None):
    n_tok = x.shape[0]

    def body(duv_ref, dxbc_ref, dz_ref, ddt_ref, x_ref, dx1_ref, wuv_ref, wxbc_ref, wz_ref, wdt_ref, nw_ref,
             gx_ref, h_ref, dnw_ref):
        dh = _dot_nt(duv_ref[...], wuv_ref[...]) + _dot_nt(dxbc_ref[...], wxbc_ref[...])
        dh = dh + _dot_nt(dz_ref[...], wz_ref[...]) + _dot_nt(ddt_ref[...], wdt_ref[...])
        xv = x_ref[...]
        h, r = _rms_fwd(xv, nw_ref[...])
        dx, dnw = _rms_bwd(xv, r, nw_ref[...], dh)
        gx_ref[...] = dx1_ref[...] + dx
        h_ref[...] = h.astype(BF16)
        dnw_ref[...] += dnw

    return _rows_call("inproj_bwd", body, tm, [dp_uv, dp_xbc, dp_z, dp_dt, x, dx1], [w_uv, w_xbc, w_z, w_dt, nw],
                      [_sds((n_tok, D_MODEL), F32), _sds((n_tok, D_MODEL), BF16)], [_sds((1, D_MODEL), F32)],
                      carried=carried)


def _const_maps():
    lane = jnp.arange(SSM_WIDTH) // HEAD_DIM
    e_bf = (jnp.arange(DT_PAD)[:, None] == lane[None, :]).astype(BF16)
    return e_bf, e_bf.T


def _pad_lanes(v, width):
    return jnp.pad(v, ((0, 0), (0, width - v.shape[1])))


SHARD_COLS = IN_COLS // N_CHIPS
_UV_END = 2 * GM_WIDTH
_Z_END = _UV_END + SSM_WIDTH
_XBC_END = _Z_END + CONV_CH


def _cols_from_shards(w4, lo, hi):
    pieces = []
    for j in range(N_CHIPS):
        a, b = max(lo, j * SHARD_COLS), min(hi, (j + 1) * SHARD_COLS)
        if a < b:
            pieces.append(w4[j][:, a - j * SHARD_COLS:b - j * SHARD_COLS])
    return pieces[0] if len(pieces) == 1 else jnp.concatenate(pieces, axis=1)


def _shards_from_cols(blocks):
    shards = []
    for j in range(N_CHIPS):
        pieces = []
        for arr, lo, hi in blocks:
            a, b = max(lo, j * SHARD_COLS), min(hi, (j + 1) * SHARD_COLS)
            if a < b:
                pieces.append(arr[:, a - lo:b - lo])
        shards.append(pieces[0] if len(pieces) == 1 else jnp.concatenate(pieces, axis=1))
    return jnp.stack(shards)


def _forward_backward(x, tgt, w_in4, conv_w, small, out_shard, up_shard, down_shard, core, adam_args):
    n_seq, seq_len, _ = x.shape
    n_tok = n_seq * seq_len
    x2 = x.reshape(n_tok, D_MODEL)
    tgt2 = tgt.reshape(n_tok, D_MODEL)
    e_bf, et_bf = _const_maps()

    w_uv = _cols_from_shards(w_in4, 0, _UV_END)
    w_z = _cols_from_shards(w_in4, _UV_END, _Z_END)
    w_xbc = _cols_from_shards(w_in4, _Z_END, _XBC_END)
    w_dt = _pad_lanes(_cols_from_shards(w_in4, _XBC_END, IN_COLS), DT_PAD)

    nw_pre = small["norm_mix_pre"]
    lnw = small["gm_ln_w"].reshape(1, GM_WIDTH)
    lnb = small["gm_ln_b"].reshape(1, GM_WIDTH)
    w_stack = small["gm_w_s"].reshape(N_HEADS * CHUNK, CHUNK)
    w_cat = jnp.transpose(small["gm_w_s"], (1, 0, 2)).reshape(CHUNK, N_HEADS * CHUNK)
    bmap = jnp.repeat(small["gm_b_s"].T, HEAD_DIM, axis=1)
    cw3 = conv_w.reshape(CONV_K, 1, CONV_CH)
    conv_b = small["conv_b"]
    dt_bias = _pad_lanes(small["dt_bias"], DT_PAD)
    a_log = _pad_lanes(small["a_log"], DT_PAD)
    dskip_map = jnp.repeat(small["d_skip"], HEAD_DIM, axis=1)
    ssm_nw = small["ssm_norm_w"]

    half = down_shard.shape[0] // 2
    p_uv, p_xbc, p_z, p_dt, w_out4, w_down_a = _inproj_fwd(
        x2, nw_pre, w_uv, w_xbc, w_z, w_dt, carried=_allgather_exchange([out_shard, down_shard[:half]]))
    ssd_consts = (cw3, conv_b, dt_bias, a_log, dskip_map, ssm_nw, e_bf, et_bf)
    ya, yb, yssd, sprev, w_up4, w_down_b = _mixer_fwd(
        p_uv, p_xbc, p_z, p_dt, lnw, lnb, w_cat, bmap, *ssd_consts, n_seq,
        carried=_allgather_exchange([up_shard, down_shard[half:]]))
    w_out_b = w_out4.reshape(D_MODEL, D_MODEL)
    o, x1, h2 = _outproj_fwd(ya, yb, x2, w_out_b, small["norm_mix_post"], small["norm_ffn_pre"])
    f, dd, dy, loss_acc, d_nffn_post = _mlp_fwd(h2, x1, tgt2, w_up4, w_down_a, w_down_b, small["norm_ffn_post"])

    dup, dx1, d_nffn_pre = _mlp_bwd(dd, f, x1, dy, w_down_a, w_down_b, w_up4, small["norm_ffn_pre"])
    tk = min(DW_TOKENS_PER_STEP, n_tok)
    g_up = _matmul_tn("dw_up", h2, dup, D_MODEL, D_MODEL, tk, stacked=True)
    g_down = _matmul_tn("dw_down", f, dd, 1024, D_MODEL, tk).reshape(N_CHIPS, D_FF // N_CHIPS, D_MODEL)
    do, dya, dyb, d_nmix_post, got_up, got_down = _outproj_bwd(
        dx1, o, w_out_b, small["norm_mix_post"], carried=_pair_exchange([g_up, g_down]))
    h_up = _pair_sum(core, g_up, got_up, 256)
    h_down = _pair_sum(core, g_down, got_down, 256)
    g_out_a = _matmul_tn("dw_out_a", ya, do, GM_WIDTH, D_MODEL, tk)
    g_out_b = _matmul_tn("dw_out_b", yb, do, SSM_WIDTH, D_MODEL, tk)
    g_out = jnp.concatenate([g_out_a, g_out_b], axis=0).reshape(N_CHIPS, D_MODEL // N_CHIPS, D_MODEL)
    dp_uv, d_ws, d_bs_t, d_lnw, d_lnb, slab_up, got_out = _gmlp_bwd(
        p_uv, dya, lnw, lnb, e_bf, et_bf, w_cat, w_stack, bmap,
        carried=_both(_chip_exchange([h_up]), _pair_exchange([g_out])))
    h_out = _pair_sum(core, g_out, got_out, 128)
    early = {
        "gm_ln_w": d_lnw.reshape(N_HEADS, HEAD_DIM), "gm_ln_b": d_lnb.reshape(N_HEADS, HEAD_DIM),
        "gm_w_s": d_ws.reshape(N_HEADS, CHUNK, CHUNK), "gm_b_s": d_bs_t[:, :N_HEADS].T,
        "norm_mix_post": d_nmix_post, "norm_ffn_pre": d_nffn_pre, "norm_ffn_post": d_nffn_post,
    }
    packed_early = _pack(early, tuple(early), tail=loss_acc[0, 0].reshape(1))
    (dp_xbc, dp_z, dp_dt, d_cw, d_cb, d_dtb, d_alog, d_dsk, d_ssm_nw, slab_down, slab_out, all_early) = _ssd_bwd(
        p_xbc, p_z, p_dt, yssd, sprev, dyb, *ssd_consts, n_seq,
        carried=_both(_chip_exchange([h_down, h_out]), _device_gather_exchange(packed_early)))
    gx, h, d_nmix_pre = _inproj_bwd(dp_uv, dp_xbc, dp_z, dp_dt, x2, dx1, w_uv, w_xbc, w_z, w_dt, nw_pre)
    late = {
        "norm_mix_pre": d_nmix_pre, "conv_w": d_cw.reshape(CONV_K, CONV_CH), "conv_b": d_cb,
        "dt_bias": d_dtb[:, :N_HEADS], "a_log": d_alog[:, :N_HEADS], "d_skip": d_dsk[:, :N_HEADS],
        "ssm_norm_w": d_ssm_nw,
    }
    g_uv, all_late = _matmul_tn("dw_in_uv", h, dp_uv, D_MODEL, 2 * GM_WIDTH, tk,
                                carried=_device_gather_exchange(_pack(late, tuple(late))))
    sum_early = _ordered_sum("small_sum_early", all_early)
    small_sum = _unpack(sum_early, {n: v.shape for n, v in early.items()}, tuple(early))
    small_sum.update(_unpack(_ordered_sum("small_sum_late", all_late), {n: v.shape for n, v in late.items()}, tuple(late)))
    loss = sum_early.reshape(-1)[sum(v.size for v in early.values())]
    g_xbc = _matmul_tn("dw_in_xbc", h, dp_xbc, D_MODEL, CONV_CH, tk)
    g_z = _matmul_tn("dw_in_z", h, dp_z, D_MODEL, SSM_WIDTH, tk)
    g_dt = _matmul_tn("dw_in_dt", h, dp_dt, D_MODEL, DT_PAD, tk)

    g_in = _shards_from_cols([(g_uv, 0, _UV_END), (g_z, _UV_END, _Z_END), (g_xbc, _Z_END, _XBC_END),
                              (g_dt, _XBC_END, IN_COLS)])

    red_up, red_down, red_out = _chip_sum(slab_up, 256), _chip_sum(slab_down, 256), _chip_sum(slab_out, 128)
    oth_up, oth_down, oth_out, got_in = _run_exchange(
        "grad_pair_swap", _both(_pair_swap([red_up, red_down, red_out]), _pair_exchange([g_in])))
    h_in = _pair_sum(core, g_in, got_in, 256)
    (slab_in,) = _run_exchange("grad_chip_exchange", _chip_exchange([h_in]))
    res = _adamw_halves("adamw_mlp", [(adam_args["w_up"][0], red_up, oth_up) + adam_args["w_up"][1:],
                                      (adam_args["w_down"][0], red_down, oth_down) + adam_args["w_down"][1:]], 256)
    big_out = {"w_up": res[0:4], "w_down": res[4:8]}
    big_out["w_out"] = _adamw_halves("adamw_w_out", [(adam_args["w_out"][0], red_out, oth_out) + adam_args["w_out"][1:]], 128)
    red_in = _chip_sum(slab_in, 256)
    (oth_in,) = _run_exchange("grad_pair_swap_in", _pair_swap([red_in]))
    big_out["w_in"] = _adamw_halves("adamw_w_in", [(adam_args["w_in"][0], red_in, oth_in) + adam_args["w_in"][1:]], 256)

    return loss, gx.reshape(x.shape), big_out, small_sum


_HBM = pl.BlockSpec(memory_space=pltpu.HBM)


D2D_CHUNKS = 8
ICI_CHUNKS = 1
ROW_ALIGN = 16


def _row_chunks(rows, n_chunks):
    size = min(max(rows // n_chunks, ROW_ALIGN), rows)
    assert rows % size == 0
    return [(start, size) for start in range(0, rows, size)]


def _position():
    x, y, c = lax.axis_index("x"), lax.axis_index("y"), lax.axis_index("c")
    chips = [(1 - x, y), (x, 1 - y), (1 - x, 1 - y)]
    return x, y, c, chips


def _allgather_exchange(arrs):
    n = len(arrs)

    def copies(ins, outs, send_sems, recv_sems, local_sems):
        x, y, c, chips = _position()
        me = 2 * x + y
        sibling = (x, y, 1 - c)

        def copy(a, k, src, dst, to):
            return pltpu.make_async_remote_copy(src_ref=src, dst_ref=dst, send_sem=send_sems.at[a, k],
                                                recv_sem=recv_sems.at[a, k], device_id=to, device_id_type=MESH)

        def half_rows(a, pc):
            half = ins[a].shape[0] // 2
            return pl.ds(pc * half, half)

        local = [pltpu.make_async_copy(ins[a], outs[a].at[me], local_sems.at[a]) for a in range(n)]
        ici_out = [[copy(a, k, ins[a].at[half_rows(a, c)], outs[a].at[me, half_rows(a, c)], (px, py, c))
                    for k, (px, py) in enumerate(chips)] for a in range(n)]
        return c, chips, sibling, copy, half_rows, local, ici_out

    def start(ins, outs, send_sems, recv_sems, local_sems):
        c, chips, _, copy, _, local, _ = copies(ins, outs, send_sems, recv_sems, local_sems)
        x, y, _, _ = _position()
        me = 2 * x + y
        for cp in local:
            cp.start()
        for a in range(n):
            half = ins[a].shape[0] // 2
            for k, (px, py) in enumerate(chips):
                for first, size in _row_chunks(half, ICI_CHUNKS):
                    rows = pl.ds(c * half + first, size)
                    copy(a, k, ins[a].at[rows], outs[a].at[me, rows], (px, py, c)).start()

    def finish(ins, outs, send_sems, recv_sems, local_sems):
        c, chips, sibling, copy, half_rows, local, ici_out = copies(ins, outs, send_sems, recv_sems, local_sems)
        for a in range(n):
            half = ins[a].shape[0] // 2
            for k, (px, py) in enumerate(chips):
                blk = outs[a].at[2 * px + py, half_rows(a, c)]
                copy(a, k, blk, blk, (px, py, c)).wait_recv()
                for first, size in _row_chunks(half, D2D_CHUNKS):
                    piece = outs[a].at[2 * px + py, pl.ds(c * half + first, size)]
                    copy(a, 3 + k, piece, piece, sibling).start()
        for a in range(n):
            for k, (px, py) in enumerate(chips):
                theirs = outs[a].at[2 * px + py, half_rows(a, 1 - c)]
                copy(a, 3 + k, theirs, theirs, sibling).wait_recv()
                mine = outs[a].at[2 * px + py, half_rows(a, c)]
                copy(a, 3 + k, mine, mine, sibling).wait_send()
        for a in range(n):
            for cp in ici_out[a]:
                cp.wait_send()
        for cp in local:
            cp.wait()

    return _Carried(arrs, [_sds((N_CHIPS,) + a.shape, a.dtype) for a in arrs],
                    [pltpu.SemaphoreType.DMA((n, 6)), pltpu.SemaphoreType.DMA((n, 6)), pltpu.SemaphoreType.DMA((n,))],
                    start, finish)


def _run_exchange(name, exchange):
    n_in, n_out = len(exchange.ins), len(exchange.out_shapes)

    def body(*refs):
        ins, outs, sems = refs[:n_in], refs[n_in:n_in + n_out], refs[n_in + n_out:]
        exchange.start(ins, outs, *sems)
        exchange.finish(ins, outs, *sems)

    return pl.pallas_call(
        body, name=name, out_shape=tuple(exchange.out_shapes), in_specs=[_HBM] * n_in,
        out_specs=tuple([_HBM] * n_out), scratch_shapes=exchange.sems,
    )(*exchange.ins)


def _pair_exchange(grads):
    n = len(grads)

    def copier(send_sems, recv_sems):
        x, y, c, _ = _position()

        def copy(a, src, dst):
            return pltpu.make_async_remote_copy(src_ref=src, dst_ref=dst, send_sem=send_sems.at[a],
                                                recv_sem=recv_sems.at[a], device_id=(x, y, 1 - c), device_id_type=MESH)
        return c, copy

    def start(ins, got, send_sems, recv_sems):
        c, copy = copier(send_sems, recv_sems)
        for a in range(n):
            half = ins[a].shape[1] // 2
            for slab in range(N_CHIPS):
                for first, size in _row_chunks(half, D2D_CHUNKS):
                    copy(a, ins[a].at[slab, pl.ds((1 - c) * half + first, size), :],
                         got[a].at[slab, pl.ds(first, size), :]).start()

    def finish(ins, got, send_sems, recv_sems):
        c, copy = copier(send_sems, recv_sems)
        for a in range(n):
            half = ins[a].shape[1] // 2
            copy(a, ins[a].at[:, pl.ds((1 - c) * half, half), :], got[a]).wait()

    return _Carried(grads, [_sds((N_CHIPS, g.shape[1] // 2, g.shape[2]), g.dtype) for g in grads],
                    [pltpu.SemaphoreType.DMA((n,)), pltpu.SemaphoreType.DMA((n,))], start, finish)


def _chip_exchange(hsums):
    n = len(hsums)

    def copies(ins, outs, send_sems, recv_sems, local_sems, pieces):
        x, y, c, chips = _position()
        me = 2 * x + y
        cps = []
        for a in range(n):
            cps.append(pltpu.make_async_copy(ins[a].at[me], outs[a].at[me], local_sems.at[a]))
            rows = ins[a].shape[1]
            for k, (px, py) in enumerate(chips):
                for first, size in (_row_chunks(rows, ICI_CHUNKS) if pieces else [(0, rows)]):
                    cps.append(pltpu.make_async_remote_copy(
                        src_ref=ins[a].at[2 * px + py, pl.ds(first, size)], dst_ref=outs[a].at[me, pl.ds(first, size)],
                        send_sem=send_sems.at[a, k], recv_sem=recv_sems.at[a, k], device_id=(px, py, c),
                        device_id_type=MESH))
        return cps

    def start(*refs):
        for cp in copies(*refs, pieces=True):
            cp.start()

    def finish(*refs):
        for cp in copies(*refs, pieces=False):
            cp.wait()

    return _Carried(hsums, [_sds(h.shape, h.dtype) for h in hsums],
                    [pltpu.SemaphoreType.DMA((n, 3)), pltpu.SemaphoreType.DMA((n, 3)), pltpu.SemaphoreType.DMA((n,))],
                    start, finish)


def _pair_swap(reds):
    n = len(reds)

    def copier(send_sems, recv_sems):
        x, y, c, _ = _position()

        def copy(a, src, dst):
            return pltpu.make_async_remote_copy(src_ref=src, dst_ref=dst, send_sem=send_sems.at[a],
                                                recv_sem=recv_sems.at[a], device_id=(x, y, 1 - c), device_id_type=MESH)
        return copy

    def start(ins, outs, send_sems, recv_sems):
        copy = copier(send_sems, recv_sems)
        for a in range(n):
            for first, size in _row_chunks(ins[a].shape[0], 2 * D2D_CHUNKS):
                copy(a, ins[a].at[pl.ds(first, size), :], outs[a].at[pl.ds(first, size), :]).start()

    def finish(ins, outs, send_sems, recv_sems):
        copy = copier(send_sems, recv_sems)
        for a in range(n):
            copy(a, ins[a], outs[a]).wait()

    return _Carried(reds, [_sds(r.shape, r.dtype) for r in reds],
                    [pltpu.SemaphoreType.DMA((n,)), pltpu.SemaphoreType.DMA((n,))], start, finish)


def _device_gather_exchange(packed):
    def copies(ins, outs, send_sems, recv_sems, local_sem):
        (x_ref,), (all_ref,) = ins, outs
        x, y, c, chips = _position()
        me, sibling = (x, y, c), (x, y, 1 - c)

        def slab(px, py, pc):
            return all_ref.at[4 * px + 2 * py + pc]

        def copy(k, block, to, src=None):
            return pltpu.make_async_remote_copy(
                src_ref=slab(*block) if src is None else src, dst_ref=slab(*block), send_sem=send_sems.at[k],
                recv_sem=recv_sems.at[k], device_id=to, device_id_type=MESH)

        mine = pltpu.make_async_copy(x_ref, slab(*me), local_sem)
        first = [copy(0, me, sibling, src=x_ref)]
        first += [copy(1 + j, me, (*chip, c), src=x_ref) for j, chip in enumerate(chips)]
        passed = [copy(4 + j, (*chip, c), sibling) for j, chip in enumerate(chips)]
        return c, chips, me, sibling, copy, mine, first, passed

    def start(ins, outs, send_sems, recv_sems, local_sem):
        _, _, _, _, _, mine, first, _ = copies(ins, outs, send_sems, recv_sems, local_sem)
        mine.start()
        for cp in first:
            cp.start()

    def finish(ins, outs, send_sems, recv_sems, local_sem):
        c, chips, me, sibling, copy, mine, first, passed = copies(ins, outs, send_sems, recv_sems, local_sem)
        for j, chip in enumerate(chips):
            copy(1 + j, (*chip, c), me).wait_recv()
            passed[j].start()
        copy(0, sibling, me).wait_recv()
        for j, chip in enumerate(chips):
            copy(4 + j, (*chip, 1 - c), me).wait_recv()
        for cp in first + passed:
            cp.wait_send()
        mine.wait()

    return _Carried([packed], [_sds((N_DEV,) + packed.shape, F32)],
                    [pltpu.SemaphoreType.DMA((7,)), pltpu.SemaphoreType.DMA((7,)), pltpu.SemaphoreType.DMA],
                    start, finish)


def _ordered_sum(name, slabs):
    _, m_per, n_cols = slabs.shape

    def body(s_ref, o_ref):
        acc = s_ref[0]
        for d in range(1, N_DEV):
            acc = acc + s_ref[d]
        o_ref[...] = acc

    vmem = pl.BlockSpec(memory_space=pltpu.VMEM)
    return pl.pallas_call(body, name=name, out_shape=_sds((m_per, n_cols), F32), in_specs=[vmem], out_specs=vmem)(slabs)


def _pair_sum(core, own, got, tm):
    _, half, cols = got.shape
    nb = half // tm

    def body(c_ref, a_ref, b_ref, o_ref):
        o_ref[...] = (a_ref[...] + b_ref[...]).astype(BF16)

    return pl.pallas_call(
        body, name="grad_pair_sum", out_shape=_sds(got.shape, BF16),
        grid_spec=pltpu.PrefetchScalarGridSpec(
            num_scalar_prefetch=1, grid=(N_CHIPS, nb),
            in_specs=[pl.BlockSpec((None, tm, cols), lambda s, i, c_ref: (s, c_ref[0] * nb + i, 0)),
                      pl.BlockSpec((None, tm, cols), lambda s, i, c_ref: (s, i, 0))],
            out_specs=pl.BlockSpec((None, tm, cols), lambda s, i, c_ref: (s, i, 0))),
        compiler_params=_cparams(2),
    )(core, own, got)


def _chip_sum(slabs, tm):
    _, half, cols = slabs.shape

    def body(s_ref, o_ref):
        acc = s_ref[0].astype(F32)
        for k in range(1, N_CHIPS):
            acc = acc + s_ref[k].astype(F32)
        o_ref[...] = acc

    return pl.pallas_call(
        body, name="grad_chip_sum", out_shape=_sds((half, cols), F32), grid=(half // tm,),
        in_specs=[pl.BlockSpec((N_CHIPS, tm, cols), lambda i: (0, i, 0))],
        out_specs=pl.BlockSpec((tm, cols), lambda i: (i, 0)), compiler_params=_cparams(1),
    )(slabs)


def _adam_math(w, g, m, v):
    m2 = ADAM_B1 * m + (1.0 - ADAM_B1) * g
    v2 = ADAM_B2 * v + (1.0 - ADAM_B2) * (g * g)
    m_hat = m2 / (1.0 - ADAM_B1 ** ADAM_STEP)
    v_hat = v2 / (1.0 - ADAM_B2 ** ADAM_STEP)
    delta = -ADAM_LR * (m_hat / (jnp.sqrt(v_hat) + ADAM_EPS) + ADAM_WD * w)
    return delta, m2, v2


def _adamw_halves(name, items, tm, carried=None):
    rows, cols = items[0][0].shape
    nb = rows // 2 // tm
    n = len(items)

    def body(*refs):
        mine = (pl.program_id(0) // nb) == lax.axis_index("c")
        for k in range(n):
            w_ref, own_ref, oth_ref, m_ref, v_ref = refs[5 * k:5 * k + 5]
            g_ref, d_ref, m2_ref, v2_ref = refs[5 * n + 4 * k:5 * n + 4 * k + 4]
            g = jnp.where(mine, own_ref[...], oth_ref[...])
            d, m2, v2 = _adam_math(w_ref[...], g, m_ref[...], v_ref[...])
            g_ref[...] = g
            d_ref[...] = d
            m2_ref[...] = m2
            v2_ref[...] = v2

    full = pl.BlockSpec((tm, cols), lambda i: (i, 0))
    half = pl.BlockSpec((tm, cols), lambda i: (i % nb, 0))
    return _call_carrying(
        body, carried, name=name, grid=(rows // tm,), in_specs=[full, half, half, full, full] * n,
        out_specs=[full] * (4 * n), out_shape=tuple([_sds((rows, cols), F32)] * (4 * n)), scratch_shapes=[],
        operands=[a for item in items for a in item])


def _adamw(name, w, g, m, v, tm):
    def body(w_ref, g_ref, m_ref, v_ref, gout_ref, d_ref, m2_ref, v2_ref):
        gv = g_ref[...]
        d, m2, v2 = _adam_math(w_ref[...], gv, m_ref[...], v_ref[...])
        gout_ref[...] = gv
        d_ref[...] = d
        m2_ref[...] = m2
        v2_ref[...] = v2

    return _rows_call(name, body, tm, [w, g, m, v], [], [_sds(w.shape, F32)] * 4)


_SMALL_NAMES = ("norm_mix_pre", "gm_ln_w", "gm_ln_b", "gm_w_s", "gm_b_s", "conv_w", "conv_b", "dt_bias", "a_log",
                "d_skip", "ssm_norm_w", "norm_mix_post", "norm_ffn_pre", "norm_ffn_post")
_PACK_COLS = 1024


def _pack(parts, names=_SMALL_NAMES, tail=None):
    pieces = [parts[n].reshape(-1) for n in names]
    flat = jnp.concatenate(pieces if tail is None else pieces + [tail])
    rows = -(-flat.shape[0] // (8 * _PACK_COLS)) * 8
    flat = jnp.pad(flat, (0, rows * _PACK_COLS - flat.shape[0]))
    return flat.reshape(rows, _PACK_COLS)


def _unpack(packed, shapes, names=_SMALL_NAMES):
    flat = packed.reshape(-1)
    out, off = {}, 0
    for n in names:
        size = 1
        for s in shapes[n]:
            size *= s
        out[n] = flat[off:off + size].reshape(shapes[n])
        off += size
    return out


def kernel(x, norm_mix_pre, w_in, gm_ln_w, gm_ln_b, gm_w_s, gm_b_s, conv_w, conv_b, dt_bias, a_log, d_skip, ssm_norm_w, w_out, norm_mix_post, norm_ffn_pre, w_up, w_down, norm_ffn_post, loss_target, m_norm_mix_pre, m_w_in, m_gm_ln_w, m_gm_ln_b, m_gm_w_s, m_gm_b_s, m_conv_w, m_conv_b, m_dt_bias, m_a_log, m_d_skip, m_ssm_norm_w, m_w_out, m_norm_mix_post, m_norm_ffn_pre, m_w_up, m_w_down, m_norm_ffn_post, v_norm_mix_pre, v_w_in, v_gm_ln_w, v_gm_ln_b, v_gm_w_s, v_gm_b_s, v_conv_w, v_conv_b, v_dt_bias, v_a_log, v_d_skip, v_ssm_norm_w, v_w_out, v_norm_mix_post, v_norm_ffn_pre, v_w_up, v_w_down, v_norm_ffn_post):
    params = dict(norm_mix_pre=norm_mix_pre, w_in=w_in, gm_ln_w=gm_ln_w, gm_ln_b=gm_ln_b, gm_w_s=gm_w_s, gm_b_s=gm_b_s,
                  conv_w=conv_w, conv_b=conv_b, dt_bias=dt_bias, a_log=a_log, d_skip=d_skip, ssm_norm_w=ssm_norm_w,
                  w_out=w_out, norm_mix_post=norm_mix_post, norm_ffn_pre=norm_ffn_pre, w_up=w_up, w_down=w_down,
                  norm_ffn_post=norm_ffn_post)
    mom1 = dict(norm_mix_pre=m_norm_mix_pre, w_in=m_w_in, gm_ln_w=m_gm_ln_w, gm_ln_b=m_gm_ln_b, gm_w_s=m_gm_w_s,
                gm_b_s=m_gm_b_s, conv_w=m_conv_w, conv_b=m_conv_b, dt_bias=m_dt_bias, a_log=m_a_log, d_skip=m_d_skip,
                ssm_norm_w=m_ssm_norm_w, w_out=m_w_out, norm_mix_post=m_norm_mix_post, norm_ffn_pre=m_norm_ffn_pre,
                w_up=m_w_up, w_down=m_w_down, norm_ffn_post=m_norm_ffn_post)
    mom2 = dict(norm_mix_pre=v_norm_mix_pre, w_in=v_w_in, gm_ln_w=v_gm_ln_w, gm_ln_b=v_gm_ln_b, gm_w_s=v_gm_w_s,
                gm_b_s=v_gm_b_s, conv_w=v_conv_w, conv_b=v_conv_b, dt_bias=v_dt_bias, a_log=v_a_log, d_skip=v_d_skip,
                ssm_norm_w=v_ssm_norm_w, w_out=v_w_out, norm_mix_post=v_norm_mix_post, norm_ffn_pre=v_norm_ffn_pre,
                w_up=v_w_up, w_down=v_w_down, norm_ffn_post=v_norm_ffn_post)
    names = list(params)
    big = ("w_in", "w_out", "w_up", "w_down")
    chip = 2 * lax.axis_index("x") + lax.axis_index("y")

    shards = {n: params[n][0].astype(BF16) for n in big}
    conv_shard = jnp.pad(conv_w[0], ((0, 16 - CONV_K), (0, 0)))
    g_in4, g_conv4 = _run_exchange("allgather_w_in", _allgather_exchange([shards["w_in"], conv_shard]))
    conv_full = jnp.transpose(g_conv4[:, :CONV_K, :], (1, 0, 2)).reshape(CONV_K, CONV_CH)

    small = {n: params[n][0] if params[n].ndim >= 3 else params[n] for n in _SMALL_NAMES if n != "conv_w"}
    core = lax.axis_index("c").astype(jnp.int32).reshape(1)
    adam_args = {n: (params[n][0], mom1[n][0], mom2[n][0]) for n in big}
    loss, grad_x, big_out, small_sum = _forward_backward(
        x, loss_target, g_in4, conv_full, small, shards["w_out"], shards["w_up"], shards["w_down"], core, adam_args)
    grads, delta, new_m, new_v = {}, {}, {}, {}
    for n in big:
        grads[n], delta[n], new_m[n], new_v[n] = [a[None] for a in big_out[n]]

    small_sum["conv_w"] = lax.dynamic_slice_in_dim(small_sum["conv_w"], chip * (CONV_CH // N_CHIPS), CONV_CH // N_CHIPS, axis=1)

    local_shapes = {n: params[n].shape[1:] if params[n].ndim >= 3 else params[n].shape for n in _SMALL_NAMES}
    flat = lambda tree: {n: tree[n].reshape(local_shapes[n]) for n in _SMALL_NAMES}
    packed = [_pack(flat(t)) for t in (params, small_sum, mom1, mom2)]
    _, d_p, m_p, v_p = _adamw("adamw_small", *packed, packed[0].shape[0])
    for src, dst in ((d_p, delta), (m_p, new_m), (v_p, new_v)):
        for n, val in _unpack(src, local_shapes).items():
            dst[n] = val.reshape(params[n].shape)
    for n in _SMALL_NAMES:
        grads[n] = small_sum[n].reshape(params[n].shape)

    out = [loss, grad_x]
    for tree in (grads, delta, new_m, new_v):
        out += [tree[n] for n in names]
    return tuple(out)
```

```python
import functools

import jax
import jax.numpy as jnp
from jax import lax
from jax.experimental import pallas as pl
from jax.experimental.pallas import tpu as pltpu

F32 = jnp.float32
BF16 = jnp.bfloat16
HI = lax.Precision.HIGHEST
MESH = pl.DeviceIdType.MESH

EPS = 1e-6
D_MODEL = 1024
GM_WIDTH = 512
SSM_WIDTH = 512
N_HEADS = 8
HEAD_DIM = 64
CHUNK = 128
SSM_GROUPS = 2
GROUP_W = SSM_WIDTH // SSM_GROUPS
SSM_STATE = 128
CONV_K = 4
CONV_CH = 1024
D_FF = 4096
IN_COLS = 2568
DT_PAD = 128
N_CHIPS = 4
N_DEV = 8

ADAM_LR = 0.001
ADAM_B1 = 0.9
ADAM_B2 = 0.999
ADAM_EPS = 1e-08
ADAM_WD = 0.01
ADAM_STEP = 10

VMEM_LIMIT_BYTES = 56 * 1024 * 1024
FF_TILE = 512
DW_TOKENS_PER_STEP = 2048


def _cparams(n_axes):
    return pltpu.CompilerParams(dimension_semantics=("arbitrary",) * n_axes, vmem_limit_bytes=VMEM_LIMIT_BYTES)


def _dot(a, b):
    return jnp.dot(a.astype(BF16), b.astype(BF16), preferred_element_type=F32)


def _dot_nt(a, b):
    return lax.dot_general(a.astype(BF16), b.astype(BF16), (((1,), (1,)), ((), ())), preferred_element_type=F32)


def _dot_tn(a, b):
    return lax.dot_general(a.astype(BF16), b.astype(BF16), (((0,), (0,)), ((), ())), preferred_element_type=F32)


def _sigmoid(x):
    return 1.0 / (1.0 + jnp.exp(-x))


_GELU_C = 0.7978845608028654
_GELU_A = 0.044715


def _gelu(x):
    t = jnp.tanh(_GELU_C * (x + _GELU_A * (x * x * x)))
    return 0.5 * x * (1.0 + t), t


def _gelu_grad(x, t):
    return 0.5 * (1.0 + t) + 0.5 * x * (1.0 - t * t) * (_GELU_C * (1.0 + 3.0 * _GELU_A * x * x))


def _rms_fwd(x, w):
    r = lax.rsqrt(jnp.mean(x * x, axis=-1, keepdims=True) + EPS)
    return x * r * w, r


def _rms_bwd(x, r, w, dy):
    g = dy * w
    dx = r * g - x * (r * r * r) * jnp.mean(g * x, axis=-1, keepdims=True)
    dw = jnp.sum(dy * x * r, axis=0, keepdims=True)
    return dx, dw


class _Carried:
    def __init__(self, ins, out_shapes, sems, start, finish):
        self.ins, self.out_shapes, self.sems = list(ins), list(out_shapes), list(sems)
        self.start, self.finish = start, finish


def _both(first, second):
    n_i, n_o, n_s = len(first.ins), len(first.out_shapes), len(first.sems)

    def split(ins, outs, sems):
        return (ins[:n_i], outs[:n_o], sems[:n_s]), (ins[n_i:], outs[n_o:], sems[n_s:])

    def start(ins, outs, *sems):
        (i1, o1, s1), (i2, o2, s2) = split(ins, outs, sems)
        first.start(i1, o1, *s1)
        second.start(i2, o2, *s2)

    def finish(ins, outs, *sems):
        (i1, o1, s1), (i2, o2, s2) = split(ins, outs, sems)
        first.finish(i1, o1, *s1)
        second.finish(i2, o2, *s2)

    return _Carried(first.ins + second.ins, first.out_shapes + second.out_shapes, first.sems + second.sems, start, finish)


def _split_carried(refs, n_in, n_out, n_scratch, carried):
    n_ci, n_co, n_cs = len(carried.ins), len(carried.out_shapes), len(carried.sems)
    ins, rest = refs[:n_in], refs[n_in:]
    c_ins, rest = rest[:n_ci], rest[n_ci:]
    outs, rest = rest[:n_out], rest[n_out:]
    c_outs, rest = rest[:n_co], rest[n_co:]
    scr, c_sems = rest[:n_scratch], rest[n_scratch:]
    assert len(c_sems) == n_cs
    return tuple(ins) + tuple(outs) + tuple(scr), c_ins, c_outs, c_sems


def _rows_call(name, body, tm, row_ins, const_ins, row_outs, acc_outs=(), scratch=(), carried=None):
    n_rows = row_ins[0].shape[0]
    assert n_rows % tm == 0
    n_steps = n_rows // tm
    n_in = len(row_ins) + len(const_ins)
    n_ro = len(row_outs)
    n_acc = len(acc_outs)

    def kern(*refs):
        accs = refs[n_in + n_ro:n_in + n_ro + n_acc]

        @pl.when(pl.program_id(0) == 0)
        def _():
            for a in accs:
                a[...] = jnp.zeros_like(a)

        body(*refs)

    def whole(shape):
        nd = len(shape)
        return pl.BlockSpec(tuple(shape), lambda i: (0,) * nd)

    in_specs = [pl.BlockSpec((tm, a.shape[1]), lambda i: (i, 0)) for a in row_ins]
    in_specs += [whole(a.shape) for a in const_ins]
    out_specs = [pl.BlockSpec((tm, s.shape[1]), lambda i: (i, 0)) for s in row_outs]
    out_specs += [whole(s.shape) for s in acc_outs]
    return _call_carrying(
        kern, carried, name=name, grid=(n_steps,), in_specs=in_specs, out_specs=out_specs,
        out_shape=tuple(row_outs) + tuple(acc_outs), scratch_shapes=list(scratch), operands=list(row_ins) + list(const_ins))


def _call_carrying(body, carried, *, name, grid, in_specs, out_specs, out_shape, scratch_shapes, operands):
    n_in, n_out, n_scratch = len(in_specs), len(out_specs), len(scratch_shapes)
    kern = body
    if carried is not None:
        def kern(*refs):
            plain, c_ins, c_outs, c_sems = _split_carried(refs, n_in, n_out, n_scratch, carried)
            first, last = True, True
            for d, size in enumerate(grid):
                first = jnp.logical_and(first, pl.program_id(d) == 0)
                last = jnp.logical_and(last, pl.program_id(d) == size - 1)

            @pl.when(first)
            def _():
                carried.start(c_ins, c_outs, *c_sems)

            body(*plain)

            @pl.when(last)
            def _():
                carried.finish(c_ins, c_outs, *c_sems)

        in_specs = list(in_specs) + [_HBM] * len(carried.ins)
        out_specs = list(out_specs) + [_HBM] * len(carried.out_shapes)
        out_shape = tuple(out_shape) + tuple(carried.out_shapes)
        operands = list(operands) + carried.ins
        scratch_shapes = list(scratch_shapes) + carried.sems
    return pl.pallas_call(
        kern, name=name, grid=grid, in_specs=in_specs, out_specs=out_specs, out_shape=out_shape,
        scratch_shapes=scratch_shapes, compiler_params=_cparams(len(grid)),
    )(*operands)


def _sds(shape, dtype):
    return jax.ShapeDtypeStruct(tuple(shape), dtype)


def _matmul_tn(name, a, b, tm, tn, tk, stacked=False, carried=None):
    k_dim, m_dim = a.shape
    n_dim = b.shape[1]
    assert m_dim % tm == 0 and n_dim % tn == 0 and k_dim % tk == 0
    nk = k_dim // tk

    def kern(a_ref, b_ref, o_ref, acc_ref):
        k = pl.program_id(2)
        prod = _dot_tn(a_ref[...], b_ref[...])

        @pl.when(k == 0)
        def _():
            acc_ref[...] = prod

        @pl.when(k > 0)
        def _():
            acc_ref[...] += prod

        @pl.when(k == nk - 1)
        def _():
            o_ref[...] = acc_ref[...].astype(o_ref.dtype)

    if stacked:
        assert tm == m_dim
        out_shape = _sds((n_dim // tn, m_dim, tn), BF16)
        out_spec = pl.BlockSpec((None, tm, tn), lambda i, j, k: (j, i, 0))
    else:
        out_shape = _sds((m_dim, n_dim), BF16)
        out_spec = pl.BlockSpec((tm, tn), lambda i, j, k: (i, j))
    outs = _call_carrying(
        kern, carried, name=name, grid=(m_dim // tm, n_dim // tn, nk),
        in_specs=[pl.BlockSpec((tk, tm), lambda i, j, k: (k, i)), pl.BlockSpec((tk, tn), lambda i, j, k: (k, j))],
        out_specs=[out_spec], out_shape=(out_shape,), scratch_shapes=[pltpu.VMEM((tm, tn), F32)], operands=[a, b])
    return outs[0] if carried is None else outs


def _inproj_fwd(x, nw, w_uv, w_xbc, w_z, w_dt, tm=256, carried=None):
    n_tok = x.shape[0]

    def body(x_ref, nw_ref, wuv_ref, wxbc_ref, wz_ref, wdt_ref, puv_ref, pxbc_ref, pz_ref, pdt_ref):
        h, _ = _rms_fwd(x_ref[...], nw_ref[...])
        h = h.astype(BF16)
        puv_ref[...] = jnp.dot(h, wuv_ref[...], preferred_element_type=F32)
        pxbc_ref[...] = jnp.dot(h, wxbc_ref[...], preferred_element_type=F32)
        pz_ref[...] = jnp.dot(h, wz_ref[...], preferred_element_type=F32)
        pdt_ref[...] = jnp.dot(h, wdt_ref[...], preferred_element_type=F32)

    return _rows_call(
        "inproj_fwd", body, tm, [x], [nw, w_uv, w_xbc, w_z, w_dt],
        [_sds((n_tok, 2 * GM_WIDTH), F32), _sds((n_tok, CONV_CH), F32), _sds((n_tok, SSM_WIDTH), F32),
         _sds((n_tok, DT_PAD), F32)], carried=carried)


def _head_lane_mask(width, head):
    lane = lax.broadcasted_iota(jnp.int32, (1, width), 1)
    return (lane // HEAD_DIM) == head


def _split_terms(x, terms):
    parts = []
    for _ in range(terms):
        p = x.astype(BF16)
        parts.append(p)
        x = x - p.astype(F32)
    return parts


def _seg_dots(vals, ind, terms=2):
    m = vals[0].shape[0]
    parts = []
    for v in vals:
        parts += _split_terms(v, terms)
    red = jnp.dot(jnp.concatenate(parts, axis=0), ind, preferred_element_type=F32)
    outs = []
    for i in range(len(vals)):
        acc = red[i * terms * m:(i * terms + 1) * m]
        for t in range(1, terms):
            acc = acc + red[(i * terms + t) * m:(i * terms + t + 1) * m]
        outs.append(acc)
    return outs


def _tri_dot(mask, x, terms=3):
    n = x.shape[1]
    red = jnp.dot(mask.astype(BF16), jnp.concatenate(_split_terms(x, terms), axis=1), preferred_element_type=F32)
    acc = red[:, :n]
    for t in range(1, terms):
        acc = acc + red[:, t * n:(t + 1) * n]
    return acc


def _gmlp_common(puv, lnw, lnb, e_bf, et_bf):
    u = puv[:, :GM_WIDTH]
    v = puv[:, GM_WIDTH:]
    gu, tu = _gelu(u)
    gv, tv = _gelu(v)
    (s1,) = _seg_dots([gv], et_bf)
    (mu,) = _seg_dots([s1 * (1.0 / HEAD_DIM)], e_bf)
    xc = gv - mu
    (s2,) = _seg_dots([xc * xc], et_bf)
    (rstd,) = _seg_dots([lax.rsqrt(s2 * (1.0 / HEAD_DIM) + EPS)], e_bf)
    xhat = xc * rstd
    vn = xhat * lnw + lnb
    return u, v, gu, tu, tv, rstd, xhat, vn


def _tril_mask():
    r = lax.broadcasted_iota(jnp.int32, (CHUNK, CHUNK), 0)
    c = lax.broadcasted_iota(jnp.int32, (CHUNK, CHUNK), 1)
    return r >= c


def _head_blocks(v):
    return jnp.concatenate([jnp.where(_head_lane_mask(GM_WIDTH, h), v, jnp.zeros_like(v)) for h in range(N_HEADS)], axis=0)


def _causal_w_cat(w_cat):
    t = lax.broadcasted_iota(jnp.int32, (CHUNK, N_HEADS * CHUNK), 0)
    s = lax.broadcasted_iota(jnp.int32, (CHUNK, N_HEADS * CHUNK), 1) % CHUNK
    return jnp.where(t >= s, w_cat, 0.0).astype(BF16)


def _gmlp_chunk_fwd(puv, lnw, lnb, e_bf, et_bf, wm, bmap):
    _, _, gu, _, _, _, _, vn = _gmlp_common(puv, lnw, lnb, e_bf, et_bf)
    mixed = jnp.dot(wm, _head_blocks(vn.astype(BF16)), preferred_element_type=F32) + bmap
    return (gu * mixed).astype(BF16)


SUBLANES = 8


def _shift_down(x, tail, s):
    main = pltpu.roll(x, s, 0)
    row = lax.broadcasted_iota(jnp.int32, (SUBLANES, 1), 0)
    head = jnp.where(row < s, pltpu.roll(tail, s, 0), main[:SUBLANES])
    return jnp.concatenate([head, main[SUBLANES:]], axis=0)


def _shift_up(x, head_next, s):
    n = x.shape[0]
    main = pltpu.roll(x, n - s, 0)
    row = lax.broadcasted_iota(jnp.int32, (SUBLANES, 1), 0)
    last = jnp.where(row >= SUBLANES - s, pltpu.roll(head_next, SUBLANES - s, 0), main[n - SUBLANES:])
    return jnp.concatenate([main[:n - SUBLANES], last], axis=0)


def _ssd_pre(xr, tail, cw_ref, cb, pdt, dtb, alog, emap):
    rowi = lax.broadcasted_iota(jnp.int32, (CHUNK, 1), 0)
    shifted = [_shift_down(xr, tail, 3), _shift_down(xr, tail, 2), _shift_down(xr, tail, 1), xr]
    xc = cb
    for k in range(CONV_K):
        xc = xc + cw_ref[k] * shifted[k]
    sg = _sigmoid(xc)
    xa = xc * sg
    pre = pdt + dtb
    dt = jnp.maximum(pre, 0.0) + jnp.log(1.0 + jnp.exp(-jnp.abs(pre)))
    a_neg = -jnp.exp(alog)
    a_cs = _tri_dot(_tril_mask(), dt * a_neg)
    acs_map, dt_map = _seg_dots([a_cs, dt], emap, terms=3)
    return dict(shifted=shifted, xc=xc, sg=sg, xa=xa, pre=pre, dt=dt, a_neg=a_neg, a_cs=a_cs,
                acs_map=acs_map, dt_map=dt_map, rowi=rowi)


def _ssd_maps(p):
    last = p["rowi"] == CHUNK - 1
    aq_map = jnp.sum(jnp.where(last, p["acs_map"], 0.0), axis=0, keepdims=True)
    e_exp = jnp.exp(p["acs_map"])
    dte = jnp.exp(aq_map - p["acs_map"])
    cd = jnp.exp(aq_map)
    return last, e_exp, dte, cd


def _head_decay(a_cs, a_cs_t, head, tri):
    lane = lax.broadcasted_iota(jnp.int32, (1, DT_PAD), 1)
    sub = lax.broadcasted_iota(jnp.int32, (DT_PAD, 1), 0)
    col = jnp.sum(jnp.where(lane == head, a_cs, 0.0), axis=1, keepdims=True)
    row = jnp.sum(jnp.where(sub == head, a_cs_t, 0.0), axis=0, keepdims=True)
    return jnp.exp(jnp.where(tri, col - row, -1e30))


def _gate_fwd(y, z, nw):
    sz = _sigmoid(z)
    zg = z * sz
    yg = y * zg
    outs, rs = [], []
    for g in range(SSM_GROUPS):
        gs = slice(g * GROUP_W, (g + 1) * GROUP_W)
        o, r = _rms_fwd(yg[:, gs], nw[:, gs])
        outs.append(o)
        rs.append(r)
    return sz, zg, yg, outs, rs


def _ssd_const_specs():
    def whole(shape):
        nd = len(shape)
        return pl.BlockSpec(tuple(shape), lambda c: (0,) * nd)
    return [whole((CONV_K, 1, CONV_CH)), whole((1, CONV_CH)), whole((1, DT_PAD)), whole((1, DT_PAD)),
            whole((1, SSM_WIDTH)), whole((1, SSM_WIDTH)), whole((DT_PAD, SSM_WIDTH)), whole((SSM_WIDTH, DT_PAD))]


def _mixer_fwd(p_uv, p_xbc, p_z, p_dt, lnw, lnb, w_cat, bmap, conv_w, conv_b, dt_bias, a_log, dskip_map, norm_w,
               e_bf, et_bf, n_seq, carried=None):
    n_tok = p_xbc.shape[0]
    nc = n_tok // n_seq // CHUNK

    def body(puv3, xr3, z3, pdt3, lnw_ref, lnb_ref, wcat_ref, bmap_ref,
             cw_ref, cb_ref, dtb_ref, alog_ref, dsk_ref, nw_ref, e_ref, et_ref,
             ya3, yb3, yssd3, sprev3, wm_scr, prev3_scr, s3_scr):
        @pl.when(pl.program_id(0) == 0)
        def _():
            wm_scr[...] = _causal_w_cat(wcat_ref[...])
            prev3_scr[...] = jnp.zeros_like(prev3_scr)
            s3_scr[...] = jnp.zeros_like(s3_scr)

        for b in range(n_seq):
            one_sequence(puv3.at[b], xr3.at[b], z3.at[b], pdt3.at[b], lnw_ref, lnb_ref, bmap_ref,
                         cw_ref, cb_ref, dtb_ref, alog_ref, dsk_ref, nw_ref, e_ref, et_ref,
                         ya3.at[b], yb3.at[b], yssd3.at[b], sprev3.at[b], wm_scr, prev3_scr.at[b], s3_scr.at[b])

    def one_sequence(puv_ref, xr_ref, z_ref, pdt_ref, lnw_ref, lnb_ref, bmap_ref,
                     cw_ref, cb_ref, dtb_ref, alog_ref, dsk_ref, nw_ref, e_ref, et_ref,
                     ya_ref, yb_ref, yssd_ref, sprev_ref, wm_scr, prev_scr, s_scr):
        ya_ref[...] = _gmlp_chunk_fwd(puv_ref[...], lnw_ref[...], lnb_ref[...], e_ref[...], et_ref[...], wm_scr[...],
                                      bmap_ref[...])
        xr = xr_ref[...]
        p = _ssd_pre(xr, prev_scr[...], cw_ref, cb_ref[...], pdt_ref[...], dtb_ref[...], alog_ref[...], e_ref[...])
        _, e_exp, dte, cd = _ssd_maps(p)
        xs = p["xa"][:, :SSM_WIDTH]
        xd = xs * p["dt_map"]
        a_cs_t = p["a_cs"].T
        tri = _tril_mask()
        s_old = s_scr[...]
        sprev_ref[...] = s_old
        for g in range(SSM_GROUPS):
            gs = slice(g * GROUP_W, (g + 1) * GROUP_W)
            bm = p["xa"][:, SSM_WIDTH + g * SSM_STATE: SSM_WIDTH + (g + 1) * SSM_STATE].astype(BF16)
            cm = p["xa"][:, SSM_WIDTH + (SSM_GROUPS + g) * SSM_STATE: SSM_WIDTH + (SSM_GROUPS + g + 1) * SSM_STATE].astype(BF16)
            cb_mat = _dot_nt(cm, bm)
            xdg = xd[:, gs].astype(BF16)
            y_g = _dot(cm, s_old[:, gs]) * e_exp[:, gs] + dsk_ref[:, gs] * xs[:, gs]
            for r in range(SSM_GROUPS * 2):
                dm = _head_decay(p["a_cs"], a_cs_t, g * 4 + r, tri)
                full = jnp.dot((cb_mat * dm).astype(BF16), xdg, preferred_element_type=F32)
                y_g = y_g + jnp.where(_head_lane_mask(GROUP_W, r), full, 0.0)
            yssd_ref[:, gs] = y_g
            s_scr[:, gs] = cd[:, gs] * s_old[:, gs] + _dot_tn(bm, xd[:, gs] * dte[:, gs])
        _, _, _, outs, _ = _gate_fwd(yssd_ref[...], z_ref[...], nw_ref[...])
        for g in range(SSM_GROUPS):
            yb_ref[:, g * GROUP_W:(g + 1) * GROUP_W] = outs[g].astype(BF16)
        prev_scr[...] = xr[CHUNK - SUBLANES:, :]

    seq_len = n_tok // n_seq

    def rows(width):
        return pl.BlockSpec((n_seq, CHUNK, width), lambda c: (0, c, 0))

    def whole(shape):
        nd = len(shape)
        return pl.BlockSpec(tuple(shape), lambda c: (0,) * nd)

    def by_seq(a):
        return a.reshape(n_seq, seq_len, a.shape[-1])

    outs = _call_carrying(
        body, carried, name="mixer_fwd", grid=(nc,),
        in_specs=[rows(2 * GM_WIDTH), rows(CONV_CH), rows(SSM_WIDTH), rows(DT_PAD), whole(lnw.shape), whole(lnb.shape),
                  whole(w_cat.shape), whole(bmap.shape)] + _ssd_const_specs(),
        out_specs=[rows(GM_WIDTH), rows(SSM_WIDTH), rows(SSM_WIDTH), rows(SSM_WIDTH)],
        out_shape=(_sds((n_seq, seq_len, GM_WIDTH), BF16), _sds((n_seq, seq_len, SSM_WIDTH), BF16),
                   _sds((n_seq, seq_len, SSM_WIDTH), F32), _sds((n_seq, seq_len, SSM_WIDTH), F32)),
        scratch_shapes=[pltpu.VMEM((CHUNK, N_HEADS * CHUNK), BF16), pltpu.VMEM((n_seq, SUBLANES, CONV_CH), F32),
                        pltpu.VMEM((n_seq, SSM_STATE, SSM_WIDTH), F32)],
        operands=[by_seq(p_uv), by_seq(p_xbc), by_seq(p_z), by_seq(p_dt), lnw, lnb, w_cat, bmap, conv_w, conv_b, dt_bias,
                  a_log, dskip_map, norm_w, e_bf, et_bf])
    return tuple(o.reshape(n_tok, o.shape[-1]) for o in outs[:4]) + tuple(outs[4:])


def _outproj_fwd(ya, yb, x, w_out, nw_post, nw_pre2, tm=256):
    n_tok = x.shape[0]

    def body(ya_ref, yb_ref, x_ref, wo_ref, nwa_ref, nwb_ref, o_ref, x1_ref, h2_ref):
        o = jnp.dot(ya_ref[...], wo_ref[:GM_WIDTH, :], preferred_element_type=F32)
        o = o + jnp.dot(yb_ref[...], wo_ref[GM_WIDTH:, :], preferred_element_type=F32)
        on, _ = _rms_fwd(o, nwa_ref[...])
        x1 = x_ref[...] + on
        h2, _ = _rms_fwd(x1, nwb_ref[...])
        o_ref[...] = o
        x1_ref[...] = x1
        h2_ref[...] = h2.astype(BF16)

    return _rows_call("outproj_fwd", body, tm, [ya, yb, x], [w_out, nw_post, nw_pre2],
                      [_sds((n_tok, D_MODEL), F32), _sds((n_tok, D_MODEL), F32), _sds((n_tok, D_MODEL), BF16)])


def _up_cols(wup_ref, j):
    per = (D_FF // N_CHIPS) // FF_TILE
    return wup_ref[j // per, :, (j % per) * FF_TILE:(j % per + 1) * FF_TILE]


def _down_rows(wda_ref, wdb_ref, j):
    assert 2 * FF_TILE == D_FF // N_CHIPS
    return (wda_ref if j % 2 == 0 else wdb_ref)[j // 2]


def _skewed_rows_call(name, main, tail, tm, lead_ins, lag_ins, const_ins, lead_outs, lag_outs, acc_outs, carry):
    n_rows = lead_ins[0].shape[0]
    assert n_rows % tm == 0
    n = n_rows // tm
    counts = [len(lead_ins), len(lag_ins), len(const_ins), len(lead_outs), len(lag_outs), len(acc_outs)]

    def kern(*refs):
        groups, pos = [], 0
        for cnt in counts:
            groups.append(refs[pos:pos + cnt])
            pos += cnt
        lead_i, lag_i, consts, lead_o, lag_o, accs = groups
        carry_scr = refs[pos]
        i = pl.program_id(0)

        @pl.when(i == 0)
        def _():
            for a in accs:
                a[...] = jnp.zeros_like(a)
            carry_scr[...] = main(lead_i, consts, lead_o)

        @pl.when(jnp.logical_and(i > 0, i < n))
        def _():
            previous = carry_scr[...]
            carry_scr[...] = main(lead_i, consts, lead_o)
            tail(previous, lag_i, consts, lag_o, accs)

        @pl.when(i == n)
        def _():
            tail(carry_scr[...], lag_i, consts, lag_o, accs)

    def lead(width):
        return pl.BlockSpec((tm, width), lambda i: (jnp.minimum(i, n - 1), 0))

    def lag(width):
        return pl.BlockSpec((tm, width), lambda i: (jnp.maximum(i - 1, 0), 0))

    def whole(shape, **kw):
        nd = len(shape)
        return pl.BlockSpec(tuple(shape), lambda i: (0,) * nd, **kw)

    const_specs = [whole(a.shape, pipeline_mode=pl.Buffered(1)) for a in const_ins]
    return pl.pallas_call(
        kern, name=name, grid=(n + 1,),
        in_specs=[lead(a.shape[1]) for a in lead_ins] + [lag(a.shape[1]) for a in lag_ins] + const_specs,
        out_specs=[lead(s.shape[1]) for s in lead_outs] + [lag(s.shape[1]) for s in lag_outs] + [whole(s.shape) for s in acc_outs],
        out_shape=tuple(lead_outs) + tuple(lag_outs) + tuple(acc_outs),
        scratch_shapes=[pltpu.VMEM(carry, F32)], compiler_params=_cparams(1),
    )(*lead_ins, *lag_ins, *const_ins)


def _mlp_fwd(h2, x1, tgt, w_up, w_down_a, w_down_b, nw, tm=512):
    n_tok = x1.shape[0]

    def main(lead_i, consts, lead_o):
        (h2_ref,), (wup_ref, wda_ref, wdb_ref, _), (f_ref,) = lead_i, consts, lead_o
        h2v = h2_ref[...]
        acc = jnp.zeros((tm, D_MODEL), F32)
        for j in range(D_FF // FF_TILE):
            cs = slice(j * FF_TILE, (j + 1) * FF_TILE)
            u = jnp.dot(h2v, _up_cols(wup_ref, j), preferred_element_type=F32)
            f = jnp.square(jnp.maximum(u, 0.0)).astype(BF16)
            f_ref[:, cs] = f
            acc = acc + jnp.dot(f, _down_rows(wda_ref, wdb_ref, j), preferred_element_type=F32)
        return acc

    def tail(acc, lag_i, consts, lag_o, accs):
        (x1_ref, tgt_ref), nw_ref, (dd_ref, dy_ref), (loss_ref, dnw_ref) = lag_i, consts[3], lag_o, accs
        dn, r = _rms_fwd(acc, nw_ref[...])
        e = x1_ref[...] + dn - tgt_ref[...]
        loss_ref[...] += jnp.full(loss_ref.shape, (0.5 / D_MODEL) * jnp.sum(e * e), F32)
        dy = e * (1.0 / D_MODEL)
        dd, dnw = _rms_bwd(acc, r, nw_ref[...], dy)
        dy_ref[...] = dy
        dd_ref[...] = dd.astype(BF16)
        dnw_ref[...] += dnw

    return _skewed_rows_call(
        "mlp_fwd", main, tail, tm, [h2], [x1, tgt], [w_up, w_down_a, w_down_b, nw],
        [_sds((n_tok, D_FF), BF16)], [_sds((n_tok, D_MODEL), BF16), _sds((n_tok, D_MODEL), F32)],
        [_sds((8, 128), F32), _sds((1, D_MODEL), F32)], carry=(tm, D_MODEL))


def _mlp_bwd(dd, f, x1, dy, w_down_a, w_down_b, w_up, nw, tm=256):
    n_tok = x1.shape[0]

    def main(lead_i, consts, lead_o):
        (dd_ref, f_ref), (wda_ref, wdb_ref, wup_ref, _), (dup_ref,) = lead_i, consts, lead_o
        ddv = dd_ref[...]
        acc = jnp.zeros((tm, D_MODEL), F32)
        for j in range(D_FF // FF_TILE):
            cs = slice(j * FF_TILE, (j + 1) * FF_TILE)
            df = _dot_nt(ddv, _down_rows(wda_ref, wdb_ref, j))
            du = (df * (2.0 * jnp.sqrt(f_ref[:, cs].astype(F32)))).astype(BF16)
            dup_ref[:, cs] = du
            acc = acc + _dot_nt(du, _up_cols(wup_ref, j))
        return acc

    def tail(acc, lag_i, consts, lag_o, accs):
        (x1_ref, dy_ref), nw_ref, (dx1_ref,), (dnw_ref,) = lag_i, consts[3], lag_o, accs
        x1v = x1_ref[...]
        _, r = _rms_fwd(x1v, nw_ref[...])
        dx, dnw = _rms_bwd(x1v, r, nw_ref[...], acc)
        dx1_ref[...] = dy_ref[...] + dx
        dnw_ref[...] += dnw

    return _skewed_rows_call(
        "mlp_bwd", main, tail, tm, [dd, f], [x1, dy], [w_down_a, w_down_b, w_up, nw],
        [_sds((n_tok, D_FF), BF16)], [_sds((n_tok, D_MODEL), F32)], [_sds((1, D_MODEL), F32)], carry=(tm, D_MODEL))


def _outproj_bwd(dx1, o, w_out, nw, tm=256, carried=None):
    n_tok = dx1.shape[0]

    def body(dx1_ref, o_ref, wo_ref, nw_ref, do_ref, dya_ref, dyb_ref, dnw_ref):
        ov = o_ref[...]
        _, r = _rms_fwd(ov, nw_ref[...])
        do, dnw = _rms_bwd(ov, r, nw_ref[...], dx1_ref[...])
        dob = do.astype(BF16)
        do_ref[...] = dob
        dya_ref[...] = _dot_nt(dob, wo_ref[:GM_WIDTH, :])
        dyb_ref[...] = _dot_nt(dob, wo_ref[GM_WIDTH:, :])
        dnw_ref[...] += dnw

    return _rows_call("outproj_bwd", body, tm, [dx1, o], [w_out, nw],
                      [_sds((n_tok, D_MODEL), BF16), _sds((n_tok, GM_WIDTH), F32), _sds((n_tok, SSM_WIDTH), F32)],
                      [_sds((1, D_MODEL), F32)], carried=carried)


def _gmlp_bwd(p_uv, dya, lnw, lnb, e_bf, et_bf, w_cat, w_stack, bmap, carried=None):
    n_tok = p_uv.shape[0]
    chunks_per_step = 2

    def body(puv_ref, dya_ref, lnw_ref, lnb_ref, e_ref, et_ref, wcat_ref, wstack_ref, bmap_ref,
             dpuv_ref, dws_ref, dbs_ref, dlnw_ref, dlnb_ref, wm_scr, wsm_scr):
        t_stk = lax.broadcasted_iota(jnp.int32, (N_HEADS * CHUNK, CHUNK), 0) % CHUNK
        s_stk = lax.broadcasted_iota(jnp.int32, (N_HEADS * CHUNK, CHUNK), 1)

        @pl.when(pl.program_id(0) == 0)
        def _():
            wm_scr[...] = _causal_w_cat(wcat_ref[...])
            wsm_scr[...] = jnp.where(t_stk >= s_stk, wstack_ref[...], 0.0).astype(BF16)

        lnw_v = lnw_ref[...]
        e_v, et_v = e_ref[...], et_ref[...]

        def one_chunk(rows):
            u, v, gu, tu, tv, rstd, xhat, vn = _gmlp_common(puv_ref[rows, :], lnw_v, lnb_ref[...], e_v, et_v)
            vnb = vn.astype(BF16)
            mixed = jnp.dot(wm_scr[...], _head_blocks(vnb), preferred_element_type=F32) + bmap_ref[...]
            dy = dya_ref[rows, :]
            du = dy * mixed * _gelu_grad(u, tu)
            dmixed = dy * gu
            (dbs,) = _seg_dots([dmixed], et_v)
            dblocks = _head_blocks(dmixed.astype(BF16))
            dvn = lax.dot_general(wsm_scr[...], dblocks, (((0,), (0,)), ((), ())), preferred_element_type=F32)
            dws = lax.dot_general(dblocks, vnb, (((1,), (1,)), ((), ())), preferred_element_type=F32)
            dxh = dvn * lnw_v
            m1, m2 = _seg_dots([dxh, dxh * xhat], et_v)
            m1, m2 = _seg_dots([m1 * (1.0 / HEAD_DIM), m2 * (1.0 / HEAD_DIM)], e_v)
            dgv = rstd * (dxh - m1 - xhat * m2)
            dv = dgv * _gelu_grad(v, tv)
            dpuv_ref[rows, :GM_WIDTH] = du.astype(BF16)
            dpuv_ref[rows, GM_WIDTH:] = dv.astype(BF16)
            return dbs, dws, jnp.sum(dvn * xhat, axis=0, keepdims=True), jnp.sum(dvn, axis=0, keepdims=True)

        parts = [one_chunk(slice(k * CHUNK, (k + 1) * CHUNK)) for k in range(chunks_per_step)]
        dbs, dws, dlnw, dlnb = [functools.reduce(lambda a, b: a + b, vals) for vals in zip(*parts)]
        dbs_ref[...] += dbs
        dws_ref[...] += jnp.where(t_stk >= s_stk, dws, 0.0)
        dlnw_ref[...] += dlnw
        dlnb_ref[...] += dlnb

    return _rows_call(
        "gmlp_bwd", body, chunks_per_step * CHUNK, [p_uv, dya], [lnw, lnb, e_bf, et_bf, w_cat, w_stack, bmap],
        [_sds((n_tok, 2 * GM_WIDTH), BF16)],
        [_sds((N_HEADS * CHUNK, CHUNK), F32), _sds((CHUNK, DT_PAD), F32), _sds((1, GM_WIDTH), F32),
         _sds((1, GM_WIDTH), F32)],
        scratch=[pltpu.VMEM((CHUNK, N_HEADS * CHUNK), BF16), pltpu.VMEM((N_HEADS * CHUNK, CHUNK), BF16)],
        carried=carried)


def _ssd_bwd(p_xbc, p_z, p_dt, yssd, sprev, dyb, conv_w, conv_b, dt_bias, a_log, dskip_map, norm_w, e_bf, et_bf, n_seq,
             carried=None):
    n_tok = p_xbc.shape[0]
    nc = n_tok // n_seq // CHUNK

    def body(xr3, xprev3, z3, pdt3, yssd3, sprev3, dyb3,
             cw_ref, cb_ref, dtb_ref, alog_ref, dsk_ref, nw_ref, e_ref, et_ref,
             dpxbc3, dpz3, dpdt3, dcw_ref, dcb_ref, ddtb_ref, dalog_ref, ddsk_ref, dnw_ref,
             ds3_scr, nxt3_scr, dxa3_scr):
        @pl.when(pl.program_id(0) == 0)
        def _():
            for a in (dcw_ref, dcb_ref, ddtb_ref, dalog_ref, ddsk_ref, dnw_ref, ds3_scr, nxt3_scr):
                a[...] = jnp.zeros_like(a)

        for b in range(n_seq):
            one_sequence(xr3.at[b], xprev3.at[b], z3.at[b], pdt3.at[b], yssd3.at[b], sprev3.at[b], dyb3.at[b],
                         cw_ref, cb_ref, dtb_ref, alog_ref, dsk_ref, nw_ref, e_ref, et_ref,
                         dpxbc3.at[b], dpz3.at[b], dpdt3.at[b], dcw_ref, dcb_ref, ddtb_ref, dalog_ref, ddsk_ref, dnw_ref,
                         ds3_scr.at[b], nxt3_scr.at[b], dxa3_scr.at[b])

    def one_sequence(xr_ref, xprev_ref, z_ref, pdt_ref, yssd_ref, sprev_ref, dyb_ref,
                     cw_ref, cb_ref, dtb_ref, alog_ref, dsk_ref, nw_ref, e_ref, et_ref,
                     dpxbc_ref, dpz_ref, dpdt_ref, dcw_ref, dcb_ref, ddtb_ref, dalog_ref, ddsk_ref, dnw_ref,
                     ds_scr, nxt_scr, dxa_scr):
        chunk = nc - 1 - pl.program_id(0)
        xr = xr_ref[...]
        prev = jnp.where(chunk == 0, 0.0, xprev_ref[...])
        et_v = et_ref[...]
        p = _ssd_pre(xr, prev, cw_ref, cb_ref[...], pdt_ref[...], dtb_ref[...], alog_ref[...], e_ref[...])
        last, e_exp, dte, cd = _ssd_maps(p)
        rowi = p["rowi"]
        xs = p["xa"][:, :SSM_WIDTH]
        xd = xs * p["dt_map"]
        a_cs_t = p["a_cs"].T
        tri = _tril_mask()
        dsk = dsk_ref[...]
        nw_v = nw_ref[...]

        yv = yssd_ref[...]
        zv = z_ref[...]
        sz, zg, yg, _, rs = _gate_fwd(yv, zv, nw_v)
        dout = dyb_ref[...]
        for g in range(SSM_GROUPS):
            gs = slice(g * GROUP_W, (g + 1) * GROUP_W)
            dyg_g, dnw_g = _rms_bwd(yg[:, gs], rs[g], nw_v[:, gs], dout[:, gs])
            dnw_ref[:, gs] += dnw_g
            dxa_scr[:, gs] = dyg_g
        dyg = dxa_scr[:, :SSM_WIDTH]
        d_y = dyg * zg
        dpz_ref[...] = (dyg * yv * (sz + zv * sz * (1.0 - sz))).astype(BF16)

        s_prev = sprev_ref[...]
        ds_next = ds_scr[...]
        lane_dt = lax.broadcasted_iota(jnp.int32, (1, DT_PAD), 1)
        da_cols = jnp.zeros((CHUNK, DT_PAD), F32)
        for g in range(SSM_GROUPS):
            gs = slice(g * GROUP_W, (g + 1) * GROUP_W)
            b_off = SSM_WIDTH + g * SSM_STATE
            c_off = SSM_WIDTH + (SSM_GROUPS + g) * SSM_STATE
            bm = p["xa"][:, b_off:b_off + SSM_STATE].astype(BF16)
            cm = p["xa"][:, c_off:c_off + SSM_STATE].astype(BF16)
            cb_mat = _dot_nt(cm, bm)
            d_yg = d_y[:, gs]
            d_ygb = d_yg.astype(BF16)
            xdg = xd[:, gs]
            xdgb = xdg.astype(BF16)
            ds_g = ds_next[:, gs]
            sp_g = s_prev[:, gs]
            bds = _dot(bm, ds_g)
            dcs = d_yg * e_exp[:, gs]
            d_c = _dot_nt(dcs, sp_g)
            ds_scr[:, gs] = cd[:, gs] * ds_g + _dot_tn(cm, dcs)
            d_b = _dot_nt(xdg * dte[:, gs], ds_g)
            dxd_g = bds * dte[:, gs]
            sum_dcb = jnp.zeros((CHUNK, CHUNK), F32)
            for r in range(SSM_GROUPS * 2):
                head = g * 4 + r
                mask = _head_lane_mask(GROUP_W, r)
                dm = _head_decay(p["a_cs"], a_cs_t, head, tri)
                m_mat = cb_mat * dm
                g_mat = _dot_nt(jnp.where(mask, d_yg, 0.0), xdgb)
                w_mat = g_mat * m_mat
                sum_dcb = sum_dcb + g_mat * dm
                dxd_g = dxd_g + jnp.where(mask, _dot_tn(m_mat, d_ygb), 0.0)
                da_h = jnp.sum(w_mat - w_mat.T, axis=1, keepdims=True)
                da_cols = da_cols + jnp.where(lane_dt == head, da_h, 0.0)
            d_c = d_c + _dot(sum_dcb, bm)
            d_b = d_b + _dot_tn(sum_dcb, cm)
            dxa_scr[:, b_off:b_off + SSM_STATE] = d_b
            dxa_scr[:, c_off:c_off + SSM_STATE] = d_c
            y_off_g = _dot(cm, sp_g) * e_exp[:, gs]
            t3 = bds * xdg * dte[:, gs]
            tail = jnp.sum(t3, axis=0, keepdims=True) + jnp.sum(ds_g * sp_g, axis=0, keepdims=True) * cd[:, gs]
            pre_g = d_yg * y_off_g - t3 + jnp.where(last, tail, 0.0)
            s_pre, ddt_g, s_dsk = _seg_dots([pre_g, dxd_g * xs[:, gs], d_yg * xs[:, gs]], et_v[gs, :])
            da_cols = da_cols + s_pre
            ddsk_ref[...] += jnp.sum(s_dsk, axis=0, keepdims=True)
            dxa_scr[:, gs] = dxd_g * p["dt_map"][:, gs] + dsk[:, gs] * d_yg
            if g == 0:
                ddt = ddt_g
            else:
                ddt = ddt + ddt_g
        r_i = lax.broadcasted_iota(jnp.int32, (CHUNK, CHUNK), 0)
        c_i = lax.broadcasted_iota(jnp.int32, (CHUNK, CHUNK), 1)
        ddta = _tri_dot(r_i <= c_i, da_cols, terms=2)
        ddt = ddt + ddta * p["a_neg"]
        dalog_ref[...] += jnp.sum(ddta * p["dt"], axis=0, keepdims=True) * p["a_neg"]
        draw = ddt * _sigmoid(p["pre"])
        ddtb_ref[...] += jnp.sum(draw, axis=0, keepdims=True)
        dpdt_ref[...] = draw.astype(BF16)

        xc = p["xc"]
        sg = p["sg"]
        dxc = dxa_scr[...] * (sg + xc * sg * (1.0 - sg))
        dcb_ref[...] += jnp.sum(dxc, axis=0, keepdims=True)
        for k in range(CONV_K):
            dcw_ref[k] += jnp.sum(dxc * p["shifted"][k], axis=0, keepdims=True)
        nxt = nxt_scr[...]
        dxr = cw_ref[3] * dxc
        for s in range(1, CONV_K):
            dxr = dxr + cw_ref[CONV_K - 1 - s] * _shift_up(dxc, nxt, s)
        dpxbc_ref[...] = dxr.astype(BF16)
        nxt_scr[...] = dxc[:SUBLANES, :]

    seq_len = n_tok // n_seq

    def rows(width):
        return pl.BlockSpec((n_seq, CHUNK, width), lambda s: (0, nc - 1 - s, 0))

    tiles = CHUNK // SUBLANES
    prev_rows = pl.BlockSpec((n_seq, SUBLANES, CONV_CH), lambda s: (0, jnp.maximum((nc - 1 - s) * tiles - 1, 0), 0))

    def whole(shape):
        nd = len(shape)
        return pl.BlockSpec(tuple(shape), lambda s: (0,) * nd)

    def by_seq(a):
        return a.reshape(n_seq, seq_len, a.shape[-1])

    acc_shapes = [(CONV_K, 1, CONV_CH), (1, CONV_CH), (1, DT_PAD), (1, DT_PAD), (1, DT_PAD), (1, SSM_WIDTH)]
    xbc3 = by_seq(p_xbc)
    outs = _call_carrying(
        body, carried, name="ssd_bwd", grid=(nc,),
        in_specs=[rows(CONV_CH), prev_rows, rows(SSM_WIDTH), rows(DT_PAD), rows(SSM_WIDTH), rows(SSM_WIDTH),
                  rows(SSM_WIDTH)] + _ssd_const_specs(),
        out_specs=[rows(CONV_CH), rows(SSM_WIDTH), rows(DT_PAD)] + [whole(s) for s in acc_shapes],
        out_shape=tuple([_sds((n_seq, seq_len, CONV_CH), BF16), _sds((n_seq, seq_len, SSM_WIDTH), BF16),
                         _sds((n_seq, seq_len, DT_PAD), BF16)] + [_sds(s, F32) for s in acc_shapes]),
        scratch_shapes=[pltpu.VMEM((n_seq, SSM_STATE, SSM_WIDTH), F32), pltpu.VMEM((n_seq, SUBLANES, CONV_CH), F32),
                        pltpu.VMEM((n_seq, CHUNK, CONV_CH), F32)],
        operands=[xbc3, xbc3, by_seq(p_z), by_seq(p_dt), by_seq(yssd), by_seq(sprev), by_seq(dyb), conv_w, conv_b, dt_bias,
                  a_log, dskip_map, norm_w, e_bf, et_bf])
    return tuple(o.reshape(n_tok, o.shape[-1]) for o in outs[:3]) + tuple(outs[3:])


def _inproj_bwd(dp_uv, dp_xbc, dp_z, dp_dt, x, dx1, w_uv, w_xbc, w_z, w_dt, nw, tm=256, carried=None):
    n_tok = x.shape[0]

    def body(duv_ref, dxbc_ref, dz_ref, ddt_ref, x_ref, dx1_ref, wuv_ref, wxbc_ref, wz_ref, wdt_ref, nw_ref,
             gx_ref, h_ref, dnw_ref):
        dh = _dot_nt(duv_ref[...], wuv_ref[...]) + _dot_nt(dxbc_ref[...], wxbc_ref[...])
        dh = dh + _dot_nt(dz_ref[...], wz_ref[...]) + _dot_nt(ddt_ref[...], wdt_ref[...])
        xv = x_ref[...]
        h, r = _rms_fwd(xv, nw_ref[...])
        dx, dnw = _rms_bwd(xv, r, nw_ref[...], dh)
        gx_ref[...] = dx1_ref[...] + dx
        h_ref[...] = h.astype(BF16)
        dnw_ref[...] += dnw

    return _rows_call("inproj_bwd", body, tm, [dp_uv, dp_xbc, dp_z, dp_dt, x, dx1], [w_uv, w_xbc, w_z, w_dt, nw],
                      [_sds((n_tok, D_MODEL), F32), _sds((n_tok, D_MODEL), BF16)], [_sds((1, D_MODEL), F32)],
                      carried=carried)


def _const_maps():
    lane = jnp.arange(SSM_WIDTH) // HEAD_DIM
    e_bf = (jnp.arange(DT_PAD)[:, None] == lane[None, :]).astype(BF16)
    return e_bf, e_bf.T


def _pad_lanes(v, width):
    return jnp.pad(v, ((0, 0), (0, width - v.shape[1])))


SHARD_COLS = IN_COLS // N_CHIPS
_UV_END = 2 * GM_WIDTH
_Z_END = _UV_END + SSM_WIDTH
_XBC_END = _Z_END + CONV_CH


def _cols_from_shards(w4, lo, hi):
    pieces = []
    for j in range(N_CHIPS):
        a, b = max(lo, j * SHARD_COLS), min(hi, (j + 1) * SHARD_COLS)
        if a < b:
            pieces.append(w4[j][:, a - j * SHARD_COLS:b - j * SHARD_COLS])
    return pieces[0] if len(pieces) == 1 else jnp.concatenate(pieces, axis=1)


def _shards_from_cols(blocks):
    shards = []
    for j in range(N_CHIPS):
        pieces = []
        for arr, lo, hi in blocks:
            a, b = max(lo, j * SHARD_COLS), min(hi, (j + 1) * SHARD_COLS)
            if a < b:
                pieces.append(arr[:, a - lo:b - lo])
        shards.append(pieces[0] if len(pieces) == 1 else jnp.concatenate(pieces, axis=1))
    return jnp.stack(shards)


def _forward_backward(x, tgt, w_in4, conv_w, small, out_shard, up_shard, down_shard, core, adam_args):
    n_seq, seq_len, _ = x.shape
    n_tok = n_seq * seq_len
    x2 = x.reshape(n_tok, D_MODEL)
    tgt2 = tgt.reshape(n_tok, D_MODEL)
    e_bf, et_bf = _const_maps()

    w_uv = _cols_from_shards(w_in4, 0, _UV_END)
    w_z = _cols_from_shards(w_in4, _UV_END, _Z_END)
    w_xbc = _cols_from_shards(w_in4, _Z_END, _XBC_END)
    w_dt = _pad_lanes(_cols_from_shards(w_in4, _XBC_END, IN_COLS), DT_PAD)

    nw_pre = small["norm_mix_pre"]
    lnw = small["gm_ln_w"].reshape(1, GM_WIDTH)
    lnb = small["gm_ln_b"].reshape(1, GM_WIDTH)
    w_stack = small["gm_w_s"].reshape(N_HEADS * CHUNK, CHUNK)
    w_cat = jnp.transpose(small["gm_w_s"], (1, 0, 2)).reshape(CHUNK, N_HEADS * CHUNK)
    bmap = jnp.repeat(small["gm_b_s"].T, HEAD_DIM, axis=1)
    cw3 = conv_w.reshape(CONV_K, 1, CONV_CH)
    conv_b = small["conv_b"]
    dt_bias = _pad_lanes(small["dt_bias"], DT_PAD)
    a_log = _pad_lanes(small["a_log"], DT_PAD)
    dskip_map = jnp.repeat(small["d_skip"], HEAD_DIM, axis=1)
    ssm_nw = small["ssm_norm_w"]

    half = down_shard.shape[0] // 2
    p_uv, p_xbc, p_z, p_dt, w_out4, w_down_a = _inproj_fwd(
        x2, nw_pre, w_uv, w_xbc, w_z, w_dt, carried=_allgather_exchange([out_shard, down_shard[:half]]))
    ssd_consts = (cw3, conv_b, dt_bias, a_log, dskip_map, ssm_nw, e_bf, et_bf)
    ya, yb, yssd, sprev, w_up4, w_down_b = _mixer_fwd(
        p_uv, p_xbc, p_z, p_dt, lnw, lnb, w_cat, bmap, *ssd_consts, n_seq,
        carried=_allgather_exchange([up_shard, down_shard[half:]]))
    w_out_b = w_out4.reshape(D_MODEL, D_MODEL)
    o, x1, h2 = _outproj_fwd(ya, yb, x2, w_out_b, small["norm_mix_post"], small["norm_ffn_pre"])
    f, dd, dy, loss_acc, d_nffn_post = _mlp_fwd(h2, x1, tgt2, w_up4, w_down_a, w_down_b, small["norm_ffn_post"])

    dup, dx1, d_nffn_pre = _mlp_bwd(dd, f, x1, dy, w_down_a, w_down_b, w_up4, small["norm_ffn_pre"])
    tk = min(DW_TOKENS_PER_STEP, n_tok)
    g_up = _matmul_tn("dw_up", h2, dup, D_MODEL, D_MODEL, tk, stacked=True)
    g_down = _matmul_tn("dw_down", f, dd, 1024, D_MODEL, tk).reshape(N_CHIPS, D_FF // N_CHIPS, D_MODEL)
    do, dya, dyb, d_nmix_post, got_up, got_down = _outproj_bwd(
        dx1, o, w_out_b, small["norm_mix_post"], carried=_pair_exchange([g_up, g_down]))
    h_up = _pair_sum(core, g_up, got_up, 256)
    h_down = _pair_sum(core, g_down, got_down, 256)
    g_out_a = _matmul_tn("dw_out_a", ya, do, GM_WIDTH, D_MODEL, tk)
    g_out_b = _matmul_tn("dw_out_b", yb, do, SSM_WIDTH, D_MODEL, tk)
    g_out = jnp.concatenate([g_out_a, g_out_b], axis=0).reshape(N_CHIPS, D_MODEL // N_CHIPS, D_MODEL)
    dp_uv, d_ws, d_bs_t, d_lnw, d_lnb, slab_up, got_out = _gmlp_bwd(
        p_uv, dya, lnw, lnb, e_bf, et_bf, w_cat, w_stack, bmap,
        carried=_both(_chip_exchange([h_up]), _pair_exchange([g_out])))
    h_out = _pair_sum(core, g_out, got_out, 128)
    early = {
        "gm_ln_w": d_lnw.reshape(N_HEADS, HEAD_DIM), "gm_ln_b": d_lnb.reshape(N_HEADS, HEAD_DIM),
        "gm_w_s": d_ws.reshape(N_HEADS, CHUNK, CHUNK), "gm_b_s": d_bs_t[:, :N_HEADS].T,
        "norm_mix_post": d_nmix_post, "norm_ffn_pre": d_nffn_pre, "norm_ffn_post": d_nffn_post,
    }
    packed_early = _pack(early, tuple(early), tail=loss_acc[0, 0].reshape(1))
    (dp_xbc, dp_z, dp_dt, d_cw, d_cb, d_dtb, d_alog, d_dsk, d_ssm_nw, slab_down, slab_out, all_early) = _ssd_bwd(
        p_xbc, p_z, p_dt, yssd, sprev, dyb, *ssd_consts, n_seq,
        carried=_both(_chip_exchange([h_down, h_out]), _device_gather_exchange(packed_early)))
    gx, h, d_nmix_pre = _inproj_bwd(dp_uv, dp_xbc, dp_z, dp_dt, x2, dx1, w_uv, w_xbc, w_z, w_dt, nw_pre)
    late = {
        "norm_mix_pre": d_nmix_pre, "conv_w": d_cw.reshape(CONV_K, CONV_CH), "conv_b": d_cb,
        "dt_bias": d_dtb[:, :N_HEADS], "a_log": d_alog[:, :N_HEADS], "d_skip": d_dsk[:, :N_HEADS],
        "ssm_norm_w": d_ssm_nw,
    }
    g_uv, all_late = _matmul_tn("dw_in_uv", h, dp_uv, D_MODEL, 2 * GM_WIDTH, tk,
                                carried=_device_gather_exchange(_pack(late, tuple(late))))
    sum_early = _ordered_sum("small_sum_early", all_early)
    small_sum = _unpack(sum_early, {n: v.shape for n, v in early.items()}, tuple(early))
    small_sum.update(_unpack(_ordered_sum("small_sum_late", all_late), {n: v.shape for n, v in late.items()}, tuple(late)))
    loss = sum_early.reshape(-1)[sum(v.size for v in early.values())]
    g_xbc = _matmul_tn("dw_in_xbc", h, dp_xbc, D_MODEL, CONV_CH, tk)
    g_z = _matmul_tn("dw_in_z", h, dp_z, D_MODEL, SSM_WIDTH, tk)
    g_dt = _matmul_tn("dw_in_dt", h, dp_dt, D_MODEL, DT_PAD, tk)

    g_in = _shards_from_cols([(g_uv, 0, _UV_END), (g_z, _UV_END, _Z_END), (g_xbc, _Z_END, _XBC_END),
                              (g_dt, _XBC_END, IN_COLS)])

    red_up, red_down, red_out = _chip_sum(slab_up, 256), _chip_sum(slab_down, 256), _chip_sum(slab_out, 128)
    oth_up, oth_down, oth_out, got_in = _run_exchange(
        "grad_pair_swap", _both(_pair_swap([red_up, red_down, red_out]), _pair_exchange([g_in])))
    h_in = _pair_sum(core, g_in, got_in, 256)
    (slab_in,) = _run_exchange("grad_chip_exchange", _chip_exchange([h_in]))
    res = _adamw_halves("adamw_mlp", [(adam_args["w_up"][0], red_up, oth_up) + adam_args["w_up"][1:],
                                      (adam_args["w_down"][0], red_down, oth_down) + adam_args["w_down"][1:]], 256)
    big_out = {"w_up": res[0:4], "w_down": res[4:8]}
    big_out["w_out"] = _adamw_halves("adamw_w_out", [(adam_args["w_out"][0], red_out, oth_out) + adam_args["w_out"][1:]], 128)
    red_in = _chip_sum(slab_in, 256)
    (oth_in,) = _run_exchange("grad_pair_swap_in", _pair_swap([red_in]))
    big_out["w_in"] = _adamw_halves("adamw_w_in", [(adam_args["w_in"][0], red_in, oth_in) + adam_args["w_in"][1:]], 256)

    return loss, gx.reshape(x.shape), big_out, small_sum


_HBM = pl.BlockSpec(memory_space=pltpu.HBM)


D2D_CHUNKS = 8
ICI_CHUNKS = 1
ROW_ALIGN = 16


def _row_chunks(rows, n_chunks):
    size = min(max(rows // n_chunks, ROW_ALIGN), rows)
    assert rows % size == 0
    return [(start, size) for start in range(0, rows, size)]


def _position():
    x, y, c = lax.axis_index("x"), lax.axis_index("y"), lax.axis_index("c")
    chips = [(1 - x, y), (x, 1 - y), (1 - x, 1 - y)]
    return x, y, c, chips


def _allgather_exchange(arrs):
    n = len(arrs)

    def copies(ins, outs, send_sems, recv_sems, local_sems):
        x, y, c, chips = _position()
        me = 2 * x + y
        sibling = (x, y, 1 - c)

        def copy(a, k, src, dst, to):
            return pltpu.make_async_remote_copy(src_ref=src, dst_ref=dst, send_sem=send_sems.at[a, k],
                                                recv_sem=recv_sems.at[a, k], device_id=to, device_id_type=MESH)

        def half_rows(a, pc):
            half = ins[a].shape[0] // 2
            return pl.ds(pc * half, half)

        local = [pltpu.make_async_copy(ins[a], outs[a].at[me], local_sems.at[a]) for a in range(n)]
        ici_out = [[copy(a, k, ins[a].at[half_rows(a, c)], outs[a].at[me, half_rows(a, c)], (px, py, c))
                    for k, (px, py) in enumerate(chips)] for a in range(n)]
        return c, chips, sibling, copy, half_rows, local, ici_out

    def start(ins, outs, send_sems, recv_sems, local_sems):
        c, chips, _, copy, _, local, _ = copies(ins, outs, send_sems, recv_sems, local_sems)
        x, y, _, _ = _position()
        me = 2 * x + y
        for cp in local:
            cp.start()
        for a in range(n):
            half = ins[a].shape[0] // 2
            for k, (px, py) in enumerate(chips):
                for first, size in _row_chunks(half, ICI_CHUNKS):
                    rows = pl.ds(c * half + first, size)
                    copy(a, k, ins[a].at[rows], outs[a].at[me, rows], (px, py, c)).start()

    def finish(ins, outs, send_sems, recv_sems, local_sems):
        c, chips, sibling, copy, half_rows, local, ici_out = copies(ins, outs, send_sems, recv_sems, local_sems)
        for a in range(n):
            half = ins[a].shape[0] // 2
            for k, (px, py) in enumerate(chips):
                blk = outs[a].at[2 * px + py, half_rows(a, c)]
                copy(a, k, blk, blk, (px, py, c)).wait_recv()
                for first, size in _row_chunks(half, D2D_CHUNKS):
                    piece = outs[a].at[2 * px + py, pl.ds(c * half + first, size)]
                    copy(a, 3 + k, piece, piece, sibling).start()
        for a in range(n):
            for k, (px, py) in enumerate(chips):
                theirs = outs[a].at[2 * px + py, half_rows(a, 1 - c)]
                copy(a, 3 + k, theirs, theirs, sibling).wait_recv()
                mine = outs[a].at[2 * px + py, half_rows(a, c)]
                copy(a, 3 + k, mine, mine, sibling).wait_send()
        for a in range(n):
            for cp in ici_out[a]:
                cp.wait_send()
        for cp in local:
            cp.wait()

    return _Carried(arrs, [_sds((N_CHIPS,) + a.shape, a.dtype) for a in arrs],
                    [pltpu.SemaphoreType.DMA((n, 6)), pltpu.SemaphoreType.DMA((n, 6)), pltpu.SemaphoreType.DMA((n,))],
                    start, finish)


def _run_exchange(name, exchange):
    n_in, n_out = len(exchange.ins), len(exchange.out_shapes)

    def body(*refs):
        ins, outs, sems = refs[:n_in], refs[n_in:n_in + n_out], refs[n_in + n_out:]
        exchange.start(ins, outs, *sems)
        exchange.finish(ins, outs, *sems)

    return pl.pallas_call(
        body, name=name, out_shape=tuple(exchange.out_shapes), in_specs=[_HBM] * n_in,
        out_specs=tuple([_HBM] * n_out), scratch_shapes=exchange.sems,
    )(*exchange.ins)


def _pair_exchange(grads):
    n = len(grads)

    def copier(send_sems, recv_sems):
        x, y, c, _ = _position()

        def copy(a, src, dst):
            return pltpu.make_async_remote_copy(src_ref=src, dst_ref=dst, send_sem=send_sems.at[a],
                                                recv_sem=recv_sems.at[a], device_id=(x, y, 1 - c), device_id_type=MESH)
        return c, copy

    def start(ins, got, send_sems, recv_sems):
        c, copy = copier(send_sems, recv_sems)
        for a in range(n):
            half = ins[a].shape[1] // 2
            for slab in range(N_CHIPS):
                for first, size in _row_chunks(half, D2D_CHUNKS):
                    copy(a, ins[a].at[slab, pl.ds((1 - c) * half + first, size), :],
                         got[a].at[slab, pl.ds(first, size), :]).start()

    def finish(ins, got, send_sems, recv_sems):
        c, copy = copier(send_sems, recv_sems)
        for a in range(n):
            half = ins[a].shape[1] // 2
            copy(a, ins[a].at[:, pl.ds((1 - c) * half, half), :], got[a]).wait()

    return _Carried(grads, [_sds((N_CHIPS, g.shape[1] // 2, g.shape[2]), g.dtype) for g in grads],
                    [pltpu.SemaphoreType.DMA((n,)), pltpu.SemaphoreType.DMA((n,))], start, finish)


def _chip_exchange(hsums):
    n = len(hsums)

    def copies(ins, outs, send_sems, recv_sems, local_sems, pieces):
        x, y, c, chips = _position()
        me = 2 * x + y
        cps = []
        for a in range(n):
            cps.append(pltpu.make_async_copy(ins[a].at[me], outs[a].at[me], local_sems.at[a]))
            rows = ins[a].shape[1]
            for k, (px, py) in enumerate(chips):
                for first, size in (_row_chunks(rows, ICI_CHUNKS) if pieces else [(0, rows)]):
                    cps.append(pltpu.make_async_remote_copy(
                        src_ref=ins[a].at[2 * px + py, pl.ds(first, size)], dst_ref=outs[a].at[me, pl.ds(first, size)],
                        send_sem=send_sems.at[a, k], recv_sem=recv_sems.at[a, k], device_id=(px, py, c),
                        device_id_type=MESH))
        return cps

    def start(*refs):
        for cp in copies(*refs, pieces=True):
            cp.start()

    def finish(*refs):
        for cp in copies(*refs, pieces=False):
            cp.wait()

    return _Carried(hsums, [_sds(h.shape, h.dtype) for h in hsums],
                    [pltpu.SemaphoreType.DMA((n, 3)), pltpu.SemaphoreType.DMA((n, 3)), pltpu.SemaphoreType.DMA((n,))],
                    start, finish)


def _pair_swap(reds):
    n = len(reds)

    def copier(send_sems, recv_sems):
        x, y, c, _ = _position()

        def copy(a, src, dst):
            return pltpu.make_async_remote_copy(src_ref=src, dst_ref=dst, send_sem=send_sems.at[a],
                                                recv_sem=recv_sems.at[a], device_id=(x, y, 1 - c), device_id_type=MESH)
        return copy

    def start(ins, outs, send_sems, recv_sems):
        copy = copier(send_sems, recv_sems)
        for a in range(n):
            for first, size in _row_chunks(ins[a].shape[0], 2 * D2D_CHUNKS):
                copy(a, ins[a].at[pl.ds(first, size), :], outs[a].at[pl.ds(first, size), :]).start()

    def finish(ins, outs, send_sems, recv_sems):
        copy = copier(send_sems, recv_sems)
        for a in range(n):
            copy(a, ins[a], outs[a]).wait()

    return _Carried(reds, [_sds(r.shape, r.dtype) for r in reds],
                    [pltpu.SemaphoreType.DMA((n,)), pltpu.SemaphoreType.DMA((n,))], start, finish)


def _device_gather_exchange(packed):
    def copies(ins, outs, send_sems, recv_sems, local_sem):
        (x_ref,), (all_ref,) = ins, outs
        x, y, c, chips = _position()
        me, sibling = (x, y, c), (x, y, 1 - c)

        def slab(px, py, pc):
            return all_ref.at[4 * px + 2 * py + pc]

        def copy(k, block, to, src=None):
            return pltpu.make_async_remote_copy(
                src_ref=slab(*block) if src is None else src, dst_ref=slab(*block), send_sem=send_sems.at[k],
                recv_sem=recv_sems.at[k], device_id=to, device_id_type=MESH)

        mine = pltpu.make_async_copy(x_ref, slab(*me), local_sem)
        first = [copy(0, me, sibling, src=x_ref)]
        first += [copy(1 + j, me, (*chip, c), src=x_ref) for j, chip in enumerate(chips)]
        passed = [copy(4 + j, (*chip, c), sibling) for j, chip in enumerate(chips)]
        return c, chips, me, sibling, copy, mine, first, passed

    def start(ins, outs, send_sems, recv_sems, local_sem):
        _, _, _, _, _, mine, first, _ = copies(ins, outs, send_sems, recv_sems, local_sem)
        mine.start()
        for cp in first:
            cp.start()

    def finish(ins, outs, send_sems, recv_sems, local_sem):
        c, chips, me, sibling, copy, mine, first, passed = copies(ins, outs, send_sems, recv_sems, local_sem)
        for j, chip in enumerate(chips):
            copy(1 + j, (*chip, c), me).wait_recv()
            passed[j].start()
        copy(0, sibling, me).wait_recv()
        for j, chip in enumerate(chips):
            copy(4 + j, (*chip, 1 - c), me).wait_recv()
        for cp in first + passed:
            cp.wait_send()
        mine.wait()

    return _Carried([packed], [_sds((N_DEV,) + packed.shape, F32)],
                    [pltpu.SemaphoreType.DMA((7,)), pltpu.SemaphoreType.DMA((7,)), pltpu.SemaphoreType.DMA],
                    start, finish)


def _ordered_sum(name, slabs):
    _, m_per, n_cols = slabs.shape

    def body(s_ref, o_ref):
        acc = s_ref[0]
        for d in range(1, N_DEV):
            acc = acc + s_ref[d]
        o_ref[...] = acc

    vmem = pl.BlockSpec(memory_space=pltpu.VMEM)
    return pl.pallas_call(body, name=name, out_shape=_sds((m_per, n_cols), F32), in_specs=[vmem], out_specs=vmem)(slabs)


def _pair_sum(core, own, got, tm):
    _, half, cols = got.shape
    nb = half // tm

    def body(c_ref, a_ref, b_ref, o_ref):
        o_ref[...] = (a_ref[...].astype(F32) + b_ref[...].astype(F32)).astype(BF16)

    return pl.pallas_call(
        body, name="grad_pair_sum", out_shape=_sds(got.shape, BF16),
        grid_spec=pltpu.PrefetchScalarGridSpec(
            num_scalar_prefetch=1, grid=(N_CHIPS, nb),
            in_specs=[pl.BlockSpec((None, tm, cols), lambda s, i, c_ref: (s, c_ref[0] * nb + i, 0)),
                      pl.BlockSpec((None, tm, cols), lambda s, i, c_ref: (s, i, 0))],
            out_specs=pl.BlockSpec((None, tm, cols), lambda s, i, c_ref: (s, i, 0))),
        compiler_params=_cparams(2),
    )(core, own, got)


def _chip_sum(slabs, tm):
    _, half, cols = slabs.shape

    def body(s_ref, o_ref):
        acc = s_ref[0].astype(F32)
        for k in range(1, N_CHIPS):
            acc = acc + s_ref[k].astype(F32)
        o_ref[...] = acc

    return pl.pallas_call(
        body, name="grad_chip_sum", out_shape=_sds((half, cols), F32), grid=(half // tm,),
        in_specs=[pl.BlockSpec((N_CHIPS, tm, cols), lambda i: (0, i, 0))],
        out_specs=pl.BlockSpec((tm, cols), lambda i: (i, 0)), compiler_params=_cparams(1),
    )(slabs)


def _adam_math(w, g, m, v):
    m2 = ADAM_B1 * m + (1.0 - ADAM_B1) * g
    v2 = ADAM_B2 * v + (1.0 - ADAM_B2) * (g * g)
    m_hat = m2 / (1.0 - ADAM_B1 ** ADAM_STEP)
    v_hat = v2 / (1.0 - ADAM_B2 ** ADAM_STEP)
    delta = -ADAM_LR * (m_hat / (jnp.sqrt(v_hat) + ADAM_EPS) + ADAM_WD * w)
    return delta, m2, v2


def _adamw_halves(name, items, tm, carried=None):
    rows, cols = items[0][0].shape
    nb = rows // 2 // tm
    n = len(items)

    def body(*refs):
        mine = (pl.program_id(0) // nb) == lax.axis_index("c")
        for k in range(n):
            w_ref, own_ref, oth_ref, m_ref, v_ref = refs[5 * k:5 * k + 5]
            g_ref, d_ref, m2_ref, v2_ref = refs[5 * n + 4 * k:5 * n + 4 * k + 4]
            g = jnp.where(mine, own_ref[...], oth_ref[...])
            d, m2, v2 = _adam_math(w_ref[...], g, m_ref[...], v_ref[...])
            g_ref[...] = g
            d_ref[...] = d
            m2_ref[...] = m2
            v2_ref[...] = v2

    full = pl.BlockSpec((tm, cols), lambda i: (i, 0))
    half = pl.BlockSpec((tm, cols), lambda i: (i % nb, 0))
    return _call_carrying(
        body, carried, name=name, grid=(rows // tm,), in_specs=[full, half, half, full, full] * n,
        out_specs=[full] * (4 * n), out_shape=tuple([_sds((rows, cols), F32)] * (4 * n)), scratch_shapes=[],
        operands=[a for item in items for a in item])


def _adamw(name, w, g, m, v, tm):
    def body(w_ref, g_ref, m_ref, v_ref, gout_ref, d_ref, m2_ref, v2_ref):
        gv = g_ref[...]
        d, m2, v2 = _adam_math(w_ref[...], gv, m_ref[...], v_ref[...])
        gout_ref[...] = gv
        d_ref[...] = d
        m2_ref[...] = m2
        v2_ref[...] = v2

    return _rows_call(name, body, tm, [w, g, m, v], [], [_sds(w.shape, F32)] * 4)


_SMALL_NAMES = ("norm_mix_pre", "gm_ln_w", "gm_ln_b", "gm_w_s", "gm_b_s", "conv_w", "conv_b", "dt_bias", "a_log",
                "d_skip", "ssm_norm_w", "norm_mix_post", "norm_ffn_pre", "norm_ffn_post")
_PACK_COLS = 1024


def _pack(parts, names=_SMALL_NAMES, tail=None):
    pieces = [parts[n].reshape(-1) for n in names]
    flat = jnp.concatenate(pieces if tail is None else pieces + [tail])
    rows = -(-flat.shape[0] // (8 * _PACK_COLS)) * 8
    flat = jnp.pad(flat, (0, rows * _PACK_COLS - flat.shape[0]))
    return flat.reshape(rows, _PACK_COLS)


def _unpack(packed, shapes, names=_SMALL_NAMES):
    flat = packed.reshape(-1)
    out, off = {}, 0
    for n in names:
        size = 1
        for s in shapes[n]:
            size *= s
        out[n] = flat[off:off + size].reshape(shapes[n])
        off += size
    return out


def kernel(x, norm_mix_pre, w_in, gm_ln_w, gm_ln_b, gm_w_s, gm_b_s, conv_w, conv_b, dt_bias, a_log, d_skip, ssm_norm_w, w_out, norm_mix_post, norm_ffn_pre, w_up, w_down, norm_ffn_post, loss_target, m_norm_mix_pre, m_w_in, m_gm_ln_w, m_gm_ln_b, m_gm_w_s, m_gm_b_s, m_conv_w, m_conv_b, m_dt_bias, m_a_log, m_d_skip, m_ssm_norm_w, m_w_out, m_norm_mix_post, m_norm_ffn_pre, m_w_up, m_w_down, m_norm_ffn_post, v_norm_mix_pre, v_w_in, v_gm_ln_w, v_gm_ln_b, v_gm_w_s, v_gm_b_s, v_conv_w, v_conv_b, v_dt_bias, v_a_log, v_d_skip, v_ssm_norm_w, v_w_out, v_norm_mix_post, v_norm_ffn_pre, v_w_up, v_w_down, v_norm_ffn_post):
    params = dict(norm_mix_pre=norm_mix_pre, w_in=w_in, gm_ln_w=gm_ln_w, gm_ln_b=gm_ln_b, gm_w_s=gm_w_s, gm_b_s=gm_b_s,
                  conv_w=conv_w, conv_b=conv_b, dt_bias=dt_bias, a_log=a_log, d_skip=d_skip, ssm_norm_w=ssm_norm_w,
                  w_out=w_out, norm_mix_post=norm_mix_post, norm_ffn_pre=norm_ffn_pre, w_up=w_up, w_down=w_down,
                  norm_ffn_post=norm_ffn_post)
    mom1 = dict(norm_mix_pre=m_norm_mix_pre, w_in=m_w_in, gm_ln_w=m_gm_ln_w, gm_ln_b=m_gm_ln_b, gm_w_s=m_gm_w_s,
                gm_b_s=m_gm_b_s, conv_w=m_conv_w, conv_b=m_conv_b, dt_bias=m_dt_bias, a_log=m_a_log, d_skip=m_d_skip,
                ssm_norm_w=m_ssm_norm_w, w_out=m_w_out, norm_mix_post=m_norm_mix_post, norm_ffn_pre=m_norm_ffn_pre,
                w_up=m_w_up, w_down=m_w_down, norm_ffn_post=m_norm_ffn_post)
    mom2 = dict(norm_mix_pre=v_norm_mix_pre, w_in=v_w_in, gm_ln_w=v_gm_ln_w, gm_ln_b=v_gm_ln_b, gm_w_s=v_gm_w_s,
                gm_b_s=v_gm_b_s, conv_w=v_conv_w, conv_b=v_conv_b, dt_bias=v_dt_bias, a_log=v_a_log, d_skip=v_d_skip,
                ssm_norm_w=v_ssm_norm_w, w_out=v_w_out, norm_mix_post=v_norm_mix_post, norm_ffn_pre=v_norm_ffn_pre,
                w_up=v_w_up, w_down=v_w_down, norm_ffn_post=v_norm_ffn_post)
    names = list(params)
    big = ("w_in", "w_out", "w_up", "w_down")
    chip = 2 * lax.axis_index("x") + lax.axis_index("y")

    shards = {n: params[n][0].astype(BF16) for n in big}
    conv_shard = jnp.pad(conv_w[0], ((0, 16 - CONV_K), (0, 0)))
    g_in4, g_conv4 = _run_exchange("allgather_w_in", _allgather_exchange([shards["w_in"], conv_shard]))
    conv_full = jnp.transpose(g_conv4[:, :CONV_K, :], (1, 0, 2)).reshape(CONV_K, CONV_CH)

    small = {n: params[n][0] if params[n].ndim >= 3 else params[n] for n in _SMALL_NAMES if n != "conv_w"}
    core = lax.axis_index("c").astype(jnp.int32).reshape(1)
    adam_args = {n: (params[n][0], mom1[n][0], mom2[n][0]) for n in big}
    loss, grad_x, big_out, small_sum = _forward_backward(
        x, loss_target, g_in4, conv_full, small, shards["w_out"], shards["w_up"], shards["w_down"], core, adam_args)
    grads, delta, new_m, new_v = {}, {}, {}, {}
    for n in big:
        grads[n], delta[n], new_m[n], new_v[n] = [a[None] for a in big_out[n]]

    small_sum["conv_w"] = lax.dynamic_slice_in_dim(small_sum["conv_w"], chip * (CONV_CH // N_CHIPS), CONV_CH // N_CHIPS, axis=1)

    local_shapes = {n: params[n].shape[1:] if params[n].ndim >= 3 else params[n].shape for n in _SMALL_NAMES}
    flat = lambda tree: {n: tree[n].reshape(local_shapes[n]) for n in _SMALL_NAMES}
    packed = [_pack(flat(t)) for t in (params, small_sum, mom1, mom2)]
    _, d_p, m_p, v_p = _adamw("adamw_small", *packed, packed[0].shape[0])
    for src, dst in ((d_p, delta), (m_p, new_m), (v_p, new_v)):
        for n, val in _unpack(src, local_shapes).items():
            dst[n] = val.reshape(params[n].shape)
    for n in _SMALL_NAMES:
        grads[n] = small_sum[n].reshape(params[n].shape)

    out = [loss, grad_x]
    for tree in (grads, delta, new_m, new_v):
        out += [tree[n] for n in names]
    return tuple(out)
```

```python
import functools

import jax
import jax.numpy as jnp
from jax import lax
from jax.experimental import pallas as pl
from jax.experimental.pallas import tpu as pltpu

F32 = jnp.float32
BF16 = jnp.bfloat16
HI = lax.Precision.HIGHEST
MESH = pl.DeviceIdType.MESH

EPS = 1e-6
D_MODEL = 1024
GM_WIDTH = 512
SSM_WIDTH = 512
N_HEADS = 8
HEAD_DIM = 64
CHUNK = 128
SSM_GROUPS = 2
GROUP_W = SSM_WIDTH // SSM_GROUPS
SSM_STATE = 128
CONV_K = 4
CONV_CH = 1024
D_FF = 4096
IN_COLS = 2568
DT_PAD = 128
N_CHIPS = 4
N_DEV = 8

ADAM_LR = 0.001
ADAM_B1 = 0.9
ADAM_B2 = 0.999
ADAM_EPS = 1e-08
ADAM_WD = 0.01
ADAM_STEP = 10

VMEM_LIMIT_BYTES = 56 * 1024 * 1024
FF_TILE = 512
DW_TOKENS_PER_STEP = 2048


def _cparams(n_axes):
    return pltpu.CompilerParams(dimension_semantics=("arbitrary",) * n_axes, vmem_limit_bytes=VMEM_LIMIT_BYTES)


def _dot(a, b):
    return jnp.dot(a.astype(BF16), b.astype(BF16), preferred_element_type=F32)


def _dot_nt(a, b):
    return lax.dot_general(a.astype(BF16), b.astype(BF16), (((1,), (1,)), ((), ())), preferred_element_type=F32)


def _dot_tn(a, b):
    return lax.dot_general(a.astype(BF16), b.astype(BF16), (((0,), (0,)), ((), ())), preferred_element_type=F32)


def _sigmoid(x):
    return 1.0 / (1.0 + jnp.exp(-x))


_GELU_C = 0.7978845608028654
_GELU_A = 0.044715


def _gelu(x):
    t = jnp.tanh(_GELU_C * (x + _GELU_A * (x * x * x)))
    return 0.5 * x * (1.0 + t), t


def _gelu_grad(x, t):
    return 0.5 * (1.0 + t) + 0.5 * x * (1.0 - t * t) * (_GELU_C * (1.0 + 3.0 * _GELU_A * x * x))


def _rms_fwd(x, w):
    r = lax.rsqrt(jnp.mean(x * x, axis=-1, keepdims=True) + EPS)
    return x * r * w, r


def _rms_bwd(x, r, w, dy):
    g = dy * w
    dx = r * g - x * (r * r * r) * jnp.mean(g * x, axis=-1, keepdims=True)
    dw = jnp.sum(dy * x * r, axis=0, keepdims=True)
    return dx, dw


class _Carried:
    def __init__(self, ins, out_shapes, sems, start, finish):
        self.ins, self.out_shapes, self.sems = list(ins), list(out_shapes), list(sems)
        self.start, self.finish = start, finish


def _both(first, second):
    n_i, n_o, n_s = len(first.ins), len(first.out_shapes), len(first.sems)

    def split(ins, outs, sems):
        return (ins[:n_i], outs[:n_o], sems[:n_s]), (ins[n_i:], outs[n_o:], sems[n_s:])

    def start(ins, outs, *sems):
        (i1, o1, s1), (i2, o2, s2) = split(ins, outs, sems)
        first.start(i1, o1, *s1)
        second.start(i2, o2, *s2)

    def finish(ins, outs, *sems):
        (i1, o1, s1), (i2, o2, s2) = split(ins, outs, sems)
        first.finish(i1, o1, *s1)
        second.finish(i2, o2, *s2)

    return _Carried(first.ins + second.ins, first.out_shapes + second.out_shapes, first.sems + second.sems, start, finish)


def _split_carried(refs, n_in, n_out, n_scratch, carried):
    n_ci, n_co, n_cs = len(carried.ins), len(carried.out_shapes), len(carried.sems)
    ins, rest = refs[:n_in], refs[n_in:]
    c_ins, rest = rest[:n_ci], rest[n_ci:]
    outs, rest = rest[:n_out], rest[n_out:]
    c_outs, rest = rest[:n_co], rest[n_co:]
    scr, c_sems = rest[:n_scratch], rest[n_scratch:]
    assert len(c_sems) == n_cs
    return tuple(ins) + tuple(outs) + tuple(scr), c_ins, c_outs, c_sems


def _rows_call(name, body, tm, row_ins, const_ins, row_outs, acc_outs=(), scratch=(), carried=None):
    n_rows = row_ins[0].shape[0]
    assert n_rows % tm == 0
    n_steps = n_rows // tm
    n_in = len(row_ins) + len(const_ins)
    n_ro = len(row_outs)
    n_acc = len(acc_outs)

    def kern(*refs):
        accs = refs[n_in + n_ro:n_in + n_ro + n_acc]

        @pl.when(pl.program_id(0) == 0)
        def _():
            for a in accs:
                a[...] = jnp.zeros_like(a)

        body(*refs)

    def whole(shape):
        nd = len(shape)
        return pl.BlockSpec(tuple(shape), lambda i: (0,) * nd)

    in_specs = [pl.BlockSpec((tm, a.shape[1]), lambda i: (i, 0)) for a in row_ins]
    in_specs += [whole(a.shape) for a in const_ins]
    out_specs = [pl.BlockSpec((tm, s.shape[1]), lambda i: (i, 0)) for s in row_outs]
    out_specs += [whole(s.shape) for s in acc_outs]
    return _call_carrying(
        kern, carried, name=name, grid=(n_steps,), in_specs=in_specs, out_specs=out_specs,
        out_shape=tuple(row_outs) + tuple(acc_outs), scratch_shapes=list(scratch), operands=list(row_ins) + list(const_ins))


def _call_carrying(body, carried, *, name, grid, in_specs, out_specs, out_shape, scratch_shapes, operands):
    n_in, n_out, n_scratch = len(in_specs), len(out_specs), len(scratch_shapes)
    kern = body
    if carried is not None:
        def kern(*refs):
            plain, c_ins, c_outs, c_sems = _split_carried(refs, n_in, n_out, n_scratch, carried)
            first, last = True, True
            for d, size in enumerate(grid):
                first = jnp.logical_and(first, pl.program_id(d) == 0)
                last = jnp.logical_and(last, pl.program_id(d) == size - 1)

            @pl.when(first)
            def _():
                carried.start(c_ins, c_outs, *c_sems)

            body(*plain)

            @pl.when(last)
            def _():
                carried.finish(c_ins, c_outs, *c_sems)

        in_specs = list(in_specs) + [_HBM] * len(carried.ins)
        out_specs = list(out_specs) + [_HBM] * len(carried.out_shapes)
        out_shape = tuple(out_shape) + tuple(carried.out_shapes)
        operands = list(operands) + carried.ins
        scratch_shapes = list(scratch_shapes) + carried.sems
    return pl.pallas_call(
        kern, name=name, grid=grid, in_specs=in_specs, out_specs=out_specs, out_shape=out_shape,
        scratch_shapes=scratch_shapes, compiler_params=_cparams(len(grid)),
    )(*operands)


def _sds(shape, dtype):
    return jax.ShapeDtypeStruct(tuple(shape), dtype)


def _matmul_tn(name, a, b, tm, tn, tk, stacked=False, carried=None):
    k_dim, m_dim = a.shape
    n_dim = b.shape[1]
    assert m_dim % tm == 0 and n_dim % tn == 0 and k_dim % tk == 0
    nk = k_dim // tk

    def kern(a_ref, b_ref, o_ref, acc_ref):
        k = pl.program_id(2)
        prod = _dot_tn(a_ref[...], b_ref[...])

        @pl.when(k == 0)
        def _():
            acc_ref[...] = prod

        @pl.when(k > 0)
        def _():
            acc_ref[...] += prod

        @pl.when(k == nk - 1)
        def _():
            o_ref[...] = acc_ref[...].astype(o_ref.dtype)

    if stacked:
        assert tm == m_dim
        out_shape = _sds((n_dim // tn, m_dim, tn), BF16)
        out_spec = pl.BlockSpec((None, tm, tn), lambda i, j, k: (j, i, 0))
    else:
        out_shape = _sds((m_dim, n_dim), BF16)
        out_spec = pl.BlockSpec((tm, tn), lambda i, j, k: (i, j))
    outs = _call_carrying(
        kern, carried, name=name, grid=(m_dim // tm, n_dim // tn, nk),
        in_specs=[pl.BlockSpec((tk, tm), lambda i, j, k: (k, i)), pl.BlockSpec((tk, tn), lambda i, j, k: (k, j))],
        out_specs=[out_spec], out_shape=(out_shape,), scratch_shapes=[pltpu.VMEM((tm, tn), F32)], operands=[a, b])
    return outs[0] if carried is None else outs


def _inproj_fwd(x, nw, w_uv, w_xbc, w_z, w_dt, tm=256, carried=None):
    n_tok = x.shape[0]

    def body(x_ref, nw_ref, wuv_ref, wxbc_ref, wz_ref, wdt_ref, puv_ref, pxbc_ref, pz_ref, pdt_ref):
        h, _ = _rms_fwd(x_ref[...], nw_ref[...])
        h = h.astype(BF16)
        puv_ref[...] = jnp.dot(h, wuv_ref[...], preferred_element_type=F32)
        pxbc_ref[...] = jnp.dot(h, wxbc_ref[...], preferred_element_type=F32)
        pz_ref[...] = jnp.dot(h, wz_ref[...], preferred_element_type=F32)
        pdt_ref[...] = jnp.dot(h, wdt_ref[...], preferred_element_type=F32)

    return _rows_call(
        "inproj_fwd", body, tm, [x], [nw, w_uv, w_xbc, w_z, w_dt],
        [_sds((n_tok, 2 * GM_WIDTH), F32), _sds((n_tok, CONV_CH), F32), _sds((n_tok, SSM_WIDTH), F32),
         _sds((n_tok, DT_PAD), F32)], carried=carried)


def _head_lane_mask(width, head):
    lane = lax.broadcasted_iota(jnp.int32, (1, width), 1)
    return (lane // HEAD_DIM) == head


def _split_terms(x, terms):
    parts = []
    for _ in range(terms):
        p = x.astype(BF16)
        parts.append(p)
        x = x - p.astype(F32)
    return parts


def _seg_dots(vals, ind, terms=2):
    m = vals[0].shape[0]
    parts = []
    for v in vals:
        parts += _split_terms(v, terms)
    red = jnp.dot(jnp.concatenate(parts, axis=0), ind, preferred_element_type=F32)
    outs = []
    for i in range(len(vals)):
        acc = red[i * terms * m:(i * terms + 1) * m]
        for t in range(1, terms):
            acc = acc + red[(i * terms + t) * m:(i * terms + t + 1) * m]
        outs.append(acc)
    return outs


def _tri_dot(mask, x, terms=3):
    n = x.shape[1]
    red = jnp.dot(mask.astype(BF16), jnp.concatenate(_split_terms(x, terms), axis=1), preferred_element_type=F32)
    acc = red[:, :n]
    for t in range(1, terms):
        acc = acc + red[:, t * n:(t + 1) * n]
    return acc


def _gmlp_common(puv, lnw, lnb, e_bf, et_bf):
    u = puv[:, :GM_WIDTH]
    v = puv[:, GM_WIDTH:]
    gu, tu = _gelu(u)
    gv, tv = _gelu(v)
    (s1,) = _seg_dots([gv], et_bf)
    (mu,) = _seg_dots([s1 * (1.0 / HEAD_DIM)], e_bf)
    xc = gv - mu
    (s2,) = _seg_dots([xc * xc], et_bf)
    (rstd,) = _seg_dots([lax.rsqrt(s2 * (1.0 / HEAD_DIM) + EPS)], e_bf)
    xhat = xc * rstd
    vn = xhat * lnw + lnb
    return u, v, gu, tu, tv, rstd, xhat, vn


def _tril_mask():
    r = lax.broadcasted_iota(jnp.int32, (CHUNK, CHUNK), 0)
    c = lax.broadcasted_iota(jnp.int32, (CHUNK, CHUNK), 1)
    return r >= c


def _head_blocks(v):
    return jnp.concatenate([jnp.where(_head_lane_mask(GM_WIDTH, h), v, jnp.zeros_like(v)) for h in range(N_HEADS)], axis=0)


def _causal_w_cat(w_cat):
    t = lax.broadcasted_iota(jnp.int32, (CHUNK, N_HEADS * CHUNK), 0)
    s = lax.broadcasted_iota(jnp.int32, (CHUNK, N_HEADS * CHUNK), 1) % CHUNK
    return jnp.where(t >= s, w_cat, 0.0).astype(BF16)


def _gmlp_chunk_fwd(puv, lnw, lnb, e_bf, et_bf, wm, bmap):
    _, _, gu, _, _, _, _, vn = _gmlp_common(puv, lnw, lnb, e_bf, et_bf)
    mixed = jnp.dot(wm, _head_blocks(vn.astype(BF16)), preferred_element_type=F32) + bmap
    return (gu * mixed).astype(BF16)


SUBLANES = 8


def _shift_down(x, tail, s):
    main = pltpu.roll(x, s, 0)
    row = lax.broadcasted_iota(jnp.int32, (SUBLANES, 1), 0)
    head = jnp.where(row < s, pltpu.roll(tail, s, 0), main[:SUBLANES])
    return jnp.concatenate([head, main[SUBLANES:]], axis=0)


def _shift_up(x, head_next, s):
    n = x.shape[0]
    main = pltpu.roll(x, n - s, 0)
    row = lax.broadcasted_iota(jnp.int32, (SUBLANES, 1), 0)
    last = jnp.where(row >= SUBLANES - s, pltpu.roll(head_next, SUBLANES - s, 0), main[n - SUBLANES:])
    return jnp.concatenate([main[:n - SUBLANES], last], axis=0)


def _ssd_pre(xr, tail, cw_ref, cb, pdt, dtb, alog, emap):
    rowi = lax.broadcasted_iota(jnp.int32, (CHUNK, 1), 0)
    shifted = [_shift_down(xr, tail, 3), _shift_down(xr, tail, 2), _shift_down(xr, tail, 1), xr]
    xc = cb
    for k in range(CONV_K):
        xc = xc + cw_ref[k] * shifted[k]
    sg = _sigmoid(xc)
    xa = xc * sg
    pre = pdt + dtb
    dt = jnp.maximum(pre, 0.0) + jnp.log(1.0 + jnp.exp(-jnp.abs(pre)))
    a_neg = -jnp.exp(alog)
    a_cs = _tri_dot(_tril_mask(), dt * a_neg)
    acs_map, dt_map = _seg_dots([a_cs, dt], emap, terms=3)
    return dict(shifted=shifted, xc=xc, sg=sg, xa=xa, pre=pre, dt=dt, a_neg=a_neg, a_cs=a_cs,
                acs_map=acs_map, dt_map=dt_map, rowi=rowi)


def _ssd_maps(p):
    last = p["rowi"] == CHUNK - 1
    aq_map = jnp.sum(jnp.where(last, p["acs_map"], 0.0), axis=0, keepdims=True)
    e_exp = jnp.exp(p["acs_map"])
    dte = jnp.exp(aq_map - p["acs_map"])
    cd = jnp.exp(aq_map)
    return last, e_exp, dte, cd


def _head_decay(a_cs, a_cs_t, head, tri):
    lane = lax.broadcasted_iota(jnp.int32, (1, DT_PAD), 1)
    sub = lax.broadcasted_iota(jnp.int32, (DT_PAD, 1), 0)
    col = jnp.sum(jnp.where(lane == head, a_cs, 0.0), axis=1, keepdims=True)
    row = jnp.sum(jnp.where(sub == head, a_cs_t, 0.0), axis=0, keepdims=True)
    return jnp.exp(jnp.where(tri, col - row, -1e30))


def _gate_fwd(y, z, nw):
    sz = _sigmoid(z)
    zg = z * sz
    yg = y * zg
    outs, rs = [], []
    for g in range(SSM_GROUPS):
        gs = slice(g * GROUP_W, (g + 1) * GROUP_W)
        o, r = _rms_fwd(yg[:, gs], nw[:, gs])
        outs.append(o)
        rs.append(r)
    return sz, zg, yg, outs, rs


def _ssd_const_specs():
    def whole(shape):
        nd = len(shape)
        return pl.BlockSpec(tuple(shape), lambda c: (0,) * nd)
    return [whole((CONV_K, 1, CONV_CH)), whole((1, CONV_CH)), whole((1, DT_PAD)), whole((1, DT_PAD)),
            whole((1, SSM_WIDTH)), whole((1, SSM_WIDTH)), whole((DT_PAD, SSM_WIDTH)), whole((SSM_WIDTH, DT_PAD))]


def _mixer_fwd(p_uv, p_xbc, p_z, p_dt, lnw, lnb, w_cat, bmap, conv_w, conv_b, dt_bias, a_log, dskip_map, norm_w,
               e_bf, et_bf, n_seq, carried=None):
    n_tok = p_xbc.shape[0]
    nc = n_tok // n_seq // CHUNK

    def body(puv3, xr3, z3, pdt3, lnw_ref, lnb_ref, wcat_ref, bmap_ref,
             cw_ref, cb_ref, dtb_ref, alog_ref, dsk_ref, nw_ref, e_ref, et_ref,
             ya3, yb3, yssd3, sprev3, wm_scr, prev3_scr, s3_scr):
        @pl.when(pl.program_id(0) == 0)
        def _():
            wm_scr[...] = _causal_w_cat(wcat_ref[...])
            prev3_scr[...] = jnp.zeros_like(prev3_scr)
            s3_scr[...] = jnp.zeros_like(s3_scr)

        for b in range(n_seq):
            one_sequence(puv3.at[b], xr3.at[b], z3.at[b], pdt3.at[b], lnw_ref, lnb_ref, bmap_ref,
                         cw_ref, cb_ref, dtb_ref, alog_ref, dsk_ref, nw_ref, e_ref, et_ref,
                         ya3.at[b], yb3.at[b], yssd3.at[b], sprev3.at[b], wm_scr, prev3_scr.at[b], s3_scr.at[b])

    def one_sequence(puv_ref, xr_ref, z_ref, pdt_ref, lnw_ref, lnb_ref, bmap_ref,
                     cw_ref, cb_ref, dtb_ref, alog_ref, dsk_ref, nw_ref, e_ref, et_ref,
                     ya_ref, yb_ref, yssd_ref, sprev_ref, wm_scr, prev_scr, s_scr):
        ya_ref[...] = _gmlp_chunk_fwd(puv_ref[...], lnw_ref[...], lnb_ref[...], e_ref[...], et_ref[...], wm_scr[...],
                                      bmap_ref[...])
        xr = xr_ref[...]
        p = _ssd_pre(xr, prev_scr[...], cw_ref, cb_ref[...], pdt_ref[...], dtb_ref[...], alog_ref[...], e_ref[...])
        _, e_exp, dte, cd = _ssd_maps(p)
        xs = p["xa"][:, :SSM_WIDTH]
        xd = xs * p["dt_map"]
        a_cs_t = p["a_cs"].T
        tri = _tril_mask()
        s_old = s_scr[...]
        sprev_ref[...] = s_old
        for g in range(SSM_GROUPS):
            gs = slice(g * GROUP_W, (g + 1) * GROUP_W)
            bm = p["xa"][:, SSM_WIDTH + g * SSM_STATE: SSM_WIDTH + (g + 1) * SSM_STATE].astype(BF16)
            cm = p["xa"][:, SSM_WIDTH + (SSM_GROUPS + g) * SSM_STATE: SSM_WIDTH + (SSM_GROUPS + g + 1) * SSM_STATE].astype(BF16)
            cb_mat = _dot_nt(cm, bm)
            xdg = xd[:, gs].astype(BF16)
            y_g = _dot(cm, s_old[:, gs]) * e_exp[:, gs] + dsk_ref[:, gs] * xs[:, gs]
            for r in range(SSM_GROUPS * 2):
                dm = _head_decay(p["a_cs"], a_cs_t, g * 4 + r, tri)
                full = jnp.dot((cb_mat * dm).astype(BF16), xdg, preferred_element_type=F32)
                y_g = y_g + jnp.where(_head_lane_mask(GROUP_W, r), full, 0.0)
            yssd_ref[:, gs] = y_g
            s_scr[:, gs] = cd[:, gs] * s_old[:, gs] + _dot_tn(bm, xd[:, gs] * dte[:, gs])
        _, _, _, outs, _ = _gate_fwd(yssd_ref[...], z_ref[...], nw_ref[...])
        for g in range(SSM_GROUPS):
            yb_ref[:, g * GROUP_W:(g + 1) * GROUP_W] = outs[g].astype(BF16)
        prev_scr[...] = xr[CHUNK - SUBLANES:, :]

    seq_len = n_tok // n_seq

    def rows(width):
        return pl.BlockSpec((n_seq, CHUNK, width), lambda c: (0, c, 0))

    def whole(shape):
        nd = len(shape)
        return pl.BlockSpec(tuple(shape), lambda c: (0,) * nd)

    def by_seq(a):
        return a.reshape(n_seq, seq_len, a.shape[-1])

    outs = _call_carrying(
        body, carried, name="mixer_fwd", grid=(nc,),
        in_specs=[rows(2 * GM_WIDTH), rows(CONV_CH), rows(SSM_WIDTH), rows(DT_PAD), whole(lnw.shape), whole(lnb.shape),
                  whole(w_cat.shape), whole(bmap.shape)] + _ssd_const_specs(),
        out_specs=[rows(GM_WIDTH), rows(SSM_WIDTH), rows(SSM_WIDTH), rows(SSM_WIDTH)],
        out_shape=(_sds((n_seq, seq_len, GM_WIDTH), BF16), _sds((n_seq, seq_len, SSM_WIDTH), BF16),
                   _sds((n_seq, seq_len, SSM_WIDTH), F32), _sds((n_seq, seq_len, SSM_WIDTH), F32)),
        scratch_shapes=[pltpu.VMEM((CHUNK, N_HEADS * CHUNK), BF16), pltpu.VMEM((n_seq, SUBLANES, CONV_CH), F32),
                        pltpu.VMEM((n_seq, SSM_STATE, SSM_WIDTH), F32)],
        operands=[by_seq(p_uv), by_seq(p_xbc), by_seq(p_z), by_seq(p_dt), lnw, lnb, w_cat, bmap, conv_w, conv_b, dt_bias,
                  a_log, dskip_map, norm_w, e_bf, et_bf])
    return tuple(o.reshape(n_tok, o.shape[-1]) for o in outs[:4]) + tuple(outs[4:])


def _outproj_fwd(ya, yb, x, w_out, nw_post, nw_pre2, tm=256):
    n_tok = x.shape[0]

    def body(ya_ref, yb_ref, x_ref, wo_ref, nwa_ref, nwb_ref, o_ref, x1_ref, h2_ref):
        o = jnp.dot(ya_ref[...], wo_ref[:GM_WIDTH, :], preferred_element_type=F32)
        o = o + jnp.dot(yb_ref[...], wo_ref[GM_WIDTH:, :], preferred_element_type=F32)
        on, _ = _rms_fwd(o, nwa_ref[...])
        x1 = x_ref[...] + on
        h2, _ = _rms_fwd(x1, nwb_ref[...])
        o_ref[...] = o
        x1_ref[...] = x1
        h2_ref[...] = h2.astype(BF16)

    return _rows_call("outproj_fwd", body, tm, [ya, yb, x], [w_out, nw_post, nw_pre2],
                      [_sds((n_tok, D_MODEL), F32), _sds((n_tok, D_MODEL), F32), _sds((n_tok, D_MODEL), BF16)])


def _up_cols(wup_ref, j):
    per = (D_FF // N_CHIPS) // FF_TILE
    return wup_ref[j // per, :, (j % per) * FF_TILE:(j % per + 1) * FF_TILE]


def _down_rows(wda_ref, wdb_ref, j):
    assert 2 * FF_TILE == D_FF // N_CHIPS
    return (wda_ref if j % 2 == 0 else wdb_ref)[j // 2]


def _skewed_rows_call(name, main, tail, tm, lead_ins, lag_ins, const_ins, lead_outs, lag_outs, acc_outs, carry,
                      streamed, tile_copies, n_copies):
    n_rows = lead_ins[0].shape[0]
    assert n_rows % tm == 0
    n = n_rows // tm
    counts = [len(lead_ins), len(lag_ins), len(const_ins), len(streamed), len(lead_outs), len(lag_outs), len(acc_outs),
              1, len(streamed)]

    def kern(*refs):
        groups, pos = [], 0
        for cnt in counts:
            groups.append(refs[pos:pos + cnt])
            pos += cnt
        lead_i, lag_i, consts, w_hbm, lead_o, lag_o, accs, (carry_scr,), w_vmem = groups
        sems = refs[pos]
        i = pl.program_id(0)
        pieces, k = [], 0
        for piece in tile_copies(w_hbm, w_vmem):
            pieces.append([pltpu.make_async_copy(src, dst, sems.at[k + q]) for q, (src, dst) in enumerate(piece)])
            k += len(piece)

        def ready(j):
            for cp in pieces[j]:
                cp.wait()

        @pl.when(i == 0)
        def _():
            for piece in pieces:
                for cp in piece:
                    cp.start()
            for a in accs:
                a[...] = jnp.zeros_like(a)
            carry_scr[...] = main(lead_i, consts, lead_o, w_vmem, ready)

        @pl.when(jnp.logical_and(i > 0, i < n))
        def _():
            previous = carry_scr[...]
            carry_scr[...] = main(lead_i, consts, lead_o, w_vmem, lambda j: None)
            tail(previous, lag_i, consts, lag_o, accs)

        @pl.when(i == n)
        def _():
            tail(carry_scr[...], lag_i, consts, lag_o, accs)

    def lead(width):
        return pl.BlockSpec((tm, width), lambda i: (jnp.minimum(i, n - 1), 0))

    def lag(width):
        return pl.BlockSpec((tm, width), lambda i: (jnp.maximum(i - 1, 0), 0))

    def whole(shape):
        nd = len(shape)
        return pl.BlockSpec(tuple(shape), lambda i: (0,) * nd)

    return pl.pallas_call(
        kern, name=name, grid=(n + 1,),
        in_specs=([lead(a.shape[1]) for a in lead_ins] + [lag(a.shape[1]) for a in lag_ins]
                  + [whole(a.shape) for a in const_ins] + [_HBM] * len(streamed)),
        out_specs=[lead(s.shape[1]) for s in lead_outs] + [lag(s.shape[1]) for s in lag_outs] + [whole(s.shape) for s in acc_outs],
        out_shape=tuple(lead_outs) + tuple(lag_outs) + tuple(acc_outs),
        scratch_shapes=([pltpu.VMEM(carry, F32)] + [pltpu.VMEM(a.shape, a.dtype) for a in streamed]
                        + [pltpu.SemaphoreType.DMA((n_copies,))]),
        compiler_params=_cparams(1),
    )(*lead_ins, *lag_ins, *const_ins, *streamed)


def _mlp_weight_pieces(order):
    per = (D_FF // N_CHIPS) // FF_TILE

    def tile_copies(hbm, vmem):
        pieces = []
        for j in range(D_FF // FF_TILE):
            cols = (j // per, slice(None), pl.ds((j % per) * FF_TILE, FF_TILE))
            up = (hbm[0].at[cols], vmem[0].at[cols])
            down = (hbm[1 + j % 2].at[j // 2], vmem[1 + j % 2].at[j // 2])
            pieces.append([up, down] if order == "up_down" else [down, up])
        return pieces

    return tile_copies


def _mlp_fwd(h2, x1, tgt, w_up, w_down_a, w_down_b, nw, tm=512):
    n_tok = x1.shape[0]

    def main(lead_i, consts, lead_o, weights, ready):
        (h2_ref,), (f_ref,), (wup_ref, wda_ref, wdb_ref) = lead_i, lead_o, weights
        h2v = h2_ref[...]
        acc = jnp.zeros((tm, D_MODEL), F32)
        for j in range(D_FF // FF_TILE):
            cs = slice(j * FF_TILE, (j + 1) * FF_TILE)
            ready(j)
            u = jnp.dot(h2v, _up_cols(wup_ref, j), preferred_element_type=F32)
            f = jnp.square(jnp.maximum(u, 0.0)).astype(BF16)
            f_ref[:, cs] = f
            acc = acc + jnp.dot(f, _down_rows(wda_ref, wdb_ref, j), preferred_element_type=F32)
        return acc

    def tail(acc, lag_i, consts, lag_o, accs):
        (x1_ref, tgt_ref), (nw_ref,), (dd_ref, dy_ref), (loss_ref, dnw_ref) = lag_i, consts, lag_o, accs
        dn, r = _rms_fwd(acc, nw_ref[...])
        e = x1_ref[...] + dn - tgt_ref[...]
        loss_ref[...] += jnp.full(loss_ref.shape, (0.5 / D_MODEL) * jnp.sum(e * e), F32)
        dy = e * (1.0 / D_MODEL)
        dd, dnw = _rms_bwd(acc, r, nw_ref[...], dy)
        dy_ref[...] = dy
        dd_ref[...] = dd.astype(BF16)
        dnw_ref[...] += dnw

    return _skewed_rows_call(
        "mlp_fwd", main, tail, tm, [h2], [x1, tgt], [nw],
        [_sds((n_tok, D_FF), BF16)], [_sds((n_tok, D_MODEL), BF16), _sds((n_tok, D_MODEL), F32)],
        [_sds((8, 128), F32), _sds((1, D_MODEL), F32)], carry=(tm, D_MODEL),
        streamed=[w_up, w_down_a, w_down_b], tile_copies=_mlp_weight_pieces("up_down"), n_copies=2 * (D_FF // FF_TILE))


def _mlp_bwd(dd, f, x1, dy, w_down_a, w_down_b, w_up, nw, tm=256):
    n_tok = x1.shape[0]

    def main(lead_i, consts, lead_o, weights, ready):
        (dd_ref, f_ref), (dup_ref,), (wup_ref, wda_ref, wdb_ref) = lead_i, lead_o, weights
        ddv = dd_ref[...]
        acc = jnp.zeros((tm, D_MODEL), F32)
        for j in range(D_FF // FF_TILE):
            cs = slice(j * FF_TILE, (j + 1) * FF_TILE)
            ready(j)
            df = _dot_nt(ddv, _down_rows(wda_ref, wdb_ref, j))
            du = (df * (2.0 * jnp.sqrt(f_ref[:, cs].astype(F32)))).astype(BF16)
            dup_ref[:, cs] = du
            acc = acc + _dot_nt(du, _up_cols(wup_ref, j))
        return acc

    def tail(acc, lag_i, consts, lag_o, accs):
        (x1_ref, dy_ref), (nw_ref,), (dx1_ref,), (dnw_ref,) = lag_i, consts, lag_o, accs
        x1v = x1_ref[...]
        _, r = _rms_fwd(x1v, nw_ref[...])
        dx, dnw = _rms_bwd(x1v, r, nw_ref[...], acc)
        dx1_ref[...] = dy_ref[...] + dx
        dnw_ref[...] += dnw

    return _skewed_rows_call(
        "mlp_bwd", main, tail, tm, [dd, f], [x1, dy], [nw],
        [_sds((n_tok, D_FF), BF16)], [_sds((n_tok, D_MODEL), F32)], [_sds((1, D_MODEL), F32)], carry=(tm, D_MODEL),
        streamed=[w_up, w_down_a, w_down_b], tile_copies=_mlp_weight_pieces("down_up"), n_copies=2 * (D_FF // FF_TILE))


def _outproj_bwd(dx1, o, w_out, nw, tm=256, carried=None):
    n_tok = dx1.shape[0]

    def body(dx1_ref, o_ref, wo_ref, nw_ref, do_ref, dya_ref, dyb_ref, dnw_ref):
        ov = o_ref[...]
        _, r = _rms_fwd(ov, nw_ref[...])
        do, dnw = _rms_bwd(ov, r, nw_ref[...], dx1_ref[...])
        dob = do.astype(BF16)
        do_ref[...] = dob
        dya_ref[...] = _dot_nt(dob, wo_ref[:GM_WIDTH, :])
        dyb_ref[...] = _dot_nt(dob, wo_ref[GM_WIDTH:, :])
        dnw_ref[...] += dnw

    return _rows_call("outproj_bwd", body, tm, [dx1, o], [w_out, nw],
                      [_sds((n_tok, D_MODEL), BF16), _sds((n_tok, GM_WIDTH), F32), _sds((n_tok, SSM_WIDTH), F32)],
                      [_sds((1, D_MODEL), F32)], carried=carried)


def _gmlp_bwd(p_uv, dya, lnw, lnb, e_bf, et_bf, w_cat, w_stack, bmap, carried=None):
    n_tok = p_uv.shape[0]
    chunks_per_step = 2

    def body(puv_ref, dya_ref, lnw_ref, lnb_ref, e_ref, et_ref, wcat_ref, wstack_ref, bmap_ref,
             dpuv_ref, dws_ref, dbs_ref, dlnw_ref, dlnb_ref, wm_scr, wsm_scr):
        t_stk = lax.broadcasted_iota(jnp.int32, (N_HEADS * CHUNK, CHUNK), 0) % CHUNK
        s_stk = lax.broadcasted_iota(jnp.int32, (N_HEADS * CHUNK, CHUNK), 1)

        @pl.when(pl.program_id(0) == 0)
        def _():
            wm_scr[...] = _causal_w_cat(wcat_ref[...])
            wsm_scr[...] = jnp.where(t_stk >= s_stk, wstack_ref[...], 0.0).astype(BF16)

        lnw_v = lnw_ref[...]
        e_v, et_v = e_ref[...], et_ref[...]

        def one_chunk(rows):
            u, v, gu, tu, tv, rstd, xhat, vn = _gmlp_common(puv_ref[rows, :], lnw_v, lnb_ref[...], e_v, et_v)
            vnb = vn.astype(BF16)
            mixed = jnp.dot(wm_scr[...], _head_blocks(vnb), preferred_element_type=F32) + bmap_ref[...]
            dy = dya_ref[rows, :]
            du = dy * mixed * _gelu_grad(u, tu)
            dmixed = dy * gu
            (dbs,) = _seg_dots([dmixed], et_v)
            dblocks = _head_blocks(dmixed.astype(BF16))
            dvn = lax.dot_general(wsm_scr[...], dblocks, (((0,), (0,)), ((), ())), preferred_element_type=F32)
            dws = lax.dot_general(dblocks, vnb, (((1,), (1,)), ((), ())), preferred_element_type=F32)
            dxh = dvn * lnw_v
            m1, m2 = _seg_dots([dxh, dxh * xhat], et_v)
            m1, m2 = _seg_dots([m1 * (1.0 / HEAD_DIM), m2 * (1.0 / HEAD_DIM)], e_v)
            dgv = rstd * (dxh - m1 - xhat * m2)
            dv = dgv * _gelu_grad(v, tv)
            dpuv_ref[rows, :GM_WIDTH] = du.astype(BF16)
            dpuv_ref[rows, GM_WIDTH:] = dv.astype(BF16)
            return dbs, dws, jnp.sum(dvn * xhat, axis=0, keepdims=True), jnp.sum(dvn, axis=0, keepdims=True)

        parts = [one_chunk(slice(k * CHUNK, (k + 1) * CHUNK)) for k in range(chunks_per_step)]
        dbs, dws, dlnw, dlnb = [functools.reduce(lambda a, b: a + b, vals) for vals in zip(*parts)]
        dbs_ref[...] += dbs
        dws_ref[...] += jnp.where(t_stk >= s_stk, dws, 0.0)
        dlnw_ref[...] += dlnw
        dlnb_ref[...] += dlnb

    return _rows_call(
        "gmlp_bwd", body, chunks_per_step * CHUNK, [p_uv, dya], [lnw, lnb, e_bf, et_bf, w_cat, w_stack, bmap],
        [_sds((n_tok, 2 * GM_WIDTH), BF16)],
        [_sds((N_HEADS * CHUNK, CHUNK), F32), _sds((CHUNK, DT_PAD), F32), _sds((1, GM_WIDTH), F32),
         _sds((1, GM_WIDTH), F32)],
        scratch=[pltpu.VMEM((CHUNK, N_HEADS * CHUNK), BF16), pltpu.VMEM((N_HEADS * CHUNK, CHUNK), BF16)],
        carried=carried)


def _ssd_bwd(p_xbc, p_z, p_dt, yssd, sprev, dyb, conv_w, conv_b, dt_bias, a_log, dskip_map, norm_w, e_bf, et_bf, n_seq,
             carried=None):
    n_tok = p_xbc.shape[0]
    nc = n_tok // n_seq // CHUNK

    def body(xr3, xprev3, z3, pdt3, yssd3, sprev3, dyb3,
             cw_ref, cb_ref, dtb_ref, alog_ref, dsk_ref, nw_ref, e_ref, et_ref,
             dpxbc3, dpz3, dpdt3, dcw_ref, dcb_ref, ddtb_ref, dalog_ref, ddsk_ref, dnw_ref,
             ds3_scr, nxt3_scr, dxa3_scr):
        @pl.when(pl.program_id(0) == 0)
        def _():
            for a in (dcw_ref, dcb_ref, ddtb_ref, dalog_ref, ddsk_ref, dnw_ref, ds3_scr, nxt3_scr):
                a[...] = jnp.zeros_like(a)

        for b in range(n_seq):
            one_sequence(xr3.at[b], xprev3.at[b], z3.at[b], pdt3.at[b], yssd3.at[b], sprev3.at[b], dyb3.at[b],
                         cw_ref, cb_ref, dtb_ref, alog_ref, dsk_ref, nw_ref, e_ref, et_ref,
                         dpxbc3.at[b], dpz3.at[b], dpdt3.at[b], dcw_ref, dcb_ref, ddtb_ref, dalog_ref, ddsk_ref, dnw_ref,
                         ds3_scr.at[b], nxt3_scr.at[b], dxa3_scr.at[b])

    def one_sequence(xr_ref, xprev_ref, z_ref, pdt_ref, yssd_ref, sprev_ref, dyb_ref,
                     cw_ref, cb_ref, dtb_ref, alog_ref, dsk_ref, nw_ref, e_ref, et_ref,
                     dpxbc_ref, dpz_ref, dpdt_ref, dcw_ref, dcb_ref, ddtb_ref, dalog_ref, ddsk_ref, dnw_ref,
                     ds_scr, nxt_scr, dxa_scr):
        chunk = nc - 1 - pl.program_id(0)
        xr = xr_ref[...]
        prev = jnp.where(chunk == 0, 0.0, xprev_ref[...])
        et_v = et_ref[...]
        p = _ssd_pre(xr, prev, cw_ref, cb_ref[...], pdt_ref[...], dtb_ref[...], alog_ref[...], e_ref[...])
        last, e_exp, dte, cd = _ssd_maps(p)
        rowi = p["rowi"]
        xs = p["xa"][:, :SSM_WIDTH]
        xd = xs * p["dt_map"]
        a_cs_t = p["a_cs"].T
        tri = _tril_mask()
        dsk = dsk_ref[...]
        nw_v = nw_ref[...]

        yv = yssd_ref[...]
        zv = z_ref[...]
        sz, zg, yg, _, rs = _gate_fwd(yv, zv, nw_v)
        dout = dyb_ref[...]
        for g in range(SSM_GROUPS):
            gs = slice(g * GROUP_W, (g + 1) * GROUP_W)
            dyg_g, dnw_g = _rms_bwd(yg[:, gs], rs[g], nw_v[:, gs], dout[:, gs])
            dnw_ref[:, gs] += dnw_g
            dxa_scr[:, gs] = dyg_g
        dyg = dxa_scr[:, :SSM_WIDTH]
        d_y = dyg * zg
        dpz_ref[...] = (dyg * yv * (sz + zv * sz * (1.0 - sz))).astype(BF16)

        s_prev = sprev_ref[...]
        ds_next = ds_scr[...]
        lane_dt = lax.broadcasted_iota(jnp.int32, (1, DT_PAD), 1)
        da_cols = jnp.zeros((CHUNK, DT_PAD), F32)
        for g in range(SSM_GROUPS):
            gs = slice(g * GROUP_W, (g + 1) * GROUP_W)
            b_off = SSM_WIDTH + g * SSM_STATE
            c_off = SSM_WIDTH + (SSM_GROUPS + g) * SSM_STATE
            bm = p["xa"][:, b_off:b_off + SSM_STATE].astype(BF16)
            cm = p["xa"][:, c_off:c_off + SSM_STATE].astype(BF16)
            cb_mat = _dot_nt(cm, bm)
            d_yg = d_y[:, gs]
            d_ygb = d_yg.astype(BF16)
            xdg = xd[:, gs]
            xdgb = xdg.astype(BF16)
            ds_g = ds_next[:, gs]
            sp_g = s_prev[:, gs]
            bds = _dot(bm, ds_g)
            dcs = d_yg * e_exp[:, gs]
            d_c = _dot_nt(dcs, sp_g)
            ds_scr[:, gs] = cd[:, gs] * ds_g + _dot_tn(cm, dcs)
            d_b = _dot_nt(xdg * dte[:, gs], ds_g)
            dxd_g = bds * dte[:, gs]
            sum_dcb = jnp.zeros((CHUNK, CHUNK), F32)
            for r in range(SSM_GROUPS * 2):
                head = g * 4 + r
                mask = _head_lane_mask(GROUP_W, r)
                dm = _head_decay(p["a_cs"], a_cs_t, head, tri)
                m_mat = cb_mat * dm
                g_mat = _dot_nt(jnp.where(mask, d_yg, 0.0), xdgb)
                w_mat = g_mat * m_mat
                sum_dcb = sum_dcb + g_mat * dm
                dxd_g = dxd_g + jnp.where(mask, _dot_tn(m_mat, d_ygb), 0.0)
                da_h = jnp.sum(w_mat - w_mat.T, axis=1, keepdims=True)
                da_cols = da_cols + jnp.where(lane_dt == head, da_h, 0.0)
            d_c = d_c + _dot(sum_dcb, bm)
            d_b = d_b + _dot_tn(sum_dcb, cm)
            dxa_scr[:, b_off:b_off + SSM_STATE] = d_b
            dxa_scr[:, c_off:c_off + SSM_STATE] = d_c
            y_off_g = _dot(cm, sp_g) * e_exp[:, gs]
            t3 = bds * xdg * dte[:, gs]
            tail = jnp.sum(t3, axis=0, keepdims=True) + jnp.sum(ds_g * sp_g, axis=0, keepdims=True) * cd[:, gs]
            pre_g = d_yg * y_off_g - t3 + jnp.where(last, tail, 0.0)
            s_pre, ddt_g, s_dsk = _seg_dots([pre_g, dxd_g * xs[:, gs], d_yg * xs[:, gs]], et_v[gs, :])
            da_cols = da_cols + s_pre
            ddsk_ref[...] += jnp.sum(s_dsk, axis=0, keepdims=True)
            dxa_scr[:, gs] = dxd_g * p["dt_map"][:, gs] + dsk[:, gs] * d_yg
            if g == 0:
                ddt = ddt_g
            else:
                ddt = ddt + ddt_g
        r_i = lax.broadcasted_iota(jnp.int32, (CHUNK, CHUNK), 0)
        c_i = lax.broadcasted_iota(jnp.int32, (CHUNK, CHUNK), 1)
        ddta = _tri_dot(r_i <= c_i, da_cols, terms=2)
        ddt = ddt + ddta * p["a_neg"]
        dalog_ref[...] += jnp.sum(ddta * p["dt"], axis=0, keepdims=True) * p["a_neg"]
        draw = ddt * _sigmoid(p["pre"])
        ddtb_ref[...] += jnp.sum(draw, axis=0, keepdims=True)
        dpdt_ref[...] = draw.astype(BF16)

        xc = p["xc"]
        sg = p["sg"]
        dxc = dxa_scr[...] * (sg + xc * sg * (1.0 - sg))
        dcb_ref[...] += jnp.sum(dxc, axis=0, keepdims=True)
        for k in range(CONV_K):
            dcw_ref[k] += jnp.sum(dxc * p["shifted"][k], axis=0, keepdims=True)
        nxt = nxt_scr[...]
        dxr = cw_ref[3] * dxc
        for s in range(1, CONV_K):
            dxr = dxr + cw_ref[CONV_K - 1 - s] * _shift_up(dxc, nxt, s)
        dpxbc_ref[...] = dxr.astype(BF16)
        nxt_scr[...] = dxc[:SUBLANES, :]

    seq_len = n_tok // n_seq

    def rows(width):
        return pl.BlockSpec((n_seq, CHUNK, width), lambda s: (0, nc - 1 - s, 0))

    tiles = CHUNK // SUBLANES
    prev_rows = pl.BlockSpec((n_seq, SUBLANES, CONV_CH), lambda s: (0, jnp.maximum((nc - 1 - s) * tiles - 1, 0), 0))

    def whole(shape):
        nd = len(shape)
        return pl.BlockSpec(tuple(shape), lambda s: (0,) * nd)

    def by_seq(a):
        return a.reshape(n_seq, seq_len, a.shape[-1])

    acc_shapes = [(CONV_K, 1, CONV_CH), (1, CONV_CH), (1, DT_PAD), (1, DT_PAD), (1, DT_PAD), (1, SSM_WIDTH)]
    xbc3 = by_seq(p_xbc)
    outs = _call_carrying(
        body, carried, name="ssd_bwd", grid=(nc,),
        in_specs=[rows(CONV_CH), prev_rows, rows(SSM_WIDTH), rows(DT_PAD), rows(SSM_WIDTH), rows(SSM_WIDTH),
                  rows(SSM_WIDTH)] + _ssd_const_specs(),
        out_specs=[rows(CONV_CH), rows(SSM_WIDTH), rows(DT_PAD)] + [whole(s) for s in acc_shapes],
        out_shape=tuple([_sds((n_seq, seq_len, CONV_CH), BF16), _sds((n_seq, seq_len, SSM_WIDTH), BF16),
                         _sds((n_seq, seq_len, DT_PAD), BF16)] + [_sds(s, F32) for s in acc_shapes]),
        scratch_shapes=[pltpu.VMEM((n_seq, SSM_STATE, SSM_WIDTH), F32), pltpu.VMEM((n_seq, SUBLANES, CONV_CH), F32),
                        pltpu.VMEM((n_seq, CHUNK, CONV_CH), F32)],
        operands=[xbc3, xbc3, by_seq(p_z), by_seq(p_dt), by_seq(yssd), by_seq(sprev), by_seq(dyb), conv_w, conv_b, dt_bias,
                  a_log, dskip_map, norm_w, e_bf, et_bf])
    return tuple(o.reshape(n_tok, o.shape[-1]) for o in outs[:3]) + tuple(outs[3:])


def _inproj_bwd(dp_uv, dp_xbc, dp_z, dp_dt, x, dx1, w_uv, w_xbc, w_z, w_dt, nw, tm=256, carried=None):
    n_tok = x.shape[0]

    def body(duv_ref, dxbc_ref, dz_ref, ddt_ref, x_ref, dx1_ref, wuv_ref, wxbc_ref, wz_ref, wdt_ref, nw_ref,
             gx_ref, h_ref, dnw_ref):
        dh = _dot_nt(duv_ref[...], wuv_ref[...]) + _dot_nt(dxbc_ref[...], wxbc_ref[...])
        dh = dh + _dot_nt(dz_ref[...], wz_ref[...]) + _dot_nt(ddt_ref[...], wdt_ref[...])
        xv = x_ref[...]
        h, r = _rms_fwd(xv, nw_ref[...])
        dx, dnw = _rms_bwd(xv, r, nw_ref[...], dh)
        gx_ref[...] = dx1_ref[...] + dx
        h_ref[...] = h.astype(BF16)
        dnw_ref[...] += dnw

    return _rows_call("inproj_bwd", body, tm, [dp_uv, dp_xbc, dp_z, dp_dt, x, dx1], [w_uv, w_xbc, w_z, w_dt, nw],
                      [_sds((n_tok, D_MODEL), F32), _sds((n_tok, D_MODEL), BF16)], [_sds((1, D_MODEL), F32)],
                      carried=carried)


def _const_maps():
    lane = jnp.arange(SSM_WIDTH) // HEAD_DIM
    e_bf = (jnp.arange(DT_PAD)[:, None] == lane[None, :]).astype(BF16)
    return e_bf, e_bf.T


def _pad_lanes(v, width):
    return jnp.pad(v, ((0, 0), (0, width - v.shape[1])))


SHARD_COLS = IN_COLS // N_CHIPS
_UV_END = 2 * GM_WIDTH
_Z_END = _UV_END + SSM_WIDTH
_XBC_END = _Z_END + CONV_CH


def _cols_from_shards(w4, lo, hi):
    pieces = []
    for j in range(N_CHIPS):
        a, b = max(lo, j * SHARD_COLS), min(hi, (j + 1) * SHARD_COLS)
        if a < b:
            pieces.append(w4[j][:, a - j * SHARD_COLS:b - j * SHARD_COLS])
    return pieces[0] if len(pieces) == 1 else jnp.concatenate(pieces, axis=1)


def _shards_from_cols(blocks):
    shards = []
    for j in range(N_CHIPS):
        pieces = []
        for arr, lo, hi in blocks:
            a, b = max(lo, j * SHARD_COLS), min(hi, (j + 1) * SHARD_COLS)
            if a < b:
                pieces.append(arr[:, a - lo:b - lo])
        shards.append(pieces[0] if len(pieces) == 1 else jnp.concatenate(pieces, axis=1))
    return jnp.stack(shards)


def _forward_backward(x, tgt, w_in4, conv_w, small, out_shard, up_shard, down_shard, core, adam_args):
    n_seq, seq_len, _ = x.shape
    n_tok = n_seq * seq_len
    x2 = x.reshape(n_tok, D_MODEL)
    tgt2 = tgt.reshape(n_tok, D_MODEL)
    e_bf, et_bf = _const_maps()

    w_uv = _cols_from_shards(w_in4, 0, _UV_END)
    w_z = _cols_from_shards(w_in4, _UV_END, _Z_END)
    w_xbc = _cols_from_shards(w_in4, _Z_END, _XBC_END)
    w_dt = _pad_lanes(_cols_from_shards(w_in4, _XBC_END, IN_COLS), DT_PAD)

    nw_pre = small["norm_mix_pre"]
    lnw = small["gm_ln_w"].reshape(1, GM_WIDTH)
    lnb = small["gm_ln_b"].reshape(1, GM_WIDTH)
    w_stack = small["gm_w_s"].reshape(N_HEADS * CHUNK, CHUNK)
    w_cat = jnp.transpose(small["gm_w_s"], (1, 0, 2)).reshape(CHUNK, N_HEADS * CHUNK)
    bmap = jnp.repeat(small["gm_b_s"].T, HEAD_DIM, axis=1)
    cw3 = conv_w.reshape(CONV_K, 1, CONV_CH)
    conv_b = small["conv_b"]
    dt_bias = _pad_lanes(small["dt_bias"], DT_PAD)
    a_log = _pad_lanes(small["a_log"], DT_PAD)
    dskip_map = jnp.repeat(small["d_skip"], HEAD_DIM, axis=1)
    ssm_nw = small["ssm_norm_w"]

    half = down_shard.shape[0] // 2
    p_uv, p_xbc, p_z, p_dt, w_out4, w_down_a = _inproj_fwd(
        x2, nw_pre, w_uv, w_xbc, w_z, w_dt, carried=_allgather_exchange([out_shard, down_shard[:half]]))
    ssd_consts = (cw3, conv_b, dt_bias, a_log, dskip_map, ssm_nw, e_bf, et_bf)
    ya, yb, yssd, sprev, w_up4, w_down_b = _mixer_fwd(
        p_uv, p_xbc, p_z, p_dt, lnw, lnb, w_cat, bmap, *ssd_consts, n_seq,
        carried=_allgather_exchange([up_shard, down_shard[half:]]))
    w_out_b = w_out4.reshape(D_MODEL, D_MODEL)
    o, x1, h2 = _outproj_fwd(ya, yb, x2, w_out_b, small["norm_mix_post"], small["norm_ffn_pre"])
    f, dd, dy, loss_acc, d_nffn_post = _mlp_fwd(h2, x1, tgt2, w_up4, w_down_a, w_down_b, small["norm_ffn_post"])

    dup, dx1, d_nffn_pre = _mlp_bwd(dd, f, x1, dy, w_down_a, w_down_b, w_up4, small["norm_ffn_pre"])
    tk = min(DW_TOKENS_PER_STEP, n_tok)
    g_up = _matmul_tn("dw_up", h2, dup, D_MODEL, D_MODEL, tk, stacked=True)
    g_down = _matmul_tn("dw_down", f, dd, 1024, D_MODEL, tk).reshape(N_CHIPS, D_FF // N_CHIPS, D_MODEL)
    do, dya, dyb, d_nmix_post, got_up, got_down = _outproj_bwd(
        dx1, o, w_out_b, small["norm_mix_post"], carried=_pair_exchange([g_up, g_down]))
    h_up = _pair_sum(core, g_up, got_up, 256)
    h_down = _pair_sum(core, g_down, got_down, 256)
    g_out_a = _matmul_tn("dw_out_a", ya, do, GM_WIDTH, D_MODEL, tk)
    g_out_b = _matmul_tn("dw_out_b", yb, do, SSM_WIDTH, D_MODEL, tk)
    g_out = jnp.concatenate([g_out_a, g_out_b], axis=0).reshape(N_CHIPS, D_MODEL // N_CHIPS, D_MODEL)
    dp_uv, d_ws, d_bs_t, d_lnw, d_lnb, slab_up, got_out = _gmlp_bwd(
        p_uv, dya, lnw, lnb, e_bf, et_bf, w_cat, w_stack, bmap,
        carried=_both(_chip_exchange([h_up]), _pair_exchange([g_out])))
    h_out = _pair_sum(core, g_out, got_out, 128)
    early = {
        "gm_ln_w": d_lnw.reshape(N_HEADS, HEAD_DIM), "gm_ln_b": d_lnb.reshape(N_HEADS, HEAD_DIM),
        "gm_w_s": d_ws.reshape(N_HEADS, CHUNK, CHUNK), "gm_b_s": d_bs_t[:, :N_HEADS].T,
        "norm_mix_post": d_nmix_post, "norm_ffn_pre": d_nffn_pre, "norm_ffn_post": d_nffn_post,
    }
    packed_early = _pack(early, tuple(early), tail=loss_acc[0, 0].reshape(1))
    (dp_xbc, dp_z, dp_dt, d_cw, d_cb, d_dtb, d_alog, d_dsk, d_ssm_nw, slab_down, slab_out, all_early) = _ssd_bwd(
        p_xbc, p_z, p_dt, yssd, sprev, dyb, *ssd_consts, n_seq,
        carried=_both(_chip_exchange([h_down, h_out]), _device_gather_exchange(packed_early)))
    gx, h, d_nmix_pre = _inproj_bwd(dp_uv, dp_xbc, dp_z, dp_dt, x2, dx1, w_uv, w_xbc, w_z, w_dt, nw_pre)
    late = {
        "norm_mix_pre": d_nmix_pre, "conv_w": d_cw.reshape(CONV_K, CONV_CH), "conv_b": d_cb,
        "dt_bias": d_dtb[:, :N_HEADS], "a_log": d_alog[:, :N_HEADS], "d_skip": d_dsk[:, :N_HEADS],
        "ssm_norm_w": d_ssm_nw,
    }
    g_uv, all_late = _matmul_tn("dw_in_uv", h, dp_uv, D_MODEL, 2 * GM_WIDTH, tk,
                                carried=_device_gather_exchange(_pack(late, tuple(late))))
    sum_early = _ordered_sum("small_sum_early", all_early)
    small_sum = _unpack(sum_early, {n: v.shape for n, v in early.items()}, tuple(early))
    small_sum.update(_unpack(_ordered_sum("small_sum_late", all_late), {n: v.shape for n, v in late.items()}, tuple(late)))
    loss = sum_early.reshape(-1)[sum(v.size for v in early.values())]
    g_xbc = _matmul_tn("dw_in_xbc", h, dp_xbc, D_MODEL, CONV_CH, tk)
    g_z = _matmul_tn("dw_in_z", h, dp_z, D_MODEL, SSM_WIDTH, tk)
    g_dt = _matmul_tn("dw_in_dt", h, dp_dt, D_MODEL, DT_PAD, tk)

    g_in = _shards_from_cols([(g_uv, 0, _UV_END), (g_z, _UV_END, _Z_END), (g_xbc, _Z_END, _XBC_END),
                              (g_dt, _XBC_END, IN_COLS)])

    red_up, red_down, red_out = _chip_sum(slab_up, 256), _chip_sum(slab_down, 256), _chip_sum(slab_out, 128)
    oth_up, oth_down, oth_out, got_in = _run_exchange(
        "grad_pair_swap", _both(_pair_swap([red_up, red_down, red_out]), _pair_exchange([g_in])))
    h_in = _pair_sum(core, g_in, got_in, 256)
    (slab_in,) = _run_exchange("grad_chip_exchange", _chip_exchange([h_in]))
    res = _adamw_halves("adamw_mlp", [(adam_args["w_up"][0], red_up, oth_up) + adam_args["w_up"][1:],
                                      (adam_args["w_down"][0], red_down, oth_down) + adam_args["w_down"][1:]], 256)
    big_out = {"w_up": res[0:4], "w_down": res[4:8]}
    big_out["w_out"] = _adamw_halves("adamw_w_out", [(adam_args["w_out"][0], red_out, oth_out) + adam_args["w_out"][1:]], 128)
    red_in = _chip_sum(slab_in, 256)
    (oth_in,) = _run_exchange("grad_pair_swap_in", _pair_swap([red_in]))
    big_out["w_in"] = _adamw_halves("adamw_w_in", [(adam_args["w_in"][0], red_in, oth_in) + adam_args["w_in"][1:]], 256)

    return loss, gx.reshape(x.shape), big_out, small_sum


_HBM = pl.BlockSpec(memory_space=pltpu.HBM)


D2D_CHUNKS = 8
ICI_CHUNKS = 1
ROW_ALIGN = 16


def _row_chunks(rows, n_chunks):
    size = min(max(rows // n_chunks, ROW_ALIGN), rows)
    assert rows % size == 0
    return [(start, size) for start in range(0, rows, size)]


def _position():
    x, y, c = lax.axis_index("x"), lax.axis_index("y"), lax.axis_index("c")
    chips = [(1 - x, y), (x, 1 - y), (1 - x, 1 - y)]
    return x, y, c, chips


def _allgather_exchange(arrs):
    n = len(arrs)

    def copies(ins, outs, send_sems, recv_sems, local_sems):
        x, y, c, chips = _position()
        me = 2 * x + y
        sibling = (x, y, 1 - c)

        def copy(a, k, src, dst, to):
            return pltpu.make_async_remote_copy(src_ref=src, dst_ref=dst, send_sem=send_sems.at[a, k],
                                                recv_sem=recv_sems.at[a, k], device_id=to, device_id_type=MESH)

        def half_rows(a, pc):
            half = ins[a].shape[0] // 2
            return pl.ds(pc * half, half)

        local = [pltpu.make_async_copy(ins[a], outs[a].at[me], local_sems.at[a]) for a in range(n)]
        ici_out = [[copy(a, k, ins[a].at[half_rows(a, c)], outs[a].at[me, half_rows(a, c)], (px, py, c))
                    for k, (px, py) in enumerate(chips)] for a in range(n)]
        return c, chips, sibling, copy, half_rows, local, ici_out

    def start(ins, outs, send_sems, recv_sems, local_sems):
        c, chips, _, copy, _, local, _ = copies(ins, outs, send_sems, recv_sems, local_sems)
        x, y, _, _ = _position()
        me = 2 * x + y
        for cp in local:
            cp.start()
        for a in range(n):
            half = ins[a].shape[0] // 2
            for k, (px, py) in enumerate(chips):
                for first, size in _row_chunks(half, ICI_CHUNKS):
                    rows = pl.ds(c * half + first, size)
                    copy(a, k, ins[a].at[rows], outs[a].at[me, rows], (px, py, c)).start()

    def finish(ins, outs, send_sems, recv_sems, local_sems):
        c, chips, sibling, copy, half_rows, local, ici_out = copies(ins, outs, send_sems, recv_sems, local_sems)
        for a in range(n):
            half = ins[a].shape[0] // 2
            for k, (px, py) in enumerate(chips):
                blk = outs[a].at[2 * px + py, half_rows(a, c)]
                copy(a, k, blk, blk, (px, py, c)).wait_recv()
                for first, size in _row_chunks(half, D2D_CHUNKS):
                    piece = outs[a].at[2 * px + py, pl.ds(c * half + first, size)]
                    copy(a, 3 + k, piece, piece, sibling).start()
        for a in range(n):
            for k, (px, py) in enumerate(chips):
                theirs = outs[a].at[2 * px + py, half_rows(a, 1 - c)]
                copy(a, 3 + k, theirs, theirs, sibling).wait_recv()
                mine = outs[a].at[2 * px + py, half_rows(a, c)]
                copy(a, 3 + k, mine, mine, sibling).wait_send()
        for a in range(n):
            for cp in ici_out[a]:
                cp.wait_send()
        for cp in local:
            cp.wait()

    return _Carried(arrs, [_sds((N_CHIPS,) + a.shape, a.dtype) for a in arrs],
                    [pltpu.SemaphoreType.DMA((n, 6)), pltpu.SemaphoreType.DMA((n, 6)), pltpu.SemaphoreType.DMA((n,))],
                    start, finish)


def _run_exchange(name, exchange):
    n_in, n_out = len(exchange.ins), len(exchange.out_shapes)

    def body(*refs):
        ins, outs, sems = refs[:n_in], refs[n_in:n_in + n_out], refs[n_in + n_out:]
        exchange.start(ins, outs, *sems)
        exchange.finish(ins, outs, *sems)

    return pl.pallas_call(
        body, name=name, out_shape=tuple(exchange.out_shapes), in_specs=[_HBM] * n_in,
        out_specs=tuple([_HBM] * n_out), scratch_shapes=exchange.sems,
    )(*exchange.ins)


def _pair_exchange(grads):
    n = len(grads)

    def copier(send_sems, recv_sems):
        x, y, c, _ = _position()

        def copy(a, src, dst):
            return pltpu.make_async_remote_copy(src_ref=src, dst_ref=dst, send_sem=send_sems.at[a],
                                                recv_sem=recv_sems.at[a], device_id=(x, y, 1 - c), device_id_type=MESH)
        return c, copy

    def start(ins, got, send_sems, recv_sems):
        c, copy = copier(send_sems, recv_sems)
        for a in range(n):
            half = ins[a].shape[1] // 2
            for slab in range(N_CHIPS):
                for first, size in _row_chunks(half, D2D_CHUNKS):
                    copy(a, ins[a].at[slab, pl.ds((1 - c) * half + first, size), :],
                         got[a].at[slab, pl.ds(first, size), :]).start()

    def finish(ins, got, send_sems, recv_sems):
        c, copy = copier(send_sems, recv_sems)
        for a in range(n):
            half = ins[a].shape[1] // 2
            copy(a, ins[a].at[:, pl.ds((1 - c) * half, half), :], got[a]).wait()

    return _Carried(grads, [_sds((N_CHIPS, g.shape[1] // 2, g.shape[2]), g.dtype) for g in grads],
                    [pltpu.SemaphoreType.DMA((n,)), pltpu.SemaphoreType.DMA((n,))], start, finish)


def _chip_exchange(hsums):
    n = len(hsums)

    def copies(ins, outs, send_sems, recv_sems, local_sems, pieces):
        x, y, c, chips = _position()
        me = 2 * x + y
        cps = []
        for a in range(n):
            cps.append(pltpu.make_async_copy(ins[a].at[me], outs[a].at[me], local_sems.at[a]))
            rows = ins[a].shape[1]
            for k, (px, py) in enumerate(chips):
                for first, size in (_row_chunks(rows, ICI_CHUNKS) if pieces else [(0, rows)]):
                    cps.append(pltpu.make_async_remote_copy(
                        src_ref=ins[a].at[2 * px + py, pl.ds(first, size)], dst_ref=outs[a].at[me, pl.ds(first, size)],
                        send_sem=send_sems.at[a, k], recv_sem=recv_sems.at[a, k], device_id=(px, py, c),
                        device_id_type=MESH))
        return cps

    def start(*refs):
        for cp in copies(*refs, pieces=True):
            cp.start()

    def finish(*refs):
        for cp in copies(*refs, pieces=False):
            cp.wait()

    return _Carried(hsums, [_sds(h.shape, h.dtype) for h in hsums],
                    [pltpu.SemaphoreType.DMA((n, 3)), pltpu.SemaphoreType.DMA((n, 3)), pltpu.SemaphoreType.DMA((n,))],
                    start, finish)


def _pair_swap(reds):
    n = len(reds)

    def copier(send_sems, recv_sems):
        x, y, c, _ = _position()

        def copy(a, src, dst):
            return pltpu.make_async_remote_copy(src_ref=src, dst_ref=dst, send_sem=send_sems.at[a],
                                                recv_sem=recv_sems.at[a], device_id=(x, y, 1 - c), device_id_type=MESH)
        return copy

    def start(ins, outs, send_sems, recv_sems):
        copy = copier(send_sems, recv_sems)
        for a in range(n):
            for first, size in _row_chunks(ins[a].shape[0], 2 * D2D_CHUNKS):
                copy(a, ins[a].at[pl.ds(first, size), :], outs[a].at[pl.ds(first, size), :]).start()

    def finish(ins, outs, send_sems, recv_sems):
        copy = copier(send_sems, recv_sems)
        for a in range(n):
            copy(a, ins[a], outs[a]).wait()

    return _Carried(reds, [_sds(r.shape, r.dtype) for r in reds],
                    [pltpu.SemaphoreType.DMA((n,)), pltpu.SemaphoreType.DMA((n,))], start, finish)


def _device_gather_exchange(packed):
    def copies(ins, outs, send_sems, recv_sems, local_sem):
        (x_ref,), (all_ref,) = ins, outs
        x, y, c, chips = _position()
        me, sibling = (x, y, c), (x, y, 1 - c)

        def slab(px, py, pc):
            return all_ref.at[4 * px + 2 * py + pc]

        def copy(k, block, to, src=None):
            return pltpu.make_async_remote_copy(
                src_ref=slab(*block) if src is None else src, dst_ref=slab(*block), send_sem=send_sems.at[k],
                recv_sem=recv_sems.at[k], device_id=to, device_id_type=MESH)

        mine = pltpu.make_async_copy(x_ref, slab(*me), local_sem)
        first = [copy(0, me, sibling, src=x_ref)]
        first += [copy(1 + j, me, (*chip, c), src=x_ref) for j, chip in enumerate(chips)]
        passed = [copy(4 + j, (*chip, c), sibling) for j, chip in enumerate(chips)]
        return c, chips, me, sibling, copy, mine, first, passed

    def start(ins, outs, send_sems, recv_sems, local_sem):
        _, _, _, _, _, mine, first, _ = copies(ins, outs, send_sems, recv_sems, local_sem)
        mine.start()
        for cp in first:
            cp.start()

    def finish(ins, outs, send_sems, recv_sems, local_sem):
        c, chips, me, sibling, copy, mine, first, passed = copies(ins, outs, send_sems, recv_sems, local_sem)
        for j, chip in enumerate(chips):
            copy(1 + j, (*chip, c), me).wait_recv()
            passed[j].start()
        copy(0, sibling, me).wait_recv()
        for j, chip in enumerate(chips):
            copy(4 + j, (*chip, 1 - c), me).wait_recv()
        for cp in first + passed:
            cp.wait_send()
        mine.wait()

    return _Carried([packed], [_sds((N_DEV,) + packed.shape, F32)],
                    [pltpu.SemaphoreType.DMA((7,)), pltpu.SemaphoreType.DMA((7,)), pltpu.SemaphoreType.DMA],
                    start, finish)


def _ordered_sum(name, slabs):
    _, m_per, n_cols = slabs.shape

    def body(s_ref, o_ref):
        acc = s_ref[0]
        for d in range(1, N_DEV):
            acc = acc + s_ref[d]
        o_ref[...] = acc

    vmem = pl.BlockSpec(memory_space=pltpu.VMEM)
    return pl.pallas_call(body, name=name, out_shape=_sds((m_per, n_cols), F32), in_specs=[vmem], out_specs=vmem)(slabs)


def _pair_sum(core, own, got, tm):
    _, half, cols = got.shape
    nb = half // tm

    def body(c_ref, a_ref, b_ref, o_ref):
        o_ref[...] = (a_ref[...].astype(F32) + b_ref[...].astype(F32)).astype(BF16)

    return pl.pallas_call(
        body, name="grad_pair_sum", out_shape=_sds(got.shape, BF16),
        grid_spec=pltpu.PrefetchScalarGridSpec(
            num_scalar_prefetch=1, grid=(N_CHIPS, nb),
            in_specs=[pl.BlockSpec((None, tm, cols), lambda s, i, c_ref: (s, c_ref[0] * nb + i, 0)),
                      pl.BlockSpec((None, tm, cols), lambda s, i, c_ref: (s, i, 0))],
            out_specs=pl.BlockSpec((None, tm, cols), lambda s, i, c_ref: (s, i, 0))),
        compiler_params=_cparams(2),
    )(core, own, got)


def _chip_sum(slabs, tm):
    _, half, cols = slabs.shape

    def body(s_ref, o_ref):
        acc = s_ref[0].astype(F32)
        for k in range(1, N_CHIPS):
            acc = acc + s_ref[k].astype(F32)
        o_ref[...] = acc

    return pl.pallas_call(
        body, name="grad_chip_sum", out_shape=_sds((half, cols), F32), grid=(half // tm,),
        in_specs=[pl.BlockSpec((N_CHIPS, tm, cols), lambda i: (0, i, 0))],
        out_specs=pl.BlockSpec((tm, cols), lambda i: (i, 0)), compiler_params=_cparams(1),
    )(slabs)


def _adam_math(w, g, m, v):
    m2 = ADAM_B1 * m + (1.0 - ADAM_B1) * g
    v2 = ADAM_B2 * v + (1.0 - ADAM_B2) * (g * g)
    m_hat = m2 / (1.0 - ADAM_B1 ** ADAM_STEP)
    v_hat = v2 / (1.0 - ADAM_B2 ** ADAM_STEP)
    delta = -ADAM_LR * (m_hat / (jnp.sqrt(v_hat) + ADAM_EPS) + ADAM_WD * w)
    return delta, m2, v2


def _adamw_halves(name, items, tm, carried=None):
    rows, cols = items[0][0].shape
    nb = rows // 2 // tm
    n = len(items)

    def body(*refs):
        mine = (pl.program_id(0) // nb) == lax.axis_index("c")
        for k in range(n):
            w_ref, own_ref, oth_ref, m_ref, v_ref = refs[5 * k:5 * k + 5]
            g_ref, d_ref, m2_ref, v2_ref = refs[5 * n + 4 * k:5 * n + 4 * k + 4]
            g = jnp.where(mine, own_ref[...], oth_ref[...])
            d, m2, v2 = _adam_math(w_ref[...], g, m_ref[...], v_ref[...])
            g_ref[...] = g
            d_ref[...] = d
            m2_ref[...] = m2
            v2_ref[...] = v2

    full = pl.BlockSpec((tm, cols), lambda i: (i, 0))
    half = pl.BlockSpec((tm, cols), lambda i: (i % nb, 0))
    return _call_carrying(
        body, carried, name=name, grid=(rows // tm,), in_specs=[full, half, half, full, full] * n,
        out_specs=[full] * (4 * n), out_shape=tuple([_sds((rows, cols), F32)] * (4 * n)), scratch_shapes=[],
        operands=[a for item in items for a in item])


def _adamw(name, w, g, m, v, tm):
    def body(w_ref, g_ref, m_ref, v_ref, gout_ref, d_ref, m2_ref, v2_ref):
        gv = g_ref[...]
        d, m2, v2 = _adam_math(w_ref[...], gv, m_ref[...], v_ref[...])
        gout_ref[...] = gv
        d_ref[...] = d
        m2_ref[...] = m2
        v2_ref[...] = v2

    return _rows_call(name, body, tm, [w, g, m, v], [], [_sds(w.shape, F32)] * 4)


_SMALL_NAMES = ("norm_mix_pre", "gm_ln_w", "gm_ln_b", "gm_w_s", "gm_b_s", "conv_w", "conv_b", "dt_bias", "a_log",
                "d_skip", "ssm_norm_w", "norm_mix_post", "norm_ffn_pre", "norm_ffn_post")
_PACK_COLS = 1024


def _pack(parts, names=_SMALL_NAMES, tail=None):
    pieces = [parts[n].reshape(-1) for n in names]
    flat = jnp.concatenate(pieces if tail is None else pieces + [tail])
    rows = -(-flat.shape[0] // (8 * _PACK_COLS)) * 8
    flat = jnp.pad(flat, (0, rows * _PACK_COLS - flat.shape[0]))
    return flat.reshape(rows, _PACK_COLS)


def _unpack(packed, shapes, names=_SMALL_NAMES):
    flat = packed.reshape(-1)
    out, off = {}, 0
    for n in names:
        size = 1
        for s in shapes[n]:
            size *= s
        out[n] = flat[off:off + size].reshape(shapes[n])
        off += size
    return out


def kernel(x, norm_mix_pre, w_in, gm_ln_w, gm_ln_b, gm_w_s, gm_b_s, conv_w, conv_b, dt_bias, a_log, d_skip, ssm_norm_w, w_out, norm_mix_post, norm_ffn_pre, w_up, w_down, norm_ffn_post, loss_target, m_norm_mix_pre, m_w_in, m_gm_ln_w, m_gm_ln_b, m_gm_w_s, m_gm_b_s, m_conv_w, m_conv_b, m_dt_bias, m_a_log, m_d_skip, m_ssm_norm_w, m_w_out, m_norm_mix_post, m_norm_ffn_pre, m_w_up, m_w_down, m_norm_ffn_post, v_norm_mix_pre, v_w_in, v_gm_ln_w, v_gm_ln_b, v_gm_w_s, v_gm_b_s, v_conv_w, v_conv_b, v_dt_bias, v_a_log, v_d_skip, v_ssm_norm_w, v_w_out, v_norm_mix_post, v_norm_ffn_pre, v_w_up, v_w_down, v_norm_ffn_post):
    params = dict(norm_mix_pre=norm_mix_pre, w_in=w_in, gm_ln_w=gm_ln_w, gm_ln_b=gm_ln_b, gm_w_s=gm_w_s, gm_b_s=gm_b_s,
                  conv_w=conv_w, conv_b=conv_b, dt_bias=dt_bias, a_log=a_log, d_skip=d_skip, ssm_norm_w=ssm_norm_w,
                  w_out=w_out, norm_mix_post=norm_mix_post, norm_ffn_pre=norm_ffn_pre, w_up=w_up, w_down=w_down,
                  norm_ffn_post=norm_ffn_post)
    mom1 = dict(norm_mix_pre=m_norm_mix_pre, w_in=m_w_in, gm_ln_w=m_gm_ln_w, gm_ln_b=m_gm_ln_b, gm_w_s=m_gm_w_s,
                gm_b_s=m_gm_b_s, conv_w=m_conv_w, conv_b=m_conv_b, dt_bias=m_dt_bias, a_log=m_a_log, d_skip=m_d_skip,
                ssm_norm_w=m_ssm_norm_w, w_out=m_w_out, norm_mix_post=m_norm_mix_post, norm_ffn_pre=m_norm_ffn_pre,
                w_up=m_w_up, w_down=m_w_down, norm_ffn_post=m_norm_ffn_post)
    mom2 = dict(norm_mix_pre=v_norm_mix_pre, w_in=v_w_in, gm_ln_w=v_gm_ln_w, gm_ln_b=v_gm_ln_b, gm_w_s=v_gm_w_s,
                gm_b_s=v_gm_b_s, conv_w=v_conv_w, conv_b=v_conv_b, dt_bias=v_dt_bias, a_log=v_a_log, d_skip=v_d_skip,
                ssm_norm_w=v_ssm_norm_w, w_out=v_w_out, norm_mix_post=v_norm_mix_post, norm_ffn_pre=v_norm_ffn_pre,
                w_up=v_w_up, w_down=v_w_down, norm_ffn_post=v_norm_ffn_post)
    names = list(params)
    big = ("w_in", "w_out", "w_up", "w_down")
    chip = 2 * lax.axis_index("x") + lax.axis_index("y")

    shards = {n: params[n][0].astype(BF16) for n in big}
    conv_shard = jnp.pad(conv_w[0], ((0, 16 - CONV_K), (0, 0)))
    g_in4, g_conv4 = _run_exchange("allgather_w_in", _allgather_exchange([shards["w_in"], conv_shard]))
    conv_full = jnp.transpose(g_conv4[:, :CONV_K, :], (1, 0, 2)).reshape(CONV_K, CONV_CH)

    small = {n: params[n][0] if params[n].ndim >= 3 else params[n] for n in _SMALL_NAMES if n != "conv_w"}
    core = lax.axis_index("c").astype(jnp.int32).reshape(1)
    adam_args = {n: (params[n][0], mom1[n][0], mom2[n][0]) for n in big}
    loss, grad_x, big_out, small_sum = _forward_backward(
        x, loss_target, g_in4, conv_full, small, shards["w_out"], shards["w_up"], shards["w_down"], core, adam_args)
    grads, delta, new_m, new_v = {}, {}, {}, {}
    for n in big:
        grads[n], delta[n], new_m[n], new_v[n] = [a[None] for a in big_out[n]]

    small_sum["conv_w"] = lax.dynamic_slice_in_dim(small_sum["conv_w"], chip * (CONV_CH // N_CHIPS), CONV_CH // N_CHIPS, axis=1)

    local_shapes = {n: params[n].shape[1:] if params[n].ndim >= 3 else params[n].shape for n in _SMALL_NAMES}
    flat = lambda tree: {n: tree[n].reshape(local_shapes[n]) for n in _SMALL_NAMES}
    packed = [_pack(flat(t)) for t in (params, small_sum, mom1, mom2)]
    _, d_p, m_p, v_p = _adamw("adamw_small", *packed, packed[0].shape[0])
    for src, dst in ((d_p, delta), (m_p, new_m), (v_p, new_v)):
        for n, val in _unpack(src, local_shapes).items():
            dst[n] = val.reshape(params[n].shape)
    for n in _SMALL_NAMES:
        grads[n] = small_sum[n].reshape(params[n].shape)

    out = [loss, grad_x]
    for tree in (grads, delta, new_m, new_v):
        out += [tree[n] for n in names]
    return tuple(out)
```

```python
import functools

import jax
import jax.numpy as jnp
from jax import lax
from jax.experimental import pallas as pl
from jax.experimental.pallas import tpu as pltpu

F32 = jnp.float32
BF16 = jnp.bfloat16
HI = lax.Precision.HIGHEST
MESH = pl.DeviceIdType.MESH

EPS = 1e-6
D_MODEL = 1024
GM_WIDTH = 512
SSM_WIDTH = 512
N_HEADS = 8
HEAD_DIM = 64
CHUNK = 128
SSM_GROUPS = 2
GROUP_W = SSM_WIDTH // SSM_GROUPS
SSM_STATE = 128
CONV_K = 4
CONV_CH = 1024
D_FF = 4096
IN_COLS = 2568
DT_PAD = 128
N_CHIPS = 4
N_DEV = 8

ADAM_LR = 0.001
ADAM_B1 = 0.9
ADAM_B2 = 0.999
ADAM_EPS = 1e-08
ADAM_WD = 0.01
ADAM_STEP = 10

VMEM_LIMIT_BYTES = 56 * 1024 * 1024
FF_TILE = 512
DW_TOKENS_PER_STEP = 2048


def _cparams(n_axes):
    return pltpu.CompilerParams(dimension_semantics=("arbitrary",) * n_axes, vmem_limit_bytes=VMEM_LIMIT_BYTES)


def _dot(a, b):
    return jnp.dot(a.astype(BF16), b.astype(BF16), preferred_element_type=F32)


def _dot_nt(a, b):
    return lax.dot_general(a.astype(BF16), b.astype(BF16), (((1,), (1,)), ((), ())), preferred_element_type=F32)


def _dot_tn(a, b):
    return lax.dot_general(a.astype(BF16), b.astype(BF16), (((0,), (0,)), ((), ())), preferred_element_type=F32)


def _sigmoid(x):
    return 1.0 / (1.0 + jnp.exp(-x))


_GELU_C = 0.7978845608028654
_GELU_A = 0.044715


def _gelu(x):
    t = jnp.tanh(_GELU_C * (x + _GELU_A * (x * x * x)))
    return 0.5 * x * (1.0 + t), t


def _gelu_grad(x, t):
    return 0.5 * (1.0 + t) + 0.5 * x * (1.0 - t * t) * (_GELU_C * (1.0 + 3.0 * _GELU_A * x * x))


def _rms_fwd(x, w):
    r = lax.rsqrt(jnp.mean(x * x, axis=-1, keepdims=True) + EPS)
    return x * r * w, r


def _rms_bwd(x, r, w, dy):
    g = dy * w
    dx = r * g - x * (r * r * r) * jnp.mean(g * x, axis=-1, keepdims=True)
    dw = jnp.sum(dy * x * r, axis=0, keepdims=True)
    return dx, dw


class _Carried:
    def __init__(self, ins, out_shapes, sems, start, finish):
        self.ins, self.out_shapes, self.sems = list(ins), list(out_shapes), list(sems)
        self.start, self.finish = start, finish


def _both(first, second):
    n_i, n_o, n_s = len(first.ins), len(first.out_shapes), len(first.sems)

    def split(ins, outs, sems):
        return (ins[:n_i], outs[:n_o], sems[:n_s]), (ins[n_i:], outs[n_o:], sems[n_s:])

    def start(ins, outs, *sems):
        (i1, o1, s1), (i2, o2, s2) = split(ins, outs, sems)
        first.start(i1, o1, *s1)
        second.start(i2, o2, *s2)

    def finish(ins, outs, *sems):
        (i1, o1, s1), (i2, o2, s2) = split(ins, outs, sems)
        first.finish(i1, o1, *s1)
        second.finish(i2, o2, *s2)

    return _Carried(first.ins + second.ins, first.out_shapes + second.out_shapes, first.sems + second.sems, start, finish)


def _split_carried(refs, n_in, n_out, n_scratch, carried):
    n_ci, n_co, n_cs = len(carried.ins), len(carried.out_shapes), len(carried.sems)
    ins, rest = refs[:n_in], refs[n_in:]
    c_ins, rest = rest[:n_ci], rest[n_ci:]
    outs, rest = rest[:n_out], rest[n_out:]
    c_outs, rest = rest[:n_co], rest[n_co:]
    scr, c_sems = rest[:n_scratch], rest[n_scratch:]
    assert len(c_sems) == n_cs
    return tuple(ins) + tuple(outs) + tuple(scr), c_ins, c_outs, c_sems


def _rows_call(name, body, tm, row_ins, const_ins, row_outs, acc_outs=(), scratch=(), carried=None):
    n_rows = row_ins[0].shape[0]
    assert n_rows % tm == 0
    n_steps = n_rows // tm
    n_in = len(row_ins) + len(const_ins)
    n_ro = len(row_outs)
    n_acc = len(acc_outs)

    def kern(*refs):
        accs = refs[n_in + n_ro:n_in + n_ro + n_acc]

        @pl.when(pl.program_id(0) == 0)
        def _():
            for a in accs:
                a[...] = jnp.zeros_like(a)

        body(*refs)

    def whole(shape):
        nd = len(shape)
        return pl.BlockSpec(tuple(shape), lambda i: (0,) * nd)

    in_specs = [pl.BlockSpec((tm, a.shape[1]), lambda i: (i, 0)) for a in row_ins]
    in_specs += [whole(a.shape) for a in const_ins]
    out_specs = [pl.BlockSpec((tm, s.shape[1]), lambda i: (i, 0)) for s in row_outs]
    out_specs += [whole(s.shape) for s in acc_outs]
    return _call_carrying(
        kern, carried, name=name, grid=(n_steps,), in_specs=in_specs, out_specs=out_specs,
        out_shape=tuple(row_outs) + tuple(acc_outs), scratch_shapes=list(scratch), operands=list(row_ins) + list(const_ins))


def _call_carrying(body, carried, *, name, grid, in_specs, out_specs, out_shape, scratch_shapes, operands):
    n_in, n_out, n_scratch = len(in_specs), len(out_specs), len(scratch_shapes)
    kern = body
    if carried is not None:
        def kern(*refs):
            plain, c_ins, c_outs, c_sems = _split_carried(refs, n_in, n_out, n_scratch, carried)
            first, last = True, True
            for d, size in enumerate(grid):
                first = jnp.logical_and(first, pl.program_id(d) == 0)
                last = jnp.logical_and(last, pl.program_id(d) == size - 1)

            @pl.when(first)
            def _():
                carried.start(c_ins, c_outs, *c_sems)

            body(*plain)

            @pl.when(last)
            def _():
                carried.finish(c_ins, c_outs, *c_sems)

        in_specs = list(in_specs) + [_HBM] * len(carried.ins)
        out_specs = list(out_specs) + [_HBM] * len(carried.out_shapes)
        out_shape = tuple(out_shape) + tuple(carried.out_shapes)
        operands = list(operands) + carried.ins
        scratch_shapes = list(scratch_shapes) + carried.sems
    return pl.pallas_call(
        kern, name=name, grid=grid, in_specs=in_specs, out_specs=out_specs, out_shape=out_shape,
        scratch_shapes=scratch_shapes, compiler_params=_cparams(len(grid)),
    )(*operands)


def _sds(shape, dtype):
    return jax.ShapeDtypeStruct(tuple(shape), dtype)


def _matmul_tn(name, a, b, tm, tn, tk, stacked=False, carried=None):
    k_dim, m_dim = a.shape
    n_dim = b.shape[1]
    assert m_dim % tm == 0 and n_dim % tn == 0 and k_dim % tk == 0
    nk = k_dim // tk

    def kern(a_ref, b_ref, o_ref, acc_ref):
        k = pl.program_id(2)
        prod = _dot_tn(a_ref[...], b_ref[...])

        @pl.when(k == 0)
        def _():
            acc_ref[...] = prod

        @pl.when(k > 0)
        def _():
            acc_ref[...] += prod

        @pl.when(k == nk - 1)
        def _():
            o_ref[...] = acc_ref[...].astype(o_ref.dtype)

    if stacked:
        assert tm == m_dim
        out_shape = _sds((n_dim // tn, m_dim, tn), BF16)
        out_spec = pl.BlockSpec((None, tm, tn), lambda i, j, k: (j, i, 0))
    else:
        out_shape = _sds((m_dim, n_dim), BF16)
        out_spec = pl.BlockSpec((tm, tn), lambda i, j, k: (i, j))
    outs = _call_carrying(
        kern, carried, name=name, grid=(m_dim // tm, n_dim // tn, nk),
        in_specs=[pl.BlockSpec((tk, tm), lambda i, j, k: (k, i)), pl.BlockSpec((tk, tn), lambda i, j, k: (k, j))],
        out_specs=[out_spec], out_shape=(out_shape,), scratch_shapes=[pltpu.VMEM((tm, tn), F32)], operands=[a, b])
    return outs[0] if carried is None else outs


def _inproj_fwd(x, nw, w_uv, w_xbc, w_z, w_dt, tm=256, carried=None):
    n_tok = x.shape[0]

    def body(x_ref, nw_ref, wuv_ref, wxbc_ref, wz_ref, wdt_ref, puv_ref, pxbc_ref, pz_ref, pdt_ref):
        h, _ = _rms_fwd(x_ref[...], nw_ref[...])
        h = h.astype(BF16)
        puv_ref[...] = jnp.dot(h, wuv_ref[...], preferred_element_type=F32)
        pxbc_ref[...] = jnp.dot(h, wxbc_ref[...], preferred_element_type=F32)
        pz_ref[...] = jnp.dot(h, wz_ref[...], preferred_element_type=F32)
        pdt_ref[...] = jnp.dot(h, wdt_ref[...], preferred_element_type=F32)

    return _rows_call(
        "inproj_fwd", body, tm, [x], [nw, w_uv, w_xbc, w_z, w_dt],
        [_sds((n_tok, 2 * GM_WIDTH), F32), _sds((n_tok, CONV_CH), F32), _sds((n_tok, SSM_WIDTH), F32),
         _sds((n_tok, DT_PAD), F32)], carried=carried)


def _head_lane_mask(width, head):
    lane = lax.broadcasted_iota(jnp.int32, (1, width), 1)
    return (lane // HEAD_DIM) == head


def _split_terms(x, terms):
    parts = []
    for _ in range(terms):
        p = x.astype(BF16)
        parts.append(p)
        x = x - p.astype(F32)
    return parts


def _seg_dots(vals, ind, terms=2):
    m = vals[0].shape[0]
    parts = []
    for v in vals:
        parts += _split_terms(v, terms)
    red = jnp.dot(jnp.concatenate(parts, axis=0), ind, preferred_element_type=F32)
    outs = []
    for i in range(len(vals)):
        acc = red[i * terms * m:(i * terms + 1) * m]
        for t in range(1, terms):
            acc = acc + red[(i * terms + t) * m:(i * terms + t + 1) * m]
        outs.append(acc)
    return outs


def _tri_dot(mask, x, terms=3):
    n = x.shape[1]
    red = jnp.dot(mask.astype(BF16), jnp.concatenate(_split_terms(x, terms), axis=1), preferred_element_type=F32)
    acc = red[:, :n]
    for t in range(1, terms):
        acc = acc + red[:, t * n:(t + 1) * n]
    return acc


def _gmlp_common(puv, lnw, lnb, e_bf, et_bf):
    u = puv[:, :GM_WIDTH]
    v = puv[:, GM_WIDTH:]
    gu, tu = _gelu(u)
    gv, tv = _gelu(v)
    (s1,) = _seg_dots([gv], et_bf)
    (mu,) = _seg_dots([s1 * (1.0 / HEAD_DIM)], e_bf)
    xc = gv - mu
    (s2,) = _seg_dots([xc * xc], et_bf)
    (rstd,) = _seg_dots([lax.rsqrt(s2 * (1.0 / HEAD_DIM) + EPS)], e_bf)
    xhat = xc * rstd
    vn = xhat * lnw + lnb
    return u, v, gu, tu, tv, rstd, xhat, vn


def _tril_mask():
    r = lax.broadcasted_iota(jnp.int32, (CHUNK, CHUNK), 0)
    c = lax.broadcasted_iota(jnp.int32, (CHUNK, CHUNK), 1)
    return r >= c


def _head_blocks(v):
    return jnp.concatenate([jnp.where(_head_lane_mask(GM_WIDTH, h), v, jnp.zeros_like(v)) for h in range(N_HEADS)], axis=0)


def _causal_w_cat(w_cat):
    t = lax.broadcasted_iota(jnp.int32, (CHUNK, N_HEADS * CHUNK), 0)
    s = lax.broadcasted_iota(jnp.int32, (CHUNK, N_HEADS * CHUNK), 1) % CHUNK
    return jnp.where(t >= s, w_cat, 0.0).astype(BF16)


def _gmlp_chunk_fwd(puv, lnw, lnb, e_bf, et_bf, wm, bmap):
    _, _, gu, _, _, _, _, vn = _gmlp_common(puv, lnw, lnb, e_bf, et_bf)
    mixed = jnp.dot(wm, _head_blocks(vn.astype(BF16)), preferred_element_type=F32) + bmap
    return (gu * mixed).astype(BF16)


SUBLANES = 8


def _shift_down(x, tail, s):
    main = pltpu.roll(x, s, 0)
    row = lax.broadcasted_iota(jnp.int32, (SUBLANES, 1), 0)
    head = jnp.where(row < s, pltpu.roll(tail, s, 0), main[:SUBLANES])
    return jnp.concatenate([head, main[SUBLANES:]], axis=0)


def _shift_up(x, head_next, s):
    n = x.shape[0]
    main = pltpu.roll(x, n - s, 0)
    row = lax.broadcasted_iota(jnp.int32, (SUBLANES, 1), 0)
    last = jnp.where(row >= SUBLANES - s, pltpu.roll(head_next, SUBLANES - s, 0), main[n - SUBLANES:])
    return jnp.concatenate([main[:n - SUBLANES], last], axis=0)


def _ssd_pre(xr, tail, cw_ref, cb, pdt, dtb, alog, emap):
    rowi = lax.broadcasted_iota(jnp.int32, (CHUNK, 1), 0)
    shifted = [_shift_down(xr, tail, 3), _shift_down(xr, tail, 2), _shift_down(xr, tail, 1), xr]
    xc = cb
    for k in range(CONV_K):
        xc = xc + cw_ref[k] * shifted[k]
    sg = _sigmoid(xc)
    xa = xc * sg
    pre = pdt + dtb
    dt = jnp.maximum(pre, 0.0) + jnp.log(1.0 + jnp.exp(-jnp.abs(pre)))
    a_neg = -jnp.exp(alog)
    a_cs = _tri_dot(_tril_mask(), dt * a_neg)
    acs_map, dt_map = _seg_dots([a_cs, dt], emap, terms=3)
    return dict(shifted=shifted, xc=xc, sg=sg, xa=xa, pre=pre, dt=dt, a_neg=a_neg, a_cs=a_cs,
                acs_map=acs_map, dt_map=dt_map, rowi=rowi)


def _ssd_maps(p):
    last = p["rowi"] == CHUNK - 1
    aq_map = jnp.sum(jnp.where(last, p["acs_map"], 0.0), axis=0, keepdims=True)
    e_exp = jnp.exp(p["acs_map"])
    dte = jnp.exp(aq_map - p["acs_map"])
    cd = jnp.exp(aq_map)
    return last, e_exp, dte, cd


def _head_decay(a_cs, a_cs_t, head, tri):
    lane = lax.broadcasted_iota(jnp.int32, (1, DT_PAD), 1)
    sub = lax.broadcasted_iota(jnp.int32, (DT_PAD, 1), 0)
    col = jnp.sum(jnp.where(lane == head, a_cs, 0.0), axis=1, keepdims=True)
    row = jnp.sum(jnp.where(sub == head, a_cs_t, 0.0), axis=0, keepdims=True)
    return jnp.exp(jnp.where(tri, col - row, -1e30))


def _gate_fwd(y, z, nw):
    sz = _sigmoid(z)
    zg = z * sz
    yg = y * zg
    outs, rs = [], []
    for g in range(SSM_GROUPS):
        gs = slice(g * GROUP_W, (g + 1) * GROUP_W)
        o, r = _rms_fwd(yg[:, gs], nw[:, gs])
        outs.append(o)
        rs.append(r)
    return sz, zg, yg, outs, rs


def _ssd_const_specs():
    def whole(shape):
        nd = len(shape)
        return pl.BlockSpec(tuple(shape), lambda c: (0,) * nd)
    return [whole((CONV_K, 1, CONV_CH)), whole((1, CONV_CH)), whole((1, DT_PAD)), whole((1, DT_PAD)),
            whole((1, SSM_WIDTH)), whole((1, SSM_WIDTH)), whole((DT_PAD, SSM_WIDTH)), whole((SSM_WIDTH, DT_PAD))]


def _mixer_fwd(p_uv, p_xbc, p_z, p_dt, lnw, lnb, w_cat, bmap, conv_w, conv_b, dt_bias, a_log, dskip_map, norm_w,
               e_bf, et_bf, n_seq, carried=None):
    n_tok = p_xbc.shape[0]
    nc = n_tok // n_seq // CHUNK

    def body(puv3, xr3, z3, pdt3, lnw_ref, lnb_ref, wcat_ref, bmap_ref,
             cw_ref, cb_ref, dtb_ref, alog_ref, dsk_ref, nw_ref, e_ref, et_ref,
             ya3, yb3, yssd3, sprev3, wm_scr, prev3_scr, s3_scr):
        @pl.when(pl.program_id(0) == 0)
        def _():
            wm_scr[...] = _causal_w_cat(wcat_ref[...])
            prev3_scr[...] = jnp.zeros_like(prev3_scr)
            s3_scr[...] = jnp.zeros_like(s3_scr)

        for b in range(n_seq):
            one_sequence(puv3.at[b], xr3.at[b], z3.at[b], pdt3.at[b], lnw_ref, lnb_ref, bmap_ref,
                         cw_ref, cb_ref, dtb_ref, alog_ref, dsk_ref, nw_ref, e_ref, et_ref,
                         ya3.at[b], yb3.at[b], yssd3.at[b], sprev3.at[b], wm_scr, prev3_scr.at[b], s3_scr.at[b])

    def one_sequence(puv_ref, xr_ref, z_ref, pdt_ref, lnw_ref, lnb_ref, bmap_ref,
                     cw_ref, cb_ref, dtb_ref, alog_ref, dsk_ref, nw_ref, e_ref, et_ref,
                     ya_ref, yb_ref, yssd_ref, sprev_ref, wm_scr, prev_scr, s_scr):
        ya_ref[...] = _gmlp_chunk_fwd(puv_ref[...], lnw_ref[...], lnb_ref[...], e_ref[...], et_ref[...], wm_scr[...],
                                      bmap_ref[...])
        xr = xr_ref[...]
        p = _ssd_pre(xr, prev_scr[...], cw_ref, cb_ref[...], pdt_ref[...], dtb_ref[...], alog_ref[...], e_ref[...])
        _, e_exp, dte, cd = _ssd_maps(p)
        xs = p["xa"][:, :SSM_WIDTH]
        xd = xs * p["dt_map"]
        a_cs_t = p["a_cs"].T
        tri = _tril_mask()
        s_old = s_scr[...]
        sprev_ref[...] = s_old
        for g in range(SSM_GROUPS):
            gs = slice(g * GROUP_W, (g + 1) * GROUP_W)
            bm = p["xa"][:, SSM_WIDTH + g * SSM_STATE: SSM_WIDTH + (g + 1) * SSM_STATE].astype(BF16)
            cm = p["xa"][:, SSM_WIDTH + (SSM_GROUPS + g) * SSM_STATE: SSM_WIDTH + (SSM_GROUPS + g + 1) * SSM_STATE].astype(BF16)
            cb_mat = _dot_nt(cm, bm)
            xdg = xd[:, gs].astype(BF16)
            y_g = _dot(cm, s_old[:, gs]) * e_exp[:, gs] + dsk_ref[:, gs] * xs[:, gs]
            for r in range(SSM_GROUPS * 2):
                dm = _head_decay(p["a_cs"], a_cs_t, g * 4 + r, tri)
                full = jnp.dot((cb_mat * dm).astype(BF16), xdg, preferred_element_type=F32)
                y_g = y_g + jnp.where(_head_lane_mask(GROUP_W, r), full, 0.0)
            yssd_ref[:, gs] = y_g
            s_scr[:, gs] = cd[:, gs] * s_old[:, gs] + _dot_tn(bm, xd[:, gs] * dte[:, gs])
        _, _, _, outs, _ = _gate_fwd(yssd_ref[...], z_ref[...], nw_ref[...])
        for g in range(SSM_GROUPS):
            yb_ref[:, g * GROUP_W:(g + 1) * GROUP_W] = outs[g].astype(BF16)
        prev_scr[...] = xr[CHUNK - SUBLANES:, :]

    seq_len = n_tok // n_seq

    def rows(width):
        return pl.BlockSpec((n_seq, CHUNK, width), lambda c: (0, c, 0))

    def whole(shape):
        nd = len(shape)
        return pl.BlockSpec(tuple(shape), lambda c: (0,) * nd)

    def by_seq(a):
        return a.reshape(n_seq, seq_len, a.shape[-1])

    outs = _call_carrying(
        body, carried, name="mixer_fwd", grid=(nc,),
        in_specs=[rows(2 * GM_WIDTH), rows(CONV_CH), rows(SSM_WIDTH), rows(DT_PAD), whole(lnw.shape), whole(lnb.shape),
                  whole(w_cat.shape), whole(bmap.shape)] + _ssd_const_specs(),
        out_specs=[rows(GM_WIDTH), rows(SSM_WIDTH), rows(SSM_WIDTH), rows(SSM_WIDTH)],
        out_shape=(_sds((n_seq, seq_len, GM_WIDTH), BF16), _sds((n_seq, seq_len, SSM_WIDTH), BF16),
                   _sds((n_seq, seq_len, SSM_WIDTH), F32), _sds((n_seq, seq_len, SSM_WIDTH), F32)),
        scratch_shapes=[pltpu.VMEM((CHUNK, N_HEADS * CHUNK), BF16), pltpu.VMEM((n_seq, SUBLANES, CONV_CH), F32),
                        pltpu.VMEM((n_seq, SSM_STATE, SSM_WIDTH), F32)],
        operands=[by_seq(p_uv), by_seq(p_xbc), by_seq(p_z), by_seq(p_dt), lnw, lnb, w_cat, bmap, conv_w, conv_b, dt_bias,
                  a_log, dskip_map, norm_w, e_bf, et_bf])
    return tuple(o.reshape(n_tok, o.shape[-1]) for o in outs[:4]) + tuple(outs[4:])


def _outproj_fwd(ya, yb, x, w_out, nw_post, nw_pre2, tm=256, carried=None):
    n_tok = x.shape[0]

    def body(ya_ref, yb_ref, x_ref, wo_ref, nwa_ref, nwb_ref, o_ref, x1_ref, h2_ref):
        o = jnp.dot(ya_ref[...], wo_ref[:GM_WIDTH, :], preferred_element_type=F32)
        o = o + jnp.dot(yb_ref[...], wo_ref[GM_WIDTH:, :], preferred_element_type=F32)
        on, _ = _rms_fwd(o, nwa_ref[...])
        x1 = x_ref[...] + on
        h2, _ = _rms_fwd(x1, nwb_ref[...])
        o_ref[...] = o
        x1_ref[...] = x1
        h2_ref[...] = h2.astype(BF16)

    return _rows_call("outproj_fwd", body, tm, [ya, yb, x], [w_out, nw_post, nw_pre2],
                      [_sds((n_tok, D_MODEL), F32), _sds((n_tok, D_MODEL), F32), _sds((n_tok, D_MODEL), BF16)],
                      carried=carried)


def _up_cols(wup_ref, j):
    per = (D_FF // N_CHIPS) // FF_TILE
    return wup_ref[j // per, :, (j % per) * FF_TILE:(j % per + 1) * FF_TILE]


def _down_rows(wda_ref, wdb_ref, j):
    assert 2 * FF_TILE == D_FF // N_CHIPS
    return (wda_ref if j % 2 == 0 else wdb_ref)[j // 2]


def _skewed_rows_call(name, main, tail, tm, lead_ins, lag_ins, const_ins, lead_outs, lag_outs, acc_outs, carry,
                      streamed, tile_copies, n_copies):
    n_rows = lead_ins[0].shape[0]
    assert n_rows % tm == 0
    n = n_rows // tm
    counts = [len(lead_ins), len(lag_ins), len(const_ins), len(streamed), len(lead_outs), len(lag_outs), len(acc_outs),
              1, len(streamed)]

    def kern(*refs):
        groups, pos = [], 0
        for cnt in counts:
            groups.append(refs[pos:pos + cnt])
            pos += cnt
        lead_i, lag_i, consts, w_hbm, lead_o, lag_o, accs, (carry_scr,), w_vmem = groups
        sems = refs[pos]
        i = pl.program_id(0)
        pieces, k = [], 0
        for piece in tile_copies(w_hbm, w_vmem):
            pieces.append([pltpu.make_async_copy(src, dst, sems.at[k + q]) for q, (src, dst) in enumerate(piece)])
            k += len(piece)

        def ready(j):
            for cp in pieces[j]:
                cp.wait()

        @pl.when(i == 0)
        def _():
            for piece in pieces:
                for cp in piece:
                    cp.start()
            for a in accs:
                a[...] = jnp.zeros_like(a)
            carry_scr[...] = main(lead_i, consts, lead_o, w_vmem, ready)

        @pl.when(jnp.logical_and(i > 0, i < n))
        def _():
            previous = carry_scr[...]
            carry_scr[...] = main(lead_i, consts, lead_o, w_vmem, lambda j: None)
            tail(previous, lag_i, consts, lag_o, accs)

        @pl.when(i == n)
        def _():
            tail(carry_scr[...], lag_i, consts, lag_o, accs)

    def lead(width):
        return pl.BlockSpec((tm, width), lambda i: (jnp.minimum(i, n - 1), 0))

    def lag(width):
        return pl.BlockSpec((tm, width), lambda i: (jnp.maximum(i - 1, 0), 0))

    def whole(shape):
        nd = len(shape)
        return pl.BlockSpec(tuple(shape), lambda i: (0,) * nd)

    return pl.pallas_call(
        kern, name=name, grid=(n + 1,),
        in_specs=([lead(a.shape[1]) for a in lead_ins] + [lag(a.shape[1]) for a in lag_ins]
                  + [whole(a.shape) for a in const_ins] + [_HBM] * len(streamed)),
        out_specs=[lead(s.shape[1]) for s in lead_outs] + [lag(s.shape[1]) for s in lag_outs] + [whole(s.shape) for s in acc_outs],
        out_shape=tuple(lead_outs) + tuple(lag_outs) + tuple(acc_outs),
        scratch_shapes=([pltpu.VMEM(carry, F32)] + [pltpu.VMEM(a.shape, a.dtype) for a in streamed]
                        + [pltpu.SemaphoreType.DMA((n_copies,))]),
        compiler_params=_cparams(1),
    )(*lead_ins, *lag_ins, *const_ins, *streamed)


def _mlp_weight_pieces(order):
    per = (D_FF // N_CHIPS) // FF_TILE

    def tile_copies(hbm, vmem):
        pieces = []
        for j in range(D_FF // FF_TILE):
            cols = (j // per, slice(None), pl.ds((j % per) * FF_TILE, FF_TILE))
            up = (hbm[0].at[cols], vmem[0].at[cols])
            down = (hbm[1 + j % 2].at[j // 2], vmem[1 + j % 2].at[j // 2])
            pieces.append([up, down] if order == "up_down" else [down, up])
        return pieces

    return tile_copies


def _mlp_fwd(h2, x1, tgt, w_up, w_down_a, w_down_b, nw, tm=512):
    n_tok = x1.shape[0]

    def main(lead_i, consts, lead_o, weights, ready):
        (h2_ref,), (f_ref,), (wup_ref, wda_ref, wdb_ref) = lead_i, lead_o, weights
        h2v = h2_ref[...]
        acc = jnp.zeros((tm, D_MODEL), F32)
        for j in range(D_FF // FF_TILE):
            cs = slice(j * FF_TILE, (j + 1) * FF_TILE)
            ready(j)
            u = jnp.dot(h2v, _up_cols(wup_ref, j), preferred_element_type=F32)
            f = jnp.square(jnp.maximum(u, 0.0)).astype(BF16)
            f_ref[:, cs] = f
            acc = acc + jnp.dot(f, _down_rows(wda_ref, wdb_ref, j), preferred_element_type=F32)
        return acc

    def tail(acc, lag_i, consts, lag_o, accs):
        (x1_ref, tgt_ref), (nw_ref,), (dd_ref, dy_ref), (loss_ref, dnw_ref) = lag_i, consts, lag_o, accs
        dn, r = _rms_fwd(acc, nw_ref[...])
        e = x1_ref[...] + dn - tgt_ref[...]
        loss_ref[...] += jnp.full(loss_ref.shape, (0.5 / D_MODEL) * jnp.sum(e * e), F32)
        dy = e * (1.0 / D_MODEL)
        dd, dnw = _rms_bwd(acc, r, nw_ref[...], dy)
        dy_ref[...] = dy
        dd_ref[...] = dd.astype(BF16)
        dnw_ref[...] += dnw

    return _skewed_rows_call(
        "mlp_fwd", main, tail, tm, [h2], [x1, tgt], [nw],
        [_sds((n_tok, D_FF), BF16)], [_sds((n_tok, D_MODEL), BF16), _sds((n_tok, D_MODEL), F32)],
        [_sds((8, 128), F32), _sds((1, D_MODEL), F32)], carry=(tm, D_MODEL),
        streamed=[w_up, w_down_a, w_down_b], tile_copies=_mlp_weight_pieces("up_down"), n_copies=2 * (D_FF // FF_TILE))


def _mlp_bwd(dd, f, x1, dy, w_down_a, w_down_b, w_up, nw, tm=256):
    n_tok = x1.shape[0]

    def main(lead_i, consts, lead_o, weights, ready):
        (dd_ref, f_ref), (dup_ref,), (wup_ref, wda_ref, wdb_ref) = lead_i, lead_o, weights
        ddv = dd_ref[...]
        acc = jnp.zeros((tm, D_MODEL), F32)
        for j in range(D_FF // FF_TILE):
            cs = slice(j * FF_TILE, (j + 1) * FF_TILE)
            ready(j)
            df = _dot_nt(ddv, _down_rows(wda_ref, wdb_ref, j))
            du = (df * (2.0 * jnp.sqrt(f_ref[:, cs].astype(F32)))).astype(BF16)
            dup_ref[:, cs] = du
            acc = acc + _dot_nt(du, _up_cols(wup_ref, j))
        return acc

    def tail(acc, lag_i, consts, lag_o, accs):
        (x1_ref, dy_ref), (nw_ref,), (dx1_ref,), (dnw_ref,) = lag_i, consts, lag_o, accs
        x1v = x1_ref[...]
        _, r = _rms_fwd(x1v, nw_ref[...])
        dx, dnw = _rms_bwd(x1v, r, nw_ref[...], acc)
        dx1_ref[...] = dy_ref[...] + dx
        dnw_ref[...] += dnw

    return _skewed_rows_call(
        "mlp_bwd", main, tail, tm, [dd, f], [x1, dy], [nw],
        [_sds((n_tok, D_FF), BF16)], [_sds((n_tok, D_MODEL), F32)], [_sds((1, D_MODEL), F32)], carry=(tm, D_MODEL),
        streamed=[w_up, w_down_a, w_down_b], tile_copies=_mlp_weight_pieces("down_up"), n_copies=2 * (D_FF // FF_TILE))


def _outproj_bwd(dx1, o, w_out, nw, tm=256, carried=None):
    n_tok = dx1.shape[0]

    def body(dx1_ref, o_ref, wo_ref, nw_ref, do_ref, dya_ref, dyb_ref, dnw_ref):
        ov = o_ref[...]
        _, r = _rms_fwd(ov, nw_ref[...])
        do, dnw = _rms_bwd(ov, r, nw_ref[...], dx1_ref[...])
        dob = do.astype(BF16)
        do_ref[...] = dob
        dya_ref[...] = _dot_nt(dob, wo_ref[:GM_WIDTH, :])
        dyb_ref[...] = _dot_nt(dob, wo_ref[GM_WIDTH:, :])
        dnw_ref[...] += dnw

    return _rows_call("outproj_bwd", body, tm, [dx1, o], [w_out, nw],
                      [_sds((n_tok, D_MODEL), BF16), _sds((n_tok, GM_WIDTH), F32), _sds((n_tok, SSM_WIDTH), F32)],
                      [_sds((1, D_MODEL), F32)], carried=carried)


def _gmlp_bwd(p_uv, dya, lnw, lnb, e_bf, et_bf, w_cat, w_stack, bmap, carried=None):
    n_tok = p_uv.shape[0]
    chunks_per_step = 2

    def body(puv_ref, dya_ref, lnw_ref, lnb_ref, e_ref, et_ref, wcat_ref, wstack_ref, bmap_ref,
             dpuv_ref, dws_ref, dbs_ref, dlnw_ref, dlnb_ref, wm_scr, wsm_scr):
        t_stk = lax.broadcasted_iota(jnp.int32, (N_HEADS * CHUNK, CHUNK), 0) % CHUNK
        s_stk = lax.broadcasted_iota(jnp.int32, (N_HEADS * CHUNK, CHUNK), 1)

        @pl.when(pl.program_id(0) == 0)
        def _():
            wm_scr[...] = _causal_w_cat(wcat_ref[...])
            wsm_scr[...] = jnp.where(t_stk >= s_stk, wstack_ref[...], 0.0).astype(BF16)

        lnw_v = lnw_ref[...]
        e_v, et_v = e_ref[...], et_ref[...]

        def one_chunk(rows):
            u, v, gu, tu, tv, rstd, xhat, vn = _gmlp_common(puv_ref[rows, :], lnw_v, lnb_ref[...], e_v, et_v)
            vnb = vn.astype(BF16)
            mixed = jnp.dot(wm_scr[...], _head_blocks(vnb), preferred_element_type=F32) + bmap_ref[...]
            dy = dya_ref[rows, :]
            du = dy * mixed * _gelu_grad(u, tu)
            dmixed = dy * gu
            (dbs,) = _seg_dots([dmixed], et_v)
            dblocks = _head_blocks(dmixed.astype(BF16))
            dvn = lax.dot_general(wsm_scr[...], dblocks, (((0,), (0,)), ((), ())), preferred_element_type=F32)
            dws = lax.dot_general(dblocks, vnb, (((1,), (1,)), ((), ())), preferred_element_type=F32)
            dxh = dvn * lnw_v
            m1, m2 = _seg_dots([dxh, dxh * xhat], et_v)
            m1, m2 = _seg_dots([m1 * (1.0 / HEAD_DIM), m2 * (1.0 / HEAD_DIM)], e_v)
            dgv = rstd * (dxh - m1 - xhat * m2)
            dv = dgv * _gelu_grad(v, tv)
            dpuv_ref[rows, :GM_WIDTH] = du.astype(BF16)
            dpuv_ref[rows, GM_WIDTH:] = dv.astype(BF16)
            return dbs, dws, jnp.sum(dvn * xhat, axis=0, keepdims=True), jnp.sum(dvn, axis=0, keepdims=True)

        parts = [one_chunk(slice(k * CHUNK, (k + 1) * CHUNK)) for k in range(chunks_per_step)]
        dbs, dws, dlnw, dlnb = [functools.reduce(lambda a, b: a + b, vals) for vals in zip(*parts)]
        dbs_ref[...] += dbs
        dws_ref[...] += jnp.where(t_stk >= s_stk, dws, 0.0)
        dlnw_ref[...] += dlnw
        dlnb_ref[...] += dlnb

    return _rows_call(
        "gmlp_bwd", body, chunks_per_step * CHUNK, [p_uv, dya], [lnw, lnb, e_bf, et_bf, w_cat, w_stack, bmap],
        [_sds((n_tok, 2 * GM_WIDTH), BF16)],
        [_sds((N_HEADS * CHUNK, CHUNK), F32), _sds((CHUNK, DT_PAD), F32), _sds((1, GM_WIDTH), F32),
         _sds((1, GM_WIDTH), F32)],
        scratch=[pltpu.VMEM((CHUNK, N_HEADS * CHUNK), BF16), pltpu.VMEM((N_HEADS * CHUNK, CHUNK), BF16)],
        carried=carried)


def _ssd_bwd(p_xbc, p_z, p_dt, yssd, sprev, dyb, conv_w, conv_b, dt_bias, a_log, dskip_map, norm_w, e_bf, et_bf, n_seq,
             carried=None):
    n_tok = p_xbc.shape[0]
    nc = n_tok // n_seq // CHUNK

    def body(xr3, xprev3, z3, pdt3, yssd3, sprev3, dyb3,
             cw_ref, cb_ref, dtb_ref, alog_ref, dsk_ref, nw_ref, e_ref, et_ref,
             dpxbc3, dpz3, dpdt3, dcw_ref, dcb_ref, ddtb_ref, dalog_ref, ddsk_ref, dnw_ref,
             ds3_scr, nxt3_scr, dxa3_scr):
        @pl.when(pl.program_id(0) == 0)
        def _():
            for a in (dcw_ref, dcb_ref, ddtb_ref, dalog_ref, ddsk_ref, dnw_ref, ds3_scr, nxt3_scr):
                a[...] = jnp.zeros_like(a)

        for b in range(n_seq):
            one_sequence(xr3.at[b], xprev3.at[b], z3.at[b], pdt3.at[b], yssd3.at[b], sprev3.at[b], dyb3.at[b],
                         cw_ref, cb_ref, dtb_ref, alog_ref, dsk_ref, nw_ref, e_ref, et_ref,
                         dpxbc3.at[b], dpz3.at[b], dpdt3.at[b], dcw_ref, dcb_ref, ddtb_ref, dalog_ref, ddsk_ref, dnw_ref,
                         ds3_scr.at[b], nxt3_scr.at[b], dxa3_scr.at[b])

    def one_sequence(xr_ref, xprev_ref, z_ref, pdt_ref, yssd_ref, sprev_ref, dyb_ref,
                     cw_ref, cb_ref, dtb_ref, alog_ref, dsk_ref, nw_ref, e_ref, et_ref,
                     dpxbc_ref, dpz_ref, dpdt_ref, dcw_ref, dcb_ref, ddtb_ref, dalog_ref, ddsk_ref, dnw_ref,
                     ds_scr, nxt_scr, dxa_scr):
        chunk = nc - 1 - pl.program_id(0)
        xr = xr_ref[...]
        prev = jnp.where(chunk == 0, 0.0, xprev_ref[...])
        et_v = et_ref[...]
        p = _ssd_pre(xr, prev, cw_ref, cb_ref[...], pdt_ref[...], dtb_ref[...], alog_ref[...], e_ref[...])
        last, e_exp, dte, cd = _ssd_maps(p)
        rowi = p["rowi"]
        xs = p["xa"][:, :SSM_WIDTH]
        xd = xs * p["dt_map"]
        a_cs_t = p["a_cs"].T
        tri = _tril_mask()
        dsk = dsk_ref[...]
        nw_v = nw_ref[...]

        yv = yssd_ref[...]
        zv = z_ref[...]
        sz, zg, yg, _, rs = _gate_fwd(yv, zv, nw_v)
        dout = dyb_ref[...]
        for g in range(SSM_GROUPS):
            gs = slice(g * GROUP_W, (g + 1) * GROUP_W)
            dyg_g, dnw_g = _rms_bwd(yg[:, gs], rs[g], nw_v[:, gs], dout[:, gs])
            dnw_ref[:, gs] += dnw_g
            dxa_scr[:, gs] = dyg_g
        dyg = dxa_scr[:, :SSM_WIDTH]
        d_y = dyg * zg
        dpz_ref[...] = (dyg * yv * (sz + zv * sz * (1.0 - sz))).astype(BF16)

        s_prev = sprev_ref[...]
        ds_next = ds_scr[...]
        lane_dt = lax.broadcasted_iota(jnp.int32, (1, DT_PAD), 1)
        da_cols = jnp.zeros((CHUNK, DT_PAD), F32)
        for g in range(SSM_GROUPS):
            gs = slice(g * GROUP_W, (g + 1) * GROUP_W)
            b_off = SSM_WIDTH + g * SSM_STATE
            c_off = SSM_WIDTH + (SSM_GROUPS + g) * SSM_STATE
            bm = p["xa"][:, b_off:b_off + SSM_STATE].astype(BF16)
            cm = p["xa"][:, c_off:c_off + SSM_STATE].astype(BF16)
            cb_mat = _dot_nt(cm, bm)
            d_yg = d_y[:, gs]
            d_ygb = d_yg.astype(BF16)
            xdg = xd[:, gs]
            xdgb = xdg.astype(BF16)
            ds_g = ds_next[:, gs]
            sp_g = s_prev[:, gs]
            bds = _dot(bm, ds_g)
            dcs = d_yg * e_exp[:, gs]
            d_c = _dot_nt(dcs, sp_g)
            ds_scr[:, gs] = cd[:, gs] * ds_g + _dot_tn(cm, dcs)
            d_b = _dot_nt(xdg * dte[:, gs], ds_g)
            dxd_g = bds * dte[:, gs]
            sum_dcb = jnp.zeros((CHUNK, CHUNK), F32)
            for r in range(SSM_GROUPS * 2):
                head = g * 4 + r
                mask = _head_lane_mask(GROUP_W, r)
                dm = _head_decay(p["a_cs"], a_cs_t, head, tri)
                m_mat = cb_mat * dm
                g_mat = _dot_nt(jnp.where(mask, d_yg, 0.0), xdgb)
                w_mat = g_mat * m_mat
                sum_dcb = sum_dcb + g_mat * dm
                dxd_g = dxd_g + jnp.where(mask, _dot_tn(m_mat, d_ygb), 0.0)
                da_h = jnp.sum(w_mat - w_mat.T, axis=1, keepdims=True)
                da_cols = da_cols + jnp.where(lane_dt == head, da_h, 0.0)
            d_c = d_c + _dot(sum_dcb, bm)
            d_b = d_b + _dot_tn(sum_dcb, cm)
            dxa_scr[:, b_off:b_off + SSM_STATE] = d_b
            dxa_scr[:, c_off:c_off + SSM_STATE] = d_c
            y_off_g = _dot(cm, sp_g) * e_exp[:, gs]
            t3 = bds * xdg * dte[:, gs]
            tail = jnp.sum(t3, axis=0, keepdims=True) + jnp.sum(ds_g * sp_g, axis=0, keepdims=True) * cd[:, gs]
            pre_g = d_yg * y_off_g - t3 + jnp.where(last, tail, 0.0)
            s_pre, ddt_g, s_dsk = _seg_dots([pre_g, dxd_g * xs[:, gs], d_yg * xs[:, gs]], et_v[gs, :])
            da_cols = da_cols + s_pre
            ddsk_ref[...] += jnp.sum(s_dsk, axis=0, keepdims=True)
            dxa_scr[:, gs] = dxd_g * p["dt_map"][:, gs] + dsk[:, gs] * d_yg
            if g == 0:
                ddt = ddt_g
            else:
                ddt = ddt + ddt_g
        r_i = lax.broadcasted_iota(jnp.int32, (CHUNK, CHUNK), 0)
        c_i = lax.broadcasted_iota(jnp.int32, (CHUNK, CHUNK), 1)
        ddta = _tri_dot(r_i <= c_i, da_cols, terms=2)
        ddt = ddt + ddta * p["a_neg"]
        dalog_ref[...] += jnp.sum(ddta * p["dt"], axis=0, keepdims=True) * p["a_neg"]
        draw = ddt * _sigmoid(p["pre"])
        ddtb_ref[...] += jnp.sum(draw, axis=0, keepdims=True)
        dpdt_ref[...] = draw.astype(BF16)

        xc = p["xc"]
        sg = p["sg"]
        dxc = dxa_scr[...] * (sg + xc * sg * (1.0 - sg))
        dcb_ref[...] += jnp.sum(dxc, axis=0, keepdims=True)
        for k in range(CONV_K):
            dcw_ref[k] += jnp.sum(dxc * p["shifted"][k], axis=0, keepdims=True)
        nxt = nxt_scr[...]
        dxr = cw_ref[3] * dxc
        for s in range(1, CONV_K):
            dxr = dxr + cw_ref[CONV_K - 1 - s] * _shift_up(dxc, nxt, s)
        dpxbc_ref[...] = dxr.astype(BF16)
        nxt_scr[...] = dxc[:SUBLANES, :]

    seq_len = n_tok // n_seq

    def rows(width):
        return pl.BlockSpec((n_seq, CHUNK, width), lambda s: (0, nc - 1 - s, 0))

    tiles = CHUNK // SUBLANES
    prev_rows = pl.BlockSpec((n_seq, SUBLANES, CONV_CH), lambda s: (0, jnp.maximum((nc - 1 - s) * tiles - 1, 0), 0))

    def whole(shape):
        nd = len(shape)
        return pl.BlockSpec(tuple(shape), lambda s: (0,) * nd)

    def by_seq(a):
        return a.reshape(n_seq, seq_len, a.shape[-1])

    acc_shapes = [(CONV_K, 1, CONV_CH), (1, CONV_CH), (1, DT_PAD), (1, DT_PAD), (1, DT_PAD), (1, SSM_WIDTH)]
    xbc3 = by_seq(p_xbc)
    outs = _call_carrying(
        body, carried, name="ssd_bwd", grid=(nc,),
        in_specs=[rows(CONV_CH), prev_rows, rows(SSM_WIDTH), rows(DT_PAD), rows(SSM_WIDTH), rows(SSM_WIDTH),
                  rows(SSM_WIDTH)] + _ssd_const_specs(),
        out_specs=[rows(CONV_CH), rows(SSM_WIDTH), rows(DT_PAD)] + [whole(s) for s in acc_shapes],
        out_shape=tuple([_sds((n_seq, seq_len, CONV_CH), BF16), _sds((n_seq, seq_len, SSM_WIDTH), BF16),
                         _sds((n_seq, seq_len, DT_PAD), BF16)] + [_sds(s, F32) for s in acc_shapes]),
        scratch_shapes=[pltpu.VMEM((n_seq, SSM_STATE, SSM_WIDTH), F32), pltpu.VMEM((n_seq, SUBLANES, CONV_CH), F32),
                        pltpu.VMEM((n_seq, CHUNK, CONV_CH), F32)],
        operands=[xbc3, xbc3, by_seq(p_z), by_seq(p_dt), by_seq(yssd), by_seq(sprev), by_seq(dyb), conv_w, conv_b, dt_bias,
                  a_log, dskip_map, norm_w, e_bf, et_bf])
    return tuple(o.reshape(n_tok, o.shape[-1]) for o in outs[:3]) + tuple(outs[3:])


def _inproj_bwd(dp_uv, dp_xbc, dp_z, dp_dt, x, dx1, w_uv, w_xbc, w_z, w_dt, nw, tm=256, carried=None):
    n_tok = x.shape[0]

    def body(duv_ref, dxbc_ref, dz_ref, ddt_ref, x_ref, dx1_ref, wuv_ref, wxbc_ref, wz_ref, wdt_ref, nw_ref,
             gx_ref, h_ref, dnw_ref):
        dh = _dot_nt(duv_ref[...], wuv_ref[...]) + _dot_nt(dxbc_ref[...], wxbc_ref[...])
        dh = dh + _dot_nt(dz_ref[...], wz_ref[...]) + _dot_nt(ddt_ref[...], wdt_ref[...])
        xv = x_ref[...]
        h, r = _rms_fwd(xv, nw_ref[...])
        dx, dnw = _rms_bwd(xv, r, nw_ref[...], dh)
        gx_ref[...] = dx1_ref[...] + dx
        h_ref[...] = h.astype(BF16)
        dnw_ref[...] += dnw

    return _rows_call("inproj_bwd", body, tm, [dp_uv, dp_xbc, dp_z, dp_dt, x, dx1], [w_uv, w_xbc, w_z, w_dt, nw],
                      [_sds((n_tok, D_MODEL), F32), _sds((n_tok, D_MODEL), BF16)], [_sds((1, D_MODEL), F32)],
                      carried=carried)


def _const_maps():
    lane = jnp.arange(SSM_WIDTH) // HEAD_DIM
    e_bf = (jnp.arange(DT_PAD)[:, None] == lane[None, :]).astype(BF16)
    return e_bf, e_bf.T


def _pad_lanes(v, width):
    return jnp.pad(v, ((0, 0), (0, width - v.shape[1])))


SHARD_COLS = IN_COLS // N_CHIPS
_UV_END = 2 * GM_WIDTH
_Z_END = _UV_END + SSM_WIDTH
_XBC_END = _Z_END + CONV_CH


def _cols_from_shards(w4, lo, hi):
    pieces = []
    for j in range(N_CHIPS):
        a, b = max(lo, j * SHARD_COLS), min(hi, (j + 1) * SHARD_COLS)
        if a < b:
            pieces.append(w4[j][:, a - j * SHARD_COLS:b - j * SHARD_COLS])
    return pieces[0] if len(pieces) == 1 else jnp.concatenate(pieces, axis=1)


def _shards_from_cols(blocks):
    shards = []
    for j in range(N_CHIPS):
        pieces = []
        for arr, lo, hi in blocks:
            a, b = max(lo, j * SHARD_COLS), min(hi, (j + 1) * SHARD_COLS)
            if a < b:
                pieces.append(arr[:, a - lo:b - lo])
        shards.append(pieces[0] if len(pieces) == 1 else jnp.concatenate(pieces, axis=1))
    return jnp.stack(shards)


def _forward_backward(x, tgt, w_in4, conv_w, small, out_shard, up_shard, down_shard, core, adam_args):
    n_seq, seq_len, _ = x.shape
    n_tok = n_seq * seq_len
    x2 = x.reshape(n_tok, D_MODEL)
    tgt2 = tgt.reshape(n_tok, D_MODEL)
    e_bf, et_bf = _const_maps()

    w_uv = _cols_from_shards(w_in4, 0, _UV_END)
    w_z = _cols_from_shards(w_in4, _UV_END, _Z_END)
    w_xbc = _cols_from_shards(w_in4, _Z_END, _XBC_END)
    w_dt = _pad_lanes(_cols_from_shards(w_in4, _XBC_END, IN_COLS), DT_PAD)

    nw_pre = small["norm_mix_pre"]
    lnw = small["gm_ln_w"].reshape(1, GM_WIDTH)
    lnb = small["gm_ln_b"].reshape(1, GM_WIDTH)
    w_stack = small["gm_w_s"].reshape(N_HEADS * CHUNK, CHUNK)
    w_cat = jnp.transpose(small["gm_w_s"], (1, 0, 2)).reshape(CHUNK, N_HEADS * CHUNK)
    bmap = jnp.repeat(small["gm_b_s"].T, HEAD_DIM, axis=1)
    cw3 = conv_w.reshape(CONV_K, 1, CONV_CH)
    conv_b = small["conv_b"]
    dt_bias = _pad_lanes(small["dt_bias"], DT_PAD)
    a_log = _pad_lanes(small["a_log"], DT_PAD)
    dskip_map = jnp.repeat(small["d_skip"], HEAD_DIM, axis=1)
    ssm_nw = small["ssm_norm_w"]

    half = down_shard.shape[0] // 2
    p_uv, p_xbc, p_z, p_dt, w_out4 = _inproj_fwd(
        x2, nw_pre, w_uv, w_xbc, w_z, w_dt, carried=_allgather_exchange([out_shard]))
    ssd_consts = (cw3, conv_b, dt_bias, a_log, dskip_map, ssm_nw, e_bf, et_bf)
    ya, yb, yssd, sprev, w_up4, w_down_a = _mixer_fwd(
        p_uv, p_xbc, p_z, p_dt, lnw, lnb, w_cat, bmap, *ssd_consts, n_seq,
        carried=_allgather_exchange([up_shard, down_shard[:half]]))
    w_out_b = w_out4.reshape(D_MODEL, D_MODEL)
    o, x1, h2, w_down_b = _outproj_fwd(ya, yb, x2, w_out_b, small["norm_mix_post"], small["norm_ffn_pre"],
                                       carried=_allgather_exchange([down_shard[half:]]))
    f, dd, dy, loss_acc, d_nffn_post = _mlp_fwd(h2, x1, tgt2, w_up4, w_down_a, w_down_b, small["norm_ffn_post"])

    dup, dx1, d_nffn_pre = _mlp_bwd(dd, f, x1, dy, w_down_a, w_down_b, w_up4, small["norm_ffn_pre"])
    tk = min(DW_TOKENS_PER_STEP, n_tok)
    g_up = _matmul_tn("dw_up", h2, dup, D_MODEL, D_MODEL, tk, stacked=True)
    g_down = _matmul_tn("dw_down", f, dd, 1024, D_MODEL, tk).reshape(N_CHIPS, D_FF // N_CHIPS, D_MODEL)
    do, dya, dyb, d_nmix_post, got_up, got_down = _outproj_bwd(
        dx1, o, w_out_b, small["norm_mix_post"], carried=_pair_exchange([g_up, g_down]))
    h_up = _pair_sum(core, g_up, got_up, 256)
    h_down = _pair_sum(core, g_down, got_down, 256)
    g_out_a = _matmul_tn("dw_out_a", ya, do, GM_WIDTH, D_MODEL, tk)
    g_out_b = _matmul_tn("dw_out_b", yb, do, SSM_WIDTH, D_MODEL, tk)
    g_out = jnp.concatenate([g_out_a, g_out_b], axis=0).reshape(N_CHIPS, D_MODEL // N_CHIPS, D_MODEL)
    dp_uv, d_ws, d_bs_t, d_lnw, d_lnb, slab_up, got_out = _gmlp_bwd(
        p_uv, dya, lnw, lnb, e_bf, et_bf, w_cat, w_stack, bmap,
        carried=_both(_chip_exchange([h_up]), _pair_exchange([g_out])))
    h_out = _pair_sum(core, g_out, got_out, 128)
    early = {
        "gm_ln_w": d_lnw.reshape(N_HEADS, HEAD_DIM), "gm_ln_b": d_lnb.reshape(N_HEADS, HEAD_DIM),
        "gm_w_s": d_ws.reshape(N_HEADS, CHUNK, CHUNK), "gm_b_s": d_bs_t[:, :N_HEADS].T,
        "norm_mix_post": d_nmix_post, "norm_ffn_pre": d_nffn_pre, "norm_ffn_post": d_nffn_post,
    }
    packed_early = _pack(early, tuple(early), tail=loss_acc[0, 0].reshape(1))
    (dp_xbc, dp_z, dp_dt, d_cw, d_cb, d_dtb, d_alog, d_dsk, d_ssm_nw, slab_down, slab_out, all_early) = _ssd_bwd(
        p_xbc, p_z, p_dt, yssd, sprev, dyb, *ssd_consts, n_seq,
        carried=_both(_chip_exchange([h_down, h_out]), _device_gather_exchange(packed_early)))
    gx, h, d_nmix_pre = _inproj_bwd(dp_uv, dp_xbc, dp_z, dp_dt, x2, dx1, w_uv, w_xbc, w_z, w_dt, nw_pre)
    late = {
        "norm_mix_pre": d_nmix_pre, "conv_w": d_cw.reshape(CONV_K, CONV_CH), "conv_b": d_cb,
        "dt_bias": d_dtb[:, :N_HEADS], "a_log": d_alog[:, :N_HEADS], "d_skip": d_dsk[:, :N_HEADS],
        "ssm_norm_w": d_ssm_nw,
    }
    g_uv, all_late = _matmul_tn("dw_in_uv", h, dp_uv, D_MODEL, 2 * GM_WIDTH, tk,
                                carried=_device_gather_exchange(_pack(late, tuple(late))))
    sum_early = _ordered_sum("small_sum_early", all_early)
    small_sum = _unpack(sum_early, {n: v.shape for n, v in early.items()}, tuple(early))
    small_sum.update(_unpack(_ordered_sum("small_sum_late", all_late), {n: v.shape for n, v in late.items()}, tuple(late)))
    loss = sum_early.reshape(-1)[sum(v.size for v in early.values())]
    g_xbc = _matmul_tn("dw_in_xbc", h, dp_xbc, D_MODEL, CONV_CH, tk)
    g_z = _matmul_tn("dw_in_z", h, dp_z, D_MODEL, SSM_WIDTH, tk)
    g_dt = _matmul_tn("dw_in_dt", h, dp_dt, D_MODEL, DT_PAD, tk)

    g_in = _shards_from_cols([(g_uv, 0, _UV_END), (g_z, _UV_END, _Z_END), (g_xbc, _Z_END, _XBC_END),
                              (g_dt, _XBC_END, IN_COLS)])

    red_up, red_down, red_out = _chip_sum(slab_up, 256), _chip_sum(slab_down, 256), _chip_sum(slab_out, 128)
    oth_up, oth_down, oth_out, got_in = _run_exchange(
        "grad_pair_swap", _both(_pair_swap([red_up, red_down, red_out]), _pair_exchange([g_in])))
    h_in = _pair_sum(core, g_in, got_in, 256)
    (slab_in,) = _run_exchange("grad_chip_exchange", _chip_exchange([h_in]))
    res = _adamw_halves("adamw_mlp", [(adam_args["w_up"][0], red_up, oth_up) + adam_args["w_up"][1:],
                                      (adam_args["w_down"][0], red_down, oth_down) + adam_args["w_down"][1:]], 256)
    big_out = {"w_up": res[0:4], "w_down": res[4:8]}
    big_out["w_out"] = _adamw_halves("adamw_w_out", [(adam_args["w_out"][0], red_out, oth_out) + adam_args["w_out"][1:]], 128)
    red_in = _chip_sum(slab_in, 256)
    (oth_in,) = _run_exchange("grad_pair_swap_in", _pair_swap([red_in]))
    big_out["w_in"] = _adamw_halves("adamw_w_in", [(adam_args["w_in"][0], red_in, oth_in) + adam_args["w_in"][1:]], 256)

    return loss, gx.reshape(x.shape), big_out, small_sum


_HBM = pl.BlockSpec(memory_space=pltpu.HBM)


D2D_CHUNKS = 8
ICI_CHUNKS = 1
ROW_ALIGN = 16


def _row_chunks(rows, n_chunks):
    size = min(max(rows // n_chunks, ROW_ALIGN), rows)
    assert rows % size == 0
    return [(start, size) for start in range(0, rows, size)]


def _position():
    x, y, c = lax.axis_index("x"), lax.axis_index("y"), lax.axis_index("c")
    chips = [(1 - x, y), (x, 1 - y), (1 - x, 1 - y)]
    return x, y, c, chips


def _allgather_exchange(arrs):
    n = len(arrs)

    def copies(ins, outs, send_sems, recv_sems, local_sems):
        x, y, c, chips = _position()
        me = 2 * x + y
        sibling = (x, y, 1 - c)

        def copy(a, k, src, dst, to):
            return pltpu.make_async_remote_copy(src_ref=src, dst_ref=dst, send_sem=send_sems.at[a, k],
                                                recv_sem=recv_sems.at[a, k], device_id=to, device_id_type=MESH)

        def half_rows(a, pc):
            half = ins[a].shape[0] // 2
            return pl.ds(pc * half, half)

        local = [pltpu.make_async_copy(ins[a], outs[a].at[me], local_sems.at[a]) for a in range(n)]
        ici_out = [[copy(a, k, ins[a].at[half_rows(a, c)], outs[a].at[me, half_rows(a, c)], (px, py, c))
                    for k, (px, py) in enumerate(chips)] for a in range(n)]
        return c, chips, sibling, copy, half_rows, local, ici_out

    def start(ins, outs, send_sems, recv_sems, local_sems):
        c, chips, _, copy, _, local, _ = copies(ins, outs, send_sems, recv_sems, local_sems)
        x, y, _, _ = _position()
        me = 2 * x + y
        for cp in local:
            cp.start()
        for a in range(n):
            half = ins[a].shape[0] // 2
            for k, (px, py) in enumerate(chips):
                for first, size in _row_chunks(half, ICI_CHUNKS):
                    rows = pl.ds(c * half + first, size)
                    copy(a, k, ins[a].at[rows], outs[a].at[me, rows], (px, py, c)).start()

    def finish(ins, outs, send_sems, recv_sems, local_sems):
        c, chips, sibling, copy, half_rows, local, ici_out = copies(ins, outs, send_sems, recv_sems, local_sems)
        for a in range(n):
            half = ins[a].shape[0] // 2
            for k, (px, py) in enumerate(chips):
                blk = outs[a].at[2 * px + py, half_rows(a, c)]
                copy(a, k, blk, blk, (px, py, c)).wait_recv()
                for first, size in _row_chunks(half, D2D_CHUNKS):
                    piece = outs[a].at[2 * px + py, pl.ds(c * half + first, size)]
                    copy(a, 3 + k, piece, piece, sibling).start()
        for a in range(n):
            for k, (px, py) in enumerate(chips):
                theirs = outs[a].at[2 * px + py, half_rows(a, 1 - c)]
                copy(a, 3 + k, theirs, theirs, sibling).wait_recv()
                mine = outs[a].at[2 * px + py, half_rows(a, c)]
                copy(a, 3 + k, mine, mine, sibling).wait_send()
        for a in range(n):
            for cp in ici_out[a]:
                cp.wait_send()
        for cp in local:
            cp.wait()

    return _Carried(arrs, [_sds((N_CHIPS,) + a.shape, a.dtype) for a in arrs],
                    [pltpu.SemaphoreType.DMA((n, 6)), pltpu.SemaphoreType.DMA((n, 6)), pltpu.SemaphoreType.DMA((n,))],
                    start, finish)


def _run_exchange(name, exchange):
    n_in, n_out = len(exchange.ins), len(exchange.out_shapes)

    def body(*refs):
        ins, outs, sems = refs[:n_in], refs[n_in:n_in + n_out], refs[n_in + n_out:]
        exchange.start(ins, outs, *sems)
        exchange.finish(ins, outs, *sems)

    return pl.pallas_call(
        body, name=name, out_shape=tuple(exchange.out_shapes), in_specs=[_HBM] * n_in,
        out_specs=tuple([_HBM] * n_out), scratch_shapes=exchange.sems,
    )(*exchange.ins)


def _pair_exchange(grads):
    n = len(grads)

    def copier(send_sems, recv_sems):
        x, y, c, _ = _position()

        def copy(a, src, dst):
            return pltpu.make_async_remote_copy(src_ref=src, dst_ref=dst, send_sem=send_sems.at[a],
                                                recv_sem=recv_sems.at[a], device_id=(x, y, 1 - c), device_id_type=MESH)
        return c, copy

    def start(ins, got, send_sems, recv_sems):
        c, copy = copier(send_sems, recv_sems)
        for a in range(n):
            half = ins[a].shape[1] // 2
            for slab in range(N_CHIPS):
                for first, size in _row_chunks(half, D2D_CHUNKS):
                    copy(a, ins[a].at[slab, pl.ds((1 - c) * half + first, size), :],
                         got[a].at[slab, pl.ds(first, size), :]).start()

    def finish(ins, got, send_sems, recv_sems):
        c, copy = copier(send_sems, recv_sems)
        for a in range(n):
            half = ins[a].shape[1] // 2
            copy(a, ins[a].at[:, pl.ds((1 - c) * half, half), :], got[a]).wait()

    return _Carried(grads, [_sds((N_CHIPS, g.shape[1] // 2, g.shape[2]), g.dtype) for g in grads],
                    [pltpu.SemaphoreType.DMA((n,)), pltpu.SemaphoreType.DMA((n,))], start, finish)


def _chip_exchange(hsums):
    n = len(hsums)

    def copies(ins, outs, send_sems, recv_sems, local_sems, pieces):
        x, y, c, chips = _position()
        me = 2 * x + y
        cps = []
        for a in range(n):
            cps.append(pltpu.make_async_copy(ins[a].at[me], outs[a].at[me], local_sems.at[a]))
            rows = ins[a].shape[1]
            for k, (px, py) in enumerate(chips):
                for first, size in (_row_chunks(rows, ICI_CHUNKS) if pieces else [(0, rows)]):
                    cps.append(pltpu.make_async_remote_copy(
                        src_ref=ins[a].at[2 * px + py, pl.ds(first, size)], dst_ref=outs[a].at[me, pl.ds(first, size)],
                        send_sem=send_sems.at[a, k], recv_sem=recv_sems.at[a, k], device_id=(px, py, c),
                        device_id_type=MESH))
        return cps

    def start(*refs):
        for cp in copies(*refs, pieces=True):
            cp.start()

    def finish(*refs):
        for cp in copies(*refs, pieces=False):
            cp.wait()

    return _Carried(hsums, [_sds(h.shape, h.dtype) for h in hsums],
                    [pltpu.SemaphoreType.DMA((n, 3)), pltpu.SemaphoreType.DMA((n, 3)), pltpu.SemaphoreType.DMA((n,))],
                    start, finish)


def _pair_swap(reds):
    n = len(reds)

    def copier(send_sems, recv_sems):
        x, y, c, _ = _position()

        def copy(a, src, dst):
            return pltpu.make_async_remote_copy(src_ref=src, dst_ref=dst, send_sem=send_sems.at[a],
                                                recv_sem=recv_sems.at[a], device_id=(x, y, 1 - c), device_id_type=MESH)
        return copy

    def start(ins, outs, send_sems, recv_sems):
        copy = copier(send_sems, recv_sems)
        for a in range(n):
            for first, size in _row_chunks(ins[a].shape[0], 2 * D2D_CHUNKS):
                copy(a, ins[a].at[pl.ds(first, size), :], outs[a].at[pl.ds(first, size), :]).start()

    def finish(ins, outs, send_sems, recv_sems):
        copy = copier(send_sems, recv_sems)
        for a in range(n):
            copy(a, ins[a], outs[a]).wait()

    return _Carried(reds, [_sds(r.shape, r.dtype) for r in reds],
                    [pltpu.SemaphoreType.DMA((n,)), pltpu.SemaphoreType.DMA((n,))], start, finish)


def _device_gather_exchange(packed):
    def copies(ins, outs, send_sems, recv_sems, local_sem):
        (x_ref,), (all_ref,) = ins, outs
        x, y, c, chips = _position()
        me, sibling = (x, y, c), (x, y, 1 - c)

        def slab(px, py, pc):
            return all_ref.at[4 * px + 2 * py + pc]

        def copy(k, block, to, src=None):
            return pltpu.make_async_remote_copy(
                src_ref=slab(*block) if src is None else src, dst_ref=slab(*block), send_sem=send_sems.at[k],
                recv_sem=recv_sems.at[k], device_id=to, device_id_type=MESH)

        mine = pltpu.make_async_copy(x_ref, slab(*me), local_sem)
        first = [copy(0, me, sibling, src=x_ref)]
        first += [copy(1 + j, me, (*chip, c), src=x_ref) for j, chip in enumerate(chips)]
        passed = [copy(4 + j, (*chip, c), sibling) for j, chip in enumerate(chips)]
        return c, chips, me, sibling, copy, mine, first, passed

    def start(ins, outs, send_sems, recv_sems, local_sem):
        _, _, _, _, _, mine, first, _ = copies(ins, outs, send_sems, recv_sems, local_sem)
        mine.start()
        for cp in first:
            cp.start()

    def finish(ins, outs, send_sems, recv_sems, local_sem):
        c, chips, me, sibling, copy, mine, first, passed = copies(ins, outs, send_sems, recv_sems, local_sem)
        for j, chip in enumerate(chips):
            copy(1 + j, (*chip, c), me).wait_recv()
            passed[j].start()
        copy(0, sibling, me).wait_recv()
        for j, chip in enumerate(chips):
            copy(4 + j, (*chip, 1 - c), me).wait_recv()
        for cp in first + passed:
            cp.wait_send()
        mine.wait()

    return _Carried([packed], [_sds((N_DEV,) + packed.shape, F32)],
                    [pltpu.SemaphoreType.DMA((7,)), pltpu.SemaphoreType.DMA((7,)), pltpu.SemaphoreType.DMA],
                    start, finish)


def _ordered_sum(name, slabs):
    _, m_per, n_cols = slabs.shape

    def body(s_ref, o_ref):
        acc = s_ref[0]
        for d in range(1, N_DEV):
            acc = acc + s_ref[d]
        o_ref[...] = acc

    vmem = pl.BlockSpec(memory_space=pltpu.VMEM)
    return pl.pallas_call(body, name=name, out_shape=_sds((m_per, n_cols), F32), in_specs=[vmem], out_specs=vmem)(slabs)


def _pair_sum(core, own, got, tm):
    _, half, cols = got.shape
    nb = half // tm

    def body(c_ref, a_ref, b_ref, o_ref):
        o_ref[...] = (a_ref[...].astype(F32) + b_ref[...].astype(F32)).astype(BF16)

    return pl.pallas_call(
        body, name="grad_pair_sum", out_shape=_sds(got.shape, BF16),
        grid_spec=pltpu.PrefetchScalarGridSpec(
            num_scalar_prefetch=1, grid=(N_CHIPS, nb),
            in_specs=[pl.BlockSpec((None, tm, cols), lambda s, i, c_ref: (s, c_ref[0] * nb + i, 0)),
                      pl.BlockSpec((None, tm, cols), lambda s, i, c_ref: (s, i, 0))],
            out_specs=pl.BlockSpec((None, tm, cols), lambda s, i, c_ref: (s, i, 0))),
        compiler_params=_cparams(2),
    )(core, own, got)


def _chip_sum(slabs, tm):
    _, half, cols = slabs.shape

    def body(s_ref, o_ref):
        acc = s_ref[0].astype(F32)
        for k in range(1, N_CHIPS):
            acc = acc + s_ref[k].astype(F32)
        o_ref[...] = acc

    return pl.pallas_call(
        body, name="grad_chip_sum", out_shape=_sds((half, cols), F32), grid=(half // tm,),
        in_specs=[pl.BlockSpec((N_CHIPS, tm, cols), lambda i: (0, i, 0))],
        out_specs=pl.BlockSpec((tm, cols), lambda i: (i, 0)), compiler_params=_cparams(1),
    )(slabs)


def _adam_math(w, g, m, v):
    m2 = ADAM_B1 * m + (1.0 - ADAM_B1) * g
    v2 = ADAM_B2 * v + (1.0 - ADAM_B2) * (g * g)
    m_hat = m2 / (1.0 - ADAM_B1 ** ADAM_STEP)
    v_hat = v2 / (1.0 - ADAM_B2 ** ADAM_STEP)
    delta = -ADAM_LR * (m_hat / (jnp.sqrt(v_hat) + ADAM_EPS) + ADAM_WD * w)
    return delta, m2, v2


def _adamw_halves(name, items, tm, carried=None):
    rows, cols = items[0][0].shape
    nb = rows // 2 // tm
    n = len(items)

    def body(*refs):
        mine = (pl.program_id(0) // nb) == lax.axis_index("c")
        for k in range(n):
            w_ref, own_ref, oth_ref, m_ref, v_ref = refs[5 * k:5 * k + 5]
            g_ref, d_ref, m2_ref, v2_ref = refs[5 * n + 4 * k:5 * n + 4 * k + 4]
            g = jnp.where(mine, own_ref[...], oth_ref[...])
            d, m2, v2 = _adam_math(w_ref[...], g, m_ref[...], v_ref[...])
            g_ref[...] = g
            d_ref[...] = d
            m2_ref[...] = m2
            v2_ref[...] = v2

    full = pl.BlockSpec((tm, cols), lambda i: (i, 0))
    half = pl.BlockSpec((tm, cols), lambda i: (i % nb, 0))
    return _call_carrying(
        body, carried, name=name, grid=(rows // tm,), in_specs=[full, half, half, full, full] * n,
        out_specs=[full] * (4 * n), out_shape=tuple([_sds((rows, cols), F32)] * (4 * n)), scratch_shapes=[],
        operands=[a for item in items for a in item])


def _adamw(name, w, g, m, v, tm):
    def body(w_ref, g_ref, m_ref, v_ref, gout_ref, d_ref, m2_ref, v2_ref):
        gv = g_ref[...]
        d, m2, v2 = _adam_math(w_ref[...], gv, m_ref[...], v_ref[...])
        gout_ref[...] = gv
        d_ref[...] = d
        m2_ref[...] = m2
        v2_ref[...] = v2

    return _rows_call(name, body, tm, [w, g, m, v], [], [_sds(w.shape, F32)] * 4)


_SMALL_NAMES = ("norm_mix_pre", "gm_ln_w", "gm_ln_b", "gm_w_s", "gm_b_s", "conv_w", "conv_b", "dt_bias", "a_log",
                "d_skip", "ssm_norm_w", "norm_mix_post", "norm_ffn_pre", "norm_ffn_post")
_PACK_COLS = 1024


def _pack(parts, names=_SMALL_NAMES, tail=None):
    pieces = [parts[n].reshape(-1) for n in names]
    flat = jnp.concatenate(pieces if tail is None else pieces + [tail])
    rows = -(-flat.shape[0] // (8 * _PACK_COLS)) * 8
    flat = jnp.pad(flat, (0, rows * _PACK_COLS - flat.shape[0]))
    return flat.reshape(rows, _PACK_COLS)


def _unpack(packed, shapes, names=_SMALL_NAMES):
    flat = packed.reshape(-1)
    out, off = {}, 0
    for n in names:
        size = 1
        for s in shapes[n]:
            size *= s
        out[n] = flat[off:off + size].reshape(shapes[n])
        off += size
    return out


def kernel(x, norm_mix_pre, w_in, gm_ln_w, gm_ln_b, gm_w_s, gm_b_s, conv_w, conv_b, dt_bias, a_log, d_skip, ssm_norm_w, w_out, norm_mix_post, norm_ffn_pre, w_up, w_down, norm_ffn_post, loss_target, m_norm_mix_pre, m_w_in, m_gm_ln_w, m_gm_ln_b, m_gm_w_s, m_gm_b_s, m_conv_w, m_conv_b, m_dt_bias, m_a_log, m_d_skip, m_ssm_norm_w, m_w_out, m_norm_mix_post, m_norm_ffn_pre, m_w_up, m_w_down, m_norm_ffn_post, v_norm_mix_pre, v_w_in, v_gm_ln_w, v_gm_ln_b, v_gm_w_s, v_gm_b_s, v_conv_w, v_conv_b, v_dt_bias, v_a_log, v_d_skip, v_ssm_norm_w, v_w_out, v_norm_mix_post, v_norm_ffn_pre, v_w_up, v_w_down, v_norm_ffn_post):
    params = dict(norm_mix_pre=norm_mix_pre, w_in=w_in, gm_ln_w=gm_ln_w, gm_ln_b=gm_ln_b, gm_w_s=gm_w_s, gm_b_s=gm_b_s,
                  conv_w=conv_w, conv_b=conv_b, dt_bias=dt_bias, a_log=a_log, d_skip=d_skip, ssm_norm_w=ssm_norm_w,
                  w_out=w_out, norm_mix_post=norm_mix_post, norm_ffn_pre=norm_ffn_pre, w_up=w_up, w_down=w_down,
                  norm_ffn_post=norm_ffn_post)
    mom1 = dict(norm_mix_pre=m_norm_mix_pre, w_in=m_w_in, gm_ln_w=m_gm_ln_w, gm_ln_b=m_gm_ln_b, gm_w_s=m_gm_w_s,
                gm_b_s=m_gm_b_s, conv_w=m_conv_w, conv_b=m_conv_b, dt_bias=m_dt_bias, a_log=m_a_log, d_skip=m_d_skip,
                ssm_norm_w=m_ssm_norm_w, w_out=m_w_out, norm_mix_post=m_norm_mix_post, norm_ffn_pre=m_norm_ffn_pre,
                w_up=m_w_up, w_down=m_w_down, norm_ffn_post=m_norm_ffn_post)
    mom2 = dict(norm_mix_pre=v_norm_mix_pre, w_in=v_w_in, gm_ln_w=v_gm_ln_w, gm_ln_b=v_gm_ln_b, gm_w_s=v_gm_w_s,
                gm_b_s=v_gm_b_s, conv_w=v_conv_w, conv_b=v_conv_b, dt_bias=v_dt_bias, a_log=v_a_log, d_skip=v_d_skip,
                ssm_norm_w=v_ssm_norm_w, w_out=v_w_out, norm_mix_post=v_norm_mix_post, norm_ffn_pre=v_norm_ffn_pre,
                w_up=v_w_up, w_down=v_w_down, norm_ffn_post=v_norm_ffn_post)
    names = list(params)
    big = ("w_in", "w_out", "w_up", "w_down")
    chip = 2 * lax.axis_index("x") + lax.axis_index("y")

    shards = {n: params[n][0].astype(BF16) for n in big}
    conv_shard = jnp.pad(conv_w[0], ((0, 16 - CONV_K), (0, 0)))
    g_in4, g_conv4 = _run_exchange("allgather_w_in", _allgather_exchange([shards["w_in"], conv_shard]))
    conv_full = jnp.transpose(g_conv4[:, :CONV_K, :], (1, 0, 2)).reshape(CONV_K, CONV_CH)

    small = {n: params[n][0] if params[n].ndim >= 3 else params[n] for n in _SMALL_NAMES if n != "conv_w"}
    core = lax.axis_index("c").astype(jnp.int32).reshape(1)
    adam_args = {n: (params[n][0], mom1[n][0], mom2[n][0]) for n in big}
    loss, grad_x, big_out, small_sum = _forward_backward(
        x, loss_target, g_in4, conv_full, small, shards["w_out"], shards["w_up"], shards["w_down"], core, adam_args)
    grads, delta, new_m, new_v = {}, {}, {}, {}
    for n in big:
        grads[n], delta[n], new_m[n], new_v[n] = [a[None] for a in big_out[n]]

    small_sum["conv_w"] = lax.dynamic_slice_in_dim(small_sum["conv_w"], chip * (CONV_CH // N_CHIPS), CONV_CH // N_CHIPS, axis=1)

    local_shapes = {n: params[n].shape[1:] if params[n].ndim >= 3 else params[n].shape for n in _SMALL_NAMES}
    flat = lambda tree: {n: tree[n].reshape(local_shapes[n]) for n in _SMALL_NAMES}
    packed = [_pack(flat(t)) for t in (params, small_sum, mom1, mom2)]
    _, d_p, m_p, v_p = _adamw("adamw_small", *packed, packed[0].shape[0])
    for src, dst in ((d_p, delta), (m_p, new_m), (v_p, new_v)):
        for n, val in _unpack(src, local_shapes).items():
            dst[n] = val.reshape(params[n].shape)
    for n in _SMALL_NAMES:
        grads[n] = small_sum[n].reshape(params[n].shape)

    out = [loss, grad_x]
    for tree in (grads, delta, new_m, new_v):
        out += [tree[n] for n in names]
    return tuple(out)
```

```python
import functools

import jax
import jax.numpy as jnp
from jax import lax
from jax.experimental import pallas as pl
from jax.experimental.pallas import tpu as pltpu

F32 = jnp.float32
BF16 = jnp.bfloat16
HI = lax.Precision.HIGHEST
MESH = pl.DeviceIdType.MESH

EPS = 1e-6
D_MODEL = 1024
GM_WIDTH = 512
SSM_WIDTH = 512
N_HEADS = 8
HEAD_DIM = 64
CHUNK = 128
SSM_GROUPS = 2
GROUP_W = SSM_WIDTH // SSM_GROUPS
SSM_STATE = 128
CONV_K = 4
CONV_CH = 1024
D_FF = 4096
IN_COLS = 2568
DT_PAD = 128
N_CHIPS = 4
N_DEV = 8

ADAM_LR = 0.001
ADAM_B1 = 0.9
ADAM_B2 = 0.999
ADAM_EPS = 1e-08
ADAM_WD = 0.01
ADAM_STEP = 10

VMEM_LIMIT_BYTES = 56 * 1024 * 1024
FF_TILE = 512
DW_TOKENS_PER_STEP = 2048


def _cparams(n_axes):
    return pltpu.CompilerParams(dimension_semantics=("arbitrary",) * n_axes, vmem_limit_bytes=VMEM_LIMIT_BYTES)


def _dot(a, b):
    return jnp.dot(a.astype(BF16), b.astype(BF16), preferred_element_type=F32)


def _dot_nt(a, b):
    return lax.dot_general(a.astype(BF16), b.astype(BF16), (((1,), (1,)), ((), ())), preferred_element_type=F32)


def _dot_tn(a, b):
    return lax.dot_general(a.astype(BF16), b.astype(BF16), (((0,), (0,)), ((), ())), preferred_element_type=F32)


def _sigmoid(x):
    return 1.0 / (1.0 + jnp.exp(-x))


_GELU_C = 0.7978845608028654
_GELU_A = 0.044715


def _gelu(x):
    t = jnp.tanh(_GELU_C * (x + _GELU_A * (x * x * x)))
    return 0.5 * x * (1.0 + t), t


def _gelu_grad(x, t):
    return 0.5 * (1.0 + t) + 0.5 * x * (1.0 - t * t) * (_GELU_C * (1.0 + 3.0 * _GELU_A * x * x))


def _rms_fwd(x, w):
    r = lax.rsqrt(jnp.mean(x * x, axis=-1, keepdims=True) + EPS)
    return x * r * w, r


def _rms_bwd(x, r, w, dy):
    g = dy * w
    dx = r * g - x * (r * r * r) * jnp.mean(g * x, axis=-1, keepdims=True)
    dw = jnp.sum(dy * x * r, axis=0, keepdims=True)
    return dx, dw


class _Carried:
    def __init__(self, ins, out_shapes, sems, start, finish):
        self.ins, self.out_shapes, self.sems = list(ins), list(out_shapes), list(sems)
        self.start, self.finish = start, finish


def _both(first, second):
    n_i, n_o, n_s = len(first.ins), len(first.out_shapes), len(first.sems)

    def split(ins, outs, sems):
        return (ins[:n_i], outs[:n_o], sems[:n_s]), (ins[n_i:], outs[n_o:], sems[n_s:])

    def start(ins, outs, *sems):
        (i1, o1, s1), (i2, o2, s2) = split(ins, outs, sems)
        first.start(i1, o1, *s1)
        second.start(i2, o2, *s2)

    def finish(ins, outs, *sems):
        (i1, o1, s1), (i2, o2, s2) = split(ins, outs, sems)
        first.finish(i1, o1, *s1)
        second.finish(i2, o2, *s2)

    return _Carried(first.ins + second.ins, first.out_shapes + second.out_shapes, first.sems + second.sems, start, finish)


def _split_carried(refs, n_in, n_out, n_scratch, carried):
    n_ci, n_co, n_cs = len(carried.ins), len(carried.out_shapes), len(carried.sems)
    ins, rest = refs[:n_in], refs[n_in:]
    c_ins, rest = rest[:n_ci], rest[n_ci:]
    outs, rest = rest[:n_out], rest[n_out:]
    c_outs, rest = rest[:n_co], rest[n_co:]
    scr, c_sems = rest[:n_scratch], rest[n_scratch:]
    assert len(c_sems) == n_cs
    return tuple(ins) + tuple(outs) + tuple(scr), c_ins, c_outs, c_sems


def _rows_call(name, body, tm, row_ins, const_ins, row_outs, acc_outs=(), scratch=(), carried=None):
    n_rows = row_ins[0].shape[0]
    assert n_rows % tm == 0
    n_steps = n_rows // tm
    n_in = len(row_ins) + len(const_ins)
    n_ro = len(row_outs)
    n_acc = len(acc_outs)

    def kern(*refs):
        accs = refs[n_in + n_ro:n_in + n_ro + n_acc]

        @pl.when(pl.program_id(0) == 0)
        def _():
            for a in accs:
                a[...] = jnp.zeros_like(a)

        body(*refs)

    def whole(shape):
        nd = len(shape)
        return pl.BlockSpec(tuple(shape), lambda i: (0,) * nd)

    in_specs = [pl.BlockSpec((tm, a.shape[1]), lambda i: (i, 0)) for a in row_ins]
    in_specs += [whole(a.shape) for a in const_ins]
    out_specs = [pl.BlockSpec((tm, s.shape[1]), lambda i: (i, 0)) for s in row_outs]
    out_specs += [whole(s.shape) for s in acc_outs]
    return _call_carrying(
        kern, carried, name=name, grid=(n_steps,), in_specs=in_specs, out_specs=out_specs,
        out_shape=tuple(row_outs) + tuple(acc_outs), scratch_shapes=list(scratch), operands=list(row_ins) + list(const_ins))


def _call_carrying(body, carried, *, name, grid, in_specs, out_specs, out_shape, scratch_shapes, operands):
    n_in, n_out, n_scratch = len(in_specs), len(out_specs), len(scratch_shapes)
    kern = body
    if carried is not None:
        def kern(*refs):
            plain, c_ins, c_outs, c_sems = _split_carried(refs, n_in, n_out, n_scratch, carried)
            first, last = True, True
            for d, size in enumerate(grid):
                first = jnp.logical_and(first, pl.program_id(d) == 0)
                last = jnp.logical_and(last, pl.program_id(d) == size - 1)

            @pl.when(first)
            def _():
                carried.start(c_ins, c_outs, *c_sems)

            body(*plain)

            @pl.when(last)
            def _():
                carried.finish(c_ins, c_outs, *c_sems)

        in_specs = list(in_specs) + [_HBM] * len(carried.ins)
        out_specs = list(out_specs) + [_HBM] * len(carried.out_shapes)
        out_shape = tuple(out_shape) + tuple(carried.out_shapes)
        operands = list(operands) + carried.ins
        scratch_shapes = list(scratch_shapes) + carried.sems
    return pl.pallas_call(
        kern, name=name, grid=grid, in_specs=in_specs, out_specs=out_specs, out_shape=out_shape,
        scratch_shapes=scratch_shapes, compiler_params=_cparams(len(grid)),
    )(*operands)


def _sds(shape, dtype):
    return jax.ShapeDtypeStruct(tuple(shape), dtype)


def _matmul_tn(name, a, b, tm, tn, tk, stacked=False, carried=None):
    k_dim, m_dim = a.shape
    n_dim = b.shape[1]
    assert m_dim % tm == 0 and n_dim % tn == 0 and k_dim % tk == 0
    nk = k_dim // tk

    def kern(a_ref, b_ref, o_ref, acc_ref):
        k = pl.program_id(2)
        prod = _dot_tn(a_ref[...], b_ref[...])

        @pl.when(k == 0)
        def _():
            acc_ref[...] = prod

        @pl.when(k > 0)
        def _():
            acc_ref[...] += prod

        @pl.when(k == nk - 1)
        def _():
            o_ref[...] = acc_ref[...].astype(o_ref.dtype)

    if stacked:
        assert tm == m_dim
        out_shape = _sds((n_dim // tn, m_dim, tn), BF16)
        out_spec = pl.BlockSpec((None, tm, tn), lambda i, j, k: (j, i, 0))
    else:
        out_shape = _sds((m_dim, n_dim), BF16)
        out_spec = pl.BlockSpec((tm, tn), lambda i, j, k: (i, j))
    outs = _call_carrying(
        kern, carried, name=name, grid=(m_dim // tm, n_dim // tn, nk),
        in_specs=[pl.BlockSpec((tk, tm), lambda i, j, k: (k, i)), pl.BlockSpec((tk, tn), lambda i, j, k: (k, j))],
        out_specs=[out_spec], out_shape=(out_shape,), scratch_shapes=[pltpu.VMEM((tm, tn), F32)], operands=[a, b])
    return outs[0] if carried is None else outs


def _inproj_fwd(x, nw, w_uv, w_xbc, w_z, w_dt, tm=256, carried=None):
    n_tok = x.shape[0]

    def body(x_ref, nw_ref, wuv_ref, wxbc_ref, wz_ref, wdt_ref, puv_ref, pxbc_ref, pz_ref, pdt_ref):
        h, _ = _rms_fwd(x_ref[...], nw_ref[...])
        h = h.astype(BF16)
        puv_ref[...] = jnp.dot(h, wuv_ref[...], preferred_element_type=F32)
        pxbc_ref[...] = jnp.dot(h, wxbc_ref[...], preferred_element_type=F32)
        pz_ref[...] = jnp.dot(h, wz_ref[...], preferred_element_type=F32)
        pdt_ref[...] = jnp.dot(h, wdt_ref[...], preferred_element_type=F32)

    return _rows_call(
        "inproj_fwd", body, tm, [x], [nw, w_uv, w_xbc, w_z, w_dt],
        [_sds((n_tok, 2 * GM_WIDTH), F32), _sds((n_tok, CONV_CH), F32), _sds((n_tok, SSM_WIDTH), F32),
         _sds((n_tok, DT_PAD), F32)], carried=carried)


def _head_lane_mask(width, head):
    lane = lax.broadcasted_iota(jnp.int32, (1, width), 1)
    return (lane // HEAD_DIM) == head


def _split_terms(x, terms):
    parts = []
    for _ in range(terms):
        p = x.astype(BF16)
        parts.append(p)
        x = x - p.astype(F32)
    return parts


def _seg_dots(vals, ind, terms=2):
    m = vals[0].shape[0]
    parts = []
    for v in vals:
        parts += _split_terms(v, terms)
    red = jnp.dot(jnp.concatenate(parts, axis=0), ind, preferred_element_type=F32)
    outs = []
    for i in range(len(vals)):
        acc = red[i * terms * m:(i * terms + 1) * m]
        for t in range(1, terms):
            acc = acc + red[(i * terms + t) * m:(i * terms + t + 1) * m]
        outs.append(acc)
    return outs


def _tri_dot(mask, x, terms=3):
    n = x.shape[1]
    red = jnp.dot(mask.astype(BF16), jnp.concatenate(_split_terms(x, terms), axis=1), preferred_element_type=F32)
    acc = red[:, :n]
    for t in range(1, terms):
        acc = acc + red[:, t * n:(t + 1) * n]
    return acc


def _gmlp_common(puv, lnw, lnb, e_bf, et_bf):
    u = puv[:, :GM_WIDTH]
    v = puv[:, GM_WIDTH:]
    gu, tu = _gelu(u)
    gv, tv = _gelu(v)
    (s1,) = _seg_dots([gv], et_bf)
    (mu,) = _seg_dots([s1 * (1.0 / HEAD_DIM)], e_bf)
    xc = gv - mu
    (s2,) = _seg_dots([xc * xc], et_bf)
    (rstd,) = _seg_dots([lax.rsqrt(s2 * (1.0 / HEAD_DIM) + EPS)], e_bf)
    xhat = xc * rstd
    vn = xhat * lnw + lnb
    return u, v, gu, tu, tv, rstd, xhat, vn


def _tril_mask():
    r = lax.broadcasted_iota(jnp.int32, (CHUNK, CHUNK), 0)
    c = lax.broadcasted_iota(jnp.int32, (CHUNK, CHUNK), 1)
    return r >= c


def _head_blocks(v):
    return jnp.concatenate([jnp.where(_head_lane_mask(GM_WIDTH, h), v, jnp.zeros_like(v)) for h in range(N_HEADS)], axis=0)


def _causal_w_cat(w_cat):
    t = lax.broadcasted_iota(jnp.int32, (CHUNK, N_HEADS * CHUNK), 0)
    s = lax.broadcasted_iota(jnp.int32, (CHUNK, N_HEADS * CHUNK), 1) % CHUNK
    return jnp.where(t >= s, w_cat, 0.0).astype(BF16)


def _gmlp_chunk_fwd(puv, lnw, lnb, e_bf, et_bf, wm, bmap):
    _, _, gu, _, _, _, _, vn = _gmlp_common(puv, lnw, lnb, e_bf, et_bf)
    mixed = jnp.dot(wm, _head_blocks(vn.astype(BF16)), preferred_element_type=F32) + bmap
    return (gu * mixed).astype(BF16)


SUBLANES = 8


def _shift_down(x, tail, s):
    main = pltpu.roll(x, s, 0)
    row = lax.broadcasted_iota(jnp.int32, (SUBLANES, 1), 0)
    head = jnp.where(row < s, pltpu.roll(tail, s, 0), main[:SUBLANES])
    return jnp.concatenate([head, main[SUBLANES:]], axis=0)


def _shift_up(x, head_next, s):
    n = x.shape[0]
    main = pltpu.roll(x, n - s, 0)
    row = lax.broadcasted_iota(jnp.int32, (SUBLANES, 1), 0)
    last = jnp.where(row >= SUBLANES - s, pltpu.roll(head_next, SUBLANES - s, 0), main[n - SUBLANES:])
    return jnp.concatenate([main[:n - SUBLANES], last], axis=0)


def _ssd_pre(xr, tail, cw_ref, cb, pdt, dtb, alog, emap):
    rowi = lax.broadcasted_iota(jnp.int32, (CHUNK, 1), 0)
    shifted = [_shift_down(xr, tail, 3), _shift_down(xr, tail, 2), _shift_down(xr, tail, 1), xr]
    xc = cb
    for k in range(CONV_K):
        xc = xc + cw_ref[k] * shifted[k]
    sg = _sigmoid(xc)
    xa = xc * sg
    pre = pdt + dtb
    dt = jnp.maximum(pre, 0.0) + jnp.log(1.0 + jnp.exp(-jnp.abs(pre)))
    a_neg = -jnp.exp(alog)
    a_cs = _tri_dot(_tril_mask(), dt * a_neg)
    acs_map, dt_map = _seg_dots([a_cs, dt], emap, terms=3)
    return dict(shifted=shifted, xc=xc, sg=sg, xa=xa, pre=pre, dt=dt, a_neg=a_neg, a_cs=a_cs,
                acs_map=acs_map, dt_map=dt_map, rowi=rowi)


def _ssd_maps(p):
    last = p["rowi"] == CHUNK - 1
    aq_map = jnp.sum(jnp.where(last, p["acs_map"], 0.0), axis=0, keepdims=True)
    e_exp = jnp.exp(p["acs_map"])
    dte = jnp.exp(aq_map - p["acs_map"])
    cd = jnp.exp(aq_map)
    return last, e_exp, dte, cd


def _head_decay(a_cs, a_cs_t, head, tri):
    lane = lax.broadcasted_iota(jnp.int32, (1, DT_PAD), 1)
    sub = lax.broadcasted_iota(jnp.int32, (DT_PAD, 1), 0)
    col = jnp.sum(jnp.where(lane == head, a_cs, 0.0), axis=1, keepdims=True)
    row = jnp.sum(jnp.where(sub == head, a_cs_t, 0.0), axis=0, keepdims=True)
    return jnp.exp(jnp.where(tri, col - row, -1e30))


def _gate_fwd(y, z, nw):
    sz = _sigmoid(z)
    zg = z * sz
    yg = y * zg
    outs, rs = [], []
    for g in range(SSM_GROUPS):
        gs = slice(g * GROUP_W, (g + 1) * GROUP_W)
        o, r = _rms_fwd(yg[:, gs], nw[:, gs])
        outs.append(o)
        rs.append(r)
    return sz, zg, yg, outs, rs


def _ssd_const_specs():
    def whole(shape):
        nd = len(shape)
        return pl.BlockSpec(tuple(shape), lambda c: (0,) * nd)
    return [whole((CONV_K, 1, CONV_CH)), whole((1, CONV_CH)), whole((1, DT_PAD)), whole((1, DT_PAD)),
            whole((1, SSM_WIDTH)), whole((1, SSM_WIDTH)), whole((DT_PAD, SSM_WIDTH)), whole((SSM_WIDTH, DT_PAD))]


def _mixer_fwd(p_uv, p_xbc, p_z, p_dt, lnw, lnb, w_cat, bmap, conv_w, conv_b, dt_bias, a_log, dskip_map, norm_w,
               e_bf, et_bf, n_seq, carried=None):
    n_tok = p_xbc.shape[0]
    nc = n_tok // n_seq // CHUNK

    def body(puv3, xr3, z3, pdt3, lnw_ref, lnb_ref, wcat_ref, bmap_ref,
             cw_ref, cb_ref, dtb_ref, alog_ref, dsk_ref, nw_ref, e_ref, et_ref,
             ya3, yb3, yssd3, sprev3, wm_scr, prev3_scr, s3_scr):
        @pl.when(pl.program_id(0) == 0)
        def _():
            wm_scr[...] = _causal_w_cat(wcat_ref[...])
            prev3_scr[...] = jnp.zeros_like(prev3_scr)
            s3_scr[...] = jnp.zeros_like(s3_scr)

        for b in range(n_seq):
            one_sequence(puv3.at[b], xr3.at[b], z3.at[b], pdt3.at[b], lnw_ref, lnb_ref, bmap_ref,
                         cw_ref, cb_ref, dtb_ref, alog_ref, dsk_ref, nw_ref, e_ref, et_ref,
                         ya3.at[b], yb3.at[b], yssd3.at[b], sprev3.at[b], wm_scr, prev3_scr.at[b], s3_scr.at[b])

    def one_sequence(puv_ref, xr_ref, z_ref, pdt_ref, lnw_ref, lnb_ref, bmap_ref,
                     cw_ref, cb_ref, dtb_ref, alog_ref, dsk_ref, nw_ref, e_ref, et_ref,
                     ya_ref, yb_ref, yssd_ref, sprev_ref, wm_scr, prev_scr, s_scr):
        ya_ref[...] = _gmlp_chunk_fwd(puv_ref[...], lnw_ref[...], lnb_ref[...], e_ref[...], et_ref[...], wm_scr[...],
                                      bmap_ref[...])
        xr = xr_ref[...]
        p = _ssd_pre(xr, prev_scr[...], cw_ref, cb_ref[...], pdt_ref[...], dtb_ref[...], alog_ref[...], e_ref[...])
        _, e_exp, dte, cd = _ssd_maps(p)
        xs = p["xa"][:, :SSM_WIDTH]
        xd = xs * p["dt_map"]
        a_cs_t = p["a_cs"].T
        tri = _tril_mask()
        s_old = s_scr[...]
        sprev_ref[...] = s_old
        for g in range(SSM_GROUPS):
            gs = slice(g * GROUP_W, (g + 1) * GROUP_W)
            bm = p["xa"][:, SSM_WIDTH + g * SSM_STATE: SSM_WIDTH + (g + 1) * SSM_STATE].astype(BF16)
            cm = p["xa"][:, SSM_WIDTH + (SSM_GROUPS + g) * SSM_STATE: SSM_WIDTH + (SSM_GROUPS + g + 1) * SSM_STATE].astype(BF16)
            cb_mat = _dot_nt(cm, bm)
            xdg = xd[:, gs].astype(BF16)
            y_g = _dot(cm, s_old[:, gs]) * e_exp[:, gs] + dsk_ref[:, gs] * xs[:, gs]
            for r in range(SSM_GROUPS * 2):
                dm = _head_decay(p["a_cs"], a_cs_t, g * 4 + r, tri)
                full = jnp.dot((cb_mat * dm).astype(BF16), xdg, preferred_element_type=F32)
                y_g = y_g + jnp.where(_head_lane_mask(GROUP_W, r), full, 0.0)
            yssd_ref[:, gs] = y_g
            s_scr[:, gs] = cd[:, gs] * s_old[:, gs] + _dot_tn(bm, xd[:, gs] * dte[:, gs])
        _, _, _, outs, _ = _gate_fwd(yssd_ref[...], z_ref[...], nw_ref[...])
        for g in range(SSM_GROUPS):
            yb_ref[:, g * GROUP_W:(g + 1) * GROUP_W] = outs[g].astype(BF16)
        prev_scr[...] = xr[CHUNK - SUBLANES:, :]

    seq_len = n_tok // n_seq

    def rows(width):
        return pl.BlockSpec((n_seq, CHUNK, width), lambda c: (0, c, 0))

    def whole(shape):
        nd = len(shape)
        return pl.BlockSpec(tuple(shape), lambda c: (0,) * nd)

    def by_seq(a):
        return a.reshape(n_seq, seq_len, a.shape[-1])

    outs = _call_carrying(
        body, carried, name="mixer_fwd", grid=(nc,),
        in_specs=[rows(2 * GM_WIDTH), rows(CONV_CH), rows(SSM_WIDTH), rows(DT_PAD), whole(lnw.shape), whole(lnb.shape),
                  whole(w_cat.shape), whole(bmap.shape)] + _ssd_const_specs(),
        out_specs=[rows(GM_WIDTH), rows(SSM_WIDTH), rows(SSM_WIDTH), rows(SSM_WIDTH)],
        out_shape=(_sds((n_seq, seq_len, GM_WIDTH), BF16), _sds((n_seq, seq_len, SSM_WIDTH), BF16),
                   _sds((n_seq, seq_len, SSM_WIDTH), F32), _sds((n_seq, seq_len, SSM_WIDTH), F32)),
        scratch_shapes=[pltpu.VMEM((CHUNK, N_HEADS * CHUNK), BF16), pltpu.VMEM((n_seq, SUBLANES, CONV_CH), F32),
                        pltpu.VMEM((n_seq, SSM_STATE, SSM_WIDTH), F32)],
        operands=[by_seq(p_uv), by_seq(p_xbc), by_seq(p_z), by_seq(p_dt), lnw, lnb, w_cat, bmap, conv_w, conv_b, dt_bias,
                  a_log, dskip_map, norm_w, e_bf, et_bf])
    return tuple(o.reshape(n_tok, o.shape[-1]) for o in outs[:4]) + tuple(outs[4:])


def _outproj_fwd(ya, yb, x, w_out, nw_post, nw_pre2, tm=256):
    n_tok = x.shape[0]

    def body(ya_ref, yb_ref, x_ref, wo_ref, nwa_ref, nwb_ref, o_ref, x1_ref, h2_ref):
        o = jnp.dot(ya_ref[...], wo_ref[:GM_WIDTH, :], preferred_element_type=F32)
        o = o + jnp.dot(yb_ref[...], wo_ref[GM_WIDTH:, :], preferred_element_type=F32)
        on, _ = _rms_fwd(o, nwa_ref[...])
        x1 = x_ref[...] + on
        h2, _ = _rms_fwd(x1, nwb_ref[...])
        o_ref[...] = o
        x1_ref[...] = x1
        h2_ref[...] = h2.astype(BF16)

    return _rows_call("outproj_fwd", body, tm, [ya, yb, x], [w_out, nw_post, nw_pre2],
                      [_sds((n_tok, D_MODEL), F32), _sds((n_tok, D_MODEL), F32), _sds((n_tok, D_MODEL), BF16)])


def _up_cols(wup_ref, j):
    per = (D_FF // N_CHIPS) // FF_TILE
    return wup_ref[j // per, :, (j % per) * FF_TILE:(j % per + 1) * FF_TILE]


def _down_rows(wda_ref, wdb_ref, j):
    assert 2 * FF_TILE == D_FF // N_CHIPS
    return (wda_ref if j % 2 == 0 else wdb_ref)[j // 2]


def _skewed_rows_call(name, main, tail, tm, lead_ins, lag_ins, const_ins, lead_outs, lag_outs, acc_outs, carry,
                      streamed, tile_copies, n_copies):
    n_rows = lead_ins[0].shape[0]
    assert n_rows % tm == 0
    n = n_rows // tm
    counts = [len(lead_ins), len(lag_ins), len(const_ins), len(streamed), len(lead_outs), len(lag_outs), len(acc_outs),
              1, len(streamed)]

    def kern(*refs):
        groups, pos = [], 0
        for cnt in counts:
            groups.append(refs[pos:pos + cnt])
            pos += cnt
        lead_i, lag_i, consts, w_hbm, lead_o, lag_o, accs, (carry_scr,), w_vmem = groups
        sems = refs[pos]
        i = pl.program_id(0)
        pieces, k = [], 0
        for piece in tile_copies(w_hbm, w_vmem):
            pieces.append([pltpu.make_async_copy(src, dst, sems.at[k + q]) for q, (src, dst) in enumerate(piece)])
            k += len(piece)

        def ready(j):
            for cp in pieces[j]:
                cp.wait()

        @pl.when(i == 0)
        def _():
            for piece in pieces:
                for cp in piece:
                    cp.start()
            for a in accs:
                a[...] = jnp.zeros_like(a)
            carry_scr[...] = main(lead_i, consts, lead_o, w_vmem, ready)

        @pl.when(jnp.logical_and(i > 0, i < n))
        def _():
            previous = carry_scr[...]
            carry_scr[...] = main(lead_i, consts, lead_o, w_vmem, lambda j: None)
            tail(previous, lag_i, consts, lag_o, accs)

        @pl.when(i == n)
        def _():
            tail(carry_scr[...], lag_i, consts, lag_o, accs)

    def lead(width):
        return pl.BlockSpec((tm, width), lambda i: (jnp.minimum(i, n - 1), 0))

    def lag(width):
        return pl.BlockSpec((tm, width), lambda i: (jnp.maximum(i - 1, 0), 0))

    def whole(shape):
        nd = len(shape)
        return pl.BlockSpec(tuple(shape), lambda i: (0,) * nd)

    return pl.pallas_call(
        kern, name=name, grid=(n + 1,),
        in_specs=([lead(a.shape[1]) for a in lead_ins] + [lag(a.shape[1]) for a in lag_ins]
                  + [whole(a.shape) for a in const_ins] + [_HBM] * len(streamed)),
        out_specs=[lead(s.shape[1]) for s in lead_outs] + [lag(s.shape[1]) for s in lag_outs] + [whole(s.shape) for s in acc_outs],
        out_shape=tuple(lead_outs) + tuple(lag_outs) + tuple(acc_outs),
        scratch_shapes=([pltpu.VMEM(carry, F32)] + [pltpu.VMEM(a.shape, a.dtype) for a in streamed]
                        + [pltpu.SemaphoreType.DMA((n_copies,))]),
        compiler_params=_cparams(1),
    )(*lead_ins, *lag_ins, *const_ins, *streamed)


def _mlp_weight_pieces(order):
    per = (D_FF // N_CHIPS) // FF_TILE

    def tile_copies(hbm, vmem):
        pieces = []
        for j in range(D_FF // FF_TILE):
            cols = (j // per, slice(None), pl.ds((j % per) * FF_TILE, FF_TILE))
            up = (hbm[0].at[cols], vmem[0].at[cols])
            down = (hbm[1 + j % 2].at[j // 2], vmem[1 + j % 2].at[j // 2])
            pieces.append([up, down] if order == "up_down" else [down, up])
        return pieces

    return tile_copies


def _mlp_fwd(h2, x1, tgt, w_up, w_down_a, w_down_b, nw, tm=512):
    n_tok = x1.shape[0]

    def main(lead_i, consts, lead_o, weights, ready):
        (h2_ref,), (f_ref,), (wup_ref, wda_ref, wdb_ref) = lead_i, lead_o, weights
        h2v = h2_ref[...]
        acc = jnp.zeros((tm, D_MODEL), F32)
        for j in range(D_FF // FF_TILE):
            cs = slice(j * FF_TILE, (j + 1) * FF_TILE)
            ready(j)
            u = jnp.dot(h2v, _up_cols(wup_ref, j), preferred_element_type=F32)
            f = jnp.square(jnp.maximum(u, 0.0)).astype(BF16)
            f_ref[:, cs] = f
            acc = acc + jnp.dot(f, _down_rows(wda_ref, wdb_ref, j), preferred_element_type=F32)
        return acc

    def tail(acc, lag_i, consts, lag_o, accs):
        (x1_ref, tgt_ref), (nw_ref,), (dd_ref, dy_ref), (loss_ref, dnw_ref) = lag_i, consts, lag_o, accs
        dn, r = _rms_fwd(acc, nw_ref[...])
        e = x1_ref[...] + dn - tgt_ref[...]
        loss_ref[...] += jnp.full(loss_ref.shape, (0.5 / D_MODEL) * jnp.sum(e * e), F32)
        dy = e * (1.0 / D_MODEL)
        dd, dnw = _rms_bwd(acc, r, nw_ref[...], dy)
        dy_ref[...] = dy
        dd_ref[...] = dd.astype(BF16)
        dnw_ref[...] += dnw

    return _skewed_rows_call(
        "mlp_fwd", main, tail, tm, [h2], [x1, tgt], [nw],
        [_sds((n_tok, D_FF), BF16)], [_sds((n_tok, D_MODEL), BF16), _sds((n_tok, D_MODEL), F32)],
        [_sds((8, 128), F32), _sds((1, D_MODEL), F32)], carry=(tm, D_MODEL),
        streamed=[w_up, w_down_a, w_down_b], tile_copies=_mlp_weight_pieces("up_down"), n_copies=2 * (D_FF // FF_TILE))


def _mlp_bwd(dd, f, x1, dy, w_down_a, w_down_b, w_up, nw, tm=256):
    n_tok = x1.shape[0]

    def main(lead_i, consts, lead_o, weights, ready):
        (dd_ref, f_ref), (dup_ref,), (wup_ref, wda_ref, wdb_ref) = lead_i, lead_o, weights
        ddv = dd_ref[...]
        acc = jnp.zeros((tm, D_MODEL), F32)
        for j in range(D_FF // FF_TILE):
            cs = slice(j * FF_TILE, (j + 1) * FF_TILE)
            ready(j)
            df = _dot_nt(ddv, _down_rows(wda_ref, wdb_ref, j))
            du = (df * (2.0 * jnp.sqrt(f_ref[:, cs].astype(F32)))).astype(BF16)
            dup_ref[:, cs] = du
            acc = acc + _dot_nt(du, _up_cols(wup_ref, j))
        return acc

    def tail(acc, lag_i, consts, lag_o, accs):
        (x1_ref, dy_ref), (nw_ref,), (dx1_ref,), (dnw_ref,) = lag_i, consts, lag_o, accs
        x1v = x1_ref[...]
        _, r = _rms_fwd(x1v, nw_ref[...])
        dx, dnw = _rms_bwd(x1v, r, nw_ref[...], acc)
        dx1_ref[...] = dy_ref[...] + dx
        dnw_ref[...] += dnw

    return _skewed_rows_call(
        "mlp_bwd", main, tail, tm, [dd, f], [x1, dy], [nw],
        [_sds((n_tok, D_FF), BF16)], [_sds((n_tok, D_MODEL), F32)], [_sds((1, D_MODEL), F32)], carry=(tm, D_MODEL),
        streamed=[w_up, w_down_a, w_down_b], tile_copies=_mlp_weight_pieces("down_up"), n_copies=2 * (D_FF // FF_TILE))


def _outproj_bwd(dx1, o, w_out, nw, tm=256, carried=None):
    n_tok = dx1.shape[0]

    def body(dx1_ref, o_ref, wo_ref, nw_ref, do_ref, dya_ref, dyb_ref, dnw_ref):
        ov = o_ref[...]
        _, r = _rms_fwd(ov, nw_ref[...])
        do, dnw = _rms_bwd(ov, r, nw_ref[...], dx1_ref[...])
        dob = do.astype(BF16)
        do_ref[...] = dob
        dya_ref[...] = _dot_nt(dob, wo_ref[:GM_WIDTH, :])
        dyb_ref[...] = _dot_nt(dob, wo_ref[GM_WIDTH:, :])
        dnw_ref[...] += dnw

    return _rows_call("outproj_bwd", body, tm, [dx1, o], [w_out, nw],
                      [_sds((n_tok, D_MODEL), BF16), _sds((n_tok, GM_WIDTH), F32), _sds((n_tok, SSM_WIDTH), F32)],
                      [_sds((1, D_MODEL), F32)], carried=carried)


def _gmlp_bwd(p_uv, dya, lnw, lnb, e_bf, et_bf, w_cat, w_stack, bmap, carried=None):
    n_tok = p_uv.shape[0]
    chunks_per_step = 4

    def body(puv_ref, dya_ref, lnw_ref, lnb_ref, e_ref, et_ref, wcat_ref, wstack_ref, bmap_ref,
             dpuv_ref, dws_ref, dbs_ref, dlnw_ref, dlnb_ref, wm_scr, wsm_scr):
        t_stk = lax.broadcasted_iota(jnp.int32, (N_HEADS * CHUNK, CHUNK), 0) % CHUNK
        s_stk = lax.broadcasted_iota(jnp.int32, (N_HEADS * CHUNK, CHUNK), 1)

        @pl.when(pl.program_id(0) == 0)
        def _():
            wm_scr[...] = _causal_w_cat(wcat_ref[...])
            wsm_scr[...] = jnp.where(t_stk >= s_stk, wstack_ref[...], 0.0).astype(BF16)

        lnw_v = lnw_ref[...]
        e_v, et_v = e_ref[...], et_ref[...]

        def one_chunk(rows):
            u, v, gu, tu, tv, rstd, xhat, vn = _gmlp_common(puv_ref[rows, :], lnw_v, lnb_ref[...], e_v, et_v)
            vnb = vn.astype(BF16)
            mixed = jnp.dot(wm_scr[...], _head_blocks(vnb), preferred_element_type=F32) + bmap_ref[...]
            dy = dya_ref[rows, :]
            du = dy * mixed * _gelu_grad(u, tu)
            dmixed = dy * gu
            (dbs,) = _seg_dots([dmixed], et_v)
            dblocks = _head_blocks(dmixed.astype(BF16))
            dvn = lax.dot_general(wsm_scr[...], dblocks, (((0,), (0,)), ((), ())), preferred_element_type=F32)
            dws = lax.dot_general(dblocks, vnb, (((1,), (1,)), ((), ())), preferred_element_type=F32)
            dxh = dvn * lnw_v
            m1, m2 = _seg_dots([dxh, dxh * xhat], et_v)
            m1, m2 = _seg_dots([m1 * (1.0 / HEAD_DIM), m2 * (1.0 / HEAD_DIM)], e_v)
            dgv = rstd * (dxh - m1 - xhat * m2)
            dv = dgv * _gelu_grad(v, tv)
            dpuv_ref[rows, :GM_WIDTH] = du.astype(BF16)
            dpuv_ref[rows, GM_WIDTH:] = dv.astype(BF16)
            return dbs, dws, jnp.sum(dvn * xhat, axis=0, keepdims=True), jnp.sum(dvn, axis=0, keepdims=True)

        parts = [one_chunk(slice(k * CHUNK, (k + 1) * CHUNK)) for k in range(chunks_per_step)]
        dbs, dws, dlnw, dlnb = [functools.reduce(lambda a, b: a + b, vals) for vals in zip(*parts)]
        dbs_ref[...] += dbs
        dws_ref[...] += jnp.where(t_stk >= s_stk, dws, 0.0)
        dlnw_ref[...] += dlnw
        dlnb_ref[...] += dlnb

    return _rows_call(
        "gmlp_bwd", body, chunks_per_step * CHUNK, [p_uv, dya], [lnw, lnb, e_bf, et_bf, w_cat, w_stack, bmap],
        [_sds((n_tok, 2 * GM_WIDTH), BF16)],
        [_sds((N_HEADS * CHUNK, CHUNK), F32), _sds((CHUNK, DT_PAD), F32), _sds((1, GM_WIDTH), F32),
         _sds((1, GM_WIDTH), F32)],
        scratch=[pltpu.VMEM((CHUNK, N_HEADS * CHUNK), BF16), pltpu.VMEM((N_HEADS * CHUNK, CHUNK), BF16)],
        carried=carried)


def _ssd_bwd(p_xbc, p_z, p_dt, yssd, sprev, dyb, conv_w, conv_b, dt_bias, a_log, dskip_map, norm_w, e_bf, et_bf, n_seq,
             carried=None):
    n_tok = p_xbc.shape[0]
    nc = n_tok // n_seq // CHUNK

    def body(xr3, xprev3, z3, pdt3, yssd3, sprev3, dyb3,
             cw_ref, cb_ref, dtb_ref, alog_ref, dsk_ref, nw_ref, e_ref, et_ref,
             dpxbc3, dpz3, dpdt3, dcw_ref, dcb_ref, ddtb_ref, dalog_ref, ddsk_ref, dnw_ref,
             ds3_scr, nxt3_scr, dxa3_scr):
        @pl.when(pl.program_id(0) == 0)
        def _():
            for a in (dcw_ref, dcb_ref, ddtb_ref, dalog_ref, ddsk_ref, dnw_ref, ds3_scr, nxt3_scr):
                a[...] = jnp.zeros_like(a)

        for b in range(n_seq):
            one_sequence(xr3.at[b], xprev3.at[b], z3.at[b], pdt3.at[b], yssd3.at[b], sprev3.at[b], dyb3.at[b],
                         cw_ref, cb_ref, dtb_ref, alog_ref, dsk_ref, nw_ref, e_ref, et_ref,
                         dpxbc3.at[b], dpz3.at[b], dpdt3.at[b], dcw_ref, dcb_ref, ddtb_ref, dalog_ref, ddsk_ref, dnw_ref,
                         ds3_scr.at[b], nxt3_scr.at[b], dxa3_scr.at[b])

    def one_sequence(xr_ref, xprev_ref, z_ref, pdt_ref, yssd_ref, sprev_ref, dyb_ref,
                     cw_ref, cb_ref, dtb_ref, alog_ref, dsk_ref, nw_ref, e_ref, et_ref,
                     dpxbc_ref, dpz_ref, dpdt_ref, dcw_ref, dcb_ref, ddtb_ref, dalog_ref, ddsk_ref, dnw_ref,
                     ds_scr, nxt_scr, dxa_scr):
        chunk = nc - 1 - pl.program_id(0)
        xr = xr_ref[...]
        prev = jnp.where(chunk == 0, 0.0, xprev_ref[...])
        et_v = et_ref[...]
        p = _ssd_pre(xr, prev, cw_ref, cb_ref[...], pdt_ref[...], dtb_ref[...], alog_ref[...], e_ref[...])
        last, e_exp, dte, cd = _ssd_maps(p)
        rowi = p["rowi"]
        xs = p["xa"][:, :SSM_WIDTH]
        xd = xs * p["dt_map"]
        a_cs_t = p["a_cs"].T
        tri = _tril_mask()
        dsk = dsk_ref[...]
        nw_v = nw_ref[...]

        yv = yssd_ref[...]
        zv = z_ref[...]
        sz, zg, yg, _, rs = _gate_fwd(yv, zv, nw_v)
        dout = dyb_ref[...]
        for g in range(SSM_GROUPS):
            gs = slice(g * GROUP_W, (g + 1) * GROUP_W)
            dyg_g, dnw_g = _rms_bwd(yg[:, gs], rs[g], nw_v[:, gs], dout[:, gs])
            dnw_ref[:, gs] += dnw_g
            dxa_scr[:, gs] = dyg_g
        dyg = dxa_scr[:, :SSM_WIDTH]
        d_y = dyg * zg
        dpz_ref[...] = (dyg * yv * (sz + zv * sz * (1.0 - sz))).astype(BF16)

        s_prev = sprev_ref[...]
        ds_next = ds_scr[...]
        lane_dt = lax.broadcasted_iota(jnp.int32, (1, DT_PAD), 1)
        da_cols = jnp.zeros((CHUNK, DT_PAD), F32)
        for g in range(SSM_GROUPS):
            gs = slice(g * GROUP_W, (g + 1) * GROUP_W)
            b_off = SSM_WIDTH + g * SSM_STATE
            c_off = SSM_WIDTH + (SSM_GROUPS + g) * SSM_STATE
            bm = p["xa"][:, b_off:b_off + SSM_STATE].astype(BF16)
            cm = p["xa"][:, c_off:c_off + SSM_STATE].astype(BF16)
            cb_mat = _dot_nt(cm, bm)
            d_yg = d_y[:, gs]
            d_ygb = d_yg.astype(BF16)
            xdg = xd[:, gs]
            xdgb = xdg.astype(BF16)
            ds_g = ds_next[:, gs]
            sp_g = s_prev[:, gs]
            bds = _dot(bm, ds_g)
            dcs = d_yg * e_exp[:, gs]
            d_c = _dot_nt(dcs, sp_g)
            ds_scr[:, gs] = cd[:, gs] * ds_g + _dot_tn(cm, dcs)
            d_b = _dot_nt(xdg * dte[:, gs], ds_g)
            dxd_g = bds * dte[:, gs]
            sum_dcb = jnp.zeros((CHUNK, CHUNK), F32)
            for r in range(SSM_GROUPS * 2):
                head = g * 4 + r
                mask = _head_lane_mask(GROUP_W, r)
                dm = _head_decay(p["a_cs"], a_cs_t, head, tri)
                m_mat = cb_mat * dm
                g_mat = _dot_nt(jnp.where(mask, d_yg, 0.0), xdgb)
                w_mat = g_mat * m_mat
                sum_dcb = sum_dcb + g_mat * dm
                dxd_g = dxd_g + jnp.where(mask, _dot_tn(m_mat, d_ygb), 0.0)
                da_h = jnp.sum(w_mat - w_mat.T, axis=1, keepdims=True)
                da_cols = da_cols + jnp.where(lane_dt == head, da_h, 0.0)
            d_c = d_c + _dot(sum_dcb, bm)
            d_b = d_b + _dot_tn(sum_dcb, cm)
            dxa_scr[:, b_off:b_off + SSM_STATE] = d_b
            dxa_scr[:, c_off:c_off + SSM_STATE] = d_c
            y_off_g = _dot(cm, sp_g) * e_exp[:, gs]
            t3 = bds * xdg * dte[:, gs]
            tail = jnp.sum(t3, axis=0, keepdims=True) + jnp.sum(ds_g * sp_g, axis=0, keepdims=True) * cd[:, gs]
            pre_g = d_yg * y_off_g - t3 + jnp.where(last, tail, 0.0)
            s_pre, ddt_g, s_dsk = _seg_dots([pre_g, dxd_g * xs[:, gs], d_yg * xs[:, gs]], et_v[gs, :])
            da_cols = da_cols + s_pre
            ddsk_ref[...] += jnp.sum(s_dsk, axis=0, keepdims=True)
            dxa_scr[:, gs] = dxd_g * p["dt_map"][:, gs] + dsk[:, gs] * d_yg
            if g == 0:
                ddt = ddt_g
            else:
                ddt = ddt + ddt_g
        r_i = lax.broadcasted_iota(jnp.int32, (CHUNK, CHUNK), 0)
        c_i = lax.broadcasted_iota(jnp.int32, (CHUNK, CHUNK), 1)
        ddta = _tri_dot(r_i <= c_i, da_cols, terms=2)
        ddt = ddt + ddta * p["a_neg"]
        dalog_ref[...] += jnp.sum(ddta * p["dt"], axis=0, keepdims=True) * p["a_neg"]
        draw = ddt * _sigmoid(p["pre"])
        ddtb_ref[...] += jnp.sum(draw, axis=0, keepdims=True)
        dpdt_ref[...] = draw.astype(BF16)

        xc = p["xc"]
        sg = p["sg"]
        dxc = dxa_scr[...] * (sg + xc * sg * (1.0 - sg))
        dcb_ref[...] += jnp.sum(dxc, axis=0, keepdims=True)
        for k in range(CONV_K):
            dcw_ref[k] += jnp.sum(dxc * p["shifted"][k], axis=0, keepdims=True)
        nxt = nxt_scr[...]
        dxr = cw_ref[3] * dxc
        for s in range(1, CONV_K):
            dxr = dxr + cw_ref[CONV_K - 1 - s] * _shift_up(dxc, nxt, s)
        dpxbc_ref[...] = dxr.astype(BF16)
        nxt_scr[...] = dxc[:SUBLANES, :]

    seq_len = n_tok // n_seq

    def rows(width):
        return pl.BlockSpec((n_seq, CHUNK, width), lambda s: (0, nc - 1 - s, 0))

    tiles = CHUNK // SUBLANES
    prev_rows = pl.BlockSpec((n_seq, SUBLANES, CONV_CH), lambda s: (0, jnp.maximum((nc - 1 - s) * tiles - 1, 0), 0))

    def whole(shape):
        nd = len(shape)
        return pl.BlockSpec(tuple(shape), lambda s: (0,) * nd)

    def by_seq(a):
        return a.reshape(n_seq, seq_len, a.shape[-1])

    acc_shapes = [(CONV_K, 1, CONV_CH), (1, CONV_CH), (1, DT_PAD), (1, DT_PAD), (1, DT_PAD), (1, SSM_WIDTH)]
    xbc3 = by_seq(p_xbc)
    outs = _call_carrying(
        body, carried, name="ssd_bwd", grid=(nc,),
        in_specs=[rows(CONV_CH), prev_rows, rows(SSM_WIDTH), rows(DT_PAD), rows(SSM_WIDTH), rows(SSM_WIDTH),
                  rows(SSM_WIDTH)] + _ssd_const_specs(),
        out_specs=[rows(CONV_CH), rows(SSM_WIDTH), rows(DT_PAD)] + [whole(s) for s in acc_shapes],
        out_shape=tuple([_sds((n_seq, seq_len, CONV_CH), BF16), _sds((n_seq, seq_len, SSM_WIDTH), BF16),
                         _sds((n_seq, seq_len, DT_PAD), BF16)] + [_sds(s, F32) for s in acc_shapes]),
        scratch_shapes=[pltpu.VMEM((n_seq, SSM_STATE, SSM_WIDTH), F32), pltpu.VMEM((n_seq, SUBLANES, CONV_CH), F32),
                        pltpu.VMEM((n_seq, CHUNK, CONV_CH), F32)],
        operands=[xbc3, xbc3, by_seq(p_z), by_seq(p_dt), by_seq(yssd), by_seq(sprev), by_seq(dyb), conv_w, conv_b, dt_bias,
                  a_log, dskip_map, norm_w, e_bf, et_bf])
    return tuple(o.reshape(n_tok, o.shape[-1]) for o in outs[:3]) + tuple(outs[3:])


def _inproj_bwd(dp_uv, dp_xbc, dp_z, dp_dt, x, dx1, w_uv, w_xbc, w_z, w_dt, nw, tm=256, carried=None):
    n_tok = x.shape[0]

    def body(duv_ref, dxbc_ref, dz_ref, ddt_ref, x_ref, dx1_ref, wuv_ref, wxbc_ref, wz_ref, wdt_ref, nw_ref,
             gx_ref, h_ref, dnw_ref):
        dh = _dot_nt(duv_ref[...], wuv_ref[...]) + _dot_nt(dxbc_ref[...], wxbc_ref[...])
        dh = dh + _dot_nt(dz_ref[...], wz_ref[...]) + _dot_nt(ddt_ref[...], wdt_ref[...])
        xv = x_ref[...]
        h, r = _rms_fwd(xv, nw_ref[...])
        dx, dnw = _rms_bwd(xv, r, nw_ref[...], dh)
        gx_ref[...] = dx1_ref[...] + dx
        h_ref[...] = h.astype(BF16)
        dnw_ref[...] += dnw

    return _rows_call("inproj_bwd", body, tm, [dp_uv, dp_xbc, dp_z, dp_dt, x, dx1], [w_uv, w_xbc, w_z, w_dt, nw],
                      [_sds((n_tok, D_MODEL), F32), _sds((n_tok, D_MODEL), BF16)], [_sds((1, D_MODEL), F32)],
                      carried=carried)


def _const_maps():
    lane = jnp.arange(SSM_WIDTH) // HEAD_DIM
    e_bf = (jnp.arange(DT_PAD)[:, None] == lane[None, :]).astype(BF16)
    return e_bf, e_bf.T


def _pad_lanes(v, width):
    return jnp.pad(v, ((0, 0), (0, width - v.shape[1])))


SHARD_COLS = IN_COLS // N_CHIPS
_UV_END = 2 * GM_WIDTH
_Z_END = _UV_END + SSM_WIDTH
_XBC_END = _Z_END + CONV_CH


def _cols_from_shards(w4, lo, hi):
    pieces = []
    for j in range(N_CHIPS):
        a, b = max(lo, j * SHARD_COLS), min(hi, (j + 1) * SHARD_COLS)
        if a < b:
            pieces.append(w4[j][:, a - j * SHARD_COLS:b - j * SHARD_COLS])
    return pieces[0] if len(pieces) == 1 else jnp.concatenate(pieces, axis=1)


def _shards_from_cols(blocks):
    shards = []
    for j in range(N_CHIPS):
        pieces = []
        for arr, lo, hi in blocks:
            a, b = max(lo, j * SHARD_COLS), min(hi, (j + 1) * SHARD_COLS)
            if a < b:
                pieces.append(arr[:, a - lo:b - lo])
        shards.append(pieces[0] if len(pieces) == 1 else jnp.concatenate(pieces, axis=1))
    return jnp.stack(shards)


def _forward_backward(x, tgt, w_in4, conv_w, small, out_shard, up_shard, down_shard, core, adam_args):
    n_seq, seq_len, _ = x.shape
    n_tok = n_seq * seq_len
    x2 = x.reshape(n_tok, D_MODEL)
    tgt2 = tgt.reshape(n_tok, D_MODEL)
    e_bf, et_bf = _const_maps()

    w_uv = _cols_from_shards(w_in4, 0, _UV_END)
    w_z = _cols_from_shards(w_in4, _UV_END, _Z_END)
    w_xbc = _cols_from_shards(w_in4, _Z_END, _XBC_END)
    w_dt = _pad_lanes(_cols_from_shards(w_in4, _XBC_END, IN_COLS), DT_PAD)

    nw_pre = small["norm_mix_pre"]
    lnw = small["gm_ln_w"].reshape(1, GM_WIDTH)
    lnb = small["gm_ln_b"].reshape(1, GM_WIDTH)
    w_stack = small["gm_w_s"].reshape(N_HEADS * CHUNK, CHUNK)
    w_cat = jnp.transpose(small["gm_w_s"], (1, 0, 2)).reshape(CHUNK, N_HEADS * CHUNK)
    bmap = jnp.repeat(small["gm_b_s"].T, HEAD_DIM, axis=1)
    cw3 = conv_w.reshape(CONV_K, 1, CONV_CH)
    conv_b = small["conv_b"]
    dt_bias = _pad_lanes(small["dt_bias"], DT_PAD)
    a_log = _pad_lanes(small["a_log"], DT_PAD)
    dskip_map = jnp.repeat(small["d_skip"], HEAD_DIM, axis=1)
    ssm_nw = small["ssm_norm_w"]

    half = down_shard.shape[0] // 2
    p_uv, p_xbc, p_z, p_dt, w_out4, w_down_a = _inproj_fwd(
        x2, nw_pre, w_uv, w_xbc, w_z, w_dt, carried=_allgather_exchange([out_shard, down_shard[:half]]))
    ssd_consts = (cw3, conv_b, dt_bias, a_log, dskip_map, ssm_nw, e_bf, et_bf)
    ya, yb, yssd, sprev, w_up4, w_down_b = _mixer_fwd(
        p_uv, p_xbc, p_z, p_dt, lnw, lnb, w_cat, bmap, *ssd_consts, n_seq,
        carried=_allgather_exchange([up_shard, down_shard[half:]]))
    w_out_b = w_out4.reshape(D_MODEL, D_MODEL)
    o, x1, h2 = _outproj_fwd(ya, yb, x2, w_out_b, small["norm_mix_post"], small["norm_ffn_pre"])
    f, dd, dy, loss_acc, d_nffn_post = _mlp_fwd(h2, x1, tgt2, w_up4, w_down_a, w_down_b, small["norm_ffn_post"])

    dup, dx1, d_nffn_pre = _mlp_bwd(dd, f, x1, dy, w_down_a, w_down_b, w_up4, small["norm_ffn_pre"])
    tk = min(DW_TOKENS_PER_STEP, n_tok)
    g_up = _matmul_tn("dw_up", h2, dup, D_MODEL, D_MODEL, tk, stacked=True)
    g_down = _matmul_tn("dw_down", f, dd, 1024, D_MODEL, tk).reshape(N_CHIPS, D_FF // N_CHIPS, D_MODEL)
    do, dya, dyb, d_nmix_post, got_up, got_down = _outproj_bwd(
        dx1, o, w_out_b, small["norm_mix_post"], carried=_pair_exchange([g_up, g_down]))
    h_up = _pair_sum(core, g_up, got_up, 256)
    h_down = _pair_sum(core, g_down, got_down, 256)
    g_out_a = _matmul_tn("dw_out_a", ya, do, GM_WIDTH, D_MODEL, tk)
    g_out_b = _matmul_tn("dw_out_b", yb, do, SSM_WIDTH, D_MODEL, tk)
    g_out = jnp.concatenate([g_out_a, g_out_b], axis=0).reshape(N_CHIPS, D_MODEL // N_CHIPS, D_MODEL)
    dp_uv, d_ws, d_bs_t, d_lnw, d_lnb, slab_up, got_out = _gmlp_bwd(
        p_uv, dya, lnw, lnb, e_bf, et_bf, w_cat, w_stack, bmap,
        carried=_both(_chip_exchange([h_up]), _pair_exchange([g_out])))
    h_out = _pair_sum(core, g_out, got_out, 128)
    early = {
        "gm_ln_w": d_lnw.reshape(N_HEADS, HEAD_DIM), "gm_ln_b": d_lnb.reshape(N_HEADS, HEAD_DIM),
        "gm_w_s": d_ws.reshape(N_HEADS, CHUNK, CHUNK), "gm_b_s": d_bs_t[:, :N_HEADS].T,
        "norm_mix_post": d_nmix_post, "norm_ffn_pre": d_nffn_pre, "norm_ffn_post": d_nffn_post,
    }
    packed_early = _pack(early, tuple(early), tail=loss_acc[0, 0].reshape(1))
    (dp_xbc, dp_z, dp_dt, d_cw, d_cb, d_dtb, d_alog, d_dsk, d_ssm_nw, slab_down, slab_out, all_early) = _ssd_bwd(
        p_xbc, p_z, p_dt, yssd, sprev, dyb, *ssd_consts, n_seq,
        carried=_both(_chip_exchange([h_down, h_out]), _device_gather_exchange(packed_early)))
    gx, h, d_nmix_pre = _inproj_bwd(dp_uv, dp_xbc, dp_z, dp_dt, x2, dx1, w_uv, w_xbc, w_z, w_dt, nw_pre)
    late = {
        "norm_mix_pre": d_nmix_pre, "conv_w": d_cw.reshape(CONV_K, CONV_CH), "conv_b": d_cb,
        "dt_bias": d_dtb[:, :N_HEADS], "a_log": d_alog[:, :N_HEADS], "d_skip": d_dsk[:, :N_HEADS],
        "ssm_norm_w": d_ssm_nw,
    }
    g_uv, all_late = _matmul_tn("dw_in_uv", h, dp_uv, D_MODEL, 2 * GM_WIDTH, tk,
                                carried=_device_gather_exchange(_pack(late, tuple(late))))
    sum_early = _ordered_sum("small_sum_early", all_early)
    small_sum = _unpack(sum_early, {n: v.shape for n, v in early.items()}, tuple(early))
    small_sum.update(_unpack(_ordered_sum("small_sum_late", all_late), {n: v.shape for n, v in late.items()}, tuple(late)))
    loss = sum_early.reshape(-1)[sum(v.size for v in early.values())]
    red_up, red_down, red_out = _chip_sum(slab_up, 256), _chip_sum(slab_down, 256), _chip_sum(slab_out, 128)
    g_xbc, oth_up, oth_down, oth_out = _matmul_tn("dw_in_xbc", h, dp_xbc, D_MODEL, CONV_CH, tk,
                                                  carried=_pair_swap([red_up, red_down, red_out]))
    g_z = _matmul_tn("dw_in_z", h, dp_z, D_MODEL, SSM_WIDTH, tk)
    g_dt = _matmul_tn("dw_in_dt", h, dp_dt, D_MODEL, DT_PAD, tk)

    g_in = _shards_from_cols([(g_uv, 0, _UV_END), (g_z, _UV_END, _Z_END), (g_xbc, _Z_END, _XBC_END),
                              (g_dt, _XBC_END, IN_COLS)])
    (got_in,) = _run_exchange("grad_pair_exchange_in", _pair_exchange([g_in]))
    h_in = _pair_sum(core, g_in, got_in, 256)
    (slab_in,) = _run_exchange("grad_chip_exchange", _chip_exchange([h_in]))
    res = _adamw_halves("adamw_mlp", [(adam_args["w_up"][0], red_up, oth_up) + adam_args["w_up"][1:],
                                      (adam_args["w_down"][0], red_down, oth_down) + adam_args["w_down"][1:]], 256)
    big_out = {"w_up": res[0:4], "w_down": res[4:8]}
    big_out["w_out"] = _adamw_halves("adamw_w_out", [(adam_args["w_out"][0], red_out, oth_out) + adam_args["w_out"][1:]], 128)
    red_in = _chip_sum(slab_in, 256)
    (oth_in,) = _run_exchange("grad_pair_swap_in", _pair_swap([red_in]))
    big_out["w_in"] = _adamw_halves("adamw_w_in", [(adam_args["w_in"][0], red_in, oth_in) + adam_args["w_in"][1:]], 256)

    return loss, gx.reshape(x.shape), big_out, small_sum


_HBM = pl.BlockSpec(memory_space=pltpu.HBM)


D2D_CHUNKS = 8
ICI_CHUNKS = 1
ROW_ALIGN = 16


def _row_chunks(rows, n_chunks):
    size = min(max(rows // n_chunks, ROW_ALIGN), rows)
    assert rows % size == 0
    return [(start, size) for start in range(0, rows, size)]


def _position():
    x, y, c = lax.axis_index("x"), lax.axis_index("y"), lax.axis_index("c")
    chips = [(1 - x, y), (x, 1 - y), (1 - x, 1 - y)]
    return x, y, c, chips


def _allgather_exchange(arrs):
    n = len(arrs)

    def copies(ins, outs, send_sems, recv_sems, local_sems):
        x, y, c, chips = _position()
        me = 2 * x + y
        sibling = (x, y, 1 - c)

        def copy(a, k, src, dst, to):
            return pltpu.make_async_remote_copy(src_ref=src, dst_ref=dst, send_sem=send_sems.at[a, k],
                                                recv_sem=recv_sems.at[a, k], device_id=to, device_id_type=MESH)

        def half_rows(a, pc):
            half = ins[a].shape[0] // 2
            return pl.ds(pc * half, half)

        local = [pltpu.make_async_copy(ins[a], outs[a].at[me], local_sems.at[a]) for a in range(n)]
        ici_out = [[copy(a, k, ins[a].at[half_rows(a, c)], outs[a].at[me, half_rows(a, c)], (px, py, c))
                    for k, (px, py) in enumerate(chips)] for a in range(n)]
        return c, chips, sibling, copy, half_rows, local, ici_out

    def start(ins, outs, send_sems, recv_sems, local_sems):
        c, chips, _, copy, _, local, _ = copies(ins, outs, send_sems, recv_sems, local_sems)
        x, y, _, _ = _position()
        me = 2 * x + y
        for cp in local:
            cp.start()
        for a in range(n):
            half = ins[a].shape[0] // 2
            for k, (px, py) in enumerate(chips):
                for first, size in _row_chunks(half, ICI_CHUNKS):
                    rows = pl.ds(c * half + first, size)
                    copy(a, k, ins[a].at[rows], outs[a].at[me, rows], (px, py, c)).start()

    def finish(ins, outs, send_sems, recv_sems, local_sems):
        c, chips, sibling, copy, half_rows, local, ici_out = copies(ins, outs, send_sems, recv_sems, local_sems)
        for a in range(n):
            half = ins[a].shape[0] // 2
            for k, (px, py) in enumerate(chips):
                blk = outs[a].at[2 * px + py, half_rows(a, c)]
                copy(a, k, blk, blk, (px, py, c)).wait_recv()
                for first, size in _row_chunks(half, D2D_CHUNKS):
                    piece = outs[a].at[2 * px + py, pl.ds(c * half + first, size)]
                    copy(a, 3 + k, piece, piece, sibling).start()
        for a in range(n):
            for k, (px, py) in enumerate(chips):
                theirs = outs[a].at[2 * px + py, half_rows(a, 1 - c)]
                copy(a, 3 + k, theirs, theirs, sibling).wait_recv()
                mine = outs[a].at[2 * px + py, half_rows(a, c)]
                copy(a, 3 + k, mine, mine, sibling).wait_send()
        for a in range(n):
            for cp in ici_out[a]:
                cp.wait_send()
        for cp in local:
            cp.wait()

    return _Carried(arrs, [_sds((N_CHIPS,) + a.shape, a.dtype) for a in arrs],
                    [pltpu.SemaphoreType.DMA((n, 6)), pltpu.SemaphoreType.DMA((n, 6)), pltpu.SemaphoreType.DMA((n,))],
                    start, finish)


def _run_exchange(name, exchange):
    n_in, n_out = len(exchange.ins), len(exchange.out_shapes)

    def body(*refs):
        ins, outs, sems = refs[:n_in], refs[n_in:n_in + n_out], refs[n_in + n_out:]
        exchange.start(ins, outs, *sems)
        exchange.finish(ins, outs, *sems)

    return pl.pallas_call(
        body, name=name, out_shape=tuple(exchange.out_shapes), in_specs=[_HBM] * n_in,
        out_specs=tuple([_HBM] * n_out), scratch_shapes=exchange.sems,
    )(*exchange.ins)


def _pair_exchange(grads):
    n = len(grads)

    def copier(send_sems, recv_sems):
        x, y, c, _ = _position()

        def copy(a, src, dst):
            return pltpu.make_async_remote_copy(src_ref=src, dst_ref=dst, send_sem=send_sems.at[a],
                                                recv_sem=recv_sems.at[a], device_id=(x, y, 1 - c), device_id_type=MESH)
        return c, copy

    def start(ins, got, send_sems, recv_sems):
        c, copy = copier(send_sems, recv_sems)
        for a in range(n):
            half = ins[a].shape[1] // 2
            for slab in range(N_CHIPS):
                for first, size in _row_chunks(half, D2D_CHUNKS):
                    copy(a, ins[a].at[slab, pl.ds((1 - c) * half + first, size), :],
                         got[a].at[slab, pl.ds(first, size), :]).start()

    def finish(ins, got, send_sems, recv_sems):
        c, copy = copier(send_sems, recv_sems)
        for a in range(n):
            half = ins[a].shape[1] // 2
            copy(a, ins[a].at[:, pl.ds((1 - c) * half, half), :], got[a]).wait()

    return _Carried(grads, [_sds((N_CHIPS, g.shape[1] // 2, g.shape[2]), g.dtype) for g in grads],
                    [pltpu.SemaphoreType.DMA((n,)), pltpu.SemaphoreType.DMA((n,))], start, finish)


def _chip_exchange(hsums):
    n = len(hsums)

    def copies(ins, outs, send_sems, recv_sems, local_sems, pieces):
        x, y, c, chips = _position()
        me = 2 * x + y
        cps = []
        for a in range(n):
            cps.append(pltpu.make_async_copy(ins[a].at[me], outs[a].at[me], local_sems.at[a]))
            rows = ins[a].shape[1]
            for k, (px, py) in enumerate(chips):
                for first, size in (_row_chunks(rows, ICI_CHUNKS) if pieces else [(0, rows)]):
                    cps.append(pltpu.make_async_remote_copy(
                        src_ref=ins[a].at[2 * px + py, pl.ds(first, size)], dst_ref=outs[a].at[me, pl.ds(first, size)],
                        send_sem=send_sems.at[a, k], recv_sem=recv_sems.at[a, k], device_id=(px, py, c),
                        device_id_type=MESH))
        return cps

    def start(*refs):
        for cp in copies(*refs, pieces=True):
            cp.start()

    def finish(*refs):
        for cp in copies(*refs, pieces=False):
            cp.wait()

    return _Carried(hsums, [_sds(h.shape, h.dtype) for h in hsums],
                    [pltpu.SemaphoreType.DMA((n, 3)), pltpu.SemaphoreType.DMA((n, 3)), pltpu.SemaphoreType.DMA((n,))],
                    start, finish)


def _pair_swap(reds):
    n = len(reds)

    def copier(send_sems, recv_sems):
        x, y, c, _ = _position()

        def copy(a, src, dst):
            return pltpu.make_async_remote_copy(src_ref=src, dst_ref=dst, send_sem=send_sems.at[a],
                                                recv_sem=recv_sems.at[a], device_id=(x, y, 1 - c), device_id_type=MESH)
        return copy

    def start(ins, outs, send_sems, recv_sems):
        copy = copier(send_sems, recv_sems)
        for a in range(n):
            for first, size in _row_chunks(ins[a].shape[0], 2 * D2D_CHUNKS):
                copy(a, ins[a].at[pl.ds(first, size), :], outs[a].at[pl.ds(first, size), :]).start()

    def finish(ins, outs, send_sems, recv_sems):
        copy = copier(send_sems, recv_sems)
        for a in range(n):
            copy(a, ins[a], outs[a]).wait()

    return _Carried(reds, [_sds(r.shape, r.dtype) for r in reds],
                    [pltpu.SemaphoreType.DMA((n,)), pltpu.SemaphoreType.DMA((n,))], start, finish)


def _device_gather_exchange(packed):
    def copies(ins, outs, send_sems, recv_sems, local_sem):
        (x_ref,), (all_ref,) = ins, outs
        x, y, c, chips = _position()
        me, sibling = (x, y, c), (x, y, 1 - c)

        def slab(px, py, pc):
            return all_ref.at[4 * px + 2 * py + pc]

        def copy(k, block, to, src=None):
            return pltpu.make_async_remote_copy(
                src_ref=slab(*block) if src is None else src, dst_ref=slab(*block), send_sem=send_sems.at[k],
                recv_sem=recv_sems.at[k], device_id=to, device_id_type=MESH)

        mine = pltpu.make_async_copy(x_ref, slab(*me), local_sem)
        first = [copy(0, me, sibling, src=x_ref)]
        first += [copy(1 + j, me, (*chip, c), src=x_ref) for j, chip in enumerate(chips)]
        passed = [copy(4 + j, (*chip, c), sibling) for j, chip in enumerate(chips)]
        return c, chips, me, sibling, copy, mine, first, passed

    def start(ins, outs, send_sems, recv_sems, local_sem):
        _, _, _, _, _, mine, first, _ = copies(ins, outs, send_sems, recv_sems, local_sem)
        mine.start()
        for cp in first:
            cp.start()

    def finish(ins, outs, send_sems, recv_sems, local_sem):
        c, chips, me, sibling, copy, mine, first, passed = copies(ins, outs, send_sems, recv_sems, local_sem)
        for j, chip in enumerate(chips):
            copy(1 + j, (*chip, c), me).wait_recv()
            passed[j].start()
        copy(0, sibling, me).wait_recv()
        for j, chip in enumerate(chips):
            copy(4 + j, (*chip, 1 - c), me).wait_recv()
        for cp in first + passed:
            cp.wait_send()
        mine.wait()

    return _Carried([packed], [_sds((N_DEV,) + packed.shape, F32)],
                    [pltpu.SemaphoreType.DMA((7,)), pltpu.SemaphoreType.DMA((7,)), pltpu.SemaphoreType.DMA],
                    start, finish)


def _ordered_sum(name, slabs):
    _, m_per, n_cols = slabs.shape

    def body(s_ref, o_ref):
        acc = s_ref[0]
        for d in range(1, N_DEV):
            acc = acc + s_ref[d]
        o_ref[...] = acc

    vmem = pl.BlockSpec(memory_space=pltpu.VMEM)
    return pl.pallas_call(body, name=name, out_shape=_sds((m_per, n_cols), F32), in_specs=[vmem], out_specs=vmem)(slabs)


def _pair_sum(core, own, got, tm):
    _, half, cols = got.shape
    nb = half // tm

    def body(c_ref, a_ref, b_ref, o_ref):
        o_ref[...] = (a_ref[...].astype(F32) + b_ref[...].astype(F32)).astype(BF16)

    return pl.pallas_call(
        body, name="grad_pair_sum", out_shape=_sds(got.shape, BF16),
        grid_spec=pltpu.PrefetchScalarGridSpec(
            num_scalar_prefetch=1, grid=(N_CHIPS, nb),
            in_specs=[pl.BlockSpec((None, tm, cols), lambda s, i, c_ref: (s, c_ref[0] * nb + i, 0)),
                      pl.BlockSpec((None, tm, cols), lambda s, i, c_ref: (s, i, 0))],
            out_specs=pl.BlockSpec((None, tm, cols), lambda s, i, c_ref: (s, i, 0))),
        compiler_params=_cparams(2),
    )(core, own, got)


def _chip_sum(slabs, tm):
    _, half, cols = slabs.shape

    def body(s_ref, o_ref):
        acc = s_ref[0].astype(F32)
        for k in range(1, N_CHIPS):
            acc = acc + s_ref[k].astype(F32)
        o_ref[...] = acc

    return pl.pallas_call(
        body, name="grad_chip_sum", out_shape=_sds((half, cols), F32), grid=(half // tm,),
        in_specs=[pl.BlockSpec((N_CHIPS, tm, cols), lambda i: (0, i, 0))],
        out_specs=pl.BlockSpec((tm, cols), lambda i: (i, 0)), compiler_params=_cparams(1),
    )(slabs)


def _adam_math(w, g, m, v):
    m2 = ADAM_B1 * m + (1.0 - ADAM_B1) * g
    v2 = ADAM_B2 * v + (1.0 - ADAM_B2) * (g * g)
    m_hat = m2 / (1.0 - ADAM_B1 ** ADAM_STEP)
    v_hat = v2 / (1.0 - ADAM_B2 ** ADAM_STEP)
    delta = -ADAM_LR * (m_hat / (jnp.sqrt(v_hat) + ADAM_EPS) + ADAM_WD * w)
    return delta, m2, v2


def _adamw_halves(name, items, tm, carried=None):
    rows, cols = items[0][0].shape
    nb = rows // 2 // tm
    n = len(items)

    def body(*refs):
        mine = (pl.program_id(0) // nb) == lax.axis_index("c")
        for k in range(n):
            w_ref, own_ref, oth_ref, m_ref, v_ref = refs[5 * k:5 * k + 5]
            g_ref, d_ref, m2_ref, v2_ref = refs[5 * n + 4 * k:5 * n + 4 * k + 4]
            g = jnp.where(mine, own_ref[...], oth_ref[...])
            d, m2, v2 = _adam_math(w_ref[...], g, m_ref[...], v_ref[...])
            g_ref[...] = g
            d_ref[...] = d
            m2_ref[...] = m2
            v2_ref[...] = v2

    full = pl.BlockSpec((tm, cols), lambda i: (i, 0))
    half = pl.BlockSpec((tm, cols), lambda i: (i % nb, 0))
    return _call_carrying(
        body, carried, name=name, grid=(rows // tm,), in_specs=[full, half, half, full, full] * n,
        out_specs=[full] * (4 * n), out_shape=tuple([_sds((rows, cols), F32)] * (4 * n)), scratch_shapes=[],
        operands=[a for item in items for a in item])


def _adamw(name, w, g, m, v, tm):
    def body(w_ref, g_ref, m_ref, v_ref, gout_ref, d_ref, m2_ref, v2_ref):
        gv = g_ref[...]
        d, m2, v2 = _adam_math(w_ref[...], gv, m_ref[...], v_ref[...])
        gout_ref[...] = gv
        d_ref[...] = d
        m2_ref[...] = m2
        v2_ref[...] = v2

    return _rows_call(name, body, tm, [w, g, m, v], [], [_sds(w.shape, F32)] * 4)


_SMALL_NAMES = ("norm_mix_pre", "gm_ln_w", "gm_ln_b", "gm_w_s", "gm_b_s", "conv_w", "conv_b", "dt_bias", "a_log",
                "d_skip", "ssm_norm_w", "norm_mix_post", "norm_ffn_pre", "norm_ffn_post")
_PACK_COLS = 1024


def _pack(parts, names=_SMALL_NAMES, tail=None):
    pieces = [parts[n].reshape(-1) for n in names]
    flat = jnp.concatenate(pieces if tail is None else pieces + [tail])
    rows = -(-flat.shape[0] // (8 * _PACK_COLS)) * 8
    flat = jnp.pad(flat, (0, rows * _PACK_COLS - flat.shape[0]))
    return flat.reshape(rows, _PACK_COLS)


def _unpack(packed, shapes, names=_SMALL_NAMES):
    flat = packed.reshape(-1)
    out, off = {}, 0
    for n in names:
        size = 1
        for s in shapes[n]:
            size *= s
        out[n] = flat[off:off + size].reshape(shapes[n])
        off += size
    return out


def kernel(x, norm_mix_pre, w_in, gm_ln_w, gm_ln_b, gm_w_s, gm_b_s, conv_w, conv_b, dt_bias, a_log, d_skip, ssm_norm_w, w_out, norm_mix_post, norm_ffn_pre, w_up, w_down, norm_ffn_post, loss_target, m_norm_mix_pre, m_w_in, m_gm_ln_w, m_gm_ln_b, m_gm_w_s, m_gm_b_s, m_conv_w, m_conv_b, m_dt_bias, m_a_log, m_d_skip, m_ssm_norm_w, m_w_out, m_norm_mix_post, m_norm_ffn_pre, m_w_up, m_w_down, m_norm_ffn_post, v_norm_mix_pre, v_w_in, v_gm_ln_w, v_gm_ln_b, v_gm_w_s, v_gm_b_s, v_conv_w, v_conv_b, v_dt_bias, v_a_log, v_d_skip, v_ssm_norm_w, v_w_out, v_norm_mix_post, v_norm_ffn_pre, v_w_up, v_w_down, v_norm_ffn_post):
    params = dict(norm_mix_pre=norm_mix_pre, w_in=w_in, gm_ln_w=gm_ln_w, gm_ln_b=gm_ln_b, gm_w_s=gm_w_s, gm_b_s=gm_b_s,
                  conv_w=conv_w, conv_b=conv_b, dt_bias=dt_bias, a_log=a_log, d_skip=d_skip, ssm_norm_w=ssm_norm_w,
                  w_out=w_out, norm_mix_post=norm_mix_post, norm_ffn_pre=norm_ffn_pre, w_up=w_up, w_down=w_down,
                  norm_ffn_post=norm_ffn_post)
    mom1 = dict(norm_mix_pre=m_norm_mix_pre, w_in=m_w_in, gm_ln_w=m_gm_ln_w, gm_ln_b=m_gm_ln_b, gm_w_s=m_gm_w_s,
                gm_b_s=m_gm_b_s, conv_w=m_conv_w, conv_b=m_conv_b, dt_bias=m_dt_bias, a_log=m_a_log, d_skip=m_d_skip,
                ssm_norm_w=m_ssm_norm_w, w_out=m_w_out, norm_mix_post=m_norm_mix_post, norm_ffn_pre=m_norm_ffn_pre,
                w_up=m_w_up, w_down=m_w_down, norm_ffn_post=m_norm_ffn_post)
    mom2 = dict(norm_mix_pre=v_norm_mix_pre, w_in=v_w_in, gm_ln_w=v_gm_ln_w, gm_ln_b=v_gm_ln_b, gm_w_s=v_gm_w_s,
                gm_b_s=v_gm_b_s, conv_w=v_conv_w, conv_b=v_conv_b, dt_bias=v_dt_bias, a_log=v_a_log, d_skip=v_d_skip,
                ssm_norm_w=v_ssm_norm_w, w_out=v_w_out, norm_mix_post=v_norm_mix_post, norm_ffn_pre=v_norm_ffn_pre,
                w_up=v_w_up, w_down=v_w_down, norm_ffn_post=v_norm_ffn_post)
    names = list(params)
    big = ("w_in", "w_out", "w_up", "w_down")
    chip = 2 * lax.axis_index("x") + lax.axis_index("y")

    shards = {n: params[n][0].astype(BF16) for n in big}
    conv_shard = jnp.pad(conv_w[0], ((0, 16 - CONV_K), (0, 0)))
    g_in4, g_conv4 = _run_exchange("allgather_w_in", _allgather_exchange([shards["w_in"], conv_shard]))
    conv_full = jnp.transpose(g_conv4[:, :CONV_K, :], (1, 0, 2)).reshape(CONV_K, CONV_CH)

    small = {n: params[n][0] if params[n].ndim >= 3 else params[n] for n in _SMALL_NAMES if n != "conv_w"}
    core = lax.axis_index("c").astype(jnp.int32).reshape(1)
    adam_args = {n: (params[n][0], mom1[n][0], mom2[n][0]) for n in big}
    loss, grad_x, big_out, small_sum = _forward_backward(
        x, loss_target, g_in4, conv_full, small, shards["w_out"], shards["w_up"], shards["w_down"], core, adam_args)
    grads, delta, new_m, new_v = {}, {}, {}, {}
    for n in big:
        grads[n], delta[n], new_m[n], new_v[n] = [a[None] for a in big_out[n]]

    small_sum["conv_w"] = lax.dynamic_slice_in_dim(small_sum["conv_w"], chip * (CONV_CH // N_CHIPS), CONV_CH // N_CHIPS, axis=1)

    local_shapes = {n: params[n].shape[1:] if params[n].ndim >= 3 else params[n].shape for n in _SMALL_NAMES}
    flat = lambda tree: {n: tree[n].reshape(local_shapes[n]) for n in _SMALL_NAMES}
    packed = [_pack(flat(t)) for t in (params, small_sum, mom1, mom2)]
    _, d_p, m_p, v_p = _adamw("adamw_small", *packed, packed[0].shape[0])
    for src, dst in ((d_p, delta), (m_p, new_m), (v_p, new_v)):
        for n, val in _unpack(src, local_shapes).items():
            dst[n] = val.reshape(params[n].shape)
    for n in _SMALL_NAMES:
        grads[n] = small_sum[n].reshape(params[n].shape)

    out = [loss, grad_x]
    for tree in (grads, delta, new_m, new_v):
        out += [tree[n] for n in names]
    return tuple(out)
```

```python
import functools

import jax
import jax.numpy as jnp
from jax import lax
from jax.experimental import pallas as pl
from jax.experimental.pallas import tpu as pltpu

F32 = jnp.float32
BF16 = jnp.bfloat16
HI = lax.Precision.HIGHEST
MESH = pl.DeviceIdType.MESH

EPS = 1e-6
D_MODEL = 1024
GM_WIDTH = 512
SSM_WIDTH = 512
N_HEADS = 8
HEAD_DIM = 64
CHUNK = 128
SSM_GROUPS = 2
GROUP_W = SSM_WIDTH // SSM_GROUPS
SSM_STATE = 128
CONV_K = 4
CONV_CH = 1024
D_FF = 4096
IN_COLS = 2568
DT_PAD = 128
N_CHIPS = 4
N_DEV = 8

ADAM_LR = 0.001
ADAM_B1 = 0.9
ADAM_B2 = 0.999
ADAM_EPS = 1e-08
ADAM_WD = 0.01
ADAM_STEP = 10

VMEM_LIMIT_BYTES = 56 * 1024 * 1024
FF_TILE = 512
DW_TOKENS_PER_STEP = 2048


def _cparams(n_axes):
    return pltpu.CompilerParams(dimension_semantics=("arbitrary",) * n_axes, vmem_limit_bytes=VMEM_LIMIT_BYTES)


def _dot(a, b):
    return jnp.dot(a.astype(BF16), b.astype(BF16), preferred_element_type=F32)


def _dot_nt(a, b):
    return lax.dot_general(a.astype(BF16), b.astype(BF16), (((1,), (1,)), ((), ())), preferred_element_type=F32)


def _dot_tn(a, b):
    return lax.dot_general(a.astype(BF16), b.astype(BF16), (((0,), (0,)), ((), ())), preferred_element_type=F32)


def _sigmoid(x):
    return 1.0 / (1.0 + jnp.exp(-x))


_GELU_C = 0.7978845608028654
_GELU_A = 0.044715


def _gelu(x):
    t = jnp.tanh(_GELU_C * (x + _GELU_A * (x * x * x)))
    return 0.5 * x * (1.0 + t), t


def _gelu_grad(x, t):
    return 0.5 * (1.0 + t) + 0.5 * x * (1.0 - t * t) * (_GELU_C * (1.0 + 3.0 * _GELU_A * x * x))


def _rms_fwd(x, w):
    r = lax.rsqrt(jnp.mean(x * x, axis=-1, keepdims=True) + EPS)
    return x * r * w, r


def _rms_bwd(x, r, w, dy):
    g = dy * w
    dx = r * g - x * (r * r * r) * jnp.mean(g * x, axis=-1, keepdims=True)
    dw = jnp.sum(dy * x * r, axis=0, keepdims=True)
    return dx, dw


class _Carried:
    def __init__(self, ins, out_shapes, sems, start, finish):
        self.ins, self.out_shapes, self.sems = list(ins), list(out_shapes), list(sems)
        self.start, self.finish = start, finish


def _both(first, second):
    n_i, n_o, n_s = len(first.ins), len(first.out_shapes), len(first.sems)

    def split(ins, outs, sems):
        return (ins[:n_i], outs[:n_o], sems[:n_s]), (ins[n_i:], outs[n_o:], sems[n_s:])

    def start(ins, outs, *sems):
        (i1, o1, s1), (i2, o2, s2) = split(ins, outs, sems)
        first.start(i1, o1, *s1)
        second.start(i2, o2, *s2)

    def finish(ins, outs, *sems):
        (i1, o1, s1), (i2, o2, s2) = split(ins, outs, sems)
        first.finish(i1, o1, *s1)
        second.finish(i2, o2, *s2)

    return _Carried(first.ins + second.ins, first.out_shapes + second.out_shapes, first.sems + second.sems, start, finish)


def _split_carried(refs, n_in, n_out, n_scratch, carried):
    n_ci, n_co, n_cs = len(carried.ins), len(carried.out_shapes), len(carried.sems)
    ins, rest = refs[:n_in], refs[n_in:]
    c_ins, rest = rest[:n_ci], rest[n_ci:]
    outs, rest = rest[:n_out], rest[n_out:]
    c_outs, rest = rest[:n_co], rest[n_co:]
    scr, c_sems = rest[:n_scratch], rest[n_scratch:]
    assert len(c_sems) == n_cs
    return tuple(ins) + tuple(outs) + tuple(scr), c_ins, c_outs, c_sems


def _rows_call(name, body, tm, row_ins, const_ins, row_outs, acc_outs=(), scratch=(), carried=None):
    n_rows = row_ins[0].shape[0]
    assert n_rows % tm == 0
    n_steps = n_rows // tm
    n_in = len(row_ins) + len(const_ins)
    n_ro = len(row_outs)
    n_acc = len(acc_outs)

    def kern(*refs):
        accs = refs[n_in + n_ro:n_in + n_ro + n_acc]

        @pl.when(pl.program_id(0) == 0)
        def _():
            for a in accs:
                a[...] = jnp.zeros_like(a)

        body(*refs)

    def whole(shape):
        nd = len(shape)
        return pl.BlockSpec(tuple(shape), lambda i: (0,) * nd)

    in_specs = [pl.BlockSpec((tm, a.shape[1]), lambda i: (i, 0)) for a in row_ins]
    in_specs += [whole(a.shape) for a in const_ins]
    out_specs = [pl.BlockSpec((tm, s.shape[1]), lambda i: (i, 0)) for s in row_outs]
    out_specs += [whole(s.shape) for s in acc_outs]
    return _call_carrying(
        kern, carried, name=name, grid=(n_steps,), in_specs=in_specs, out_specs=out_specs,
        out_shape=tuple(row_outs) + tuple(acc_outs), scratch_shapes=list(scratch), operands=list(row_ins) + list(const_ins))


def _call_carrying(body, carried, *, name, grid, in_specs, out_specs, out_shape, scratch_shapes, operands):
    n_in, n_out, n_scratch = len(in_specs), len(out_specs), len(scratch_shapes)
    kern = body
    if carried is not None:
        def kern(*refs):
            plain, c_ins, c_outs, c_sems = _split_carried(refs, n_in, n_out, n_scratch, carried)
            first, last = True, True
            for d, size in enumerate(grid):
                first = jnp.logical_and(first, pl.program_id(d) == 0)
                last = jnp.logical_and(last, pl.program_id(d) == size - 1)

            @pl.when(first)
            def _():
                carried.start(c_ins, c_outs, *c_sems)

            body(*plain)

            @pl.when(last)
            def _():
                carried.finish(c_ins, c_outs, *c_sems)

        in_specs = list(in_specs) + [_HBM] * len(carried.ins)
        out_specs = list(out_specs) + [_HBM] * len(carried.out_shapes)
        out_shape = tuple(out_shape) + tuple(carried.out_shapes)
        operands = list(operands) + carried.ins
        scratch_shapes = list(scratch_shapes) + carried.sems
    return pl.pallas_call(
        kern, name=name, grid=grid, in_specs=in_specs, out_specs=out_specs, out_shape=out_shape,
        scratch_shapes=scratch_shapes, compiler_params=_cparams(len(grid)),
    )(*operands)


def _sds(shape, dtype):
    return jax.ShapeDtypeStruct(tuple(shape), dtype)


def _matmul_tn(name, a, b, tm, tn, tk, stacked=False, carried=None):
    k_dim, m_dim = a.shape
    n_dim = b.shape[1]
    assert m_dim % tm == 0 and n_dim % tn == 0 and k_dim % tk == 0
    nk = k_dim // tk

    def kern(a_ref, b_ref, o_ref, acc_ref):
        k = pl.program_id(2)
        prod = _dot_tn(a_ref[...], b_ref[...])

        @pl.when(k == 0)
        def _():
            acc_ref[...] = prod

        @pl.when(k > 0)
        def _():
            acc_ref[...] += prod

        @pl.when(k == nk - 1)
        def _():
            o_ref[...] = acc_ref[...].astype(o_ref.dtype)

    if stacked:
        assert tm == m_dim
        out_shape = _sds((n_dim // tn, m_dim, tn), BF16)
        out_spec = pl.BlockSpec((None, tm, tn), lambda i, j, k: (j, i, 0))
    else:
        out_shape = _sds((m_dim, n_dim), BF16)
        out_spec = pl.BlockSpec((tm, tn), lambda i, j, k: (i, j))
    outs = _call_carrying(
        kern, carried, name=name, grid=(m_dim // tm, n_dim // tn, nk),
        in_specs=[pl.BlockSpec((tk, tm), lambda i, j, k: (k, i)), pl.BlockSpec((tk, tn), lambda i, j, k: (k, j))],
        out_specs=[out_spec], out_shape=(out_shape,), scratch_shapes=[pltpu.VMEM((tm, tn), F32)], operands=[a, b])
    return outs[0] if carried is None else outs


def _inproj_fwd(x, nw, w_uv, w_xbc, w_z, w_dt, tm=256, carried=None):
    n_tok = x.shape[0]

    def body(x_ref, nw_ref, wuv_ref, wxbc_ref, wz_ref, wdt_ref, puv_ref, pxbc_ref, pz_ref, pdt_ref):
        h, _ = _rms_fwd(x_ref[...], nw_ref[...])
        h = h.astype(BF16)
        puv_ref[...] = jnp.dot(h, wuv_ref[...], preferred_element_type=F32)
        pxbc_ref[...] = jnp.dot(h, wxbc_ref[...], preferred_element_type=F32)
        pz_ref[...] = jnp.dot(h, wz_ref[...], preferred_element_type=F32)
        pdt_ref[...] = jnp.dot(h, wdt_ref[...], preferred_element_type=F32)

    return _rows_call(
        "inproj_fwd", body, tm, [x], [nw, w_uv, w_xbc, w_z, w_dt],
        [_sds((n_tok, 2 * GM_WIDTH), F32), _sds((n_tok, CONV_CH), F32), _sds((n_tok, SSM_WIDTH), F32),
         _sds((n_tok, DT_PAD), F32)], carried=carried)


def _head_lane_mask(width, head):
    lane = lax.broadcasted_iota(jnp.int32, (1, width), 1)
    return (lane // HEAD_DIM) == head


def _split_terms(x, terms):
    parts = []
    for _ in range(terms):
        p = x.astype(BF16)
        parts.append(p)
        x = x - p.astype(F32)
    return parts


def _seg_dots(vals, ind, terms=2):
    m = vals[0].shape[0]
    parts = []
    for v in vals:
        parts += _split_terms(v, terms)
    red = jnp.dot(jnp.concatenate(parts, axis=0), ind, preferred_element_type=F32)
    outs = []
    for i in range(len(vals)):
        acc = red[i * terms * m:(i * terms + 1) * m]
        for t in range(1, terms):
            acc = acc + red[(i * terms + t) * m:(i * terms + t + 1) * m]
        outs.append(acc)
    return outs


def _tri_dot(mask, x, terms=3):
    n = x.shape[1]
    red = jnp.dot(mask.astype(BF16), jnp.concatenate(_split_terms(x, terms), axis=1), preferred_element_type=F32)
    acc = red[:, :n]
    for t in range(1, terms):
        acc = acc + red[:, t * n:(t + 1) * n]
    return acc


def _gmlp_common(puv, lnw, lnb, e_bf, et_bf):
    u = puv[:, :GM_WIDTH]
    v = puv[:, GM_WIDTH:]
    gu, tu = _gelu(u)
    gv, tv = _gelu(v)
    (s1,) = _seg_dots([gv], et_bf)
    (mu,) = _seg_dots([s1 * (1.0 / HEAD_DIM)], e_bf)
    xc = gv - mu
    (s2,) = _seg_dots([xc * xc], et_bf)
    (rstd,) = _seg_dots([lax.rsqrt(s2 * (1.0 / HEAD_DIM) + EPS)], e_bf)
    xhat = xc * rstd
    vn = xhat * lnw + lnb
    return u, v, gu, tu, tv, rstd, xhat, vn


def _tril_mask():
    r = lax.broadcasted_iota(jnp.int32, (CHUNK, CHUNK), 0)
    c = lax.broadcasted_iota(jnp.int32, (CHUNK, CHUNK), 1)
    return r >= c


def _head_blocks(v):
    return jnp.concatenate([jnp.where(_head_lane_mask(GM_WIDTH, h), v, jnp.zeros_like(v)) for h in range(N_HEADS)], axis=0)


def _causal_w_cat(w_cat):
    t = lax.broadcasted_iota(jnp.int32, (CHUNK, N_HEADS * CHUNK), 0)
    s = lax.broadcasted_iota(jnp.int32, (CHUNK, N_HEADS * CHUNK), 1) % CHUNK
    return jnp.where(t >= s, w_cat, 0.0).astype(BF16)


def _gmlp_chunk_fwd(puv, lnw, lnb, e_bf, et_bf, wm, bmap):
    _, _, gu, _, _, _, _, vn = _gmlp_common(puv, lnw, lnb, e_bf, et_bf)
    mixed = jnp.dot(wm, _head_blocks(vn.astype(BF16)), preferred_element_type=F32) + bmap
    return (gu * mixed).astype(BF16)


SUBLANES = 8


def _shift_down(x, tail, s):
    main = pltpu.roll(x, s, 0)
    row = lax.broadcasted_iota(jnp.int32, (SUBLANES, 1), 0)
    head = jnp.where(row < s, pltpu.roll(tail, s, 0), main[:SUBLANES])
    return jnp.concatenate([head, main[SUBLANES:]], axis=0)


def _shift_up(x, head_next, s):
    n = x.shape[0]
    main = pltpu.roll(x, n - s, 0)
    row = lax.broadcasted_iota(jnp.int32, (SUBLANES, 1), 0)
    last = jnp.where(row >= SUBLANES - s, pltpu.roll(head_next, SUBLANES - s, 0), main[n - SUBLANES:])
    return jnp.concatenate([main[:n - SUBLANES], last], axis=0)


def _ssd_pre(xr, tail, cw_ref, cb, pdt, dtb, alog, emap):
    rowi = lax.broadcasted_iota(jnp.int32, (CHUNK, 1), 0)
    shifted = [_shift_down(xr, tail, 3), _shift_down(xr, tail, 2), _shift_down(xr, tail, 1), xr]
    xc = cb
    for k in range(CONV_K):
        xc = xc + cw_ref[k] * shifted[k]
    sg = _sigmoid(xc)
    xa = xc * sg
    pre = pdt + dtb
    dt = jnp.maximum(pre, 0.0) + jnp.log(1.0 + jnp.exp(-jnp.abs(pre)))
    a_neg = -jnp.exp(alog)
    a_cs = _tri_dot(_tril_mask(), dt * a_neg)
    acs_map, dt_map = _seg_dots([a_cs, dt], emap, terms=3)
    return dict(shifted=shifted, xc=xc, sg=sg, xa=xa, pre=pre, dt=dt, a_neg=a_neg, a_cs=a_cs,
                acs_map=acs_map, dt_map=dt_map, rowi=rowi)


def _ssd_maps(p):
    last = p["rowi"] == CHUNK - 1
    aq_map = jnp.sum(jnp.where(last, p["acs_map"], 0.0), axis=0, keepdims=True)
    e_exp = jnp.exp(p["acs_map"])
    dte = jnp.exp(aq_map - p["acs_map"])
    cd = jnp.exp(aq_map)
    return last, e_exp, dte, cd


def _head_decay(a_cs, a_cs_t, head, tri):
    lane = lax.broadcasted_iota(jnp.int32, (1, DT_PAD), 1)
    sub = lax.broadcasted_iota(jnp.int32, (DT_PAD, 1), 0)
    col = jnp.sum(jnp.where(lane == head, a_cs, 0.0), axis=1, keepdims=True)
    row = jnp.sum(jnp.where(sub == head, a_cs_t, 0.0), axis=0, keepdims=True)
    return jnp.exp(jnp.where(tri, col - row, -1e30))


def _gate_fwd(y, z, nw):
    sz = _sigmoid(z)
    zg = z * sz
    yg = y * zg
    outs, rs = [], []
    for g in range(SSM_GROUPS):
        gs = slice(g * GROUP_W, (g + 1) * GROUP_W)
        o, r = _rms_fwd(yg[:, gs], nw[:, gs])
        outs.append(o)
        rs.append(r)
    return sz, zg, yg, outs, rs


def _ssd_const_specs():
    def whole(shape):
        nd = len(shape)
        return pl.BlockSpec(tuple(shape), lambda c: (0,) * nd)
    return [whole((CONV_K, 1, CONV_CH)), whole((1, CONV_CH)), whole((1, DT_PAD)), whole((1, DT_PAD)),
            whole((1, SSM_WIDTH)), whole((1, SSM_WIDTH)), whole((DT_PAD, SSM_WIDTH)), whole((SSM_WIDTH, DT_PAD))]


def _mixer_fwd(p_uv, p_xbc, p_z, p_dt, x, lnw, lnb, w_cat, bmap, w_out, nw_post, nw_pre2, conv_w, conv_b, dt_bias, a_log,
               dskip_map, norm_w, e_bf, et_bf, n_seq, carried=None):
    n_tok = p_xbc.shape[0]
    nc = n_tok // n_seq // CHUNK

    def body(puv3, xr3, z3, pdt3, x3, lnw_ref, lnb_ref, wcat_ref, bmap_ref, wo_ref, nwa_ref, nwb_ref,
             cw_ref, cb_ref, dtb_ref, alog_ref, dsk_ref, nw_ref, e_ref, et_ref,
             ya3, yb3, yssd3, sprev3, o3, x13, h23, wm_scr, prev3_scr, s3_scr):
        @pl.when(pl.program_id(0) == 0)
        def _():
            wm_scr[...] = _causal_w_cat(wcat_ref[...])
            prev3_scr[...] = jnp.zeros_like(prev3_scr)
            s3_scr[...] = jnp.zeros_like(s3_scr)

        for b in range(n_seq):
            one_sequence(puv3.at[b], xr3.at[b], z3.at[b], pdt3.at[b], lnw_ref, lnb_ref, bmap_ref,
                         cw_ref, cb_ref, dtb_ref, alog_ref, dsk_ref, nw_ref, e_ref, et_ref,
                         ya3.at[b], yb3.at[b], yssd3.at[b], sprev3.at[b], wm_scr, prev3_scr.at[b], s3_scr.at[b])
            ya, yb = ya3[b], yb3[b]
            o = jnp.dot(ya, wo_ref[:GM_WIDTH, :], preferred_element_type=F32)
            o = o + jnp.dot(yb, wo_ref[GM_WIDTH:, :], preferred_element_type=F32)
            on, _ = _rms_fwd(o, nwa_ref[...])
            x1 = x3[b] + on
            h2, _ = _rms_fwd(x1, nwb_ref[...])
            o3[b] = o
            x13[b] = x1
            h23[b] = h2.astype(BF16)

    def one_sequence(puv_ref, xr_ref, z_ref, pdt_ref, lnw_ref, lnb_ref, bmap_ref,
                     cw_ref, cb_ref, dtb_ref, alog_ref, dsk_ref, nw_ref, e_ref, et_ref,
                     ya_ref, yb_ref, yssd_ref, sprev_ref, wm_scr, prev_scr, s_scr):
        ya_ref[...] = _gmlp_chunk_fwd(puv_ref[...], lnw_ref[...], lnb_ref[...], e_ref[...], et_ref[...], wm_scr[...],
                                      bmap_ref[...])
        xr = xr_ref[...]
        p = _ssd_pre(xr, prev_scr[...], cw_ref, cb_ref[...], pdt_ref[...], dtb_ref[...], alog_ref[...], e_ref[...])
        _, e_exp, dte, cd = _ssd_maps(p)
        xs = p["xa"][:, :SSM_WIDTH]
        xd = xs * p["dt_map"]
        a_cs_t = p["a_cs"].T
        tri = _tril_mask()
        s_old = s_scr[...]
        sprev_ref[...] = s_old
        for g in range(SSM_GROUPS):
            gs = slice(g * GROUP_W, (g + 1) * GROUP_W)
            bm = p["xa"][:, SSM_WIDTH + g * SSM_STATE: SSM_WIDTH + (g + 1) * SSM_STATE].astype(BF16)
            cm = p["xa"][:, SSM_WIDTH + (SSM_GROUPS + g) * SSM_STATE: SSM_WIDTH + (SSM_GROUPS + g + 1) * SSM_STATE].astype(BF16)
            cb_mat = _dot_nt(cm, bm)
            xdg = xd[:, gs].astype(BF16)
            y_g = _dot(cm, s_old[:, gs]) * e_exp[:, gs] + dsk_ref[:, gs] * xs[:, gs]
            for r in range(SSM_GROUPS * 2):
                dm = _head_decay(p["a_cs"], a_cs_t, g * 4 + r, tri)
                full = jnp.dot((cb_mat * dm).astype(BF16), xdg, preferred_element_type=F32)
                y_g = y_g + jnp.where(_head_lane_mask(GROUP_W, r), full, 0.0)
            yssd_ref[:, gs] = y_g
            s_scr[:, gs] = cd[:, gs] * s_old[:, gs] + _dot_tn(bm, xd[:, gs] * dte[:, gs])
        _, _, _, outs, _ = _gate_fwd(yssd_ref[...], z_ref[...], nw_ref[...])
        for g in range(SSM_GROUPS):
            yb_ref[:, g * GROUP_W:(g + 1) * GROUP_W] = outs[g].astype(BF16)
        prev_scr[...] = xr[CHUNK - SUBLANES:, :]

    seq_len = n_tok // n_seq

    def rows(width):
        return pl.BlockSpec((n_seq, CHUNK, width), lambda c: (0, c, 0))

    def whole(shape):
        nd = len(shape)
        return pl.BlockSpec(tuple(shape), lambda c: (0,) * nd)

    def by_seq(a):
        return a.reshape(n_seq, seq_len, a.shape[-1])

    outs = _call_carrying(
        body, carried, name="mixer_fwd", grid=(nc,),
        in_specs=[rows(2 * GM_WIDTH), rows(CONV_CH), rows(SSM_WIDTH), rows(DT_PAD), rows(D_MODEL), whole(lnw.shape),
                  whole(lnb.shape), whole(w_cat.shape), whole(bmap.shape), whole(w_out.shape), whole(nw_post.shape),
                  whole(nw_pre2.shape)] + _ssd_const_specs(),
        out_specs=[rows(GM_WIDTH), rows(SSM_WIDTH), rows(SSM_WIDTH), rows(SSM_WIDTH), rows(D_MODEL), rows(D_MODEL),
                   rows(D_MODEL)],
        out_shape=(_sds((n_seq, seq_len, GM_WIDTH), BF16), _sds((n_seq, seq_len, SSM_WIDTH), BF16),
                   _sds((n_seq, seq_len, SSM_WIDTH), F32), _sds((n_seq, seq_len, SSM_WIDTH), F32),
                   _sds((n_seq, seq_len, D_MODEL), F32), _sds((n_seq, seq_len, D_MODEL), F32),
                   _sds((n_seq, seq_len, D_MODEL), BF16)),
        scratch_shapes=[pltpu.VMEM((CHUNK, N_HEADS * CHUNK), BF16), pltpu.VMEM((n_seq, SUBLANES, CONV_CH), F32),
                        pltpu.VMEM((n_seq, SSM_STATE, SSM_WIDTH), F32)],
        operands=[by_seq(p_uv), by_seq(p_xbc), by_seq(p_z), by_seq(p_dt), by_seq(x), lnw, lnb, w_cat, bmap, w_out, nw_post,
                  nw_pre2, conv_w, conv_b, dt_bias, a_log, dskip_map, norm_w, e_bf, et_bf])
    return tuple(o.reshape(n_tok, o.shape[-1]) for o in outs[:7]) + tuple(outs[7:])


def _up_cols(wup_ref, j):
    per = (D_FF // N_CHIPS) // FF_TILE
    return wup_ref[j // per, :, (j % per) * FF_TILE:(j % per + 1) * FF_TILE]


def _down_rows(wda_ref, wdb_ref, j):
    assert 2 * FF_TILE == D_FF // N_CHIPS
    return (wda_ref if j % 2 == 0 else wdb_ref)[j // 2]


def _skewed_rows_call(name, main, tail, tm, lead_ins, lag_ins, const_ins, lead_outs, lag_outs, acc_outs, carry,
                      streamed, tile_copies, n_copies):
    n_rows = lead_ins[0].shape[0]
    assert n_rows % tm == 0
    n = n_rows // tm
    counts = [len(lead_ins), len(lag_ins), len(const_ins), len(streamed), len(lead_outs), len(lag_outs), len(acc_outs),
              1, len(streamed)]

    def kern(*refs):
        groups, pos = [], 0
        for cnt in counts:
            groups.append(refs[pos:pos + cnt])
            pos += cnt
        lead_i, lag_i, consts, w_hbm, lead_o, lag_o, accs, (carry_scr,), w_vmem = groups
        sems = refs[pos]
        i = pl.program_id(0)
        pieces, k = [], 0
        for piece in tile_copies(w_hbm, w_vmem):
            pieces.append([pltpu.make_async_copy(src, dst, sems.at[k + q]) for q, (src, dst) in enumerate(piece)])
            k += len(piece)

        def ready(j):
            for cp in pieces[j]:
                cp.wait()

        @pl.when(i == 0)
        def _():
            for piece in pieces:
                for cp in piece:
                    cp.start()
            for a in accs:
                a[...] = jnp.zeros_like(a)
            carry_scr[...] = main(lead_i, consts, lead_o, w_vmem, ready)

        @pl.when(jnp.logical_and(i > 0, i < n))
        def _():
            previous = carry_scr[...]
            carry_scr[...] = main(lead_i, consts, lead_o, w_vmem, lambda j: None)
            tail(previous, lag_i, consts, lag_o, accs)

        @pl.when(i == n)
        def _():
            tail(carry_scr[...], lag_i, consts, lag_o, accs)

    def lead(width):
        return pl.BlockSpec((tm, width), lambda i: (jnp.minimum(i, n - 1), 0))

    def lag(width):
        return pl.BlockSpec((tm, width), lambda i: (jnp.maximum(i - 1, 0), 0))

    def whole(shape):
        nd = len(shape)
        return pl.BlockSpec(tuple(shape), lambda i: (0,) * nd)

    return pl.pallas_call(
        kern, name=name, grid=(n + 1,),
        in_specs=([lead(a.shape[1]) for a in lead_ins] + [lag(a.shape[1]) for a in lag_ins]
                  + [whole(a.shape) for a in const_ins] + [_HBM] * len(streamed)),
        out_specs=[lead(s.shape[1]) for s in lead_outs] + [lag(s.shape[1]) for s in lag_outs] + [whole(s.shape) for s in acc_outs],
        out_shape=tuple(lead_outs) + tuple(lag_outs) + tuple(acc_outs),
        scratch_shapes=([pltpu.VMEM(carry, F32)] + [pltpu.VMEM(a.shape, a.dtype) for a in streamed]
                        + [pltpu.SemaphoreType.DMA((n_copies,))]),
        compiler_params=_cparams(1),
    )(*lead_ins, *lag_ins, *const_ins, *streamed)


def _mlp_weight_pieces(order):
    per = (D_FF // N_CHIPS) // FF_TILE

    def tile_copies(hbm, vmem):
        pieces = []
        for j in range(D_FF // FF_TILE):
            cols = (j // per, slice(None), pl.ds((j % per) * FF_TILE, FF_TILE))
            up = (hbm[0].at[cols], vmem[0].at[cols])
            down = (hbm[1 + j % 2].at[j // 2], vmem[1 + j % 2].at[j // 2])
            pieces.append([up, down] if order == "up_down" else [down, up])
        return pieces

    return tile_copies


def _mlp_fwd(h2, x1, tgt, w_up, w_down_a, w_down_b, nw, tm=512):
    n_tok = x1.shape[0]

    def main(lead_i, consts, lead_o, weights, ready):
        (h2_ref,), (f_ref,), (wup_ref, wda_ref, wdb_ref) = lead_i, lead_o, weights
        h2v = h2_ref[...]
        acc = jnp.zeros((tm, D_MODEL), F32)
        for j in range(D_FF // FF_TILE):
            cs = slice(j * FF_TILE, (j + 1) * FF_TILE)
            ready(j)
            u = jnp.dot(h2v, _up_cols(wup_ref, j), preferred_element_type=F32)
            f = jnp.square(jnp.maximum(u, 0.0)).astype(BF16)
            f_ref[:, cs] = f
            acc = acc + jnp.dot(f, _down_rows(wda_ref, wdb_ref, j), preferred_element_type=F32)
        return acc

    def tail(acc, lag_i, consts, lag_o, accs):
        (x1_ref, tgt_ref), (nw_ref,), (dd_ref, dy_ref), (loss_ref, dnw_ref) = lag_i, consts, lag_o, accs
        dn, r = _rms_fwd(acc, nw_ref[...])
        e = x1_ref[...] + dn - tgt_ref[...]
        loss_ref[...] += jnp.full(loss_ref.shape, (0.5 / D_MODEL) * jnp.sum(e * e), F32)
        dy = e * (1.0 / D_MODEL)
        dd, dnw = _rms_bwd(acc, r, nw_ref[...], dy)
        dy_ref[...] = dy
        dd_ref[...] = dd.astype(BF16)
        dnw_ref[...] += dnw

    return _skewed_rows_call(
        "mlp_fwd", main, tail, tm, [h2], [x1, tgt], [nw],
        [_sds((n_tok, D_FF), BF16)], [_sds((n_tok, D_MODEL), BF16), _sds((n_tok, D_MODEL), F32)],
        [_sds((8, 128), F32), _sds((1, D_MODEL), F32)], carry=(tm, D_MODEL),
        streamed=[w_up, w_down_a, w_down_b], tile_copies=_mlp_weight_pieces("up_down"), n_copies=2 * (D_FF // FF_TILE))


def _mlp_bwd(dd, f, x1, dy, w_down_a, w_down_b, w_up, nw, tm=256):
    n_tok = x1.shape[0]

    def main(lead_i, consts, lead_o, weights, ready):
        (dd_ref, f_ref), (dup_ref,), (wup_ref, wda_ref, wdb_ref) = lead_i, lead_o, weights
        ddv = dd_ref[...]
        acc = jnp.zeros((tm, D_MODEL), F32)
        for j in range(D_FF // FF_TILE):
            cs = slice(j * FF_TILE, (j + 1) * FF_TILE)
            ready(j)
            df = _dot_nt(ddv, _down_rows(wda_ref, wdb_ref, j))
            du = (df * (2.0 * jnp.sqrt(f_ref[:, cs].astype(F32)))).astype(BF16)
            dup_ref[:, cs] = du
            acc = acc + _dot_nt(du, _up_cols(wup_ref, j))
        return acc

    def tail(acc, lag_i, consts, lag_o, accs):
        (x1_ref, dy_ref), (nw_ref,), (dx1_ref,), (dnw_ref,) = lag_i, consts, lag_o, accs
        x1v = x1_ref[...]
        _, r = _rms_fwd(x1v, nw_ref[...])
        dx, dnw = _rms_bwd(x1v, r, nw_ref[...], acc)
        dx1_ref[...] = dy_ref[...] + dx
        dnw_ref[...] += dnw

    return _skewed_rows_call(
        "mlp_bwd", main, tail, tm, [dd, f], [x1, dy], [nw],
        [_sds((n_tok, D_FF), BF16)], [_sds((n_tok, D_MODEL), F32)], [_sds((1, D_MODEL), F32)], carry=(tm, D_MODEL),
        streamed=[w_up, w_down_a, w_down_b], tile_copies=_mlp_weight_pieces("down_up"), n_copies=2 * (D_FF // FF_TILE))


def _outproj_bwd(dx1, o, w_out, nw, tm=256, carried=None):
    n_tok = dx1.shape[0]

    def body(dx1_ref, o_ref, wo_ref, nw_ref, do_ref, dya_ref, dyb_ref, dnw_ref):
        ov = o_ref[...]
        _, r = _rms_fwd(ov, nw_ref[...])
        do, dnw = _rms_bwd(ov, r, nw_ref[...], dx1_ref[...])
        dob = do.astype(BF16)
        do_ref[...] = dob
        dya_ref[...] = _dot_nt(dob, wo_ref[:GM_WIDTH, :])
        dyb_ref[...] = _dot_nt(dob, wo_ref[GM_WIDTH:, :])
        dnw_ref[...] += dnw

    return _rows_call("outproj_bwd", body, tm, [dx1, o], [w_out, nw],
                      [_sds((n_tok, D_MODEL), BF16), _sds((n_tok, GM_WIDTH), F32), _sds((n_tok, SSM_WIDTH), F32)],
                      [_sds((1, D_MODEL), F32)], carried=carried)


def _gmlp_bwd(p_uv, dya, lnw, lnb, e_bf, et_bf, w_cat, w_stack, bmap, carried=None):
    n_tok = p_uv.shape[0]
    chunks_per_step = 2

    def body(puv_ref, dya_ref, lnw_ref, lnb_ref, e_ref, et_ref, wcat_ref, wstack_ref, bmap_ref,
             dpuv_ref, dws_ref, dbs_ref, dlnw_ref, dlnb_ref, wm_scr, wsm_scr):
        t_stk = lax.broadcasted_iota(jnp.int32, (N_HEADS * CHUNK, CHUNK), 0) % CHUNK
        s_stk = lax.broadcasted_iota(jnp.int32, (N_HEADS * CHUNK, CHUNK), 1)

        @pl.when(pl.program_id(0) == 0)
        def _():
            wm_scr[...] = _causal_w_cat(wcat_ref[...])
            wsm_scr[...] = jnp.where(t_stk >= s_stk, wstack_ref[...], 0.0).astype(BF16)

        lnw_v = lnw_ref[...]
        e_v, et_v = e_ref[...], et_ref[...]

        def one_chunk(rows):
            u, v, gu, tu, tv, rstd, xhat, vn = _gmlp_common(puv_ref[rows, :], lnw_v, lnb_ref[...], e_v, et_v)
            vnb = vn.astype(BF16)
            mixed = jnp.dot(wm_scr[...], _head_blocks(vnb), preferred_element_type=F32) + bmap_ref[...]
            dy = dya_ref[rows, :]
            du = dy * mixed * _gelu_grad(u, tu)
            dmixed = dy * gu
            (dbs,) = _seg_dots([dmixed], et_v)
            dblocks = _head_blocks(dmixed.astype(BF16))
            dvn = lax.dot_general(wsm_scr[...], dblocks, (((0,), (0,)), ((), ())), preferred_element_type=F32)
            dws = lax.dot_general(dblocks, vnb, (((1,), (1,)), ((), ())), preferred_element_type=F32)
            dxh = dvn * lnw_v
            m1, m2 = _seg_dots([dxh, dxh * xhat], et_v)
            m1, m2 = _seg_dots([m1 * (1.0 / HEAD_DIM), m2 * (1.0 / HEAD_DIM)], e_v)
            dgv = rstd * (dxh - m1 - xhat * m2)
            dv = dgv * _gelu_grad(v, tv)
            dpuv_ref[rows, :GM_WIDTH] = du.astype(BF16)
            dpuv_ref[rows, GM_WIDTH:] = dv.astype(BF16)
            return dbs, dws, jnp.sum(dvn * xhat, axis=0, keepdims=True), jnp.sum(dvn, axis=0, keepdims=True)

        parts = [one_chunk(slice(k * CHUNK, (k + 1) * CHUNK)) for k in range(chunks_per_step)]
        dbs, dws, dlnw, dlnb = [functools.reduce(lambda a, b: a + b, vals) for vals in zip(*parts)]
        dbs_ref[...] += dbs
        dws_ref[...] += jnp.where(t_stk >= s_stk, dws, 0.0)
        dlnw_ref[...] += dlnw
        dlnb_ref[...] += dlnb

    return _rows_call(
        "gmlp_bwd", body, chunks_per_step * CHUNK, [p_uv, dya], [lnw, lnb, e_bf, et_bf, w_cat, w_stack, bmap],
        [_sds((n_tok, 2 * GM_WIDTH), BF16)],
        [_sds((N_HEADS * CHUNK, CHUNK), F32), _sds((CHUNK, DT_PAD), F32), _sds((1, GM_WIDTH), F32),
         _sds((1, GM_WIDTH), F32)],
        scratch=[pltpu.VMEM((CHUNK, N_HEADS * CHUNK), BF16), pltpu.VMEM((N_HEADS * CHUNK, CHUNK), BF16)],
        carried=carried)


def _ssd_bwd(p_xbc, p_z, p_dt, yssd, sprev, dyb, conv_w, conv_b, dt_bias, a_log, dskip_map, norm_w, e_bf, et_bf, n_seq,
             carried=None):
    n_tok = p_xbc.shape[0]
    nc = n_tok // n_seq // CHUNK

    def body(xr3, xprev3, z3, pdt3, yssd3, sprev3, dyb3,
             cw_ref, cb_ref, dtb_ref, alog_ref, dsk_ref, nw_ref, e_ref, et_ref,
             dpxbc3, dpz3, dpdt3, dcw_ref, dcb_ref, ddtb_ref, dalog_ref, ddsk_ref, dnw_ref,
             ds3_scr, nxt3_scr, dxa3_scr):
        @pl.when(pl.program_id(0) == 0)
        def _():
            for a in (dcw_ref, dcb_ref, ddtb_ref, dalog_ref, ddsk_ref, dnw_ref, ds3_scr, nxt3_scr):
                a[...] = jnp.zeros_like(a)

        for b in range(n_seq):
            one_sequence(xr3.at[b], xprev3.at[b], z3.at[b], pdt3.at[b], yssd3.at[b], sprev3.at[b], dyb3.at[b],
                         cw_ref, cb_ref, dtb_ref, alog_ref, dsk_ref, nw_ref, e_ref, et_ref,
                         dpxbc3.at[b], dpz3.at[b], dpdt3.at[b], dcw_ref, dcb_ref, ddtb_ref, dalog_ref, ddsk_ref, dnw_ref,
                         ds3_scr.at[b], nxt3_scr.at[b], dxa3_scr.at[b])

    def one_sequence(xr_ref, xprev_ref, z_ref, pdt_ref, yssd_ref, sprev_ref, dyb_ref,
                     cw_ref, cb_ref, dtb_ref, alog_ref, dsk_ref, nw_ref, e_ref, et_ref,
                     dpxbc_ref, dpz_ref, dpdt_ref, dcw_ref, dcb_ref, ddtb_ref, dalog_ref, ddsk_ref, dnw_ref,
                     ds_scr, nxt_scr, dxa_scr):
        chunk = nc - 1 - pl.program_id(0)
        xr = xr_ref[...]
        prev = jnp.where(chunk == 0, 0.0, xprev_ref[...])
        et_v = et_ref[...]
        p = _ssd_pre(xr, prev, cw_ref, cb_ref[...], pdt_ref[...], dtb_ref[...], alog_ref[...], e_ref[...])
        last, e_exp, dte, cd = _ssd_maps(p)
        rowi = p["rowi"]
        xs = p["xa"][:, :SSM_WIDTH]
        xd = xs * p["dt_map"]
        a_cs_t = p["a_cs"].T
        tri = _tril_mask()
        dsk = dsk_ref[...]
        nw_v = nw_ref[...]

        yv = yssd_ref[...]
        zv = z_ref[...]
        sz, zg, yg, _, rs = _gate_fwd(yv, zv, nw_v)
        dout = dyb_ref[...]
        for g in range(SSM_GROUPS):
            gs = slice(g * GROUP_W, (g + 1) * GROUP_W)
            dyg_g, dnw_g = _rms_bwd(yg[:, gs], rs[g], nw_v[:, gs], dout[:, gs])
            dnw_ref[:, gs] += dnw_g
            dxa_scr[:, gs] = dyg_g
        dyg = dxa_scr[:, :SSM_WIDTH]
        d_y = dyg * zg
        dpz_ref[...] = (dyg * yv * (sz + zv * sz * (1.0 - sz))).astype(BF16)

        s_prev = sprev_ref[...]
        ds_next = ds_scr[...]
        lane_dt = lax.broadcasted_iota(jnp.int32, (1, DT_PAD), 1)
        da_cols = jnp.zeros((CHUNK, DT_PAD), F32)
        for g in range(SSM_GROUPS):
            gs = slice(g * GROUP_W, (g + 1) * GROUP_W)
            b_off = SSM_WIDTH + g * SSM_STATE
            c_off = SSM_WIDTH + (SSM_GROUPS + g) * SSM_STATE
            bm = p["xa"][:, b_off:b_off + SSM_STATE].astype(BF16)
            cm = p["xa"][:, c_off:c_off + SSM_STATE].astype(BF16)
            cb_mat = _dot_nt(cm, bm)
            d_yg = d_y[:, gs]
            d_ygb = d_yg.astype(BF16)
            xdg = xd[:, gs]
            xdgb = xdg.astype(BF16)
            ds_g = ds_next[:, gs]
            sp_g = s_prev[:, gs]
            bds = _dot(bm, ds_g)
            dcs = d_yg * e_exp[:, gs]
            d_c = _dot_nt(dcs, sp_g)
            ds_scr[:, gs] = cd[:, gs] * ds_g + _dot_tn(cm, dcs)
            d_b = _dot_nt(xdg * dte[:, gs], ds_g)
            dxd_g = bds * dte[:, gs]
            sum_dcb = jnp.zeros((CHUNK, CHUNK), F32)
            for r in range(SSM_GROUPS * 2):
                head = g * 4 + r
                mask = _head_lane_mask(GROUP_W, r)
                dm = _head_decay(p["a_cs"], a_cs_t, head, tri)
                m_mat = cb_mat * dm
                g_mat = _dot_nt(jnp.where(mask, d_yg, 0.0), xdgb)
                w_mat = g_mat * m_mat
                sum_dcb = sum_dcb + g_mat * dm
                dxd_g = dxd_g + jnp.where(mask, _dot_tn(m_mat, d_ygb), 0.0)
                da_h = jnp.sum(w_mat - w_mat.T, axis=1, keepdims=True)
                da_cols = da_cols + jnp.where(lane_dt == head, da_h, 0.0)
            d_c = d_c + _dot(sum_dcb, bm)
            d_b = d_b + _dot_tn(sum_dcb, cm)
            dxa_scr[:, b_off:b_off + SSM_STATE] = d_b
            dxa_scr[:, c_off:c_off + SSM_STATE] = d_c
            y_off_g = _dot(cm, sp_g) * e_exp[:, gs]
            t3 = bds * xdg * dte[:, gs]
            tail = jnp.sum(t3, axis=0, keepdims=True) + jnp.sum(ds_g * sp_g, axis=0, keepdims=True) * cd[:, gs]
            pre_g = d_yg * y_off_g - t3 + jnp.where(last, tail, 0.0)
            s_pre, ddt_g, s_dsk = _seg_dots([pre_g, dxd_g * xs[:, gs], d_yg * xs[:, gs]], et_v[gs, :])
            da_cols = da_cols + s_pre
            ddsk_ref[...] += jnp.sum(s_dsk, axis=0, keepdims=True)
            dxa_scr[:, gs] = dxd_g * p["dt_map"][:, gs] + dsk[:, gs] * d_yg
            if g == 0:
                ddt = ddt_g
            else:
                ddt = ddt + ddt_g
        r_i = lax.broadcasted_iota(jnp.int32, (CHUNK, CHUNK), 0)
        c_i = lax.broadcasted_iota(jnp.int32, (CHUNK, CHUNK), 1)
        ddta = _tri_dot(r_i <= c_i, da_cols, terms=2)
        ddt = ddt + ddta * p["a_neg"]
        dalog_ref[...] += jnp.sum(ddta * p["dt"], axis=0, keepdims=True) * p["a_neg"]
        draw = ddt * _sigmoid(p["pre"])
        ddtb_ref[...] += jnp.sum(draw, axis=0, keepdims=True)
        dpdt_ref[...] = draw.astype(BF16)

        xc = p["xc"]
        sg = p["sg"]
        dxc = dxa_scr[...] * (sg + xc * sg * (1.0 - sg))
        dcb_ref[...] += jnp.sum(dxc, axis=0, keepdims=True)
        for k in range(CONV_K):
            dcw_ref[k] += jnp.sum(dxc * p["shifted"][k], axis=0, keepdims=True)
        nxt = nxt_scr[...]
        dxr = cw_ref[3] * dxc
        for s in range(1, CONV_K):
            dxr = dxr + cw_ref[CONV_K - 1 - s] * _shift_up(dxc, nxt, s)
        dpxbc_ref[...] = dxr.astype(BF16)
        nxt_scr[...] = dxc[:SUBLANES, :]

    seq_len = n_tok // n_seq

    def rows(width):
        return pl.BlockSpec((n_seq, CHUNK, width), lambda s: (0, nc - 1 - s, 0))

    tiles = CHUNK // SUBLANES
    prev_rows = pl.BlockSpec((n_seq, SUBLANES, CONV_CH), lambda s: (0, jnp.maximum((nc - 1 - s) * tiles - 1, 0), 0))

    def whole(shape):
        nd = len(shape)
        return pl.BlockSpec(tuple(shape), lambda s: (0,) * nd)

    def by_seq(a):
        return a.reshape(n_seq, seq_len, a.shape[-1])

    acc_shapes = [(CONV_K, 1, CONV_CH), (1, CONV_CH), (1, DT_PAD), (1, DT_PAD), (1, DT_PAD), (1, SSM_WIDTH)]
    xbc3 = by_seq(p_xbc)
    outs = _call_carrying(
        body, carried, name="ssd_bwd", grid=(nc,),
        in_specs=[rows(CONV_CH), prev_rows, rows(SSM_WIDTH), rows(DT_PAD), rows(SSM_WIDTH), rows(SSM_WIDTH),
                  rows(SSM_WIDTH)] + _ssd_const_specs(),
        out_specs=[rows(CONV_CH), rows(SSM_WIDTH), rows(DT_PAD)] + [whole(s) for s in acc_shapes],
        out_shape=tuple([_sds((n_seq, seq_len, CONV_CH), BF16), _sds((n_seq, seq_len, SSM_WIDTH), BF16),
                         _sds((n_seq, seq_len, DT_PAD), BF16)] + [_sds(s, F32) for s in acc_shapes]),
        scratch_shapes=[pltpu.VMEM((n_seq, SSM_STATE, SSM_WIDTH), F32), pltpu.VMEM((n_seq, SUBLANES, CONV_CH), F32),
                        pltpu.VMEM((n_seq, CHUNK, CONV_CH), F32)],
        operands=[xbc3, xbc3, by_seq(p_z), by_seq(p_dt), by_seq(yssd), by_seq(sprev), by_seq(dyb), conv_w, conv_b, dt_bias,
                  a_log, dskip_map, norm_w, e_bf, et_bf])
    return tuple(o.reshape(n_tok, o.shape[-1]) for o in outs[:3]) + tuple(outs[3:])


def _inproj_bwd(dp_uv, dp_xbc, dp_z, dp_dt, x, dx1, w_uv, w_xbc, w_z, w_dt, nw, tm=256, carried=None):
    n_tok = x.shape[0]

    def body(duv_ref, dxbc_ref, dz_ref, ddt_ref, x_ref, dx1_ref, wuv_ref, wxbc_ref, wz_ref, wdt_ref, nw_ref,
             gx_ref, h_ref, dnw_ref):
        dh = _dot_nt(duv_ref[...], wuv_ref[...]) + _dot_nt(dxbc_ref[...], wxbc_ref[...])
        dh = dh + _dot_nt(dz_ref[...], wz_ref[...]) + _dot_nt(ddt_ref[...], wdt_ref[...])
        xv = x_ref[...]
        h, r = _rms_fwd(xv, nw_ref[...])
        dx, dnw = _rms_bwd(xv, r, nw_ref[...], dh)
        gx_ref[...] = dx1_ref[...] + dx
        h_ref[...] = h.astype(BF16)
        dnw_ref[...] += dnw

    return _rows_call("inproj_bwd", body, tm, [dp_uv, dp_xbc, dp_z, dp_dt, x, dx1], [w_uv, w_xbc, w_z, w_dt, nw],
                      [_sds((n_tok, D_MODEL), F32), _sds((n_tok, D_MODEL), BF16)], [_sds((1, D_MODEL), F32)],
                      carried=carried)


def _const_maps():
    lane = jnp.arange(SSM_WIDTH) // HEAD_DIM
    e_bf = (jnp.arange(DT_PAD)[:, None] == lane[None, :]).astype(BF16)
    return e_bf, e_bf.T


def _pad_lanes(v, width):
    return jnp.pad(v, ((0, 0), (0, width - v.shape[1])))


SHARD_COLS = IN_COLS // N_CHIPS
_UV_END = 2 * GM_WIDTH
_Z_END = _UV_END + SSM_WIDTH
_XBC_END = _Z_END + CONV_CH


def _cols_from_shards(w4, lo, hi):
    pieces = []
    for j in range(N_CHIPS):
        a, b = max(lo, j * SHARD_COLS), min(hi, (j + 1) * SHARD_COLS)
        if a < b:
            pieces.append(w4[j][:, a - j * SHARD_COLS:b - j * SHARD_COLS])
    return pieces[0] if len(pieces) == 1 else jnp.concatenate(pieces, axis=1)


def _shards_from_cols(blocks):
    shards = []
    for j in range(N_CHIPS):
        pieces = []
        for arr, lo, hi in blocks:
            a, b = max(lo, j * SHARD_COLS), min(hi, (j + 1) * SHARD_COLS)
            if a < b:
                pieces.append(arr[:, a - lo:b - lo])
        shards.append(pieces[0] if len(pieces) == 1 else jnp.concatenate(pieces, axis=1))
    return jnp.stack(shards)


def _forward_backward(x, tgt, w_in4, conv_w, small, out_shard, up_shard, down_shard, core, adam_args):
    n_seq, seq_len, _ = x.shape
    n_tok = n_seq * seq_len
    x2 = x.reshape(n_tok, D_MODEL)
    tgt2 = tgt.reshape(n_tok, D_MODEL)
    e_bf, et_bf = _const_maps()

    w_uv = _cols_from_shards(w_in4, 0, _UV_END)
    w_z = _cols_from_shards(w_in4, _UV_END, _Z_END)
    w_xbc = _cols_from_shards(w_in4, _Z_END, _XBC_END)
    w_dt = _pad_lanes(_cols_from_shards(w_in4, _XBC_END, IN_COLS), DT_PAD)

    nw_pre = small["norm_mix_pre"]
    lnw = small["gm_ln_w"].reshape(1, GM_WIDTH)
    lnb = small["gm_ln_b"].reshape(1, GM_WIDTH)
    w_stack = small["gm_w_s"].reshape(N_HEADS * CHUNK, CHUNK)
    w_cat = jnp.transpose(small["gm_w_s"], (1, 0, 2)).reshape(CHUNK, N_HEADS * CHUNK)
    bmap = jnp.repeat(small["gm_b_s"].T, HEAD_DIM, axis=1)
    cw3 = conv_w.reshape(CONV_K, 1, CONV_CH)
    conv_b = small["conv_b"]
    dt_bias = _pad_lanes(small["dt_bias"], DT_PAD)
    a_log = _pad_lanes(small["a_log"], DT_PAD)
    dskip_map = jnp.repeat(small["d_skip"], HEAD_DIM, axis=1)
    ssm_nw = small["ssm_norm_w"]

    half = down_shard.shape[0] // 2
    p_uv, p_xbc, p_z, p_dt, w_out4, w_down_a = _inproj_fwd(
        x2, nw_pre, w_uv, w_xbc, w_z, w_dt, carried=_allgather_exchange([out_shard, down_shard[:half]]))
    ssd_consts = (cw3, conv_b, dt_bias, a_log, dskip_map, ssm_nw, e_bf, et_bf)
    w_out_b = w_out4.reshape(D_MODEL, D_MODEL)
    ya, yb, yssd, sprev, o, x1, h2, w_up4, w_down_b = _mixer_fwd(
        p_uv, p_xbc, p_z, p_dt, x2, lnw, lnb, w_cat, bmap, w_out_b, small["norm_mix_post"], small["norm_ffn_pre"],
        *ssd_consts, n_seq, carried=_allgather_exchange([up_shard, down_shard[half:]]))
    f, dd, dy, loss_acc, d_nffn_post = _mlp_fwd(h2, x1, tgt2, w_up4, w_down_a, w_down_b, small["norm_ffn_post"])

    dup, dx1, d_nffn_pre = _mlp_bwd(dd, f, x1, dy, w_down_a, w_down_b, w_up4, small["norm_ffn_pre"])
    tk = min(DW_TOKENS_PER_STEP, n_tok)
    g_up = _matmul_tn("dw_up", h2, dup, D_MODEL, D_MODEL, tk, stacked=True)
    g_down = _matmul_tn("dw_down", f, dd, 1024, D_MODEL, tk).reshape(N_CHIPS, D_FF // N_CHIPS, D_MODEL)
    do, dya, dyb, d_nmix_post, got_up, got_down = _outproj_bwd(
        dx1, o, w_out_b, small["norm_mix_post"], carried=_pair_exchange([g_up, g_down]))
    h_up = _pair_sum(core, g_up, got_up, 256)
    h_down = _pair_sum(core, g_down, got_down, 256)
    g_out_a = _matmul_tn("dw_out_a", ya, do, GM_WIDTH, D_MODEL, tk)
    g_out_b = _matmul_tn("dw_out_b", yb, do, SSM_WIDTH, D_MODEL, tk)
    g_out = jnp.concatenate([g_out_a, g_out_b], axis=0).reshape(N_CHIPS, D_MODEL // N_CHIPS, D_MODEL)
    dp_uv, d_ws, d_bs_t, d_lnw, d_lnb, slab_up, got_out = _gmlp_bwd(
        p_uv, dya, lnw, lnb, e_bf, et_bf, w_cat, w_stack, bmap,
        carried=_both(_chip_exchange([h_up]), _pair_exchange([g_out])))
    h_out = _pair_sum(core, g_out, got_out, 128)
    early = {
        "gm_ln_w": d_lnw.reshape(N_HEADS, HEAD_DIM), "gm_ln_b": d_lnb.reshape(N_HEADS, HEAD_DIM),
        "gm_w_s": d_ws.reshape(N_HEADS, CHUNK, CHUNK), "gm_b_s": d_bs_t[:, :N_HEADS].T,
        "norm_mix_post": d_nmix_post, "norm_ffn_pre": d_nffn_pre, "norm_ffn_post": d_nffn_post,
    }
    packed_early = _pack(early, tuple(early), tail=loss_acc[0, 0].reshape(1))
    (dp_xbc, dp_z, dp_dt, d_cw, d_cb, d_dtb, d_alog, d_dsk, d_ssm_nw, slab_down, slab_out, all_early) = _ssd_bwd(
        p_xbc, p_z, p_dt, yssd, sprev, dyb, *ssd_consts, n_seq,
        carried=_both(_chip_exchange([h_down, h_out]), _device_gather_exchange(packed_early)))
    gx, h, d_nmix_pre = _inproj_bwd(dp_uv, dp_xbc, dp_z, dp_dt, x2, dx1, w_uv, w_xbc, w_z, w_dt, nw_pre)
    late = {
        "norm_mix_pre": d_nmix_pre, "conv_w": d_cw.reshape(CONV_K, CONV_CH), "conv_b": d_cb,
        "dt_bias": d_dtb[:, :N_HEADS], "a_log": d_alog[:, :N_HEADS], "d_skip": d_dsk[:, :N_HEADS],
        "ssm_norm_w": d_ssm_nw,
    }
    g_uv, all_late = _matmul_tn("dw_in_uv", h, dp_uv, D_MODEL, 2 * GM_WIDTH, tk,
                                carried=_device_gather_exchange(_pack(late, tuple(late))))
    sum_early = _ordered_sum("small_sum_early", all_early)
    small_sum = _unpack(sum_early, {n: v.shape for n, v in early.items()}, tuple(early))
    small_sum.update(_unpack(_ordered_sum("small_sum_late", all_late), {n: v.shape for n, v in late.items()}, tuple(late)))
    loss = sum_early.reshape(-1)[sum(v.size for v in early.values())]
    red_up, red_down, red_out = _chip_sum(slab_up, 256), _chip_sum(slab_down, 256), _chip_sum(slab_out, 128)
    g_xbc, oth_up, oth_down, oth_out = _matmul_tn("dw_in_xbc", h, dp_xbc, D_MODEL, CONV_CH, tk,
                                                  carried=_pair_swap([red_up, red_down, red_out]))
    g_z = _matmul_tn("dw_in_z", h, dp_z, D_MODEL, SSM_WIDTH, tk)
    g_dt = _matmul_tn("dw_in_dt", h, dp_dt, D_MODEL, DT_PAD, tk)

    g_in = _shards_from_cols([(g_uv, 0, _UV_END), (g_z, _UV_END, _Z_END), (g_xbc, _Z_END, _XBC_END),
                              (g_dt, _XBC_END, IN_COLS)])
    (got_in,) = _run_exchange("grad_pair_exchange_in", _pair_exchange([g_in]))
    h_in = _pair_sum(core, g_in, got_in, 256)
    (slab_in,) = _run_exchange("grad_chip_exchange", _chip_exchange([h_in]))
    res = _adamw_halves("adamw_mlp", [(adam_args["w_up"][0], red_up, oth_up) + adam_args["w_up"][1:],
                                      (adam_args["w_down"][0], red_down, oth_down) + adam_args["w_down"][1:]], 256)
    big_out = {"w_up": res[0:4], "w_down": res[4:8]}
    big_out["w_out"] = _adamw_halves("adamw_w_out", [(adam_args["w_out"][0], red_out, oth_out) + adam_args["w_out"][1:]], 128)
    red_in = _chip_sum(slab_in, 256)
    (oth_in,) = _run_exchange("grad_pair_swap_in", _pair_swap([red_in]))
    big_out["w_in"] = _adamw_halves("adamw_w_in", [(adam_args["w_in"][0], red_in, oth_in) + adam_args["w_in"][1:]], 256)

    return loss, gx.reshape(x.shape), big_out, small_sum


_HBM = pl.BlockSpec(memory_space=pltpu.HBM)


D2D_CHUNKS = 8
ICI_CHUNKS = 1
ROW_ALIGN = 16


def _row_chunks(rows, n_chunks):
    size = min(max(rows // n_chunks, ROW_ALIGN), rows)
    assert rows % size == 0
    return [(start, size) for start in range(0, rows, size)]


def _position():
    x, y, c = lax.axis_index("x"), lax.axis_index("y"), lax.axis_index("c")
    chips = [(1 - x, y), (x, 1 - y), (1 - x, 1 - y)]
    return x, y, c, chips


def _allgather_exchange(arrs):
    n = len(arrs)

    def copies(ins, outs, send_sems, recv_sems, local_sems):
        x, y, c, chips = _position()
        me = 2 * x + y
        sibling = (x, y, 1 - c)

        def copy(a, k, src, dst, to):
            return pltpu.make_async_remote_copy(src_ref=src, dst_ref=dst, send_sem=send_sems.at[a, k],
                                                recv_sem=recv_sems.at[a, k], device_id=to, device_id_type=MESH)

        def half_rows(a, pc):
            half = ins[a].shape[0] // 2
            return pl.ds(pc * half, half)

        local = [pltpu.make_async_copy(ins[a], outs[a].at[me], local_sems.at[a]) for a in range(n)]
        ici_out = [[copy(a, k, ins[a].at[half_rows(a, c)], outs[a].at[me, half_rows(a, c)], (px, py, c))
                    for k, (px, py) in enumerate(chips)] for a in range(n)]
        return c, chips, sibling, copy, half_rows, local, ici_out

    def start(ins, outs, send_sems, recv_sems, local_sems):
        c, chips, _, copy, _, local, _ = copies(ins, outs, send_sems, recv_sems, local_sems)
        x, y, _, _ = _position()
        me = 2 * x + y
        for cp in local:
            cp.start()
        for a in range(n):
            half = ins[a].shape[0] // 2
            for k, (px, py) in enumerate(chips):
                for first, size in _row_chunks(half, ICI_CHUNKS):
                    rows = pl.ds(c * half + first, size)
                    copy(a, k, ins[a].at[rows], outs[a].at[me, rows], (px, py, c)).start()

    def finish(ins, outs, send_sems, recv_sems, local_sems):
        c, chips, sibling, copy, half_rows, local, ici_out = copies(ins, outs, send_sems, recv_sems, local_sems)
        for a in range(n):
            half = ins[a].shape[0] // 2
            for k, (px, py) in enumerate(chips):
                blk = outs[a].at[2 * px + py, half_rows(a, c)]
                copy(a, k, blk, blk, (px, py, c)).wait_recv()
                for first, size in _row_chunks(half, D2D_CHUNKS):
                    piece = outs[a].at[2 * px + py, pl.ds(c * half + first, size)]
                    copy(a, 3 + k, piece, piece, sibling).start()
        for a in range(n):
            for k, (px, py) in enumerate(chips):
                theirs = outs[a].at[2 * px + py, half_rows(a, 1 - c)]
                copy(a, 3 + k, theirs, theirs, sibling).wait_recv()
                mine = outs[a].at[2 * px + py, half_rows(a, c)]
                copy(a, 3 + k, mine, mine, sibling).wait_send()
        for a in range(n):
            for cp in ici_out[a]:
                cp.wait_send()
        for cp in local:
            cp.wait()

    return _Carried(arrs, [_sds((N_CHIPS,) + a.shape, a.dtype) for a in arrs],
                    [pltpu.SemaphoreType.DMA((n, 6)), pltpu.SemaphoreType.DMA((n, 6)), pltpu.SemaphoreType.DMA((n,))],
                    start, finish)


def _run_exchange(name, exchange):
    n_in, n_out = len(exchange.ins), len(exchange.out_shapes)

    def body(*refs):
        ins, outs, sems = refs[:n_in], refs[n_in:n_in + n_out], refs[n_in + n_out:]
        exchange.start(ins, outs, *sems)
        exchange.finish(ins, outs, *sems)

    return pl.pallas_call(
        body, name=name, out_shape=tuple(exchange.out_shapes), in_specs=[_HBM] * n_in,
        out_specs=tuple([_HBM] * n_out), scratch_shapes=exchange.sems,
    )(*exchange.ins)


def _pair_exchange(grads):
    n = len(grads)

    def copier(send_sems, recv_sems):
        x, y, c, _ = _position()

        def copy(a, src, dst):
            return pltpu.make_async_remote_copy(src_ref=src, dst_ref=dst, send_sem=send_sems.at[a],
                                                recv_sem=recv_sems.at[a], device_id=(x, y, 1 - c), device_id_type=MESH)
        return c, copy

    def start(ins, got, send_sems, recv_sems):
        c, copy = copier(send_sems, recv_sems)
        for a in range(n):
            half = ins[a].shape[1] // 2
            for slab in range(N_CHIPS):
                for first, size in _row_chunks(half, D2D_CHUNKS):
                    copy(a, ins[a].at[slab, pl.ds((1 - c) * half + first, size), :],
                         got[a].at[slab, pl.ds(first, size), :]).start()

    def finish(ins, got, send_sems, recv_sems):
        c, copy = copier(send_sems, recv_sems)
        for a in range(n):
            half = ins[a].shape[1] // 2
            copy(a, ins[a].at[:, pl.ds((1 - c) * half, half), :], got[a]).wait()

    return _Carried(grads, [_sds((N_CHIPS, g.shape[1] // 2, g.shape[2]), g.dtype) for g in grads],
                    [pltpu.SemaphoreType.DMA((n,)), pltpu.SemaphoreType.DMA((n,))], start, finish)


def _chip_exchange(hsums):
    n = len(hsums)

    def copies(ins, outs, send_sems, recv_sems, local_sems, pieces):
        x, y, c, chips = _position()
        me = 2 * x + y
        cps = []
        for a in range(n):
            cps.append(pltpu.make_async_copy(ins[a].at[me], outs[a].at[me], local_sems.at[a]))
            rows = ins[a].shape[1]
            for k, (px, py) in enumerate(chips):
                for first, size in (_row_chunks(rows, ICI_CHUNKS) if pieces else [(0, rows)]):
                    cps.append(pltpu.make_async_remote_copy(
                        src_ref=ins[a].at[2 * px + py, pl.ds(first, size)], dst_ref=outs[a].at[me, pl.ds(first, size)],
                        send_sem=send_sems.at[a, k], recv_sem=recv_sems.at[a, k], device_id=(px, py, c),
                        device_id_type=MESH))
        return cps

    def start(*refs):
        for cp in copies(*refs, pieces=True):
            cp.start()

    def finish(*refs):
        for cp in copies(*refs, pieces=False):
            cp.wait()

    return _Carried(hsums, [_sds(h.shape, h.dtype) for h in hsums],
                    [pltpu.SemaphoreType.DMA((n, 3)), pltpu.SemaphoreType.DMA((n, 3)), pltpu.SemaphoreType.DMA((n,))],
                    start, finish)


def _pair_swap(reds):
    n = len(reds)

    def copier(send_sems, recv_sems):
        x, y, c, _ = _position()

        def copy(a, src, dst):
            return pltpu.make_async_remote_copy(src_ref=src, dst_ref=dst, send_sem=send_sems.at[a],
                                                recv_sem=recv_sems.at[a], device_id=(x, y, 1 - c), device_id_type=MESH)
        return copy

    def start(ins, outs, send_sems, recv_sems):
        copy = copier(send_sems, recv_sems)
        for a in range(n):
            for first, size in _row_chunks(ins[a].shape[0], 2 * D2D_CHUNKS):
                copy(a, ins[a].at[pl.ds(first, size), :], outs[a].at[pl.ds(first, size), :]).start()

    def finish(ins, outs, send_sems, recv_sems):
        copy = copier(send_sems, recv_sems)
        for a in range(n):
            copy(a, ins[a], outs[a]).wait()

    return _Carried(reds, [_sds(r.shape, r.dtype) for r in reds],
                    [pltpu.SemaphoreType.DMA((n,)), pltpu.SemaphoreType.DMA((n,))], start, finish)


def _device_gather_exchange(packed):
    def copies(ins, outs, send_sems, recv_sems, local_sem):
        (x_ref,), (all_ref,) = ins, outs
        x, y, c, chips = _position()
        me, sibling = (x, y, c), (x, y, 1 - c)

        def slab(px, py, pc):
            return all_ref.at[4 * px + 2 * py + pc]

        def copy(k, block, to, src=None):
            return pltpu.make_async_remote_copy(
                src_ref=slab(*block) if src is None else src, dst_ref=slab(*block), send_sem=send_sems.at[k],
                recv_sem=recv_sems.at[k], device_id=to, device_id_type=MESH)

        mine = pltpu.make_async_copy(x_ref, slab(*me), local_sem)
        first = [copy(0, me, sibling, src=x_ref)]
        first += [copy(1 + j, me, (*chip, c), src=x_ref) for j, chip in enumerate(chips)]
        passed = [copy(4 + j, (*chip, c), sibling) for j, chip in enumerate(chips)]
        return c, chips, me, sibling, copy, mine, first, passed

    def start(ins, outs, send_sems, recv_sems, local_sem):
        _, _, _, _, _, mine, first, _ = copies(ins, outs, send_sems, recv_sems, local_sem)
        mine.start()
        for cp in first:
            cp.start()

    def finish(ins, outs, send_sems, recv_sems, local_sem):
        c, chips, me, sibling, copy, mine, first, passed = copies(ins, outs, send_sems, recv_sems, local_sem)
        for j, chip in enumerate(chips):
            copy(1 + j, (*chip, c), me).wait_recv()
            passed[j].start()
        copy(0, sibling, me).wait_recv()
        for j, chip in enumerate(chips):
            copy(4 + j, (*chip, 1 - c), me).wait_recv()
        for cp in first + passed:
            cp.wait_send()
        mine.wait()

    return _Carried([packed], [_sds((N_DEV,) + packed.shape, F32)],
                    [pltpu.SemaphoreType.DMA((7,)), pltpu.SemaphoreType.DMA((7,)), pltpu.SemaphoreType.DMA],
                    start, finish)


def _ordered_sum(name, slabs):
    _, m_per, n_cols = slabs.shape

    def body(s_ref, o_ref):
        acc = s_ref[0]
        for d in range(1, N_DEV):
            acc = acc + s_ref[d]
        o_ref[...] = acc

    vmem = pl.BlockSpec(memory_space=pltpu.VMEM)
    return pl.pallas_call(body, name=name, out_shape=_sds((m_per, n_cols), F32), in_specs=[vmem], out_specs=vmem)(slabs)


def _pair_sum(core, own, got, tm):
    _, half, cols = got.shape
    nb = half // tm

    def body(c_ref, a_ref, b_ref, o_ref):
        o_ref[...] = (a_ref[...].astype(F32) + b_ref[...].astype(F32)).astype(BF16)

    return pl.pallas_call(
        body, name="grad_pair_sum", out_shape=_sds(got.shape, BF16),
        grid_spec=pltpu.PrefetchScalarGridSpec(
            num_scalar_prefetch=1, grid=(N_CHIPS, nb),
            in_specs=[pl.BlockSpec((None, tm, cols), lambda s, i, c_ref: (s, c_ref[0] * nb + i, 0)),
                      pl.BlockSpec((None, tm, cols), lambda s, i, c_ref: (s, i, 0))],
            out_specs=pl.BlockSpec((None, tm, cols), lambda s, i, c_ref: (s, i, 0))),
        compiler_params=_cparams(2),
    )(core, own, got)


def _chip_sum(slabs, tm):
    _, half, cols = slabs.shape

    def body(s_ref, o_ref):
        acc = s_ref[0].astype(F32)
        for k in range(1, N_CHIPS):
            acc = acc + s_ref[k].astype(F32)
        o_ref[...] = acc

    return pl.pallas_call(
        body, name="grad_chip_sum", out_shape=_sds((half, cols), F32), grid=(half // tm,),
        in_specs=[pl.BlockSpec((N_CHIPS, tm, cols), lambda i: (0, i, 0))],
        out_specs=pl.BlockSpec((tm, cols), lambda i: (i, 0)), compiler_params=_cparams(1),
    )(slabs)


def _adam_math(w, g, m, v):
    m2 = ADAM_B1 * m + (1.0 - ADAM_B1) * g
    v2 = ADAM_B2 * v + (1.0 - ADAM_B2) * (g * g)
    m_hat = m2 / (1.0 - ADAM_B1 ** ADAM_STEP)
    v_hat = v2 / (1.0 - ADAM_B2 ** ADAM_STEP)
    delta = -ADAM_LR * (m_hat / (jnp.sqrt(v_hat) + ADAM_EPS) + ADAM_WD * w)
    return delta, m2, v2


def _adamw_halves(name, items, tm, carried=None):
    rows, cols = items[0][0].shape
    nb = rows // 2 // tm
    n = len(items)

    def body(*refs):
        mine = (pl.program_id(0) // nb) == lax.axis_index("c")
        for k in range(n):
            w_ref, own_ref, oth_ref, m_ref, v_ref = refs[5 * k:5 * k + 5]
            g_ref, d_ref, m2_ref, v2_ref = refs[5 * n + 4 * k:5 * n + 4 * k + 4]
            g = jnp.where(mine, own_ref[...], oth_ref[...])
            d, m2, v2 = _adam_math(w_ref[...], g, m_ref[...], v_ref[...])
            g_ref[...] = g
            d_ref[...] = d
            m2_ref[...] = m2
            v2_ref[...] = v2

    full = pl.BlockSpec((tm, cols), lambda i: (i, 0))
    half = pl.BlockSpec((tm, cols), lambda i: (i % nb, 0))
    return _call_carrying(
        body, carried, name=name, grid=(rows // tm,), in_specs=[full, half, half, full, full] * n,
        out_specs=[full] * (4 * n), out_shape=tuple([_sds((rows, cols), F32)] * (4 * n)), scratch_shapes=[],
        operands=[a for item in items for a in item])


def _adamw(name, w, g, m, v, tm):
    def body(w_ref, g_ref, m_ref, v_ref, gout_ref, d_ref, m2_ref, v2_ref):
        gv = g_ref[...]
        d, m2, v2 = _adam_math(w_ref[...], gv, m_ref[...], v_ref[...])
        gout_ref[...] = gv
        d_ref[...] = d
        m2_ref[...] = m2
        v2_ref[...] = v2

    return _rows_call(name, body, tm, [w, g, m, v], [], [_sds(w.shape, F32)] * 4)


_SMALL_NAMES = ("norm_mix_pre", "gm_ln_w", "gm_ln_b", "gm_w_s", "gm_b_s", "conv_w", "conv_b", "dt_bias", "a_log",
                "d_skip", "ssm_norm_w", "norm_mix_post", "norm_ffn_pre", "norm_ffn_post")
_PACK_COLS = 1024


def _pack(parts, names=_SMALL_NAMES, tail=None):
    pieces = [parts[n].reshape(-1) for n in names]
    flat = jnp.concatenate(pieces if tail is None else pieces + [tail])
    rows = -(-flat.shape[0] // (8 * _PACK_COLS)) * 8
    flat = jnp.pad(flat, (0, rows * _PACK_COLS - flat.shape[0]))
    return flat.reshape(rows, _PACK_COLS)


def _unpack(packed, shapes, names=_SMALL_NAMES):
    flat = packed.reshape(-1)
    out, off = {}, 0
    for n in names:
        size = 1
        for s in shapes[n]:
            size *= s
        out[n] = flat[off:off + size].reshape(shapes[n])
        off += size
    return out


def kernel(x, norm_mix_pre, w_in, gm_ln_w, gm_ln_b, gm_w_s, gm_b_s, conv_w, conv_b, dt_bias, a_log, d_skip, ssm_norm_w, w_out, norm_mix_post, norm_ffn_pre, w_up, w_down, norm_ffn_post, loss_target, m_norm_mix_pre, m_w_in, m_gm_ln_w, m_gm_ln_b, m_gm_w_s, m_gm_b_s, m_conv_w, m_conv_b, m_dt_bias, m_a_log, m_d_skip, m_ssm_norm_w, m_w_out, m_norm_mix_post, m_norm_ffn_pre, m_w_up, m_w_down, m_norm_ffn_post, v_norm_mix_pre, v_w_in, v_gm_ln_w, v_gm_ln_b, v_gm_w_s, v_gm_b_s, v_conv_w, v_conv_b, v_dt_bias, v_a_log, v_d_skip, v_ssm_norm_w, v_w_out, v_norm_mix_post, v_norm_ffn_pre, v_w_up, v_w_down, v_norm_ffn_post):
    params = dict(norm_mix_pre=norm_mix_pre, w_in=w_in, gm_ln_w=gm_ln_w, gm_ln_b=gm_ln_b, gm_w_s=gm_w_s, gm_b_s=gm_b_s,
                  conv_w=conv_w, conv_b=conv_b, dt_bias=dt_bias, a_log=a_log, d_skip=d_skip, ssm_norm_w=ssm_norm_w,
                  w_out=w_out, norm_mix_post=norm_mix_post, norm_ffn_pre=norm_ffn_pre, w_up=w_up, w_down=w_down,
                  norm_ffn_post=norm_ffn_post)
    mom1 = dict(norm_mix_pre=m_norm_mix_pre, w_in=m_w_in, gm_ln_w=m_gm_ln_w, gm_ln_b=m_gm_ln_b, gm_w_s=m_gm_w_s,
                gm_b_s=m_gm_b_s, conv_w=m_conv_w, conv_b=m_conv_b, dt_bias=m_dt_bias, a_log=m_a_log, d_skip=m_d_skip,
                ssm_norm_w=m_ssm_norm_w, w_out=m_w_out, norm_mix_post=m_norm_mix_post, norm_ffn_pre=m_norm_ffn_pre,
                w_up=m_w_up, w_down=m_w_down, norm_ffn_post=m_norm_ffn_post)
    mom2 = dict(norm_mix_pre=v_norm_mix_pre, w_in=v_w_in, gm_ln_w=v_gm_ln_w, gm_ln_b=v_gm_ln_b, gm_w_s=v_gm_w_s,
                gm_b_s=v_gm_b_s, conv_w=v_conv_w, conv_b=v_conv_b, dt_bias=v_dt_bias, a_log=v_a_log, d_skip=v_d_skip,
                ssm_norm_w=v_ssm_norm_w, w_out=v_w_out, norm_mix_post=v_norm_mix_post, norm_ffn_pre=v_norm_ffn_pre,
                w_up=v_w_up, w_down=v_w_down, norm_ffn_post=v_norm_ffn_post)
    names = list(params)
    big = ("w_in", "w_out", "w_up", "w_down")
    chip = 2 * lax.axis_index("x") + lax.axis_index("y")

    shards = {n: params[n][0].astype(BF16) for n in big}
    conv_shard = jnp.pad(conv_w[0], ((0, 16 - CONV_K), (0, 0)))
    g_in4, g_conv4 = _run_exchange("allgather_w_in", _allgather_exchange([shards["w_in"], conv_shard]))
    conv_full = jnp.transpose(g_conv4[:, :CONV_K, :], (1, 0, 2)).reshape(CONV_K, CONV_CH)

    small = {n: params[n][0] if params[n].ndim >= 3 else params[n] for n in _SMALL_NAMES if n != "conv_w"}
    core = lax.axis_index("c").astype(jnp.int32).reshape(1)
    adam_args = {n: (params[n][0], mom1[n][0], mom2[n][0]) for n in big}
    loss, grad_x, big_out, small_sum = _forward_backward(
        x, loss_target, g_in4, conv_full, small, shards["w_out"], shards["w_up"], shards["w_down"], core, adam_args)
    grads, delta, new_m, new_v = {}, {}, {}, {}
    for n in big:
        grads[n], delta[n], new_m[n], new_v[n] = [a[None] for a in big_out[n]]

    small_sum["conv_w"] = lax.dynamic_slice_in_dim(small_sum["conv_w"], chip * (CONV_CH // N_CHIPS), CONV_CH // N_CHIPS, axis=1)

    local_shapes = {n: params[n].shape[1:] if params[n].ndim >= 3 else params[n].shape for n in _SMALL_NAMES}
    flat = lambda tree: {n: tree[n].reshape(local_shapes[n]) for n in _SMALL_NAMES}
    packed = [_pack(flat(t)) for t in (params, small_sum, mom1, mom2)]
    _, d_p, m_p, v_p = _adamw("adamw_small", *packed, packed[0].shape[0])
    for src, dst in ((d_p, delta), (m_p, new_m), (v_p, new_v)):
        for n, val in _unpack(src, local_shapes).items():
            dst[n] = val.reshape(params[n].shape)
    for n in _SMALL_NAMES:
        grads[n] = small_sum[n].reshape(params[n].shape)

    out = [loss, grad_x]
    for tree in (grads, delta, new_m, new_v):
        out += [tree[n] for n in names]
    return tuple(out)
```

```python
import functools

import jax
import jax.numpy as jnp
from jax import lax
from jax.experimental import pallas as pl
from jax.experimental.pallas import tpu as pltpu

F32 = jnp.float32
BF16 = jnp.bfloat16
HI = lax.Precision.HIGHEST
MESH = pl.DeviceIdType.MESH

EPS = 1e-6
D_MODEL = 1024
GM_WIDTH = 512
SSM_WIDTH = 512
N_HEADS = 8
HEAD_DIM = 64
CHUNK = 128
SSM_GROUPS = 2
GROUP_W = SSM_WIDTH // SSM_GROUPS
SSM_STATE = 128
CONV_K = 4
CONV_CH = 1024
D_FF = 4096
IN_COLS = 2568
DT_PAD = 128
N_CHIPS = 4
N_DEV = 8

ADAM_LR = 0.001
ADAM_B1 = 0.9
ADAM_B2 = 0.999
ADAM_EPS = 1e-08
ADAM_WD = 0.01
ADAM_STEP = 10

VMEM_LIMIT_BYTES = 56 * 1024 * 1024
FF_TILE = 512
DW_TOKENS_PER_STEP = 2048


def _cparams(n_axes):
    return pltpu.CompilerParams(dimension_semantics=("arbitrary",) * n_axes, vmem_limit_bytes=VMEM_LIMIT_BYTES)


def _dot(a, b):
    return jnp.dot(a.astype(BF16), b.astype(BF16), preferred_element_type=F32)


def _dot_nt(a, b):
    return lax.dot_general(a.astype(BF16), b.astype(BF16), (((1,), (1,)), ((), ())), preferred_element_type=F32)


def _dot_tn(a, b):
    return lax.dot_general(a.astype(BF16), b.astype(BF16), (((0,), (0,)), ((), ())), preferred_element_type=F32)


def _sigmoid(x):
    return 1.0 / (1.0 + jnp.exp(-x))


_GELU_C = 0.7978845608028654
_GELU_A = 0.044715


def _gelu(x):
    t = jnp.tanh(_GELU_C * (x + _GELU_A * (x * x * x)))
    return 0.5 * x * (1.0 + t), t


def _gelu_grad(x, t):
    return 0.5 * (1.0 + t) + 0.5 * x * (1.0 - t * t) * (_GELU_C * (1.0 + 3.0 * _GELU_A * x * x))


def _rms_fwd(x, w):
    r = lax.rsqrt(jnp.mean(x * x, axis=-1, keepdims=True) + EPS)
    return x * r * w, r


def _rms_bwd(x, r, w, dy):
    g = dy * w
    dx = r * g - x * (r * r * r) * jnp.mean(g * x, axis=-1, keepdims=True)
    dw = jnp.sum(dy * x * r, axis=0, keepdims=True)
    return dx, dw


class _Carried:
    def __init__(self, ins, out_shapes, sems, start, finish):
        self.ins, self.out_shapes, self.sems = list(ins), list(out_shapes), list(sems)
        self.start, self.finish = start, finish


def _both(first, second):
    n_i, n_o, n_s = len(first.ins), len(first.out_shapes), len(first.sems)

    def split(ins, outs, sems):
        return (ins[:n_i], outs[:n_o], sems[:n_s]), (ins[n_i:], outs[n_o:], sems[n_s:])

    def start(ins, outs, *sems):
        (i1, o1, s1), (i2, o2, s2) = split(ins, outs, sems)
        first.start(i1, o1, *s1)
        second.start(i2, o2, *s2)

    def finish(ins, outs, *sems):
        (i1, o1, s1), (i2, o2, s2) = split(ins, outs, sems)
        first.finish(i1, o1, *s1)
        second.finish(i2, o2, *s2)

    return _Carried(first.ins + second.ins, first.out_shapes + second.out_shapes, first.sems + second.sems, start, finish)


def _split_carried(refs, n_in, n_out, n_scratch, carried):
    n_ci, n_co, n_cs = len(carried.ins), len(carried.out_shapes), len(carried.sems)
    ins, rest = refs[:n_in], refs[n_in:]
    c_ins, rest = rest[:n_ci], rest[n_ci:]
    outs, rest = rest[:n_out], rest[n_out:]
    c_outs, rest = rest[:n_co], rest[n_co:]
    scr, c_sems = rest[:n_scratch], rest[n_scratch:]
    assert len(c_sems) == n_cs
    return tuple(ins) + tuple(outs) + tuple(scr), c_ins, c_outs, c_sems


def _rows_call(name, body, tm, row_ins, const_ins, row_outs, acc_outs=(), scratch=(), carried=None):
    n_rows = row_ins[0].shape[0]
    assert n_rows % tm == 0
    n_steps = n_rows // tm
    n_in = len(row_ins) + len(const_ins)
    n_ro = len(row_outs)
    n_acc = len(acc_outs)

    def kern(*refs):
        accs = refs[n_in + n_ro:n_in + n_ro + n_acc]

        @pl.when(pl.program_id(0) == 0)
        def _():
            for a in accs:
                a[...] = jnp.zeros_like(a)

        body(*refs)

    def whole(shape):
        nd = len(shape)
        return pl.BlockSpec(tuple(shape), lambda i: (0,) * nd)

    in_specs = [pl.BlockSpec((tm, a.shape[1]), lambda i: (i, 0)) for a in row_ins]
    in_specs += [whole(a.shape) for a in const_ins]
    out_specs = [pl.BlockSpec((tm, s.shape[1]), lambda i: (i, 0)) for s in row_outs]
    out_specs += [whole(s.shape) for s in acc_outs]
    return _call_carrying(
        kern, carried, name=name, grid=(n_steps,), in_specs=in_specs, out_specs=out_specs,
        out_shape=tuple(row_outs) + tuple(acc_outs), scratch_shapes=list(scratch), operands=list(row_ins) + list(const_ins))


def _call_carrying(body, carried, *, name, grid, in_specs, out_specs, out_shape, scratch_shapes, operands):
    n_in, n_out, n_scratch = len(in_specs), len(out_specs), len(scratch_shapes)
    kern = body
    if carried is not None:
        def kern(*refs):
            plain, c_ins, c_outs, c_sems = _split_carried(refs, n_in, n_out, n_scratch, carried)
            first, last = True, True
            for d, size in enumerate(grid):
                first = jnp.logical_and(first, pl.program_id(d) == 0)
                last = jnp.logical_and(last, pl.program_id(d) == size - 1)

            @pl.when(first)
            def _():
                carried.start(c_ins, c_outs, *c_sems)

            body(*plain)

            @pl.when(last)
            def _():
                carried.finish(c_ins, c_outs, *c_sems)

        in_specs = list(in_specs) + [_HBM] * len(carried.ins)
        out_specs = list(out_specs) + [_HBM] * len(carried.out_shapes)
        out_shape = tuple(out_shape) + tuple(carried.out_shapes)
        operands = list(operands) + carried.ins
        scratch_shapes = list(scratch_shapes) + carried.sems
    return pl.pallas_call(
        kern, name=name, grid=grid, in_specs=in_specs, out_specs=out_specs, out_shape=out_shape,
        scratch_shapes=scratch_shapes, compiler_params=_cparams(len(grid)),
    )(*operands)


def _sds(shape, dtype):
    return jax.ShapeDtypeStruct(tuple(shape), dtype)


def _matmul_tn(name, a, b, tm, tn, tk, stacked=False, carried=None):
    k_dim, m_dim = a.shape
    n_dim = b.shape[1]
    assert m_dim % tm == 0 and n_dim % tn == 0 and k_dim % tk == 0
    nk = k_dim // tk

    def kern(a_ref, b_ref, o_ref, acc_ref):
        k = pl.program_id(2)
        prod = _dot_tn(a_ref[...], b_ref[...])

        @pl.when(k == 0)
        def _():
            acc_ref[...] = prod

        @pl.when(k > 0)
        def _():
            acc_ref[...] += prod

        @pl.when(k == nk - 1)
        def _():
            o_ref[...] = acc_ref[...].astype(o_ref.dtype)

    if stacked:
        assert tm == m_dim
        out_shape = _sds((n_dim // tn, m_dim, tn), BF16)
        out_spec = pl.BlockSpec((None, tm, tn), lambda i, j, k: (j, i, 0))
    else:
        out_shape = _sds((m_dim, n_dim), BF16)
        out_spec = pl.BlockSpec((tm, tn), lambda i, j, k: (i, j))
    outs = _call_carrying(
        kern, carried, name=name, grid=(m_dim // tm, n_dim // tn, nk),
        in_specs=[pl.BlockSpec((tk, tm), lambda i, j, k: (k, i)), pl.BlockSpec((tk, tn), lambda i, j, k: (k, j))],
        out_specs=[out_spec], out_shape=(out_shape,), scratch_shapes=[pltpu.VMEM((tm, tn), F32)], operands=[a, b])
    return outs[0] if carried is None else outs


def _inproj_fwd(x, nw, w_uv, w_xbc, w_z, w_dt, tm=256, carried=None):
    n_tok = x.shape[0]

    def body(x_ref, nw_ref, wuv_ref, wxbc_ref, wz_ref, wdt_ref, puv_ref, pxbc_ref, pz_ref, pdt_ref):
        h, _ = _rms_fwd(x_ref[...], nw_ref[...])
        h = h.astype(BF16)
        puv_ref[...] = jnp.dot(h, wuv_ref[...], preferred_element_type=F32)
        pxbc_ref[...] = jnp.dot(h, wxbc_ref[...], preferred_element_type=F32)
        pz_ref[...] = jnp.dot(h, wz_ref[...], preferred_element_type=F32)
        pdt_ref[...] = jnp.dot(h, wdt_ref[...], preferred_element_type=F32)

    return _rows_call(
        "inproj_fwd", body, tm, [x], [nw, w_uv, w_xbc, w_z, w_dt],
        [_sds((n_tok, 2 * GM_WIDTH), F32), _sds((n_tok, CONV_CH), F32), _sds((n_tok, SSM_WIDTH), F32),
         _sds((n_tok, DT_PAD), F32)], carried=carried)


def _head_lane_mask(width, head):
    lane = lax.broadcasted_iota(jnp.int32, (1, width), 1)
    return (lane // HEAD_DIM) == head


def _split_terms(x, terms):
    parts = []
    for _ in range(terms):
        p = x.astype(BF16)
        parts.append(p)
        x = x - p.astype(F32)
    return parts


def _seg_dots(vals, ind, terms=2):
    m = vals[0].shape[0]
    parts = []
    for v in vals:
        parts += _split_terms(v, terms)
    red = jnp.dot(jnp.concatenate(parts, axis=0), ind, preferred_element_type=F32)
    outs = []
    for i in range(len(vals)):
        acc = red[i * terms * m:(i * terms + 1) * m]
        for t in range(1, terms):
            acc = acc + red[(i * terms + t) * m:(i * terms + t + 1) * m]
        outs.append(acc)
    return outs


def _tri_dot(mask, x, terms=3):
    n = x.shape[1]
    red = jnp.dot(mask.astype(BF16), jnp.concatenate(_split_terms(x, terms), axis=1), preferred_element_type=F32)
    acc = red[:, :n]
    for t in range(1, terms):
        acc = acc + red[:, t * n:(t + 1) * n]
    return acc


def _gmlp_common(puv, lnw, lnb, e_bf, et_bf):
    u = puv[:, :GM_WIDTH]
    v = puv[:, GM_WIDTH:]
    gu, tu = _gelu(u)
    gv, tv = _gelu(v)
    (s1,) = _seg_dots([gv], et_bf)
    (mu,) = _seg_dots([s1 * (1.0 / HEAD_DIM)], e_bf)
    xc = gv - mu
    (s2,) = _seg_dots([xc * xc], et_bf)
    (rstd,) = _seg_dots([lax.rsqrt(s2 * (1.0 / HEAD_DIM) + EPS)], e_bf)
    xhat = xc * rstd
    vn = xhat * lnw + lnb
    return u, v, gu, tu, tv, rstd, xhat, vn


def _tril_mask():
    r = lax.broadcasted_iota(jnp.int32, (CHUNK, CHUNK), 0)
    c = lax.broadcasted_iota(jnp.int32, (CHUNK, CHUNK), 1)
    return r >= c


def _head_blocks(v):
    return jnp.concatenate([jnp.where(_head_lane_mask(GM_WIDTH, h), v, jnp.zeros_like(v)) for h in range(N_HEADS)], axis=0)


def _causal_w_cat(w_cat):
    t = lax.broadcasted_iota(jnp.int32, (CHUNK, N_HEADS * CHUNK), 0)
    s = lax.broadcasted_iota(jnp.int32, (CHUNK, N_HEADS * CHUNK), 1) % CHUNK
    return jnp.where(t >= s, w_cat, 0.0).astype(BF16)


def _gmlp_chunk_fwd(puv, lnw, lnb, e_bf, et_bf, wm, bmap):
    _, _, gu, _, _, _, _, vn = _gmlp_common(puv, lnw, lnb, e_bf, et_bf)
    mixed = jnp.dot(wm, _head_blocks(vn.astype(BF16)), preferred_element_type=F32) + bmap
    return (gu * mixed).astype(BF16)


SUBLANES = 8


def _shift_down(x, tail, s):
    main = pltpu.roll(x, s, 0)
    row = lax.broadcasted_iota(jnp.int32, (SUBLANES, 1), 0)
    head = jnp.where(row < s, pltpu.roll(tail, s, 0), main[:SUBLANES])
    return jnp.concatenate([head, main[SUBLANES:]], axis=0)


def _shift_up(x, head_next, s):
    n = x.shape[0]
    main = pltpu.roll(x, n - s, 0)
    row = lax.broadcasted_iota(jnp.int32, (SUBLANES, 1), 0)
    last = jnp.where(row >= SUBLANES - s, pltpu.roll(head_next, SUBLANES - s, 0), main[n - SUBLANES:])
    return jnp.concatenate([main[:n - SUBLANES], last], axis=0)


def _ssd_pre(xr, tail, cw_ref, cb, pdt, dtb, alog, emap):
    rowi = lax.broadcasted_iota(jnp.int32, (CHUNK, 1), 0)
    shifted = [_shift_down(xr, tail, 3), _shift_down(xr, tail, 2), _shift_down(xr, tail, 1), xr]
    xc = cb
    for k in range(CONV_K):
        xc = xc + cw_ref[k] * shifted[k]
    sg = _sigmoid(xc)
    xa = xc * sg
    pre = pdt + dtb
    dt = jnp.maximum(pre, 0.0) + jnp.log(1.0 + jnp.exp(-jnp.abs(pre)))
    a_neg = -jnp.exp(alog)
    a_cs = _tri_dot(_tril_mask(), dt * a_neg)
    acs_map, dt_map = _seg_dots([a_cs, dt], emap, terms=3)
    return dict(shifted=shifted, xc=xc, sg=sg, xa=xa, pre=pre, dt=dt, a_neg=a_neg, a_cs=a_cs,
                acs_map=acs_map, dt_map=dt_map, rowi=rowi)


def _ssd_maps(p):
    last = p["rowi"] == CHUNK - 1
    aq_map = jnp.sum(jnp.where(last, p["acs_map"], 0.0), axis=0, keepdims=True)
    e_exp = jnp.exp(p["acs_map"])
    dte = jnp.exp(aq_map - p["acs_map"])
    cd = jnp.exp(aq_map)
    return last, e_exp, dte, cd


def _head_decay(a_cs, a_cs_t, head, tri):
    lane = lax.broadcasted_iota(jnp.int32, (1, DT_PAD), 1)
    sub = lax.broadcasted_iota(jnp.int32, (DT_PAD, 1), 0)
    col = jnp.sum(jnp.where(lane == head, a_cs, 0.0), axis=1, keepdims=True)
    row = jnp.sum(jnp.where(sub == head, a_cs_t, 0.0), axis=0, keepdims=True)
    return jnp.exp(jnp.where(tri, col - row, -1e30))


def _gate_fwd(y, z, nw):
    sz = _sigmoid(z)
    zg = z * sz
    yg = y * zg
    outs, rs = [], []
    for g in range(SSM_GROUPS):
        gs = slice(g * GROUP_W, (g + 1) * GROUP_W)
        o, r = _rms_fwd(yg[:, gs], nw[:, gs])
        outs.append(o)
        rs.append(r)
    return sz, zg, yg, outs, rs


def _ssd_const_specs():
    def whole(shape):
        nd = len(shape)
        return pl.BlockSpec(tuple(shape), lambda c: (0,) * nd)
    return [whole((CONV_K, 1, CONV_CH)), whole((1, CONV_CH)), whole((1, DT_PAD)), whole((1, DT_PAD)),
            whole((1, SSM_WIDTH)), whole((1, SSM_WIDTH)), whole((DT_PAD, SSM_WIDTH)), whole((SSM_WIDTH, DT_PAD))]


def _mixer_fwd(p_uv, p_xbc, p_z, p_dt, x, lnw, lnb, w_cat, bmap, w_out, nw_post, nw_pre2, conv_w, conv_b, dt_bias, a_log,
               dskip_map, norm_w, e_bf, et_bf, n_seq, carried=None):
    n_tok = p_xbc.shape[0]
    nc = n_tok // n_seq // CHUNK

    def body(puv3, xr3, z3, pdt3, x3, lnw_ref, lnb_ref, wcat_ref, bmap_ref, wo_ref, nwa_ref, nwb_ref,
             cw_ref, cb_ref, dtb_ref, alog_ref, dsk_ref, nw_ref, e_ref, et_ref,
             mix3, yssd3, sprev3, o3, x13, h23, wm_scr, prev3_scr, s3_scr):
        @pl.when(pl.program_id(0) == 0)
        def _():
            wm_scr[...] = _causal_w_cat(wcat_ref[...])
            prev3_scr[...] = jnp.zeros_like(prev3_scr)
            s3_scr[...] = jnp.zeros_like(s3_scr)

        for b in range(n_seq):
            one_sequence(puv3.at[b], xr3.at[b], z3.at[b], pdt3.at[b], lnw_ref, lnb_ref, bmap_ref,
                         cw_ref, cb_ref, dtb_ref, alog_ref, dsk_ref, nw_ref, e_ref, et_ref,
                         mix3.at[b], yssd3.at[b], sprev3.at[b], wm_scr, prev3_scr.at[b], s3_scr.at[b])
            o = jnp.dot(mix3[b], wo_ref[...], preferred_element_type=F32)
            on, _ = _rms_fwd(o, nwa_ref[...])
            x1 = x3[b] + on
            h2, _ = _rms_fwd(x1, nwb_ref[...])
            o3[b] = o
            x13[b] = x1
            h23[b] = h2.astype(BF16)

    def one_sequence(puv_ref, xr_ref, z_ref, pdt_ref, lnw_ref, lnb_ref, bmap_ref,
                     cw_ref, cb_ref, dtb_ref, alog_ref, dsk_ref, nw_ref, e_ref, et_ref,
                     mix_ref, yssd_ref, sprev_ref, wm_scr, prev_scr, s_scr):
        mix_ref[:, :GM_WIDTH] = _gmlp_chunk_fwd(puv_ref[...], lnw_ref[...], lnb_ref[...], e_ref[...], et_ref[...], wm_scr[...],
                                      bmap_ref[...])
        xr = xr_ref[...]
        p = _ssd_pre(xr, prev_scr[...], cw_ref, cb_ref[...], pdt_ref[...], dtb_ref[...], alog_ref[...], e_ref[...])
        _, e_exp, dte, cd = _ssd_maps(p)
        xs = p["xa"][:, :SSM_WIDTH]
        xd = xs * p["dt_map"]
        a_cs_t = p["a_cs"].T
        tri = _tril_mask()
        s_old = s_scr[...]
        sprev_ref[...] = s_old
        for g in range(SSM_GROUPS):
            gs = slice(g * GROUP_W, (g + 1) * GROUP_W)
            bm = p["xa"][:, SSM_WIDTH + g * SSM_STATE: SSM_WIDTH + (g + 1) * SSM_STATE].astype(BF16)
            cm = p["xa"][:, SSM_WIDTH + (SSM_GROUPS + g) * SSM_STATE: SSM_WIDTH + (SSM_GROUPS + g + 1) * SSM_STATE].astype(BF16)
            cb_mat = _dot_nt(cm, bm)
            xdg = xd[:, gs].astype(BF16)
            y_g = _dot(cm, s_old[:, gs]) * e_exp[:, gs] + dsk_ref[:, gs] * xs[:, gs]
            for r in range(SSM_GROUPS * 2):
                dm = _head_decay(p["a_cs"], a_cs_t, g * 4 + r, tri)
                full = jnp.dot((cb_mat * dm).astype(BF16), xdg, preferred_element_type=F32)
                y_g = y_g + jnp.where(_head_lane_mask(GROUP_W, r), full, 0.0)
            yssd_ref[:, gs] = y_g
            s_scr[:, gs] = cd[:, gs] * s_old[:, gs] + _dot_tn(bm, xd[:, gs] * dte[:, gs])
        _, _, _, outs, _ = _gate_fwd(yssd_ref[...], z_ref[...], nw_ref[...])
        for g in range(SSM_GROUPS):
            mix_ref[:, GM_WIDTH + g * GROUP_W:GM_WIDTH + (g + 1) * GROUP_W] = outs[g].astype(BF16)
        prev_scr[...] = xr[CHUNK - SUBLANES:, :]

    seq_len = n_tok // n_seq

    def rows(width):
        return pl.BlockSpec((n_seq, CHUNK, width), lambda c: (0, c, 0))

    def whole(shape):
        nd = len(shape)
        return pl.BlockSpec(tuple(shape), lambda c: (0,) * nd)

    def by_seq(a):
        return a.reshape(n_seq, seq_len, a.shape[-1])

    outs = _call_carrying(
        body, carried, name="mixer_fwd", grid=(nc,),
        in_specs=[rows(2 * GM_WIDTH), rows(CONV_CH), rows(SSM_WIDTH), rows(DT_PAD), rows(D_MODEL), whole(lnw.shape),
                  whole(lnb.shape), whole(w_cat.shape), whole(bmap.shape), whole(w_out.shape), whole(nw_post.shape),
                  whole(nw_pre2.shape)] + _ssd_const_specs(),
        out_specs=[rows(D_MODEL), rows(SSM_WIDTH), rows(SSM_WIDTH), rows(D_MODEL), rows(D_MODEL), rows(D_MODEL)],
        out_shape=(_sds((n_seq, seq_len, D_MODEL), BF16),
                   _sds((n_seq, seq_len, SSM_WIDTH), F32), _sds((n_seq, seq_len, SSM_WIDTH), F32),
                   _sds((n_seq, seq_len, D_MODEL), F32), _sds((n_seq, seq_len, D_MODEL), F32),
                   _sds((n_seq, seq_len, D_MODEL), BF16)),
        scratch_shapes=[pltpu.VMEM((CHUNK, N_HEADS * CHUNK), BF16), pltpu.VMEM((n_seq, SUBLANES, CONV_CH), F32),
                        pltpu.VMEM((n_seq, SSM_STATE, SSM_WIDTH), F32)],
        operands=[by_seq(p_uv), by_seq(p_xbc), by_seq(p_z), by_seq(p_dt), by_seq(x), lnw, lnb, w_cat, bmap, w_out, nw_post,
                  nw_pre2, conv_w, conv_b, dt_bias, a_log, dskip_map, norm_w, e_bf, et_bf])
    return tuple(o.reshape(n_tok, o.shape[-1]) for o in outs[:6]) + tuple(outs[6:])


def _up_cols(wup_ref, j):
    per = (D_FF // N_CHIPS) // FF_TILE
    return wup_ref[j // per, :, (j % per) * FF_TILE:(j % per + 1) * FF_TILE]


def _down_rows(wda_ref, wdb_ref, j):
    assert 2 * FF_TILE == D_FF // N_CHIPS
    return (wda_ref if j % 2 == 0 else wdb_ref)[j // 2]


def _skewed_rows_call(name, main, tail, tm, lead_ins, lag_ins, const_ins, lead_outs, lag_outs, acc_outs, carry,
                      streamed, tile_copies, n_copies):
    n_rows = lead_ins[0].shape[0]
    assert n_rows % tm == 0
    n = n_rows // tm
    counts = [len(lead_ins), len(lag_ins), len(const_ins), len(streamed), len(lead_outs), len(lag_outs), len(acc_outs),
              1, len(streamed)]

    def kern(*refs):
        groups, pos = [], 0
        for cnt in counts:
            groups.append(refs[pos:pos + cnt])
            pos += cnt
        lead_i, lag_i, consts, w_hbm, lead_o, lag_o, accs, (carry_scr,), w_vmem = groups
        sems = refs[pos]
        i = pl.program_id(0)
        pieces, k = [], 0
        for piece in tile_copies(w_hbm, w_vmem):
            pieces.append([pltpu.make_async_copy(src, dst, sems.at[k + q]) for q, (src, dst) in enumerate(piece)])
            k += len(piece)

        def ready(j):
            for cp in pieces[j]:
                cp.wait()

        @pl.when(i == 0)
        def _():
            for piece in pieces:
                for cp in piece:
                    cp.start()
            for a in accs:
                a[...] = jnp.zeros_like(a)
            carry_scr[...] = main(lead_i, consts, lead_o, w_vmem, ready)

        @pl.when(jnp.logical_and(i > 0, i < n))
        def _():
            previous = carry_scr[...]
            carry_scr[...] = main(lead_i, consts, lead_o, w_vmem, lambda j: None)
            tail(previous, lag_i, consts, lag_o, accs)

        @pl.when(i == n)
        def _():
            tail(carry_scr[...], lag_i, consts, lag_o, accs)

    def lead(width):
        return pl.BlockSpec((tm, width), lambda i: (jnp.minimum(i, n - 1), 0))

    def lag(width):
        return pl.BlockSpec((tm, width), lambda i: (jnp.maximum(i - 1, 0), 0))

    def whole(shape):
        nd = len(shape)
        return pl.BlockSpec(tuple(shape), lambda i: (0,) * nd)

    return pl.pallas_call(
        kern, name=name, grid=(n + 1,),
        in_specs=([lead(a.shape[1]) for a in lead_ins] + [lag(a.shape[1]) for a in lag_ins]
                  + [whole(a.shape) for a in const_ins] + [_HBM] * len(streamed)),
        out_specs=[lead(s.shape[1]) for s in lead_outs] + [lag(s.shape[1]) for s in lag_outs] + [whole(s.shape) for s in acc_outs],
        out_shape=tuple(lead_outs) + tuple(lag_outs) + tuple(acc_outs),
        scratch_shapes=([pltpu.VMEM(carry, F32)] + [pltpu.VMEM(a.shape, a.dtype) for a in streamed]
                        + [pltpu.SemaphoreType.DMA((n_copies,))]),
        compiler_params=_cparams(1),
    )(*lead_ins, *lag_ins, *const_ins, *streamed)


def _mlp_weight_pieces(order):
    per = (D_FF // N_CHIPS) // FF_TILE

    def tile_copies(hbm, vmem):
        pieces = []
        for j in range(D_FF // FF_TILE):
            cols = (j // per, slice(None), pl.ds((j % per) * FF_TILE, FF_TILE))
            up = (hbm[0].at[cols], vmem[0].at[cols])
            down = (hbm[1 + j % 2].at[j // 2], vmem[1 + j % 2].at[j // 2])
            pieces.append([up, down] if order == "up_down" else [down, up])
        return pieces

    return tile_copies


def _mlp_fwd(h2, x1, tgt, w_up, w_down_a, w_down_b, nw, tm=512):
    n_tok = x1.shape[0]

    def main(lead_i, consts, lead_o, weights, ready):
        (h2_ref,), (f_ref,), (wup_ref, wda_ref, wdb_ref) = lead_i, lead_o, weights
        h2v = h2_ref[...]
        acc = jnp.zeros((tm, D_MODEL), F32)
        for j in range(D_FF // FF_TILE):
            cs = slice(j * FF_TILE, (j + 1) * FF_TILE)
            ready(j)
            u = jnp.dot(h2v, _up_cols(wup_ref, j), preferred_element_type=F32)
            f = jnp.square(jnp.maximum(u, 0.0)).astype(BF16)
            f_ref[:, cs] = f
            acc = acc + jnp.dot(f, _down_rows(wda_ref, wdb_ref, j), preferred_element_type=F32)
        return acc

    def tail(acc, lag_i, consts, lag_o, accs):
        (x1_ref, tgt_ref), (nw_ref,), (dd_ref, dy_ref), (loss_ref, dnw_ref) = lag_i, consts, lag_o, accs
        dn, r = _rms_fwd(acc, nw_ref[...])
        e = x1_ref[...] + dn - tgt_ref[...]
        loss_ref[...] += jnp.full(loss_ref.shape, (0.5 / D_MODEL) * jnp.sum(e * e), F32)
        dy = e * (1.0 / D_MODEL)
        dd, dnw = _rms_bwd(acc, r, nw_ref[...], dy)
        dy_ref[...] = dy
        dd_ref[...] = dd.astype(BF16)
        dnw_ref[...] += dnw

    return _skewed_rows_call(
        "mlp_fwd", main, tail, tm, [h2], [x1, tgt], [nw],
        [_sds((n_tok, D_FF), BF16)], [_sds((n_tok, D_MODEL), BF16), _sds((n_tok, D_MODEL), F32)],
        [_sds((8, 128), F32), _sds((1, D_MODEL), F32)], carry=(tm, D_MODEL),
        streamed=[w_up, w_down_a, w_down_b], tile_copies=_mlp_weight_pieces("up_down"), n_copies=2 * (D_FF // FF_TILE))


def _mlp_bwd(dd, f, x1, dy, w_down_a, w_down_b, w_up, nw, tm=256):
    n_tok = x1.shape[0]

    def main(lead_i, consts, lead_o, weights, ready):
        (dd_ref, f_ref), (dup_ref,), (wup_ref, wda_ref, wdb_ref) = lead_i, lead_o, weights
        ddv = dd_ref[...]
        acc = jnp.zeros((tm, D_MODEL), F32)
        for j in range(D_FF // FF_TILE):
            cs = slice(j * FF_TILE, (j + 1) * FF_TILE)
            ready(j)
            df = _dot_nt(ddv, _down_rows(wda_ref, wdb_ref, j))
            du = (df * (2.0 * jnp.sqrt(f_ref[:, cs].astype(F32)))).astype(BF16)
            dup_ref[:, cs] = du
            acc = acc + _dot_nt(du, _up_cols(wup_ref, j))
        return acc

    def tail(acc, lag_i, consts, lag_o, accs):
        (x1_ref, dy_ref), (nw_ref,), (dx1_ref,), (dnw_ref,) = lag_i, consts, lag_o, accs
        x1v = x1_ref[...]
        _, r = _rms_fwd(x1v, nw_ref[...])
        dx, dnw = _rms_bwd(x1v, r, nw_ref[...], acc)
        dx1_ref[...] = dy_ref[...] + dx
        dnw_ref[...] += dnw

    return _skewed_rows_call(
        "mlp_bwd", main, tail, tm, [dd, f], [x1, dy], [nw],
        [_sds((n_tok, D_FF), BF16)], [_sds((n_tok, D_MODEL), F32)], [_sds((1, D_MODEL), F32)], carry=(tm, D_MODEL),
        streamed=[w_up, w_down_a, w_down_b], tile_copies=_mlp_weight_pieces("down_up"), n_copies=2 * (D_FF // FF_TILE))


def _outproj_bwd(dx1, o, w_out, nw, tm=256, carried=None):
    n_tok = dx1.shape[0]

    def body(dx1_ref, o_ref, wo_ref, nw_ref, do_ref, dya_ref, dyb_ref, dnw_ref):
        ov = o_ref[...]
        _, r = _rms_fwd(ov, nw_ref[...])
        do, dnw = _rms_bwd(ov, r, nw_ref[...], dx1_ref[...])
        dob = do.astype(BF16)
        do_ref[...] = dob
        dya_ref[...] = _dot_nt(dob, wo_ref[:GM_WIDTH, :])
        dyb_ref[...] = _dot_nt(dob, wo_ref[GM_WIDTH:, :])
        dnw_ref[...] += dnw

    return _rows_call("outproj_bwd", body, tm, [dx1, o], [w_out, nw],
                      [_sds((n_tok, D_MODEL), BF16), _sds((n_tok, GM_WIDTH), F32), _sds((n_tok, SSM_WIDTH), F32)],
                      [_sds((1, D_MODEL), F32)], carried=carried)


def _gmlp_bwd(p_uv, dya, lnw, lnb, e_bf, et_bf, w_cat, w_stack, bmap, carried=None):
    n_tok = p_uv.shape[0]
    chunks_per_step = 2

    def body(puv_ref, dya_ref, lnw_ref, lnb_ref, e_ref, et_ref, wcat_ref, wstack_ref, bmap_ref,
             dpuv_ref, dws_ref, dbs_ref, dlnw_ref, dlnb_ref, wm_scr, wsm_scr):
        t_stk = lax.broadcasted_iota(jnp.int32, (N_HEADS * CHUNK, CHUNK), 0) % CHUNK
        s_stk = lax.broadcasted_iota(jnp.int32, (N_HEADS * CHUNK, CHUNK), 1)

        @pl.when(pl.program_id(0) == 0)
        def _():
            wm_scr[...] = _causal_w_cat(wcat_ref[...])
            wsm_scr[...] = jnp.where(t_stk >= s_stk, wstack_ref[...], 0.0).astype(BF16)

        lnw_v = lnw_ref[...]
        e_v, et_v = e_ref[...], et_ref[...]

        def one_chunk(rows):
            u, v, gu, tu, tv, rstd, xhat, vn = _gmlp_common(puv_ref[rows, :], lnw_v, lnb_ref[...], e_v, et_v)
            vnb = vn.astype(BF16)
            mixed = jnp.dot(wm_scr[...], _head_blocks(vnb), preferred_element_type=F32) + bmap_ref[...]
            dy = dya_ref[rows, :]
            du = dy * mixed * _gelu_grad(u, tu)
            dmixed = dy * gu
            (dbs,) = _seg_dots([dmixed], et_v)
            dblocks = _head_blocks(dmixed.astype(BF16))
            dvn = lax.dot_general(wsm_scr[...], dblocks, (((0,), (0,)), ((), ())), preferred_element_type=F32)
            dws = lax.dot_general(dblocks, vnb, (((1,), (1,)), ((), ())), preferred_element_type=F32)
            dxh = dvn * lnw_v
            m1, m2 = _seg_dots([dxh, dxh * xhat], et_v)
            m1, m2 = _seg_dots([m1 * (1.0 / HEAD_DIM), m2 * (1.0 / HEAD_DIM)], e_v)
            dgv = rstd * (dxh - m1 - xhat * m2)
            dv = dgv * _gelu_grad(v, tv)
            dpuv_ref[rows, :GM_WIDTH] = du.astype(BF16)
            dpuv_ref[rows, GM_WIDTH:] = dv.astype(BF16)
            return dbs, dws, jnp.sum(dvn * xhat, axis=0, keepdims=True), jnp.sum(dvn, axis=0, keepdims=True)

        parts = [one_chunk(slice(k * CHUNK, (k + 1) * CHUNK)) for k in range(chunks_per_step)]
        dbs, dws, dlnw, dlnb = [functools.reduce(lambda a, b: a + b, vals) for vals in zip(*parts)]
        dbs_ref[...] += dbs
        dws_ref[...] += jnp.where(t_stk >= s_stk, dws, 0.0)
        dlnw_ref[...] += dlnw
        dlnb_ref[...] += dlnb

    return _rows_call(
        "gmlp_bwd", body, chunks_per_step * CHUNK, [p_uv, dya], [lnw, lnb, e_bf, et_bf, w_cat, w_stack, bmap],
        [_sds((n_tok, 2 * GM_WIDTH), BF16)],
        [_sds((N_HEADS * CHUNK, CHUNK), F32), _sds((CHUNK, DT_PAD), F32), _sds((1, GM_WIDTH), F32),
         _sds((1, GM_WIDTH), F32)],
        scratch=[pltpu.VMEM((CHUNK, N_HEADS * CHUNK), BF16), pltpu.VMEM((N_HEADS * CHUNK, CHUNK), BF16)],
        carried=carried)


def _ssd_bwd(p_xbc, p_z, p_dt, yssd, sprev, dyb, conv_w, conv_b, dt_bias, a_log, dskip_map, norm_w, e_bf, et_bf, n_seq,
             carried=None):
    n_tok = p_xbc.shape[0]
    nc = n_tok // n_seq // CHUNK

    def body(xr3, xprev3, z3, pdt3, yssd3, sprev3, dyb3,
             cw_ref, cb_ref, dtb_ref, alog_ref, dsk_ref, nw_ref, e_ref, et_ref,
             dpxbc3, dpz3, dpdt3, dcw_ref, dcb_ref, ddtb_ref, dalog_ref, ddsk_ref, dnw_ref,
             ds3_scr, nxt3_scr, dxa3_scr):
        @pl.when(pl.program_id(0) == 0)
        def _():
            for a in (dcw_ref, dcb_ref, ddtb_ref, dalog_ref, ddsk_ref, dnw_ref, ds3_scr, nxt3_scr):
                a[...] = jnp.zeros_like(a)

        for b in range(n_seq):
            one_sequence(xr3.at[b], xprev3.at[b], z3.at[b], pdt3.at[b], yssd3.at[b], sprev3.at[b], dyb3.at[b],
                         cw_ref, cb_ref, dtb_ref, alog_ref, dsk_ref, nw_ref, e_ref, et_ref,
                         dpxbc3.at[b], dpz3.at[b], dpdt3.at[b], dcw_ref, dcb_ref, ddtb_ref, dalog_ref, ddsk_ref, dnw_ref,
                         ds3_scr.at[b], nxt3_scr.at[b], dxa3_scr.at[b])

    def one_sequence(xr_ref, xprev_ref, z_ref, pdt_ref, yssd_ref, sprev_ref, dyb_ref,
                     cw_ref, cb_ref, dtb_ref, alog_ref, dsk_ref, nw_ref, e_ref, et_ref,
                     dpxbc_ref, dpz_ref, dpdt_ref, dcw_ref, dcb_ref, ddtb_ref, dalog_ref, ddsk_ref, dnw_ref,
                     ds_scr, nxt_scr, dxa_scr):
        chunk = nc - 1 - pl.program_id(0)
        xr = xr_ref[...]
        prev = jnp.where(chunk == 0, 0.0, xprev_ref[...])
        et_v = et_ref[...]
        p = _ssd_pre(xr, prev, cw_ref, cb_ref[...], pdt_ref[...], dtb_ref[...], alog_ref[...], e_ref[...])
        last, e_exp, dte, cd = _ssd_maps(p)
        rowi = p["rowi"]
        xs = p["xa"][:, :SSM_WIDTH]
        xd = xs * p["dt_map"]
        a_cs_t = p["a_cs"].T
        tri = _tril_mask()
        dsk = dsk_ref[...]
        nw_v = nw_ref[...]

        yv = yssd_ref[...]
        zv = z_ref[...]
        sz, zg, yg, _, rs = _gate_fwd(yv, zv, nw_v)
        dout = dyb_ref[...]
        for g in range(SSM_GROUPS):
            gs = slice(g * GROUP_W, (g + 1) * GROUP_W)
            dyg_g, dnw_g = _rms_bwd(yg[:, gs], rs[g], nw_v[:, gs], dout[:, gs])
            dnw_ref[:, gs] += dnw_g
            dxa_scr[:, gs] = dyg_g
        dyg = dxa_scr[:, :SSM_WIDTH]
        d_y = dyg * zg
        dpz_ref[...] = (dyg * yv * (sz + zv * sz * (1.0 - sz))).astype(BF16)

        s_prev = sprev_ref[...]
        ds_next = ds_scr[...]
        lane_dt = lax.broadcasted_iota(jnp.int32, (1, DT_PAD), 1)
        da_cols = jnp.zeros((CHUNK, DT_PAD), F32)
        for g in range(SSM_GROUPS):
            gs = slice(g * GROUP_W, (g + 1) * GROUP_W)
            b_off = SSM_WIDTH + g * SSM_STATE
            c_off = SSM_WIDTH + (SSM_GROUPS + g) * SSM_STATE
            bm = p["xa"][:, b_off:b_off + SSM_STATE].astype(BF16)
            cm = p["xa"][:, c_off:c_off + SSM_STATE].astype(BF16)
            cb_mat = _dot_nt(cm, bm)
            d_yg = d_y[:, gs]
            d_ygb = d_yg.astype(BF16)
            xdg = xd[:, gs]
            xdgb = xdg.astype(BF16)
            ds_g = ds_next[:, gs]
            sp_g = s_prev[:, gs]
            bds = _dot(bm, ds_g)
            dcs = d_yg * e_exp[:, gs]
            d_c = _dot_nt(dcs, sp_g)
            ds_scr[:, gs] = cd[:, gs] * ds_g + _dot_tn(cm, dcs)
            d_b = _dot_nt(xdg * dte[:, gs], ds_g)
            dxd_g = bds * dte[:, gs]
            sum_dcb = jnp.zeros((CHUNK, CHUNK), F32)
            for r in range(SSM_GROUPS * 2):
                head = g * 4 + r
                mask = _head_lane_mask(GROUP_W, r)
                dm = _head_decay(p["a_cs"], a_cs_t, head, tri)
                m_mat = cb_mat * dm
                g_mat = _dot_nt(jnp.where(mask, d_yg, 0.0), xdgb)
                w_mat = g_mat * m_mat
                sum_dcb = sum_dcb + g_mat * dm
                dxd_g = dxd_g + jnp.where(mask, _dot_tn(m_mat, d_ygb), 0.0)
                da_h = jnp.sum(w_mat - w_mat.T, axis=1, keepdims=True)
                da_cols = da_cols + jnp.where(lane_dt == head, da_h, 0.0)
            d_c = d_c + _dot(sum_dcb, bm)
            d_b = d_b + _dot_tn(sum_dcb, cm)
            dxa_scr[:, b_off:b_off + SSM_STATE] = d_b
            dxa_scr[:, c_off:c_off + SSM_STATE] = d_c
            y_off_g = _dot(cm, sp_g) * e_exp[:, gs]
            t3 = bds * xdg * dte[:, gs]
            tail = jnp.sum(t3, axis=0, keepdims=True) + jnp.sum(ds_g * sp_g, axis=0, keepdims=True) * cd[:, gs]
            pre_g = d_yg * y_off_g - t3 + jnp.where(last, tail, 0.0)
            s_pre, ddt_g, s_dsk = _seg_dots([pre_g, dxd_g * xs[:, gs], d_yg * xs[:, gs]], et_v[gs, :])
            da_cols = da_cols + s_pre
            ddsk_ref[...] += jnp.sum(s_dsk, axis=0, keepdims=True)
            dxa_scr[:, gs] = dxd_g * p["dt_map"][:, gs] + dsk[:, gs] * d_yg
            if g == 0:
                ddt = ddt_g
            else:
                ddt = ddt + ddt_g
        r_i = lax.broadcasted_iota(jnp.int32, (CHUNK, CHUNK), 0)
        c_i = lax.broadcasted_iota(jnp.int32, (CHUNK, CHUNK), 1)
        ddta = _tri_dot(r_i <= c_i, da_cols, terms=2)
        ddt = ddt + ddta * p["a_neg"]
        dalog_ref[...] += jnp.sum(ddta * p["dt"], axis=0, keepdims=True) * p["a_neg"]
        draw = ddt * _sigmoid(p["pre"])
        ddtb_ref[...] += jnp.sum(draw, axis=0, keepdims=True)
        dpdt_ref[...] = draw.astype(BF16)

        xc = p["xc"]
        sg = p["sg"]
        dxc = dxa_scr[...] * (sg + xc * sg * (1.0 - sg))
        dcb_ref[...] += jnp.sum(dxc, axis=0, keepdims=True)
        for k in range(CONV_K):
            dcw_ref[k] += jnp.sum(dxc * p["shifted"][k], axis=0, keepdims=True)
        nxt = nxt_scr[...]
        dxr = cw_ref[3] * dxc
        for s in range(1, CONV_K):
            dxr = dxr + cw_ref[CONV_K - 1 - s] * _shift_up(dxc, nxt, s)
        dpxbc_ref[...] = dxr.astype(BF16)
        nxt_scr[...] = dxc[:SUBLANES, :]

    seq_len = n_tok // n_seq

    def rows(width):
        return pl.BlockSpec((n_seq, CHUNK, width), lambda s: (0, nc - 1 - s, 0))

    tiles = CHUNK // SUBLANES
    prev_rows = pl.BlockSpec((n_seq, SUBLANES, CONV_CH), lambda s: (0, jnp.maximum((nc - 1 - s) * tiles - 1, 0), 0))

    def whole(shape):
        nd = len(shape)
        return pl.BlockSpec(tuple(shape), lambda s: (0,) * nd)

    def by_seq(a):
        return a.reshape(n_seq, seq_len, a.shape[-1])

    acc_shapes = [(CONV_K, 1, CONV_CH), (1, CONV_CH), (1, DT_PAD), (1, DT_PAD), (1, DT_PAD), (1, SSM_WIDTH)]
    xbc3 = by_seq(p_xbc)
    outs = _call_carrying(
        body, carried, name="ssd_bwd", grid=(nc,),
        in_specs=[rows(CONV_CH), prev_rows, rows(SSM_WIDTH), rows(DT_PAD), rows(SSM_WIDTH), rows(SSM_WIDTH),
                  rows(SSM_WIDTH)] + _ssd_const_specs(),
        out_specs=[rows(CONV_CH), rows(SSM_WIDTH), rows(DT_PAD)] + [whole(s) for s in acc_shapes],
        out_shape=tuple([_sds((n_seq, seq_len, CONV_CH), BF16), _sds((n_seq, seq_len, SSM_WIDTH), BF16),
                         _sds((n_seq, seq_len, DT_PAD), BF16)] + [_sds(s, F32) for s in acc_shapes]),
        scratch_shapes=[pltpu.VMEM((n_seq, SSM_STATE, SSM_WIDTH), F32), pltpu.VMEM((n_seq, SUBLANES, CONV_CH), F32),
                        pltpu.VMEM((n_seq, CHUNK, CONV_CH), F32)],
        operands=[xbc3, xbc3, by_seq(p_z), by_seq(p_dt), by_seq(yssd), by_seq(sprev), by_seq(dyb), conv_w, conv_b, dt_bias,
                  a_log, dskip_map, norm_w, e_bf, et_bf])
    return tuple(o.reshape(n_tok, o.shape[-1]) for o in outs[:3]) + tuple(outs[3:])


def _inproj_bwd(dp_uv, dp_xbc, dp_z, dp_dt, x, dx1, w_uv, w_xbc, w_z, w_dt, nw, tm=256, carried=None):
    n_tok = x.shape[0]

    def body(duv_ref, dxbc_ref, dz_ref, ddt_ref, x_ref, dx1_ref, wuv_ref, wxbc_ref, wz_ref, wdt_ref, nw_ref,
             gx_ref, h_ref, dnw_ref):
        dh = _dot_nt(duv_ref[...], wuv_ref[...]) + _dot_nt(dxbc_ref[...], wxbc_ref[...])
        dh = dh + _dot_nt(dz_ref[...], wz_ref[...]) + _dot_nt(ddt_ref[...], wdt_ref[...])
        xv = x_ref[...]
        h, r = _rms_fwd(xv, nw_ref[...])
        dx, dnw = _rms_bwd(xv, r, nw_ref[...], dh)
        gx_ref[...] = dx1_ref[...] + dx
        h_ref[...] = h.astype(BF16)
        dnw_ref[...] += dnw

    return _rows_call("inproj_bwd", body, tm, [dp_uv, dp_xbc, dp_z, dp_dt, x, dx1], [w_uv, w_xbc, w_z, w_dt, nw],
                      [_sds((n_tok, D_MODEL), F32), _sds((n_tok, D_MODEL), BF16)], [_sds((1, D_MODEL), F32)],
                      carried=carried)


def _const_maps():
    lane = jnp.arange(SSM_WIDTH) // HEAD_DIM
    e_bf = (jnp.arange(DT_PAD)[:, None] == lane[None, :]).astype(BF16)
    return e_bf, e_bf.T


def _pad_lanes(v, width):
    return jnp.pad(v, ((0, 0), (0, width - v.shape[1])))


SHARD_COLS = IN_COLS // N_CHIPS
_UV_END = 2 * GM_WIDTH
_Z_END = _UV_END + SSM_WIDTH
_XBC_END = _Z_END + CONV_CH


def _cols_from_shards(w4, lo, hi):
    pieces = []
    for j in range(N_CHIPS):
        a, b = max(lo, j * SHARD_COLS), min(hi, (j + 1) * SHARD_COLS)
        if a < b:
            pieces.append(w4[j][:, a - j * SHARD_COLS:b - j * SHARD_COLS])
    return pieces[0] if len(pieces) == 1 else jnp.concatenate(pieces, axis=1)


def _shards_from_cols(blocks):
    shards = []
    for j in range(N_CHIPS):
        pieces = []
        for arr, lo, hi in blocks:
            a, b = max(lo, j * SHARD_COLS), min(hi, (j + 1) * SHARD_COLS)
            if a < b:
                pieces.append(arr[:, a - lo:b - lo])
        shards.append(pieces[0] if len(pieces) == 1 else jnp.concatenate(pieces, axis=1))
    return jnp.stack(shards)


def _forward_backward(x, tgt, w_in4, conv_w, small, out_shard, up_shard, down_shard, core, adam_args):
    n_seq, seq_len, _ = x.shape
    n_tok = n_seq * seq_len
    x2 = x.reshape(n_tok, D_MODEL)
    tgt2 = tgt.reshape(n_tok, D_MODEL)
    e_bf, et_bf = _const_maps()

    w_uv = _cols_from_shards(w_in4, 0, _UV_END)
    w_z = _cols_from_shards(w_in4, _UV_END, _Z_END)
    w_xbc = _cols_from_shards(w_in4, _Z_END, _XBC_END)
    w_dt = _pad_lanes(_cols_from_shards(w_in4, _XBC_END, IN_COLS), DT_PAD)

    nw_pre = small["norm_mix_pre"]
    lnw = small["gm_ln_w"].reshape(1, GM_WIDTH)
    lnb = small["gm_ln_b"].reshape(1, GM_WIDTH)
    w_stack = small["gm_w_s"].reshape(N_HEADS * CHUNK, CHUNK)
    w_cat = jnp.transpose(small["gm_w_s"], (1, 0, 2)).reshape(CHUNK, N_HEADS * CHUNK)
    bmap = jnp.repeat(small["gm_b_s"].T, HEAD_DIM, axis=1)
    cw3 = conv_w.reshape(CONV_K, 1, CONV_CH)
    conv_b = small["conv_b"]
    dt_bias = _pad_lanes(small["dt_bias"], DT_PAD)
    a_log = _pad_lanes(small["a_log"], DT_PAD)
    dskip_map = jnp.repeat(small["d_skip"], HEAD_DIM, axis=1)
    ssm_nw = small["ssm_norm_w"]

    half = down_shard.shape[0] // 2
    p_uv, p_xbc, p_z, p_dt, w_out4, w_down_a = _inproj_fwd(
        x2, nw_pre, w_uv, w_xbc, w_z, w_dt, carried=_allgather_exchange([out_shard, down_shard[:half]]))
    ssd_consts = (cw3, conv_b, dt_bias, a_log, dskip_map, ssm_nw, e_bf, et_bf)
    w_out_b = w_out4.reshape(D_MODEL, D_MODEL)
    mix, yssd, sprev, o, x1, h2, w_up4, w_down_b = _mixer_fwd(
        p_uv, p_xbc, p_z, p_dt, x2, lnw, lnb, w_cat, bmap, w_out_b, small["norm_mix_post"], small["norm_ffn_pre"],
        *ssd_consts, n_seq, carried=_allgather_exchange([up_shard, down_shard[half:]]))
    f, dd, dy, loss_acc, d_nffn_post = _mlp_fwd(h2, x1, tgt2, w_up4, w_down_a, w_down_b, small["norm_ffn_post"])

    dup, dx1, d_nffn_pre = _mlp_bwd(dd, f, x1, dy, w_down_a, w_down_b, w_up4, small["norm_ffn_pre"])
    tk = min(DW_TOKENS_PER_STEP, n_tok)
    g_up = _matmul_tn("dw_up", h2, dup, D_MODEL, D_MODEL, tk, stacked=True)
    g_down = _matmul_tn("dw_down", f, dd, 1024, D_MODEL, tk).reshape(N_CHIPS, D_FF // N_CHIPS, D_MODEL)
    do, dya, dyb, d_nmix_post, got_up, got_down = _outproj_bwd(
        dx1, o, w_out_b, small["norm_mix_post"], carried=_pair_exchange([g_up, g_down]))
    h_up = _pair_sum(core, g_up, got_up, 256)
    h_down = _pair_sum(core, g_down, got_down, 256)
    g_out = _matmul_tn("dw_out", mix, do, D_MODEL, D_MODEL, tk).reshape(N_CHIPS, D_MODEL // N_CHIPS, D_MODEL)
    dp_uv, d_ws, d_bs_t, d_lnw, d_lnb, slab_up, got_out = _gmlp_bwd(
        p_uv, dya, lnw, lnb, e_bf, et_bf, w_cat, w_stack, bmap,
        carried=_both(_chip_exchange([h_up]), _pair_exchange([g_out])))
    h_out = _pair_sum(core, g_out, got_out, 128)
    early = {
        "gm_ln_w": d_lnw.reshape(N_HEADS, HEAD_DIM), "gm_ln_b": d_lnb.reshape(N_HEADS, HEAD_DIM),
        "gm_w_s": d_ws.reshape(N_HEADS, CHUNK, CHUNK), "gm_b_s": d_bs_t[:, :N_HEADS].T,
        "norm_mix_post": d_nmix_post, "norm_ffn_pre": d_nffn_pre, "norm_ffn_post": d_nffn_post,
    }
    packed_early = _pack(early, tuple(early), tail=loss_acc[0, 0].reshape(1))
    (dp_xbc, dp_z, dp_dt, d_cw, d_cb, d_dtb, d_alog, d_dsk, d_ssm_nw, slab_down, slab_out, all_early) = _ssd_bwd(
        p_xbc, p_z, p_dt, yssd, sprev, dyb, *ssd_consts, n_seq,
        carried=_both(_chip_exchange([h_down, h_out]), _device_gather_exchange(packed_early)))
    gx, h, d_nmix_pre = _inproj_bwd(dp_uv, dp_xbc, dp_z, dp_dt, x2, dx1, w_uv, w_xbc, w_z, w_dt, nw_pre)
    late = {
        "norm_mix_pre": d_nmix_pre, "conv_w": d_cw.reshape(CONV_K, CONV_CH), "conv_b": d_cb,
        "dt_bias": d_dtb[:, :N_HEADS], "a_log": d_alog[:, :N_HEADS], "d_skip": d_dsk[:, :N_HEADS],
        "ssm_norm_w": d_ssm_nw,
    }
    g_uv, all_late = _matmul_tn("dw_in_uv", h, dp_uv, D_MODEL, 2 * GM_WIDTH, tk,
                                carried=_device_gather_exchange(_pack(late, tuple(late))))
    sum_early = _ordered_sum("small_sum_early", all_early)
    small_sum = _unpack(sum_early, {n: v.shape for n, v in early.items()}, tuple(early))
    small_sum.update(_unpack(_ordered_sum("small_sum_late", all_late), {n: v.shape for n, v in late.items()}, tuple(late)))
    loss = sum_early.reshape(-1)[sum(v.size for v in early.values())]
    red_up, red_down, red_out = _chip_sum(slab_up, 256), _chip_sum(slab_down, 256), _chip_sum(slab_out, 128)
    g_xbc, oth_up, oth_down, oth_out = _matmul_tn("dw_in_xbc", h, dp_xbc, D_MODEL, CONV_CH, tk,
                                                  carried=_pair_swap([red_up, red_down, red_out]))
    g_z = _matmul_tn("dw_in_z", h, dp_z, D_MODEL, SSM_WIDTH, tk)
    g_dt = _matmul_tn("dw_in_dt", h, dp_dt, D_MODEL, DT_PAD, tk)

    g_in = _shards_from_cols([(g_uv, 0, _UV_END), (g_z, _UV_END, _Z_END), (g_xbc, _Z_END, _XBC_END),
                              (g_dt, _XBC_END, IN_COLS)])
    (got_in,) = _run_exchange("grad_pair_exchange_in", _pair_exchange([g_in]))
    h_in = _pair_sum(core, g_in, got_in, 256)
    (slab_in,) = _run_exchange("grad_chip_exchange", _chip_exchange([h_in]))
    res = _adamw_halves("adamw_mlp", [(adam_args["w_up"][0], red_up, oth_up) + adam_args["w_up"][1:],
                                      (adam_args["w_down"][0], red_down, oth_down) + adam_args["w_down"][1:]], 256)
    big_out = {"w_up": res[0:4], "w_down": res[4:8]}
    big_out["w_out"] = _adamw_halves("adamw_w_out", [(adam_args["w_out"][0], red_out, oth_out) + adam_args["w_out"][1:]], 128)
    red_in = _chip_sum(slab_in, 256)
    (oth_in,) = _run_exchange("grad_pair_swap_in", _pair_swap([red_in]))
    big_out["w_in"] = _adamw_halves("adamw_w_in", [(adam_args["w_in"][0], red_in, oth_in) + adam_args["w_in"][1:]], 256)

    return loss, gx.reshape(x.shape), big_out, small_sum


_HBM = pl.BlockSpec(memory_space=pltpu.HBM)


D2D_CHUNKS = 8
ICI_CHUNKS = 1
ROW_ALIGN = 16


def _row_chunks(rows, n_chunks):
    size = min(max(rows // n_chunks, ROW_ALIGN), rows)
    assert rows % size == 0
    return [(start, size) for start in range(0, rows, size)]


def _position():
    x, y, c = lax.axis_index("x"), lax.axis_index("y"), lax.axis_index("c")
    chips = [(1 - x, y), (x, 1 - y), (1 - x, 1 - y)]
    return x, y, c, chips


def _allgather_exchange(arrs):
    n = len(arrs)

    def copies(ins, outs, send_sems, recv_sems, local_sems):
        x, y, c, chips = _position()
        me = 2 * x + y
        sibling = (x, y, 1 - c)

        def copy(a, k, src, dst, to):
            return pltpu.make_async_remote_copy(src_ref=src, dst_ref=dst, send_sem=send_sems.at[a, k],
                                                recv_sem=recv_sems.at[a, k], device_id=to, device_id_type=MESH)

        def half_rows(a, pc):
            half = ins[a].shape[0] // 2
            return pl.ds(pc * half, half)

        local = [pltpu.make_async_copy(ins[a], outs[a].at[me], local_sems.at[a]) for a in range(n)]
        ici_out = [[copy(a, k, ins[a].at[half_rows(a, c)], outs[a].at[me, half_rows(a, c)], (px, py, c))
                    for k, (px, py) in enumerate(chips)] for a in range(n)]
        return c, chips, sibling, copy, half_rows, local, ici_out

    def start(ins, outs, send_sems, recv_sems, local_sems):
        c, chips, _, copy, _, local, _ = copies(ins, outs, send_sems, recv_sems, local_sems)
        x, y, _, _ = _position()
        me = 2 * x + y
        for cp in local:
            cp.start()
        for a in range(n):
            half = ins[a].shape[0] // 2
            for k, (px, py) in enumerate(chips):
                for first, size in _row_chunks(half, ICI_CHUNKS):
                    rows = pl.ds(c * half + first, size)
                    copy(a, k, ins[a].at[rows], outs[a].at[me, rows], (px, py, c)).start()

    def finish(ins, outs, send_sems, recv_sems, local_sems):
        c, chips, sibling, copy, half_rows, local, ici_out = copies(ins, outs, send_sems, recv_sems, local_sems)
        for a in range(n):
            half = ins[a].shape[0] // 2
            for k, (px, py) in enumerate(chips):
                blk = outs[a].at[2 * px + py, half_rows(a, c)]
                copy(a, k, blk, blk, (px, py, c)).wait_recv()
                for first, size in _row_chunks(half, D2D_CHUNKS):
                    piece = outs[a].at[2 * px + py, pl.ds(c * half + first, size)]
                    copy(a, 3 + k, piece, piece, sibling).start()
        for a in range(n):
            for k, (px, py) in enumerate(chips):
                theirs = outs[a].at[2 * px + py, half_rows(a, 1 - c)]
                copy(a, 3 + k, theirs, theirs, sibling).wait_recv()
                mine = outs[a].at[2 * px + py, half_rows(a, c)]
                copy(a, 3 + k, mine, mine, sibling).wait_send()
        for a in range(n):
            for cp in ici_out[a]:
                cp.wait_send()
        for cp in local:
            cp.wait()

    return _Carried(arrs, [_sds((N_CHIPS,) + a.shape, a.dtype) for a in arrs],
                    [pltpu.SemaphoreType.DMA((n, 6)), pltpu.SemaphoreType.DMA((n, 6)), pltpu.SemaphoreType.DMA((n,))],
                    start, finish)


def _run_exchange(name, exchange):
    n_in, n_out = len(exchange.ins), len(exchange.out_shapes)

    def body(*refs):
        ins, outs, sems = refs[:n_in], refs[n_in:n_in + n_out], refs[n_in + n_out:]
        exchange.start(ins, outs, *sems)
        exchange.finish(ins, outs, *sems)

    return pl.pallas_call(
        body, name=name, out_shape=tuple(exchange.out_shapes), in_specs=[_HBM] * n_in,
        out_specs=tuple([_HBM] * n_out), scratch_shapes=exchange.sems,
    )(*exchange.ins)


def _pair_exchange(grads):
    n = len(grads)

    def copier(send_sems, recv_sems):
        x, y, c, _ = _position()

        def copy(a, src, dst):
            return pltpu.make_async_remote_copy(src_ref=src, dst_ref=dst, send_sem=send_sems.at[a],
                                                recv_sem=recv_sems.at[a], device_id=(x, y, 1 - c), device_id_type=MESH)
        return c, copy

    def start(ins, got, send_sems, recv_sems):
        c, copy = copier(send_sems, recv_sems)
        for a in range(n):
            half = ins[a].shape[1] // 2
            for slab in range(N_CHIPS):
                for first, size in _row_chunks(half, D2D_CHUNKS):
                    copy(a, ins[a].at[slab, pl.ds((1 - c) * half + first, size), :],
                         got[a].at[slab, pl.ds(first, size), :]).start()

    def finish(ins, got, send_sems, recv_sems):
        c, copy = copier(send_sems, recv_sems)
        for a in range(n):
            half = ins[a].shape[1] // 2
            copy(a, ins[a].at[:, pl.ds((1 - c) * half, half), :], got[a]).wait()

    return _Carried(grads, [_sds((N_CHIPS, g.shape[1] // 2, g.shape[2]), g.dtype) for g in grads],
                    [pltpu.SemaphoreType.DMA((n,)), pltpu.SemaphoreType.DMA((n,))], start, finish)


def _chip_exchange(hsums):
    n = len(hsums)

    def copies(ins, outs, send_sems, recv_sems, local_sems, pieces):
        x, y, c, chips = _position()
        me = 2 * x + y
        cps = []
        for a in range(n):
            cps.append(pltpu.make_async_copy(ins[a].at[me], outs[a].at[me], local_sems.at[a]))
            rows = ins[a].shape[1]
            for k, (px, py) in enumerate(chips):
                for first, size in (_row_chunks(rows, ICI_CHUNKS) if pieces else [(0, rows)]):
                    cps.append(pltpu.make_async_remote_copy(
                        src_ref=ins[a].at[2 * px + py, pl.ds(first, size)], dst_ref=outs[a].at[me, pl.ds(first, size)],
                        send_sem=send_sems.at[a, k], recv_sem=recv_sems.at[a, k], device_id=(px, py, c),
                        device_id_type=MESH))
        return cps

    def start(*refs):
        for cp in copies(*refs, pieces=True):
            cp.start()

    def finish(*refs):
        for cp in copies(*refs, pieces=False):
            cp.wait()

    return _Carried(hsums, [_sds(h.shape, h.dtype) for h in hsums],
                    [pltpu.SemaphoreType.DMA((n, 3)), pltpu.SemaphoreType.DMA((n, 3)), pltpu.SemaphoreType.DMA((n,))],
                    start, finish)


def _pair_swap(reds):
    n = len(reds)

    def copier(send_sems, recv_sems):
        x, y, c, _ = _position()

        def copy(a, src, dst):
            return pltpu.make_async_remote_copy(src_ref=src, dst_ref=dst, send_sem=send_sems.at[a],
                                                recv_sem=recv_sems.at[a], device_id=(x, y, 1 - c), device_id_type=MESH)
        return copy

    def start(ins, outs, send_sems, recv_sems):
        copy = copier(send_sems, recv_sems)
        for a in range(n):
            for first, size in _row_chunks(ins[a].shape[0], 2 * D2D_CHUNKS):
                copy(a, ins[a].at[pl.ds(first, size), :], outs[a].at[pl.ds(first, size), :]).start()

    def finish(ins, outs, send_sems, recv_sems):
        copy = copier(send_sems, recv_sems)
        for a in range(n):
            copy(a, ins[a], outs[a]).wait()

    return _Carried(reds, [_sds(r.shape, r.dtype) for r in reds],
                    [pltpu.SemaphoreType.DMA((n,)), pltpu.SemaphoreType.DMA((n,))], start, finish)


def _device_gather_exchange(packed):
    def copies(ins, outs, send_sems, recv_sems, local_sem):
        (x_ref,), (all_ref,) = ins, outs
        x, y, c, chips = _position()
        me, sibling = (x, y, c), (x, y, 1 - c)

        def slab(px, py, pc):
            return all_ref.at[4 * px + 2 * py + pc]

        def copy(k, block, to, src=None):
            return pltpu.make_async_remote_copy(
                src_ref=slab(*block) if src is None else src, dst_ref=slab(*block), send_sem=send_sems.at[k],
                recv_sem=recv_sems.at[k], device_id=to, device_id_type=MESH)

        mine = pltpu.make_async_copy(x_ref, slab(*me), local_sem)
        first = [copy(0, me, sibling, src=x_ref)]
        first += [copy(1 + j, me, (*chip, c), src=x_ref) for j, chip in enumerate(chips)]
        passed = [copy(4 + j, (*chip, c), sibling) for j, chip in enumerate(chips)]
        return c, chips, me, sibling, copy, mine, first, passed

    def start(ins, outs, send_sems, recv_sems, local_sem):
        _, _, _, _, _, mine, first, _ = copies(ins, outs, send_sems, recv_sems, local_sem)
        mine.start()
        for cp in first:
            cp.start()

    def finish(ins, outs, send_sems, recv_sems, local_sem):
        c, chips, me, sibling, copy, mine, first, passed = copies(ins, outs, send_sems, recv_sems, local_sem)
        for j, chip in enumerate(chips):
            copy(1 + j, (*chip, c), me).wait_recv()
            passed[j].start()
        copy(0, sibling, me).wait_recv()
        for j, chip in enumerate(chips):
            copy(4 + j, (*chip, 1 - c), me).wait_recv()
        for cp in first + passed:
            cp.wait_send()
        mine.wait()

    return _Carried([packed], [_sds((N_DEV,) + packed.shape, F32)],
                    [pltpu.SemaphoreType.DMA((7,)), pltpu.SemaphoreType.DMA((7,)), pltpu.SemaphoreType.DMA],
                    start, finish)


def _ordered_sum(name, slabs):
    _, m_per, n_cols = slabs.shape

    def body(s_ref, o_ref):
        acc = s_ref[0]
        for d in range(1, N_DEV):
            acc = acc + s_ref[d]
        o_ref[...] = acc

    vmem = pl.BlockSpec(memory_space=pltpu.VMEM)
    return pl.pallas_call(body, name=name, out_shape=_sds((m_per, n_cols), F32), in_specs=[vmem], out_specs=vmem)(slabs)


def _pair_sum(core, own, got, tm):
    _, half, cols = got.shape
    nb = half // tm

    def body(c_ref, a_ref, b_ref, o_ref):
        o_ref[...] = (a_ref[...].astype(F32) + b_ref[...].astype(F32)).astype(BF16)

    return pl.pallas_call(
        body, name="grad_pair_sum", out_shape=_sds(got.shape, BF16),
        grid_spec=pltpu.PrefetchScalarGridSpec(
            num_scalar_prefetch=1, grid=(N_CHIPS, nb),
            in_specs=[pl.BlockSpec((None, tm, cols), lambda s, i, c_ref: (s, c_ref[0] * nb + i, 0)),
                      pl.BlockSpec((None, tm, cols), lambda s, i, c_ref: (s, i, 0))],
            out_specs=pl.BlockSpec((None, tm, cols), lambda s, i, c_ref: (s, i, 0))),
        compiler_params=_cparams(2),
    )(core, own, got)


def _chip_sum(slabs, tm):
    _, half, cols = slabs.shape

    def body(s_ref, o_ref):
        acc = s_ref[0].astype(F32)
        for k in range(1, N_CHIPS):
            acc = acc + s_ref[k].astype(F32)
        o_ref[...] = acc

    return pl.pallas_call(
        body, name="grad_chip_sum", out_shape=_sds((half, cols), F32), grid=(half // tm,),
        in_specs=[pl.BlockSpec((N_CHIPS, tm, cols), lambda i: (0, i, 0))],
        out_specs=pl.BlockSpec((tm, cols), lambda i: (i, 0)), compiler_params=_cparams(1),
    )(slabs)


def _adam_math(w, g, m, v):
    m2 = ADAM_B1 * m + (1.0 - ADAM_B1) * g
    v2 = ADAM_B2 * v + (1.0 - ADAM_B2) * (g * g)
    m_hat = m2 / (1.0 - ADAM_B1 ** ADAM_STEP)
    v_hat = v2 / (1.0 - ADAM_B2 ** ADAM_STEP)
    delta = -ADAM_LR * (m_hat / (jnp.sqrt(v_hat) + ADAM_EPS) + ADAM_WD * w)
    return delta, m2, v2


def _adamw_halves(name, items, tm, carried=None):
    rows, cols = items[0][0].shape
    nb = rows // 2 // tm
    n = len(items)

    def body(*refs):
        mine = (pl.program_id(0) // nb) == lax.axis_index("c")
        for k in range(n):
            w_ref, own_ref, oth_ref, m_ref, v_ref = refs[5 * k:5 * k + 5]
            g_ref, d_ref, m2_ref, v2_ref = refs[5 * n + 4 * k:5 * n + 4 * k + 4]
            g = jnp.where(mine, own_ref[...], oth_ref[...])
            d, m2, v2 = _adam_math(w_ref[...], g, m_ref[...], v_ref[...])
            g_ref[...] = g
            d_ref[...] = d
            m2_ref[...] = m2
            v2_ref[...] = v2

    full = pl.BlockSpec((tm, cols), lambda i: (i, 0))
    half = pl.BlockSpec((tm, cols), lambda i: (i % nb, 0))
    return _call_carrying(
        body, carried, name=name, grid=(rows // tm,), in_specs=[full, half, half, full, full] * n,
        out_specs=[full] * (4 * n), out_shape=tuple([_sds((rows, cols), F32)] * (4 * n)), scratch_shapes=[],
        operands=[a for item in items for a in item])


def _adamw(name, w, g, m, v, tm):
    def body(w_ref, g_ref, m_ref, v_ref, gout_ref, d_ref, m2_ref, v2_ref):
        gv = g_ref[...]
        d, m2, v2 = _adam_math(w_ref[...], gv, m_ref[...], v_ref[...])
        gout_ref[...] = gv
        d_ref[...] = d
        m2_ref[...] = m2
        v2_ref[...] = v2

    return _rows_call(name, body, tm, [w, g, m, v], [], [_sds(w.shape, F32)] * 4)


_SMALL_NAMES = ("norm_mix_pre", "gm_ln_w", "gm_ln_b", "gm_w_s", "gm_b_s", "conv_w", "conv_b", "dt_bias", "a_log",
                "d_skip", "ssm_norm_w", "norm_mix_post", "norm_ffn_pre", "norm_ffn_post")
_PACK_COLS = 1024


def _pack(parts, names=_SMALL_NAMES, tail=None):
    pieces = [parts[n].reshape(-1) for n in names]
    flat = jnp.concatenate(pieces if tail is None else pieces + [tail])
    rows = -(-flat.shape[0] // (8 * _PACK_COLS)) * 8
    flat = jnp.pad(flat, (0, rows * _PACK_COLS - flat.shape[0]))
    return flat.reshape(rows, _PACK_COLS)


def _unpack(packed, shapes, names=_SMALL_NAMES):
    flat = packed.reshape(-1)
    out, off = {}, 0
    for n in names:
        size = 1
        for s in shapes[n]:
            size *= s
        out[n] = flat[off:off + size].reshape(shapes[n])
        off += size
    return out


def kernel(x, norm_mix_pre, w_in, gm_ln_w, gm_ln_b, gm_w_s, gm_b_s, conv_w, conv_b, dt_bias, a_log, d_skip, ssm_norm_w, w_out, norm_mix_post, norm_ffn_pre, w_up, w_down, norm_ffn_post, loss_target, m_norm_mix_pre, m_w_in, m_gm_ln_w, m_gm_ln_b, m_gm_w_s, m_gm_b_s, m_conv_w, m_conv_b, m_dt_bias, m_a_log, m_d_skip, m_ssm_norm_w, m_w_out, m_norm_mix_post, m_norm_ffn_pre, m_w_up, m_w_down, m_norm_ffn_post, v_norm_mix_pre, v_w_in, v_gm_ln_w, v_gm_ln_b, v_gm_w_s, v_gm_b_s, v_conv_w, v_conv_b, v_dt_bias, v_a_log, v_d_skip, v_ssm_norm_w, v_w_out, v_norm_mix_post, v_norm_ffn_pre, v_w_up, v_w_down, v_norm_ffn_post):
    params = dict(norm_mix_pre=norm_mix_pre, w_in=w_in, gm_ln_w=gm_ln_w, gm_ln_b=gm_ln_b, gm_w_s=gm_w_s, gm_b_s=gm_b_s,
                  conv_w=conv_w, conv_b=conv_b, dt_bias=dt_bias, a_log=a_log, d_skip=d_skip, ssm_norm_w=ssm_norm_w,
                  w_out=w_out, norm_mix_post=norm_mix_post, norm_ffn_pre=norm_ffn_pre, w_up=w_up, w_down=w_down,
                  norm_ffn_post=norm_ffn_post)
    mom1 = dict(norm_mix_pre=m_norm_mix_pre, w_in=m_w_in, gm_ln_w=m_gm_ln_w, gm_ln_b=m_gm_ln_b, gm_w_s=m_gm_w_s,
                gm_b_s=m_gm_b_s, conv_w=m_conv_w, conv_b=m_conv_b, dt_bias=m_dt_bias, a_log=m_a_log, d_skip=m_d_skip,
                ssm_norm_w=m_ssm_norm_w, w_out=m_w_out, norm_mix_post=m_norm_mix_post, norm_ffn_pre=m_norm_ffn_pre,
                w_up=m_w_up, w_down=m_w_down, norm_ffn_post=m_norm_ffn_post)
    mom2 = dict(norm_mix_pre=v_norm_mix_pre, w_in=v_w_in, gm_ln_w=v_gm_ln_w, gm_ln_b=v_gm_ln_b, gm_w_s=v_gm_w_s,
                gm_b_s=v_gm_b_s, conv_w=v_conv_w, conv_b=v_conv_b, dt_bias=v_dt_bias, a_log=v_a_log, d_skip=v_d_skip,
                ssm_norm_w=v_ssm_norm_w, w_out=v_w_out, norm_mix_post=v_norm_mix_post, norm_ffn_pre=v_norm_ffn_pre,
                w_up=v_w_up, w_down=v_w_down, norm_ffn_post=v_norm_ffn_post)
    names = list(params)
    big = ("w_in", "w_out", "w_up", "w_down")
    chip = 2 * lax.axis_index("x") + lax.axis_index("y")

    shards = {n: params[n][0].astype(BF16) for n in big}
    conv_shard = jnp.pad(conv_w[0], ((0, 16 - CONV_K), (0, 0)))
    g_in4, g_conv4 = _run_exchange("allgather_w_in", _allgather_exchange([shards["w_in"], conv_shard]))
    conv_full = jnp.transpose(g_conv4[:, :CONV_K, :], (1, 0, 2)).reshape(CONV_K, CONV_CH)

    small = {n: params[n][0] if params[n].ndim >= 3 else params[n] for n in _SMALL_NAMES if n != "conv_w"}
    core = lax.axis_index("c").astype(jnp.int32).reshape(1)
    adam_args = {n: (params[n][0], mom1[n][0], mom2[n][0]) for n in big}
    loss, grad_x, big_out, small_sum = _forward_backward(
        x, loss_target, g_in4, conv_full, small, shards["w_out"], shards["w_up"], shards["w_down"], core, adam_args)
    grads, delta, new_m, new_v = {}, {}, {}, {}
    for n in big:
        grads[n], delta[n], new_m[n], new_v[n] = [a[None] for a in big_out[n]]

    small_sum["conv_w"] = lax.dynamic_slice_in_dim(small_sum["conv_w"], chip * (CONV_CH // N_CHIPS), CONV_CH // N_CHIPS, axis=1)

    local_shapes = {n: params[n].shape[1:] if params[n].ndim >= 3 else params[n].shape for n in _SMALL_NAMES}
    flat = lambda tree: {n: tree[n].reshape(local_shapes[n]) for n in _SMALL_NAMES}
    packed = [_pack(flat(t)) for t in (params, small_sum, mom1, mom2)]
    _, d_p, m_p, v_p = _adamw("adamw_small", *packed, packed[0].shape[0])
    for src, dst in ((d_p, delta), (m_p, new_m), (v_p, new_v)):
        for n, val in _unpack(src, local_shapes).items():
            dst[n] = val.reshape(params[n].shape)
    for n in _SMALL_NAMES:
        grads[n] = small_sum[n].reshape(params[n].shape)

    out = [loss, grad_x]
    for tree in (grads, delta, new_m, new_v):
        out += [tree[n] for n in names]
    return tuple(out)
```

```python
import functools

import jax
import jax.numpy as jnp
from jax import lax
from jax.experimental import pallas as pl
from jax.experimental.pallas import tpu as pltpu

F32 = jnp.float32
BF16 = jnp.bfloat16
HI = lax.Precision.HIGHEST
MESH = pl.DeviceIdType.MESH

EPS = 1e-6
D_MODEL = 1024
GM_WIDTH = 512
SSM_WIDTH = 512
N_HEADS = 8
HEAD_DIM = 64
CHUNK = 128
SSM_GROUPS = 2
GROUP_W = SSM_WIDTH // SSM_GROUPS
SSM_STATE = 128
CONV_K = 4
CONV_CH = 1024
D_FF = 4096
IN_COLS = 2568
DT_PAD = 128
N_CHIPS = 4
N_DEV = 8

ADAM_LR = 0.001
ADAM_B1 = 0.9
ADAM_B2 = 0.999
ADAM_EPS = 1e-08
ADAM_WD = 0.01
ADAM_STEP = 10

VMEM_LIMIT_BYTES = 56 * 1024 * 1024
FF_TILE = 512
DW_TOKENS_PER_STEP = 2048


def _cparams(n_axes):
    return pltpu.CompilerParams(dimension_semantics=("arbitrary",) * n_axes, vmem_limit_bytes=VMEM_LIMIT_BYTES)


def _dot(a, b):
    return jnp.dot(a.astype(BF16), b.astype(BF16), preferred_element_type=F32)


def _dot_nt(a, b):
    return lax.dot_general(a.astype(BF16), b.astype(BF16), (((1,), (1,)), ((), ())), preferred_element_type=F32)


def _dot_tn(a, b):
    return lax.dot_general(a.astype(BF16), b.astype(BF16), (((0,), (0,)), ((), ())), preferred_element_type=F32)


def _sigmoid(x):
    return 1.0 / (1.0 + jnp.exp(-x))


_GELU_C = 0.7978845608028654
_GELU_A = 0.044715


def _gelu(x):
    t = jnp.tanh(_GELU_C * (x + _GELU_A * (x * x * x)))
    return 0.5 * x * (1.0 + t), t


def _gelu_grad(x, t):
    return 0.5 * (1.0 + t) + 0.5 * x * (1.0 - t * t) * (_GELU_C * (1.0 + 3.0 * _GELU_A * x * x))


def _rms_fwd(x, w):
    r = lax.rsqrt(jnp.mean(x * x, axis=-1, keepdims=True) + EPS)
    return x * r * w, r


def _rms_bwd(x, r, w, dy):
    g = dy * w
    dx = r * g - x * (r * r * r) * jnp.mean(g * x, axis=-1, keepdims=True)
    dw = jnp.sum(dy * x * r, axis=0, keepdims=True)
    return dx, dw


class _Carried:
    def __init__(self, ins, out_shapes, sems, start, finish):
        self.ins, self.out_shapes, self.sems = list(ins), list(out_shapes), list(sems)
        self.start, self.finish = start, finish


def _both(first, second):
    n_i, n_o, n_s = len(first.ins), len(first.out_shapes), len(first.sems)

    def split(ins, outs, sems):
        return (ins[:n_i], outs[:n_o], sems[:n_s]), (ins[n_i:], outs[n_o:], sems[n_s:])

    def start(ins, outs, *sems):
        (i1, o1, s1), (i2, o2, s2) = split(ins, outs, sems)
        first.start(i1, o1, *s1)
        second.start(i2, o2, *s2)

    def finish(ins, outs, *sems):
        (i1, o1, s1), (i2, o2, s2) = split(ins, outs, sems)
        first.finish(i1, o1, *s1)
        second.finish(i2, o2, *s2)

    return _Carried(first.ins + second.ins, first.out_shapes + second.out_shapes, first.sems + second.sems, start, finish)


def _split_carried(refs, n_in, n_out, n_scratch, carried):
    n_ci, n_co, n_cs = len(carried.ins), len(carried.out_shapes), len(carried.sems)
    ins, rest = refs[:n_in], refs[n_in:]
    c_ins, rest = rest[:n_ci], rest[n_ci:]
    outs, rest = rest[:n_out], rest[n_out:]
    c_outs, rest = rest[:n_co], rest[n_co:]
    scr, c_sems = rest[:n_scratch], rest[n_scratch:]
    assert len(c_sems) == n_cs
    return tuple(ins) + tuple(outs) + tuple(scr), c_ins, c_outs, c_sems


def _rows_call(name, body, tm, row_ins, const_ins, row_outs, acc_outs=(), scratch=(), carried=None):
    n_rows = row_ins[0].shape[0]
    assert n_rows % tm == 0
    n_steps = n_rows // tm
    n_in = len(row_ins) + len(const_ins)
    n_ro = len(row_outs)
    n_acc = len(acc_outs)

    def kern(*refs):
        accs = refs[n_in + n_ro:n_in + n_ro + n_acc]

        @pl.when(pl.program_id(0) == 0)
        def _():
            for a in accs:
                a[...] = jnp.zeros_like(a)

        body(*refs)

    def whole(shape):
        nd = len(shape)
        return pl.BlockSpec(tuple(shape), lambda i: (0,) * nd)

    in_specs = [pl.BlockSpec((tm, a.shape[1]), lambda i: (i, 0)) for a in row_ins]
    in_specs += [whole(a.shape) for a in const_ins]
    out_specs = [pl.BlockSpec((tm, s.shape[1]), lambda i: (i, 0)) for s in row_outs]
    out_specs += [whole(s.shape) for s in acc_outs]
    return _call_carrying(
        kern, carried, name=name, grid=(n_steps,), in_specs=in_specs, out_specs=out_specs,
        out_shape=tuple(row_outs) + tuple(acc_outs), scratch_shapes=list(scratch), operands=list(row_ins) + list(const_ins))


def _call_carrying(body, carried, *, name, grid, in_specs, out_specs, out_shape, scratch_shapes, operands):
    n_in, n_out, n_scratch = len(in_specs), len(out_specs), len(scratch_shapes)
    kern = body
    if carried is not None:
        def kern(*refs):
            plain, c_ins, c_outs, c_sems = _split_carried(refs, n_in, n_out, n_scratch, carried)
            first, last = True, True
            for d, size in enumerate(grid):
                first = jnp.logical_and(first, pl.program_id(d) == 0)
                last = jnp.logical_and(last, pl.program_id(d) == size - 1)

            @pl.when(first)
            def _():
                carried.start(c_ins, c_outs, *c_sems)

            body(*plain)

            @pl.when(last)
            def _():
                carried.finish(c_ins, c_outs, *c_sems)

        in_specs = list(in_specs) + [_HBM] * len(carried.ins)
        out_specs = list(out_specs) + [_HBM] * len(carried.out_shapes)
        out_shape = tuple(out_shape) + tuple(carried.out_shapes)
        operands = list(operands) + carried.ins
        scratch_shapes = list(scratch_shapes) + carried.sems
    return pl.pallas_call(
        kern, name=name, grid=grid, in_specs=in_specs, out_specs=out_specs, out_shape=out_shape,
        scratch_shapes=scratch_shapes, compiler_params=_cparams(len(grid)),
    )(*operands)


def _sds(shape, dtype):
    return jax.ShapeDtypeStruct(tuple(shape), dtype)


def _matmul_tn(name, a, b, tm, tn, tk, stacked=False, carried=None):
    k_dim, m_dim = a.shape
    n_dim = b.shape[1]
    assert m_dim % tm == 0 and n_dim % tn == 0 and k_dim % tk == 0
    nk = k_dim // tk

    def kern(a_ref, b_ref, o_ref, acc_ref):
        k = pl.program_id(2)
        prod = _dot_tn(a_ref[...], b_ref[...])

        @pl.when(k == 0)
        def _():
            acc_ref[...] = prod

        @pl.when(k > 0)
        def _():
            acc_ref[...] += prod

        @pl.when(k == nk - 1)
        def _():
            o_ref[...] = acc_ref[...].astype(o_ref.dtype)

    if stacked:
        assert tm == m_dim
        out_shape = _sds((n_dim // tn, m_dim, tn), BF16)
        out_spec = pl.BlockSpec((None, tm, tn), lambda i, j, k: (j, i, 0))
    else:
        out_shape = _sds((m_dim, n_dim), BF16)
        out_spec = pl.BlockSpec((tm, tn), lambda i, j, k: (i, j))
    outs = _call_carrying(
        kern, carried, name=name, grid=(m_dim // tm, n_dim // tn, nk),
        in_specs=[pl.BlockSpec((tk, tm), lambda i, j, k: (k, i)), pl.BlockSpec((tk, tn), lambda i, j, k: (k, j))],
        out_specs=[out_spec], out_shape=(out_shape,), scratch_shapes=[pltpu.VMEM((tm, tn), F32)], operands=[a, b])
    return outs[0] if carried is None else outs


def _inproj_fwd(x, nw, w_uv, w_xbc, w_z, w_dt, tm=256, carried=None):
    n_tok = x.shape[0]

    def body(x_ref, nw_ref, wuv_ref, wxbc_ref, wz_ref, wdt_ref, puv_ref, pxbc_ref, pz_ref, pdt_ref):
        h, _ = _rms_fwd(x_ref[...], nw_ref[...])
        h = h.astype(BF16)
        puv_ref[...] = jnp.dot(h, wuv_ref[...], preferred_element_type=F32)
        pxbc_ref[...] = jnp.dot(h, wxbc_ref[...], preferred_element_type=F32)
        pz_ref[...] = jnp.dot(h, wz_ref[...], preferred_element_type=F32)
        pdt_ref[...] = jnp.dot(h, wdt_ref[...], preferred_element_type=F32)

    return _rows_call(
        "inproj_fwd", body, tm, [x], [nw, w_uv, w_xbc, w_z, w_dt],
        [_sds((n_tok, 2 * GM_WIDTH), F32), _sds((n_tok, CONV_CH), F32), _sds((n_tok, SSM_WIDTH), F32),
         _sds((n_tok, DT_PAD), F32)], carried=carried)


def _head_lane_mask(width, head):
    lane = lax.broadcasted_iota(jnp.int32, (1, width), 1)
    return (lane // HEAD_DIM) == head


def _split_terms(x, terms):
    parts = []
    for _ in range(terms):
        p = x.astype(BF16)
        parts.append(p)
        x = x - p.astype(F32)
    return parts


def _seg_dots(vals, ind, terms=2):
    m = vals[0].shape[0]
    parts = []
    for v in vals:
        parts += _split_terms(v, terms)
    red = jnp.dot(jnp.concatenate(parts, axis=0), ind, preferred_element_type=F32)
    outs = []
    for i in range(len(vals)):
        acc = red[i * terms * m:(i * terms + 1) * m]
        for t in range(1, terms):
            acc = acc + red[(i * terms + t) * m:(i * terms + t + 1) * m]
        outs.append(acc)
    return outs


def _tri_dot(mask, x, terms=3):
    n = x.shape[1]
    red = jnp.dot(mask.astype(BF16), jnp.concatenate(_split_terms(x, terms), axis=1), preferred_element_type=F32)
    acc = red[:, :n]
    for t in range(1, terms):
        acc = acc + red[:, t * n:(t + 1) * n]
    return acc


def _gmlp_common(puv, lnw, lnb, e_bf, et_bf):
    u = puv[:, :GM_WIDTH]
    v = puv[:, GM_WIDTH:]
    gu, tu = _gelu(u)
    gv, tv = _gelu(v)
    (s1,) = _seg_dots([gv], et_bf)
    (mu,) = _seg_dots([s1 * (1.0 / HEAD_DIM)], e_bf)
    xc = gv - mu
    (s2,) = _seg_dots([xc * xc], et_bf)
    (rstd,) = _seg_dots([lax.rsqrt(s2 * (1.0 / HEAD_DIM) + EPS)], e_bf)
    xhat = xc * rstd
    vn = xhat * lnw + lnb
    return u, v, gu, tu, tv, rstd, xhat, vn


def _tril_mask():
    r = lax.broadcasted_iota(jnp.int32, (CHUNK, CHUNK), 0)
    c = lax.broadcasted_iota(jnp.int32, (CHUNK, CHUNK), 1)
    return r >= c


def _head_blocks(v):
    return jnp.concatenate([jnp.where(_head_lane_mask(GM_WIDTH, h), v, jnp.zeros_like(v)) for h in range(N_HEADS)], axis=0)


def _causal_w_cat(w_cat):
    t = lax.broadcasted_iota(jnp.int32, (CHUNK, N_HEADS * CHUNK), 0)
    s = lax.broadcasted_iota(jnp.int32, (CHUNK, N_HEADS * CHUNK), 1) % CHUNK
    return jnp.where(t >= s, w_cat, 0.0).astype(BF16)


def _gmlp_chunk_fwd(puv, lnw, lnb, e_bf, et_bf, wm, bmap):
    _, _, gu, _, _, _, _, vn = _gmlp_common(puv, lnw, lnb, e_bf, et_bf)
    mixed = jnp.dot(wm, _head_blocks(vn.astype(BF16)), preferred_element_type=F32) + bmap
    return (gu * mixed).astype(BF16)


SUBLANES = 8


def _shift_down(x, tail, s):
    main = pltpu.roll(x, s, 0)
    row = lax.broadcasted_iota(jnp.int32, (SUBLANES, 1), 0)
    head = jnp.where(row < s, pltpu.roll(tail, s, 0), main[:SUBLANES])
    return jnp.concatenate([head, main[SUBLANES:]], axis=0)


def _shift_up(x, head_next, s):
    n = x.shape[0]
    main = pltpu.roll(x, n - s, 0)
    row = lax.broadcasted_iota(jnp.int32, (SUBLANES, 1), 0)
    last = jnp.where(row >= SUBLANES - s, pltpu.roll(head_next, SUBLANES - s, 0), main[n - SUBLANES:])
    return jnp.concatenate([main[:n - SUBLANES], last], axis=0)


def _ssd_pre(xr, tail, cw_ref, cb, pdt, dtb, alog, emap):
    rowi = lax.broadcasted_iota(jnp.int32, (CHUNK, 1), 0)
    shifted = [_shift_down(xr, tail, 3), _shift_down(xr, tail, 2), _shift_down(xr, tail, 1), xr]
    xc = cb
    for k in range(CONV_K):
        xc = xc + cw_ref[k] * shifted[k]
    sg = _sigmoid(xc)
    xa = xc * sg
    pre = pdt + dtb
    dt = jnp.maximum(pre, 0.0) + jnp.log(1.0 + jnp.exp(-jnp.abs(pre)))
    a_neg = -jnp.exp(alog)
    a_cs = _tri_dot(_tril_mask(), dt * a_neg)
    acs_map, dt_map = _seg_dots([a_cs, dt], emap, terms=3)
    return dict(shifted=shifted, xc=xc, sg=sg, xa=xa, pre=pre, dt=dt, a_neg=a_neg, a_cs=a_cs,
                acs_map=acs_map, dt_map=dt_map, rowi=rowi)


def _ssd_maps(p):
    last = p["rowi"] == CHUNK - 1
    aq_map = jnp.sum(jnp.where(last, p["acs_map"], 0.0), axis=0, keepdims=True)
    e_exp = jnp.exp(p["acs_map"])
    dte = jnp.exp(aq_map - p["acs_map"])
    cd = jnp.exp(aq_map)
    return last, e_exp, dte, cd


def _head_decay(a_cs, a_cs_t, head, tri):
    lane = lax.broadcasted_iota(jnp.int32, (1, DT_PAD), 1)
    sub = lax.broadcasted_iota(jnp.int32, (DT_PAD, 1), 0)
    col = jnp.sum(jnp.where(lane == head, a_cs, 0.0), axis=1, keepdims=True)
    row = jnp.sum(jnp.where(sub == head, a_cs_t, 0.0), axis=0, keepdims=True)
    return jnp.exp(jnp.where(tri, col - row, -1e30))


def _gate_fwd(y, z, nw):
    sz = _sigmoid(z)
    zg = z * sz
    yg = y * zg
    outs, rs = [], []
    for g in range(SSM_GROUPS):
        gs = slice(g * GROUP_W, (g + 1) * GROUP_W)
        o, r = _rms_fwd(yg[:, gs], nw[:, gs])
        outs.append(o)
        rs.append(r)
    return sz, zg, yg, outs, rs


def _ssd_const_specs():
    def whole(shape):
        nd = len(shape)
        return pl.BlockSpec(tuple(shape), lambda c: (0,) * nd)
    return [whole((CONV_K, 1, CONV_CH)), whole((1, CONV_CH)), whole((1, DT_PAD)), whole((1, DT_PAD)),
            whole((1, SSM_WIDTH)), whole((1, SSM_WIDTH)), whole((DT_PAD, SSM_WIDTH)), whole((SSM_WIDTH, DT_PAD))]


def _mixer_fwd(p_uv, p_xbc, p_z, p_dt, x, lnw, lnb, w_cat, bmap, w_out, nw_post, nw_pre2, conv_w, conv_b, dt_bias, a_log,
               dskip_map, norm_w, e_bf, et_bf, n_seq, carried=None):
    n_tok = p_xbc.shape[0]
    nc = n_tok // n_seq // CHUNK

    def body(puv3, xr3, z3, pdt3, x3, lnw_ref, lnb_ref, wcat_ref, bmap_ref, wo_ref, nwa_ref, nwb_ref,
             cw_ref, cb_ref, dtb_ref, alog_ref, dsk_ref, nw_ref, e_ref, et_ref,
             mix3, yssd3, sprev3, o3, x13, h23, wm_scr, prev3_scr, s3_scr):
        @pl.when(pl.program_id(0) == 0)
        def _():
            wm_scr[...] = _causal_w_cat(wcat_ref[...])
            prev3_scr[...] = jnp.zeros_like(prev3_scr)
            s3_scr[...] = jnp.zeros_like(s3_scr)

        for b in range(n_seq):
            one_sequence(puv3.at[b], xr3.at[b], z3.at[b], pdt3.at[b], lnw_ref, lnb_ref, bmap_ref,
                         cw_ref, cb_ref, dtb_ref, alog_ref, dsk_ref, nw_ref, e_ref, et_ref,
                         mix3.at[b], yssd3.at[b], sprev3.at[b], wm_scr, prev3_scr.at[b], s3_scr.at[b])
            o = jnp.dot(mix3[b], wo_ref[...], preferred_element_type=F32)
            on, _ = _rms_fwd(o, nwa_ref[...])
            x1 = x3[b] + on
            h2, _ = _rms_fwd(x1, nwb_ref[...])
            o3[b] = o
            x13[b] = x1
            h23[b] = h2.astype(BF16)

    def one_sequence(puv_ref, xr_ref, z_ref, pdt_ref, lnw_ref, lnb_ref, bmap_ref,
                     cw_ref, cb_ref, dtb_ref, alog_ref, dsk_ref, nw_ref, e_ref, et_ref,
                     mix_ref, yssd_ref, sprev_ref, wm_scr, prev_scr, s_scr):
        mix_ref[:, :GM_WIDTH] = _gmlp_chunk_fwd(puv_ref[...], lnw_ref[...], lnb_ref[...], e_ref[...], et_ref[...], wm_scr[...],
                                      bmap_ref[...])
        xr = xr_ref[...]
        p = _ssd_pre(xr, prev_scr[...], cw_ref, cb_ref[...], pdt_ref[...], dtb_ref[...], alog_ref[...], e_ref[...])
        _, e_exp, dte, cd = _ssd_maps(p)
        xs = p["xa"][:, :SSM_WIDTH]
        xd = xs * p["dt_map"]
        a_cs_t = p["a_cs"].T
        tri = _tril_mask()
        s_old = s_scr[...]
        sprev_ref[...] = s_old
        for g in range(SSM_GROUPS):
            gs = slice(g * GROUP_W, (g + 1) * GROUP_W)
            bm = p["xa"][:, SSM_WIDTH + g * SSM_STATE: SSM_WIDTH + (g + 1) * SSM_STATE].astype(BF16)
            cm = p["xa"][:, SSM_WIDTH + (SSM_GROUPS + g) * SSM_STATE: SSM_WIDTH + (SSM_GROUPS + g + 1) * SSM_STATE].astype(BF16)
            cb_mat = _dot_nt(cm, bm)
            xdg = xd[:, gs].astype(BF16)
            y_g = _dot(cm, s_old[:, gs]) * e_exp[:, gs] + dsk_ref[:, gs] * xs[:, gs]
            for r in range(SSM_GROUPS * 2):
                dm = _head_decay(p["a_cs"], a_cs_t, g * 4 + r, tri)
                full = jnp.dot((cb_mat * dm).astype(BF16), xdg, preferred_element_type=F32)
                y_g = y_g + jnp.where(_head_lane_mask(GROUP_W, r), full, 0.0)
            yssd_ref[:, gs] = y_g
            s_scr[:, gs] = cd[:, gs] * s_old[:, gs] + _dot_tn(bm, xd[:, gs] * dte[:, gs])
        _, _, _, outs, _ = _gate_fwd(yssd_ref[...], z_ref[...], nw_ref[...])
        for g in range(SSM_GROUPS):
            mix_ref[:, GM_WIDTH + g * GROUP_W:GM_WIDTH + (g + 1) * GROUP_W] = outs[g].astype(BF16)
        prev_scr[...] = xr[CHUNK - SUBLANES:, :]

    seq_len = n_tok // n_seq

    def rows(width):
        return pl.BlockSpec((n_seq, CHUNK, width), lambda c: (0, c, 0))

    def whole(shape):
        nd = len(shape)
        return pl.BlockSpec(tuple(shape), lambda c: (0,) * nd)

    def by_seq(a):
        return a.reshape(n_seq, seq_len, a.shape[-1])

    outs = _call_carrying(
        body, carried, name="mixer_fwd", grid=(nc,),
        in_specs=[rows(2 * GM_WIDTH), rows(CONV_CH), rows(SSM_WIDTH), rows(DT_PAD), rows(D_MODEL), whole(lnw.shape),
                  whole(lnb.shape), whole(w_cat.shape), whole(bmap.shape), whole(w_out.shape), whole(nw_post.shape),
                  whole(nw_pre2.shape)] + _ssd_const_specs(),
        out_specs=[rows(D_MODEL), rows(SSM_WIDTH), rows(SSM_WIDTH), rows(D_MODEL), rows(D_MODEL), rows(D_MODEL)],
        out_shape=(_sds((n_seq, seq_len, D_MODEL), BF16),
                   _sds((n_seq, seq_len, SSM_WIDTH), F32), _sds((n_seq, seq_len, SSM_WIDTH), F32),
                   _sds((n_seq, seq_len, D_MODEL), F32), _sds((n_seq, seq_len, D_MODEL), F32),
                   _sds((n_seq, seq_len, D_MODEL), BF16)),
        scratch_shapes=[pltpu.VMEM((CHUNK, N_HEADS * CHUNK), BF16), pltpu.VMEM((n_seq, SUBLANES, CONV_CH), F32),
                        pltpu.VMEM((n_seq, SSM_STATE, SSM_WIDTH), F32)],
        operands=[by_seq(p_uv), by_seq(p_xbc), by_seq(p_z), by_seq(p_dt), by_seq(x), lnw, lnb, w_cat, bmap, w_out, nw_post,
                  nw_pre2, conv_w, conv_b, dt_bias, a_log, dskip_map, norm_w, e_bf, et_bf])
    return tuple(o.reshape(n_tok, o.shape[-1]) for o in outs[:6]) + tuple(outs[6:])


def _up_cols(wup_ref, j):
    per = (D_FF // N_CHIPS) // FF_TILE
    return wup_ref[j // per, :, (j % per) * FF_TILE:(j % per + 1) * FF_TILE]


def _down_rows(wda_ref, wdb_ref, j):
    assert 2 * FF_TILE == D_FF // N_CHIPS
    return (wda_ref if j % 2 == 0 else wdb_ref)[j // 2]


def _skewed_rows_call(name, main, tail, tm, lead_ins, lag_ins, const_ins, lead_outs, lag_outs, acc_outs, carry,
                      streamed, tile_copies, n_copies):
    n_rows = lead_ins[0].shape[0]
    assert n_rows % tm == 0
    n = n_rows // tm
    counts = [len(lead_ins), len(lag_ins), len(const_ins), len(streamed), len(lead_outs), len(lag_outs), len(acc_outs),
              1, len(streamed)]

    def kern(*refs):
        groups, pos = [], 0
        for cnt in counts:
            groups.append(refs[pos:pos + cnt])
            pos += cnt
        lead_i, lag_i, consts, w_hbm, lead_o, lag_o, accs, (carry_scr,), w_vmem = groups
        sems = refs[pos]
        i = pl.program_id(0)
        pieces, k = [], 0
        for piece in tile_copies(w_hbm, w_vmem):
            pieces.append([pltpu.make_async_copy(src, dst, sems.at[k + q]) for q, (src, dst) in enumerate(piece)])
            k += len(piece)

        def ready(j):
            for cp in pieces[j]:
                cp.wait()

        @pl.when(i == 0)
        def _():
            for piece in pieces:
                for cp in piece:
                    cp.start()
            for a in accs:
                a[...] = jnp.zeros_like(a)
            carry_scr[...] = main(lead_i, consts, lead_o, w_vmem, ready)

        @pl.when(jnp.logical_and(i > 0, i < n))
        def _():
            previous = carry_scr[...]
            carry_scr[...] = main(lead_i, consts, lead_o, w_vmem, lambda j: None)
            tail(previous, lag_i, consts, lag_o, accs)

        @pl.when(i == n)
        def _():
            tail(carry_scr[...], lag_i, consts, lag_o, accs)

    def lead(width):
        return pl.BlockSpec((tm, width), lambda i: (jnp.minimum(i, n - 1), 0))

    def lag(width):
        return pl.BlockSpec((tm, width), lambda i: (jnp.maximum(i - 1, 0), 0))

    def whole(shape):
        nd = len(shape)
        return pl.BlockSpec(tuple(shape), lambda i: (0,) * nd)

    return pl.pallas_call(
        kern, name=name, grid=(n + 1,),
        in_specs=([lead(a.shape[1]) for a in lead_ins] + [lag(a.shape[1]) for a in lag_ins]
                  + [whole(a.shape) for a in const_ins] + [_HBM] * len(streamed)),
        out_specs=[lead(s.shape[1]) for s in lead_outs] + [lag(s.shape[1]) for s in lag_outs] + [whole(s.shape) for s in acc_outs],
        out_shape=tuple(lead_outs) + tuple(lag_outs) + tuple(acc_outs),
        scratch_shapes=([pltpu.VMEM(carry, F32)] + [pltpu.VMEM(a.shape, a.dtype) for a in streamed]
                        + [pltpu.SemaphoreType.DMA((n_copies,))]),
        compiler_params=_cparams(1),
    )(*lead_ins, *lag_ins, *const_ins, *streamed)


def _mlp_weight_pieces(order):
    per = (D_FF // N_CHIPS) // FF_TILE

    def tile_copies(hbm, vmem):
        pieces = []
        for j in range(D_FF // FF_TILE):
            cols = (j // per, slice(None), pl.ds((j % per) * FF_TILE, FF_TILE))
            up = (hbm[0].at[cols], vmem[0].at[cols])
            down = (hbm[1 + j % 2].at[j // 2], vmem[1 + j % 2].at[j // 2])
            pieces.append([up, down] if order == "up_down" else [down, up])
        return pieces

    return tile_copies


def _mlp_fwd(h2, x1, tgt, w_up, w_down_a, w_down_b, nw, tm=512):
    n_tok = x1.shape[0]

    def main(lead_i, consts, lead_o, weights, ready):
        (h2_ref,), (f_ref,), (wup_ref, wda_ref, wdb_ref) = lead_i, lead_o, weights
        h2v = h2_ref[...]
        acc = jnp.zeros((tm, D_MODEL), F32)
        for j in range(D_FF // FF_TILE):
            cs = slice(j * FF_TILE, (j + 1) * FF_TILE)
            ready(j)
            u = jnp.dot(h2v, _up_cols(wup_ref, j), preferred_element_type=F32)
            f = jnp.square(jnp.maximum(u, 0.0)).astype(BF16)
            f_ref[:, cs] = f
            acc = acc + jnp.dot(f, _down_rows(wda_ref, wdb_ref, j), preferred_element_type=F32)
        return acc

    def tail(acc, lag_i, consts, lag_o, accs):
        (x1_ref, tgt_ref), (nw_ref,), (dd_ref, dy_ref), (loss_ref, dnw_ref) = lag_i, consts, lag_o, accs
        dn, r = _rms_fwd(acc, nw_ref[...])
        e = x1_ref[...] + dn - tgt_ref[...]
        loss_ref[...] += jnp.full(loss_ref.shape, (0.5 / D_MODEL) * jnp.sum(e * e), F32)
        dy = e * (1.0 / D_MODEL)
        dd, dnw = _rms_bwd(acc, r, nw_ref[...], dy)
        dy_ref[...] = dy
        dd_ref[...] = dd.astype(BF16)
        dnw_ref[...] += dnw

    return _skewed_rows_call(
        "mlp_fwd", main, tail, tm, [h2], [x1, tgt], [nw],
        [_sds((n_tok, D_FF), BF16)], [_sds((n_tok, D_MODEL), BF16), _sds((n_tok, D_MODEL), F32)],
        [_sds((8, 128), F32), _sds((1, D_MODEL), F32)], carry=(tm, D_MODEL),
        streamed=[w_up, w_down_a, w_down_b], tile_copies=_mlp_weight_pieces("up_down"), n_copies=2 * (D_FF // FF_TILE))


def _mlp_bwd(dd, f, x1, dy, w_down_a, w_down_b, w_up, nw, tm=256):
    n_tok = x1.shape[0]

    def main(lead_i, consts, lead_o, weights, ready):
        (dd_ref, f_ref), (dup_ref,), (wup_ref, wda_ref, wdb_ref) = lead_i, lead_o, weights
        ddv = dd_ref[...]
        acc = jnp.zeros((tm, D_MODEL), F32)
        for j in range(D_FF // FF_TILE):
            cs = slice(j * FF_TILE, (j + 1) * FF_TILE)
            ready(j)
            df = _dot_nt(ddv, _down_rows(wda_ref, wdb_ref, j))
            du = (df * (2.0 * jnp.sqrt(f_ref[:, cs].astype(F32)))).astype(BF16)
            dup_ref[:, cs] = du
            acc = acc + _dot_nt(du, _up_cols(wup_ref, j))
        return acc

    def tail(acc, lag_i, consts, lag_o, accs):
        (x1_ref, dy_ref), (nw_ref,), (dx1_ref,), (dnw_ref,) = lag_i, consts, lag_o, accs
        x1v = x1_ref[...]
        _, r = _rms_fwd(x1v, nw_ref[...])
        dx, dnw = _rms_bwd(x1v, r, nw_ref[...], acc)
        dx1_ref[...] = dy_ref[...] + dx
        dnw_ref[...] += dnw

    return _skewed_rows_call(
        "mlp_bwd", main, tail, tm, [dd, f], [x1, dy], [nw],
        [_sds((n_tok, D_FF), BF16)], [_sds((n_tok, D_MODEL), F32)], [_sds((1, D_MODEL), F32)], carry=(tm, D_MODEL),
        streamed=[w_up, w_down_a, w_down_b], tile_copies=_mlp_weight_pieces("down_up"), n_copies=2 * (D_FF // FF_TILE))


def _outproj_bwd(dx1, o, w_out, nw, tm=256, carried=None):
    n_tok = dx1.shape[0]

    def body(dx1_ref, o_ref, wo_ref, nw_ref, do_ref, dya_ref, dyb_ref, dnw_ref):
        ov = o_ref[...]
        _, r = _rms_fwd(ov, nw_ref[...])
        do, dnw = _rms_bwd(ov, r, nw_ref[...], dx1_ref[...])
        dob = do.astype(BF16)
        do_ref[...] = dob
        dya_ref[...] = _dot_nt(dob, wo_ref[:GM_WIDTH, :])
        dyb_ref[...] = _dot_nt(dob, wo_ref[GM_WIDTH:, :])
        dnw_ref[...] += dnw

    return _rows_call("outproj_bwd", body, tm, [dx1, o], [w_out, nw],
                      [_sds((n_tok, D_MODEL), BF16), _sds((n_tok, GM_WIDTH), F32), _sds((n_tok, SSM_WIDTH), F32)],
                      [_sds((1, D_MODEL), F32)], carried=carried)


def _gmlp_bwd(p_uv, dya, lnw, lnb, e_bf, et_bf, w_cat, w_stack, bmap, carried=None):
    n_tok = p_uv.shape[0]
    chunks_per_step = 2

    def body(puv_ref, dya_ref, lnw_ref, lnb_ref, e_ref, et_ref, wcat_ref, wstack_ref, bmap_ref,
             dpuv_ref, dws_ref, dbs_ref, dlnw_ref, dlnb_ref, wm_scr, wsm_scr):
        t_stk = lax.broadcasted_iota(jnp.int32, (N_HEADS * CHUNK, CHUNK), 0) % CHUNK
        s_stk = lax.broadcasted_iota(jnp.int32, (N_HEADS * CHUNK, CHUNK), 1)

        @pl.when(pl.program_id(0) == 0)
        def _():
            wm_scr[...] = _causal_w_cat(wcat_ref[...])
            wsm_scr[...] = jnp.where(t_stk >= s_stk, wstack_ref[...], 0.0).astype(BF16)

        lnw_v = lnw_ref[...]
        e_v, et_v = e_ref[...], et_ref[...]

        def one_chunk(rows):
            u, v, gu, tu, tv, rstd, xhat, vn = _gmlp_common(puv_ref[rows, :], lnw_v, lnb_ref[...], e_v, et_v)
            vnb = vn.astype(BF16)
            mixed = jnp.dot(wm_scr[...], _head_blocks(vnb), preferred_element_type=F32) + bmap_ref[...]
            dy = dya_ref[rows, :]
            du = dy * mixed * _gelu_grad(u, tu)
            dmixed = dy * gu
            (dbs,) = _seg_dots([dmixed], et_v)
            dblocks = _head_blocks(dmixed.astype(BF16))
            dvn = lax.dot_general(wsm_scr[...], dblocks, (((0,), (0,)), ((), ())), preferred_element_type=F32)
            dws = lax.dot_general(dblocks, vnb, (((1,), (1,)), ((), ())), preferred_element_type=F32)
            dxh = dvn * lnw_v
            m1, m2 = _seg_dots([dxh, dxh * xhat], et_v)
            m1, m2 = _seg_dots([m1 * (1.0 / HEAD_DIM), m2 * (1.0 / HEAD_DIM)], e_v)
            dgv = rstd * (dxh - m1 - xhat * m2)
            dv = dgv * _gelu_grad(v, tv)
            dpuv_ref[rows, :GM_WIDTH] = du.astype(BF16)
            dpuv_ref[rows, GM_WIDTH:] = dv.astype(BF16)
            return dbs, dws, jnp.sum(dvn * xhat, axis=0, keepdims=True), jnp.sum(dvn, axis=0, keepdims=True)

        parts = [one_chunk(slice(k * CHUNK, (k + 1) * CHUNK)) for k in range(chunks_per_step)]
        dbs, dws, dlnw, dlnb = [functools.reduce(lambda a, b: a + b, vals) for vals in zip(*parts)]
        dbs_ref[...] += dbs
        dws_ref[...] += jnp.where(t_stk >= s_stk, dws, 0.0)
        dlnw_ref[...] += dlnw
        dlnb_ref[...] += dlnb

    return _rows_call(
        "gmlp_bwd", body, chunks_per_step * CHUNK, [p_uv, dya], [lnw, lnb, e_bf, et_bf, w_cat, w_stack, bmap],
        [_sds((n_tok, 2 * GM_WIDTH), BF16)],
        [_sds((N_HEADS * CHUNK, CHUNK), F32), _sds((CHUNK, DT_PAD), F32), _sds((1, GM_WIDTH), F32),
         _sds((1, GM_WIDTH), F32)],
        scratch=[pltpu.VMEM((CHUNK, N_HEADS * CHUNK), BF16), pltpu.VMEM((N_HEADS * CHUNK, CHUNK), BF16)],
        carried=carried)


def _ssd_bwd(p_xbc, p_z, p_dt, yssd, sprev, dyb, conv_w, conv_b, dt_bias, a_log, dskip_map, norm_w, e_bf, et_bf, n_seq,
             carried=None):
    n_tok = p_xbc.shape[0]
    nc = n_tok // n_seq // CHUNK

    def body(xr3, xprev3, z3, pdt3, yssd3, sprev3, dyb3,
             cw_ref, cb_ref, dtb_ref, alog_ref, dsk_ref, nw_ref, e_ref, et_ref,
             dpxbc3, dpz3, dpdt3, dcw_ref, dcb_ref, ddtb_ref, dalog_ref, ddsk_ref, dnw_ref,
             ds3_scr, nxt3_scr, dxa3_scr):
        @pl.when(pl.program_id(0) == 0)
        def _():
            for a in (dcw_ref, dcb_ref, ddtb_ref, dalog_ref, ddsk_ref, dnw_ref, ds3_scr, nxt3_scr):
                a[...] = jnp.zeros_like(a)

        for b in range(n_seq):
            one_sequence(xr3.at[b], xprev3.at[b], z3.at[b], pdt3.at[b], yssd3.at[b], sprev3.at[b], dyb3.at[b],
                         cw_ref, cb_ref, dtb_ref, alog_ref, dsk_ref, nw_ref, e_ref, et_ref,
                         dpxbc3.at[b], dpz3.at[b], dpdt3.at[b], dcw_ref, dcb_ref, ddtb_ref, dalog_ref, ddsk_ref, dnw_ref,
                         ds3_scr.at[b], nxt3_scr.at[b], dxa3_scr.at[b])

    def one_sequence(xr_ref, xprev_ref, z_ref, pdt_ref, yssd_ref, sprev_ref, dyb_ref,
                     cw_ref, cb_ref, dtb_ref, alog_ref, dsk_ref, nw_ref, e_ref, et_ref,
                     dpxbc_ref, dpz_ref, dpdt_ref, dcw_ref, dcb_ref, ddtb_ref, dalog_ref, ddsk_ref, dnw_ref,
                     ds_scr, nxt_scr, dxa_scr):
        chunk = nc - 1 - pl.program_id(0)
        xr = xr_ref[...]
        prev = jnp.where(chunk == 0, 0.0, xprev_ref[...])
        et_v = et_ref[...]
        p = _ssd_pre(xr, prev, cw_ref, cb_ref[...], pdt_ref[...], dtb_ref[...], alog_ref[...], e_ref[...])
        last, e_exp, dte, cd = _ssd_maps(p)
        rowi = p["rowi"]
        xs = p["xa"][:, :SSM_WIDTH]
        xd = xs * p["dt_map"]
        a_cs_t = p["a_cs"].T
        tri = _tril_mask()
        dsk = dsk_ref[...]
        nw_v = nw_ref[...]

        yv = yssd_ref[...]
        zv = z_ref[...]
        sz, zg, yg, _, rs = _gate_fwd(yv, zv, nw_v)
        dout = dyb_ref[...]
        for g in range(SSM_GROUPS):
            gs = slice(g * GROUP_W, (g + 1) * GROUP_W)
            dyg_g, dnw_g = _rms_bwd(yg[:, gs], rs[g], nw_v[:, gs], dout[:, gs])
            dnw_ref[:, gs] += dnw_g
            dxa_scr[:, gs] = dyg_g
        dyg = dxa_scr[:, :SSM_WIDTH]
        d_y = dyg * zg
        dpz_ref[...] = (dyg * yv * (sz + zv * sz * (1.0 - sz))).astype(BF16)

        s_prev = sprev_ref[...]
        ds_next = ds_scr[...]
        lane_dt = lax.broadcasted_iota(jnp.int32, (1, DT_PAD), 1)
        da_cols = jnp.zeros((CHUNK, DT_PAD), F32)
        for g in range(SSM_GROUPS):
            gs = slice(g * GROUP_W, (g + 1) * GROUP_W)
            b_off = SSM_WIDTH + g * SSM_STATE
            c_off = SSM_WIDTH + (SSM_GROUPS + g) * SSM_STATE
            bm = p["xa"][:, b_off:b_off + SSM_STATE].astype(BF16)
            cm = p["xa"][:, c_off:c_off + SSM_STATE].astype(BF16)
            cb_mat = _dot_nt(cm, bm)
            d_yg = d_y[:, gs]
            d_ygb = d_yg.astype(BF16)
            xdg = xd[:, gs]
            xdgb = xdg.astype(BF16)
            ds_g = ds_next[:, gs]
            sp_g = s_prev[:, gs]
            bds = _dot(bm, ds_g)
            dcs = d_yg * e_exp[:, gs]
            d_c = _dot_nt(dcs, sp_g)
            ds_scr[:, gs] = cd[:, gs] * ds_g + _dot_tn(cm, dcs)
            d_b = _dot_nt(xdg * dte[:, gs], ds_g)
            dxd_g = bds * dte[:, gs]
            sum_dcb = jnp.zeros((CHUNK, CHUNK), F32)
            for r in range(SSM_GROUPS * 2):
                head = g * 4 + r
                mask = _head_lane_mask(GROUP_W, r)
                dm = _head_decay(p["a_cs"], a_cs_t, head, tri)
                m_mat = cb_mat * dm
                g_mat = _dot_nt(jnp.where(mask, d_yg, 0.0), xdgb)
                w_mat = g_mat * m_mat
                sum_dcb = sum_dcb + g_mat * dm
                dxd_g = dxd_g + jnp.where(mask, _dot_tn(m_mat, d_ygb), 0.0)
                da_h = jnp.sum(w_mat - w_mat.T, axis=1, keepdims=True)
                da_cols = da_cols + jnp.where(lane_dt == head, da_h, 0.0)
            d_c = d_c + _dot(sum_dcb, bm)
            d_b = d_b + _dot_tn(sum_dcb, cm)
            dxa_scr[:, b_off:b_off + SSM_STATE] = d_b
            dxa_scr[:, c_off:c_off + SSM_STATE] = d_c
            y_off_g = _dot(cm, sp_g) * e_exp[:, gs]
            t3 = bds * xdg * dte[:, gs]
            tail = jnp.sum(t3, axis=0, keepdims=True) + jnp.sum(ds_g * sp_g, axis=0, keepdims=True) * cd[:, gs]
            pre_g = d_yg * y_off_g - t3 + jnp.where(last, tail, 0.0)
            s_pre, ddt_g, s_dsk = _seg_dots([pre_g, dxd_g * xs[:, gs], d_yg * xs[:, gs]], et_v[gs, :])
            da_cols = da_cols + s_pre
            ddsk_ref[...] += jnp.sum(s_dsk, axis=0, keepdims=True)
            dxa_scr[:, gs] = dxd_g * p["dt_map"][:, gs] + dsk[:, gs] * d_yg
            if g == 0:
                ddt = ddt_g
            else:
                ddt = ddt + ddt_g
        r_i = lax.broadcasted_iota(jnp.int32, (CHUNK, CHUNK), 0)
        c_i = lax.broadcasted_iota(jnp.int32, (CHUNK, CHUNK), 1)
        ddta = _tri_dot(r_i <= c_i, da_cols, terms=2)
        ddt = ddt + ddta * p["a_neg"]
        dalog_ref[...] += jnp.sum(ddta * p["dt"], axis=0, keepdims=True) * p["a_neg"]
        draw = ddt * _sigmoid(p["pre"])
        ddtb_ref[...] += jnp.sum(draw, axis=0, keepdims=True)
        dpdt_ref[...] = draw.astype(BF16)

        xc = p["xc"]
        sg = p["sg"]
        dxc = dxa_scr[...] * (sg + xc * sg * (1.0 - sg))
        dcb_ref[...] += jnp.sum(dxc, axis=0, keepdims=True)
        for k in range(CONV_K):
            dcw_ref[k] += jnp.sum(dxc * p["shifted"][k], axis=0, keepdims=True)
        nxt = nxt_scr[...]
        dxr = cw_ref[3] * dxc
        for s in range(1, CONV_K):
            dxr = dxr + cw_ref[CONV_K - 1 - s] * _shift_up(dxc, nxt, s)
        dpxbc_ref[...] = dxr.astype(BF16)
        nxt_scr[...] = dxc[:SUBLANES, :]

    seq_len = n_tok // n_seq

    def rows(width):
        return pl.BlockSpec((n_seq, CHUNK, width), lambda s: (0, nc - 1 - s, 0))

    tiles = CHUNK // SUBLANES
    prev_rows = pl.BlockSpec((n_seq, SUBLANES, CONV_CH), lambda s: (0, jnp.maximum((nc - 1 - s) * tiles - 1, 0), 0))

    def whole(shape):
        nd = len(shape)
        return pl.BlockSpec(tuple(shape), lambda s: (0,) * nd)

    def by_seq(a):
        return a.reshape(n_seq, seq_len, a.shape[-1])

    acc_shapes = [(CONV_K, 1, CONV_CH), (1, CONV_CH), (1, DT_PAD), (1, DT_PAD), (1, DT_PAD), (1, SSM_WIDTH)]
    xbc3 = by_seq(p_xbc)
    outs = _call_carrying(
        body, carried, name="ssd_bwd", grid=(nc,),
        in_specs=[rows(CONV_CH), prev_rows, rows(SSM_WIDTH), rows(DT_PAD), rows(SSM_WIDTH), rows(SSM_WIDTH),
                  rows(SSM_WIDTH)] + _ssd_const_specs(),
        out_specs=[rows(CONV_CH), rows(SSM_WIDTH), rows(DT_PAD)] + [whole(s) for s in acc_shapes],
        out_shape=tuple([_sds((n_seq, seq_len, CONV_CH), BF16), _sds((n_seq, seq_len, SSM_WIDTH), BF16),
                         _sds((n_seq, seq_len, DT_PAD), BF16)] + [_sds(s, F32) for s in acc_shapes]),
        scratch_shapes=[pltpu.VMEM((n_seq, SSM_STATE, SSM_WIDTH), F32), pltpu.VMEM((n_seq, SUBLANES, CONV_CH), F32),
                        pltpu.VMEM((n_seq, CHUNK, CONV_CH), F32)],
        operands=[xbc3, xbc3, by_seq(p_z), by_seq(p_dt), by_seq(yssd), by_seq(sprev), by_seq(dyb), conv_w, conv_b, dt_bias,
                  a_log, dskip_map, norm_w, e_bf, et_bf])
    return tuple(o.reshape(n_tok, o.shape[-1]) for o in outs[:3]) + tuple(outs[3:])


def _inproj_bwd(dp_uv, dp_xbc, dp_z, dp_dt, x, dx1, w_uv, w_xbc, w_z, w_dt, nw, tm=256, carried=None):
    n_tok = x.shape[0]

    def body(duv_ref, dxbc_ref, dz_ref, ddt_ref, x_ref, dx1_ref, wuv_ref, wxbc_ref, wz_ref, wdt_ref, nw_ref,
             gx_ref, h_ref, dnw_ref):
        dh = _dot_nt(duv_ref[...], wuv_ref[...]) + _dot_nt(dxbc_ref[...], wxbc_ref[...])
        dh = dh + _dot_nt(dz_ref[...], wz_ref[...]) + _dot_nt(ddt_ref[...], wdt_ref[...])
        xv = x_ref[...]
        h, r = _rms_fwd(xv, nw_ref[...])
        dx, dnw = _rms_bwd(xv, r, nw_ref[...], dh)
        gx_ref[...] = dx1_ref[...] + dx
        h_ref[...] = h.astype(BF16)
        dnw_ref[...] += dnw

    return _rows_call("inproj_bwd", body, tm, [dp_uv, dp_xbc, dp_z, dp_dt, x, dx1], [w_uv, w_xbc, w_z, w_dt, nw],
                      [_sds((n_tok, D_MODEL), F32), _sds((n_tok, D_MODEL), BF16)], [_sds((1, D_MODEL), F32)],
                      carried=carried)


def _const_maps():
    lane = jnp.arange(SSM_WIDTH) // HEAD_DIM
    e_bf = (jnp.arange(DT_PAD)[:, None] == lane[None, :]).astype(BF16)
    return e_bf, e_bf.T


def _pad_lanes(v, width):
    return jnp.pad(v, ((0, 0), (0, width - v.shape[1])))


SHARD_COLS = IN_COLS // N_CHIPS
_UV_END = 2 * GM_WIDTH
_Z_END = _UV_END + SSM_WIDTH
_XBC_END = _Z_END + CONV_CH


def _cols_from_shards(w4, lo, hi):
    pieces = []
    for j in range(N_CHIPS):
        a, b = max(lo, j * SHARD_COLS), min(hi, (j + 1) * SHARD_COLS)
        if a < b:
            pieces.append(w4[j][:, a - j * SHARD_COLS:b - j * SHARD_COLS])
    return pieces[0] if len(pieces) == 1 else jnp.concatenate(pieces, axis=1)


def _shards_from_cols(blocks):
    shards = []
    for j in range(N_CHIPS):
        pieces = []
        for arr, lo, hi in blocks:
            a, b = max(lo, j * SHARD_COLS), min(hi, (j + 1) * SHARD_COLS)
            if a < b:
                pieces.append(arr[:, a - lo:b - lo])
        shards.append(pieces[0] if len(pieces) == 1 else jnp.concatenate(pieces, axis=1))
    return jnp.stack(shards)


def _forward_backward(x, tgt, w_in4, conv_w, small, out_shard, up_shard, down_shard, core, adam_args):
    n_seq, seq_len, _ = x.shape
    n_tok = n_seq * seq_len
    x2 = x.reshape(n_tok, D_MODEL)
    tgt2 = tgt.reshape(n_tok, D_MODEL)
    e_bf, et_bf = _const_maps()

    w_uv = _cols_from_shards(w_in4, 0, _UV_END)
    w_z = _cols_from_shards(w_in4, _UV_END, _Z_END)
    w_xbc = _cols_from_shards(w_in4, _Z_END, _XBC_END)
    w_dt = _pad_lanes(_cols_from_shards(w_in4, _XBC_END, IN_COLS), DT_PAD)

    nw_pre = small["norm_mix_pre"]
    lnw = small["gm_ln_w"].reshape(1, GM_WIDTH)
    lnb = small["gm_ln_b"].reshape(1, GM_WIDTH)
    w_stack = small["gm_w_s"].reshape(N_HEADS * CHUNK, CHUNK)
    w_cat = jnp.transpose(small["gm_w_s"], (1, 0, 2)).reshape(CHUNK, N_HEADS * CHUNK)
    bmap = jnp.repeat(small["gm_b_s"].T, HEAD_DIM, axis=1)
    cw3 = conv_w.reshape(CONV_K, 1, CONV_CH)
    conv_b = small["conv_b"]
    dt_bias = _pad_lanes(small["dt_bias"], DT_PAD)
    a_log = _pad_lanes(small["a_log"], DT_PAD)
    dskip_map = jnp.repeat(small["d_skip"], HEAD_DIM, axis=1)
    ssm_nw = small["ssm_norm_w"]

    half = down_shard.shape[0] // 2
    p_uv, p_xbc, p_z, p_dt, w_out4 = _inproj_fwd(
        x2, nw_pre, w_uv, w_xbc, w_z, w_dt, carried=_allgather_exchange([out_shard]))
    ssd_consts = (cw3, conv_b, dt_bias, a_log, dskip_map, ssm_nw, e_bf, et_bf)
    w_out_b = w_out4.reshape(D_MODEL, D_MODEL)
    mix, yssd, sprev, o, x1, h2, w_up4, w_down_a, w_down_b = _mixer_fwd(
        p_uv, p_xbc, p_z, p_dt, x2, lnw, lnb, w_cat, bmap, w_out_b, small["norm_mix_post"], small["norm_ffn_pre"],
        *ssd_consts, n_seq, carried=_allgather_exchange([up_shard, down_shard[:half], down_shard[half:]]))
    f, dd, dy, loss_acc, d_nffn_post = _mlp_fwd(h2, x1, tgt2, w_up4, w_down_a, w_down_b, small["norm_ffn_post"])

    dup, dx1, d_nffn_pre = _mlp_bwd(dd, f, x1, dy, w_down_a, w_down_b, w_up4, small["norm_ffn_pre"])
    tk = min(DW_TOKENS_PER_STEP, n_tok)
    g_up = _matmul_tn("dw_up", h2, dup, D_MODEL, D_MODEL, tk, stacked=True)
    g_down = _matmul_tn("dw_down", f, dd, 1024, D_MODEL, tk).reshape(N_CHIPS, D_FF // N_CHIPS, D_MODEL)
    do, dya, dyb, d_nmix_post, got_up, got_down = _outproj_bwd(
        dx1, o, w_out_b, small["norm_mix_post"], carried=_pair_exchange([g_up, g_down]))
    h_up = _pair_sum(core, g_up, got_up, 256)
    h_down = _pair_sum(core, g_down, got_down, 256)
    g_out = _matmul_tn("dw_out", mix, do, D_MODEL, D_MODEL, tk).reshape(N_CHIPS, D_MODEL // N_CHIPS, D_MODEL)
    dp_uv, d_ws, d_bs_t, d_lnw, d_lnb, slab_up, got_out = _gmlp_bwd(
        p_uv, dya, lnw, lnb, e_bf, et_bf, w_cat, w_stack, bmap,
        carried=_both(_chip_exchange([h_up]), _pair_exchange([g_out])))
    h_out = _pair_sum(core, g_out, got_out, 128)
    early = {
        "gm_ln_w": d_lnw.reshape(N_HEADS, HEAD_DIM), "gm_ln_b": d_lnb.reshape(N_HEADS, HEAD_DIM),
        "gm_w_s": d_ws.reshape(N_HEADS, CHUNK, CHUNK), "gm_b_s": d_bs_t[:, :N_HEADS].T,
        "norm_mix_post": d_nmix_post, "norm_ffn_pre": d_nffn_pre, "norm_ffn_post": d_nffn_post,
    }
    packed_early = _pack(early, tuple(early), tail=loss_acc[0, 0].reshape(1))
    (dp_xbc, dp_z, dp_dt, d_cw, d_cb, d_dtb, d_alog, d_dsk, d_ssm_nw, slab_down, slab_out, all_early) = _ssd_bwd(
        p_xbc, p_z, p_dt, yssd, sprev, dyb, *ssd_consts, n_seq,
        carried=_both(_chip_exchange([h_down, h_out]), _device_gather_exchange(packed_early)))
    gx, h, d_nmix_pre = _inproj_bwd(dp_uv, dp_xbc, dp_z, dp_dt, x2, dx1, w_uv, w_xbc, w_z, w_dt, nw_pre)
    late = {
        "norm_mix_pre": d_nmix_pre, "conv_w": d_cw.reshape(CONV_K, CONV_CH), "conv_b": d_cb,
        "dt_bias": d_dtb[:, :N_HEADS], "a_log": d_alog[:, :N_HEADS], "d_skip": d_dsk[:, :N_HEADS],
        "ssm_norm_w": d_ssm_nw,
    }
    g_uv, all_late = _matmul_tn("dw_in_uv", h, dp_uv, D_MODEL, 2 * GM_WIDTH, tk,
                                carried=_device_gather_exchange(_pack(late, tuple(late))))
    sum_early = _ordered_sum("small_sum_early", all_early)
    small_sum = _unpack(sum_early, {n: v.shape for n, v in early.items()}, tuple(early))
    small_sum.update(_unpack(_ordered_sum("small_sum_late", all_late), {n: v.shape for n, v in late.items()}, tuple(late)))
    loss = sum_early.reshape(-1)[sum(v.size for v in early.values())]
    red_up, red_down, red_out = _chip_sum(slab_up, 256), _chip_sum(slab_down, 256), _chip_sum(slab_out, 128)
    g_xbc, oth_up, oth_down, oth_out = _matmul_tn("dw_in_xbc", h, dp_xbc, D_MODEL, CONV_CH, tk,
                                                  carried=_pair_swap([red_up, red_down, red_out]))
    g_z = _matmul_tn("dw_in_z", h, dp_z, D_MODEL, SSM_WIDTH, tk)
    g_dt = _matmul_tn("dw_in_dt", h, dp_dt, D_MODEL, DT_PAD, tk)

    g_in = _shards_from_cols([(g_uv, 0, _UV_END), (g_z, _UV_END, _Z_END), (g_xbc, _Z_END, _XBC_END),
                              (g_dt, _XBC_END, IN_COLS)])
    (got_in,) = _run_exchange("grad_pair_exchange_in", _pair_exchange([g_in]))
    h_in = _pair_sum(core, g_in, got_in, 256)
    (slab_in,) = _run_exchange("grad_chip_exchange", _chip_exchange([h_in]))
    res = _adamw_halves("adamw_mlp", [(adam_args["w_up"][0], red_up, oth_up) + adam_args["w_up"][1:],
                                      (adam_args["w_down"][0], red_down, oth_down) + adam_args["w_down"][1:]], 256)
    big_out = {"w_up": res[0:4], "w_down": res[4:8]}
    big_out["w_out"] = _adamw_halves("adamw_w_out", [(adam_args["w_out"][0], red_out, oth_out) + adam_args["w_out"][1:]], 128)
    red_in = _chip_sum(slab_in, 256)
    (oth_in,) = _run_exchange("grad_pair_swap_in", _pair_swap([red_in]))
    big_out["w_in"] = _adamw_halves("adamw_w_in", [(adam_args["w_in"][0], red_in, oth_in) + adam_args["w_in"][1:]], 256)

    return loss, gx.reshape(x.shape), big_out, small_sum


_HBM = pl.BlockSpec(memory_space=pltpu.HBM)


D2D_CHUNKS = 8
ICI_CHUNKS = 1
ROW_ALIGN = 16


def _row_chunks(rows, n_chunks):
    size = min(max(rows // n_chunks, ROW_ALIGN), rows)
    assert rows % size == 0
    return [(start, size) for start in range(0, rows, size)]


def _position():
    x, y, c = lax.axis_index("x"), lax.axis_index("y"), lax.axis_index("c")
    chips = [(1 - x, y), (x, 1 - y), (1 - x, 1 - y)]
    return x, y, c, chips


def _allgather_exchange(arrs):
    n = len(arrs)

    def copies(ins, outs, send_sems, recv_sems, local_sems):
        x, y, c, chips = _position()
        me = 2 * x + y
        sibling = (x, y, 1 - c)

        def copy(a, k, src, dst, to):
            return pltpu.make_async_remote_copy(src_ref=src, dst_ref=dst, send_sem=send_sems.at[a, k],
                                                recv_sem=recv_sems.at[a, k], device_id=to, device_id_type=MESH)

        def half_rows(a, pc):
            half = ins[a].shape[0] // 2
            return pl.ds(pc * half, half)

        local = [pltpu.make_async_copy(ins[a], outs[a].at[me], local_sems.at[a]) for a in range(n)]
        ici_out = [[copy(a, k, ins[a].at[half_rows(a, c)], outs[a].at[me, half_rows(a, c)], (px, py, c))
                    for k, (px, py) in enumerate(chips)] for a in range(n)]
        return c, chips, sibling, copy, half_rows, local, ici_out

    def start(ins, outs, send_sems, recv_sems, local_sems):
        c, chips, _, copy, _, local, _ = copies(ins, outs, send_sems, recv_sems, local_sems)
        x, y, _, _ = _position()
        me = 2 * x + y
        for cp in local:
            cp.start()
        for a in range(n):
            half = ins[a].shape[0] // 2
            for k, (px, py) in enumerate(chips):
                for first, size in _row_chunks(half, ICI_CHUNKS):
                    rows = pl.ds(c * half + first, size)
                    copy(a, k, ins[a].at[rows], outs[a].at[me, rows], (px, py, c)).start()

    def finish(ins, outs, send_sems, recv_sems, local_sems):
        c, chips, sibling, copy, half_rows, local, ici_out = copies(ins, outs, send_sems, recv_sems, local_sems)
        for a in range(n):
            half = ins[a].shape[0] // 2
            for k, (px, py) in enumerate(chips):
                blk = outs[a].at[2 * px + py, half_rows(a, c)]
                copy(a, k, blk, blk, (px, py, c)).wait_recv()
                for first, size in _row_chunks(half, D2D_CHUNKS):
                    piece = outs[a].at[2 * px + py, pl.ds(c * half + first, size)]
                    copy(a, 3 + k, piece, piece, sibling).start()
        for a in range(n):
            for k, (px, py) in enumerate(chips):
                theirs = outs[a].at[2 * px + py, half_rows(a, 1 - c)]
                copy(a, 3 + k, theirs, theirs, sibling).wait_recv()
                mine = outs[a].at[2 * px + py, half_rows(a, c)]
                copy(a, 3 + k, mine, mine, sibling).wait_send()
        for a in range(n):
            for cp in ici_out[a]:
                cp.wait_send()
        for cp in local:
            cp.wait()

    return _Carried(arrs, [_sds((N_CHIPS,) + a.shape, a.dtype) for a in arrs],
                    [pltpu.SemaphoreType.DMA((n, 6)), pltpu.SemaphoreType.DMA((n, 6)), pltpu.SemaphoreType.DMA((n,))],
                    start, finish)


def _run_exchange(name, exchange):
    n_in, n_out = len(exchange.ins), len(exchange.out_shapes)

    def body(*refs):
        ins, outs, sems = refs[:n_in], refs[n_in:n_in + n_out], refs[n_in + n_out:]
        exchange.start(ins, outs, *sems)
        exchange.finish(ins, outs, *sems)

    return pl.pallas_call(
        body, name=name, out_shape=tuple(exchange.out_shapes), in_specs=[_HBM] * n_in,
        out_specs=tuple([_HBM] * n_out), scratch_shapes=exchange.sems,
    )(*exchange.ins)


def _pair_exchange(grads):
    n = len(grads)

    def copier(send_sems, recv_sems):
        x, y, c, _ = _position()

        def copy(a, src, dst):
            return pltpu.make_async_remote_copy(src_ref=src, dst_ref=dst, send_sem=send_sems.at[a],
                                                recv_sem=recv_sems.at[a], device_id=(x, y, 1 - c), device_id_type=MESH)
        return c, copy

    def start(ins, got, send_sems, recv_sems):
        c, copy = copier(send_sems, recv_sems)
        for a in range(n):
            half = ins[a].shape[1] // 2
            for slab in range(N_CHIPS):
                for first, size in _row_chunks(half, D2D_CHUNKS):
                    copy(a, ins[a].at[slab, pl.ds((1 - c) * half + first, size), :],
                         got[a].at[slab, pl.ds(first, size), :]).start()

    def finish(ins, got, send_sems, recv_sems):
        c, copy = copier(send_sems, recv_sems)
        for a in range(n):
            half = ins[a].shape[1] // 2
            copy(a, ins[a].at[:, pl.ds((1 - c) * half, half), :], got[a]).wait()

    return _Carried(grads, [_sds((N_CHIPS, g.shape[1] // 2, g.shape[2]), g.dtype) for g in grads],
                    [pltpu.SemaphoreType.DMA((n,)), pltpu.SemaphoreType.DMA((n,))], start, finish)


def _chip_exchange(hsums):
    n = len(hsums)

    def copies(ins, outs, send_sems, recv_sems, local_sems, pieces):
        x, y, c, chips = _position()
        me = 2 * x + y
        cps = []
        for a in range(n):
            cps.append(pltpu.make_async_copy(ins[a].at[me], outs[a].at[me], local_sems.at[a]))
            rows = ins[a].shape[1]
            for k, (px, py) in enumerate(chips):
                for first, size in (_row_chunks(rows, ICI_CHUNKS) if pieces else [(0, rows)]):
                    cps.append(pltpu.make_async_remote_copy(
                        src_ref=ins[a].at[2 * px + py, pl.ds(first, size)], dst_ref=outs[a].at[me, pl.ds(first, size)],
                        send_sem=send_sems.at[a, k], recv_sem=recv_sems.at[a, k], device_id=(px, py, c),
                        device_id_type=MESH))
        return cps

    def start(*refs):
        for cp in copies(*refs, pieces=True):
            cp.start()

    def finish(*refs):
        for cp in copies(*refs, pieces=False):
            cp.wait()

    return _Carried(hsums, [_sds(h.shape, h.dtype) for h in hsums],
                    [pltpu.SemaphoreType.DMA((n, 3)), pltpu.SemaphoreType.DMA((n, 3)), pltpu.SemaphoreType.DMA((n,))],
                    start, finish)


def _pair_swap(reds):
    n = len(reds)

    def copier(send_sems, recv_sems):
        x, y, c, _ = _position()

        def copy(a, src, dst):
            return pltpu.make_async_remote_copy(src_ref=src, dst_ref=dst, send_sem=send_sems.at[a],
                                                recv_sem=recv_sems.at[a], device_id=(x, y, 1 - c), device_id_type=MESH)
        return copy

    def start(ins, outs, send_sems, recv_sems):
        copy = copier(send_sems, recv_sems)
        for a in range(n):
            for first, size in _row_chunks(ins[a].shape[0], 2 * D2D_CHUNKS):
                copy(a, ins[a].at[pl.ds(first, size), :], outs[a].at[pl.ds(first, size), :]).start()

    def finish(ins, outs, send_sems, recv_sems):
        copy = copier(send_sems, recv_sems)
        for a in range(n):
            copy(a, ins[a], outs[a]).wait()

    return _Carried(reds, [_sds(r.shape, r.dtype) for r in reds],
                    [pltpu.SemaphoreType.DMA((n,)), pltpu.SemaphoreType.DMA((n,))], start, finish)


def _device_gather_exchange(packed):
    def copies(ins, outs, send_sems, recv_sems, local_sem):
        (x_ref,), (all_ref,) = ins, outs
        x, y, c, chips = _position()
        me, sibling = (x, y, c), (x, y, 1 - c)

        def slab(px, py, pc):
            return all_ref.at[4 * px + 2 * py + pc]

        def copy(k, block, to, src=None):
            return pltpu.make_async_remote_copy(
                src_ref=slab(*block) if src is None else src, dst_ref=slab(*block), send_sem=send_sems.at[k],
                recv_sem=recv_sems.at[k], device_id=to, device_id_type=MESH)

        mine = pltpu.make_async_copy(x_ref, slab(*me), local_sem)
        first = [copy(0, me, sibling, src=x_ref)]
        first += [copy(1 + j, me, (*chip, c), src=x_ref) for j, chip in enumerate(chips)]
        passed = [copy(4 + j, (*chip, c), sibling) for j, chip in enumerate(chips)]
        return c, chips, me, sibling, copy, mine, first, passed

    def start(ins, outs, send_sems, recv_sems, local_sem):
        _, _, _, _, _, mine, first, _ = copies(ins, outs, send_sems, recv_sems, local_sem)
        mine.start()
        for cp in first:
            cp.start()

    def finish(ins, outs, send_sems, recv_sems, local_sem):
        c, chips, me, sibling, copy, mine, first, passed = copies(ins, outs, send_sems, recv_sems, local_sem)
        for j, chip in enumerate(chips):
            copy(1 + j, (*chip, c), me).wait_recv()
            passed[j].start()
        copy(0, sibling, me).wait_recv()
        for j, chip in enumerate(chips):
            copy(4 + j, (*chip, 1 - c), me).wait_recv()
        for cp in first + passed:
            cp.wait_send()
        mine.wait()

    return _Carried([packed], [_sds((N_DEV,) + packed.shape, F32)],
                    [pltpu.SemaphoreType.DMA((7,)), pltpu.SemaphoreType.DMA((7,)), pltpu.SemaphoreType.DMA],
                    start, finish)


def _ordered_sum(name, slabs):
    _, m_per, n_cols = slabs.shape

    def body(s_ref, o_ref):
        acc = s_ref[0]
        for d in range(1, N_DEV):
            acc = acc + s_ref[d]
        o_ref[...] = acc

    vmem = pl.BlockSpec(memory_space=pltpu.VMEM)
    return pl.pallas_call(body, name=name, out_shape=_sds((m_per, n_cols), F32), in_specs=[vmem], out_specs=vmem)(slabs)


def _pair_sum(core, own, got, tm):
    _, half, cols = got.shape
    nb = half // tm

    def body(c_ref, a_ref, b_ref, o_ref):
        o_ref[...] = (a_ref[...].astype(F32) + b_ref[...].astype(F32)).astype(BF16)

    return pl.pallas_call(
        body, name="grad_pair_sum", out_shape=_sds(got.shape, BF16),
        grid_spec=pltpu.PrefetchScalarGridSpec(
            num_scalar_prefetch=1, grid=(N_CHIPS, nb),
            in_specs=[pl.BlockSpec((None, tm, cols), lambda s, i, c_ref: (s, c_ref[0] * nb + i, 0)),
                      pl.BlockSpec((None, tm, cols), lambda s, i, c_ref: (s, i, 0))],
            out_specs=pl.BlockSpec((None, tm, cols), lambda s, i, c_ref: (s, i, 0))),
        compiler_params=_cparams(2),
    )(core, own, got)


def _chip_sum(slabs, tm):
    _, half, cols = slabs.shape

    def body(s_ref, o_ref):
        acc = s_ref[0].astype(F32)
        for k in range(1, N_CHIPS):
            acc = acc + s_ref[k].astype(F32)
        o_ref[...] = acc

    return pl.pallas_call(
        body, name="grad_chip_sum", out_shape=_sds((half, cols), F32), grid=(half // tm,),
        in_specs=[pl.BlockSpec((N_CHIPS, tm, cols), lambda i: (0, i, 0))],
        out_specs=pl.BlockSpec((tm, cols), lambda i: (i, 0)), compiler_params=_cparams(1),
    )(slabs)


def _adam_math(w, g, m, v):
    m2 = ADAM_B1 * m + (1.0 - ADAM_B1) * g
    v2 = ADAM_B2 * v + (1.0 - ADAM_B2) * (g * g)
    m_hat = m2 / (1.0 - ADAM_B1 ** ADAM_STEP)
    v_hat = v2 / (1.0 - ADAM_B2 ** ADAM_STEP)
    delta = -ADAM_LR * (m_hat / (jnp.sqrt(v_hat) + ADAM_EPS) + ADAM_WD * w)
    return delta, m2, v2


def _adamw_halves(name, items, tm, carried=None):
    rows, cols = items[0][0].shape
    nb = rows // 2 // tm
    n = len(items)

    def body(*refs):
        mine = (pl.program_id(0) // nb) == lax.axis_index("c")
        for k in range(n):
            w_ref, own_ref, oth_ref, m_ref, v_ref = refs[5 * k:5 * k + 5]
            g_ref, d_ref, m2_ref, v2_ref = refs[5 * n + 4 * k:5 * n + 4 * k + 4]
            g = jnp.where(mine, own_ref[...], oth_ref[...])
            d, m2, v2 = _adam_math(w_ref[...], g, m_ref[...], v_ref[...])
            g_ref[...] = g
            d_ref[...] = d
            m2_ref[...] = m2
            v2_ref[...] = v2

    full = pl.BlockSpec((tm, cols), lambda i: (i, 0))
    half = pl.BlockSpec((tm, cols), lambda i: (i % nb, 0))
    return _call_carrying(
        body, carried, name=name, grid=(rows // tm,), in_specs=[full, half, half, full, full] * n,
        out_specs=[full] * (4 * n), out_shape=tuple([_sds((rows, cols), F32)] * (4 * n)), scratch_shapes=[],
        operands=[a for item in items for a in item])


def _adamw(name, w, g, m, v, tm):
    def body(w_ref, g_ref, m_ref, v_ref, gout_ref, d_ref, m2_ref, v2_ref):
        gv = g_ref[...]
        d, m2, v2 = _adam_math(w_ref[...], gv, m_ref[...], v_ref[...])
        gout_ref[...] = gv
        d_ref[...] = d
        m2_ref[...] = m2
        v2_ref[...] = v2

    return _rows_call(name, body, tm, [w, g, m, v], [], [_sds(w.shape, F32)] * 4)


_SMALL_NAMES = ("norm_mix_pre", "gm_ln_w", "gm_ln_b", "gm_w_s", "gm_b_s", "conv_w", "conv_b", "dt_bias", "a_log",
                "d_skip", "ssm_norm_w", "norm_mix_post", "norm_ffn_pre", "norm_ffn_post")
_PACK_COLS = 1024


def _pack(parts, names=_SMALL_NAMES, tail=None):
    pieces = [parts[n].reshape(-1) for n in names]
    flat = jnp.concatenate(pieces if tail is None else pieces + [tail])
    rows = -(-flat.shape[0] // (8 * _PACK_COLS)) * 8
    flat = jnp.pad(flat, (0, rows * _PACK_COLS - flat.shape[0]))
    return flat.reshape(rows, _PACK_COLS)


def _unpack(packed, shapes, names=_SMALL_NAMES):
    flat = packed.reshape(-1)
    out, off = {}, 0
    for n in names:
        size = 1
        for s in shapes[n]:
            size *= s
        out[n] = flat[off:off + size].reshape(shapes[n])
        off += size
    return out


def kernel(x, norm_mix_pre, w_in, gm_ln_w, gm_ln_b, gm_w_s, gm_b_s, conv_w, conv_b, dt_bias, a_log, d_skip, ssm_norm_w, w_out, norm_mix_post, norm_ffn_pre, w_up, w_down, norm_ffn_post, loss_target, m_norm_mix_pre, m_w_in, m_gm_ln_w, m_gm_ln_b, m_gm_w_s, m_gm_b_s, m_conv_w, m_conv_b, m_dt_bias, m_a_log, m_d_skip, m_ssm_norm_w, m_w_out, m_norm_mix_post, m_norm_ffn_pre, m_w_up, m_w_down, m_norm_ffn_post, v_norm_mix_pre, v_w_in, v_gm_ln_w, v_gm_ln_b, v_gm_w_s, v_gm_b_s, v_conv_w, v_conv_b, v_dt_bias, v_a_log, v_d_skip, v_ssm_norm_w, v_w_out, v_norm_mix_post, v_norm_ffn_pre, v_w_up, v_w_down, v_norm_ffn_post):
    params = dict(norm_mix_pre=norm_mix_pre, w_in=w_in, gm_ln_w=gm_ln_w, gm_ln_b=gm_ln_b, gm_w_s=gm_w_s, gm_b_s=gm_b_s,
                  conv_w=conv_w, conv_b=conv_b, dt_bias=dt_bias, a_log=a_log, d_skip=d_skip, ssm_norm_w=ssm_norm_w,
                  w_out=w_out, norm_mix_post=norm_mix_post, norm_ffn_pre=norm_ffn_pre, w_up=w_up, w_down=w_down,
                  norm_ffn_post=norm_ffn_post)
    mom1 = dict(norm_mix_pre=m_norm_mix_pre, w_in=m_w_in, gm_ln_w=m_gm_ln_w, gm_ln_b=m_gm_ln_b, gm_w_s=m_gm_w_s,
                gm_b_s=m_gm_b_s, conv_w=m_conv_w, conv_b=m_conv_b, dt_bias=m_dt_bias, a_log=m_a_log, d_skip=m_d_skip,
                ssm_norm_w=m_ssm_norm_w, w_out=m_w_out, norm_mix_post=m_norm_mix_post, norm_ffn_pre=m_norm_ffn_pre,
                w_up=m_w_up, w_down=m_w_down, norm_ffn_post=m_norm_ffn_post)
    mom2 = dict(norm_mix_pre=v_norm_mix_pre, w_in=v_w_in, gm_ln_w=v_gm_ln_w, gm_ln_b=v_gm_ln_b, gm_w_s=v_gm_w_s,
                gm_b_s=v_gm_b_s, conv_w=v_conv_w, conv_b=v_conv_b, dt_bias=v_dt_bias, a_log=v_a_log, d_skip=v_d_skip,
                ssm_norm_w=v_ssm_norm_w, w_out=v_w_out, norm_mix_post=v_norm_mix_post, norm_ffn_pre=v_norm_ffn_pre,
                w_up=v_w_up, w_down=v_w_down, norm_ffn_post=v_norm_ffn_post)
    names = list(params)
    big = ("w_in", "w_out", "w_up", "w_down")
    chip = 2 * lax.axis_index("x") + lax.axis_index("y")

    shards = {n: params[n][0].astype(BF16) for n in big}
    conv_shard = jnp.pad(conv_w[0], ((0, 16 - CONV_K), (0, 0)))
    g_in4, g_conv4 = _run_exchange("allgather_w_in", _allgather_exchange([shards["w_in"], conv_shard]))
    conv_full = jnp.transpose(g_conv4[:, :CONV_K, :], (1, 0, 2)).reshape(CONV_K, CONV_CH)

    small = {n: params[n][0] if params[n].ndim >= 3 else params[n] for n in _SMALL_NAMES if n != "conv_w"}
    core = lax.axis_index("c").astype(jnp.int32).reshape(1)
    adam_args = {n: (params[n][0], mom1[n][0], mom2[n][0]) for n in big}
    loss, grad_x, big_out, small_sum = _forward_backward(
        x, loss_target, g_in4, conv_full, small, shards["w_out"], shards["w_up"], shards["w_down"], core, adam_args)
    grads, delta, new_m, new_v = {}, {}, {}, {}
    for n in big:
        grads[n], delta[n], new_m[n], new_v[n] = [a[None] for a in big_out[n]]

    small_sum["conv_w"] = lax.dynamic_slice_in_dim(small_sum["conv_w"], chip * (CONV_CH // N_CHIPS), CONV_CH // N_CHIPS, axis=1)

    local_shapes = {n: params[n].shape[1:] if params[n].ndim >= 3 else params[n].shape for n in _SMALL_NAMES}
    flat = lambda tree: {n: tree[n].reshape(local_shapes[n]) for n in _SMALL_NAMES}
    packed = [_pack(flat(t)) for t in (params, small_sum, mom1, mom2)]
    _, d_p, m_p, v_p = _adamw("adamw_small", *packed, packed[0].shape[0])
    for src, dst in ((d_p, delta), (m_p, new_m), (v_p, new_v)):
        for n, val in _unpack(src, local_shapes).items():
            dst[n] = val.reshape(params[n].shape)
    for n in _SMALL_NAMES:
        grads[n] = small_sum[n].reshape(params[n].shape)

    out = [loss, grad_x]
    for tree in (grads, delta, new_m, new_v):
        out += [tree[n] for n in names]
    return tuple(out)
```

```python
import functools

import jax
import jax.numpy as jnp
from jax import lax
from jax.experimental import pallas as pl
from jax.experimental.pallas import tpu as pltpu

F32 = jnp.float32
BF16 = jnp.bfloat16
HI = lax.Precision.HIGHEST
MESH = pl.DeviceIdType.MESH

EPS = 1e-6
D_MODEL = 1024
GM_WIDTH = 512
SSM_WIDTH = 512
N_HEADS = 8
HEAD_DIM = 64
CHUNK = 128
SSM_GROUPS = 2
GROUP_W = SSM_WIDTH // SSM_GROUPS
SSM_STATE = 128
CONV_K = 4
CONV_CH = 1024
D_FF = 4096
IN_COLS = 2568
DT_PAD = 128
N_CHIPS = 4
N_DEV = 8

ADAM_LR = 0.001
ADAM_B1 = 0.9
ADAM_B2 = 0.999
ADAM_EPS = 1e-08
ADAM_WD = 0.01
ADAM_STEP = 10

VMEM_LIMIT_BYTES = 56 * 1024 * 1024
FF_TILE = 512
DW_TOKENS_PER_STEP = 2048


def _cparams(n_axes):
    return pltpu.CompilerParams(dimension_semantics=("arbitrary",) * n_axes, vmem_limit_bytes=VMEM_LIMIT_BYTES)


def _dot(a, b):
    return jnp.dot(a.astype(BF16), b.astype(BF16), preferred_element_type=F32)


def _dot_nt(a, b):
    return lax.dot_general(a.astype(BF16), b.astype(BF16), (((1,), (1,)), ((), ())), preferred_element_type=F32)


def _dot_tn(a, b):
    return lax.dot_general(a.astype(BF16), b.astype(BF16), (((0,), (0,)), ((), ())), preferred_element_type=F32)


def _sigmoid(x):
    return 1.0 / (1.0 + jnp.exp(-x))


_GELU_C = 0.7978845608028654
_GELU_A = 0.044715


def _gelu(x):
    t = jnp.tanh(_GELU_C * (x + _GELU_A * (x * x * x)))
    return 0.5 * x * (1.0 + t), t


def _gelu_grad(x, t):
    return 0.5 * (1.0 + t) + 0.5 * x * (1.0 - t * t) * (_GELU_C * (1.0 + 3.0 * _GELU_A * x * x))


def _rms_fwd(x, w):
    r = lax.rsqrt(jnp.mean(x * x, axis=-1, keepdims=True) + EPS)
    return x * r * w, r


def _rms_bwd(x, r, w, dy):
    g = dy * w
    dx = r * g - x * (r * r * r) * jnp.mean(g * x, axis=-1, keepdims=True)
    dw = jnp.sum(dy * x * r, axis=0, keepdims=True)
    return dx, dw


class _Carried:
    def __init__(self, ins, out_shapes, sems, start, finish):
        self.ins, self.out_shapes, self.sems = list(ins), list(out_shapes), list(sems)
        self.start, self.finish = start, finish


def _both(first, second):
    n_i, n_o, n_s = len(first.ins), len(first.out_shapes), len(first.sems)

    def split(ins, outs, sems):
        return (ins[:n_i], outs[:n_o], sems[:n_s]), (ins[n_i:], outs[n_o:], sems[n_s:])

    def start(ins, outs, *sems):
        (i1, o1, s1), (i2, o2, s2) = split(ins, outs, sems)
        first.start(i1, o1, *s1)
        second.start(i2, o2, *s2)

    def finish(ins, outs, *sems):
        (i1, o1, s1), (i2, o2, s2) = split(ins, outs, sems)
        first.finish(i1, o1, *s1)
        second.finish(i2, o2, *s2)

    return _Carried(first.ins + second.ins, first.out_shapes + second.out_shapes, first.sems + second.sems, start, finish)


def _split_carried(refs, n_in, n_out, n_scratch, carried):
    n_ci, n_co, n_cs = len(carried.ins), len(carried.out_shapes), len(carried.sems)
    ins, rest = refs[:n_in], refs[n_in:]
    c_ins, rest = rest[:n_ci], rest[n_ci:]
    outs, rest = rest[:n_out], rest[n_out:]
    c_outs, rest = rest[:n_co], rest[n_co:]
    scr, c_sems = rest[:n_scratch], rest[n_scratch:]
    assert len(c_sems) == n_cs
    return tuple(ins) + tuple(outs) + tuple(scr), c_ins, c_outs, c_sems


def _rows_call(name, body, tm, row_ins, const_ins, row_outs, acc_outs=(), scratch=(), carried=None):
    n_rows = row_ins[0].shape[0]
    assert n_rows % tm == 0
    n_steps = n_rows // tm
    n_in = len(row_ins) + len(const_ins)
    n_ro = len(row_outs)
    n_acc = len(acc_outs)

    def kern(*refs):
        accs = refs[n_in + n_ro:n_in + n_ro + n_acc]

        @pl.when(pl.program_id(0) == 0)
        def _():
            for a in accs:
                a[...] = jnp.zeros_like(a)

        body(*refs)

    def whole(shape):
        nd = len(shape)
        return pl.BlockSpec(tuple(shape), lambda i: (0,) * nd)

    in_specs = [pl.BlockSpec((tm, a.shape[1]), lambda i: (i, 0)) for a in row_ins]
    in_specs += [whole(a.shape) for a in const_ins]
    out_specs = [pl.BlockSpec((tm, s.shape[1]), lambda i: (i, 0)) for s in row_outs]
    out_specs += [whole(s.shape) for s in acc_outs]
    return _call_carrying(
        kern, carried, name=name, grid=(n_steps,), in_specs=in_specs, out_specs=out_specs,
        out_shape=tuple(row_outs) + tuple(acc_outs), scratch_shapes=list(scratch), operands=list(row_ins) + list(const_ins))


def _call_carrying(body, carried, *, name, grid, in_specs, out_specs, out_shape, scratch_shapes, operands):
    n_in, n_out, n_scratch = len(in_specs), len(out_specs), len(scratch_shapes)
    kern = body
    if carried is not None:
        def kern(*refs):
            plain, c_ins, c_outs, c_sems = _split_carried(refs, n_in, n_out, n_scratch, carried)
            first, last = True, True
            for d, size in enumerate(grid):
                first = jnp.logical_and(first, pl.program_id(d) == 0)
                last = jnp.logical_and(last, pl.program_id(d) == size - 1)

            @pl.when(first)
            def _():
                carried.start(c_ins, c_outs, *c_sems)

            body(*plain)

            @pl.when(last)
            def _():
                carried.finish(c_ins, c_outs, *c_sems)

        in_specs = list(in_specs) + [_HBM] * len(carried.ins)
        out_specs = list(out_specs) + [_HBM] * len(carried.out_shapes)
        out_shape = tuple(out_shape) + tuple(carried.out_shapes)
        operands = list(operands) + carried.ins
        scratch_shapes = list(scratch_shapes) + carried.sems
    return pl.pallas_call(
        kern, name=name, grid=grid, in_specs=in_specs, out_specs=out_specs, out_shape=out_shape,
        scratch_shapes=scratch_shapes, compiler_params=_cparams(len(grid)),
    )(*operands)


def _sds(shape, dtype):
    return jax.ShapeDtypeStruct(tuple(shape), dtype)


def _matmul_tn(name, a, b, tm, tn, tk, stacked=False, carried=None):
    k_dim, m_dim = a.shape
    n_dim = b.shape[1]
    assert m_dim % tm == 0 and n_dim % tn == 0 and k_dim % tk == 0
    nk = k_dim // tk

    def kern(a_ref, b_ref, o_ref, acc_ref):
        k = pl.program_id(2)
        prod = _dot_tn(a_ref[...], b_ref[...])

        @pl.when(k == 0)
        def _():
            acc_ref[...] = prod

        @pl.when(k > 0)
        def _():
            acc_ref[...] += prod

        @pl.when(k == nk - 1)
        def _():
            o_ref[...] = acc_ref[...].astype(o_ref.dtype)

    if stacked:
        assert tm == m_dim
        out_shape = _sds((n_dim // tn, m_dim, tn), BF16)
        out_spec = pl.BlockSpec((None, tm, tn), lambda i, j, k: (j, i, 0))
    else:
        out_shape = _sds((m_dim, n_dim), BF16)
        out_spec = pl.BlockSpec((tm, tn), lambda i, j, k: (i, j))
    outs = _call_carrying(
        kern, carried, name=name, grid=(m_dim // tm, n_dim // tn, nk),
        in_specs=[pl.BlockSpec((tk, tm), lambda i, j, k: (k, i)), pl.BlockSpec((tk, tn), lambda i, j, k: (k, j))],
        out_specs=[out_spec], out_shape=(out_shape,), scratch_shapes=[pltpu.VMEM((tm, tn), F32)], operands=[a, b])
    return outs[0] if carried is None else outs


def _inproj_fwd(x, nw, w_uv, w_xbc, w_z, w_dt, tm=256, carried=None):
    n_tok = x.shape[0]

    def body(x_ref, nw_ref, wuv_ref, wxbc_ref, wz_ref, wdt_ref, puv_ref, pxbc_ref, pz_ref, pdt_ref):
        h, _ = _rms_fwd(x_ref[...], nw_ref[...])
        h = h.astype(BF16)
        puv_ref[...] = jnp.dot(h, wuv_ref[...], preferred_element_type=F32)
        pxbc_ref[...] = jnp.dot(h, wxbc_ref[...], preferred_element_type=F32)
        pz_ref[...] = jnp.dot(h, wz_ref[...], preferred_element_type=F32)
        pdt_ref[...] = jnp.dot(h, wdt_ref[...], preferred_element_type=F32)

    return _rows_call(
        "inproj_fwd", body, tm, [x], [nw, w_uv, w_xbc, w_z, w_dt],
        [_sds((n_tok, 2 * GM_WIDTH), F32), _sds((n_tok, CONV_CH), F32), _sds((n_tok, SSM_WIDTH), F32),
         _sds((n_tok, DT_PAD), F32)], carried=carried)


def _head_lane_mask(width, head):
    lane = lax.broadcasted_iota(jnp.int32, (1, width), 1)
    return (lane // HEAD_DIM) == head


def _split_terms(x, terms):
    parts = []
    for _ in range(terms):
        p = x.astype(BF16)
        parts.append(p)
        x = x - p.astype(F32)
    return parts


def _seg_dots(vals, ind, terms=2):
    m = vals[0].shape[0]
    parts = []
    for v in vals:
        parts += _split_terms(v, terms)
    red = jnp.dot(jnp.concatenate(parts, axis=0), ind, preferred_element_type=F32)
    outs = []
    for i in range(len(vals)):
        acc = red[i * terms * m:(i * terms + 1) * m]
        for t in range(1, terms):
            acc = acc + red[(i * terms + t) * m:(i * terms + t + 1) * m]
        outs.append(acc)
    return outs


def _tri_dot(mask, x, terms=3):
    n = x.shape[1]
    red = jnp.dot(mask.astype(BF16), jnp.concatenate(_split_terms(x, terms), axis=1), preferred_element_type=F32)
    acc = red[:, :n]
    for t in range(1, terms):
        acc = acc + red[:, t * n:(t + 1) * n]
    return acc


def _gmlp_common(puv, lnw, lnb, e_bf, et_bf):
    u = puv[:, :GM_WIDTH]
    v = puv[:, GM_WIDTH:]
    gu, tu = _gelu(u)
    gv, tv = _gelu(v)
    (s1,) = _seg_dots([gv], et_bf)
    (mu,) = _seg_dots([s1 * (1.0 / HEAD_DIM)], e_bf)
    xc = gv - mu
    (s2,) = _seg_dots([xc * xc], et_bf)
    (rstd,) = _seg_dots([lax.rsqrt(s2 * (1.0 / HEAD_DIM) + EPS)], e_bf)
    xhat = xc * rstd
    vn = xhat * lnw + lnb
    return u, v, gu, tu, tv, rstd, xhat, vn


def _tril_mask():
    r = lax.broadcasted_iota(jnp.int32, (CHUNK, CHUNK), 0)
    c = lax.broadcasted_iota(jnp.int32, (CHUNK, CHUNK), 1)
    return r >= c


def _head_blocks(v):
    return jnp.concatenate([jnp.where(_head_lane_mask(GM_WIDTH, h), v, jnp.zeros_like(v)) for h in range(N_HEADS)], axis=0)


def _causal_w_cat(w_cat):
    t = lax.broadcasted_iota(jnp.int32, (CHUNK, N_HEADS * CHUNK), 0)
    s = lax.broadcasted_iota(jnp.int32, (CHUNK, N_HEADS * CHUNK), 1) % CHUNK
    return jnp.where(t >= s, w_cat, 0.0).astype(BF16)


def _gmlp_chunk_fwd(puv, lnw, lnb, e_bf, et_bf, wm, bmap):
    _, _, gu, _, _, _, _, vn = _gmlp_common(puv, lnw, lnb, e_bf, et_bf)
    mixed = jnp.dot(wm, _head_blocks(vn.astype(BF16)), preferred_element_type=F32) + bmap
    return (gu * mixed).astype(BF16)


SUBLANES = 8


def _shift_down(x, tail, s):
    main = pltpu.roll(x, s, 0)
    row = lax.broadcasted_iota(jnp.int32, (SUBLANES, 1), 0)
    head = jnp.where(row < s, pltpu.roll(tail, s, 0), main[:SUBLANES])
    return jnp.concatenate([head, main[SUBLANES:]], axis=0)


def _shift_up(x, head_next, s):
    n = x.shape[0]
    main = pltpu.roll(x, n - s, 0)
    row = lax.broadcasted_iota(jnp.int32, (SUBLANES, 1), 0)
    last = jnp.where(row >= SUBLANES - s, pltpu.roll(head_next, SUBLANES - s, 0), main[n - SUBLANES:])
    return jnp.concatenate([main[:n - SUBLANES], last], axis=0)


def _ssd_pre(xr, tail, cw_ref, cb, pdt, dtb, alog, emap):
    rowi = lax.broadcasted_iota(jnp.int32, (CHUNK, 1), 0)
    shifted = [_shift_down(xr, tail, 3), _shift_down(xr, tail, 2), _shift_down(xr, tail, 1), xr]
    xc = cb
    for k in range(CONV_K):
        xc = xc + cw_ref[k] * shifted[k]
    sg = _sigmoid(xc)
    xa = xc * sg
    pre = pdt + dtb
    dt = jnp.maximum(pre, 0.0) + jnp.log(1.0 + jnp.exp(-jnp.abs(pre)))
    a_neg = -jnp.exp(alog)
    a_cs = _tri_dot(_tril_mask(), dt * a_neg)
    acs_map, dt_map = _seg_dots([a_cs, dt], emap, terms=3)
    return dict(shifted=shifted, xc=xc, sg=sg, xa=xa, pre=pre, dt=dt, a_neg=a_neg, a_cs=a_cs,
                acs_map=acs_map, dt_map=dt_map, rowi=rowi)


def _ssd_maps(p):
    last = p["rowi"] == CHUNK - 1
    aq_map = jnp.sum(jnp.where(last, p["acs_map"], 0.0), axis=0, keepdims=True)
    e_exp = jnp.exp(p["acs_map"])
    dte = jnp.exp(aq_map - p["acs_map"])
    cd = jnp.exp(aq_map)
    return last, e_exp, dte, cd


def _head_decay(a_cs, a_cs_t, head, tri):
    lane = lax.broadcasted_iota(jnp.int32, (1, DT_PAD), 1)
    sub = lax.broadcasted_iota(jnp.int32, (DT_PAD, 1), 0)
    col = jnp.sum(jnp.where(lane == head, a_cs, 0.0), axis=1, keepdims=True)
    row = jnp.sum(jnp.where(sub == head, a_cs_t, 0.0), axis=0, keepdims=True)
    return jnp.exp(jnp.where(tri, col - row, -1e30))


def _gate_fwd(y, z, nw):
    sz = _sigmoid(z)
    zg = z * sz
    yg = y * zg
    outs, rs = [], []
    for g in range(SSM_GROUPS):
        gs = slice(g * GROUP_W, (g + 1) * GROUP_W)
        o, r = _rms_fwd(yg[:, gs], nw[:, gs])
        outs.append(o)
        rs.append(r)
    return sz, zg, yg, outs, rs


def _ssd_const_specs():
    def whole(shape):
        nd = len(shape)
        return pl.BlockSpec(tuple(shape), lambda c: (0,) * nd)
    return [whole((CONV_K, 1, CONV_CH)), whole((1, CONV_CH)), whole((1, DT_PAD)), whole((1, DT_PAD)),
            whole((1, SSM_WIDTH)), whole((1, SSM_WIDTH)), whole((DT_PAD, SSM_WIDTH)), whole((SSM_WIDTH, DT_PAD))]


def _mixer_fwd(p_uv, p_xbc, p_z, p_dt, x, lnw, lnb, w_cat, bmap, w_out, nw_post, nw_pre2, conv_w, conv_b, dt_bias, a_log,
               dskip_map, norm_w, e_bf, et_bf, n_seq, carried=None):
    n_tok = p_xbc.shape[0]
    nc = n_tok // n_seq // CHUNK

    def body(puv3, xr3, z3, pdt3, x3, lnw_ref, lnb_ref, wcat_ref, bmap_ref, wo_ref, nwa_ref, nwb_ref,
             cw_ref, cb_ref, dtb_ref, alog_ref, dsk_ref, nw_ref, e_ref, et_ref,
             mix3, yssd3, sprev3, o3, x13, h23, wm_scr, prev3_scr, s3_scr):
        @pl.when(pl.program_id(0) == 0)
        def _():
            wm_scr[...] = _causal_w_cat(wcat_ref[...])
            prev3_scr[...] = jnp.zeros_like(prev3_scr)
            s3_scr[...] = jnp.zeros_like(s3_scr)

        for b in range(n_seq):
            one_sequence(puv3.at[b], xr3.at[b], z3.at[b], pdt3.at[b], lnw_ref, lnb_ref, bmap_ref,
                         cw_ref, cb_ref, dtb_ref, alog_ref, dsk_ref, nw_ref, e_ref, et_ref,
                         mix3.at[b], yssd3.at[b], sprev3.at[b], wm_scr, prev3_scr.at[b], s3_scr.at[b])
            o = jnp.dot(mix3[b], wo_ref[...], preferred_element_type=F32)
            on, _ = _rms_fwd(o, nwa_ref[...])
            x1 = x3[b] + on
            h2, _ = _rms_fwd(x1, nwb_ref[...])
            o3[b] = o
            x13[b] = x1
            h23[b] = h2.astype(BF16)

    def one_sequence(puv_ref, xr_ref, z_ref, pdt_ref, lnw_ref, lnb_ref, bmap_ref,
                     cw_ref, cb_ref, dtb_ref, alog_ref, dsk_ref, nw_ref, e_ref, et_ref,
                     mix_ref, yssd_ref, sprev_ref, wm_scr, prev_scr, s_scr):
        mix_ref[:, :GM_WIDTH] = _gmlp_chunk_fwd(puv_ref[...], lnw_ref[...], lnb_ref[...], e_ref[...], et_ref[...], wm_scr[...],
                                      bmap_ref[...])
        xr = xr_ref[...]
        p = _ssd_pre(xr, prev_scr[...], cw_ref, cb_ref[...], pdt_ref[...], dtb_ref[...], alog_ref[...], e_ref[...])
        _, e_exp, dte, cd = _ssd_maps(p)
        xs = p["xa"][:, :SSM_WIDTH]
        xd = xs * p["dt_map"]
        a_cs_t = p["a_cs"].T
        tri = _tril_mask()
        s_old = s_scr[...]
        sprev_ref[...] = s_old
        for g in range(SSM_GROUPS):
            gs = slice(g * GROUP_W, (g + 1) * GROUP_W)
            bm = p["xa"][:, SSM_WIDTH + g * SSM_STATE: SSM_WIDTH + (g + 1) * SSM_STATE].astype(BF16)
            cm = p["xa"][:, SSM_WIDTH + (SSM_GROUPS + g) * SSM_STATE: SSM_WIDTH + (SSM_GROUPS + g + 1) * SSM_STATE].astype(BF16)
            cb_mat = _dot_nt(cm, bm)
            xdg = xd[:, gs].astype(BF16)
            y_g = _dot(cm, s_old[:, gs]) * e_exp[:, gs] + dsk_ref[:, gs] * xs[:, gs]
            for r in range(SSM_GROUPS * 2):
                dm = _head_decay(p["a_cs"], a_cs_t, g * 4 + r, tri)
                full = jnp.dot((cb_mat * dm).astype(BF16), xdg, preferred_element_type=F32)
                y_g = y_g + jnp.where(_head_lane_mask(GROUP_W, r), full, 0.0)
            yssd_ref[:, gs] = y_g
            s_scr[:, gs] = cd[:, gs] * s_old[:, gs] + _dot_tn(bm, xd[:, gs] * dte[:, gs])
        _, _, _, outs, _ = _gate_fwd(yssd_ref[...], z_ref[...], nw_ref[...])
        for g in range(SSM_GROUPS):
            mix_ref[:, GM_WIDTH + g * GROUP_W:GM_WIDTH + (g + 1) * GROUP_W] = outs[g].astype(BF16)
        prev_scr[...] = xr[CHUNK - SUBLANES:, :]

    seq_len = n_tok // n_seq

    def rows(width):
        return pl.BlockSpec((n_seq, CHUNK, width), lambda c: (0, c, 0))

    def whole(shape):
        nd = len(shape)
        return pl.BlockSpec(tuple(shape), lambda c: (0,) * nd)

    def by_seq(a):
        return a.reshape(n_seq, seq_len, a.shape[-1])

    outs = _call_carrying(
        body, carried, name="mixer_fwd", grid=(nc,),
        in_specs=[rows(2 * GM_WIDTH), rows(CONV_CH), rows(SSM_WIDTH), rows(DT_PAD), rows(D_MODEL), whole(lnw.shape),
                  whole(lnb.shape), whole(w_cat.shape), whole(bmap.shape), whole(w_out.shape), whole(nw_post.shape),
                  whole(nw_pre2.shape)] + _ssd_const_specs(),
        out_specs=[rows(D_MODEL), rows(SSM_WIDTH), rows(SSM_WIDTH), rows(D_MODEL), rows(D_MODEL), rows(D_MODEL)],
        out_shape=(_sds((n_seq, seq_len, D_MODEL), BF16),
                   _sds((n_seq, seq_len, SSM_WIDTH), F32), _sds((n_seq, seq_len, SSM_WIDTH), F32),
                   _sds((n_seq, seq_len, D_MODEL), F32), _sds((n_seq, seq_len, D_MODEL), F32),
                   _sds((n_seq, seq_len, D_MODEL), BF16)),
        scratch_shapes=[pltpu.VMEM((CHUNK, N_HEADS * CHUNK), BF16), pltpu.VMEM((n_seq, SUBLANES, CONV_CH), F32),
                        pltpu.VMEM((n_seq, SSM_STATE, SSM_WIDTH), F32)],
        operands=[by_seq(p_uv), by_seq(p_xbc), by_seq(p_z), by_seq(p_dt), by_seq(x), lnw, lnb, w_cat, bmap, w_out, nw_post,
                  nw_pre2, conv_w, conv_b, dt_bias, a_log, dskip_map, norm_w, e_bf, et_bf])
    return tuple(o.reshape(n_tok, o.shape[-1]) for o in outs[:6]) + tuple(outs[6:])


def _up_cols(wup_ref, j):
    per = (D_FF // N_CHIPS) // FF_TILE
    return wup_ref[j // per, :, (j % per) * FF_TILE:(j % per + 1) * FF_TILE]


def _down_rows(wda_ref, wdb_ref, j):
    assert 2 * FF_TILE == D_FF // N_CHIPS
    return (wda_ref if j % 2 == 0 else wdb_ref)[j // 2]


def _skewed_rows_call(name, main, tail, tm, lead_ins, lag_ins, const_ins, lead_outs, lag_outs, acc_outs, carry,
                      streamed, tile_copies, n_copies):
    n_rows = lead_ins[0].shape[0]
    assert n_rows % tm == 0
    n = n_rows // tm
    counts = [len(lead_ins), len(lag_ins), len(const_ins), len(streamed), len(lead_outs), len(lag_outs), len(acc_outs),
              1, len(streamed)]

    def kern(*refs):
        groups, pos = [], 0
        for cnt in counts:
            groups.append(refs[pos:pos + cnt])
            pos += cnt
        lead_i, lag_i, consts, w_hbm, lead_o, lag_o, accs, (carry_scr,), w_vmem = groups
        sems = refs[pos]
        i = pl.program_id(0)
        pieces, k = [], 0
        for piece in tile_copies(w_hbm, w_vmem):
            pieces.append([pltpu.make_async_copy(src, dst, sems.at[k + q]) for q, (src, dst) in enumerate(piece)])
            k += len(piece)

        def ready(j):
            for cp in pieces[j]:
                cp.wait()

        @pl.when(i == 0)
        def _():
            for piece in pieces:
                for cp in piece:
                    cp.start()
            for a in accs:
                a[...] = jnp.zeros_like(a)
            carry_scr[...] = main(lead_i, consts, lead_o, w_vmem, ready)

        @pl.when(jnp.logical_and(i > 0, i < n))
        def _():
            previous = carry_scr[...]
            carry_scr[...] = main(lead_i, consts, lead_o, w_vmem, lambda j: None)
            tail(previous, lag_i, consts, lag_o, accs)

        @pl.when(i == n)
        def _():
            tail(carry_scr[...], lag_i, consts, lag_o, accs)

    def lead(width):
        return pl.BlockSpec((tm, width), lambda i: (jnp.minimum(i, n - 1), 0))

    def lag(width):
        return pl.BlockSpec((tm, width), lambda i: (jnp.maximum(i - 1, 0), 0))

    def whole(shape):
        nd = len(shape)
        return pl.BlockSpec(tuple(shape), lambda i: (0,) * nd)

    return pl.pallas_call(
        kern, name=name, grid=(n + 1,),
        in_specs=([lead(a.shape[1]) for a in lead_ins] + [lag(a.shape[1]) for a in lag_ins]
                  + [whole(a.shape) for a in const_ins] + [_HBM] * len(streamed)),
        out_specs=[lead(s.shape[1]) for s in lead_outs] + [lag(s.shape[1]) for s in lag_outs] + [whole(s.shape) for s in acc_outs],
        out_shape=tuple(lead_outs) + tuple(lag_outs) + tuple(acc_outs),
        scratch_shapes=([pltpu.VMEM(carry, F32)] + [pltpu.VMEM(a.shape, a.dtype) for a in streamed]
                        + [pltpu.SemaphoreType.DMA((n_copies,))]),
        compiler_params=_cparams(1),
    )(*lead_ins, *lag_ins, *const_ins, *streamed)


def _mlp_weight_pieces(order):
    per = (D_FF // N_CHIPS) // FF_TILE

    def tile_copies(hbm, vmem):
        pieces = []
        for j in range(D_FF // FF_TILE):
            cols = (j // per, slice(None), pl.ds((j % per) * FF_TILE, FF_TILE))
            up = (hbm[0].at[cols], vmem[0].at[cols])
            down = (hbm[1 + j % 2].at[j // 2], vmem[1 + j % 2].at[j // 2])
            pieces.append([up, down] if order == "up_down" else [down, up])
        return pieces

    return tile_copies


def _mlp_fwd(h2, x1, tgt, w_up, w_down_a, w_down_b, nw, tm=512):
    n_tok = x1.shape[0]

    def main(lead_i, consts, lead_o, weights, ready):
        (h2_ref,), (f_ref,), (wup_ref, wda_ref, wdb_ref) = lead_i, lead_o, weights
        h2v = h2_ref[...]
        acc = jnp.zeros((tm, D_MODEL), F32)
        for j in range(D_FF // FF_TILE):
            cs = slice(j * FF_TILE, (j + 1) * FF_TILE)
            ready(j)
            u = jnp.dot(h2v, _up_cols(wup_ref, j), preferred_element_type=F32)
            f = jnp.square(jnp.maximum(u, 0.0)).astype(BF16)
            f_ref[:, cs] = f
            acc = acc + jnp.dot(f, _down_rows(wda_ref, wdb_ref, j), preferred_element_type=F32)
        return acc

    def tail(acc, lag_i, consts, lag_o, accs):
        (x1_ref, tgt_ref), (nw_ref,), (dd_ref, dy_ref), (loss_ref, dnw_ref) = lag_i, consts, lag_o, accs
        dn, r = _rms_fwd(acc, nw_ref[...])
        e = x1_ref[...] + dn - tgt_ref[...]
        loss_ref[...] += jnp.full(loss_ref.shape, (0.5 / D_MODEL) * jnp.sum(e * e), F32)
        dy = e * (1.0 / D_MODEL)
        dd, dnw = _rms_bwd(acc, r, nw_ref[...], dy)
        dy_ref[...] = dy
        dd_ref[...] = dd.astype(BF16)
        dnw_ref[...] += dnw

    return _skewed_rows_call(
        "mlp_fwd", main, tail, tm, [h2], [x1, tgt], [nw],
        [_sds((n_tok, D_FF), BF16)], [_sds((n_tok, D_MODEL), BF16), _sds((n_tok, D_MODEL), F32)],
        [_sds((8, 128), F32), _sds((1, D_MODEL), F32)], carry=(tm, D_MODEL),
        streamed=[w_up, w_down_a, w_down_b], tile_copies=_mlp_weight_pieces("up_down"), n_copies=2 * (D_FF // FF_TILE))


def _mlp_bwd(dd, f, x1, dy, w_down_a, w_down_b, w_up, nw, tm=256):
    n_tok = x1.shape[0]

    def main(lead_i, consts, lead_o, weights, ready):
        (dd_ref, f_ref), (dup_ref,), (wup_ref, wda_ref, wdb_ref) = lead_i, lead_o, weights
        ddv = dd_ref[...]
        acc = jnp.zeros((tm, D_MODEL), F32)
        for j in range(D_FF // FF_TILE):
            cs = slice(j * FF_TILE, (j + 1) * FF_TILE)
            ready(j)
            df = _dot_nt(ddv, _down_rows(wda_ref, wdb_ref, j))
            du = (df * (2.0 * jnp.sqrt(f_ref[:, cs].astype(F32)))).astype(BF16)
            dup_ref[:, cs] = du
            acc = acc + _dot_nt(du, _up_cols(wup_ref, j))
        return acc

    def tail(acc, lag_i, consts, lag_o, accs):
        (x1_ref, dy_ref), (nw_ref,), (dx1_ref,), (dnw_ref,) = lag_i, consts, lag_o, accs
        x1v = x1_ref[...]
        _, r = _rms_fwd(x1v, nw_ref[...])
        dx, dnw = _rms_bwd(x1v, r, nw_ref[...], acc)
        dx1_ref[...] = dy_ref[...] + dx
        dnw_ref[...] += dnw

    return _skewed_rows_call(
        "mlp_bwd", main, tail, tm, [dd, f], [x1, dy], [nw],
        [_sds((n_tok, D_FF), BF16)], [_sds((n_tok, D_MODEL), F32)], [_sds((1, D_MODEL), F32)], carry=(tm, D_MODEL),
        streamed=[w_up, w_down_a, w_down_b], tile_copies=_mlp_weight_pieces("down_up"), n_copies=2 * (D_FF // FF_TILE))


def _outproj_bwd(dx1, o, w_out, nw, tm=256, carried=None):
    n_tok = dx1.shape[0]

    def body(dx1_ref, o_ref, wo_ref, nw_ref, do_ref, dya_ref, dyb_ref, dnw_ref):
        ov = o_ref[...]
        _, r = _rms_fwd(ov, nw_ref[...])
        do, dnw = _rms_bwd(ov, r, nw_ref[...], dx1_ref[...])
        dob = do.astype(BF16)
        do_ref[...] = dob
        dya_ref[...] = _dot_nt(dob, wo_ref[:GM_WIDTH, :])
        dyb_ref[...] = _dot_nt(dob, wo_ref[GM_WIDTH:, :])
        dnw_ref[...] += dnw

    return _rows_call("outproj_bwd", body, tm, [dx1, o], [w_out, nw],
                      [_sds((n_tok, D_MODEL), BF16), _sds((n_tok, GM_WIDTH), F32), _sds((n_tok, SSM_WIDTH), F32)],
                      [_sds((1, D_MODEL), F32)], carried=carried)


def _gmlp_bwd(p_uv, dya, lnw, lnb, e_bf, et_bf, w_cat, w_stack, bmap, carried=None):
    n_tok = p_uv.shape[0]
    chunks_per_step = 2

    def body(puv_ref, dya_ref, lnw_ref, lnb_ref, e_ref, et_ref, wcat_ref, wstack_ref, bmap_ref,
             dpuv_ref, dws_ref, dbs_ref, dlnw_ref, dlnb_ref, wm_scr, wsm_scr):
        t_stk = lax.broadcasted_iota(jnp.int32, (N_HEADS * CHUNK, CHUNK), 0) % CHUNK
        s_stk = lax.broadcasted_iota(jnp.int32, (N_HEADS * CHUNK, CHUNK), 1)

        @pl.when(pl.program_id(0) == 0)
        def _():
            wm_scr[...] = _causal_w_cat(wcat_ref[...])
            wsm_scr[...] = jnp.where(t_stk >= s_stk, wstack_ref[...], 0.0).astype(BF16)

        lnw_v = lnw_ref[...]
        e_v, et_v = e_ref[...], et_ref[...]

        def one_chunk(rows):
            u, v, gu, tu, tv, rstd, xhat, vn = _gmlp_common(puv_ref[rows, :], lnw_v, lnb_ref[...], e_v, et_v)
            vnb = vn.astype(BF16)
            mixed = jnp.dot(wm_scr[...], _head_blocks(vnb), preferred_element_type=F32) + bmap_ref[...]
            dy = dya_ref[rows, :]
            du = dy * mixed * _gelu_grad(u, tu)
            dmixed = dy * gu
            (dbs,) = _seg_dots([dmixed], et_v)
            dblocks = _head_blocks(dmixed.astype(BF16))
            dvn = lax.dot_general(wsm_scr[...], dblocks, (((0,), (0,)), ((), ())), preferred_element_type=F32)
            dws = lax.dot_general(dblocks, vnb, (((1,), (1,)), ((), ())), preferred_element_type=F32)
            dxh = dvn * lnw_v
            m1, m2 = _seg_dots([dxh, dxh * xhat], et_v)
            m1, m2 = _seg_dots([m1 * (1.0 / HEAD_DIM), m2 * (1.0 / HEAD_DIM)], e_v)
            dgv = rstd * (dxh - m1 - xhat * m2)
            dv = dgv * _gelu_grad(v, tv)
            dpuv_ref[rows, :GM_WIDTH] = du.astype(BF16)
            dpuv_ref[rows, GM_WIDTH:] = dv.astype(BF16)
            return dbs, dws, jnp.sum(dvn * xhat, axis=0, keepdims=True), jnp.sum(dvn, axis=0, keepdims=True)

        parts = [one_chunk(slice(k * CHUNK, (k + 1) * CHUNK)) for k in range(chunks_per_step)]
        dbs, dws, dlnw, dlnb = [functools.reduce(lambda a, b: a + b, vals) for vals in zip(*parts)]
        dbs_ref[...] += dbs
        dws_ref[...] += jnp.where(t_stk >= s_stk, dws, 0.0)
        dlnw_ref[...] += dlnw
        dlnb_ref[...] += dlnb

    return _rows_call(
        "gmlp_bwd", body, chunks_per_step * CHUNK, [p_uv, dya], [lnw, lnb, e_bf, et_bf, w_cat, w_stack, bmap],
        [_sds((n_tok, 2 * GM_WIDTH), BF16)],
        [_sds((N_HEADS * CHUNK, CHUNK), F32), _sds((CHUNK, DT_PAD), F32), _sds((1, GM_WIDTH), F32),
         _sds((1, GM_WIDTH), F32)],
        scratch=[pltpu.VMEM((CHUNK, N_HEADS * CHUNK), BF16), pltpu.VMEM((N_HEADS * CHUNK, CHUNK), BF16)],
        carried=carried)


def _ssd_bwd(p_xbc, p_z, p_dt, yssd, sprev, dyb, conv_w, conv_b, dt_bias, a_log, dskip_map, norm_w, e_bf, et_bf, n_seq,
             carried=None):
    n_tok = p_xbc.shape[0]
    nc = n_tok // n_seq // CHUNK

    def body(xr3, xprev3, z3, pdt3, yssd3, sprev3, dyb3,
             cw_ref, cb_ref, dtb_ref, alog_ref, dsk_ref, nw_ref, e_ref, et_ref,
             dpxbc3, dpz3, dpdt3, dcw_ref, dcb_ref, ddtb_ref, dalog_ref, ddsk_ref, dnw_ref,
             ds3_scr, nxt3_scr, dxa3_scr):
        @pl.when(pl.program_id(0) == 0)
        def _():
            for a in (dcw_ref, dcb_ref, ddtb_ref, dalog_ref, ddsk_ref, dnw_ref, ds3_scr, nxt3_scr):
                a[...] = jnp.zeros_like(a)

        for b in range(n_seq):
            one_sequence(xr3.at[b], xprev3.at[b], z3.at[b], pdt3.at[b], yssd3.at[b], sprev3.at[b], dyb3.at[b],
                         cw_ref, cb_ref, dtb_ref, alog_ref, dsk_ref, nw_ref, e_ref, et_ref,
                         dpxbc3.at[b], dpz3.at[b], dpdt3.at[b], dcw_ref, dcb_ref, ddtb_ref, dalog_ref, ddsk_ref, dnw_ref,
                         ds3_scr.at[b], nxt3_scr.at[b], dxa3_scr.at[b])

    def one_sequence(xr_ref, xprev_ref, z_ref, pdt_ref, yssd_ref, sprev_ref, dyb_ref,
                     cw_ref, cb_ref, dtb_ref, alog_ref, dsk_ref, nw_ref, e_ref, et_ref,
                     dpxbc_ref, dpz_ref, dpdt_ref, dcw_ref, dcb_ref, ddtb_ref, dalog_ref, ddsk_ref, dnw_ref,
                     ds_scr, nxt_scr, dxa_scr):
        chunk = nc - 1 - pl.program_id(0)
        xr = xr_ref[...]
        prev = jnp.where(chunk == 0, 0.0, xprev_ref[...])
        et_v = et_ref[...]
        p = _ssd_pre(xr, prev, cw_ref, cb_ref[...], pdt_ref[...], dtb_ref[...], alog_ref[...], e_ref[...])
        last, e_exp, dte, cd = _ssd_maps(p)
        rowi = p["rowi"]
        xs = p["xa"][:, :SSM_WIDTH]
        xd = xs * p["dt_map"]
        a_cs_t = p["a_cs"].T
        tri = _tril_mask()
        dsk = dsk_ref[...]
        nw_v = nw_ref[...]

        yv = yssd_ref[...]
        zv = z_ref[...]
        sz, zg, yg, _, rs = _gate_fwd(yv, zv, nw_v)
        dout = dyb_ref[...]
        for g in range(SSM_GROUPS):
            gs = slice(g * GROUP_W, (g + 1) * GROUP_W)
            dyg_g, dnw_g = _rms_bwd(yg[:, gs], rs[g], nw_v[:, gs], dout[:, gs])
            dnw_ref[:, gs] += dnw_g
            dxa_scr[:, gs] = dyg_g
        dyg = dxa_scr[:, :SSM_WIDTH]
        d_y = dyg * zg
        dpz_ref[...] = (dyg * yv * (sz + zv * sz * (1.0 - sz))).astype(BF16)

        s_prev = sprev_ref[...]
        ds_next = ds_scr[...]
        lane_dt = lax.broadcasted_iota(jnp.int32, (1, DT_PAD), 1)
        da_cols = jnp.zeros((CHUNK, DT_PAD), F32)
        for g in range(SSM_GROUPS):
            gs = slice(g * GROUP_W, (g + 1) * GROUP_W)
            b_off = SSM_WIDTH + g * SSM_STATE
            c_off = SSM_WIDTH + (SSM_GROUPS + g) * SSM_STATE
            bm = p["xa"][:, b_off:b_off + SSM_STATE].astype(BF16)
            cm = p["xa"][:, c_off:c_off + SSM_STATE].astype(BF16)
            cb_mat = _dot_nt(cm, bm)
            d_yg = d_y[:, gs]
            d_ygb = d_yg.astype(BF16)
            xdg = xd[:, gs]
            xdgb = xdg.astype(BF16)
            ds_g = ds_next[:, gs]
            sp_g = s_prev[:, gs]
            bds = _dot(bm, ds_g)
            dcs = d_yg * e_exp[:, gs]
            d_c = _dot_nt(dcs, sp_g)
            ds_scr[:, gs] = cd[:, gs] * ds_g + _dot_tn(cm, dcs)
            d_b = _dot_nt(xdg * dte[:, gs], ds_g)
            dxd_g = bds * dte[:, gs]
            sum_dcb = jnp.zeros((CHUNK, CHUNK), F32)
            for r in range(SSM_GROUPS * 2):
                head = g * 4 + r
                mask = _head_lane_mask(GROUP_W, r)
                dm = _head_decay(p["a_cs"], a_cs_t, head, tri)
                m_mat = cb_mat * dm
                g_mat = _dot_nt(jnp.where(mask, d_yg, 0.0), xdgb)
                w_mat = g_mat * m_mat
                sum_dcb = sum_dcb + g_mat * dm
                dxd_g = dxd_g + jnp.where(mask, _dot_tn(m_mat, d_ygb), 0.0)
                da_h = jnp.sum(w_mat - w_mat.T, axis=1, keepdims=True)
                da_cols = da_cols + jnp.where(lane_dt == head, da_h, 0.0)
            d_c = d_c + _dot(sum_dcb, bm)
            d_b = d_b + _dot_tn(sum_dcb, cm)
            dxa_scr[:, b_off:b_off + SSM_STATE] = d_b
            dxa_scr[:, c_off:c_off + SSM_STATE] = d_c
            y_off_g = _dot(cm, sp_g) * e_exp[:, gs]
            t3 = bds * xdg * dte[:, gs]
            tail = jnp.sum(t3, axis=0, keepdims=True) + jnp.sum(ds_g * sp_g, axis=0, keepdims=True) * cd[:, gs]
            pre_g = d_yg * y_off_g - t3 + jnp.where(last, tail, 0.0)
            s_pre, ddt_g, s_dsk = _seg_dots([pre_g, dxd_g * xs[:, gs], d_yg * xs[:, gs]], et_v[gs, :])
            da_cols = da_cols + s_pre
            ddsk_ref[...] += jnp.sum(s_dsk, axis=0, keepdims=True)
            dxa_scr[:, gs] = dxd_g * p["dt_map"][:, gs] + dsk[:, gs] * d_yg
            if g == 0:
                ddt = ddt_g
            else:
                ddt = ddt + ddt_g
        r_i = lax.broadcasted_iota(jnp.int32, (CHUNK, CHUNK), 0)
        c_i = lax.broadcasted_iota(jnp.int32, (CHUNK, CHUNK), 1)
        ddta = _tri_dot(r_i <= c_i, da_cols, terms=2)
        ddt = ddt + ddta * p["a_neg"]
        dalog_ref[...] += jnp.sum(ddta * p["dt"], axis=0, keepdims=True) * p["a_neg"]
        draw = ddt * _sigmoid(p["pre"])
        ddtb_ref[...] += jnp.sum(draw, axis=0, keepdims=True)
        dpdt_ref[...] = draw.astype(BF16)

        xc = p["xc"]
        sg = p["sg"]
        dxc = dxa_scr[...] * (sg + xc * sg * (1.0 - sg))
        dcb_ref[...] += jnp.sum(dxc, axis=0, keepdims=True)
        for k in range(CONV_K):
            dcw_ref[k] += jnp.sum(dxc * p["shifted"][k], axis=0, keepdims=True)
        nxt = nxt_scr[...]
        dxr = cw_ref[3] * dxc
        for s in range(1, CONV_K):
            dxr = dxr + cw_ref[CONV_K - 1 - s] * _shift_up(dxc, nxt, s)
        dpxbc_ref[...] = dxr.astype(BF16)
        nxt_scr[...] = dxc[:SUBLANES, :]

    seq_len = n_tok // n_seq

    def rows(width):
        return pl.BlockSpec((n_seq, CHUNK, width), lambda s: (0, nc - 1 - s, 0))

    tiles = CHUNK // SUBLANES
    prev_rows = pl.BlockSpec((n_seq, SUBLANES, CONV_CH), lambda s: (0, jnp.maximum((nc - 1 - s) * tiles - 1, 0), 0))

    def whole(shape):
        nd = len(shape)
        return pl.BlockSpec(tuple(shape), lambda s: (0,) * nd)

    def by_seq(a):
        return a.reshape(n_seq, seq_len, a.shape[-1])

    acc_shapes = [(CONV_K, 1, CONV_CH), (1, CONV_CH), (1, DT_PAD), (1, DT_PAD), (1, DT_PAD), (1, SSM_WIDTH)]
    xbc3 = by_seq(p_xbc)
    outs = _call_carrying(
        body, carried, name="ssd_bwd", grid=(nc,),
        in_specs=[rows(CONV_CH), prev_rows, rows(SSM_WIDTH), rows(DT_PAD), rows(SSM_WIDTH), rows(SSM_WIDTH),
                  rows(SSM_WIDTH)] + _ssd_const_specs(),
        out_specs=[rows(CONV_CH), rows(SSM_WIDTH), rows(DT_PAD)] + [whole(s) for s in acc_shapes],
        out_shape=tuple([_sds((n_seq, seq_len, CONV_CH), BF16), _sds((n_seq, seq_len, SSM_WIDTH), BF16),
                         _sds((n_seq, seq_len, DT_PAD), BF16)] + [_sds(s, F32) for s in acc_shapes]),
        scratch_shapes=[pltpu.VMEM((n_seq, SSM_STATE, SSM_WIDTH), F32), pltpu.VMEM((n_seq, SUBLANES, CONV_CH), F32),
                        pltpu.VMEM((n_seq, CHUNK, CONV_CH), F32)],
        operands=[xbc3, xbc3, by_seq(p_z), by_seq(p_dt), by_seq(yssd), by_seq(sprev), by_seq(dyb), conv_w, conv_b, dt_bias,
                  a_log, dskip_map, norm_w, e_bf, et_bf])
    return tuple(o.reshape(n_tok, o.shape[-1]) for o in outs[:3]) + tuple(outs[3:])


def _inproj_bwd(dp_uv, dp_xbc, dp_z, dp_dt, x, dx1, w_uv, w_xbc, w_z, w_dt, nw, tm=256, carried=None):
    n_tok = x.shape[0]

    def body(duv_ref, dxbc_ref, dz_ref, ddt_ref, x_ref, dx1_ref, wuv_ref, wxbc_ref, wz_ref, wdt_ref, nw_ref,
             gx_ref, h_ref, dnw_ref):
        dh = _dot_nt(duv_ref[...], wuv_ref[...]) + _dot_nt(dxbc_ref[...], wxbc_ref[...])
        dh = dh + _dot_nt(dz_ref[...], wz_ref[...]) + _dot_nt(ddt_ref[...], wdt_ref[...])
        xv = x_ref[...]
        h, r = _rms_fwd(xv, nw_ref[...])
        dx, dnw = _rms_bwd(xv, r, nw_ref[...], dh)
        gx_ref[...] = dx1_ref[...] + dx
        h_ref[...] = h.astype(BF16)
        dnw_ref[...] += dnw

    return _rows_call("inproj_bwd", body, tm, [dp_uv, dp_xbc, dp_z, dp_dt, x, dx1], [w_uv, w_xbc, w_z, w_dt, nw],
                      [_sds((n_tok, D_MODEL), F32), _sds((n_tok, D_MODEL), BF16)], [_sds((1, D_MODEL), F32)],
                      carried=carried)


def _const_maps():
    lane = jnp.arange(SSM_WIDTH) // HEAD_DIM
    e_bf = (jnp.arange(DT_PAD)[:, None] == lane[None, :]).astype(BF16)
    return e_bf, e_bf.T


def _pad_lanes(v, width):
    return jnp.pad(v, ((0, 0), (0, width - v.shape[1])))


SHARD_COLS = IN_COLS // N_CHIPS
_UV_END = 2 * GM_WIDTH
_Z_END = _UV_END + SSM_WIDTH
_XBC_END = _Z_END + CONV_CH


def _cols_from_shards(w4, lo, hi):
    pieces = []
    for j in range(N_CHIPS):
        a, b = max(lo, j * SHARD_COLS), min(hi, (j + 1) * SHARD_COLS)
        if a < b:
            pieces.append(w4[j][:, a - j * SHARD_COLS:b - j * SHARD_COLS])
    return pieces[0] if len(pieces) == 1 else jnp.concatenate(pieces, axis=1)


def _shards_from_cols(blocks):
    shards = []
    for j in range(N_CHIPS):
        pieces = []
        for arr, lo, hi in blocks:
            a, b = max(lo, j * SHARD_COLS), min(hi, (j + 1) * SHARD_COLS)
            if a < b:
                pieces.append(arr[:, a - lo:b - lo])
        shards.append(pieces[0] if len(pieces) == 1 else jnp.concatenate(pieces, axis=1))
    return jnp.stack(shards)


def _forward_backward(x, tgt, w_in4, conv_w, small, out_shard, up_shard, down_shard, core, adam_args):
    n_seq, seq_len, _ = x.shape
    n_tok = n_seq * seq_len
    x2 = x.reshape(n_tok, D_MODEL)
    tgt2 = tgt.reshape(n_tok, D_MODEL)
    e_bf, et_bf = _const_maps()

    w_uv = _cols_from_shards(w_in4, 0, _UV_END)
    w_z = _cols_from_shards(w_in4, _UV_END, _Z_END)
    w_xbc = _cols_from_shards(w_in4, _Z_END, _XBC_END)
    w_dt = _pad_lanes(_cols_from_shards(w_in4, _XBC_END, IN_COLS), DT_PAD)

    nw_pre = small["norm_mix_pre"]
    lnw = small["gm_ln_w"].reshape(1, GM_WIDTH)
    lnb = small["gm_ln_b"].reshape(1, GM_WIDTH)
    w_stack = small["gm_w_s"].reshape(N_HEADS * CHUNK, CHUNK)
    w_cat = jnp.transpose(small["gm_w_s"], (1, 0, 2)).reshape(CHUNK, N_HEADS * CHUNK)
    bmap = jnp.repeat(small["gm_b_s"].T, HEAD_DIM, axis=1)
    cw3 = conv_w.reshape(CONV_K, 1, CONV_CH)
    conv_b = small["conv_b"]
    dt_bias = _pad_lanes(small["dt_bias"], DT_PAD)
    a_log = _pad_lanes(small["a_log"], DT_PAD)
    dskip_map = jnp.repeat(small["d_skip"], HEAD_DIM, axis=1)
    ssm_nw = small["ssm_norm_w"]

    half = down_shard.shape[0] // 2
    p_uv, p_xbc, p_z, p_dt, w_out4, w_down_a = _inproj_fwd(
        x2, nw_pre, w_uv, w_xbc, w_z, w_dt, carried=_allgather_exchange([out_shard, down_shard[:half]]))
    ssd_consts = (cw3, conv_b, dt_bias, a_log, dskip_map, ssm_nw, e_bf, et_bf)
    w_out_b = w_out4.reshape(D_MODEL, D_MODEL)
    mix, yssd, sprev, o, x1, h2, w_up4, w_down_b = _mixer_fwd(
        p_uv, p_xbc, p_z, p_dt, x2, lnw, lnb, w_cat, bmap, w_out_b, small["norm_mix_post"], small["norm_ffn_pre"],
        *ssd_consts, n_seq, carried=_allgather_exchange([up_shard, down_shard[half:]]))
    f, dd, dy, loss_acc, d_nffn_post = _mlp_fwd(h2, x1, tgt2, w_up4, w_down_a, w_down_b, small["norm_ffn_post"])

    dup, dx1, d_nffn_pre = _mlp_bwd(dd, f, x1, dy, w_down_a, w_down_b, w_up4, small["norm_ffn_pre"])
    tk = min(DW_TOKENS_PER_STEP, n_tok)
    g_up = _matmul_tn("dw_up", h2, dup, D_MODEL, D_MODEL, tk, stacked=True)
    g_down = _matmul_tn("dw_down", f, dd, 1024, D_MODEL, tk).reshape(N_CHIPS, D_FF // N_CHIPS, D_MODEL)
    do, dya, dyb, d_nmix_post, got_up, got_down = _outproj_bwd(
        dx1, o, w_out_b, small["norm_mix_post"], carried=_pair_exchange([g_up, g_down]))
    h_up = _pair_sum(core, g_up, got_up, 256)
    h_down = _pair_sum(core, g_down, got_down, 256)
    g_out = _matmul_tn("dw_out", mix, do, D_MODEL, D_MODEL, tk).reshape(N_CHIPS, D_MODEL // N_CHIPS, D_MODEL)
    dp_uv, d_ws, d_bs_t, d_lnw, d_lnb, slab_up, got_out = _gmlp_bwd(
        p_uv, dya, lnw, lnb, e_bf, et_bf, w_cat, w_stack, bmap,
        carried=_both(_chip_exchange([h_up]), _pair_exchange([g_out])))
    h_out = _pair_sum(core, g_out, got_out, 128)
    early = {
        "gm_ln_w": d_lnw.reshape(N_HEADS, HEAD_DIM), "gm_ln_b": d_lnb.reshape(N_HEADS, HEAD_DIM),
        "gm_w_s": d_ws.reshape(N_HEADS, CHUNK, CHUNK), "gm_b_s": d_bs_t[:, :N_HEADS].T,
        "norm_mix_post": d_nmix_post, "norm_ffn_pre": d_nffn_pre, "norm_ffn_post": d_nffn_post,
    }
    packed_early = _pack(early, tuple(early), tail=loss_acc[0, 0].reshape(1))
    (dp_xbc, dp_z, dp_dt, d_cw, d_cb, d_dtb, d_alog, d_dsk, d_ssm_nw, slab_down, slab_out, all_early) = _ssd_bwd(
        p_xbc, p_z, p_dt, yssd, sprev, dyb, *ssd_consts, n_seq,
        carried=_both(_chip_exchange([h_down, h_out]), _device_gather_exchange(packed_early)))
    gx, h, d_nmix_pre = _inproj_bwd(dp_uv, dp_xbc, dp_z, dp_dt, x2, dx1, w_uv, w_xbc, w_z, w_dt, nw_pre)
    late = {
        "norm_mix_pre": d_nmix_pre, "conv_w": d_cw.reshape(CONV_K, CONV_CH), "conv_b": d_cb,
        "dt_bias": d_dtb[:, :N_HEADS], "a_log": d_alog[:, :N_HEADS], "d_skip": d_dsk[:, :N_HEADS],
        "ssm_norm_w": d_ssm_nw,
    }
    g_uv, all_late = _matmul_tn("dw_in_uv", h, dp_uv, D_MODEL, 2 * GM_WIDTH, tk,
                                carried=_device_gather_exchange(_pack(late, tuple(late))))
    sum_early = _ordered_sum("small_sum_early", all_early)
    small_sum = _unpack(sum_early, {n: v.shape for n, v in early.items()}, tuple(early))
    small_sum.update(_unpack(_ordered_sum("small_sum_late", all_late), {n: v.shape for n, v in late.items()}, tuple(late)))
    loss = sum_early.reshape(-1)[sum(v.size for v in early.values())]
    red_up, red_down, red_out = _chip_sum(slab_up, 256), _chip_sum(slab_down, 256), _chip_sum(slab_out, 128)
    g_xbc, oth_up, oth_down, oth_out = _matmul_tn("dw_in_xbc", h, dp_xbc, D_MODEL, CONV_CH, tk,
                                                  carried=_pair_swap([red_up, red_down, red_out]))
    g_z = _matmul_tn("dw_in_z", h, dp_z, D_MODEL, SSM_WIDTH, tk)
    g_dt = _matmul_tn("dw_in_dt", h, dp_dt, D_MODEL, DT_PAD, tk)

    g_in = _shards_from_cols([(g_uv, 0, _UV_END), (g_z, _UV_END, _Z_END), (g_xbc, _Z_END, _XBC_END),
                              (g_dt, _XBC_END, IN_COLS)])
    (got_in,) = _run_exchange("grad_pair_exchange_in", _pair_exchange([g_in]))
    h_in = _pair_sum(core, g_in, got_in, 256)
    (slab_in,) = _run_exchange("grad_chip_exchange", _chip_exchange([h_in]))
    res = _adamw_halves("adamw_mlp", [(adam_args["w_up"][0], red_up, oth_up) + adam_args["w_up"][1:],
                                      (adam_args["w_down"][0], red_down, oth_down) + adam_args["w_down"][1:]], 256)
    big_out = {"w_up": res[0:4], "w_down": res[4:8]}
    big_out["w_out"] = _adamw_halves("adamw_w_out", [(adam_args["w_out"][0], red_out, oth_out) + adam_args["w_out"][1:]], 128)
    red_in = _chip_sum(slab_in, 256)
    (oth_in,) = _run_exchange("grad_pair_swap_in", _pair_swap([red_in]))
    big_out["w_in"] = _adamw_halves("adamw_w_in", [(adam_args["w_in"][0], red_in, oth_in) + adam_args["w_in"][1:]], 256)

    return loss, gx.reshape(x.shape), big_out, small_sum


_HBM = pl.BlockSpec(memory_space=pltpu.HBM)


D2D_CHUNKS = 32
ICI_CHUNKS = 1
ROW_ALIGN = 16


def _row_chunks(rows, n_chunks):
    size = min(max(rows // n_chunks, ROW_ALIGN), rows)
    assert rows % size == 0
    return [(start, size) for start in range(0, rows, size)]


def _position():
    x, y, c = lax.axis_index("x"), lax.axis_index("y"), lax.axis_index("c")
    chips = [(1 - x, y), (x, 1 - y), (1 - x, 1 - y)]
    return x, y, c, chips


def _allgather_exchange(arrs):
    n = len(arrs)

    def copies(ins, outs, send_sems, recv_sems, local_sems):
        x, y, c, chips = _position()
        me = 2 * x + y
        sibling = (x, y, 1 - c)

        def copy(a, k, src, dst, to):
            return pltpu.make_async_remote_copy(src_ref=src, dst_ref=dst, send_sem=send_sems.at[a, k],
                                                recv_sem=recv_sems.at[a, k], device_id=to, device_id_type=MESH)

        def half_rows(a, pc):
            half = ins[a].shape[0] // 2
            return pl.ds(pc * half, half)

        local = [pltpu.make_async_copy(ins[a], outs[a].at[me], local_sems.at[a]) for a in range(n)]
        ici_out = [[copy(a, k, ins[a].at[half_rows(a, c)], outs[a].at[me, half_rows(a, c)], (px, py, c))
                    for k, (px, py) in enumerate(chips)] for a in range(n)]
        return c, chips, sibling, copy, half_rows, local, ici_out

    def start(ins, outs, send_sems, recv_sems, local_sems):
        c, chips, _, copy, _, local, _ = copies(ins, outs, send_sems, recv_sems, local_sems)
        x, y, _, _ = _position()
        me = 2 * x + y
        for cp in local:
            cp.start()
        for a in range(n):
            half = ins[a].shape[0] // 2
            for k, (px, py) in enumerate(chips):
                for first, size in _row_chunks(half, ICI_CHUNKS):
                    rows = pl.ds(c * half + first, size)
                    copy(a, k, ins[a].at[rows], outs[a].at[me, rows], (px, py, c)).start()

    def finish(ins, outs, send_sems, recv_sems, local_sems):
        c, chips, sibling, copy, half_rows, local, ici_out = copies(ins, outs, send_sems, recv_sems, local_sems)
        for a in range(n):
            half = ins[a].shape[0] // 2
            for k, (px, py) in enumerate(chips):
                blk = outs[a].at[2 * px + py, half_rows(a, c)]
                copy(a, k, blk, blk, (px, py, c)).wait_recv()
                for first, size in _row_chunks(half, D2D_CHUNKS):
                    piece = outs[a].at[2 * px + py, pl.ds(c * half + first, size)]
                    copy(a, 3 + k, piece, piece, sibling).start()
        for a in range(n):
            for k, (px, py) in enumerate(chips):
                theirs = outs[a].at[2 * px + py, half_rows(a, 1 - c)]
                copy(a, 3 + k, theirs, theirs, sibling).wait_recv()
                mine = outs[a].at[2 * px + py, half_rows(a, c)]
                copy(a, 3 + k, mine, mine, sibling).wait_send()
        for a in range(n):
            for cp in ici_out[a]:
                cp.wait_send()
        for cp in local:
            cp.wait()

    return _Carried(arrs, [_sds((N_CHIPS,) + a.shape, a.dtype) for a in arrs],
                    [pltpu.SemaphoreType.DMA((n, 6)), pltpu.SemaphoreType.DMA((n, 6)), pltpu.SemaphoreType.DMA((n,))],
                    start, finish)


def _run_exchange(name, exchange):
    n_in, n_out = len(exchange.ins), len(exchange.out_shapes)

    def body(*refs):
        ins, outs, sems = refs[:n_in], refs[n_in:n_in + n_out], refs[n_in + n_out:]
        exchange.start(ins, outs, *sems)
        exchange.finish(ins, outs, *sems)

    return pl.pallas_call(
        body, name=name, out_shape=tuple(exchange.out_shapes), in_specs=[_HBM] * n_in,
        out_specs=tuple([_HBM] * n_out), scratch_shapes=exchange.sems,
    )(*exchange.ins)


def _pair_exchange(grads):
    n = len(grads)

    def copier(send_sems, recv_sems):
        x, y, c, _ = _position()

        def copy(a, src, dst):
            return pltpu.make_async_remote_copy(src_ref=src, dst_ref=dst, send_sem=send_sems.at[a],
                                                recv_sem=recv_sems.at[a], device_id=(x, y, 1 - c), device_id_type=MESH)
        return c, copy

    def start(ins, got, send_sems, recv_sems):
        c, copy = copier(send_sems, recv_sems)
        for a in range(n):
            half = ins[a].shape[1] // 2
            for slab in range(N_CHIPS):
                for first, size in _row_chunks(half, D2D_CHUNKS):
                    copy(a, ins[a].at[slab, pl.ds((1 - c) * half + first, size), :],
                         got[a].at[slab, pl.ds(first, size), :]).start()

    def finish(ins, got, send_sems, recv_sems):
        c, copy = copier(send_sems, recv_sems)
        for a in range(n):
            half = ins[a].shape[1] // 2
            copy(a, ins[a].at[:, pl.ds((1 - c) * half, half), :], got[a]).wait()

    return _Carried(grads, [_sds((N_CHIPS, g.shape[1] // 2, g.shape[2]), g.dtype) for g in grads],
                    [pltpu.SemaphoreType.DMA((n,)), pltpu.SemaphoreType.DMA((n,))], start, finish)


def _chip_exchange(hsums):
    n = len(hsums)

    def copies(ins, outs, send_sems, recv_sems, local_sems, pieces):
        x, y, c, chips = _position()
        me = 2 * x + y
        cps = []
        for a in range(n):
            cps.append(pltpu.make_async_copy(ins[a].at[me], outs[a].at[me], local_sems.at[a]))
            rows = ins[a].shape[1]
            for k, (px, py) in enumerate(chips):
                for first, size in (_row_chunks(rows, ICI_CHUNKS) if pieces else [(0, rows)]):
                    cps.append(pltpu.make_async_remote_copy(
                        src_ref=ins[a].at[2 * px + py, pl.ds(first, size)], dst_ref=outs[a].at[me, pl.ds(first, size)],
                        send_sem=send_sems.at[a, k], recv_sem=recv_sems.at[a, k], device_id=(px, py, c),
                        device_id_type=MESH))
        return cps

    def start(*refs):
        for cp in copies(*refs, pieces=True):
            cp.start()

    def finish(*refs):
        for cp in copies(*refs, pieces=False):
            cp.wait()

    return _Carried(hsums, [_sds(h.shape, h.dtype) for h in hsums],
                    [pltpu.SemaphoreType.DMA((n, 3)), pltpu.SemaphoreType.DMA((n, 3)), pltpu.SemaphoreType.DMA((n,))],
                    start, finish)


def _pair_swap(reds):
    n = len(reds)

    def copier(send_sems, recv_sems):
        x, y, c, _ = _position()

        def copy(a, src, dst):
            return pltpu.make_async_remote_copy(src_ref=src, dst_ref=dst, send_sem=send_sems.at[a],
                                                recv_sem=recv_sems.at[a], device_id=(x, y, 1 - c), device_id_type=MESH)
        return copy

    def start(ins, outs, send_sems, recv_sems):
        copy = copier(send_sems, recv_sems)
        for a in range(n):
            for first, size in _row_chunks(ins[a].shape[0], 2 * D2D_CHUNKS):
                copy(a, ins[a].at[pl.ds(first, size), :], outs[a].at[pl.ds(first, size), :]).start()

    def finish(ins, outs, send_sems, recv_sems):
        copy = copier(send_sems, recv_sems)
        for a in range(n):
            copy(a, ins[a], outs[a]).wait()

    return _Carried(reds, [_sds(r.shape, r.dtype) for r in reds],
                    [pltpu.SemaphoreType.DMA((n,)), pltpu.SemaphoreType.DMA((n,))], start, finish)


def _device_gather_exchange(packed):
    def copies(ins, outs, send_sems, recv_sems, local_sem):
        (x_ref,), (all_ref,) = ins, outs
        x, y, c, chips = _position()
        me, sibling = (x, y, c), (x, y, 1 - c)

        def slab(px, py, pc):
            return all_ref.at[4 * px + 2 * py + pc]

        def copy(k, block, to, src=None):
            return pltpu.make_async_remote_copy(
                src_ref=slab(*block) if src is None else src, dst_ref=slab(*block), send_sem=send_sems.at[k],
                recv_sem=recv_sems.at[k], device_id=to, device_id_type=MESH)

        mine = pltpu.make_async_copy(x_ref, slab(*me), local_sem)
        first = [copy(0, me, sibling, src=x_ref)]
        first += [copy(1 + j, me, (*chip, c), src=x_ref) for j, chip in enumerate(chips)]
        passed = [copy(4 + j, (*chip, c), sibling) for j, chip in enumerate(chips)]
        return c, chips, me, sibling, copy, mine, first, passed

    def start(ins, outs, send_sems, recv_sems, local_sem):
        _, _, _, _, _, mine, first, _ = copies(ins, outs, send_sems, recv_sems, local_sem)
        mine.start()
        for cp in first:
            cp.start()

    def finish(ins, outs, send_sems, recv_sems, local_sem):
        c, chips, me, sibling, copy, mine, first, passed = copies(ins, outs, send_sems, recv_sems, local_sem)
        for j, chip in enumerate(chips):
            copy(1 + j, (*chip, c), me).wait_recv()
            passed[j].start()
        copy(0, sibling, me).wait_recv()
        for j, chip in enumerate(chips):
            copy(4 + j, (*chip, 1 - c), me).wait_recv()
        for cp in first + passed:
            cp.wait_send()
        mine.wait()

    return _Carried([packed], [_sds((N_DEV,) + packed.shape, F32)],
                    [pltpu.SemaphoreType.DMA((7,)), pltpu.SemaphoreType.DMA((7,)), pltpu.SemaphoreType.DMA],
                    start, finish)


def _ordered_sum(name, slabs):
    _, m_per, n_cols = slabs.shape

    def body(s_ref, o_ref):
        acc = s_ref[0]
        for d in range(1, N_DEV):
            acc = acc + s_ref[d]
        o_ref[...] = acc

    vmem = pl.BlockSpec(memory_space=pltpu.VMEM)
    return pl.pallas_call(body, name=name, out_shape=_sds((m_per, n_cols), F32), in_specs=[vmem], out_specs=vmem)(slabs)


def _pair_sum(core, own, got, tm):
    _, half, cols = got.shape
    nb = half // tm

    def body(c_ref, a_ref, b_ref, o_ref):
        o_ref[...] = (a_ref[...].astype(F32) + b_ref[...].astype(F32)).astype(BF16)

    return pl.pallas_call(
        body, name="grad_pair_sum", out_shape=_sds(got.shape, BF16),
        grid_spec=pltpu.PrefetchScalarGridSpec(
            num_scalar_prefetch=1, grid=(N_CHIPS, nb),
            in_specs=[pl.BlockSpec((None, tm, cols), lambda s, i, c_ref: (s, c_ref[0] * nb + i, 0)),
                      pl.BlockSpec((None, tm, cols), lambda s, i, c_ref: (s, i, 0))],
            out_specs=pl.BlockSpec((None, tm, cols), lambda s, i, c_ref: (s, i, 0))),
        compiler_params=_cparams(2),
    )(core, own, got)


def _chip_sum(slabs, tm):
    _, half, cols = slabs.shape

    def body(s_ref, o_ref):
        acc = s_ref[0].astype(F32)
        for k in range(1, N_CHIPS):
            acc = acc + s_ref[k].astype(F32)
        o_ref[...] = acc

    return pl.pallas_call(
        body, name="grad_chip_sum", out_shape=_sds((half, cols), F32), grid=(half // tm,),
        in_specs=[pl.BlockSpec((N_CHIPS, tm, cols), lambda i: (0, i, 0))],
        out_specs=pl.BlockSpec((tm, cols), lambda i: (i, 0)), compiler_params=_cparams(1),
    )(slabs)


def _adam_math(w, g, m, v):
    m2 = ADAM_B1 * m + (1.0 - ADAM_B1) * g
    v2 = ADAM_B2 * v + (1.0 - ADAM_B2) * (g * g)
    m_hat = m2 / (1.0 - ADAM_B1 ** ADAM_STEP)
    v_hat = v2 / (1.0 - ADAM_B2 ** ADAM_STEP)
    delta = -ADAM_LR * (m_hat / (jnp.sqrt(v_hat) + ADAM_EPS) + ADAM_WD * w)
    return delta, m2, v2


def _adamw_halves(name, items, tm, carried=None):
    rows, cols = items[0][0].shape
    nb = rows // 2 // tm
    n = len(items)

    def body(*refs):
        mine = (pl.program_id(0) // nb) == lax.axis_index("c")
        for k in range(n):
            w_ref, own_ref, oth_ref, m_ref, v_ref = refs[5 * k:5 * k + 5]
            g_ref, d_ref, m2_ref, v2_ref = refs[5 * n + 4 * k:5 * n + 4 * k + 4]
            g = jnp.where(mine, own_ref[...], oth_ref[...])
            d, m2, v2 = _adam_math(w_ref[...], g, m_ref[...], v_ref[...])
            g_ref[...] = g
            d_ref[...] = d
            m2_ref[...] = m2
            v2_ref[...] = v2

    full = pl.BlockSpec((tm, cols), lambda i: (i, 0))
    half = pl.BlockSpec((tm, cols), lambda i: (i % nb, 0))
    return _call_carrying(
        body, carried, name=name, grid=(rows // tm,), in_specs=[full, half, half, full, full] * n,
        out_specs=[full] * (4 * n), out_shape=tuple([_sds((rows, cols), F32)] * (4 * n)), scratch_shapes=[],
        operands=[a for item in items for a in item])


def _adamw(name, w, g, m, v, tm):
    def body(w_ref, g_ref, m_ref, v_ref, gout_ref, d_ref, m2_ref, v2_ref):
        gv = g_ref[...]
        d, m2, v2 = _adam_math(w_ref[...], gv, m_ref[...], v_ref[...])
        gout_ref[...] = gv
        d_ref[...] = d
        m2_ref[...] = m2
        v2_ref[...] = v2

    return _rows_call(name, body, tm, [w, g, m, v], [], [_sds(w.shape, F32)] * 4)


_SMALL_NAMES = ("norm_mix_pre", "gm_ln_w", "gm_ln_b", "gm_w_s", "gm_b_s", "conv_w", "conv_b", "dt_bias", "a_log",
                "d_skip", "ssm_norm_w", "norm_mix_post", "norm_ffn_pre", "norm_ffn_post")
_PACK_COLS = 1024


def _pack(parts, names=_SMALL_NAMES, tail=None):
    pieces = [parts[n].reshape(-1) for n in names]
    flat = jnp.concatenate(pieces if tail is None else pieces + [tail])
    rows = -(-flat.shape[0] // (8 * _PACK_COLS)) * 8
    flat = jnp.pad(flat, (0, rows * _PACK_COLS - flat.shape[0]))
    return flat.reshape(rows, _PACK_COLS)


def _unpack(packed, shapes, names=_SMALL_NAMES):
    flat = packed.reshape(-1)
    out, off = {}, 0
    for n in names:
        size = 1
        for s in shapes[n]:
            size *= s
        out[n] = flat[off:off + size].reshape(shapes[n])
        off += size
    return out


def kernel(x, norm_mix_pre, w_in, gm_ln_w, gm_ln_b, gm_w_s, gm_b_s, conv_w, conv_b, dt_bias, a_log, d_skip, ssm_norm_w, w_out, norm_mix_post, norm_ffn_pre, w_up, w_down, norm_ffn_post, loss_target, m_norm_mix_pre, m_w_in, m_gm_ln_w, m_gm_ln_b, m_gm_w_s, m_gm_b_s, m_conv_w, m_conv_b, m_dt_bias, m_a_log, m_d_skip, m_ssm_norm_w, m_w_out, m_norm_mix_post, m_norm_ffn_pre, m_w_up, m_w_down, m_norm_ffn_post, v_norm_mix_pre, v_w_in, v_gm_ln_w, v_gm_ln_b, v_gm_w_s, v_gm_b_s, v_conv_w, v_conv_b, v_dt_bias, v_a_log, v_d_skip, v_ssm_norm_w, v_w_out, v_norm_mix_post, v_norm_ffn_pre, v_w_up, v_w_down, v_norm_ffn_post):
    params = dict(norm_mix_pre=norm_mix_pre, w_in=w_in, gm_ln_w=gm_ln_w, gm_ln_b=gm_ln_b, gm_w_s=gm_w_s, gm_b_s=gm_b_s,
                  conv_w=conv_w, conv_b=conv_b, dt_bias=dt_bias, a_log=a_log, d_skip=d_skip, ssm_norm_w=ssm_norm_w,
                  w_out=w_out, norm_mix_post=norm_mix_post, norm_ffn_pre=norm_ffn_pre, w_up=w_up, w_down=w_down,
                  norm_ffn_post=norm_ffn_post)
    mom1 = dict(norm_mix_pre=m_norm_mix_pre, w_in=m_w_in, gm_ln_w=m_gm_ln_w, gm_ln_b=m_gm_ln_b, gm_w_s=m_gm_w_s,
                gm_b_s=m_gm_b_s, conv_w=m_conv_w, conv_b=m_conv_b, dt_bias=m_dt_bias, a_log=m_a_log, d_skip=m_d_skip,
                ssm_norm_w=m_ssm_norm_w, w_out=m_w_out, norm_mix_post=m_norm_mix_post, norm_ffn_pre=m_norm_ffn_pre,
                w_up=m_w_up, w_down=m_w_down, norm_ffn_post=m_norm_ffn_post)
    mom2 = dict(norm_mix_pre=v_norm_mix_pre, w_in=v_w_in, gm_ln_w=v_gm_ln_w, gm_ln_b=v_gm_ln_b, gm_w_s=v_gm_w_s,
                gm_b_s=v_gm_b_s, conv_w=v_conv_w, conv_b=v_conv_b, dt_bias=v_dt_bias, a_log=v_a_log, d_skip=v_d_skip,
                ssm_norm_w=v_ssm_norm_w, w_out=v_w_out, norm_mix_post=v_norm_mix_post, norm_ffn_pre=v_norm_ffn_pre,
                w_up=v_w_up, w_down=v_w_down, norm_ffn_post=v_norm_ffn_post)
    names = list(params)
    big = ("w_in", "w_out", "w_up", "w_down")
    chip = 2 * lax.axis_index("x") + lax.axis_index("y")

    shards = {n: params[n][0].astype(BF16) for n in big}
    conv_shard = jnp.pad(conv_w[0], ((0, 16 - CONV_K), (0, 0)))
    g_in4, g_conv4 = _run_exchange("allgather_w_in", _allgather_exchange([shards["w_in"], conv_shard]))
    conv_full = jnp.transpose(g_conv4[:, :CONV_K, :], (1, 0, 2)).reshape(CONV_K, CONV_CH)

    small = {n: params[n][0] if params[n].ndim >= 3 else params[n] for n in _SMALL_NAMES if n != "conv_w"}
    core = lax.axis_index("c").astype(jnp.int32).reshape(1)
    adam_args = {n: (params[n][0], mom1[n][0], mom2[n][0]) for n in big}
    loss, grad_x, big_out, small_sum = _forward_backward(
        x, loss_target, g_in4, conv_full, small, shards["w_out"], shards["w_up"], shards["w_down"], core, adam_args)
    grads, delta, new_m, new_v = {}, {}, {}, {}
    for n in big:
        grads[n], delta[n], new_m[n], new_v[n] = [a[None] for a in big_out[n]]

    small_sum["conv_w"] = lax.dynamic_slice_in_dim(small_sum["conv_w"], chip * (CONV_CH // N_CHIPS), CONV_CH // N_CHIPS, axis=1)

    local_shapes = {n: params[n].shape[1:] if params[n].ndim >= 3 else params[n].shape for n in _SMALL_NAMES}
    flat = lambda tree: {n: tree[n].reshape(local_shapes[n]) for n in _SMALL_NAMES}
    packed = [_pack(flat(t)) for t in (params, small_sum, mom1, mom2)]
    _, d_p, m_p, v_p = _adamw("adamw_small", *packed, packed[0].shape[0])
    for src, dst in ((d_p, delta), (m_p, new_m), (v_p, new_v)):
        for n, val in _unpack(src, local_shapes).items():
            dst[n] = val.reshape(params[n].shape)
    for n in _SMALL_NAMES:
        grads[n] = small_sum[n].reshape(params[n].shape)

    out = [loss, grad_x]
    for tree in (grads, delta, new_m, new_v):
        out += [tree[n] for n in names]
    return tuple(out)
```

```python
import functools

import jax
import jax.numpy as jnp
from jax import lax
from jax.experimental import pallas as pl
from jax.experimental.pallas import tpu as pltpu

F32 = jnp.float32
BF16 = jnp.bfloat16
MESH = pl.DeviceIdType.MESH

EPS = 1e-6
D_MODEL = 1024
GM_WIDTH = 512
SSM_WIDTH = 512
N_HEADS = 8
HEAD_DIM = 64
CHUNK = 128
SSM_GROUPS = 2
GROUP_W = SSM_WIDTH // SSM_GROUPS
SSM_STATE = 128
CONV_K = 4
CONV_CH = 1024
D_FF = 4096
IN_COLS = 2568
DT_PAD = 128
N_CHIPS = 4
N_DEV = 8

ADAM_LR = 0.001
ADAM_B1 = 0.9
ADAM_B2 = 0.999
ADAM_EPS = 1e-08
ADAM_WD = 0.01
ADAM_STEP = 10

VMEM_LIMIT_BYTES = 56 * 1024 * 1024
FF_TILE = 512
DW_TOKENS_PER_STEP = 4096


def _cparams(n_axes):
    return pltpu.CompilerParams(dimension_semantics=("arbitrary",) * n_axes, vmem_limit_bytes=VMEM_LIMIT_BYTES)


def _dot(a, b):
    return jnp.dot(a.astype(BF16), b.astype(BF16), preferred_element_type=F32)


def _dot_nt(a, b):
    return lax.dot_general(a.astype(BF16), b.astype(BF16), (((1,), (1,)), ((), ())), preferred_element_type=F32)


def _dot_tn(a, b):
    return lax.dot_general(a.astype(BF16), b.astype(BF16), (((0,), (0,)), ((), ())), preferred_element_type=F32)


def _sigmoid(x):
    return 1.0 / (1.0 + jnp.exp(-x))


_GELU_C = 0.7978845608028654
_GELU_A = 0.044715


def _gelu(x):
    t = jnp.tanh(_GELU_C * (x + _GELU_A * (x * x * x)))
    return 0.5 * x * (1.0 + t), t


def _gelu_grad(x, t):
    return 0.5 * (1.0 + t) + 0.5 * x * (1.0 - t * t) * (_GELU_C * (1.0 + 3.0 * _GELU_A * x * x))


def _rms_fwd(x, w):
    r = lax.rsqrt(jnp.mean(x * x, axis=-1, keepdims=True) + EPS)
    return x * r * w, r


def _rms_bwd(x, r, w, dy):
    g = dy * w
    dx = r * g - x * (r * r * r) * jnp.mean(g * x, axis=-1, keepdims=True)
    dw = jnp.sum(dy * x * r, axis=0, keepdims=True)
    return dx, dw


class _Carried:
    def __init__(self, ins, out_shapes, sems, start, finish):
        self.ins, self.out_shapes, self.sems = list(ins), list(out_shapes), list(sems)
        self.start, self.finish = start, finish


def _both(first, second):
    n_i, n_o, n_s = len(first.ins), len(first.out_shapes), len(first.sems)

    def split(ins, outs, sems):
        return (ins[:n_i], outs[:n_o], sems[:n_s]), (ins[n_i:], outs[n_o:], sems[n_s:])

    def start(ins, outs, *sems):
        (i1, o1, s1), (i2, o2, s2) = split(ins, outs, sems)
        first.start(i1, o1, *s1)
        second.start(i2, o2, *s2)

    def finish(ins, outs, *sems):
        (i1, o1, s1), (i2, o2, s2) = split(ins, outs, sems)
        first.finish(i1, o1, *s1)
        second.finish(i2, o2, *s2)

    return _Carried(first.ins + second.ins, first.out_shapes + second.out_shapes, first.sems + second.sems, start, finish)


def _split_carried(refs, n_in, n_out, n_scratch, carried):
    n_ci, n_co, n_cs = len(carried.ins), len(carried.out_shapes), len(carried.sems)
    ins, rest = refs[:n_in], refs[n_in:]
    c_ins, rest = rest[:n_ci], rest[n_ci:]
    outs, rest = rest[:n_out], rest[n_out:]
    c_outs, rest = rest[:n_co], rest[n_co:]
    scr, c_sems = rest[:n_scratch], rest[n_scratch:]
    assert len(c_sems) == n_cs
    return tuple(ins) + tuple(outs) + tuple(scr), c_ins, c_outs, c_sems


def _rows_call(name, body, tm, row_ins, const_ins, row_outs, acc_outs=(), scratch=(), carried=None):
    n_rows = row_ins[0].shape[0]
    assert n_rows % tm == 0
    n_steps = n_rows // tm
    n_in = len(row_ins) + len(const_ins)
    n_ro = len(row_outs)
    n_acc = len(acc_outs)

    def kern(*refs):
        accs = refs[n_in + n_ro:n_in + n_ro + n_acc]

        @pl.when(pl.program_id(0) == 0)
        def _():
            for a in accs:
                a[...] = jnp.zeros_like(a)

        body(*refs)

    def whole(shape):
        nd = len(shape)
        return pl.BlockSpec(tuple(shape), lambda i: (0,) * nd)

    in_specs = [pl.BlockSpec((tm, a.shape[1]), lambda i: (i, 0)) for a in row_ins]
    in_specs += [whole(a.shape) for a in const_ins]
    out_specs = [pl.BlockSpec((tm, s.shape[1]), lambda i: (i, 0)) for s in row_outs]
    out_specs += [whole(s.shape) for s in acc_outs]
    return _call_carrying(
        kern, carried, name=name, grid=(n_steps,), in_specs=in_specs, out_specs=out_specs,
        out_shape=tuple(row_outs) + tuple(acc_outs), scratch_shapes=list(scratch), operands=list(row_ins) + list(const_ins))


def _call_carrying(body, carried, *, name, grid, in_specs, out_specs, out_shape, scratch_shapes, operands):
    n_in, n_out, n_scratch = len(in_specs), len(out_specs), len(scratch_shapes)
    kern = body
    if carried is not None:
        def kern(*refs):
            plain, c_ins, c_outs, c_sems = _split_carried(refs, n_in, n_out, n_scratch, carried)
            first, last = True, True
            for d, size in enumerate(grid):
                first = jnp.logical_and(first, pl.program_id(d) == 0)
                last = jnp.logical_and(last, pl.program_id(d) == size - 1)

            @pl.when(first)
            def _():
                carried.start(c_ins, c_outs, *c_sems)

            body(*plain)

            @pl.when(last)
            def _():
                carried.finish(c_ins, c_outs, *c_sems)

        in_specs = list(in_specs) + [_HBM] * len(carried.ins)
        out_specs = list(out_specs) + [_HBM] * len(carried.out_shapes)
        out_shape = tuple(out_shape) + tuple(carried.out_shapes)
        operands = list(operands) + carried.ins
        scratch_shapes = list(scratch_shapes) + carried.sems
    return pl.pallas_call(
        kern, name=name, grid=grid, in_specs=in_specs, out_specs=out_specs, out_shape=out_shape,
        scratch_shapes=scratch_shapes, compiler_params=_cparams(len(grid)),
    )(*operands)


def _sds(shape, dtype):
    return jax.ShapeDtypeStruct(tuple(shape), dtype)


def _matmul_tn(name, a, b, tm, tn, tk, stacked=False, carried=None):
    k_dim, m_dim = a.shape
    n_dim = b.shape[1]
    assert m_dim % tm == 0 and n_dim % tn == 0 and k_dim % tk == 0
    nk = k_dim // tk

    def kern(a_ref, b_ref, o_ref, acc_ref):
        k = pl.program_id(2)
        prod = _dot_tn(a_ref[...], b_ref[...])

        @pl.when(k == 0)
        def _():
            acc_ref[...] = prod

        @pl.when(k > 0)
        def _():
            acc_ref[...] += prod

        @pl.when(k == nk - 1)
        def _():
            o_ref[...] = acc_ref[...].astype(o_ref.dtype)

    if stacked:
        assert tm == m_dim
        out_shape = _sds((n_dim // tn, m_dim, tn), BF16)
        out_spec = pl.BlockSpec((None, tm, tn), lambda i, j, k: (j, i, 0))
    else:
        out_shape = _sds((m_dim, n_dim), BF16)
        out_spec = pl.BlockSpec((tm, tn), lambda i, j, k: (i, j))
    outs = _call_carrying(
        kern, carried, name=name, grid=(m_dim // tm, n_dim // tn, nk),
        in_specs=[pl.BlockSpec((tk, tm), lambda i, j, k: (k, i)), pl.BlockSpec((tk, tn), lambda i, j, k: (k, j))],
        out_specs=[out_spec], out_shape=(out_shape,), scratch_shapes=[pltpu.VMEM((tm, tn), F32)], operands=[a, b])
    return outs[0] if carried is None else outs


def _inproj_fwd(x, nw, w_uv, w_xbc, w_z, w_dt, tm=256, carried=None):
    n_tok = x.shape[0]

    def body(x_ref, nw_ref, wuv_ref, wxbc_ref, wz_ref, wdt_ref, puv_ref, pxbc_ref, pz_ref, pdt_ref):
        h, _ = _rms_fwd(x_ref[...], nw_ref[...])
        h = h.astype(BF16)
        puv_ref[...] = jnp.dot(h, wuv_ref[...], preferred_element_type=F32)
        pxbc_ref[...] = jnp.dot(h, wxbc_ref[...], preferred_element_type=F32)
        pz_ref[...] = jnp.dot(h, wz_ref[...], preferred_element_type=F32)
        pdt_ref[...] = jnp.dot(h, wdt_ref[...], preferred_element_type=F32)

    return _rows_call(
        "inproj_fwd", body, tm, [x], [nw, w_uv, w_xbc, w_z, w_dt],
        [_sds((n_tok, 2 * GM_WIDTH), F32), _sds((n_tok, CONV_CH), F32), _sds((n_tok, SSM_WIDTH), F32),
         _sds((n_tok, DT_PAD), F32)], carried=carried)


def _head_lane_mask(width, head):
    lane = lax.broadcasted_iota(jnp.int32, (1, width), 1)
    return (lane // HEAD_DIM) == head


def _split_terms(x, terms):
    parts = []
    for _ in range(terms):
        p = x.astype(BF16)
        parts.append(p)
        x = x - p.astype(F32)
    return parts


def _seg_dots(vals, ind, terms=2):
    m = vals[0].shape[0]
    parts = []
    for v in vals:
        parts += _split_terms(v, terms)
    red = jnp.dot(jnp.concatenate(parts, axis=0), ind, preferred_element_type=F32)
    outs = []
    for i in range(len(vals)):
        acc = red[i * terms * m:(i * terms + 1) * m]
        for t in range(1, terms):
            acc = acc + red[(i * terms + t) * m:(i * terms + t + 1) * m]
        outs.append(acc)
    return outs


def _tri_dot(mask, x, terms=3):
    n = x.shape[1]
    red = jnp.dot(mask.astype(BF16), jnp.concatenate(_split_terms(x, terms), axis=1), preferred_element_type=F32)
    acc = red[:, :n]
    for t in range(1, terms):
        acc = acc + red[:, t * n:(t + 1) * n]
    return acc


def _gmlp_common(puv, lnw, lnb, e_bf, et_bf):
    u = puv[:, :GM_WIDTH]
    v = puv[:, GM_WIDTH:]
    gu, tu = _gelu(u)
    gv, tv = _gelu(v)
    (s1,) = _seg_dots([gv], et_bf)
    (mu,) = _seg_dots([s1 * (1.0 / HEAD_DIM)], e_bf)
    xc = gv - mu
    (s2,) = _seg_dots([xc * xc], et_bf)
    (rstd,) = _seg_dots([lax.rsqrt(s2 * (1.0 / HEAD_DIM) + EPS)], e_bf)
    xhat = xc * rstd
    vn = xhat * lnw + lnb
    return u, v, gu, tu, tv, rstd, xhat, vn


def _tril_mask():
    r = lax.broadcasted_iota(jnp.int32, (CHUNK, CHUNK), 0)
    c = lax.broadcasted_iota(jnp.int32, (CHUNK, CHUNK), 1)
    return r >= c


def _head_blocks(v):
    return jnp.concatenate([jnp.where(_head_lane_mask(GM_WIDTH, h), v, jnp.zeros_like(v)) for h in range(N_HEADS)], axis=0)


def _causal_w_cat(w_cat):
    t = lax.broadcasted_iota(jnp.int32, (CHUNK, N_HEADS * CHUNK), 0)
    s = lax.broadcasted_iota(jnp.int32, (CHUNK, N_HEADS * CHUNK), 1) % CHUNK
    return jnp.where(t >= s, w_cat, 0.0).astype(BF16)


def _gmlp_chunk_fwd(puv, lnw, lnb, e_bf, et_bf, wm, bmap):
    _, _, gu, _, _, _, _, vn = _gmlp_common(puv, lnw, lnb, e_bf, et_bf)
    mixed = jnp.dot(wm, _head_blocks(vn.astype(BF16)), preferred_element_type=F32) + bmap
    return (gu * mixed).astype(BF16)


SUBLANES = 8


def _shift_down(x, tail, s):
    main = pltpu.roll(x, s, 0)
    row = lax.broadcasted_iota(jnp.int32, (SUBLANES, 1), 0)
    head = jnp.where(row < s, pltpu.roll(tail, s, 0), main[:SUBLANES])
    return jnp.concatenate([head, main[SUBLANES:]], axis=0)


def _shift_up(x, head_next, s):
    n = x.shape[0]
    main = pltpu.roll(x, n - s, 0)
    row = lax.broadcasted_iota(jnp.int32, (SUBLANES, 1), 0)
    last = jnp.where(row >= SUBLANES - s, pltpu.roll(head_next, SUBLANES - s, 0), main[n - SUBLANES:])
    return jnp.concatenate([main[:n - SUBLANES], last], axis=0)


def _ssd_pre(xr, tail, cw_ref, cb, pdt, dtb, alog, emap):
    rowi = lax.broadcasted_iota(jnp.int32, (CHUNK, 1), 0)
    shifted = [_shift_down(xr, tail, 3), _shift_down(xr, tail, 2), _shift_down(xr, tail, 1), xr]
    xc = cb
    for k in range(CONV_K):
        xc = xc + cw_ref[k] * shifted[k]
    sg = _sigmoid(xc)
    xa = xc * sg
    pre = pdt + dtb
    dt = jnp.maximum(pre, 0.0) + jnp.log(1.0 + jnp.exp(-jnp.abs(pre)))
    a_neg = -jnp.exp(alog)
    a_cs = _tri_dot(_tril_mask(), dt * a_neg)
    acs_map, dt_map = _seg_dots([a_cs, dt], emap, terms=3)
    return dict(shifted=shifted, xc=xc, sg=sg, xa=xa, pre=pre, dt=dt, a_neg=a_neg, a_cs=a_cs,
                acs_map=acs_map, dt_map=dt_map, rowi=rowi)


def _ssd_maps(p):
    last = p["rowi"] == CHUNK - 1
    aq_map = jnp.sum(jnp.where(last, p["acs_map"], 0.0), axis=0, keepdims=True)
    e_exp = jnp.exp(p["acs_map"])
    dte = jnp.exp(aq_map - p["acs_map"])
    cd = jnp.exp(aq_map)
    return last, e_exp, dte, cd


def _head_decay(a_cs, a_cs_t, head, tri):
    lane = lax.broadcasted_iota(jnp.int32, (1, DT_PAD), 1)
    sub = lax.broadcasted_iota(jnp.int32, (DT_PAD, 1), 0)
    col = jnp.sum(jnp.where(lane == head, a_cs, 0.0), axis=1, keepdims=True)
    row = jnp.sum(jnp.where(sub == head, a_cs_t, 0.0), axis=0, keepdims=True)
    return jnp.exp(jnp.where(tri, col - row, -1e30))


def _gate_fwd(y, z, nw):
    sz = _sigmoid(z)
    zg = z * sz
    yg = y * zg
    outs, rs = [], []
    for g in range(SSM_GROUPS):
        gs = slice(g * GROUP_W, (g + 1) * GROUP_W)
        o, r = _rms_fwd(yg[:, gs], nw[:, gs])
        outs.append(o)
        rs.append(r)
    return sz, zg, yg, outs, rs


def _ssd_const_specs():
    def whole(shape):
        nd = len(shape)
        return pl.BlockSpec(tuple(shape), lambda c: (0,) * nd)
    return [whole((CONV_K, 1, CONV_CH)), whole((1, CONV_CH)), whole((1, DT_PAD)), whole((1, DT_PAD)),
            whole((1, SSM_WIDTH)), whole((1, SSM_WIDTH)), whole((DT_PAD, SSM_WIDTH)), whole((SSM_WIDTH, DT_PAD))]


def _mixer_fwd(p_uv, p_xbc, p_z, p_dt, x, lnw, lnb, w_cat, bmap, w_out, nw_post, nw_pre2, conv_w, conv_b, dt_bias, a_log,
               dskip_map, norm_w, e_bf, et_bf, n_seq, carried=None):
    n_tok = p_xbc.shape[0]
    nc = n_tok // n_seq // CHUNK

    def body(puv3, xr3, z3, pdt3, x3, lnw_ref, lnb_ref, wcat_ref, bmap_ref, wo_ref, nwa_ref, nwb_ref,
             cw_ref, cb_ref, dtb_ref, alog_ref, dsk_ref, nw_ref, e_ref, et_ref,
             mix3, yssd3, sprev3, o3, x13, h23, wm_scr, prev3_scr, s3_scr):
        @pl.when(pl.program_id(0) == 0)
        def _():
            wm_scr[...] = _causal_w_cat(wcat_ref[...])
            prev3_scr[...] = jnp.zeros_like(prev3_scr)
            s3_scr[...] = jnp.zeros_like(s3_scr)

        for b in range(n_seq):
            one_sequence(puv3.at[b], xr3.at[b], z3.at[b], pdt3.at[b], lnw_ref, lnb_ref, bmap_ref,
                         cw_ref, cb_ref, dtb_ref, alog_ref, dsk_ref, nw_ref, e_ref, et_ref,
                         mix3.at[b], yssd3.at[b], sprev3.at[b], wm_scr, prev3_scr.at[b], s3_scr.at[b])
            o = jnp.dot(mix3[b], wo_ref[...], preferred_element_type=F32)
            on, _ = _rms_fwd(o, nwa_ref[...])
            x1 = x3[b] + on
            h2, _ = _rms_fwd(x1, nwb_ref[...])
            o3[b] = o
            x13[b] = x1
            h23[b] = h2.astype(BF16)

    def one_sequence(puv_ref, xr_ref, z_ref, pdt_ref, lnw_ref, lnb_ref, bmap_ref,
                     cw_ref, cb_ref, dtb_ref, alog_ref, dsk_ref, nw_ref, e_ref, et_ref,
                     mix_ref, yssd_ref, sprev_ref, wm_scr, prev_scr, s_scr):
        mix_ref[:, :GM_WIDTH] = _gmlp_chunk_fwd(puv_ref[...], lnw_ref[...], lnb_ref[...], e_ref[...], et_ref[...], wm_scr[...],
                                      bmap_ref[...])
        xr = xr_ref[...]
        p = _ssd_pre(xr, prev_scr[...], cw_ref, cb_ref[...], pdt_ref[...], dtb_ref[...], alog_ref[...], e_ref[...])
        _, e_exp, dte, cd = _ssd_maps(p)
        xs = p["xa"][:, :SSM_WIDTH]
        xd = xs * p["dt_map"]
        a_cs_t = p["a_cs"].T
        tri = _tril_mask()
        s_old = s_scr[...]
        sprev_ref[...] = s_old
        for g in range(SSM_GROUPS):
            gs = slice(g * GROUP_W, (g + 1) * GROUP_W)
            bm = p["xa"][:, SSM_WIDTH + g * SSM_STATE: SSM_WIDTH + (g + 1) * SSM_STATE].astype(BF16)
            cm = p["xa"][:, SSM_WIDTH + (SSM_GROUPS + g) * SSM_STATE: SSM_WIDTH + (SSM_GROUPS + g + 1) * SSM_STATE].astype(BF16)
            cb_mat = _dot_nt(cm, bm)
            xdg = xd[:, gs].astype(BF16)
            y_g = _dot(cm, s_old[:, gs]) * e_exp[:, gs] + dsk_ref[:, gs] * xs[:, gs]
            for r in range(SSM_GROUPS * 2):
                dm = _head_decay(p["a_cs"], a_cs_t, g * 4 + r, tri)
                full = jnp.dot((cb_mat * dm).astype(BF16), xdg, preferred_element_type=F32)
                y_g = y_g + jnp.where(_head_lane_mask(GROUP_W, r), full, 0.0)
            yssd_ref[:, gs] = y_g
            s_scr[:, gs] = cd[:, gs] * s_old[:, gs] + _dot_tn(bm, xd[:, gs] * dte[:, gs])
        _, _, _, outs, _ = _gate_fwd(yssd_ref[...], z_ref[...], nw_ref[...])
        for g in range(SSM_GROUPS):
            mix_ref[:, GM_WIDTH + g * GROUP_W:GM_WIDTH + (g + 1) * GROUP_W] = outs[g].astype(BF16)
        prev_scr[...] = xr[CHUNK - SUBLANES:, :]

    seq_len = n_tok // n_seq

    def rows(width):
        return pl.BlockSpec((n_seq, CHUNK, width), lambda c: (0, c, 0))

    def whole(shape):
        nd = len(shape)
        return pl.BlockSpec(tuple(shape), lambda c: (0,) * nd)

    def by_seq(a):
        return a.reshape(n_seq, seq_len, a.shape[-1])

    outs = _call_carrying(
        body, carried, name="mixer_fwd", grid=(nc,),
        in_specs=[rows(2 * GM_WIDTH), rows(CONV_CH), rows(SSM_WIDTH), rows(DT_PAD), rows(D_MODEL), whole(lnw.shape),
                  whole(lnb.shape), whole(w_cat.shape), whole(bmap.shape), whole(w_out.shape), whole(nw_post.shape),
                  whole(nw_pre2.shape)] + _ssd_const_specs(),
        out_specs=[rows(D_MODEL), rows(SSM_WIDTH), rows(SSM_WIDTH), rows(D_MODEL), rows(D_MODEL), rows(D_MODEL)],
        out_shape=(_sds((n_seq, seq_len, D_MODEL), BF16),
                   _sds((n_seq, seq_len, SSM_WIDTH), F32), _sds((n_seq, seq_len, SSM_WIDTH), F32),
                   _sds((n_seq, seq_len, D_MODEL), F32), _sds((n_seq, seq_len, D_MODEL), F32),
                   _sds((n_seq, seq_len, D_MODEL), BF16)),
        scratch_shapes=[pltpu.VMEM((CHUNK, N_HEADS * CHUNK), BF16), pltpu.VMEM((n_seq, SUBLANES, CONV_CH), F32),
                        pltpu.VMEM((n_seq, SSM_STATE, SSM_WIDTH), F32)],
        operands=[by_seq(p_uv), by_seq(p_xbc), by_seq(p_z), by_seq(p_dt), by_seq(x), lnw, lnb, w_cat, bmap, w_out, nw_post,
                  nw_pre2, conv_w, conv_b, dt_bias, a_log, dskip_map, norm_w, e_bf, et_bf])
    return tuple(o.reshape(n_tok, o.shape[-1]) for o in outs[:6]) + tuple(outs[6:])


def _up_cols(wup_ref, j):
    per = (D_FF // N_CHIPS) // FF_TILE
    return wup_ref[j // per, :, (j % per) * FF_TILE:(j % per + 1) * FF_TILE]


def _down_rows(wda_ref, wdb_ref, j):
    assert 2 * FF_TILE == D_FF // N_CHIPS
    return (wda_ref if j % 2 == 0 else wdb_ref)[j // 2]


def _skewed_rows_call(name, main, tail, tm, lead_ins, lag_ins, const_ins, lead_outs, lag_outs, acc_outs, carry,
                      streamed, tile_copies, n_copies):
    n_rows = lead_ins[0].shape[0]
    assert n_rows % tm == 0
    n = n_rows // tm
    counts = [len(lead_ins), len(lag_ins), len(const_ins), len(streamed), len(lead_outs), len(lag_outs), len(acc_outs),
              1, len(streamed)]

    def kern(*refs):
        groups, pos = [], 0
        for cnt in counts:
            groups.append(refs[pos:pos + cnt])
            pos += cnt
        lead_i, lag_i, consts, w_hbm, lead_o, lag_o, accs, (carry_scr,), w_vmem = groups
        sems = refs[pos]
        i = pl.program_id(0)
        pieces, k = [], 0
        for piece in tile_copies(w_hbm, w_vmem):
            pieces.append([pltpu.make_async_copy(src, dst, sems.at[k + q]) for q, (src, dst) in enumerate(piece)])
            k += len(piece)

        def ready(j):
            for cp in pieces[j]:
                cp.wait()

        @pl.when(i == 0)
        def _():
            for piece in pieces:
                for cp in piece:
                    cp.start()
            for a in accs:
                a[...] = jnp.zeros_like(a)
            carry_scr[...] = main(lead_i, consts, lead_o, w_vmem, ready)

        @pl.when(jnp.logical_and(i > 0, i < n))
        def _():
            previous = carry_scr[...]
            carry_scr[...] = main(lead_i, consts, lead_o, w_vmem, lambda j: None)
            tail(previous, lag_i, consts, lag_o, accs)

        @pl.when(i == n)
        def _():
            tail(carry_scr[...], lag_i, consts, lag_o, accs)

    def lead(width):
        return pl.BlockSpec((tm, width), lambda i: (jnp.minimum(i, n - 1), 0))

    def lag(width):
        return pl.BlockSpec((tm, width), lambda i: (jnp.maximum(i - 1, 0), 0))

    def whole(shape):
        nd = len(shape)
        return pl.BlockSpec(tuple(shape), lambda i: (0,) * nd)

    return pl.pallas_call(
        kern, name=name, grid=(n + 1,),
        in_specs=([lead(a.shape[1]) for a in lead_ins] + [lag(a.shape[1]) for a in lag_ins]
                  + [whole(a.shape) for a in const_ins] + [_HBM] * len(streamed)),
        out_specs=[lead(s.shape[1]) for s in lead_outs] + [lag(s.shape[1]) for s in lag_outs] + [whole(s.shape) for s in acc_outs],
        out_shape=tuple(lead_outs) + tuple(lag_outs) + tuple(acc_outs),
        scratch_shapes=([pltpu.VMEM(carry, F32)] + [pltpu.VMEM(a.shape, a.dtype) for a in streamed]
                        + [pltpu.SemaphoreType.DMA((n_copies,))]),
        compiler_params=_cparams(1),
    )(*lead_ins, *lag_ins, *const_ins, *streamed)


def _mlp_weight_pieces(order):
    per = (D_FF // N_CHIPS) // FF_TILE

    def tile_copies(hbm, vmem):
        pieces = []
        for j in range(D_FF // FF_TILE):
            cols = (j // per, slice(None), pl.ds((j % per) * FF_TILE, FF_TILE))
            up = (hbm[0].at[cols], vmem[0].at[cols])
            down = (hbm[1 + j % 2].at[j // 2], vmem[1 + j % 2].at[j // 2])
            pieces.append([up, down] if order == "up_down" else [down, up])
        return pieces

    return tile_copies


def _mlp_fwd(h2, x1, tgt, w_up, w_down_a, w_down_b, nw, tm=512):
    n_tok = x1.shape[0]

    def main(lead_i, consts, lead_o, weights, ready):
        (h2_ref,), (f_ref,), (wup_ref, wda_ref, wdb_ref) = lead_i, lead_o, weights
        h2v = h2_ref[...]
        acc = jnp.zeros((tm, D_MODEL), F32)
        for j in range(D_FF // FF_TILE):
            cs = slice(j * FF_TILE, (j + 1) * FF_TILE)
            ready(j)
            u = jnp.dot(h2v, _up_cols(wup_ref, j), preferred_element_type=F32)
            f = jnp.square(jnp.maximum(u, 0.0)).astype(BF16)
            f_ref[:, cs] = f
            acc = acc + jnp.dot(f, _down_rows(wda_ref, wdb_ref, j), preferred_element_type=F32)
        return acc

    def tail(acc, lag_i, consts, lag_o, accs):
        (x1_ref, tgt_ref), (nw_ref,), (dd_ref, dy_ref), (loss_ref, dnw_ref) = lag_i, consts, lag_o, accs
        dn, r = _rms_fwd(acc, nw_ref[...])
        e = x1_ref[...] + dn - tgt_ref[...]
        loss_ref[...] += jnp.full(loss_ref.shape, (0.5 / D_MODEL) * jnp.sum(e * e), F32)
        dy = e * (1.0 / D_MODEL)
        dd, dnw = _rms_bwd(acc, r, nw_ref[...], dy)
        dy_ref[...] = dy
        dd_ref[...] = dd.astype(BF16)
        dnw_ref[...] += dnw

    return _skewed_rows_call(
        "mlp_fwd", main, tail, tm, [h2], [x1, tgt], [nw],
        [_sds((n_tok, D_FF), BF16)], [_sds((n_tok, D_MODEL), BF16), _sds((n_tok, D_MODEL), F32)],
        [_sds((8, 128), F32), _sds((1, D_MODEL), F32)], carry=(tm, D_MODEL),
        streamed=[w_up, w_down_a, w_down_b], tile_copies=_mlp_weight_pieces("up_down"), n_copies=2 * (D_FF // FF_TILE))


def _mlp_bwd(dd, f, x1, dy, w_down_a, w_down_b, w_up, nw, tm=256):
    n_tok = x1.shape[0]

    def main(lead_i, consts, lead_o, weights, ready):
        (dd_ref, f_ref), (dup_ref,), (wup_ref, wda_ref, wdb_ref) = lead_i, lead_o, weights
        ddv = dd_ref[...]
        acc = jnp.zeros((tm, D_MODEL), F32)
        for j in range(D_FF // FF_TILE):
            cs = slice(j * FF_TILE, (j + 1) * FF_TILE)
            ready(j)
            df = _dot_nt(ddv, _down_rows(wda_ref, wdb_ref, j))
            du = (df * (2.0 * jnp.sqrt(f_ref[:, cs].astype(F32)))).astype(BF16)
            dup_ref[:, cs] = du
            acc = acc + _dot_nt(du, _up_cols(wup_ref, j))
        return acc

    def tail(acc, lag_i, consts, lag_o, accs):
        (x1_ref, dy_ref), (nw_ref,), (dx1_ref,), (dnw_ref,) = lag_i, consts, lag_o, accs
        x1v = x1_ref[...]
        _, r = _rms_fwd(x1v, nw_ref[...])
        dx, dnw = _rms_bwd(x1v, r, nw_ref[...], acc)
        dx1_ref[...] = dy_ref[...] + dx
        dnw_ref[...] += dnw

    return _skewed_rows_call(
        "mlp_bwd", main, tail, tm, [dd, f], [x1, dy], [nw],
        [_sds((n_tok, D_FF), BF16)], [_sds((n_tok, D_MODEL), F32)], [_sds((1, D_MODEL), F32)], carry=(tm, D_MODEL),
        streamed=[w_up, w_down_a, w_down_b], tile_copies=_mlp_weight_pieces("down_up"), n_copies=2 * (D_FF // FF_TILE))


def _outproj_bwd(dx1, o, w_out, nw, tm=256, carried=None):
    n_tok = dx1.shape[0]

    def body(dx1_ref, o_ref, wo_ref, nw_ref, do_ref, dya_ref, dyb_ref, dnw_ref):
        ov = o_ref[...]
        _, r = _rms_fwd(ov, nw_ref[...])
        do, dnw = _rms_bwd(ov, r, nw_ref[...], dx1_ref[...])
        dob = do.astype(BF16)
        do_ref[...] = dob
        dya_ref[...] = _dot_nt(dob, wo_ref[:GM_WIDTH, :])
        dyb_ref[...] = _dot_nt(dob, wo_ref[GM_WIDTH:, :])
        dnw_ref[...] += dnw

    return _rows_call("outproj_bwd", body, tm, [dx1, o], [w_out, nw],
                      [_sds((n_tok, D_MODEL), BF16), _sds((n_tok, GM_WIDTH), F32), _sds((n_tok, SSM_WIDTH), F32)],
                      [_sds((1, D_MODEL), F32)], carried=carried)


def _gmlp_bwd(p_uv, dya, lnw, lnb, e_bf, et_bf, w_cat, w_stack, bmap, carried=None):
    n_tok = p_uv.shape[0]
    chunks_per_step = 2

    def body(puv_ref, dya_ref, lnw_ref, lnb_ref, e_ref, et_ref, wcat_ref, wstack_ref, bmap_ref,
             dpuv_ref, dws_ref, dbs_ref, dlnw_ref, dlnb_ref, wm_scr, wsm_scr):
        t_stk = lax.broadcasted_iota(jnp.int32, (N_HEADS * CHUNK, CHUNK), 0) % CHUNK
        s_stk = lax.broadcasted_iota(jnp.int32, (N_HEADS * CHUNK, CHUNK), 1)

        @pl.when(pl.program_id(0) == 0)
        def _():
            wm_scr[...] = _causal_w_cat(wcat_ref[...])
            wsm_scr[...] = jnp.where(t_stk >= s_stk, wstack_ref[...], 0.0).astype(BF16)

        lnw_v = lnw_ref[...]
        e_v, et_v = e_ref[...], et_ref[...]

        def one_chunk(rows):
            u, v, gu, tu, tv, rstd, xhat, vn = _gmlp_common(puv_ref[rows, :], lnw_v, lnb_ref[...], e_v, et_v)
            vnb = vn.astype(BF16)
            mixed = jnp.dot(wm_scr[...], _head_blocks(vnb), preferred_element_type=F32) + bmap_ref[...]
            dy = dya_ref[rows, :]
            du = dy * mixed * _gelu_grad(u, tu)
            dmixed = dy * gu
            (dbs,) = _seg_dots([dmixed], et_v)
            dblocks = _head_blocks(dmixed.astype(BF16))
            dvn = lax.dot_general(wsm_scr[...], dblocks, (((0,), (0,)), ((), ())), preferred_element_type=F32)
            dws = lax.dot_general(dblocks, vnb, (((1,), (1,)), ((), ())), preferred_element_type=F32)
            dxh = dvn * lnw_v
            m1, m2 = _seg_dots([dxh, dxh * xhat], et_v)
            m1, m2 = _seg_dots([m1 * (1.0 / HEAD_DIM), m2 * (1.0 / HEAD_DIM)], e_v)
            dgv = rstd * (dxh - m1 - xhat * m2)
            dv = dgv * _gelu_grad(v, tv)
            dpuv_ref[rows, :GM_WIDTH] = du.astype(BF16)
            dpuv_ref[rows, GM_WIDTH:] = dv.astype(BF16)
            return dbs, dws, jnp.sum(dvn * xhat, axis=0, keepdims=True), jnp.sum(dvn, axis=0, keepdims=True)

        parts = [one_chunk(slice(k * CHUNK, (k + 1) * CHUNK)) for k in range(chunks_per_step)]
        dbs, dws, dlnw, dlnb = [functools.reduce(lambda a, b: a + b, vals) for vals in zip(*parts)]
        dbs_ref[...] += dbs
        dws_ref[...] += jnp.where(t_stk >= s_stk, dws, 0.0)
        dlnw_ref[...] += dlnw
        dlnb_ref[...] += dlnb

    return _rows_call(
        "gmlp_bwd", body, chunks_per_step * CHUNK, [p_uv, dya], [lnw, lnb, e_bf, et_bf, w_cat, w_stack, bmap],
        [_sds((n_tok, 2 * GM_WIDTH), BF16)],
        [_sds((N_HEADS * CHUNK, CHUNK), F32), _sds((CHUNK, DT_PAD), F32), _sds((1, GM_WIDTH), F32),
         _sds((1, GM_WIDTH), F32)],
        scratch=[pltpu.VMEM((CHUNK, N_HEADS * CHUNK), BF16), pltpu.VMEM((N_HEADS * CHUNK, CHUNK), BF16)],
        carried=carried)


def _ssd_bwd(p_xbc, p_z, p_dt, yssd, sprev, dyb, conv_w, conv_b, dt_bias, a_log, dskip_map, norm_w, e_bf, et_bf, n_seq,
             carried=None):
    n_tok = p_xbc.shape[0]
    nc = n_tok // n_seq // CHUNK

    def body(xr3, xprev3, z3, pdt3, yssd3, sprev3, dyb3,
             cw_ref, cb_ref, dtb_ref, alog_ref, dsk_ref, nw_ref, e_ref, et_ref,
             dpxbc3, dpz3, dpdt3, dcw_ref, dcb_ref, ddtb_ref, dalog_ref, ddsk_ref, dnw_ref,
             ds3_scr, nxt3_scr, dxa3_scr):
        @pl.when(pl.program_id(0) == 0)
        def _():
            for a in (dcw_ref, dcb_ref, ddtb_ref, dalog_ref, ddsk_ref, dnw_ref, ds3_scr, nxt3_scr):
                a[...] = jnp.zeros_like(a)

        for b in range(n_seq):
            one_sequence(xr3.at[b], xprev3.at[b], z3.at[b], pdt3.at[b], yssd3.at[b], sprev3.at[b], dyb3.at[b],
                         cw_ref, cb_ref, dtb_ref, alog_ref, dsk_ref, nw_ref, e_ref, et_ref,
                         dpxbc3.at[b], dpz3.at[b], dpdt3.at[b], dcw_ref, dcb_ref, ddtb_ref, dalog_ref, ddsk_ref, dnw_ref,
                         ds3_scr.at[b], nxt3_scr.at[b], dxa3_scr.at[b])

    def one_sequence(xr_ref, xprev_ref, z_ref, pdt_ref, yssd_ref, sprev_ref, dyb_ref,
                     cw_ref, cb_ref, dtb_ref, alog_ref, dsk_ref, nw_ref, e_ref, et_ref,
                     dpxbc_ref, dpz_ref, dpdt_ref, dcw_ref, dcb_ref, ddtb_ref, dalog_ref, ddsk_ref, dnw_ref,
                     ds_scr, nxt_scr, dxa_scr):
        chunk = nc - 1 - pl.program_id(0)
        xr = xr_ref[...]
        prev = jnp.where(chunk == 0, 0.0, xprev_ref[...])
        et_v = et_ref[...]
        p = _ssd_pre(xr, prev, cw_ref, cb_ref[...], pdt_ref[...], dtb_ref[...], alog_ref[...], e_ref[...])
        last, e_exp, dte, cd = _ssd_maps(p)
        rowi = p["rowi"]
        xs = p["xa"][:, :SSM_WIDTH]
        xd = xs * p["dt_map"]
        a_cs_t = p["a_cs"].T
        tri = _tril_mask()
        dsk = dsk_ref[...]
        nw_v = nw_ref[...]

        yv = yssd_ref[...]
        zv = z_ref[...]
        sz, zg, yg, _, rs = _gate_fwd(yv, zv, nw_v)
        dout = dyb_ref[...]
        for g in range(SSM_GROUPS):
            gs = slice(g * GROUP_W, (g + 1) * GROUP_W)
            dyg_g, dnw_g = _rms_bwd(yg[:, gs], rs[g], nw_v[:, gs], dout[:, gs])
            dnw_ref[:, gs] += dnw_g
            dxa_scr[:, gs] = dyg_g
        dyg = dxa_scr[:, :SSM_WIDTH]
        d_y = dyg * zg
        dpz_ref[...] = (dyg * yv * (sz + zv * sz * (1.0 - sz))).astype(BF16)

        s_prev = sprev_ref[...]
        ds_next = ds_scr[...]
        lane_dt = lax.broadcasted_iota(jnp.int32, (1, DT_PAD), 1)
        da_cols = jnp.zeros((CHUNK, DT_PAD), F32)
        for g in range(SSM_GROUPS):
            gs = slice(g * GROUP_W, (g + 1) * GROUP_W)
            b_off = SSM_WIDTH + g * SSM_STATE
            c_off = SSM_WIDTH + (SSM_GROUPS + g) * SSM_STATE
            bm = p["xa"][:, b_off:b_off + SSM_STATE].astype(BF16)
            cm = p["xa"][:, c_off:c_off + SSM_STATE].astype(BF16)
            cb_mat = _dot_nt(cm, bm)
            d_yg = d_y[:, gs]
            d_ygb = d_yg.astype(BF16)
            xdg = xd[:, gs]
            xdgb = xdg.astype(BF16)
            ds_g = ds_next[:, gs]
            sp_g = s_prev[:, gs]
            bds = _dot(bm, ds_g)
            dcs = d_yg * e_exp[:, gs]
            d_c = _dot_nt(dcs, sp_g)
            ds_scr[:, gs] = cd[:, gs] * ds_g + _dot_tn(cm, dcs)
            d_b = _dot_nt(xdg * dte[:, gs], ds_g)
            dxd_g = bds * dte[:, gs]
            sum_dcb = jnp.zeros((CHUNK, CHUNK), F32)
            for r in range(SSM_GROUPS * 2):
                head = g * 4 + r
                mask = _head_lane_mask(GROUP_W, r)
                dm = _head_decay(p["a_cs"], a_cs_t, head, tri)
                m_mat = cb_mat * dm
                g_mat = _dot_nt(jnp.where(mask, d_yg, 0.0), xdgb)
                w_mat = g_mat * m_mat
                sum_dcb = sum_dcb + g_mat * dm
                dxd_g = dxd_g + jnp.where(mask, _dot_tn(m_mat, d_ygb), 0.0)
                da_h = jnp.sum(w_mat - w_mat.T, axis=1, keepdims=True)
                da_cols = da_cols + jnp.where(lane_dt == head, da_h, 0.0)
            d_c = d_c + _dot(sum_dcb, bm)
            d_b = d_b + _dot_tn(sum_dcb, cm)
            dxa_scr[:, b_off:b_off + SSM_STATE] = d_b
            dxa_scr[:, c_off:c_off + SSM_STATE] = d_c
            y_off_g = _dot(cm, sp_g) * e_exp[:, gs]
            t3 = bds * xdg * dte[:, gs]
            tail = jnp.sum(t3, axis=0, keepdims=True) + jnp.sum(ds_g * sp_g, axis=0, keepdims=True) * cd[:, gs]
            pre_g = d_yg * y_off_g - t3 + jnp.where(last, tail, 0.0)
            s_pre, ddt_g, s_dsk = _seg_dots([pre_g, dxd_g * xs[:, gs], d_yg * xs[:, gs]], et_v[gs, :])
            da_cols = da_cols + s_pre
            ddsk_ref[...] += jnp.sum(s_dsk, axis=0, keepdims=True)
            dxa_scr[:, gs] = dxd_g * p["dt_map"][:, gs] + dsk[:, gs] * d_yg
            if g == 0:
                ddt = ddt_g
            else:
                ddt = ddt + ddt_g
        r_i = lax.broadcasted_iota(jnp.int32, (CHUNK, CHUNK), 0)
        c_i = lax.broadcasted_iota(jnp.int32, (CHUNK, CHUNK), 1)
        ddta = _tri_dot(r_i <= c_i, da_cols, terms=2)
        ddt = ddt + ddta * p["a_neg"]
        dalog_ref[...] += jnp.sum(ddta * p["dt"], axis=0, keepdims=True) * p["a_neg"]
        draw = ddt * _sigmoid(p["pre"])
        ddtb_ref[...] += jnp.sum(draw, axis=0, keepdims=True)
        dpdt_ref[...] = draw.astype(BF16)

        xc = p["xc"]
        sg = p["sg"]
        dxc = dxa_scr[...] * (sg + xc * sg * (1.0 - sg))
        dcb_ref[...] += jnp.sum(dxc, axis=0, keepdims=True)
        for k in range(CONV_K):
            dcw_ref[k] += jnp.sum(dxc * p["shifted"][k], axis=0, keepdims=True)
        nxt = nxt_scr[...]
        dxr = cw_ref[3] * dxc
        for s in range(1, CONV_K):
            dxr = dxr + cw_ref[CONV_K - 1 - s] * _shift_up(dxc, nxt, s)
        dpxbc_ref[...] = dxr.astype(BF16)
        nxt_scr[...] = dxc[:SUBLANES, :]

    seq_len = n_tok // n_seq

    def rows(width):
        return pl.BlockSpec((n_seq, CHUNK, width), lambda s: (0, nc - 1 - s, 0))

    tiles = CHUNK // SUBLANES
    prev_rows = pl.BlockSpec((n_seq, SUBLANES, CONV_CH), lambda s: (0, jnp.maximum((nc - 1 - s) * tiles - 1, 0), 0))

    def whole(shape):
        nd = len(shape)
        return pl.BlockSpec(tuple(shape), lambda s: (0,) * nd)

    def by_seq(a):
        return a.reshape(n_seq, seq_len, a.shape[-1])

    acc_shapes = [(CONV_K, 1, CONV_CH), (1, CONV_CH), (1, DT_PAD), (1, DT_PAD), (1, DT_PAD), (1, SSM_WIDTH)]
    xbc3 = by_seq(p_xbc)
    outs = _call_carrying(
        body, carried, name="ssd_bwd", grid=(nc,),
        in_specs=[rows(CONV_CH), prev_rows, rows(SSM_WIDTH), rows(DT_PAD), rows(SSM_WIDTH), rows(SSM_WIDTH),
                  rows(SSM_WIDTH)] + _ssd_const_specs(),
        out_specs=[rows(CONV_CH), rows(SSM_WIDTH), rows(DT_PAD)] + [whole(s) for s in acc_shapes],
        out_shape=tuple([_sds((n_seq, seq_len, CONV_CH), BF16), _sds((n_seq, seq_len, SSM_WIDTH), BF16),
                         _sds((n_seq, seq_len, DT_PAD), BF16)] + [_sds(s, F32) for s in acc_shapes]),
        scratch_shapes=[pltpu.VMEM((n_seq, SSM_STATE, SSM_WIDTH), F32), pltpu.VMEM((n_seq, SUBLANES, CONV_CH), F32),
                        pltpu.VMEM((n_seq, CHUNK, CONV_CH), F32)],
        operands=[xbc3, xbc3, by_seq(p_z), by_seq(p_dt), by_seq(yssd), by_seq(sprev), by_seq(dyb), conv_w, conv_b, dt_bias,
                  a_log, dskip_map, norm_w, e_bf, et_bf])
    return tuple(o.reshape(n_tok, o.shape[-1]) for o in outs[:3]) + tuple(outs[3:])


def _inproj_bwd(dp_uv, dp_xbc, dp_z, dp_dt, x, dx1, w_uv, w_xbc, w_z, w_dt, nw, tm=256, carried=None):
    n_tok = x.shape[0]

    def body(duv_ref, dxbc_ref, dz_ref, ddt_ref, x_ref, dx1_ref, wuv_ref, wxbc_ref, wz_ref, wdt_ref, nw_ref,
             gx_ref, h_ref, dnw_ref):
        dh = _dot_nt(duv_ref[...], wuv_ref[...]) + _dot_nt(dxbc_ref[...], wxbc_ref[...])
        dh = dh + _dot_nt(dz_ref[...], wz_ref[...]) + _dot_nt(ddt_ref[...], wdt_ref[...])
        xv = x_ref[...]
        h, r = _rms_fwd(xv, nw_ref[...])
        dx, dnw = _rms_bwd(xv, r, nw_ref[...], dh)
        gx_ref[...] = dx1_ref[...] + dx
        h_ref[...] = h.astype(BF16)
        dnw_ref[...] += dnw

    return _rows_call("inproj_bwd", body, tm, [dp_uv, dp_xbc, dp_z, dp_dt, x, dx1], [w_uv, w_xbc, w_z, w_dt, nw],
                      [_sds((n_tok, D_MODEL), F32), _sds((n_tok, D_MODEL), BF16)], [_sds((1, D_MODEL), F32)],
                      carried=carried)


def _const_maps():
    lane = jnp.arange(SSM_WIDTH) // HEAD_DIM
    e_bf = (jnp.arange(DT_PAD)[:, None] == lane[None, :]).astype(BF16)
    return e_bf, e_bf.T


def _pad_lanes(v, width):
    return jnp.pad(v, ((0, 0), (0, width - v.shape[1])))


SHARD_COLS = IN_COLS // N_CHIPS
_UV_END = 2 * GM_WIDTH
_Z_END = _UV_END + SSM_WIDTH
_XBC_END = _Z_END + CONV_CH


def _cols_from_shards(w4, lo, hi):
    pieces = []
    for j in range(N_CHIPS):
        a, b = max(lo, j * SHARD_COLS), min(hi, (j + 1) * SHARD_COLS)
        if a < b:
            pieces.append(w4[j][:, a - j * SHARD_COLS:b - j * SHARD_COLS])
    return pieces[0] if len(pieces) == 1 else jnp.concatenate(pieces, axis=1)


def _shards_from_cols(blocks):
    shards = []
    for j in range(N_CHIPS):
        pieces = []
        for arr, lo, hi in blocks:
            a, b = max(lo, j * SHARD_COLS), min(hi, (j + 1) * SHARD_COLS)
            if a < b:
                pieces.append(arr[:, a - lo:b - lo])
        shards.append(pieces[0] if len(pieces) == 1 else jnp.concatenate(pieces, axis=1))
    return jnp.stack(shards)


def _forward_backward(x, tgt, w_in4, conv_w, small, out_shard, up_shard, down_shard, core, adam_args):
    n_seq, seq_len, _ = x.shape
    n_tok = n_seq * seq_len
    x2 = x.reshape(n_tok, D_MODEL)
    tgt2 = tgt.reshape(n_tok, D_MODEL)
    e_bf, et_bf = _const_maps()

    w_uv = _cols_from_shards(w_in4, 0, _UV_END)
    w_z = _cols_from_shards(w_in4, _UV_END, _Z_END)
    w_xbc = _cols_from_shards(w_in4, _Z_END, _XBC_END)
    w_dt = _pad_lanes(_cols_from_shards(w_in4, _XBC_END, IN_COLS), DT_PAD)

    nw_pre = small["norm_mix_pre"]
    lnw = small["gm_ln_w"].reshape(1, GM_WIDTH)
    lnb = small["gm_ln_b"].reshape(1, GM_WIDTH)
    w_stack = small["gm_w_s"].reshape(N_HEADS * CHUNK, CHUNK)
    w_cat = jnp.transpose(small["gm_w_s"], (1, 0, 2)).reshape(CHUNK, N_HEADS * CHUNK)
    bmap = jnp.repeat(small["gm_b_s"].T, HEAD_DIM, axis=1)
    cw3 = conv_w.reshape(CONV_K, 1, CONV_CH)
    conv_b = small["conv_b"]
    dt_bias = _pad_lanes(small["dt_bias"], DT_PAD)
    a_log = _pad_lanes(small["a_log"], DT_PAD)
    dskip_map = jnp.repeat(small["d_skip"], HEAD_DIM, axis=1)
    ssm_nw = small["ssm_norm_w"]

    half = down_shard.shape[0] // 2
    p_uv, p_xbc, p_z, p_dt, w_out4, w_down_a = _inproj_fwd(
        x2, nw_pre, w_uv, w_xbc, w_z, w_dt, carried=_allgather_exchange([out_shard, down_shard[:half]]))
    ssd_consts = (cw3, conv_b, dt_bias, a_log, dskip_map, ssm_nw, e_bf, et_bf)
    w_out_b = w_out4.reshape(D_MODEL, D_MODEL)
    mix, yssd, sprev, o, x1, h2, w_up4, w_down_b = _mixer_fwd(
        p_uv, p_xbc, p_z, p_dt, x2, lnw, lnb, w_cat, bmap, w_out_b, small["norm_mix_post"], small["norm_ffn_pre"],
        *ssd_consts, n_seq, carried=_allgather_exchange([up_shard, down_shard[half:]]))
    f, dd, dy, loss_acc, d_nffn_post = _mlp_fwd(h2, x1, tgt2, w_up4, w_down_a, w_down_b, small["norm_ffn_post"])

    dup, dx1, d_nffn_pre = _mlp_bwd(dd, f, x1, dy, w_down_a, w_down_b, w_up4, small["norm_ffn_pre"])
    tk = min(DW_TOKENS_PER_STEP, n_tok)
    g_up = _matmul_tn("dw_up", h2, dup, D_MODEL, D_MODEL, tk, stacked=True)
    g_down = _matmul_tn("dw_down", f, dd, 1024, D_MODEL, tk).reshape(N_CHIPS, D_FF // N_CHIPS, D_MODEL)
    do, dya, dyb, d_nmix_post, got_up, got_down = _outproj_bwd(
        dx1, o, w_out_b, small["norm_mix_post"], carried=_pair_exchange([g_up, g_down]))
    h_up = _pair_sum(core, g_up, got_up, 256)
    h_down = _pair_sum(core, g_down, got_down, 256)
    g_out = _matmul_tn("dw_out", mix, do, D_MODEL, D_MODEL, tk).reshape(N_CHIPS, D_MODEL // N_CHIPS, D_MODEL)
    dp_uv, d_ws, d_bs_t, d_lnw, d_lnb, slab_up, got_out = _gmlp_bwd(
        p_uv, dya, lnw, lnb, e_bf, et_bf, w_cat, w_stack, bmap,
        carried=_both(_chip_exchange([h_up]), _pair_exchange([g_out])))
    h_out = _pair_sum(core, g_out, got_out, 128)
    early = {
        "gm_ln_w": d_lnw.reshape(N_HEADS, HEAD_DIM), "gm_ln_b": d_lnb.reshape(N_HEADS, HEAD_DIM),
        "gm_w_s": d_ws.reshape(N_HEADS, CHUNK, CHUNK), "gm_b_s": d_bs_t[:, :N_HEADS].T,
        "norm_mix_post": d_nmix_post, "norm_ffn_pre": d_nffn_pre, "norm_ffn_post": d_nffn_post,
    }
    packed_early = _pack(early, tuple(early), tail=loss_acc[0, 0].reshape(1))
    (dp_xbc, dp_z, dp_dt, d_cw, d_cb, d_dtb, d_alog, d_dsk, d_ssm_nw, slab_down, slab_out, all_early) = _ssd_bwd(
        p_xbc, p_z, p_dt, yssd, sprev, dyb, *ssd_consts, n_seq,
        carried=_both(_chip_exchange([h_down, h_out]), _device_gather_exchange(packed_early)))
    gx, h, d_nmix_pre = _inproj_bwd(dp_uv, dp_xbc, dp_z, dp_dt, x2, dx1, w_uv, w_xbc, w_z, w_dt, nw_pre)
    late = {
        "norm_mix_pre": d_nmix_pre, "conv_w": d_cw.reshape(CONV_K, CONV_CH), "conv_b": d_cb,
        "dt_bias": d_dtb[:, :N_HEADS], "a_log": d_alog[:, :N_HEADS], "d_skip": d_dsk[:, :N_HEADS],
        "ssm_norm_w": d_ssm_nw,
    }
    g_uv, all_late = _matmul_tn("dw_in_uv", h, dp_uv, D_MODEL, 2 * GM_WIDTH, tk,
                                carried=_device_gather_exchange(_pack(late, tuple(late))))
    sum_early = _ordered_sum("small_sum_early", all_early)
    small_sum = _unpack(sum_early, {n: v.shape for n, v in early.items()}, tuple(early))
    small_sum.update(_unpack(_ordered_sum("small_sum_late", all_late), {n: v.shape for n, v in late.items()}, tuple(late)))
    loss = sum_early.reshape(-1)[sum(v.size for v in early.values())]
    red_up, red_down, red_out = _chip_sum(slab_up, 256), _chip_sum(slab_down, 256), _chip_sum(slab_out, 128)
    g_xbc, oth_up, oth_down, oth_out = _matmul_tn("dw_in_xbc", h, dp_xbc, D_MODEL, CONV_CH, tk,
                                                  carried=_pair_swap([red_up, red_down, red_out]))
    g_z = _matmul_tn("dw_in_z", h, dp_z, D_MODEL, SSM_WIDTH, tk)
    g_dt = _matmul_tn("dw_in_dt", h, dp_dt, D_MODEL, DT_PAD, tk)

    g_in = _shards_from_cols([(g_uv, 0, _UV_END), (g_z, _UV_END, _Z_END), (g_xbc, _Z_END, _XBC_END),
                              (g_dt, _XBC_END, IN_COLS)])
    (got_in,) = _run_exchange("grad_pair_exchange_in", _pair_exchange([g_in]))
    h_in = _pair_sum(core, g_in, got_in, 256)
    (slab_in,) = _run_exchange("grad_chip_exchange", _chip_exchange([h_in]))
    res = _adamw_halves("adamw_mlp", [(adam_args["w_up"][0], red_up, oth_up) + adam_args["w_up"][1:],
                                      (adam_args["w_down"][0], red_down, oth_down) + adam_args["w_down"][1:]], 256)
    big_out = {"w_up": res[0:4], "w_down": res[4:8]}
    big_out["w_out"] = _adamw_halves("adamw_w_out", [(adam_args["w_out"][0], red_out, oth_out) + adam_args["w_out"][1:]], 128)
    red_in = _chip_sum(slab_in, 256)
    (oth_in,) = _run_exchange("grad_pair_swap_in", _pair_swap([red_in]))
    big_out["w_in"] = _adamw_halves("adamw_w_in", [(adam_args["w_in"][0], red_in, oth_in) + adam_args["w_in"][1:]], 256)

    return loss, gx.reshape(x.shape), big_out, small_sum


_HBM = pl.BlockSpec(memory_space=pltpu.HBM)


D2D_CHUNKS = 8
ICI_CHUNKS = 1
ROW_ALIGN = 16


def _row_chunks(rows, n_chunks):
    size = min(max(rows // n_chunks, ROW_ALIGN), rows)
    assert rows % size == 0
    return [(start, size) for start in range(0, rows, size)]


def _position():
    x, y, c = lax.axis_index("x"), lax.axis_index("y"), lax.axis_index("c")
    chips = [(1 - x, y), (x, 1 - y), (1 - x, 1 - y)]
    return x, y, c, chips


def _allgather_exchange(arrs):
    n = len(arrs)

    def copies(ins, outs, send_sems, recv_sems, local_sems):
        x, y, c, chips = _position()
        me = 2 * x + y
        sibling = (x, y, 1 - c)

        def copy(a, k, src, dst, to):
            return pltpu.make_async_remote_copy(src_ref=src, dst_ref=dst, send_sem=send_sems.at[a, k],
                                                recv_sem=recv_sems.at[a, k], device_id=to, device_id_type=MESH)

        def half_rows(a, pc):
            half = ins[a].shape[0] // 2
            return pl.ds(pc * half, half)

        local = [pltpu.make_async_copy(ins[a], outs[a].at[me], local_sems.at[a]) for a in range(n)]
        ici_out = [[copy(a, k, ins[a].at[half_rows(a, c)], outs[a].at[me, half_rows(a, c)], (px, py, c))
                    for k, (px, py) in enumerate(chips)] for a in range(n)]
        return c, chips, sibling, copy, half_rows, local, ici_out

    def start(ins, outs, send_sems, recv_sems, local_sems):
        c, chips, _, copy, _, local, _ = copies(ins, outs, send_sems, recv_sems, local_sems)
        x, y, _, _ = _position()
        me = 2 * x + y
        for cp in local:
            cp.start()
        for a in range(n):
            half = ins[a].shape[0] // 2
            for k, (px, py) in enumerate(chips):
                for first, size in _row_chunks(half, ICI_CHUNKS):
                    rows = pl.ds(c * half + first, size)
                    copy(a, k, ins[a].at[rows], outs[a].at[me, rows], (px, py, c)).start()

    def finish(ins, outs, send_sems, recv_sems, local_sems):
        c, chips, sibling, copy, half_rows, local, ici_out = copies(ins, outs, send_sems, recv_sems, local_sems)
        for a in range(n):
            half = ins[a].shape[0] // 2
            for k, (px, py) in enumerate(chips):
                blk = outs[a].at[2 * px + py, half_rows(a, c)]
                copy(a, k, blk, blk, (px, py, c)).wait_recv()
                for first, size in _row_chunks(half, D2D_CHUNKS):
                    piece = outs[a].at[2 * px + py, pl.ds(c * half + first, size)]
                    copy(a, 3 + k, piece, piece, sibling).start()
        for a in range(n):
            for k, (px, py) in enumerate(chips):
                theirs = outs[a].at[2 * px + py, half_rows(a, 1 - c)]
                copy(a, 3 + k, theirs, theirs, sibling).wait_recv()
                mine = outs[a].at[2 * px + py, half_rows(a, c)]
                copy(a, 3 + k, mine, mine, sibling).wait_send()
        for a in range(n):
            for cp in ici_out[a]:
                cp.wait_send()
        for cp in local:
            cp.wait()

    return _Carried(arrs, [_sds((N_CHIPS,) + a.shape, a.dtype) for a in arrs],
                    [pltpu.SemaphoreType.DMA((n, 6)), pltpu.SemaphoreType.DMA((n, 6)), pltpu.SemaphoreType.DMA((n,))],
                    start, finish)


def _run_exchange(name, exchange):
    n_in, n_out = len(exchange.ins), len(exchange.out_shapes)

    def body(*refs):
        ins, outs, sems = refs[:n_in], refs[n_in:n_in + n_out], refs[n_in + n_out:]
        exchange.start(ins, outs, *sems)
        exchange.finish(ins, outs, *sems)

    return pl.pallas_call(
        body, name=name, out_shape=tuple(exchange.out_shapes), in_specs=[_HBM] * n_in,
        out_specs=tuple([_HBM] * n_out), scratch_shapes=exchange.sems,
    )(*exchange.ins)


def _pair_exchange(grads):
    n = len(grads)

    def copier(send_sems, recv_sems):
        x, y, c, _ = _position()

        def copy(a, src, dst):
            return pltpu.make_async_remote_copy(src_ref=src, dst_ref=dst, send_sem=send_sems.at[a],
                                                recv_sem=recv_sems.at[a], device_id=(x, y, 1 - c), device_id_type=MESH)
        return c, copy

    def start(ins, got, send_sems, recv_sems):
        c, copy = copier(send_sems, recv_sems)
        for a in range(n):
            half = ins[a].shape[1] // 2
            for slab in range(N_CHIPS):
                for first, size in _row_chunks(half, D2D_CHUNKS):
                    copy(a, ins[a].at[slab, pl.ds((1 - c) * half + first, size), :],
                         got[a].at[slab, pl.ds(first, size), :]).start()

    def finish(ins, got, send_sems, recv_sems):
        c, copy = copier(send_sems, recv_sems)
        for a in range(n):
            half = ins[a].shape[1] // 2
            copy(a, ins[a].at[:, pl.ds((1 - c) * half, half), :], got[a]).wait()

    return _Carried(grads, [_sds((N_CHIPS, g.shape[1] // 2, g.shape[2]), g.dtype) for g in grads],
                    [pltpu.SemaphoreType.DMA((n,)), pltpu.SemaphoreType.DMA((n,))], start, finish)


def _chip_exchange(hsums):
    n = len(hsums)

    def copies(ins, outs, send_sems, recv_sems, local_sems, pieces):
        x, y, c, chips = _position()
        me = 2 * x + y
        cps = []
        for a in range(n):
            cps.append(pltpu.make_async_copy(ins[a].at[me], outs[a].at[me], local_sems.at[a]))
            rows = ins[a].shape[1]
            for k, (px, py) in enumerate(chips):
                for first, size in (_row_chunks(rows, ICI_CHUNKS) if pieces else [(0, rows)]):
                    cps.append(pltpu.make_async_remote_copy(
                        src_ref=ins[a].at[2 * px + py, pl.ds(first, size)], dst_ref=outs[a].at[me, pl.ds(first, size)],
                        send_sem=send_sems.at[a, k], recv_sem=recv_sems.at[a, k], device_id=(px, py, c),
                        device_id_type=MESH))
        return cps

    def start(*refs):
        for cp in copies(*refs, pieces=True):
            cp.start()

    def finish(*refs):
        for cp in copies(*refs, pieces=False):
            cp.wait()

    return _Carried(hsums, [_sds(h.shape, h.dtype) for h in hsums],
                    [pltpu.SemaphoreType.DMA((n, 3)), pltpu.SemaphoreType.DMA((n, 3)), pltpu.SemaphoreType.DMA((n,))],
                    start, finish)


def _pair_swap(reds):
    n = len(reds)

    def copier(send_sems, recv_sems):
        x, y, c, _ = _position()

        def copy(a, src, dst):
            return pltpu.make_async_remote_copy(src_ref=src, dst_ref=dst, send_sem=send_sems.at[a],
                                                recv_sem=recv_sems.at[a], device_id=(x, y, 1 - c), device_id_type=MESH)
        return copy

    def start(ins, outs, send_sems, recv_sems):
        copy = copier(send_sems, recv_sems)
        for a in range(n):
            for first, size in _row_chunks(ins[a].shape[0], 2 * D2D_CHUNKS):
                copy(a, ins[a].at[pl.ds(first, size), :], outs[a].at[pl.ds(first, size), :]).start()

    def finish(ins, outs, send_sems, recv_sems):
        copy = copier(send_sems, recv_sems)
        for a in range(n):
            copy(a, ins[a], outs[a]).wait()

    return _Carried(reds, [_sds(r.shape, r.dtype) for r in reds],
                    [pltpu.SemaphoreType.DMA((n,)), pltpu.SemaphoreType.DMA((n,))], start, finish)


def _device_gather_exchange(packed):
    def copies(ins, outs, send_sems, recv_sems, local_sem):
        (x_ref,), (all_ref,) = ins, outs
        x, y, c, chips = _position()
        me, sibling = (x, y, c), (x, y, 1 - c)

        def slab(px, py, pc):
            return all_ref.at[4 * px + 2 * py + pc]

        def copy(k, block, to, src=None):
            return pltpu.make_async_remote_copy(
                src_ref=slab(*block) if src is None else src, dst_ref=slab(*block), send_sem=send_sems.at[k],
                recv_sem=recv_sems.at[k], device_id=to, device_id_type=MESH)

        mine = pltpu.make_async_copy(x_ref, slab(*me), local_sem)
        first = [copy(0, me, sibling, src=x_ref)]
        first += [copy(1 + j, me, (*chip, c), src=x_ref) for j, chip in enumerate(chips)]
        passed = [copy(4 + j, (*chip, c), sibling) for j, chip in enumerate(chips)]
        return c, chips, me, sibling, copy, mine, first, passed

    def start(ins, outs, send_sems, recv_sems, local_sem):
        _, _, _, _, _, mine, first, _ = copies(ins, outs, send_sems, recv_sems, local_sem)
        mine.start()
        for cp in first:
            cp.start()

    def finish(ins, outs, send_sems, recv_sems, local_sem):
        c, chips, me, sibling, copy, mine, first, passed = copies(ins, outs, send_sems, recv_sems, local_sem)
        for j, chip in enumerate(chips):
            copy(1 + j, (*chip, c), me).wait_recv()
            passed[j].start()
        copy(0, sibling, me).wait_recv()
        for j, chip in enumerate(chips):
            copy(4 + j, (*chip, 1 - c), me).wait_recv()
        for cp in first + passed:
            cp.wait_send()
        mine.wait()

    return _Carried([packed], [_sds((N_DEV,) + packed.shape, F32)],
                    [pltpu.SemaphoreType.DMA((7,)), pltpu.SemaphoreType.DMA((7,)), pltpu.SemaphoreType.DMA],
                    start, finish)


def _ordered_sum(name, slabs):
    _, m_per, n_cols = slabs.shape

    def body(s_ref, o_ref):
        acc = s_ref[0]
        for d in range(1, N_DEV):
            acc = acc + s_ref[d]
        o_ref[...] = acc

    vmem = pl.BlockSpec(memory_space=pltpu.VMEM)
    return pl.pallas_call(body, name=name, out_shape=_sds((m_per, n_cols), F32), in_specs=[vmem], out_specs=vmem)(slabs)


def _pair_sum(core, own, got, tm):
    _, half, cols = got.shape
    nb = half // tm

    def body(c_ref, a_ref, b_ref, o_ref):
        o_ref[...] = (a_ref[...].astype(F32) + b_ref[...].astype(F32)).astype(BF16)

    return pl.pallas_call(
        body, name="grad_pair_sum", out_shape=_sds(got.shape, BF16),
        grid_spec=pltpu.PrefetchScalarGridSpec(
            num_scalar_prefetch=1, grid=(N_CHIPS, nb),
            in_specs=[pl.BlockSpec((None, tm, cols), lambda s, i, c_ref: (s, c_ref[0] * nb + i, 0)),
                      pl.BlockSpec((None, tm, cols), lambda s, i, c_ref: (s, i, 0))],
            out_specs=pl.BlockSpec((None, tm, cols), lambda s, i, c_ref: (s, i, 0))),
        compiler_params=_cparams(2),
    )(core, own, got)


def _chip_sum(slabs, tm):
    _, half, cols = slabs.shape

    def body(s_ref, o_ref):
        acc = s_ref[0].astype(F32)
        for k in range(1, N_CHIPS):
            acc = acc + s_ref[k].astype(F32)
        o_ref[...] = acc

    return pl.pallas_call(
        body, name="grad_chip_sum", out_shape=_sds((half, cols), F32), grid=(half // tm,),
        in_specs=[pl.BlockSpec((N_CHIPS, tm, cols), lambda i: (0, i, 0))],
        out_specs=pl.BlockSpec((tm, cols), lambda i: (i, 0)), compiler_params=_cparams(1),
    )(slabs)


def _adam_math(w, g, m, v):
    m2 = ADAM_B1 * m + (1.0 - ADAM_B1) * g
    v2 = ADAM_B2 * v + (1.0 - ADAM_B2) * (g * g)
    m_hat = m2 / (1.0 - ADAM_B1 ** ADAM_STEP)
    v_hat = v2 / (1.0 - ADAM_B2 ** ADAM_STEP)
    delta = -ADAM_LR * (m_hat / (jnp.sqrt(v_hat) + ADAM_EPS) + ADAM_WD * w)
    return delta, m2, v2


def _adamw_halves(name, items, tm, carried=None):
    rows, cols = items[0][0].shape
    nb = rows // 2 // tm
    n = len(items)

    def body(*refs):
        mine = (pl.program_id(0) // nb) == lax.axis_index("c")
        for k in range(n):
            w_ref, own_ref, oth_ref, m_ref, v_ref = refs[5 * k:5 * k + 5]
            g_ref, d_ref, m2_ref, v2_ref = refs[5 * n + 4 * k:5 * n + 4 * k + 4]
            g = jnp.where(mine, own_ref[...], oth_ref[...])
            d, m2, v2 = _adam_math(w_ref[...], g, m_ref[...], v_ref[...])
            g_ref[...] = g
            d_ref[...] = d
            m2_ref[...] = m2
            v2_ref[...] = v2

    full = pl.BlockSpec((tm, cols), lambda i: (i, 0))
    half = pl.BlockSpec((tm, cols), lambda i: (i % nb, 0))
    return _call_carrying(
        body, carried, name=name, grid=(rows // tm,), in_specs=[full, half, half, full, full] * n,
        out_specs=[full] * (4 * n), out_shape=tuple([_sds((rows, cols), F32)] * (4 * n)), scratch_shapes=[],
        operands=[a for item in items for a in item])


def _adamw(name, w, g, m, v, tm):
    def body(w_ref, g_ref, m_ref, v_ref, gout_ref, d_ref, m2_ref, v2_ref):
        gv = g_ref[...]
        d, m2, v2 = _adam_math(w_ref[...], gv, m_ref[...], v_ref[...])
        gout_ref[...] = gv
        d_ref[...] = d
        m2_ref[...] = m2
        v2_ref[...] = v2

    return _rows_call(name, body, tm, [w, g, m, v], [], [_sds(w.shape, F32)] * 4)


_SMALL_NAMES = ("norm_mix_pre", "gm_ln_w", "gm_ln_b", "gm_w_s", "gm_b_s", "conv_w", "conv_b", "dt_bias", "a_log",
                "d_skip", "ssm_norm_w", "norm_mix_post", "norm_ffn_pre", "norm_ffn_post")
_PACK_COLS = 1024


def _pack(parts, names=_SMALL_NAMES, tail=None):
    pieces = [parts[n].reshape(-1) for n in names]
    flat = jnp.concatenate(pieces if tail is None else pieces + [tail])
    rows = -(-flat.shape[0] // (8 * _PACK_COLS)) * 8
    flat = jnp.pad(flat, (0, rows * _PACK_COLS - flat.shape[0]))
    return flat.reshape(rows, _PACK_COLS)


def _unpack(packed, shapes, names=_SMALL_NAMES):
    flat = packed.reshape(-1)
    out, off = {}, 0
    for n in names:
        size = 1
        for s in shapes[n]:
            size *= s
        out[n] = flat[off:off + size].reshape(shapes[n])
        off += size
    return out


def kernel(x, norm_mix_pre, w_in, gm_ln_w, gm_ln_b, gm_w_s, gm_b_s, conv_w, conv_b, dt_bias, a_log, d_skip, ssm_norm_w, w_out, norm_mix_post, norm_ffn_pre, w_up, w_down, norm_ffn_post, loss_target, m_norm_mix_pre, m_w_in, m_gm_ln_w, m_gm_ln_b, m_gm_w_s, m_gm_b_s, m_conv_w, m_conv_b, m_dt_bias, m_a_log, m_d_skip, m_ssm_norm_w, m_w_out, m_norm_mix_post, m_norm_ffn_pre, m_w_up, m_w_down, m_norm_ffn_post, v_norm_mix_pre, v_w_in, v_gm_ln_w, v_gm_ln_b, v_gm_w_s, v_gm_b_s, v_conv_w, v_conv_b, v_dt_bias, v_a_log, v_d_skip, v_ssm_norm_w, v_w_out, v_norm_mix_post, v_norm_ffn_pre, v_w_up, v_w_down, v_norm_ffn_post):
    params = dict(norm_mix_pre=norm_mix_pre, w_in=w_in, gm_ln_w=gm_ln_w, gm_ln_b=gm_ln_b, gm_w_s=gm_w_s, gm_b_s=gm_b_s,
                  conv_w=conv_w, conv_b=conv_b, dt_bias=dt_bias, a_log=a_log, d_skip=d_skip, ssm_norm_w=ssm_norm_w,
                  w_out=w_out, norm_mix_post=norm_mix_post, norm_ffn_pre=norm_ffn_pre, w_up=w_up, w_down=w_down,
                  norm_ffn_post=norm_ffn_post)
    mom1 = dict(norm_mix_pre=m_norm_mix_pre, w_in=m_w_in, gm_ln_w=m_gm_ln_w, gm_ln_b=m_gm_ln_b, gm_w_s=m_gm_w_s,
                gm_b_s=m_gm_b_s, conv_w=m_conv_w, conv_b=m_conv_b, dt_bias=m_dt_bias, a_log=m_a_log, d_skip=m_d_skip,
                ssm_norm_w=m_ssm_norm_w, w_out=m_w_out, norm_mix_post=m_norm_mix_post, norm_ffn_pre=m_norm_ffn_pre,
                w_up=m_w_up, w_down=m_w_down, norm_ffn_post=m_norm_ffn_post)
    mom2 = dict(norm_mix_pre=v_norm_mix_pre, w_in=v_w_in, gm_ln_w=v_gm_ln_w, gm_ln_b=v_gm_ln_b, gm_w_s=v_gm_w_s,
                gm_b_s=v_gm_b_s, conv_w=v_conv_w, conv_b=v_conv_b, dt_bias=v_dt_bias, a_log=v_a_log, d_skip=v_d_skip,
                ssm_norm_w=v_ssm_norm_w, w_out=v_w_out, norm_mix_post=v_norm_mix_post, norm_ffn_pre=v_norm_ffn_pre,
                w_up=v_w_up, w_down=v_w_down, norm_ffn_post=v_norm_ffn_post)
    names = list(params)
    big = ("w_in", "w_out", "w_up", "w_down")
    chip = 2 * lax.axis_index("x") + lax.axis_index("y")

    shards = {n: params[n][0].astype(BF16) for n in big}
    conv_shard = jnp.pad(conv_w[0], ((0, 16 - CONV_K), (0, 0)))
    g_in4, g_conv4 = _run_exchange("allgather_w_in", _allgather_exchange([shards["w_in"], conv_shard]))
    conv_full = jnp.transpose(g_conv4[:, :CONV_K, :], (1, 0, 2)).reshape(CONV_K, CONV_CH)

    small = {n: params[n][0] if params[n].ndim >= 3 else params[n] for n in _SMALL_NAMES if n != "conv_w"}
    core = lax.axis_index("c").astype(jnp.int32).reshape(1)
    adam_args = {n: (params[n][0], mom1[n][0], mom2[n][0]) for n in big}
    loss, grad_x, big_out, small_sum = _forward_backward(
        x, loss_target, g_in4, conv_full, small, shards["w_out"], shards["w_up"], shards["w_down"], core, adam_args)
    grads, delta, new_m, new_v = {}, {}, {}, {}
    for n in big:
        grads[n], delta[n], new_m[n], new_v[n] = [a[None] for a in big_out[n]]

    small_sum["conv_w"] = lax.dynamic_slice_in_dim(small_sum["conv_w"], chip * (CONV_CH // N_CHIPS), CONV_CH // N_CHIPS, axis=1)

    local_shapes = {n: params[n].shape[1:] if params[n].ndim >= 3 else params[n].shape for n in _SMALL_NAMES}
    flat = lambda tree: {n: tree[n].reshape(local_shapes[n]) for n in _SMALL_NAMES}
    packed = [_pack(flat(t)) for t in (params, small_sum, mom1, mom2)]
    _, d_p, m_p, v_p = _adamw("adamw_small", *packed, packed[0].shape[0])
    for src, dst in ((d_p, delta), (m_p, new_m), (v_p, new_v)):
        for n, val in _unpack(src, local_shapes).items():
            dst[n] = val.reshape(params[n].shape)
    for n in _SMALL_NAMES:
        grads[n] = small_sum[n].reshape(params[n].shape)

    out = [loss, grad_x]
    for tree in (grads, delta, new_m, new_v):
        out += [tree[n] for n in names]
    return tuple(out)
```

```python
import functools

import jax
import jax.numpy as jnp
from jax import lax
from jax.experimental import pallas as pl
from jax.experimental.pallas import tpu as pltpu

F32 = jnp.float32
BF16 = jnp.bfloat16
MESH = pl.DeviceIdType.MESH

EPS = 1e-6
D_MODEL = 1024
GM_WIDTH = 512
SSM_WIDTH = 512
N_HEADS = 8
HEAD_DIM = 64
CHUNK = 128
SSM_GROUPS = 2
GROUP_W = SSM_WIDTH // SSM_GROUPS
SSM_STATE = 128
CONV_K = 4
CONV_CH = 1024
D_FF = 4096
IN_COLS = 2568
DT_PAD = 128
N_CHIPS = 4
N_DEV = 8

ADAM_LR = 0.001
ADAM_B1 = 0.9
ADAM_B2 = 0.999
ADAM_EPS = 1e-08
ADAM_WD = 0.01
ADAM_STEP = 10

VMEM_LIMIT_BYTES = 56 * 1024 * 1024
FF_TILE = 512
DW_TOKENS_PER_STEP = 2048


def _cparams(n_axes):
    return pltpu.CompilerParams(dimension_semantics=("arbitrary",) * n_axes, vmem_limit_bytes=VMEM_LIMIT_BYTES)


def _dot(a, b):
    return jnp.dot(a.astype(BF16), b.astype(BF16), preferred_element_type=F32)


def _dot_nt(a, b):
    return lax.dot_general(a.astype(BF16), b.astype(BF16), (((1,), (1,)), ((), ())), preferred_element_type=F32)


def _dot_tn(a, b):
    return lax.dot_general(a.astype(BF16), b.astype(BF16), (((0,), (0,)), ((), ())), preferred_element_type=F32)


def _sigmoid(x):
    return 1.0 / (1.0 + jnp.exp(-x))


_GELU_C = 0.7978845608028654
_GELU_A = 0.044715


def _gelu(x):
    t = jnp.tanh(_GELU_C * (x + _GELU_A * (x * x * x)))
    return 0.5 * x * (1.0 + t), t


def _gelu_grad(x, t):
    return 0.5 * (1.0 + t) + 0.5 * x * (1.0 - t * t) * (_GELU_C * (1.0 + 3.0 * _GELU_A * x * x))


def _rms_fwd(x, w):
    r = lax.rsqrt(jnp.mean(x * x, axis=-1, keepdims=True) + EPS)
    return x * r * w, r


def _rms_bwd(x, r, w, dy):
    g = dy * w
    dx = r * g - x * (r * r * r) * jnp.mean(g * x, axis=-1, keepdims=True)
    dw = jnp.sum(dy * x * r, axis=0, keepdims=True)
    return dx, dw


class _Carried:
    def __init__(self, ins, out_shapes, sems, start, finish):
        self.ins, self.out_shapes, self.sems = list(ins), list(out_shapes), list(sems)
        self.start, self.finish = start, finish


def _both(first, second):
    n_i, n_o, n_s = len(first.ins), len(first.out_shapes), len(first.sems)

    def split(ins, outs, sems):
        return (ins[:n_i], outs[:n_o], sems[:n_s]), (ins[n_i:], outs[n_o:], sems[n_s:])

    def start(ins, outs, *sems):
        (i1, o1, s1), (i2, o2, s2) = split(ins, outs, sems)
        first.start(i1, o1, *s1)
        second.start(i2, o2, *s2)

    def finish(ins, outs, *sems):
        (i1, o1, s1), (i2, o2, s2) = split(ins, outs, sems)
        first.finish(i1, o1, *s1)
        second.finish(i2, o2, *s2)

    return _Carried(first.ins + second.ins, first.out_shapes + second.out_shapes, first.sems + second.sems, start, finish)


def _split_carried(refs, n_in, n_out, n_scratch, carried):
    n_ci, n_co, n_cs = len(carried.ins), len(carried.out_shapes), len(carried.sems)
    ins, rest = refs[:n_in], refs[n_in:]
    c_ins, rest = rest[:n_ci], rest[n_ci:]
    outs, rest = rest[:n_out], rest[n_out:]
    c_outs, rest = rest[:n_co], rest[n_co:]
    scr, c_sems = rest[:n_scratch], rest[n_scratch:]
    assert len(c_sems) == n_cs
    return tuple(ins) + tuple(outs) + tuple(scr), c_ins, c_outs, c_sems


def _rows_call(name, body, tm, row_ins, const_ins, row_outs, acc_outs=(), scratch=(), carried=None):
    n_rows = row_ins[0].shape[0]
    assert n_rows % tm == 0
    n_steps = n_rows // tm
    n_in = len(row_ins) + len(const_ins)
    n_ro = len(row_outs)
    n_acc = len(acc_outs)

    def kern(*refs):
        accs = refs[n_in + n_ro:n_in + n_ro + n_acc]

        @pl.when(pl.program_id(0) == 0)
        def _():
            for a in accs:
                a[...] = jnp.zeros_like(a)

        body(*refs)

    def whole(shape):
        nd = len(shape)
        return pl.BlockSpec(tuple(shape), lambda i: (0,) * nd)

    in_specs = [pl.BlockSpec((tm, a.shape[1]), lambda i: (i, 0)) for a in row_ins]
    in_specs += [whole(a.shape) for a in const_ins]
    out_specs = [pl.BlockSpec((tm, s.shape[1]), lambda i: (i, 0)) for s in row_outs]
    out_specs += [whole(s.shape) for s in acc_outs]
    return _call_carrying(
        kern, carried, name=name, grid=(n_steps,), in_specs=in_specs, out_specs=out_specs,
        out_shape=tuple(row_outs) + tuple(acc_outs), scratch_shapes=list(scratch), operands=list(row_ins) + list(const_ins))


def _call_carrying(body, carried, *, name, grid, in_specs, out_specs, out_shape, scratch_shapes, operands):
    n_in, n_out, n_scratch = len(in_specs), len(out_specs), len(scratch_shapes)
    kern = body
    if carried is not None:
        def kern(*refs):
            plain, c_ins, c_outs, c_sems = _split_carried(refs, n_in, n_out, n_scratch, carried)
            first, last = True, True
            for d, size in enumerate(grid):
                first = jnp.logical_and(first, pl.program_id(d) == 0)
                last = jnp.logical_and(last, pl.program_id(d) == size - 1)

            @pl.when(first)
            def _():
                carried.start(c_ins, c_outs, *c_sems)

            body(*plain)

            @pl.when(last)
            def _():
                carried.finish(c_ins, c_outs, *c_sems)

        in_specs = list(in_specs) + [_HBM] * len(carried.ins)
        out_specs = list(out_specs) + [_HBM] * len(carried.out_shapes)
        out_shape = tuple(out_shape) + tuple(carried.out_shapes)
        operands = list(operands) + carried.ins
        scratch_shapes = list(scratch_shapes) + carried.sems
    return pl.pallas_call(
        kern, name=name, grid=grid, in_specs=in_specs, out_specs=out_specs, out_shape=out_shape,
        scratch_shapes=scratch_shapes, compiler_params=_cparams(len(grid)),
    )(*operands)


def _sds(shape, dtype):
    return jax.ShapeDtypeStruct(tuple(shape), dtype)


def _matmul_tn(name, a, b, tm, tn, tk, stacked=False, carried=None):
    k_dim, m_dim = a.shape
    n_dim = b.shape[1]
    assert m_dim % tm == 0 and n_dim % tn == 0 and k_dim % tk == 0
    nk = k_dim // tk

    def kern(a_ref, b_ref, o_ref, acc_ref):
        k = pl.program_id(2)
        prod = _dot_tn(a_ref[...], b_ref[...])

        @pl.when(k == 0)
        def _():
            acc_ref[...] = prod

        @pl.when(k > 0)
        def _():
            acc_ref[...] += prod

        @pl.when(k == nk - 1)
        def _():
            o_ref[...] = acc_ref[...].astype(o_ref.dtype)

    if stacked:
        assert tm == m_dim
        out_shape = _sds((n_dim // tn, m_dim, tn), BF16)
        out_spec = pl.BlockSpec((None, tm, tn), lambda i, j, k: (j, i, 0))
    else:
        out_shape = _sds((m_dim, n_dim), BF16)
        out_spec = pl.BlockSpec((tm, tn), lambda i, j, k: (i, j))
    outs = _call_carrying(
        kern, carried, name=name, grid=(m_dim // tm, n_dim // tn, nk),
        in_specs=[pl.BlockSpec((tk, tm), lambda i, j, k: (k, i)), pl.BlockSpec((tk, tn), lambda i, j, k: (k, j))],
        out_specs=[out_spec], out_shape=(out_shape,), scratch_shapes=[pltpu.VMEM((tm, tn), F32)], operands=[a, b])
    return outs[0] if carried is None else outs


def _inproj_fwd(x, nw, w_uv, w_xbc, w_z, w_dt, tm=256, carried=None):
    n_tok = x.shape[0]

    def body(x_ref, nw_ref, wuv_ref, wxbc_ref, wz_ref, wdt_ref, puv_ref, pxbc_ref, pz_ref, pdt_ref):
        h, _ = _rms_fwd(x_ref[...], nw_ref[...])
        h = h.astype(BF16)
        puv_ref[...] = jnp.dot(h, wuv_ref[...], preferred_element_type=F32)
        pxbc_ref[...] = jnp.dot(h, wxbc_ref[...], preferred_element_type=F32)
        pz_ref[...] = jnp.dot(h, wz_ref[...], preferred_element_type=F32)
        pdt_ref[...] = jnp.dot(h, wdt_ref[...], preferred_element_type=F32)

    return _rows_call(
        "inproj_fwd", body, tm, [x], [nw, w_uv, w_xbc, w_z, w_dt],
        [_sds((n_tok, 2 * GM_WIDTH), F32), _sds((n_tok, CONV_CH), F32), _sds((n_tok, SSM_WIDTH), F32),
         _sds((n_tok, DT_PAD), F32)], carried=carried)


def _head_lane_mask(width, head):
    lane = lax.broadcasted_iota(jnp.int32, (1, width), 1)
    return (lane // HEAD_DIM) == head


def _split_terms(x, terms):
    parts = []
    for _ in range(terms):
        p = x.astype(BF16)
        parts.append(p)
        x = x - p.astype(F32)
    return parts


def _seg_dots(vals, ind, terms=2):
    m = vals[0].shape[0]
    parts = []
    for v in vals:
        parts += _split_terms(v, terms)
    red = jnp.dot(jnp.concatenate(parts, axis=0), ind, preferred_element_type=F32)
    outs = []
    for i in range(len(vals)):
        acc = red[i * terms * m:(i * terms + 1) * m]
        for t in range(1, terms):
            acc = acc + red[(i * terms + t) * m:(i * terms + t + 1) * m]
        outs.append(acc)
    return outs


def _tri_dot(mask, x, terms=3):
    n = x.shape[1]
    red = jnp.dot(mask.astype(BF16), jnp.concatenate(_split_terms(x, terms), axis=1), preferred_element_type=F32)
    acc = red[:, :n]
    for t in range(1, terms):
        acc = acc + red[:, t * n:(t + 1) * n]
    return acc


def _gmlp_common(puv, lnw, lnb, e_bf, et_bf):
    u = puv[:, :GM_WIDTH]
    v = puv[:, GM_WIDTH:]
    gu, tu = _gelu(u)
    gv, tv = _gelu(v)
    (s1,) = _seg_dots([gv], et_bf)
    (mu,) = _seg_dots([s1 * (1.0 / HEAD_DIM)], e_bf)
    xc = gv - mu
    (s2,) = _seg_dots([xc * xc], et_bf)
    (rstd,) = _seg_dots([lax.rsqrt(s2 * (1.0 / HEAD_DIM) + EPS)], e_bf)
    xhat = xc * rstd
    vn = xhat * lnw + lnb
    return u, v, gu, tu, tv, rstd, xhat, vn


def _tril_mask():
    r = lax.broadcasted_iota(jnp.int32, (CHUNK, CHUNK), 0)
    c = lax.broadcasted_iota(jnp.int32, (CHUNK, CHUNK), 1)
    return r >= c


def _head_blocks(v):
    return jnp.concatenate([jnp.where(_head_lane_mask(GM_WIDTH, h), v, jnp.zeros_like(v)) for h in range(N_HEADS)], axis=0)


def _causal_w_cat(w_cat):
    t = lax.broadcasted_iota(jnp.int32, (CHUNK, N_HEADS * CHUNK), 0)
    s = lax.broadcasted_iota(jnp.int32, (CHUNK, N_HEADS * CHUNK), 1) % CHUNK
    return jnp.where(t >= s, w_cat, 0.0).astype(BF16)


def _gmlp_chunk_fwd(puv, lnw, lnb, e_bf, et_bf, wm, bmap):
    _, _, gu, _, _, _, _, vn = _gmlp_common(puv, lnw, lnb, e_bf, et_bf)
    mixed = jnp.dot(wm, _head_blocks(vn.astype(BF16)), preferred_element_type=F32) + bmap
    return (gu * mixed).astype(BF16)


SUBLANES = 8


def _shift_down(x, tail, s):
    main = pltpu.roll(x, s, 0)
    row = lax.broadcasted_iota(jnp.int32, (SUBLANES, 1), 0)
    head = jnp.where(row < s, pltpu.roll(tail, s, 0), main[:SUBLANES])
    return jnp.concatenate([head, main[SUBLANES:]], axis=0)


def _shift_up(x, head_next, s):
    n = x.shape[0]
    main = pltpu.roll(x, n - s, 0)
    row = lax.broadcasted_iota(jnp.int32, (SUBLANES, 1), 0)
    last = jnp.where(row >= SUBLANES - s, pltpu.roll(head_next, SUBLANES - s, 0), main[n - SUBLANES:])
    return jnp.concatenate([main[:n - SUBLANES], last], axis=0)


def _ssd_pre(xr, tail, cw_ref, cb, pdt, dtb, alog, emap):
    rowi = lax.broadcasted_iota(jnp.int32, (CHUNK, 1), 0)
    shifted = [_shift_down(xr, tail, 3), _shift_down(xr, tail, 2), _shift_down(xr, tail, 1), xr]
    xc = cb
    for k in range(CONV_K):
        xc = xc + cw_ref[k] * shifted[k]
    sg = _sigmoid(xc)
    xa = xc * sg
    pre = pdt + dtb
    dt = jnp.maximum(pre, 0.0) + jnp.log(1.0 + jnp.exp(-jnp.abs(pre)))
    a_neg = -jnp.exp(alog)
    a_cs = _tri_dot(_tril_mask(), dt * a_neg)
    acs_map, dt_map = _seg_dots([a_cs, dt], emap, terms=3)
    return dict(shifted=shifted, xc=xc, sg=sg, xa=xa, pre=pre, dt=dt, a_neg=a_neg, a_cs=a_cs,
                acs_map=acs_map, dt_map=dt_map, rowi=rowi)


def _ssd_maps(p):
    last = p["rowi"] == CHUNK - 1
    aq_map = jnp.sum(jnp.where(last, p["acs_map"], 0.0), axis=0, keepdims=True)
    e_exp = jnp.exp(p["acs_map"])
    dte = jnp.exp(aq_map - p["acs_map"])
    cd = jnp.exp(aq_map)
    return last, e_exp, dte, cd


def _head_decay(a_cs, a_cs_t, head, tri):
    lane = lax.broadcasted_iota(jnp.int32, (1, DT_PAD), 1)
    sub = lax.broadcasted_iota(jnp.int32, (DT_PAD, 1), 0)
    col = jnp.sum(jnp.where(lane == head, a_cs, 0.0), axis=1, keepdims=True)
    row = jnp.sum(jnp.where(sub == head, a_cs_t, 0.0), axis=0, keepdims=True)
    return jnp.exp(jnp.where(tri, col - row, -1e30))


def _gate_fwd(y, z, nw):
    sz = _sigmoid(z)
    zg = z * sz
    yg = y * zg
    outs, rs = [], []
    for g in range(SSM_GROUPS):
        gs = slice(g * GROUP_W, (g + 1) * GROUP_W)
        o, r = _rms_fwd(yg[:, gs], nw[:, gs])
        outs.append(o)
        rs.append(r)
    return sz, zg, yg, outs, rs


def _ssd_const_specs():
    def whole(shape):
        nd = len(shape)
        return pl.BlockSpec(tuple(shape), lambda c: (0,) * nd)
    return [whole((CONV_K, 1, CONV_CH)), whole((1, CONV_CH)), whole((1, DT_PAD)), whole((1, DT_PAD)),
            whole((1, SSM_WIDTH)), whole((1, SSM_WIDTH)), whole((DT_PAD, SSM_WIDTH)), whole((SSM_WIDTH, DT_PAD))]


def _mixer_fwd(p_uv, p_xbc, p_z, p_dt, x, lnw, lnb, w_cat, bmap, w_out, nw_post, nw_pre2, conv_w, conv_b, dt_bias, a_log,
               dskip_map, norm_w, e_bf, et_bf, n_seq, carried=None):
    n_tok = p_xbc.shape[0]
    nc = n_tok // n_seq // CHUNK

    def body(puv3, xr3, z3, pdt3, x3, lnw_ref, lnb_ref, wcat_ref, bmap_ref, wo_ref, nwa_ref, nwb_ref,
             cw_ref, cb_ref, dtb_ref, alog_ref, dsk_ref, nw_ref, e_ref, et_ref,
             mix3, yssd3, sprev3, o3, x13, h23, wm_scr, prev3_scr, s3_scr):
        @pl.when(pl.program_id(0) == 0)
        def _():
            wm_scr[...] = _causal_w_cat(wcat_ref[...])
            prev3_scr[...] = jnp.zeros_like(prev3_scr)
            s3_scr[...] = jnp.zeros_like(s3_scr)

        for b in range(n_seq):
            one_sequence(puv3.at[b], xr3.at[b], z3.at[b], pdt3.at[b], lnw_ref, lnb_ref, bmap_ref,
                         cw_ref, cb_ref, dtb_ref, alog_ref, dsk_ref, nw_ref, e_ref, et_ref,
                         mix3.at[b], yssd3.at[b], sprev3.at[b], wm_scr, prev3_scr.at[b], s3_scr.at[b])
            o = jnp.dot(mix3[b], wo_ref[...], preferred_element_type=F32)
            on, _ = _rms_fwd(o, nwa_ref[...])
            x1 = x3[b] + on
            h2, _ = _rms_fwd(x1, nwb_ref[...])
            o3[b] = o
            x13[b] = x1
            h23[b] = h2.astype(BF16)

    def one_sequence(puv_ref, xr_ref, z_ref, pdt_ref, lnw_ref, lnb_ref, bmap_ref,
                     cw_ref, cb_ref, dtb_ref, alog_ref, dsk_ref, nw_ref, e_ref, et_ref,
                     mix_ref, yssd_ref, sprev_ref, wm_scr, prev_scr, s_scr):
        mix_ref[:, :GM_WIDTH] = _gmlp_chunk_fwd(puv_ref[...], lnw_ref[...], lnb_ref[...], e_ref[...], et_ref[...], wm_scr[...],
                                      bmap_ref[...])
        xr = xr_ref[...]
        p = _ssd_pre(xr, prev_scr[...], cw_ref, cb_ref[...], pdt_ref[...], dtb_ref[...], alog_ref[...], e_ref[...])
        _, e_exp, dte, cd = _ssd_maps(p)
        xs = p["xa"][:, :SSM_WIDTH]
        xd = xs * p["dt_map"]
        a_cs_t = p["a_cs"].T
        tri = _tril_mask()
        s_old = s_scr[...]
        sprev_ref[...] = s_old
        for g in range(SSM_GROUPS):
            gs = slice(g * GROUP_W, (g + 1) * GROUP_W)
            bm = p["xa"][:, SSM_WIDTH + g * SSM_STATE: SSM_WIDTH + (g + 1) * SSM_STATE].astype(BF16)
            cm = p["xa"][:, SSM_WIDTH + (SSM_GROUPS + g) * SSM_STATE: SSM_WIDTH + (SSM_GROUPS + g + 1) * SSM_STATE].astype(BF16)
            cb_mat = _dot_nt(cm, bm)
            xdg = xd[:, gs].astype(BF16)
            y_g = _dot(cm, s_old[:, gs]) * e_exp[:, gs] + dsk_ref[:, gs] * xs[:, gs]
            for r in range(SSM_GROUPS * 2):
                dm = _head_decay(p["a_cs"], a_cs_t, g * 4 + r, tri)
                full = jnp.dot((cb_mat * dm).astype(BF16), xdg, preferred_element_type=F32)
                y_g = y_g + jnp.where(_head_lane_mask(GROUP_W, r), full, 0.0)
            yssd_ref[:, gs] = y_g
            s_scr[:, gs] = cd[:, gs] * s_old[:, gs] + _dot_tn(bm, xd[:, gs] * dte[:, gs])
        _, _, _, outs, _ = _gate_fwd(yssd_ref[...], z_ref[...], nw_ref[...])
        for g in range(SSM_GROUPS):
            mix_ref[:, GM_WIDTH + g * GROUP_W:GM_WIDTH + (g + 1) * GROUP_W] = outs[g].astype(BF16)
        prev_scr[...] = xr[CHUNK - SUBLANES:, :]

    seq_len = n_tok // n_seq

    def rows(width):
        return pl.BlockSpec((n_seq, CHUNK, width), lambda c: (0, c, 0))

    def whole(shape):
        nd = len(shape)
        return pl.BlockSpec(tuple(shape), lambda c: (0,) * nd)

    def by_seq(a):
        return a.reshape(n_seq, seq_len, a.shape[-1])

    outs = _call_carrying(
        body, carried, name="mixer_fwd", grid=(nc,),
        in_specs=[rows(2 * GM_WIDTH), rows(CONV_CH), rows(SSM_WIDTH), rows(DT_PAD), rows(D_MODEL), whole(lnw.shape),
                  whole(lnb.shape), whole(w_cat.shape), whole(bmap.shape), whole(w_out.shape), whole(nw_post.shape),
                  whole(nw_pre2.shape)] + _ssd_const_specs(),
        out_specs=[rows(D_MODEL), rows(SSM_WIDTH), rows(SSM_WIDTH), rows(D_MODEL), rows(D_MODEL), rows(D_MODEL)],
        out_shape=(_sds((n_seq, seq_len, D_MODEL), BF16),
                   _sds((n_seq, seq_len, SSM_WIDTH), F32), _sds((n_seq, seq_len, SSM_WIDTH), F32),
                   _sds((n_seq, seq_len, D_MODEL), F32), _sds((n_seq, seq_len, D_MODEL), F32),
                   _sds((n_seq, seq_len, D_MODEL), BF16)),
        scratch_shapes=[pltpu.VMEM((CHUNK, N_HEADS * CHUNK), BF16), pltpu.VMEM((n_seq, SUBLANES, CONV_CH), F32),
                        pltpu.VMEM((n_seq, SSM_STATE, SSM_WIDTH), F32)],
        operands=[by_seq(p_uv), by_seq(p_xbc), by_seq(p_z), by_seq(p_dt), by_seq(x), lnw, lnb, w_cat, bmap, w_out, nw_post,
                  nw_pre2, conv_w, conv_b, dt_bias, a_log, dskip_map, norm_w, e_bf, et_bf])
    return tuple(o.reshape(n_tok, o.shape[-1]) for o in outs[:6]) + tuple(outs[6:])


def _up_cols(wup_ref, j):
    per = (D_FF // N_CHIPS) // FF_TILE
    return wup_ref[j // per, :, (j % per) * FF_TILE:(j % per + 1) * FF_TILE]


def _down_rows(wda_ref, wdb_ref, j):
    assert 2 * FF_TILE == D_FF // N_CHIPS
    return (wda_ref if j % 2 == 0 else wdb_ref)[j // 2]


def _skewed_rows_call(name, main, tail, tm, lead_ins, lag_ins, const_ins, lead_outs, lag_outs, acc_outs, carry,
                      streamed, tile_copies, n_copies):
    n_rows = lead_ins[0].shape[0]
    assert n_rows % tm == 0
    n = n_rows // tm
    counts = [len(lead_ins), len(lag_ins), len(const_ins), len(streamed), len(lead_outs), len(lag_outs), len(acc_outs),
              1, len(streamed)]

    def kern(*refs):
        groups, pos = [], 0
        for cnt in counts:
            groups.append(refs[pos:pos + cnt])
            pos += cnt
        lead_i, lag_i, consts, w_hbm, lead_o, lag_o, accs, (carry_scr,), w_vmem = groups
        sems = refs[pos]
        i = pl.program_id(0)
        pieces, k = [], 0
        for piece in tile_copies(w_hbm, w_vmem):
            pieces.append([pltpu.make_async_copy(src, dst, sems.at[k + q]) for q, (src, dst) in enumerate(piece)])
            k += len(piece)

        def ready(j):
            for cp in pieces[j]:
                cp.wait()

        @pl.when(i == 0)
        def _():
            for piece in pieces:
                for cp in piece:
                    cp.start()
            for a in accs:
                a[...] = jnp.zeros_like(a)
            carry_scr[...] = main(lead_i, consts, lead_o, w_vmem, ready)

        @pl.when(jnp.logical_and(i > 0, i < n))
        def _():
            previous = carry_scr[...]
            carry_scr[...] = main(lead_i, consts, lead_o, w_vmem, lambda j: None)
            tail(previous, lag_i, consts, lag_o, accs)

        @pl.when(i == n)
        def _():
            tail(carry_scr[...], lag_i, consts, lag_o, accs)

    def lead(width):
        return pl.BlockSpec((tm, width), lambda i: (jnp.minimum(i, n - 1), 0))

    def lag(width):
        return pl.BlockSpec((tm, width), lambda i: (jnp.maximum(i - 1, 0), 0))

    def whole(shape):
        nd = len(shape)
        return pl.BlockSpec(tuple(shape), lambda i: (0,) * nd)

    return pl.pallas_call(
        kern, name=name, grid=(n + 1,),
        in_specs=([lead(a.shape[1]) for a in lead_ins] + [lag(a.shape[1]) for a in lag_ins]
                  + [whole(a.shape) for a in const_ins] + [_HBM] * len(streamed)),
        out_specs=[lead(s.shape[1]) for s in lead_outs] + [lag(s.shape[1]) for s in lag_outs] + [whole(s.shape) for s in acc_outs],
        out_shape=tuple(lead_outs) + tuple(lag_outs) + tuple(acc_outs),
        scratch_shapes=([pltpu.VMEM(carry, F32)] + [pltpu.VMEM(a.shape, a.dtype) for a in streamed]
                        + [pltpu.SemaphoreType.DMA((n_copies,))]),
        compiler_params=_cparams(1),
    )(*lead_ins, *lag_ins, *const_ins, *streamed)


def _mlp_weight_pieces(order):
    per = (D_FF // N_CHIPS) // FF_TILE

    def tile_copies(hbm, vmem):
        pieces = []
        for j in range(D_FF // FF_TILE):
            cols = (j // per, slice(None), pl.ds((j % per) * FF_TILE, FF_TILE))
            up = (hbm[0].at[cols], vmem[0].at[cols])
            down = (hbm[1 + j % 2].at[j // 2], vmem[1 + j % 2].at[j // 2])
            pieces.append([up, down] if order == "up_down" else [down, up])
        return pieces

    return tile_copies


def _mlp_fwd(h2, x1, tgt, w_up, w_down_a, w_down_b, nw, tm=512):
    n_tok = x1.shape[0]

    def main(lead_i, consts, lead_o, weights, ready):
        (h2_ref,), (f_ref,), (wup_ref, wda_ref, wdb_ref) = lead_i, lead_o, weights
        h2v = h2_ref[...]
        acc = jnp.zeros((tm, D_MODEL), F32)
        for j in range(D_FF // FF_TILE):
            cs = slice(j * FF_TILE, (j + 1) * FF_TILE)
            ready(j)
            u = jnp.dot(h2v, _up_cols(wup_ref, j), preferred_element_type=F32)
            f = jnp.square(jnp.maximum(u, 0.0)).astype(BF16)
            f_ref[:, cs] = f
            acc = acc + jnp.dot(f, _down_rows(wda_ref, wdb_ref, j), preferred_element_type=F32)
        return acc

    def tail(acc, lag_i, consts, lag_o, accs):
        (x1_ref, tgt_ref), (nw_ref,), (dd_ref, dy_ref), (loss_ref, dnw_ref) = lag_i, consts, lag_o, accs
        dn, r = _rms_fwd(acc, nw_ref[...])
        e = x1_ref[...] + dn - tgt_ref[...]
        loss_ref[...] += jnp.full(loss_ref.shape, (0.5 / D_MODEL) * jnp.sum(e * e), F32)
        dy = e * (1.0 / D_MODEL)
        dd, dnw = _rms_bwd(acc, r, nw_ref[...], dy)
        dy_ref[...] = dy
        dd_ref[...] = dd.astype(BF16)
        dnw_ref[...] += dnw

    return _skewed_rows_call(
        "mlp_fwd", main, tail, tm, [h2], [x1, tgt], [nw],
        [_sds((n_tok, D_FF), BF16)], [_sds((n_tok, D_MODEL), BF16), _sds((n_tok, D_MODEL), F32)],
        [_sds((8, 128), F32), _sds((1, D_MODEL), F32)], carry=(tm, D_MODEL),
        streamed=[w_up, w_down_a, w_down_b], tile_copies=_mlp_weight_pieces("up_down"), n_copies=2 * (D_FF // FF_TILE))


def _mlp_bwd(dd, f, x1, dy, w_down_a, w_down_b, w_up, nw, tm=256):
    n_tok = x1.shape[0]

    def main(lead_i, consts, lead_o, weights, ready):
        (dd_ref, f_ref), (dup_ref,), (wup_ref, wda_ref, wdb_ref) = lead_i, lead_o, weights
        ddv = dd_ref[...]
        acc = jnp.zeros((tm, D_MODEL), F32)
        for j in range(D_FF // FF_TILE):
            cs = slice(j * FF_TILE, (j + 1) * FF_TILE)
            ready(j)
            df = _dot_nt(ddv, _down_rows(wda_ref, wdb_ref, j))
            du = (df * (2.0 * jnp.sqrt(f_ref[:, cs].astype(F32)))).astype(BF16)
            dup_ref[:, cs] = du
            acc = acc + _dot_nt(du, _up_cols(wup_ref, j))
        return acc

    def tail(acc, lag_i, consts, lag_o, accs):
        (x1_ref, dy_ref), (nw_ref,), (dx1_ref,), (dnw_ref,) = lag_i, consts, lag_o, accs
        x1v = x1_ref[...]
        _, r = _rms_fwd(x1v, nw_ref[...])
        dx, dnw = _rms_bwd(x1v, r, nw_ref[...], acc)
        dx1_ref[...] = dy_ref[...] + dx
        dnw_ref[...] += dnw

    return _skewed_rows_call(
        "mlp_bwd", main, tail, tm, [dd, f], [x1, dy], [nw],
        [_sds((n_tok, D_FF), BF16)], [_sds((n_tok, D_MODEL), F32)], [_sds((1, D_MODEL), F32)], carry=(tm, D_MODEL),
        streamed=[w_up, w_down_a, w_down_b], tile_copies=_mlp_weight_pieces("down_up"), n_copies=2 * (D_FF // FF_TILE))


def _outproj_bwd(dx1, o, w_out, nw, tm=256, carried=None):
    n_tok = dx1.shape[0]

    def body(dx1_ref, o_ref, wo_ref, nw_ref, do_ref, dya_ref, dyb_ref, dnw_ref):
        ov = o_ref[...]
        _, r = _rms_fwd(ov, nw_ref[...])
        do, dnw = _rms_bwd(ov, r, nw_ref[...], dx1_ref[...])
        dob = do.astype(BF16)
        do_ref[...] = dob
        dya_ref[...] = _dot_nt(dob, wo_ref[:GM_WIDTH, :])
        dyb_ref[...] = _dot_nt(dob, wo_ref[GM_WIDTH:, :])
        dnw_ref[...] += dnw

    return _rows_call("outproj_bwd", body, tm, [dx1, o], [w_out, nw],
                      [_sds((n_tok, D_MODEL), BF16), _sds((n_tok, GM_WIDTH), F32), _sds((n_tok, SSM_WIDTH), F32)],
                      [_sds((1, D_MODEL), F32)], carried=carried)


def _gmlp_bwd(p_uv, dya, lnw, lnb, e_bf, et_bf, w_cat, w_stack, bmap, carried=None):
    n_tok = p_uv.shape[0]
    chunks_per_step = 2

    def body(puv_ref, dya_ref, lnw_ref, lnb_ref, e_ref, et_ref, wcat_ref, wstack_ref, bmap_ref,
             dpuv_ref, dws_ref, dbs_ref, dlnw_ref, dlnb_ref, wm_scr, wsm_scr):
        t_stk = lax.broadcasted_iota(jnp.int32, (N_HEADS * CHUNK, CHUNK), 0) % CHUNK
        s_stk = lax.broadcasted_iota(jnp.int32, (N_HEADS * CHUNK, CHUNK), 1)

        @pl.when(pl.program_id(0) == 0)
        def _():
            wm_scr[...] = _causal_w_cat(wcat_ref[...])
            wsm_scr[...] = jnp.where(t_stk >= s_stk, wstack_ref[...], 0.0).astype(BF16)

        lnw_v = lnw_ref[...]
        e_v, et_v = e_ref[...], et_ref[...]

        def one_chunk(rows):
            u, v, gu, tu, tv, rstd, xhat, vn = _gmlp_common(puv_ref[rows, :], lnw_v, lnb_ref[...], e_v, et_v)
            vnb = vn.astype(BF16)
            mixed = jnp.dot(wm_scr[...], _head_blocks(vnb), preferred_element_type=F32) + bmap_ref[...]
            dy = dya_ref[rows, :]
            du = dy * mixed * _gelu_grad(u, tu)
            dmixed = dy * gu
            (dbs,) = _seg_dots([dmixed], et_v)
            dblocks = _head_blocks(dmixed.astype(BF16))
            dvn = lax.dot_general(wsm_scr[...], dblocks, (((0,), (0,)), ((), ())), preferred_element_type=F32)
            dws = lax.dot_general(dblocks, vnb, (((1,), (1,)), ((), ())), preferred_element_type=F32)
            dxh = dvn * lnw_v
            m1, m2 = _seg_dots([dxh, dxh * xhat], et_v)
            m1, m2 = _seg_dots([m1 * (1.0 / HEAD_DIM), m2 * (1.0 / HEAD_DIM)], e_v)
            dgv = rstd * (dxh - m1 - xhat * m2)
            dv = dgv * _gelu_grad(v, tv)
            dpuv_ref[rows, :GM_WIDTH] = du.astype(BF16)
            dpuv_ref[rows, GM_WIDTH:] = dv.astype(BF16)
            return dbs, dws, jnp.sum(dvn * xhat, axis=0, keepdims=True), jnp.sum(dvn, axis=0, keepdims=True)

        parts = [one_chunk(slice(k * CHUNK, (k + 1) * CHUNK)) for k in range(chunks_per_step)]
        dbs, dws, dlnw, dlnb = [functools.reduce(lambda a, b: a + b, vals) for vals in zip(*parts)]
        dbs_ref[...] += dbs
        dws_ref[...] += jnp.where(t_stk >= s_stk, dws, 0.0)
        dlnw_ref[...] += dlnw
        dlnb_ref[...] += dlnb

    return _rows_call(
        "gmlp_bwd", body, chunks_per_step * CHUNK, [p_uv, dya], [lnw, lnb, e_bf, et_bf, w_cat, w_stack, bmap],
        [_sds((n_tok, 2 * GM_WIDTH), BF16)],
        [_sds((N_HEADS * CHUNK, CHUNK), F32), _sds((CHUNK, DT_PAD), F32), _sds((1, GM_WIDTH), F32),
         _sds((1, GM_WIDTH), F32)],
        scratch=[pltpu.VMEM((CHUNK, N_HEADS * CHUNK), BF16), pltpu.VMEM((N_HEADS * CHUNK, CHUNK), BF16)],
        carried=carried)


def _ssd_bwd(p_xbc, p_z, p_dt, yssd, sprev, dyb, conv_w, conv_b, dt_bias, a_log, dskip_map, norm_w, e_bf, et_bf, n_seq,
             carried=None):
    n_tok = p_xbc.shape[0]
    nc = n_tok // n_seq // CHUNK

    def body(xr3, xprev3, z3, pdt3, yssd3, sprev3, dyb3,
             cw_ref, cb_ref, dtb_ref, alog_ref, dsk_ref, nw_ref, e_ref, et_ref,
             dpxbc3, dpz3, dpdt3, dcw_ref, dcb_ref, ddtb_ref, dalog_ref, ddsk_ref, dnw_ref,
             ds3_scr, nxt3_scr, dxa3_scr):
        @pl.when(pl.program_id(0) == 0)
        def _():
            for a in (dcw_ref, dcb_ref, ddtb_ref, dalog_ref, ddsk_ref, dnw_ref, ds3_scr, nxt3_scr):
                a[...] = jnp.zeros_like(a)

        for b in range(n_seq):
            one_sequence(xr3.at[b], xprev3.at[b], z3.at[b], pdt3.at[b], yssd3.at[b], sprev3.at[b], dyb3.at[b],
                         cw_ref, cb_ref, dtb_ref, alog_ref, dsk_ref, nw_ref, e_ref, et_ref,
                         dpxbc3.at[b], dpz3.at[b], dpdt3.at[b], dcw_ref, dcb_ref, ddtb_ref, dalog_ref, ddsk_ref, dnw_ref,
                         ds3_scr.at[b], nxt3_scr.at[b], dxa3_scr.at[b])

    def one_sequence(xr_ref, xprev_ref, z_ref, pdt_ref, yssd_ref, sprev_ref, dyb_ref,
                     cw_ref, cb_ref, dtb_ref, alog_ref, dsk_ref, nw_ref, e_ref, et_ref,
                     dpxbc_ref, dpz_ref, dpdt_ref, dcw_ref, dcb_ref, ddtb_ref, dalog_ref, ddsk_ref, dnw_ref,
                     ds_scr, nxt_scr, dxa_scr):
        chunk = nc - 1 - pl.program_id(0)
        xr = xr_ref[...]
        prev = jnp.where(chunk == 0, 0.0, xprev_ref[...])
        et_v = et_ref[...]
        p = _ssd_pre(xr, prev, cw_ref, cb_ref[...], pdt_ref[...], dtb_ref[...], alog_ref[...], e_ref[...])
        last, e_exp, dte, cd = _ssd_maps(p)
        rowi = p["rowi"]
        xs = p["xa"][:, :SSM_WIDTH]
        xd = xs * p["dt_map"]
        a_cs_t = p["a_cs"].T
        tri = _tril_mask()
        dsk = dsk_ref[...]
        nw_v = nw_ref[...]

        yv = yssd_ref[...]
        zv = z_ref[...]
        sz, zg, yg, _, rs = _gate_fwd(yv, zv, nw_v)
        dout = dyb_ref[...]
        for g in range(SSM_GROUPS):
            gs = slice(g * GROUP_W, (g + 1) * GROUP_W)
            dyg_g, dnw_g = _rms_bwd(yg[:, gs], rs[g], nw_v[:, gs], dout[:, gs])
            dnw_ref[:, gs] += dnw_g
            dxa_scr[:, gs] = dyg_g
        dyg = dxa_scr[:, :SSM_WIDTH]
        d_y = dyg * zg
        dpz_ref[...] = (dyg * yv * (sz + zv * sz * (1.0 - sz))).astype(BF16)

        s_prev = sprev_ref[...]
        ds_next = ds_scr[...]
        lane_dt = lax.broadcasted_iota(jnp.int32, (1, DT_PAD), 1)
        da_cols = jnp.zeros((CHUNK, DT_PAD), F32)
        for g in range(SSM_GROUPS):
            gs = slice(g * GROUP_W, (g + 1) * GROUP_W)
            b_off = SSM_WIDTH + g * SSM_STATE
            c_off = SSM_WIDTH + (SSM_GROUPS + g) * SSM_STATE
            bm = p["xa"][:, b_off:b_off + SSM_STATE].astype(BF16)
            cm = p["xa"][:, c_off:c_off + SSM_STATE].astype(BF16)
            cb_mat = _dot_nt(cm, bm)
            d_yg = d_y[:, gs]
            d_ygb = d_yg.astype(BF16)
            xdg = xd[:, gs]
            xdgb = xdg.astype(BF16)
            ds_g = ds_next[:, gs]
            sp_g = s_prev[:, gs]
            bds = _dot(bm, ds_g)
            dcs = d_yg * e_exp[:, gs]
            d_c = _dot_nt(dcs, sp_g)
            ds_scr[:, gs] = cd[:, gs] * ds_g + _dot_tn(cm, dcs)
            d_b = _dot_nt(xdg * dte[:, gs], ds_g)
            dxd_g = bds * dte[:, gs]
            sum_dcb = jnp.zeros((CHUNK, CHUNK), F32)
            for r in range(SSM_GROUPS * 2):
                head = g * 4 + r
                mask = _head_lane_mask(GROUP_W, r)
                dm = _head_decay(p["a_cs"], a_cs_t, head, tri)
                m_mat = cb_mat * dm
                g_mat = _dot_nt(jnp.where(mask, d_yg, 0.0), xdgb)
                w_mat = g_mat * m_mat
                sum_dcb = sum_dcb + g_mat * dm
                dxd_g = dxd_g + jnp.where(mask, _dot_tn(m_mat, d_ygb), 0.0)
                da_h = jnp.sum(w_mat - w_mat.T, axis=1, keepdims=True)
                da_cols = da_cols + jnp.where(lane_dt == head, da_h, 0.0)
            d_c = d_c + _dot(sum_dcb, bm)
            d_b = d_b + _dot_tn(sum_dcb, cm)
            dxa_scr[:, b_off:b_off + SSM_STATE] = d_b
            dxa_scr[:, c_off:c_off + SSM_STATE] = d_c
            y_off_g = _dot(cm, sp_g) * e_exp[:, gs]
            t3 = bds * xdg * dte[:, gs]
            tail = jnp.sum(t3, axis=0, keepdims=True) + jnp.sum(ds_g * sp_g, axis=0, keepdims=True) * cd[:, gs]
            pre_g = d_yg * y_off_g - t3 + jnp.where(last, tail, 0.0)
            s_pre, ddt_g, s_dsk = _seg_dots([pre_g, dxd_g * xs[:, gs], d_yg * xs[:, gs]], et_v[gs, :])
            da_cols = da_cols + s_pre
            ddsk_ref[...] += jnp.sum(s_dsk, axis=0, keepdims=True)
            dxa_scr[:, gs] = dxd_g * p["dt_map"][:, gs] + dsk[:, gs] * d_yg
            if g == 0:
                ddt = ddt_g
            else:
                ddt = ddt + ddt_g
        r_i = lax.broadcasted_iota(jnp.int32, (CHUNK, CHUNK), 0)
        c_i = lax.broadcasted_iota(jnp.int32, (CHUNK, CHUNK), 1)
        ddta = _tri_dot(r_i <= c_i, da_cols, terms=2)
        ddt = ddt + ddta * p["a_neg"]
        dalog_ref[...] += jnp.sum(ddta * p["dt"], axis=0, keepdims=True) * p["a_neg"]
        draw = ddt * _sigmoid(p["pre"])
        ddtb_ref[...] += jnp.sum(draw, axis=0, keepdims=True)
        dpdt_ref[...] = draw.astype(BF16)

        xc = p["xc"]
        sg = p["sg"]
        dxc = dxa_scr[...] * (sg + xc * sg * (1.0 - sg))
        dcb_ref[...] += jnp.sum(dxc, axis=0, keepdims=True)
        for k in range(CONV_K):
            dcw_ref[k] += jnp.sum(dxc * p["shifted"][k], axis=0, keepdims=True)
        nxt = nxt_scr[...]
        dxr = cw_ref[3] * dxc
        for s in range(1, CONV_K):
            dxr = dxr + cw_ref[CONV_K - 1 - s] * _shift_up(dxc, nxt, s)
        dpxbc_ref[...] = dxr.astype(BF16)
        nxt_scr[...] = dxc[:SUBLANES, :]

    seq_len = n_tok // n_seq

    def rows(width):
        return pl.BlockSpec((n_seq, CHUNK, width), lambda s: (0, nc - 1 - s, 0))

    tiles = CHUNK // SUBLANES
    prev_rows = pl.BlockSpec((n_seq, SUBLANES, CONV_CH), lambda s: (0, jnp.maximum((nc - 1 - s) * tiles - 1, 0), 0))

    def whole(shape):
        nd = len(shape)
        return pl.BlockSpec(tuple(shape), lambda s: (0,) * nd)

    def by_seq(a):
        return a.reshape(n_seq, seq_len, a.shape[-1])

    acc_shapes = [(CONV_K, 1, CONV_CH), (1, CONV_CH), (1, DT_PAD), (1, DT_PAD), (1, DT_PAD), (1, SSM_WIDTH)]
    xbc3 = by_seq(p_xbc)
    outs = _call_carrying(
        body, carried, name="ssd_bwd", grid=(nc,),
        in_specs=[rows(CONV_CH), prev_rows, rows(SSM_WIDTH), rows(DT_PAD), rows(SSM_WIDTH), rows(SSM_WIDTH),
                  rows(SSM_WIDTH)] + _ssd_const_specs(),
        out_specs=[rows(CONV_CH), rows(SSM_WIDTH), rows(DT_PAD)] + [whole(s) for s in acc_shapes],
        out_shape=tuple([_sds((n_seq, seq_len, CONV_CH), BF16), _sds((n_seq, seq_len, SSM_WIDTH), BF16),
                         _sds((n_seq, seq_len, DT_PAD), BF16)] + [_sds(s, F32) for s in acc_shapes]),
        scratch_shapes=[pltpu.VMEM((n_seq, SSM_STATE, SSM_WIDTH), F32), pltpu.VMEM((n_seq, SUBLANES, CONV_CH), F32),
                        pltpu.VMEM((n_seq, CHUNK, CONV_CH), F32)],
        operands=[xbc3, xbc3, by_seq(p_z), by_seq(p_dt), by_seq(yssd), by_seq(sprev), by_seq(dyb), conv_w, conv_b, dt_bias,
                  a_log, dskip_map, norm_w, e_bf, et_bf])
    return tuple(o.reshape(n_tok, o.shape[-1]) for o in outs[:3]) + tuple(outs[3:])


def _inproj_bwd(dp_uv, dp_xbc, dp_z, dp_dt, x, dx1, w_uv, w_xbc, w_z, w_dt, nw, tm=256, carried=None):
    n_tok = x.shape[0]

    def body(duv_ref, dxbc_ref, dz_ref, ddt_ref, x_ref, dx1_ref, wuv_ref, wxbc_ref, wz_ref, wdt_ref, nw_ref,
             gx_ref, h_ref, dnw_ref):
        dh = _dot_nt(duv_ref[...], wuv_ref[...]) + _dot_nt(dxbc_ref[...], wxbc_ref[...])
        dh = dh + _dot_nt(dz_ref[...], wz_ref[...]) + _dot_nt(ddt_ref[...], wdt_ref[...])
        xv = x_ref[...]
        h, r = _rms_fwd(xv, nw_ref[...])
        dx, dnw = _rms_bwd(xv, r, nw_ref[...], dh)
        gx_ref[...] = dx1_ref[...] + dx
        h_ref[...] = h.astype(BF16)
        dnw_ref[...] += dnw

    return _rows_call("inproj_bwd", body, tm, [dp_uv, dp_xbc, dp_z, dp_dt, x, dx1], [w_uv, w_xbc, w_z, w_dt, nw],
                      [_sds((n_tok, D_MODEL), F32), _sds((n_tok, D_MODEL), BF16)], [_sds((1, D_MODEL), F32)],
                      carried=carried)


def _const_maps():
    lane = jnp.arange(SSM_WIDTH) // HEAD_DIM
    e_bf = (jnp.arange(DT_PAD)[:, None] == lane[None, :]).astype(BF16)
    return e_bf, e_bf.T


def _pad_lanes(v, width):
    return jnp.pad(v, ((0, 0), (0, width - v.shape[1])))


SHARD_COLS = IN_COLS // N_CHIPS
_UV_END = 2 * GM_WIDTH
_Z_END = _UV_END + SSM_WIDTH
_XBC_END = _Z_END + CONV_CH


def _cols_from_shards(w4, lo, hi):
    pieces = []
    for j in range(N_CHIPS):
        a, b = max(lo, j * SHARD_COLS), min(hi, (j + 1) * SHARD_COLS)
        if a < b:
            pieces.append(w4[j][:, a - j * SHARD_COLS:b - j * SHARD_COLS])
    return pieces[0] if len(pieces) == 1 else jnp.concatenate(pieces, axis=1)


def _shards_from_cols(blocks):
    shards = []
    for j in range(N_CHIPS):
        pieces = []
        for arr, lo, hi in blocks:
            a, b = max(lo, j * SHARD_COLS), min(hi, (j + 1) * SHARD_COLS)
            if a < b:
                pieces.append(arr[:, a - lo:b - lo])
        shards.append(pieces[0] if len(pieces) == 1 else jnp.concatenate(pieces, axis=1))
    return jnp.stack(shards)


def _forward_backward(x, tgt, w_in4, conv_w, small, out_shard, up_shard, down_shard, core, adam_args):
    n_seq, seq_len, _ = x.shape
    n_tok = n_seq * seq_len
    x2 = x.reshape(n_tok, D_MODEL)
    tgt2 = tgt.reshape(n_tok, D_MODEL)
    e_bf, et_bf = _const_maps()

    w_uv = _cols_from_shards(w_in4, 0, _UV_END)
    w_z = _cols_from_shards(w_in4, _UV_END, _Z_END)
    w_xbc = _cols_from_shards(w_in4, _Z_END, _XBC_END)
    w_dt = _pad_lanes(_cols_from_shards(w_in4, _XBC_END, IN_COLS), DT_PAD)

    nw_pre = small["norm_mix_pre"]
    lnw = small["gm_ln_w"].reshape(1, GM_WIDTH)
    lnb = small["gm_ln_b"].reshape(1, GM_WIDTH)
    w_stack = small["gm_w_s"].reshape(N_HEADS * CHUNK, CHUNK)
    w_cat = jnp.transpose(small["gm_w_s"], (1, 0, 2)).reshape(CHUNK, N_HEADS * CHUNK)
    bmap = jnp.repeat(small["gm_b_s"].T, HEAD_DIM, axis=1)
    cw3 = conv_w.reshape(CONV_K, 1, CONV_CH)
    conv_b = small["conv_b"]
    dt_bias = _pad_lanes(small["dt_bias"], DT_PAD)
    a_log = _pad_lanes(small["a_log"], DT_PAD)
    dskip_map = jnp.repeat(small["d_skip"], HEAD_DIM, axis=1)
    ssm_nw = small["ssm_norm_w"]

    half = down_shard.shape[0] // 2
    p_uv, p_xbc, p_z, p_dt, w_out4, w_down_a = _inproj_fwd(
        x2, nw_pre, w_uv, w_xbc, w_z, w_dt, carried=_allgather_exchange([out_shard, down_shard[:half]]))
    ssd_consts = (cw3, conv_b, dt_bias, a_log, dskip_map, ssm_nw, e_bf, et_bf)
    w_out_b = w_out4.reshape(D_MODEL, D_MODEL)
    mix, yssd, sprev, o, x1, h2, w_up4, w_down_b = _mixer_fwd(
        p_uv, p_xbc, p_z, p_dt, x2, lnw, lnb, w_cat, bmap, w_out_b, small["norm_mix_post"], small["norm_ffn_pre"],
        *ssd_consts, n_seq, carried=_allgather_exchange([up_shard, down_shard[half:]]))
    f, dd, dy, loss_acc, d_nffn_post = _mlp_fwd(h2, x1, tgt2, w_up4, w_down_a, w_down_b, small["norm_ffn_post"])

    dup, dx1, d_nffn_pre = _mlp_bwd(dd, f, x1, dy, w_down_a, w_down_b, w_up4, small["norm_ffn_pre"])
    tk = min(DW_TOKENS_PER_STEP, n_tok)
    g_up = _matmul_tn("dw_up", h2, dup, D_MODEL, D_MODEL, tk, stacked=True)
    g_down = _matmul_tn("dw_down", f, dd, 1024, D_MODEL, tk).reshape(N_CHIPS, D_FF // N_CHIPS, D_MODEL)
    do, dya, dyb, d_nmix_post, got_up, got_down = _outproj_bwd(
        dx1, o, w_out_b, small["norm_mix_post"], carried=_pair_exchange([g_up, g_down]))
    h_up = _pair_sum(core, g_up, got_up, 256)
    h_down = _pair_sum(core, g_down, got_down, 256)
    g_out = _matmul_tn("dw_out", mix, do, D_MODEL, D_MODEL, tk).reshape(N_CHIPS, D_MODEL // N_CHIPS, D_MODEL)
    dp_uv, d_ws, d_bs_t, d_lnw, d_lnb, slab_up, got_out = _gmlp_bwd(
        p_uv, dya, lnw, lnb, e_bf, et_bf, w_cat, w_stack, bmap,
        carried=_both(_chip_exchange([h_up]), _pair_exchange([g_out])))
    h_out = _pair_sum(core, g_out, got_out, 128)
    early = {
        "gm_ln_w": d_lnw.reshape(N_HEADS, HEAD_DIM), "gm_ln_b": d_lnb.reshape(N_HEADS, HEAD_DIM),
        "gm_w_s": d_ws.reshape(N_HEADS, CHUNK, CHUNK), "gm_b_s": d_bs_t[:, :N_HEADS].T,
        "norm_mix_post": d_nmix_post, "norm_ffn_pre": d_nffn_pre, "norm_ffn_post": d_nffn_post,
    }
    packed_early = _pack(early, tuple(early), tail=loss_acc[0, 0].reshape(1))
    (dp_xbc, dp_z, dp_dt, d_cw, d_cb, d_dtb, d_alog, d_dsk, d_ssm_nw, slab_down, slab_out, all_early) = _ssd_bwd(
        p_xbc, p_z, p_dt, yssd, sprev, dyb, *ssd_consts, n_seq,
        carried=_both(_chip_exchange([h_down, h_out]), _device_gather_exchange(packed_early)))
    gx, h, d_nmix_pre = _inproj_bwd(dp_uv, dp_xbc, dp_z, dp_dt, x2, dx1, w_uv, w_xbc, w_z, w_dt, nw_pre)
    late = {
        "norm_mix_pre": d_nmix_pre, "conv_w": d_cw.reshape(CONV_K, CONV_CH), "conv_b": d_cb,
        "dt_bias": d_dtb[:, :N_HEADS], "a_log": d_alog[:, :N_HEADS], "d_skip": d_dsk[:, :N_HEADS],
        "ssm_norm_w": d_ssm_nw,
    }
    g_uv, all_late = _matmul_tn("dw_in_uv", h, dp_uv, D_MODEL, 2 * GM_WIDTH, tk,
                                carried=_device_gather_exchange(_pack(late, tuple(late))))
    sum_early = _ordered_sum("small_sum_early", all_early)
    small_sum = _unpack(sum_early, {n: v.shape for n, v in early.items()}, tuple(early))
    small_sum.update(_unpack(_ordered_sum("small_sum_late", all_late), {n: v.shape for n, v in late.items()}, tuple(late)))
    loss = sum_early.reshape(-1)[sum(v.size for v in early.values())]
    red_up, red_down, red_out = _chip_sum(slab_up, 256), _chip_sum(slab_down, 256), _chip_sum(slab_out, 128)
    g_xbc, oth_up, oth_down, oth_out = _matmul_tn("dw_in_xbc", h, dp_xbc, D_MODEL, CONV_CH, tk,
                                                  carried=_pair_swap([red_up, red_down, red_out]))
    g_z = _matmul_tn("dw_in_z", h, dp_z, D_MODEL, SSM_WIDTH, tk)
    g_dt = _matmul_tn("dw_in_dt", h, dp_dt, D_MODEL, DT_PAD, tk)

    g_in = _shards_from_cols([(g_uv, 0, _UV_END), (g_z, _UV_END, _Z_END), (g_xbc, _Z_END, _XBC_END),
                              (g_dt, _XBC_END, IN_COLS)])
    red_in, oth_in = _reduce_scatter_last(g_in)
    res = _adamw_halves("adamw_mlp", [(adam_args["w_up"][0], red_up, oth_up) + adam_args["w_up"][1:],
                                      (adam_args["w_down"][0], red_down, oth_down) + adam_args["w_down"][1:]], 256)
    big_out = {"w_up": res[0:4], "w_down": res[4:8]}
    big_out["w_out"] = _adamw_halves("adamw_w_out", [(adam_args["w_out"][0], red_out, oth_out) + adam_args["w_out"][1:]], 128)
    big_out["w_in"] = _adamw_halves("adamw_w_in", [(adam_args["w_in"][0], red_in, oth_in) + adam_args["w_in"][1:]], 256)

    return loss, gx.reshape(x.shape), big_out, small_sum


_HBM = pl.BlockSpec(memory_space=pltpu.HBM)


D2D_CHUNKS = 8
ICI_CHUNKS = 1
ROW_ALIGN = 16


def _row_chunks(rows, n_chunks):
    size = min(max(rows // n_chunks, ROW_ALIGN), rows)
    assert rows % size == 0
    return [(start, size) for start in range(0, rows, size)]


def _position():
    x, y, c = lax.axis_index("x"), lax.axis_index("y"), lax.axis_index("c")
    chips = [(1 - x, y), (x, 1 - y), (1 - x, 1 - y)]
    return x, y, c, chips


def _allgather_exchange(arrs):
    n = len(arrs)

    def copies(ins, outs, send_sems, recv_sems, local_sems):
        x, y, c, chips = _position()
        me = 2 * x + y
        sibling = (x, y, 1 - c)

        def copy(a, k, src, dst, to):
            return pltpu.make_async_remote_copy(src_ref=src, dst_ref=dst, send_sem=send_sems.at[a, k],
                                                recv_sem=recv_sems.at[a, k], device_id=to, device_id_type=MESH)

        def half_rows(a, pc):
            half = ins[a].shape[0] // 2
            return pl.ds(pc * half, half)

        local = [pltpu.make_async_copy(ins[a], outs[a].at[me], local_sems.at[a]) for a in range(n)]
        ici_out = [[copy(a, k, ins[a].at[half_rows(a, c)], outs[a].at[me, half_rows(a, c)], (px, py, c))
                    for k, (px, py) in enumerate(chips)] for a in range(n)]
        return c, chips, sibling, copy, half_rows, local, ici_out

    def start(ins, outs, send_sems, recv_sems, local_sems):
        c, chips, _, copy, _, local, _ = copies(ins, outs, send_sems, recv_sems, local_sems)
        x, y, _, _ = _position()
        me = 2 * x + y
        for cp in local:
            cp.start()
        for a in range(n):
            half = ins[a].shape[0] // 2
            for k, (px, py) in enumerate(chips):
                for first, size in _row_chunks(half, ICI_CHUNKS):
                    rows = pl.ds(c * half + first, size)
                    copy(a, k, ins[a].at[rows], outs[a].at[me, rows], (px, py, c)).start()

    def finish(ins, outs, send_sems, recv_sems, local_sems):
        c, chips, sibling, copy, half_rows, local, ici_out = copies(ins, outs, send_sems, recv_sems, local_sems)
        for a in range(n):
            half = ins[a].shape[0] // 2
            for k, (px, py) in enumerate(chips):
                blk = outs[a].at[2 * px + py, half_rows(a, c)]
                copy(a, k, blk, blk, (px, py, c)).wait_recv()
                for first, size in _row_chunks(half, D2D_CHUNKS):
                    piece = outs[a].at[2 * px + py, pl.ds(c * half + first, size)]
                    copy(a, 3 + k, piece, piece, sibling).start()
        for a in range(n):
            for k, (px, py) in enumerate(chips):
                theirs = outs[a].at[2 * px + py, half_rows(a, 1 - c)]
                copy(a, 3 + k, theirs, theirs, sibling).wait_recv()
                mine = outs[a].at[2 * px + py, half_rows(a, c)]
                copy(a, 3 + k, mine, mine, sibling).wait_send()
        for a in range(n):
            for cp in ici_out[a]:
                cp.wait_send()
        for cp in local:
            cp.wait()

    return _Carried(arrs, [_sds((N_CHIPS,) + a.shape, a.dtype) for a in arrs],
                    [pltpu.SemaphoreType.DMA((n, 6)), pltpu.SemaphoreType.DMA((n, 6)), pltpu.SemaphoreType.DMA((n,))],
                    start, finish)


def _run_exchange(name, exchange):
    n_in, n_out = len(exchange.ins), len(exchange.out_shapes)

    def body(*refs):
        ins, outs, sems = refs[:n_in], refs[n_in:n_in + n_out], refs[n_in + n_out:]
        exchange.start(ins, outs, *sems)
        exchange.finish(ins, outs, *sems)

    return pl.pallas_call(
        body, name=name, out_shape=tuple(exchange.out_shapes), in_specs=[_HBM] * n_in,
        out_specs=tuple([_HBM] * n_out), scratch_shapes=exchange.sems,
    )(*exchange.ins)


def _pair_exchange(grads):
    n = len(grads)

    def copier(send_sems, recv_sems):
        x, y, c, _ = _position()

        def copy(a, src, dst):
            return pltpu.make_async_remote_copy(src_ref=src, dst_ref=dst, send_sem=send_sems.at[a],
                                                recv_sem=recv_sems.at[a], device_id=(x, y, 1 - c), device_id_type=MESH)
        return c, copy

    def start(ins, got, send_sems, recv_sems):
        c, copy = copier(send_sems, recv_sems)
        for a in range(n):
            half = ins[a].shape[1] // 2
            for slab in range(N_CHIPS):
                for first, size in _row_chunks(half, D2D_CHUNKS):
                    copy(a, ins[a].at[slab, pl.ds((1 - c) * half + first, size), :],
                         got[a].at[slab, pl.ds(first, size), :]).start()

    def finish(ins, got, send_sems, recv_sems):
        c, copy = copier(send_sems, recv_sems)
        for a in range(n):
            half = ins[a].shape[1] // 2
            copy(a, ins[a].at[:, pl.ds((1 - c) * half, half), :], got[a]).wait()

    return _Carried(grads, [_sds((N_CHIPS, g.shape[1] // 2, g.shape[2]), g.dtype) for g in grads],
                    [pltpu.SemaphoreType.DMA((n,)), pltpu.SemaphoreType.DMA((n,))], start, finish)


def _chip_exchange(hsums):
    n = len(hsums)

    def copies(ins, outs, send_sems, recv_sems, local_sems, pieces):
        x, y, c, chips = _position()
        me = 2 * x + y
        cps = []
        for a in range(n):
            cps.append(pltpu.make_async_copy(ins[a].at[me], outs[a].at[me], local_sems.at[a]))
            rows = ins[a].shape[1]
            for k, (px, py) in enumerate(chips):
                for first, size in (_row_chunks(rows, ICI_CHUNKS) if pieces else [(0, rows)]):
                    cps.append(pltpu.make_async_remote_copy(
                        src_ref=ins[a].at[2 * px + py, pl.ds(first, size)], dst_ref=outs[a].at[me, pl.ds(first, size)],
                        send_sem=send_sems.at[a, k], recv_sem=recv_sems.at[a, k], device_id=(px, py, c),
                        device_id_type=MESH))
        return cps

    def start(*refs):
        for cp in copies(*refs, pieces=True):
            cp.start()

    def finish(*refs):
        for cp in copies(*refs, pieces=False):
            cp.wait()

    return _Carried(hsums, [_sds(h.shape, h.dtype) for h in hsums],
                    [pltpu.SemaphoreType.DMA((n, 3)), pltpu.SemaphoreType.DMA((n, 3)), pltpu.SemaphoreType.DMA((n,))],
                    start, finish)


def _pair_swap(reds):
    n = len(reds)

    def copier(send_sems, recv_sems):
        x, y, c, _ = _position()

        def copy(a, src, dst):
            return pltpu.make_async_remote_copy(src_ref=src, dst_ref=dst, send_sem=send_sems.at[a],
                                                recv_sem=recv_sems.at[a], device_id=(x, y, 1 - c), device_id_type=MESH)
        return copy

    def start(ins, outs, send_sems, recv_sems):
        copy = copier(send_sems, recv_sems)
        for a in range(n):
            for first, size in _row_chunks(ins[a].shape[0], 2 * D2D_CHUNKS):
                copy(a, ins[a].at[pl.ds(first, size), :], outs[a].at[pl.ds(first, size), :]).start()

    def finish(ins, outs, send_sems, recv_sems):
        copy = copier(send_sems, recv_sems)
        for a in range(n):
            copy(a, ins[a], outs[a]).wait()

    return _Carried(reds, [_sds(r.shape, r.dtype) for r in reds],
                    [pltpu.SemaphoreType.DMA((n,)), pltpu.SemaphoreType.DMA((n,))], start, finish)


def _reduce_scatter_last(grad):
    _, rows, cols = grad.shape
    half = rows // 2
    pieces = _row_chunks(half, D2D_CHUNKS)

    def body(g_ref, mine_ref, theirs_ref, got_scr, hsum_scr, slab_scr, pair_sems, ici_send, ici_recv, swap_sems):
        x, y, c, chips = _position()
        me = 2 * x + y
        sibling = (x, y, 1 - c)

        def to_sibling(src, dst, sems):
            return pltpu.make_async_remote_copy(src_ref=src, dst_ref=dst, send_sem=sems.at[0], recv_sem=sems.at[1],
                                                device_id=sibling, device_id_type=MESH)

        for slab in range(N_CHIPS):
            for first, size in pieces:
                to_sibling(g_ref.at[slab, pl.ds((1 - c) * half + first, size)], got_scr.at[slab, pl.ds(first, size)],
                           pair_sems).start()
        to_sibling(g_ref.at[:, pl.ds((1 - c) * half, half)], got_scr, pair_sems).wait()
        own = g_ref[:, pl.ds(pl.multiple_of(c * half, half), half), :]
        hsum_scr[...] = (own.astype(F32) + got_scr[...].astype(F32)).astype(BF16)

        slab_scr[me] = hsum_scr[me]
        ici = [pltpu.make_async_remote_copy(src_ref=hsum_scr.at[2 * px + py], dst_ref=slab_scr.at[me],
                                            send_sem=ici_send.at[k], recv_sem=ici_recv.at[k], device_id=(px, py, c),
                                            device_id_type=MESH) for k, (px, py) in enumerate(chips)]
        for cp in ici:
            cp.start()
        for cp in ici:
            cp.wait()
        acc = slab_scr[0].astype(F32)
        for k in range(1, N_CHIPS):
            acc = acc + slab_scr[k].astype(F32)
        mine_ref[...] = acc

        for first, size in pieces:
            to_sibling(mine_ref.at[pl.ds(first, size)], theirs_ref.at[pl.ds(first, size)], swap_sems).start()
        to_sibling(mine_ref, theirs_ref, swap_sems).wait()

    vmem = pl.BlockSpec(memory_space=pltpu.VMEM)
    halves = (N_CHIPS, half, cols)
    return pl.pallas_call(
        body, name="grad_reduce_scatter_last", out_shape=(_sds((half, cols), F32), _sds((half, cols), F32)),
        in_specs=[vmem], out_specs=(vmem, vmem),
        scratch_shapes=[pltpu.VMEM(halves, BF16), pltpu.VMEM(halves, BF16), pltpu.VMEM(halves, BF16),
                        pltpu.SemaphoreType.DMA((2,)), pltpu.SemaphoreType.DMA((3,)), pltpu.SemaphoreType.DMA((3,)),
                        pltpu.SemaphoreType.DMA((2,))],
        compiler_params=pltpu.CompilerParams(vmem_limit_bytes=VMEM_LIMIT_BYTES),
    )(grad)


def _device_gather_exchange(packed):
    def copies(ins, outs, send_sems, recv_sems, local_sem):
        (x_ref,), (all_ref,) = ins, outs
        x, y, c, chips = _position()
        me, sibling = (x, y, c), (x, y, 1 - c)

        def slab(px, py, pc):
            return all_ref.at[4 * px + 2 * py + pc]

        def copy(k, block, to, src=None):
            return pltpu.make_async_remote_copy(
                src_ref=slab(*block) if src is None else src, dst_ref=slab(*block), send_sem=send_sems.at[k],
                recv_sem=recv_sems.at[k], device_id=to, device_id_type=MESH)

        mine = pltpu.make_async_copy(x_ref, slab(*me), local_sem)
        first = [copy(0, me, sibling, src=x_ref)]
        first += [copy(1 + j, me, (*chip, c), src=x_ref) for j, chip in enumerate(chips)]
        passed = [copy(4 + j, (*chip, c), sibling) for j, chip in enumerate(chips)]
        return c, chips, me, sibling, copy, mine, first, passed

    def start(ins, outs, send_sems, recv_sems, local_sem):
        _, _, _, _, _, mine, first, _ = copies(ins, outs, send_sems, recv_sems, local_sem)
        mine.start()
        for cp in first:
            cp.start()

    def finish(ins, outs, send_sems, recv_sems, local_sem):
        c, chips, me, sibling, copy, mine, first, passed = copies(ins, outs, send_sems, recv_sems, local_sem)
        for j, chip in enumerate(chips):
            copy(1 + j, (*chip, c), me).wait_recv()
            passed[j].start()
        copy(0, sibling, me).wait_recv()
        for j, chip in enumerate(chips):
            copy(4 + j, (*chip, 1 - c), me).wait_recv()
        for cp in first + passed:
            cp.wait_send()
        mine.wait()

    return _Carried([packed], [_sds((N_DEV,) + packed.shape, F32)],
                    [pltpu.SemaphoreType.DMA((7,)), pltpu.SemaphoreType.DMA((7,)), pltpu.SemaphoreType.DMA],
                    start, finish)


def _ordered_sum(name, slabs):
    _, m_per, n_cols = slabs.shape

    def body(s_ref, o_ref):
        acc = s_ref[0]
        for d in range(1, N_DEV):
            acc = acc + s_ref[d]
        o_ref[...] = acc

    vmem = pl.BlockSpec(memory_space=pltpu.VMEM)
    return pl.pallas_call(body, name=name, out_shape=_sds((m_per, n_cols), F32), in_specs=[vmem], out_specs=vmem)(slabs)


def _pair_sum(core, own, got, tm):
    _, half, cols = got.shape
    nb = half // tm

    def body(c_ref, a_ref, b_ref, o_ref):
        o_ref[...] = (a_ref[...].astype(F32) + b_ref[...].astype(F32)).astype(BF16)

    return pl.pallas_call(
        body, name="grad_pair_sum", out_shape=_sds(got.shape, BF16),
        grid_spec=pltpu.PrefetchScalarGridSpec(
            num_scalar_prefetch=1, grid=(N_CHIPS, nb),
            in_specs=[pl.BlockSpec((None, tm, cols), lambda s, i, c_ref: (s, c_ref[0] * nb + i, 0)),
                      pl.BlockSpec((None, tm, cols), lambda s, i, c_ref: (s, i, 0))],
            out_specs=pl.BlockSpec((None, tm, cols), lambda s, i, c_ref: (s, i, 0))),
        compiler_params=_cparams(2),
    )(core, own, got)


def _chip_sum(slabs, tm):
    _, half, cols = slabs.shape

    def body(s_ref, o_ref):
        acc = s_ref[0].astype(F32)
        for k in range(1, N_CHIPS):
            acc = acc + s_ref[k].astype(F32)
        o_ref[...] = acc

    return pl.pallas_call(
        body, name="grad_chip_sum", out_shape=_sds((half, cols), F32), grid=(half // tm,),
        in_specs=[pl.BlockSpec((N_CHIPS, tm, cols), lambda i: (0, i, 0))],
        out_specs=pl.BlockSpec((tm, cols), lambda i: (i, 0)), compiler_params=_cparams(1),
    )(slabs)


def _adam_math(w, g, m, v):
    m2 = ADAM_B1 * m + (1.0 - ADAM_B1) * g
    v2 = ADAM_B2 * v + (1.0 - ADAM_B2) * (g * g)
    m_hat = m2 / (1.0 - ADAM_B1 ** ADAM_STEP)
    v_hat = v2 / (1.0 - ADAM_B2 ** ADAM_STEP)
    delta = -ADAM_LR * (m_hat / (jnp.sqrt(v_hat) + ADAM_EPS) + ADAM_WD * w)
    return delta, m2, v2


def _adamw_halves(name, items, tm, carried=None):
    rows, cols = items[0][0].shape
    nb = rows // 2 // tm
    n = len(items)

    def body(*refs):
        mine = (pl.program_id(0) // nb) == lax.axis_index("c")
        for k in range(n):
            w_ref, own_ref, oth_ref, m_ref, v_ref = refs[5 * k:5 * k + 5]
            g_ref, d_ref, m2_ref, v2_ref = refs[5 * n + 4 * k:5 * n + 4 * k + 4]
            g = jnp.where(mine, own_ref[...], oth_ref[...])
            d, m2, v2 = _adam_math(w_ref[...], g, m_ref[...], v_ref[...])
            g_ref[...] = g
            d_ref[...] = d
            m2_ref[...] = m2
            v2_ref[...] = v2

    full = pl.BlockSpec((tm, cols), lambda i: (i, 0))
    half = pl.BlockSpec((tm, cols), lambda i: (i % nb, 0))
    return _call_carrying(
        body, carried, name=name, grid=(rows // tm,), in_specs=[full, half, half, full, full] * n,
        out_specs=[full] * (4 * n), out_shape=tuple([_sds((rows, cols), F32)] * (4 * n)), scratch_shapes=[],
        operands=[a for item in items for a in item])


def _adamw(name, w, g, m, v, tm):
    def body(w_ref, g_ref, m_ref, v_ref, gout_ref, d_ref, m2_ref, v2_ref):
        gv = g_ref[...]
        d, m2, v2 = _adam_math(w_ref[...], gv, m_ref[...], v_ref[...])
        gout_ref[...] = gv
        d_ref[...] = d
        m2_ref[...] = m2
        v2_ref[...] = v2

    return _rows_call(name, body, tm, [w, g, m, v], [], [_sds(w.shape, F32)] * 4)


_SMALL_NAMES = ("norm_mix_pre", "gm_ln_w", "gm_ln_b", "gm_w_s", "gm_b_s", "conv_w", "conv_b", "dt_bias", "a_log",
                "d_skip", "ssm_norm_w", "norm_mix_post", "norm_ffn_pre", "norm_ffn_post")
_PACK_COLS = 1024


def _pack(parts, names=_SMALL_NAMES, tail=None):
    pieces = [parts[n].reshape(-1) for n in names]
    flat = jnp.concatenate(pieces if tail is None else pieces + [tail])
    rows = -(-flat.shape[0] // (8 * _PACK_COLS)) * 8
    flat = jnp.pad(flat, (0, rows * _PACK_COLS - flat.shape[0]))
    return flat.reshape(rows, _PACK_COLS)


def _unpack(packed, shapes, names=_SMALL_NAMES):
    flat = packed.reshape(-1)
    out, off = {}, 0
    for n in names:
        size = 1
        for s in shapes[n]:
            size *= s
        out[n] = flat[off:off + size].reshape(shapes[n])
        off += size
    return out


def kernel(x, norm_mix_pre, w_in, gm_ln_w, gm_ln_b, gm_w_s, gm_b_s, conv_w, conv_b, dt_bias, a_log, d_skip, ssm_norm_w, w_out, norm_mix_post, norm_ffn_pre, w_up, w_down, norm_ffn_post, loss_target, m_norm_mix_pre, m_w_in, m_gm_ln_w, m_gm_ln_b, m_gm_w_s, m_gm_b_s, m_conv_w, m_conv_b, m_dt_bias, m_a_log, m_d_skip, m_ssm_norm_w, m_w_out, m_norm_mix_post, m_norm_ffn_pre, m_w_up, m_w_down, m_norm_ffn_post, v_norm_mix_pre, v_w_in, v_gm_ln_w, v_gm_ln_b, v_gm_w_s, v_gm_b_s, v_conv_w, v_conv_b, v_dt_bias, v_a_log, v_d_skip, v_ssm_norm_w, v_w_out, v_norm_mix_post, v_norm_ffn_pre, v_w_up, v_w_down, v_norm_ffn_post):
    params = dict(norm_mix_pre=norm_mix_pre, w_in=w_in, gm_ln_w=gm_ln_w, gm_ln_b=gm_ln_b, gm_w_s=gm_w_s, gm_b_s=gm_b_s,
                  conv_w=conv_w, conv_b=conv_b, dt_bias=dt_bias, a_log=a_log, d_skip=d_skip, ssm_norm_w=ssm_norm_w,
                  w_out=w_out, norm_mix_post=norm_mix_post, norm_ffn_pre=norm_ffn_pre, w_up=w_up, w_down=w_down,
                  norm_ffn_post=norm_ffn_post)
    mom1 = dict(norm_mix_pre=m_norm_mix_pre, w_in=m_w_in, gm_ln_w=m_gm_ln_w, gm_ln_b=m_gm_ln_b, gm_w_s=m_gm_w_s,
                gm_b_s=m_gm_b_s, conv_w=m_conv_w, conv_b=m_conv_b, dt_bias=m_dt_bias, a_log=m_a_log, d_skip=m_d_skip,
                ssm_norm_w=m_ssm_norm_w, w_out=m_w_out, norm_mix_post=m_norm_mix_post, norm_ffn_pre=m_norm_ffn_pre,
                w_up=m_w_up, w_down=m_w_down, norm_ffn_post=m_norm_ffn_post)
    mom2 = dict(norm_mix_pre=v_norm_mix_pre, w_in=v_w_in, gm_ln_w=v_gm_ln_w, gm_ln_b=v_gm_ln_b, gm_w_s=v_gm_w_s,
                gm_b_s=v_gm_b_s, conv_w=v_conv_w, conv_b=v_conv_b, dt_bias=v_dt_bias, a_log=v_a_log, d_skip=v_d_skip,
                ssm_norm_w=v_ssm_norm_w, w_out=v_w_out, norm_mix_post=v_norm_mix_post, norm_ffn_pre=v_norm_ffn_pre,
                w_up=v_w_up, w_down=v_w_down, norm_ffn_post=v_norm_ffn_post)
    names = list(params)
    big = ("w_in", "w_out", "w_up", "w_down")
    chip = 2 * lax.axis_index("x") + lax.axis_index("y")

    shards = {n: params[n][0].astype(BF16) for n in big}
    conv_shard = jnp.pad(conv_w[0], ((0, 16 - CONV_K), (0, 0)))
    g_in4, g_conv4 = _run_exchange("allgather_w_in", _allgather_exchange([shards["w_in"], conv_shard]))
    conv_full = jnp.transpose(g_conv4[:, :CONV_K, :], (1, 0, 2)).reshape(CONV_K, CONV_CH)

    small = {n: params[n][0] if params[n].ndim >= 3 else params[n] for n in _SMALL_NAMES if n != "conv_w"}
    core = lax.axis_index("c").astype(jnp.int32).reshape(1)
    adam_args = {n: (params[n][0], mom1[n][0], mom2[n][0]) for n in big}
    loss, grad_x, big_out, small_sum = _forward_backward(
        x, loss_target, g_in4, conv_full, small, shards["w_out"], shards["w_up"], shards["w_down"], core, adam_args)
    grads, delta, new_m, new_v = {}, {}, {}, {}
    for n in big:
        grads[n], delta[n], new_m[n], new_v[n] = [a[None] for a in big_out[n]]

    small_sum["conv_w"] = lax.dynamic_slice_in_dim(small_sum["conv_w"], chip * (CONV_CH // N_CHIPS), CONV_CH // N_CHIPS, axis=1)

    local_shapes = {n: params[n].shape[1:] if params[n].ndim >= 3 else params[n].shape for n in _SMALL_NAMES}
    flat = lambda tree: {n: tree[n].reshape(local_shapes[n]) for n in _SMALL_NAMES}
    packed = [_pack(flat(t)) for t in (params, small_sum, mom1, mom2)]
    _, d_p, m_p, v_p = _adamw("adamw_small", *packed, packed[0].shape[0])
    for src, dst in ((d_p, delta), (m_p, new_m), (v_p, new_v)):
        for n, val in _unpack(src, local_shapes).items():
            dst[n] = val.reshape(params[n].shape)
    for n in _SMALL_NAMES:
        grads[n] = small_sum[n].reshape(params[n].shape)

    out = [loss, grad_x]
    for tree in (grads, delta, new_m, new_v):
        out += [tree[n] for n in names]
    return tuple(out)
```

```python
import functools

import jax
import jax.numpy as jnp
from jax import lax
from jax.experimental import pallas as pl
from jax.experimental.pallas import tpu as pltpu

F32 = jnp.float32
BF16 = jnp.bfloat16
MESH = pl.DeviceIdType.MESH

EPS = 1e-6
D_MODEL = 1024
GM_WIDTH = 512
SSM_WIDTH = 512
N_HEADS = 8
HEAD_DIM = 64
CHUNK = 128
SSM_GROUPS = 2
GROUP_W = SSM_WIDTH // SSM_GROUPS
SSM_STATE = 128
CONV_K = 4
CONV_CH = 1024
D_FF = 4096
IN_COLS = 2568
DT_PAD = 128
N_CHIPS = 4
N_DEV = 8

ADAM_LR = 0.001
ADAM_B1 = 0.9
ADAM_B2 = 0.999
ADAM_EPS = 1e-08
ADAM_WD = 0.01
ADAM_STEP = 10

VMEM_LIMIT_BYTES = 56 * 1024 * 1024
FF_TILE = 512
DW_TOKENS_PER_STEP = 2048


def _cparams(n_axes):
    return pltpu.CompilerParams(dimension_semantics=("arbitrary",) * n_axes, vmem_limit_bytes=VMEM_LIMIT_BYTES)


def _dot(a, b):
    return jnp.dot(a.astype(BF16), b.astype(BF16), preferred_element_type=F32)


def _dot_nt(a, b):
    return lax.dot_general(a.astype(BF16), b.astype(BF16), (((1,), (1,)), ((), ())), preferred_element_type=F32)


def _dot_tn(a, b):
    return lax.dot_general(a.astype(BF16), b.astype(BF16), (((0,), (0,)), ((), ())), preferred_element_type=F32)


def _sigmoid(x):
    return 1.0 / (1.0 + jnp.exp(-x))


_GELU_C = 0.7978845608028654
_GELU_A = 0.044715


def _gelu(x):
    t = jnp.tanh(_GELU_C * (x + _GELU_A * (x * x * x)))
    return 0.5 * x * (1.0 + t), t


def _gelu_grad(x, t):
    return 0.5 * (1.0 + t) + 0.5 * x * (1.0 - t * t) * (_GELU_C * (1.0 + 3.0 * _GELU_A * x * x))


def _rms_fwd(x, w):
    r = lax.rsqrt(jnp.mean(x * x, axis=-1, keepdims=True) + EPS)
    return x * r * w, r


def _rms_bwd(x, r, w, dy):
    g = dy * w
    dx = r * g - x * (r * r * r) * jnp.mean(g * x, axis=-1, keepdims=True)
    dw = jnp.sum(dy * x * r, axis=0, keepdims=True)
    return dx, dw


class _Carried:
    def __init__(self, ins, out_shapes, sems, start, finish):
        self.ins, self.out_shapes, self.sems = list(ins), list(out_shapes), list(sems)
        self.start, self.finish = start, finish


def _both(first, second):
    n_i, n_o, n_s = len(first.ins), len(first.out_shapes), len(first.sems)

    def split(ins, outs, sems):
        return (ins[:n_i], outs[:n_o], sems[:n_s]), (ins[n_i:], outs[n_o:], sems[n_s:])

    def start(ins, outs, *sems):
        (i1, o1, s1), (i2, o2, s2) = split(ins, outs, sems)
        first.start(i1, o1, *s1)
        second.start(i2, o2, *s2)

    def finish(ins, outs, *sems):
        (i1, o1, s1), (i2, o2, s2) = split(ins, outs, sems)
        first.finish(i1, o1, *s1)
        second.finish(i2, o2, *s2)

    return _Carried(first.ins + second.ins, first.out_shapes + second.out_shapes, first.sems + second.sems, start, finish)


def _split_carried(refs, n_in, n_out, n_scratch, carried):
    n_ci, n_co, n_cs = len(carried.ins), len(carried.out_shapes), len(carried.sems)
    ins, rest = refs[:n_in], refs[n_in:]
    c_ins, rest = rest[:n_ci], rest[n_ci:]
    outs, rest = rest[:n_out], rest[n_out:]
    c_outs, rest = rest[:n_co], rest[n_co:]
    scr, c_sems = rest[:n_scratch], rest[n_scratch:]
    assert len(c_sems) == n_cs
    return tuple(ins) + tuple(outs) + tuple(scr), c_ins, c_outs, c_sems


def _rows_call(name, body, tm, row_ins, const_ins, row_outs, acc_outs=(), scratch=(), carried=None):
    n_rows = row_ins[0].shape[0]
    assert n_rows % tm == 0
    n_steps = n_rows // tm
    n_in = len(row_ins) + len(const_ins)
    n_ro = len(row_outs)
    n_acc = len(acc_outs)

    def kern(*refs):
        accs = refs[n_in + n_ro:n_in + n_ro + n_acc]

        @pl.when(pl.program_id(0) == 0)
        def _():
            for a in accs:
                a[...] = jnp.zeros_like(a)

        body(*refs)

    def whole(shape):
        nd = len(shape)
        return pl.BlockSpec(tuple(shape), lambda i: (0,) * nd)

    in_specs = [pl.BlockSpec((tm, a.shape[1]), lambda i: (i, 0)) for a in row_ins]
    in_specs += [whole(a.shape) for a in const_ins]
    out_specs = [pl.BlockSpec((tm, s.shape[1]), lambda i: (i, 0)) for s in row_outs]
    out_specs += [whole(s.shape) for s in acc_outs]
    return _call_carrying(
        kern, carried, name=name, grid=(n_steps,), in_specs=in_specs, out_specs=out_specs,
        out_shape=tuple(row_outs) + tuple(acc_outs), scratch_shapes=list(scratch), operands=list(row_ins) + list(const_ins))


def _call_carrying(body, carried, *, name, grid, in_specs, out_specs, out_shape, scratch_shapes, operands):
    n_in, n_out, n_scratch = len(in_specs), len(out_specs), len(scratch_shapes)
    kern = body
    if carried is not None:
        def kern(*refs):
            plain, c_ins, c_outs, c_sems = _split_carried(refs, n_in, n_out, n_scratch, carried)
            first, last = True, True
            for d, size in enumerate(grid):
                first = jnp.logical_and(first, pl.program_id(d) == 0)
                last = jnp.logical_and(last, pl.program_id(d) == size - 1)

            @pl.when(first)
            def _():
                carried.start(c_ins, c_outs, *c_sems)

            body(*plain)

            @pl.when(last)
            def _():
                carried.finish(c_ins, c_outs, *c_sems)

        in_specs = list(in_specs) + [_HBM] * len(carried.ins)
        out_specs = list(out_specs) + [_HBM] * len(carried.out_shapes)
        out_shape = tuple(out_shape) + tuple(carried.out_shapes)
        operands = list(operands) + carried.ins
        scratch_shapes = list(scratch_shapes) + carried.sems
    return pl.pallas_call(
        kern, name=name, grid=grid, in_specs=in_specs, out_specs=out_specs, out_shape=out_shape,
        scratch_shapes=scratch_shapes, compiler_params=_cparams(len(grid)),
    )(*operands)


def _sds(shape, dtype):
    return jax.ShapeDtypeStruct(tuple(shape), dtype)


def _matmul_tn(name, a, b, tm, tn, tk, stacked=False, carried=None):
    k_dim, m_dim = a.shape
    n_dim = b.shape[1]
    assert m_dim % tm == 0 and n_dim % tn == 0 and k_dim % tk == 0
    nk = k_dim // tk

    def kern(a_ref, b_ref, o_ref, acc_ref):
        k = pl.program_id(2)
        prod = _dot_tn(a_ref[...], b_ref[...])

        @pl.when(k == 0)
        def _():
            acc_ref[...] = prod

        @pl.when(k > 0)
        def _():
            acc_ref[...] += prod

        @pl.when(k == nk - 1)
        def _():
            o_ref[...] = acc_ref[...].astype(o_ref.dtype)

    if stacked:
        assert tm == m_dim
        out_shape = _sds((n_dim // tn, m_dim, tn), BF16)
        out_spec = pl.BlockSpec((None, tm, tn), lambda i, j, k: (j, i, 0))
    else:
        out_shape = _sds((m_dim, n_dim), BF16)
        out_spec = pl.BlockSpec((tm, tn), lambda i, j, k: (i, j))
    outs = _call_carrying(
        kern, carried, name=name, grid=(m_dim // tm, n_dim // tn, nk),
        in_specs=[pl.BlockSpec((tk, tm), lambda i, j, k: (k, i)), pl.BlockSpec((tk, tn), lambda i, j, k: (k, j))],
        out_specs=[out_spec], out_shape=(out_shape,), scratch_shapes=[pltpu.VMEM((tm, tn), F32)], operands=[a, b])
    return outs[0] if carried is None else outs


def _inproj_fwd(x, nw, w_uv, w_xbc, w_z, w_dt, tm=256, carried=None):
    n_tok = x.shape[0]

    def body(x_ref, nw_ref, wuv_ref, wxbc_ref, wz_ref, wdt_ref, puv_ref, pxbc_ref, pz_ref, pdt_ref):
        h, _ = _rms_fwd(x_ref[...], nw_ref[...])
        h = h.astype(BF16)
        puv_ref[...] = jnp.dot(h, wuv_ref[...], preferred_element_type=F32)
        pxbc_ref[...] = jnp.dot(h, wxbc_ref[...], preferred_element_type=F32)
        pz_ref[...] = jnp.dot(h, wz_ref[...], preferred_element_type=F32)
        pdt_ref[...] = jnp.dot(h, wdt_ref[...], preferred_element_type=F32)

    return _rows_call(
        "inproj_fwd", body, tm, [x], [nw, w_uv, w_xbc, w_z, w_dt],
        [_sds((n_tok, 2 * GM_WIDTH), F32), _sds((n_tok, CONV_CH), F32), _sds((n_tok, SSM_WIDTH), F32),
         _sds((n_tok, DT_PAD), F32)], carried=carried)


def _head_lane_mask(width, head):
    lane = lax.broadcasted_iota(jnp.int32, (1, width), 1)
    return (lane // HEAD_DIM) == head


def _split_terms(x, terms):
    parts = []
    for _ in range(terms):
        p = x.astype(BF16)
        parts.append(p)
        x = x - p.astype(F32)
    return parts


def _seg_dots(vals, ind, terms=2):
    m = vals[0].shape[0]
    parts = []
    for v in vals:
        parts += _split_terms(v, terms)
    red = jnp.dot(jnp.concatenate(parts, axis=0), ind, preferred_element_type=F32)
    outs = []
    for i in range(len(vals)):
        acc = red[i * terms * m:(i * terms + 1) * m]
        for t in range(1, terms):
            acc = acc + red[(i * terms + t) * m:(i * terms + t + 1) * m]
        outs.append(acc)
    return outs


def _tri_dot(mask, x, terms=3):
    n = x.shape[1]
    red = jnp.dot(mask.astype(BF16), jnp.concatenate(_split_terms(x, terms), axis=1), preferred_element_type=F32)
    acc = red[:, :n]
    for t in range(1, terms):
        acc = acc + red[:, t * n:(t + 1) * n]
    return acc


def _gmlp_common(puv, lnw, lnb, e_bf, et_bf):
    u = puv[:, :GM_WIDTH]
    v = puv[:, GM_WIDTH:]
    gu, tu = _gelu(u)
    gv, tv = _gelu(v)
    (s1,) = _seg_dots([gv], et_bf)
    (mu,) = _seg_dots([s1 * (1.0 / HEAD_DIM)], e_bf)
    xc = gv - mu
    (s2,) = _seg_dots([xc * xc], et_bf)
    (rstd,) = _seg_dots([lax.rsqrt(s2 * (1.0 / HEAD_DIM) + EPS)], e_bf)
    xhat = xc * rstd
    vn = xhat * lnw + lnb
    return u, v, gu, tu, tv, rstd, xhat, vn


def _tril_mask():
    r = lax.broadcasted_iota(jnp.int32, (CHUNK, CHUNK), 0)
    c = lax.broadcasted_iota(jnp.int32, (CHUNK, CHUNK), 1)
    return r >= c


def _head_blocks(v):
    return jnp.concatenate([jnp.where(_head_lane_mask(GM_WIDTH, h), v, jnp.zeros_like(v)) for h in range(N_HEADS)], axis=0)


def _causal_w_cat(w_cat):
    t = lax.broadcasted_iota(jnp.int32, (CHUNK, N_HEADS * CHUNK), 0)
    s = lax.broadcasted_iota(jnp.int32, (CHUNK, N_HEADS * CHUNK), 1) % CHUNK
    return jnp.where(t >= s, w_cat, 0.0).astype(BF16)


def _gmlp_chunk_fwd(puv, lnw, lnb, e_bf, et_bf, wm, bmap):
    _, _, gu, _, _, _, _, vn = _gmlp_common(puv, lnw, lnb, e_bf, et_bf)
    mixed = jnp.dot(wm, _head_blocks(vn.astype(BF16)), preferred_element_type=F32) + bmap
    return (gu * mixed).astype(BF16)


SUBLANES = 8


def _shift_down(x, tail, s):
    main = pltpu.roll(x, s, 0)
    row = lax.broadcasted_iota(jnp.int32, (SUBLANES, 1), 0)
    head = jnp.where(row < s, pltpu.roll(tail, s, 0), main[:SUBLANES])
    return jnp.concatenate([head, main[SUBLANES:]], axis=0)


def _shift_up(x, head_next, s):
    n = x.shape[0]
    main = pltpu.roll(x, n - s, 0)
    row = lax.broadcasted_iota(jnp.int32, (SUBLANES, 1), 0)
    last = jnp.where(row >= SUBLANES - s, pltpu.roll(head_next, SUBLANES - s, 0), main[n - SUBLANES:])
    return jnp.concatenate([main[:n - SUBLANES], last], axis=0)


def _ssd_pre(xr, tail, cw_ref, cb, pdt, dtb, alog, emap):
    rowi = lax.broadcasted_iota(jnp.int32, (CHUNK, 1), 0)
    shifted = [_shift_down(xr, tail, 3), _shift_down(xr, tail, 2), _shift_down(xr, tail, 1), xr]
    xc = cb
    for k in range(CONV_K):
        xc = xc + cw_ref[k] * shifted[k]
    sg = _sigmoid(xc)
    xa = xc * sg
    pre = pdt + dtb
    dt = jnp.maximum(pre, 0.0) + jnp.log(1.0 + jnp.exp(-jnp.abs(pre)))
    a_neg = -jnp.exp(alog)
    a_cs = _tri_dot(_tril_mask(), dt * a_neg)
    acs_map, dt_map = _seg_dots([a_cs, dt], emap, terms=3)
    return dict(shifted=shifted, xc=xc, sg=sg, xa=xa, pre=pre, dt=dt, a_neg=a_neg, a_cs=a_cs,
                acs_map=acs_map, dt_map=dt_map, rowi=rowi)


def _ssd_maps(p):
    last = p["rowi"] == CHUNK - 1
    aq_map = jnp.sum(jnp.where(last, p["acs_map"], 0.0), axis=0, keepdims=True)
    e_exp = jnp.exp(p["acs_map"])
    dte = jnp.exp(aq_map - p["acs_map"])
    cd = jnp.exp(aq_map)
    return last, e_exp, dte, cd


def _head_decay(a_cs, a_cs_t, head, tri):
    lane = lax.broadcasted_iota(jnp.int32, (1, DT_PAD), 1)
    sub = lax.broadcasted_iota(jnp.int32, (DT_PAD, 1), 0)
    col = jnp.sum(jnp.where(lane == head, a_cs, 0.0), axis=1, keepdims=True)
    row = jnp.sum(jnp.where(sub == head, a_cs_t, 0.0), axis=0, keepdims=True)
    return jnp.exp(jnp.where(tri, col - row, -1e30))


def _gate_fwd(y, z, nw):
    sz = _sigmoid(z)
    zg = z * sz
    yg = y * zg
    outs, rs = [], []
    for g in range(SSM_GROUPS):
        gs = slice(g * GROUP_W, (g + 1) * GROUP_W)
        o, r = _rms_fwd(yg[:, gs], nw[:, gs])
        outs.append(o)
        rs.append(r)
    return sz, zg, yg, outs, rs


def _ssd_const_specs():
    def whole(shape):
        nd = len(shape)
        return pl.BlockSpec(tuple(shape), lambda c: (0,) * nd)
    return [whole((CONV_K, 1, CONV_CH)), whole((1, CONV_CH)), whole((1, DT_PAD)), whole((1, DT_PAD)),
            whole((1, SSM_WIDTH)), whole((1, SSM_WIDTH)), whole((DT_PAD, SSM_WIDTH)), whole((SSM_WIDTH, DT_PAD))]


def _mixer_fwd(p_uv, p_xbc, p_z, p_dt, x, lnw, lnb, w_cat, bmap, w_out, nw_post, nw_pre2, conv_w, conv_b, dt_bias, a_log,
               dskip_map, norm_w, e_bf, et_bf, n_seq, carried=None):
    n_tok = p_xbc.shape[0]
    nc = n_tok // n_seq // CHUNK

    def body(puv3, xr3, z3, pdt3, x3, lnw_ref, lnb_ref, wcat_ref, bmap_ref, wo_ref, nwa_ref, nwb_ref,
             cw_ref, cb_ref, dtb_ref, alog_ref, dsk_ref, nw_ref, e_ref, et_ref,
             mix3, yssd3, sprev3, o3, x13, h23, wm_scr, prev3_scr, s3_scr):
        @pl.when(pl.program_id(0) == 0)
        def _():
            wm_scr[...] = _causal_w_cat(wcat_ref[...])
            prev3_scr[...] = jnp.zeros_like(prev3_scr)
            s3_scr[...] = jnp.zeros_like(s3_scr)

        for b in range(n_seq):
            one_sequence(puv3.at[b], xr3.at[b], z3.at[b], pdt3.at[b], lnw_ref, lnb_ref, bmap_ref,
                         cw_ref, cb_ref, dtb_ref, alog_ref, dsk_ref, nw_ref, e_ref, et_ref,
                         mix3.at[b], yssd3.at[b], sprev3.at[b], wm_scr, prev3_scr.at[b], s3_scr.at[b])
            o = jnp.dot(mix3[b], wo_ref[...], preferred_element_type=F32)
            on, _ = _rms_fwd(o, nwa_ref[...])
            x1 = x3[b] + on
            h2, _ = _rms_fwd(x1, nwb_ref[...])
            o3[b] = o
            x13[b] = x1
            h23[b] = h2.astype(BF16)

    def one_sequence(puv_ref, xr_ref, z_ref, pdt_ref, lnw_ref, lnb_ref, bmap_ref,
                     cw_ref, cb_ref, dtb_ref, alog_ref, dsk_ref, nw_ref, e_ref, et_ref,
                     mix_ref, yssd_ref, sprev_ref, wm_scr, prev_scr, s_scr):
        mix_ref[:, :GM_WIDTH] = _gmlp_chunk_fwd(puv_ref[...], lnw_ref[...], lnb_ref[...], e_ref[...], et_ref[...], wm_scr[...],
                                      bmap_ref[...])
        xr = xr_ref[...]
        p = _ssd_pre(xr, prev_scr[...], cw_ref, cb_ref[...], pdt_ref[...], dtb_ref[...], alog_ref[...], e_ref[...])
        _, e_exp, dte, cd = _ssd_maps(p)
        xs = p["xa"][:, :SSM_WIDTH]
        xd = xs * p["dt_map"]
        a_cs_t = p["a_cs"].T
        tri = _tril_mask()
        s_old = s_scr[...]
        sprev_ref[...] = s_old
        for g in range(SSM_GROUPS):
            gs = slice(g * GROUP_W, (g + 1) * GROUP_W)
            bm = p["xa"][:, SSM_WIDTH + g * SSM_STATE: SSM_WIDTH + (g + 1) * SSM_STATE].astype(BF16)
            cm = p["xa"][:, SSM_WIDTH + (SSM_GROUPS + g) * SSM_STATE: SSM_WIDTH + (SSM_GROUPS + g + 1) * SSM_STATE].astype(BF16)
            cb_mat = _dot_nt(cm, bm)
            xdg = xd[:, gs].astype(BF16)
            y_g = _dot(cm, s_old[:, gs]) * e_exp[:, gs] + dsk_ref[:, gs] * xs[:, gs]
            for r in range(SSM_GROUPS * 2):
                dm = _head_decay(p["a_cs"], a_cs_t, g * 4 + r, tri)
                full = jnp.dot((cb_mat * dm).astype(BF16), xdg, preferred_element_type=F32)
                y_g = y_g + jnp.where(_head_lane_mask(GROUP_W, r), full, 0.0)
            yssd_ref[:, gs] = y_g
            s_scr[:, gs] = cd[:, gs] * s_old[:, gs] + _dot_tn(bm, xd[:, gs] * dte[:, gs])
        _, _, _, outs, _ = _gate_fwd(yssd_ref[...], z_ref[...], nw_ref[...])
        for g in range(SSM_GROUPS):
            mix_ref[:, GM_WIDTH + g * GROUP_W:GM_WIDTH + (g + 1) * GROUP_W] = outs[g].astype(BF16)
        prev_scr[...] = xr[CHUNK - SUBLANES:, :]

    seq_len = n_tok // n_seq

    def rows(width):
        return pl.BlockSpec((n_seq, CHUNK, width), lambda c: (0, c, 0))

    def whole(shape):
        nd = len(shape)
        return pl.BlockSpec(tuple(shape), lambda c: (0,) * nd)

    def by_seq(a):
        return a.reshape(n_seq, seq_len, a.shape[-1])

    outs = _call_carrying(
        body, carried, name="mixer_fwd", grid=(nc,),
        in_specs=[rows(2 * GM_WIDTH), rows(CONV_CH), rows(SSM_WIDTH), rows(DT_PAD), rows(D_MODEL), whole(lnw.shape),
                  whole(lnb.shape), whole(w_cat.shape), whole(bmap.shape), whole(w_out.shape), whole(nw_post.shape),
                  whole(nw_pre2.shape)] + _ssd_const_specs(),
        out_specs=[rows(D_MODEL), rows(SSM_WIDTH), rows(SSM_WIDTH), rows(D_MODEL), rows(D_MODEL), rows(D_MODEL)],
        out_shape=(_sds((n_seq, seq_len, D_MODEL), BF16),
                   _sds((n_seq, seq_len, SSM_WIDTH), F32), _sds((n_seq, seq_len, SSM_WIDTH), F32),
                   _sds((n_seq, seq_len, D_MODEL), F32), _sds((n_seq, seq_len, D_MODEL), F32),
                   _sds((n_seq, seq_len, D_MODEL), BF16)),
        scratch_shapes=[pltpu.VMEM((CHUNK, N_HEADS * CHUNK), BF16), pltpu.VMEM((n_seq, SUBLANES, CONV_CH), F32),
                        pltpu.VMEM((n_seq, SSM_STATE, SSM_WIDTH), F32)],
        operands=[by_seq(p_uv), by_seq(p_xbc), by_seq(p_z), by_seq(p_dt), by_seq(x), lnw, lnb, w_cat, bmap, w_out, nw_post,
                  nw_pre2, conv_w, conv_b, dt_bias, a_log, dskip_map, norm_w, e_bf, et_bf])
    return tuple(o.reshape(n_tok, o.shape[-1]) for o in outs[:6]) + tuple(outs[6:])


def _up_cols(wup_ref, j):
    per = (D_FF // N_CHIPS) // FF_TILE
    return wup_ref[j // per, :, (j % per) * FF_TILE:(j % per + 1) * FF_TILE]


def _down_rows(wda_ref, wdb_ref, j):
    assert 2 * FF_TILE == D_FF // N_CHIPS
    return (wda_ref if j % 2 == 0 else wdb_ref)[j // 2]


def _skewed_rows_call(name, main, tail, tm, lead_ins, lag_ins, const_ins, lead_outs, lag_outs, acc_outs, carry,
                      streamed, tile_copies, n_copies):
    n_rows = lead_ins[0].shape[0]
    assert n_rows % tm == 0
    n = n_rows // tm
    counts = [len(lead_ins), len(lag_ins), len(const_ins), len(streamed), len(lead_outs), len(lag_outs), len(acc_outs),
              1, len(streamed)]

    def kern(*refs):
        groups, pos = [], 0
        for cnt in counts:
            groups.append(refs[pos:pos + cnt])
            pos += cnt
        lead_i, lag_i, consts, w_hbm, lead_o, lag_o, accs, (carry_scr,), w_vmem = groups
        sems = refs[pos]
        i = pl.program_id(0)
        pieces, k = [], 0
        for piece in tile_copies(w_hbm, w_vmem):
            pieces.append([pltpu.make_async_copy(src, dst, sems.at[k + q]) for q, (src, dst) in enumerate(piece)])
            k += len(piece)

        def ready(j):
            for cp in pieces[j]:
                cp.wait()

        @pl.when(i == 0)
        def _():
            for piece in pieces:
                for cp in piece:
                    cp.start()
            for a in accs:
                a[...] = jnp.zeros_like(a)
            carry_scr[...] = main(lead_i, consts, lead_o, w_vmem, ready)

        @pl.when(jnp.logical_and(i > 0, i < n))
        def _():
            previous = carry_scr[...]
            carry_scr[...] = main(lead_i, consts, lead_o, w_vmem, lambda j: None)
            tail(previous, lag_i, consts, lag_o, accs)

        @pl.when(i == n)
        def _():
            tail(carry_scr[...], lag_i, consts, lag_o, accs)

    def lead(width):
        return pl.BlockSpec((tm, width), lambda i: (jnp.minimum(i, n - 1), 0))

    def lag(width):
        return pl.BlockSpec((tm, width), lambda i: (jnp.maximum(i - 1, 0), 0))

    def whole(shape):
        nd = len(shape)
        return pl.BlockSpec(tuple(shape), lambda i: (0,) * nd)

    return pl.pallas_call(
        kern, name=name, grid=(n + 1,),
        in_specs=([lead(a.shape[1]) for a in lead_ins] + [lag(a.shape[1]) for a in lag_ins]
                  + [whole(a.shape) for a in const_ins] + [_HBM] * len(streamed)),
        out_specs=[lead(s.shape[1]) for s in lead_outs] + [lag(s.shape[1]) for s in lag_outs] + [whole(s.shape) for s in acc_outs],
        out_shape=tuple(lead_outs) + tuple(lag_outs) + tuple(acc_outs),
        scratch_shapes=([pltpu.VMEM(carry, F32)] + [pltpu.VMEM(a.shape, a.dtype) for a in streamed]
                        + [pltpu.SemaphoreType.DMA((n_copies,))]),
        compiler_params=_cparams(1),
    )(*lead_ins, *lag_ins, *const_ins, *streamed)


def _mlp_weight_pieces(order):
    per = (D_FF // N_CHIPS) // FF_TILE

    def tile_copies(hbm, vmem):
        pieces = []
        for j in range(D_FF // FF_TILE):
            cols = (j // per, slice(None), pl.ds((j % per) * FF_TILE, FF_TILE))
            up = (hbm[0].at[cols], vmem[0].at[cols])
            down = (hbm[1 + j % 2].at[j // 2], vmem[1 + j % 2].at[j // 2])
            pieces.append([up, down] if order == "up_down" else [down, up])
        return pieces

    return tile_copies


def _mlp_fwd(h2, x1, tgt, w_up, w_down_a, w_down_b, nw, tm=512):
    n_tok = x1.shape[0]

    def main(lead_i, consts, lead_o, weights, ready):
        (h2_ref,), (f_ref,), (wup_ref, wda_ref, wdb_ref) = lead_i, lead_o, weights
        h2v = h2_ref[...]
        acc = jnp.zeros((tm, D_MODEL), F32)
        for j in range(D_FF // FF_TILE):
            cs = slice(j * FF_TILE, (j + 1) * FF_TILE)
            ready(j)
            u = jnp.dot(h2v, _up_cols(wup_ref, j), preferred_element_type=F32)
            f = jnp.square(jnp.maximum(u, 0.0)).astype(BF16)
            f_ref[:, cs] = f
            acc = acc + jnp.dot(f, _down_rows(wda_ref, wdb_ref, j), preferred_element_type=F32)
        return acc

    def tail(acc, lag_i, consts, lag_o, accs):
        (x1_ref, tgt_ref), (nw_ref,), (dd_ref, dy_ref), (loss_ref, dnw_ref) = lag_i, consts, lag_o, accs
        dn, r = _rms_fwd(acc, nw_ref[...])
        e = x1_ref[...] + dn - tgt_ref[...]
        loss_ref[...] += jnp.full(loss_ref.shape, (0.5 / D_MODEL) * jnp.sum(e * e), F32)
        dy = e * (1.0 / D_MODEL)
        dd, dnw = _rms_bwd(acc, r, nw_ref[...], dy)
        dy_ref[...] = dy
        dd_ref[...] = dd.astype(BF16)
        dnw_ref[...] += dnw

    return _skewed_rows_call(
        "mlp_fwd", main, tail, tm, [h2], [x1, tgt], [nw],
        [_sds((n_tok, D_FF), BF16)], [_sds((n_tok, D_MODEL), BF16), _sds((n_tok, D_MODEL), F32)],
        [_sds((8, 128), F32), _sds((1, D_MODEL), F32)], carry=(tm, D_MODEL),
        streamed=[w_up, w_down_a, w_down_b], tile_copies=_mlp_weight_pieces("up_down"), n_copies=2 * (D_FF // FF_TILE))


def _mlp_bwd(dd, f, x1, dy, w_down_a, w_down_b, w_up, nw, tm=256):
    n_tok = x1.shape[0]

    def main(lead_i, consts, lead_o, weights, ready):
        (dd_ref, f_ref), (dup_ref,), (wup_ref, wda_ref, wdb_ref) = lead_i, lead_o, weights
        ddv = dd_ref[...]
        acc = jnp.zeros((tm, D_MODEL), F32)
        for j in range(D_FF // FF_TILE):
            cs = slice(j * FF_TILE, (j + 1) * FF_TILE)
            ready(j)
            df = _dot_nt(ddv, _down_rows(wda_ref, wdb_ref, j))
            du = (df * (2.0 * jnp.sqrt(f_ref[:, cs].astype(F32)))).astype(BF16)
            dup_ref[:, cs] = du
            acc = acc + _dot_nt(du, _up_cols(wup_ref, j))
        return acc

    def tail(acc, lag_i, consts, lag_o, accs):
        (x1_ref, dy_ref), (nw_ref,), (dx1_ref,), (dnw_ref,) = lag_i, consts, lag_o, accs
        x1v = x1_ref[...]
        _, r = _rms_fwd(x1v, nw_ref[...])
        dx, dnw = _rms_bwd(x1v, r, nw_ref[...], acc)
        dx1_ref[...] = dy_ref[...] + dx
        dnw_ref[...] += dnw

    return _skewed_rows_call(
        "mlp_bwd", main, tail, tm, [dd, f], [x1, dy], [nw],
        [_sds((n_tok, D_FF), BF16)], [_sds((n_tok, D_MODEL), F32)], [_sds((1, D_MODEL), F32)], carry=(tm, D_MODEL),
        streamed=[w_up, w_down_a, w_down_b], tile_copies=_mlp_weight_pieces("down_up"), n_copies=2 * (D_FF // FF_TILE))


def _outproj_bwd(dx1, o, w_out, nw, tm=256, carried=None):
    n_tok = dx1.shape[0]

    def body(dx1_ref, o_ref, wo_ref, nw_ref, do_ref, dya_ref, dyb_ref, dnw_ref):
        ov = o_ref[...]
        _, r = _rms_fwd(ov, nw_ref[...])
        do, dnw = _rms_bwd(ov, r, nw_ref[...], dx1_ref[...])
        dob = do.astype(BF16)
        do_ref[...] = dob
        dya_ref[...] = _dot_nt(dob, wo_ref[:GM_WIDTH, :])
        dyb_ref[...] = _dot_nt(dob, wo_ref[GM_WIDTH:, :])
        dnw_ref[...] += dnw

    return _rows_call("outproj_bwd", body, tm, [dx1, o], [w_out, nw],
                      [_sds((n_tok, D_MODEL), BF16), _sds((n_tok, GM_WIDTH), F32), _sds((n_tok, SSM_WIDTH), F32)],
                      [_sds((1, D_MODEL), F32)], carried=carried)


def _gmlp_bwd(p_uv, dya, lnw, lnb, e_bf, et_bf, w_cat, w_stack, bmap, carried=None):
    n_tok = p_uv.shape[0]
    chunks_per_step = 2

    def body(puv_ref, dya_ref, lnw_ref, lnb_ref, e_ref, et_ref, wcat_ref, wstack_ref, bmap_ref,
             dpuv_ref, dws_ref, dbs_ref, dlnw_ref, dlnb_ref, wm_scr, wsm_scr):
        t_stk = lax.broadcasted_iota(jnp.int32, (N_HEADS * CHUNK, CHUNK), 0) % CHUNK
        s_stk = lax.broadcasted_iota(jnp.int32, (N_HEADS * CHUNK, CHUNK), 1)

        @pl.when(pl.program_id(0) == 0)
        def _():
            wm_scr[...] = _causal_w_cat(wcat_ref[...])
            wsm_scr[...] = jnp.where(t_stk >= s_stk, wstack_ref[...], 0.0).astype(BF16)

        lnw_v = lnw_ref[...]
        e_v, et_v = e_ref[...], et_ref[...]

        def one_chunk(rows):
            u, v, gu, tu, tv, rstd, xhat, vn = _gmlp_common(puv_ref[rows, :], lnw_v, lnb_ref[...], e_v, et_v)
            vnb = vn.astype(BF16)
            mixed = jnp.dot(wm_scr[...], _head_blocks(vnb), preferred_element_type=F32) + bmap_ref[...]
            dy = dya_ref[rows, :]
            du = dy * mixed * _gelu_grad(u, tu)
            dmixed = dy * gu
            (dbs,) = _seg_dots([dmixed], et_v)
            dblocks = _head_blocks(dmixed.astype(BF16))
            dvn = lax.dot_general(wsm_scr[...], dblocks, (((0,), (0,)), ((), ())), preferred_element_type=F32)
            dws = lax.dot_general(dblocks, vnb, (((1,), (1,)), ((), ())), preferred_element_type=F32)
            dxh = dvn * lnw_v
            m1, m2 = _seg_dots([dxh, dxh * xhat], et_v)
            m1, m2 = _seg_dots([m1 * (1.0 / HEAD_DIM), m2 * (1.0 / HEAD_DIM)], e_v)
            dgv = rstd * (dxh - m1 - xhat * m2)
            dv = dgv * _gelu_grad(v, tv)
            dpuv_ref[rows, :GM_WIDTH] = du.astype(BF16)
            dpuv_ref[rows, GM_WIDTH:] = dv.astype(BF16)
            return dbs, dws, jnp.sum(dvn * xhat, axis=0, keepdims=True), jnp.sum(dvn, axis=0, keepdims=True)

        parts = [one_chunk(slice(k * CHUNK, (k + 1) * CHUNK)) for k in range(chunks_per_step)]
        dbs, dws, dlnw, dlnb = [functools.reduce(lambda a, b: a + b, vals) for vals in zip(*parts)]
        dbs_ref[...] += dbs
        dws_ref[...] += jnp.where(t_stk >= s_stk, dws, 0.0)
        dlnw_ref[...] += dlnw
        dlnb_ref[...] += dlnb

    return _rows_call(
        "gmlp_bwd", body, chunks_per_step * CHUNK, [p_uv, dya], [lnw, lnb, e_bf, et_bf, w_cat, w_stack, bmap],
        [_sds((n_tok, 2 * GM_WIDTH), BF16)],
        [_sds((N_HEADS * CHUNK, CHUNK), F32), _sds((CHUNK, DT_PAD), F32), _sds((1, GM_WIDTH), F32),
         _sds((1, GM_WIDTH), F32)],
        scratch=[pltpu.VMEM((CHUNK, N_HEADS * CHUNK), BF16), pltpu.VMEM((N_HEADS * CHUNK, CHUNK), BF16)],
        carried=carried)


def _ssd_bwd(p_xbc, p_z, p_dt, yssd, sprev, dyb, conv_w, conv_b, dt_bias, a_log, dskip_map, norm_w, e_bf, et_bf, n_seq,
             carried=None):
    n_tok = p_xbc.shape[0]
    nc = n_tok // n_seq // CHUNK

    def body(xr3, xprev3, z3, pdt3, yssd3, sprev3, dyb3,
             cw_ref, cb_ref, dtb_ref, alog_ref, dsk_ref, nw_ref, e_ref, et_ref,
             dpxbc3, dpz3, dpdt3, dcw_ref, dcb_ref, ddtb_ref, dalog_ref, ddsk_ref, dnw_ref,
             ds3_scr, nxt3_scr, dxa3_scr):
        @pl.when(pl.program_id(0) == 0)
        def _():
            for a in (dcw_ref, dcb_ref, ddtb_ref, dalog_ref, ddsk_ref, dnw_ref, ds3_scr, nxt3_scr):
                a[...] = jnp.zeros_like(a)

        for b in range(n_seq):
            one_sequence(xr3.at[b], xprev3.at[b], z3.at[b], pdt3.at[b], yssd3.at[b], sprev3.at[b], dyb3.at[b],
                         cw_ref, cb_ref, dtb_ref, alog_ref, dsk_ref, nw_ref, e_ref, et_ref,
                         dpxbc3.at[b], dpz3.at[b], dpdt3.at[b], dcw_ref, dcb_ref, ddtb_ref, dalog_ref, ddsk_ref, dnw_ref,
                         ds3_scr.at[b], nxt3_scr.at[b], dxa3_scr.at[b])

    def one_sequence(xr_ref, xprev_ref, z_ref, pdt_ref, yssd_ref, sprev_ref, dyb_ref,
                     cw_ref, cb_ref, dtb_ref, alog_ref, dsk_ref, nw_ref, e_ref, et_ref,
                     dpxbc_ref, dpz_ref, dpdt_ref, dcw_ref, dcb_ref, ddtb_ref, dalog_ref, ddsk_ref, dnw_ref,
                     ds_scr, nxt_scr, dxa_scr):
        chunk = nc - 1 - pl.program_id(0)
        xr = xr_ref[...]
        prev = jnp.where(chunk == 0, 0.0, xprev_ref[...])
        et_v = et_ref[...]
        p = _ssd_pre(xr, prev, cw_ref, cb_ref[...], pdt_ref[...], dtb_ref[...], alog_ref[...], e_ref[...])
        last, e_exp, dte, cd = _ssd_maps(p)
        rowi = p["rowi"]
        xs = p["xa"][:, :SSM_WIDTH]
        xd = xs * p["dt_map"]
        a_cs_t = p["a_cs"].T
        tri = _tril_mask()
        dsk = dsk_ref[...]
        nw_v = nw_ref[...]

        yv = yssd_ref[...]
        zv = z_ref[...]
        sz, zg, yg, _, rs = _gate_fwd(yv, zv, nw_v)
        dout = dyb_ref[...]
        for g in range(SSM_GROUPS):
            gs = slice(g * GROUP_W, (g + 1) * GROUP_W)
            dyg_g, dnw_g = _rms_bwd(yg[:, gs], rs[g], nw_v[:, gs], dout[:, gs])
            dnw_ref[:, gs] += dnw_g
            dxa_scr[:, gs] = dyg_g
        dyg = dxa_scr[:, :SSM_WIDTH]
        d_y = dyg * zg
        dpz_ref[...] = (dyg * yv * (sz + zv * sz * (1.0 - sz))).astype(BF16)

        s_prev = sprev_ref[...]
        ds_next = ds_scr[...]
        lane_dt = lax.broadcasted_iota(jnp.int32, (1, DT_PAD), 1)
        da_cols = jnp.zeros((CHUNK, DT_PAD), F32)
        for g in range(SSM_GROUPS):
            gs = slice(g * GROUP_W, (g + 1) * GROUP_W)
            b_off = SSM_WIDTH + g * SSM_STATE
            c_off = SSM_WIDTH + (SSM_GROUPS + g) * SSM_STATE
            bm = p["xa"][:, b_off:b_off + SSM_STATE].astype(BF16)
            cm = p["xa"][:, c_off:c_off + SSM_STATE].astype(BF16)
            cb_mat = _dot_nt(cm, bm)
            d_yg = d_y[:, gs]
            d_ygb = d_yg.astype(BF16)
            xdg = xd[:, gs]
            xdgb = xdg.astype(BF16)
            ds_g = ds_next[:, gs]
            sp_g = s_prev[:, gs]
            bds = _dot(bm, ds_g)
            dcs = d_yg * e_exp[:, gs]
            d_c = _dot_nt(dcs, sp_g)
            ds_scr[:, gs] = cd[:, gs] * ds_g + _dot_tn(cm, dcs)
            d_b = _dot_nt(xdg * dte[:, gs], ds_g)
            dxd_g = bds * dte[:, gs]
            sum_dcb = jnp.zeros((CHUNK, CHUNK), F32)
            for r in range(SSM_GROUPS * 2):
                head = g * 4 + r
                mask = _head_lane_mask(GROUP_W, r)
                dm = _head_decay(p["a_cs"], a_cs_t, head, tri)
                m_mat = cb_mat * dm
                g_mat = _dot_nt(jnp.where(mask, d_yg, 0.0), xdgb)
                w_mat = g_mat * m_mat
                sum_dcb = sum_dcb + g_mat * dm
                dxd_g = dxd_g + jnp.where(mask, _dot_tn(m_mat, d_ygb), 0.0)
                da_h = jnp.sum(w_mat - w_mat.T, axis=1, keepdims=True)
                da_cols = da_cols + jnp.where(lane_dt == head, da_h, 0.0)
            d_c = d_c + _dot(sum_dcb, bm)
            d_b = d_b + _dot_tn(sum_dcb, cm)
            dxa_scr[:, b_off:b_off + SSM_STATE] = d_b
            dxa_scr[:, c_off:c_off + SSM_STATE] = d_c
            y_off_g = _dot(cm, sp_g) * e_exp[:, gs]
            t3 = bds * xdg * dte[:, gs]
            tail = jnp.sum(t3, axis=0, keepdims=True) + jnp.sum(ds_g * sp_g, axis=0, keepdims=True) * cd[:, gs]
            pre_g = d_yg * y_off_g - t3 + jnp.where(last, tail, 0.0)
            s_pre, ddt_g, s_dsk = _seg_dots([pre_g, dxd_g * xs[:, gs], d_yg * xs[:, gs]], et_v[gs, :])
            da_cols = da_cols + s_pre
            ddsk_ref[...] += jnp.sum(s_dsk, axis=0, keepdims=True)
            dxa_scr[:, gs] = dxd_g * p["dt_map"][:, gs] + dsk[:, gs] * d_yg
            if g == 0:
                ddt = ddt_g
            else:
                ddt = ddt + ddt_g
        r_i = lax.broadcasted_iota(jnp.int32, (CHUNK, CHUNK), 0)
        c_i = lax.broadcasted_iota(jnp.int32, (CHUNK, CHUNK), 1)
        ddta = _tri_dot(r_i <= c_i, da_cols, terms=2)
        ddt = ddt + ddta * p["a_neg"]
        dalog_ref[...] += jnp.sum(ddta * p["dt"], axis=0, keepdims=True) * p["a_neg"]
        draw = ddt * _sigmoid(p["pre"])
        ddtb_ref[...] += jnp.sum(draw, axis=0, keepdims=True)
        dpdt_ref[...] = draw.astype(BF16)

        xc = p["xc"]
        sg = p["sg"]
        dxc = dxa_scr[...] * (sg + xc * sg * (1.0 - sg))
        dcb_ref[...] += jnp.sum(dxc, axis=0, keepdims=True)
        for k in range(CONV_K):
            dcw_ref[k] += jnp.sum(dxc * p["shifted"][k], axis=0, keepdims=True)
        nxt = nxt_scr[...]
        dxr = cw_ref[3] * dxc
        for s in range(1, CONV_K):
            dxr = dxr + cw_ref[CONV_K - 1 - s] * _shift_up(dxc, nxt, s)
        dpxbc_ref[...] = dxr.astype(BF16)
        nxt_scr[...] = dxc[:SUBLANES, :]

    seq_len = n_tok // n_seq

    def rows(width):
        return pl.BlockSpec((n_seq, CHUNK, width), lambda s: (0, nc - 1 - s, 0))

    tiles = CHUNK // SUBLANES
    prev_rows = pl.BlockSpec((n_seq, SUBLANES, CONV_CH), lambda s: (0, jnp.maximum((nc - 1 - s) * tiles - 1, 0), 0))

    def whole(shape):
        nd = len(shape)
        return pl.BlockSpec(tuple(shape), lambda s: (0,) * nd)

    def by_seq(a):
        return a.reshape(n_seq, seq_len, a.shape[-1])

    acc_shapes = [(CONV_K, 1, CONV_CH), (1, CONV_CH), (1, DT_PAD), (1, DT_PAD), (1, DT_PAD), (1, SSM_WIDTH)]
    xbc3 = by_seq(p_xbc)
    outs = _call_carrying(
        body, carried, name="ssd_bwd", grid=(nc,),
        in_specs=[rows(CONV_CH), prev_rows, rows(SSM_WIDTH), rows(DT_PAD), rows(SSM_WIDTH), rows(SSM_WIDTH),
                  rows(SSM_WIDTH)] + _ssd_const_specs(),
        out_specs=[rows(CONV_CH), rows(SSM_WIDTH), rows(DT_PAD)] + [whole(s) for s in acc_shapes],
        out_shape=tuple([_sds((n_seq, seq_len, CONV_CH), BF16), _sds((n_seq, seq_len, SSM_WIDTH), BF16),
                         _sds((n_seq, seq_len, DT_PAD), BF16)] + [_sds(s, F32) for s in acc_shapes]),
        scratch_shapes=[pltpu.VMEM((n_seq, SSM_STATE, SSM_WIDTH), F32), pltpu.VMEM((n_seq, SUBLANES, CONV_CH), F32),
                        pltpu.VMEM((n_seq, CHUNK, CONV_CH), F32)],
        operands=[xbc3, xbc3, by_seq(p_z), by_seq(p_dt), by_seq(yssd), by_seq(sprev), by_seq(dyb), conv_w, conv_b, dt_bias,
                  a_log, dskip_map, norm_w, e_bf, et_bf])
    return tuple(o.reshape(n_tok, o.shape[-1]) for o in outs[:3]) + tuple(outs[3:])


def _inproj_bwd(dp_uv, dp_xbc, dp_z, dp_dt, x, dx1, w_uv, w_xbc, w_z, w_dt, nw, tm=256, carried=None):
    n_tok = x.shape[0]

    def body(duv_ref, dxbc_ref, dz_ref, ddt_ref, x_ref, dx1_ref, wuv_ref, wxbc_ref, wz_ref, wdt_ref, nw_ref,
             gx_ref, h_ref, dnw_ref):
        dh = _dot_nt(duv_ref[...], wuv_ref[...]) + _dot_nt(dxbc_ref[...], wxbc_ref[...])
        dh = dh + _dot_nt(dz_ref[...], wz_ref[...]) + _dot_nt(ddt_ref[...], wdt_ref[...])
        xv = x_ref[...]
        h, r = _rms_fwd(xv, nw_ref[...])
        dx, dnw = _rms_bwd(xv, r, nw_ref[...], dh)
        gx_ref[...] = dx1_ref[...] + dx
        h_ref[...] = h.astype(BF16)
        dnw_ref[...] += dnw

    return _rows_call("inproj_bwd", body, tm, [dp_uv, dp_xbc, dp_z, dp_dt, x, dx1], [w_uv, w_xbc, w_z, w_dt, nw],
                      [_sds((n_tok, D_MODEL), F32), _sds((n_tok, D_MODEL), BF16)], [_sds((1, D_MODEL), F32)],
                      carried=carried)


def _const_maps():
    lane = jnp.arange(SSM_WIDTH) // HEAD_DIM
    e_bf = (jnp.arange(DT_PAD)[:, None] == lane[None, :]).astype(BF16)
    return e_bf, e_bf.T


def _pad_lanes(v, width):
    return jnp.pad(v, ((0, 0), (0, width - v.shape[1])))


SHARD_COLS = IN_COLS // N_CHIPS
_UV_END = 2 * GM_WIDTH
_Z_END = _UV_END + SSM_WIDTH
_XBC_END = _Z_END + CONV_CH


def _cols_from_shards(w4, lo, hi):
    pieces = []
    for j in range(N_CHIPS):
        a, b = max(lo, j * SHARD_COLS), min(hi, (j + 1) * SHARD_COLS)
        if a < b:
            pieces.append(w4[j][:, a - j * SHARD_COLS:b - j * SHARD_COLS])
    return pieces[0] if len(pieces) == 1 else jnp.concatenate(pieces, axis=1)


def _shards_from_cols(blocks):
    shards = []
    for j in range(N_CHIPS):
        pieces = []
        for arr, lo, hi in blocks:
            a, b = max(lo, j * SHARD_COLS), min(hi, (j + 1) * SHARD_COLS)
            if a < b:
                pieces.append(arr[:, a - lo:b - lo])
        shards.append(pieces[0] if len(pieces) == 1 else jnp.concatenate(pieces, axis=1))
    return jnp.stack(shards)


def _forward_backward(x, tgt, w_in4, conv_w, small, out_shard, up_shard, down_shard, core, adam_args):
    n_seq, seq_len, _ = x.shape
    n_tok = n_seq * seq_len
    x2 = x.reshape(n_tok, D_MODEL)
    tgt2 = tgt.reshape(n_tok, D_MODEL)
    e_bf, et_bf = _const_maps()

    w_uv = _cols_from_shards(w_in4, 0, _UV_END)
    w_z = _cols_from_shards(w_in4, _UV_END, _Z_END)
    w_xbc = _cols_from_shards(w_in4, _Z_END, _XBC_END)
    w_dt = _pad_lanes(_cols_from_shards(w_in4, _XBC_END, IN_COLS), DT_PAD)

    nw_pre = small["norm_mix_pre"]
    lnw = small["gm_ln_w"].reshape(1, GM_WIDTH)
    lnb = small["gm_ln_b"].reshape(1, GM_WIDTH)
    w_stack = small["gm_w_s"].reshape(N_HEADS * CHUNK, CHUNK)
    w_cat = jnp.transpose(small["gm_w_s"], (1, 0, 2)).reshape(CHUNK, N_HEADS * CHUNK)
    bmap = jnp.repeat(small["gm_b_s"].T, HEAD_DIM, axis=1)
    cw3 = conv_w.reshape(CONV_K, 1, CONV_CH)
    conv_b = small["conv_b"]
    dt_bias = _pad_lanes(small["dt_bias"], DT_PAD)
    a_log = _pad_lanes(small["a_log"], DT_PAD)
    dskip_map = jnp.repeat(small["d_skip"], HEAD_DIM, axis=1)
    ssm_nw = small["ssm_norm_w"]

    half = down_shard.shape[0] // 2
    p_uv, p_xbc, p_z, p_dt, w_out4, w_down_a = _inproj_fwd(
        x2, nw_pre, w_uv, w_xbc, w_z, w_dt, carried=_allgather_exchange([out_shard, down_shard[:half]]))
    ssd_consts = (cw3, conv_b, dt_bias, a_log, dskip_map, ssm_nw, e_bf, et_bf)
    w_out_b = w_out4.reshape(D_MODEL, D_MODEL)
    mix, yssd, sprev, o, x1, h2, w_up4, w_down_b = _mixer_fwd(
        p_uv, p_xbc, p_z, p_dt, x2, lnw, lnb, w_cat, bmap, w_out_b, small["norm_mix_post"], small["norm_ffn_pre"],
        *ssd_consts, n_seq, carried=_allgather_exchange([up_shard, down_shard[half:]]))
    f, dd, dy, loss_acc, d_nffn_post = _mlp_fwd(h2, x1, tgt2, w_up4, w_down_a, w_down_b, small["norm_ffn_post"])

    dup, dx1, d_nffn_pre = _mlp_bwd(dd, f, x1, dy, w_down_a, w_down_b, w_up4, small["norm_ffn_pre"])
    tk = min(DW_TOKENS_PER_STEP, n_tok)
    g_up = _matmul_tn("dw_up", h2, dup, D_MODEL, D_MODEL, tk, stacked=True)
    g_down = _matmul_tn("dw_down", f, dd, 1024, D_MODEL, tk).reshape(N_CHIPS, D_FF // N_CHIPS, D_MODEL)
    do, dya, dyb, d_nmix_post, got_up, got_down = _outproj_bwd(
        dx1, o, w_out_b, small["norm_mix_post"], carried=_pair_exchange([g_up, g_down]))
    h_up = _pair_sum(core, g_up, got_up, 512)
    h_down = _pair_sum(core, g_down, got_down, 512)
    g_out = _matmul_tn("dw_out", mix, do, D_MODEL, D_MODEL, tk).reshape(N_CHIPS, D_MODEL // N_CHIPS, D_MODEL)
    dp_uv, d_ws, d_bs_t, d_lnw, d_lnb, slab_up, got_out = _gmlp_bwd(
        p_uv, dya, lnw, lnb, e_bf, et_bf, w_cat, w_stack, bmap,
        carried=_both(_chip_exchange([h_up]), _pair_exchange([g_out])))
    h_out = _pair_sum(core, g_out, got_out, 128)
    early = {
        "gm_ln_w": d_lnw.reshape(N_HEADS, HEAD_DIM), "gm_ln_b": d_lnb.reshape(N_HEADS, HEAD_DIM),
        "gm_w_s": d_ws.reshape(N_HEADS, CHUNK, CHUNK), "gm_b_s": d_bs_t[:, :N_HEADS].T,
        "norm_mix_post": d_nmix_post, "norm_ffn_pre": d_nffn_pre, "norm_ffn_post": d_nffn_post,
    }
    packed_early = _pack(early, tuple(early), tail=loss_acc[0, 0].reshape(1))
    (dp_xbc, dp_z, dp_dt, d_cw, d_cb, d_dtb, d_alog, d_dsk, d_ssm_nw, slab_down, slab_out, all_early) = _ssd_bwd(
        p_xbc, p_z, p_dt, yssd, sprev, dyb, *ssd_consts, n_seq,
        carried=_both(_chip_exchange([h_down, h_out]), _device_gather_exchange(packed_early)))
    gx, h, d_nmix_pre = _inproj_bwd(dp_uv, dp_xbc, dp_z, dp_dt, x2, dx1, w_uv, w_xbc, w_z, w_dt, nw_pre)
    late = {
        "norm_mix_pre": d_nmix_pre, "conv_w": d_cw.reshape(CONV_K, CONV_CH), "conv_b": d_cb,
        "dt_bias": d_dtb[:, :N_HEADS], "a_log": d_alog[:, :N_HEADS], "d_skip": d_dsk[:, :N_HEADS],
        "ssm_norm_w": d_ssm_nw,
    }
    g_uv, all_late = _matmul_tn("dw_in_uv", h, dp_uv, D_MODEL, 2 * GM_WIDTH, tk,
                                carried=_device_gather_exchange(_pack(late, tuple(late))))
    sum_early = _ordered_sum("small_sum_early", all_early)
    small_sum = _unpack(sum_early, {n: v.shape for n, v in early.items()}, tuple(early))
    small_sum.update(_unpack(_ordered_sum("small_sum_late", all_late), {n: v.shape for n, v in late.items()}, tuple(late)))
    loss = sum_early.reshape(-1)[sum(v.size for v in early.values())]
    red_up, red_down, red_out = _chip_sum(slab_up, 512), _chip_sum(slab_down, 512), _chip_sum(slab_out, 128)
    g_xbc, oth_up, oth_down, oth_out = _matmul_tn("dw_in_xbc", h, dp_xbc, D_MODEL, CONV_CH, tk,
                                                  carried=_pair_swap([red_up, red_down, red_out]))
    g_z = _matmul_tn("dw_in_z", h, dp_z, D_MODEL, SSM_WIDTH, tk)
    g_dt = _matmul_tn("dw_in_dt", h, dp_dt, D_MODEL, DT_PAD, tk)

    g_in = _shards_from_cols([(g_uv, 0, _UV_END), (g_z, _UV_END, _Z_END), (g_xbc, _Z_END, _XBC_END),
                              (g_dt, _XBC_END, IN_COLS)])
    red_in, oth_in = _reduce_scatter_last(g_in)
    res = _adamw_halves("adamw_mlp", [(adam_args["w_up"][0], red_up, oth_up) + adam_args["w_up"][1:],
                                      (adam_args["w_down"][0], red_down, oth_down) + adam_args["w_down"][1:]], 256)
    big_out = {"w_up": res[0:4], "w_down": res[4:8]}
    big_out["w_out"] = _adamw_halves("adamw_w_out", [(adam_args["w_out"][0], red_out, oth_out) + adam_args["w_out"][1:]], 128)
    big_out["w_in"] = _adamw_halves("adamw_w_in", [(adam_args["w_in"][0], red_in, oth_in) + adam_args["w_in"][1:]], 256)

    return loss, gx.reshape(x.shape), big_out, small_sum


_HBM = pl.BlockSpec(memory_space=pltpu.HBM)


D2D_CHUNKS = 8
ICI_CHUNKS = 1
ROW_ALIGN = 16


def _row_chunks(rows, n_chunks):
    size = min(max(rows // n_chunks, ROW_ALIGN), rows)
    assert rows % size == 0
    return [(start, size) for start in range(0, rows, size)]


def _position():
    x, y, c = lax.axis_index("x"), lax.axis_index("y"), lax.axis_index("c")
    chips = [(1 - x, y), (x, 1 - y), (1 - x, 1 - y)]
    return x, y, c, chips


def _allgather_exchange(arrs):
    n = len(arrs)

    def copies(ins, outs, send_sems, recv_sems, local_sems):
        x, y, c, chips = _position()
        me = 2 * x + y
        sibling = (x, y, 1 - c)

        def copy(a, k, src, dst, to):
            return pltpu.make_async_remote_copy(src_ref=src, dst_ref=dst, send_sem=send_sems.at[a, k],
                                                recv_sem=recv_sems.at[a, k], device_id=to, device_id_type=MESH)

        def half_rows(a, pc):
            half = ins[a].shape[0] // 2
            return pl.ds(pc * half, half)

        local = [pltpu.make_async_copy(ins[a], outs[a].at[me], local_sems.at[a]) for a in range(n)]
        ici_out = [[copy(a, k, ins[a].at[half_rows(a, c)], outs[a].at[me, half_rows(a, c)], (px, py, c))
                    for k, (px, py) in enumerate(chips)] for a in range(n)]
        return c, chips, sibling, copy, half_rows, local, ici_out

    def start(ins, outs, send_sems, recv_sems, local_sems):
        c, chips, _, copy, _, local, _ = copies(ins, outs, send_sems, recv_sems, local_sems)
        x, y, _, _ = _position()
        me = 2 * x + y
        for cp in local:
            cp.start()
        for a in range(n):
            half = ins[a].shape[0] // 2
            for k, (px, py) in enumerate(chips):
                for first, size in _row_chunks(half, ICI_CHUNKS):
                    rows = pl.ds(c * half + first, size)
                    copy(a, k, ins[a].at[rows], outs[a].at[me, rows], (px, py, c)).start()

    def finish(ins, outs, send_sems, recv_sems, local_sems):
        c, chips, sibling, copy, half_rows, local, ici_out = copies(ins, outs, send_sems, recv_sems, local_sems)
        for a in range(n):
            half = ins[a].shape[0] // 2
            for k, (px, py) in enumerate(chips):
                blk = outs[a].at[2 * px + py, half_rows(a, c)]
                copy(a, k, blk, blk, (px, py, c)).wait_recv()
                for first, size in _row_chunks(half, D2D_CHUNKS):
                    piece = outs[a].at[2 * px + py, pl.ds(c * half + first, size)]
                    copy(a, 3 + k, piece, piece, sibling).start()
        for a in range(n):
            for k, (px, py) in enumerate(chips):
                theirs = outs[a].at[2 * px + py, half_rows(a, 1 - c)]
                copy(a, 3 + k, theirs, theirs, sibling).wait_recv()
                mine = outs[a].at[2 * px + py, half_rows(a, c)]
                copy(a, 3 + k, mine, mine, sibling).wait_send()
        for a in range(n):
            for cp in ici_out[a]:
                cp.wait_send()
        for cp in local:
            cp.wait()

    return _Carried(arrs, [_sds((N_CHIPS,) + a.shape, a.dtype) for a in arrs],
                    [pltpu.SemaphoreType.DMA((n, 6)), pltpu.SemaphoreType.DMA((n, 6)), pltpu.SemaphoreType.DMA((n,))],
                    start, finish)


def _run_exchange(name, exchange):
    n_in, n_out = len(exchange.ins), len(exchange.out_shapes)

    def body(*refs):
        ins, outs, sems = refs[:n_in], refs[n_in:n_in + n_out], refs[n_in + n_out:]
        exchange.start(ins, outs, *sems)
        exchange.finish(ins, outs, *sems)

    return pl.pallas_call(
        body, name=name, out_shape=tuple(exchange.out_shapes), in_specs=[_HBM] * n_in,
        out_specs=tuple([_HBM] * n_out), scratch_shapes=exchange.sems,
    )(*exchange.ins)


def _pair_exchange(grads):
    n = len(grads)

    def copier(send_sems, recv_sems):
        x, y, c, _ = _position()

        def copy(a, src, dst):
            return pltpu.make_async_remote_copy(src_ref=src, dst_ref=dst, send_sem=send_sems.at[a],
                                                recv_sem=recv_sems.at[a], device_id=(x, y, 1 - c), device_id_type=MESH)
        return c, copy

    def start(ins, got, send_sems, recv_sems):
        c, copy = copier(send_sems, recv_sems)
        for a in range(n):
            half = ins[a].shape[1] // 2
            for slab in range(N_CHIPS):
                for first, size in _row_chunks(half, D2D_CHUNKS):
                    copy(a, ins[a].at[slab, pl.ds((1 - c) * half + first, size), :],
                         got[a].at[slab, pl.ds(first, size), :]).start()

    def finish(ins, got, send_sems, recv_sems):
        c, copy = copier(send_sems, recv_sems)
        for a in range(n):
            half = ins[a].shape[1] // 2
            copy(a, ins[a].at[:, pl.ds((1 - c) * half, half), :], got[a]).wait()

    return _Carried(grads, [_sds((N_CHIPS, g.shape[1] // 2, g.shape[2]), g.dtype) for g in grads],
                    [pltpu.SemaphoreType.DMA((n,)), pltpu.SemaphoreType.DMA((n,))], start, finish)


def _chip_exchange(hsums):
    n = len(hsums)

    def copies(ins, outs, send_sems, recv_sems, local_sems, pieces):
        x, y, c, chips = _position()
        me = 2 * x + y
        cps = []
        for a in range(n):
            cps.append(pltpu.make_async_copy(ins[a].at[me], outs[a].at[me], local_sems.at[a]))
            rows = ins[a].shape[1]
            for k, (px, py) in enumerate(chips):
                for first, size in (_row_chunks(rows, ICI_CHUNKS) if pieces else [(0, rows)]):
                    cps.append(pltpu.make_async_remote_copy(
                        src_ref=ins[a].at[2 * px + py, pl.ds(first, size)], dst_ref=outs[a].at[me, pl.ds(first, size)],
                        send_sem=send_sems.at[a, k], recv_sem=recv_sems.at[a, k], device_id=(px, py, c),
                        device_id_type=MESH))
        return cps

    def start(*refs):
        for cp in copies(*refs, pieces=True):
            cp.start()

    def finish(*refs):
        for cp in copies(*refs, pieces=False):
            cp.wait()

    return _Carried(hsums, [_sds(h.shape, h.dtype) for h in hsums],
                    [pltpu.SemaphoreType.DMA((n, 3)), pltpu.SemaphoreType.DMA((n, 3)), pltpu.SemaphoreType.DMA((n,))],
                    start, finish)


def _pair_swap(reds):
    n = len(reds)

    def copier(send_sems, recv_sems):
        x, y, c, _ = _position()

        def copy(a, src, dst):
            return pltpu.make_async_remote_copy(src_ref=src, dst_ref=dst, send_sem=send_sems.at[a],
                                                recv_sem=recv_sems.at[a], device_id=(x, y, 1 - c), device_id_type=MESH)
        return copy

    def start(ins, outs, send_sems, recv_sems):
        copy = copier(send_sems, recv_sems)
        for a in range(n):
            for first, size in _row_chunks(ins[a].shape[0], 2 * D2D_CHUNKS):
                copy(a, ins[a].at[pl.ds(first, size), :], outs[a].at[pl.ds(first, size), :]).start()

    def finish(ins, outs, send_sems, recv_sems):
        copy = copier(send_sems, recv_sems)
        for a in range(n):
            copy(a, ins[a], outs[a]).wait()

    return _Carried(reds, [_sds(r.shape, r.dtype) for r in reds],
                    [pltpu.SemaphoreType.DMA((n,)), pltpu.SemaphoreType.DMA((n,))], start, finish)


def _reduce_scatter_last(grad):
    _, rows, cols = grad.shape
    half = rows // 2
    pieces = _row_chunks(half, D2D_CHUNKS)

    def body(g_ref, mine_ref, theirs_ref, got_scr, hsum_scr, slab_scr, pair_sems, ici_send, ici_recv, swap_sems):
        x, y, c, chips = _position()
        me = 2 * x + y
        sibling = (x, y, 1 - c)

        def to_sibling(src, dst, sems):
            return pltpu.make_async_remote_copy(src_ref=src, dst_ref=dst, send_sem=sems.at[0], recv_sem=sems.at[1],
                                                device_id=sibling, device_id_type=MESH)

        for slab in range(N_CHIPS):
            for first, size in pieces:
                to_sibling(g_ref.at[slab, pl.ds((1 - c) * half + first, size)], got_scr.at[slab, pl.ds(first, size)],
                           pair_sems).start()
        to_sibling(g_ref.at[:, pl.ds((1 - c) * half, half)], got_scr, pair_sems).wait()
        own = g_ref[:, pl.ds(pl.multiple_of(c * half, half), half), :]
        hsum_scr[...] = (own.astype(F32) + got_scr[...].astype(F32)).astype(BF16)

        slab_scr[me] = hsum_scr[me]
        ici = [pltpu.make_async_remote_copy(src_ref=hsum_scr.at[2 * px + py], dst_ref=slab_scr.at[me],
                                            send_sem=ici_send.at[k], recv_sem=ici_recv.at[k], device_id=(px, py, c),
                                            device_id_type=MESH) for k, (px, py) in enumerate(chips)]
        for cp in ici:
            cp.start()
        for cp in ici:
            cp.wait()
        acc = slab_scr[0].astype(F32)
        for k in range(1, N_CHIPS):
            acc = acc + slab_scr[k].astype(F32)
        mine_ref[...] = acc

        for first, size in pieces:
            to_sibling(mine_ref.at[pl.ds(first, size)], theirs_ref.at[pl.ds(first, size)], swap_sems).start()
        to_sibling(mine_ref, theirs_ref, swap_sems).wait()

    vmem = pl.BlockSpec(memory_space=pltpu.VMEM)
    halves = (N_CHIPS, half, cols)
    return pl.pallas_call(
        body, name="grad_reduce_scatter_last", out_shape=(_sds((half, cols), F32), _sds((half, cols), F32)),
        in_specs=[vmem], out_specs=(vmem, vmem),
        scratch_shapes=[pltpu.VMEM(halves, BF16), pltpu.VMEM(halves, BF16), pltpu.VMEM(halves, BF16),
                        pltpu.SemaphoreType.DMA((2,)), pltpu.SemaphoreType.DMA((3,)), pltpu.SemaphoreType.DMA((3,)),
                        pltpu.SemaphoreType.DMA((2,))],
        compiler_params=pltpu.CompilerParams(vmem_limit_bytes=VMEM_LIMIT_BYTES),
    )(grad)


def _device_gather_exchange(packed):
    def copies(ins, outs, send_sems, recv_sems, local_sem):
        (x_ref,), (all_ref,) = ins, outs
        x, y, c, chips = _position()
        me, sibling = (x, y, c), (x, y, 1 - c)

        def slab(px, py, pc):
            return all_ref.at[4 * px + 2 * py + pc]

        def copy(k, block, to, src=None):
            return pltpu.make_async_remote_copy(
                src_ref=slab(*block) if src is None else src, dst_ref=slab(*block), send_sem=send_sems.at[k],
                recv_sem=recv_sems.at[k], device_id=to, device_id_type=MESH)

        mine = pltpu.make_async_copy(x_ref, slab(*me), local_sem)
        first = [copy(0, me, sibling, src=x_ref)]
        first += [copy(1 + j, me, (*chip, c), src=x_ref) for j, chip in enumerate(chips)]
        passed = [copy(4 + j, (*chip, c), sibling) for j, chip in enumerate(chips)]
        return c, chips, me, sibling, copy, mine, first, passed

    def start(ins, outs, send_sems, recv_sems, local_sem):
        _, _, _, _, _, mine, first, _ = copies(ins, outs, send_sems, recv_sems, local_sem)
        mine.start()
        for cp in first:
            cp.start()

    def finish(ins, outs, send_sems, recv_sems, local_sem):
        c, chips, me, sibling, copy, mine, first, passed = copies(ins, outs, send_sems, recv_sems, local_sem)
        for j, chip in enumerate(chips):
            copy(1 + j, (*chip, c), me).wait_recv()
            passed[j].start()
        copy(0, sibling, me).wait_recv()
        for j, chip in enumerate(chips):
            copy(4 + j, (*chip, 1 - c), me).wait_recv()
        for cp in first + passed:
            cp.wait_send()
        mine.wait()

    return _Carried([packed], [_sds((N_DEV,) + packed.shape, F32)],
                    [pltpu.SemaphoreType.DMA((7,)), pltpu.SemaphoreType.DMA((7,)), pltpu.SemaphoreType.DMA],
                    start, finish)


def _ordered_sum(name, slabs):
    _, m_per, n_cols = slabs.shape

    def body(s_ref, o_ref):
        acc = s_ref[0]
        for d in range(1, N_DEV):
            acc = acc + s_ref[d]
        o_ref[...] = acc

    vmem = pl.BlockSpec(memory_space=pltpu.VMEM)
    return pl.pallas_call(body, name=name, out_shape=_sds((m_per, n_cols), F32), in_specs=[vmem], out_specs=vmem)(slabs)


def _pair_sum(core, own, got, tm):
    _, half, cols = got.shape
    nb = half // tm

    def body(c_ref, a_ref, b_ref, o_ref):
        o_ref[...] = (a_ref[...].astype(F32) + b_ref[...].astype(F32)).astype(BF16)

    return pl.pallas_call(
        body, name="grad_pair_sum", out_shape=_sds(got.shape, BF16),
        grid_spec=pltpu.PrefetchScalarGridSpec(
            num_scalar_prefetch=1, grid=(N_CHIPS, nb),
            in_specs=[pl.BlockSpec((None, tm, cols), lambda s, i, c_ref: (s, c_ref[0] * nb + i, 0)),
                      pl.BlockSpec((None, tm, cols), lambda s, i, c_ref: (s, i, 0))],
            out_specs=pl.BlockSpec((None, tm, cols), lambda s, i, c_ref: (s, i, 0))),
        compiler_params=_cparams(2),
    )(core, own, got)


def _chip_sum(slabs, tm):
    _, half, cols = slabs.shape

    def body(s_ref, o_ref):
        acc = s_ref[0].astype(F32)
        for k in range(1, N_CHIPS):
            acc = acc + s_ref[k].astype(F32)
        o_ref[...] = acc

    return pl.pallas_call(
        body, name="grad_chip_sum", out_shape=_sds((half, cols), F32), grid=(half // tm,),
        in_specs=[pl.BlockSpec((N_CHIPS, tm, cols), lambda i: (0, i, 0))],
        out_specs=pl.BlockSpec((tm, cols), lambda i: (i, 0)), compiler_params=_cparams(1),
    )(slabs)


def _adam_math(w, g, m, v):
    m2 = ADAM_B1 * m + (1.0 - ADAM_B1) * g
    v2 = ADAM_B2 * v + (1.0 - ADAM_B2) * (g * g)
    m_hat = m2 / (1.0 - ADAM_B1 ** ADAM_STEP)
    v_hat = v2 / (1.0 - ADAM_B2 ** ADAM_STEP)
    delta = -ADAM_LR * (m_hat / (jnp.sqrt(v_hat) + ADAM_EPS) + ADAM_WD * w)
    return delta, m2, v2


def _adamw_halves(name, items, tm, carried=None):
    rows, cols = items[0][0].shape
    nb = rows // 2 // tm
    n = len(items)

    def body(*refs):
        mine = (pl.program_id(0) // nb) == lax.axis_index("c")
        for k in range(n):
            w_ref, own_ref, oth_ref, m_ref, v_ref = refs[5 * k:5 * k + 5]
            g_ref, d_ref, m2_ref, v2_ref = refs[5 * n + 4 * k:5 * n + 4 * k + 4]
            g = jnp.where(mine, own_ref[...], oth_ref[...])
            d, m2, v2 = _adam_math(w_ref[...], g, m_ref[...], v_ref[...])
            g_ref[...] = g
            d_ref[...] = d
            m2_ref[...] = m2
            v2_ref[...] = v2

    full = pl.BlockSpec((tm, cols), lambda i: (i, 0))
    half = pl.BlockSpec((tm, cols), lambda i: (i % nb, 0))
    return _call_carrying(
        body, carried, name=name, grid=(rows // tm,), in_specs=[full, half, half, full, full] * n,
        out_specs=[full] * (4 * n), out_shape=tuple([_sds((rows, cols), F32)] * (4 * n)), scratch_shapes=[],
        operands=[a for item in items for a in item])


def _adamw(name, w, g, m, v, tm):
    def body(w_ref, g_ref, m_ref, v_ref, gout_ref, d_ref, m2_ref, v2_ref):
        gv = g_ref[...]
        d, m2, v2 = _adam_math(w_ref[...], gv, m_ref[...], v_ref[...])
        gout_ref[...] = gv
        d_ref[...] = d
        m2_ref[...] = m2
        v2_ref[...] = v2

    return _rows_call(name, body, tm, [w, g, m, v], [], [_sds(w.shape, F32)] * 4)


_SMALL_NAMES = ("norm_mix_pre", "gm_ln_w", "gm_ln_b", "gm_w_s", "gm_b_s", "conv_w", "conv_b", "dt_bias", "a_log",
                "d_skip", "ssm_norm_w", "norm_mix_post", "norm_ffn_pre", "norm_ffn_post")
_PACK_COLS = 1024


def _pack(parts, names=_SMALL_NAMES, tail=None):
    pieces = [parts[n].reshape(-1) for n in names]
    flat = jnp.concatenate(pieces if tail is None else pieces + [tail])
    rows = -(-flat.shape[0] // (8 * _PACK_COLS)) * 8
    flat = jnp.pad(flat, (0, rows * _PACK_COLS - flat.shape[0]))
    return flat.reshape(rows, _PACK_COLS)


def _unpack(packed, shapes, names=_SMALL_NAMES):
    flat = packed.reshape(-1)
    out, off = {}, 0
    for n in names:
        size = 1
        for s in shapes[n]:
            size *= s
        out[n] = flat[off:off + size].reshape(shapes[n])
        off += size
    return out


def kernel(x, norm_mix_pre, w_in, gm_ln_w, gm_ln_b, gm_w_s, gm_b_s, conv_w, conv_b, dt_bias, a_log, d_skip, ssm_norm_w, w_out, norm_mix_post, norm_ffn_pre, w_up, w_down, norm_ffn_post, loss_target, m_norm_mix_pre, m_w_in, m_gm_ln_w, m_gm_ln_b, m_gm_w_s, m_gm_b_s, m_conv_w, m_conv_b, m_dt_bias, m_a_log, m_d_skip, m_ssm_norm_w, m_w_out, m_norm_mix_post, m_norm_ffn_pre, m_w_up, m_w_down, m_norm_ffn_post, v_norm_mix_pre, v_w_in, v_gm_ln_w, v_gm_ln_b, v_gm_w_s, v_gm_b_s, v_conv_w, v_conv_b, v_dt_bias, v_a_log, v_d_skip, v_ssm_norm_w, v_w_out, v_norm_mix_post, v_norm_ffn_pre, v_w_up, v_w_down, v_norm_ffn_post):
    params = dict(norm_mix_pre=norm_mix_pre, w_in=w_in, gm_ln_w=gm_ln_w, gm_ln_b=gm_ln_b, gm_w_s=gm_w_s, gm_b_s=gm_b_s,
                  conv_w=conv_w, conv_b=conv_b, dt_bias=dt_bias, a_log=a_log, d_skip=d_skip, ssm_norm_w=ssm_norm_w,
                  w_out=w_out, norm_mix_post=norm_mix_post, norm_ffn_pre=norm_ffn_pre, w_up=w_up, w_down=w_down,
                  norm_ffn_post=norm_ffn_post)
    mom1 = dict(norm_mix_pre=m_norm_mix_pre, w_in=m_w_in, gm_ln_w=m_gm_ln_w, gm_ln_b=m_gm_ln_b, gm_w_s=m_gm_w_s,
                gm_b_s=m_gm_b_s, conv_w=m_conv_w, conv_b=m_conv_b, dt_bias=m_dt_bias, a_log=m_a_log, d_skip=m_d_skip,
                ssm_norm_w=m_ssm_norm_w, w_out=m_w_out, norm_mix_post=m_norm_mix_post, norm_ffn_pre=m_norm_ffn_pre,
                w_up=m_w_up, w_down=m_w_down, norm_ffn_post=m_norm_ffn_post)
    mom2 = dict(norm_mix_pre=v_norm_mix_pre, w_in=v_w_in, gm_ln_w=v_gm_ln_w, gm_ln_b=v_gm_ln_b, gm_w_s=v_gm_w_s,
                gm_b_s=v_gm_b_s, conv_w=v_conv_w, conv_b=v_conv_b, dt_bias=v_dt_bias, a_log=v_a_log, d_skip=v_d_skip,
                ssm_norm_w=v_ssm_norm_w, w_out=v_w_out, norm_mix_post=v_norm_mix_post, norm_ffn_pre=v_norm_ffn_pre,
                w_up=v_w_up, w_down=v_w_down, norm_ffn_post=v_norm_ffn_post)
    names = list(params)
    big = ("w_in", "w_out", "w_up", "w_down")
    chip = 2 * lax.axis_index("x") + lax.axis_index("y")

    shards = {n: params[n][0].astype(BF16) for n in big}
    conv_shard = jnp.pad(conv_w[0], ((0, 16 - CONV_K), (0, 0)))
    g_in4, g_conv4 = _run_exchange("allgather_w_in", _allgather_exchange([shards["w_in"], conv_shard]))
    conv_full = jnp.transpose(g_conv4[:, :CONV_K, :], (1, 0, 2)).reshape(CONV_K, CONV_CH)

    small = {n: params[n][0] if params[n].ndim >= 3 else params[n] for n in _SMALL_NAMES if n != "conv_w"}
    core = lax.axis_index("c").astype(jnp.int32).reshape(1)
    adam_args = {n: (params[n][0], mom1[n][0], mom2[n][0]) for n in big}
    loss, grad_x, big_out, small_sum = _forward_backward(
        x, loss_target, g_in4, conv_full, small, shards["w_out"], shards["w_up"], shards["w_down"], core, adam_args)
    grads, delta, new_m, new_v = {}, {}, {}, {}
    for n in big:
        grads[n], delta[n], new_m[n], new_v[n] = [a[None] for a in big_out[n]]

    small_sum["conv_w"] = lax.dynamic_slice_in_dim(small_sum["conv_w"], chip * (CONV_CH // N_CHIPS), CONV_CH // N_CHIPS, axis=1)

    local_shapes = {n: params[n].shape[1:] if params[n].ndim >= 3 else params[n].shape for n in _SMALL_NAMES}
    flat = lambda tree: {n: tree[n].reshape(local_shapes[n]) for n in _SMALL_NAMES}
    packed = [_pack(flat(t)) for t in (params, small_sum, mom1, mom2)]
    _, d_p, m_p, v_p = _adamw("adamw_small", *packed, packed[0].shape[0])
    for src, dst in ((d_p, delta), (m_p, new_m), (v_p, new_v)):
        for n, val in _unpack(src, local_shapes).items():
            dst[n] = val.reshape(params[n].shape)
    for n in _SMALL_NAMES:
        grads[n] = small_sum[n].reshape(params[n].shape)

    out = [loss, grad_x]
    for tree in (grads, delta, new_m, new_v):
        out += [tree[n] for n in names]
    return tuple(out)
```

```python
import functools

import jax
import jax.numpy as jnp
from jax import lax
from jax.experimental import pallas as pl
from jax.experimental.pallas import tpu as pltpu

F32 = jnp.float32
BF16 = jnp.bfloat16
MESH = pl.DeviceIdType.MESH

EPS = 1e-6
D_MODEL = 1024
GM_WIDTH = 512
SSM_WIDTH = 512
N_HEADS = 8
HEAD_DIM = 64
CHUNK = 128
SSM_GROUPS = 2
GROUP_W = SSM_WIDTH // SSM_GROUPS
SSM_STATE = 128
CONV_K = 4
CONV_CH = 1024
D_FF = 4096
IN_COLS = 2568
DT_PAD = 128
SSD_COLS = CONV_CH + SSM_WIDTH + DT_PAD
N_CHIPS = 4
N_DEV = 8

ADAM_LR = 0.001
ADAM_B1 = 0.9
ADAM_B2 = 0.999
ADAM_EPS = 1e-08
ADAM_WD = 0.01
ADAM_STEP = 10

VMEM_LIMIT_BYTES = 56 * 1024 * 1024
FF_TILE = 512
DW_TOKENS_PER_STEP = 2048


def _cparams(n_axes):
    return pltpu.CompilerParams(dimension_semantics=("arbitrary",) * n_axes, vmem_limit_bytes=VMEM_LIMIT_BYTES)


def _dot(a, b):
    return jnp.dot(a.astype(BF16), b.astype(BF16), preferred_element_type=F32)


def _dot_nt(a, b):
    return lax.dot_general(a.astype(BF16), b.astype(BF16), (((1,), (1,)), ((), ())), preferred_element_type=F32)


def _dot_tn(a, b):
    return lax.dot_general(a.astype(BF16), b.astype(BF16), (((0,), (0,)), ((), ())), preferred_element_type=F32)


def _sigmoid(x):
    return 1.0 / (1.0 + jnp.exp(-x))


_GELU_C = 0.7978845608028654
_GELU_A = 0.044715


def _gelu(x):
    t = jnp.tanh(_GELU_C * (x + _GELU_A * (x * x * x)))
    return 0.5 * x * (1.0 + t), t


def _gelu_grad(x, t):
    return 0.5 * (1.0 + t) + 0.5 * x * (1.0 - t * t) * (_GELU_C * (1.0 + 3.0 * _GELU_A * x * x))


def _rms_fwd(x, w):
    r = lax.rsqrt(jnp.mean(x * x, axis=-1, keepdims=True) + EPS)
    return x * r * w, r


def _rms_bwd(x, r, w, dy):
    g = dy * w
    dx = r * g - x * (r * r * r) * jnp.mean(g * x, axis=-1, keepdims=True)
    dw = jnp.sum(dy * x * r, axis=0, keepdims=True)
    return dx, dw


class _Carried:
    def __init__(self, ins, out_shapes, sems, start, finish):
        self.ins, self.out_shapes, self.sems = list(ins), list(out_shapes), list(sems)
        self.start, self.finish = start, finish


def _both(first, second):
    n_i, n_o, n_s = len(first.ins), len(first.out_shapes), len(first.sems)

    def split(ins, outs, sems):
        return (ins[:n_i], outs[:n_o], sems[:n_s]), (ins[n_i:], outs[n_o:], sems[n_s:])

    def start(ins, outs, *sems):
        (i1, o1, s1), (i2, o2, s2) = split(ins, outs, sems)
        first.start(i1, o1, *s1)
        second.start(i2, o2, *s2)

    def finish(ins, outs, *sems):
        (i1, o1, s1), (i2, o2, s2) = split(ins, outs, sems)
        first.finish(i1, o1, *s1)
        second.finish(i2, o2, *s2)

    return _Carried(first.ins + second.ins, first.out_shapes + second.out_shapes, first.sems + second.sems, start, finish)


def _split_carried(refs, n_in, n_out, n_scratch, carried):
    n_ci, n_co, n_cs = len(carried.ins), len(carried.out_shapes), len(carried.sems)
    ins, rest = refs[:n_in], refs[n_in:]
    c_ins, rest = rest[:n_ci], rest[n_ci:]
    outs, rest = rest[:n_out], rest[n_out:]
    c_outs, rest = rest[:n_co], rest[n_co:]
    scr, c_sems = rest[:n_scratch], rest[n_scratch:]
    assert len(c_sems) == n_cs
    return tuple(ins) + tuple(outs) + tuple(scr), c_ins, c_outs, c_sems


def _rows_call(name, body, tm, row_ins, const_ins, row_outs, acc_outs=(), scratch=(), carried=None):
    n_rows = row_ins[0].shape[0]
    assert n_rows % tm == 0
    n_steps = n_rows // tm
    n_in = len(row_ins) + len(const_ins)
    n_ro = len(row_outs)
    n_acc = len(acc_outs)

    def kern(*refs):
        accs = refs[n_in + n_ro:n_in + n_ro + n_acc]

        @pl.when(pl.program_id(0) == 0)
        def _():
            for a in accs:
                a[...] = jnp.zeros_like(a)

        body(*refs)

    def whole(shape):
        nd = len(shape)
        return pl.BlockSpec(tuple(shape), lambda i: (0,) * nd)

    in_specs = [pl.BlockSpec((tm, a.shape[1]), lambda i: (i, 0)) for a in row_ins]
    in_specs += [whole(a.shape) for a in const_ins]
    out_specs = [pl.BlockSpec((tm, s.shape[1]), lambda i: (i, 0)) for s in row_outs]
    out_specs += [whole(s.shape) for s in acc_outs]
    return _call_carrying(
        kern, carried, name=name, grid=(n_steps,), in_specs=in_specs, out_specs=out_specs,
        out_shape=tuple(row_outs) + tuple(acc_outs), scratch_shapes=list(scratch), operands=list(row_ins) + list(const_ins))


def _call_carrying(body, carried, *, name, grid, in_specs, out_specs, out_shape, scratch_shapes, operands):
    n_in, n_out, n_scratch = len(in_specs), len(out_specs), len(scratch_shapes)
    kern = body
    if carried is not None:
        def kern(*refs):
            plain, c_ins, c_outs, c_sems = _split_carried(refs, n_in, n_out, n_scratch, carried)
            first, last = True, True
            for d, size in enumerate(grid):
                first = jnp.logical_and(first, pl.program_id(d) == 0)
                last = jnp.logical_and(last, pl.program_id(d) == size - 1)

            @pl.when(first)
            def _():
                carried.start(c_ins, c_outs, *c_sems)

            body(*plain)

            @pl.when(last)
            def _():
                carried.finish(c_ins, c_outs, *c_sems)

        in_specs = list(in_specs) + [_HBM] * len(carried.ins)
        out_specs = list(out_specs) + [_HBM] * len(carried.out_shapes)
        out_shape = tuple(out_shape) + tuple(carried.out_shapes)
        operands = list(operands) + carried.ins
        scratch_shapes = list(scratch_shapes) + carried.sems
    return pl.pallas_call(
        kern, name=name, grid=grid, in_specs=in_specs, out_specs=out_specs, out_shape=out_shape,
        scratch_shapes=scratch_shapes, compiler_params=_cparams(len(grid)),
    )(*operands)


def _sds(shape, dtype):
    return jax.ShapeDtypeStruct(tuple(shape), dtype)


def _matmul_tn(name, a, b, tm, tn, tk, stacked=False, carried=None):
    k_dim, m_dim = a.shape
    n_dim = b.shape[1]
    assert m_dim % tm == 0 and n_dim % tn == 0 and k_dim % tk == 0
    nk = k_dim // tk

    def kern(a_ref, b_ref, o_ref, acc_ref):
        k = pl.program_id(2)
        prod = _dot_tn(a_ref[...], b_ref[...])

        @pl.when(k == 0)
        def _():
            acc_ref[...] = prod

        @pl.when(k > 0)
        def _():
            acc_ref[...] += prod

        @pl.when(k == nk - 1)
        def _():
            o_ref[...] = acc_ref[...].astype(o_ref.dtype)

    if stacked:
        assert tm == m_dim
        out_shape = _sds((n_dim // tn, m_dim, tn), BF16)
        out_spec = pl.BlockSpec((None, tm, tn), lambda i, j, k: (j, i, 0))
    else:
        out_shape = _sds((m_dim, n_dim), BF16)
        out_spec = pl.BlockSpec((tm, tn), lambda i, j, k: (i, j))
    outs = _call_carrying(
        kern, carried, name=name, grid=(m_dim // tm, n_dim // tn, nk),
        in_specs=[pl.BlockSpec((tk, tm), lambda i, j, k: (k, i)), pl.BlockSpec((tk, tn), lambda i, j, k: (k, j))],
        out_specs=[out_spec], out_shape=(out_shape,), scratch_shapes=[pltpu.VMEM((tm, tn), F32)], operands=[a, b])
    return outs[0] if carried is None else outs


def _inproj_fwd(x, nw, w_uv, w_xbc, w_z, w_dt, tm=256, carried=None):
    n_tok = x.shape[0]

    def body(x_ref, nw_ref, wuv_ref, wxbc_ref, wz_ref, wdt_ref, puv_ref, pxbc_ref, pz_ref, pdt_ref):
        h, _ = _rms_fwd(x_ref[...], nw_ref[...])
        h = h.astype(BF16)
        puv_ref[...] = jnp.dot(h, wuv_ref[...], preferred_element_type=F32)
        pxbc_ref[...] = jnp.dot(h, wxbc_ref[...], preferred_element_type=F32)
        pz_ref[...] = jnp.dot(h, wz_ref[...], preferred_element_type=F32)
        pdt_ref[...] = jnp.dot(h, wdt_ref[...], preferred_element_type=F32)

    return _rows_call(
        "inproj_fwd", body, tm, [x], [nw, w_uv, w_xbc, w_z, w_dt],
        [_sds((n_tok, 2 * GM_WIDTH), F32), _sds((n_tok, CONV_CH), F32), _sds((n_tok, SSM_WIDTH), F32),
         _sds((n_tok, DT_PAD), F32)], carried=carried)


def _head_lane_mask(width, head):
    lane = lax.broadcasted_iota(jnp.int32, (1, width), 1)
    return (lane // HEAD_DIM) == head


def _split_terms(x, terms):
    parts = []
    for _ in range(terms):
        p = x.astype(BF16)
        parts.append(p)
        x = x - p.astype(F32)
    return parts


def _seg_dots(vals, ind, terms=2):
    m = vals[0].shape[0]
    parts = []
    for v in vals:
        parts += _split_terms(v, terms)
    red = jnp.dot(jnp.concatenate(parts, axis=0), ind, preferred_element_type=F32)
    outs = []
    for i in range(len(vals)):
        acc = red[i * terms * m:(i * terms + 1) * m]
        for t in range(1, terms):
            acc = acc + red[(i * terms + t) * m:(i * terms + t + 1) * m]
        outs.append(acc)
    return outs


def _tri_dot(mask, x, terms=3):
    n = x.shape[1]
    red = jnp.dot(mask.astype(BF16), jnp.concatenate(_split_terms(x, terms), axis=1), preferred_element_type=F32)
    acc = red[:, :n]
    for t in range(1, terms):
        acc = acc + red[:, t * n:(t + 1) * n]
    return acc


def _gmlp_common(puv, lnw, lnb, e_bf, et_bf):
    u = puv[:, :GM_WIDTH]
    v = puv[:, GM_WIDTH:]
    gu, tu = _gelu(u)
    gv, tv = _gelu(v)
    (s1,) = _seg_dots([gv], et_bf)
    (mu,) = _seg_dots([s1 * (1.0 / HEAD_DIM)], e_bf)
    xc = gv - mu
    (s2,) = _seg_dots([xc * xc], et_bf)
    (rstd,) = _seg_dots([lax.rsqrt(s2 * (1.0 / HEAD_DIM) + EPS)], e_bf)
    xhat = xc * rstd
    vn = xhat * lnw + lnb
    return u, v, gu, tu, tv, rstd, xhat, vn


def _tril_mask():
    r = lax.broadcasted_iota(jnp.int32, (CHUNK, CHUNK), 0)
    c = lax.broadcasted_iota(jnp.int32, (CHUNK, CHUNK), 1)
    return r >= c


def _head_blocks(v):
    return jnp.concatenate([jnp.where(_head_lane_mask(GM_WIDTH, h), v, jnp.zeros_like(v)) for h in range(N_HEADS)], axis=0)


def _causal_w_cat(w_cat):
    t = lax.broadcasted_iota(jnp.int32, (CHUNK, N_HEADS * CHUNK), 0)
    s = lax.broadcasted_iota(jnp.int32, (CHUNK, N_HEADS * CHUNK), 1) % CHUNK
    return jnp.where(t >= s, w_cat, 0.0).astype(BF16)


def _gmlp_chunk_fwd(puv, lnw, lnb, e_bf, et_bf, wm, bmap):
    _, _, gu, _, _, _, _, vn = _gmlp_common(puv, lnw, lnb, e_bf, et_bf)
    mixed = jnp.dot(wm, _head_blocks(vn.astype(BF16)), preferred_element_type=F32) + bmap
    return (gu * mixed).astype(BF16)


SUBLANES = 8


def _shift_down(x, tail, s):
    main = pltpu.roll(x, s, 0)
    row = lax.broadcasted_iota(jnp.int32, (SUBLANES, 1), 0)
    head = jnp.where(row < s, pltpu.roll(tail, s, 0), main[:SUBLANES])
    return jnp.concatenate([head, main[SUBLANES:]], axis=0)


def _shift_up(x, head_next, s):
    n = x.shape[0]
    main = pltpu.roll(x, n - s, 0)
    row = lax.broadcasted_iota(jnp.int32, (SUBLANES, 1), 0)
    last = jnp.where(row >= SUBLANES - s, pltpu.roll(head_next, SUBLANES - s, 0), main[n - SUBLANES:])
    return jnp.concatenate([main[:n - SUBLANES], last], axis=0)


def _ssd_pre(xr, tail, cw_ref, cb, pdt, dtb, alog, emap):
    rowi = lax.broadcasted_iota(jnp.int32, (CHUNK, 1), 0)
    shifted = [_shift_down(xr, tail, 3), _shift_down(xr, tail, 2), _shift_down(xr, tail, 1), xr]
    xc = cb
    for k in range(CONV_K):
        xc = xc + cw_ref[k] * shifted[k]
    sg = _sigmoid(xc)
    xa = xc * sg
    pre = pdt + dtb
    dt = jnp.maximum(pre, 0.0) + jnp.log(1.0 + jnp.exp(-jnp.abs(pre)))
    a_neg = -jnp.exp(alog)
    a_cs = _tri_dot(_tril_mask(), dt * a_neg)
    acs_map, dt_map = _seg_dots([a_cs, dt], emap, terms=3)
    return dict(shifted=shifted, xc=xc, sg=sg, xa=xa, pre=pre, dt=dt, a_neg=a_neg, a_cs=a_cs,
                acs_map=acs_map, dt_map=dt_map, rowi=rowi)


def _ssd_maps(p):
    last = p["rowi"] == CHUNK - 1
    aq_map = jnp.sum(jnp.where(last, p["acs_map"], 0.0), axis=0, keepdims=True)
    e_exp = jnp.exp(p["acs_map"])
    dte = jnp.exp(aq_map - p["acs_map"])
    cd = jnp.exp(aq_map)
    return last, e_exp, dte, cd


def _head_decay(a_cs, a_cs_t, head, tri):
    lane = lax.broadcasted_iota(jnp.int32, (1, DT_PAD), 1)
    sub = lax.broadcasted_iota(jnp.int32, (DT_PAD, 1), 0)
    col = jnp.sum(jnp.where(lane == head, a_cs, 0.0), axis=1, keepdims=True)
    row = jnp.sum(jnp.where(sub == head, a_cs_t, 0.0), axis=0, keepdims=True)
    return jnp.exp(jnp.where(tri, col - row, -1e30))


def _gate_fwd(y, z, nw):
    sz = _sigmoid(z)
    zg = z * sz
    yg = y * zg
    outs, rs = [], []
    for g in range(SSM_GROUPS):
        gs = slice(g * GROUP_W, (g + 1) * GROUP_W)
        o, r = _rms_fwd(yg[:, gs], nw[:, gs])
        outs.append(o)
        rs.append(r)
    return sz, zg, yg, outs, rs


def _ssd_const_specs():
    def whole(shape):
        nd = len(shape)
        return pl.BlockSpec(tuple(shape), lambda c: (0,) * nd)
    return [whole((CONV_K, 1, CONV_CH)), whole((1, CONV_CH)), whole((1, DT_PAD)), whole((1, DT_PAD)),
            whole((1, SSM_WIDTH)), whole((1, SSM_WIDTH)), whole((DT_PAD, SSM_WIDTH)), whole((SSM_WIDTH, DT_PAD))]


def _mixer_fwd(p_uv, p_xbc, p_z, p_dt, x, lnw, lnb, w_cat, bmap, w_out, nw_post, nw_pre2, conv_w, conv_b, dt_bias, a_log,
               dskip_map, norm_w, e_bf, et_bf, n_seq, carried=None):
    n_tok = p_xbc.shape[0]
    nc = n_tok // n_seq // CHUNK

    def body(puv3, xr3, z3, pdt3, x3, lnw_ref, lnb_ref, wcat_ref, bmap_ref, wo_ref, nwa_ref, nwb_ref,
             cw_ref, cb_ref, dtb_ref, alog_ref, dsk_ref, nw_ref, e_ref, et_ref,
             mix3, yssd3, sprev3, o3, x13, h23, wm_scr, prev3_scr, s3_scr):
        @pl.when(pl.program_id(0) == 0)
        def _():
            wm_scr[...] = _causal_w_cat(wcat_ref[...])
            prev3_scr[...] = jnp.zeros_like(prev3_scr)
            s3_scr[...] = jnp.zeros_like(s3_scr)

        for b in range(n_seq):
            one_sequence(puv3.at[b], xr3.at[b], z3.at[b], pdt3.at[b], lnw_ref, lnb_ref, bmap_ref,
                         cw_ref, cb_ref, dtb_ref, alog_ref, dsk_ref, nw_ref, e_ref, et_ref,
                         mix3.at[b], yssd3.at[b], sprev3.at[b], wm_scr, prev3_scr.at[b], s3_scr.at[b])
            o = jnp.dot(mix3[b], wo_ref[...], preferred_element_type=F32)
            on, _ = _rms_fwd(o, nwa_ref[...])
            x1 = x3[b] + on
            h2, _ = _rms_fwd(x1, nwb_ref[...])
            o3[b] = o
            x13[b] = x1
            h23[b] = h2.astype(BF16)

    def one_sequence(puv_ref, xr_ref, z_ref, pdt_ref, lnw_ref, lnb_ref, bmap_ref,
                     cw_ref, cb_ref, dtb_ref, alog_ref, dsk_ref, nw_ref, e_ref, et_ref,
                     mix_ref, yssd_ref, sprev_ref, wm_scr, prev_scr, s_scr):
        mix_ref[:, :GM_WIDTH] = _gmlp_chunk_fwd(puv_ref[...], lnw_ref[...], lnb_ref[...], e_ref[...], et_ref[...], wm_scr[...],
                                      bmap_ref[...])
        xr = xr_ref[...]
        p = _ssd_pre(xr, prev_scr[...], cw_ref, cb_ref[...], pdt_ref[...], dtb_ref[...], alog_ref[...], e_ref[...])
        _, e_exp, dte, cd = _ssd_maps(p)
        xs = p["xa"][:, :SSM_WIDTH]
        xd = xs * p["dt_map"]
        a_cs_t = p["a_cs"].T
        tri = _tril_mask()
        s_old = s_scr[...]
        sprev_ref[...] = s_old
        for g in range(SSM_GROUPS):
            gs = slice(g * GROUP_W, (g + 1) * GROUP_W)
            bm = p["xa"][:, SSM_WIDTH + g * SSM_STATE: SSM_WIDTH + (g + 1) * SSM_STATE].astype(BF16)
            cm = p["xa"][:, SSM_WIDTH + (SSM_GROUPS + g) * SSM_STATE: SSM_WIDTH + (SSM_GROUPS + g + 1) * SSM_STATE].astype(BF16)
            cb_mat = _dot_nt(cm, bm)
            xdg = xd[:, gs].astype(BF16)
            y_g = _dot(cm, s_old[:, gs]) * e_exp[:, gs] + dsk_ref[:, gs] * xs[:, gs]
            for r in range(SSM_GROUPS * 2):
                dm = _head_decay(p["a_cs"], a_cs_t, g * 4 + r, tri)
                full = jnp.dot((cb_mat * dm).astype(BF16), xdg, preferred_element_type=F32)
                y_g = y_g + jnp.where(_head_lane_mask(GROUP_W, r), full, 0.0)
            yssd_ref[:, gs] = y_g
            s_scr[:, gs] = cd[:, gs] * s_old[:, gs] + _dot_tn(bm, xd[:, gs] * dte[:, gs])
        _, _, _, outs, _ = _gate_fwd(yssd_ref[...], z_ref[...], nw_ref[...])
        for g in range(SSM_GROUPS):
            mix_ref[:, GM_WIDTH + g * GROUP_W:GM_WIDTH + (g + 1) * GROUP_W] = outs[g].astype(BF16)
        prev_scr[...] = xr[CHUNK - SUBLANES:, :]

    seq_len = n_tok // n_seq

    def rows(width):
        return pl.BlockSpec((n_seq, CHUNK, width), lambda c: (0, c, 0))

    def whole(shape):
        nd = len(shape)
        return pl.BlockSpec(tuple(shape), lambda c: (0,) * nd)

    def by_seq(a):
        return a.reshape(n_seq, seq_len, a.shape[-1])

    outs = _call_carrying(
        body, carried, name="mixer_fwd", grid=(nc,),
        in_specs=[rows(2 * GM_WIDTH), rows(CONV_CH), rows(SSM_WIDTH), rows(DT_PAD), rows(D_MODEL), whole(lnw.shape),
                  whole(lnb.shape), whole(w_cat.shape), whole(bmap.shape), whole(w_out.shape), whole(nw_post.shape),
                  whole(nw_pre2.shape)] + _ssd_const_specs(),
        out_specs=[rows(D_MODEL), rows(SSM_WIDTH), rows(SSM_WIDTH), rows(D_MODEL), rows(D_MODEL), rows(D_MODEL)],
        out_shape=(_sds((n_seq, seq_len, D_MODEL), BF16),
                   _sds((n_seq, seq_len, SSM_WIDTH), F32), _sds((n_seq, seq_len, SSM_WIDTH), F32),
                   _sds((n_seq, seq_len, D_MODEL), F32), _sds((n_seq, seq_len, D_MODEL), F32),
                   _sds((n_seq, seq_len, D_MODEL), BF16)),
        scratch_shapes=[pltpu.VMEM((CHUNK, N_HEADS * CHUNK), BF16), pltpu.VMEM((n_seq, SUBLANES, CONV_CH), F32),
                        pltpu.VMEM((n_seq, SSM_STATE, SSM_WIDTH), F32)],
        operands=[by_seq(p_uv), by_seq(p_xbc), by_seq(p_z), by_seq(p_dt), by_seq(x), lnw, lnb, w_cat, bmap, w_out, nw_post,
                  nw_pre2, conv_w, conv_b, dt_bias, a_log, dskip_map, norm_w, e_bf, et_bf])
    return tuple(o.reshape(n_tok, o.shape[-1]) for o in outs[:6]) + tuple(outs[6:])


def _up_cols(wup_ref, j):
    per = (D_FF // N_CHIPS) // FF_TILE
    return wup_ref[j // per, :, (j % per) * FF_TILE:(j % per + 1) * FF_TILE]


def _down_rows(wda_ref, wdb_ref, j):
    assert 2 * FF_TILE == D_FF // N_CHIPS
    return (wda_ref if j % 2 == 0 else wdb_ref)[j // 2]


def _skewed_rows_call(name, main, tail, tm, lead_ins, lag_ins, const_ins, lead_outs, lag_outs, acc_outs, carry,
                      streamed, tile_copies, n_copies):
    n_rows = lead_ins[0].shape[0]
    assert n_rows % tm == 0
    n = n_rows // tm
    counts = [len(lead_ins), len(lag_ins), len(const_ins), len(streamed), len(lead_outs), len(lag_outs), len(acc_outs),
              1, len(streamed)]

    def kern(*refs):
        groups, pos = [], 0
        for cnt in counts:
            groups.append(refs[pos:pos + cnt])
            pos += cnt
        lead_i, lag_i, consts, w_hbm, lead_o, lag_o, accs, (carry_scr,), w_vmem = groups
        sems = refs[pos]
        i = pl.program_id(0)
        pieces, k = [], 0
        for piece in tile_copies(w_hbm, w_vmem):
            pieces.append([pltpu.make_async_copy(src, dst, sems.at[k + q]) for q, (src, dst) in enumerate(piece)])
            k += len(piece)

        def ready(j):
            for cp in pieces[j]:
                cp.wait()

        @pl.when(i == 0)
        def _():
            for piece in pieces:
                for cp in piece:
                    cp.start()
            for a in accs:
                a[...] = jnp.zeros_like(a)
            carry_scr[...] = main(lead_i, consts, lead_o, w_vmem, ready)

        @pl.when(jnp.logical_and(i > 0, i < n))
        def _():
            previous = carry_scr[...]
            carry_scr[...] = main(lead_i, consts, lead_o, w_vmem, lambda j: None)
            tail(previous, lag_i, consts, lag_o, accs)

        @pl.when(i == n)
        def _():
            tail(carry_scr[...], lag_i, consts, lag_o, accs)

    def lead(width):
        return pl.BlockSpec((tm, width), lambda i: (jnp.minimum(i, n - 1), 0))

    def lag(width):
        return pl.BlockSpec((tm, width), lambda i: (jnp.maximum(i - 1, 0), 0))

    def whole(shape):
        nd = len(shape)
        return pl.BlockSpec(tuple(shape), lambda i: (0,) * nd)

    return pl.pallas_call(
        kern, name=name, grid=(n + 1,),
        in_specs=([lead(a.shape[1]) for a in lead_ins] + [lag(a.shape[1]) for a in lag_ins]
                  + [whole(a.shape) for a in const_ins] + [_HBM] * len(streamed)),
        out_specs=[lead(s.shape[1]) for s in lead_outs] + [lag(s.shape[1]) for s in lag_outs] + [whole(s.shape) for s in acc_outs],
        out_shape=tuple(lead_outs) + tuple(lag_outs) + tuple(acc_outs),
        scratch_shapes=([pltpu.VMEM(carry, F32)] + [pltpu.VMEM(a.shape, a.dtype) for a in streamed]
                        + [pltpu.SemaphoreType.DMA((n_copies,))]),
        compiler_params=_cparams(1),
    )(*lead_ins, *lag_ins, *const_ins, *streamed)


def _mlp_weight_pieces(order):
    per = (D_FF // N_CHIPS) // FF_TILE

    def tile_copies(hbm, vmem):
        pieces = []
        for j in range(D_FF // FF_TILE):
            cols = (j // per, slice(None), pl.ds((j % per) * FF_TILE, FF_TILE))
            up = (hbm[0].at[cols], vmem[0].at[cols])
            down = (hbm[1 + j % 2].at[j // 2], vmem[1 + j % 2].at[j // 2])
            pieces.append([up, down] if order == "up_down" else [down, up])
        return pieces

    return tile_copies


def _mlp_fwd(h2, x1, tgt, w_up, w_down_a, w_down_b, nw, tm=512):
    n_tok = x1.shape[0]

    def main(lead_i, consts, lead_o, weights, ready):
        (h2_ref,), (f_ref,), (wup_ref, wda_ref, wdb_ref) = lead_i, lead_o, weights
        h2v = h2_ref[...]
        acc = jnp.zeros((tm, D_MODEL), F32)
        for j in range(D_FF // FF_TILE):
            cs = slice(j * FF_TILE, (j + 1) * FF_TILE)
            ready(j)
            u = jnp.dot(h2v, _up_cols(wup_ref, j), preferred_element_type=F32)
            f = jnp.square(jnp.maximum(u, 0.0)).astype(BF16)
            f_ref[:, cs] = f
            acc = acc + jnp.dot(f, _down_rows(wda_ref, wdb_ref, j), preferred_element_type=F32)
        return acc

    def tail(acc, lag_i, consts, lag_o, accs):
        (x1_ref, tgt_ref), (nw_ref,), (dd_ref, dy_ref), (loss_ref, dnw_ref) = lag_i, consts, lag_o, accs
        dn, r = _rms_fwd(acc, nw_ref[...])
        e = x1_ref[...] + dn - tgt_ref[...]
        loss_ref[...] += jnp.full(loss_ref.shape, (0.5 / D_MODEL) * jnp.sum(e * e), F32)
        dy = e * (1.0 / D_MODEL)
        dd, dnw = _rms_bwd(acc, r, nw_ref[...], dy)
        dy_ref[...] = dy
        dd_ref[...] = dd.astype(BF16)
        dnw_ref[...] += dnw

    return _skewed_rows_call(
        "mlp_fwd", main, tail, tm, [h2], [x1, tgt], [nw],
        [_sds((n_tok, D_FF), BF16)], [_sds((n_tok, D_MODEL), BF16), _sds((n_tok, D_MODEL), F32)],
        [_sds((8, 128), F32), _sds((1, D_MODEL), F32)], carry=(tm, D_MODEL),
        streamed=[w_up, w_down_a, w_down_b], tile_copies=_mlp_weight_pieces("up_down"), n_copies=2 * (D_FF // FF_TILE))


def _mlp_bwd(dd, f, x1, dy, w_down_a, w_down_b, w_up, nw, tm=256):
    n_tok = x1.shape[0]

    def main(lead_i, consts, lead_o, weights, ready):
        (dd_ref, f_ref), (dup_ref,), (wup_ref, wda_ref, wdb_ref) = lead_i, lead_o, weights
        ddv = dd_ref[...]
        acc = jnp.zeros((tm, D_MODEL), F32)
        for j in range(D_FF // FF_TILE):
            cs = slice(j * FF_TILE, (j + 1) * FF_TILE)
            ready(j)
            df = _dot_nt(ddv, _down_rows(wda_ref, wdb_ref, j))
            du = (df * (2.0 * jnp.sqrt(f_ref[:, cs].astype(F32)))).astype(BF16)
            dup_ref[:, cs] = du
            acc = acc + _dot_nt(du, _up_cols(wup_ref, j))
        return acc

    def tail(acc, lag_i, consts, lag_o, accs):
        (x1_ref, dy_ref), (nw_ref,), (dx1_ref,), (dnw_ref,) = lag_i, consts, lag_o, accs
        x1v = x1_ref[...]
        _, r = _rms_fwd(x1v, nw_ref[...])
        dx, dnw = _rms_bwd(x1v, r, nw_ref[...], acc)
        dx1_ref[...] = dy_ref[...] + dx
        dnw_ref[...] += dnw

    return _skewed_rows_call(
        "mlp_bwd", main, tail, tm, [dd, f], [x1, dy], [nw],
        [_sds((n_tok, D_FF), BF16)], [_sds((n_tok, D_MODEL), F32)], [_sds((1, D_MODEL), F32)], carry=(tm, D_MODEL),
        streamed=[w_up, w_down_a, w_down_b], tile_copies=_mlp_weight_pieces("down_up"), n_copies=2 * (D_FF // FF_TILE))


def _outproj_bwd(dx1, o, w_out, nw, tm=256, carried=None):
    n_tok = dx1.shape[0]

    def body(dx1_ref, o_ref, wo_ref, nw_ref, do_ref, dya_ref, dyb_ref, dnw_ref):
        ov = o_ref[...]
        _, r = _rms_fwd(ov, nw_ref[...])
        do, dnw = _rms_bwd(ov, r, nw_ref[...], dx1_ref[...])
        dob = do.astype(BF16)
        do_ref[...] = dob
        dya_ref[...] = _dot_nt(dob, wo_ref[:GM_WIDTH, :])
        dyb_ref[...] = _dot_nt(dob, wo_ref[GM_WIDTH:, :])
        dnw_ref[...] += dnw

    return _rows_call("outproj_bwd", body, tm, [dx1, o], [w_out, nw],
                      [_sds((n_tok, D_MODEL), BF16), _sds((n_tok, GM_WIDTH), F32), _sds((n_tok, SSM_WIDTH), F32)],
                      [_sds((1, D_MODEL), F32)], carried=carried)


def _gmlp_bwd(p_uv, dya, lnw, lnb, e_bf, et_bf, w_cat, w_stack, bmap, carried=None):
    n_tok = p_uv.shape[0]
    chunks_per_step = 2

    def body(puv_ref, dya_ref, lnw_ref, lnb_ref, e_ref, et_ref, wcat_ref, wstack_ref, bmap_ref,
             dpuv_ref, dws_ref, dbs_ref, dlnw_ref, dlnb_ref, wm_scr, wsm_scr):
        t_stk = lax.broadcasted_iota(jnp.int32, (N_HEADS * CHUNK, CHUNK), 0) % CHUNK
        s_stk = lax.broadcasted_iota(jnp.int32, (N_HEADS * CHUNK, CHUNK), 1)

        @pl.when(pl.program_id(0) == 0)
        def _():
            wm_scr[...] = _causal_w_cat(wcat_ref[...])
            wsm_scr[...] = jnp.where(t_stk >= s_stk, wstack_ref[...], 0.0).astype(BF16)

        lnw_v = lnw_ref[...]
        e_v, et_v = e_ref[...], et_ref[...]

        def one_chunk(rows):
            u, v, gu, tu, tv, rstd, xhat, vn = _gmlp_common(puv_ref[rows, :], lnw_v, lnb_ref[...], e_v, et_v)
            vnb = vn.astype(BF16)
            mixed = jnp.dot(wm_scr[...], _head_blocks(vnb), preferred_element_type=F32) + bmap_ref[...]
            dy = dya_ref[rows, :]
            du = dy * mixed * _gelu_grad(u, tu)
            dmixed = dy * gu
            (dbs,) = _seg_dots([dmixed], et_v)
            dblocks = _head_blocks(dmixed.astype(BF16))
            dvn = lax.dot_general(wsm_scr[...], dblocks, (((0,), (0,)), ((), ())), preferred_element_type=F32)
            dws = lax.dot_general(dblocks, vnb, (((1,), (1,)), ((), ())), preferred_element_type=F32)
            dxh = dvn * lnw_v
            m1, m2 = _seg_dots([dxh, dxh * xhat], et_v)
            m1, m2 = _seg_dots([m1 * (1.0 / HEAD_DIM), m2 * (1.0 / HEAD_DIM)], e_v)
            dgv = rstd * (dxh - m1 - xhat * m2)
            dv = dgv * _gelu_grad(v, tv)
            dpuv_ref[rows, :GM_WIDTH] = du.astype(BF16)
            dpuv_ref[rows, GM_WIDTH:] = dv.astype(BF16)
            return dbs, dws, jnp.sum(dvn * xhat, axis=0, keepdims=True), jnp.sum(dvn, axis=0, keepdims=True)

        parts = [one_chunk(slice(k * CHUNK, (k + 1) * CHUNK)) for k in range(chunks_per_step)]
        dbs, dws, dlnw, dlnb = [functools.reduce(lambda a, b: a + b, vals) for vals in zip(*parts)]
        dbs_ref[...] += dbs
        dws_ref[...] += jnp.where(t_stk >= s_stk, dws, 0.0)
        dlnw_ref[...] += dlnw
        dlnb_ref[...] += dlnb

    return _rows_call(
        "gmlp_bwd", body, chunks_per_step * CHUNK, [p_uv, dya], [lnw, lnb, e_bf, et_bf, w_cat, w_stack, bmap],
        [_sds((n_tok, 2 * GM_WIDTH), BF16)],
        [_sds((N_HEADS * CHUNK, CHUNK), F32), _sds((CHUNK, DT_PAD), F32), _sds((1, GM_WIDTH), F32),
         _sds((1, GM_WIDTH), F32)],
        scratch=[pltpu.VMEM((CHUNK, N_HEADS * CHUNK), BF16), pltpu.VMEM((N_HEADS * CHUNK, CHUNK), BF16)],
        carried=carried)


def _ssd_bwd(p_xbc, p_z, p_dt, yssd, sprev, dyb, conv_w, conv_b, dt_bias, a_log, dskip_map, norm_w, e_bf, et_bf, n_seq,
             carried=None):
    n_tok = p_xbc.shape[0]
    nc = n_tok // n_seq // CHUNK

    def body(xr3, xprev3, z3, pdt3, yssd3, sprev3, dyb3,
             cw_ref, cb_ref, dtb_ref, alog_ref, dsk_ref, nw_ref, e_ref, et_ref,
             dps3, dcw_ref, dcb_ref, ddtb_ref, dalog_ref, ddsk_ref, dnw_ref,
             ds3_scr, nxt3_scr, dxa3_scr):
        @pl.when(pl.program_id(0) == 0)
        def _():
            for a in (dcw_ref, dcb_ref, ddtb_ref, dalog_ref, ddsk_ref, dnw_ref, ds3_scr, nxt3_scr):
                a[...] = jnp.zeros_like(a)

        for b in range(n_seq):
            one_sequence(xr3.at[b], xprev3.at[b], z3.at[b], pdt3.at[b], yssd3.at[b], sprev3.at[b], dyb3.at[b],
                         cw_ref, cb_ref, dtb_ref, alog_ref, dsk_ref, nw_ref, e_ref, et_ref,
                         dps3.at[b], dcw_ref, dcb_ref, ddtb_ref, dalog_ref, ddsk_ref, dnw_ref,
                         ds3_scr.at[b], nxt3_scr.at[b], dxa3_scr.at[b])

    def one_sequence(xr_ref, xprev_ref, z_ref, pdt_ref, yssd_ref, sprev_ref, dyb_ref,
                     cw_ref, cb_ref, dtb_ref, alog_ref, dsk_ref, nw_ref, e_ref, et_ref,
                     dps_ref, dcw_ref, dcb_ref, ddtb_ref, dalog_ref, ddsk_ref, dnw_ref,
                     ds_scr, nxt_scr, dxa_scr):
        chunk = nc - 1 - pl.program_id(0)
        xr = xr_ref[...]
        prev = jnp.where(chunk == 0, 0.0, xprev_ref[...])
        et_v = et_ref[...]
        p = _ssd_pre(xr, prev, cw_ref, cb_ref[...], pdt_ref[...], dtb_ref[...], alog_ref[...], e_ref[...])
        last, e_exp, dte, cd = _ssd_maps(p)
        rowi = p["rowi"]
        xs = p["xa"][:, :SSM_WIDTH]
        xd = xs * p["dt_map"]
        a_cs_t = p["a_cs"].T
        tri = _tril_mask()
        dsk = dsk_ref[...]
        nw_v = nw_ref[...]

        yv = yssd_ref[...]
        zv = z_ref[...]
        sz, zg, yg, _, rs = _gate_fwd(yv, zv, nw_v)
        dout = dyb_ref[...]
        for g in range(SSM_GROUPS):
            gs = slice(g * GROUP_W, (g + 1) * GROUP_W)
            dyg_g, dnw_g = _rms_bwd(yg[:, gs], rs[g], nw_v[:, gs], dout[:, gs])
            dnw_ref[:, gs] += dnw_g
            dxa_scr[:, gs] = dyg_g
        dyg = dxa_scr[:, :SSM_WIDTH]
        d_y = dyg * zg
        dps_ref[:, CONV_CH:CONV_CH + SSM_WIDTH] = (dyg * yv * (sz + zv * sz * (1.0 - sz))).astype(BF16)

        s_prev = sprev_ref[...]
        ds_next = ds_scr[...]
        lane_dt = lax.broadcasted_iota(jnp.int32, (1, DT_PAD), 1)
        da_cols = jnp.zeros((CHUNK, DT_PAD), F32)
        for g in range(SSM_GROUPS):
            gs = slice(g * GROUP_W, (g + 1) * GROUP_W)
            b_off = SSM_WIDTH + g * SSM_STATE
            c_off = SSM_WIDTH + (SSM_GROUPS + g) * SSM_STATE
            bm = p["xa"][:, b_off:b_off + SSM_STATE].astype(BF16)
            cm = p["xa"][:, c_off:c_off + SSM_STATE].astype(BF16)
            cb_mat = _dot_nt(cm, bm)
            d_yg = d_y[:, gs]
            d_ygb = d_yg.astype(BF16)
            xdg = xd[:, gs]
            xdgb = xdg.astype(BF16)
            ds_g = ds_next[:, gs]
            sp_g = s_prev[:, gs]
            bds = _dot(bm, ds_g)
            dcs = d_yg * e_exp[:, gs]
            d_c = _dot_nt(dcs, sp_g)
            ds_scr[:, gs] = cd[:, gs] * ds_g + _dot_tn(cm, dcs)
            d_b = _dot_nt(xdg * dte[:, gs], ds_g)
            dxd_g = bds * dte[:, gs]
            sum_dcb = jnp.zeros((CHUNK, CHUNK), F32)
            for r in range(SSM_GROUPS * 2):
                head = g * 4 + r
                mask = _head_lane_mask(GROUP_W, r)
                dm = _head_decay(p["a_cs"], a_cs_t, head, tri)
                m_mat = cb_mat * dm
                g_mat = _dot_nt(jnp.where(mask, d_yg, 0.0), xdgb)
                w_mat = g_mat * m_mat
                sum_dcb = sum_dcb + g_mat * dm
                dxd_g = dxd_g + jnp.where(mask, _dot_tn(m_mat, d_ygb), 0.0)
                da_h = jnp.sum(w_mat - w_mat.T, axis=1, keepdims=True)
                da_cols = da_cols + jnp.where(lane_dt == head, da_h, 0.0)
            d_c = d_c + _dot(sum_dcb, bm)
            d_b = d_b + _dot_tn(sum_dcb, cm)
            dxa_scr[:, b_off:b_off + SSM_STATE] = d_b
            dxa_scr[:, c_off:c_off + SSM_STATE] = d_c
            y_off_g = _dot(cm, sp_g) * e_exp[:, gs]
            t3 = bds * xdg * dte[:, gs]
            tail = jnp.sum(t3, axis=0, keepdims=True) + jnp.sum(ds_g * sp_g, axis=0, keepdims=True) * cd[:, gs]
            pre_g = d_yg * y_off_g - t3 + jnp.where(last, tail, 0.0)
            s_pre, ddt_g, s_dsk = _seg_dots([pre_g, dxd_g * xs[:, gs], d_yg * xs[:, gs]], et_v[gs, :])
            da_cols = da_cols + s_pre
            ddsk_ref[...] += jnp.sum(s_dsk, axis=0, keepdims=True)
            dxa_scr[:, gs] = dxd_g * p["dt_map"][:, gs] + dsk[:, gs] * d_yg
            if g == 0:
                ddt = ddt_g
            else:
                ddt = ddt + ddt_g
        r_i = lax.broadcasted_iota(jnp.int32, (CHUNK, CHUNK), 0)
        c_i = lax.broadcasted_iota(jnp.int32, (CHUNK, CHUNK), 1)
        ddta = _tri_dot(r_i <= c_i, da_cols, terms=2)
        ddt = ddt + ddta * p["a_neg"]
        dalog_ref[...] += jnp.sum(ddta * p["dt"], axis=0, keepdims=True) * p["a_neg"]
        draw = ddt * _sigmoid(p["pre"])
        ddtb_ref[...] += jnp.sum(draw, axis=0, keepdims=True)
        dps_ref[:, CONV_CH + SSM_WIDTH:] = draw.astype(BF16)

        xc = p["xc"]
        sg = p["sg"]
        dxc = dxa_scr[...] * (sg + xc * sg * (1.0 - sg))
        dcb_ref[...] += jnp.sum(dxc, axis=0, keepdims=True)
        for k in range(CONV_K):
            dcw_ref[k] += jnp.sum(dxc * p["shifted"][k], axis=0, keepdims=True)
        nxt = nxt_scr[...]
        dxr = cw_ref[3] * dxc
        for s in range(1, CONV_K):
            dxr = dxr + cw_ref[CONV_K - 1 - s] * _shift_up(dxc, nxt, s)
        dps_ref[:, :CONV_CH] = dxr.astype(BF16)
        nxt_scr[...] = dxc[:SUBLANES, :]

    seq_len = n_tok // n_seq

    def rows(width):
        return pl.BlockSpec((n_seq, CHUNK, width), lambda s: (0, nc - 1 - s, 0))

    tiles = CHUNK // SUBLANES
    prev_rows = pl.BlockSpec((n_seq, SUBLANES, CONV_CH), lambda s: (0, jnp.maximum((nc - 1 - s) * tiles - 1, 0), 0))

    def whole(shape):
        nd = len(shape)
        return pl.BlockSpec(tuple(shape), lambda s: (0,) * nd)

    def by_seq(a):
        return a.reshape(n_seq, seq_len, a.shape[-1])

    acc_shapes = [(CONV_K, 1, CONV_CH), (1, CONV_CH), (1, DT_PAD), (1, DT_PAD), (1, DT_PAD), (1, SSM_WIDTH)]
    xbc3 = by_seq(p_xbc)
    outs = _call_carrying(
        body, carried, name="ssd_bwd", grid=(nc,),
        in_specs=[rows(CONV_CH), prev_rows, rows(SSM_WIDTH), rows(DT_PAD), rows(SSM_WIDTH), rows(SSM_WIDTH),
                  rows(SSM_WIDTH)] + _ssd_const_specs(),
        out_specs=[rows(SSD_COLS)] + [whole(s) for s in acc_shapes],
        out_shape=tuple([_sds((n_seq, seq_len, SSD_COLS), BF16)] + [_sds(s, F32) for s in acc_shapes]),
        scratch_shapes=[pltpu.VMEM((n_seq, SSM_STATE, SSM_WIDTH), F32), pltpu.VMEM((n_seq, SUBLANES, CONV_CH), F32),
                        pltpu.VMEM((n_seq, CHUNK, CONV_CH), F32)],
        operands=[xbc3, xbc3, by_seq(p_z), by_seq(p_dt), by_seq(yssd), by_seq(sprev), by_seq(dyb), conv_w, conv_b, dt_bias,
                  a_log, dskip_map, norm_w, e_bf, et_bf])
    return (outs[0].reshape(n_tok, SSD_COLS),) + tuple(outs[1:])


def _inproj_bwd(dp_uv, dp_ssd, x, dx1, w_uv, w_ssd, nw, tm=512, carried=None):
    n_tok = x.shape[0]

    def body(duv_ref, dssd_ref, x_ref, dx1_ref, wuv_ref, wssd_ref, nw_ref, gx_ref, h_ref, dnw_ref):
        dh = _dot_nt(duv_ref[...], wuv_ref[...]) + _dot_nt(dssd_ref[...], wssd_ref[...])
        xv = x_ref[...]
        h, r = _rms_fwd(xv, nw_ref[...])
        dx, dnw = _rms_bwd(xv, r, nw_ref[...], dh)
        gx_ref[...] = dx1_ref[...] + dx
        h_ref[...] = h.astype(BF16)
        dnw_ref[...] += dnw

    return _rows_call("inproj_bwd", body, tm, [dp_uv, dp_ssd, x, dx1], [w_uv, w_ssd, nw],
                      [_sds((n_tok, D_MODEL), F32), _sds((n_tok, D_MODEL), BF16)], [_sds((1, D_MODEL), F32)],
                      carried=carried)


def _const_maps():
    lane = jnp.arange(SSM_WIDTH) // HEAD_DIM
    e_bf = (jnp.arange(DT_PAD)[:, None] == lane[None, :]).astype(BF16)
    return e_bf, e_bf.T


def _pad_lanes(v, width):
    return jnp.pad(v, ((0, 0), (0, width - v.shape[1])))


SHARD_COLS = IN_COLS // N_CHIPS
_UV_END = 2 * GM_WIDTH
_Z_END = _UV_END + SSM_WIDTH
_XBC_END = _Z_END + CONV_CH


def _cols_from_shards(w4, lo, hi):
    pieces = []
    for j in range(N_CHIPS):
        a, b = max(lo, j * SHARD_COLS), min(hi, (j + 1) * SHARD_COLS)
        if a < b:
            pieces.append(w4[j][:, a - j * SHARD_COLS:b - j * SHARD_COLS])
    return pieces[0] if len(pieces) == 1 else jnp.concatenate(pieces, axis=1)


def _shards_from_cols(blocks):
    shards = []
    for j in range(N_CHIPS):
        pieces = []
        for arr, lo, hi in blocks:
            a, b = max(lo, j * SHARD_COLS), min(hi, (j + 1) * SHARD_COLS)
            if a < b:
                pieces.append(arr[:, a - lo:b - lo])
        shards.append(pieces[0] if len(pieces) == 1 else jnp.concatenate(pieces, axis=1))
    return jnp.stack(shards)


def _forward_backward(x, tgt, w_in4, conv_w, small, out_shard, up_shard, down_shard, core, adam_args):
    n_seq, seq_len, _ = x.shape
    n_tok = n_seq * seq_len
    x2 = x.reshape(n_tok, D_MODEL)
    tgt2 = tgt.reshape(n_tok, D_MODEL)
    e_bf, et_bf = _const_maps()

    w_uv = _cols_from_shards(w_in4, 0, _UV_END)
    w_z = _cols_from_shards(w_in4, _UV_END, _Z_END)
    w_xbc = _cols_from_shards(w_in4, _Z_END, _XBC_END)
    w_dt = _pad_lanes(_cols_from_shards(w_in4, _XBC_END, IN_COLS), DT_PAD)

    nw_pre = small["norm_mix_pre"]
    lnw = small["gm_ln_w"].reshape(1, GM_WIDTH)
    lnb = small["gm_ln_b"].reshape(1, GM_WIDTH)
    w_stack = small["gm_w_s"].reshape(N_HEADS * CHUNK, CHUNK)
    w_cat = jnp.transpose(small["gm_w_s"], (1, 0, 2)).reshape(CHUNK, N_HEADS * CHUNK)
    bmap = jnp.repeat(small["gm_b_s"].T, HEAD_DIM, axis=1)
    cw3 = conv_w.reshape(CONV_K, 1, CONV_CH)
    conv_b = small["conv_b"]
    dt_bias = _pad_lanes(small["dt_bias"], DT_PAD)
    a_log = _pad_lanes(small["a_log"], DT_PAD)
    dskip_map = jnp.repeat(small["d_skip"], HEAD_DIM, axis=1)
    ssm_nw = small["ssm_norm_w"]

    half = down_shard.shape[0] // 2
    p_uv, p_xbc, p_z, p_dt, w_out4, w_down_a = _inproj_fwd(
        x2, nw_pre, w_uv, w_xbc, w_z, w_dt, carried=_allgather_exchange([out_shard, down_shard[:half]]))
    ssd_consts = (cw3, conv_b, dt_bias, a_log, dskip_map, ssm_nw, e_bf, et_bf)
    w_out_b = w_out4.reshape(D_MODEL, D_MODEL)
    mix, yssd, sprev, o, x1, h2, w_up4, w_down_b = _mixer_fwd(
        p_uv, p_xbc, p_z, p_dt, x2, lnw, lnb, w_cat, bmap, w_out_b, small["norm_mix_post"], small["norm_ffn_pre"],
        *ssd_consts, n_seq, carried=_allgather_exchange([up_shard, down_shard[half:]]))
    f, dd, dy, loss_acc, d_nffn_post = _mlp_fwd(h2, x1, tgt2, w_up4, w_down_a, w_down_b, small["norm_ffn_post"])

    dup, dx1, d_nffn_pre = _mlp_bwd(dd, f, x1, dy, w_down_a, w_down_b, w_up4, small["norm_ffn_pre"])
    tk = min(DW_TOKENS_PER_STEP, n_tok)
    g_up = _matmul_tn("dw_up", h2, dup, D_MODEL, D_MODEL, tk, stacked=True)
    g_down = _matmul_tn("dw_down", f, dd, 1024, D_MODEL, tk).reshape(N_CHIPS, D_FF // N_CHIPS, D_MODEL)
    do, dya, dyb, d_nmix_post, got_up, got_down = _outproj_bwd(
        dx1, o, w_out_b, small["norm_mix_post"], carried=_pair_exchange([g_up, g_down]))
    h_up = _pair_sum(core, g_up, got_up, 512)
    h_down = _pair_sum(core, g_down, got_down, 512)
    g_out = _matmul_tn("dw_out", mix, do, D_MODEL, D_MODEL, tk).reshape(N_CHIPS, D_MODEL // N_CHIPS, D_MODEL)
    dp_uv, d_ws, d_bs_t, d_lnw, d_lnb, slab_up, got_out = _gmlp_bwd(
        p_uv, dya, lnw, lnb, e_bf, et_bf, w_cat, w_stack, bmap,
        carried=_both(_chip_exchange([h_up]), _pair_exchange([g_out])))
    h_out = _pair_sum(core, g_out, got_out, 128)
    early = {
        "gm_ln_w": d_lnw.reshape(N_HEADS, HEAD_DIM), "gm_ln_b": d_lnb.reshape(N_HEADS, HEAD_DIM),
        "gm_w_s": d_ws.reshape(N_HEADS, CHUNK, CHUNK), "gm_b_s": d_bs_t[:, :N_HEADS].T,
        "norm_mix_post": d_nmix_post, "norm_ffn_pre": d_nffn_pre, "norm_ffn_post": d_nffn_post,
    }
    packed_early = _pack(early, tuple(early), tail=loss_acc[0, 0].reshape(1))
    (dp_ssd, d_cw, d_cb, d_dtb, d_alog, d_dsk, d_ssm_nw, slab_down, slab_out, all_early) = _ssd_bwd(
        p_xbc, p_z, p_dt, yssd, sprev, dyb, *ssd_consts, n_seq,
        carried=_both(_chip_exchange([h_down, h_out]), _device_gather_exchange(packed_early)))
    w_ssd = jnp.concatenate([w_xbc, w_z, w_dt], axis=1)
    gx, h, d_nmix_pre = _inproj_bwd(dp_uv, dp_ssd, x2, dx1, w_uv, w_ssd, nw_pre)
    late = {
        "norm_mix_pre": d_nmix_pre, "conv_w": d_cw.reshape(CONV_K, CONV_CH), "conv_b": d_cb,
        "dt_bias": d_dtb[:, :N_HEADS], "a_log": d_alog[:, :N_HEADS], "d_skip": d_dsk[:, :N_HEADS],
        "ssm_norm_w": d_ssm_nw,
    }
    g_uv, all_late = _matmul_tn("dw_in_uv", h, dp_uv, D_MODEL, 2 * GM_WIDTH, tk,
                                carried=_device_gather_exchange(_pack(late, tuple(late))))
    sum_early = _ordered_sum("small_sum_early", all_early)
    small_sum = _unpack(sum_early, {n: v.shape for n, v in early.items()}, tuple(early))
    small_sum.update(_unpack(_ordered_sum("small_sum_late", all_late), {n: v.shape for n, v in late.items()}, tuple(late)))
    loss = sum_early.reshape(-1)[sum(v.size for v in early.values())]
    red_up, red_down, red_out = _chip_sum(slab_up, 512), _chip_sum(slab_down, 512), _chip_sum(slab_out, 128)
    g_ssd, oth_up, oth_down, oth_out = _matmul_tn("dw_in_ssd", h, dp_ssd, D_MODEL, SSD_COLS, tk,
                                                  carried=_pair_swap([red_up, red_down, red_out]))
    g_xbc, g_z, g_dt = g_ssd[:, :CONV_CH], g_ssd[:, CONV_CH:CONV_CH + SSM_WIDTH], g_ssd[:, CONV_CH + SSM_WIDTH:]
    g_in = _shards_from_cols([(g_uv, 0, _UV_END), (g_z, _UV_END, _Z_END), (g_xbc, _Z_END, _XBC_END),
                              (g_dt, _XBC_END, IN_COLS)])
    red_in, oth_in = _reduce_scatter_last(g_in)
    res = _adamw_halves("adamw_mlp", [(adam_args["w_up"][0], red_up, oth_up) + adam_args["w_up"][1:],
                                      (adam_args["w_down"][0], red_down, oth_down) + adam_args["w_down"][1:]], 256)
    big_out = {"w_up": res[0:4], "w_down": res[4:8]}
    big_out["w_out"] = _adamw_halves("adamw_w_out", [(adam_args["w_out"][0], red_out, oth_out) + adam_args["w_out"][1:]], 128)
    big_out["w_in"] = _adamw_halves("adamw_w_in", [(adam_args["w_in"][0], red_in, oth_in) + adam_args["w_in"][1:]], 256)

    return loss, gx.reshape(x.shape), big_out, small_sum


_HBM = pl.BlockSpec(memory_space=pltpu.HBM)


D2D_CHUNKS = 8
ICI_CHUNKS = 1
ROW_ALIGN = 16


def _row_chunks(rows, n_chunks):
    size = min(max(rows // n_chunks, ROW_ALIGN), rows)
    assert rows % size == 0
    return [(start, size) for start in range(0, rows, size)]


def _position():
    x, y, c = lax.axis_index("x"), lax.axis_index("y"), lax.axis_index("c")
    chips = [(1 - x, y), (x, 1 - y), (1 - x, 1 - y)]
    return x, y, c, chips


def _allgather_exchange(arrs):
    n = len(arrs)

    def copies(ins, outs, send_sems, recv_sems, local_sems):
        x, y, c, chips = _position()
        me = 2 * x + y
        sibling = (x, y, 1 - c)

        def copy(a, k, src, dst, to):
            return pltpu.make_async_remote_copy(src_ref=src, dst_ref=dst, send_sem=send_sems.at[a, k],
                                                recv_sem=recv_sems.at[a, k], device_id=to, device_id_type=MESH)

        def half_rows(a, pc):
            half = ins[a].shape[0] // 2
            return pl.ds(pc * half, half)

        local = [pltpu.make_async_copy(ins[a], outs[a].at[me], local_sems.at[a]) for a in range(n)]
        ici_out = [[copy(a, k, ins[a].at[half_rows(a, c)], outs[a].at[me, half_rows(a, c)], (px, py, c))
                    for k, (px, py) in enumerate(chips)] for a in range(n)]
        return c, chips, sibling, copy, half_rows, local, ici_out

    def start(ins, outs, send_sems, recv_sems, local_sems):
        c, chips, _, copy, _, local, _ = copies(ins, outs, send_sems, recv_sems, local_sems)
        x, y, _, _ = _position()
        me = 2 * x + y
        for cp in local:
            cp.start()
        for a in range(n):
            half = ins[a].shape[0] // 2
            for k, (px, py) in enumerate(chips):
                for first, size in _row_chunks(half, ICI_CHUNKS):
                    rows = pl.ds(c * half + first, size)
                    copy(a, k, ins[a].at[rows], outs[a].at[me, rows], (px, py, c)).start()

    def finish(ins, outs, send_sems, recv_sems, local_sems):
        c, chips, sibling, copy, half_rows, local, ici_out = copies(ins, outs, send_sems, recv_sems, local_sems)
        for a in range(n):
            half = ins[a].shape[0] // 2
            for k, (px, py) in enumerate(chips):
                blk = outs[a].at[2 * px + py, half_rows(a, c)]
                copy(a, k, blk, blk, (px, py, c)).wait_recv()
                for first, size in _row_chunks(half, D2D_CHUNKS):
                    piece = outs[a].at[2 * px + py, pl.ds(c * half + first, size)]
                    copy(a, 3 + k, piece, piece, sibling).start()
        for a in range(n):
            for k, (px, py) in enumerate(chips):
                theirs = outs[a].at[2 * px + py, half_rows(a, 1 - c)]
                copy(a, 3 + k, theirs, theirs, sibling).wait_recv()
                mine = outs[a].at[2 * px + py, half_rows(a, c)]
                copy(a, 3 + k, mine, mine, sibling).wait_send()
        for a in range(n):
            for cp in ici_out[a]:
                cp.wait_send()
        for cp in local:
            cp.wait()

    return _Carried(arrs, [_sds((N_CHIPS,) + a.shape, a.dtype) for a in arrs],
                    [pltpu.SemaphoreType.DMA((n, 6)), pltpu.SemaphoreType.DMA((n, 6)), pltpu.SemaphoreType.DMA((n,))],
                    start, finish)


def _run_exchange(name, exchange):
    n_in, n_out = len(exchange.ins), len(exchange.out_shapes)

    def body(*refs):
        ins, outs, sems = refs[:n_in], refs[n_in:n_in + n_out], refs[n_in + n_out:]
        exchange.start(ins, outs, *sems)
        exchange.finish(ins, outs, *sems)

    return pl.pallas_call(
        body, name=name, out_shape=tuple(exchange.out_shapes), in_specs=[_HBM] * n_in,
        out_specs=tuple([_HBM] * n_out), scratch_shapes=exchange.sems,
    )(*exchange.ins)


def _pair_exchange(grads):
    n = len(grads)

    def copier(send_sems, recv_sems):
        x, y, c, _ = _position()

        def copy(a, src, dst):
            return pltpu.make_async_remote_copy(src_ref=src, dst_ref=dst, send_sem=send_sems.at[a],
                                                recv_sem=recv_sems.at[a], device_id=(x, y, 1 - c), device_id_type=MESH)
        return c, copy

    def start(ins, got, send_sems, recv_sems):
        c, copy = copier(send_sems, recv_sems)
        for a in range(n):
            half = ins[a].shape[1] // 2
            for slab in range(N_CHIPS):
                for first, size in _row_chunks(half, D2D_CHUNKS):
                    copy(a, ins[a].at[slab, pl.ds((1 - c) * half + first, size), :],
                         got[a].at[slab, pl.ds(first, size), :]).start()

    def finish(ins, got, send_sems, recv_sems):
        c, copy = copier(send_sems, recv_sems)
        for a in range(n):
            half = ins[a].shape[1] // 2
            copy(a, ins[a].at[:, pl.ds((1 - c) * half, half), :], got[a]).wait()

    return _Carried(grads, [_sds((N_CHIPS, g.shape[1] // 2, g.shape[2]), g.dtype) for g in grads],
                    [pltpu.SemaphoreType.DMA((n,)), pltpu.SemaphoreType.DMA((n,))], start, finish)


def _chip_exchange(hsums):
    n = len(hsums)

    def copies(ins, outs, send_sems, recv_sems, local_sems, pieces):
        x, y, c, chips = _position()
        me = 2 * x + y
        cps = []
        for a in range(n):
            cps.append(pltpu.make_async_copy(ins[a].at[me], outs[a].at[me], local_sems.at[a]))
            rows = ins[a].shape[1]
            for k, (px, py) in enumerate(chips):
                for first, size in (_row_chunks(rows, ICI_CHUNKS) if pieces else [(0, rows)]):
                    cps.append(pltpu.make_async_remote_copy(
                        src_ref=ins[a].at[2 * px + py, pl.ds(first, size)], dst_ref=outs[a].at[me, pl.ds(first, size)],
                        send_sem=send_sems.at[a, k], recv_sem=recv_sems.at[a, k], device_id=(px, py, c),
                        device_id_type=MESH))
        return cps

    def start(*refs):
        for cp in copies(*refs, pieces=True):
            cp.start()

    def finish(*refs):
        for cp in copies(*refs, pieces=False):
            cp.wait()

    return _Carried(hsums, [_sds(h.shape, h.dtype) for h in hsums],
                    [pltpu.SemaphoreType.DMA((n, 3)), pltpu.SemaphoreType.DMA((n, 3)), pltpu.SemaphoreType.DMA((n,))],
                    start, finish)


def _pair_swap(reds):
    n = len(reds)

    def copier(send_sems, recv_sems):
        x, y, c, _ = _position()

        def copy(a, src, dst):
            return pltpu.make_async_remote_copy(src_ref=src, dst_ref=dst, send_sem=send_sems.at[a],
                                                recv_sem=recv_sems.at[a], device_id=(x, y, 1 - c), device_id_type=MESH)
        return copy

    def start(ins, outs, send_sems, recv_sems):
        copy = copier(send_sems, recv_sems)
        for a in range(n):
            for first, size in _row_chunks(ins[a].shape[0], 2 * D2D_CHUNKS):
                copy(a, ins[a].at[pl.ds(first, size), :], outs[a].at[pl.ds(first, size), :]).start()

    def finish(ins, outs, send_sems, recv_sems):
        copy = copier(send_sems, recv_sems)
        for a in range(n):
            copy(a, ins[a], outs[a]).wait()

    return _Carried(reds, [_sds(r.shape, r.dtype) for r in reds],
                    [pltpu.SemaphoreType.DMA((n,)), pltpu.SemaphoreType.DMA((n,))], start, finish)


def _reduce_scatter_last(grad):
    _, rows, cols = grad.shape
    half = rows // 2
    pieces = _row_chunks(half, D2D_CHUNKS)

    def body(g_ref, mine_ref, theirs_ref, got_scr, hsum_scr, slab_scr, pair_sems, ici_send, ici_recv, swap_sems):
        x, y, c, chips = _position()
        me = 2 * x + y
        sibling = (x, y, 1 - c)

        def to_sibling(src, dst, sems):
            return pltpu.make_async_remote_copy(src_ref=src, dst_ref=dst, send_sem=sems.at[0], recv_sem=sems.at[1],
                                                device_id=sibling, device_id_type=MESH)

        for slab in range(N_CHIPS):
            for first, size in pieces:
                to_sibling(g_ref.at[slab, pl.ds((1 - c) * half + first, size)], got_scr.at[slab, pl.ds(first, size)],
                           pair_sems).start()
        to_sibling(g_ref.at[:, pl.ds((1 - c) * half, half)], got_scr, pair_sems).wait()
        own = g_ref[:, pl.ds(pl.multiple_of(c * half, half), half), :]
        hsum_scr[...] = (own.astype(F32) + got_scr[...].astype(F32)).astype(BF16)

        slab_scr[me] = hsum_scr[me]
        ici = [pltpu.make_async_remote_copy(src_ref=hsum_scr.at[2 * px + py], dst_ref=slab_scr.at[me],
                                            send_sem=ici_send.at[k], recv_sem=ici_recv.at[k], device_id=(px, py, c),
                                            device_id_type=MESH) for k, (px, py) in enumerate(chips)]
        for cp in ici:
            cp.start()
        for cp in ici:
            cp.wait()
        acc = slab_scr[0].astype(F32)
        for k in range(1, N_CHIPS):
            acc = acc + slab_scr[k].astype(F32)
        mine_ref[...] = acc

        for first, size in pieces:
            to_sibling(mine_ref.at[pl.ds(first, size)], theirs_ref.at[pl.ds(first, size)], swap_sems).start()
        to_sibling(mine_ref, theirs_ref, swap_sems).wait()

    vmem = pl.BlockSpec(memory_space=pltpu.VMEM)
    halves = (N_CHIPS, half, cols)
    return pl.pallas_call(
        body, name="grad_reduce_scatter_last", out_shape=(_sds((half, cols), F32), _sds((half, cols), F32)),
        in_specs=[vmem], out_specs=(vmem, vmem),
        scratch_shapes=[pltpu.VMEM(halves, BF16), pltpu.VMEM(halves, BF16), pltpu.VMEM(halves, BF16),
                        pltpu.SemaphoreType.DMA((2,)), pltpu.SemaphoreType.DMA((3,)), pltpu.SemaphoreType.DMA((3,)),
                        pltpu.SemaphoreType.DMA((2,))],
        compiler_params=pltpu.CompilerParams(vmem_limit_bytes=VMEM_LIMIT_BYTES),
    )(grad)


def _device_gather_exchange(packed):
    def copies(ins, outs, send_sems, recv_sems, local_sem):
        (x_ref,), (all_ref,) = ins, outs
        x, y, c, chips = _position()
        me, sibling = (x, y, c), (x, y, 1 - c)

        def slab(px, py, pc):
            return all_ref.at[4 * px + 2 * py + pc]

        def copy(k, block, to, src=None):
            return pltpu.make_async_remote_copy(
                src_ref=slab(*block) if src is None else src, dst_ref=slab(*block), send_sem=send_sems.at[k],
                recv_sem=recv_sems.at[k], device_id=to, device_id_type=MESH)

        mine = pltpu.make_async_copy(x_ref, slab(*me), local_sem)
        first = [copy(0, me, sibling, src=x_ref)]
        first += [copy(1 + j, me, (*chip, c), src=x_ref) for j, chip in enumerate(chips)]
        passed = [copy(4 + j, (*chip, c), sibling) for j, chip in enumerate(chips)]
        return c, chips, me, sibling, copy, mine, first, passed

    def start(ins, outs, send_sems, recv_sems, local_sem):
        _, _, _, _, _, mine, first, _ = copies(ins, outs, send_sems, recv_sems, local_sem)
        mine.start()
        for cp in first:
            cp.start()

    def finish(ins, outs, send_sems, recv_sems, local_sem):
        c, chips, me, sibling, copy, mine, first, passed = copies(ins, outs, send_sems, recv_sems, local_sem)
        for j, chip in enumerate(chips):
            copy(1 + j, (*chip, c), me).wait_recv()
            passed[j].start()
        copy(0, sibling, me).wait_recv()
        for j, chip in enumerate(chips):
            copy(4 + j, (*chip, 1 - c), me).wait_recv()
        for cp in first + passed:
            cp.wait_send()
        mine.wait()

    return _Carried([packed], [_sds((N_DEV,) + packed.shape, F32)],
                    [pltpu.SemaphoreType.DMA((7,)), pltpu.SemaphoreType.DMA((7,)), pltpu.SemaphoreType.DMA],
                    start, finish)


def _ordered_sum(name, slabs):
    _, m_per, n_cols = slabs.shape

    def body(s_ref, o_ref):
        acc = s_ref[0]
        for d in range(1, N_DEV):
            acc = acc + s_ref[d]
        o_ref[...] = acc

    vmem = pl.BlockSpec(memory_space=pltpu.VMEM)
    return pl.pallas_call(body, name=name, out_shape=_sds((m_per, n_cols), F32), in_specs=[vmem], out_specs=vmem)(slabs)


def _pair_sum(core, own, got, tm):
    _, half, cols = got.shape
    nb = half // tm

    def body(c_ref, a_ref, b_ref, o_ref):
        o_ref[...] = (a_ref[...].astype(F32) + b_ref[...].astype(F32)).astype(BF16)

    return pl.pallas_call(
        body, name="grad_pair_sum", out_shape=_sds(got.shape, BF16),
        grid_spec=pltpu.PrefetchScalarGridSpec(
            num_scalar_prefetch=1, grid=(N_CHIPS, nb),
            in_specs=[pl.BlockSpec((None, tm, cols), lambda s, i, c_ref: (s, c_ref[0] * nb + i, 0)),
                      pl.BlockSpec((None, tm, cols), lambda s, i, c_ref: (s, i, 0))],
            out_specs=pl.BlockSpec((None, tm, cols), lambda s, i, c_ref: (s, i, 0))),
        compiler_params=_cparams(2),
    )(core, own, got)


def _chip_sum(slabs, tm):
    _, half, cols = slabs.shape

    def body(s_ref, o_ref):
        acc = s_ref[0].astype(F32)
        for k in range(1, N_CHIPS):
            acc = acc + s_ref[k].astype(F32)
        o_ref[...] = acc

    return pl.pallas_call(
        body, name="grad_chip_sum", out_shape=_sds((half, cols), F32), grid=(half // tm,),
        in_specs=[pl.BlockSpec((N_CHIPS, tm, cols), lambda i: (0, i, 0))],
        out_specs=pl.BlockSpec((tm, cols), lambda i: (i, 0)), compiler_params=_cparams(1),
    )(slabs)


def _adam_math(w, g, m, v):
    m2 = ADAM_B1 * m + (1.0 - ADAM_B1) * g
    v2 = ADAM_B2 * v + (1.0 - ADAM_B2) * (g * g)
    m_hat = m2 / (1.0 - ADAM_B1 ** ADAM_STEP)
    v_hat = v2 / (1.0 - ADAM_B2 ** ADAM_STEP)
    delta = -ADAM_LR * (m_hat / (jnp.sqrt(v_hat) + ADAM_EPS) + ADAM_WD * w)
    return delta, m2, v2


def _adamw_halves(name, items, tm, carried=None):
    rows, cols = items[0][0].shape
    nb = rows // 2 // tm
    n = len(items)

    def body(*refs):
        mine = (pl.program_id(0) // nb) == lax.axis_index("c")
        for k in range(n):
            w_ref, own_ref, oth_ref, m_ref, v_ref = refs[5 * k:5 * k + 5]
            g_ref, d_ref, m2_ref, v2_ref = refs[5 * n + 4 * k:5 * n + 4 * k + 4]
            g = jnp.where(mine, own_ref[...], oth_ref[...])
            d, m2, v2 = _adam_math(w_ref[...], g, m_ref[...], v_ref[...])
            g_ref[...] = g
            d_ref[...] = d
            m2_ref[...] = m2
            v2_ref[...] = v2

    full = pl.BlockSpec((tm, cols), lambda i: (i, 0))
    half = pl.BlockSpec((tm, cols), lambda i: (i % nb, 0))
    return _call_carrying(
        body, carried, name=name, grid=(rows // tm,), in_specs=[full, half, half, full, full] * n,
        out_specs=[full] * (4 * n), out_shape=tuple([_sds((rows, cols), F32)] * (4 * n)), scratch_shapes=[],
        operands=[a for item in items for a in item])


def _adamw(name, w, g, m, v, tm):
    def body(w_ref, g_ref, m_ref, v_ref, gout_ref, d_ref, m2_ref, v2_ref):
        gv = g_ref[...]
        d, m2, v2 = _adam_math(w_ref[...], gv, m_ref[...], v_ref[...])
        gout_ref[...] = gv
        d_ref[...] = d
        m2_ref[...] = m2
        v2_ref[...] = v2

    return _rows_call(name, body, tm, [w, g, m, v], [], [_sds(w.shape, F32)] * 4)


_SMALL_NAMES = ("norm_mix_pre", "gm_ln_w", "gm_ln_b", "gm_w_s", "gm_b_s", "conv_w", "conv_b", "dt_bias", "a_log",
                "d_skip", "ssm_norm_w", "norm_mix_post", "norm_ffn_pre", "norm_ffn_post")
_PACK_COLS = 1024


def _pack(parts, names=_SMALL_NAMES, tail=None):
    pieces = [parts[n].reshape(-1) for n in names]
    flat = jnp.concatenate(pieces if tail is None else pieces + [tail])
    rows = -(-flat.shape[0] // (8 * _PACK_COLS)) * 8
    flat = jnp.pad(flat, (0, rows * _PACK_COLS - flat.shape[0]))
    return flat.reshape(rows, _PACK_COLS)


def _unpack(packed, shapes, names=_SMALL_NAMES):
    flat = packed.reshape(-1)
    out, off = {}, 0
    for n in names:
        size = 1
        for s in shapes[n]:
            size *= s
        out[n] = flat[off:off + size].reshape(shapes[n])
        off += size
    return out


def kernel(x, norm_mix_pre, w_in, gm_ln_w, gm_ln_b, gm_w_s, gm_b_s, conv_w, conv_b, dt_bias, a_log, d_skip, ssm_norm_w, w_out, norm_mix_post, norm_ffn_pre, w_up, w_down, norm_ffn_post, loss_target, m_norm_mix_pre, m_w_in, m_gm_ln_w, m_gm_ln_b, m_gm_w_s, m_gm_b_s, m_conv_w, m_conv_b, m_dt_bias, m_a_log, m_d_skip, m_ssm_norm_w, m_w_out, m_norm_mix_post, m_norm_ffn_pre, m_w_up, m_w_down, m_norm_ffn_post, v_norm_mix_pre, v_w_in, v_gm_ln_w, v_gm_ln_b, v_gm_w_s, v_gm_b_s, v_conv_w, v_conv_b, v_dt_bias, v_a_log, v_d_skip, v_ssm_norm_w, v_w_out, v_norm_mix_post, v_norm_ffn_pre, v_w_up, v_w_down, v_norm_ffn_post):
    params = dict(norm_mix_pre=norm_mix_pre, w_in=w_in, gm_ln_w=gm_ln_w, gm_ln_b=gm_ln_b, gm_w_s=gm_w_s, gm_b_s=gm_b_s,
                  conv_w=conv_w, conv_b=conv_b, dt_bias=dt_bias, a_log=a_log, d_skip=d_skip, ssm_norm_w=ssm_norm_w,
                  w_out=w_out, norm_mix_post=norm_mix_post, norm_ffn_pre=norm_ffn_pre, w_up=w_up, w_down=w_down,
                  norm_ffn_post=norm_ffn_post)
    mom1 = dict(norm_mix_pre=m_norm_mix_pre, w_in=m_w_in, gm_ln_w=m_gm_ln_w, gm_ln_b=m_gm_ln_b, gm_w_s=m_gm_w_s,
                gm_b_s=m_gm_b_s, conv_w=m_conv_w, conv_b=m_conv_b, dt_bias=m_dt_bias, a_log=m_a_log, d_skip=m_d_skip,
                ssm_norm_w=m_ssm_norm_w, w_out=m_w_out, norm_mix_post=m_norm_mix_post, norm_ffn_pre=m_norm_ffn_pre,
                w_up=m_w_up, w_down=m_w_down, norm_ffn_post=m_norm_ffn_post)
    mom2 = dict(norm_mix_pre=v_norm_mix_pre, w_in=v_w_in, gm_ln_w=v_gm_ln_w, gm_ln_b=v_gm_ln_b, gm_w_s=v_gm_w_s,
                gm_b_s=v_gm_b_s, conv_w=v_conv_w, conv_b=v_conv_b, dt_bias=v_dt_bias, a_log=v_a_log, d_skip=v_d_skip,
                ssm_norm_w=v_ssm_norm_w, w_out=v_w_out, norm_mix_post=v_norm_mix_post, norm_ffn_pre=v_norm_ffn_pre,
                w_up=v_w_up, w_down=v_w_down, norm_ffn_post=v_norm_ffn_post)
    names = list(params)
    big = ("w_in", "w_out", "w_up", "w_down")
    chip = 2 * lax.axis_index("x") + lax.axis_index("y")

    shards = {n: params[n][0].astype(BF16) for n in big}
    conv_shard = jnp.pad(conv_w[0], ((0, 16 - CONV_K), (0, 0)))
    g_in4, g_conv4 = _run_exchange("allgather_w_in", _allgather_exchange([shards["w_in"], conv_shard]))
    conv_full = jnp.transpose(g_conv4[:, :CONV_K, :], (1, 0, 2)).reshape(CONV_K, CONV_CH)

    small = {n: params[n][0] if params[n].ndim >= 3 else params[n] for n in _SMALL_NAMES if n != "conv_w"}
    core = lax.axis_index("c").astype(jnp.int32).reshape(1)
    adam_args = {n: (params[n][0], mom1[n][0], mom2[n][0]) for n in big}
    loss, grad_x, big_out, small_sum = _forward_backward(
        x, loss_target, g_in4, conv_full, small, shards["w_out"], shards["w_up"], shards["w_down"], core, adam_args)
    grads, delta, new_m, new_v = {}, {}, {}, {}
    for n in big:
        grads[n], delta[n], new_m[n], new_v[n] = [a[None] for a in big_out[n]]

    small_sum["conv_w"] = lax.dynamic_slice_in_dim(small_sum["conv_w"], chip * (CONV_CH // N_CHIPS), CONV_CH // N_CHIPS, axis=1)

    local_shapes = {n: params[n].shape[1:] if params[n].ndim >= 3 else params[n].shape for n in _SMALL_NAMES}
    flat = lambda tree: {n: tree[n].reshape(local_shapes[n]) for n in _SMALL_NAMES}
    packed = [_pack(flat(t)) for t in (params, small_sum, mom1, mom2)]
    _, d_p, m_p, v_p = _adamw("adamw_small", *packed, packed[0].shape[0])
    for src, dst in ((d_p, delta), (m_p, new_m), (v_p, new_v)):
        for n, val in _unpack(src, local_shapes).items():
            dst[n] = val.reshape(params[n].shape)
    for n in _SMALL_NAMES:
        grads[n] = small_sum[n].reshape(params[n].shape)

    out = [loss, grad_x]
    for tree in (grads, delta, new_m, new_v):
        out += [tree[n] for n in names]
    return tuple(out)
```

```python
import functools

import jax
import jax.numpy as jnp
from jax import lax
from jax.experimental import pallas as pl
from jax.experimental.pallas import tpu as pltpu

F32 = jnp.float32
BF16 = jnp.bfloat16
MESH = pl.DeviceIdType.MESH

EPS = 1e-6
D_MODEL = 1024
GM_WIDTH = 512
SSM_WIDTH = 512
N_HEADS = 8
HEAD_DIM = 64
CHUNK = 128
SSM_GROUPS = 2
GROUP_W = SSM_WIDTH // SSM_GROUPS
SSM_STATE = 128
CONV_K = 4
CONV_CH = 1024
D_FF = 4096
IN_COLS = 2568
DT_PAD = 128
SSD_COLS = SSM_WIDTH + CONV_CH + DT_PAD
N_CHIPS = 4
N_DEV = 8

ADAM_LR = 0.001
ADAM_B1 = 0.9
ADAM_B2 = 0.999
ADAM_EPS = 1e-08
ADAM_WD = 0.01
ADAM_STEP = 10

VMEM_LIMIT_BYTES = 56 * 1024 * 1024
FF_TILE = 512
DW_TOKENS_PER_STEP = 2048


def _cparams(n_axes):
    return pltpu.CompilerParams(dimension_semantics=("arbitrary",) * n_axes, vmem_limit_bytes=VMEM_LIMIT_BYTES)


def _dot(a, b):
    return jnp.dot(a.astype(BF16), b.astype(BF16), preferred_element_type=F32)


def _dot_nt(a, b):
    return lax.dot_general(a.astype(BF16), b.astype(BF16), (((1,), (1,)), ((), ())), preferred_element_type=F32)


def _dot_tn(a, b):
    return lax.dot_general(a.astype(BF16), b.astype(BF16), (((0,), (0,)), ((), ())), preferred_element_type=F32)


def _sigmoid(x):
    return 1.0 / (1.0 + jnp.exp(-x))


_GELU_C = 0.7978845608028654
_GELU_A = 0.044715


def _gelu(x):
    t = jnp.tanh(_GELU_C * (x + _GELU_A * (x * x * x)))
    return 0.5 * x * (1.0 + t), t


def _gelu_grad(x, t):
    return 0.5 * (1.0 + t) + 0.5 * x * (1.0 - t * t) * (_GELU_C * (1.0 + 3.0 * _GELU_A * x * x))


def _rms_fwd(x, w):
    r = lax.rsqrt(jnp.mean(x * x, axis=-1, keepdims=True) + EPS)
    return x * r * w, r


def _rms_bwd(x, r, w, dy):
    g = dy * w
    dx = r * g - x * (r * r * r) * jnp.mean(g * x, axis=-1, keepdims=True)
    dw = jnp.sum(dy * x * r, axis=0, keepdims=True)
    return dx, dw


class _Carried:
    def __init__(self, ins, out_shapes, sems, start, finish):
        self.ins, self.out_shapes, self.sems = list(ins), list(out_shapes), list(sems)
        self.start, self.finish = start, finish


def _both(first, second):
    n_i, n_o, n_s = len(first.ins), len(first.out_shapes), len(first.sems)

    def split(ins, outs, sems):
        return (ins[:n_i], outs[:n_o], sems[:n_s]), (ins[n_i:], outs[n_o:], sems[n_s:])

    def start(ins, outs, *sems):
        (i1, o1, s1), (i2, o2, s2) = split(ins, outs, sems)
        first.start(i1, o1, *s1)
        second.start(i2, o2, *s2)

    def finish(ins, outs, *sems):
        (i1, o1, s1), (i2, o2, s2) = split(ins, outs, sems)
        first.finish(i1, o1, *s1)
        second.finish(i2, o2, *s2)

    return _Carried(first.ins + second.ins, first.out_shapes + second.out_shapes, first.sems + second.sems, start, finish)


def _split_carried(refs, n_in, n_out, n_scratch, carried):
    n_ci, n_co, n_cs = len(carried.ins), len(carried.out_shapes), len(carried.sems)
    ins, rest = refs[:n_in], refs[n_in:]
    c_ins, rest = rest[:n_ci], rest[n_ci:]
    outs, rest = rest[:n_out], rest[n_out:]
    c_outs, rest = rest[:n_co], rest[n_co:]
    scr, c_sems = rest[:n_scratch], rest[n_scratch:]
    assert len(c_sems) == n_cs
    return tuple(ins) + tuple(outs) + tuple(scr), c_ins, c_outs, c_sems


def _rows_call(name, body, tm, row_ins, const_ins, row_outs, acc_outs=(), scratch=(), carried=None):
    n_rows = row_ins[0].shape[0]
    assert n_rows % tm == 0
    n_steps = n_rows // tm
    n_in = len(row_ins) + len(const_ins)
    n_ro = len(row_outs)
    n_acc = len(acc_outs)

    def kern(*refs):
        accs = refs[n_in + n_ro:n_in + n_ro + n_acc]

        @pl.when(pl.program_id(0) == 0)
        def _():
            for a in accs:
                a[...] = jnp.zeros_like(a)

        body(*refs)

    def whole(shape):
        nd = len(shape)
        return pl.BlockSpec(tuple(shape), lambda i: (0,) * nd)

    in_specs = [pl.BlockSpec((tm, a.shape[1]), lambda i: (i, 0)) for a in row_ins]
    in_specs += [whole(a.shape) for a in const_ins]
    out_specs = [pl.BlockSpec((tm, s.shape[1]), lambda i: (i, 0)) for s in row_outs]
    out_specs += [whole(s.shape) for s in acc_outs]
    return _call_carrying(
        kern, carried, name=name, grid=(n_steps,), in_specs=in_specs, out_specs=out_specs,
        out_shape=tuple(row_outs) + tuple(acc_outs), scratch_shapes=list(scratch), operands=list(row_ins) + list(const_ins))


def _call_carrying(body, carried, *, name, grid, in_specs, out_specs, out_shape, scratch_shapes, operands):
    n_in, n_out, n_scratch = len(in_specs), len(out_specs), len(scratch_shapes)
    kern = body
    if carried is not None:
        def kern(*refs):
            plain, c_ins, c_outs, c_sems = _split_carried(refs, n_in, n_out, n_scratch, carried)
            first, last = True, True
            for d, size in enumerate(grid):
                first = jnp.logical_and(first, pl.program_id(d) == 0)
                last = jnp.logical_and(last, pl.program_id(d) == size - 1)

            @pl.when(first)
            def _():
                carried.start(c_ins, c_outs, *c_sems)

            body(*plain)

            @pl.when(last)
            def _():
                carried.finish(c_ins, c_outs, *c_sems)

        in_specs = list(in_specs) + [_HBM] * len(carried.ins)
        out_specs = list(out_specs) + [_HBM] * len(carried.out_shapes)
        out_shape = tuple(out_shape) + tuple(carried.out_shapes)
        operands = list(operands) + carried.ins
        scratch_shapes = list(scratch_shapes) + carried.sems
    return pl.pallas_call(
        kern, name=name, grid=grid, in_specs=in_specs, out_specs=out_specs, out_shape=out_shape,
        scratch_shapes=scratch_shapes, compiler_params=_cparams(len(grid)),
    )(*operands)


def _sds(shape, dtype):
    return jax.ShapeDtypeStruct(tuple(shape), dtype)


def _matmul_tn(name, a, b, tm, tn, tk, stacked=False, carried=None):
    k_dim, m_dim = a.shape
    n_dim = b.shape[1]
    assert m_dim % tm == 0 and n_dim % tn == 0 and k_dim % tk == 0
    nk = k_dim // tk

    def kern(a_ref, b_ref, o_ref, acc_ref):
        k = pl.program_id(2)
        prod = _dot_tn(a_ref[...], b_ref[...])

        @pl.when(k == 0)
        def _():
            acc_ref[...] = prod

        @pl.when(k > 0)
        def _():
            acc_ref[...] += prod

        @pl.when(k == nk - 1)
        def _():
            o_ref[...] = acc_ref[...].astype(o_ref.dtype)

    if stacked:
        assert tm == m_dim
        out_shape = _sds((n_dim // tn, m_dim, tn), BF16)
        out_spec = pl.BlockSpec((None, tm, tn), lambda i, j, k: (j, i, 0))
    else:
        out_shape = _sds((m_dim, n_dim), BF16)
        out_spec = pl.BlockSpec((tm, tn), lambda i, j, k: (i, j))
    outs = _call_carrying(
        kern, carried, name=name, grid=(m_dim // tm, n_dim // tn, nk),
        in_specs=[pl.BlockSpec((tk, tm), lambda i, j, k: (k, i)), pl.BlockSpec((tk, tn), lambda i, j, k: (k, j))],
        out_specs=[out_spec], out_shape=(out_shape,), scratch_shapes=[pltpu.VMEM((tm, tn), F32)], operands=[a, b])
    return outs[0] if carried is None else outs


def _inproj_fwd(x, nw, w_uv, w_ssd, tm=256, carried=None):
    n_tok = x.shape[0]

    def body(x_ref, nw_ref, wuv_ref, wssd_ref, puv_ref, pssd_ref):
        h, _ = _rms_fwd(x_ref[...], nw_ref[...])
        h = h.astype(BF16)
        puv_ref[...] = jnp.dot(h, wuv_ref[...], preferred_element_type=F32)
        pssd_ref[...] = jnp.dot(h, wssd_ref[...], preferred_element_type=F32)

    return _rows_call("inproj_fwd", body, tm, [x], [nw, w_uv, w_ssd],
                      [_sds((n_tok, 2 * GM_WIDTH), F32), _sds((n_tok, SSD_COLS), F32)], carried=carried)


_Z_COLS = pl.ds(0, SSM_WIDTH)
_XBC_COLS = pl.ds(SSM_WIDTH, CONV_CH)
_DT_COLS = pl.ds(SSM_WIDTH + CONV_CH, DT_PAD)


def _head_lane_mask(width, head):
    lane = lax.broadcasted_iota(jnp.int32, (1, width), 1)
    return (lane // HEAD_DIM) == head


def _split_terms(x, terms):
    parts = []
    for _ in range(terms):
        p = x.astype(BF16)
        parts.append(p)
        x = x - p.astype(F32)
    return parts


def _seg_dots(vals, ind, terms=2):
    m = vals[0].shape[0]
    parts = []
    for v in vals:
        parts += _split_terms(v, terms)
    red = jnp.dot(jnp.concatenate(parts, axis=0), ind, preferred_element_type=F32)
    outs = []
    for i in range(len(vals)):
        acc = red[i * terms * m:(i * terms + 1) * m]
        for t in range(1, terms):
            acc = acc + red[(i * terms + t) * m:(i * terms + t + 1) * m]
        outs.append(acc)
    return outs


def _tri_dot(mask, x, terms=3):
    n = x.shape[1]
    red = jnp.dot(mask.astype(BF16), jnp.concatenate(_split_terms(x, terms), axis=1), preferred_element_type=F32)
    acc = red[:, :n]
    for t in range(1, terms):
        acc = acc + red[:, t * n:(t + 1) * n]
    return acc


def _gmlp_common(puv, lnw, lnb, e_bf, et_bf):
    u = puv[:, :GM_WIDTH]
    v = puv[:, GM_WIDTH:]
    gu, tu = _gelu(u)
    gv, tv = _gelu(v)
    (s1,) = _seg_dots([gv], et_bf)
    (mu,) = _seg_dots([s1 * (1.0 / HEAD_DIM)], e_bf)
    xc = gv - mu
    (s2,) = _seg_dots([xc * xc], et_bf)
    (rstd,) = _seg_dots([lax.rsqrt(s2 * (1.0 / HEAD_DIM) + EPS)], e_bf)
    xhat = xc * rstd
    vn = xhat * lnw + lnb
    return u, v, gu, tu, tv, rstd, xhat, vn


def _tril_mask():
    r = lax.broadcasted_iota(jnp.int32, (CHUNK, CHUNK), 0)
    c = lax.broadcasted_iota(jnp.int32, (CHUNK, CHUNK), 1)
    return r >= c


def _head_blocks(v):
    return jnp.concatenate([jnp.where(_head_lane_mask(GM_WIDTH, h), v, jnp.zeros_like(v)) for h in range(N_HEADS)], axis=0)


def _causal_w_cat(w_cat):
    t = lax.broadcasted_iota(jnp.int32, (CHUNK, N_HEADS * CHUNK), 0)
    s = lax.broadcasted_iota(jnp.int32, (CHUNK, N_HEADS * CHUNK), 1) % CHUNK
    return jnp.where(t >= s, w_cat, 0.0).astype(BF16)


def _gmlp_chunk_fwd(puv, lnw, lnb, e_bf, et_bf, wm, bmap):
    _, _, gu, _, _, _, _, vn = _gmlp_common(puv, lnw, lnb, e_bf, et_bf)
    mixed = jnp.dot(wm, _head_blocks(vn.astype(BF16)), preferred_element_type=F32) + bmap
    return (gu * mixed).astype(BF16)


SUBLANES = 8


def _shift_down(x, tail, s):
    main = pltpu.roll(x, s, 0)
    row = lax.broadcasted_iota(jnp.int32, (SUBLANES, 1), 0)
    head = jnp.where(row < s, pltpu.roll(tail, s, 0), main[:SUBLANES])
    return jnp.concatenate([head, main[SUBLANES:]], axis=0)


def _shift_up(x, head_next, s):
    n = x.shape[0]
    main = pltpu.roll(x, n - s, 0)
    row = lax.broadcasted_iota(jnp.int32, (SUBLANES, 1), 0)
    last = jnp.where(row >= SUBLANES - s, pltpu.roll(head_next, SUBLANES - s, 0), main[n - SUBLANES:])
    return jnp.concatenate([main[:n - SUBLANES], last], axis=0)


def _ssd_pre(xr, tail, cw_ref, cb, pdt, dtb, alog, emap):
    rowi = lax.broadcasted_iota(jnp.int32, (CHUNK, 1), 0)
    shifted = [_shift_down(xr, tail, 3), _shift_down(xr, tail, 2), _shift_down(xr, tail, 1), xr]
    xc = cb
    for k in range(CONV_K):
        xc = xc + cw_ref[k] * shifted[k]
    sg = _sigmoid(xc)
    xa = xc * sg
    pre = pdt + dtb
    dt = jnp.maximum(pre, 0.0) + jnp.log(1.0 + jnp.exp(-jnp.abs(pre)))
    a_neg = -jnp.exp(alog)
    a_cs = _tri_dot(_tril_mask(), dt * a_neg)
    acs_map, dt_map = _seg_dots([a_cs, dt], emap, terms=3)
    return dict(shifted=shifted, xc=xc, sg=sg, xa=xa, pre=pre, dt=dt, a_neg=a_neg, a_cs=a_cs,
                acs_map=acs_map, dt_map=dt_map, rowi=rowi)


def _ssd_maps(p):
    last = p["rowi"] == CHUNK - 1
    aq_map = jnp.sum(jnp.where(last, p["acs_map"], 0.0), axis=0, keepdims=True)
    e_exp = jnp.exp(p["acs_map"])
    dte = jnp.exp(aq_map - p["acs_map"])
    cd = jnp.exp(aq_map)
    return last, e_exp, dte, cd


def _head_decay(a_cs, a_cs_t, head, tri):
    lane = lax.broadcasted_iota(jnp.int32, (1, DT_PAD), 1)
    sub = lax.broadcasted_iota(jnp.int32, (DT_PAD, 1), 0)
    col = jnp.sum(jnp.where(lane == head, a_cs, 0.0), axis=1, keepdims=True)
    row = jnp.sum(jnp.where(sub == head, a_cs_t, 0.0), axis=0, keepdims=True)
    return jnp.exp(jnp.where(tri, col - row, -1e30))


def _gate_fwd(y, z, nw):
    sz = _sigmoid(z)
    zg = z * sz
    yg = y * zg
    outs, rs = [], []
    for g in range(SSM_GROUPS):
        gs = slice(g * GROUP_W, (g + 1) * GROUP_W)
        o, r = _rms_fwd(yg[:, gs], nw[:, gs])
        outs.append(o)
        rs.append(r)
    return sz, zg, yg, outs, rs


def _ssd_const_specs():
    def whole(shape):
        nd = len(shape)
        return pl.BlockSpec(tuple(shape), lambda c: (0,) * nd)
    return [whole((CONV_K, 1, CONV_CH)), whole((1, CONV_CH)), whole((1, DT_PAD)), whole((1, DT_PAD)),
            whole((1, SSM_WIDTH)), whole((1, SSM_WIDTH)), whole((DT_PAD, SSM_WIDTH)), whole((SSM_WIDTH, DT_PAD))]


def _mixer_fwd(p_uv, p_ssd, x, lnw, lnb, w_cat, bmap, w_out, nw_post, nw_pre2, conv_w, conv_b, dt_bias, a_log,
               dskip_map, norm_w, e_bf, et_bf, n_seq, carried=None):
    n_tok = p_uv.shape[0]
    nc = n_tok // n_seq // CHUNK

    def body(puv3, ps3, x3, lnw_ref, lnb_ref, wcat_ref, bmap_ref, wo_ref, nwa_ref, nwb_ref,
             cw_ref, cb_ref, dtb_ref, alog_ref, dsk_ref, nw_ref, e_ref, et_ref,
             mix3, yssd3, sprev3, o3, x13, h23, wm_scr, prev3_scr, s3_scr):
        @pl.when(pl.program_id(0) == 0)
        def _():
            wm_scr[...] = _causal_w_cat(wcat_ref[...])
            prev3_scr[...] = jnp.zeros_like(prev3_scr)
            s3_scr[...] = jnp.zeros_like(s3_scr)

        for b in range(n_seq):
            one_sequence(puv3.at[b], ps3.at[b, :, _XBC_COLS], ps3.at[b, :, _Z_COLS], ps3.at[b, :, _DT_COLS],
                         lnw_ref, lnb_ref, bmap_ref, cw_ref, cb_ref, dtb_ref, alog_ref, dsk_ref, nw_ref, e_ref, et_ref,
                         mix3.at[b], yssd3.at[b], sprev3.at[b], wm_scr, prev3_scr.at[b], s3_scr.at[b])
            o = jnp.dot(mix3[b], wo_ref[...], preferred_element_type=F32)
            on, _ = _rms_fwd(o, nwa_ref[...])
            x1 = x3[b] + on
            h2, _ = _rms_fwd(x1, nwb_ref[...])
            o3[b] = o
            x13[b] = x1
            h23[b] = h2.astype(BF16)

    def one_sequence(puv_ref, xr_ref, z_ref, pdt_ref, lnw_ref, lnb_ref, bmap_ref,
                     cw_ref, cb_ref, dtb_ref, alog_ref, dsk_ref, nw_ref, e_ref, et_ref,
                     mix_ref, yssd_ref, sprev_ref, wm_scr, prev_scr, s_scr):
        mix_ref[:, :GM_WIDTH] = _gmlp_chunk_fwd(puv_ref[...], lnw_ref[...], lnb_ref[...], e_ref[...], et_ref[...], wm_scr[...],
                                      bmap_ref[...])
        xr = xr_ref[...]
        p = _ssd_pre(xr, prev_scr[...], cw_ref, cb_ref[...], pdt_ref[...], dtb_ref[...], alog_ref[...], e_ref[...])
        _, e_exp, dte, cd = _ssd_maps(p)
        xs = p["xa"][:, :SSM_WIDTH]
        xd = xs * p["dt_map"]
        a_cs_t = p["a_cs"].T
        tri = _tril_mask()
        s_old = s_scr[...]
        sprev_ref[...] = s_old
        for g in range(SSM_GROUPS):
            gs = slice(g * GROUP_W, (g + 1) * GROUP_W)
            bm = p["xa"][:, SSM_WIDTH + g * SSM_STATE: SSM_WIDTH + (g + 1) * SSM_STATE].astype(BF16)
            cm = p["xa"][:, SSM_WIDTH + (SSM_GROUPS + g) * SSM_STATE: SSM_WIDTH + (SSM_GROUPS + g + 1) * SSM_STATE].astype(BF16)
            cb_mat = _dot_nt(cm, bm)
            xdg = xd[:, gs].astype(BF16)
            y_g = _dot(cm, s_old[:, gs]) * e_exp[:, gs] + dsk_ref[:, gs] * xs[:, gs]
            for r in range(SSM_GROUPS * 2):
                dm = _head_decay(p["a_cs"], a_cs_t, g * 4 + r, tri)
                full = jnp.dot((cb_mat * dm).astype(BF16), xdg, preferred_element_type=F32)
                y_g = y_g + jnp.where(_head_lane_mask(GROUP_W, r), full, 0.0)
            yssd_ref[:, gs] = y_g
            s_scr[:, gs] = cd[:, gs] * s_old[:, gs] + _dot_tn(bm, xd[:, gs] * dte[:, gs])
        _, _, _, outs, _ = _gate_fwd(yssd_ref[...], z_ref[...], nw_ref[...])
        for g in range(SSM_GROUPS):
            mix_ref[:, GM_WIDTH + g * GROUP_W:GM_WIDTH + (g + 1) * GROUP_W] = outs[g].astype(BF16)
        prev_scr[...] = xr[CHUNK - SUBLANES:, :]

    seq_len = n_tok // n_seq

    def rows(width):
        return pl.BlockSpec((n_seq, CHUNK, width), lambda c: (0, c, 0))

    def whole(shape):
        nd = len(shape)
        return pl.BlockSpec(tuple(shape), lambda c: (0,) * nd)

    def by_seq(a):
        return a.reshape(n_seq, seq_len, a.shape[-1])

    outs = _call_carrying(
        body, carried, name="mixer_fwd", grid=(nc,),
        in_specs=[rows(2 * GM_WIDTH), rows(SSD_COLS), rows(D_MODEL), whole(lnw.shape),
                  whole(lnb.shape), whole(w_cat.shape), whole(bmap.shape), whole(w_out.shape), whole(nw_post.shape),
                  whole(nw_pre2.shape)] + _ssd_const_specs(),
        out_specs=[rows(D_MODEL), rows(SSM_WIDTH), rows(SSM_WIDTH), rows(D_MODEL), rows(D_MODEL), rows(D_MODEL)],
        out_shape=(_sds((n_seq, seq_len, D_MODEL), BF16),
                   _sds((n_seq, seq_len, SSM_WIDTH), F32), _sds((n_seq, seq_len, SSM_WIDTH), F32),
                   _sds((n_seq, seq_len, D_MODEL), F32), _sds((n_seq, seq_len, D_MODEL), F32),
                   _sds((n_seq, seq_len, D_MODEL), BF16)),
        scratch_shapes=[pltpu.VMEM((CHUNK, N_HEADS * CHUNK), BF16), pltpu.VMEM((n_seq, SUBLANES, CONV_CH), F32),
                        pltpu.VMEM((n_seq, SSM_STATE, SSM_WIDTH), F32)],
        operands=[by_seq(p_uv), by_seq(p_ssd), by_seq(x), lnw, lnb, w_cat, bmap, w_out, nw_post,
                  nw_pre2, conv_w, conv_b, dt_bias, a_log, dskip_map, norm_w, e_bf, et_bf])
    return tuple(o.reshape(n_tok, o.shape[-1]) for o in outs[:6]) + tuple(outs[6:])


def _up_cols(wup_ref, j):
    per = (D_FF // N_CHIPS) // FF_TILE
    return wup_ref[j // per, :, (j % per) * FF_TILE:(j % per + 1) * FF_TILE]


def _down_rows(wda_ref, wdb_ref, j):
    assert 2 * FF_TILE == D_FF // N_CHIPS
    return (wda_ref if j % 2 == 0 else wdb_ref)[j // 2]


def _skewed_rows_call(name, main, tail, tm, lead_ins, lag_ins, const_ins, lead_outs, lag_outs, acc_outs, carry,
                      streamed, tile_copies, n_copies):
    n_rows = lead_ins[0].shape[0]
    assert n_rows % tm == 0
    n = n_rows // tm
    counts = [len(lead_ins), len(lag_ins), len(const_ins), len(streamed), len(lead_outs), len(lag_outs), len(acc_outs),
              1, len(streamed)]

    def kern(*refs):
        groups, pos = [], 0
        for cnt in counts:
            groups.append(refs[pos:pos + cnt])
            pos += cnt
        lead_i, lag_i, consts, w_hbm, lead_o, lag_o, accs, (carry_scr,), w_vmem = groups
        sems = refs[pos]
        i = pl.program_id(0)
        pieces, k = [], 0
        for piece in tile_copies(w_hbm, w_vmem):
            pieces.append([pltpu.make_async_copy(src, dst, sems.at[k + q]) for q, (src, dst) in enumerate(piece)])
            k += len(piece)

        def ready(j):
            for cp in pieces[j]:
                cp.wait()

        @pl.when(i == 0)
        def _():
            for piece in pieces:
                for cp in piece:
                    cp.start()
            for a in accs:
                a[...] = jnp.zeros_like(a)
            carry_scr[...] = main(lead_i, consts, lead_o, w_vmem, ready)

        @pl.when(jnp.logical_and(i > 0, i < n))
        def _():
            previous = carry_scr[...]
            carry_scr[...] = main(lead_i, consts, lead_o, w_vmem, lambda j: None)
            tail(previous, lag_i, consts, lag_o, accs)

        @pl.when(i == n)
        def _():
            tail(carry_scr[...], lag_i, consts, lag_o, accs)

    def lead(width):
        return pl.BlockSpec((tm, width), lambda i: (jnp.minimum(i, n - 1), 0))

    def lag(width):
        return pl.BlockSpec((tm, width), lambda i: (jnp.maximum(i - 1, 0), 0))

    def whole(shape):
        nd = len(shape)
        return pl.BlockSpec(tuple(shape), lambda i: (0,) * nd)

    return pl.pallas_call(
        kern, name=name, grid=(n + 1,),
        in_specs=([lead(a.shape[1]) for a in lead_ins] + [lag(a.shape[1]) for a in lag_ins]
                  + [whole(a.shape) for a in const_ins] + [_HBM] * len(streamed)),
        out_specs=[lead(s.shape[1]) for s in lead_outs] + [lag(s.shape[1]) for s in lag_outs] + [whole(s.shape) for s in acc_outs],
        out_shape=tuple(lead_outs) + tuple(lag_outs) + tuple(acc_outs),
        scratch_shapes=([pltpu.VMEM(carry, F32)] + [pltpu.VMEM(a.shape, a.dtype) for a in streamed]
                        + [pltpu.SemaphoreType.DMA((n_copies,))]),
        compiler_params=_cparams(1),
    )(*lead_ins, *lag_ins, *const_ins, *streamed)


def _mlp_weight_pieces(order):
    per = (D_FF // N_CHIPS) // FF_TILE

    def tile_copies(hbm, vmem):
        pieces = []
        for j in range(D_FF // FF_TILE):
            cols = (j // per, slice(None), pl.ds((j % per) * FF_TILE, FF_TILE))
            up = (hbm[0].at[cols], vmem[0].at[cols])
            down = (hbm[1 + j % 2].at[j // 2], vmem[1 + j % 2].at[j // 2])
            pieces.append([up, down] if order == "up_down" else [down, up])
        return pieces

    return tile_copies


def _mlp_fwd(h2, x1, tgt, w_up, w_down_a, w_down_b, nw, tm=512):
    n_tok = x1.shape[0]

    def main(lead_i, consts, lead_o, weights, ready):
        (h2_ref,), (f_ref,), (wup_ref, wda_ref, wdb_ref) = lead_i, lead_o, weights
        h2v = h2_ref[...]
        acc = jnp.zeros((tm, D_MODEL), F32)
        for j in range(D_FF // FF_TILE):
            cs = slice(j * FF_TILE, (j + 1) * FF_TILE)
            ready(j)
            u = jnp.dot(h2v, _up_cols(wup_ref, j), preferred_element_type=F32)
            f = jnp.square(jnp.maximum(u, 0.0)).astype(BF16)
            f_ref[:, cs] = f
            acc = acc + jnp.dot(f, _down_rows(wda_ref, wdb_ref, j), preferred_element_type=F32)
        return acc

    def tail(acc, lag_i, consts, lag_o, accs):
        (x1_ref, tgt_ref), (nw_ref,), (dd_ref, dy_ref), (loss_ref, dnw_ref) = lag_i, consts, lag_o, accs
        dn, r = _rms_fwd(acc, nw_ref[...])
        e = x1_ref[...] + dn - tgt_ref[...]
        loss_ref[...] += jnp.full(loss_ref.shape, (0.5 / D_MODEL) * jnp.sum(e * e), F32)
        dy = e * (1.0 / D_MODEL)
        dd, dnw = _rms_bwd(acc, r, nw_ref[...], dy)
        dy_ref[...] = dy
        dd_ref[...] = dd.astype(BF16)
        dnw_ref[...] += dnw

    return _skewed_rows_call(
        "mlp_fwd", main, tail, tm, [h2], [x1, tgt], [nw],
        [_sds((n_tok, D_FF), BF16)], [_sds((n_tok, D_MODEL), BF16), _sds((n_tok, D_MODEL), F32)],
        [_sds((8, 128), F32), _sds((1, D_MODEL), F32)], carry=(tm, D_MODEL),
        streamed=[w_up, w_down_a, w_down_b], tile_copies=_mlp_weight_pieces("up_down"), n_copies=2 * (D_FF // FF_TILE))


def _mlp_bwd(dd, f, x1, dy, w_down_a, w_down_b, w_up, nw, tm=256):
    n_tok = x1.shape[0]

    def main(lead_i, consts, lead_o, weights, ready):
        (dd_ref, f_ref), (dup_ref,), (wup_ref, wda_ref, wdb_ref) = lead_i, lead_o, weights
        ddv = dd_ref[...]
        acc = jnp.zeros((tm, D_MODEL), F32)
        for j in range(D_FF // FF_TILE):
            cs = slice(j * FF_TILE, (j + 1) * FF_TILE)
            ready(j)
            df = _dot_nt(ddv, _down_rows(wda_ref, wdb_ref, j))
            du = (df * (2.0 * jnp.sqrt(f_ref[:, cs].astype(F32)))).astype(BF16)
            dup_ref[:, cs] = du
            acc = acc + _dot_nt(du, _up_cols(wup_ref, j))
        return acc

    def tail(acc, lag_i, consts, lag_o, accs):
        (x1_ref, dy_ref), (nw_ref,), (dx1_ref,), (dnw_ref,) = lag_i, consts, lag_o, accs
        x1v = x1_ref[...]
        _, r = _rms_fwd(x1v, nw_ref[...])
        dx, dnw = _rms_bwd(x1v, r, nw_ref[...], acc)
        dx1_ref[...] = dy_ref[...] + dx
        dnw_ref[...] += dnw

    return _skewed_rows_call(
        "mlp_bwd", main, tail, tm, [dd, f], [x1, dy], [nw],
        [_sds((n_tok, D_FF), BF16)], [_sds((n_tok, D_MODEL), F32)], [_sds((1, D_MODEL), F32)], carry=(tm, D_MODEL),
        streamed=[w_up, w_down_a, w_down_b], tile_copies=_mlp_weight_pieces("down_up"), n_copies=2 * (D_FF // FF_TILE))


def _outproj_bwd(dx1, o, w_out, nw, tm=256, carried=None):
    n_tok = dx1.shape[0]

    def body(dx1_ref, o_ref, wo_ref, nw_ref, do_ref, dya_ref, dyb_ref, dnw_ref):
        ov = o_ref[...]
        _, r = _rms_fwd(ov, nw_ref[...])
        do, dnw = _rms_bwd(ov, r, nw_ref[...], dx1_ref[...])
        dob = do.astype(BF16)
        do_ref[...] = dob
        dya_ref[...] = _dot_nt(dob, wo_ref[:GM_WIDTH, :])
        dyb_ref[...] = _dot_nt(dob, wo_ref[GM_WIDTH:, :])
        dnw_ref[...] += dnw

    return _rows_call("outproj_bwd", body, tm, [dx1, o], [w_out, nw],
                      [_sds((n_tok, D_MODEL), BF16), _sds((n_tok, GM_WIDTH), F32), _sds((n_tok, SSM_WIDTH), F32)],
                      [_sds((1, D_MODEL), F32)], carried=carried)


def _gmlp_bwd(p_uv, dya, lnw, lnb, e_bf, et_bf, w_cat, w_stack, bmap, carried=None):
    n_tok = p_uv.shape[0]
    chunks_per_step = 2

    def body(puv_ref, dya_ref, lnw_ref, lnb_ref, e_ref, et_ref, wcat_ref, wstack_ref, bmap_ref,
             dpuv_ref, dws_ref, dbs_ref, dlnw_ref, dlnb_ref, wm_scr, wsm_scr):
        t_stk = lax.broadcasted_iota(jnp.int32, (N_HEADS * CHUNK, CHUNK), 0) % CHUNK
        s_stk = lax.broadcasted_iota(jnp.int32, (N_HEADS * CHUNK, CHUNK), 1)

        @pl.when(pl.program_id(0) == 0)
        def _():
            wm_scr[...] = _causal_w_cat(wcat_ref[...])
            wsm_scr[...] = jnp.where(t_stk >= s_stk, wstack_ref[...], 0.0).astype(BF16)

        lnw_v = lnw_ref[...]
        e_v, et_v = e_ref[...], et_ref[...]

        def one_chunk(rows):
            u, v, gu, tu, tv, rstd, xhat, vn = _gmlp_common(puv_ref[rows, :], lnw_v, lnb_ref[...], e_v, et_v)
            vnb = vn.astype(BF16)
            mixed = jnp.dot(wm_scr[...], _head_blocks(vnb), preferred_element_type=F32) + bmap_ref[...]
            dy = dya_ref[rows, :]
            du = dy * mixed * _gelu_grad(u, tu)
            dmixed = dy * gu
            (dbs,) = _seg_dots([dmixed], et_v)
            dblocks = _head_blocks(dmixed.astype(BF16))
            dvn = lax.dot_general(wsm_scr[...], dblocks, (((0,), (0,)), ((), ())), preferred_element_type=F32)
            dws = lax.dot_general(dblocks, vnb, (((1,), (1,)), ((), ())), preferred_element_type=F32)
            dxh = dvn * lnw_v
            m1, m2 = _seg_dots([dxh, dxh * xhat], et_v)
            m1, m2 = _seg_dots([m1 * (1.0 / HEAD_DIM), m2 * (1.0 / HEAD_DIM)], e_v)
            dgv = rstd * (dxh - m1 - xhat * m2)
            dv = dgv * _gelu_grad(v, tv)
            dpuv_ref[rows, :GM_WIDTH] = du.astype(BF16)
            dpuv_ref[rows, GM_WIDTH:] = dv.astype(BF16)
            return dbs, dws, jnp.sum(dvn * xhat, axis=0, keepdims=True), jnp.sum(dvn, axis=0, keepdims=True)

        parts = [one_chunk(slice(k * CHUNK, (k + 1) * CHUNK)) for k in range(chunks_per_step)]
        dbs, dws, dlnw, dlnb = [functools.reduce(lambda a, b: a + b, vals) for vals in zip(*parts)]
        dbs_ref[...] += dbs
        dws_ref[...] += jnp.where(t_stk >= s_stk, dws, 0.0)
        dlnw_ref[...] += dlnw
        dlnb_ref[...] += dlnb

    return _rows_call(
        "gmlp_bwd", body, chunks_per_step * CHUNK, [p_uv, dya], [lnw, lnb, e_bf, et_bf, w_cat, w_stack, bmap],
        [_sds((n_tok, 2 * GM_WIDTH), BF16)],
        [_sds((N_HEADS * CHUNK, CHUNK), F32), _sds((CHUNK, DT_PAD), F32), _sds((1, GM_WIDTH), F32),
         _sds((1, GM_WIDTH), F32)],
        scratch=[pltpu.VMEM((CHUNK, N_HEADS * CHUNK), BF16), pltpu.VMEM((N_HEADS * CHUNK, CHUNK), BF16)],
        carried=carried)


def _ssd_bwd(p_ssd, yssd, sprev, dyb, conv_w, conv_b, dt_bias, a_log, dskip_map, norm_w, e_bf, et_bf, n_seq,
             carried=None):
    n_tok = p_ssd.shape[0]
    nc = n_tok // n_seq // CHUNK

    def body(ps3, psprev3, yssd3, sprev3, dyb3,
             cw_ref, cb_ref, dtb_ref, alog_ref, dsk_ref, nw_ref, e_ref, et_ref,
             dps3, dcw_ref, dcb_ref, ddtb_ref, dalog_ref, ddsk_ref, dnw_ref,
             ds3_scr, nxt3_scr, dxa3_scr):
        @pl.when(pl.program_id(0) == 0)
        def _():
            for a in (dcw_ref, dcb_ref, ddtb_ref, dalog_ref, ddsk_ref, dnw_ref, ds3_scr, nxt3_scr):
                a[...] = jnp.zeros_like(a)

        for b in range(n_seq):
            one_sequence(ps3.at[b, :, _XBC_COLS], psprev3.at[b, :, _XBC_COLS], ps3.at[b, :, _Z_COLS],
                         ps3.at[b, :, _DT_COLS], yssd3.at[b], sprev3.at[b], dyb3.at[b],
                         cw_ref, cb_ref, dtb_ref, alog_ref, dsk_ref, nw_ref, e_ref, et_ref,
                         dps3.at[b], dcw_ref, dcb_ref, ddtb_ref, dalog_ref, ddsk_ref, dnw_ref,
                         ds3_scr.at[b], nxt3_scr.at[b], dxa3_scr.at[b])

    def one_sequence(xr_ref, xprev_ref, z_ref, pdt_ref, yssd_ref, sprev_ref, dyb_ref,
                     cw_ref, cb_ref, dtb_ref, alog_ref, dsk_ref, nw_ref, e_ref, et_ref,
                     dps_ref, dcw_ref, dcb_ref, ddtb_ref, dalog_ref, ddsk_ref, dnw_ref,
                     ds_scr, nxt_scr, dxa_scr):
        chunk = nc - 1 - pl.program_id(0)
        xr = xr_ref[...]
        prev = jnp.where(chunk == 0, 0.0, xprev_ref[...])
        et_v = et_ref[...]
        p = _ssd_pre(xr, prev, cw_ref, cb_ref[...], pdt_ref[...], dtb_ref[...], alog_ref[...], e_ref[...])
        last, e_exp, dte, cd = _ssd_maps(p)
        rowi = p["rowi"]
        xs = p["xa"][:, :SSM_WIDTH]
        xd = xs * p["dt_map"]
        a_cs_t = p["a_cs"].T
        tri = _tril_mask()
        dsk = dsk_ref[...]
        nw_v = nw_ref[...]

        yv = yssd_ref[...]
        zv = z_ref[...]
        sz, zg, yg, _, rs = _gate_fwd(yv, zv, nw_v)
        dout = dyb_ref[...]
        for g in range(SSM_GROUPS):
            gs = slice(g * GROUP_W, (g + 1) * GROUP_W)
            dyg_g, dnw_g = _rms_bwd(yg[:, gs], rs[g], nw_v[:, gs], dout[:, gs])
            dnw_ref[:, gs] += dnw_g
            dxa_scr[:, gs] = dyg_g
        dyg = dxa_scr[:, :SSM_WIDTH]
        d_y = dyg * zg
        dps_ref[:, _Z_COLS] = (dyg * yv * (sz + zv * sz * (1.0 - sz))).astype(BF16)

        s_prev = sprev_ref[...]
        ds_next = ds_scr[...]
        lane_dt = lax.broadcasted_iota(jnp.int32, (1, DT_PAD), 1)
        da_cols = jnp.zeros((CHUNK, DT_PAD), F32)
        for g in range(SSM_GROUPS):
            gs = slice(g * GROUP_W, (g + 1) * GROUP_W)
            b_off = SSM_WIDTH + g * SSM_STATE
            c_off = SSM_WIDTH + (SSM_GROUPS + g) * SSM_STATE
            bm = p["xa"][:, b_off:b_off + SSM_STATE].astype(BF16)
            cm = p["xa"][:, c_off:c_off + SSM_STATE].astype(BF16)
            cb_mat = _dot_nt(cm, bm)
            d_yg = d_y[:, gs]
            d_ygb = d_yg.astype(BF16)
            xdg = xd[:, gs]
            xdgb = xdg.astype(BF16)
            ds_g = ds_next[:, gs]
            sp_g = s_prev[:, gs]
            bds = _dot(bm, ds_g)
            dcs = d_yg * e_exp[:, gs]
            d_c = _dot_nt(dcs, sp_g)
            ds_scr[:, gs] = cd[:, gs] * ds_g + _dot_tn(cm, dcs)
            d_b = _dot_nt(xdg * dte[:, gs], ds_g)
            dxd_g = bds * dte[:, gs]
            sum_dcb = jnp.zeros((CHUNK, CHUNK), F32)
            for r in range(SSM_GROUPS * 2):
                head = g * 4 + r
                mask = _head_lane_mask(GROUP_W, r)
                dm = _head_decay(p["a_cs"], a_cs_t, head, tri)
                m_mat = cb_mat * dm
                g_mat = _dot_nt(jnp.where(mask, d_yg, 0.0), xdgb)
                w_mat = g_mat * m_mat
                sum_dcb = sum_dcb + g_mat * dm
                dxd_g = dxd_g + jnp.where(mask, _dot_tn(m_mat, d_ygb), 0.0)
                da_h = jnp.sum(w_mat - w_mat.T, axis=1, keepdims=True)
                da_cols = da_cols + jnp.where(lane_dt == head, da_h, 0.0)
            d_c = d_c + _dot(sum_dcb, bm)
            d_b = d_b + _dot_tn(sum_dcb, cm)
            dxa_scr[:, b_off:b_off + SSM_STATE] = d_b
            dxa_scr[:, c_off:c_off + SSM_STATE] = d_c
            y_off_g = _dot(cm, sp_g) * e_exp[:, gs]
            t3 = bds * xdg * dte[:, gs]
            tail = jnp.sum(t3, axis=0, keepdims=True) + jnp.sum(ds_g * sp_g, axis=0, keepdims=True) * cd[:, gs]
            pre_g = d_yg * y_off_g - t3 + jnp.where(last, tail, 0.0)
            s_pre, ddt_g, s_dsk = _seg_dots([pre_g, dxd_g * xs[:, gs], d_yg * xs[:, gs]], et_v[gs, :])
            da_cols = da_cols + s_pre
            ddsk_ref[...] += jnp.sum(s_dsk, axis=0, keepdims=True)
            dxa_scr[:, gs] = dxd_g * p["dt_map"][:, gs] + dsk[:, gs] * d_yg
            if g == 0:
                ddt = ddt_g
            else:
                ddt = ddt + ddt_g
        r_i = lax.broadcasted_iota(jnp.int32, (CHUNK, CHUNK), 0)
        c_i = lax.broadcasted_iota(jnp.int32, (CHUNK, CHUNK), 1)
        ddta = _tri_dot(r_i <= c_i, da_cols, terms=2)
        ddt = ddt + ddta * p["a_neg"]
        dalog_ref[...] += jnp.sum(ddta * p["dt"], axis=0, keepdims=True) * p["a_neg"]
        draw = ddt * _sigmoid(p["pre"])
        ddtb_ref[...] += jnp.sum(draw, axis=0, keepdims=True)
        dps_ref[:, _DT_COLS] = draw.astype(BF16)

        xc = p["xc"]
        sg = p["sg"]
        dxc = dxa_scr[...] * (sg + xc * sg * (1.0 - sg))
        dcb_ref[...] += jnp.sum(dxc, axis=0, keepdims=True)
        for k in range(CONV_K):
            dcw_ref[k] += jnp.sum(dxc * p["shifted"][k], axis=0, keepdims=True)
        nxt = nxt_scr[...]
        dxr = cw_ref[3] * dxc
        for s in range(1, CONV_K):
            dxr = dxr + cw_ref[CONV_K - 1 - s] * _shift_up(dxc, nxt, s)
        dps_ref[:, _XBC_COLS] = dxr.astype(BF16)
        nxt_scr[...] = dxc[:SUBLANES, :]

    seq_len = n_tok // n_seq

    def rows(width):
        return pl.BlockSpec((n_seq, CHUNK, width), lambda s: (0, nc - 1 - s, 0))

    tiles = CHUNK // SUBLANES
    prev_rows = pl.BlockSpec((n_seq, SUBLANES, SSD_COLS), lambda s: (0, jnp.maximum((nc - 1 - s) * tiles - 1, 0), 0))

    def whole(shape):
        nd = len(shape)
        return pl.BlockSpec(tuple(shape), lambda s: (0,) * nd)

    def by_seq(a):
        return a.reshape(n_seq, seq_len, a.shape[-1])

    acc_shapes = [(CONV_K, 1, CONV_CH), (1, CONV_CH), (1, DT_PAD), (1, DT_PAD), (1, DT_PAD), (1, SSM_WIDTH)]
    ps3 = by_seq(p_ssd)
    outs = _call_carrying(
        body, carried, name="ssd_bwd", grid=(nc,),
        in_specs=[rows(SSD_COLS), prev_rows, rows(SSM_WIDTH), rows(SSM_WIDTH), rows(SSM_WIDTH)] + _ssd_const_specs(),
        out_specs=[rows(SSD_COLS)] + [whole(s) for s in acc_shapes],
        out_shape=tuple([_sds((n_seq, seq_len, SSD_COLS), BF16)] + [_sds(s, F32) for s in acc_shapes]),
        scratch_shapes=[pltpu.VMEM((n_seq, SSM_STATE, SSM_WIDTH), F32), pltpu.VMEM((n_seq, SUBLANES, CONV_CH), F32),
                        pltpu.VMEM((n_seq, CHUNK, CONV_CH), F32)],
        operands=[ps3, ps3, by_seq(yssd), by_seq(sprev), by_seq(dyb), conv_w, conv_b, dt_bias,
                  a_log, dskip_map, norm_w, e_bf, et_bf])
    return (outs[0].reshape(n_tok, SSD_COLS),) + tuple(outs[1:])


def _inproj_bwd(dp_uv, dp_ssd, x, dx1, w_uv, w_ssd, nw, tm=512, carried=None):
    n_tok = x.shape[0]

    def body(duv_ref, dssd_ref, x_ref, dx1_ref, wuv_ref, wssd_ref, nw_ref, gx_ref, h_ref, dnw_ref):
        dh = _dot_nt(duv_ref[...], wuv_ref[...]) + _dot_nt(dssd_ref[...], wssd_ref[...])
        xv = x_ref[...]
        h, r = _rms_fwd(xv, nw_ref[...])
        dx, dnw = _rms_bwd(xv, r, nw_ref[...], dh)
        gx_ref[...] = dx1_ref[...] + dx
        h_ref[...] = h.astype(BF16)
        dnw_ref[...] += dnw

    return _rows_call("inproj_bwd", body, tm, [dp_uv, dp_ssd, x, dx1], [w_uv, w_ssd, nw],
                      [_sds((n_tok, D_MODEL), F32), _sds((n_tok, D_MODEL), BF16)], [_sds((1, D_MODEL), F32)],
                      carried=carried)


def _const_maps():
    lane = jnp.arange(SSM_WIDTH) // HEAD_DIM
    e_bf = (jnp.arange(DT_PAD)[:, None] == lane[None, :]).astype(BF16)
    return e_bf, e_bf.T


def _pad_lanes(v, width):
    return jnp.pad(v, ((0, 0), (0, width - v.shape[1])))


SHARD_COLS = IN_COLS // N_CHIPS
_UV_END = 2 * GM_WIDTH


def _cols_from_shards(w4, lo, hi):
    pieces = []
    for j in range(N_CHIPS):
        a, b = max(lo, j * SHARD_COLS), min(hi, (j + 1) * SHARD_COLS)
        if a < b:
            pieces.append(w4[j][:, a - j * SHARD_COLS:b - j * SHARD_COLS])
    return pieces[0] if len(pieces) == 1 else jnp.concatenate(pieces, axis=1)


def _shards_from_cols(blocks):
    shards = []
    for j in range(N_CHIPS):
        pieces = []
        for arr, lo, hi in blocks:
            a, b = max(lo, j * SHARD_COLS), min(hi, (j + 1) * SHARD_COLS)
            if a < b:
                pieces.append(arr[:, a - lo:b - lo])
        shards.append(pieces[0] if len(pieces) == 1 else jnp.concatenate(pieces, axis=1))
    return jnp.stack(shards)


def _forward_backward(x, tgt, w_in4, conv_w, small, out_shard, up_shard, down_shard, core, adam_args):
    n_seq, seq_len, _ = x.shape
    n_tok = n_seq * seq_len
    x2 = x.reshape(n_tok, D_MODEL)
    tgt2 = tgt.reshape(n_tok, D_MODEL)
    e_bf, et_bf = _const_maps()

    w_uv = _cols_from_shards(w_in4, 0, _UV_END)
    w_ssd = _pad_lanes(_cols_from_shards(w_in4, _UV_END, IN_COLS), SSD_COLS)

    nw_pre = small["norm_mix_pre"]
    lnw = small["gm_ln_w"].reshape(1, GM_WIDTH)
    lnb = small["gm_ln_b"].reshape(1, GM_WIDTH)
    w_stack = small["gm_w_s"].reshape(N_HEADS * CHUNK, CHUNK)
    w_cat = jnp.transpose(small["gm_w_s"], (1, 0, 2)).reshape(CHUNK, N_HEADS * CHUNK)
    bmap = jnp.repeat(small["gm_b_s"].T, HEAD_DIM, axis=1)
    cw3 = conv_w.reshape(CONV_K, 1, CONV_CH)
    conv_b = small["conv_b"]
    dt_bias = _pad_lanes(small["dt_bias"], DT_PAD)
    a_log = _pad_lanes(small["a_log"], DT_PAD)
    dskip_map = jnp.repeat(small["d_skip"], HEAD_DIM, axis=1)
    ssm_nw = small["ssm_norm_w"]

    half = down_shard.shape[0] // 2
    p_uv, p_ssd, w_out4, w_down_a = _inproj_fwd(
        x2, nw_pre, w_uv, w_ssd, carried=_allgather_exchange([out_shard, down_shard[:half]]))
    ssd_consts = (cw3, conv_b, dt_bias, a_log, dskip_map, ssm_nw, e_bf, et_bf)
    w_out_b = w_out4.reshape(D_MODEL, D_MODEL)
    mix, yssd, sprev, o, x1, h2, w_up4, w_down_b = _mixer_fwd(
        p_uv, p_ssd, x2, lnw, lnb, w_cat, bmap, w_out_b, small["norm_mix_post"], small["norm_ffn_pre"],
        *ssd_consts, n_seq, carried=_allgather_exchange([up_shard, down_shard[half:]]))
    f, dd, dy, loss_acc, d_nffn_post = _mlp_fwd(h2, x1, tgt2, w_up4, w_down_a, w_down_b, small["norm_ffn_post"])

    dup, dx1, d_nffn_pre = _mlp_bwd(dd, f, x1, dy, w_down_a, w_down_b, w_up4, small["norm_ffn_pre"])
    tk = min(DW_TOKENS_PER_STEP, n_tok)
    g_up = _matmul_tn("dw_up", h2, dup, D_MODEL, D_MODEL, tk, stacked=True)
    g_down = _matmul_tn("dw_down", f, dd, 1024, D_MODEL, tk).reshape(N_CHIPS, D_FF // N_CHIPS, D_MODEL)
    do, dya, dyb, d_nmix_post, got_up, got_down = _outproj_bwd(
        dx1, o, w_out_b, small["norm_mix_post"], carried=_pair_exchange([g_up, g_down]))
    h_up = _pair_sum(core, g_up, got_up, 512)
    h_down = _pair_sum(core, g_down, got_down, 512)
    g_out = _matmul_tn("dw_out", mix, do, D_MODEL, D_MODEL, tk).reshape(N_CHIPS, D_MODEL // N_CHIPS, D_MODEL)
    dp_uv, d_ws, d_bs_t, d_lnw, d_lnb, slab_up, got_out = _gmlp_bwd(
        p_uv, dya, lnw, lnb, e_bf, et_bf, w_cat, w_stack, bmap,
        carried=_both(_chip_exchange([h_up]), _pair_exchange([g_out])))
    h_out = _pair_sum(core, g_out, got_out, 128)
    early = {
        "gm_ln_w": d_lnw.reshape(N_HEADS, HEAD_DIM), "gm_ln_b": d_lnb.reshape(N_HEADS, HEAD_DIM),
        "gm_w_s": d_ws.reshape(N_HEADS, CHUNK, CHUNK), "gm_b_s": d_bs_t[:, :N_HEADS].T,
        "norm_mix_post": d_nmix_post, "norm_ffn_pre": d_nffn_pre, "norm_ffn_post": d_nffn_post,
    }
    packed_early = _pack(early, tuple(early), tail=loss_acc[0, 0].reshape(1))
    (dp_ssd, d_cw, d_cb, d_dtb, d_alog, d_dsk, d_ssm_nw, slab_down, slab_out, all_early) = _ssd_bwd(
        p_ssd, yssd, sprev, dyb, *ssd_consts, n_seq,
        carried=_both(_chip_exchange([h_down, h_out]), _device_gather_exchange(packed_early)))
    gx, h, d_nmix_pre = _inproj_bwd(dp_uv, dp_ssd, x2, dx1, w_uv, w_ssd, nw_pre)
    late = {
        "norm_mix_pre": d_nmix_pre, "conv_w": d_cw.reshape(CONV_K, CONV_CH), "conv_b": d_cb,
        "dt_bias": d_dtb[:, :N_HEADS], "a_log": d_alog[:, :N_HEADS], "d_skip": d_dsk[:, :N_HEADS],
        "ssm_norm_w": d_ssm_nw,
    }
    g_uv, all_late = _matmul_tn("dw_in_uv", h, dp_uv, D_MODEL, 2 * GM_WIDTH, tk,
                                carried=_device_gather_exchange(_pack(late, tuple(late))))
    sum_early = _ordered_sum("small_sum_early", all_early)
    small_sum = _unpack(sum_early, {n: v.shape for n, v in early.items()}, tuple(early))
    small_sum.update(_unpack(_ordered_sum("small_sum_late", all_late), {n: v.shape for n, v in late.items()}, tuple(late)))
    loss = sum_early.reshape(-1)[sum(v.size for v in early.values())]
    red_up, red_down, red_out = _chip_sum(slab_up, 512), _chip_sum(slab_down, 512), _chip_sum(slab_out, 128)
    g_ssd, oth_up, oth_down, oth_out = _matmul_tn("dw_in_ssd", h, dp_ssd, D_MODEL, SSD_COLS, tk,
                                                  carried=_pair_swap([red_up, red_down, red_out]))
    g_in = _shards_from_cols([(g_uv, 0, _UV_END), (g_ssd, _UV_END, IN_COLS)])
    red_in, oth_in = _reduce_scatter_last(g_in)
    res = _adamw_halves("adamw_mlp", [(adam_args["w_up"][0], red_up, oth_up) + adam_args["w_up"][1:],
                                      (adam_args["w_down"][0], red_down, oth_down) + adam_args["w_down"][1:]], 256)
    big_out = {"w_up": res[0:4], "w_down": res[4:8]}
    big_out["w_out"] = _adamw_halves("adamw_w_out", [(adam_args["w_out"][0], red_out, oth_out) + adam_args["w_out"][1:]], 128)
    big_out["w_in"] = _adamw_halves("adamw_w_in", [(adam_args["w_in"][0], red_in, oth_in) + adam_args["w_in"][1:]], 256)

    return loss, gx.reshape(x.shape), big_out, small_sum


_HBM = pl.BlockSpec(memory_space=pltpu.HBM)


D2D_CHUNKS = 8
ICI_CHUNKS = 1
ROW_ALIGN = 16


def _row_chunks(rows, n_chunks):
    size = min(max(rows // n_chunks, ROW_ALIGN), rows)
    assert rows % size == 0
    return [(start, size) for start in range(0, rows, size)]


def _position():
    x, y, c = lax.axis_index("x"), lax.axis_index("y"), lax.axis_index("c")
    chips = [(1 - x, y), (x, 1 - y), (1 - x, 1 - y)]
    return x, y, c, chips


def _allgather_exchange(arrs):
    n = len(arrs)

    def copies(ins, outs, send_sems, recv_sems, local_sems):
        x, y, c, chips = _position()
        me = 2 * x + y
        sibling = (x, y, 1 - c)

        def copy(a, k, src, dst, to):
            return pltpu.make_async_remote_copy(src_ref=src, dst_ref=dst, send_sem=send_sems.at[a, k],
                                                recv_sem=recv_sems.at[a, k], device_id=to, device_id_type=MESH)

        def half_rows(a, pc):
            half = ins[a].shape[0] // 2
            return pl.ds(pc * half, half)

        local = [pltpu.make_async_copy(ins[a], outs[a].at[me], local_sems.at[a]) for a in range(n)]
        ici_out = [[copy(a, k, ins[a].at[half_rows(a, c)], outs[a].at[me, half_rows(a, c)], (px, py, c))
                    for k, (px, py) in enumerate(chips)] for a in range(n)]
        return c, chips, sibling, copy, half_rows, local, ici_out

    def start(ins, outs, send_sems, recv_sems, local_sems):
        c, chips, _, copy, _, local, _ = copies(ins, outs, send_sems, recv_sems, local_sems)
        x, y, _, _ = _position()
        me = 2 * x + y
        for cp in local:
            cp.start()
        for a in range(n):
            half = ins[a].shape[0] // 2
            for k, (px, py) in enumerate(chips):
                for first, size in _row_chunks(half, ICI_CHUNKS):
                    rows = pl.ds(c * half + first, size)
                    copy(a, k, ins[a].at[rows], outs[a].at[me, rows], (px, py, c)).start()

    def finish(ins, outs, send_sems, recv_sems, local_sems):
        c, chips, sibling, copy, half_rows, local, ici_out = copies(ins, outs, send_sems, recv_sems, local_sems)
        for a in range(n):
            half = ins[a].shape[0] // 2
            for k, (px, py) in enumerate(chips):
                blk = outs[a].at[2 * px + py, half_rows(a, c)]
                copy(a, k, blk, blk, (px, py, c)).wait_recv()
                for first, size in _row_chunks(half, D2D_CHUNKS):
                    piece = outs[a].at[2 * px + py, pl.ds(c * half + first, size)]
                    copy(a, 3 + k, piece, piece, sibling).start()
        for a in range(n):
            for k, (px, py) in enumerate(chips):
                theirs = outs[a].at[2 * px + py, half_rows(a, 1 - c)]
                copy(a, 3 + k, theirs, theirs, sibling).wait_recv()
                mine = outs[a].at[2 * px + py, half_rows(a, c)]
                copy(a, 3 + k, mine, mine, sibling).wait_send()
        for a in range(n):
            for cp in ici_out[a]:
                cp.wait_send()
        for cp in local:
            cp.wait()

    return _Carried(arrs, [_sds((N_CHIPS,) + a.shape, a.dtype) for a in arrs],
                    [pltpu.SemaphoreType.DMA((n, 6)), pltpu.SemaphoreType.DMA((n, 6)), pltpu.SemaphoreType.DMA((n,))],
                    start, finish)


def _run_exchange(name, exchange):
    n_in, n_out = len(exchange.ins), len(exchange.out_shapes)

    def body(*refs):
        ins, outs, sems = refs[:n_in], refs[n_in:n_in + n_out], refs[n_in + n_out:]
        exchange.start(ins, outs, *sems)
        exchange.finish(ins, outs, *sems)

    return pl.pallas_call(
        body, name=name, out_shape=tuple(exchange.out_shapes), in_specs=[_HBM] * n_in,
        out_specs=tuple([_HBM] * n_out), scratch_shapes=exchange.sems,
    )(*exchange.ins)


def _pair_exchange(grads):
    n = len(grads)

    def copier(send_sems, recv_sems):
        x, y, c, _ = _position()

        def copy(a, src, dst):
            return pltpu.make_async_remote_copy(src_ref=src, dst_ref=dst, send_sem=send_sems.at[a],
                                                recv_sem=recv_sems.at[a], device_id=(x, y, 1 - c), device_id_type=MESH)
        return c, copy

    def start(ins, got, send_sems, recv_sems):
        c, copy = copier(send_sems, recv_sems)
        for a in range(n):
            half = ins[a].shape[1] // 2
            for slab in range(N_CHIPS):
                for first, size in _row_chunks(half, D2D_CHUNKS):
                    copy(a, ins[a].at[slab, pl.ds((1 - c) * half + first, size), :],
                         got[a].at[slab, pl.ds(first, size), :]).start()

    def finish(ins, got, send_sems, recv_sems):
        c, copy = copier(send_sems, recv_sems)
        for a in range(n):
            half = ins[a].shape[1] // 2
            copy(a, ins[a].at[:, pl.ds((1 - c) * half, half), :], got[a]).wait()

    return _Carried(grads, [_sds((N_CHIPS, g.shape[1] // 2, g.shape[2]), g.dtype) for g in grads],
                    [pltpu.SemaphoreType.DMA((n,)), pltpu.SemaphoreType.DMA((n,))], start, finish)


def _chip_exchange(hsums):
    n = len(hsums)

    def copies(ins, outs, send_sems, recv_sems, local_sems, pieces):
        x, y, c, chips = _position()
        me = 2 * x + y
        cps = []
        for a in range(n):
            cps.append(pltpu.make_async_copy(ins[a].at[me], outs[a].at[me], local_sems.at[a]))
            rows = ins[a].shape[1]
            for k, (px, py) in enumerate(chips):
                for first, size in (_row_chunks(rows, ICI_CHUNKS) if pieces else [(0, rows)]):
                    cps.append(pltpu.make_async_remote_copy(
                        src_ref=ins[a].at[2 * px + py, pl.ds(first, size)], dst_ref=outs[a].at[me, pl.ds(first, size)],
                        send_sem=send_sems.at[a, k], recv_sem=recv_sems.at[a, k], device_id=(px, py, c),
                        device_id_type=MESH))
        return cps

    def start(*refs):
        for cp in copies(*refs, pieces=True):
            cp.start()

    def finish(*refs):
        for cp in copies(*refs, pieces=False):
            cp.wait()

    return _Carried(hsums, [_sds(h.shape, h.dtype) for h in hsums],
                    [pltpu.SemaphoreType.DMA((n, 3)), pltpu.SemaphoreType.DMA((n, 3)), pltpu.SemaphoreType.DMA((n,))],
                    start, finish)


def _pair_swap(reds):
    n = len(reds)

    def copier(send_sems, recv_sems):
        x, y, c, _ = _position()

        def copy(a, src, dst):
            return pltpu.make_async_remote_copy(src_ref=src, dst_ref=dst, send_sem=send_sems.at[a],
                                                recv_sem=recv_sems.at[a], device_id=(x, y, 1 - c), device_id_type=MESH)
        return copy

    def start(ins, outs, send_sems, recv_sems):
        copy = copier(send_sems, recv_sems)
        for a in range(n):
            for first, size in _row_chunks(ins[a].shape[0], 2 * D2D_CHUNKS):
                copy(a, ins[a].at[pl.ds(first, size), :], outs[a].at[pl.ds(first, size), :]).start()

    def finish(ins, outs, send_sems, recv_sems):
        copy = copier(send_sems, recv_sems)
        for a in range(n):
            copy(a, ins[a], outs[a]).wait()

    return _Carried(reds, [_sds(r.shape, r.dtype) for r in reds],
                    [pltpu.SemaphoreType.DMA((n,)), pltpu.SemaphoreType.DMA((n,))], start, finish)


def _reduce_scatter_last(grad):
    _, rows, cols = grad.shape
    half = rows // 2
    pieces = _row_chunks(half, D2D_CHUNKS)

    def body(g_ref, mine_ref, theirs_ref, got_scr, hsum_scr, slab_scr, pair_sems, ici_send, ici_recv, swap_sems):
        x, y, c, chips = _position()
        me = 2 * x + y
        sibling = (x, y, 1 - c)

        def to_sibling(src, dst, sems):
            return pltpu.make_async_remote_copy(src_ref=src, dst_ref=dst, send_sem=sems.at[0], recv_sem=sems.at[1],
                                                device_id=sibling, device_id_type=MESH)

        for slab in range(N_CHIPS):
            for first, size in pieces:
                to_sibling(g_ref.at[slab, pl.ds((1 - c) * half + first, size)], got_scr.at[slab, pl.ds(first, size)],
                           pair_sems).start()
        to_sibling(g_ref.at[:, pl.ds((1 - c) * half, half)], got_scr, pair_sems).wait()
        own = g_ref[:, pl.ds(pl.multiple_of(c * half, half), half), :]
        hsum_scr[...] = (own.astype(F32) + got_scr[...].astype(F32)).astype(BF16)

        slab_scr[me] = hsum_scr[me]
        ici = [pltpu.make_async_remote_copy(src_ref=hsum_scr.at[2 * px + py], dst_ref=slab_scr.at[me],
                                            send_sem=ici_send.at[k], recv_sem=ici_recv.at[k], device_id=(px, py, c),
                                            device_id_type=MESH) for k, (px, py) in enumerate(chips)]
        for cp in ici:
            cp.start()
        for cp in ici:
            cp.wait()
        acc = slab_scr[0].astype(F32)
        for k in range(1, N_CHIPS):
            acc = acc + slab_scr[k].astype(F32)
        mine_ref[...] = acc

        for first, size in pieces:
            to_sibling(mine_ref.at[pl.ds(first, size)], theirs_ref.at[pl.ds(first, size)], swap_sems).start()
        to_sibling(mine_ref, theirs_ref, swap_sems).wait()

    vmem = pl.BlockSpec(memory_space=pltpu.VMEM)
    halves = (N_CHIPS, half, cols)
    return pl.pallas_call(
        body, name="grad_reduce_scatter_last", out_shape=(_sds((half, cols), F32), _sds((half, cols), F32)),
        in_specs=[vmem], out_specs=(vmem, vmem),
        scratch_shapes=[pltpu.VMEM(halves, BF16), pltpu.VMEM(halves, BF16), pltpu.VMEM(halves, BF16),
                        pltpu.SemaphoreType.DMA((2,)), pltpu.SemaphoreType.DMA((3,)), pltpu.SemaphoreType.DMA((3,)),
                        pltpu.SemaphoreType.DMA((2,))],
        compiler_params=pltpu.CompilerParams(vmem_limit_bytes=VMEM_LIMIT_BYTES),
    )(grad)


def _device_gather_exchange(packed):
    def copies(ins, outs, send_sems, recv_sems, local_sem):
        (x_ref,), (all_ref,) = ins, outs
        x, y, c, chips = _position()
        me, sibling = (x, y, c), (x, y, 1 - c)

        def slab(px, py, pc):
            return all_ref.at[4 * px + 2 * py + pc]

        def copy(k, block, to, src=None):
            return pltpu.make_async_remote_copy(
                src_ref=slab(*block) if src is None else src, dst_ref=slab(*block), send_sem=send_sems.at[k],
                recv_sem=recv_sems.at[k], device_id=to, device_id_type=MESH)

        mine = pltpu.make_async_copy(x_ref, slab(*me), local_sem)
        first = [copy(0, me, sibling, src=x_ref)]
        first += [copy(1 + j, me, (*chip, c), src=x_ref) for j, chip in enumerate(chips)]
        passed = [copy(4 + j, (*chip, c), sibling) for j, chip in enumerate(chips)]
        return c, chips, me, sibling, copy, mine, first, passed

    def start(ins, outs, send_sems, recv_sems, local_sem):
        _, _, _, _, _, mine, first, _ = copies(ins, outs, send_sems, recv_sems, local_sem)
        mine.start()
        for cp in first:
            cp.start()

    def finish(ins, outs, send_sems, recv_sems, local_sem):
        c, chips, me, sibling, copy, mine, first, passed = copies(ins, outs, send_sems, recv_sems, local_sem)
        for j, chip in enumerate(chips):
            copy(1 + j, (*chip, c), me).wait_recv()
            passed[j].start()
        copy(0, sibling, me).wait_recv()
        for j, chip in enumerate(chips):
            copy(4 + j, (*chip, 1 - c), me).wait_recv()
        for cp in first + passed:
            cp.wait_send()
        mine.wait()

    return _Carried([packed], [_sds((N_DEV,) + packed.shape, F32)],
                    [pltpu.SemaphoreType.DMA((7,)), pltpu.SemaphoreType.DMA((7,)), pltpu.SemaphoreType.DMA],
                    start, finish)


def _ordered_sum(name, slabs):
    _, m_per, n_cols = slabs.shape

    def body(s_ref, o_ref):
        acc = s_ref[0]
        for d in range(1, N_DEV):
            acc = acc + s_ref[d]
        o_ref[...] = acc

    vmem = pl.BlockSpec(memory_space=pltpu.VMEM)
    return pl.pallas_call(body, name=name, out_shape=_sds((m_per, n_cols), F32), in_specs=[vmem], out_specs=vmem)(slabs)


def _pair_sum(core, own, got, tm):
    _, half, cols = got.shape
    nb = half // tm

    def body(c_ref, a_ref, b_ref, o_ref):
        o_ref[...] = (a_ref[...].astype(F32) + b_ref[...].astype(F32)).astype(BF16)

    return pl.pallas_call(
        body, name="grad_pair_sum", out_shape=_sds(got.shape, BF16),
        grid_spec=pltpu.PrefetchScalarGridSpec(
            num_scalar_prefetch=1, grid=(N_CHIPS, nb),
            in_specs=[pl.BlockSpec((None, tm, cols), lambda s, i, c_ref: (s, c_ref[0] * nb + i, 0)),
                      pl.BlockSpec((None, tm, cols), lambda s, i, c_ref: (s, i, 0))],
            out_specs=pl.BlockSpec((None, tm, cols), lambda s, i, c_ref: (s, i, 0))),
        compiler_params=_cparams(2),
    )(core, own, got)


def _chip_sum(slabs, tm):
    _, half, cols = slabs.shape

    def body(s_ref, o_ref):
        acc = s_ref[0].astype(F32)
        for k in range(1, N_CHIPS):
            acc = acc + s_ref[k].astype(F32)
        o_ref[...] = acc

    return pl.pallas_call(
        body, name="grad_chip_sum", out_shape=_sds((half, cols), F32), grid=(half // tm,),
        in_specs=[pl.BlockSpec((N_CHIPS, tm, cols), lambda i: (0, i, 0))],
        out_specs=pl.BlockSpec((tm, cols), lambda i: (i, 0)), compiler_params=_cparams(1),
    )(slabs)


def _adam_math(w, g, m, v):
    m2 = ADAM_B1 * m + (1.0 - ADAM_B1) * g
    v2 = ADAM_B2 * v + (1.0 - ADAM_B2) * (g * g)
    m_hat = m2 / (1.0 - ADAM_B1 ** ADAM_STEP)
    v_hat = v2 / (1.0 - ADAM_B2 ** ADAM_STEP)
    delta = -ADAM_LR * (m_hat / (jnp.sqrt(v_hat) + ADAM_EPS) + ADAM_WD * w)
    return delta, m2, v2


def _adamw_halves(name, items, tm, carried=None):
    rows, cols = items[0][0].shape
    nb = rows // 2 // tm
    n = len(items)

    def body(*refs):
        mine = (pl.program_id(0) // nb) == lax.axis_index("c")
        for k in range(n):
            w_ref, own_ref, oth_ref, m_ref, v_ref = refs[5 * k:5 * k + 5]
            g_ref, d_ref, m2_ref, v2_ref = refs[5 * n + 4 * k:5 * n + 4 * k + 4]
            g = jnp.where(mine, own_ref[...], oth_ref[...])
            d, m2, v2 = _adam_math(w_ref[...], g, m_ref[...], v_ref[...])
            g_ref[...] = g
            d_ref[...] = d
            m2_ref[...] = m2
            v2_ref[...] = v2

    full = pl.BlockSpec((tm, cols), lambda i: (i, 0))
    half = pl.BlockSpec((tm, cols), lambda i: (i % nb, 0))
    return _call_carrying(
        body, carried, name=name, grid=(rows // tm,), in_specs=[full, half, half, full, full] * n,
        out_specs=[full] * (4 * n), out_shape=tuple([_sds((rows, cols), F32)] * (4 * n)), scratch_shapes=[],
        operands=[a for item in items for a in item])


def _adamw(name, w, g, m, v, tm):
    def body(w_ref, g_ref, m_ref, v_ref, gout_ref, d_ref, m2_ref, v2_ref):
        gv = g_ref[...]
        d, m2, v2 = _adam_math(w_ref[...], gv, m_ref[...], v_ref[...])
        gout_ref[...] = gv
        d_ref[...] = d
        m2_ref[...] = m2
        v2_ref[...] = v2

    return _rows_call(name, body, tm, [w, g, m, v], [], [_sds(w.shape, F32)] * 4)


_SMALL_NAMES = ("norm_mix_pre", "gm_ln_w", "gm_ln_b", "gm_w_s", "gm_b_s", "conv_w", "conv_b", "dt_bias", "a_log",
                "d_skip", "ssm_norm_w", "norm_mix_post", "norm_ffn_pre", "norm_ffn_post")
_PACK_COLS = 1024


def _pack(parts, names=_SMALL_NAMES, tail=None):
    pieces = [parts[n].reshape(-1) for n in names]
    flat = jnp.concatenate(pieces if tail is None else pieces + [tail])
    rows = -(-flat.shape[0] // (8 * _PACK_COLS)) * 8
    flat = jnp.pad(flat, (0, rows * _PACK_COLS - flat.shape[0]))
    return flat.reshape(rows, _PACK_COLS)


def _unpack(packed, shapes, names=_SMALL_NAMES):
    flat = packed.reshape(-1)
    out, off = {}, 0
    for n in names:
        size = 1
        for s in shapes[n]:
            size *= s
        out[n] = flat[off:off + size].reshape(shapes[n])
        off += size
    return out


def kernel(x, norm_mix_pre, w_in, gm_ln_w, gm_ln_b, gm_w_s, gm_b_s, conv_w, conv_b, dt_bias, a_log, d_skip, ssm_norm_w, w_out, norm_mix_post, norm_ffn_pre, w_up, w_down, norm_ffn_post, loss_target, m_norm_mix_pre, m_w_in, m_gm_ln_w, m_gm_ln_b, m_gm_w_s, m_gm_b_s, m_conv_w, m_conv_b, m_dt_bias, m_a_log, m_d_skip, m_ssm_norm_w, m_w_out, m_norm_mix_post, m_norm_ffn_pre, m_w_up, m_w_down, m_norm_ffn_post, v_norm_mix_pre, v_w_in, v_gm_ln_w, v_gm_ln_b, v_gm_w_s, v_gm_b_s, v_conv_w, v_conv_b, v_dt_bias, v_a_log, v_d_skip, v_ssm_norm_w, v_w_out, v_norm_mix_post, v_norm_ffn_pre, v_w_up, v_w_down, v_norm_ffn_post):
    params = dict(norm_mix_pre=norm_mix_pre, w_in=w_in, gm_ln_w=gm_ln_w, gm_ln_b=gm_ln_b, gm_w_s=gm_w_s, gm_b_s=gm_b_s,
                  conv_w=conv_w, conv_b=conv_b, dt_bias=dt_bias, a_log=a_log, d_skip=d_skip, ssm_norm_w=ssm_norm_w,
                  w_out=w_out, norm_mix_post=norm_mix_post, norm_ffn_pre=norm_ffn_pre, w_up=w_up, w_down=w_down,
                  norm_ffn_post=norm_ffn_post)
    mom1 = dict(norm_mix_pre=m_norm_mix_pre, w_in=m_w_in, gm_ln_w=m_gm_ln_w, gm_ln_b=m_gm_ln_b, gm_w_s=m_gm_w_s,
                gm_b_s=m_gm_b_s, conv_w=m_conv_w, conv_b=m_conv_b, dt_bias=m_dt_bias, a_log=m_a_log, d_skip=m_d_skip,
                ssm_norm_w=m_ssm_norm_w, w_out=m_w_out, norm_mix_post=m_norm_mix_post, norm_ffn_pre=m_norm_ffn_pre,
                w_up=m_w_up, w_down=m_w_down, norm_ffn_post=m_norm_ffn_post)
    mom2 = dict(norm_mix_pre=v_norm_mix_pre, w_in=v_w_in, gm_ln_w=v_gm_ln_w, gm_ln_b=v_gm_ln_b, gm_w_s=v_gm_w_s,
                gm_b_s=v_gm_b_s, conv_w=v_conv_w, conv_b=v_conv_b, dt_bias=v_dt_bias, a_log=v_a_log, d_skip=v_d_skip,
                ssm_norm_w=v_ssm_norm_w, w_out=v_w_out, norm_mix_post=v_norm_mix_post, norm_ffn_pre=v_norm_ffn_pre,
                w_up=v_w_up, w_down=v_w_down, norm_ffn_post=v_norm_ffn_post)
    names = list(params)
    big = ("w_in", "w_out", "w_up", "w_down")
    chip = 2 * lax.axis_index("x") + lax.axis_index("y")

    shards = {n: params[n][0].astype(BF16) for n in big}
    conv_shard = jnp.pad(conv_w[0], ((0, 16 - CONV_K), (0, 0)))
    g_in4, g_conv4 = _run_exchange("allgather_w_in", _allgather_exchange([shards["w_in"], conv_shard]))
    conv_full = jnp.transpose(g_conv4[:, :CONV_K, :], (1, 0, 2)).reshape(CONV_K, CONV_CH)

    small = {n: params[n][0] if params[n].ndim >= 3 else params[n] for n in _SMALL_NAMES if n != "conv_w"}
    core = lax.axis_index("c").astype(jnp.int32).reshape(1)
    adam_args = {n: (params[n][0], mom1[n][0], mom2[n][0]) for n in big}
    loss, grad_x, big_out, small_sum = _forward_backward(
        x, loss_target, g_in4, conv_full, small, shards["w_out"], shards["w_up"], shards["w_down"], core, adam_args)
    grads, delta, new_m, new_v = {}, {}, {}, {}
    for n in big:
        grads[n], delta[n], new_m[n], new_v[n] = [a[None] for a in big_out[n]]

    small_sum["conv_w"] = lax.dynamic_slice_in_dim(small_sum["conv_w"], chip * (CONV_CH // N_CHIPS), CONV_CH // N_CHIPS, axis=1)

    local_shapes = {n: params[n].shape[1:] if params[n].ndim >= 3 else params[n].shape for n in _SMALL_NAMES}
    flat = lambda tree: {n: tree[n].reshape(local_shapes[n]) for n in _SMALL_NAMES}
    packed = [_pack(flat(t)) for t in (params, small_sum, mom1, mom2)]
    _, d_p, m_p, v_p = _adamw("adamw_small", *packed, packed[0].shape[0])
    for src, dst in ((d_p, delta), (m_p, new_m), (v_p, new_v)):
        for n, val in _unpack(src, local_shapes).items():
            dst[n] = val.reshape(params[n].shape)
    for n in _SMALL_NAMES:
        grads[n] = small_sum[n].reshape(params[n].shape)

    out = [loss, grad_x]
    for tree in (grads, delta, new_m, new_v):
        out += [tree[n] for n in names]
    return tuple(out)
```

```python
import functools

import jax
import jax.numpy as jnp
from jax import lax
from jax.experimental import pallas as pl
from jax.experimental.pallas import tpu as pltpu

F32 = jnp.float32
BF16 = jnp.bfloat16
MESH = pl.DeviceIdType.MESH

EPS = 1e-6
D_MODEL = 1024
GM_WIDTH = 512
SSM_WIDTH = 512
N_HEADS = 8
HEAD_DIM = 64
CHUNK = 128
SSM_GROUPS = 2
GROUP_W = SSM_WIDTH // SSM_GROUPS
SSM_STATE = 128
CONV_K = 4
CONV_CH = 1024
D_FF = 4096
IN_COLS = 2568
DT_PAD = 128
SSD_COLS = SSM_WIDTH + CONV_CH + DT_PAD
N_CHIPS = 4
N_DEV = 8

ADAM_LR = 0.001
ADAM_B1 = 0.9
ADAM_B2 = 0.999
ADAM_EPS = 1e-08
ADAM_WD = 0.01
ADAM_STEP = 10

VMEM_LIMIT_BYTES = 56 * 1024 * 1024
FF_TILE = 512
DW_TOKENS_PER_STEP = 2048


def _cparams(n_axes):
    return pltpu.CompilerParams(dimension_semantics=("arbitrary",) * n_axes, vmem_limit_bytes=VMEM_LIMIT_BYTES)


def _dot(a, b):
    return jnp.dot(a.astype(BF16), b.astype(BF16), preferred_element_type=F32)


def _dot_nt(a, b):
    return lax.dot_general(a.astype(BF16), b.astype(BF16), (((1,), (1,)), ((), ())), preferred_element_type=F32)


def _dot_tn(a, b):
    return lax.dot_general(a.astype(BF16), b.astype(BF16), (((0,), (0,)), ((), ())), preferred_element_type=F32)


def _sigmoid(x):
    return 1.0 / (1.0 + jnp.exp(-x))


_GELU_C = 0.7978845608028654
_GELU_A = 0.044715


def _gelu(x):
    t = jnp.tanh(_GELU_C * (x + _GELU_A * (x * x * x)))
    return 0.5 * x * (1.0 + t), t


def _gelu_grad(x, t):
    return 0.5 * (1.0 + t) + 0.5 * x * (1.0 - t * t) * (_GELU_C * (1.0 + 3.0 * _GELU_A * x * x))


def _rms_fwd(x, w):
    r = lax.rsqrt(jnp.mean(x * x, axis=-1, keepdims=True) + EPS)
    return x * r * w, r


def _rms_bwd(x, r, w, dy):
    g = dy * w
    dx = r * g - x * (r * r * r) * jnp.mean(g * x, axis=-1, keepdims=True)
    dw = jnp.sum(dy * x * r, axis=0, keepdims=True)
    return dx, dw


class _Carried:
    def __init__(self, ins, out_shapes, sems, start, finish):
        self.ins, self.out_shapes, self.sems = list(ins), list(out_shapes), list(sems)
        self.start, self.finish = start, finish


def _both(first, second):
    n_i, n_o, n_s = len(first.ins), len(first.out_shapes), len(first.sems)

    def split(ins, outs, sems):
        return (ins[:n_i], outs[:n_o], sems[:n_s]), (ins[n_i:], outs[n_o:], sems[n_s:])

    def start(ins, outs, *sems):
        (i1, o1, s1), (i2, o2, s2) = split(ins, outs, sems)
        first.start(i1, o1, *s1)
        second.start(i2, o2, *s2)

    def finish(ins, outs, *sems):
        (i1, o1, s1), (i2, o2, s2) = split(ins, outs, sems)
        first.finish(i1, o1, *s1)
        second.finish(i2, o2, *s2)

    return _Carried(first.ins + second.ins, first.out_shapes + second.out_shapes, first.sems + second.sems, start, finish)


def _split_carried(refs, n_in, n_out, n_scratch, carried):
    n_ci, n_co, n_cs = len(carried.ins), len(carried.out_shapes), len(carried.sems)
    ins, rest = refs[:n_in], refs[n_in:]
    c_ins, rest = rest[:n_ci], rest[n_ci:]
    outs, rest = rest[:n_out], rest[n_out:]
    c_outs, rest = rest[:n_co], rest[n_co:]
    scr, c_sems = rest[:n_scratch], rest[n_scratch:]
    assert len(c_sems) == n_cs
    return tuple(ins) + tuple(outs) + tuple(scr), c_ins, c_outs, c_sems


def _rows_call(name, body, tm, row_ins, const_ins, row_outs, acc_outs=(), scratch=(), carried=None):
    n_rows = row_ins[0].shape[0]
    assert n_rows % tm == 0
    n_steps = n_rows // tm
    n_in = len(row_ins) + len(const_ins)
    n_ro = len(row_outs)
    n_acc = len(acc_outs)

    def kern(*refs):
        accs = refs[n_in + n_ro:n_in + n_ro + n_acc]

        @pl.when(pl.program_id(0) == 0)
        def _():
            for a in accs:
                a[...] = jnp.zeros_like(a)

        body(*refs)

    def whole(shape):
        nd = len(shape)
        return pl.BlockSpec(tuple(shape), lambda i: (0,) * nd)

    in_specs = [pl.BlockSpec((tm, a.shape[1]), lambda i: (i, 0)) for a in row_ins]
    in_specs += [whole(a.shape) for a in const_ins]
    out_specs = [pl.BlockSpec((tm, s.shape[1]), lambda i: (i, 0)) for s in row_outs]
    out_specs += [whole(s.shape) for s in acc_outs]
    return _call_carrying(
        kern, carried, name=name, grid=(n_steps,), in_specs=in_specs, out_specs=out_specs,
        out_shape=tuple(row_outs) + tuple(acc_outs), scratch_shapes=list(scratch), operands=list(row_ins) + list(const_ins))


def _call_carrying(body, carried, *, name, grid, in_specs, out_specs, out_shape, scratch_shapes, operands):
    n_in, n_out, n_scratch = len(in_specs), len(out_specs), len(scratch_shapes)
    kern = body
    if carried is not None:
        def kern(*refs):
            plain, c_ins, c_outs, c_sems = _split_carried(refs, n_in, n_out, n_scratch, carried)
            first, last = True, True
            for d, size in enumerate(grid):
                first = jnp.logical_and(first, pl.program_id(d) == 0)
                last = jnp.logical_and(last, pl.program_id(d) == size - 1)

            @pl.when(first)
            def _():
                carried.start(c_ins, c_outs, *c_sems)

            body(*plain)

            @pl.when(last)
            def _():
                carried.finish(c_ins, c_outs, *c_sems)

        in_specs = list(in_specs) + [_HBM] * len(carried.ins)
        out_specs = list(out_specs) + [_HBM] * len(carried.out_shapes)
        out_shape = tuple(out_shape) + tuple(carried.out_shapes)
        operands = list(operands) + carried.ins
        scratch_shapes = list(scratch_shapes) + carried.sems
    return pl.pallas_call(
        kern, name=name, grid=grid, in_specs=in_specs, out_specs=out_specs, out_shape=out_shape,
        scratch_shapes=scratch_shapes, compiler_params=_cparams(len(grid)),
    )(*operands)


def _sds(shape, dtype):
    return jax.ShapeDtypeStruct(tuple(shape), dtype)


def _matmul_tn(name, a, b, tm, tn, tk, stacked=False, carried=None):
    k_dim, m_dim = a.shape
    n_dim = b.shape[1]
    assert m_dim % tm == 0 and n_dim % tn == 0 and k_dim % tk == 0
    nk = k_dim // tk

    def kern(a_ref, b_ref, o_ref, acc_ref):
        k = pl.program_id(2)
        prod = _dot_tn(a_ref[...], b_ref[...])

        @pl.when(k == 0)
        def _():
            acc_ref[...] = prod

        @pl.when(k > 0)
        def _():
            acc_ref[...] += prod

        @pl.when(k == nk - 1)
        def _():
            o_ref[...] = acc_ref[...].astype(o_ref.dtype)

    if stacked:
        assert tm == m_dim
        out_shape = _sds((n_dim // tn, m_dim, tn), BF16)
        out_spec = pl.BlockSpec((None, tm, tn), lambda i, j, k: (j, i, 0))
    else:
        out_shape = _sds((m_dim, n_dim), BF16)
        out_spec = pl.BlockSpec((tm, tn), lambda i, j, k: (i, j))
    outs = _call_carrying(
        kern, carried, name=name, grid=(m_dim // tm, n_dim // tn, nk),
        in_specs=[pl.BlockSpec((tk, tm), lambda i, j, k: (k, i)), pl.BlockSpec((tk, tn), lambda i, j, k: (k, j))],
        out_specs=[out_spec], out_shape=(out_shape,), scratch_shapes=[pltpu.VMEM((tm, tn), F32)], operands=[a, b])
    return outs[0] if carried is None else outs


def _inproj_fwd(x, nw, w_uv, w_ssd, tm=256, carried=None):
    n_tok = x.shape[0]

    def body(x_ref, nw_ref, wuv_ref, wssd_ref, puv_ref, pssd_ref, h_ref):
        h, _ = _rms_fwd(x_ref[...], nw_ref[...])
        h = h.astype(BF16)
        puv_ref[...] = jnp.dot(h, wuv_ref[...], preferred_element_type=F32)
        pssd_ref[...] = jnp.dot(h, wssd_ref[...], preferred_element_type=F32)
        h_ref[...] = h

    return _rows_call("inproj_fwd", body, tm, [x], [nw, w_uv, w_ssd],
                      [_sds((n_tok, 2 * GM_WIDTH), F32), _sds((n_tok, SSD_COLS), F32), _sds((n_tok, D_MODEL), BF16)],
                      carried=carried)


_Z_COLS = pl.ds(0, SSM_WIDTH)
_XBC_COLS = pl.ds(SSM_WIDTH, CONV_CH)
_DT_COLS = pl.ds(SSM_WIDTH + CONV_CH, DT_PAD)


def _head_lane_mask(width, head):
    lane = lax.broadcasted_iota(jnp.int32, (1, width), 1)
    return (lane // HEAD_DIM) == head


def _split_terms(x, terms):
    parts = []
    for _ in range(terms):
        p = x.astype(BF16)
        parts.append(p)
        x = x - p.astype(F32)
    return parts


def _seg_dots(vals, ind, terms=2):
    m = vals[0].shape[0]
    parts = []
    for v in vals:
        parts += _split_terms(v, terms)
    red = jnp.dot(jnp.concatenate(parts, axis=0), ind, preferred_element_type=F32)
    outs = []
    for i in range(len(vals)):
        acc = red[i * terms * m:(i * terms + 1) * m]
        for t in range(1, terms):
            acc = acc + red[(i * terms + t) * m:(i * terms + t + 1) * m]
        outs.append(acc)
    return outs


def _tri_dot(mask, x, terms=3):
    n = x.shape[1]
    red = jnp.dot(mask.astype(BF16), jnp.concatenate(_split_terms(x, terms), axis=1), preferred_element_type=F32)
    acc = red[:, :n]
    for t in range(1, terms):
        acc = acc + red[:, t * n:(t + 1) * n]
    return acc


def _gmlp_common(puv, lnw, lnb, e_bf, et_bf):
    u = puv[:, :GM_WIDTH]
    v = puv[:, GM_WIDTH:]
    gu, tu = _gelu(u)
    gv, tv = _gelu(v)
    (s1,) = _seg_dots([gv], et_bf)
    (mu,) = _seg_dots([s1 * (1.0 / HEAD_DIM)], e_bf)
    xc = gv - mu
    (s2,) = _seg_dots([xc * xc], et_bf)
    (rstd,) = _seg_dots([lax.rsqrt(s2 * (1.0 / HEAD_DIM) + EPS)], e_bf)
    xhat = xc * rstd
    vn = xhat * lnw + lnb
    return u, v, gu, tu, tv, rstd, xhat, vn


def _tril_mask():
    r = lax.broadcasted_iota(jnp.int32, (CHUNK, CHUNK), 0)
    c = lax.broadcasted_iota(jnp.int32, (CHUNK, CHUNK), 1)
    return r >= c


def _head_blocks(v):
    return jnp.concatenate([jnp.where(_head_lane_mask(GM_WIDTH, h), v, jnp.zeros_like(v)) for h in range(N_HEADS)], axis=0)


def _causal_w_cat(w_cat):
    t = lax.broadcasted_iota(jnp.int32, (CHUNK, N_HEADS * CHUNK), 0)
    s = lax.broadcasted_iota(jnp.int32, (CHUNK, N_HEADS * CHUNK), 1) % CHUNK
    return jnp.where(t >= s, w_cat, 0.0).astype(BF16)


def _gmlp_chunk_fwd(puv, lnw, lnb, e_bf, et_bf, wm, bmap):
    _, _, gu, _, _, _, _, vn = _gmlp_common(puv, lnw, lnb, e_bf, et_bf)
    mixed = jnp.dot(wm, _head_blocks(vn.astype(BF16)), preferred_element_type=F32) + bmap
    return (gu * mixed).astype(BF16)


SUBLANES = 8


def _shift_down(x, tail, s):
    main = pltpu.roll(x, s, 0)
    row = lax.broadcasted_iota(jnp.int32, (SUBLANES, 1), 0)
    head = jnp.where(row < s, pltpu.roll(tail, s, 0), main[:SUBLANES])
    return jnp.concatenate([head, main[SUBLANES:]], axis=0)


def _shift_up(x, head_next, s):
    n = x.shape[0]
    main = pltpu.roll(x, n - s, 0)
    row = lax.broadcasted_iota(jnp.int32, (SUBLANES, 1), 0)
    last = jnp.where(row >= SUBLANES - s, pltpu.roll(head_next, SUBLANES - s, 0), main[n - SUBLANES:])
    return jnp.concatenate([main[:n - SUBLANES], last], axis=0)


def _ssd_pre(xr, tail, cw_ref, cb, pdt, dtb, alog, emap):
    rowi = lax.broadcasted_iota(jnp.int32, (CHUNK, 1), 0)
    shifted = [_shift_down(xr, tail, 3), _shift_down(xr, tail, 2), _shift_down(xr, tail, 1), xr]
    xc = cb
    for k in range(CONV_K):
        xc = xc + cw_ref[k] * shifted[k]
    sg = _sigmoid(xc)
    xa = xc * sg
    pre = pdt + dtb
    dt = jnp.maximum(pre, 0.0) + jnp.log(1.0 + jnp.exp(-jnp.abs(pre)))
    a_neg = -jnp.exp(alog)
    a_cs = _tri_dot(_tril_mask(), dt * a_neg)
    acs_map, dt_map = _seg_dots([a_cs, dt], emap, terms=3)
    return dict(shifted=shifted, xc=xc, sg=sg, xa=xa, pre=pre, dt=dt, a_neg=a_neg, a_cs=a_cs,
                acs_map=acs_map, dt_map=dt_map, rowi=rowi)


def _ssd_maps(p):
    last = p["rowi"] == CHUNK - 1
    aq_map = jnp.sum(jnp.where(last, p["acs_map"], 0.0), axis=0, keepdims=True)
    e_exp = jnp.exp(p["acs_map"])
    dte = jnp.exp(aq_map - p["acs_map"])
    cd = jnp.exp(aq_map)
    return last, e_exp, dte, cd


def _head_decay(a_cs, a_cs_t, head, tri):
    lane = lax.broadcasted_iota(jnp.int32, (1, DT_PAD), 1)
    sub = lax.broadcasted_iota(jnp.int32, (DT_PAD, 1), 0)
    col = jnp.sum(jnp.where(lane == head, a_cs, 0.0), axis=1, keepdims=True)
    row = jnp.sum(jnp.where(sub == head, a_cs_t, 0.0), axis=0, keepdims=True)
    return jnp.exp(jnp.where(tri, col - row, -1e30))


def _gate_fwd(y, z, nw):
    sz = _sigmoid(z)
    zg = z * sz
    yg = y * zg
    outs, rs = [], []
    for g in range(SSM_GROUPS):
        gs = slice(g * GROUP_W, (g + 1) * GROUP_W)
        o, r = _rms_fwd(yg[:, gs], nw[:, gs])
        outs.append(o)
        rs.append(r)
    return sz, zg, yg, outs, rs


def _ssd_const_specs():
    def whole(shape):
        nd = len(shape)
        return pl.BlockSpec(tuple(shape), lambda c: (0,) * nd)
    return [whole((CONV_K, 1, CONV_CH)), whole((1, CONV_CH)), whole((1, DT_PAD)), whole((1, DT_PAD)),
            whole((1, SSM_WIDTH)), whole((1, SSM_WIDTH)), whole((DT_PAD, SSM_WIDTH)), whole((SSM_WIDTH, DT_PAD))]


def _mixer_fwd(p_uv, p_ssd, x, lnw, lnb, w_cat, bmap, w_out, nw_post, nw_pre2, conv_w, conv_b, dt_bias, a_log,
               dskip_map, norm_w, e_bf, et_bf, n_seq, carried=None):
    n_tok = p_uv.shape[0]
    nc = n_tok // n_seq // CHUNK

    def body(puv3, ps3, x3, lnw_ref, lnb_ref, wcat_ref, bmap_ref, wo_ref, nwa_ref, nwb_ref,
             cw_ref, cb_ref, dtb_ref, alog_ref, dsk_ref, nw_ref, e_ref, et_ref,
             mix3, yssd3, sprev3, o3, x13, h23, wm_scr, prev3_scr, s3_scr):
        @pl.when(pl.program_id(0) == 0)
        def _():
            wm_scr[...] = _causal_w_cat(wcat_ref[...])
            prev3_scr[...] = jnp.zeros_like(prev3_scr)
            s3_scr[...] = jnp.zeros_like(s3_scr)

        for b in range(n_seq):
            one_sequence(puv3.at[b], ps3.at[b, :, _XBC_COLS], ps3.at[b, :, _Z_COLS], ps3.at[b, :, _DT_COLS],
                         lnw_ref, lnb_ref, bmap_ref, cw_ref, cb_ref, dtb_ref, alog_ref, dsk_ref, nw_ref, e_ref, et_ref,
                         mix3.at[b], yssd3.at[b], sprev3.at[b], wm_scr, prev3_scr.at[b], s3_scr.at[b])
            o = jnp.dot(mix3[b], wo_ref[...], preferred_element_type=F32)
            on, _ = _rms_fwd(o, nwa_ref[...])
            x1 = x3[b] + on
            h2, _ = _rms_fwd(x1, nwb_ref[...])
            o3[b] = o
            x13[b] = x1
            h23[b] = h2.astype(BF16)

    def one_sequence(puv_ref, xr_ref, z_ref, pdt_ref, lnw_ref, lnb_ref, bmap_ref,
                     cw_ref, cb_ref, dtb_ref, alog_ref, dsk_ref, nw_ref, e_ref, et_ref,
                     mix_ref, yssd_ref, sprev_ref, wm_scr, prev_scr, s_scr):
        mix_ref[:, :GM_WIDTH] = _gmlp_chunk_fwd(puv_ref[...], lnw_ref[...], lnb_ref[...], e_ref[...], et_ref[...], wm_scr[...],
                                      bmap_ref[...])
        xr = xr_ref[...]
        p = _ssd_pre(xr, prev_scr[...], cw_ref, cb_ref[...], pdt_ref[...], dtb_ref[...], alog_ref[...], e_ref[...])
        _, e_exp, dte, cd = _ssd_maps(p)
        xs = p["xa"][:, :SSM_WIDTH]
        xd = xs * p["dt_map"]
        a_cs_t = p["a_cs"].T
        tri = _tril_mask()
        s_old = s_scr[...]
        sprev_ref[...] = s_old
        for g in range(SSM_GROUPS):
            gs = slice(g * GROUP_W, (g + 1) * GROUP_W)
            bm = p["xa"][:, SSM_WIDTH + g * SSM_STATE: SSM_WIDTH + (g + 1) * SSM_STATE].astype(BF16)
            cm = p["xa"][:, SSM_WIDTH + (SSM_GROUPS + g) * SSM_STATE: SSM_WIDTH + (SSM_GROUPS + g + 1) * SSM_STATE].astype(BF16)
            cb_mat = _dot_nt(cm, bm)
            xdg = xd[:, gs].astype(BF16)
            y_g = _dot(cm, s_old[:, gs]) * e_exp[:, gs] + dsk_ref[:, gs] * xs[:, gs]
            for r in range(SSM_GROUPS * 2):
                dm = _head_decay(p["a_cs"], a_cs_t, g * 4 + r, tri)
                full = jnp.dot((cb_mat * dm).astype(BF16), xdg, preferred_element_type=F32)
                y_g = y_g + jnp.where(_head_lane_mask(GROUP_W, r), full, 0.0)
            yssd_ref[:, gs] = y_g
            s_scr[:, gs] = cd[:, gs] * s_old[:, gs] + _dot_tn(bm, xd[:, gs] * dte[:, gs])
        _, _, _, outs, _ = _gate_fwd(yssd_ref[...], z_ref[...], nw_ref[...])
        for g in range(SSM_GROUPS):
            mix_ref[:, GM_WIDTH + g * GROUP_W:GM_WIDTH + (g + 1) * GROUP_W] = outs[g].astype(BF16)
        prev_scr[...] = xr[CHUNK - SUBLANES:, :]

    seq_len = n_tok // n_seq

    def rows(width):
        return pl.BlockSpec((n_seq, CHUNK, width), lambda c: (0, c, 0))

    def whole(shape):
        nd = len(shape)
        return pl.BlockSpec(tuple(shape), lambda c: (0,) * nd)

    def by_seq(a):
        return a.reshape(n_seq, seq_len, a.shape[-1])

    outs = _call_carrying(
        body, carried, name="mixer_fwd", grid=(nc,),
        in_specs=[rows(2 * GM_WIDTH), rows(SSD_COLS), rows(D_MODEL), whole(lnw.shape),
                  whole(lnb.shape), whole(w_cat.shape), whole(bmap.shape), whole(w_out.shape), whole(nw_post.shape),
                  whole(nw_pre2.shape)] + _ssd_const_specs(),
        out_specs=[rows(D_MODEL), rows(SSM_WIDTH), rows(SSM_WIDTH), rows(D_MODEL), rows(D_MODEL), rows(D_MODEL)],
        out_shape=(_sds((n_seq, seq_len, D_MODEL), BF16),
                   _sds((n_seq, seq_len, SSM_WIDTH), F32), _sds((n_seq, seq_len, SSM_WIDTH), F32),
                   _sds((n_seq, seq_len, D_MODEL), F32), _sds((n_seq, seq_len, D_MODEL), F32),
                   _sds((n_seq, seq_len, D_MODEL), BF16)),
        scratch_shapes=[pltpu.VMEM((CHUNK, N_HEADS * CHUNK), BF16), pltpu.VMEM((n_seq, SUBLANES, CONV_CH), F32),
                        pltpu.VMEM((n_seq, SSM_STATE, SSM_WIDTH), F32)],
        operands=[by_seq(p_uv), by_seq(p_ssd), by_seq(x), lnw, lnb, w_cat, bmap, w_out, nw_post,
                  nw_pre2, conv_w, conv_b, dt_bias, a_log, dskip_map, norm_w, e_bf, et_bf])
    return tuple(o.reshape(n_tok, o.shape[-1]) for o in outs[:6]) + tuple(outs[6:])


def _up_cols(wup_ref, j):
    per = (D_FF // N_CHIPS) // FF_TILE
    return wup_ref[j // per, :, (j % per) * FF_TILE:(j % per + 1) * FF_TILE]


def _down_rows(wda_ref, wdb_ref, j):
    assert 2 * FF_TILE == D_FF // N_CHIPS
    return (wda_ref if j % 2 == 0 else wdb_ref)[j // 2]


def _skewed_rows_call(name, main, tail, tm, lead_ins, lag_ins, const_ins, lead_outs, lag_outs, acc_outs, carry,
                      streamed, tile_copies, n_copies):
    n_rows = lead_ins[0].shape[0]
    assert n_rows % tm == 0
    n = n_rows // tm
    counts = [len(lead_ins), len(lag_ins), len(const_ins), len(streamed), len(lead_outs), len(lag_outs), len(acc_outs),
              1, len(streamed)]

    def kern(*refs):
        groups, pos = [], 0
        for cnt in counts:
            groups.append(refs[pos:pos + cnt])
            pos += cnt
        lead_i, lag_i, consts, w_hbm, lead_o, lag_o, accs, (carry_scr,), w_vmem = groups
        sems = refs[pos]
        i = pl.program_id(0)
        pieces, k = [], 0
        for piece in tile_copies(w_hbm, w_vmem):
            pieces.append([pltpu.make_async_copy(src, dst, sems.at[k + q]) for q, (src, dst) in enumerate(piece)])
            k += len(piece)

        def ready(j):
            for cp in pieces[j]:
                cp.wait()

        @pl.when(i == 0)
        def _():
            for piece in pieces:
                for cp in piece:
                    cp.start()
            for a in accs:
                a[...] = jnp.zeros_like(a)
            carry_scr[...] = main(lead_i, consts, lead_o, w_vmem, ready)

        @pl.when(jnp.logical_and(i > 0, i < n))
        def _():
            previous = carry_scr[...]
            carry_scr[...] = main(lead_i, consts, lead_o, w_vmem, lambda j: None)
            tail(previous, lag_i, consts, lag_o, accs)

        @pl.when(i == n)
        def _():
            tail(carry_scr[...], lag_i, consts, lag_o, accs)

    def lead(width):
        return pl.BlockSpec((tm, width), lambda i: (jnp.minimum(i, n - 1), 0))

    def lag(width):
        return pl.BlockSpec((tm, width), lambda i: (jnp.maximum(i - 1, 0), 0))

    def whole(shape):
        nd = len(shape)
        return pl.BlockSpec(tuple(shape), lambda i: (0,) * nd)

    return pl.pallas_call(
        kern, name=name, grid=(n + 1,),
        in_specs=([lead(a.shape[1]) for a in lead_ins] + [lag(a.shape[1]) for a in lag_ins]
                  + [whole(a.shape) for a in const_ins] + [_HBM] * len(streamed)),
        out_specs=[lead(s.shape[1]) for s in lead_outs] + [lag(s.shape[1]) for s in lag_outs] + [whole(s.shape) for s in acc_outs],
        out_shape=tuple(lead_outs) + tuple(lag_outs) + tuple(acc_outs),
        scratch_shapes=([pltpu.VMEM(carry, F32)] + [pltpu.VMEM(a.shape, a.dtype) for a in streamed]
                        + [pltpu.SemaphoreType.DMA((n_copies,))]),
        compiler_params=_cparams(1),
    )(*lead_ins, *lag_ins, *const_ins, *streamed)


def _mlp_weight_pieces(order):
    per = (D_FF // N_CHIPS) // FF_TILE

    def tile_copies(hbm, vmem):
        pieces = []
        for j in range(D_FF // FF_TILE):
            cols = (j // per, slice(None), pl.ds((j % per) * FF_TILE, FF_TILE))
            up = (hbm[0].at[cols], vmem[0].at[cols])
            down = (hbm[1 + j % 2].at[j // 2], vmem[1 + j % 2].at[j // 2])
            pieces.append([up, down] if order == "up_down" else [down, up])
        return pieces

    return tile_copies


def _mlp_fwd(h2, x1, tgt, w_up, w_down_a, w_down_b, nw, tm=512):
    n_tok = x1.shape[0]

    def main(lead_i, consts, lead_o, weights, ready):
        (h2_ref,), (f_ref,), (wup_ref, wda_ref, wdb_ref) = lead_i, lead_o, weights
        h2v = h2_ref[...]
        acc = jnp.zeros((tm, D_MODEL), F32)
        for j in range(D_FF // FF_TILE):
            cs = slice(j * FF_TILE, (j + 1) * FF_TILE)
            ready(j)
            u = jnp.dot(h2v, _up_cols(wup_ref, j), preferred_element_type=F32)
            f = jnp.square(jnp.maximum(u, 0.0)).astype(BF16)
            f_ref[:, cs] = f
            acc = acc + jnp.dot(f, _down_rows(wda_ref, wdb_ref, j), preferred_element_type=F32)
        return acc

    def tail(acc, lag_i, consts, lag_o, accs):
        (x1_ref, tgt_ref), (nw_ref,), (dd_ref, dy_ref), (loss_ref, dnw_ref) = lag_i, consts, lag_o, accs
        dn, r = _rms_fwd(acc, nw_ref[...])
        e = x1_ref[...] + dn - tgt_ref[...]
        loss_ref[...] += jnp.full(loss_ref.shape, (0.5 / D_MODEL) * jnp.sum(e * e), F32)
        dy = e * (1.0 / D_MODEL)
        dd, dnw = _rms_bwd(acc, r, nw_ref[...], dy)
        dy_ref[...] = dy
        dd_ref[...] = dd.astype(BF16)
        dnw_ref[...] += dnw

    return _skewed_rows_call(
        "mlp_fwd", main, tail, tm, [h2], [x1, tgt], [nw],
        [_sds((n_tok, D_FF), BF16)], [_sds((n_tok, D_MODEL), BF16), _sds((n_tok, D_MODEL), F32)],
        [_sds((8, 128), F32), _sds((1, D_MODEL), F32)], carry=(tm, D_MODEL),
        streamed=[w_up, w_down_a, w_down_b], tile_copies=_mlp_weight_pieces("up_down"), n_copies=2 * (D_FF // FF_TILE))


def _mlp_bwd(dd, f, x1, dy, w_down_a, w_down_b, w_up, nw, tm=256):
    n_tok = x1.shape[0]

    def main(lead_i, consts, lead_o, weights, ready):
        (dd_ref, f_ref), (dup_ref,), (wup_ref, wda_ref, wdb_ref) = lead_i, lead_o, weights
        ddv = dd_ref[...]
        acc = jnp.zeros((tm, D_MODEL), F32)
        for j in range(D_FF // FF_TILE):
            cs = slice(j * FF_TILE, (j + 1) * FF_TILE)
            ready(j)
            df = _dot_nt(ddv, _down_rows(wda_ref, wdb_ref, j))
            du = (df * (2.0 * jnp.sqrt(f_ref[:, cs].astype(F32)))).astype(BF16)
            dup_ref[:, cs] = du
            acc = acc + _dot_nt(du, _up_cols(wup_ref, j))
        return acc

    def tail(acc, lag_i, consts, lag_o, accs):
        (x1_ref, dy_ref), (nw_ref,), (dx1_ref,), (dnw_ref,) = lag_i, consts, lag_o, accs
        x1v = x1_ref[...]
        _, r = _rms_fwd(x1v, nw_ref[...])
        dx, dnw = _rms_bwd(x1v, r, nw_ref[...], acc)
        dx1_ref[...] = dy_ref[...] + dx
        dnw_ref[...] += dnw

    return _skewed_rows_call(
        "mlp_bwd", main, tail, tm, [dd, f], [x1, dy], [nw],
        [_sds((n_tok, D_FF), BF16)], [_sds((n_tok, D_MODEL), F32)], [_sds((1, D_MODEL), F32)], carry=(tm, D_MODEL),
        streamed=[w_up, w_down_a, w_down_b], tile_copies=_mlp_weight_pieces("down_up"), n_copies=2 * (D_FF // FF_TILE))


def _outproj_bwd(dx1, o, w_out, nw, tm=256, carried=None):
    n_tok = dx1.shape[0]

    def body(dx1_ref, o_ref, wo_ref, nw_ref, do_ref, dya_ref, dyb_ref, dnw_ref):
        ov = o_ref[...]
        _, r = _rms_fwd(ov, nw_ref[...])
        do, dnw = _rms_bwd(ov, r, nw_ref[...], dx1_ref[...])
        dob = do.astype(BF16)
        do_ref[...] = dob
        dya_ref[...] = _dot_nt(dob, wo_ref[:GM_WIDTH, :])
        dyb_ref[...] = _dot_nt(dob, wo_ref[GM_WIDTH:, :])
        dnw_ref[...] += dnw

    return _rows_call("outproj_bwd", body, tm, [dx1, o], [w_out, nw],
                      [_sds((n_tok, D_MODEL), BF16), _sds((n_tok, GM_WIDTH), F32), _sds((n_tok, SSM_WIDTH), F32)],
                      [_sds((1, D_MODEL), F32)], carried=carried)


def _gmlp_bwd(p_uv, dya, lnw, lnb, e_bf, et_bf, w_cat, w_stack, bmap, carried=None):
    n_tok = p_uv.shape[0]
    chunks_per_step = 2

    def body(puv_ref, dya_ref, lnw_ref, lnb_ref, e_ref, et_ref, wcat_ref, wstack_ref, bmap_ref,
             dpuv_ref, dws_ref, dbs_ref, dlnw_ref, dlnb_ref, wm_scr, wsm_scr):
        t_stk = lax.broadcasted_iota(jnp.int32, (N_HEADS * CHUNK, CHUNK), 0) % CHUNK
        s_stk = lax.broadcasted_iota(jnp.int32, (N_HEADS * CHUNK, CHUNK), 1)

        @pl.when(pl.program_id(0) == 0)
        def _():
            wm_scr[...] = _causal_w_cat(wcat_ref[...])
            wsm_scr[...] = jnp.where(t_stk >= s_stk, wstack_ref[...], 0.0).astype(BF16)

        lnw_v = lnw_ref[...]
        e_v, et_v = e_ref[...], et_ref[...]

        def one_chunk(rows):
            u, v, gu, tu, tv, rstd, xhat, vn = _gmlp_common(puv_ref[rows, :], lnw_v, lnb_ref[...], e_v, et_v)
            vnb = vn.astype(BF16)
            mixed = jnp.dot(wm_scr[...], _head_blocks(vnb), preferred_element_type=F32) + bmap_ref[...]
            dy = dya_ref[rows, :]
            du = dy * mixed * _gelu_grad(u, tu)
            dmixed = dy * gu
            (dbs,) = _seg_dots([dmixed], et_v)
            dblocks = _head_blocks(dmixed.astype(BF16))
            dvn = lax.dot_general(wsm_scr[...], dblocks, (((0,), (0,)), ((), ())), preferred_element_type=F32)
            dws = lax.dot_general(dblocks, vnb, (((1,), (1,)), ((), ())), preferred_element_type=F32)
            dxh = dvn * lnw_v
            m1, m2 = _seg_dots([dxh, dxh * xhat], et_v)
            m1, m2 = _seg_dots([m1 * (1.0 / HEAD_DIM), m2 * (1.0 / HEAD_DIM)], e_v)
            dgv = rstd * (dxh - m1 - xhat * m2)
            dv = dgv * _gelu_grad(v, tv)
            dpuv_ref[rows, :GM_WIDTH] = du.astype(BF16)
            dpuv_ref[rows, GM_WIDTH:] = dv.astype(BF16)
            return dbs, dws, jnp.sum(dvn * xhat, axis=0, keepdims=True), jnp.sum(dvn, axis=0, keepdims=True)

        parts = [one_chunk(slice(k * CHUNK, (k + 1) * CHUNK)) for k in range(chunks_per_step)]
        dbs, dws, dlnw, dlnb = [functools.reduce(lambda a, b: a + b, vals) for vals in zip(*parts)]
        dbs_ref[...] += dbs
        dws_ref[...] += jnp.where(t_stk >= s_stk, dws, 0.0)
        dlnw_ref[...] += dlnw
        dlnb_ref[...] += dlnb

    return _rows_call(
        "gmlp_bwd", body, chunks_per_step * CHUNK, [p_uv, dya], [lnw, lnb, e_bf, et_bf, w_cat, w_stack, bmap],
        [_sds((n_tok, 2 * GM_WIDTH), BF16)],
        [_sds((N_HEADS * CHUNK, CHUNK), F32), _sds((CHUNK, DT_PAD), F32), _sds((1, GM_WIDTH), F32),
         _sds((1, GM_WIDTH), F32)],
        scratch=[pltpu.VMEM((CHUNK, N_HEADS * CHUNK), BF16), pltpu.VMEM((N_HEADS * CHUNK, CHUNK), BF16)],
        carried=carried)


def _ssd_bwd(p_ssd, yssd, sprev, dyb, conv_w, conv_b, dt_bias, a_log, dskip_map, norm_w, e_bf, et_bf, n_seq,
             carried=None):
    n_tok = p_ssd.shape[0]
    nc = n_tok // n_seq // CHUNK

    def body(ps3, psprev3, yssd3, sprev3, dyb3,
             cw_ref, cb_ref, dtb_ref, alog_ref, dsk_ref, nw_ref, e_ref, et_ref,
             dps3, dcw_ref, dcb_ref, ddtb_ref, dalog_ref, ddsk_ref, dnw_ref,
             ds3_scr, nxt3_scr, dxa3_scr):
        @pl.when(pl.program_id(0) == 0)
        def _():
            for a in (dcw_ref, dcb_ref, ddtb_ref, dalog_ref, ddsk_ref, dnw_ref, ds3_scr, nxt3_scr):
                a[...] = jnp.zeros_like(a)

        for b in range(n_seq):
            one_sequence(ps3.at[b, :, _XBC_COLS], psprev3.at[b, :, _XBC_COLS], ps3.at[b, :, _Z_COLS],
                         ps3.at[b, :, _DT_COLS], yssd3.at[b], sprev3.at[b], dyb3.at[b],
                         cw_ref, cb_ref, dtb_ref, alog_ref, dsk_ref, nw_ref, e_ref, et_ref,
                         dps3.at[b], dcw_ref, dcb_ref, ddtb_ref, dalog_ref, ddsk_ref, dnw_ref,
                         ds3_scr.at[b], nxt3_scr.at[b], dxa3_scr.at[b])

    def one_sequence(xr_ref, xprev_ref, z_ref, pdt_ref, yssd_ref, sprev_ref, dyb_ref,
                     cw_ref, cb_ref, dtb_ref, alog_ref, dsk_ref, nw_ref, e_ref, et_ref,
                     dps_ref, dcw_ref, dcb_ref, ddtb_ref, dalog_ref, ddsk_ref, dnw_ref,
                     ds_scr, nxt_scr, dxa_scr):
        chunk = nc - 1 - pl.program_id(0)
        xr = xr_ref[...]
        prev = jnp.where(chunk == 0, 0.0, xprev_ref[...])
        et_v = et_ref[...]
        p = _ssd_pre(xr, prev, cw_ref, cb_ref[...], pdt_ref[...], dtb_ref[...], alog_ref[...], e_ref[...])
        last, e_exp, dte, cd = _ssd_maps(p)
        rowi = p["rowi"]
        xs = p["xa"][:, :SSM_WIDTH]
        xd = xs * p["dt_map"]
        a_cs_t = p["a_cs"].T
        tri = _tril_mask()
        dsk = dsk_ref[...]
        nw_v = nw_ref[...]

        yv = yssd_ref[...]
        zv = z_ref[...]
        sz, zg, yg, _, rs = _gate_fwd(yv, zv, nw_v)
        dout = dyb_ref[...]
        for g in range(SSM_GROUPS):
            gs = slice(g * GROUP_W, (g + 1) * GROUP_W)
            dyg_g, dnw_g = _rms_bwd(yg[:, gs], rs[g], nw_v[:, gs], dout[:, gs])
            dnw_ref[:, gs] += dnw_g
            dxa_scr[:, gs] = dyg_g
        dyg = dxa_scr[:, :SSM_WIDTH]
        d_y = dyg * zg
        dps_ref[:, _Z_COLS] = (dyg * yv * (sz + zv * sz * (1.0 - sz))).astype(BF16)

        s_prev = sprev_ref[...]
        ds_next = ds_scr[...]
        lane_dt = lax.broadcasted_iota(jnp.int32, (1, DT_PAD), 1)
        da_cols = jnp.zeros((CHUNK, DT_PAD), F32)
        for g in range(SSM_GROUPS):
            gs = slice(g * GROUP_W, (g + 1) * GROUP_W)
            b_off = SSM_WIDTH + g * SSM_STATE
            c_off = SSM_WIDTH + (SSM_GROUPS + g) * SSM_STATE
            bm = p["xa"][:, b_off:b_off + SSM_STATE].astype(BF16)
            cm = p["xa"][:, c_off:c_off + SSM_STATE].astype(BF16)
            cb_mat = _dot_nt(cm, bm)
            d_yg = d_y[:, gs]
            d_ygb = d_yg.astype(BF16)
            xdg = xd[:, gs]
            xdgb = xdg.astype(BF16)
            ds_g = ds_next[:, gs]
            sp_g = s_prev[:, gs]
            bds = _dot(bm, ds_g)
            dcs = d_yg * e_exp[:, gs]
            d_c = _dot_nt(dcs, sp_g)
            ds_scr[:, gs] = cd[:, gs] * ds_g + _dot_tn(cm, dcs)
            d_b = _dot_nt(xdg * dte[:, gs], ds_g)
            dxd_g = bds * dte[:, gs]
            sum_dcb = jnp.zeros((CHUNK, CHUNK), F32)
            for r in range(SSM_GROUPS * 2):
                head = g * 4 + r
                mask = _head_lane_mask(GROUP_W, r)
                dm = _head_decay(p["a_cs"], a_cs_t, head, tri)
                m_mat = cb_mat * dm
                g_mat = _dot_nt(jnp.where(mask, d_yg, 0.0), xdgb)
                w_mat = g_mat * m_mat
                sum_dcb = sum_dcb + g_mat * dm
                dxd_g = dxd_g + jnp.where(mask, _dot_tn(m_mat, d_ygb), 0.0)
                da_h = jnp.sum(w_mat - w_mat.T, axis=1, keepdims=True)
                da_cols = da_cols + jnp.where(lane_dt == head, da_h, 0.0)
            d_c = d_c + _dot(sum_dcb, bm)
            d_b = d_b + _dot_tn(sum_dcb, cm)
            dxa_scr[:, b_off:b_off + SSM_STATE] = d_b
            dxa_scr[:, c_off:c_off + SSM_STATE] = d_c
            y_off_g = _dot(cm, sp_g) * e_exp[:, gs]
            t3 = bds * xdg * dte[:, gs]
            tail = jnp.sum(t3, axis=0, keepdims=True) + jnp.sum(ds_g * sp_g, axis=0, keepdims=True) * cd[:, gs]
            pre_g = d_yg * y_off_g - t3 + jnp.where(last, tail, 0.0)
            s_pre, ddt_g, s_dsk = _seg_dots([pre_g, dxd_g * xs[:, gs], d_yg * xs[:, gs]], et_v[gs, :])
            da_cols = da_cols + s_pre
            ddsk_ref[...] += jnp.sum(s_dsk, axis=0, keepdims=True)
            dxa_scr[:, gs] = dxd_g * p["dt_map"][:, gs] + dsk[:, gs] * d_yg
            if g == 0:
                ddt = ddt_g
            else:
                ddt = ddt + ddt_g
        r_i = lax.broadcasted_iota(jnp.int32, (CHUNK, CHUNK), 0)
        c_i = lax.broadcasted_iota(jnp.int32, (CHUNK, CHUNK), 1)
        ddta = _tri_dot(r_i <= c_i, da_cols, terms=2)
        ddt = ddt + ddta * p["a_neg"]
        dalog_ref[...] += jnp.sum(ddta * p["dt"], axis=0, keepdims=True) * p["a_neg"]
        draw = ddt * _sigmoid(p["pre"])
        ddtb_ref[...] += jnp.sum(draw, axis=0, keepdims=True)
        dps_ref[:, _DT_COLS] = draw.astype(BF16)

        xc = p["xc"]
        sg = p["sg"]
        dxc = dxa_scr[...] * (sg + xc * sg * (1.0 - sg))
        dcb_ref[...] += jnp.sum(dxc, axis=0, keepdims=True)
        for k in range(CONV_K):
            dcw_ref[k] += jnp.sum(dxc * p["shifted"][k], axis=0, keepdims=True)
        nxt = nxt_scr[...]
        dxr = cw_ref[3] * dxc
        for s in range(1, CONV_K):
            dxr = dxr + cw_ref[CONV_K - 1 - s] * _shift_up(dxc, nxt, s)
        dps_ref[:, _XBC_COLS] = dxr.astype(BF16)
        nxt_scr[...] = dxc[:SUBLANES, :]

    seq_len = n_tok // n_seq

    def rows(width):
        return pl.BlockSpec((n_seq, CHUNK, width), lambda s: (0, nc - 1 - s, 0))

    tiles = CHUNK // SUBLANES
    prev_rows = pl.BlockSpec((n_seq, SUBLANES, SSD_COLS), lambda s: (0, jnp.maximum((nc - 1 - s) * tiles - 1, 0), 0))

    def whole(shape):
        nd = len(shape)
        return pl.BlockSpec(tuple(shape), lambda s: (0,) * nd)

    def by_seq(a):
        return a.reshape(n_seq, seq_len, a.shape[-1])

    acc_shapes = [(CONV_K, 1, CONV_CH), (1, CONV_CH), (1, DT_PAD), (1, DT_PAD), (1, DT_PAD), (1, SSM_WIDTH)]
    ps3 = by_seq(p_ssd)
    outs = _call_carrying(
        body, carried, name="ssd_bwd", grid=(nc,),
        in_specs=[rows(SSD_COLS), prev_rows, rows(SSM_WIDTH), rows(SSM_WIDTH), rows(SSM_WIDTH)] + _ssd_const_specs(),
        out_specs=[rows(SSD_COLS)] + [whole(s) for s in acc_shapes],
        out_shape=tuple([_sds((n_seq, seq_len, SSD_COLS), BF16)] + [_sds(s, F32) for s in acc_shapes]),
        scratch_shapes=[pltpu.VMEM((n_seq, SSM_STATE, SSM_WIDTH), F32), pltpu.VMEM((n_seq, SUBLANES, CONV_CH), F32),
                        pltpu.VMEM((n_seq, CHUNK, CONV_CH), F32)],
        operands=[ps3, ps3, by_seq(yssd), by_seq(sprev), by_seq(dyb), conv_w, conv_b, dt_bias,
                  a_log, dskip_map, norm_w, e_bf, et_bf])
    return (outs[0].reshape(n_tok, SSD_COLS),) + tuple(outs[1:])


def _inproj_bwd(dp_uv, dp_ssd, x, dx1, w_uv, w_ssd, nw, tm=512, carried=None):
    n_tok = x.shape[0]

    def body(duv_ref, dssd_ref, x_ref, dx1_ref, wuv_ref, wssd_ref, nw_ref, gx_ref, dnw_ref):
        dh = _dot_nt(duv_ref[...], wuv_ref[...]) + _dot_nt(dssd_ref[...], wssd_ref[...])
        xv = x_ref[...]
        _, r = _rms_fwd(xv, nw_ref[...])
        dx, dnw = _rms_bwd(xv, r, nw_ref[...], dh)
        gx_ref[...] = dx1_ref[...] + dx
        dnw_ref[...] += dnw

    return _rows_call("inproj_bwd", body, tm, [dp_uv, dp_ssd, x, dx1], [w_uv, w_ssd, nw],
                      [_sds((n_tok, D_MODEL), F32)], [_sds((1, D_MODEL), F32)], carried=carried)


def _const_maps():
    lane = jnp.arange(SSM_WIDTH) // HEAD_DIM
    e_bf = (jnp.arange(DT_PAD)[:, None] == lane[None, :]).astype(BF16)
    return e_bf, e_bf.T


def _pad_lanes(v, width):
    return jnp.pad(v, ((0, 0), (0, width - v.shape[1])))


SHARD_COLS = IN_COLS // N_CHIPS
_UV_END = 2 * GM_WIDTH


def _cols_from_shards(w4, lo, hi):
    pieces = []
    for j in range(N_CHIPS):
        a, b = max(lo, j * SHARD_COLS), min(hi, (j + 1) * SHARD_COLS)
        if a < b:
            pieces.append(w4[j][:, a - j * SHARD_COLS:b - j * SHARD_COLS])
    return pieces[0] if len(pieces) == 1 else jnp.concatenate(pieces, axis=1)


def _shards_from_cols(blocks):
    shards = []
    for j in range(N_CHIPS):
        pieces = []
        for arr, lo, hi in blocks:
            a, b = max(lo, j * SHARD_COLS), min(hi, (j + 1) * SHARD_COLS)
            if a < b:
                pieces.append(arr[:, a - lo:b - lo])
        shards.append(pieces[0] if len(pieces) == 1 else jnp.concatenate(pieces, axis=1))
    return jnp.stack(shards)


def _forward_backward(x, tgt, w_in4, conv_w, small, out_shard, up_shard, down_shard, core, adam_args):
    n_seq, seq_len, _ = x.shape
    n_tok = n_seq * seq_len
    x2 = x.reshape(n_tok, D_MODEL)
    tgt2 = tgt.reshape(n_tok, D_MODEL)
    e_bf, et_bf = _const_maps()

    w_uv = _cols_from_shards(w_in4, 0, _UV_END)
    w_ssd = _pad_lanes(_cols_from_shards(w_in4, _UV_END, IN_COLS), SSD_COLS)

    nw_pre = small["norm_mix_pre"]
    lnw = small["gm_ln_w"].reshape(1, GM_WIDTH)
    lnb = small["gm_ln_b"].reshape(1, GM_WIDTH)
    w_stack = small["gm_w_s"].reshape(N_HEADS * CHUNK, CHUNK)
    w_cat = jnp.transpose(small["gm_w_s"], (1, 0, 2)).reshape(CHUNK, N_HEADS * CHUNK)
    bmap = jnp.repeat(small["gm_b_s"].T, HEAD_DIM, axis=1)
    cw3 = conv_w.reshape(CONV_K, 1, CONV_CH)
    conv_b = small["conv_b"]
    dt_bias = _pad_lanes(small["dt_bias"], DT_PAD)
    a_log = _pad_lanes(small["a_log"], DT_PAD)
    dskip_map = jnp.repeat(small["d_skip"], HEAD_DIM, axis=1)
    ssm_nw = small["ssm_norm_w"]

    half = down_shard.shape[0] // 2
    p_uv, p_ssd, h, w_out4, w_down_a = _inproj_fwd(
        x2, nw_pre, w_uv, w_ssd, carried=_allgather_exchange([out_shard, down_shard[:half]]))
    ssd_consts = (cw3, conv_b, dt_bias, a_log, dskip_map, ssm_nw, e_bf, et_bf)
    w_out_b = w_out4.reshape(D_MODEL, D_MODEL)
    mix, yssd, sprev, o, x1, h2, w_up4, w_down_b = _mixer_fwd(
        p_uv, p_ssd, x2, lnw, lnb, w_cat, bmap, w_out_b, small["norm_mix_post"], small["norm_ffn_pre"],
        *ssd_consts, n_seq, carried=_allgather_exchange([up_shard, down_shard[half:]]))
    f, dd, dy, loss_acc, d_nffn_post = _mlp_fwd(h2, x1, tgt2, w_up4, w_down_a, w_down_b, small["norm_ffn_post"])

    dup, dx1, d_nffn_pre = _mlp_bwd(dd, f, x1, dy, w_down_a, w_down_b, w_up4, small["norm_ffn_pre"])
    tk = min(DW_TOKENS_PER_STEP, n_tok)
    g_up = _matmul_tn("dw_up", h2, dup, D_MODEL, D_MODEL, tk, stacked=True)
    g_down = _matmul_tn("dw_down", f, dd, 1024, D_MODEL, tk).reshape(N_CHIPS, D_FF // N_CHIPS, D_MODEL)
    do, dya, dyb, d_nmix_post, got_up, got_down = _outproj_bwd(
        dx1, o, w_out_b, small["norm_mix_post"], carried=_pair_exchange([g_up, g_down]))
    h_up = _pair_sum(core, g_up, got_up, 512)
    h_down = _pair_sum(core, g_down, got_down, 512)
    g_out = _matmul_tn("dw_out", mix, do, D_MODEL, D_MODEL, tk).reshape(N_CHIPS, D_MODEL // N_CHIPS, D_MODEL)
    dp_uv, d_ws, d_bs_t, d_lnw, d_lnb, slab_up, got_out = _gmlp_bwd(
        p_uv, dya, lnw, lnb, e_bf, et_bf, w_cat, w_stack, bmap,
        carried=_both(_chip_exchange([h_up]), _pair_exchange([g_out])))
    h_out = _pair_sum(core, g_out, got_out, 128)
    early = {
        "gm_ln_w": d_lnw.reshape(N_HEADS, HEAD_DIM), "gm_ln_b": d_lnb.reshape(N_HEADS, HEAD_DIM),
        "gm_w_s": d_ws.reshape(N_HEADS, CHUNK, CHUNK), "gm_b_s": d_bs_t[:, :N_HEADS].T,
        "norm_mix_post": d_nmix_post, "norm_ffn_pre": d_nffn_pre, "norm_ffn_post": d_nffn_post,
    }
    packed_early = _pack(early, tuple(early), tail=loss_acc[0, 0].reshape(1))
    (dp_ssd, d_cw, d_cb, d_dtb, d_alog, d_dsk, d_ssm_nw, slab_down, slab_out, all_early) = _ssd_bwd(
        p_ssd, yssd, sprev, dyb, *ssd_consts, n_seq,
        carried=_both(_chip_exchange([h_down, h_out]), _device_gather_exchange(packed_early)))
    gx, d_nmix_pre = _inproj_bwd(dp_uv, dp_ssd, x2, dx1, w_uv, w_ssd, nw_pre)
    late = {
        "norm_mix_pre": d_nmix_pre, "conv_w": d_cw.reshape(CONV_K, CONV_CH), "conv_b": d_cb,
        "dt_bias": d_dtb[:, :N_HEADS], "a_log": d_alog[:, :N_HEADS], "d_skip": d_dsk[:, :N_HEADS],
        "ssm_norm_w": d_ssm_nw,
    }
    g_uv, all_late = _matmul_tn("dw_in_uv", h, dp_uv, D_MODEL, 2 * GM_WIDTH, tk,
                                carried=_device_gather_exchange(_pack(late, tuple(late))))
    sum_early = _ordered_sum("small_sum_early", all_early)
    small_sum = _unpack(sum_early, {n: v.shape for n, v in early.items()}, tuple(early))
    small_sum.update(_unpack(_ordered_sum("small_sum_late", all_late), {n: v.shape for n, v in late.items()}, tuple(late)))
    loss = sum_early.reshape(-1)[sum(v.size for v in early.values())]
    red_up, red_down, red_out = _chip_sum(slab_up, 512), _chip_sum(slab_down, 512), _chip_sum(slab_out, 128)
    g_ssd, oth_up, oth_down, oth_out = _matmul_tn("dw_in_ssd", h, dp_ssd, D_MODEL, SSD_COLS, tk,
                                                  carried=_pair_swap([red_up, red_down, red_out]))
    g_in = _shards_from_cols([(g_uv, 0, _UV_END), (g_ssd, _UV_END, IN_COLS)])
    red_in, oth_in = _reduce_scatter_last(g_in)
    res = _adamw_halves("adamw_mlp", [(adam_args["w_up"][0], red_up, oth_up) + adam_args["w_up"][1:],
                                      (adam_args["w_down"][0], red_down, oth_down) + adam_args["w_down"][1:]], 256)
    big_out = {"w_up": res[0:4], "w_down": res[4:8]}
    big_out["w_out"] = _adamw_halves("adamw_w_out", [(adam_args["w_out"][0], red_out, oth_out) + adam_args["w_out"][1:]], 128)
    big_out["w_in"] = _adamw_halves("adamw_w_in", [(adam_args["w_in"][0], red_in, oth_in) + adam_args["w_in"][1:]], 256)

    return loss, gx.reshape(x.shape), big_out, small_sum


_HBM = pl.BlockSpec(memory_space=pltpu.HBM)


D2D_CHUNKS = 8
ICI_CHUNKS = 1
ROW_ALIGN = 16


def _row_chunks(rows, n_chunks):
    size = min(max(rows // n_chunks, ROW_ALIGN), rows)
    assert rows % size == 0
    return [(start, size) for start in range(0, rows, size)]


def _position():
    x, y, c = lax.axis_index("x"), lax.axis_index("y"), lax.axis_index("c")
    chips = [(1 - x, y), (x, 1 - y), (1 - x, 1 - y)]
    return x, y, c, chips


def _allgather_exchange(arrs):
    n = len(arrs)

    def copies(ins, outs, send_sems, recv_sems, local_sems):
        x, y, c, chips = _position()
        me = 2 * x + y
        sibling = (x, y, 1 - c)

        def copy(a, k, src, dst, to):
            return pltpu.make_async_remote_copy(src_ref=src, dst_ref=dst, send_sem=send_sems.at[a, k],
                                                recv_sem=recv_sems.at[a, k], device_id=to, device_id_type=MESH)

        def half_rows(a, pc):
            half = ins[a].shape[0] // 2
            return pl.ds(pc * half, half)

        local = [pltpu.make_async_copy(ins[a], outs[a].at[me], local_sems.at[a]) for a in range(n)]
        ici_out = [[copy(a, k, ins[a].at[half_rows(a, c)], outs[a].at[me, half_rows(a, c)], (px, py, c))
                    for k, (px, py) in enumerate(chips)] for a in range(n)]
        return c, chips, sibling, copy, half_rows, local, ici_out

    def start(ins, outs, send_sems, recv_sems, local_sems):
        c, chips, _, copy, _, local, _ = copies(ins, outs, send_sems, recv_sems, local_sems)
        x, y, _, _ = _position()
        me = 2 * x + y
        for cp in local:
            cp.start()
        for a in range(n):
            half = ins[a].shape[0] // 2
            for k, (px, py) in enumerate(chips):
                for first, size in _row_chunks(half, ICI_CHUNKS):
                    rows = pl.ds(c * half + first, size)
                    copy(a, k, ins[a].at[rows], outs[a].at[me, rows], (px, py, c)).start()

    def finish(ins, outs, send_sems, recv_sems, local_sems):
        c, chips, sibling, copy, half_rows, local, ici_out = copies(ins, outs, send_sems, recv_sems, local_sems)
        for a in range(n):
            half = ins[a].shape[0] // 2
            for k, (px, py) in enumerate(chips):
                blk = outs[a].at[2 * px + py, half_rows(a, c)]
                copy(a, k, blk, blk, (px, py, c)).wait_recv()
                for first, size in _row_chunks(half, D2D_CHUNKS):
                    piece = outs[a].at[2 * px + py, pl.ds(c * half + first, size)]
                    copy(a, 3 + k, piece, piece, sibling).start()
        for a in range(n):
            for k, (px, py) in enumerate(chips):
                theirs = outs[a].at[2 * px + py, half_rows(a, 1 - c)]
                copy(a, 3 + k, theirs, theirs, sibling).wait_recv()
                mine = outs[a].at[2 * px + py, half_rows(a, c)]
                copy(a, 3 + k, mine, mine, sibling).wait_send()
        for a in range(n):
            for cp in ici_out[a]:
                cp.wait_send()
        for cp in local:
            cp.wait()

    return _Carried(arrs, [_sds((N_CHIPS,) + a.shape, a.dtype) for a in arrs],
                    [pltpu.SemaphoreType.DMA((n, 6)), pltpu.SemaphoreType.DMA((n, 6)), pltpu.SemaphoreType.DMA((n,))],
                    start, finish)


def _run_exchange(name, exchange):
    n_in, n_out = len(exchange.ins), len(exchange.out_shapes)

    def body(*refs):
        ins, outs, sems = refs[:n_in], refs[n_in:n_in + n_out], refs[n_in + n_out:]
        exchange.start(ins, outs, *sems)
        exchange.finish(ins, outs, *sems)

    return pl.pallas_call(
        body, name=name, out_shape=tuple(exchange.out_shapes), in_specs=[_HBM] * n_in,
        out_specs=tuple([_HBM] * n_out), scratch_shapes=exchange.sems,
    )(*exchange.ins)


def _pair_exchange(grads):
    n = len(grads)

    def copier(send_sems, recv_sems):
        x, y, c, _ = _position()

        def copy(a, src, dst):
            return pltpu.make_async_remote_copy(src_ref=src, dst_ref=dst, send_sem=send_sems.at[a],
                                                recv_sem=recv_sems.at[a], device_id=(x, y, 1 - c), device_id_type=MESH)
        return c, copy

    def start(ins, got, send_sems, recv_sems):
        c, copy = copier(send_sems, recv_sems)
        for a in range(n):
            half = ins[a].shape[1] // 2
            for slab in range(N_CHIPS):
                for first, size in _row_chunks(half, D2D_CHUNKS):
                    copy(a, ins[a].at[slab, pl.ds((1 - c) * half + first, size), :],
                         got[a].at[slab, pl.ds(first, size), :]).start()

    def finish(ins, got, send_sems, recv_sems):
        c, copy = copier(send_sems, recv_sems)
        for a in range(n):
            half = ins[a].shape[1] // 2
            copy(a, ins[a].at[:, pl.ds((1 - c) * half, half), :], got[a]).wait()

    return _Carried(grads, [_sds((N_CHIPS, g.shape[1] // 2, g.shape[2]), g.dtype) for g in grads],
                    [pltpu.SemaphoreType.DMA((n,)), pltpu.SemaphoreType.DMA((n,))], start, finish)


def _chip_exchange(hsums):
    n = len(hsums)

    def copies(ins, outs, send_sems, recv_sems, local_sems, pieces):
        x, y, c, chips = _position()
        me = 2 * x + y
        cps = []
        for a in range(n):
            cps.append(pltpu.make_async_copy(ins[a].at[me], outs[a].at[me], local_sems.at[a]))
            rows = ins[a].shape[1]
            for k, (px, py) in enumerate(chips):
                for first, size in (_row_chunks(rows, ICI_CHUNKS) if pieces else [(0, rows)]):
                    cps.append(pltpu.make_async_remote_copy(
                        src_ref=ins[a].at[2 * px + py, pl.ds(first, size)], dst_ref=outs[a].at[me, pl.ds(first, size)],
                        send_sem=send_sems.at[a, k], recv_sem=recv_sems.at[a, k], device_id=(px, py, c),
                        device_id_type=MESH))
        return cps

    def start(*refs):
        for cp in copies(*refs, pieces=True):
            cp.start()

    def finish(*refs):
        for cp in copies(*refs, pieces=False):
            cp.wait()

    return _Carried(hsums, [_sds(h.shape, h.dtype) for h in hsums],
                    [pltpu.SemaphoreType.DMA((n, 3)), pltpu.SemaphoreType.DMA((n, 3)), pltpu.SemaphoreType.DMA((n,))],
                    start, finish)


def _pair_swap(reds):
    n = len(reds)

    def copier(send_sems, recv_sems):
        x, y, c, _ = _position()

        def copy(a, src, dst):
            return pltpu.make_async_remote_copy(src_ref=src, dst_ref=dst, send_sem=send_sems.at[a],
                                                recv_sem=recv_sems.at[a], device_id=(x, y, 1 - c), device_id_type=MESH)
        return copy

    def start(ins, outs, send_sems, recv_sems):
        copy = copier(send_sems, recv_sems)
        for a in range(n):
            for first, size in _row_chunks(ins[a].shape[0], 2 * D2D_CHUNKS):
                copy(a, ins[a].at[pl.ds(first, size), :], outs[a].at[pl.ds(first, size), :]).start()

    def finish(ins, outs, send_sems, recv_sems):
        copy = copier(send_sems, recv_sems)
        for a in range(n):
            copy(a, ins[a], outs[a]).wait()

    return _Carried(reds, [_sds(r.shape, r.dtype) for r in reds],
                    [pltpu.SemaphoreType.DMA((n,)), pltpu.SemaphoreType.DMA((n,))], start, finish)


def _reduce_scatter_last(grad):
    _, rows, cols = grad.shape
    half = rows // 2
    pieces = _row_chunks(half, D2D_CHUNKS)

    def body(g_ref, mine_ref, theirs_ref, got_scr, hsum_scr, slab_scr, pair_sems, ici_send, ici_recv, swap_sems):
        x, y, c, chips = _position()
        me = 2 * x + y
        sibling = (x, y, 1 - c)

        def to_sibling(src, dst, sems):
            return pltpu.make_async_remote_copy(src_ref=src, dst_ref=dst, send_sem=sems.at[0], recv_sem=sems.at[1],
                                                device_id=sibling, device_id_type=MESH)

        for slab in range(N_CHIPS):
            for first, size in pieces:
                to_sibling(g_ref.at[slab, pl.ds((1 - c) * half + first, size)], got_scr.at[slab, pl.ds(first, size)],
                           pair_sems).start()
        to_sibling(g_ref.at[:, pl.ds((1 - c) * half, half)], got_scr, pair_sems).wait()
        own = g_ref[:, pl.ds(pl.multiple_of(c * half, half), half), :]
        hsum_scr[...] = (own.astype(F32) + got_scr[...].astype(F32)).astype(BF16)

        slab_scr[me] = hsum_scr[me]
        ici = [pltpu.make_async_remote_copy(src_ref=hsum_scr.at[2 * px + py], dst_ref=slab_scr.at[me],
                                            send_sem=ici_send.at[k], recv_sem=ici_recv.at[k], device_id=(px, py, c),
                                            device_id_type=MESH) for k, (px, py) in enumerate(chips)]
        for cp in ici:
            cp.start()
        for cp in ici:
            cp.wait()
        acc = slab_scr[0].astype(F32)
        for k in range(1, N_CHIPS):
            acc = acc + slab_scr[k].astype(F32)
        mine_ref[...] = acc

        for first, size in pieces:
            to_sibling(mine_ref.at[pl.ds(first, size)], theirs_ref.at[pl.ds(first, size)], swap_sems).start()
        to_sibling(mine_ref, theirs_ref, swap_sems).wait()

    vmem = pl.BlockSpec(memory_space=pltpu.VMEM)
    halves = (N_CHIPS, half, cols)
    return pl.pallas_call(
        body, name="grad_reduce_scatter_last", out_shape=(_sds((half, cols), F32), _sds((half, cols), F32)),
        in_specs=[vmem], out_specs=(vmem, vmem),
        scratch_shapes=[pltpu.VMEM(halves, BF16), pltpu.VMEM(halves, BF16), pltpu.VMEM(halves, BF16),
                        pltpu.SemaphoreType.DMA((2,)), pltpu.SemaphoreType.DMA((3,)), pltpu.SemaphoreType.DMA((3,)),
                        pltpu.SemaphoreType.DMA((2,))],
        compiler_params=pltpu.CompilerParams(vmem_limit_bytes=VMEM_LIMIT_BYTES),
    )(grad)


def _device_gather_exchange(packed):
    def copies(ins, outs, send_sems, recv_sems, local_sem):
        (x_ref,), (all_ref,) = ins, outs
        x, y, c, chips = _position()
        me, sibling = (x, y, c), (x, y, 1 - c)

        def slab(px, py, pc):
            return all_ref.at[4 * px + 2 * py + pc]

        def copy(k, block, to, src=None):
            return pltpu.make_async_remote_copy(
                src_ref=slab(*block) if src is None else src, dst_ref=slab(*block), send_sem=send_sems.at[k],
                recv_sem=recv_sems.at[k], device_id=to, device_id_type=MESH)

        mine = pltpu.make_async_copy(x_ref, slab(*me), local_sem)
        first = [copy(0, me, sibling, src=x_ref)]
        first += [copy(1 + j, me, (*chip, c), src=x_ref) for j, chip in enumerate(chips)]
        passed = [copy(4 + j, (*chip, c), sibling) for j, chip in enumerate(chips)]
        return c, chips, me, sibling, copy, mine, first, passed

    def start(ins, outs, send_sems, recv_sems, local_sem):
        _, _, _, _, _, mine, first, _ = copies(ins, outs, send_sems, recv_sems, local_sem)
        mine.start()
        for cp in first:
            cp.start()

    def finish(ins, outs, send_sems, recv_sems, local_sem):
        c, chips, me, sibling, copy, mine, first, passed = copies(ins, outs, send_sems, recv_sems, local_sem)
        for j, chip in enumerate(chips):
            copy(1 + j, (*chip, c), me).wait_recv()
            passed[j].start()
        copy(0, sibling, me).wait_recv()
        for j, chip in enumerate(chips):
            copy(4 + j, (*chip, 1 - c), me).wait_recv()
        for cp in first + passed:
            cp.wait_send()
        mine.wait()

    return _Carried([packed], [_sds((N_DEV,) + packed.shape, F32)],
                    [pltpu.SemaphoreType.DMA((7,)), pltpu.SemaphoreType.DMA((7,)), pltpu.SemaphoreType.DMA],
                    start, finish)


def _ordered_sum(name, slabs):
    _, m_per, n_cols = slabs.shape

    def body(s_ref, o_ref):
        acc = s_ref[0]
        for d in range(1, N_DEV):
            acc = acc + s_ref[d]
        o_ref[...] = acc

    vmem = pl.BlockSpec(memory_space=pltpu.VMEM)
    return pl.pallas_call(body, name=name, out_shape=_sds((m_per, n_cols), F32), in_specs=[vmem], out_specs=vmem)(slabs)


def _pair_sum(core, own, got, tm):
    _, half, cols = got.shape
    nb = half // tm

    def body(c_ref, a_ref, b_ref, o_ref):
        o_ref[...] = (a_ref[...].astype(F32) + b_ref[...].astype(F32)).astype(BF16)

    return pl.pallas_call(
        body, name="grad_pair_sum", out_shape=_sds(got.shape, BF16),
        grid_spec=pltpu.PrefetchScalarGridSpec(
            num_scalar_prefetch=1, grid=(N_CHIPS, nb),
            in_specs=[pl.BlockSpec((None, tm, cols), lambda s, i, c_ref: (s, c_ref[0] * nb + i, 0)),
                      pl.BlockSpec((None, tm, cols), lambda s, i, c_ref: (s, i, 0))],
            out_specs=pl.BlockSpec((None, tm, cols), lambda s, i, c_ref: (s, i, 0))),
        compiler_params=_cparams(2),
    )(core, own, got)


def _chip_sum(slabs, tm):
    _, half, cols = slabs.shape

    def body(s_ref, o_ref):
        acc = s_ref[0].astype(F32)
        for k in range(1, N_CHIPS):
            acc = acc + s_ref[k].astype(F32)
        o_ref[...] = acc

    return pl.pallas_call(
        body, name="grad_chip_sum", out_shape=_sds((half, cols), F32), grid=(half // tm,),
        in_specs=[pl.BlockSpec((N_CHIPS, tm, cols), lambda i: (0, i, 0))],
        out_specs=pl.BlockSpec((tm, cols), lambda i: (i, 0)), compiler_params=_cparams(1),
    )(slabs)


def _adam_math(w, g, m, v):
    m2 = ADAM_B1 * m + (1.0 - ADAM_B1) * g
    v2 = ADAM_B2 * v + (1.0 - ADAM_B2) * (g * g)
    m_hat = m2 / (1.0 - ADAM_B1 ** ADAM_STEP)
    v_hat = v2 / (1.0 - ADAM_B2 ** ADAM_STEP)
    delta = -ADAM_LR * (m_hat / (jnp.sqrt(v_hat) + ADAM_EPS) + ADAM_WD * w)
    return delta, m2, v2


def _adamw_halves(name, items, tm, carried=None):
    rows, cols = items[0][0].shape
    nb = rows // 2 // tm
    n = len(items)

    def body(*refs):
        mine = (pl.program_id(0) // nb) == lax.axis_index("c")
        for k in range(n):
            w_ref, own_ref, oth_ref, m_ref, v_ref = refs[5 * k:5 * k + 5]
            g_ref, d_ref, m2_ref, v2_ref = refs[5 * n + 4 * k:5 * n + 4 * k + 4]
            g = jnp.where(mine, own_ref[...], oth_ref[...])
            d, m2, v2 = _adam_math(w_ref[...], g, m_ref[...], v_ref[...])
            g_ref[...] = g
            d_ref[...] = d
            m2_ref[...] = m2
            v2_ref[...] = v2

    full = pl.BlockSpec((tm, cols), lambda i: (i, 0))
    half = pl.BlockSpec((tm, cols), lambda i: (i % nb, 0))
    return _call_carrying(
        body, carried, name=name, grid=(rows // tm,), in_specs=[full, half, half, full, full] * n,
        out_specs=[full] * (4 * n), out_shape=tuple([_sds((rows, cols), F32)] * (4 * n)), scratch_shapes=[],
        operands=[a for item in items for a in item])


def _adamw(name, w, g, m, v, tm):
    def body(w_ref, g_ref, m_ref, v_ref, gout_ref, d_ref, m2_ref, v2_ref):
        gv = g_ref[...]
        d, m2, v2 = _adam_math(w_ref[...], gv, m_ref[...], v_ref[...])
        gout_ref[...] = gv
        d_ref[...] = d
        m2_ref[...] = m2
        v2_ref[...] = v2

    return _rows_call(name, body, tm, [w, g, m, v], [], [_sds(w.shape, F32)] * 4)


_SMALL_NAMES = ("norm_mix_pre", "gm_ln_w", "gm_ln_b", "gm_w_s", "gm_b_s", "conv_w", "conv_b", "dt_bias", "a_log",
                "d_skip", "ssm_norm_w", "norm_mix_post", "norm_ffn_pre", "norm_ffn_post")
_PACK_COLS = 1024


def _pack(parts, names=_SMALL_NAMES, tail=None):
    pieces = [parts[n].reshape(-1) for n in names]
    flat = jnp.concatenate(pieces if tail is None else pieces + [tail])
    rows = -(-flat.shape[0] // (8 * _PACK_COLS)) * 8
    flat = jnp.pad(flat, (0, rows * _PACK_COLS - flat.shape[0]))
    return flat.reshape(rows, _PACK_COLS)


def _unpack(packed, shapes, names=_SMALL_NAMES):
    flat = packed.reshape(-1)
    out, off = {}, 0
    for n in names:
        size = 1
        for s in shapes[n]:
            size *= s
        out[n] = flat[off:off + size].reshape(shapes[n])
        off += size
    return out


def kernel(x, norm_mix_pre, w_in, gm_ln_w, gm_ln_b, gm_w_s, gm_b_s, conv_w, conv_b, dt_bias, a_log, d_skip, ssm_norm_w, w_out, norm_mix_post, norm_ffn_pre, w_up, w_down, norm_ffn_post, loss_target, m_norm_mix_pre, m_w_in, m_gm_ln_w, m_gm_ln_b, m_gm_w_s, m_gm_b_s, m_conv_w, m_conv_b, m_dt_bias, m_a_log, m_d_skip, m_ssm_norm_w, m_w_out, m_norm_mix_post, m_norm_ffn_pre, m_w_up, m_w_down, m_norm_ffn_post, v_norm_mix_pre, v_w_in, v_gm_ln_w, v_gm_ln_b, v_gm_w_s, v_gm_b_s, v_conv_w, v_conv_b, v_dt_bias, v_a_log, v_d_skip, v_ssm_norm_w, v_w_out, v_norm_mix_post, v_norm_ffn_pre, v_w_up, v_w_down, v_norm_ffn_post):
    params = dict(norm_mix_pre=norm_mix_pre, w_in=w_in, gm_ln_w=gm_ln_w, gm_ln_b=gm_ln_b, gm_w_s=gm_w_s, gm_b_s=gm_b_s,
                  conv_w=conv_w, conv_b=conv_b, dt_bias=dt_bias, a_log=a_log, d_skip=d_skip, ssm_norm_w=ssm_norm_w,
                  w_out=w_out, norm_mix_post=norm_mix_post, norm_ffn_pre=norm_ffn_pre, w_up=w_up, w_down=w_down,
                  norm_ffn_post=norm_ffn_post)
    mom1 = dict(norm_mix_pre=m_norm_mix_pre, w_in=m_w_in, gm_ln_w=m_gm_ln_w, gm_ln_b=m_gm_ln_b, gm_w_s=m_gm_w_s,
                gm_b_s=m_gm_b_s, conv_w=m_conv_w, conv_b=m_conv_b, dt_bias=m_dt_bias, a_log=m_a_log, d_skip=m_d_skip,
                ssm_norm_w=m_ssm_norm_w, w_out=m_w_out, norm_mix_post=m_norm_mix_post, norm_ffn_pre=m_norm_ffn_pre,
                w_up=m_w_up, w_down=m_w_down, norm_ffn_post=m_norm_ffn_post)
    mom2 = dict(norm_mix_pre=v_norm_mix_pre, w_in=v_w_in, gm_ln_w=v_gm_ln_w, gm_ln_b=v_gm_ln_b, gm_w_s=v_gm_w_s,
                gm_b_s=v_gm_b_s, conv_w=v_conv_w, conv_b=v_conv_b, dt_bias=v_dt_bias, a_log=v_a_log, d_skip=v_d_skip,
                ssm_norm_w=v_ssm_norm_w, w_out=v_w_out, norm_mix_post=v_norm_mix_post, norm_ffn_pre=v_norm_ffn_pre,
                w_up=v_w_up, w_down=v_w_down, norm_ffn_post=v_norm_ffn_post)
    names = list(params)
    big = ("w_in", "w_out", "w_up", "w_down")
    chip = 2 * lax.axis_index("x") + lax.axis_index("y")

    shards = {n: params[n][0].astype(BF16) for n in big}
    conv_shard = jnp.pad(conv_w[0], ((0, 16 - CONV_K), (0, 0)))
    g_in4, g_conv4 = _run_exchange("allgather_w_in", _allgather_exchange([shards["w_in"], conv_shard]))
    conv_full = jnp.transpose(g_conv4[:, :CONV_K, :], (1, 0, 2)).reshape(CONV_K, CONV_CH)

    small = {n: params[n][0] if params[n].ndim >= 3 else params[n] for n in _SMALL_NAMES if n != "conv_w"}
    core = lax.axis_index("c").astype(jnp.int32).reshape(1)
    adam_args = {n: (params[n][0], mom1[n][0], mom2[n][0]) for n in big}
    loss, grad_x, big_out, small_sum = _forward_backward(
        x, loss_target, g_in4, conv_full, small, shards["w_out"], shards["w_up"], shards["w_down"], core, adam_args)
    grads, delta, new_m, new_v = {}, {}, {}, {}
    for n in big:
        grads[n], delta[n], new_m[n], new_v[n] = [a[None] for a in big_out[n]]

    small_sum["conv_w"] = lax.dynamic_slice_in_dim(small_sum["conv_w"], chip * (CONV_CH // N_CHIPS), CONV_CH // N_CHIPS, axis=1)

    local_shapes = {n: params[n].shape[1:] if params[n].ndim >= 3 else params[n].shape for n in _SMALL_NAMES}
    flat = lambda tree: {n: tree[n].reshape(local_shapes[n]) for n in _SMALL_NAMES}
    packed = [_pack(flat(t)) for t in (params, small_sum, mom1, mom2)]
    _, d_p, m_p, v_p = _adamw("adamw_small", *packed, packed[0].shape[0])
    for src, dst in ((d_p, delta), (m_p, new_m), (v_p, new_v)):
        for n, val in _unpack(src, local_shapes).items():
            dst[n] = val.reshape(params[n].shape)
    for n in _SMALL_NAMES:
        grads[n] = small_sum[n].reshape(params[n].shape)

    out = [loss, grad_x]
    for tree in (grads, delta, new_m, new_v):
        out += [tree[n] for n in names]
    return tuple(out)
```

```python
import functools

import jax
import jax.numpy as jnp
from jax import lax
from jax.experimental import pallas as pl
from jax.experimental.pallas import tpu as pltpu

F32 = jnp.float32
BF16 = jnp.bfloat16
MESH = pl.DeviceIdType.MESH

EPS = 1e-6
D_MODEL = 1024
GM_WIDTH = 512
SSM_WIDTH = 512
N_HEADS = 8
HEAD_DIM = 64
CHUNK = 128
SSM_GROUPS = 2
GROUP_W = SSM_WIDTH // SSM_GROUPS
SSM_STATE = 128
CONV_K = 4
CONV_CH = 1024
D_FF = 4096
IN_COLS = 2568
DT_PAD = 128
SSD_COLS = SSM_WIDTH + CONV_CH + DT_PAD
N_CHIPS = 4
N_DEV = 8

ADAM_LR = 0.001
ADAM_B1 = 0.9
ADAM_B2 = 0.999
ADAM_EPS = 1e-08
ADAM_WD = 0.01
ADAM_STEP = 10

VMEM_LIMIT_BYTES = 56 * 1024 * 1024
FF_TILE = 512
DW_TOKENS_PER_STEP = 2048


def _cparams(n_axes):
    return pltpu.CompilerParams(dimension_semantics=("arbitrary",) * n_axes, vmem_limit_bytes=VMEM_LIMIT_BYTES)


def _dot(a, b):
    return jnp.dot(a.astype(BF16), b.astype(BF16), preferred_element_type=F32)


def _dot_nt(a, b):
    return lax.dot_general(a.astype(BF16), b.astype(BF16), (((1,), (1,)), ((), ())), preferred_element_type=F32)


def _dot_tn(a, b):
    return lax.dot_general(a.astype(BF16), b.astype(BF16), (((0,), (0,)), ((), ())), preferred_element_type=F32)


def _sigmoid(x):
    return 1.0 / (1.0 + jnp.exp(-x))


_GELU_C = 0.7978845608028654
_GELU_A = 0.044715


def _gelu(x):
    t = jnp.tanh(_GELU_C * (x + _GELU_A * (x * x * x)))
    return 0.5 * x * (1.0 + t), t


def _gelu_grad(x, t):
    return 0.5 * (1.0 + t) + 0.5 * x * (1.0 - t * t) * (_GELU_C * (1.0 + 3.0 * _GELU_A * x * x))


def _rms_fwd(x, w):
    r = lax.rsqrt(jnp.mean(x * x, axis=-1, keepdims=True) + EPS)
    return x * r * w, r


def _rms_bwd(x, r, w, dy):
    g = dy * w
    dx = r * g - x * (r * r * r) * jnp.mean(g * x, axis=-1, keepdims=True)
    dw = jnp.sum(dy * x * r, axis=0, keepdims=True)
    return dx, dw


class _Carried:
    def __init__(self, ins, out_shapes, sems, start, finish):
        self.ins, self.out_shapes, self.sems = list(ins), list(out_shapes), list(sems)
        self.start, self.finish = start, finish


def _both(first, second):
    n_i, n_o, n_s = len(first.ins), len(first.out_shapes), len(first.sems)

    def split(ins, outs, sems):
        return (ins[:n_i], outs[:n_o], sems[:n_s]), (ins[n_i:], outs[n_o:], sems[n_s:])

    def start(ins, outs, *sems):
        (i1, o1, s1), (i2, o2, s2) = split(ins, outs, sems)
        first.start(i1, o1, *s1)
        second.start(i2, o2, *s2)

    def finish(ins, outs, *sems):
        (i1, o1, s1), (i2, o2, s2) = split(ins, outs, sems)
        first.finish(i1, o1, *s1)
        second.finish(i2, o2, *s2)

    return _Carried(first.ins + second.ins, first.out_shapes + second.out_shapes, first.sems + second.sems, start, finish)


def _split_carried(refs, n_in, n_out, n_scratch, carried):
    n_ci, n_co, n_cs = len(carried.ins), len(carried.out_shapes), len(carried.sems)
    ins, rest = refs[:n_in], refs[n_in:]
    c_ins, rest = rest[:n_ci], rest[n_ci:]
    outs, rest = rest[:n_out], rest[n_out:]
    c_outs, rest = rest[:n_co], rest[n_co:]
    scr, c_sems = rest[:n_scratch], rest[n_scratch:]
    assert len(c_sems) == n_cs
    return tuple(ins) + tuple(outs) + tuple(scr), c_ins, c_outs, c_sems


def _rows_call(name, body, tm, row_ins, const_ins, row_outs, acc_outs=(), scratch=(), carried=None):
    n_rows = row_ins[0].shape[0]
    assert n_rows % tm == 0
    n_steps = n_rows // tm
    n_in = len(row_ins) + len(const_ins)
    n_ro = len(row_outs)
    n_acc = len(acc_outs)

    def kern(*refs):
        accs = refs[n_in + n_ro:n_in + n_ro + n_acc]

        @pl.when(pl.program_id(0) == 0)
        def _():
            for a in accs:
                a[...] = jnp.zeros_like(a)

        body(*refs)

    def whole(shape):
        nd = len(shape)
        return pl.BlockSpec(tuple(shape), lambda i: (0,) * nd)

    in_specs = [pl.BlockSpec((tm, a.shape[1]), lambda i: (i, 0)) for a in row_ins]
    in_specs += [whole(a.shape) for a in const_ins]
    out_specs = [pl.BlockSpec((tm, s.shape[1]), lambda i: (i, 0)) for s in row_outs]
    out_specs += [whole(s.shape) for s in acc_outs]
    return _call_carrying(
        kern, carried, name=name, grid=(n_steps,), in_specs=in_specs, out_specs=out_specs,
        out_shape=tuple(row_outs) + tuple(acc_outs), scratch_shapes=list(scratch), operands=list(row_ins) + list(const_ins))


def _call_carrying(body, carried, *, name, grid, in_specs, out_specs, out_shape, scratch_shapes, operands):
    n_in, n_out, n_scratch = len(in_specs), len(out_specs), len(scratch_shapes)
    kern = body
    if carried is not None:
        def kern(*refs):
            plain, c_ins, c_outs, c_sems = _split_carried(refs, n_in, n_out, n_scratch, carried)
            first, last = True, True
            for d, size in enumerate(grid):
                first = jnp.logical_and(first, pl.program_id(d) == 0)
                last = jnp.logical_and(last, pl.program_id(d) == size - 1)

            @pl.when(first)
            def _():
                carried.start(c_ins, c_outs, *c_sems)

            body(*plain)

            @pl.when(last)
            def _():
                carried.finish(c_ins, c_outs, *c_sems)

        in_specs = list(in_specs) + [_HBM] * len(carried.ins)
        out_specs = list(out_specs) + [_HBM] * len(carried.out_shapes)
        out_shape = tuple(out_shape) + tuple(carried.out_shapes)
        operands = list(operands) + carried.ins
        scratch_shapes = list(scratch_shapes) + carried.sems
    return pl.pallas_call(
        kern, name=name, grid=grid, in_specs=in_specs, out_specs=out_specs, out_shape=out_shape,
        scratch_shapes=scratch_shapes, compiler_params=_cparams(len(grid)),
    )(*operands)


def _sds(shape, dtype):
    return jax.ShapeDtypeStruct(tuple(shape), dtype)


def _matmul_tn(name, a, b, tm, tn, tk, stacked=False, carried=None):
    k_dim, m_dim = a.shape
    n_dim = b.shape[1]
    assert m_dim % tm == 0 and n_dim % tn == 0 and k_dim % tk == 0
    nk = k_dim // tk

    def kern(a_ref, b_ref, o_ref, acc_ref):
        k = pl.program_id(2)
        prod = _dot_tn(a_ref[...], b_ref[...])

        @pl.when(k == 0)
        def _():
            acc_ref[...] = prod

        @pl.when(k > 0)
        def _():
            acc_ref[...] += prod

        @pl.when(k == nk - 1)
        def _():
            o_ref[...] = acc_ref[...].astype(o_ref.dtype)

    if stacked:
        assert tm == m_dim
        out_shape = _sds((n_dim // tn, m_dim, tn), BF16)
        out_spec = pl.BlockSpec((None, tm, tn), lambda i, j, k: (j, i, 0))
    else:
        out_shape = _sds((m_dim, n_dim), BF16)
        out_spec = pl.BlockSpec((tm, tn), lambda i, j, k: (i, j))
    outs = _call_carrying(
        kern, carried, name=name, grid=(m_dim // tm, n_dim // tn, nk),
        in_specs=[pl.BlockSpec((tk, tm), lambda i, j, k: (k, i)), pl.BlockSpec((tk, tn), lambda i, j, k: (k, j))],
        out_specs=[out_spec], out_shape=(out_shape,), scratch_shapes=[pltpu.VMEM((tm, tn), F32)], operands=[a, b])
    return outs[0] if carried is None else outs


def _inproj_fwd(x, nw, w_uv, w_xbc, w_z, w_dt, tm=256, carried=None):
    n_tok = x.shape[0]

    def body(x_ref, nw_ref, wuv_ref, wxbc_ref, wz_ref, wdt_ref, puv_ref, pxbc_ref, pz_ref, pdt_ref, h_ref):
        h, _ = _rms_fwd(x_ref[...], nw_ref[...])
        h = h.astype(BF16)
        puv_ref[...] = jnp.dot(h, wuv_ref[...], preferred_element_type=F32)
        pxbc_ref[...] = jnp.dot(h, wxbc_ref[...], preferred_element_type=F32)
        pz_ref[...] = jnp.dot(h, wz_ref[...], preferred_element_type=F32)
        pdt_ref[...] = jnp.dot(h, wdt_ref[...], preferred_element_type=F32)
        h_ref[...] = h

    return _rows_call(
        "inproj_fwd", body, tm, [x], [nw, w_uv, w_xbc, w_z, w_dt],
        [_sds((n_tok, 2 * GM_WIDTH), F32), _sds((n_tok, CONV_CH), F32), _sds((n_tok, SSM_WIDTH), F32),
         _sds((n_tok, DT_PAD), F32), _sds((n_tok, D_MODEL), BF16)], carried=carried)


def _head_lane_mask(width, head):
    lane = lax.broadcasted_iota(jnp.int32, (1, width), 1)
    return (lane // HEAD_DIM) == head


def _split_terms(x, terms):
    parts = []
    for _ in range(terms):
        p = x.astype(BF16)
        parts.append(p)
        x = x - p.astype(F32)
    return parts


def _seg_dots(vals, ind, terms=2):
    m = vals[0].shape[0]
    parts = []
    for v in vals:
        parts += _split_terms(v, terms)
    red = jnp.dot(jnp.concatenate(parts, axis=0), ind, preferred_element_type=F32)
    outs = []
    for i in range(len(vals)):
        acc = red[i * terms * m:(i * terms + 1) * m]
        for t in range(1, terms):
            acc = acc + red[(i * terms + t) * m:(i * terms + t + 1) * m]
        outs.append(acc)
    return outs


def _tri_dot(mask, x, terms=3):
    n = x.shape[1]
    red = jnp.dot(mask.astype(BF16), jnp.concatenate(_split_terms(x, terms), axis=1), preferred_element_type=F32)
    acc = red[:, :n]
    for t in range(1, terms):
        acc = acc + red[:, t * n:(t + 1) * n]
    return acc


def _gmlp_common(puv, lnw, lnb, e_bf, et_bf):
    u = puv[:, :GM_WIDTH]
    v = puv[:, GM_WIDTH:]
    gu, tu = _gelu(u)
    gv, tv = _gelu(v)
    (s1,) = _seg_dots([gv], et_bf)
    (mu,) = _seg_dots([s1 * (1.0 / HEAD_DIM)], e_bf)
    xc = gv - mu
    (s2,) = _seg_dots([xc * xc], et_bf)
    (rstd,) = _seg_dots([lax.rsqrt(s2 * (1.0 / HEAD_DIM) + EPS)], e_bf)
    xhat = xc * rstd
    vn = xhat * lnw + lnb
    return u, v, gu, tu, tv, rstd, xhat, vn


def _tril_mask():
    r = lax.broadcasted_iota(jnp.int32, (CHUNK, CHUNK), 0)
    c = lax.broadcasted_iota(jnp.int32, (CHUNK, CHUNK), 1)
    return r >= c


def _head_blocks(v):
    return jnp.concatenate([jnp.where(_head_lane_mask(GM_WIDTH, h), v, jnp.zeros_like(v)) for h in range(N_HEADS)], axis=0)


def _causal_w_cat(w_cat):
    t = lax.broadcasted_iota(jnp.int32, (CHUNK, N_HEADS * CHUNK), 0)
    s = lax.broadcasted_iota(jnp.int32, (CHUNK, N_HEADS * CHUNK), 1) % CHUNK
    return jnp.where(t >= s, w_cat, 0.0).astype(BF16)


def _gmlp_chunk_fwd(puv, lnw, lnb, e_bf, et_bf, wm, bmap):
    _, _, gu, _, _, _, _, vn = _gmlp_common(puv, lnw, lnb, e_bf, et_bf)
    mixed = jnp.dot(wm, _head_blocks(vn.astype(BF16)), preferred_element_type=F32) + bmap
    return (gu * mixed).astype(BF16)


SUBLANES = 8


def _shift_down(x, tail, s):
    main = pltpu.roll(x, s, 0)
    row = lax.broadcasted_iota(jnp.int32, (SUBLANES, 1), 0)
    head = jnp.where(row < s, pltpu.roll(tail, s, 0), main[:SUBLANES])
    return jnp.concatenate([head, main[SUBLANES:]], axis=0)


def _shift_up(x, head_next, s):
    n = x.shape[0]
    main = pltpu.roll(x, n - s, 0)
    row = lax.broadcasted_iota(jnp.int32, (SUBLANES, 1), 0)
    last = jnp.where(row >= SUBLANES - s, pltpu.roll(head_next, SUBLANES - s, 0), main[n - SUBLANES:])
    return jnp.concatenate([main[:n - SUBLANES], last], axis=0)


def _ssd_pre(xr, tail, cw_ref, cb, pdt, dtb, alog, emap):
    rowi = lax.broadcasted_iota(jnp.int32, (CHUNK, 1), 0)
    shifted = [_shift_down(xr, tail, 3), _shift_down(xr, tail, 2), _shift_down(xr, tail, 1), xr]
    xc = cb
    for k in range(CONV_K):
        xc = xc + cw_ref[k] * shifted[k]
    sg = _sigmoid(xc)
    xa = xc * sg
    pre = pdt + dtb
    dt = jnp.maximum(pre, 0.0) + jnp.log(1.0 + jnp.exp(-jnp.abs(pre)))
    a_neg = -jnp.exp(alog)
    a_cs = _tri_dot(_tril_mask(), dt * a_neg)
    acs_map, dt_map = _seg_dots([a_cs, dt], emap, terms=3)
    return dict(shifted=shifted, xc=xc, sg=sg, xa=xa, pre=pre, dt=dt, a_neg=a_neg, a_cs=a_cs,
                acs_map=acs_map, dt_map=dt_map, rowi=rowi)


def _ssd_maps(p):
    last = p["rowi"] == CHUNK - 1
    aq_map = jnp.sum(jnp.where(last, p["acs_map"], 0.0), axis=0, keepdims=True)
    e_exp = jnp.exp(p["acs_map"])
    dte = jnp.exp(aq_map - p["acs_map"])
    cd = jnp.exp(aq_map)
    return last, e_exp, dte, cd


def _head_decay(a_cs, a_cs_t, head, tri):
    lane = lax.broadcasted_iota(jnp.int32, (1, DT_PAD), 1)
    sub = lax.broadcasted_iota(jnp.int32, (DT_PAD, 1), 0)
    col = jnp.sum(jnp.where(lane == head, a_cs, 0.0), axis=1, keepdims=True)
    row = jnp.sum(jnp.where(sub == head, a_cs_t, 0.0), axis=0, keepdims=True)
    return jnp.exp(jnp.where(tri, col - row, -1e30))


def _gate_fwd(y, z, nw):
    sz = _sigmoid(z)
    zg = z * sz
    yg = y * zg
    outs, rs = [], []
    for g in range(SSM_GROUPS):
        gs = slice(g * GROUP_W, (g + 1) * GROUP_W)
        o, r = _rms_fwd(yg[:, gs], nw[:, gs])
        outs.append(o)
        rs.append(r)
    return sz, zg, yg, outs, rs


def _ssd_const_specs():
    def whole(shape):
        nd = len(shape)
        return pl.BlockSpec(tuple(shape), lambda c: (0,) * nd)
    return [whole((CONV_K, 1, CONV_CH)), whole((1, CONV_CH)), whole((1, DT_PAD)), whole((1, DT_PAD)),
            whole((1, SSM_WIDTH)), whole((1, SSM_WIDTH)), whole((DT_PAD, SSM_WIDTH)), whole((SSM_WIDTH, DT_PAD))]


def _mixer_fwd(p_uv, p_xbc, p_z, p_dt, x, lnw, lnb, w_cat, bmap, w_out, nw_post, nw_pre2, conv_w, conv_b, dt_bias, a_log,
               dskip_map, norm_w, e_bf, et_bf, n_seq, carried=None):
    n_tok = p_xbc.shape[0]
    nc = n_tok // n_seq // CHUNK

    def body(puv3, xr3, z3, pdt3, x3, lnw_ref, lnb_ref, wcat_ref, bmap_ref, wo_ref, nwa_ref, nwb_ref,
             cw_ref, cb_ref, dtb_ref, alog_ref, dsk_ref, nw_ref, e_ref, et_ref,
             mix3, yssd3, sprev3, o3, x13, h23, wm_scr, prev3_scr, s3_scr):
        @pl.when(pl.program_id(0) == 0)
        def _():
            wm_scr[...] = _causal_w_cat(wcat_ref[...])
            prev3_scr[...] = jnp.zeros_like(prev3_scr)
            s3_scr[...] = jnp.zeros_like(s3_scr)

        for b in range(n_seq):
            one_sequence(puv3.at[b], xr3.at[b], z3.at[b], pdt3.at[b], lnw_ref, lnb_ref, bmap_ref,
                         cw_ref, cb_ref, dtb_ref, alog_ref, dsk_ref, nw_ref, e_ref, et_ref,
                         mix3.at[b], yssd3.at[b], sprev3.at[b], wm_scr, prev3_scr.at[b], s3_scr.at[b])
            o = jnp.dot(mix3[b], wo_ref[...], preferred_element_type=F32)
            on, _ = _rms_fwd(o, nwa_ref[...])
            x1 = x3[b] + on
            h2, _ = _rms_fwd(x1, nwb_ref[...])
            o3[b] = o
            x13[b] = x1
            h23[b] = h2.astype(BF16)

    def one_sequence(puv_ref, xr_ref, z_ref, pdt_ref, lnw_ref, lnb_ref, bmap_ref,
                     cw_ref, cb_ref, dtb_ref, alog_ref, dsk_ref, nw_ref, e_ref, et_ref,
                     mix_ref, yssd_ref, sprev_ref, wm_scr, prev_scr, s_scr):
        mix_ref[:, :GM_WIDTH] = _gmlp_chunk_fwd(puv_ref[...], lnw_ref[...], lnb_ref[...], e_ref[...], et_ref[...], wm_scr[...],
                                      bmap_ref[...])
        xr = xr_ref[...]
        p = _ssd_pre(xr, prev_scr[...], cw_ref, cb_ref[...], pdt_ref[...], dtb_ref[...], alog_ref[...], e_ref[...])
        _, e_exp, dte, cd = _ssd_maps(p)
        xs = p["xa"][:, :SSM_WIDTH]
        xd = xs * p["dt_map"]
        a_cs_t = p["a_cs"].T
        tri = _tril_mask()
        s_old = s_scr[...]
        sprev_ref[...] = s_old
        for g in range(SSM_GROUPS):
            gs = slice(g * GROUP_W, (g + 1) * GROUP_W)
            bm = p["xa"][:, SSM_WIDTH + g * SSM_STATE: SSM_WIDTH + (g + 1) * SSM_STATE].astype(BF16)
            cm = p["xa"][:, SSM_WIDTH + (SSM_GROUPS + g) * SSM_STATE: SSM_WIDTH + (SSM_GROUPS + g + 1) * SSM_STATE].astype(BF16)
            cb_mat = _dot_nt(cm, bm)
            xdg = xd[:, gs].astype(BF16)
            y_g = _dot(cm, s_old[:, gs]) * e_exp[:, gs] + dsk_ref[:, gs] * xs[:, gs]
            for r in range(SSM_GROUPS * 2):
                dm = _head_decay(p["a_cs"], a_cs_t, g * 4 + r, tri)
                full = jnp.dot((cb_mat * dm).astype(BF16), xdg, preferred_element_type=F32)
                y_g = y_g + jnp.where(_head_lane_mask(GROUP_W, r), full, 0.0)
            yssd_ref[:, gs] = y_g
            s_scr[:, gs] = cd[:, gs] * s_old[:, gs] + _dot_tn(bm, xd[:, gs] * dte[:, gs])
        _, _, _, outs, _ = _gate_fwd(yssd_ref[...], z_ref[...], nw_ref[...])
        for g in range(SSM_GROUPS):
            mix_ref[:, GM_WIDTH + g * GROUP_W:GM_WIDTH + (g + 1) * GROUP_W] = outs[g].astype(BF16)
        prev_scr[...] = xr[CHUNK - SUBLANES:, :]

    seq_len = n_tok // n_seq

    def rows(width):
        return pl.BlockSpec((n_seq, CHUNK, width), lambda c: (0, c, 0))

    def whole(shape):
        nd = len(shape)
        return pl.BlockSpec(tuple(shape), lambda c: (0,) * nd)

    def by_seq(a):
        return a.reshape(n_seq, seq_len, a.shape[-1])

    outs = _call_carrying(
        body, carried, name="mixer_fwd", grid=(nc,),
        in_specs=[rows(2 * GM_WIDTH), rows(CONV_CH), rows(SSM_WIDTH), rows(DT_PAD), rows(D_MODEL), whole(lnw.shape),
                  whole(lnb.shape), whole(w_cat.shape), whole(bmap.shape), whole(w_out.shape), whole(nw_post.shape),
                  whole(nw_pre2.shape)] + _ssd_const_specs(),
        out_specs=[rows(D_MODEL), rows(SSM_WIDTH), rows(SSM_WIDTH), rows(D_MODEL), rows(D_MODEL), rows(D_MODEL)],
        out_shape=(_sds((n_seq, seq_len, D_MODEL), BF16),
                   _sds((n_seq, seq_len, SSM_WIDTH), F32), _sds((n_seq, seq_len, SSM_WIDTH), F32),
                   _sds((n_seq, seq_len, D_MODEL), F32), _sds((n_seq, seq_len, D_MODEL), F32),
                   _sds((n_seq, seq_len, D_MODEL), BF16)),
        scratch_shapes=[pltpu.VMEM((CHUNK, N_HEADS * CHUNK), BF16), pltpu.VMEM((n_seq, SUBLANES, CONV_CH), F32),
                        pltpu.VMEM((n_seq, SSM_STATE, SSM_WIDTH), F32)],
        operands=[by_seq(p_uv), by_seq(p_xbc), by_seq(p_z), by_seq(p_dt), by_seq(x), lnw, lnb, w_cat, bmap, w_out, nw_post,
                  nw_pre2, conv_w, conv_b, dt_bias, a_log, dskip_map, norm_w, e_bf, et_bf])
    return tuple(o.reshape(n_tok, o.shape[-1]) for o in outs[:6]) + tuple(outs[6:])


def _up_cols(wup_ref, j):
    per = (D_FF // N_CHIPS) // FF_TILE
    return wup_ref[j // per, :, (j % per) * FF_TILE:(j % per + 1) * FF_TILE]


def _down_rows(wda_ref, wdb_ref, j):
    assert 2 * FF_TILE == D_FF // N_CHIPS
    return (wda_ref if j % 2 == 0 else wdb_ref)[j // 2]


def _skewed_rows_call(name, main, tail, tm, lead_ins, lag_ins, const_ins, lead_outs, lag_outs, acc_outs, carry,
                      streamed, tile_copies, n_copies):
    n_rows = lead_ins[0].shape[0]
    assert n_rows % tm == 0
    n = n_rows // tm
    counts = [len(lead_ins), len(lag_ins), len(const_ins), len(streamed), len(lead_outs), len(lag_outs), len(acc_outs),
              1, len(streamed)]

    def kern(*refs):
        groups, pos = [], 0
        for cnt in counts:
            groups.append(refs[pos:pos + cnt])
            pos += cnt
        lead_i, lag_i, consts, w_hbm, lead_o, lag_o, accs, (carry_scr,), w_vmem = groups
        sems = refs[pos]
        i = pl.program_id(0)
        pieces, k = [], 0
        for piece in tile_copies(w_hbm, w_vmem):
            pieces.append([pltpu.make_async_copy(src, dst, sems.at[k + q]) for q, (src, dst) in enumerate(piece)])
            k += len(piece)

        def ready(j):
            for cp in pieces[j]:
                cp.wait()

        @pl.when(i == 0)
        def _():
            for piece in pieces:
                for cp in piece:
                    cp.start()
            for a in accs:
                a[...] = jnp.zeros_like(a)
            carry_scr[...] = main(lead_i, consts, lead_o, w_vmem, ready)

        @pl.when(jnp.logical_and(i > 0, i < n))
        def _():
            previous = carry_scr[...]
            carry_scr[...] = main(lead_i, consts, lead_o, w_vmem, lambda j: None)
            tail(previous, lag_i, consts, lag_o, accs)

        @pl.when(i == n)
        def _():
            tail(carry_scr[...], lag_i, consts, lag_o, accs)

    def lead(width):
        return pl.BlockSpec((tm, width), lambda i: (jnp.minimum(i, n - 1), 0))

    def lag(width):
        return pl.BlockSpec((tm, width), lambda i: (jnp.maximum(i - 1, 0), 0))

    def whole(shape):
        nd = len(shape)
        return pl.BlockSpec(tuple(shape), lambda i: (0,) * nd)

    return pl.pallas_call(
        kern, name=name, grid=(n + 1,),
        in_specs=([lead(a.shape[1]) for a in lead_ins] + [lag(a.shape[1]) for a in lag_ins]
                  + [whole(a.shape) for a in const_ins] + [_HBM] * len(streamed)),
        out_specs=[lead(s.shape[1]) for s in lead_outs] + [lag(s.shape[1]) for s in lag_outs] + [whole(s.shape) for s in acc_outs],
        out_shape=tuple(lead_outs) + tuple(lag_outs) + tuple(acc_outs),
        scratch_shapes=([pltpu.VMEM(carry, F32)] + [pltpu.VMEM(a.shape, a.dtype) for a in streamed]
                        + [pltpu.SemaphoreType.DMA((n_copies,))]),
        compiler_params=_cparams(1),
    )(*lead_ins, *lag_ins, *const_ins, *streamed)


def _mlp_weight_pieces(order):
    per = (D_FF // N_CHIPS) // FF_TILE

    def tile_copies(hbm, vmem):
        pieces = []
        for j in range(D_FF // FF_TILE):
            cols = (j // per, slice(None), pl.ds((j % per) * FF_TILE, FF_TILE))
            up = (hbm[0].at[cols], vmem[0].at[cols])
            down = (hbm[1 + j % 2].at[j // 2], vmem[1 + j % 2].at[j // 2])
            pieces.append([up, down] if order == "up_down" else [down, up])
        return pieces

    return tile_copies


def _mlp_fwd(h2, x1, tgt, w_up, w_down_a, w_down_b, nw, tm=512):
    n_tok = x1.shape[0]

    def main(lead_i, consts, lead_o, weights, ready):
        (h2_ref,), (f_ref,), (wup_ref, wda_ref, wdb_ref) = lead_i, lead_o, weights
        h2v = h2_ref[...]
        acc = jnp.zeros((tm, D_MODEL), F32)
        for j in range(D_FF // FF_TILE):
            cs = slice(j * FF_TILE, (j + 1) * FF_TILE)
            ready(j)
            u = jnp.dot(h2v, _up_cols(wup_ref, j), preferred_element_type=F32)
            f = jnp.square(jnp.maximum(u, 0.0)).astype(BF16)
            f_ref[:, cs] = f
            acc = acc + jnp.dot(f, _down_rows(wda_ref, wdb_ref, j), preferred_element_type=F32)
        return acc

    def tail(acc, lag_i, consts, lag_o, accs):
        (x1_ref, tgt_ref), (nw_ref,), (dd_ref, dy_ref), (loss_ref, dnw_ref) = lag_i, consts, lag_o, accs
        dn, r = _rms_fwd(acc, nw_ref[...])
        e = x1_ref[...] + dn - tgt_ref[...]
        loss_ref[...] += jnp.full(loss_ref.shape, (0.5 / D_MODEL) * jnp.sum(e * e), F32)
        dy = e * (1.0 / D_MODEL)
        dd, dnw = _rms_bwd(acc, r, nw_ref[...], dy)
        dy_ref[...] = dy
        dd_ref[...] = dd.astype(BF16)
        dnw_ref[...] += dnw

    return _skewed_rows_call(
        "mlp_fwd", main, tail, tm, [h2], [x1, tgt], [nw],
        [_sds((n_tok, D_FF), BF16)], [_sds((n_tok, D_MODEL), BF16), _sds((n_tok, D_MODEL), F32)],
        [_sds((8, 128), F32), _sds((1, D_MODEL), F32)], carry=(tm, D_MODEL),
        streamed=[w_up, w_down_a, w_down_b], tile_copies=_mlp_weight_pieces("up_down"), n_copies=2 * (D_FF // FF_TILE))


def _mlp_bwd(dd, f, x1, dy, w_down_a, w_down_b, w_up, nw, tm=256):
    n_tok = x1.shape[0]

    def main(lead_i, consts, lead_o, weights, ready):
        (dd_ref, f_ref), (dup_ref,), (wup_ref, wda_ref, wdb_ref) = lead_i, lead_o, weights
        ddv = dd_ref[...]
        acc = jnp.zeros((tm, D_MODEL), F32)
        for j in range(D_FF // FF_TILE):
            cs = slice(j * FF_TILE, (j + 1) * FF_TILE)
            ready(j)
            df = _dot_nt(ddv, _down_rows(wda_ref, wdb_ref, j))
            du = (df * (2.0 * jnp.sqrt(f_ref[:, cs].astype(F32)))).astype(BF16)
            dup_ref[:, cs] = du
            acc = acc + _dot_nt(du, _up_cols(wup_ref, j))
        return acc

    def tail(acc, lag_i, consts, lag_o, accs):
        (x1_ref, dy_ref), (nw_ref,), (dx1_ref,), (dnw_ref,) = lag_i, consts, lag_o, accs
        x1v = x1_ref[...]
        _, r = _rms_fwd(x1v, nw_ref[...])
        dx, dnw = _rms_bwd(x1v, r, nw_ref[...], acc)
        dx1_ref[...] = dy_ref[...] + dx
        dnw_ref[...] += dnw

    return _skewed_rows_call(
        "mlp_bwd", main, tail, tm, [dd, f], [x1, dy], [nw],
        [_sds((n_tok, D_FF), BF16)], [_sds((n_tok, D_MODEL), F32)], [_sds((1, D_MODEL), F32)], carry=(tm, D_MODEL),
        streamed=[w_up, w_down_a, w_down_b], tile_copies=_mlp_weight_pieces("down_up"), n_copies=2 * (D_FF // FF_TILE))


def _outproj_bwd(dx1, o, w_out, nw, tm=256, carried=None):
    n_tok = dx1.shape[0]

    def body(dx1_ref, o_ref, wo_ref, nw_ref, do_ref, dya_ref, dyb_ref, dnw_ref):
        ov = o_ref[...]
        _, r = _rms_fwd(ov, nw_ref[...])
        do, dnw = _rms_bwd(ov, r, nw_ref[...], dx1_ref[...])
        dob = do.astype(BF16)
        do_ref[...] = dob
        dya_ref[...] = _dot_nt(dob, wo_ref[:GM_WIDTH, :])
        dyb_ref[...] = _dot_nt(dob, wo_ref[GM_WIDTH:, :])
        dnw_ref[...] += dnw

    return _rows_call("outproj_bwd", body, tm, [dx1, o], [w_out, nw],
                      [_sds((n_tok, D_MODEL), BF16), _sds((n_tok, GM_WIDTH), F32), _sds((n_tok, SSM_WIDTH), F32)],
                      [_sds((1, D_MODEL), F32)], carried=carried)


def _gmlp_bwd(p_uv, dya, lnw, lnb, e_bf, et_bf, w_cat, w_stack, bmap, carried=None):
    n_tok = p_uv.shape[0]
    chunks_per_step = 2

    def body(puv_ref, dya_ref, lnw_ref, lnb_ref, e_ref, et_ref, wcat_ref, wstack_ref, bmap_ref,
             dpuv_ref, dws_ref, dbs_ref, dlnw_ref, dlnb_ref, wm_scr, wsm_scr):
        t_stk = lax.broadcasted_iota(jnp.int32, (N_HEADS * CHUNK, CHUNK), 0) % CHUNK
        s_stk = lax.broadcasted_iota(jnp.int32, (N_HEADS * CHUNK, CHUNK), 1)

        @pl.when(pl.program_id(0) == 0)
        def _():
            wm_scr[...] = _causal_w_cat(wcat_ref[...])
            wsm_scr[...] = jnp.where(t_stk >= s_stk, wstack_ref[...], 0.0).astype(BF16)

        lnw_v = lnw_ref[...]
        e_v, et_v = e_ref[...], et_ref[...]

        def one_chunk(rows):
            u, v, gu, tu, tv, rstd, xhat, vn = _gmlp_common(puv_ref[rows, :], lnw_v, lnb_ref[...], e_v, et_v)
            vnb = vn.astype(BF16)
            mixed = jnp.dot(wm_scr[...], _head_blocks(vnb), preferred_element_type=F32) + bmap_ref[...]
            dy = dya_ref[rows, :]
            du = dy * mixed * _gelu_grad(u, tu)
            dmixed = dy * gu
            (dbs,) = _seg_dots([dmixed], et_v)
            dblocks = _head_blocks(dmixed.astype(BF16))
            dvn = lax.dot_general(wsm_scr[...], dblocks, (((0,), (0,)), ((), ())), preferred_element_type=F32)
            dws = lax.dot_general(dblocks, vnb, (((1,), (1,)), ((), ())), preferred_element_type=F32)
            dxh = dvn * lnw_v
            m1, m2 = _seg_dots([dxh, dxh * xhat], et_v)
            m1, m2 = _seg_dots([m1 * (1.0 / HEAD_DIM), m2 * (1.0 / HEAD_DIM)], e_v)
            dgv = rstd * (dxh - m1 - xhat * m2)
            dv = dgv * _gelu_grad(v, tv)
            dpuv_ref[rows, :GM_WIDTH] = du.astype(BF16)
            dpuv_ref[rows, GM_WIDTH:] = dv.astype(BF16)
            return dbs, dws, jnp.sum(dvn * xhat, axis=0, keepdims=True), jnp.sum(dvn, axis=0, keepdims=True)

        parts = [one_chunk(slice(k * CHUNK, (k + 1) * CHUNK)) for k in range(chunks_per_step)]
        dbs, dws, dlnw, dlnb = [functools.reduce(lambda a, b: a + b, vals) for vals in zip(*parts)]
        dbs_ref[...] += dbs
        dws_ref[...] += jnp.where(t_stk >= s_stk, dws, 0.0)
        dlnw_ref[...] += dlnw
        dlnb_ref[...] += dlnb

    return _rows_call(
        "gmlp_bwd", body, chunks_per_step * CHUNK, [p_uv, dya], [lnw, lnb, e_bf, et_bf, w_cat, w_stack, bmap],
        [_sds((n_tok, 2 * GM_WIDTH), BF16)],
        [_sds((N_HEADS * CHUNK, CHUNK), F32), _sds((CHUNK, DT_PAD), F32), _sds((1, GM_WIDTH), F32),
         _sds((1, GM_WIDTH), F32)],
        scratch=[pltpu.VMEM((CHUNK, N_HEADS * CHUNK), BF16), pltpu.VMEM((N_HEADS * CHUNK, CHUNK), BF16)],
        carried=carried)


def _ssd_bwd(p_xbc, p_z, p_dt, yssd, sprev, dyb, conv_w, conv_b, dt_bias, a_log, dskip_map, norm_w, e_bf, et_bf, n_seq,
             carried=None):
    n_tok = p_xbc.shape[0]
    nc = n_tok // n_seq // CHUNK

    def body(xr3, xprev3, z3, pdt3, yssd3, sprev3, dyb3,
             cw_ref, cb_ref, dtb_ref, alog_ref, dsk_ref, nw_ref, e_ref, et_ref,
             dps3, dcw_ref, dcb_ref, ddtb_ref, dalog_ref, ddsk_ref, dnw_ref,
             ds3_scr, nxt3_scr, dxa3_scr):
        @pl.when(pl.program_id(0) == 0)
        def _():
            for a in (dcw_ref, dcb_ref, ddtb_ref, dalog_ref, ddsk_ref, dnw_ref, ds3_scr, nxt3_scr):
                a[...] = jnp.zeros_like(a)

        for b in range(n_seq):
            one_sequence(xr3.at[b], xprev3.at[b], z3.at[b], pdt3.at[b], yssd3.at[b], sprev3.at[b], dyb3.at[b],
                         cw_ref, cb_ref, dtb_ref, alog_ref, dsk_ref, nw_ref, e_ref, et_ref,
                         dps3.at[b], dcw_ref, dcb_ref, ddtb_ref, dalog_ref, ddsk_ref, dnw_ref,
                         ds3_scr.at[b], nxt3_scr.at[b], dxa3_scr.at[b])

    def one_sequence(xr_ref, xprev_ref, z_ref, pdt_ref, yssd_ref, sprev_ref, dyb_ref,
                     cw_ref, cb_ref, dtb_ref, alog_ref, dsk_ref, nw_ref, e_ref, et_ref,
                     dps_ref, dcw_ref, dcb_ref, ddtb_ref, dalog_ref, ddsk_ref, dnw_ref,
                     ds_scr, nxt_scr, dxa_scr):
        chunk = nc - 1 - pl.program_id(0)
        xr = xr_ref[...]
        prev = jnp.where(chunk == 0, 0.0, xprev_ref[...])
        et_v = et_ref[...]
        p = _ssd_pre(xr, prev, cw_ref, cb_ref[...], pdt_ref[...], dtb_ref[...], alog_ref[...], e_ref[...])
        last, e_exp, dte, cd = _ssd_maps(p)
        rowi = p["rowi"]
        xs = p["xa"][:, :SSM_WIDTH]
        xd = xs * p["dt_map"]
        a_cs_t = p["a_cs"].T
        tri = _tril_mask()
        dsk = dsk_ref[...]
        nw_v = nw_ref[...]

        yv = yssd_ref[...]
        zv = z_ref[...]
        sz, zg, yg, _, rs = _gate_fwd(yv, zv, nw_v)
        dout = dyb_ref[...]
        for g in range(SSM_GROUPS):
            gs = slice(g * GROUP_W, (g + 1) * GROUP_W)
            dyg_g, dnw_g = _rms_bwd(yg[:, gs], rs[g], nw_v[:, gs], dout[:, gs])
            dnw_ref[:, gs] += dnw_g
            dxa_scr[:, gs] = dyg_g
        dyg = dxa_scr[:, :SSM_WIDTH]
        d_y = dyg * zg
        dps_ref[:, :SSM_WIDTH] = (dyg * yv * (sz + zv * sz * (1.0 - sz))).astype(BF16)

        s_prev = sprev_ref[...]
        ds_next = ds_scr[...]
        lane_dt = lax.broadcasted_iota(jnp.int32, (1, DT_PAD), 1)
        da_cols = jnp.zeros((CHUNK, DT_PAD), F32)
        for g in range(SSM_GROUPS):
            gs = slice(g * GROUP_W, (g + 1) * GROUP_W)
            b_off = SSM_WIDTH + g * SSM_STATE
            c_off = SSM_WIDTH + (SSM_GROUPS + g) * SSM_STATE
            bm = p["xa"][:, b_off:b_off + SSM_STATE].astype(BF16)
            cm = p["xa"][:, c_off:c_off + SSM_STATE].astype(BF16)
            cb_mat = _dot_nt(cm, bm)
            d_yg = d_y[:, gs]
            d_ygb = d_yg.astype(BF16)
            xdg = xd[:, gs]
            xdgb = xdg.astype(BF16)
            ds_g = ds_next[:, gs]
            sp_g = s_prev[:, gs]
            bds = _dot(bm, ds_g)
            dcs = d_yg * e_exp[:, gs]
            d_c = _dot_nt(dcs, sp_g)
            ds_scr[:, gs] = cd[:, gs] * ds_g + _dot_tn(cm, dcs)
            d_b = _dot_nt(xdg * dte[:, gs], ds_g)
            dxd_g = bds * dte[:, gs]
            sum_dcb = jnp.zeros((CHUNK, CHUNK), F32)
            for r in range(SSM_GROUPS * 2):
                head = g * 4 + r
                mask = _head_lane_mask(GROUP_W, r)
                dm = _head_decay(p["a_cs"], a_cs_t, head, tri)
                m_mat = cb_mat * dm
                g_mat = _dot_nt(jnp.where(mask, d_yg, 0.0), xdgb)
                w_mat = g_mat * m_mat
                sum_dcb = sum_dcb + g_mat * dm
                dxd_g = dxd_g + jnp.where(mask, _dot_tn(m_mat, d_ygb), 0.0)
                da_h = jnp.sum(w_mat - w_mat.T, axis=1, keepdims=True)
                da_cols = da_cols + jnp.where(lane_dt == head, da_h, 0.0)
            d_c = d_c + _dot(sum_dcb, bm)
            d_b = d_b + _dot_tn(sum_dcb, cm)
            dxa_scr[:, b_off:b_off + SSM_STATE] = d_b
            dxa_scr[:, c_off:c_off + SSM_STATE] = d_c
            y_off_g = _dot(cm, sp_g) * e_exp[:, gs]
            t3 = bds * xdg * dte[:, gs]
            tail = jnp.sum(t3, axis=0, keepdims=True) + jnp.sum(ds_g * sp_g, axis=0, keepdims=True) * cd[:, gs]
            pre_g = d_yg * y_off_g - t3 + jnp.where(last, tail, 0.0)
            s_pre, ddt_g, s_dsk = _seg_dots([pre_g, dxd_g * xs[:, gs], d_yg * xs[:, gs]], et_v[gs, :])
            da_cols = da_cols + s_pre
            ddsk_ref[...] += jnp.sum(s_dsk, axis=0, keepdims=True)
            dxa_scr[:, gs] = dxd_g * p["dt_map"][:, gs] + dsk[:, gs] * d_yg
            if g == 0:
                ddt = ddt_g
            else:
                ddt = ddt + ddt_g
        r_i = lax.broadcasted_iota(jnp.int32, (CHUNK, CHUNK), 0)
        c_i = lax.broadcasted_iota(jnp.int32, (CHUNK, CHUNK), 1)
        ddta = _tri_dot(r_i <= c_i, da_cols, terms=2)
        ddt = ddt + ddta * p["a_neg"]
        dalog_ref[...] += jnp.sum(ddta * p["dt"], axis=0, keepdims=True) * p["a_neg"]
        draw = ddt * _sigmoid(p["pre"])
        ddtb_ref[...] += jnp.sum(draw, axis=0, keepdims=True)
        dps_ref[:, SSM_WIDTH + CONV_CH:] = draw.astype(BF16)

        xc = p["xc"]
        sg = p["sg"]
        dxc = dxa_scr[...] * (sg + xc * sg * (1.0 - sg))
        dcb_ref[...] += jnp.sum(dxc, axis=0, keepdims=True)
        for k in range(CONV_K):
            dcw_ref[k] += jnp.sum(dxc * p["shifted"][k], axis=0, keepdims=True)
        nxt = nxt_scr[...]
        dxr = cw_ref[3] * dxc
        for s in range(1, CONV_K):
            dxr = dxr + cw_ref[CONV_K - 1 - s] * _shift_up(dxc, nxt, s)
        dps_ref[:, SSM_WIDTH:SSM_WIDTH + CONV_CH] = dxr.astype(BF16)
        nxt_scr[...] = dxc[:SUBLANES, :]

    seq_len = n_tok // n_seq

    def rows(width):
        return pl.BlockSpec((n_seq, CHUNK, width), lambda s: (0, nc - 1 - s, 0))

    tiles = CHUNK // SUBLANES
    prev_rows = pl.BlockSpec((n_seq, SUBLANES, CONV_CH), lambda s: (0, jnp.maximum((nc - 1 - s) * tiles - 1, 0), 0))

    def whole(shape):
        nd = len(shape)
        return pl.BlockSpec(tuple(shape), lambda s: (0,) * nd)

    def by_seq(a):
        return a.reshape(n_seq, seq_len, a.shape[-1])

    acc_shapes = [(CONV_K, 1, CONV_CH), (1, CONV_CH), (1, DT_PAD), (1, DT_PAD), (1, DT_PAD), (1, SSM_WIDTH)]
    xbc3 = by_seq(p_xbc)
    outs = _call_carrying(
        body, carried, name="ssd_bwd", grid=(nc,),
        in_specs=[rows(CONV_CH), prev_rows, rows(SSM_WIDTH), rows(DT_PAD), rows(SSM_WIDTH), rows(SSM_WIDTH),
                  rows(SSM_WIDTH)] + _ssd_const_specs(),
        out_specs=[rows(SSD_COLS)] + [whole(s) for s in acc_shapes],
        out_shape=tuple([_sds((n_seq, seq_len, SSD_COLS), BF16)] + [_sds(s, F32) for s in acc_shapes]),
        scratch_shapes=[pltpu.VMEM((n_seq, SSM_STATE, SSM_WIDTH), F32), pltpu.VMEM((n_seq, SUBLANES, CONV_CH), F32),
                        pltpu.VMEM((n_seq, CHUNK, CONV_CH), F32)],
        operands=[xbc3, xbc3, by_seq(p_z), by_seq(p_dt), by_seq(yssd), by_seq(sprev), by_seq(dyb), conv_w, conv_b, dt_bias,
                  a_log, dskip_map, norm_w, e_bf, et_bf])
    return (outs[0].reshape(n_tok, SSD_COLS),) + tuple(outs[1:])


def _inproj_bwd(dp_uv, dp_ssd, x, dx1, w_uv, w_ssd, nw, tm=512, carried=None):
    n_tok = x.shape[0]

    def body(duv_ref, dssd_ref, x_ref, dx1_ref, wuv_ref, wssd_ref, nw_ref, gx_ref, dnw_ref):
        dh = _dot_nt(duv_ref[...], wuv_ref[...]) + _dot_nt(dssd_ref[...], wssd_ref[...])
        xv = x_ref[...]
        _, r = _rms_fwd(xv, nw_ref[...])
        dx, dnw = _rms_bwd(xv, r, nw_ref[...], dh)
        gx_ref[...] = dx1_ref[...] + dx
        dnw_ref[...] += dnw

    return _rows_call("inproj_bwd", body, tm, [dp_uv, dp_ssd, x, dx1], [w_uv, w_ssd, nw],
                      [_sds((n_tok, D_MODEL), F32)], [_sds((1, D_MODEL), F32)], carried=carried)


def _const_maps():
    lane = jnp.arange(SSM_WIDTH) // HEAD_DIM
    e_bf = (jnp.arange(DT_PAD)[:, None] == lane[None, :]).astype(BF16)
    return e_bf, e_bf.T


def _pad_lanes(v, width):
    return jnp.pad(v, ((0, 0), (0, width - v.shape[1])))


SHARD_COLS = IN_COLS // N_CHIPS
_UV_END = 2 * GM_WIDTH
_Z_END = _UV_END + SSM_WIDTH
_XBC_END = _Z_END + CONV_CH


def _cols_from_shards(w4, lo, hi):
    pieces = []
    for j in range(N_CHIPS):
        a, b = max(lo, j * SHARD_COLS), min(hi, (j + 1) * SHARD_COLS)
        if a < b:
            pieces.append(w4[j][:, a - j * SHARD_COLS:b - j * SHARD_COLS])
    return pieces[0] if len(pieces) == 1 else jnp.concatenate(pieces, axis=1)


def _shards_from_cols(blocks):
    shards = []
    for j in range(N_CHIPS):
        pieces = []
        for arr, lo, hi in blocks:
            a, b = max(lo, j * SHARD_COLS), min(hi, (j + 1) * SHARD_COLS)
            if a < b:
                pieces.append(arr[:, a - lo:b - lo])
        shards.append(pieces[0] if len(pieces) == 1 else jnp.concatenate(pieces, axis=1))
    return jnp.stack(shards)


def _forward_backward(x, tgt, w_in4, conv_w, small, out_shard, up_shard, down_shard, core, adam_args):
    n_seq, seq_len, _ = x.shape
    n_tok = n_seq * seq_len
    x2 = x.reshape(n_tok, D_MODEL)
    tgt2 = tgt.reshape(n_tok, D_MODEL)
    e_bf, et_bf = _const_maps()

    w_uv = _cols_from_shards(w_in4, 0, _UV_END)
    w_z = _cols_from_shards(w_in4, _UV_END, _Z_END)
    w_xbc = _cols_from_shards(w_in4, _Z_END, _XBC_END)
    w_dt = _pad_lanes(_cols_from_shards(w_in4, _XBC_END, IN_COLS), DT_PAD)

    nw_pre = small["norm_mix_pre"]
    lnw = small["gm_ln_w"].reshape(1, GM_WIDTH)
    lnb = small["gm_ln_b"].reshape(1, GM_WIDTH)
    w_stack = small["gm_w_s"].reshape(N_HEADS * CHUNK, CHUNK)
    w_cat = jnp.transpose(small["gm_w_s"], (1, 0, 2)).reshape(CHUNK, N_HEADS * CHUNK)
    bmap = jnp.repeat(small["gm_b_s"].T, HEAD_DIM, axis=1)
    cw3 = conv_w.reshape(CONV_K, 1, CONV_CH)
    conv_b = small["conv_b"]
    dt_bias = _pad_lanes(small["dt_bias"], DT_PAD)
    a_log = _pad_lanes(small["a_log"], DT_PAD)
    dskip_map = jnp.repeat(small["d_skip"], HEAD_DIM, axis=1)
    ssm_nw = small["ssm_norm_w"]

    half = down_shard.shape[0] // 2
    p_uv, p_xbc, p_z, p_dt, h, w_out4, w_down_a = _inproj_fwd(
        x2, nw_pre, w_uv, w_xbc, w_z, w_dt, carried=_allgather_exchange([out_shard, down_shard[:half]]))
    ssd_consts = (cw3, conv_b, dt_bias, a_log, dskip_map, ssm_nw, e_bf, et_bf)
    w_out_b = w_out4.reshape(D_MODEL, D_MODEL)
    mix, yssd, sprev, o, x1, h2, w_up4, w_down_b = _mixer_fwd(
        p_uv, p_xbc, p_z, p_dt, x2, lnw, lnb, w_cat, bmap, w_out_b, small["norm_mix_post"], small["norm_ffn_pre"],
        *ssd_consts, n_seq, carried=_allgather_exchange([up_shard, down_shard[half:]]))
    f, dd, dy, loss_acc, d_nffn_post = _mlp_fwd(h2, x1, tgt2, w_up4, w_down_a, w_down_b, small["norm_ffn_post"])

    dup, dx1, d_nffn_pre = _mlp_bwd(dd, f, x1, dy, w_down_a, w_down_b, w_up4, small["norm_ffn_pre"])
    tk = min(DW_TOKENS_PER_STEP, n_tok)
    g_up = _matmul_tn("dw_up", h2, dup, D_MODEL, D_MODEL, tk, stacked=True)
    g_down = _matmul_tn("dw_down", f, dd, 1024, D_MODEL, tk).reshape(N_CHIPS, D_FF // N_CHIPS, D_MODEL)
    do, dya, dyb, d_nmix_post, got_up, got_down = _outproj_bwd(
        dx1, o, w_out_b, small["norm_mix_post"], carried=_pair_exchange([g_up, g_down]))
    h_up = _pair_sum(core, g_up, got_up, 512)
    h_down = _pair_sum(core, g_down, got_down, 512)
    g_out = _matmul_tn("dw_out", mix, do, D_MODEL, D_MODEL, tk).reshape(N_CHIPS, D_MODEL // N_CHIPS, D_MODEL)
    dp_uv, d_ws, d_bs_t, d_lnw, d_lnb, slab_up, got_out = _gmlp_bwd(
        p_uv, dya, lnw, lnb, e_bf, et_bf, w_cat, w_stack, bmap,
        carried=_both(_chip_exchange([h_up]), _pair_exchange([g_out])))
    h_out = _pair_sum(core, g_out, got_out, 128)
    early = {
        "gm_ln_w": d_lnw.reshape(N_HEADS, HEAD_DIM), "gm_ln_b": d_lnb.reshape(N_HEADS, HEAD_DIM),
        "gm_w_s": d_ws.reshape(N_HEADS, CHUNK, CHUNK), "gm_b_s": d_bs_t[:, :N_HEADS].T,
        "norm_mix_post": d_nmix_post, "norm_ffn_pre": d_nffn_pre, "norm_ffn_post": d_nffn_post,
    }
    packed_early = _pack(early, tuple(early), tail=loss_acc[0, 0].reshape(1))
    (dp_ssd, d_cw, d_cb, d_dtb, d_alog, d_dsk, d_ssm_nw, slab_down, slab_out, all_early) = _ssd_bwd(
        p_xbc, p_z, p_dt, yssd, sprev, dyb, *ssd_consts, n_seq,
        carried=_both(_chip_exchange([h_down, h_out]), _device_gather_exchange(packed_early)))
    w_ssd = _pad_lanes(_cols_from_shards(w_in4, _UV_END, IN_COLS), SSD_COLS)
    gx, d_nmix_pre = _inproj_bwd(dp_uv, dp_ssd, x2, dx1, w_uv, w_ssd, nw_pre)
    late = {
        "norm_mix_pre": d_nmix_pre, "conv_w": d_cw.reshape(CONV_K, CONV_CH), "conv_b": d_cb,
        "dt_bias": d_dtb[:, :N_HEADS], "a_log": d_alog[:, :N_HEADS], "d_skip": d_dsk[:, :N_HEADS],
        "ssm_norm_w": d_ssm_nw,
    }
    g_uv, all_late = _matmul_tn("dw_in_uv", h, dp_uv, D_MODEL, 2 * GM_WIDTH, tk,
                                carried=_device_gather_exchange(_pack(late, tuple(late))))
    sum_early = _ordered_sum("small_sum_early", all_early)
    small_sum = _unpack(sum_early, {n: v.shape for n, v in early.items()}, tuple(early))
    small_sum.update(_unpack(_ordered_sum("small_sum_late", all_late), {n: v.shape for n, v in late.items()}, tuple(late)))
    loss = sum_early.reshape(-1)[sum(v.size for v in early.values())]
    red_up, red_down, red_out = _chip_sum(slab_up, 512), _chip_sum(slab_down, 512), _chip_sum(slab_out, 128)
    g_ssd, oth_up, oth_down, oth_out = _matmul_tn("dw_in_ssd", h, dp_ssd, D_MODEL, SSD_COLS, tk,
                                                  carried=_pair_swap([red_up, red_down, red_out]))
    g_in = _shards_from_cols([(g_uv, 0, _UV_END), (g_ssd, _UV_END, IN_COLS)])
    red_in, oth_in = _reduce_scatter_last(g_in)
    res = _adamw_halves("adamw_mlp", [(adam_args["w_up"][0], red_up, oth_up) + adam_args["w_up"][1:],
                                      (adam_args["w_down"][0], red_down, oth_down) + adam_args["w_down"][1:]], 256)
    big_out = {"w_up": res[0:4], "w_down": res[4:8]}
    big_out["w_out"] = _adamw_halves("adamw_w_out", [(adam_args["w_out"][0], red_out, oth_out) + adam_args["w_out"][1:]], 128)
    big_out["w_in"] = _adamw_halves("adamw_w_in", [(adam_args["w_in"][0], red_in, oth_in) + adam_args["w_in"][1:]], 256)

    return loss, gx.reshape(x.shape), big_out, small_sum


_HBM = pl.BlockSpec(memory_space=pltpu.HBM)


D2D_CHUNKS = 8
ICI_CHUNKS = 1
ROW_ALIGN = 16


def _row_chunks(rows, n_chunks):
    size = min(max(rows // n_chunks, ROW_ALIGN), rows)
    assert rows % size == 0
    return [(start, size) for start in range(0, rows, size)]


def _position():
    x, y, c = lax.axis_index("x"), lax.axis_index("y"), lax.axis_index("c")
    chips = [(1 - x, y), (x, 1 - y), (1 - x, 1 - y)]
    return x, y, c, chips


def _allgather_exchange(arrs):
    n = len(arrs)

    def copies(ins, outs, send_sems, recv_sems, local_sems):
        x, y, c, chips = _position()
        me = 2 * x + y
        sibling = (x, y, 1 - c)

        def copy(a, k, src, dst, to):
            return pltpu.make_async_remote_copy(src_ref=src, dst_ref=dst, send_sem=send_sems.at[a, k],
                                                recv_sem=recv_sems.at[a, k], device_id=to, device_id_type=MESH)

        def half_rows(a, pc):
            half = ins[a].shape[0] // 2
            return pl.ds(pc * half, half)

        local = [pltpu.make_async_copy(ins[a], outs[a].at[me], local_sems.at[a]) for a in range(n)]
        ici_out = [[copy(a, k, ins[a].at[half_rows(a, c)], outs[a].at[me, half_rows(a, c)], (px, py, c))
                    for k, (px, py) in enumerate(chips)] for a in range(n)]
        return c, chips, sibling, copy, half_rows, local, ici_out

    def start(ins, outs, send_sems, recv_sems, local_sems):
        c, chips, _, copy, _, local, _ = copies(ins, outs, send_sems, recv_sems, local_sems)
        x, y, _, _ = _position()
        me = 2 * x + y
        for cp in local:
            cp.start()
        for a in range(n):
            half = ins[a].shape[0] // 2
            for k, (px, py) in enumerate(chips):
                for first, size in _row_chunks(half, ICI_CHUNKS):
                    rows = pl.ds(c * half + first, size)
                    copy(a, k, ins[a].at[rows], outs[a].at[me, rows], (px, py, c)).start()

    def finish(ins, outs, send_sems, recv_sems, local_sems):
        c, chips, sibling, copy, half_rows, local, ici_out = copies(ins, outs, send_sems, recv_sems, local_sems)
        for a in range(n):
            half = ins[a].shape[0] // 2
            for k, (px, py) in enumerate(chips):
                blk = outs[a].at[2 * px + py, half_rows(a, c)]
                copy(a, k, blk, blk, (px, py, c)).wait_recv()
                for first, size in _row_chunks(half, D2D_CHUNKS):
                    piece = outs[a].at[2 * px + py, pl.ds(c * half + first, size)]
                    copy(a, 3 + k, piece, piece, sibling).start()
        for a in range(n):
            for k, (px, py) in enumerate(chips):
                theirs = outs[a].at[2 * px + py, half_rows(a, 1 - c)]
                copy(a, 3 + k, theirs, theirs, sibling).wait_recv()
                mine = outs[a].at[2 * px + py, half_rows(a, c)]
                copy(a, 3 + k, mine, mine, sibling).wait_send()
        for a in range(n):
            for cp in ici_out[a]:
                cp.wait_send()
        for cp in local:
            cp.wait()

    return _Carried(arrs, [_sds((N_CHIPS,) + a.shape, a.dtype) for a in arrs],
                    [pltpu.SemaphoreType.DMA((n, 6)), pltpu.SemaphoreType.DMA((n, 6)), pltpu.SemaphoreType.DMA((n,))],
                    start, finish)


def _run_exchange(name, exchange):
    n_in, n_out = len(exchange.ins), len(exchange.out_shapes)

    def body(*refs):
        ins, outs, sems = refs[:n_in], refs[n_in:n_in + n_out], refs[n_in + n_out:]
        exchange.start(ins, outs, *sems)
        exchange.finish(ins, outs, *sems)

    return pl.pallas_call(
        body, name=name, out_shape=tuple(exchange.out_shapes), in_specs=[_HBM] * n_in,
        out_specs=tuple([_HBM] * n_out), scratch_shapes=exchange.sems,
    )(*exchange.ins)


def _pair_exchange(grads):
    n = len(grads)

    def copier(send_sems, recv_sems):
        x, y, c, _ = _position()

        def copy(a, src, dst):
            return pltpu.make_async_remote_copy(src_ref=src, dst_ref=dst, send_sem=send_sems.at[a],
                                                recv_sem=recv_sems.at[a], device_id=(x, y, 1 - c), device_id_type=MESH)
        return c, copy

    def start(ins, got, send_sems, recv_sems):
        c, copy = copier(send_sems, recv_sems)
        for a in range(n):
            half = ins[a].shape[1] // 2
            for slab in range(N_CHIPS):
                for first, size in _row_chunks(half, D2D_CHUNKS):
                    copy(a, ins[a].at[slab, pl.ds((1 - c) * half + first, size), :],
                         got[a].at[slab, pl.ds(first, size), :]).start()

    def finish(ins, got, send_sems, recv_sems):
        c, copy = copier(send_sems, recv_sems)
        for a in range(n):
            half = ins[a].shape[1] // 2
            copy(a, ins[a].at[:, pl.ds((1 - c) * half, half), :], got[a]).wait()

    return _Carried(grads, [_sds((N_CHIPS, g.shape[1] // 2, g.shape[2]), g.dtype) for g in grads],
                    [pltpu.SemaphoreType.DMA((n,)), pltpu.SemaphoreType.DMA((n,))], start, finish)


def _chip_exchange(hsums):
    n = len(hsums)

    def copies(ins, outs, send_sems, recv_sems, local_sems, pieces):
        x, y, c, chips = _position()
        me = 2 * x + y
        cps = []
        for a in range(n):
            cps.append(pltpu.make_async_copy(ins[a].at[me], outs[a].at[me], local_sems.at[a]))
            rows = ins[a].shape[1]
            for k, (px, py) in enumerate(chips):
                for first, size in (_row_chunks(rows, ICI_CHUNKS) if pieces else [(0, rows)]):
                    cps.append(pltpu.make_async_remote_copy(
                        src_ref=ins[a].at[2 * px + py, pl.ds(first, size)], dst_ref=outs[a].at[me, pl.ds(first, size)],
                        send_sem=send_sems.at[a, k], recv_sem=recv_sems.at[a, k], device_id=(px, py, c),
                        device_id_type=MESH))
        return cps

    def start(*refs):
        for cp in copies(*refs, pieces=True):
            cp.start()

    def finish(*refs):
        for cp in copies(*refs, pieces=False):
            cp.wait()

    return _Carried(hsums, [_sds(h.shape, h.dtype) for h in hsums],
                    [pltpu.SemaphoreType.DMA((n, 3)), pltpu.SemaphoreType.DMA((n, 3)), pltpu.SemaphoreType.DMA((n,))],
                    start, finish)


def _pair_swap(reds):
    n = len(reds)

    def copier(send_sems, recv_sems):
        x, y, c, _ = _position()

        def copy(a, src, dst):
            return pltpu.make_async_remote_copy(src_ref=src, dst_ref=dst, send_sem=send_sems.at[a],
                                                recv_sem=recv_sems.at[a], device_id=(x, y, 1 - c), device_id_type=MESH)
        return copy

    def start(ins, outs, send_sems, recv_sems):
        copy = copier(send_sems, recv_sems)
        for a in range(n):
            for first, size in _row_chunks(ins[a].shape[0], 2 * D2D_CHUNKS):
                copy(a, ins[a].at[pl.ds(first, size), :], outs[a].at[pl.ds(first, size), :]).start()

    def finish(ins, outs, send_sems, recv_sems):
        copy = copier(send_sems, recv_sems)
        for a in range(n):
            copy(a, ins[a], outs[a]).wait()

    return _Carried(reds, [_sds(r.shape, r.dtype) for r in reds],
                    [pltpu.SemaphoreType.DMA((n,)), pltpu.SemaphoreType.DMA((n,))], start, finish)


def _reduce_scatter_last(grad):
    _, rows, cols = grad.shape
    half = rows // 2
    pieces = _row_chunks(half, D2D_CHUNKS)

    def body(g_ref, mine_ref, theirs_ref, got_scr, hsum_scr, slab_scr, pair_sems, ici_send, ici_recv, swap_sems):
        x, y, c, chips = _position()
        me = 2 * x + y
        sibling = (x, y, 1 - c)

        def to_sibling(src, dst, sems):
            return pltpu.make_async_remote_copy(src_ref=src, dst_ref=dst, send_sem=sems.at[0], recv_sem=sems.at[1],
                                                device_id=sibling, device_id_type=MESH)

        for slab in range(N_CHIPS):
            for first, size in pieces:
                to_sibling(g_ref.at[slab, pl.ds((1 - c) * half + first, size)], got_scr.at[slab, pl.ds(first, size)],
                           pair_sems).start()
        to_sibling(g_ref.at[:, pl.ds((1 - c) * half, half)], got_scr, pair_sems).wait()
        own = g_ref[:, pl.ds(pl.multiple_of(c * half, half), half), :]
        hsum_scr[...] = (own.astype(F32) + got_scr[...].astype(F32)).astype(BF16)

        slab_scr[me] = hsum_scr[me]
        ici = [pltpu.make_async_remote_copy(src_ref=hsum_scr.at[2 * px + py], dst_ref=slab_scr.at[me],
                                            send_sem=ici_send.at[k], recv_sem=ici_recv.at[k], device_id=(px, py, c),
                                            device_id_type=MESH) for k, (px, py) in enumerate(chips)]
        for cp in ici:
            cp.start()
        for cp in ici:
            cp.wait()
        acc = slab_scr[0].astype(F32)
        for k in range(1, N_CHIPS):
            acc = acc + slab_scr[k].astype(F32)
        mine_ref[...] = acc

        for first, size in pieces:
            to_sibling(mine_ref.at[pl.ds(first, size)], theirs_ref.at[pl.ds(first, size)], swap_sems).start()
        to_sibling(mine_ref, theirs_ref, swap_sems).wait()

    vmem = pl.BlockSpec(memory_space=pltpu.VMEM)
    halves = (N_CHIPS, half, cols)
    return pl.pallas_call(
        body, name="grad_reduce_scatter_last", out_shape=(_sds((half, cols), F32), _sds((half, cols), F32)),
        in_specs=[vmem], out_specs=(vmem, vmem),
        scratch_shapes=[pltpu.VMEM(halves, BF16), pltpu.VMEM(halves, BF16), pltpu.VMEM(halves, BF16),
                        pltpu.SemaphoreType.DMA((2,)), pltpu.SemaphoreType.DMA((3,)), pltpu.SemaphoreType.DMA((3,)),
                        pltpu.SemaphoreType.DMA((2,))],
        compiler_params=pltpu.CompilerParams(vmem_limit_bytes=VMEM_LIMIT_BYTES),
    )(grad)


def _device_gather_exchange(packed):
    def copies(ins, outs, send_sems, recv_sems, local_sem):
        (x_ref,), (all_ref,) = ins, outs
        x, y, c, chips = _position()
        me, sibling = (x, y, c), (x, y, 1 - c)

        def slab(px, py, pc):
            return all_ref.at[4 * px + 2 * py + pc]

        def copy(k, block, to, src=None):
            return pltpu.make_async_remote_copy(
                src_ref=slab(*block) if src is None else src, dst_ref=slab(*block), send_sem=send_sems.at[k],
                recv_sem=recv_sems.at[k], device_id=to, device_id_type=MESH)

        mine = pltpu.make_async_copy(x_ref, slab(*me), local_sem)
        first = [copy(0, me, sibling, src=x_ref)]
        first += [copy(1 + j, me, (*chip, c), src=x_ref) for j, chip in enumerate(chips)]
        passed = [copy(4 + j, (*chip, c), sibling) for j, chip in enumerate(chips)]
        return c, chips, me, sibling, copy, mine, first, passed

    def start(ins, outs, send_sems, recv_sems, local_sem):
        _, _, _, _, _, mine, first, _ = copies(ins, outs, send_sems, recv_sems, local_sem)
        mine.start()
        for cp in first:
            cp.start()

    def finish(ins, outs, send_sems, recv_sems, local_sem):
        c, chips, me, sibling, copy, mine, first, passed = copies(ins, outs, send_sems, recv_sems, local_sem)
        for j, chip in enumerate(chips):
            copy(1 + j, (*chip, c), me).wait_recv()
            passed[j].start()
        copy(0, sibling, me).wait_recv()
        for j, chip in enumerate(chips):
            copy(4 + j, (*chip, 1 - c), me).wait_recv()
        for cp in first + passed:
            cp.wait_send()
        mine.wait()

    return _Carried([packed], [_sds((N_DEV,) + packed.shape, F32)],
                    [pltpu.SemaphoreType.DMA((7,)), pltpu.SemaphoreType.DMA((7,)), pltpu.SemaphoreType.DMA],
                    start, finish)


def _ordered_sum(name, slabs):
    _, m_per, n_cols = slabs.shape

    def body(s_ref, o_ref):
        acc = s_ref[0]
        for d in range(1, N_DEV):
            acc = acc + s_ref[d]
        o_ref[...] = acc

    vmem = pl.BlockSpec(memory_space=pltpu.VMEM)
    return pl.pallas_call(body, name=name, out_shape=_sds((m_per, n_cols), F32), in_specs=[vmem], out_specs=vmem)(slabs)


def _pair_sum(core, own, got, tm):
    _, half, cols = got.shape
    nb = half // tm

    def body(c_ref, a_ref, b_ref, o_ref):
        o_ref[...] = (a_ref[...].astype(F32) + b_ref[...].astype(F32)).astype(BF16)

    return pl.pallas_call(
        body, name="grad_pair_sum", out_shape=_sds(got.shape, BF16),
        grid_spec=pltpu.PrefetchScalarGridSpec(
            num_scalar_prefetch=1, grid=(N_CHIPS, nb),
            in_specs=[pl.BlockSpec((None, tm, cols), lambda s, i, c_ref: (s, c_ref[0] * nb + i, 0)),
                      pl.BlockSpec((None, tm, cols), lambda s, i, c_ref: (s, i, 0))],
            out_specs=pl.BlockSpec((None, tm, cols), lambda s, i, c_ref: (s, i, 0))),
        compiler_params=_cparams(2),
    )(core, own, got)


def _chip_sum(slabs, tm):
    _, half, cols = slabs.shape

    def body(s_ref, o_ref):
        acc = s_ref[0].astype(F32)
        for k in range(1, N_CHIPS):
            acc = acc + s_ref[k].astype(F32)
        o_ref[...] = acc

    return pl.pallas_call(
        body, name="grad_chip_sum", out_shape=_sds((half, cols), F32), grid=(half // tm,),
        in_specs=[pl.BlockSpec((N_CHIPS, tm, cols), lambda i: (0, i, 0))],
        out_specs=pl.BlockSpec((tm, cols), lambda i: (i, 0)), compiler_params=_cparams(1),
    )(slabs)


def _adam_math(w, g, m, v):
    m2 = ADAM_B1 * m + (1.0 - ADAM_B1) * g
    v2 = ADAM_B2 * v + (1.0 - ADAM_B2) * (g * g)
    m_hat = m2 / (1.0 - ADAM_B1 ** ADAM_STEP)
    v_hat = v2 / (1.0 - ADAM_B2 ** ADAM_STEP)
    delta = -ADAM_LR * (m_hat / (jnp.sqrt(v_hat) + ADAM_EPS) + ADAM_WD * w)
    return delta, m2, v2


def _adamw_halves(name, items, tm, carried=None):
    rows, cols = items[0][0].shape
    nb = rows // 2 // tm
    n = len(items)

    def body(*refs):
        mine = (pl.program_id(0) // nb) == lax.axis_index("c")
        for k in range(n):
            w_ref, own_ref, oth_ref, m_ref, v_ref = refs[5 * k:5 * k + 5]
            g_ref, d_ref, m2_ref, v2_ref = refs[5 * n + 4 * k:5 * n + 4 * k + 4]
            g = jnp.where(mine, own_ref[...], oth_ref[...])
            d, m2, v2 = _adam_math(w_ref[...], g, m_ref[...], v_ref[...])
            g_ref[...] = g
            d_ref[...] = d
            m2_ref[...] = m2
            v2_ref[...] = v2

    full = pl.BlockSpec((tm, cols), lambda i: (i, 0))
    half = pl.BlockSpec((tm, cols), lambda i: (i % nb, 0))
    return _call_carrying(
        body, carried, name=name, grid=(rows // tm,), in_specs=[full, half, half, full, full] * n,
        out_specs=[full] * (4 * n), out_shape=tuple([_sds((rows, cols), F32)] * (4 * n)), scratch_shapes=[],
        operands=[a for item in items for a in item])


def _adamw(name, w, g, m, v, tm):
    def body(w_ref, g_ref, m_ref, v_ref, gout_ref, d_ref, m2_ref, v2_ref):
        gv = g_ref[...]
        d, m2, v2 = _adam_math(w_ref[...], gv, m_ref[...], v_ref[...])
        gout_ref[...] = gv
        d_ref[...] = d
        m2_ref[...] = m2
        v2_ref[...] = v2

    return _rows_call(name, body, tm, [w, g, m, v], [], [_sds(w.shape, F32)] * 4)


_SMALL_NAMES = ("norm_mix_pre", "gm_ln_w", "gm_ln_b", "gm_w_s", "gm_b_s", "conv_w", "conv_b", "dt_bias", "a_log",
                "d_skip", "ssm_norm_w", "norm_mix_post", "norm_ffn_pre", "norm_ffn_post")
_PACK_COLS = 1024


def _pack(parts, names=_SMALL_NAMES, tail=None):
    pieces = [parts[n].reshape(-1) for n in names]
    flat = jnp.concatenate(pieces if tail is None else pieces + [tail])
    rows = -(-flat.shape[0] // (8 * _PACK_COLS)) * 8
    flat = jnp.pad(flat, (0, rows * _PACK_COLS - flat.shape[0]))
    return flat.reshape(rows, _PACK_COLS)


def _unpack(packed, shapes, names=_SMALL_NAMES):
    flat = packed.reshape(-1)
    out, off = {}, 0
    for n in names:
        size = 1
        for s in shapes[n]:
            size *= s
        out[n] = flat[off:off + size].reshape(shapes[n])
        off += size
    return out


def kernel(x, norm_mix_pre, w_in, gm_ln_w, gm_ln_b, gm_w_s, gm_b_s, conv_w, conv_b, dt_bias, a_log, d_skip, ssm_norm_w, w_out, norm_mix_post, norm_ffn_pre, w_up, w_down, norm_ffn_post, loss_target, m_norm_mix_pre, m_w_in, m_gm_ln_w, m_gm_ln_b, m_gm_w_s, m_gm_b_s, m_conv_w, m_conv_b, m_dt_bias, m_a_log, m_d_skip, m_ssm_norm_w, m_w_out, m_norm_mix_post, m_norm_ffn_pre, m_w_up, m_w_down, m_norm_ffn_post, v_norm_mix_pre, v_w_in, v_gm_ln_w, v_gm_ln_b, v_gm_w_s, v_gm_b_s, v_conv_w, v_conv_b, v_dt_bias, v_a_log, v_d_skip, v_ssm_norm_w, v_w_out, v_norm_mix_post, v_norm_ffn_pre, v_w_up, v_w_down, v_norm_ffn_post):
    params = dict(norm_mix_pre=norm_mix_pre, w_in=w_in, gm_ln_w=gm_ln_w, gm_ln_b=gm_ln_b, gm_w_s=gm_w_s, gm_b_s=gm_b_s,
                  conv_w=conv_w, conv_b=conv_b, dt_bias=dt_bias, a_log=a_log, d_skip=d_skip, ssm_norm_w=ssm_norm_w,
                  w_out=w_out, norm_mix_post=norm_mix_post, norm_ffn_pre=norm_ffn_pre, w_up=w_up, w_down=w_down,
                  norm_ffn_post=norm_ffn_post)
    mom1 = dict(norm_mix_pre=m_norm_mix_pre, w_in=m_w_in, gm_ln_w=m_gm_ln_w, gm_ln_b=m_gm_ln_b, gm_w_s=m_gm_w_s,
                gm_b_s=m_gm_b_s, conv_w=m_conv_w, conv_b=m_conv_b, dt_bias=m_dt_bias, a_log=m_a_log, d_skip=m_d_skip,
                ssm_norm_w=m_ssm_norm_w, w_out=m_w_out, norm_mix_post=m_norm_mix_post, norm_ffn_pre=m_norm_ffn_pre,
                w_up=m_w_up, w_down=m_w_down, norm_ffn_post=m_norm_ffn_post)
    mom2 = dict(norm_mix_pre=v_norm_mix_pre, w_in=v_w_in, gm_ln_w=v_gm_ln_w, gm_ln_b=v_gm_ln_b, gm_w_s=v_gm_w_s,
                gm_b_s=v_gm_b_s, conv_w=v_conv_w, conv_b=v_conv_b, dt_bias=v_dt_bias, a_log=v_a_log, d_skip=v_d_skip,
                ssm_norm_w=v_ssm_norm_w, w_out=v_w_out, norm_mix_post=v_norm_mix_post, norm_ffn_pre=v_norm_ffn_pre,
                w_up=v_w_up, w_down=v_w_down, norm_ffn_post=v_norm_ffn_post)
    names = list(params)
    big = ("w_in", "w_out", "w_up", "w_down")
    chip = 2 * lax.axis_index("x") + lax.axis_index("y")

    shards = {n: params[n][0].astype(BF16) for n in big}
    conv_shard = jnp.pad(conv_w[0], ((0, 16 - CONV_K), (0, 0)))
    g_in4, g_conv4 = _run_exchange("allgather_w_in", _allgather_exchange([shards["w_in"], conv_shard]))
    conv_full = jnp.transpose(g_conv4[:, :CONV_K, :], (1, 0, 2)).reshape(CONV_K, CONV_CH)

    small = {n: params[n][0] if params[n].ndim >= 3 else params[n] for n in _SMALL_NAMES if n != "conv_w"}
    core = lax.axis_index("c").astype(jnp.int32).reshape(1)
    adam_args = {n: (params[n][0], mom1[n][0], mom2[n][0]) for n in big}
    loss, grad_x, big_out, small_sum = _forward_backward(
        x, loss_target, g_in4, conv_full, small, shards["w_out"], shards["w_up"], shards["w_down"], core, adam_args)
    grads, delta, new_m, new_v = {}, {}, {}, {}
    for n in big:
        grads[n], delta[n], new_m[n], new_v[n] = [a[None] for a in big_out[n]]

    small_sum["conv_w"] = lax.dynamic_slice_in_dim(small_sum["conv_w"], chip * (CONV_CH // N_CHIPS), CONV_CH // N_CHIPS, axis=1)

    local_shapes = {n: params[n].shape[1:] if params[n].ndim >= 3 else params[n].shape for n in _SMALL_NAMES}
    flat = lambda tree: {n: tree[n].reshape(local_shapes[n]) for n in _SMALL_NAMES}
    packed = [_pack(flat(t)) for t in (params, small_sum, mom1, mom2)]
    _, d_p, m_p, v_p = _adamw("adamw_small", *packed, packed[0].shape[0])
    for src, dst in ((d_p, delta), (m_p, new_m), (v_p, new_v)):
        for n, val in _unpack(src, local_shapes).items():
            dst[n] = val.reshape(params[n].shape)
    for n in _SMALL_NAMES:
        grads[n] = small_sum[n].reshape(params[n].shape)

    out = [loss, grad_x]
    for tree in (grads, delta, new_m, new_v):
        out += [tree[n] for n in names]
    return tuple(out)
```

```python
import functools

import jax
import jax.numpy as jnp
from jax import lax
from jax.experimental import pallas as pl
from jax.experimental.pallas import tpu as pltpu

F32 = jnp.float32
BF16 = jnp.bfloat16
MESH = pl.DeviceIdType.MESH

EPS = 1e-6
D_MODEL = 1024
GM_WIDTH = 512
SSM_WIDTH = 512
N_HEADS = 8
HEAD_DIM = 64
CHUNK = 128
SSM_GROUPS = 2
GROUP_W = SSM_WIDTH // SSM_GROUPS
SSM_STATE = 128
CONV_K = 4
CONV_CH = 1024
D_FF = 4096
IN_COLS = 2568
DT_PAD = 128
SSD_COLS = CONV_CH + SSM_WIDTH + DT_PAD
N_CHIPS = 4
N_DEV = 8

ADAM_LR = 0.001
ADAM_B1 = 0.9
ADAM_B2 = 0.999
ADAM_EPS = 1e-08
ADAM_WD = 0.01
ADAM_STEP = 10

VMEM_LIMIT_BYTES = 56 * 1024 * 1024
FF_TILE = 512
DW_TOKENS_PER_STEP = 2048


def _cparams(n_axes):
    return pltpu.CompilerParams(dimension_semantics=("arbitrary",) * n_axes, vmem_limit_bytes=VMEM_LIMIT_BYTES)


def _dot(a, b):
    return jnp.dot(a.astype(BF16), b.astype(BF16), preferred_element_type=F32)


def _dot_nt(a, b):
    return lax.dot_general(a.astype(BF16), b.astype(BF16), (((1,), (1,)), ((), ())), preferred_element_type=F32)


def _dot_tn(a, b):
    return lax.dot_general(a.astype(BF16), b.astype(BF16), (((0,), (0,)), ((), ())), preferred_element_type=F32)


def _sigmoid(x):
    return 1.0 / (1.0 + jnp.exp(-x))


_GELU_C = 0.7978845608028654
_GELU_A = 0.044715


def _gelu(x):
    t = jnp.tanh(_GELU_C * (x + _GELU_A * (x * x * x)))
    return 0.5 * x * (1.0 + t), t


def _gelu_grad(x, t):
    return 0.5 * (1.0 + t) + 0.5 * x * (1.0 - t * t) * (_GELU_C * (1.0 + 3.0 * _GELU_A * x * x))


def _rms_fwd(x, w):
    r = lax.rsqrt(jnp.mean(x * x, axis=-1, keepdims=True) + EPS)
    return x * r * w, r


def _rms_bwd(x, r, w, dy):
    g = dy * w
    dx = r * g - x * (r * r * r) * jnp.mean(g * x, axis=-1, keepdims=True)
    dw = jnp.sum(dy * x * r, axis=0, keepdims=True)
    return dx, dw


class _Carried:
    def __init__(self, ins, out_shapes, sems, start, finish):
        self.ins, self.out_shapes, self.sems = list(ins), list(out_shapes), list(sems)
        self.start, self.finish = start, finish


def _both(first, second):
    n_i, n_o, n_s = len(first.ins), len(first.out_shapes), len(first.sems)

    def split(ins, outs, sems):
        return (ins[:n_i], outs[:n_o], sems[:n_s]), (ins[n_i:], outs[n_o:], sems[n_s:])

    def start(ins, outs, *sems):
        (i1, o1, s1), (i2, o2, s2) = split(ins, outs, sems)
        first.start(i1, o1, *s1)
        second.start(i2, o2, *s2)

    def finish(ins, outs, *sems):
        (i1, o1, s1), (i2, o2, s2) = split(ins, outs, sems)
        first.finish(i1, o1, *s1)
        second.finish(i2, o2, *s2)

    return _Carried(first.ins + second.ins, first.out_shapes + second.out_shapes, first.sems + second.sems, start, finish)


def _split_carried(refs, n_in, n_out, n_scratch, carried):
    n_ci, n_co, n_cs = len(carried.ins), len(carried.out_shapes), len(carried.sems)
    ins, rest = refs[:n_in], refs[n_in:]
    c_ins, rest = rest[:n_ci], rest[n_ci:]
    outs, rest = rest[:n_out], rest[n_out:]
    c_outs, rest = rest[:n_co], rest[n_co:]
    scr, c_sems = rest[:n_scratch], rest[n_scratch:]
    assert len(c_sems) == n_cs
    return tuple(ins) + tuple(outs) + tuple(scr), c_ins, c_outs, c_sems


def _rows_call(name, body, tm, row_ins, const_ins, row_outs, acc_outs=(), scratch=(), carried=None):
    n_rows = row_ins[0].shape[0]
    assert n_rows % tm == 0
    n_steps = n_rows // tm
    n_in = len(row_ins) + len(const_ins)
    n_ro = len(row_outs)
    n_acc = len(acc_outs)

    def kern(*refs):
        accs = refs[n_in + n_ro:n_in + n_ro + n_acc]

        @pl.when(pl.program_id(0) == 0)
        def _():
            for a in accs:
                a[...] = jnp.zeros_like(a)

        body(*refs)

    def whole(shape):
        nd = len(shape)
        return pl.BlockSpec(tuple(shape), lambda i: (0,) * nd)

    in_specs = [pl.BlockSpec((tm, a.shape[1]), lambda i: (i, 0)) for a in row_ins]
    in_specs += [whole(a.shape) for a in const_ins]
    out_specs = [pl.BlockSpec((tm, s.shape[1]), lambda i: (i, 0)) for s in row_outs]
    out_specs += [whole(s.shape) for s in acc_outs]
    return _call_carrying(
        kern, carried, name=name, grid=(n_steps,), in_specs=in_specs, out_specs=out_specs,
        out_shape=tuple(row_outs) + tuple(acc_outs), scratch_shapes=list(scratch), operands=list(row_ins) + list(const_ins))


def _call_carrying(body, carried, *, name, grid, in_specs, out_specs, out_shape, scratch_shapes, operands):
    n_in, n_out, n_scratch = len(in_specs), len(out_specs), len(scratch_shapes)
    kern = body
    if carried is not None:
        def kern(*refs):
            plain, c_ins, c_outs, c_sems = _split_carried(refs, n_in, n_out, n_scratch, carried)
            first, last = True, True
            for d, size in enumerate(grid):
                first = jnp.logical_and(first, pl.program_id(d) == 0)
                last = jnp.logical_and(last, pl.program_id(d) == size - 1)

            @pl.when(first)
            def _():
                carried.start(c_ins, c_outs, *c_sems)

            body(*plain)

            @pl.when(last)
            def _():
                carried.finish(c_ins, c_outs, *c_sems)

        in_specs = list(in_specs) + [_HBM] * len(carried.ins)
        out_specs = list(out_specs) + [_HBM] * len(carried.out_shapes)
        out_shape = tuple(out_shape) + tuple(carried.out_shapes)
        operands = list(operands) + carried.ins
        scratch_shapes = list(scratch_shapes) + carried.sems
    return pl.pallas_call(
        kern, name=name, grid=grid, in_specs=in_specs, out_specs=out_specs, out_shape=out_shape,
        scratch_shapes=scratch_shapes, compiler_params=_cparams(len(grid)),
    )(*operands)


def _sds(shape, dtype):
    return jax.ShapeDtypeStruct(tuple(shape), dtype)


def _matmul_tn(name, a, b, tm, tn, tk, stacked=False, carried=None):
    k_dim, m_dim = a.shape
    n_dim = b.shape[1]
    assert m_dim % tm == 0 and n_dim % tn == 0 and k_dim % tk == 0
    nk = k_dim // tk

    def kern(a_ref, b_ref, o_ref, acc_ref):
        k = pl.program_id(2)
        prod = _dot_tn(a_ref[...], b_ref[...])

        @pl.when(k == 0)
        def _():
            acc_ref[...] = prod

        @pl.when(k > 0)
        def _():
            acc_ref[...] += prod

        @pl.when(k == nk - 1)
        def _():
            o_ref[...] = acc_ref[...].astype(o_ref.dtype)

    if stacked:
        assert tm == m_dim
        out_shape = _sds((n_dim // tn, m_dim, tn), BF16)
        out_spec = pl.BlockSpec((None, tm, tn), lambda i, j, k: (j, i, 0))
    else:
        out_shape = _sds((m_dim, n_dim), BF16)
        out_spec = pl.BlockSpec((tm, tn), lambda i, j, k: (i, j))
    outs = _call_carrying(
        kern, carried, name=name, grid=(m_dim // tm, n_dim // tn, nk),
        in_specs=[pl.BlockSpec((tk, tm), lambda i, j, k: (k, i)), pl.BlockSpec((tk, tn), lambda i, j, k: (k, j))],
        out_specs=[out_spec], out_shape=(out_shape,), scratch_shapes=[pltpu.VMEM((tm, tn), F32)], operands=[a, b])
    return outs[0] if carried is None else outs


def _inproj_fwd(x, nw, w_uv, w_xbc, w_z, w_dt, tm=256, carried=None):
    n_tok = x.shape[0]

    def body(x_ref, nw_ref, wuv_ref, wxbc_ref, wz_ref, wdt_ref, puv_ref, pxbc_ref, pz_ref, pdt_ref):
        h, _ = _rms_fwd(x_ref[...], nw_ref[...])
        h = h.astype(BF16)
        puv_ref[...] = jnp.dot(h, wuv_ref[...], preferred_element_type=F32)
        pxbc_ref[...] = jnp.dot(h, wxbc_ref[...], preferred_element_type=F32)
        pz_ref[...] = jnp.dot(h, wz_ref[...], preferred_element_type=F32)
        pdt_ref[...] = jnp.dot(h, wdt_ref[...], preferred_element_type=F32)

    return _rows_call(
        "inproj_fwd", body, tm, [x], [nw, w_uv, w_xbc, w_z, w_dt],
        [_sds((n_tok, 2 * GM_WIDTH), F32), _sds((n_tok, CONV_CH), F32), _sds((n_tok, SSM_WIDTH), F32),
         _sds((n_tok, DT_PAD), F32)], carried=carried)


def _head_lane_mask(width, head):
    lane = lax.broadcasted_iota(jnp.int32, (1, width), 1)
    return (lane // HEAD_DIM) == head


def _split_terms(x, terms):
    parts = []
    for _ in range(terms):
        p = x.astype(BF16)
        parts.append(p)
        x = x - p.astype(F32)
    return parts


def _seg_dots(vals, ind, terms=2):
    m = vals[0].shape[0]
    parts = []
    for v in vals:
        parts += _split_terms(v, terms)
    red = jnp.dot(jnp.concatenate(parts, axis=0), ind, preferred_element_type=F32)
    outs = []
    for i in range(len(vals)):
        acc = red[i * terms * m:(i * terms + 1) * m]
        for t in range(1, terms):
            acc = acc + red[(i * terms + t) * m:(i * terms + t + 1) * m]
        outs.append(acc)
    return outs


def _tri_dot(mask, x, terms=3):
    n = x.shape[1]
    red = jnp.dot(mask.astype(BF16), jnp.concatenate(_split_terms(x, terms), axis=1), preferred_element_type=F32)
    acc = red[:, :n]
    for t in range(1, terms):
        acc = acc + red[:, t * n:(t + 1) * n]
    return acc


def _gmlp_common(puv, lnw, lnb, e_bf, et_bf):
    u = puv[:, :GM_WIDTH]
    v = puv[:, GM_WIDTH:]
    gu, tu = _gelu(u)
    gv, tv = _gelu(v)
    (s1,) = _seg_dots([gv], et_bf)
    (mu,) = _seg_dots([s1 * (1.0 / HEAD_DIM)], e_bf)
    xc = gv - mu
    (s2,) = _seg_dots([xc * xc], et_bf)
    (rstd,) = _seg_dots([lax.rsqrt(s2 * (1.0 / HEAD_DIM) + EPS)], e_bf)
    xhat = xc * rstd
    vn = xhat * lnw + lnb
    return u, v, gu, tu, tv, rstd, xhat, vn


def _tril_mask():
    r = lax.broadcasted_iota(jnp.int32, (CHUNK, CHUNK), 0)
    c = lax.broadcasted_iota(jnp.int32, (CHUNK, CHUNK), 1)
    return r >= c


def _head_blocks(v):
    return jnp.concatenate([jnp.where(_head_lane_mask(GM_WIDTH, h), v, jnp.zeros_like(v)) for h in range(N_HEADS)], axis=0)


def _causal_w_cat(w_cat):
    t = lax.broadcasted_iota(jnp.int32, (CHUNK, N_HEADS * CHUNK), 0)
    s = lax.broadcasted_iota(jnp.int32, (CHUNK, N_HEADS * CHUNK), 1) % CHUNK
    return jnp.where(t >= s, w_cat, 0.0).astype(BF16)


def _gmlp_chunk_fwd(puv, lnw, lnb, e_bf, et_bf, wm, bmap):
    _, _, gu, _, _, _, _, vn = _gmlp_common(puv, lnw, lnb, e_bf, et_bf)
    mixed = jnp.dot(wm, _head_blocks(vn.astype(BF16)), preferred_element_type=F32) + bmap
    return (gu * mixed).astype(BF16)


SUBLANES = 8


def _shift_down(x, tail, s):
    main = pltpu.roll(x, s, 0)
    row = lax.broadcasted_iota(jnp.int32, (SUBLANES, 1), 0)
    head = jnp.where(row < s, pltpu.roll(tail, s, 0), main[:SUBLANES])
    return jnp.concatenate([head, main[SUBLANES:]], axis=0)


def _shift_up(x, head_next, s):
    n = x.shape[0]
    main = pltpu.roll(x, n - s, 0)
    row = lax.broadcasted_iota(jnp.int32, (SUBLANES, 1), 0)
    last = jnp.where(row >= SUBLANES - s, pltpu.roll(head_next, SUBLANES - s, 0), main[n - SUBLANES:])
    return jnp.concatenate([main[:n - SUBLANES], last], axis=0)


def _ssd_pre(xr, tail, cw_ref, cb, pdt, dtb, alog, emap):
    rowi = lax.broadcasted_iota(jnp.int32, (CHUNK, 1), 0)
    shifted = [_shift_down(xr, tail, 3), _shift_down(xr, tail, 2), _shift_down(xr, tail, 1), xr]
    xc = cb
    for k in range(CONV_K):
        xc = xc + cw_ref[k] * shifted[k]
    sg = _sigmoid(xc)
    xa = xc * sg
    pre = pdt + dtb
    dt = jnp.maximum(pre, 0.0) + jnp.log(1.0 + jnp.exp(-jnp.abs(pre)))
    a_neg = -jnp.exp(alog)
    a_cs = _tri_dot(_tril_mask(), dt * a_neg)
    acs_map, dt_map = _seg_dots([a_cs, dt], emap, terms=3)
    return dict(shifted=shifted, xc=xc, sg=sg, xa=xa, pre=pre, dt=dt, a_neg=a_neg, a_cs=a_cs,
                acs_map=acs_map, dt_map=dt_map, rowi=rowi)


def _ssd_maps(p):
    last = p["rowi"] == CHUNK - 1
    aq_map = jnp.sum(jnp.where(last, p["acs_map"], 0.0), axis=0, keepdims=True)
    e_exp = jnp.exp(p["acs_map"])
    dte = jnp.exp(aq_map - p["acs_map"])
    cd = jnp.exp(aq_map)
    return last, e_exp, dte, cd


def _head_decay(a_cs, a_cs_t, head, tri):
    lane = lax.broadcasted_iota(jnp.int32, (1, DT_PAD), 1)
    sub = lax.broadcasted_iota(jnp.int32, (DT_PAD, 1), 0)
    col = jnp.sum(jnp.where(lane == head, a_cs, 0.0), axis=1, keepdims=True)
    row = jnp.sum(jnp.where(sub == head, a_cs_t, 0.0), axis=0, keepdims=True)
    return jnp.exp(jnp.where(tri, col - row, -1e30))


def _gate_fwd(y, z, nw):
    sz = _sigmoid(z)
    zg = z * sz
    yg = y * zg
    outs, rs = [], []
    for g in range(SSM_GROUPS):
        gs = slice(g * GROUP_W, (g + 1) * GROUP_W)
        o, r = _rms_fwd(yg[:, gs], nw[:, gs])
        outs.append(o)
        rs.append(r)
    return sz, zg, yg, outs, rs


def _ssd_const_specs():
    def whole(shape):
        nd = len(shape)
        return pl.BlockSpec(tuple(shape), lambda c: (0,) * nd)
    return [whole((CONV_K, 1, CONV_CH)), whole((1, CONV_CH)), whole((1, DT_PAD)), whole((1, DT_PAD)),
            whole((1, SSM_WIDTH)), whole((1, SSM_WIDTH)), whole((DT_PAD, SSM_WIDTH)), whole((SSM_WIDTH, DT_PAD))]


def _mixer_fwd(p_uv, p_xbc, p_z, p_dt, x, lnw, lnb, w_cat, bmap, w_out, nw_post, nw_pre2, conv_w, conv_b, dt_bias, a_log,
               dskip_map, norm_w, e_bf, et_bf, n_seq, carried=None):
    n_tok = p_xbc.shape[0]
    nc = n_tok // n_seq // CHUNK

    def body(puv3, xr3, z3, pdt3, x3, lnw_ref, lnb_ref, wcat_ref, bmap_ref, wo_ref, nwa_ref, nwb_ref,
             cw_ref, cb_ref, dtb_ref, alog_ref, dsk_ref, nw_ref, e_ref, et_ref,
             mix3, yssd3, sprev3, o3, x13, h23, wm_scr, prev3_scr, s3_scr):
        @pl.when(pl.program_id(0) == 0)
        def _():
            wm_scr[...] = _causal_w_cat(wcat_ref[...])
            prev3_scr[...] = jnp.zeros_like(prev3_scr)
            s3_scr[...] = jnp.zeros_like(s3_scr)

        for b in range(n_seq):
            one_sequence(puv3.at[b], xr3.at[b], z3.at[b], pdt3.at[b], lnw_ref, lnb_ref, bmap_ref,
                         cw_ref, cb_ref, dtb_ref, alog_ref, dsk_ref, nw_ref, e_ref, et_ref,
                         mix3.at[b], yssd3.at[b], sprev3.at[b], wm_scr, prev3_scr.at[b], s3_scr.at[b])
            o = jnp.dot(mix3[b], wo_ref[...], preferred_element_type=F32)
            on, _ = _rms_fwd(o, nwa_ref[...])
            x1 = x3[b] + on
            h2, _ = _rms_fwd(x1, nwb_ref[...])
            o3[b] = o
            x13[b] = x1
            h23[b] = h2.astype(BF16)

    def one_sequence(puv_ref, xr_ref, z_ref, pdt_ref, lnw_ref, lnb_ref, bmap_ref,
                     cw_ref, cb_ref, dtb_ref, alog_ref, dsk_ref, nw_ref, e_ref, et_ref,
                     mix_ref, yssd_ref, sprev_ref, wm_scr, prev_scr, s_scr):
        mix_ref[:, :GM_WIDTH] = _gmlp_chunk_fwd(puv_ref[...], lnw_ref[...], lnb_ref[...], e_ref[...], et_ref[...], wm_scr[...],
                                      bmap_ref[...])
        xr = xr_ref[...]
        p = _ssd_pre(xr, prev_scr[...], cw_ref, cb_ref[...], pdt_ref[...], dtb_ref[...], alog_ref[...], e_ref[...])
        _, e_exp, dte, cd = _ssd_maps(p)
        xs = p["xa"][:, :SSM_WIDTH]
        xd = xs * p["dt_map"]
        a_cs_t = p["a_cs"].T
        tri = _tril_mask()
        s_old = s_scr[...]
        sprev_ref[...] = s_old
        for g in range(SSM_GROUPS):
            gs = slice(g * GROUP_W, (g + 1) * GROUP_W)
            bm = p["xa"][:, SSM_WIDTH + g * SSM_STATE: SSM_WIDTH + (g + 1) * SSM_STATE].astype(BF16)
            cm = p["xa"][:, SSM_WIDTH + (SSM_GROUPS + g) * SSM_STATE: SSM_WIDTH + (SSM_GROUPS + g + 1) * SSM_STATE].astype(BF16)
            cb_mat = _dot_nt(cm, bm)
            xdg = xd[:, gs].astype(BF16)
            y_g = _dot(cm, s_old[:, gs]) * e_exp[:, gs] + dsk_ref[:, gs] * xs[:, gs]
            for r in range(SSM_GROUPS * 2):
                dm = _head_decay(p["a_cs"], a_cs_t, g * 4 + r, tri)
                full = jnp.dot((cb_mat * dm).astype(BF16), xdg, preferred_element_type=F32)
                y_g = y_g + jnp.where(_head_lane_mask(GROUP_W, r), full, 0.0)
            yssd_ref[:, gs] = y_g
            s_scr[:, gs] = cd[:, gs] * s_old[:, gs] + _dot_tn(bm, xd[:, gs] * dte[:, gs])
        _, _, _, outs, _ = _gate_fwd(yssd_ref[...], z_ref[...], nw_ref[...])
        for g in range(SSM_GROUPS):
            mix_ref[:, GM_WIDTH + g * GROUP_W:GM_WIDTH + (g + 1) * GROUP_W] = outs[g].astype(BF16)
        prev_scr[...] = xr[CHUNK - SUBLANES:, :]

    seq_len = n_tok // n_seq

    def rows(width):
        return pl.BlockSpec((n_seq, CHUNK, width), lambda c: (0, c, 0))

    def whole(shape):
        nd = len(shape)
        return pl.BlockSpec(tuple(shape), lambda c: (0,) * nd)

    def by_seq(a):
        return a.reshape(n_seq, seq_len, a.shape[-1])

    outs = _call_carrying(
        body, carried, name="mixer_fwd", grid=(nc,),
        in_specs=[rows(2 * GM_WIDTH), rows(CONV_CH), rows(SSM_WIDTH), rows(DT_PAD), rows(D_MODEL), whole(lnw.shape),
                  whole(lnb.shape), whole(w_cat.shape), whole(bmap.shape), whole(w_out.shape), whole(nw_post.shape),
                  whole(nw_pre2.shape)] + _ssd_const_specs(),
        out_specs=[rows(D_MODEL), rows(SSM_WIDTH), rows(SSM_WIDTH), rows(D_MODEL), rows(D_MODEL), rows(D_MODEL)],
        out_shape=(_sds((n_seq, seq_len, D_MODEL), BF16),
                   _sds((n_seq, seq_len, SSM_WIDTH), F32), _sds((n_seq, seq_len, SSM_WIDTH), F32),
                   _sds((n_seq, seq_len, D_MODEL), F32), _sds((n_seq, seq_len, D_MODEL), F32),
                   _sds((n_seq, seq_len, D_MODEL), BF16)),
        scratch_shapes=[pltpu.VMEM((CHUNK, N_HEADS * CHUNK), BF16), pltpu.VMEM((n_seq, SUBLANES, CONV_CH), F32),
                        pltpu.VMEM((n_seq, SSM_STATE, SSM_WIDTH), F32)],
        operands=[by_seq(p_uv), by_seq(p_xbc), by_seq(p_z), by_seq(p_dt), by_seq(x), lnw, lnb, w_cat, bmap, w_out, nw_post,
                  nw_pre2, conv_w, conv_b, dt_bias, a_log, dskip_map, norm_w, e_bf, et_bf])
    return tuple(o.reshape(n_tok, o.shape[-1]) for o in outs[:6]) + tuple(outs[6:])


def _up_cols(wup_ref, j):
    per = (D_FF // N_CHIPS) // FF_TILE
    return wup_ref[j // per, :, (j % per) * FF_TILE:(j % per + 1) * FF_TILE]


def _down_rows(wda_ref, wdb_ref, j):
    assert 2 * FF_TILE == D_FF // N_CHIPS
    return (wda_ref if j % 2 == 0 else wdb_ref)[j // 2]


def _skewed_rows_call(name, main, tail, tm, lead_ins, lag_ins, const_ins, lead_outs, lag_outs, acc_outs, carry,
                      streamed, tile_copies, n_copies):
    n_rows = lead_ins[0].shape[0]
    assert n_rows % tm == 0
    n = n_rows // tm
    counts = [len(lead_ins), len(lag_ins), len(const_ins), len(streamed), len(lead_outs), len(lag_outs), len(acc_outs),
              1, len(streamed)]

    def kern(*refs):
        groups, pos = [], 0
        for cnt in counts:
            groups.append(refs[pos:pos + cnt])
            pos += cnt
        lead_i, lag_i, consts, w_hbm, lead_o, lag_o, accs, (carry_scr,), w_vmem = groups
        sems = refs[pos]
        i = pl.program_id(0)
        pieces, k = [], 0
        for piece in tile_copies(w_hbm, w_vmem):
            pieces.append([pltpu.make_async_copy(src, dst, sems.at[k + q]) for q, (src, dst) in enumerate(piece)])
            k += len(piece)

        def ready(j):
            for cp in pieces[j]:
                cp.wait()

        @pl.when(i == 0)
        def _():
            for piece in pieces:
                for cp in piece:
                    cp.start()
            for a in accs:
                a[...] = jnp.zeros_like(a)
            carry_scr[...] = main(lead_i, consts, lead_o, w_vmem, ready)

        @pl.when(jnp.logical_and(i > 0, i < n))
        def _():
            previous = carry_scr[...]
            carry_scr[...] = main(lead_i, consts, lead_o, w_vmem, lambda j: None)
            tail(previous, lag_i, consts, lag_o, accs)

        @pl.when(i == n)
        def _():
            tail(carry_scr[...], lag_i, consts, lag_o, accs)

    def lead(width):
        return pl.BlockSpec((tm, width), lambda i: (jnp.minimum(i, n - 1), 0))

    def lag(width):
        return pl.BlockSpec((tm, width), lambda i: (jnp.maximum(i - 1, 0), 0))

    def whole(shape):
        nd = len(shape)
        return pl.BlockSpec(tuple(shape), lambda i: (0,) * nd)

    return pl.pallas_call(
        kern, name=name, grid=(n + 1,),
        in_specs=([lead(a.shape[1]) for a in lead_ins] + [lag(a.shape[1]) for a in lag_ins]
                  + [whole(a.shape) for a in const_ins] + [_HBM] * len(streamed)),
        out_specs=[lead(s.shape[1]) for s in lead_outs] + [lag(s.shape[1]) for s in lag_outs] + [whole(s.shape) for s in acc_outs],
        out_shape=tuple(lead_outs) + tuple(lag_outs) + tuple(acc_outs),
        scratch_shapes=([pltpu.VMEM(carry, F32)] + [pltpu.VMEM(a.shape, a.dtype) for a in streamed]
                        + [pltpu.SemaphoreType.DMA((n_copies,))]),
        compiler_params=_cparams(1),
    )(*lead_ins, *lag_ins, *const_ins, *streamed)


def _mlp_weight_pieces(order):
    per = (D_FF // N_CHIPS) // FF_TILE

    def tile_copies(hbm, vmem):
        pieces = []
        for j in range(D_FF // FF_TILE):
            cols = (j // per, slice(None), pl.ds((j % per) * FF_TILE, FF_TILE))
            up = (hbm[0].at[cols], vmem[0].at[cols])
            down = (hbm[1 + j % 2].at[j // 2], vmem[1 + j % 2].at[j // 2])
            pieces.append([up, down] if order == "up_down" else [down, up])
        return pieces

    return tile_copies


def _mlp_fwd(h2, x1, tgt, w_up, w_down_a, w_down_b, nw, tm=512):
    n_tok = x1.shape[0]

    def main(lead_i, consts, lead_o, weights, ready):
        (h2_ref,), (f_ref,), (wup_ref, wda_ref, wdb_ref) = lead_i, lead_o, weights
        h2v = h2_ref[...]
        acc = jnp.zeros((tm, D_MODEL), F32)
        for j in range(D_FF // FF_TILE):
            cs = slice(j * FF_TILE, (j + 1) * FF_TILE)
            ready(j)
            u = jnp.dot(h2v, _up_cols(wup_ref, j), preferred_element_type=F32)
            f = jnp.square(jnp.maximum(u, 0.0)).astype(BF16)
            f_ref[:, cs] = f
            acc = acc + jnp.dot(f, _down_rows(wda_ref, wdb_ref, j), preferred_element_type=F32)
        return acc

    def tail(acc, lag_i, consts, lag_o, accs):
        (x1_ref, tgt_ref), (nw_ref,), (dd_ref, dy_ref), (loss_ref, dnw_ref) = lag_i, consts, lag_o, accs
        dn, r = _rms_fwd(acc, nw_ref[...])
        e = x1_ref[...] + dn - tgt_ref[...]
        loss_ref[...] += jnp.full(loss_ref.shape, (0.5 / D_MODEL) * jnp.sum(e * e), F32)
        dy = e * (1.0 / D_MODEL)
        dd, dnw = _rms_bwd(acc, r, nw_ref[...], dy)
        dy_ref[...] = dy
        dd_ref[...] = dd.astype(BF16)
        dnw_ref[...] += dnw

    return _skewed_rows_call(
        "mlp_fwd", main, tail, tm, [h2], [x1, tgt], [nw],
        [_sds((n_tok, D_FF), BF16)], [_sds((n_tok, D_MODEL), BF16), _sds((n_tok, D_MODEL), F32)],
        [_sds((8, 128), F32), _sds((1, D_MODEL), F32)], carry=(tm, D_MODEL),
        streamed=[w_up, w_down_a, w_down_b], tile_copies=_mlp_weight_pieces("up_down"), n_copies=2 * (D_FF // FF_TILE))


def _mlp_bwd(dd, f, x1, dy, w_down_a, w_down_b, w_up, nw, tm=256):
    n_tok = x1.shape[0]

    def main(lead_i, consts, lead_o, weights, ready):
        (dd_ref, f_ref), (dup_ref,), (wup_ref, wda_ref, wdb_ref) = lead_i, lead_o, weights
        ddv = dd_ref[...]
        acc = jnp.zeros((tm, D_MODEL), F32)
        for j in range(D_FF // FF_TILE):
            cs = slice(j * FF_TILE, (j + 1) * FF_TILE)
            ready(j)
            df = _dot_nt(ddv, _down_rows(wda_ref, wdb_ref, j))
            du = (df * (2.0 * jnp.sqrt(f_ref[:, cs].astype(F32)))).astype(BF16)
            dup_ref[:, cs] = du
            acc = acc + _dot_nt(du, _up_cols(wup_ref, j))
        return acc

    def tail(acc, lag_i, consts, lag_o, accs):
        (x1_ref, dy_ref), (nw_ref,), (dx1_ref,), (dnw_ref,) = lag_i, consts, lag_o, accs
        x1v = x1_ref[...]
        _, r = _rms_fwd(x1v, nw_ref[...])
        dx, dnw = _rms_bwd(x1v, r, nw_ref[...], acc)
        dx1_ref[...] = dy_ref[...] + dx
        dnw_ref[...] += dnw

    return _skewed_rows_call(
        "mlp_bwd", main, tail, tm, [dd, f], [x1, dy], [nw],
        [_sds((n_tok, D_FF), BF16)], [_sds((n_tok, D_MODEL), F32)], [_sds((1, D_MODEL), F32)], carry=(tm, D_MODEL),
        streamed=[w_up, w_down_a, w_down_b], tile_copies=_mlp_weight_pieces("down_up"), n_copies=2 * (D_FF // FF_TILE))


def _outproj_bwd(dx1, o, w_out, nw, tm=256, carried=None):
    n_tok = dx1.shape[0]

    def body(dx1_ref, o_ref, wo_ref, nw_ref, do_ref, dya_ref, dyb_ref, dnw_ref):
        ov = o_ref[...]
        _, r = _rms_fwd(ov, nw_ref[...])
        do, dnw = _rms_bwd(ov, r, nw_ref[...], dx1_ref[...])
        dob = do.astype(BF16)
        do_ref[...] = dob
        dya_ref[...] = _dot_nt(dob, wo_ref[:GM_WIDTH, :])
        dyb_ref[...] = _dot_nt(dob, wo_ref[GM_WIDTH:, :])
        dnw_ref[...] += dnw

    return _rows_call("outproj_bwd", body, tm, [dx1, o], [w_out, nw],
                      [_sds((n_tok, D_MODEL), BF16), _sds((n_tok, GM_WIDTH), F32), _sds((n_tok, SSM_WIDTH), F32)],
                      [_sds((1, D_MODEL), F32)], carried=carried)


def _gmlp_bwd(p_uv, dya, lnw, lnb, e_bf, et_bf, w_cat, w_stack, bmap, carried=None):
    n_tok = p_uv.shape[0]
    chunks_per_step = 2

    def body(puv_ref, dya_ref, lnw_ref, lnb_ref, e_ref, et_ref, wcat_ref, wstack_ref, bmap_ref,
             dpuv_ref, dws_ref, dbs_ref, dlnw_ref, dlnb_ref, wm_scr, wsm_scr):
        t_stk = lax.broadcasted_iota(jnp.int32, (N_HEADS * CHUNK, CHUNK), 0) % CHUNK
        s_stk = lax.broadcasted_iota(jnp.int32, (N_HEADS * CHUNK, CHUNK), 1)

        @pl.when(pl.program_id(0) == 0)
        def _():
            wm_scr[...] = _causal_w_cat(wcat_ref[...])
            wsm_scr[...] = jnp.where(t_stk >= s_stk, wstack_ref[...], 0.0).astype(BF16)

        lnw_v = lnw_ref[...]
        e_v, et_v = e_ref[...], et_ref[...]

        def one_chunk(rows):
            u, v, gu, tu, tv, rstd, xhat, vn = _gmlp_common(puv_ref[rows, :], lnw_v, lnb_ref[...], e_v, et_v)
            vnb = vn.astype(BF16)
            mixed = jnp.dot(wm_scr[...], _head_blocks(vnb), preferred_element_type=F32) + bmap_ref[...]
            dy = dya_ref[rows, :]
            du = dy * mixed * _gelu_grad(u, tu)
            dmixed = dy * gu
            (dbs,) = _seg_dots([dmixed], et_v)
            dblocks = _head_blocks(dmixed.astype(BF16))
            dvn = lax.dot_general(wsm_scr[...], dblocks, (((0,), (0,)), ((), ())), preferred_element_type=F32)
            dws = lax.dot_general(dblocks, vnb, (((1,), (1,)), ((), ())), preferred_element_type=F32)
            dxh = dvn * lnw_v
            m1, m2 = _seg_dots([dxh, dxh * xhat], et_v)
            m1, m2 = _seg_dots([m1 * (1.0 / HEAD_DIM), m2 * (1.0 / HEAD_DIM)], e_v)
            dgv = rstd * (dxh - m1 - xhat * m2)
            dv = dgv * _gelu_grad(v, tv)
            dpuv_ref[rows, :GM_WIDTH] = du.astype(BF16)
            dpuv_ref[rows, GM_WIDTH:] = dv.astype(BF16)
            return dbs, dws, jnp.sum(dvn * xhat, axis=0, keepdims=True), jnp.sum(dvn, axis=0, keepdims=True)

        parts = [one_chunk(slice(k * CHUNK, (k + 1) * CHUNK)) for k in range(chunks_per_step)]
        dbs, dws, dlnw, dlnb = [functools.reduce(lambda a, b: a + b, vals) for vals in zip(*parts)]
        dbs_ref[...] += dbs
        dws_ref[...] += jnp.where(t_stk >= s_stk, dws, 0.0)
        dlnw_ref[...] += dlnw
        dlnb_ref[...] += dlnb

    return _rows_call(
        "gmlp_bwd", body, chunks_per_step * CHUNK, [p_uv, dya], [lnw, lnb, e_bf, et_bf, w_cat, w_stack, bmap],
        [_sds((n_tok, 2 * GM_WIDTH), BF16)],
        [_sds((N_HEADS * CHUNK, CHUNK), F32), _sds((CHUNK, DT_PAD), F32), _sds((1, GM_WIDTH), F32),
         _sds((1, GM_WIDTH), F32)],
        scratch=[pltpu.VMEM((CHUNK, N_HEADS * CHUNK), BF16), pltpu.VMEM((N_HEADS * CHUNK, CHUNK), BF16)],
        carried=carried)


def _ssd_bwd(p_xbc, p_z, p_dt, yssd, sprev, dyb, conv_w, conv_b, dt_bias, a_log, dskip_map, norm_w, e_bf, et_bf, n_seq,
             carried=None):
    n_tok = p_xbc.shape[0]
    nc = n_tok // n_seq // CHUNK

    def body(xr3, xprev3, z3, pdt3, yssd3, sprev3, dyb3,
             cw_ref, cb_ref, dtb_ref, alog_ref, dsk_ref, nw_ref, e_ref, et_ref,
             dps3, dcw_ref, dcb_ref, ddtb_ref, dalog_ref, ddsk_ref, dnw_ref,
             ds3_scr, nxt3_scr, dxa3_scr):
        @pl.when(pl.program_id(0) == 0)
        def _():
            for a in (dcw_ref, dcb_ref, ddtb_ref, dalog_ref, ddsk_ref, dnw_ref, ds3_scr, nxt3_scr):
                a[...] = jnp.zeros_like(a)

        for b in range(n_seq):
            one_sequence(xr3.at[b], xprev3.at[b], z3.at[b], pdt3.at[b], yssd3.at[b], sprev3.at[b], dyb3.at[b],
                         cw_ref, cb_ref, dtb_ref, alog_ref, dsk_ref, nw_ref, e_ref, et_ref,
                         dps3.at[b], dcw_ref, dcb_ref, ddtb_ref, dalog_ref, ddsk_ref, dnw_ref,
                         ds3_scr.at[b], nxt3_scr.at[b], dxa3_scr.at[b])

    def one_sequence(xr_ref, xprev_ref, z_ref, pdt_ref, yssd_ref, sprev_ref, dyb_ref,
                     cw_ref, cb_ref, dtb_ref, alog_ref, dsk_ref, nw_ref, e_ref, et_ref,
                     dps_ref, dcw_ref, dcb_ref, ddtb_ref, dalog_ref, ddsk_ref, dnw_ref,
                     ds_scr, nxt_scr, dxa_scr):
        chunk = nc - 1 - pl.program_id(0)
        xr = xr_ref[...]
        prev = jnp.where(chunk == 0, 0.0, xprev_ref[...])
        et_v = et_ref[...]
        p = _ssd_pre(xr, prev, cw_ref, cb_ref[...], pdt_ref[...], dtb_ref[...], alog_ref[...], e_ref[...])
        last, e_exp, dte, cd = _ssd_maps(p)
        rowi = p["rowi"]
        xs = p["xa"][:, :SSM_WIDTH]
        xd = xs * p["dt_map"]
        a_cs_t = p["a_cs"].T
        tri = _tril_mask()
        dsk = dsk_ref[...]
        nw_v = nw_ref[...]

        yv = yssd_ref[...]
        zv = z_ref[...]
        sz, zg, yg, _, rs = _gate_fwd(yv, zv, nw_v)
        dout = dyb_ref[...]
        for g in range(SSM_GROUPS):
            gs = slice(g * GROUP_W, (g + 1) * GROUP_W)
            dyg_g, dnw_g = _rms_bwd(yg[:, gs], rs[g], nw_v[:, gs], dout[:, gs])
            dnw_ref[:, gs] += dnw_g
            dxa_scr[:, gs] = dyg_g
        dyg = dxa_scr[:, :SSM_WIDTH]
        d_y = dyg * zg
        dps_ref[:, CONV_CH:CONV_CH + SSM_WIDTH] = (dyg * yv * (sz + zv * sz * (1.0 - sz))).astype(BF16)

        s_prev = sprev_ref[...]
        ds_next = ds_scr[...]
        lane_dt = lax.broadcasted_iota(jnp.int32, (1, DT_PAD), 1)
        da_cols = jnp.zeros((CHUNK, DT_PAD), F32)
        for g in range(SSM_GROUPS):
            gs = slice(g * GROUP_W, (g + 1) * GROUP_W)
            b_off = SSM_WIDTH + g * SSM_STATE
            c_off = SSM_WIDTH + (SSM_GROUPS + g) * SSM_STATE
            bm = p["xa"][:, b_off:b_off + SSM_STATE].astype(BF16)
            cm = p["xa"][:, c_off:c_off + SSM_STATE].astype(BF16)
            cb_mat = _dot_nt(cm, bm)
            d_yg = d_y[:, gs]
            d_ygb = d_yg.astype(BF16)
            xdg = xd[:, gs]
            xdgb = xdg.astype(BF16)
            ds_g = ds_next[:, gs]
            sp_g = s_prev[:, gs]
            bds = _dot(bm, ds_g)
            dcs = d_yg * e_exp[:, gs]
            d_c = _dot_nt(dcs, sp_g)
            ds_scr[:, gs] = cd[:, gs] * ds_g + _dot_tn(cm, dcs)
            d_b = _dot_nt(xdg * dte[:, gs], ds_g)
            dxd_g = bds * dte[:, gs]
            sum_dcb = jnp.zeros((CHUNK, CHUNK), F32)
            for r in range(SSM_GROUPS * 2):
                head = g * 4 + r
                mask = _head_lane_mask(GROUP_W, r)
                dm = _head_decay(p["a_cs"], a_cs_t, head, tri)
                m_mat = cb_mat * dm
                g_mat = _dot_nt(jnp.where(mask, d_yg, 0.0), xdgb)
                w_mat = g_mat * m_mat
                sum_dcb = sum_dcb + g_mat * dm
                dxd_g = dxd_g + jnp.where(mask, _dot_tn(m_mat, d_ygb), 0.0)
                da_h = jnp.sum(w_mat - w_mat.T, axis=1, keepdims=True)
                da_cols = da_cols + jnp.where(lane_dt == head, da_h, 0.0)
            d_c = d_c + _dot(sum_dcb, bm)
            d_b = d_b + _dot_tn(sum_dcb, cm)
            dxa_scr[:, b_off:b_off + SSM_STATE] = d_b
            dxa_scr[:, c_off:c_off + SSM_STATE] = d_c
            y_off_g = _dot(cm, sp_g) * e_exp[:, gs]
            t3 = bds * xdg * dte[:, gs]
            tail = jnp.sum(t3, axis=0, keepdims=True) + jnp.sum(ds_g * sp_g, axis=0, keepdims=True) * cd[:, gs]
            pre_g = d_yg * y_off_g - t3 + jnp.where(last, tail, 0.0)
            s_pre, ddt_g, s_dsk = _seg_dots([pre_g, dxd_g * xs[:, gs], d_yg * xs[:, gs]], et_v[gs, :])
            da_cols = da_cols + s_pre
            ddsk_ref[...] += jnp.sum(s_dsk, axis=0, keepdims=True)
            dxa_scr[:, gs] = dxd_g * p["dt_map"][:, gs] + dsk[:, gs] * d_yg
            if g == 0:
                ddt = ddt_g
            else:
                ddt = ddt + ddt_g
        r_i = lax.broadcasted_iota(jnp.int32, (CHUNK, CHUNK), 0)
        c_i = lax.broadcasted_iota(jnp.int32, (CHUNK, CHUNK), 1)
        ddta = _tri_dot(r_i <= c_i, da_cols, terms=2)
        ddt = ddt + ddta * p["a_neg"]
        dalog_ref[...] += jnp.sum(ddta * p["dt"], axis=0, keepdims=True) * p["a_neg"]
        draw = ddt * _sigmoid(p["pre"])
        ddtb_ref[...] += jnp.sum(draw, axis=0, keepdims=True)
        dps_ref[:, CONV_CH + SSM_WIDTH:] = draw.astype(BF16)

        xc = p["xc"]
        sg = p["sg"]
        dxc = dxa_scr[...] * (sg + xc * sg * (1.0 - sg))
        dcb_ref[...] += jnp.sum(dxc, axis=0, keepdims=True)
        for k in range(CONV_K):
            dcw_ref[k] += jnp.sum(dxc * p["shifted"][k], axis=0, keepdims=True)
        nxt = nxt_scr[...]
        dxr = cw_ref[3] * dxc
        for s in range(1, CONV_K):
            dxr = dxr + cw_ref[CONV_K - 1 - s] * _shift_up(dxc, nxt, s)
        dps_ref[:, :CONV_CH] = dxr.astype(BF16)
        nxt_scr[...] = dxc[:SUBLANES, :]

    seq_len = n_tok // n_seq

    def rows(width):
        return pl.BlockSpec((n_seq, CHUNK, width), lambda s: (0, nc - 1 - s, 0))

    tiles = CHUNK // SUBLANES
    prev_rows = pl.BlockSpec((n_seq, SUBLANES, CONV_CH), lambda s: (0, jnp.maximum((nc - 1 - s) * tiles - 1, 0), 0))

    def whole(shape):
        nd = len(shape)
        return pl.BlockSpec(tuple(shape), lambda s: (0,) * nd)

    def by_seq(a):
        return a.reshape(n_seq, seq_len, a.shape[-1])

    acc_shapes = [(CONV_K, 1, CONV_CH), (1, CONV_CH), (1, DT_PAD), (1, DT_PAD), (1, DT_PAD), (1, SSM_WIDTH)]
    xbc3 = by_seq(p_xbc)
    outs = _call_carrying(
        body, carried, name="ssd_bwd", grid=(nc,),
        in_specs=[rows(CONV_CH), prev_rows, rows(SSM_WIDTH), rows(DT_PAD), rows(SSM_WIDTH), rows(SSM_WIDTH),
                  rows(SSM_WIDTH)] + _ssd_const_specs(),
        out_specs=[rows(SSD_COLS)] + [whole(s) for s in acc_shapes],
        out_shape=tuple([_sds((n_seq, seq_len, SSD_COLS), BF16)] + [_sds(s, F32) for s in acc_shapes]),
        scratch_shapes=[pltpu.VMEM((n_seq, SSM_STATE, SSM_WIDTH), F32), pltpu.VMEM((n_seq, SUBLANES, CONV_CH), F32),
                        pltpu.VMEM((n_seq, CHUNK, CONV_CH), F32)],
        operands=[xbc3, xbc3, by_seq(p_z), by_seq(p_dt), by_seq(yssd), by_seq(sprev), by_seq(dyb), conv_w, conv_b, dt_bias,
                  a_log, dskip_map, norm_w, e_bf, et_bf])
    return (outs[0].reshape(n_tok, SSD_COLS),) + tuple(outs[1:])


def _inproj_bwd(dp_uv, dp_ssd, x, dx1, w_uv, w_xbc, w_z, w_dt, nw, tm=512, carried=None):
    n_tok = x.shape[0]

    def body(duv_ref, dssd_ref, x_ref, dx1_ref, wuv_ref, wxbc_ref, wz_ref, wdt_ref, nw_ref, gx_ref, h_ref, dnw_ref):
        dh = (_dot_nt(duv_ref[...], wuv_ref[...]) + _dot_nt(dssd_ref[:, :CONV_CH], wxbc_ref[...])
              + _dot_nt(dssd_ref[:, CONV_CH:CONV_CH + SSM_WIDTH], wz_ref[...])
              + _dot_nt(dssd_ref[:, CONV_CH + SSM_WIDTH:], wdt_ref[...]))
        xv = x_ref[...]
        h, r = _rms_fwd(xv, nw_ref[...])
        dx, dnw = _rms_bwd(xv, r, nw_ref[...], dh)
        gx_ref[...] = dx1_ref[...] + dx
        h_ref[...] = h.astype(BF16)
        dnw_ref[...] += dnw

    return _rows_call("inproj_bwd", body, tm, [dp_uv, dp_ssd, x, dx1], [w_uv, w_xbc, w_z, w_dt, nw],
                      [_sds((n_tok, D_MODEL), F32), _sds((n_tok, D_MODEL), BF16)], [_sds((1, D_MODEL), F32)],
                      carried=carried)


def _const_maps():
    lane = jnp.arange(SSM_WIDTH) // HEAD_DIM
    e_bf = (jnp.arange(DT_PAD)[:, None] == lane[None, :]).astype(BF16)
    return e_bf, e_bf.T


def _pad_lanes(v, width):
    return jnp.pad(v, ((0, 0), (0, width - v.shape[1])))


SHARD_COLS = IN_COLS // N_CHIPS
_UV_END = 2 * GM_WIDTH
_Z_END = _UV_END + SSM_WIDTH
_XBC_END = _Z_END + CONV_CH


def _cols_from_shards(w4, lo, hi):
    pieces = []
    for j in range(N_CHIPS):
        a, b = max(lo, j * SHARD_COLS), min(hi, (j + 1) * SHARD_COLS)
        if a < b:
            pieces.append(w4[j][:, a - j * SHARD_COLS:b - j * SHARD_COLS])
    return pieces[0] if len(pieces) == 1 else jnp.concatenate(pieces, axis=1)


def _shards_from_cols(blocks):
    shards = []
    for j in range(N_CHIPS):
        pieces = []
        for arr, lo, hi in blocks:
            a, b = max(lo, j * SHARD_COLS), min(hi, (j + 1) * SHARD_COLS)
            if a < b:
                pieces.append(arr[:, a - lo:b - lo])
        shards.append(pieces[0] if len(pieces) == 1 else jnp.concatenate(pieces, axis=1))
    return jnp.stack(shards)


def _forward_backward(x, tgt, w_in4, conv_w, small, out_shard, up_shard, down_shard, core, adam_args):
    n_seq, seq_len, _ = x.shape
    n_tok = n_seq * seq_len
    x2 = x.reshape(n_tok, D_MODEL)
    tgt2 = tgt.reshape(n_tok, D_MODEL)
    e_bf, et_bf = _const_maps()

    w_uv = _cols_from_shards(w_in4, 0, _UV_END)
    w_z = _cols_from_shards(w_in4, _UV_END, _Z_END)
    w_xbc = _cols_from_shards(w_in4, _Z_END, _XBC_END)
    w_dt = _pad_lanes(_cols_from_shards(w_in4, _XBC_END, IN_COLS), DT_PAD)

    nw_pre = small["norm_mix_pre"]
    lnw = small["gm_ln_w"].reshape(1, GM_WIDTH)
    lnb = small["gm_ln_b"].reshape(1, GM_WIDTH)
    w_stack = small["gm_w_s"].reshape(N_HEADS * CHUNK, CHUNK)
    w_cat = jnp.transpose(small["gm_w_s"], (1, 0, 2)).reshape(CHUNK, N_HEADS * CHUNK)
    bmap = jnp.repeat(small["gm_b_s"].T, HEAD_DIM, axis=1)
    cw3 = conv_w.reshape(CONV_K, 1, CONV_CH)
    conv_b = small["conv_b"]
    dt_bias = _pad_lanes(small["dt_bias"], DT_PAD)
    a_log = _pad_lanes(small["a_log"], DT_PAD)
    dskip_map = jnp.repeat(small["d_skip"], HEAD_DIM, axis=1)
    ssm_nw = small["ssm_norm_w"]

    half = down_shard.shape[0] // 2
    p_uv, p_xbc, p_z, p_dt, w_out4, w_down_a = _inproj_fwd(
        x2, nw_pre, w_uv, w_xbc, w_z, w_dt, carried=_allgather_exchange([out_shard, down_shard[:half]]))
    ssd_consts = (cw3, conv_b, dt_bias, a_log, dskip_map, ssm_nw, e_bf, et_bf)
    w_out_b = w_out4.reshape(D_MODEL, D_MODEL)
    mix, yssd, sprev, o, x1, h2, w_up4, w_down_b = _mixer_fwd(
        p_uv, p_xbc, p_z, p_dt, x2, lnw, lnb, w_cat, bmap, w_out_b, small["norm_mix_post"], small["norm_ffn_pre"],
        *ssd_consts, n_seq, carried=_allgather_exchange([up_shard, down_shard[half:]]))
    f, dd, dy, loss_acc, d_nffn_post = _mlp_fwd(h2, x1, tgt2, w_up4, w_down_a, w_down_b, small["norm_ffn_post"])

    dup, dx1, d_nffn_pre = _mlp_bwd(dd, f, x1, dy, w_down_a, w_down_b, w_up4, small["norm_ffn_pre"])
    tk = min(DW_TOKENS_PER_STEP, n_tok)
    g_up = _matmul_tn("dw_up", h2, dup, D_MODEL, D_MODEL, tk, stacked=True)
    g_down = _matmul_tn("dw_down", f, dd, 1024, D_MODEL, tk).reshape(N_CHIPS, D_FF // N_CHIPS, D_MODEL)
    do, dya, dyb, d_nmix_post, got_up, got_down = _outproj_bwd(
        dx1, o, w_out_b, small["norm_mix_post"], carried=_pair_exchange([g_up, g_down]))
    h_up = _pair_sum(core, g_up, got_up, 512)
    h_down = _pair_sum(core, g_down, got_down, 512)
    g_out = _matmul_tn("dw_out", mix, do, D_MODEL, D_MODEL, tk).reshape(N_CHIPS, D_MODEL // N_CHIPS, D_MODEL)
    dp_uv, d_ws, d_bs_t, d_lnw, d_lnb, slab_up, got_out = _gmlp_bwd(
        p_uv, dya, lnw, lnb, e_bf, et_bf, w_cat, w_stack, bmap,
        carried=_both(_chip_exchange([h_up]), _pair_exchange([g_out])))
    h_out = _pair_sum(core, g_out, got_out, 128)
    early = {
        "gm_ln_w": d_lnw.reshape(N_HEADS, HEAD_DIM), "gm_ln_b": d_lnb.reshape(N_HEADS, HEAD_DIM),
        "gm_w_s": d_ws.reshape(N_HEADS, CHUNK, CHUNK), "gm_b_s": d_bs_t[:, :N_HEADS].T,
        "norm_mix_post": d_nmix_post, "norm_ffn_pre": d_nffn_pre, "norm_ffn_post": d_nffn_post,
    }
    packed_early = _pack(early, tuple(early), tail=loss_acc[0, 0].reshape(1))
    (dp_ssd, d_cw, d_cb, d_dtb, d_alog, d_dsk, d_ssm_nw, slab_down, slab_out, all_early) = _ssd_bwd(
        p_xbc, p_z, p_dt, yssd, sprev, dyb, *ssd_consts, n_seq,
        carried=_both(_chip_exchange([h_down, h_out]), _device_gather_exchange(packed_early)))
    gx, h, d_nmix_pre = _inproj_bwd(dp_uv, dp_ssd, x2, dx1, w_uv, w_xbc, w_z, w_dt, nw_pre)
    late = {
        "norm_mix_pre": d_nmix_pre, "conv_w": d_cw.reshape(CONV_K, CONV_CH), "conv_b": d_cb,
        "dt_bias": d_dtb[:, :N_HEADS], "a_log": d_alog[:, :N_HEADS], "d_skip": d_dsk[:, :N_HEADS],
        "ssm_norm_w": d_ssm_nw,
    }
    g_uv, all_late = _matmul_tn("dw_in_uv", h, dp_uv, D_MODEL, 2 * GM_WIDTH, tk,
                                carried=_device_gather_exchange(_pack(late, tuple(late))))
    sum_early = _ordered_sum("small_sum_early", all_early)
    small_sum = _unpack(sum_early, {n: v.shape for n, v in early.items()}, tuple(early))
    small_sum.update(_unpack(_ordered_sum("small_sum_late", all_late), {n: v.shape for n, v in late.items()}, tuple(late)))
    loss = sum_early.reshape(-1)[sum(v.size for v in early.values())]
    red_up, red_down, red_out = _chip_sum(slab_up, 512), _chip_sum(slab_down, 512), _chip_sum(slab_out, 128)
    g_ssd, oth_up, oth_down, oth_out = _matmul_tn("dw_in_ssd", h, dp_ssd, D_MODEL, SSD_COLS, tk,
                                                  carried=_pair_swap([red_up, red_down, red_out]))
    g_xbc, g_z, g_dt = g_ssd[:, :CONV_CH], g_ssd[:, CONV_CH:CONV_CH + SSM_WIDTH], g_ssd[:, CONV_CH + SSM_WIDTH:]
    g_in = _shards_from_cols([(g_uv, 0, _UV_END), (g_z, _UV_END, _Z_END), (g_xbc, _Z_END, _XBC_END),
                              (g_dt, _XBC_END, IN_COLS)])
    red_in, oth_in = _reduce_scatter_last(g_in)
    res = _adamw_halves("adamw_mlp", [(adam_args["w_up"][0], red_up, oth_up) + adam_args["w_up"][1:],
                                      (adam_args["w_down"][0], red_down, oth_down) + adam_args["w_down"][1:]], 256)
    big_out = {"w_up": res[0:4], "w_down": res[4:8]}
    big_out["w_out"] = _adamw_halves("adamw_w_out", [(adam_args["w_out"][0], red_out, oth_out) + adam_args["w_out"][1:]], 128)
    big_out["w_in"] = _adamw_halves("adamw_w_in", [(adam_args["w_in"][0], red_in, oth_in) + adam_args["w_in"][1:]], 256)

    return loss, gx.reshape(x.shape), big_out, small_sum


_HBM = pl.BlockSpec(memory_space=pltpu.HBM)


D2D_CHUNKS = 8
ICI_CHUNKS = 1
ROW_ALIGN = 16


def _row_chunks(rows, n_chunks):
    size = min(max(rows // n_chunks, ROW_ALIGN), rows)
    assert rows % size == 0
    return [(start, size) for start in range(0, rows, size)]


def _position():
    x, y, c = lax.axis_index("x"), lax.axis_index("y"), lax.axis_index("c")
    chips = [(1 - x, y), (x, 1 - y), (1 - x, 1 - y)]
    return x, y, c, chips


def _allgather_exchange(arrs):
    n = len(arrs)

    def copies(ins, outs, send_sems, recv_sems, local_sems):
        x, y, c, chips = _position()
        me = 2 * x + y
        sibling = (x, y, 1 - c)

        def copy(a, k, src, dst, to):
            return pltpu.make_async_remote_copy(src_ref=src, dst_ref=dst, send_sem=send_sems.at[a, k],
                                                recv_sem=recv_sems.at[a, k], device_id=to, device_id_type=MESH)

        def half_rows(a, pc):
            half = ins[a].shape[0] // 2
            return pl.ds(pc * half, half)

        local = [pltpu.make_async_copy(ins[a], outs[a].at[me], local_sems.at[a]) for a in range(n)]
        ici_out = [[copy(a, k, ins[a].at[half_rows(a, c)], outs[a].at[me, half_rows(a, c)], (px, py, c))
                    for k, (px, py) in enumerate(chips)] for a in range(n)]
        return c, chips, sibling, copy, half_rows, local, ici_out

    def start(ins, outs, send_sems, recv_sems, local_sems):
        c, chips, _, copy, _, local, _ = copies(ins, outs, send_sems, recv_sems, local_sems)
        x, y, _, _ = _position()
        me = 2 * x + y
        for cp in local:
            cp.start()
        for a in range(n):
            half = ins[a].shape[0] // 2
            for k, (px, py) in enumerate(chips):
                for first, size in _row_chunks(half, ICI_CHUNKS):
                    rows = pl.ds(c * half + first, size)
                    copy(a, k, ins[a].at[rows], outs[a].at[me, rows], (px, py, c)).start()

    def finish(ins, outs, send_sems, recv_sems, local_sems):
        c, chips, sibling, copy, half_rows, local, ici_out = copies(ins, outs, send_sems, recv_sems, local_sems)
        for a in range(n):
            half = ins[a].shape[0] // 2
            for k, (px, py) in enumerate(chips):
                blk = outs[a].at[2 * px + py, half_rows(a, c)]
                copy(a, k, blk, blk, (px, py, c)).wait_recv()
                for first, size in _row_chunks(half, D2D_CHUNKS):
                    piece = outs[a].at[2 * px + py, pl.ds(c * half + first, size)]
                    copy(a, 3 + k, piece, piece, sibling).start()
        for a in range(n):
            for k, (px, py) in enumerate(chips):
                theirs = outs[a].at[2 * px + py, half_rows(a, 1 - c)]
                copy(a, 3 + k, theirs, theirs, sibling).wait_recv()
                mine = outs[a].at[2 * px + py, half_rows(a, c)]
                copy(a, 3 + k, mine, mine, sibling).wait_send()
        for a in range(n):
            for cp in ici_out[a]:
                cp.wait_send()
        for cp in local:
            cp.wait()

    return _Carried(arrs, [_sds((N_CHIPS,) + a.shape, a.dtype) for a in arrs],
                    [pltpu.SemaphoreType.DMA((n, 6)), pltpu.SemaphoreType.DMA((n, 6)), pltpu.SemaphoreType.DMA((n,))],
                    start, finish)


def _run_exchange(name, exchange):
    n_in, n_out = len(exchange.ins), len(exchange.out_shapes)

    def body(*refs):
        ins, outs, sems = refs[:n_in], refs[n_in:n_in + n_out], refs[n_in + n_out:]
        exchange.start(ins, outs, *sems)
        exchange.finish(ins, outs, *sems)

    return pl.pallas_call(
        body, name=name, out_shape=tuple(exchange.out_shapes), in_specs=[_HBM] * n_in,
        out_specs=tuple([_HBM] * n_out), scratch_shapes=exchange.sems,
    )(*exchange.ins)


def _pair_exchange(grads):
    n = len(grads)

    def copier(send_sems, recv_sems):
        x, y, c, _ = _position()

        def copy(a, src, dst):
            return pltpu.make_async_remote_copy(src_ref=src, dst_ref=dst, send_sem=send_sems.at[a],
                                                recv_sem=recv_sems.at[a], device_id=(x, y, 1 - c), device_id_type=MESH)
        return c, copy

    def start(ins, got, send_sems, recv_sems):
        c, copy = copier(send_sems, recv_sems)
        for a in range(n):
            half = ins[a].shape[1] // 2
            for slab in range(N_CHIPS):
                for first, size in _row_chunks(half, D2D_CHUNKS):
                    copy(a, ins[a].at[slab, pl.ds((1 - c) * half + first, size), :],
                         got[a].at[slab, pl.ds(first, size), :]).start()

    def finish(ins, got, send_sems, recv_sems):
        c, copy = copier(send_sems, recv_sems)
        for a in range(n):
            half = ins[a].shape[1] // 2
            copy(a, ins[a].at[:, pl.ds((1 - c) * half, half), :], got[a]).wait()

    return _Carried(grads, [_sds((N_CHIPS, g.shape[1] // 2, g.shape[2]), g.dtype) for g in grads],
                    [pltpu.SemaphoreType.DMA((n,)), pltpu.SemaphoreType.DMA((n,))], start, finish)


def _chip_exchange(hsums):
    n = len(hsums)

    def copies(ins, outs, send_sems, recv_sems, local_sems, pieces):
        x, y, c, chips = _position()
        me = 2 * x + y
        cps = []
        for a in range(n):
            cps.append(pltpu.make_async_copy(ins[a].at[me], outs[a].at[me], local_sems.at[a]))
            rows = ins[a].shape[1]
            for k, (px, py) in enumerate(chips):
                for first, size in (_row_chunks(rows, ICI_CHUNKS) if pieces else [(0, rows)]):
                    cps.append(pltpu.make_async_remote_copy(
                        src_ref=ins[a].at[2 * px + py, pl.ds(first, size)], dst_ref=outs[a].at[me, pl.ds(first, size)],
                        send_sem=send_sems.at[a, k], recv_sem=recv_sems.at[a, k], device_id=(px, py, c),
                        device_id_type=MESH))
        return cps

    def start(*refs):
        for cp in copies(*refs, pieces=True):
            cp.start()

    def finish(*refs):
        for cp in copies(*refs, pieces=False):
            cp.wait()

    return _Carried(hsums, [_sds(h.shape, h.dtype) for h in hsums],
                    [pltpu.SemaphoreType.DMA((n, 3)), pltpu.SemaphoreType.DMA((n, 3)), pltpu.SemaphoreType.DMA((n,))],
                    start, finish)


def _pair_swap(reds):
    n = len(reds)

    def copier(send_sems, recv_sems):
        x, y, c, _ = _position()

        def copy(a, src, dst):
            return pltpu.make_async_remote_copy(src_ref=src, dst_ref=dst, send_sem=send_sems.at[a],
                                                recv_sem=recv_sems.at[a], device_id=(x, y, 1 - c), device_id_type=MESH)
        return copy

    def start(ins, outs, send_sems, recv_sems):
        copy = copier(send_sems, recv_sems)
        for a in range(n):
            for first, size in _row_chunks(ins[a].shape[0], 2 * D2D_CHUNKS):
                copy(a, ins[a].at[pl.ds(first, size), :], outs[a].at[pl.ds(first, size), :]).start()

    def finish(ins, outs, send_sems, recv_sems):
        copy = copier(send_sems, recv_sems)
        for a in range(n):
            copy(a, ins[a], outs[a]).wait()

    return _Carried(reds, [_sds(r.shape, r.dtype) for r in reds],
                    [pltpu.SemaphoreType.DMA((n,)), pltpu.SemaphoreType.DMA((n,))], start, finish)


def _reduce_scatter_last(grad):
    _, rows, cols = grad.shape
    half = rows // 2
    pieces = _row_chunks(half, D2D_CHUNKS)

    def body(g_ref, mine_ref, theirs_ref, got_scr, hsum_scr, slab_scr, pair_sems, ici_send, ici_recv, swap_sems):
        x, y, c, chips = _position()
        me = 2 * x + y
        sibling = (x, y, 1 - c)

        def to_sibling(src, dst, sems):
            return pltpu.make_async_remote_copy(src_ref=src, dst_ref=dst, send_sem=sems.at[0], recv_sem=sems.at[1],
                                                device_id=sibling, device_id_type=MESH)

        for slab in range(N_CHIPS):
            for first, size in pieces:
                to_sibling(g_ref.at[slab, pl.ds((1 - c) * half + first, size)], got_scr.at[slab, pl.ds(first, size)],
                           pair_sems).start()
        to_sibling(g_ref.at[:, pl.ds((1 - c) * half, half)], got_scr, pair_sems).wait()
        own = g_ref[:, pl.ds(pl.multiple_of(c * half, half), half), :]
        hsum_scr[...] = (own.astype(F32) + got_scr[...].astype(F32)).astype(BF16)

        slab_scr[me] = hsum_scr[me]
        ici = [pltpu.make_async_remote_copy(src_ref=hsum_scr.at[2 * px + py], dst_ref=slab_scr.at[me],
                                            send_sem=ici_send.at[k], recv_sem=ici_recv.at[k], device_id=(px, py, c),
                                            device_id_type=MESH) for k, (px, py) in enumerate(chips)]
        for cp in ici:
            cp.start()
        for cp in ici:
            cp.wait()
        acc = slab_scr[0].astype(F32)
        for k in range(1, N_CHIPS):
            acc = acc + slab_scr[k].astype(F32)
        mine_ref[...] = acc

        for first, size in pieces:
            to_sibling(mine_ref.at[pl.ds(first, size)], theirs_ref.at[pl.ds(first, size)], swap_sems).start()
        to_sibling(mine_ref, theirs_ref, swap_sems).wait()

    vmem = pl.BlockSpec(memory_space=pltpu.VMEM)
    halves = (N_CHIPS, half, cols)
    return pl.pallas_call(
        body, name="grad_reduce_scatter_last", out_shape=(_sds((half, cols), F32), _sds((half, cols), F32)),
        in_specs=[vmem], out_specs=(vmem, vmem),
        scratch_shapes=[pltpu.VMEM(halves, BF16), pltpu.VMEM(halves, BF16), pltpu.VMEM(halves, BF16),
                        pltpu.SemaphoreType.DMA((2,)), pltpu.SemaphoreType.DMA((3,)), pltpu.SemaphoreType.DMA((3,)),
                        pltpu.SemaphoreType.DMA((2,))],
        compiler_params=pltpu.CompilerParams(vmem_limit_bytes=VMEM_LIMIT_BYTES),
    )(grad)


def _device_gather_exchange(packed):
    def copies(ins, outs, send_sems, recv_sems, local_sem):
        (x_ref,), (all_ref,) = ins, outs
        x, y, c, chips = _position()
        me, sibling = (x, y, c), (x, y, 1 - c)

        def slab(px, py, pc):
            return all_ref.at[4 * px + 2 * py + pc]

        def copy(k, block, to, src=None):
            return pltpu.make_async_remote_copy(
                src_ref=slab(*block) if src is None else src, dst_ref=slab(*block), send_sem=send_sems.at[k],
                recv_sem=recv_sems.at[k], device_id=to, device_id_type=MESH)

        mine = pltpu.make_async_copy(x_ref, slab(*me), local_sem)
        first = [copy(0, me, sibling, src=x_ref)]
        first += [copy(1 + j, me, (*chip, c), src=x_ref) for j, chip in enumerate(chips)]
        passed = [copy(4 + j, (*chip, c), sibling) for j, chip in enumerate(chips)]
        return c, chips, me, sibling, copy, mine, first, passed

    def start(ins, outs, send_sems, recv_sems, local_sem):
        _, _, _, _, _, mine, first, _ = copies(ins, outs, send_sems, recv_sems, local_sem)
        mine.start()
        for cp in first:
            cp.start()

    def finish(ins, outs, send_sems, recv_sems, local_sem):
        c, chips, me, sibling, copy, mine, first, passed = copies(ins, outs, send_sems, recv_sems, local_sem)
        for j, chip in enumerate(chips):
            copy(1 + j, (*chip, c), me).wait_recv()
            passed[j].start()
        copy(0, sibling, me).wait_recv()
        for j, chip in enumerate(chips):
            copy(4 + j, (*chip, 1 - c), me).wait_recv()
        for cp in first + passed:
            cp.wait_send()
        mine.wait()

    return _Carried([packed], [_sds((N_DEV,) + packed.shape, F32)],
                    [pltpu.SemaphoreType.DMA((7,)), pltpu.SemaphoreType.DMA((7,)), pltpu.SemaphoreType.DMA],
                    start, finish)


def _ordered_sum(name, slabs):
    _, m_per, n_cols = slabs.shape

    def body(s_ref, o_ref):
        acc = s_ref[0]
        for d in range(1, N_DEV):
            acc = acc + s_ref[d]
        o_ref[...] = acc

    vmem = pl.BlockSpec(memory_space=pltpu.VMEM)
    return pl.pallas_call(body, name=name, out_shape=_sds((m_per, n_cols), F32), in_specs=[vmem], out_specs=vmem)(slabs)


def _pair_sum(core, own, got, tm):
    _, half, cols = got.shape
    nb = half // tm

    def body(c_ref, a_ref, b_ref, o_ref):
        o_ref[...] = (a_ref[...].astype(F32) + b_ref[...].astype(F32)).astype(BF16)

    return pl.pallas_call(
        body, name="grad_pair_sum", out_shape=_sds(got.shape, BF16),
        grid_spec=pltpu.PrefetchScalarGridSpec(
            num_scalar_prefetch=1, grid=(N_CHIPS, nb),
            in_specs=[pl.BlockSpec((None, tm, cols), lambda s, i, c_ref: (s, c_ref[0] * nb + i, 0)),
                      pl.BlockSpec((None, tm, cols), lambda s, i, c_ref: (s, i, 0))],
            out_specs=pl.BlockSpec((None, tm, cols), lambda s, i, c_ref: (s, i, 0))),
        compiler_params=_cparams(2),
    )(core, own, got)


def _chip_sum(slabs, tm):
    _, half, cols = slabs.shape

    def body(s_ref, o_ref):
        acc = s_ref[0].astype(F32)
        for k in range(1, N_CHIPS):
            acc = acc + s_ref[k].astype(F32)
        o_ref[...] = acc

    return pl.pallas_call(
        body, name="grad_chip_sum", out_shape=_sds((half, cols), F32), grid=(half // tm,),
        in_specs=[pl.BlockSpec((N_CHIPS, tm, cols), lambda i: (0, i, 0))],
        out_specs=pl.BlockSpec((tm, cols), lambda i: (i, 0)), compiler_params=_cparams(1),
    )(slabs)


def _adam_math(w, g, m, v):
    m2 = ADAM_B1 * m + (1.0 - ADAM_B1) * g
    v2 = ADAM_B2 * v + (1.0 - ADAM_B2) * (g * g)
    m_hat = m2 / (1.0 - ADAM_B1 ** ADAM_STEP)
    v_hat = v2 / (1.0 - ADAM_B2 ** ADAM_STEP)
    delta = -ADAM_LR * (m_hat / (jnp.sqrt(v_hat) + ADAM_EPS) + ADAM_WD * w)
    return delta, m2, v2


def _adamw_halves(name, items, tm, carried=None):
    rows, cols = items[0][0].shape
    nb = rows // 2 // tm
    n = len(items)

    def body(*refs):
        mine = (pl.program_id(0) // nb) == lax.axis_index("c")
        for k in range(n):
            w_ref, own_ref, oth_ref, m_ref, v_ref = refs[5 * k:5 * k + 5]
            g_ref, d_ref, m2_ref, v2_ref = refs[5 * n + 4 * k:5 * n + 4 * k + 4]
            g = jnp.where(mine, own_ref[...], oth_ref[...])
            d, m2, v2 = _adam_math(w_ref[...], g, m_ref[...], v_ref[...])
            g_ref[...] = g
            d_ref[...] = d
            m2_ref[...] = m2
            v2_ref[...] = v2

    full = pl.BlockSpec((tm, cols), lambda i: (i, 0))
    half = pl.BlockSpec((tm, cols), lambda i: (i % nb, 0))
    return _call_carrying(
        body, carried, name=name, grid=(rows // tm,), in_specs=[full, half, half, full, full] * n,
        out_specs=[full] * (4 * n), out_shape=tuple([_sds((rows, cols), F32)] * (4 * n)), scratch_shapes=[],
        operands=[a for item in items for a in item])


def _adamw(name, w, g, m, v, tm):
    def body(w_ref, g_ref, m_ref, v_ref, gout_ref, d_ref, m2_ref, v2_ref):
        gv = g_ref[...]
        d, m2, v2 = _adam_math(w_ref[...], gv, m_ref[...], v_ref[...])
        gout_ref[...] = gv
        d_ref[...] = d
        m2_ref[...] = m2
        v2_ref[...] = v2

    return _rows_call(name, body, tm, [w, g, m, v], [], [_sds(w.shape, F32)] * 4)


_SMALL_NAMES = ("norm_mix_pre", "gm_ln_w", "gm_ln_b", "gm_w_s", "gm_b_s", "conv_w", "conv_b", "dt_bias", "a_log",
                "d_skip", "ssm_norm_w", "norm_mix_post", "norm_ffn_pre", "norm_ffn_post")
_PACK_COLS = 1024


def _pack(parts, names=_SMALL_NAMES, tail=None):
    pieces = [parts[n].reshape(-1) for n in names]
    flat = jnp.concatenate(pieces if tail is None else pieces + [tail])
    rows = -(-flat.shape[0] // (8 * _PACK_COLS)) * 8
    flat = jnp.pad(flat, (0, rows * _PACK_COLS - flat.shape[0]))
    return flat.reshape(rows, _PACK_COLS)


def _unpack(packed, shapes, names=_SMALL_NAMES):
    flat = packed.reshape(-1)
    out, off = {}, 0
    for n in names:
        size = 1
        for s in shapes[n]:
            size *= s
        out[n] = flat[off:off + size].reshape(shapes[n])
        off += size
    return out


def kernel(x, norm_mix_pre, w_in, gm_ln_w, gm_ln_b, gm_w_s, gm_b_s, conv_w, conv_b, dt_bias, a_log, d_skip, ssm_norm_w, w_out, norm_mix_post, norm_ffn_pre, w_up, w_down, norm_ffn_post, loss_target, m_norm_mix_pre, m_w_in, m_gm_ln_w, m_gm_ln_b, m_gm_w_s, m_gm_b_s, m_conv_w, m_conv_b, m_dt_bias, m_a_log, m_d_skip, m_ssm_norm_w, m_w_out, m_norm_mix_post, m_norm_ffn_pre, m_w_up, m_w_down, m_norm_ffn_post, v_norm_mix_pre, v_w_in, v_gm_ln_w, v_gm_ln_b, v_gm_w_s, v_gm_b_s, v_conv_w, v_conv_b, v_dt_bias, v_a_log, v_d_skip, v_ssm_norm_w, v_w_out, v_norm_mix_post, v_norm_ffn_pre, v_w_up, v_w_down, v_norm_ffn_post):
    params = dict(norm_mix_pre=norm_mix_pre, w_in=w_in, gm_ln_w=gm_ln_w, gm_ln_b=gm_ln_b, gm_w_s=gm_w_s, gm_b_s=gm_b_s,
                  conv_w=conv_w, conv_b=conv_b, dt_bias=dt_bias, a_log=a_log, d_skip=d_skip, ssm_norm_w=ssm_norm_w,
                  w_out=w_out, norm_mix_post=norm_mix_post, norm_ffn_pre=norm_ffn_pre, w_up=w_up, w_down=w_down,
                  norm_ffn_post=norm_ffn_post)
    mom1 = dict(norm_mix_pre=m_norm_mix_pre, w_in=m_w_in, gm_ln_w=m_gm_ln_w, gm_ln_b=m_gm_ln_b, gm_w_s=m_gm_w_s,
                gm_b_s=m_gm_b_s, conv_w=m_conv_w, conv_b=m_conv_b, dt_bias=m_dt_bias, a_log=m_a_log, d_skip=m_d_skip,
                ssm_norm_w=m_ssm_norm_w, w_out=m_w_out, norm_mix_post=m_norm_mix_post, norm_ffn_pre=m_norm_ffn_pre,
                w_up=m_w_up, w_down=m_w_down, norm_ffn_post=m_norm_ffn_post)
    mom2 = dict(norm_mix_pre=v_norm_mix_pre, w_in=v_w_in, gm_ln_w=v_gm_ln_w, gm_ln_b=v_gm_ln_b, gm_w_s=v_gm_w_s,
                gm_b_s=v_gm_b_s, conv_w=v_conv_w, conv_b=v_conv_b, dt_bias=v_dt_bias, a_log=v_a_log, d_skip=v_d_skip,
                ssm_norm_w=v_ssm_norm_w, w_out=v_w_out, norm_mix_post=v_norm_mix_post, norm_ffn_pre=v_norm_ffn_pre,
                w_up=v_w_up, w_down=v_w_down, norm_ffn_post=v_norm_ffn_post)
    names = list(params)
    big = ("w_in", "w_out", "w_up", "w_down")
    chip = 2 * lax.axis_index("x") + lax.axis_index("y")

    shards = {n: params[n][0].astype(BF16) for n in big}
    conv_shard = jnp.pad(conv_w[0], ((0, 16 - CONV_K), (0, 0)))
    g_in4, g_conv4 = _run_exchange("allgather_w_in", _allgather_exchange([shards["w_in"], conv_shard]))
    conv_full = jnp.transpose(g_conv4[:, :CONV_K, :], (1, 0, 2)).reshape(CONV_K, CONV_CH)

    small = {n: params[n][0] if params[n].ndim >= 3 else params[n] for n in _SMALL_NAMES if n != "conv_w"}
    core = lax.axis_index("c").astype(jnp.int32).reshape(1)
    adam_args = {n: (params[n][0], mom1[n][0], mom2[n][0]) for n in big}
    loss, grad_x, big_out, small_sum = _forward_backward(
        x, loss_target, g_in4, conv_full, small, shards["w_out"], shards["w_up"], shards["w_down"], core, adam_args)
    grads, delta, new_m, new_v = {}, {}, {}, {}
    for n in big:
        grads[n], delta[n], new_m[n], new_v[n] = [a[None] for a in big_out[n]]

    small_sum["conv_w"] = lax.dynamic_slice_in_dim(small_sum["conv_w"], chip * (CONV_CH // N_CHIPS), CONV_CH // N_CHIPS, axis=1)

    local_shapes = {n: params[n].shape[1:] if params[n].ndim >= 3 else params[n].shape for n in _SMALL_NAMES}
    flat = lambda tree: {n: tree[n].reshape(local_shapes[n]) for n in _SMALL_NAMES}
    packed = [_pack(flat(t)) for t in (params, small_sum, mom1, mom2)]
    _, d_p, m_p, v_p = _adamw("adamw_small", *packed, packed[0].shape[0])
    for src, dst in ((d_p, delta), (m_p, new_m), (v_p, new_v)):
        for n, val in _unpack(src, local_shapes).items():
            dst[n] = val.reshape(params[n].shape)
    for n in _SMALL_NAMES:
        grads[n] = small_sum[n].reshape(params[n].shape)

    out = [loss, grad_x]
    for tree in (grads, delta, new_m, new_v):
        out += [tree[n] for n in names]
    return tuple(out)
```

```python
import functools

import jax
import jax.numpy as jnp
from jax import lax
from jax.experimental import pallas as pl
from jax.experimental.pallas import tpu as pltpu

F32 = jnp.float32
BF16 = jnp.bfloat16
MESH = pl.DeviceIdType.MESH

EPS = 1e-6
D_MODEL = 1024
GM_WIDTH = 512
SSM_WIDTH = 512
N_HEADS = 8
HEAD_DIM = 64
CHUNK = 128
SSM_GROUPS = 2
GROUP_W = SSM_WIDTH // SSM_GROUPS
SSM_STATE = 128
CONV_K = 4
CONV_CH = 1024
D_FF = 4096
IN_COLS = 2568
DT_PAD = 128
SSD_COLS = CONV_CH + SSM_WIDTH + DT_PAD
N_CHIPS = 4
N_DEV = 8

ADAM_LR = 0.001
ADAM_B1 = 0.9
ADAM_B2 = 0.999
ADAM_EPS = 1e-08
ADAM_WD = 0.01
ADAM_STEP = 10

VMEM_LIMIT_BYTES = 56 * 1024 * 1024
FF_TILE = 512
DW_TOKENS_PER_STEP = 2048


def _cparams(n_axes):
    return pltpu.CompilerParams(dimension_semantics=("arbitrary",) * n_axes, vmem_limit_bytes=VMEM_LIMIT_BYTES)


def _dot(a, b):
    return jnp.dot(a.astype(BF16), b.astype(BF16), preferred_element_type=F32)


def _dot_nt(a, b):
    return lax.dot_general(a.astype(BF16), b.astype(BF16), (((1,), (1,)), ((), ())), preferred_element_type=F32)


def _dot_tn(a, b):
    return lax.dot_general(a.astype(BF16), b.astype(BF16), (((0,), (0,)), ((), ())), preferred_element_type=F32)


def _sigmoid(x):
    return 1.0 / (1.0 + jnp.exp(-x))


_GELU_C = 0.7978845608028654
_GELU_A = 0.044715


def _gelu(x):
    t = jnp.tanh(_GELU_C * (x + _GELU_A * (x * x * x)))
    return 0.5 * x * (1.0 + t), t


def _gelu_grad(x, t):
    return 0.5 * (1.0 + t) + 0.5 * x * (1.0 - t * t) * (_GELU_C * (1.0 + 3.0 * _GELU_A * x * x))


def _rms_fwd(x, w):
    r = lax.rsqrt(jnp.mean(x * x, axis=-1, keepdims=True) + EPS)
    return x * r * w, r


def _rms_bwd(x, r, w, dy):
    g = dy * w
    dx = r * g - x * (r * r * r) * jnp.mean(g * x, axis=-1, keepdims=True)
    dw = jnp.sum(dy * x * r, axis=0, keepdims=True)
    return dx, dw


class _Carried:
    def __init__(self, ins, out_shapes, sems, start, finish):
        self.ins, self.out_shapes, self.sems = list(ins), list(out_shapes), list(sems)
        self.start, self.finish = start, finish


def _both(first, second):
    n_i, n_o, n_s = len(first.ins), len(first.out_shapes), len(first.sems)

    def split(ins, outs, sems):
        return (ins[:n_i], outs[:n_o], sems[:n_s]), (ins[n_i:], outs[n_o:], sems[n_s:])

    def start(ins, outs, *sems):
        (i1, o1, s1), (i2, o2, s2) = split(ins, outs, sems)
        first.start(i1, o1, *s1)
        second.start(i2, o2, *s2)

    def finish(ins, outs, *sems):
        (i1, o1, s1), (i2, o2, s2) = split(ins, outs, sems)
        first.finish(i1, o1, *s1)
        second.finish(i2, o2, *s2)

    return _Carried(first.ins + second.ins, first.out_shapes + second.out_shapes, first.sems + second.sems, start, finish)


def _split_carried(refs, n_in, n_out, n_scratch, carried):
    n_ci, n_co, n_cs = len(carried.ins), len(carried.out_shapes), len(carried.sems)
    ins, rest = refs[:n_in], refs[n_in:]
    c_ins, rest = rest[:n_ci], rest[n_ci:]
    outs, rest = rest[:n_out], rest[n_out:]
    c_outs, rest = rest[:n_co], rest[n_co:]
    scr, c_sems = rest[:n_scratch], rest[n_scratch:]
    assert len(c_sems) == n_cs
    return tuple(ins) + tuple(outs) + tuple(scr), c_ins, c_outs, c_sems


def _rows_call(name, body, tm, row_ins, const_ins, row_outs, acc_outs=(), scratch=(), carried=None):
    n_rows = row_ins[0].shape[0]
    assert n_rows % tm == 0
    n_steps = n_rows // tm
    n_in = len(row_ins) + len(const_ins)
    n_ro = len(row_outs)
    n_acc = len(acc_outs)

    def kern(*refs):
        accs = refs[n_in + n_ro:n_in + n_ro + n_acc]

        @pl.when(pl.program_id(0) == 0)
        def _():
            for a in accs:
                a[...] = jnp.zeros_like(a)

        body(*refs)

    def whole(shape):
        nd = len(shape)
        return pl.BlockSpec(tuple(shape), lambda i: (0,) * nd)

    in_specs = [pl.BlockSpec((tm, a.shape[1]), lambda i: (i, 0)) for a in row_ins]
    in_specs += [whole(a.shape) for a in const_ins]
    out_specs = [pl.BlockSpec((tm, s.shape[1]), lambda i: (i, 0)) for s in row_outs]
    out_specs += [whole(s.shape) for s in acc_outs]
    return _call_carrying(
        kern, carried, name=name, grid=(n_steps,), in_specs=in_specs, out_specs=out_specs,
        out_shape=tuple(row_outs) + tuple(acc_outs), scratch_shapes=list(scratch), operands=list(row_ins) + list(const_ins))


def _call_carrying(body, carried, *, name, grid, in_specs, out_specs, out_shape, scratch_shapes, operands):
    n_in, n_out, n_scratch = len(in_specs), len(out_specs), len(scratch_shapes)
    kern = body
    if carried is not None:
        def kern(*refs):
            plain, c_ins, c_outs, c_sems = _split_carried(refs, n_in, n_out, n_scratch, carried)
            first, last = True, True
            for d, size in enumerate(grid):
                first = jnp.logical_and(first, pl.program_id(d) == 0)
                last = jnp.logical_and(last, pl.program_id(d) == size - 1)

            @pl.when(first)
            def _():
                carried.start(c_ins, c_outs, *c_sems)

            body(*plain)

            @pl.when(last)
            def _():
                carried.finish(c_ins, c_outs, *c_sems)

        in_specs = list(in_specs) + [_HBM] * len(carried.ins)
        out_specs = list(out_specs) + [_HBM] * len(carried.out_shapes)
        out_shape = tuple(out_shape) + tuple(carried.out_shapes)
        operands = list(operands) + carried.ins
        scratch_shapes = list(scratch_shapes) + carried.sems
    return pl.pallas_call(
        kern, name=name, grid=grid, in_specs=in_specs, out_specs=out_specs, out_shape=out_shape,
        scratch_shapes=scratch_shapes, compiler_params=_cparams(len(grid)),
    )(*operands)


def _sds(shape, dtype):
    return jax.ShapeDtypeStruct(tuple(shape), dtype)


def _matmul_tn(name, a, b, tm, tn, tk, stacked=False, carried=None):
    k_dim, m_dim = a.shape
    n_dim = b.shape[1]
    assert m_dim % tm == 0 and n_dim % tn == 0 and k_dim % tk == 0
    nk = k_dim // tk

    def kern(a_ref, b_ref, o_ref, acc_ref):
        k = pl.program_id(2)
        prod = _dot_tn(a_ref[...], b_ref[...])

        @pl.when(k == 0)
        def _():
            acc_ref[...] = prod

        @pl.when(k > 0)
        def _():
            acc_ref[...] += prod

        @pl.when(k == nk - 1)
        def _():
            o_ref[...] = acc_ref[...].astype(o_ref.dtype)

    if stacked:
        assert tm == m_dim
        out_shape = _sds((n_dim // tn, m_dim, tn), BF16)
        out_spec = pl.BlockSpec((None, tm, tn), lambda i, j, k: (j, i, 0))
    else:
        out_shape = _sds((m_dim, n_dim), BF16)
        out_spec = pl.BlockSpec((tm, tn), lambda i, j, k: (i, j))
    outs = _call_carrying(
        kern, carried, name=name, grid=(m_dim // tm, n_dim // tn, nk),
        in_specs=[pl.BlockSpec((tk, tm), lambda i, j, k: (k, i)), pl.BlockSpec((tk, tn), lambda i, j, k: (k, j))],
        out_specs=[out_spec], out_shape=(out_shape,), scratch_shapes=[pltpu.VMEM((tm, tn), F32)], operands=[a, b])
    return outs[0] if carried is None else outs


def _matmul_by_sequence(name, a_t, b, tn):
    n_seq, m_dim, seq_len = a_t.shape
    n_dim = b.shape[1]
    assert b.shape[0] == n_seq * seq_len and n_dim % tn == 0

    def kern(a_ref, b_ref, o_ref, acc_ref):
        k = pl.program_id(1)
        prod = jnp.dot(a_ref[...], b_ref[...], preferred_element_type=F32)

        @pl.when(k == 0)
        def _():
            acc_ref[...] = prod

        @pl.when(k > 0)
        def _():
            acc_ref[...] += prod

        @pl.when(k == n_seq - 1)
        def _():
            o_ref[...] = acc_ref[...].astype(o_ref.dtype)

    return pl.pallas_call(
        kern, name=name, grid=(n_dim // tn, n_seq),
        in_specs=[pl.BlockSpec((None, m_dim, seq_len), lambda j, k: (k, 0, 0)),
                  pl.BlockSpec((seq_len, tn), lambda j, k: (k, j))],
        out_specs=pl.BlockSpec((None, m_dim, tn), lambda j, k: (j, 0, 0)),
        out_shape=_sds((n_dim // tn, m_dim, tn), BF16),
        scratch_shapes=[pltpu.VMEM((m_dim, tn), F32)], compiler_params=_cparams(2),
    )(a_t, b)


def _inproj_fwd(x, nw, w_uv, w_xbc, w_z, w_dt, tm=256, carried=None):
    n_tok = x.shape[0]

    def body(x_ref, nw_ref, wuv_ref, wxbc_ref, wz_ref, wdt_ref, puv_ref, pxbc_ref, pz_ref, pdt_ref):
        h, _ = _rms_fwd(x_ref[...], nw_ref[...])
        h = h.astype(BF16)
        puv_ref[...] = jnp.dot(h, wuv_ref[...], preferred_element_type=F32)
        pxbc_ref[...] = jnp.dot(h, wxbc_ref[...], preferred_element_type=F32)
        pz_ref[...] = jnp.dot(h, wz_ref[...], preferred_element_type=F32)
        pdt_ref[...] = jnp.dot(h, wdt_ref[...], preferred_element_type=F32)

    return _rows_call(
        "inproj_fwd", body, tm, [x], [nw, w_uv, w_xbc, w_z, w_dt],
        [_sds((n_tok, 2 * GM_WIDTH), F32), _sds((n_tok, CONV_CH), F32), _sds((n_tok, SSM_WIDTH), F32),
         _sds((n_tok, DT_PAD), F32)], carried=carried)


def _head_lane_mask(width, head):
    lane = lax.broadcasted_iota(jnp.int32, (1, width), 1)
    return (lane // HEAD_DIM) == head


def _split_terms(x, terms):
    parts = []
    for _ in range(terms):
        p = x.astype(BF16)
        parts.append(p)
        x = x - p.astype(F32)
    return parts


def _seg_dots(vals, ind, terms=2):
    m = vals[0].shape[0]
    parts = []
    for v in vals:
        parts += _split_terms(v, terms)
    red = jnp.dot(jnp.concatenate(parts, axis=0), ind, preferred_element_type=F32)
    outs = []
    for i in range(len(vals)):
        acc = red[i * terms * m:(i * terms + 1) * m]
        for t in range(1, terms):
            acc = acc + red[(i * terms + t) * m:(i * terms + t + 1) * m]
        outs.append(acc)
    return outs


def _tri_dot(mask, x, terms=3):
    n = x.shape[1]
    red = jnp.dot(mask.astype(BF16), jnp.concatenate(_split_terms(x, terms), axis=1), preferred_element_type=F32)
    acc = red[:, :n]
    for t in range(1, terms):
        acc = acc + red[:, t * n:(t + 1) * n]
    return acc


def _gmlp_common(puv, lnw, lnb, e_bf, et_bf):
    u = puv[:, :GM_WIDTH]
    v = puv[:, GM_WIDTH:]
    gu, tu = _gelu(u)
    gv, tv = _gelu(v)
    (s1,) = _seg_dots([gv], et_bf)
    (mu,) = _seg_dots([s1 * (1.0 / HEAD_DIM)], e_bf)
    xc = gv - mu
    (s2,) = _seg_dots([xc * xc], et_bf)
    (rstd,) = _seg_dots([lax.rsqrt(s2 * (1.0 / HEAD_DIM) + EPS)], e_bf)
    xhat = xc * rstd
    vn = xhat * lnw + lnb
    return u, v, gu, tu, tv, rstd, xhat, vn


def _tril_mask():
    r = lax.broadcasted_iota(jnp.int32, (CHUNK, CHUNK), 0)
    c = lax.broadcasted_iota(jnp.int32, (CHUNK, CHUNK), 1)
    return r >= c


def _head_blocks(v):
    return jnp.concatenate([jnp.where(_head_lane_mask(GM_WIDTH, h), v, jnp.zeros_like(v)) for h in range(N_HEADS)], axis=0)


def _causal_w_cat(w_cat):
    t = lax.broadcasted_iota(jnp.int32, (CHUNK, N_HEADS * CHUNK), 0)
    s = lax.broadcasted_iota(jnp.int32, (CHUNK, N_HEADS * CHUNK), 1) % CHUNK
    return jnp.where(t >= s, w_cat, 0.0).astype(BF16)


def _gmlp_chunk_fwd(puv, lnw, lnb, e_bf, et_bf, wm, bmap):
    _, _, gu, _, _, _, _, vn = _gmlp_common(puv, lnw, lnb, e_bf, et_bf)
    mixed = jnp.dot(wm, _head_blocks(vn.astype(BF16)), preferred_element_type=F32) + bmap
    return (gu * mixed).astype(BF16)


SUBLANES = 8


def _shift_down(x, tail, s):
    main = pltpu.roll(x, s, 0)
    row = lax.broadcasted_iota(jnp.int32, (SUBLANES, 1), 0)
    head = jnp.where(row < s, pltpu.roll(tail, s, 0), main[:SUBLANES])
    return jnp.concatenate([head, main[SUBLANES:]], axis=0)


def _shift_up(x, head_next, s):
    n = x.shape[0]
    main = pltpu.roll(x, n - s, 0)
    row = lax.broadcasted_iota(jnp.int32, (SUBLANES, 1), 0)
    last = jnp.where(row >= SUBLANES - s, pltpu.roll(head_next, SUBLANES - s, 0), main[n - SUBLANES:])
    return jnp.concatenate([main[:n - SUBLANES], last], axis=0)


def _ssd_pre(xr, tail, cw_ref, cb, pdt, dtb, alog, emap):
    rowi = lax.broadcasted_iota(jnp.int32, (CHUNK, 1), 0)
    shifted = [_shift_down(xr, tail, 3), _shift_down(xr, tail, 2), _shift_down(xr, tail, 1), xr]
    xc = cb
    for k in range(CONV_K):
        xc = xc + cw_ref[k] * shifted[k]
    sg = _sigmoid(xc)
    xa = xc * sg
    pre = pdt + dtb
    dt = jnp.maximum(pre, 0.0) + jnp.log(1.0 + jnp.exp(-jnp.abs(pre)))
    a_neg = -jnp.exp(alog)
    a_cs = _tri_dot(_tril_mask(), dt * a_neg)
    acs_map, dt_map = _seg_dots([a_cs, dt], emap, terms=3)
    return dict(shifted=shifted, xc=xc, sg=sg, xa=xa, pre=pre, dt=dt, a_neg=a_neg, a_cs=a_cs,
                acs_map=acs_map, dt_map=dt_map, rowi=rowi)


def _ssd_maps(p):
    last = p["rowi"] == CHUNK - 1
    aq_map = jnp.sum(jnp.where(last, p["acs_map"], 0.0), axis=0, keepdims=True)
    e_exp = jnp.exp(p["acs_map"])
    dte = jnp.exp(aq_map - p["acs_map"])
    cd = jnp.exp(aq_map)
    return last, e_exp, dte, cd


def _head_decay(a_cs, a_cs_t, head, tri):
    lane = lax.broadcasted_iota(jnp.int32, (1, DT_PAD), 1)
    sub = lax.broadcasted_iota(jnp.int32, (DT_PAD, 1), 0)
    col = jnp.sum(jnp.where(lane == head, a_cs, 0.0), axis=1, keepdims=True)
    row = jnp.sum(jnp.where(sub == head, a_cs_t, 0.0), axis=0, keepdims=True)
    return jnp.exp(jnp.where(tri, col - row, -1e30))


def _gate_fwd(y, z, nw):
    sz = _sigmoid(z)
    zg = z * sz
    yg = y * zg
    outs, rs = [], []
    for g in range(SSM_GROUPS):
        gs = slice(g * GROUP_W, (g + 1) * GROUP_W)
        o, r = _rms_fwd(yg[:, gs], nw[:, gs])
        outs.append(o)
        rs.append(r)
    return sz, zg, yg, outs, rs


def _ssd_const_specs():
    def whole(shape):
        nd = len(shape)
        return pl.BlockSpec(tuple(shape), lambda c: (0,) * nd)
    return [whole((CONV_K, 1, CONV_CH)), whole((1, CONV_CH)), whole((1, DT_PAD)), whole((1, DT_PAD)),
            whole((1, SSM_WIDTH)), whole((1, SSM_WIDTH)), whole((DT_PAD, SSM_WIDTH)), whole((SSM_WIDTH, DT_PAD))]


def _mixer_fwd(p_uv, p_xbc, p_z, p_dt, x, lnw, lnb, w_cat, bmap, w_out, nw_post, nw_pre2, conv_w, conv_b, dt_bias, a_log,
               dskip_map, norm_w, e_bf, et_bf, n_seq, carried=None):
    n_tok = p_xbc.shape[0]
    nc = n_tok // n_seq // CHUNK

    def body(puv3, xr3, z3, pdt3, x3, lnw_ref, lnb_ref, wcat_ref, bmap_ref, wo_ref, nwa_ref, nwb_ref,
             cw_ref, cb_ref, dtb_ref, alog_ref, dsk_ref, nw_ref, e_ref, et_ref,
             mix3, yssd3, sprev3, o3, x13, h23, h2t3, wm_scr, prev3_scr, s3_scr):
        @pl.when(pl.program_id(0) == 0)
        def _():
            wm_scr[...] = _causal_w_cat(wcat_ref[...])
            prev3_scr[...] = jnp.zeros_like(prev3_scr)
            s3_scr[...] = jnp.zeros_like(s3_scr)

        for b in range(n_seq):
            one_sequence(puv3.at[b], xr3.at[b], z3.at[b], pdt3.at[b], lnw_ref, lnb_ref, bmap_ref,
                         cw_ref, cb_ref, dtb_ref, alog_ref, dsk_ref, nw_ref, e_ref, et_ref,
                         mix3.at[b], yssd3.at[b], sprev3.at[b], wm_scr, prev3_scr.at[b], s3_scr.at[b])
            o = jnp.dot(mix3[b], wo_ref[...], preferred_element_type=F32)
            on, _ = _rms_fwd(o, nwa_ref[...])
            x1 = x3[b] + on
            h2, _ = _rms_fwd(x1, nwb_ref[...])
            o3[b] = o
            x13[b] = x1
            h23[b] = h2.astype(BF16)
            h2t3[b] = h2.T.astype(BF16)

    def one_sequence(puv_ref, xr_ref, z_ref, pdt_ref, lnw_ref, lnb_ref, bmap_ref,
                     cw_ref, cb_ref, dtb_ref, alog_ref, dsk_ref, nw_ref, e_ref, et_ref,
                     mix_ref, yssd_ref, sprev_ref, wm_scr, prev_scr, s_scr):
        mix_ref[:, :GM_WIDTH] = _gmlp_chunk_fwd(puv_ref[...], lnw_ref[...], lnb_ref[...], e_ref[...], et_ref[...], wm_scr[...],
                                      bmap_ref[...])
        xr = xr_ref[...]
        p = _ssd_pre(xr, prev_scr[...], cw_ref, cb_ref[...], pdt_ref[...], dtb_ref[...], alog_ref[...], e_ref[...])
        _, e_exp, dte, cd = _ssd_maps(p)
        xs = p["xa"][:, :SSM_WIDTH]
        xd = xs * p["dt_map"]
        a_cs_t = p["a_cs"].T
        tri = _tril_mask()
        s_old = s_scr[...]
        sprev_ref[...] = s_old
        for g in range(SSM_GROUPS):
            gs = slice(g * GROUP_W, (g + 1) * GROUP_W)
            bm = p["xa"][:, SSM_WIDTH + g * SSM_STATE: SSM_WIDTH + (g + 1) * SSM_STATE].astype(BF16)
            cm = p["xa"][:, SSM_WIDTH + (SSM_GROUPS + g) * SSM_STATE: SSM_WIDTH + (SSM_GROUPS + g + 1) * SSM_STATE].astype(BF16)
            cb_mat = _dot_nt(cm, bm)
            xdg = xd[:, gs].astype(BF16)
            y_g = _dot(cm, s_old[:, gs]) * e_exp[:, gs] + dsk_ref[:, gs] * xs[:, gs]
            for r in range(SSM_GROUPS * 2):
                dm = _head_decay(p["a_cs"], a_cs_t, g * 4 + r, tri)
                full = jnp.dot((cb_mat * dm).astype(BF16), xdg, preferred_element_type=F32)
                y_g = y_g + jnp.where(_head_lane_mask(GROUP_W, r), full, 0.0)
            yssd_ref[:, gs] = y_g
            s_scr[:, gs] = cd[:, gs] * s_old[:, gs] + _dot_tn(bm, xd[:, gs] * dte[:, gs])
        _, _, _, outs, _ = _gate_fwd(yssd_ref[...], z_ref[...], nw_ref[...])
        for g in range(SSM_GROUPS):
            mix_ref[:, GM_WIDTH + g * GROUP_W:GM_WIDTH + (g + 1) * GROUP_W] = outs[g].astype(BF16)
        prev_scr[...] = xr[CHUNK - SUBLANES:, :]

    seq_len = n_tok // n_seq

    def rows(width):
        return pl.BlockSpec((n_seq, CHUNK, width), lambda c: (0, c, 0))

    def whole(shape):
        nd = len(shape)
        return pl.BlockSpec(tuple(shape), lambda c: (0,) * nd)

    def by_seq(a):
        return a.reshape(n_seq, seq_len, a.shape[-1])

    outs = _call_carrying(
        body, carried, name="mixer_fwd", grid=(nc,),
        in_specs=[rows(2 * GM_WIDTH), rows(CONV_CH), rows(SSM_WIDTH), rows(DT_PAD), rows(D_MODEL), whole(lnw.shape),
                  whole(lnb.shape), whole(w_cat.shape), whole(bmap.shape), whole(w_out.shape), whole(nw_post.shape),
                  whole(nw_pre2.shape)] + _ssd_const_specs(),
        out_specs=[rows(D_MODEL), rows(SSM_WIDTH), rows(SSM_WIDTH), rows(D_MODEL), rows(D_MODEL), rows(D_MODEL),
                   pl.BlockSpec((n_seq, D_MODEL, CHUNK), lambda c: (0, 0, c))],
        out_shape=(_sds((n_seq, seq_len, D_MODEL), BF16),
                   _sds((n_seq, seq_len, SSM_WIDTH), F32), _sds((n_seq, seq_len, SSM_WIDTH), F32),
                   _sds((n_seq, seq_len, D_MODEL), F32), _sds((n_seq, seq_len, D_MODEL), F32),
                   _sds((n_seq, seq_len, D_MODEL), BF16), _sds((n_seq, D_MODEL, seq_len), BF16)),
        scratch_shapes=[pltpu.VMEM((CHUNK, N_HEADS * CHUNK), BF16), pltpu.VMEM((n_seq, SUBLANES, CONV_CH), F32),
                        pltpu.VMEM((n_seq, SSM_STATE, SSM_WIDTH), F32)],
        operands=[by_seq(p_uv), by_seq(p_xbc), by_seq(p_z), by_seq(p_dt), by_seq(x), lnw, lnb, w_cat, bmap, w_out, nw_post,
                  nw_pre2, conv_w, conv_b, dt_bias, a_log, dskip_map, norm_w, e_bf, et_bf])
    return tuple(o.reshape(n_tok, o.shape[-1]) for o in outs[:6]) + tuple(outs[6:])


def _up_cols(wup_ref, j):
    per = (D_FF // N_CHIPS) // FF_TILE
    return wup_ref[j // per, :, (j % per) * FF_TILE:(j % per + 1) * FF_TILE]


def _down_rows(wda_ref, wdb_ref, j):
    assert 2 * FF_TILE == D_FF // N_CHIPS
    return (wda_ref if j % 2 == 0 else wdb_ref)[j // 2]


def _skewed_rows_call(name, main, tail, tm, lead_ins, lag_ins, const_ins, lead_outs, lag_outs, acc_outs, carry,
                      streamed, tile_copies, n_copies):
    n_rows = lead_ins[0].shape[0]
    assert n_rows % tm == 0
    n = n_rows // tm
    counts = [len(lead_ins), len(lag_ins), len(const_ins), len(streamed), len(lead_outs), len(lag_outs), len(acc_outs),
              1, len(streamed)]

    def kern(*refs):
        groups, pos = [], 0
        for cnt in counts:
            groups.append(refs[pos:pos + cnt])
            pos += cnt
        lead_i, lag_i, consts, w_hbm, lead_o, lag_o, accs, (carry_scr,), w_vmem = groups
        sems = refs[pos]
        i = pl.program_id(0)
        pieces, k = [], 0
        for piece in tile_copies(w_hbm, w_vmem):
            pieces.append([pltpu.make_async_copy(src, dst, sems.at[k + q]) for q, (src, dst) in enumerate(piece)])
            k += len(piece)

        def ready(j):
            for cp in pieces[j]:
                cp.wait()

        @pl.when(i == 0)
        def _():
            for piece in pieces:
                for cp in piece:
                    cp.start()
            for a in accs:
                a[...] = jnp.zeros_like(a)
            carry_scr[...] = main(lead_i, consts, lead_o, w_vmem, ready)

        @pl.when(jnp.logical_and(i > 0, i < n))
        def _():
            previous = carry_scr[...]
            carry_scr[...] = main(lead_i, consts, lead_o, w_vmem, lambda j: None)
            tail(previous, lag_i, consts, lag_o, accs)

        @pl.when(i == n)
        def _():
            tail(carry_scr[...], lag_i, consts, lag_o, accs)

    def lead(width):
        return pl.BlockSpec((tm, width), lambda i: (jnp.minimum(i, n - 1), 0))

    def lag(width):
        return pl.BlockSpec((tm, width), lambda i: (jnp.maximum(i - 1, 0), 0))

    def whole(shape):
        nd = len(shape)
        return pl.BlockSpec(tuple(shape), lambda i: (0,) * nd)

    return pl.pallas_call(
        kern, name=name, grid=(n + 1,),
        in_specs=([lead(a.shape[1]) for a in lead_ins] + [lag(a.shape[1]) for a in lag_ins]
                  + [whole(a.shape) for a in const_ins] + [_HBM] * len(streamed)),
        out_specs=[lead(s.shape[1]) for s in lead_outs] + [lag(s.shape[1]) for s in lag_outs] + [whole(s.shape) for s in acc_outs],
        out_shape=tuple(lead_outs) + tuple(lag_outs) + tuple(acc_outs),
        scratch_shapes=([pltpu.VMEM(carry, F32)] + [pltpu.VMEM(a.shape, a.dtype) for a in streamed]
                        + [pltpu.SemaphoreType.DMA((n_copies,))]),
        compiler_params=_cparams(1),
    )(*lead_ins, *lag_ins, *const_ins, *streamed)


def _mlp_weight_pieces(order):
    per = (D_FF // N_CHIPS) // FF_TILE

    def tile_copies(hbm, vmem):
        pieces = []
        for j in range(D_FF // FF_TILE):
            cols = (j // per, slice(None), pl.ds((j % per) * FF_TILE, FF_TILE))
            up = (hbm[0].at[cols], vmem[0].at[cols])
            down = (hbm[1 + j % 2].at[j // 2], vmem[1 + j % 2].at[j // 2])
            pieces.append([up, down] if order == "up_down" else [down, up])
        return pieces

    return tile_copies


def _mlp_fwd(h2, x1, tgt, w_up, w_down_a, w_down_b, nw, tm=512):
    n_tok = x1.shape[0]

    def main(lead_i, consts, lead_o, weights, ready):
        (h2_ref,), (f_ref,), (wup_ref, wda_ref, wdb_ref) = lead_i, lead_o, weights
        h2v = h2_ref[...]
        acc = jnp.zeros((tm, D_MODEL), F32)
        for j in range(D_FF // FF_TILE):
            cs = slice(j * FF_TILE, (j + 1) * FF_TILE)
            ready(j)
            u = jnp.dot(h2v, _up_cols(wup_ref, j), preferred_element_type=F32)
            f = jnp.square(jnp.maximum(u, 0.0)).astype(BF16)
            f_ref[:, cs] = f
            acc = acc + jnp.dot(f, _down_rows(wda_ref, wdb_ref, j), preferred_element_type=F32)
        return acc

    def tail(acc, lag_i, consts, lag_o, accs):
        (x1_ref, tgt_ref), (nw_ref,), (dd_ref, dy_ref), (loss_ref, dnw_ref) = lag_i, consts, lag_o, accs
        dn, r = _rms_fwd(acc, nw_ref[...])
        e = x1_ref[...] + dn - tgt_ref[...]
        loss_ref[...] += jnp.full(loss_ref.shape, (0.5 / D_MODEL) * jnp.sum(e * e), F32)
        dy = e * (1.0 / D_MODEL)
        dd, dnw = _rms_bwd(acc, r, nw_ref[...], dy)
        dy_ref[...] = dy
        dd_ref[...] = dd.astype(BF16)
        dnw_ref[...] += dnw

    return _skewed_rows_call(
        "mlp_fwd", main, tail, tm, [h2], [x1, tgt], [nw],
        [_sds((n_tok, D_FF), BF16)], [_sds((n_tok, D_MODEL), BF16), _sds((n_tok, D_MODEL), F32)],
        [_sds((8, 128), F32), _sds((1, D_MODEL), F32)], carry=(tm, D_MODEL),
        streamed=[w_up, w_down_a, w_down_b], tile_copies=_mlp_weight_pieces("up_down"), n_copies=2 * (D_FF // FF_TILE))


def _mlp_bwd(dd, f, x1, dy, w_down_a, w_down_b, w_up, nw, tm=256):
    n_tok = x1.shape[0]

    def main(lead_i, consts, lead_o, weights, ready):
        (dd_ref, f_ref), (dup_ref,), (wup_ref, wda_ref, wdb_ref) = lead_i, lead_o, weights
        ddv = dd_ref[...]
        acc = jnp.zeros((tm, D_MODEL), F32)
        for j in range(D_FF // FF_TILE):
            cs = slice(j * FF_TILE, (j + 1) * FF_TILE)
            ready(j)
            df = _dot_nt(ddv, _down_rows(wda_ref, wdb_ref, j))
            du = (df * (2.0 * jnp.sqrt(f_ref[:, cs].astype(F32)))).astype(BF16)
            dup_ref[:, cs] = du
            acc = acc + _dot_nt(du, _up_cols(wup_ref, j))
        return acc

    def tail(acc, lag_i, consts, lag_o, accs):
        (x1_ref, dy_ref), (nw_ref,), (dx1_ref,), (dnw_ref,) = lag_i, consts, lag_o, accs
        x1v = x1_ref[...]
        _, r = _rms_fwd(x1v, nw_ref[...])
        dx, dnw = _rms_bwd(x1v, r, nw_ref[...], acc)
        dx1_ref[...] = dy_ref[...] + dx
        dnw_ref[...] += dnw

    return _skewed_rows_call(
        "mlp_bwd", main, tail, tm, [dd, f], [x1, dy], [nw],
        [_sds((n_tok, D_FF), BF16)], [_sds((n_tok, D_MODEL), F32)], [_sds((1, D_MODEL), F32)], carry=(tm, D_MODEL),
        streamed=[w_up, w_down_a, w_down_b], tile_copies=_mlp_weight_pieces("down_up"), n_copies=2 * (D_FF // FF_TILE))


def _outproj_bwd(dx1, o, w_out, nw, tm=256, carried=None):
    n_tok = dx1.shape[0]

    def body(dx1_ref, o_ref, wo_ref, nw_ref, do_ref, dya_ref, dyb_ref, dnw_ref):
        ov = o_ref[...]
        _, r = _rms_fwd(ov, nw_ref[...])
        do, dnw = _rms_bwd(ov, r, nw_ref[...], dx1_ref[...])
        dob = do.astype(BF16)
        do_ref[...] = dob
        dya_ref[...] = _dot_nt(dob, wo_ref[:GM_WIDTH, :])
        dyb_ref[...] = _dot_nt(dob, wo_ref[GM_WIDTH:, :])
        dnw_ref[...] += dnw

    return _rows_call("outproj_bwd", body, tm, [dx1, o], [w_out, nw],
                      [_sds((n_tok, D_MODEL), BF16), _sds((n_tok, GM_WIDTH), F32), _sds((n_tok, SSM_WIDTH), F32)],
                      [_sds((1, D_MODEL), F32)], carried=carried)


def _gmlp_bwd(p_uv, dya, lnw, lnb, e_bf, et_bf, w_cat, w_stack, bmap, carried=None):
    n_tok = p_uv.shape[0]
    chunks_per_step = 2

    def body(puv_ref, dya_ref, lnw_ref, lnb_ref, e_ref, et_ref, wcat_ref, wstack_ref, bmap_ref,
             dpuv_ref, dws_ref, dbs_ref, dlnw_ref, dlnb_ref, wm_scr, wsm_scr):
        t_stk = lax.broadcasted_iota(jnp.int32, (N_HEADS * CHUNK, CHUNK), 0) % CHUNK
        s_stk = lax.broadcasted_iota(jnp.int32, (N_HEADS * CHUNK, CHUNK), 1)

        @pl.when(pl.program_id(0) == 0)
        def _():
            wm_scr[...] = _causal_w_cat(wcat_ref[...])
            wsm_scr[...] = jnp.where(t_stk >= s_stk, wstack_ref[...], 0.0).astype(BF16)

        lnw_v = lnw_ref[...]
        e_v, et_v = e_ref[...], et_ref[...]

        def one_chunk(rows):
            u, v, gu, tu, tv, rstd, xhat, vn = _gmlp_common(puv_ref[rows, :], lnw_v, lnb_ref[...], e_v, et_v)
            vnb = vn.astype(BF16)
            mixed = jnp.dot(wm_scr[...], _head_blocks(vnb), preferred_element_type=F32) + bmap_ref[...]
            dy = dya_ref[rows, :]
            du = dy * mixed * _gelu_grad(u, tu)
            dmixed = dy * gu
            (dbs,) = _seg_dots([dmixed], et_v)
            dblocks = _head_blocks(dmixed.astype(BF16))
            dvn = lax.dot_general(wsm_scr[...], dblocks, (((0,), (0,)), ((), ())), preferred_element_type=F32)
            dws = lax.dot_general(dblocks, vnb, (((1,), (1,)), ((), ())), preferred_element_type=F32)
            dxh = dvn * lnw_v
            m1, m2 = _seg_dots([dxh, dxh * xhat], et_v)
            m1, m2 = _seg_dots([m1 * (1.0 / HEAD_DIM), m2 * (1.0 / HEAD_DIM)], e_v)
            dgv = rstd * (dxh - m1 - xhat * m2)
            dv = dgv * _gelu_grad(v, tv)
            dpuv_ref[rows, :GM_WIDTH] = du.astype(BF16)
            dpuv_ref[rows, GM_WIDTH:] = dv.astype(BF16)
            return dbs, dws, jnp.sum(dvn * xhat, axis=0, keepdims=True), jnp.sum(dvn, axis=0, keepdims=True)

        parts = [one_chunk(slice(k * CHUNK, (k + 1) * CHUNK)) for k in range(chunks_per_step)]
        dbs, dws, dlnw, dlnb = [functools.reduce(lambda a, b: a + b, vals) for vals in zip(*parts)]
        dbs_ref[...] += dbs
        dws_ref[...] += jnp.where(t_stk >= s_stk, dws, 0.0)
        dlnw_ref[...] += dlnw
        dlnb_ref[...] += dlnb

    return _rows_call(
        "gmlp_bwd", body, chunks_per_step * CHUNK, [p_uv, dya], [lnw, lnb, e_bf, et_bf, w_cat, w_stack, bmap],
        [_sds((n_tok, 2 * GM_WIDTH), BF16)],
        [_sds((N_HEADS * CHUNK, CHUNK), F32), _sds((CHUNK, DT_PAD), F32), _sds((1, GM_WIDTH), F32),
         _sds((1, GM_WIDTH), F32)],
        scratch=[pltpu.VMEM((CHUNK, N_HEADS * CHUNK), BF16), pltpu.VMEM((N_HEADS * CHUNK, CHUNK), BF16)],
        carried=carried)


def _ssd_bwd(p_xbc, p_z, p_dt, yssd, sprev, dyb, conv_w, conv_b, dt_bias, a_log, dskip_map, norm_w, e_bf, et_bf, n_seq,
             carried=None):
    n_tok = p_xbc.shape[0]
    nc = n_tok // n_seq // CHUNK

    def body(xr3, xprev3, z3, pdt3, yssd3, sprev3, dyb3,
             cw_ref, cb_ref, dtb_ref, alog_ref, dsk_ref, nw_ref, e_ref, et_ref,
             dps3, dcw_ref, dcb_ref, ddtb_ref, dalog_ref, ddsk_ref, dnw_ref,
             ds3_scr, nxt3_scr, dxa3_scr):
        @pl.when(pl.program_id(0) == 0)
        def _():
            for a in (dcw_ref, dcb_ref, ddtb_ref, dalog_ref, ddsk_ref, dnw_ref, ds3_scr, nxt3_scr):
                a[...] = jnp.zeros_like(a)

        for b in range(n_seq):
            one_sequence(xr3.at[b], xprev3.at[b], z3.at[b], pdt3.at[b], yssd3.at[b], sprev3.at[b], dyb3.at[b],
                         cw_ref, cb_ref, dtb_ref, alog_ref, dsk_ref, nw_ref, e_ref, et_ref,
                         dps3.at[b], dcw_ref, dcb_ref, ddtb_ref, dalog_ref, ddsk_ref, dnw_ref,
                         ds3_scr.at[b], nxt3_scr.at[b], dxa3_scr.at[b])

    def one_sequence(xr_ref, xprev_ref, z_ref, pdt_ref, yssd_ref, sprev_ref, dyb_ref,
                     cw_ref, cb_ref, dtb_ref, alog_ref, dsk_ref, nw_ref, e_ref, et_ref,
                     dps_ref, dcw_ref, dcb_ref, ddtb_ref, dalog_ref, ddsk_ref, dnw_ref,
                     ds_scr, nxt_scr, dxa_scr):
        chunk = nc - 1 - pl.program_id(0)
        xr = xr_ref[...]
        prev = jnp.where(chunk == 0, 0.0, xprev_ref[...])
        et_v = et_ref[...]
        p = _ssd_pre(xr, prev, cw_ref, cb_ref[...], pdt_ref[...], dtb_ref[...], alog_ref[...], e_ref[...])
        last, e_exp, dte, cd = _ssd_maps(p)
        rowi = p["rowi"]
        xs = p["xa"][:, :SSM_WIDTH]
        xd = xs * p["dt_map"]
        a_cs_t = p["a_cs"].T
        tri = _tril_mask()
        dsk = dsk_ref[...]
        nw_v = nw_ref[...]

        yv = yssd_ref[...]
        zv = z_ref[...]
        sz, zg, yg, _, rs = _gate_fwd(yv, zv, nw_v)
        dout = dyb_ref[...]
        for g in range(SSM_GROUPS):
            gs = slice(g * GROUP_W, (g + 1) * GROUP_W)
            dyg_g, dnw_g = _rms_bwd(yg[:, gs], rs[g], nw_v[:, gs], dout[:, gs])
            dnw_ref[:, gs] += dnw_g
            dxa_scr[:, gs] = dyg_g
        dyg = dxa_scr[:, :SSM_WIDTH]
        d_y = dyg * zg
        dps_ref[:, CONV_CH:CONV_CH + SSM_WIDTH] = (dyg * yv * (sz + zv * sz * (1.0 - sz))).astype(BF16)

        s_prev = sprev_ref[...]
        ds_next = ds_scr[...]
        lane_dt = lax.broadcasted_iota(jnp.int32, (1, DT_PAD), 1)
        da_cols = jnp.zeros((CHUNK, DT_PAD), F32)
        for g in range(SSM_GROUPS):
            gs = slice(g * GROUP_W, (g + 1) * GROUP_W)
            b_off = SSM_WIDTH + g * SSM_STATE
            c_off = SSM_WIDTH + (SSM_GROUPS + g) * SSM_STATE
            bm = p["xa"][:, b_off:b_off + SSM_STATE].astype(BF16)
            cm = p["xa"][:, c_off:c_off + SSM_STATE].astype(BF16)
            cb_mat = _dot_nt(cm, bm)
            d_yg = d_y[:, gs]
            d_ygb = d_yg.astype(BF16)
            xdg = xd[:, gs]
            xdgb = xdg.astype(BF16)
            ds_g = ds_next[:, gs]
            sp_g = s_prev[:, gs]
            bds = _dot(bm, ds_g)
            dcs = d_yg * e_exp[:, gs]
            d_c = _dot_nt(dcs, sp_g)
            ds_scr[:, gs] = cd[:, gs] * ds_g + _dot_tn(cm, dcs)
            d_b = _dot_nt(xdg * dte[:, gs], ds_g)
            dxd_g = bds * dte[:, gs]
            sum_dcb = jnp.zeros((CHUNK, CHUNK), F32)
            for r in range(SSM_GROUPS * 2):
                head = g * 4 + r
                mask = _head_lane_mask(GROUP_W, r)
                dm = _head_decay(p["a_cs"], a_cs_t, head, tri)
                m_mat = cb_mat * dm
                g_mat = _dot_nt(jnp.where(mask, d_yg, 0.0), xdgb)
                w_mat = g_mat * m_mat
                sum_dcb = sum_dcb + g_mat * dm
                dxd_g = dxd_g + jnp.where(mask, _dot_tn(m_mat, d_ygb), 0.0)
                da_h = jnp.sum(w_mat - w_mat.T, axis=1, keepdims=True)
                da_cols = da_cols + jnp.where(lane_dt == head, da_h, 0.0)
            d_c = d_c + _dot(sum_dcb, bm)
            d_b = d_b + _dot_tn(sum_dcb, cm)
            dxa_scr[:, b_off:b_off + SSM_STATE] = d_b
            dxa_scr[:, c_off:c_off + SSM_STATE] = d_c
            y_off_g = _dot(cm, sp_g) * e_exp[:, gs]
            t3 = bds * xdg * dte[:, gs]
            tail = jnp.sum(t3, axis=0, keepdims=True) + jnp.sum(ds_g * sp_g, axis=0, keepdims=True) * cd[:, gs]
            pre_g = d_yg * y_off_g - t3 + jnp.where(last, tail, 0.0)
            s_pre, ddt_g, s_dsk = _seg_dots([pre_g, dxd_g * xs[:, gs], d_yg * xs[:, gs]], et_v[gs, :])
            da_cols = da_cols + s_pre
            ddsk_ref[...] += jnp.sum(s_dsk, axis=0, keepdims=True)
            dxa_scr[:, gs] = dxd_g * p["dt_map"][:, gs] + dsk[:, gs] * d_yg
            if g == 0:
                ddt = ddt_g
            else:
                ddt = ddt + ddt_g
        r_i = lax.broadcasted_iota(jnp.int32, (CHUNK, CHUNK), 0)
        c_i = lax.broadcasted_iota(jnp.int32, (CHUNK, CHUNK), 1)
        ddta = _tri_dot(r_i <= c_i, da_cols, terms=2)
        ddt = ddt + ddta * p["a_neg"]
        dalog_ref[...] += jnp.sum(ddta * p["dt"], axis=0, keepdims=True) * p["a_neg"]
        draw = ddt * _sigmoid(p["pre"])
        ddtb_ref[...] += jnp.sum(draw, axis=0, keepdims=True)
        dps_ref[:, CONV_CH + SSM_WIDTH:] = draw.astype(BF16)

        xc = p["xc"]
        sg = p["sg"]
        dxc = dxa_scr[...] * (sg + xc * sg * (1.0 - sg))
        dcb_ref[...] += jnp.sum(dxc, axis=0, keepdims=True)
        for k in range(CONV_K):
            dcw_ref[k] += jnp.sum(dxc * p["shifted"][k], axis=0, keepdims=True)
        nxt = nxt_scr[...]
        dxr = cw_ref[3] * dxc
        for s in range(1, CONV_K):
            dxr = dxr + cw_ref[CONV_K - 1 - s] * _shift_up(dxc, nxt, s)
        dps_ref[:, :CONV_CH] = dxr.astype(BF16)
        nxt_scr[...] = dxc[:SUBLANES, :]

    seq_len = n_tok // n_seq

    def rows(width):
        return pl.BlockSpec((n_seq, CHUNK, width), lambda s: (0, nc - 1 - s, 0))

    tiles = CHUNK // SUBLANES
    prev_rows = pl.BlockSpec((n_seq, SUBLANES, CONV_CH), lambda s: (0, jnp.maximum((nc - 1 - s) * tiles - 1, 0), 0))

    def whole(shape):
        nd = len(shape)
        return pl.BlockSpec(tuple(shape), lambda s: (0,) * nd)

    def by_seq(a):
        return a.reshape(n_seq, seq_len, a.shape[-1])

    acc_shapes = [(CONV_K, 1, CONV_CH), (1, CONV_CH), (1, DT_PAD), (1, DT_PAD), (1, DT_PAD), (1, SSM_WIDTH)]
    xbc3 = by_seq(p_xbc)
    outs = _call_carrying(
        body, carried, name="ssd_bwd", grid=(nc,),
        in_specs=[rows(CONV_CH), prev_rows, rows(SSM_WIDTH), rows(DT_PAD), rows(SSM_WIDTH), rows(SSM_WIDTH),
                  rows(SSM_WIDTH)] + _ssd_const_specs(),
        out_specs=[rows(SSD_COLS)] + [whole(s) for s in acc_shapes],
        out_shape=tuple([_sds((n_seq, seq_len, SSD_COLS), BF16)] + [_sds(s, F32) for s in acc_shapes]),
        scratch_shapes=[pltpu.VMEM((n_seq, SSM_STATE, SSM_WIDTH), F32), pltpu.VMEM((n_seq, SUBLANES, CONV_CH), F32),
                        pltpu.VMEM((n_seq, CHUNK, CONV_CH), F32)],
        operands=[xbc3, xbc3, by_seq(p_z), by_seq(p_dt), by_seq(yssd), by_seq(sprev), by_seq(dyb), conv_w, conv_b, dt_bias,
                  a_log, dskip_map, norm_w, e_bf, et_bf])
    return (outs[0].reshape(n_tok, SSD_COLS),) + tuple(outs[1:])


def _inproj_bwd(dp_uv, dp_ssd, x, dx1, w_uv, w_xbc, w_z, w_dt, nw, tm=512, carried=None):
    n_tok = x.shape[0]

    def body(duv_ref, dssd_ref, x_ref, dx1_ref, wuv_ref, wxbc_ref, wz_ref, wdt_ref, nw_ref, gx_ref, h_ref, dnw_ref):
        dh = (_dot_nt(duv_ref[...], wuv_ref[...]) + _dot_nt(dssd_ref[:, :CONV_CH], wxbc_ref[...])
              + _dot_nt(dssd_ref[:, CONV_CH:CONV_CH + SSM_WIDTH], wz_ref[...])
              + _dot_nt(dssd_ref[:, CONV_CH + SSM_WIDTH:], wdt_ref[...]))
        xv = x_ref[...]
        h, r = _rms_fwd(xv, nw_ref[...])
        dx, dnw = _rms_bwd(xv, r, nw_ref[...], dh)
        gx_ref[...] = dx1_ref[...] + dx
        h_ref[...] = h.astype(BF16)
        dnw_ref[...] += dnw

    return _rows_call("inproj_bwd", body, tm, [dp_uv, dp_ssd, x, dx1], [w_uv, w_xbc, w_z, w_dt, nw],
                      [_sds((n_tok, D_MODEL), F32), _sds((n_tok, D_MODEL), BF16)], [_sds((1, D_MODEL), F32)],
                      carried=carried)


def _const_maps():
    lane = jnp.arange(SSM_WIDTH) // HEAD_DIM
    e_bf = (jnp.arange(DT_PAD)[:, None] == lane[None, :]).astype(BF16)
    return e_bf, e_bf.T


def _pad_lanes(v, width):
    return jnp.pad(v, ((0, 0), (0, width - v.shape[1])))


SHARD_COLS = IN_COLS // N_CHIPS
_UV_END = 2 * GM_WIDTH
_Z_END = _UV_END + SSM_WIDTH
_XBC_END = _Z_END + CONV_CH


def _cols_from_shards(w4, lo, hi):
    pieces = []
    for j in range(N_CHIPS):
        a, b = max(lo, j * SHARD_COLS), min(hi, (j + 1) * SHARD_COLS)
        if a < b:
            pieces.append(w4[j][:, a - j * SHARD_COLS:b - j * SHARD_COLS])
    return pieces[0] if len(pieces) == 1 else jnp.concatenate(pieces, axis=1)


def _shards_from_cols(blocks):
    shards = []
    for j in range(N_CHIPS):
        pieces = []
        for arr, lo, hi in blocks:
            a, b = max(lo, j * SHARD_COLS), min(hi, (j + 1) * SHARD_COLS)
            if a < b:
                pieces.append(arr[:, a - lo:b - lo])
        shards.append(pieces[0] if len(pieces) == 1 else jnp.concatenate(pieces, axis=1))
    return jnp.stack(shards)


def _forward_backward(x, tgt, w_in4, conv_w, small, out_shard, up_shard, down_shard, core, adam_args):
    n_seq, seq_len, _ = x.shape
    n_tok = n_seq * seq_len
    x2 = x.reshape(n_tok, D_MODEL)
    tgt2 = tgt.reshape(n_tok, D_MODEL)
    e_bf, et_bf = _const_maps()

    w_uv = _cols_from_shards(w_in4, 0, _UV_END)
    w_z = _cols_from_shards(w_in4, _UV_END, _Z_END)
    w_xbc = _cols_from_shards(w_in4, _Z_END, _XBC_END)
    w_dt = _pad_lanes(_cols_from_shards(w_in4, _XBC_END, IN_COLS), DT_PAD)

    nw_pre = small["norm_mix_pre"]
    lnw = small["gm_ln_w"].reshape(1, GM_WIDTH)
    lnb = small["gm_ln_b"].reshape(1, GM_WIDTH)
    w_stack = small["gm_w_s"].reshape(N_HEADS * CHUNK, CHUNK)
    w_cat = jnp.transpose(small["gm_w_s"], (1, 0, 2)).reshape(CHUNK, N_HEADS * CHUNK)
    bmap = jnp.repeat(small["gm_b_s"].T, HEAD_DIM, axis=1)
    cw3 = conv_w.reshape(CONV_K, 1, CONV_CH)
    conv_b = small["conv_b"]
    dt_bias = _pad_lanes(small["dt_bias"], DT_PAD)
    a_log = _pad_lanes(small["a_log"], DT_PAD)
    dskip_map = jnp.repeat(small["d_skip"], HEAD_DIM, axis=1)
    ssm_nw = small["ssm_norm_w"]

    half = down_shard.shape[0] // 2
    p_uv, p_xbc, p_z, p_dt, w_out4, w_down_a = _inproj_fwd(
        x2, nw_pre, w_uv, w_xbc, w_z, w_dt, carried=_allgather_exchange([out_shard, down_shard[:half]]))
    ssd_consts = (cw3, conv_b, dt_bias, a_log, dskip_map, ssm_nw, e_bf, et_bf)
    w_out_b = w_out4.reshape(D_MODEL, D_MODEL)
    mix, yssd, sprev, o, x1, h2, h2t, w_up4, w_down_b = _mixer_fwd(
        p_uv, p_xbc, p_z, p_dt, x2, lnw, lnb, w_cat, bmap, w_out_b, small["norm_mix_post"], small["norm_ffn_pre"],
        *ssd_consts, n_seq, carried=_allgather_exchange([up_shard, down_shard[half:]]))
    f, dd, dy, loss_acc, d_nffn_post = _mlp_fwd(h2, x1, tgt2, w_up4, w_down_a, w_down_b, small["norm_ffn_post"])

    dup, dx1, d_nffn_pre = _mlp_bwd(dd, f, x1, dy, w_down_a, w_down_b, w_up4, small["norm_ffn_pre"])
    tk = min(DW_TOKENS_PER_STEP, n_tok)
    g_up = _matmul_by_sequence("dw_up", h2t, dup, D_MODEL)
    g_down = _matmul_tn("dw_down", f, dd, 1024, D_MODEL, tk).reshape(N_CHIPS, D_FF // N_CHIPS, D_MODEL)
    do, dya, dyb, d_nmix_post, got_up, got_down = _outproj_bwd(
        dx1, o, w_out_b, small["norm_mix_post"], carried=_pair_exchange([g_up, g_down]))
    h_up = _pair_sum(core, g_up, got_up, 512)
    h_down = _pair_sum(core, g_down, got_down, 512)
    g_out = _matmul_tn("dw_out", mix, do, D_MODEL, D_MODEL, tk).reshape(N_CHIPS, D_MODEL // N_CHIPS, D_MODEL)
    dp_uv, d_ws, d_bs_t, d_lnw, d_lnb, slab_up, got_out = _gmlp_bwd(
        p_uv, dya, lnw, lnb, e_bf, et_bf, w_cat, w_stack, bmap,
        carried=_both(_chip_exchange([h_up]), _pair_exchange([g_out])))
    h_out = _pair_sum(core, g_out, got_out, 128)
    early = {
        "gm_ln_w": d_lnw.reshape(N_HEADS, HEAD_DIM), "gm_ln_b": d_lnb.reshape(N_HEADS, HEAD_DIM),
        "gm_w_s": d_ws.reshape(N_HEADS, CHUNK, CHUNK), "gm_b_s": d_bs_t[:, :N_HEADS].T,
        "norm_mix_post": d_nmix_post, "norm_ffn_pre": d_nffn_pre, "norm_ffn_post": d_nffn_post,
    }
    packed_early = _pack(early, tuple(early), tail=loss_acc[0, 0].reshape(1))
    (dp_ssd, d_cw, d_cb, d_dtb, d_alog, d_dsk, d_ssm_nw, slab_down, slab_out, all_early) = _ssd_bwd(
        p_xbc, p_z, p_dt, yssd, sprev, dyb, *ssd_consts, n_seq,
        carried=_both(_chip_exchange([h_down, h_out]), _device_gather_exchange(packed_early)))
    gx, h, d_nmix_pre = _inproj_bwd(dp_uv, dp_ssd, x2, dx1, w_uv, w_xbc, w_z, w_dt, nw_pre)
    late = {
        "norm_mix_pre": d_nmix_pre, "conv_w": d_cw.reshape(CONV_K, CONV_CH), "conv_b": d_cb,
        "dt_bias": d_dtb[:, :N_HEADS], "a_log": d_alog[:, :N_HEADS], "d_skip": d_dsk[:, :N_HEADS],
        "ssm_norm_w": d_ssm_nw,
    }
    g_uv, all_late = _matmul_tn("dw_in_uv", h, dp_uv, D_MODEL, 2 * GM_WIDTH, tk,
                                carried=_device_gather_exchange(_pack(late, tuple(late))))
    sum_early = _ordered_sum("small_sum_early", all_early)
    small_sum = _unpack(sum_early, {n: v.shape for n, v in early.items()}, tuple(early))
    small_sum.update(_unpack(_ordered_sum("small_sum_late", all_late), {n: v.shape for n, v in late.items()}, tuple(late)))
    loss = sum_early.reshape(-1)[sum(v.size for v in early.values())]
    red_up, red_down, red_out = _chip_sum(slab_up, 512), _chip_sum(slab_down, 512), _chip_sum(slab_out, 128)
    g_ssd, oth_up, oth_down, oth_out = _matmul_tn("dw_in_ssd", h, dp_ssd, D_MODEL, SSD_COLS, tk,
                                                  carried=_pair_swap([red_up, red_down, red_out]))
    g_xbc, g_z, g_dt = g_ssd[:, :CONV_CH], g_ssd[:, CONV_CH:CONV_CH + SSM_WIDTH], g_ssd[:, CONV_CH + SSM_WIDTH:]
    g_in = _shards_from_cols([(g_uv, 0, _UV_END), (g_z, _UV_END, _Z_END), (g_xbc, _Z_END, _XBC_END),
                              (g_dt, _XBC_END, IN_COLS)])
    red_in, oth_in = _reduce_scatter_last(g_in)
    res = _adamw_halves("adamw_mlp", [(adam_args["w_up"][0], red_up, oth_up) + adam_args["w_up"][1:],
                                      (adam_args["w_down"][0], red_down, oth_down) + adam_args["w_down"][1:]], 256)
    big_out = {"w_up": res[0:4], "w_down": res[4:8]}
    big_out["w_out"] = _adamw_halves("adamw_w_out", [(adam_args["w_out"][0], red_out, oth_out) + adam_args["w_out"][1:]], 128)
    big_out["w_in"] = _adamw_halves("adamw_w_in", [(adam_args["w_in"][0], red_in, oth_in) + adam_args["w_in"][1:]], 256)

    return loss, gx.reshape(x.shape), big_out, small_sum


_HBM = pl.BlockSpec(memory_space=pltpu.HBM)


D2D_CHUNKS = 8
ICI_CHUNKS = 1
ROW_ALIGN = 16


def _row_chunks(rows, n_chunks):
    size = min(max(rows // n_chunks, ROW_ALIGN), rows)
    assert rows % size == 0
    return [(start, size) for start in range(0, rows, size)]


def _position():
    x, y, c = lax.axis_index("x"), lax.axis_index("y"), lax.axis_index("c")
    chips = [(1 - x, y), (x, 1 - y), (1 - x, 1 - y)]
    return x, y, c, chips


def _allgather_exchange(arrs):
    n = len(arrs)

    def copies(ins, outs, send_sems, recv_sems, local_sems):
        x, y, c, chips = _position()
        me = 2 * x + y
        sibling = (x, y, 1 - c)

        def copy(a, k, src, dst, to):
            return pltpu.make_async_remote_copy(src_ref=src, dst_ref=dst, send_sem=send_sems.at[a, k],
                                                recv_sem=recv_sems.at[a, k], device_id=to, device_id_type=MESH)

        def half_rows(a, pc):
            half = ins[a].shape[0] // 2
            return pl.ds(pc * half, half)

        local = [pltpu.make_async_copy(ins[a], outs[a].at[me], local_sems.at[a]) for a in range(n)]
        ici_out = [[copy(a, k, ins[a].at[half_rows(a, c)], outs[a].at[me, half_rows(a, c)], (px, py, c))
                    for k, (px, py) in enumerate(chips)] for a in range(n)]
        return c, chips, sibling, copy, half_rows, local, ici_out

    def start(ins, outs, send_sems, recv_sems, local_sems):
        c, chips, _, copy, _, local, _ = copies(ins, outs, send_sems, recv_sems, local_sems)
        x, y, _, _ = _position()
        me = 2 * x + y
        for cp in local:
            cp.start()
        for a in range(n):
            half = ins[a].shape[0] // 2
            for k, (px, py) in enumerate(chips):
                for first, size in _row_chunks(half, ICI_CHUNKS):
                    rows = pl.ds(c * half + first, size)
                    copy(a, k, ins[a].at[rows], outs[a].at[me, rows], (px, py, c)).start()

    def finish(ins, outs, send_sems, recv_sems, local_sems):
        c, chips, sibling, copy, half_rows, local, ici_out = copies(ins, outs, send_sems, recv_sems, local_sems)
        for a in range(n):
            half = ins[a].shape[0] // 2
            for k, (px, py) in enumerate(chips):
                blk = outs[a].at[2 * px + py, half_rows(a, c)]
                copy(a, k, blk, blk, (px, py, c)).wait_recv()
                for first, size in _row_chunks(half, D2D_CHUNKS):
                    piece = outs[a].at[2 * px + py, pl.ds(c * half + first, size)]
                    copy(a, 3 + k, piece, piece, sibling).start()
        for a in range(n):
            for k, (px, py) in enumerate(chips):
                theirs = outs[a].at[2 * px + py, half_rows(a, 1 - c)]
                copy(a, 3 + k, theirs, theirs, sibling).wait_recv()
                mine = outs[a].at[2 * px + py, half_rows(a, c)]
                copy(a, 3 + k, mine, mine, sibling).wait_send()
        for a in range(n):
            for cp in ici_out[a]:
                cp.wait_send()
        for cp in local:
            cp.wait()

    return _Carried(arrs, [_sds((N_CHIPS,) + a.shape, a.dtype) for a in arrs],
                    [pltpu.SemaphoreType.DMA((n, 6)), pltpu.SemaphoreType.DMA((n, 6)), pltpu.SemaphoreType.DMA((n,))],
                    start, finish)


def _run_exchange(name, exchange):
    n_in, n_out = len(exchange.ins), len(exchange.out_shapes)

    def body(*refs):
        ins, outs, sems = refs[:n_in], refs[n_in:n_in + n_out], refs[n_in + n_out:]
        exchange.start(ins, outs, *sems)
        exchange.finish(ins, outs, *sems)

    return pl.pallas_call(
        body, name=name, out_shape=tuple(exchange.out_shapes), in_specs=[_HBM] * n_in,
        out_specs=tuple([_HBM] * n_out), scratch_shapes=exchange.sems,
    )(*exchange.ins)


def _pair_exchange(grads):
    n = len(grads)

    def copier(send_sems, recv_sems):
        x, y, c, _ = _position()

        def copy(a, src, dst):
            return pltpu.make_async_remote_copy(src_ref=src, dst_ref=dst, send_sem=send_sems.at[a],
                                                recv_sem=recv_sems.at[a], device_id=(x, y, 1 - c), device_id_type=MESH)
        return c, copy

    def start(ins, got, send_sems, recv_sems):
        c, copy = copier(send_sems, recv_sems)
        for a in range(n):
            half = ins[a].shape[1] // 2
            for slab in range(N_CHIPS):
                for first, size in _row_chunks(half, D2D_CHUNKS):
                    copy(a, ins[a].at[slab, pl.ds((1 - c) * half + first, size), :],
                         got[a].at[slab, pl.ds(first, size), :]).start()

    def finish(ins, got, send_sems, recv_sems):
        c, copy = copier(send_sems, recv_sems)
        for a in range(n):
            half = ins[a].shape[1] // 2
            copy(a, ins[a].at[:, pl.ds((1 - c) * half, half), :], got[a]).wait()

    return _Carried(grads, [_sds((N_CHIPS, g.shape[1] // 2, g.shape[2]), g.dtype) for g in grads],
                    [pltpu.SemaphoreType.DMA((n,)), pltpu.SemaphoreType.DMA((n,))], start, finish)


def _chip_exchange(hsums):
    n = len(hsums)

    def copies(ins, outs, send_sems, recv_sems, local_sems, pieces):
        x, y, c, chips = _position()
        me = 2 * x + y
        cps = []
        for a in range(n):
            cps.append(pltpu.make_async_copy(ins[a].at[me], outs[a].at[me], local_sems.at[a]))
            rows = ins[a].shape[1]
            for k, (px, py) in enumerate(chips):
                for first, size in (_row_chunks(rows, ICI_CHUNKS) if pieces else [(0, rows)]):
                    cps.append(pltpu.make_async_remote_copy(
                        src_ref=ins[a].at[2 * px + py, pl.ds(first, size)], dst_ref=outs[a].at[me, pl.ds(first, size)],
                        send_sem=send_sems.at[a, k], recv_sem=recv_sems.at[a, k], device_id=(px, py, c),
                        device_id_type=MESH))
        return cps

    def start(*refs):
        for cp in copies(*refs, pieces=True):
            cp.start()

    def finish(*refs):
        for cp in copies(*refs, pieces=False):
            cp.wait()

    return _Carried(hsums, [_sds(h.shape, h.dtype) for h in hsums],
                    [pltpu.SemaphoreType.DMA((n, 3)), pltpu.SemaphoreType.DMA((n, 3)), pltpu.SemaphoreType.DMA((n,))],
                    start, finish)


def _pair_swap(reds):
    n = len(reds)

    def copier(send_sems, recv_sems):
        x, y, c, _ = _position()

        def copy(a, src, dst):
            return pltpu.make_async_remote_copy(src_ref=src, dst_ref=dst, send_sem=send_sems.at[a],
                                                recv_sem=recv_sems.at[a], device_id=(x, y, 1 - c), device_id_type=MESH)
        return copy

    def start(ins, outs, send_sems, recv_sems):
        copy = copier(send_sems, recv_sems)
        for a in range(n):
            for first, size in _row_chunks(ins[a].shape[0], 2 * D2D_CHUNKS):
                copy(a, ins[a].at[pl.ds(first, size), :], outs[a].at[pl.ds(first, size), :]).start()

    def finish(ins, outs, send_sems, recv_sems):
        copy = copier(send_sems, recv_sems)
        for a in range(n):
            copy(a, ins[a], outs[a]).wait()

    return _Carried(reds, [_sds(r.shape, r.dtype) for r in reds],
                    [pltpu.SemaphoreType.DMA((n,)), pltpu.SemaphoreType.DMA((n,))], start, finish)


def _reduce_scatter_last(grad):
    _, rows, cols = grad.shape
    half = rows // 2
    pieces = _row_chunks(half, D2D_CHUNKS)

    def body(g_ref, mine_ref, theirs_ref, got_scr, hsum_scr, slab_scr, pair_sems, ici_send, ici_recv, swap_sems):
        x, y, c, chips = _position()
        me = 2 * x + y
        sibling = (x, y, 1 - c)

        def to_sibling(src, dst, sems):
            return pltpu.make_async_remote_copy(src_ref=src, dst_ref=dst, send_sem=sems.at[0], recv_sem=sems.at[1],
                                                device_id=sibling, device_id_type=MESH)

        for slab in range(N_CHIPS):
            for first, size in pieces:
                to_sibling(g_ref.at[slab, pl.ds((1 - c) * half + first, size)], got_scr.at[slab, pl.ds(first, size)],
                           pair_sems).start()
        to_sibling(g_ref.at[:, pl.ds((1 - c) * half, half)], got_scr, pair_sems).wait()
        own = g_ref[:, pl.ds(pl.multiple_of(c * half, half), half), :]
        hsum_scr[...] = (own.astype(F32) + got_scr[...].astype(F32)).astype(BF16)

        slab_scr[me] = hsum_scr[me]
        ici = [pltpu.make_async_remote_copy(src_ref=hsum_scr.at[2 * px + py], dst_ref=slab_scr.at[me],
                                            send_sem=ici_send.at[k], recv_sem=ici_recv.at[k], device_id=(px, py, c),
                                            device_id_type=MESH) for k, (px, py) in enumerate(chips)]
        for cp in ici:
            cp.start()
        for cp in ici:
            cp.wait()
        acc = slab_scr[0].astype(F32)
        for k in range(1, N_CHIPS):
            acc = acc + slab_scr[k].astype(F32)
        mine_ref[...] = acc

        for first, size in pieces:
            to_sibling(mine_ref.at[pl.ds(first, size)], theirs_ref.at[pl.ds(first, size)], swap_sems).start()
        to_sibling(mine_ref, theirs_ref, swap_sems).wait()

    vmem = pl.BlockSpec(memory_space=pltpu.VMEM)
    halves = (N_CHIPS, half, cols)
    return pl.pallas_call(
        body, name="grad_reduce_scatter_last", out_shape=(_sds((half, cols), F32), _sds((half, cols), F32)),
        in_specs=[vmem], out_specs=(vmem, vmem),
        scratch_shapes=[pltpu.VMEM(halves, BF16), pltpu.VMEM(halves, BF16), pltpu.VMEM(halves, BF16),
                        pltpu.SemaphoreType.DMA((2,)), pltpu.SemaphoreType.DMA((3,)), pltpu.SemaphoreType.DMA((3,)),
                        pltpu.SemaphoreType.DMA((2,))],
        compiler_params=pltpu.CompilerParams(vmem_limit_bytes=VMEM_LIMIT_BYTES),
    )(grad)


def _device_gather_exchange(packed):
    def copies(ins, outs, send_sems, recv_sems, local_sem):
        (x_ref,), (all_ref,) = ins, outs
        x, y, c, chips = _position()
        me, sibling = (x, y, c), (x, y, 1 - c)

        def slab(px, py, pc):
            return all_ref.at[4 * px + 2 * py + pc]

        def copy(k, block, to, src=None):
            return pltpu.make_async_remote_copy(
                src_ref=slab(*block) if src is None else src, dst_ref=slab(*block), send_sem=send_sems.at[k],
                recv_sem=recv_sems.at[k], device_id=to, device_id_type=MESH)

        mine = pltpu.make_async_copy(x_ref, slab(*me), local_sem)
        first = [copy(0, me, sibling, src=x_ref)]
        first += [copy(1 + j, me, (*chip, c), src=x_ref) for j, chip in enumerate(chips)]
        passed = [copy(4 + j, (*chip, c), sibling) for j, chip in enumerate(chips)]
        return c, chips, me, sibling, copy, mine, first, passed

    def start(ins, outs, send_sems, recv_sems, local_sem):
        _, _, _, _, _, mine, first, _ = copies(ins, outs, send_sems, recv_sems, local_sem)
        mine.start()
        for cp in first:
            cp.start()

    def finish(ins, outs, send_sems, recv_sems, local_sem):
        c, chips, me, sibling, copy, mine, first, passed = copies(ins, outs, send_sems, recv_sems, local_sem)
        for j, chip in enumerate(chips):
            copy(1 + j, (*chip, c), me).wait_recv()
            passed[j].start()
        copy(0, sibling, me).wait_recv()
        for j, chip in enumerate(chips):
            copy(4 + j, (*chip, 1 - c), me).wait_recv()
        for cp in first + passed:
            cp.wait_send()
        mine.wait()

    return _Carried([packed], [_sds((N_DEV,) + packed.shape, F32)],
                    [pltpu.SemaphoreType.DMA((7,)), pltpu.SemaphoreType.DMA((7,)), pltpu.SemaphoreType.DMA],
                    start, finish)


def _ordered_sum(name, slabs):
    _, m_per, n_cols = slabs.shape

    def body(s_ref, o_ref):
        acc = s_ref[0]
        for d in range(1, N_DEV):
            acc = acc + s_ref[d]
        o_ref[...] = acc

    vmem = pl.BlockSpec(memory_space=pltpu.VMEM)
    return pl.pallas_call(body, name=name, out_shape=_sds((m_per, n_cols), F32), in_specs=[vmem], out_specs=vmem)(slabs)


def _pair_sum(core, own, got, tm):
    _, half, cols = got.shape
    nb = half // tm

    def body(c_ref, a_ref, b_ref, o_ref):
        o_ref[...] = (a_ref[...].astype(F32) + b_ref[...].astype(F32)).astype(BF16)

    return pl.pallas_call(
        body, name="grad_pair_sum", out_shape=_sds(got.shape, BF16),
        grid_spec=pltpu.PrefetchScalarGridSpec(
            num_scalar_prefetch=1, grid=(N_CHIPS, nb),
            in_specs=[pl.BlockSpec((None, tm, cols), lambda s, i, c_ref: (s, c_ref[0] * nb + i, 0)),
                      pl.BlockSpec((None, tm, cols), lambda s, i, c_ref: (s, i, 0))],
            out_specs=pl.BlockSpec((None, tm, cols), lambda s, i, c_ref: (s, i, 0))),
        compiler_params=_cparams(2),
    )(core, own, got)


def _chip_sum(slabs, tm):
    _, half, cols = slabs.shape

    def body(s_ref, o_ref):
        acc = s_ref[0].astype(F32)
        for k in range(1, N_CHIPS):
            acc = acc + s_ref[k].astype(F32)
        o_ref[...] = acc

    return pl.pallas_call(
        body, name="grad_chip_sum", out_shape=_sds((half, cols), F32), grid=(half // tm,),
        in_specs=[pl.BlockSpec((N_CHIPS, tm, cols), lambda i: (0, i, 0))],
        out_specs=pl.BlockSpec((tm, cols), lambda i: (i, 0)), compiler_params=_cparams(1),
    )(slabs)


def _adam_math(w, g, m, v):
    m2 = ADAM_B1 * m + (1.0 - ADAM_B1) * g
    v2 = ADAM_B2 * v + (1.0 - ADAM_B2) * (g * g)
    m_hat = m2 / (1.0 - ADAM_B1 ** ADAM_STEP)
    v_hat = v2 / (1.0 - ADAM_B2 ** ADAM_STEP)
    delta = -ADAM_LR * (m_hat / (jnp.sqrt(v_hat) + ADAM_EPS) + ADAM_WD * w)
    return delta, m2, v2


def _adamw_halves(name, items, tm, carried=None):
    rows, cols = items[0][0].shape
    nb = rows // 2 // tm
    n = len(items)

    def body(*refs):
        mine = (pl.program_id(0) // nb) == lax.axis_index("c")
        for k in range(n):
            w_ref, own_ref, oth_ref, m_ref, v_ref = refs[5 * k:5 * k + 5]
            g_ref, d_ref, m2_ref, v2_ref = refs[5 * n + 4 * k:5 * n + 4 * k + 4]
            g = jnp.where(mine, own_ref[...], oth_ref[...])
            d, m2, v2 = _adam_math(w_ref[...], g, m_ref[...], v_ref[...])
            g_ref[...] = g
            d_ref[...] = d
            m2_ref[...] = m2
            v2_ref[...] = v2

    full = pl.BlockSpec((tm, cols), lambda i: (i, 0))
    half = pl.BlockSpec((tm, cols), lambda i: (i % nb, 0))
    return _call_carrying(
        body, carried, name=name, grid=(rows // tm,), in_specs=[full, half, half, full, full] * n,
        out_specs=[full] * (4 * n), out_shape=tuple([_sds((rows, cols), F32)] * (4 * n)), scratch_shapes=[],
        operands=[a for item in items for a in item])


def _adamw(name, w, g, m, v, tm):
    def body(w_ref, g_ref, m_ref, v_ref, gout_ref, d_ref, m2_ref, v2_ref):
        gv = g_ref[...]
        d, m2, v2 = _adam_math(w_ref[...], gv, m_ref[...], v_ref[...])
        gout_ref[...] = gv
        d_ref[...] = d
        m2_ref[...] = m2
        v2_ref[...] = v2

    return _rows_call(name, body, tm, [w, g, m, v], [], [_sds(w.shape, F32)] * 4)


_SMALL_NAMES = ("norm_mix_pre", "gm_ln_w", "gm_ln_b", "gm_w_s", "gm_b_s", "conv_w", "conv_b", "dt_bias", "a_log",
                "d_skip", "ssm_norm_w", "norm_mix_post", "norm_ffn_pre", "norm_ffn_post")
_PACK_COLS = 1024


def _pack(parts, names=_SMALL_NAMES, tail=None):
    pieces = [parts[n].reshape(-1) for n in names]
    flat = jnp.concatenate(pieces if tail is None else pieces + [tail])
    rows = -(-flat.shape[0] // (8 * _PACK_COLS)) * 8
    flat = jnp.pad(flat, (0, rows * _PACK_COLS - flat.shape[0]))
    return flat.reshape(rows, _PACK_COLS)


def _unpack(packed, shapes, names=_SMALL_NAMES):
    flat = packed.reshape(-1)
    out, off = {}, 0
    for n in names:
        size = 1
        for s in shapes[n]:
            size *= s
        out[n] = flat[off:off + size].reshape(shapes[n])
        off += size
    return out


def kernel(x, norm_mix_pre, w_in, gm_ln_w, gm_ln_b, gm_w_s, gm_b_s, conv_w, conv_b, dt_bias, a_log, d_skip, ssm_norm_w, w_out, norm_mix_post, norm_ffn_pre, w_up, w_down, norm_ffn_post, loss_target, m_norm_mix_pre, m_w_in, m_gm_ln_w, m_gm_ln_b, m_gm_w_s, m_gm_b_s, m_conv_w, m_conv_b, m_dt_bias, m_a_log, m_d_skip, m_ssm_norm_w, m_w_out, m_norm_mix_post, m_norm_ffn_pre, m_w_up, m_w_down, m_norm_ffn_post, v_norm_mix_pre, v_w_in, v_gm_ln_w, v_gm_ln_b, v_gm_w_s, v_gm_b_s, v_conv_w, v_conv_b, v_dt_bias, v_a_log, v_d_skip, v_ssm_norm_w, v_w_out, v_norm_mix_post, v_norm_ffn_pre, v_w_up, v_w_down, v_norm_ffn_post):
    params = dict(norm_mix_pre=norm_mix_pre, w_in=w_in, gm_ln_w=gm_ln_w, gm_ln_b=gm_ln_b, gm_w_s=gm_w_s, gm_b_s=gm_b_s,
                  conv_w=conv_w, conv_b=conv_b, dt_bias=dt_bias, a_log=a_log, d_skip=d_skip, ssm_norm_w=ssm_norm_w,
                  w_out=w_out, norm_mix_post=norm_mix_post, norm_ffn_pre=norm_ffn_pre, w_up=w_up, w_down=w_down,
                  norm_ffn_post=norm_ffn_post)
    mom1 = dict(norm_mix_pre=m_norm_mix_pre, w_in=m_w_in, gm_ln_w=m_gm_ln_w, gm_ln_b=m_gm_ln_b, gm_w_s=m_gm_w_s,
                gm_b_s=m_gm_b_s, conv_w=m_conv_w, conv_b=m_conv_b, dt_bias=m_dt_bias, a_log=m_a_log, d_skip=m_d_skip,
                ssm_norm_w=m_ssm_norm_w, w_out=m_w_out, norm_mix_post=m_norm_mix_post, norm_ffn_pre=m_norm_ffn_pre,
                w_up=m_w_up, w_down=m_w_down, norm_ffn_post=m_norm_ffn_post)
    mom2 = dict(norm_mix_pre=v_norm_mix_pre, w_in=v_w_in, gm_ln_w=v_gm_ln_w, gm_ln_b=v_gm_ln_b, gm_w_s=v_gm_w_s,
                gm_b_s=v_gm_b_s, conv_w=v_conv_w, conv_b=v_conv_b, dt_bias=v_dt_bias, a_log=v_a_log, d_skip=v_d_skip,
                ssm_norm_w=v_ssm_norm_w, w_out=v_w_out, norm_mix_post=v_norm_mix_post, norm_ffn_pre=v_norm_ffn_pre,
                w_up=v_w_up, w_down=v_w_down, norm_ffn_post=v_norm_ffn_post)
    names = list(params)
    big = ("w_in", "w_out", "w_up", "w_down")
    chip = 2 * lax.axis_index("x") + lax.axis_index("y")

    shards = {n: params[n][0].astype(BF16) for n in big}
    conv_shard = jnp.pad(conv_w[0], ((0, 16 - CONV_K), (0, 0)))
    g_in4, g_conv4 = _run_exchange("allgather_w_in", _allgather_exchange([shards["w_in"], conv_shard]))
    conv_full = jnp.transpose(g_conv4[:, :CONV_K, :], (1, 0, 2)).reshape(CONV_K, CONV_CH)

    small = {n: params[n][0] if params[n].ndim >= 3 else params[n] for n in _SMALL_NAMES if n != "conv_w"}
    core = lax.axis_index("c").astype(jnp.int32).reshape(1)
    adam_args = {n: (params[n][0], mom1[n][0], mom2[n][0]) for n in big}
    loss, grad_x, big_out, small_sum = _forward_backward(
        x, loss_target, g_in4, conv_full, small, shards["w_out"], shards["w_up"], shards["w_down"], core, adam_args)
    grads, delta, new_m, new_v = {}, {}, {}, {}
    for n in big:
        grads[n], delta[n], new_m[n], new_v[n] = [a[None] for a in big_out[n]]

    small_sum["conv_w"] = lax.dynamic_slice_in_dim(small_sum["conv_w"], chip * (CONV_CH // N_CHIPS), CONV_CH // N_CHIPS, axis=1)

    local_shapes = {n: params[n].shape[1:] if params[n].ndim >= 3 else params[n].shape for n in _SMALL_NAMES}
    flat = lambda tree: {n: tree[n].reshape(local_shapes[n]) for n in _SMALL_NAMES}
    packed = [_pack(flat(t)) for t in (params, small_sum, mom1, mom2)]
    _, d_p, m_p, v_p = _adamw("adamw_small", *packed, packed[0].shape[0])
    for src, dst in ((d_p, delta), (m_p, new_m), (v_p, new_v)):
        for n, val in _unpack(src, local_shapes).items():
            dst[n] = val.reshape(params[n].shape)
    for n in _SMALL_NAMES:
        grads[n] = small_sum[n].reshape(params[n].shape)

    out = [loss, grad_x]
    for tree in (grads, delta, new_m, new_v):
        out += [tree[n] for n in names]
    return tuple(out)
```

```python
import functools

import jax
import jax.numpy as jnp
from jax import lax
from jax.experimental import pallas as pl
from jax.experimental.pallas import tpu as pltpu

F32 = jnp.float32
BF16 = jnp.bfloat16
MESH = pl.DeviceIdType.MESH

EPS = 1e-6
D_MODEL = 1024
GM_WIDTH = 512
SSM_WIDTH = 512
N_HEADS = 8
HEAD_DIM = 64
CHUNK = 128
SSM_GROUPS = 2
GROUP_W = SSM_WIDTH // SSM_GROUPS
SSM_STATE = 128
CONV_K = 4
CONV_CH = 1024
D_FF = 4096
IN_COLS = 2568
DT_PAD = 128
SSD_COLS = CONV_CH + SSM_WIDTH + DT_PAD
N_CHIPS = 4
N_DEV = 8

ADAM_LR = 0.001
ADAM_B1 = 0.9
ADAM_B2 = 0.999
ADAM_EPS = 1e-08
ADAM_WD = 0.01
ADAM_STEP = 10

VMEM_LIMIT_BYTES = 56 * 1024 * 1024
FF_TILE = 512
DW_TOKENS_PER_STEP = 2048


def _cparams(n_axes):
    return pltpu.CompilerParams(dimension_semantics=("arbitrary",) * n_axes, vmem_limit_bytes=VMEM_LIMIT_BYTES)


def _dot(a, b):
    return jnp.dot(a.astype(BF16), b.astype(BF16), preferred_element_type=F32)


def _dot_nt(a, b):
    return lax.dot_general(a.astype(BF16), b.astype(BF16), (((1,), (1,)), ((), ())), preferred_element_type=F32)


def _dot_tn(a, b):
    return lax.dot_general(a.astype(BF16), b.astype(BF16), (((0,), (0,)), ((), ())), preferred_element_type=F32)


def _sigmoid(x):
    return 1.0 / (1.0 + jnp.exp(-x))


_GELU_C = 0.7978845608028654
_GELU_A = 0.044715


def _gelu(x):
    t = jnp.tanh(_GELU_C * (x + _GELU_A * (x * x * x)))
    return 0.5 * x * (1.0 + t), t


def _gelu_grad(x, t):
    return 0.5 * (1.0 + t) + 0.5 * x * (1.0 - t * t) * (_GELU_C * (1.0 + 3.0 * _GELU_A * x * x))


def _rms_fwd(x, w):
    r = lax.rsqrt(jnp.mean(x * x, axis=-1, keepdims=True) + EPS)
    return x * r * w, r


def _rms_bwd(x, r, w, dy):
    g = dy * w
    dx = r * g - x * (r * r * r) * jnp.mean(g * x, axis=-1, keepdims=True)
    dw = jnp.sum(dy * x * r, axis=0, keepdims=True)
    return dx, dw


class _Carried:
    def __init__(self, ins, out_shapes, sems, start, finish):
        self.ins, self.out_shapes, self.sems = list(ins), list(out_shapes), list(sems)
        self.start, self.finish = start, finish


def _both(first, second):
    n_i, n_o, n_s = len(first.ins), len(first.out_shapes), len(first.sems)

    def split(ins, outs, sems):
        return (ins[:n_i], outs[:n_o], sems[:n_s]), (ins[n_i:], outs[n_o:], sems[n_s:])

    def start(ins, outs, *sems):
        (i1, o1, s1), (i2, o2, s2) = split(ins, outs, sems)
        first.start(i1, o1, *s1)
        second.start(i2, o2, *s2)

    def finish(ins, outs, *sems):
        (i1, o1, s1), (i2, o2, s2) = split(ins, outs, sems)
        first.finish(i1, o1, *s1)
        second.finish(i2, o2, *s2)

    return _Carried(first.ins + second.ins, first.out_shapes + second.out_shapes, first.sems + second.sems, start, finish)


def _split_carried(refs, n_in, n_out, n_scratch, carried):
    n_ci, n_co, n_cs = len(carried.ins), len(carried.out_shapes), len(carried.sems)
    ins, rest = refs[:n_in], refs[n_in:]
    c_ins, rest = rest[:n_ci], rest[n_ci:]
    outs, rest = rest[:n_out], rest[n_out:]
    c_outs, rest = rest[:n_co], rest[n_co:]
    scr, c_sems = rest[:n_scratch], rest[n_scratch:]
    assert len(c_sems) == n_cs
    return tuple(ins) + tuple(outs) + tuple(scr), c_ins, c_outs, c_sems


def _rows_call(name, body, tm, row_ins, const_ins, row_outs, acc_outs=(), scratch=(), carried=None):
    n_rows = row_ins[0].shape[0]
    assert n_rows % tm == 0
    n_steps = n_rows // tm
    n_in = len(row_ins) + len(const_ins)
    n_ro = len(row_outs)
    n_acc = len(acc_outs)

    def kern(*refs):
        accs = refs[n_in + n_ro:n_in + n_ro + n_acc]

        @pl.when(pl.program_id(0) == 0)
        def _():
            for a in accs:
                a[...] = jnp.zeros_like(a)

        body(*refs)

    def whole(shape):
        nd = len(shape)
        return pl.BlockSpec(tuple(shape), lambda i: (0,) * nd)

    in_specs = [pl.BlockSpec((tm, a.shape[1]), lambda i: (i, 0)) for a in row_ins]
    in_specs += [whole(a.shape) for a in const_ins]
    out_specs = [pl.BlockSpec((tm, s.shape[1]), lambda i: (i, 0)) for s in row_outs]
    out_specs += [whole(s.shape) for s in acc_outs]
    return _call_carrying(
        kern, carried, name=name, grid=(n_steps,), in_specs=in_specs, out_specs=out_specs,
        out_shape=tuple(row_outs) + tuple(acc_outs), scratch_shapes=list(scratch), operands=list(row_ins) + list(const_ins))


def _call_carrying(body, carried, *, name, grid, in_specs, out_specs, out_shape, scratch_shapes, operands):
    n_in, n_out, n_scratch = len(in_specs), len(out_specs), len(scratch_shapes)
    kern = body
    if carried is not None:
        def kern(*refs):
            plain, c_ins, c_outs, c_sems = _split_carried(refs, n_in, n_out, n_scratch, carried)
            first, last = True, True
            for d, size in enumerate(grid):
                first = jnp.logical_and(first, pl.program_id(d) == 0)
                last = jnp.logical_and(last, pl.program_id(d) == size - 1)

            @pl.when(first)
            def _():
                carried.start(c_ins, c_outs, *c_sems)

            body(*plain)

            @pl.when(last)
            def _():
                carried.finish(c_ins, c_outs, *c_sems)

        in_specs = list(in_specs) + [_HBM] * len(carried.ins)
        out_specs = list(out_specs) + [_HBM] * len(carried.out_shapes)
        out_shape = tuple(out_shape) + tuple(carried.out_shapes)
        operands = list(operands) + carried.ins
        scratch_shapes = list(scratch_shapes) + carried.sems
    return pl.pallas_call(
        kern, name=name, grid=grid, in_specs=in_specs, out_specs=out_specs, out_shape=out_shape,
        scratch_shapes=scratch_shapes, compiler_params=_cparams(len(grid)),
    )(*operands)


def _sds(shape, dtype):
    return jax.ShapeDtypeStruct(tuple(shape), dtype)


def _matmul_tn(name, a, b, tm, tn, tk, stacked=False, carried=None, resident=None):
    k_dim, m_dim = a.shape
    n_dim = b.shape[1]
    assert m_dim % tm == 0 and n_dim % tn == 0 and k_dim % tk == 0
    nk = k_dim // tk
    a_spec = pl.BlockSpec((tk, tm), lambda i, j, k: (k, i))
    b_spec = pl.BlockSpec((tk, tn), lambda i, j, k: (k, j))
    if resident == "a":
        assert tm == m_dim
        a_spec = pl.BlockSpec((k_dim, m_dim), lambda i, j, k: (0, 0))
    elif resident == "b":
        assert tn == n_dim
        b_spec = pl.BlockSpec((k_dim, n_dim), lambda i, j, k: (0, 0))

    def kern(a_ref, b_ref, o_ref, acc_ref):
        k = pl.program_id(2)
        rows = pl.ds(pl.multiple_of(k * tk, tk), tk)
        av = a_ref[rows, :] if resident == "a" else a_ref[...]
        bv = b_ref[rows, :] if resident == "b" else b_ref[...]
        prod = _dot_tn(av, bv)

        @pl.when(k == 0)
        def _():
            acc_ref[...] = prod

        @pl.when(k > 0)
        def _():
            acc_ref[...] += prod

        @pl.when(k == nk - 1)
        def _():
            o_ref[...] = acc_ref[...].astype(o_ref.dtype)

    if stacked:
        assert tm == m_dim
        out_shape = _sds((n_dim // tn, m_dim, tn), BF16)
        out_spec = pl.BlockSpec((None, tm, tn), lambda i, j, k: (j, i, 0))
    else:
        out_shape = _sds((m_dim, n_dim), BF16)
        out_spec = pl.BlockSpec((tm, tn), lambda i, j, k: (i, j))
    outs = _call_carrying(
        kern, carried, name=name, grid=(m_dim // tm, n_dim // tn, nk),
        in_specs=[a_spec, b_spec],
        out_specs=[out_spec], out_shape=(out_shape,), scratch_shapes=[pltpu.VMEM((tm, tn), F32)], operands=[a, b])
    return outs[0] if carried is None else outs


def _inproj_fwd(x, nw, w_uv, w_xbc, w_z, w_dt, tm=256, carried=None):
    n_tok = x.shape[0]

    def body(x_ref, nw_ref, wuv_ref, wxbc_ref, wz_ref, wdt_ref, puv_ref, pxbc_ref, pz_ref, pdt_ref):
        h, _ = _rms_fwd(x_ref[...], nw_ref[...])
        h = h.astype(BF16)
        puv_ref[...] = jnp.dot(h, wuv_ref[...], preferred_element_type=F32)
        pxbc_ref[...] = jnp.dot(h, wxbc_ref[...], preferred_element_type=F32)
        pz_ref[...] = jnp.dot(h, wz_ref[...], preferred_element_type=F32)
        pdt_ref[...] = jnp.dot(h, wdt_ref[...], preferred_element_type=F32)

    return _rows_call(
        "inproj_fwd", body, tm, [x], [nw, w_uv, w_xbc, w_z, w_dt],
        [_sds((n_tok, 2 * GM_WIDTH), F32), _sds((n_tok, CONV_CH), F32), _sds((n_tok, SSM_WIDTH), F32),
         _sds((n_tok, DT_PAD), F32)], carried=carried)


def _head_lane_mask(width, head):
    lane = lax.broadcasted_iota(jnp.int32, (1, width), 1)
    return (lane // HEAD_DIM) == head


def _split_terms(x, terms):
    parts = []
    for _ in range(terms):
        p = x.astype(BF16)
        parts.append(p)
        x = x - p.astype(F32)
    return parts


def _seg_dots(vals, ind, terms=2):
    m = vals[0].shape[0]
    parts = []
    for v in vals:
        parts += _split_terms(v, terms)
    red = jnp.dot(jnp.concatenate(parts, axis=0), ind, preferred_element_type=F32)
    outs = []
    for i in range(len(vals)):
        acc = red[i * terms * m:(i * terms + 1) * m]
        for t in range(1, terms):
            acc = acc + red[(i * terms + t) * m:(i * terms + t + 1) * m]
        outs.append(acc)
    return outs


def _tri_dot(mask, x, terms=3):
    n = x.shape[1]
    red = jnp.dot(mask.astype(BF16), jnp.concatenate(_split_terms(x, terms), axis=1), preferred_element_type=F32)
    acc = red[:, :n]
    for t in range(1, terms):
        acc = acc + red[:, t * n:(t + 1) * n]
    return acc


def _gmlp_common(puv, lnw, lnb, e_bf, et_bf):
    u = puv[:, :GM_WIDTH]
    v = puv[:, GM_WIDTH:]
    gu, tu = _gelu(u)
    gv, tv = _gelu(v)
    (s1,) = _seg_dots([gv], et_bf)
    (mu,) = _seg_dots([s1 * (1.0 / HEAD_DIM)], e_bf)
    xc = gv - mu
    (s2,) = _seg_dots([xc * xc], et_bf)
    (rstd,) = _seg_dots([lax.rsqrt(s2 * (1.0 / HEAD_DIM) + EPS)], e_bf)
    xhat = xc * rstd
    vn = xhat * lnw + lnb
    return u, v, gu, tu, tv, rstd, xhat, vn


def _tril_mask():
    r = lax.broadcasted_iota(jnp.int32, (CHUNK, CHUNK), 0)
    c = lax.broadcasted_iota(jnp.int32, (CHUNK, CHUNK), 1)
    return r >= c


def _head_blocks(v):
    return jnp.concatenate([jnp.where(_head_lane_mask(GM_WIDTH, h), v, jnp.zeros_like(v)) for h in range(N_HEADS)], axis=0)


def _causal_w_cat(w_cat):
    t = lax.broadcasted_iota(jnp.int32, (CHUNK, N_HEADS * CHUNK), 0)
    s = lax.broadcasted_iota(jnp.int32, (CHUNK, N_HEADS * CHUNK), 1) % CHUNK
    return jnp.where(t >= s, w_cat, 0.0).astype(BF16)


def _gmlp_chunk_fwd(puv, lnw, lnb, e_bf, et_bf, wm, bmap):
    _, _, gu, _, _, _, _, vn = _gmlp_common(puv, lnw, lnb, e_bf, et_bf)
    mixed = jnp.dot(wm, _head_blocks(vn.astype(BF16)), preferred_element_type=F32) + bmap
    return (gu * mixed).astype(BF16)


SUBLANES = 8


def _shift_down(x, tail, s):
    main = pltpu.roll(x, s, 0)
    row = lax.broadcasted_iota(jnp.int32, (SUBLANES, 1), 0)
    head = jnp.where(row < s, pltpu.roll(tail, s, 0), main[:SUBLANES])
    return jnp.concatenate([head, main[SUBLANES:]], axis=0)


def _shift_up(x, head_next, s):
    n = x.shape[0]
    main = pltpu.roll(x, n - s, 0)
    row = lax.broadcasted_iota(jnp.int32, (SUBLANES, 1), 0)
    last = jnp.where(row >= SUBLANES - s, pltpu.roll(head_next, SUBLANES - s, 0), main[n - SUBLANES:])
    return jnp.concatenate([main[:n - SUBLANES], last], axis=0)


def _ssd_pre(xr, tail, cw_ref, cb, pdt, dtb, alog, emap):
    rowi = lax.broadcasted_iota(jnp.int32, (CHUNK, 1), 0)
    shifted = [_shift_down(xr, tail, 3), _shift_down(xr, tail, 2), _shift_down(xr, tail, 1), xr]
    xc = cb
    for k in range(CONV_K):
        xc = xc + cw_ref[k] * shifted[k]
    sg = _sigmoid(xc)
    xa = xc * sg
    pre = pdt + dtb
    dt = jnp.maximum(pre, 0.0) + jnp.log(1.0 + jnp.exp(-jnp.abs(pre)))
    a_neg = -jnp.exp(alog)
    a_cs = _tri_dot(_tril_mask(), dt * a_neg)
    acs_map, dt_map = _seg_dots([a_cs, dt], emap, terms=3)
    return dict(shifted=shifted, xc=xc, sg=sg, xa=xa, pre=pre, dt=dt, a_neg=a_neg, a_cs=a_cs,
                acs_map=acs_map, dt_map=dt_map, rowi=rowi)


def _ssd_maps(p):
    last = p["rowi"] == CHUNK - 1
    aq_map = jnp.sum(jnp.where(last, p["acs_map"], 0.0), axis=0, keepdims=True)
    e_exp = jnp.exp(p["acs_map"])
    dte = jnp.exp(aq_map - p["acs_map"])
    cd = jnp.exp(aq_map)
    return last, e_exp, dte, cd


def _head_decay(a_cs, a_cs_t, head, tri):
    lane = lax.broadcasted_iota(jnp.int32, (1, DT_PAD), 1)
    sub = lax.broadcasted_iota(jnp.int32, (DT_PAD, 1), 0)
    col = jnp.sum(jnp.where(lane == head, a_cs, 0.0), axis=1, keepdims=True)
    row = jnp.sum(jnp.where(sub == head, a_cs_t, 0.0), axis=0, keepdims=True)
    return jnp.exp(jnp.where(tri, col - row, -1e30))


def _gate_fwd(y, z, nw):
    sz = _sigmoid(z)
    zg = z * sz
    yg = y * zg
    outs, rs = [], []
    for g in range(SSM_GROUPS):
        gs = slice(g * GROUP_W, (g + 1) * GROUP_W)
        o, r = _rms_fwd(yg[:, gs], nw[:, gs])
        outs.append(o)
        rs.append(r)
    return sz, zg, yg, outs, rs


def _ssd_const_specs():
    def whole(shape):
        nd = len(shape)
        return pl.BlockSpec(tuple(shape), lambda c: (0,) * nd)
    return [whole((CONV_K, 1, CONV_CH)), whole((1, CONV_CH)), whole((1, DT_PAD)), whole((1, DT_PAD)),
            whole((1, SSM_WIDTH)), whole((1, SSM_WIDTH)), whole((DT_PAD, SSM_WIDTH)), whole((SSM_WIDTH, DT_PAD))]


def _mixer_fwd(p_uv, p_xbc, p_z, p_dt, x, lnw, lnb, w_cat, bmap, w_out, nw_post, nw_pre2, conv_w, conv_b, dt_bias, a_log,
               dskip_map, norm_w, e_bf, et_bf, n_seq, carried=None):
    n_tok = p_xbc.shape[0]
    nc = n_tok // n_seq // CHUNK

    def body(puv3, xr3, z3, pdt3, x3, lnw_ref, lnb_ref, wcat_ref, bmap_ref, wo_ref, nwa_ref, nwb_ref,
             cw_ref, cb_ref, dtb_ref, alog_ref, dsk_ref, nw_ref, e_ref, et_ref,
             mix3, yssd3, sprev3, o3, x13, h23, wm_scr, prev3_scr, s3_scr):
        @pl.when(pl.program_id(0) == 0)
        def _():
            wm_scr[...] = _causal_w_cat(wcat_ref[...])
            prev3_scr[...] = jnp.zeros_like(prev3_scr)
            s3_scr[...] = jnp.zeros_like(s3_scr)

        for b in range(n_seq):
            one_sequence(puv3.at[b], xr3.at[b], z3.at[b], pdt3.at[b], lnw_ref, lnb_ref, bmap_ref,
                         cw_ref, cb_ref, dtb_ref, alog_ref, dsk_ref, nw_ref, e_ref, et_ref,
                         mix3.at[b], yssd3.at[b], sprev3.at[b], wm_scr, prev3_scr.at[b], s3_scr.at[b])
            o = jnp.dot(mix3[b], wo_ref[...], preferred_element_type=F32)
            on, _ = _rms_fwd(o, nwa_ref[...])
            x1 = x3[b] + on
            h2, _ = _rms_fwd(x1, nwb_ref[...])
            o3[b] = o
            x13[b] = x1
            h23[b] = h2.astype(BF16)

    def one_sequence(puv_ref, xr_ref, z_ref, pdt_ref, lnw_ref, lnb_ref, bmap_ref,
                     cw_ref, cb_ref, dtb_ref, alog_ref, dsk_ref, nw_ref, e_ref, et_ref,
                     mix_ref, yssd_ref, sprev_ref, wm_scr, prev_scr, s_scr):
        mix_ref[:, :GM_WIDTH] = _gmlp_chunk_fwd(puv_ref[...], lnw_ref[...], lnb_ref[...], e_ref[...], et_ref[...], wm_scr[...],
                                      bmap_ref[...])
        xr = xr_ref[...]
        p = _ssd_pre(xr, prev_scr[...], cw_ref, cb_ref[...], pdt_ref[...], dtb_ref[...], alog_ref[...], e_ref[...])
        _, e_exp, dte, cd = _ssd_maps(p)
        xs = p["xa"][:, :SSM_WIDTH]
        xd = xs * p["dt_map"]
        a_cs_t = p["a_cs"].T
        tri = _tril_mask()
        s_old = s_scr[...]
        sprev_ref[...] = s_old
        for g in range(SSM_GROUPS):
            gs = slice(g * GROUP_W, (g + 1) * GROUP_W)
            bm = p["xa"][:, SSM_WIDTH + g * SSM_STATE: SSM_WIDTH + (g + 1) * SSM_STATE].astype(BF16)
            cm = p["xa"][:, SSM_WIDTH + (SSM_GROUPS + g) * SSM_STATE: SSM_WIDTH + (SSM_GROUPS + g + 1) * SSM_STATE].astype(BF16)
            cb_mat = _dot_nt(cm, bm)
            xdg = xd[:, gs].astype(BF16)
            y_g = _dot(cm, s_old[:, gs]) * e_exp[:, gs] + dsk_ref[:, gs] * xs[:, gs]
            for r in range(SSM_GROUPS * 2):
                dm = _head_decay(p["a_cs"], a_cs_t, g * 4 + r, tri)
                full = jnp.dot((cb_mat * dm).astype(BF16), xdg, preferred_element_type=F32)
                y_g = y_g + jnp.where(_head_lane_mask(GROUP_W, r), full, 0.0)
            yssd_ref[:, gs] = y_g
            s_scr[:, gs] = cd[:, gs] * s_old[:, gs] + _dot_tn(bm, xd[:, gs] * dte[:, gs])
        _, _, _, outs, _ = _gate_fwd(yssd_ref[...], z_ref[...], nw_ref[...])
        for g in range(SSM_GROUPS):
            mix_ref[:, GM_WIDTH + g * GROUP_W:GM_WIDTH + (g + 1) * GROUP_W] = outs[g].astype(BF16)
        prev_scr[...] = xr[CHUNK - SUBLANES:, :]

    seq_len = n_tok // n_seq

    def rows(width):
        return pl.BlockSpec((n_seq, CHUNK, width), lambda c: (0, c, 0))

    def whole(shape):
        nd = len(shape)
        return pl.BlockSpec(tuple(shape), lambda c: (0,) * nd)

    def by_seq(a):
        return a.reshape(n_seq, seq_len, a.shape[-1])

    outs = _call_carrying(
        body, carried, name="mixer_fwd", grid=(nc,),
        in_specs=[rows(2 * GM_WIDTH), rows(CONV_CH), rows(SSM_WIDTH), rows(DT_PAD), rows(D_MODEL), whole(lnw.shape),
                  whole(lnb.shape), whole(w_cat.shape), whole(bmap.shape), whole(w_out.shape), whole(nw_post.shape),
                  whole(nw_pre2.shape)] + _ssd_const_specs(),
        out_specs=[rows(D_MODEL), rows(SSM_WIDTH), rows(SSM_WIDTH), rows(D_MODEL), rows(D_MODEL), rows(D_MODEL)],
        out_shape=(_sds((n_seq, seq_len, D_MODEL), BF16),
                   _sds((n_seq, seq_len, SSM_WIDTH), F32), _sds((n_seq, seq_len, SSM_WIDTH), F32),
                   _sds((n_seq, seq_len, D_MODEL), F32), _sds((n_seq, seq_len, D_MODEL), F32),
                   _sds((n_seq, seq_len, D_MODEL), BF16)),
        scratch_shapes=[pltpu.VMEM((CHUNK, N_HEADS * CHUNK), BF16), pltpu.VMEM((n_seq, SUBLANES, CONV_CH), F32),
                        pltpu.VMEM((n_seq, SSM_STATE, SSM_WIDTH), F32)],
        operands=[by_seq(p_uv), by_seq(p_xbc), by_seq(p_z), by_seq(p_dt), by_seq(x), lnw, lnb, w_cat, bmap, w_out, nw_post,
                  nw_pre2, conv_w, conv_b, dt_bias, a_log, dskip_map, norm_w, e_bf, et_bf])
    return tuple(o.reshape(n_tok, o.shape[-1]) for o in outs[:6]) + tuple(outs[6:])


def _up_cols(wup_ref, j):
    per = (D_FF // N_CHIPS) // FF_TILE
    return wup_ref[j // per, :, (j % per) * FF_TILE:(j % per + 1) * FF_TILE]


def _down_rows(wda_ref, wdb_ref, j):
    assert 2 * FF_TILE == D_FF // N_CHIPS
    return (wda_ref if j % 2 == 0 else wdb_ref)[j // 2]


def _skewed_rows_call(name, main, tail, tm, lead_ins, lag_ins, const_ins, lead_outs, lag_outs, acc_outs, carry,
                      streamed, tile_copies, n_copies):
    n_rows = lead_ins[0].shape[0]
    assert n_rows % tm == 0
    n = n_rows // tm
    counts = [len(lead_ins), len(lag_ins), len(const_ins), len(streamed), len(lead_outs), len(lag_outs), len(acc_outs),
              1, len(streamed)]

    def kern(*refs):
        groups, pos = [], 0
        for cnt in counts:
            groups.append(refs[pos:pos + cnt])
            pos += cnt
        lead_i, lag_i, consts, w_hbm, lead_o, lag_o, accs, (carry_scr,), w_vmem = groups
        sems = refs[pos]
        i = pl.program_id(0)
        pieces, k = [], 0
        for piece in tile_copies(w_hbm, w_vmem):
            pieces.append([pltpu.make_async_copy(src, dst, sems.at[k + q]) for q, (src, dst) in enumerate(piece)])
            k += len(piece)

        def ready(j):
            for cp in pieces[j]:
                cp.wait()

        @pl.when(i == 0)
        def _():
            for piece in pieces:
                for cp in piece:
                    cp.start()
            for a in accs:
                a[...] = jnp.zeros_like(a)
            carry_scr[...] = main(lead_i, consts, lead_o, w_vmem, ready)

        @pl.when(jnp.logical_and(i > 0, i < n))
        def _():
            previous = carry_scr[...]
            carry_scr[...] = main(lead_i, consts, lead_o, w_vmem, lambda j: None)
            tail(previous, lag_i, consts, lag_o, accs)

        @pl.when(i == n)
        def _():
            tail(carry_scr[...], lag_i, consts, lag_o, accs)

    def lead(width):
        return pl.BlockSpec((tm, width), lambda i: (jnp.minimum(i, n - 1), 0))

    def lag(width):
        return pl.BlockSpec((tm, width), lambda i: (jnp.maximum(i - 1, 0), 0))

    def whole(shape):
        nd = len(shape)
        return pl.BlockSpec(tuple(shape), lambda i: (0,) * nd)

    return pl.pallas_call(
        kern, name=name, grid=(n + 1,),
        in_specs=([lead(a.shape[1]) for a in lead_ins] + [lag(a.shape[1]) for a in lag_ins]
                  + [whole(a.shape) for a in const_ins] + [_HBM] * len(streamed)),
        out_specs=[lead(s.shape[1]) for s in lead_outs] + [lag(s.shape[1]) for s in lag_outs] + [whole(s.shape) for s in acc_outs],
        out_shape=tuple(lead_outs) + tuple(lag_outs) + tuple(acc_outs),
        scratch_shapes=([pltpu.VMEM(carry, F32)] + [pltpu.VMEM(a.shape, a.dtype) for a in streamed]
                        + [pltpu.SemaphoreType.DMA((n_copies,))]),
        compiler_params=_cparams(1),
    )(*lead_ins, *lag_ins, *const_ins, *streamed)


def _mlp_weight_pieces(order):
    per = (D_FF // N_CHIPS) // FF_TILE

    def tile_copies(hbm, vmem):
        pieces = []
        for j in range(D_FF // FF_TILE):
            cols = (j // per, slice(None), pl.ds((j % per) * FF_TILE, FF_TILE))
            up = (hbm[0].at[cols], vmem[0].at[cols])
            down = (hbm[1 + j % 2].at[j // 2], vmem[1 + j % 2].at[j // 2])
            pieces.append([up, down] if order == "up_down" else [down, up])
        return pieces

    return tile_copies


def _mlp_fwd(h2, x1, tgt, w_up, w_down_a, w_down_b, nw, tm=512):
    n_tok = x1.shape[0]

    def main(lead_i, consts, lead_o, weights, ready):
        (h2_ref,), (f_ref,), (wup_ref, wda_ref, wdb_ref) = lead_i, lead_o, weights
        h2v = h2_ref[...]
        acc = jnp.zeros((tm, D_MODEL), F32)
        for j in range(D_FF // FF_TILE):
            cs = slice(j * FF_TILE, (j + 1) * FF_TILE)
            ready(j)
            u = jnp.dot(h2v, _up_cols(wup_ref, j), preferred_element_type=F32)
            f = jnp.square(jnp.maximum(u, 0.0)).astype(BF16)
            f_ref[:, cs] = f
            acc = acc + jnp.dot(f, _down_rows(wda_ref, wdb_ref, j), preferred_element_type=F32)
        return acc

    def tail(acc, lag_i, consts, lag_o, accs):
        (x1_ref, tgt_ref), (nw_ref,), (dd_ref, dy_ref), (loss_ref, dnw_ref) = lag_i, consts, lag_o, accs
        dn, r = _rms_fwd(acc, nw_ref[...])
        e = x1_ref[...] + dn - tgt_ref[...]
        loss_ref[...] += jnp.full(loss_ref.shape, (0.5 / D_MODEL) * jnp.sum(e * e), F32)
        dy = e * (1.0 / D_MODEL)
        dd, dnw = _rms_bwd(acc, r, nw_ref[...], dy)
        dy_ref[...] = dy
        dd_ref[...] = dd.astype(BF16)
        dnw_ref[...] += dnw

    return _skewed_rows_call(
        "mlp_fwd", main, tail, tm, [h2], [x1, tgt], [nw],
        [_sds((n_tok, D_FF), BF16)], [_sds((n_tok, D_MODEL), BF16), _sds((n_tok, D_MODEL), F32)],
        [_sds((8, 128), F32), _sds((1, D_MODEL), F32)], carry=(tm, D_MODEL),
        streamed=[w_up, w_down_a, w_down_b], tile_copies=_mlp_weight_pieces("up_down"), n_copies=2 * (D_FF // FF_TILE))


def _mlp_bwd(dd, f, x1, dy, w_down_a, w_down_b, w_up, nw, tm=256):
    n_tok = x1.shape[0]

    def main(lead_i, consts, lead_o, weights, ready):
        (dd_ref, f_ref), (dup_ref,), (wup_ref, wda_ref, wdb_ref) = lead_i, lead_o, weights
        ddv = dd_ref[...]
        acc = jnp.zeros((tm, D_MODEL), F32)
        for j in range(D_FF // FF_TILE):
            cs = slice(j * FF_TILE, (j + 1) * FF_TILE)
            ready(j)
            df = _dot_nt(ddv, _down_rows(wda_ref, wdb_ref, j))
            du = (df * (2.0 * jnp.sqrt(f_ref[:, cs].astype(F32)))).astype(BF16)
            dup_ref[:, cs] = du
            acc = acc + _dot_nt(du, _up_cols(wup_ref, j))
        return acc

    def tail(acc, lag_i, consts, lag_o, accs):
        (x1_ref, dy_ref), (nw_ref,), (dx1_ref,), (dnw_ref,) = lag_i, consts, lag_o, accs
        x1v = x1_ref[...]
        _, r = _rms_fwd(x1v, nw_ref[...])
        dx, dnw = _rms_bwd(x1v, r, nw_ref[...], acc)
        dx1_ref[...] = dy_ref[...] + dx
        dnw_ref[...] += dnw

    return _skewed_rows_call(
        "mlp_bwd", main, tail, tm, [dd, f], [x1, dy], [nw],
        [_sds((n_tok, D_FF), BF16)], [_sds((n_tok, D_MODEL), F32)], [_sds((1, D_MODEL), F32)], carry=(tm, D_MODEL),
        streamed=[w_up, w_down_a, w_down_b], tile_copies=_mlp_weight_pieces("down_up"), n_copies=2 * (D_FF // FF_TILE))


def _outproj_bwd(dx1, o, w_out, nw, tm=256, carried=None):
    n_tok = dx1.shape[0]

    def body(dx1_ref, o_ref, wo_ref, nw_ref, do_ref, dya_ref, dyb_ref, dnw_ref):
        ov = o_ref[...]
        _, r = _rms_fwd(ov, nw_ref[...])
        do, dnw = _rms_bwd(ov, r, nw_ref[...], dx1_ref[...])
        dob = do.astype(BF16)
        do_ref[...] = dob
        dya_ref[...] = _dot_nt(dob, wo_ref[:GM_WIDTH, :])
        dyb_ref[...] = _dot_nt(dob, wo_ref[GM_WIDTH:, :])
        dnw_ref[...] += dnw

    return _rows_call("outproj_bwd", body, tm, [dx1, o], [w_out, nw],
                      [_sds((n_tok, D_MODEL), BF16), _sds((n_tok, GM_WIDTH), F32), _sds((n_tok, SSM_WIDTH), F32)],
                      [_sds((1, D_MODEL), F32)], carried=carried)


def _gmlp_bwd(p_uv, dya, lnw, lnb, e_bf, et_bf, w_cat, w_stack, bmap, carried=None):
    n_tok = p_uv.shape[0]
    chunks_per_step = 2

    def body(puv_ref, dya_ref, lnw_ref, lnb_ref, e_ref, et_ref, wcat_ref, wstack_ref, bmap_ref,
             dpuv_ref, dws_ref, dbs_ref, dlnw_ref, dlnb_ref, wm_scr, wsm_scr):
        t_stk = lax.broadcasted_iota(jnp.int32, (N_HEADS * CHUNK, CHUNK), 0) % CHUNK
        s_stk = lax.broadcasted_iota(jnp.int32, (N_HEADS * CHUNK, CHUNK), 1)

        @pl.when(pl.program_id(0) == 0)
        def _():
            wm_scr[...] = _causal_w_cat(wcat_ref[...])
            wsm_scr[...] = jnp.where(t_stk >= s_stk, wstack_ref[...], 0.0).astype(BF16)

        lnw_v = lnw_ref[...]
        e_v, et_v = e_ref[...], et_ref[...]

        def one_chunk(rows):
            u, v, gu, tu, tv, rstd, xhat, vn = _gmlp_common(puv_ref[rows, :], lnw_v, lnb_ref[...], e_v, et_v)
            vnb = vn.astype(BF16)
            mixed = jnp.dot(wm_scr[...], _head_blocks(vnb), preferred_element_type=F32) + bmap_ref[...]
            dy = dya_ref[rows, :]
            du = dy * mixed * _gelu_grad(u, tu)
            dmixed = dy * gu
            (dbs,) = _seg_dots([dmixed], et_v)
            dblocks = _head_blocks(dmixed.astype(BF16))
            dvn = lax.dot_general(wsm_scr[...], dblocks, (((0,), (0,)), ((), ())), preferred_element_type=F32)
            dws = lax.dot_general(dblocks, vnb, (((1,), (1,)), ((), ())), preferred_element_type=F32)
            dxh = dvn * lnw_v
            m1, m2 = _seg_dots([dxh, dxh * xhat], et_v)
            m1, m2 = _seg_dots([m1 * (1.0 / HEAD_DIM), m2 * (1.0 / HEAD_DIM)], e_v)
            dgv = rstd * (dxh - m1 - xhat * m2)
            dv = dgv * _gelu_grad(v, tv)
            dpuv_ref[rows, :GM_WIDTH] = du.astype(BF16)
            dpuv_ref[rows, GM_WIDTH:] = dv.astype(BF16)
            return dbs, dws, jnp.sum(dvn * xhat, axis=0, keepdims=True), jnp.sum(dvn, axis=0, keepdims=True)

        parts = [one_chunk(slice(k * CHUNK, (k + 1) * CHUNK)) for k in range(chunks_per_step)]
        dbs, dws, dlnw, dlnb = [functools.reduce(lambda a, b: a + b, vals) for vals in zip(*parts)]
        dbs_ref[...] += dbs
        dws_ref[...] += jnp.where(t_stk >= s_stk, dws, 0.0)
        dlnw_ref[...] += dlnw
        dlnb_ref[...] += dlnb

    return _rows_call(
        "gmlp_bwd", body, chunks_per_step * CHUNK, [p_uv, dya], [lnw, lnb, e_bf, et_bf, w_cat, w_stack, bmap],
        [_sds((n_tok, 2 * GM_WIDTH), BF16)],
        [_sds((N_HEADS * CHUNK, CHUNK), F32), _sds((CHUNK, DT_PAD), F32), _sds((1, GM_WIDTH), F32),
         _sds((1, GM_WIDTH), F32)],
        scratch=[pltpu.VMEM((CHUNK, N_HEADS * CHUNK), BF16), pltpu.VMEM((N_HEADS * CHUNK, CHUNK), BF16)],
        carried=carried)


def _ssd_bwd(p_xbc, p_z, p_dt, yssd, sprev, dyb, conv_w, conv_b, dt_bias, a_log, dskip_map, norm_w, e_bf, et_bf, n_seq,
             carried=None):
    n_tok = p_xbc.shape[0]
    nc = n_tok // n_seq // CHUNK

    def body(xr3, xprev3, z3, pdt3, yssd3, sprev3, dyb3,
             cw_ref, cb_ref, dtb_ref, alog_ref, dsk_ref, nw_ref, e_ref, et_ref,
             dps3, dcw_ref, dcb_ref, ddtb_ref, dalog_ref, ddsk_ref, dnw_ref,
             ds3_scr, nxt3_scr, dxa3_scr):
        @pl.when(pl.program_id(0) == 0)
        def _():
            for a in (dcw_ref, dcb_ref, ddtb_ref, dalog_ref, ddsk_ref, dnw_ref, ds3_scr, nxt3_scr):
                a[...] = jnp.zeros_like(a)

        for b in range(n_seq):
            one_sequence(xr3.at[b], xprev3.at[b], z3.at[b], pdt3.at[b], yssd3.at[b], sprev3.at[b], dyb3.at[b],
                         cw_ref, cb_ref, dtb_ref, alog_ref, dsk_ref, nw_ref, e_ref, et_ref,
                         dps3.at[b], dcw_ref, dcb_ref, ddtb_ref, dalog_ref, ddsk_ref, dnw_ref,
                         ds3_scr.at[b], nxt3_scr.at[b], dxa3_scr.at[b])

    def one_sequence(xr_ref, xprev_ref, z_ref, pdt_ref, yssd_ref, sprev_ref, dyb_ref,
                     cw_ref, cb_ref, dtb_ref, alog_ref, dsk_ref, nw_ref, e_ref, et_ref,
                     dps_ref, dcw_ref, dcb_ref, ddtb_ref, dalog_ref, ddsk_ref, dnw_ref,
                     ds_scr, nxt_scr, dxa_scr):
        chunk = nc - 1 - pl.program_id(0)
        xr = xr_ref[...]
        prev = jnp.where(chunk == 0, 0.0, xprev_ref[...])
        et_v = et_ref[...]
        p = _ssd_pre(xr, prev, cw_ref, cb_ref[...], pdt_ref[...], dtb_ref[...], alog_ref[...], e_ref[...])
        last, e_exp, dte, cd = _ssd_maps(p)
        rowi = p["rowi"]
        xs = p["xa"][:, :SSM_WIDTH]
        xd = xs * p["dt_map"]
        a_cs_t = p["a_cs"].T
        tri = _tril_mask()
        dsk = dsk_ref[...]
        nw_v = nw_ref[...]

        yv = yssd_ref[...]
        zv = z_ref[...]
        sz, zg, yg, _, rs = _gate_fwd(yv, zv, nw_v)
        dout = dyb_ref[...]
        for g in range(SSM_GROUPS):
            gs = slice(g * GROUP_W, (g + 1) * GROUP_W)
            dyg_g, dnw_g = _rms_bwd(yg[:, gs], rs[g], nw_v[:, gs], dout[:, gs])
            dnw_ref[:, gs] += dnw_g
            dxa_scr[:, gs] = dyg_g
        dyg = dxa_scr[:, :SSM_WIDTH]
        d_y = dyg * zg
        dps_ref[:, CONV_CH:CONV_CH + SSM_WIDTH] = (dyg * yv * (sz + zv * sz * (1.0 - sz))).astype(BF16)

        s_prev = sprev_ref[...]
        ds_next = ds_scr[...]
        lane_dt = lax.broadcasted_iota(jnp.int32, (1, DT_PAD), 1)
        da_cols = jnp.zeros((CHUNK, DT_PAD), F32)
        for g in range(SSM_GROUPS):
            gs = slice(g * GROUP_W, (g + 1) * GROUP_W)
            b_off = SSM_WIDTH + g * SSM_STATE
            c_off = SSM_WIDTH + (SSM_GROUPS + g) * SSM_STATE
            bm = p["xa"][:, b_off:b_off + SSM_STATE].astype(BF16)
            cm = p["xa"][:, c_off:c_off + SSM_STATE].astype(BF16)
            cb_mat = _dot_nt(cm, bm)
            d_yg = d_y[:, gs]
            d_ygb = d_yg.astype(BF16)
            xdg = xd[:, gs]
            xdgb = xdg.astype(BF16)
            ds_g = ds_next[:, gs]
            sp_g = s_prev[:, gs]
            bds = _dot(bm, ds_g)
            dcs = d_yg * e_exp[:, gs]
            d_c = _dot_nt(dcs, sp_g)
            ds_scr[:, gs] = cd[:, gs] * ds_g + _dot_tn(cm, dcs)
            d_b = _dot_nt(xdg * dte[:, gs], ds_g)
            dxd_g = bds * dte[:, gs]
            sum_dcb = jnp.zeros((CHUNK, CHUNK), F32)
            for r in range(SSM_GROUPS * 2):
                head = g * 4 + r
                mask = _head_lane_mask(GROUP_W, r)
                dm = _head_decay(p["a_cs"], a_cs_t, head, tri)
                m_mat = cb_mat * dm
                g_mat = _dot_nt(jnp.where(mask, d_yg, 0.0), xdgb)
                w_mat = g_mat * m_mat
                sum_dcb = sum_dcb + g_mat * dm
                dxd_g = dxd_g + jnp.where(mask, _dot_tn(m_mat, d_ygb), 0.0)
                da_h = jnp.sum(w_mat - w_mat.T, axis=1, keepdims=True)
                da_cols = da_cols + jnp.where(lane_dt == head, da_h, 0.0)
            d_c = d_c + _dot(sum_dcb, bm)
            d_b = d_b + _dot_tn(sum_dcb, cm)
            dxa_scr[:, b_off:b_off + SSM_STATE] = d_b
            dxa_scr[:, c_off:c_off + SSM_STATE] = d_c
            y_off_g = _dot(cm, sp_g) * e_exp[:, gs]
            t3 = bds * xdg * dte[:, gs]
            tail = jnp.sum(t3, axis=0, keepdims=True) + jnp.sum(ds_g * sp_g, axis=0, keepdims=True) * cd[:, gs]
            pre_g = d_yg * y_off_g - t3 + jnp.where(last, tail, 0.0)
            s_pre, ddt_g, s_dsk = _seg_dots([pre_g, dxd_g * xs[:, gs], d_yg * xs[:, gs]], et_v[gs, :])
            da_cols = da_cols + s_pre
            ddsk_ref[...] += jnp.sum(s_dsk, axis=0, keepdims=True)
            dxa_scr[:, gs] = dxd_g * p["dt_map"][:, gs] + dsk[:, gs] * d_yg
            if g == 0:
                ddt = ddt_g
            else:
                ddt = ddt + ddt_g
        r_i = lax.broadcasted_iota(jnp.int32, (CHUNK, CHUNK), 0)
        c_i = lax.broadcasted_iota(jnp.int32, (CHUNK, CHUNK), 1)
        ddta = _tri_dot(r_i <= c_i, da_cols, terms=2)
        ddt = ddt + ddta * p["a_neg"]
        dalog_ref[...] += jnp.sum(ddta * p["dt"], axis=0, keepdims=True) * p["a_neg"]
        draw = ddt * _sigmoid(p["pre"])
        ddtb_ref[...] += jnp.sum(draw, axis=0, keepdims=True)
        dps_ref[:, CONV_CH + SSM_WIDTH:] = draw.astype(BF16)

        xc = p["xc"]
        sg = p["sg"]
        dxc = dxa_scr[...] * (sg + xc * sg * (1.0 - sg))
        dcb_ref[...] += jnp.sum(dxc, axis=0, keepdims=True)
        for k in range(CONV_K):
            dcw_ref[k] += jnp.sum(dxc * p["shifted"][k], axis=0, keepdims=True)
        nxt = nxt_scr[...]
        dxr = cw_ref[3] * dxc
        for s in range(1, CONV_K):
            dxr = dxr + cw_ref[CONV_K - 1 - s] * _shift_up(dxc, nxt, s)
        dps_ref[:, :CONV_CH] = dxr.astype(BF16)
        nxt_scr[...] = dxc[:SUBLANES, :]

    seq_len = n_tok // n_seq

    def rows(width):
        return pl.BlockSpec((n_seq, CHUNK, width), lambda s: (0, nc - 1 - s, 0))

    tiles = CHUNK // SUBLANES
    prev_rows = pl.BlockSpec((n_seq, SUBLANES, CONV_CH), lambda s: (0, jnp.maximum((nc - 1 - s) * tiles - 1, 0), 0))

    def whole(shape):
        nd = len(shape)
        return pl.BlockSpec(tuple(shape), lambda s: (0,) * nd)

    def by_seq(a):
        return a.reshape(n_seq, seq_len, a.shape[-1])

    acc_shapes = [(CONV_K, 1, CONV_CH), (1, CONV_CH), (1, DT_PAD), (1, DT_PAD), (1, DT_PAD), (1, SSM_WIDTH)]
    xbc3 = by_seq(p_xbc)
    outs = _call_carrying(
        body, carried, name="ssd_bwd", grid=(nc,),
        in_specs=[rows(CONV_CH), prev_rows, rows(SSM_WIDTH), rows(DT_PAD), rows(SSM_WIDTH), rows(SSM_WIDTH),
                  rows(SSM_WIDTH)] + _ssd_const_specs(),
        out_specs=[rows(SSD_COLS)] + [whole(s) for s in acc_shapes],
        out_shape=tuple([_sds((n_seq, seq_len, SSD_COLS), BF16)] + [_sds(s, F32) for s in acc_shapes]),
        scratch_shapes=[pltpu.VMEM((n_seq, SSM_STATE, SSM_WIDTH), F32), pltpu.VMEM((n_seq, SUBLANES, CONV_CH), F32),
                        pltpu.VMEM((n_seq, CHUNK, CONV_CH), F32)],
        operands=[xbc3, xbc3, by_seq(p_z), by_seq(p_dt), by_seq(yssd), by_seq(sprev), by_seq(dyb), conv_w, conv_b, dt_bias,
                  a_log, dskip_map, norm_w, e_bf, et_bf])
    return (outs[0].reshape(n_tok, SSD_COLS),) + tuple(outs[1:])


def _inproj_bwd(dp_uv, dp_ssd, x, dx1, w_uv, w_xbc, w_z, w_dt, nw, tm=512, carried=None):
    n_tok = x.shape[0]

    def body(duv_ref, dssd_ref, x_ref, dx1_ref, wuv_ref, wxbc_ref, wz_ref, wdt_ref, nw_ref, gx_ref, h_ref, dnw_ref):
        dh = (_dot_nt(duv_ref[...], wuv_ref[...]) + _dot_nt(dssd_ref[:, :CONV_CH], wxbc_ref[...])
              + _dot_nt(dssd_ref[:, CONV_CH:CONV_CH + SSM_WIDTH], wz_ref[...])
              + _dot_nt(dssd_ref[:, CONV_CH + SSM_WIDTH:], wdt_ref[...]))
        xv = x_ref[...]
        h, r = _rms_fwd(xv, nw_ref[...])
        dx, dnw = _rms_bwd(xv, r, nw_ref[...], dh)
        gx_ref[...] = dx1_ref[...] + dx
        h_ref[...] = h.astype(BF16)
        dnw_ref[...] += dnw

    return _rows_call("inproj_bwd", body, tm, [dp_uv, dp_ssd, x, dx1], [w_uv, w_xbc, w_z, w_dt, nw],
                      [_sds((n_tok, D_MODEL), F32), _sds((n_tok, D_MODEL), BF16)], [_sds((1, D_MODEL), F32)],
                      carried=carried)


def _const_maps():
    lane = jnp.arange(SSM_WIDTH) // HEAD_DIM
    e_bf = (jnp.arange(DT_PAD)[:, None] == lane[None, :]).astype(BF16)
    return e_bf, e_bf.T


def _pad_lanes(v, width):
    return jnp.pad(v, ((0, 0), (0, width - v.shape[1])))


SHARD_COLS = IN_COLS // N_CHIPS
_UV_END = 2 * GM_WIDTH
_Z_END = _UV_END + SSM_WIDTH
_XBC_END = _Z_END + CONV_CH


def _cols_from_shards(w4, lo, hi):
    pieces = []
    for j in range(N_CHIPS):
        a, b = max(lo, j * SHARD_COLS), min(hi, (j + 1) * SHARD_COLS)
        if a < b:
            pieces.append(w4[j][:, a - j * SHARD_COLS:b - j * SHARD_COLS])
    return pieces[0] if len(pieces) == 1 else jnp.concatenate(pieces, axis=1)


def _shards_from_cols(blocks):
    shards = []
    for j in range(N_CHIPS):
        pieces = []
        for arr, lo, hi in blocks:
            a, b = max(lo, j * SHARD_COLS), min(hi, (j + 1) * SHARD_COLS)
            if a < b:
                pieces.append(arr[:, a - lo:b - lo])
        shards.append(pieces[0] if len(pieces) == 1 else jnp.concatenate(pieces, axis=1))
    return jnp.stack(shards)


def _forward_backward(x, tgt, w_in4, conv_w, small, out_shard, up_shard, down_shard, core, adam_args):
    n_seq, seq_len, _ = x.shape
    n_tok = n_seq * seq_len
    x2 = x.reshape(n_tok, D_MODEL)
    tgt2 = tgt.reshape(n_tok, D_MODEL)
    e_bf, et_bf = _const_maps()

    w_uv = _cols_from_shards(w_in4, 0, _UV_END)
    w_z = _cols_from_shards(w_in4, _UV_END, _Z_END)
    w_xbc = _cols_from_shards(w_in4, _Z_END, _XBC_END)
    w_dt = _pad_lanes(_cols_from_shards(w_in4, _XBC_END, IN_COLS), DT_PAD)

    nw_pre = small["norm_mix_pre"]
    lnw = small["gm_ln_w"].reshape(1, GM_WIDTH)
    lnb = small["gm_ln_b"].reshape(1, GM_WIDTH)
    w_stack = small["gm_w_s"].reshape(N_HEADS * CHUNK, CHUNK)
    w_cat = jnp.transpose(small["gm_w_s"], (1, 0, 2)).reshape(CHUNK, N_HEADS * CHUNK)
    bmap = jnp.repeat(small["gm_b_s"].T, HEAD_DIM, axis=1)
    cw3 = conv_w.reshape(CONV_K, 1, CONV_CH)
    conv_b = small["conv_b"]
    dt_bias = _pad_lanes(small["dt_bias"], DT_PAD)
    a_log = _pad_lanes(small["a_log"], DT_PAD)
    dskip_map = jnp.repeat(small["d_skip"], HEAD_DIM, axis=1)
    ssm_nw = small["ssm_norm_w"]

    half = down_shard.shape[0] // 2
    p_uv, p_xbc, p_z, p_dt, w_out4, w_down_a = _inproj_fwd(
        x2, nw_pre, w_uv, w_xbc, w_z, w_dt, carried=_allgather_exchange([out_shard, down_shard[:half]]))
    ssd_consts = (cw3, conv_b, dt_bias, a_log, dskip_map, ssm_nw, e_bf, et_bf)
    w_out_b = w_out4.reshape(D_MODEL, D_MODEL)
    mix, yssd, sprev, o, x1, h2, w_up4, w_down_b = _mixer_fwd(
        p_uv, p_xbc, p_z, p_dt, x2, lnw, lnb, w_cat, bmap, w_out_b, small["norm_mix_post"], small["norm_ffn_pre"],
        *ssd_consts, n_seq, carried=_allgather_exchange([up_shard, down_shard[half:]]))
    f, dd, dy, loss_acc, d_nffn_post = _mlp_fwd(h2, x1, tgt2, w_up4, w_down_a, w_down_b, small["norm_ffn_post"])

    dup, dx1, d_nffn_pre = _mlp_bwd(dd, f, x1, dy, w_down_a, w_down_b, w_up4, small["norm_ffn_pre"])
    tk = min(DW_TOKENS_PER_STEP, n_tok)
    g_up = _matmul_tn("dw_up", h2, dup, D_MODEL, D_MODEL, tk, stacked=True, resident="a")
    g_down = _matmul_tn("dw_down", f, dd, 1024, D_MODEL, tk, resident="b").reshape(N_CHIPS, D_FF // N_CHIPS, D_MODEL)
    do, dya, dyb, d_nmix_post, got_up, got_down = _outproj_bwd(
        dx1, o, w_out_b, small["norm_mix_post"], carried=_pair_exchange([g_up, g_down]))
    h_up = _pair_sum(core, g_up, got_up, 512)
    h_down = _pair_sum(core, g_down, got_down, 512)
    g_out = _matmul_tn("dw_out", mix, do, D_MODEL, D_MODEL, tk).reshape(N_CHIPS, D_MODEL // N_CHIPS, D_MODEL)
    dp_uv, d_ws, d_bs_t, d_lnw, d_lnb, slab_up, got_out = _gmlp_bwd(
        p_uv, dya, lnw, lnb, e_bf, et_bf, w_cat, w_stack, bmap,
        carried=_both(_chip_exchange([h_up]), _pair_exchange([g_out])))
    h_out = _pair_sum(core, g_out, got_out, 128)
    early = {
        "gm_ln_w": d_lnw.reshape(N_HEADS, HEAD_DIM), "gm_ln_b": d_lnb.reshape(N_HEADS, HEAD_DIM),
        "gm_w_s": d_ws.reshape(N_HEADS, CHUNK, CHUNK), "gm_b_s": d_bs_t[:, :N_HEADS].T,
        "norm_mix_post": d_nmix_post, "norm_ffn_pre": d_nffn_pre, "norm_ffn_post": d_nffn_post,
    }
    packed_early = _pack(early, tuple(early), tail=loss_acc[0, 0].reshape(1))
    (dp_ssd, d_cw, d_cb, d_dtb, d_alog, d_dsk, d_ssm_nw, slab_down, slab_out, all_early) = _ssd_bwd(
        p_xbc, p_z, p_dt, yssd, sprev, dyb, *ssd_consts, n_seq,
        carried=_both(_chip_exchange([h_down, h_out]), _device_gather_exchange(packed_early)))
    gx, h, d_nmix_pre = _inproj_bwd(dp_uv, dp_ssd, x2, dx1, w_uv, w_xbc, w_z, w_dt, nw_pre)
    late = {
        "norm_mix_pre": d_nmix_pre, "conv_w": d_cw.reshape(CONV_K, CONV_CH), "conv_b": d_cb,
        "dt_bias": d_dtb[:, :N_HEADS], "a_log": d_alog[:, :N_HEADS], "d_skip": d_dsk[:, :N_HEADS],
        "ssm_norm_w": d_ssm_nw,
    }
    g_uv, all_late = _matmul_tn("dw_in_uv", h, dp_uv, D_MODEL, 2 * GM_WIDTH, tk,
                                carried=_device_gather_exchange(_pack(late, tuple(late))))
    sum_early = _ordered_sum("small_sum_early", all_early)
    small_sum = _unpack(sum_early, {n: v.shape for n, v in early.items()}, tuple(early))
    small_sum.update(_unpack(_ordered_sum("small_sum_late", all_late), {n: v.shape for n, v in late.items()}, tuple(late)))
    loss = sum_early.reshape(-1)[sum(v.size for v in early.values())]
    red_up, red_down, red_out = _chip_sum(slab_up, 512), _chip_sum(slab_down, 512), _chip_sum(slab_out, 128)
    g_ssd, oth_up, oth_down, oth_out = _matmul_tn("dw_in_ssd", h, dp_ssd, D_MODEL, SSD_COLS, tk,
                                                  carried=_pair_swap([red_up, red_down, red_out]))
    g_xbc, g_z, g_dt = g_ssd[:, :CONV_CH], g_ssd[:, CONV_CH:CONV_CH + SSM_WIDTH], g_ssd[:, CONV_CH + SSM_WIDTH:]
    g_in = _shards_from_cols([(g_uv, 0, _UV_END), (g_z, _UV_END, _Z_END), (g_xbc, _Z_END, _XBC_END),
                              (g_dt, _XBC_END, IN_COLS)])
    red_in, oth_in = _reduce_scatter_last(g_in)
    res = _adamw_halves("adamw_mlp", [(adam_args["w_up"][0], red_up, oth_up) + adam_args["w_up"][1:],
                                      (adam_args["w_down"][0], red_down, oth_down) + adam_args["w_down"][1:]], 256)
    big_out = {"w_up": res[0:4], "w_down": res[4:8]}
    big_out["w_out"] = _adamw_halves("adamw_w_out", [(adam_args["w_out"][0], red_out, oth_out) + adam_args["w_out"][1:]], 128)
    big_out["w_in"] = _adamw_halves("adamw_w_in", [(adam_args["w_in"][0], red_in, oth_in) + adam_args["w_in"][1:]], 256)

    return loss, gx.reshape(x.shape), big_out, small_sum


_HBM = pl.BlockSpec(memory_space=pltpu.HBM)


D2D_CHUNKS = 8
ICI_CHUNKS = 1
ROW_ALIGN = 16


def _row_chunks(rows, n_chunks):
    size = min(max(rows // n_chunks, ROW_ALIGN), rows)
    assert rows % size == 0
    return [(start, size) for start in range(0, rows, size)]


def _position():
    x, y, c = lax.axis_index("x"), lax.axis_index("y"), lax.axis_index("c")
    chips = [(1 - x, y), (x, 1 - y), (1 - x, 1 - y)]
    return x, y, c, chips


def _allgather_exchange(arrs):
    n = len(arrs)

    def copies(ins, outs, send_sems, recv_sems, local_sems):
        x, y, c, chips = _position()
        me = 2 * x + y
        sibling = (x, y, 1 - c)

        def copy(a, k, src, dst, to):
            return pltpu.make_async_remote_copy(src_ref=src, dst_ref=dst, send_sem=send_sems.at[a, k],
                                                recv_sem=recv_sems.at[a, k], device_id=to, device_id_type=MESH)

        def half_rows(a, pc):
            half = ins[a].shape[0] // 2
            return pl.ds(pc * half, half)

        local = [pltpu.make_async_copy(ins[a], outs[a].at[me], local_sems.at[a]) for a in range(n)]
        ici_out = [[copy(a, k, ins[a].at[half_rows(a, c)], outs[a].at[me, half_rows(a, c)], (px, py, c))
                    for k, (px, py) in enumerate(chips)] for a in range(n)]
        return c, chips, sibling, copy, half_rows, local, ici_out

    def start(ins, outs, send_sems, recv_sems, local_sems):
        c, chips, _, copy, _, local, _ = copies(ins, outs, send_sems, recv_sems, local_sems)
        x, y, _, _ = _position()
        me = 2 * x + y
        for cp in local:
            cp.start()
        for a in range(n):
            half = ins[a].shape[0] // 2
            for k, (px, py) in enumerate(chips):
                for first, size in _row_chunks(half, ICI_CHUNKS):
                    rows = pl.ds(c * half + first, size)
                    copy(a, k, ins[a].at[rows], outs[a].at[me, rows], (px, py, c)).start()

    def finish(ins, outs, send_sems, recv_sems, local_sems):
        c, chips, sibling, copy, half_rows, local, ici_out = copies(ins, outs, send_sems, recv_sems, local_sems)
        for a in range(n):
            half = ins[a].shape[0] // 2
            for k, (px, py) in enumerate(chips):
                blk = outs[a].at[2 * px + py, half_rows(a, c)]
                copy(a, k, blk, blk, (px, py, c)).wait_recv()
                for first, size in _row_chunks(half, D2D_CHUNKS):
                    piece = outs[a].at[2 * px + py, pl.ds(c * half + first, size)]
                    copy(a, 3 + k, piece, piece, sibling).start()
        for a in range(n):
            for k, (px, py) in enumerate(chips):
                theirs = outs[a].at[2 * px + py, half_rows(a, 1 - c)]
                copy(a, 3 + k, theirs, theirs, sibling).wait_recv()
                mine = outs[a].at[2 * px + py, half_rows(a, c)]
                copy(a, 3 + k, mine, mine, sibling).wait_send()
        for a in range(n):
            for cp in ici_out[a]:
                cp.wait_send()
        for cp in local:
            cp.wait()

    return _Carried(arrs, [_sds((N_CHIPS,) + a.shape, a.dtype) for a in arrs],
                    [pltpu.SemaphoreType.DMA((n, 6)), pltpu.SemaphoreType.DMA((n, 6)), pltpu.SemaphoreType.DMA((n,))],
                    start, finish)


def _run_exchange(name, exchange):
    n_in, n_out = len(exchange.ins), len(exchange.out_shapes)

    def body(*refs):
        ins, outs, sems = refs[:n_in], refs[n_in:n_in + n_out], refs[n_in + n_out:]
        exchange.start(ins, outs, *sems)
        exchange.finish(ins, outs, *sems)

    return pl.pallas_call(
        body, name=name, out_shape=tuple(exchange.out_shapes), in_specs=[_HBM] * n_in,
        out_specs=tuple([_HBM] * n_out), scratch_shapes=exchange.sems,
    )(*exchange.ins)


def _pair_exchange(grads):
    n = len(grads)

    def copier(send_sems, recv_sems):
        x, y, c, _ = _position()

        def copy(a, src, dst):
            return pltpu.make_async_remote_copy(src_ref=src, dst_ref=dst, send_sem=send_sems.at[a],
                                                recv_sem=recv_sems.at[a], device_id=(x, y, 1 - c), device_id_type=MESH)
        return c, copy

    def start(ins, got, send_sems, recv_sems):
        c, copy = copier(send_sems, recv_sems)
        for a in range(n):
            half = ins[a].shape[1] // 2
            for slab in range(N_CHIPS):
                for first, size in _row_chunks(half, D2D_CHUNKS):
                    copy(a, ins[a].at[slab, pl.ds((1 - c) * half + first, size), :],
                         got[a].at[slab, pl.ds(first, size), :]).start()

    def finish(ins, got, send_sems, recv_sems):
        c, copy = copier(send_sems, recv_sems)
        for a in range(n):
            half = ins[a].shape[1] // 2
            copy(a, ins[a].at[:, pl.ds((1 - c) * half, half), :], got[a]).wait()

    return _Carried(grads, [_sds((N_CHIPS, g.shape[1] // 2, g.shape[2]), g.dtype) for g in grads],
                    [pltpu.SemaphoreType.DMA((n,)), pltpu.SemaphoreType.DMA((n,))], start, finish)


def _chip_exchange(hsums):
    n = len(hsums)

    def copies(ins, outs, send_sems, recv_sems, local_sems, pieces):
        x, y, c, chips = _position()
        me = 2 * x + y
        cps = []
        for a in range(n):
            cps.append(pltpu.make_async_copy(ins[a].at[me], outs[a].at[me], local_sems.at[a]))
            rows = ins[a].shape[1]
            for k, (px, py) in enumerate(chips):
                for first, size in (_row_chunks(rows, ICI_CHUNKS) if pieces else [(0, rows)]):
                    cps.append(pltpu.make_async_remote_copy(
                        src_ref=ins[a].at[2 * px + py, pl.ds(first, size)], dst_ref=outs[a].at[me, pl.ds(first, size)],
                        send_sem=send_sems.at[a, k], recv_sem=recv_sems.at[a, k], device_id=(px, py, c),
                        device_id_type=MESH))
        return cps

    def start(*refs):
        for cp in copies(*refs, pieces=True):
            cp.start()

    def finish(*refs):
        for cp in copies(*refs, pieces=False):
            cp.wait()

    return _Carried(hsums, [_sds(h.shape, h.dtype) for h in hsums],
                    [pltpu.SemaphoreType.DMA((n, 3)), pltpu.SemaphoreType.DMA((n, 3)), pltpu.SemaphoreType.DMA((n,))],
                    start, finish)


def _pair_swap(reds):
    n = len(reds)

    def copier(send_sems, recv_sems):
        x, y, c, _ = _position()

        def copy(a, src, dst):
            return pltpu.make_async_remote_copy(src_ref=src, dst_ref=dst, send_sem=send_sems.at[a],
                                                recv_sem=recv_sems.at[a], device_id=(x, y, 1 - c), device_id_type=MESH)
        return copy

    def start(ins, outs, send_sems, recv_sems):
        copy = copier(send_sems, recv_sems)
        for a in range(n):
            for first, size in _row_chunks(ins[a].shape[0], 2 * D2D_CHUNKS):
                copy(a, ins[a].at[pl.ds(first, size), :], outs[a].at[pl.ds(first, size), :]).start()

    def finish(ins, outs, send_sems, recv_sems):
        copy = copier(send_sems, recv_sems)
        for a in range(n):
            copy(a, ins[a], outs[a]).wait()

    return _Carried(reds, [_sds(r.shape, r.dtype) for r in reds],
                    [pltpu.SemaphoreType.DMA((n,)), pltpu.SemaphoreType.DMA((n,))], start, finish)


def _reduce_scatter_last(grad):
    _, rows, cols = grad.shape
    half = rows // 2
    pieces = _row_chunks(half, D2D_CHUNKS)

    def body(g_ref, mine_ref, theirs_ref, got_scr, hsum_scr, slab_scr, pair_sems, ici_send, ici_recv, swap_sems):
        x, y, c, chips = _position()
        me = 2 * x + y
        sibling = (x, y, 1 - c)

        def to_sibling(src, dst, sems):
            return pltpu.make_async_remote_copy(src_ref=src, dst_ref=dst, send_sem=sems.at[0], recv_sem=sems.at[1],
                                                device_id=sibling, device_id_type=MESH)

        for slab in range(N_CHIPS):
            for first, size in pieces:
                to_sibling(g_ref.at[slab, pl.ds((1 - c) * half + first, size)], got_scr.at[slab, pl.ds(first, size)],
                           pair_sems).start()
        to_sibling(g_ref.at[:, pl.ds((1 - c) * half, half)], got_scr, pair_sems).wait()
        own = g_ref[:, pl.ds(pl.multiple_of(c * half, half), half), :]
        hsum_scr[...] = (own.astype(F32) + got_scr[...].astype(F32)).astype(BF16)

        slab_scr[me] = hsum_scr[me]
        ici = [pltpu.make_async_remote_copy(src_ref=hsum_scr.at[2 * px + py], dst_ref=slab_scr.at[me],
                                            send_sem=ici_send.at[k], recv_sem=ici_recv.at[k], device_id=(px, py, c),
                                            device_id_type=MESH) for k, (px, py) in enumerate(chips)]
        for cp in ici:
            cp.start()
        for cp in ici:
            cp.wait()
        acc = slab_scr[0].astype(F32)
        for k in range(1, N_CHIPS):
            acc = acc + slab_scr[k].astype(F32)
        mine_ref[...] = acc

        for first, size in pieces:
            to_sibling(mine_ref.at[pl.ds(first, size)], theirs_ref.at[pl.ds(first, size)], swap_sems).start()
        to_sibling(mine_ref, theirs_ref, swap_sems).wait()

    vmem = pl.BlockSpec(memory_space=pltpu.VMEM)
    halves = (N_CHIPS, half, cols)
    return pl.pallas_call(
        body, name="grad_reduce_scatter_last", out_shape=(_sds((half, cols), F32), _sds((half, cols), F32)),
        in_specs=[vmem], out_specs=(vmem, vmem),
        scratch_shapes=[pltpu.VMEM(halves, BF16), pltpu.VMEM(halves, BF16), pltpu.VMEM(halves, BF16),
                        pltpu.SemaphoreType.DMA((2,)), pltpu.SemaphoreType.DMA((3,)), pltpu.SemaphoreType.DMA((3,)),
                        pltpu.SemaphoreType.DMA((2,))],
        compiler_params=pltpu.CompilerParams(vmem_limit_bytes=VMEM_LIMIT_BYTES),
    )(grad)


def _device_gather_exchange(packed):
    def copies(ins, outs, send_sems, recv_sems, local_sem):
        (x_ref,), (all_ref,) = ins, outs
        x, y, c, chips = _position()
        me, sibling = (x, y, c), (x, y, 1 - c)

        def slab(px, py, pc):
            return all_ref.at[4 * px + 2 * py + pc]

        def copy(k, block, to, src=None):
            return pltpu.make_async_remote_copy(
                src_ref=slab(*block) if src is None else src, dst_ref=slab(*block), send_sem=send_sems.at[k],
                recv_sem=recv_sems.at[k], device_id=to, device_id_type=MESH)

        mine = pltpu.make_async_copy(x_ref, slab(*me), local_sem)
        first = [copy(0, me, sibling, src=x_ref)]
        first += [copy(1 + j, me, (*chip, c), src=x_ref) for j, chip in enumerate(chips)]
        passed = [copy(4 + j, (*chip, c), sibling) for j, chip in enumerate(chips)]
        return c, chips, me, sibling, copy, mine, first, passed

    def start(ins, outs, send_sems, recv_sems, local_sem):
        _, _, _, _, _, mine, first, _ = copies(ins, outs, send_sems, recv_sems, local_sem)
        mine.start()
        for cp in first:
            cp.start()

    def finish(ins, outs, send_sems, recv_sems, local_sem):
        c, chips, me, sibling, copy, mine, first, passed = copies(ins, outs, send_sems, recv_sems, local_sem)
        for j, chip in enumerate(chips):
            copy(1 + j, (*chip, c), me).wait_recv()
            passed[j].start()
        copy(0, sibling, me).wait_recv()
        for j, chip in enumerate(chips):
            copy(4 + j, (*chip, 1 - c), me).wait_recv()
        for cp in first + passed:
            cp.wait_send()
        mine.wait()

    return _Carried([packed], [_sds((N_DEV,) + packed.shape, F32)],
                    [pltpu.SemaphoreType.DMA((7,)), pltpu.SemaphoreType.DMA((7,)), pltpu.SemaphoreType.DMA],
                    start, finish)


def _ordered_sum(name, slabs):
    _, m_per, n_cols = slabs.shape

    def body(s_ref, o_ref):
        acc = s_ref[0]
        for d in range(1, N_DEV):
            acc = acc + s_ref[d]
        o_ref[...] = acc

    vmem = pl.BlockSpec(memory_space=pltpu.VMEM)
    return pl.pallas_call(body, name=name, out_shape=_sds((m_per, n_cols), F32), in_specs=[vmem], out_specs=vmem)(slabs)


def _pair_sum(core, own, got, tm):
    _, half, cols = got.shape
    nb = half // tm

    def body(c_ref, a_ref, b_ref, o_ref):
        o_ref[...] = (a_ref[...].astype(F32) + b_ref[...].astype(F32)).astype(BF16)

    return pl.pallas_call(
        body, name="grad_pair_sum", out_shape=_sds(got.shape, BF16),
        grid_spec=pltpu.PrefetchScalarGridSpec(
            num_scalar_prefetch=1, grid=(N_CHIPS, nb),
            in_specs=[pl.BlockSpec((None, tm, cols), lambda s, i, c_ref: (s, c_ref[0] * nb + i, 0)),
                      pl.BlockSpec((None, tm, cols), lambda s, i, c_ref: (s, i, 0))],
            out_specs=pl.BlockSpec((None, tm, cols), lambda s, i, c_ref: (s, i, 0))),
        compiler_params=_cparams(2),
    )(core, own, got)


def _chip_sum(slabs, tm):
    _, half, cols = slabs.shape

    def body(s_ref, o_ref):
        acc = s_ref[0].astype(F32)
        for k in range(1, N_CHIPS):
            acc = acc + s_ref[k].astype(F32)
        o_ref[...] = acc

    return pl.pallas_call(
        body, name="grad_chip_sum", out_shape=_sds((half, cols), F32), grid=(half // tm,),
        in_specs=[pl.BlockSpec((N_CHIPS, tm, cols), lambda i: (0, i, 0))],
        out_specs=pl.BlockSpec((tm, cols), lambda i: (i, 0)), compiler_params=_cparams(1),
    )(slabs)


def _adam_math(w, g, m, v):
    m2 = ADAM_B1 * m + (1.0 - ADAM_B1) * g
    v2 = ADAM_B2 * v + (1.0 - ADAM_B2) * (g * g)
    m_hat = m2 / (1.0 - ADAM_B1 ** ADAM_STEP)
    v_hat = v2 / (1.0 - ADAM_B2 ** ADAM_STEP)
    delta = -ADAM_LR * (m_hat / (jnp.sqrt(v_hat) + ADAM_EPS) + ADAM_WD * w)
    return delta, m2, v2


def _adamw_halves(name, items, tm, carried=None):
    rows, cols = items[0][0].shape
    nb = rows // 2 // tm
    n = len(items)

    def body(*refs):
        mine = (pl.program_id(0) // nb) == lax.axis_index("c")
        for k in range(n):
            w_ref, own_ref, oth_ref, m_ref, v_ref = refs[5 * k:5 * k + 5]
            g_ref, d_ref, m2_ref, v2_ref = refs[5 * n + 4 * k:5 * n + 4 * k + 4]
            g = jnp.where(mine, own_ref[...], oth_ref[...])
            d, m2, v2 = _adam_math(w_ref[...], g, m_ref[...], v_ref[...])
            g_ref[...] = g
            d_ref[...] = d
            m2_ref[...] = m2
            v2_ref[...] = v2

    full = pl.BlockSpec((tm, cols), lambda i: (i, 0))
    half = pl.BlockSpec((tm, cols), lambda i: (i % nb, 0))
    return _call_carrying(
        body, carried, name=name, grid=(rows // tm,), in_specs=[full, half, half, full, full] * n,
        out_specs=[full] * (4 * n), out_shape=tuple([_sds((rows, cols), F32)] * (4 * n)), scratch_shapes=[],
        operands=[a for item in items for a in item])


def _adamw(name, w, g, m, v, tm):
    def body(w_ref, g_ref, m_ref, v_ref, gout_ref, d_ref, m2_ref, v2_ref):
        gv = g_ref[...]
        d, m2, v2 = _adam_math(w_ref[...], gv, m_ref[...], v_ref[...])
        gout_ref[...] = gv
        d_ref[...] = d
        m2_ref[...] = m2
        v2_ref[...] = v2

    return _rows_call(name, body, tm, [w, g, m, v], [], [_sds(w.shape, F32)] * 4)


_SMALL_NAMES = ("norm_mix_pre", "gm_ln_w", "gm_ln_b", "gm_w_s", "gm_b_s", "conv_w", "conv_b", "dt_bias", "a_log",
                "d_skip", "ssm_norm_w", "norm_mix_post", "norm_ffn_pre", "norm_ffn_post")
_PACK_COLS = 1024


def _pack(parts, names=_SMALL_NAMES, tail=None):
    pieces = [parts[n].reshape(-1) for n in names]
    flat = jnp.concatenate(pieces if tail is None else pieces + [tail])
    rows = -(-flat.shape[0] // (8 * _PACK_COLS)) * 8
    flat = jnp.pad(flat, (0, rows * _PACK_COLS - flat.shape[0]))
    return flat.reshape(rows, _PACK_COLS)


def _unpack(packed, shapes, names=_SMALL_NAMES):
    flat = packed.reshape(-1)
    out, off = {}, 0
    for n in names:
        size = 1
        for s in shapes[n]:
            size *= s
        out[n] = flat[off:off + size].reshape(shapes[n])
        off += size
    return out


def kernel(x, norm_mix_pre, w_in, gm_ln_w, gm_ln_b, gm_w_s, gm_b_s, conv_w, conv_b, dt_bias, a_log, d_skip, ssm_norm_w, w_out, norm_mix_post, norm_ffn_pre, w_up, w_down, norm_ffn_post, loss_target, m_norm_mix_pre, m_w_in, m_gm_ln_w, m_gm_ln_b, m_gm_w_s, m_gm_b_s, m_conv_w, m_conv_b, m_dt_bias, m_a_log, m_d_skip, m_ssm_norm_w, m_w_out, m_norm_mix_post, m_norm_ffn_pre, m_w_up, m_w_down, m_norm_ffn_post, v_norm_mix_pre, v_w_in, v_gm_ln_w, v_gm_ln_b, v_gm_w_s, v_gm_b_s, v_conv_w, v_conv_b, v_dt_bias, v_a_log, v_d_skip, v_ssm_norm_w, v_w_out, v_norm_mix_post, v_norm_ffn_pre, v_w_up, v_w_down, v_norm_ffn_post):
    params = dict(norm_mix_pre=norm_mix_pre, w_in=w_in, gm_ln_w=gm_ln_w, gm_ln_b=gm_ln_b, gm_w_s=gm_w_s, gm_b_s=gm_b_s,
                  conv_w=conv_w, conv_b=conv_b, dt_bias=dt_bias, a_log=a_log, d_skip=d_skip, ssm_norm_w=ssm_norm_w,
                  w_out=w_out, norm_mix_post=norm_mix_post, norm_ffn_pre=norm_ffn_pre, w_up=w_up, w_down=w_down,
                  norm_ffn_post=norm_ffn_post)
    mom1 = dict(norm_mix_pre=m_norm_mix_pre, w_in=m_w_in, gm_ln_w=m_gm_ln_w, gm_ln_b=m_gm_ln_b, gm_w_s=m_gm_w_s,
                gm_b_s=m_gm_b_s, conv_w=m_conv_w, conv_b=m_conv_b, dt_bias=m_dt_bias, a_log=m_a_log, d_skip=m_d_skip,
                ssm_norm_w=m_ssm_norm_w, w_out=m_w_out, norm_mix_post=m_norm_mix_post, norm_ffn_pre=m_norm_ffn_pre,
                w_up=m_w_up, w_down=m_w_down, norm_ffn_post=m_norm_ffn_post)
    mom2 = dict(norm_mix_pre=v_norm_mix_pre, w_in=v_w_in, gm_ln_w=v_gm_ln_w, gm_ln_b=v_gm_ln_b, gm_w_s=v_gm_w_s,
                gm_b_s=v_gm_b_s, conv_w=v_conv_w, conv_b=v_conv_b, dt_bias=v_dt_bias, a_log=v_a_log, d_skip=v_d_skip,
                ssm_norm_w=v_ssm_norm_w, w_out=v_w_out, norm_mix_post=v_norm_mix_post, norm_ffn_pre=v_norm_ffn_pre,
                w_up=v_w_up, w_down=v_w_down, norm_ffn_post=v_norm_ffn_post)
    names = list(params)
    big = ("w_in", "w_out", "w_up", "w_down")
    chip = 2 * lax.axis_index("x") + lax.axis_index("y")

    shards = {n: params[n][0].astype(BF16) for n in big}
    conv_shard = jnp.pad(conv_w[0], ((0, 16 - CONV_K), (0, 0)))
    g_in4, g_conv4 = _run_exchange("allgather_w_in", _allgather_exchange([shards["w_in"], conv_shard]))
    conv_full = jnp.transpose(g_conv4[:, :CONV_K, :], (1, 0, 2)).reshape(CONV_K, CONV_CH)

    small = {n: params[n][0] if params[n].ndim >= 3 else params[n] for n in _SMALL_NAMES if n != "conv_w"}
    core = lax.axis_index("c").astype(jnp.int32).reshape(1)
    adam_args = {n: (params[n][0], mom1[n][0], mom2[n][0]) for n in big}
    loss, grad_x, big_out, small_sum = _forward_backward(
        x, loss_target, g_in4, conv_full, small, shards["w_out"], shards["w_up"], shards["w_down"], core, adam_args)
    grads, delta, new_m, new_v = {}, {}, {}, {}
    for n in big:
        grads[n], delta[n], new_m[n], new_v[n] = [a[None] for a in big_out[n]]

    small_sum["conv_w"] = lax.dynamic_slice_in_dim(small_sum["conv_w"], chip * (CONV_CH // N_CHIPS), CONV_CH // N_CHIPS, axis=1)

    local_shapes = {n: params[n].shape[1:] if params[n].ndim >= 3 else params[n].shape for n in _SMALL_NAMES}
    flat = lambda tree: {n: tree[n].reshape(local_shapes[n]) for n in _SMALL_NAMES}
    packed = [_pack(flat(t)) for t in (params, small_sum, mom1, mom2)]
    _, d_p, m_p, v_p = _adamw("adamw_small", *packed, packed[0].shape[0])
    for src, dst in ((d_p, delta), (m_p, new_m), (v_p, new_v)):
        for n, val in _unpack(src, local_shapes).items():
            dst[n] = val.reshape(params[n].shape)
    for n in _SMALL_NAMES:
        grads[n] = small_sum[n].reshape(params[n].shape)

    out = [loss, grad_x]
    for tree in (grads, delta, new_m, new_v):
        out += [tree[n] for n in names]
    return tuple(out)
```
